```python
import jax, jax.numpy as jnp
from jax import lax
import numpy as np

D_MODEL = 2048
BATCH = 8
SEQ = 2048
DEPTH = 1

HEAD_DIM = 128
N_MIX_HEADS = D_MODEL // HEAD_DIM
N_MEM_HEADS = 4
N_SGU_HEADS = (N_MIX_HEADS - N_MEM_HEADS) // 2
N_CONV_GROUPS = N_MIX_HEADS - N_MEM_HEADS - N_SGU_HEADS
D_SGU = N_SGU_HEADS * HEAD_DIM
D_CONV = N_CONV_GROUPS * HEAD_DIM
D_MEM = N_MEM_HEADS * HEAD_DIM
D_MIX = D_SGU + D_CONV + D_MEM
D_IN = 2 * D_SGU + 3 * D_CONV + D_MEM
CHUNK = 128
CONV_W = 3
N_MEM = 256
D_FF = 4 * D_MODEL
EPS = 1e-6

kernel_name = "hybrid_sgu_shortconv_memattn_block"


def rms_norm(x, g):
    xf = x.astype(jnp.float32)
    y = xf * lax.rsqrt(jnp.mean(xf * xf, axis=-1, keepdims=True) + EPS)
    return (y * g.astype(jnp.float32)).astype(x.dtype)


def layer_norm(x, g, b):
    xf = x.astype(jnp.float32)
    mu = jnp.mean(xf, axis=-1, keepdims=True)
    xc = xf - mu
    y = xc * lax.rsqrt(jnp.mean(xc * xc, axis=-1, keepdims=True) + EPS)
    return (y * g.astype(jnp.float32) + b.astype(jnp.float32)).astype(x.dtype)


def chunked_spatial_gating(u, v, ln_g, ln_b, w_s, b_s):
    bsz, s, _ = v.shape
    u = jax.nn.gelu(u)
    v = layer_norm(jax.nn.gelu(v), ln_g, ln_b)
    vc = v.reshape(bsz, s // CHUNK, CHUNK, N_SGU_HEADS, HEAD_DIM)
    causal = jnp.tril(jnp.ones((CHUNK, CHUNK), dtype=bool))
    w = jnp.where(causal[None], w_s, jnp.zeros_like(w_s))
    mixed = jnp.einsum('hts,bcshd->bcthd', w, vc) + b_s.T[:, :, None]
    return u * mixed.reshape(bsz, s, D_SGU)


def short_gated_conv(b_gate, c_gate, xin, conv_w):
    xc = c_gate * xin
    y = lax.conv_general_dilated(
        xc, conv_w[:, None, :].astype(xc.dtype),
        window_strides=(1,), padding=[(CONV_W - 1, 0)],
        dimension_numbers=('NWC', 'WIO', 'NWC'),
        feature_group_count=D_CONV)
    return b_gate * y


def memory_attention(q, mem_n, w_kv):
    bsz, s, _ = q.shape
    m = mem_n.shape[1]
    k, v = jnp.split(mem_n @ w_kv, 2, axis=-1)
    q = q.reshape(bsz, s, N_MEM_HEADS, HEAD_DIM) * (HEAD_DIM ** -0.5)
    k = k.reshape(bsz, m, N_MEM_HEADS, HEAD_DIM)
    v = v.reshape(bsz, m, N_MEM_HEADS, HEAD_DIM)
    scores = jnp.einsum('bshd,bmhd->bhsm', q, k).astype(jnp.float32)
    p = jax.nn.softmax(scores, axis=-1).astype(v.dtype)
    o = jnp.einsum('bhsm,bmhd->bshd', p, v)
    return o.reshape(bsz, s, D_MEM)


def _fwd_setup_inputs(seed: int = 0) -> dict:
    key = jax.random.key(seed)
    ks = jax.random.split(key, 20)
    f32 = jnp.float32
    nrm = lambda k, shape, scale: jax.random.normal(k, shape, f32) * scale
    gain = lambda k, shape: 1.0 + 0.02 * jax.random.normal(k, shape, f32)
    return {
        "x": jax.random.normal(ks[0], (BATCH, SEQ, D_MODEL), f32),
        "mem": jax.random.normal(ks[1], (BATCH, N_MEM, D_MODEL), f32),
        "g_mix": gain(ks[2], (DEPTH, D_MODEL)),
        "w_in": nrm(ks[3], (DEPTH, D_MODEL, D_IN), D_MODEL ** -0.5),
        "ln_v_g": gain(ks[4], (DEPTH, D_SGU)),
        "ln_v_b": nrm(ks[5], (DEPTH, D_SGU), 0.02),
        "w_s": nrm(ks[6], (DEPTH, N_SGU_HEADS, CHUNK, CHUNK), CHUNK ** -0.5),
        "b_s": gain(ks[7], (DEPTH, N_SGU_HEADS, CHUNK)),
        "conv_w": nrm(ks[8], (DEPTH, CONV_W, D_CONV), CONV_W ** -0.5),
        "g_mem": gain(ks[9], (DEPTH, D_MODEL)),
        "w_kv": nrm(ks[10], (DEPTH, D_MODEL, 2 * D_MEM), D_MODEL ** -0.5),
        "g_head": gain(ks[11], (DEPTH, D_MIX)),
        "w_o": nrm(ks[12], (DEPTH, D_MIX, D_MODEL), D_MIX ** -0.5),
        "g_ffn": gain(ks[13], (DEPTH, D_MODEL)),
        "w_ffn1": nrm(ks[14], (DEPTH, D_MODEL, D_FF), D_MODEL ** -0.5),
        "w_ffn2": nrm(ks[15], (DEPTH, D_FF, D_MODEL), D_FF ** -0.5),
        "g_final": gain(ks[16], (D_MODEL,)),
    }


def _fwd_reference(x, mem, g_mix, w_in, ln_v_g, ln_v_b, w_s, b_s, conv_w, g_mem,
              w_kv, g_head, w_o, g_ffn, w_ffn1, w_ffn2, g_final):
    bsz, s, _ = x.shape
    split_at = np.cumsum([D_SGU, D_SGU, D_CONV, D_CONV, D_CONV])
    for l in range(DEPTH):
        h = rms_norm(x, g_mix[l])
        proj = h @ w_in[l]
        u, v, b_gate, c_gate, xin, q = jnp.split(proj, split_at, axis=-1)
        a_out = chunked_spatial_gating(u, v, ln_v_g[l], ln_v_b[l], w_s[l], b_s[l])
        c_out = short_gated_conv(b_gate, c_gate, xin, conv_w[l])
        m_out = memory_attention(q, rms_norm(mem, g_mem[l]), w_kv[l])
        heads = jnp.concatenate([a_out, c_out, m_out], axis=-1)
        heads = rms_norm(heads.reshape(bsz, s, N_MIX_HEADS, HEAD_DIM),
                         jnp.ones((HEAD_DIM,), heads.dtype)).reshape(bsz, s, D_MIX)
        x = x + (heads * g_head[l]) @ w_o[l]
        f = rms_norm(x, g_ffn[l]) @ w_ffn1[l]
        x = x + jnp.square(jax.nn.relu(f)) @ w_ffn2[l]
    return rms_norm(x, g_final)


import jax as _jax
import jax.numpy as _jnp

TWIN_FORMAT = 'train_step'
FWD_PARAMS = ['x', 'mem', 'g_mix', 'w_in', 'ln_v_g', 'ln_v_b', 'w_s', 'b_s', 'conv_w', 'g_mem', 'w_kv', 'g_head', 'w_o', 'g_ffn', 'w_ffn1', 'w_ffn2', 'g_final']
TWIN_WEIGHTS = ['g_mix', 'w_in', 'ln_v_g', 'ln_v_b', 'w_s', 'b_s', 'conv_w', 'g_mem', 'w_kv', 'g_head', 'w_o', 'g_ffn', 'w_ffn1', 'w_ffn2', 'g_final']
TWIN_DIFF_INPUT = 'x'
TWIN_INPUTS = ['x', 'mem', 'g_mix', 'w_in', 'ln_v_g', 'ln_v_b', 'w_s', 'b_s', 'conv_w', 'g_mem', 'w_kv', 'g_head', 'w_o', 'g_ffn', 'w_ffn1', 'w_ffn2', 'g_final', 'loss_target', 'm_g_mix', 'm_w_in', 'm_ln_v_g', 'm_ln_v_b', 'm_w_s', 'm_b_s', 'm_conv_w', 'm_g_mem', 'm_w_kv', 'm_g_head', 'm_w_o', 'm_g_ffn', 'm_w_ffn1', 'm_w_ffn2', 'm_g_final', 'v_g_mix', 'v_w_in', 'v_ln_v_g', 'v_ln_v_b', 'v_w_s', 'v_b_s', 'v_conv_w', 'v_g_mem', 'v_w_kv', 'v_g_head', 'v_w_o', 'v_g_ffn', 'v_w_ffn1', 'v_w_ffn2', 'v_g_final']
TWIN_OUTPUTS = ['loss', 'grad_x', 'grad_g_mix', 'grad_w_in', 'grad_ln_v_g', 'grad_ln_v_b', 'grad_w_s', 'grad_b_s', 'grad_conv_w', 'grad_g_mem', 'grad_w_kv', 'grad_g_head', 'grad_w_o', 'grad_g_ffn', 'grad_w_ffn1', 'grad_w_ffn2', 'grad_g_final', 'delta_g_mix', 'delta_w_in', 'delta_ln_v_g', 'delta_ln_v_b', 'delta_w_s', 'delta_b_s', 'delta_conv_w', 'delta_g_mem', 'delta_w_kv', 'delta_g_head', 'delta_w_o', 'delta_g_ffn', 'delta_w_ffn1', 'delta_w_ffn2', 'delta_g_final', 'new_m_g_mix', 'new_m_w_in', 'new_m_ln_v_g', 'new_m_ln_v_b', 'new_m_w_s', 'new_m_b_s', 'new_m_conv_w', 'new_m_g_mem', 'new_m_w_kv', 'new_m_g_head', 'new_m_w_o', 'new_m_g_ffn', 'new_m_w_ffn1', 'new_m_w_ffn2', 'new_m_g_final', 'new_v_g_mix', 'new_v_w_in', 'new_v_ln_v_g', 'new_v_ln_v_b', 'new_v_w_s', 'new_v_b_s', 'new_v_conv_w', 'new_v_g_mem', 'new_v_w_kv', 'new_v_g_head', 'new_v_w_o', 'new_v_g_ffn', 'new_v_w_ffn1', 'new_v_w_ffn2', 'new_v_g_final']
TWIN_LEAF_KINDS = {'loss': 'loss', 'grad_x': 'grad_x', 'grad_g_mix': 'grad_w', 'grad_w_in': 'grad_w', 'grad_ln_v_g': 'grad_w', 'grad_ln_v_b': 'grad_w', 'grad_w_s': 'grad_w', 'grad_b_s': 'grad_w', 'grad_conv_w': 'grad_w', 'grad_g_mem': 'grad_w', 'grad_w_kv': 'grad_w', 'grad_g_head': 'grad_w', 'grad_w_o': 'grad_w', 'grad_g_ffn': 'grad_w', 'grad_w_ffn1': 'grad_w', 'grad_w_ffn2': 'grad_w', 'grad_g_final': 'grad_w', 'delta_g_mix': 'delta_w', 'delta_w_in': 'delta_w', 'delta_ln_v_g': 'delta_w', 'delta_ln_v_b': 'delta_w', 'delta_w_s': 'delta_w', 'delta_b_s': 'delta_w', 'delta_conv_w': 'delta_w', 'delta_g_mem': 'delta_w', 'delta_w_kv': 'delta_w', 'delta_g_head': 'delta_w', 'delta_w_o': 'delta_w', 'delta_g_ffn': 'delta_w', 'delta_w_ffn1': 'delta_w', 'delta_w_ffn2': 'delta_w', 'delta_g_final': 'delta_w', 'new_m_g_mix': 'new_m', 'new_m_w_in': 'new_m', 'new_m_ln_v_g': 'new_m', 'new_m_ln_v_b': 'new_m', 'new_m_w_s': 'new_m', 'new_m_b_s': 'new_m', 'new_m_conv_w': 'new_m', 'new_m_g_mem': 'new_m', 'new_m_w_kv': 'new_m', 'new_m_g_head': 'new_m', 'new_m_w_o': 'new_m', 'new_m_g_ffn': 'new_m', 'new_m_w_ffn1': 'new_m', 'new_m_w_ffn2': 'new_m', 'new_m_g_final': 'new_m', 'new_v_g_mix': 'new_v', 'new_v_w_in': 'new_v', 'new_v_ln_v_g': 'new_v', 'new_v_ln_v_b': 'new_v', 'new_v_w_s': 'new_v', 'new_v_b_s': 'new_v', 'new_v_conv_w': 'new_v', 'new_v_g_mem': 'new_v', 'new_v_w_kv': 'new_v', 'new_v_g_head': 'new_v', 'new_v_w_o': 'new_v', 'new_v_g_ffn': 'new_v', 'new_v_w_ffn1': 'new_v', 'new_v_w_ffn2': 'new_v', 'new_v_g_final': 'new_v'}


def _forward(args):
    return _fwd_reference(*[args[k] for k in FWD_PARAMS])


def _output_shape():
    out = _jax.eval_shape(lambda: _forward(_fwd_setup_inputs(0)))
    return out.shape, out.dtype

N_MICROBATCH = 1
ADAM_LR = 0.001
ADAM_B1 = 0.9
ADAM_B2 = 0.999
ADAM_EPS = 1e-08
ADAM_WD = 0.01
ADAM_STEP = 10
PER_EXAMPLE_BATCH_AXIS = {'x': 0, 'mem': 0, 'loss_target': 0}
SHARED_INPUTS = []
_WEIGHT_DTYPES = {'g_mix': _jnp.float32, 'w_in': _jnp.float32, 'ln_v_g': _jnp.float32, 'ln_v_b': _jnp.float32, 'w_s': _jnp.float32, 'b_s': _jnp.float32, 'conv_w': _jnp.float32, 'g_mem': _jnp.float32, 'w_kv': _jnp.float32, 'g_head': _jnp.float32, 'w_o': _jnp.float32, 'g_ffn': _jnp.float32, 'w_ffn1': _jnp.float32, 'w_ffn2': _jnp.float32, 'g_final': _jnp.float32}
MOMENT_SCALE = {'g_mix': 6.813321e-02, 'w_in': 4.602957e-02, 'ln_v_g': 2.653432e-02, 'ln_v_b': 2.748760e-02, 'w_s': 2.639320e-02, 'b_s': 2.056963e-02, 'conv_w': 5.044177e-02, 'g_mem': 3.468701e-02, 'w_kv': 4.793961e-02, 'g_head': 4.790674e-02, 'w_o': 4.883618e-02, 'g_ffn': 4.739277e-02, 'w_ffn1': 2.391551e-02, 'w_ffn2': 4.486210e-02, 'g_final': 8.077419e+00}


def _to_microbatches(a, axis):
    t = _jnp.moveaxis(a, axis, 0)
    t = t.reshape((N_MICROBATCH, t.shape[0] // N_MICROBATCH) + t.shape[1:])
    return _jnp.moveaxis(t, 1, axis + 1)


def setup_inputs(seed: int = 0) -> dict:
    inp = _fwd_setup_inputs(seed)
    key = _jax.random.fold_in(_jax.random.key(seed), 7919)
    shape, _ = _output_shape()
    out = dict(inp)
    out["loss_target"] = _jax.random.normal(_jax.random.fold_in(key, 0), shape, _jnp.float32)
    for i, name in enumerate(TWIN_WEIGHTS):
        w = inp[name].astype(_jnp.float32)
        if MOMENT_SCALE is None:
            s = _jnp.sqrt(_jnp.mean(_jnp.square(w)) + 1e-30)
        else:
            s = MOMENT_SCALE[name]
        km, kv = _jax.random.split(_jax.random.fold_in(key, i + 1))
        out[name] = w
        out["m_" + name] = s * _jax.random.normal(km, w.shape, _jnp.float32)
        out["v_" + name] = (s * s) * _jax.random.uniform(kv, w.shape, _jnp.float32, 0.5, 1.5)
    if N_MICROBATCH > 1:
        for name, axis in PER_EXAMPLE_BATCH_AXIS.items():
            out[name] = _to_microbatches(out[name], axis)
    return {'x': out['x'], 'mem': out['mem'], 'g_mix': out['g_mix'], 'w_in': out['w_in'], 'ln_v_g': out['ln_v_g'], 'ln_v_b': out['ln_v_b'], 'w_s': out['w_s'], 'b_s': out['b_s'], 'conv_w': out['conv_w'], 'g_mem': out['g_mem'], 'w_kv': out['w_kv'], 'g_head': out['g_head'], 'w_o': out['w_o'], 'g_ffn': out['g_ffn'], 'w_ffn1': out['w_ffn1'], 'w_ffn2': out['w_ffn2'], 'g_final': out['g_final'], 'loss_target': out['loss_target'], 'm_g_mix': out['m_g_mix'], 'm_w_in': out['m_w_in'], 'm_ln_v_g': out['m_ln_v_g'], 'm_ln_v_b': out['m_ln_v_b'], 'm_w_s': out['m_w_s'], 'm_b_s': out['m_b_s'], 'm_conv_w': out['m_conv_w'], 'm_g_mem': out['m_g_mem'], 'm_w_kv': out['m_w_kv'], 'm_g_head': out['m_g_head'], 'm_w_o': out['m_w_o'], 'm_g_ffn': out['m_g_ffn'], 'm_w_ffn1': out['m_w_ffn1'], 'm_w_ffn2': out['m_w_ffn2'], 'm_g_final': out['m_g_final'], 'v_g_mix': out['v_g_mix'], 'v_w_in': out['v_w_in'], 'v_ln_v_g': out['v_ln_v_g'], 'v_ln_v_b': out['v_ln_v_b'], 'v_w_s': out['v_w_s'], 'v_b_s': out['v_b_s'], 'v_conv_w': out['v_conv_w'], 'v_g_mem': out['v_g_mem'], 'v_w_kv': out['v_w_kv'], 'v_g_head': out['v_g_head'], 'v_w_o': out['v_w_o'], 'v_g_ffn': out['v_g_ffn'], 'v_w_ffn1': out['v_w_ffn1'], 'v_w_ffn2': out['v_w_ffn2'], 'v_g_final': out['v_g_final']}


def _loss(weights, diff, rest, loss_target):
    with _jax.named_scope("forward"):
        args = {**rest, TWIN_DIFF_INPUT: diff, **{k: w.astype(_WEIGHT_DTYPES[k]) for k, w in weights.items()}}
        y = _forward(args)
    with _jax.named_scope("loss_head"):
        err = _jnp.square(y.astype(_jnp.float32) - loss_target)
        return 0.5 * _jnp.sum(_jnp.mean(err, axis=-1)) if err.ndim else 0.5 * err


def _adamw(w, g, m, v):
    m = ADAM_B1 * m + (1.0 - ADAM_B1) * g
    v = ADAM_B2 * v + (1.0 - ADAM_B2) * _jnp.square(g)
    m_hat = m / (1.0 - ADAM_B1 ** ADAM_STEP)
    v_hat = v / (1.0 - ADAM_B2 ** ADAM_STEP)
    delta = -ADAM_LR * (m_hat / (_jnp.sqrt(v_hat) + ADAM_EPS) + ADAM_WD * w)
    return delta, m, v


def reference(x, mem, g_mix, w_in, ln_v_g, ln_v_b, w_s, b_s, conv_w, g_mem, w_kv, g_head, w_o, g_ffn, w_ffn1, w_ffn2, g_final, loss_target, m_g_mix, m_w_in, m_ln_v_g, m_ln_v_b, m_w_s, m_b_s, m_conv_w, m_g_mem, m_w_kv, m_g_head, m_w_o, m_g_ffn, m_w_ffn1, m_w_ffn2, m_g_final, v_g_mix, v_w_in, v_ln_v_g, v_ln_v_b, v_w_s, v_b_s, v_conv_w, v_g_mem, v_w_kv, v_g_head, v_w_o, v_g_ffn, v_w_ffn1, v_w_ffn2, v_g_final):
    given = dict(x=x, mem=mem, g_mix=g_mix, w_in=w_in, ln_v_g=ln_v_g, ln_v_b=ln_v_b, w_s=w_s, b_s=b_s, conv_w=conv_w, g_mem=g_mem, w_kv=w_kv, g_head=g_head, w_o=w_o, g_ffn=g_ffn, w_ffn1=w_ffn1, w_ffn2=w_ffn2, g_final=g_final, loss_target=loss_target, m_g_mix=m_g_mix, m_w_in=m_w_in, m_ln_v_g=m_ln_v_g, m_ln_v_b=m_ln_v_b, m_w_s=m_w_s, m_b_s=m_b_s, m_conv_w=m_conv_w, m_g_mem=m_g_mem, m_w_kv=m_w_kv, m_g_head=m_g_head, m_w_o=m_w_o, m_g_ffn=m_g_ffn, m_w_ffn1=m_w_ffn1, m_w_ffn2=m_w_ffn2, m_g_final=m_g_final, v_g_mix=v_g_mix, v_w_in=v_w_in, v_ln_v_g=v_ln_v_g, v_ln_v_b=v_ln_v_b, v_w_s=v_w_s, v_b_s=v_b_s, v_conv_w=v_conv_w, v_g_mem=v_g_mem, v_w_kv=v_w_kv, v_g_head=v_g_head, v_w_o=v_w_o, v_g_ffn=v_g_ffn, v_w_ffn1=v_w_ffn1, v_w_ffn2=v_w_ffn2, v_g_final=v_g_final)
    weights = {n: given[n] for n in TWIN_WEIGHTS}
    shared = {n: given[n] for n in SHARED_INPUTS}
    per_example = {n: given[n] for n in ['x', 'mem']}
    grad_fn = _jax.value_and_grad(_loss, argnums=(0, 1))

    def one_microbatch(ex, loss_target):
        ex = dict(ex)
        diff = ex.pop(TWIN_DIFF_INPUT)
        return grad_fn(weights, diff, {**shared, **ex}, loss_target)

    if N_MICROBATCH == 1:
        loss, (grad_w, grad_x) = one_microbatch(per_example, given["loss_target"])
    else:
        def body(carry, xs):
            loss_sum, grad_sum = carry
            l_k, (gw_k, gx_k) = one_microbatch(xs[0], xs[1])
            with _jax.named_scope("update"):
                return (loss_sum + l_k, _jax.tree.map(_jnp.add, grad_sum, gw_k)), gx_k

        init = (_jnp.zeros((), _jnp.float32), _jax.tree.map(_jnp.zeros_like, weights))
        (loss, grad_w), grad_x = _jax.lax.scan(body, init, (per_example, given["loss_target"]))
    with _jax.named_scope("update"):
        delta_w, new_m, new_v = {}, {}, {}
        for n in TWIN_WEIGHTS:
            delta_w[n], new_m[n], new_v[n] = _adamw(weights[n], grad_w[n], given["m_" + n], given["v_" + n])
    return (loss, grad_x, *[grad_w[n] for n in TWIN_WEIGHTS], *[delta_w[n] for n in TWIN_WEIGHTS],
            *[new_m[n] for n in TWIN_WEIGHTS], *[new_v[n] for n in TWIN_WEIGHTS])
```

```python
import functools
import math

import jax
import jax.numpy as jnp
from jax import lax
from jax.experimental import pallas as pl
from jax.experimental.pallas import tpu as pltpu

F32 = jnp.float32
BF16 = jnp.bfloat16
MESH = pl.DeviceIdType.MESH

D = 2048
S = 2048
HD = 128
NH = D // HD
NMH = 4
NSH = (NH - NMH) // 2
NCH = NH - NMH - NSH
DS = NSH * HD
DC = NCH * HD
DM = NMH * HD
DIN = 2 * DS + 3 * DC + DM
CHUNK = 128
NMEM = 256
DFF = 4 * D
EPS = 1e-6
NCHIP = 4
SCALE = HD ** -0.5

ADAM_LR = 0.001
ADAM_B1 = 0.9
ADAM_B2 = 0.999
ADAM_EPS = 1e-08
ADAM_WD = 0.01
ADAM_STEP = 10

TR_EW = 256
TR_MIX = 256
TM = 512
TN = 1024
TK = 2048
VMEM_MB = 56
HALO = 8


def _pick(n, target, q=128):
    best = None
    for t in range(q, min(n, target) + 1, q):
        if n % t == 0:
            best = t
    return n if best is None else best


def _cp(sem=None, vmem_mb=None, **kw):
    d = dict(kw)
    if sem is not None:
        d["dimension_semantics"] = sem
    if vmem_mb is not None:
        d["vmem_limit_bytes"] = vmem_mb << 20
    return pltpu.CompilerParams(**d)


def _gelu(x):
    z = 0.7978845608028654 * (x + 0.044715 * (x * x * x))
    return 0.5 * x * (1.0 + jnp.tanh(z))


def _gelu_grad(x):
    x2 = x * x
    t = jnp.tanh(0.7978845608028654 * (x + 0.044715 * (x2 * x)))
    return 0.5 * (1.0 + t) + 0.5 * x * (1.0 - t * t) * (0.7978845608028654 * (1.0 + 3.0 * 0.044715 * x2))


def _matmul(a, b, *, name, ta=False, tb=False, M, N, K, tm=None, tn=None, tk=None, outs, epi=None,
            extras=(), b_spec=None, out_specs=None):
    tm = _pick(M, TM if tm is None else tm, 8)
    tn = _pick(N, TN if tn is None else tn)
    tk = _pick(K, TK if tk is None else tk)
    nk = K // tk
    grid = (N // tn, M // tm, nk)
    a_spec = (pl.BlockSpec((tk, tm), lambda j, i, k: (k, i)) if ta
              else pl.BlockSpec((tm, tk), lambda j, i, k: (i, k)))
    if b_spec is None:
        b_spec = (pl.BlockSpec((tn, tk), lambda j, i, k: (j, k)) if tb
                  else pl.BlockSpec((tk, tn), lambda j, i, k: (k, j)))
    else:
        b_spec = b_spec(tn, tk)
    if out_specs is None:
        out_specs = [pl.BlockSpec((tm, tn), lambda j, i, k: (i, j)) for _ in outs]
    else:
        out_specs = out_specs(tm, tn)
    dn = (((0 if ta else 1,), (1 if tb else 0,)), ((), ()))
    n_ex, n_out = len(extras), len(outs)

    def body(*refs):
        a_ref, b_ref = refs[0], refs[1]
        ex = refs[2:2 + n_ex]
        o = refs[2 + n_ex:2 + n_ex + n_out]
        acc = refs[2 + n_ex + n_out:]
        part = lax.dot_general(a_ref[...].astype(BF16), b_ref[...].astype(BF16), dn,
                               preferred_element_type=F32)

        def finish(val):
            res = (val,) if epi is None else epi(val, *[e[...] for e in ex])
            for r, o_ref in zip(res, o):
                o_ref[...] = r.astype(o_ref.dtype)

        if nk == 1:
            finish(part)
        else:
            k = pl.program_id(2)

            @pl.when(k == 0)
            def _():
                acc[0][...] = part

            @pl.when(k > 0)
            def _():
                acc[0][...] += part

            @pl.when(k == nk - 1)
            def _():
                finish(acc[0][...])

    return pl.pallas_call(
        body, name=name, grid=grid,
        in_specs=[a_spec, b_spec] + [sp(tm, tn) for _, sp in extras],
        out_specs=out_specs, out_shape=outs,
        scratch_shapes=([pltpu.VMEM((tm, tn), F32)] if nk > 1 else []),
        compiler_params=_cp(("parallel", "parallel", "arbitrary"), VMEM_MB),
    )(a, b, *[arr for arr, _ in extras])


def _tile_spec():
    return lambda tm, tn: pl.BlockSpec((tm, tn), lambda j, i, k: (i, j))


def _cast_bf16(w, *, name):
    R, C = w.shape
    tr = _pick(R, TR_EW, 16)

    def body(w_ref, o_ref):
        o_ref[...] = w_ref[...].astype(BF16)

    return pl.pallas_call(
        body, name=name, grid=(R // tr,),
        in_specs=[pl.BlockSpec((tr, C), lambda i: (i, 0))],
        out_specs=pl.BlockSpec((tr, C), lambda i: (i, 0)),
        out_shape=jax.ShapeDtypeStruct((R, C), BF16),
        compiler_params=_cp(("parallel",), VMEM_MB),
    )(w)


def _rms_fwd(x, g, *, name):
    R, C = x.shape
    tr = _pick(R, TR_EW, 16)

    def body(x_ref, g_ref, o_ref):
        xv = x_ref[...]
        r = lax.rsqrt(jnp.mean(xv * xv, axis=-1, keepdims=True) + EPS)
        o_ref[...] = ((xv * r) * g_ref[...]).astype(BF16)

    return pl.pallas_call(
        body, name=name, grid=(R // tr,),
        in_specs=[pl.BlockSpec((tr, C), lambda i: (i, 0)), pl.BlockSpec((1, C), lambda i: (0, 0))],
        out_specs=pl.BlockSpec((tr, C), lambda i: (i, 0)),
        out_shape=jax.ShapeDtypeStruct((R, C), BF16),
        compiler_params=_cp(("parallel",), VMEM_MB),
    )(x, g)


def _rms_bwd(dh, x, g, dres, *, name, want_dx=True, want_bf=True):
    R, C = x.shape
    tr = _pick(R, TR_EW, 16)
    has_res = dres is not None
    row = pl.BlockSpec((tr, C), lambda i: (i, 0))
    vec = pl.BlockSpec((1, C), lambda i: (0, 0))

    def body(*refs):
        dh_ref, x_ref, g_ref = refs[:3]
        pos = 3
        dres_ref = None
        if has_res:
            dres_ref = refs[pos]
            pos += 1
        outs = refs[pos:]
        i = pl.program_id(0)
        xv = x_ref[...]
        r = lax.rsqrt(jnp.mean(xv * xv, axis=-1, keepdims=True) + EPS)
        xh = xv * r
        dhv = dh_ref[...]
        dg_ref = outs[-1]
        dgp = jnp.sum(dhv * xh, axis=0, keepdims=True)

        @pl.when(i == 0)
        def _():
            dg_ref[...] = dgp

        @pl.when(i > 0)
        def _():
            dg_ref[...] += dgp

        if want_dx:
            t = dhv * g_ref[...]
            dx = r * (t - xh * jnp.mean(t * xh, axis=-1, keepdims=True))
            if has_res:
                dx = dx + dres_ref[...]
            outs[0][...] = dx
            if want_bf:
                outs[1][...] = dx.astype(BF16)

    in_specs = [row, row, vec] + ([row] if has_res else [])
    out_specs, out_shape = [], []
    if want_dx:
        out_specs.append(row)
        out_shape.append(jax.ShapeDtypeStruct((R, C), F32))
        if want_bf:
            out_specs.append(row)
            out_shape.append(jax.ShapeDtypeStruct((R, C), BF16))
    out_specs.append(vec)
    out_shape.append(jax.ShapeDtypeStruct((1, C), F32))
    args = [dh, x, g] + ([dres] if has_res else [])
    return pl.pallas_call(
        body, name=name, grid=(R // tr,), in_specs=in_specs, out_specs=out_specs, out_shape=out_shape,
        compiler_params=_cp(("arbitrary",), VMEM_MB),
    )(*args)


def _loss_bwd(x3, g, tgt, *, name):
    R, C = x3.shape
    tr = _pick(R, TR_EW, 16)
    n = R // tr
    row = pl.BlockSpec((tr, C), lambda i: (i, 0))
    vec = pl.BlockSpec((1, C), lambda i: (0, 0))

    def body(x_ref, g_ref, t_ref, dx_ref, dxb_ref, dg_ref, loss_ref, acc_ref):
        i = pl.program_id(0)
        xv = x_ref[...]
        gv = g_ref[...]
        r = lax.rsqrt(jnp.mean(xv * xv, axis=-1, keepdims=True) + EPS)
        xh = xv * r
        e = xh * gv - t_ref[...]
        dy = e * (1.0 / C)
        sq = jnp.sum(e * e, axis=0, keepdims=True)
        dgp = jnp.sum(dy * xh, axis=0, keepdims=True)

        @pl.when(i == 0)
        def _():
            acc_ref[...] = sq
            dg_ref[...] = dgp

        @pl.when(i > 0)
        def _():
            acc_ref[...] += sq
            dg_ref[...] += dgp

        t = dy * gv
        dx = r * (t - xh * jnp.mean(t * xh, axis=-1, keepdims=True))
        dx_ref[...] = dx
        dxb_ref[...] = dx.astype(BF16)

        @pl.when(i == n - 1)
        def _():
            loss_ref[...] = jnp.sum(acc_ref[...], axis=-1, keepdims=True) * (0.5 / C)

    return pl.pallas_call(
        body, name=name, grid=(n,),
        in_specs=[row, vec, row],
        out_specs=[row, row, vec, pl.BlockSpec((1, 1), lambda i: (0, 0))],
        out_shape=[jax.ShapeDtypeStruct((R, C), F32), jax.ShapeDtypeStruct((R, C), BF16),
                   jax.ShapeDtypeStruct((1, C), F32), jax.ShapeDtypeStruct((1, 1), F32)],
        scratch_shapes=[pltpu.VMEM((1, C), F32)],
        compiler_params=_cp(("arbitrary",), VMEM_MB),
    )(x3, g, tgt)


def _offsets():
    u0 = 0
    v0 = DS
    b0 = 2 * DS
    c0 = b0 + DC
    x0 = c0 + DC
    q0 = x0 + DC
    return u0, v0, b0, c0, x0, q0


def _tri_mask(lower):
    r = lax.broadcasted_iota(jnp.int32, (CHUNK, CHUNK), 0)
    c = lax.broadcasted_iota(jnp.int32, (CHUNK, CHUNK), 1)
    return (r >= c) if lower else (c >= r)


def _layer_norm_stats(vg):
    mu = jnp.mean(vg, axis=-1, keepdims=True)
    vc = vg - mu
    rstd = lax.rsqrt(jnp.mean(vc * vc, axis=-1, keepdims=True) + EPS)
    return vc * rstd, rstd


def _softmax_rows(qh, kh):
    s = lax.dot_general(qh, kh, (((1,), (1,)), ((), ())), preferred_element_type=F32)
    m = jnp.max(s, axis=-1, keepdims=True)
    e = jnp.exp(s - m)
    return e / jnp.sum(e, axis=-1, keepdims=True)


def _mix_fwd(proj, kv, w_s, bs_t, ln_g, ln_b, conv_w, g_head, *, name):
    assert DS == DC
    tr = _pick(S, TR_MIX, CHUNK)
    n = S // tr
    nck = tr // CHUNK
    u0, v0, b0, c0, x0, q0 = _offsets()
    hb = tr // HALO

    def body(p_ref, cprev_ref, xprev_ref, kv_ref, ws_ref, bst_ref, lng_ref, lnb_ref, cw_ref, gh_ref,
             heads_ref, hn_ref, ycv_ref, buf_ref):
        i = pl.program_id(0)

        def emit(col, val):
            rs = lax.rsqrt(jnp.mean(val * val, axis=-1, keepdims=True) + EPS)
            heads_ref[:, col:col + HD] = val
            hn_ref[:, col:col + HD] = ((val * rs) * gh_ref[:, col:col + HD]).astype(BF16)

        vhat, _ = _layer_norm_stats(_gelu(p_ref[:, v0:v0 + DS]))
        vnb = (vhat * lng_ref[...] + lnb_ref[...]).astype(BF16)
        low = _tri_mask(True)
        for h in range(NSH):
            wt = jnp.where(low, ws_ref[h], 0.0).astype(BF16)
            bcol = bst_ref[:, h:h + 1]
            parts = []
            for c in range(nck):
                blk = vnb[c * CHUNK:(c + 1) * CHUNK, h * HD:(h + 1) * HD]
                parts.append(jnp.dot(wt, blk, preferred_element_type=F32) + bcol)
            mixed = parts[0] if nck == 1 else jnp.concatenate(parts, axis=0)
            emit(h * HD, _gelu(p_ref[:, u0 + h * HD:u0 + (h + 1) * HD]) * mixed)

        xc = p_ref[:, c0:c0 + DC] * p_ref[:, x0:x0 + DC]
        prev = cprev_ref[...] * xprev_ref[...]
        buf_ref[0:HALO, :] = jnp.where(i > 0, prev, 0.0)
        buf_ref[HALO:HALO + tr, :] = xc
        y = (cw_ref[2:3, :] * xc + cw_ref[1:2, :] * buf_ref[HALO - 1:HALO - 1 + tr, :]
             + cw_ref[0:1, :] * buf_ref[HALO - 2:HALO - 2 + tr, :])
        ycv_ref[...] = y
        cout = p_ref[:, b0:b0 + DC] * y
        for h in range(NCH):
            emit(DS + h * HD, cout[:, h * HD:(h + 1) * HD])

        for h in range(NMH):
            qh = (p_ref[:, q0 + h * HD:q0 + (h + 1) * HD] * SCALE).astype(BF16)
            kh = kv_ref[:, h * HD:(h + 1) * HD].astype(BF16)
            vh = kv_ref[:, DM + h * HD:DM + (h + 1) * HD].astype(BF16)
            p = _softmax_rows(qh, kh)
            emit(DS + DC + h * HD, jnp.dot(p.astype(BF16), vh, preferred_element_type=F32))

    full = lambda shape: pl.BlockSpec(shape, lambda i: (0,) * len(shape))
    halo_c = pl.BlockSpec((HALO, DC), lambda i: (jnp.maximum(i * hb - 1, 0), c0 // DC))
    halo_x = pl.BlockSpec((HALO, DC), lambda i: (jnp.maximum(i * hb - 1, 0), x0 // DC))
    return pl.pallas_call(
        body, name=name, grid=(n,),
        in_specs=[pl.BlockSpec((tr, DIN), lambda i: (i, 0)), halo_c, halo_x,
                  full((NMEM, 2 * DM)), full((NSH, CHUNK, CHUNK)), full((CHUNK, NSH)),
                  full((1, DS)), full((1, DS)), full((3, DC)), full((1, D))],
        out_specs=[pl.BlockSpec((tr, D), lambda i: (i, 0)), pl.BlockSpec((tr, D), lambda i: (i, 0)),
                   pl.BlockSpec((tr, DC), lambda i: (i, 0))],
        out_shape=[jax.ShapeDtypeStruct((S, D), F32), jax.ShapeDtypeStruct((S, D), BF16),
                   jax.ShapeDtypeStruct((S, DC), F32)],
        scratch_shapes=[pltpu.VMEM((tr + HALO, DC), F32)],
        compiler_params=_cp(("parallel",), VMEM_MB),
    )(proj, proj, proj, kv, w_s, bs_t, ln_g, ln_b, conv_w, g_head)


def _mix_bwd(dhn, heads, proj, ycv, kv, w_s, bs_t, ln_g, ln_b, conv_w, g_head, *, name):
    assert DS == DC
    tr = _pick(S, TR_MIX, CHUNK)
    n = S // tr
    nck = tr // CHUNK
    u0, v0, b0, c0, x0, q0 = _offsets()
    hb = tr // HALO
    last_hb = S // HALO - 1

    def body(dhn_ref, heads_ref, p_ref, ycv_ref, dhn_nx_ref, heads_nx_ref, b_nx_ref, kv_ref, ws_ref, bst_ref,
             lng_ref, lnb_ref, cw_ref, gh_ref,
             dp_ref, dkv_ref, dws_ref, dbs_ref, dlng_ref, dlnb_ref, dcw_ref, dgh_ref, buf_ref, dvn_ref):
        i = pl.program_id(0)

        @pl.when(i == 0)
        def _():
            dkv_ref[...] = jnp.zeros_like(dkv_ref)
            dws_ref[...] = jnp.zeros_like(dws_ref)
            dbs_ref[...] = jnp.zeros_like(dbs_ref)
            dlng_ref[...] = jnp.zeros_like(dlng_ref)
            dlnb_ref[...] = jnp.zeros_like(dlnb_ref)
            dcw_ref[...] = jnp.zeros_like(dcw_ref)
            dgh_ref[...] = jnp.zeros_like(dgh_ref)

        def head_bwd(a, dn, gh):
            rs = lax.rsqrt(jnp.mean(a * a, axis=-1, keepdims=True) + EPS)
            ah = a * rs
            t = dn * gh
            return rs * (t - ah * jnp.mean(t * ah, axis=-1, keepdims=True)), jnp.sum(dn * ah, axis=0, keepdims=True)

        def head_grad(col):
            da, dg = head_bwd(heads_ref[:, col:col + HD], dhn_ref[:, col:col + HD], gh_ref[:, col:col + HD])
            dgh_ref[:, col:col + HD] += dg
            return da

        v = p_ref[:, v0:v0 + DS]
        vhat, rstd = _layer_norm_stats(_gelu(v))
        vnb = (vhat * lng_ref[...] + lnb_ref[...]).astype(BF16)
        low = _tri_mask(True)
        ones = jnp.ones((HALO, HD), BF16)
        for h in range(NSH):
            w_h = ws_ref[h]
            wt = jnp.where(low, w_h, 0.0).astype(BF16)
            bcol = bst_ref[:, h:h + 1]
            da = head_grad(h * HD)
            u = p_ref[:, u0 + h * HD:u0 + (h + 1) * HD]
            ug = _gelu(u)
            dws = jnp.zeros((CHUNK, CHUNK), F32)
            dbs = jnp.zeros((HALO, CHUNK), F32)
            mixed_parts = []
            for c in range(nck):
                rows = slice(c * CHUNK, (c + 1) * CHUNK)
                blk = vnb[rows, h * HD:(h + 1) * HD]
                mixed_parts.append(jnp.dot(wt, blk, preferred_element_type=F32) + bcol)
                dmb = (da[rows] * ug[rows]).astype(BF16)
                dws = dws + lax.dot_general(dmb, blk, (((1,), (1,)), ((), ())), preferred_element_type=F32)
                dbs = dbs + lax.dot_general(ones, dmb, (((1,), (1,)), ((), ())), preferred_element_type=F32)
                dvn_ref[c * CHUNK:(c + 1) * CHUNK, h * HD:(h + 1) * HD] = lax.dot_general(
                    wt, dmb, (((0,), (0,)), ((), ())), preferred_element_type=F32)
            mixed = mixed_parts[0] if nck == 1 else jnp.concatenate(mixed_parts, axis=0)
            dp_ref[:, u0 + h * HD:u0 + (h + 1) * HD] = ((da * mixed) * _gelu_grad(u)).astype(BF16)
            dws_ref[h] += jnp.where(low, dws, 0.0)
            dbs_ref[h] += dbs
        dvn = dvn_ref[...]
        dlng_ref[...] += jnp.sum(dvn * vhat, axis=0, keepdims=True)
        dlnb_ref[...] += jnp.sum(dvn, axis=0, keepdims=True)
        dvh = dvn * lng_ref[...]
        dvg = rstd * (dvh - jnp.mean(dvh, axis=-1, keepdims=True)
                      - vhat * jnp.mean(dvh * vhat, axis=-1, keepdims=True))
        dp_ref[:, v0:v0 + DS] = (dvg * _gelu_grad(v)).astype(BF16)

        dc = jnp.concatenate([head_grad(DS + h * HD) for h in range(NCH)], axis=1)
        dc_nx = jnp.concatenate(
            [head_bwd(heads_nx_ref[:, h * HD:(h + 1) * HD], dhn_nx_ref[:, h * HD:(h + 1) * HD],
                      gh_ref[:, DS + h * HD:DS + (h + 1) * HD])[0] for h in range(NCH)], axis=1)
        bg = p_ref[:, b0:b0 + DC]
        cg = p_ref[:, c0:c0 + DC]
        xin = p_ref[:, x0:x0 + DC]
        dp_ref[:, b0:b0 + DC] = (dc * ycv_ref[...]).astype(BF16)
        dyv = dc * bg
        buf_ref[0:tr, :] = dyv
        buf_ref[tr:tr + HALO, :] = jnp.where(i < n - 1, dc_nx * b_nx_ref[...], 0.0)
        sh1 = buf_ref[1:1 + tr, :]
        sh0 = buf_ref[2:2 + tr, :]
        dxc = cw_ref[2:3, :] * dyv + cw_ref[1:2, :] * sh1 + cw_ref[0:1, :] * sh0
        xc = cg * xin
        dp_ref[:, c0:c0 + DC] = (dxc * xin).astype(BF16)
        dp_ref[:, x0:x0 + DC] = (dxc * cg).astype(BF16)
        dcw_ref[0:1, :] += jnp.sum(sh0 * xc, axis=0, keepdims=True)
        dcw_ref[1:2, :] += jnp.sum(sh1 * xc, axis=0, keepdims=True)
        dcw_ref[2:3, :] += jnp.sum(dyv * xc, axis=0, keepdims=True)

        for h in range(NMH):
            do = head_grad(DS + DC + h * HD).astype(BF16)
            qh = (p_ref[:, q0 + h * HD:q0 + (h + 1) * HD] * SCALE).astype(BF16)
            kh = kv_ref[:, h * HD:(h + 1) * HD].astype(BF16)
            vh = kv_ref[:, DM + h * HD:DM + (h + 1) * HD].astype(BF16)
            p = _softmax_rows(qh, kh)
            dpr = lax.dot_general(do, vh, (((1,), (1,)), ((), ())), preferred_element_type=F32)
            ds = (p * (dpr - jnp.sum(dpr * p, axis=-1, keepdims=True))).astype(BF16)
            dp_ref[:, q0 + h * HD:q0 + (h + 1) * HD] = (
                jnp.dot(ds, kh, preferred_element_type=F32) * SCALE).astype(BF16)
            dkv_ref[:, h * HD:(h + 1) * HD] += lax.dot_general(
                ds, qh, (((0,), (0,)), ((), ())), preferred_element_type=F32)
            dkv_ref[:, DM + h * HD:DM + (h + 1) * HD] += lax.dot_general(
                p.astype(BF16), do, (((0,), (0,)), ((), ())), preferred_element_type=F32)

    full = lambda shape: pl.BlockSpec(shape, lambda i: (0,) * len(shape))
    row = lambda c: pl.BlockSpec((tr, c), lambda i: (i, 0))
    nxt = lambda col: pl.BlockSpec((HALO, DC), lambda i: (jnp.minimum((i + 1) * hb, last_hb), col))
    return pl.pallas_call(
        body, name=name, grid=(n,),
        in_specs=[row(D), row(D), row(DIN), row(DC), nxt(DS // DC), nxt(DS // DC), nxt(b0 // DC),
                  full((NMEM, 2 * DM)), full((NSH, CHUNK, CHUNK)), full((CHUNK, NSH)),
                  full((1, DS)), full((1, DS)), full((3, DC)), full((1, D))],
        out_specs=[row(DIN), full((NMEM, 2 * DM)), full((NSH, CHUNK, CHUNK)), full((NSH, HALO, CHUNK)),
                   full((1, DS)), full((1, DS)), full((HALO, DC)), full((1, D))],
        out_shape=[jax.ShapeDtypeStruct((S, DIN), BF16), jax.ShapeDtypeStruct((NMEM, 2 * DM), F32),
                   jax.ShapeDtypeStruct((NSH, CHUNK, CHUNK), F32), jax.ShapeDtypeStruct((NSH, HALO, CHUNK), F32),
                   jax.ShapeDtypeStruct((1, DS), F32), jax.ShapeDtypeStruct((1, DS), F32),
                   jax.ShapeDtypeStruct((HALO, DC), F32), jax.ShapeDtypeStruct((1, D), F32)],
        scratch_shapes=[pltpu.VMEM((tr + HALO, DC), F32), pltpu.VMEM((tr, DS), F32)],
        compiler_params=_cp(("arbitrary",), VMEM_MB),
    )(dhn, heads, proj, ycv, dhn, heads, proj, kv, w_s, bs_t, ln_g, ln_b, conv_w, g_head)


def _place():
    x, y, c = lax.axis_index("x"), lax.axis_index("y"), lax.axis_index("c")
    chips = [(1 - x, y), (x, 1 - y), (1 - x, 1 - y)]
    return x, y, c, chips


ANY = pl.BlockSpec(memory_space=pl.ANY)


def _allgather_shards(shards, *, name):
    nw = len(shards)

    def body(*refs):
        ins, outs = refs[:nw], refs[nw:2 * nw]
        send1, recv1, send2, recv2, lsem = refs[2 * nw:]
        x, y, c, chips = _place()
        s = 2 * x + y
        me, sibling = (x, y, c), (x, y, 1 - c)

        def half(w, which):
            hr = shards[w].shape[0] // 2
            return pl.ds(which * hr, hr)

        def ici(w, j, slot, to, src=None):
            dst = outs[w].at[slot, half(w, c)]
            return pltpu.make_async_remote_copy(
                src_ref=dst if src is None else src, dst_ref=dst, send_sem=send1.at[3 * w + j],
                recv_sem=recv1.at[3 * w + j], device_id=to, device_id_type=MESH)

        def d2d(w, j, slot, which, to):
            rows = outs[w].at[slot, half(w, which)]
            return pltpu.make_async_remote_copy(
                src_ref=rows, dst_ref=rows, send_sem=send2.at[3 * w + j], recv_sem=recv2.at[3 * w + j],
                device_id=to, device_id_type=MESH)

        mine = [pltpu.make_async_copy(ins[w], outs[w].at[s], lsem.at[w]) for w in range(nw)]
        for cp in mine:
            cp.start()
        first = [ici(w, j, s, (cx, cy, c), src=ins[w].at[half(w, c)])
                 for j, (cx, cy) in enumerate(chips) for w in range(nw)]
        for cp in first:
            cp.start()
        passed = []
        for j, (cx, cy) in enumerate(chips):
            for w in range(nw):
                ici(w, j, 2 * cx + cy, me).wait_recv()
                cp = d2d(w, j, 2 * cx + cy, c, sibling)
                cp.start()
                passed.append(cp)
        for j, (cx, cy) in enumerate(chips):
            for w in range(nw):
                d2d(w, j, 2 * cx + cy, 1 - c, me).wait_recv()
        for cp in first + passed:
            cp.wait_send()
        for cp in mine:
            cp.wait()

    return pl.pallas_call(
        body, name=name,
        in_specs=[ANY] * nw, out_specs=[ANY] * nw,
        out_shape=[jax.ShapeDtypeStruct((NCHIP,) + a.shape, a.dtype) for a in shards],
        scratch_shapes=[pltpu.SemaphoreType.DMA((3 * nw,)), pltpu.SemaphoreType.DMA((3 * nw,)),
                        pltpu.SemaphoreType.DMA((3 * nw,)), pltpu.SemaphoreType.DMA((3 * nw,)),
                        pltpu.SemaphoreType.DMA((nw,))],
            )(*shards)


def _sibling_exchange(arrs, *, name):
    nw = len(arrs)

    def body(*refs):
        ins, outs = refs[:nw], refs[nw:2 * nw]
        send, recv = refs[2 * nw:]
        x, y, c, _ = _place()
        cps = [pltpu.make_async_remote_copy(src_ref=ins[w], dst_ref=outs[w], send_sem=send.at[w], recv_sem=recv.at[w],
                                            device_id=(x, y, 1 - c), device_id_type=MESH) for w in range(nw)]
        for cp in cps:
            cp.start()
        for cp in cps:
            cp.wait()

    return pl.pallas_call(
        body, name=name, in_specs=[ANY] * nw, out_specs=[ANY] * nw,
        out_shape=[jax.ShapeDtypeStruct(a.shape, a.dtype) for a in arrs],
        scratch_shapes=[pltpu.SemaphoreType.DMA((nw,)), pltpu.SemaphoreType.DMA((nw,))],
            )(*arrs)


def _chip_exchange(parts, *, name):
    nw = len(parts)

    def body(*refs):
        ins, outs = refs[:nw], refs[nw:2 * nw]
        send, recv, lsem = refs[2 * nw:]
        x, y, c, chips = _place()
        s = 2 * x + y
        mine = [pltpu.make_async_copy(ins[w].at[s], outs[w].at[s], lsem.at[w]) for w in range(nw)]
        for cp in mine:
            cp.start()
        cps = [pltpu.make_async_remote_copy(
            src_ref=ins[w].at[2 * cx + cy], dst_ref=outs[w].at[s], send_sem=send.at[3 * w + j],
            recv_sem=recv.at[3 * w + j], device_id=(cx, cy, c), device_id_type=MESH)
            for j, (cx, cy) in enumerate(chips) for w in range(nw)]
        for cp in cps:
            cp.start()
        for j, (cx, cy) in enumerate(chips):
            for w in range(nw):
                slot = outs[w].at[2 * cx + cy]
                pltpu.make_async_remote_copy(
                    src_ref=slot, dst_ref=slot, send_sem=send.at[3 * w + j], recv_sem=recv.at[3 * w + j],
                    device_id=(x, y, c), device_id_type=MESH).wait_recv()
        for cp in cps:
            cp.wait_send()
        for cp in mine:
            cp.wait()

    return pl.pallas_call(
        body, name=name, in_specs=[ANY] * nw, out_specs=[ANY] * nw,
        out_shape=[jax.ShapeDtypeStruct(a.shape, a.dtype) for a in parts],
        scratch_shapes=[pltpu.SemaphoreType.DMA((3 * nw,)), pltpu.SemaphoreType.DMA((3 * nw,)),
                        pltpu.SemaphoreType.DMA((nw,))],
            )(*parts)


def _share_halves(halves, *, name):
    nw = len(halves)

    def body(*refs):
        ins, outs = refs[:nw], refs[nw:2 * nw]
        send, recv, lsem = refs[2 * nw:]
        x, y, c, _ = _place()

        def rows(w, which):
            hr = halves[w].shape[0]
            return outs[w].at[pl.ds(which * hr, hr)]

        mine = [pltpu.make_async_copy(ins[w], rows(w, c), lsem.at[w]) for w in range(nw)]
        for cp in mine:
            cp.start()
        cps = [pltpu.make_async_remote_copy(src_ref=ins[w], dst_ref=rows(w, c), send_sem=send.at[w],
                                            recv_sem=recv.at[w], device_id=(x, y, 1 - c), device_id_type=MESH)
               for w in range(nw)]
        for cp in cps:
            cp.start()
        for w in range(nw):
            pltpu.make_async_remote_copy(src_ref=rows(w, 1 - c), dst_ref=rows(w, 1 - c), send_sem=send.at[w],
                                         recv_sem=recv.at[w], device_id=(x, y, c), device_id_type=MESH).wait_recv()
        for cp in cps:
            cp.wait_send()
        for cp in mine:
            cp.wait()

    return pl.pallas_call(
        body, name=name, in_specs=[ANY] * nw, out_specs=[ANY] * nw,
        out_shape=[jax.ShapeDtypeStruct((2 * a.shape[0],) + a.shape[1:], a.dtype) for a in halves],
        scratch_shapes=[pltpu.SemaphoreType.DMA((nw,)), pltpu.SemaphoreType.DMA((nw,)),
                        pltpu.SemaphoreType.DMA((nw,))],
            )(*halves)


def _allreduce_small(p, *, name):
    R = p.shape[0]
    hr = R // 2

    def body(p_ref, out_ref, sib_ref, sum_ref, gat_ref, tot_ref, send, recv):
        x, y, c, chips = _place()
        s = 2 * x + y
        sibling = (x, y, 1 - c)
        rows = pl.ds(pl.multiple_of(c * hr, 8), hr)
        swap = pltpu.make_async_remote_copy(src_ref=p_ref, dst_ref=sib_ref, send_sem=send.at[0], recv_sem=recv.at[0],
                                            device_id=sibling, device_id_type=MESH)
        swap.start()
        swap.wait()
        sum_ref[...] = p_ref[...] + sib_ref[...]
        gat_ref[s] = sum_ref[rows, :]
        cps = [pltpu.make_async_remote_copy(src_ref=sum_ref.at[rows], dst_ref=gat_ref.at[s], send_sem=send.at[1 + j],
                                            recv_sem=recv.at[1 + j], device_id=(cx, cy, c), device_id_type=MESH)
               for j, (cx, cy) in enumerate(chips)]
        for cp in cps:
            cp.start()
        for cp in cps:
            cp.wait()
        tot_ref[...] = ((gat_ref[0] + gat_ref[1]) + gat_ref[2]) + gat_ref[3]
        out_ref[rows, :] = tot_ref[...]
        share = pltpu.make_async_remote_copy(src_ref=tot_ref, dst_ref=out_ref.at[rows], send_sem=send.at[4],
                                             recv_sem=recv.at[4], device_id=sibling, device_id_type=MESH)
        share.start()
        share.wait_send()
        other = out_ref.at[pl.ds(pl.multiple_of((1 - c) * hr, 8), hr)]
        pltpu.make_async_remote_copy(src_ref=other, dst_ref=other, send_sem=send.at[4], recv_sem=recv.at[4],
                                     device_id=(x, y, c), device_id_type=MESH).wait_recv()

    vmem = pl.BlockSpec(memory_space=pltpu.VMEM)
    return pl.pallas_call(
        body, name=name, in_specs=[vmem], out_specs=vmem,
        out_shape=jax.ShapeDtypeStruct((R, 128), F32),
        scratch_shapes=[pltpu.VMEM((R, 128), F32), pltpu.VMEM((R, 128), F32), pltpu.VMEM((NCHIP, hr, 128), F32),
                        pltpu.VMEM((hr, 128), F32), pltpu.SemaphoreType.DMA((5,)), pltpu.SemaphoreType.DMA((5,))],
            )(p)


def _select_half_bf16(g, other, add, *, name):
    _, R, C = g.shape
    hr = R // 2
    tr = _pick(hr, TR_EW, 16)
    nb = hr // tr
    c = lax.axis_index("c")
    which = jnp.reshape((1 - c) if other else c, (1,)).astype(jnp.int32)
    has_add = add is not None

    def body(w_ref, g_ref, *rest):
        o_ref = rest[-1]
        val = g_ref[...]
        if has_add:
            val = val + rest[0][...].astype(F32)
        o_ref[...] = val.astype(BF16)

    g_spec = pl.BlockSpec((None, tr, C), lambda j, i, w: (j, w[0] * nb + i, 0))
    o_spec = pl.BlockSpec((None, tr, C), lambda j, i, w: (j, i, 0))
    return pl.pallas_call(
        body, name=name,
        grid_spec=pltpu.PrefetchScalarGridSpec(
            num_scalar_prefetch=1, grid=(NCHIP, nb),
            in_specs=[g_spec] + ([o_spec] if has_add else []), out_specs=o_spec),
        out_shape=jax.ShapeDtypeStruct((NCHIP, hr, C), BF16),
        compiler_params=_cp(("parallel", "parallel"), VMEM_MB),
    )(which, g, *([add] if has_add else []))


def _sum_slots(r, *, name):
    _, R, C = r.shape
    tr = _pick(R, TR_EW, 16)

    def body(r_ref, o_ref):
        acc = r_ref[0].astype(F32) + r_ref[1].astype(F32)
        for j in range(2, NCHIP):
            acc = acc + r_ref[j].astype(F32)
        o_ref[...] = acc

    return pl.pallas_call(
        body, name=name, grid=(R // tr,),
        in_specs=[pl.BlockSpec((NCHIP, tr, C), lambda i: (0, i, 0))],
        out_specs=pl.BlockSpec((tr, C), lambda i: (i, 0)),
        out_shape=jax.ShapeDtypeStruct((R, C), F32),
        compiler_params=_cp(("parallel",), VMEM_MB),
    )(r)


def _adamw_math(w, g, m, v):
    m = ADAM_B1 * m + (1.0 - ADAM_B1) * g
    v = ADAM_B2 * v + (1.0 - ADAM_B2) * (g * g)
    m_hat = m / (1.0 - ADAM_B1 ** ADAM_STEP)
    v_hat = v / (1.0 - ADAM_B2 ** ADAM_STEP)
    delta = -ADAM_LR * (m_hat / (jnp.sqrt(v_hat) + ADAM_EPS) + ADAM_WD * w)
    return delta, m, v


def _adamw(w, g, m, v, *, name):
    R, C = w.shape
    tr = _pick(R, TR_EW, 8)
    row = pl.BlockSpec((tr, C), lambda i: (i, 0))

    def body(w_ref, g_ref, m_ref, v_ref, go_ref, d_ref, mo_ref, vo_ref):
        gv = g_ref[...]
        d, mn, vn = _adamw_math(w_ref[...], gv, m_ref[...], v_ref[...])
        go_ref[...] = gv
        d_ref[...] = d
        mo_ref[...] = mn
        vo_ref[...] = vn

    return pl.pallas_call(
        body, name=name, grid=(R // tr,), in_specs=[row] * 4, out_specs=[row] * 4,
        out_shape=[jax.ShapeDtypeStruct((R, C), F32)] * 4,
        compiler_params=_cp(("parallel",), VMEM_MB),
    )(w, g, m, v)


def _adamw_small(ws, gs, ms, vs, *, name):
    n = len(ws)

    def body(*refs):
        w_r, g_r, m_r, v_r = refs[:n], refs[n:2 * n], refs[2 * n:3 * n], refs[3 * n:4 * n]
        d_r, mo_r, vo_r = refs[4 * n:5 * n], refs[5 * n:6 * n], refs[6 * n:7 * n]
        for k in range(n):
            d, mn, vn = _adamw_math(w_r[k][...], g_r[k][...], m_r[k][...], v_r[k][...])
            d_r[k][...] = d
            mo_r[k][...] = mn
            vo_r[k][...] = vn

    shapes = [jax.ShapeDtypeStruct(w.shape, F32) for w in ws]
    res = pl.pallas_call(body, name=name, out_shape=shapes * 3)(*ws, *gs, *ms, *vs)
    return res[:n], res[n:2 * n], res[2 * n:]


_PACK_ROWS = 8


def _pack(parts):
    rows = []
    for a in parts:
        flat = a.reshape(-1)
        n = -(-flat.shape[0] // (_PACK_ROWS * 128)) * (_PACK_ROWS * 128)
        rows.append(jnp.pad(flat, (0, n - flat.shape[0])).reshape(-1, 128))
    total = sum(r.shape[0] for r in rows)
    if total % 16:
        rows.append(jnp.zeros((16 - total % 16, 128), F32))
    return jnp.concatenate(rows, axis=0)


def _unpack(p, shapes):
    out, r = [], 0
    for shp in shapes:
        n = math.prod(shp)
        nr = -(-n // (_PACK_ROWS * 128)) * _PACK_ROWS
        out.append(p[r:r + nr].reshape(-1)[:n].reshape(shp))
        r += nr
    return out


def kernel(x, mem, g_mix, w_in, ln_v_g, ln_v_b, w_s, b_s, conv_w, g_mem, w_kv, g_head, w_o, g_ffn, w_ffn1, w_ffn2, g_final, loss_target, m_g_mix, m_w_in, m_ln_v_g, m_ln_v_b, m_w_s, m_b_s, m_conv_w, m_g_mem, m_w_kv, m_g_head, m_w_o, m_g_ffn, m_w_ffn1, m_w_ffn2, m_g_final, v_g_mix, v_w_in, v_ln_v_g, v_ln_v_b, v_w_s, v_b_s, v_conv_w, v_g_mem, v_w_kv, v_g_head, v_w_o, v_g_ffn, v_w_ffn1, v_w_ffn2, v_g_final):
    sds = jax.ShapeDtypeStruct
    xi, yi = lax.axis_index("x"), lax.axis_index("y")
    shard = 2 * xi + yi
    x2d, mem2d, tgt = x[0], mem[0], loss_target[0]
    ws3, bs2 = w_s[0], b_s[0]
    g_final2 = g_final.reshape(1, D)
    dff4 = DFF // NCHIP
    din4 = DIN // NCHIP
    dcv4 = DC // NCHIP

    big = [w_in[0], w_kv[0], w_o[0], w_ffn1[0], w_ffn2[0]]
    big_names = ["w_in", "w_kv", "w_o", "w_ffn1", "w_ffn2"]
    shards_b = [_cast_bf16(w, name="cast_" + nm) for w, nm in zip(big, big_names)]
    conv_pad = jnp.pad(conv_w[0], ((0, 16 - 3), (0, 256 - dcv4)))
    win4, wkv4, wo4, w14, w24, conv4 = _allgather_shards(shards_b + [conv_pad], name="allgather_weights")
    w_in_full = win4.transpose(1, 0, 2).reshape(D, DIN)
    w_kv_full = wkv4.reshape(D, 2 * DM)
    w_o_full = wo4.reshape(D, D)
    w2_full = w24.reshape(DFF, D)
    conv_full = conv4[:, :3, :dcv4].transpose(1, 0, 2).reshape(3, DC)
    bs_t = bs2.T

    h = _rms_fwd(x2d, g_mix, name="rms_mix")
    (proj,) = _matmul(h, w_in_full, name="mm_proj", M=S, N=DIN, K=D, tn=DIN // 2, outs=[sds((S, DIN), F32)])
    mem_n = _rms_fwd(mem2d, g_mem, name="rms_mem")
    (kv,) = _matmul(mem_n, w_kv_full, name="mm_kv", M=NMEM, N=2 * DM, K=D, outs=[sds((NMEM, 2 * DM), F32)])
    heads, hn, ycv = _mix_fwd(proj, kv, ws3, bs_t, ln_v_g, ln_v_b, conv_full, g_head, name="mix_fwd")
    (x2,) = _matmul(hn, w_o_full, name="mm_wo", M=S, N=D, K=D, outs=[sds((S, D), F32)],
                    epi=lambda acc, res: (acc + res,), extras=[(x2d, _tile_spec())])
    h2 = _rms_fwd(x2, g_ffn, name="rms_ffn")

    def w1_cols(tn, tk):
        nb = dff4 // tn
        return pl.BlockSpec((None, tk, tn), lambda j, i, k: (j // nb, k, j % nb))

    f, act = _matmul(h2, w14, name="mm_ffn1", M=S, N=DFF, K=D, b_spec=w1_cols,
                     outs=[sds((S, DFF), F32), sds((S, DFF), BF16)],
                     epi=lambda acc: (acc, jnp.square(jnp.maximum(acc, 0.0))))
    (x3,) = _matmul(act, w2_full, name="mm_ffn2", M=S, N=D, K=DFF, outs=[sds((S, D), F32)],
                    epi=lambda acc, res: (acc + res,), extras=[(x2, _tile_spec())])

    dx3, dx3b, dg_final, loss11 = _loss_bwd(x3, g_final2, tgt, name="loss_bwd")
    (dfb,) = _matmul(dx3b, w2_full, name="mm_dact", tb=True, M=S, N=DFF, K=D, outs=[sds((S, DFF), BF16)],
                     epi=lambda acc, fv: (acc * (2.0 * jnp.maximum(fv, 0.0)),), extras=[(f, _tile_spec())])
    (dw2,) = _matmul(act, dx3b, name="mm_dw2", ta=True, M=DFF, N=D, K=S, outs=[sds((DFF, D), F32)])

    def dw1_out(tm, tn):
        nb = dff4 // tn
        return [pl.BlockSpec((None, tm, tn), lambda j, i, k: (j // nb, i, j % nb))]

    (dw1,) = _matmul(h2, dfb, name="mm_dw1", ta=True, M=D, N=DFF, K=S, outs=[sds((NCHIP, D, dff4), F32)],
                     out_specs=dw1_out)

    def w1_rows(tn, tk):
        kb = dff4 // tk
        return pl.BlockSpec((None, tn, tk), lambda j, i, k: (k // kb, j, k % kb))

    (dh2,) = _matmul(dfb, w14, name="mm_dh2", tb=True, M=S, N=D, K=DFF, b_spec=w1_rows, outs=[sds((S, D), F32)])
    dx2, dx2b, dg_ffn = _rms_bwd(dh2, x2, g_ffn, dx3, name="rms_ffn_bwd")
    (dhn,) = _matmul(dx2b, w_o_full, name="mm_dhn", tb=True, M=S, N=D, K=D, outs=[sds((S, D), F32)])
    (dwo,) = _matmul(hn, dx2b, name="mm_dwo", ta=True, M=D, N=D, K=S, outs=[sds((D, D), F32)])
    dproj, dkv, dws, dbs8, dlng, dlnb, dcw8, dgh = _mix_bwd(
        dhn, heads, proj, ycv, kv, ws3, bs_t, ln_v_g, ln_v_b, conv_full, g_head, name="mix_bwd")
    (dwin,) = _matmul(h, dproj, name="mm_dwin", ta=True, M=D, N=DIN, K=S, tn=DIN // 2, outs=[sds((D, DIN), F32)])
    (dh,) = _matmul(dproj, w_in_full, name="mm_dh", tb=True, M=S, N=D, K=DIN, tk=DIN, outs=[sds((S, D), F32)])
    dx, dg_mix = _rms_bwd(dh, x2d, g_mix, dx2, name="rms_mix_bwd", want_bf=False)
    (dwkv,) = _matmul(mem_n, dkv, name="mm_dwkv", ta=True, M=D, N=2 * DM, K=NMEM, outs=[sds((D, 2 * DM), F32)])
    (dmem_n,) = _matmul(dkv, w_kv_full, name="mm_dmem", tb=True, M=NMEM, N=D, K=2 * DM, outs=[sds((NMEM, D), F32)])
    (dg_mem,) = _rms_bwd(dmem_n, mem2d, g_mem, None, name="rms_mem_bwd", want_dx=False)

    loss = lax.psum(loss11[0, 0], ("x", "y", "c"))

    grads4 = [dwin.reshape(D, NCHIP, din4).transpose(1, 0, 2), dwkv.reshape(NCHIP, D // NCHIP, 2 * DM),
              dwo.reshape(NCHIP, D // NCHIP, D), dw1, dw2.reshape(NCHIP, dff4, D)]
    to_sib = [_select_half_bf16(g, True, None, name="rs_send_" + nm) for g, nm in zip(grads4, big_names)]
    from_sib = _sibling_exchange(to_sib, name="rs_sibling")
    chip_part = [_select_half_bf16(g, False, r, name="rs_add_" + nm) for g, r, nm in zip(grads4, from_sib, big_names)]
    gathered = _chip_exchange(chip_part, name="rs_chips")
    halves = [_sum_slots(r, name="rs_sum_" + nm) for r, nm in zip(gathered, big_names)]
    reduced = _share_halves(halves, name="rs_share")

    big_m = [m_w_in[0], m_w_kv[0], m_w_o[0], m_w_ffn1[0], m_w_ffn2[0]]
    big_v = [v_w_in[0], v_w_kv[0], v_w_o[0], v_w_ffn1[0], v_w_ffn2[0]]
    big_out = {nm: _adamw(w, g, m, v, name="adamw_" + nm)
               for nm, w, g, m, v in zip(big_names, big, reduced, big_m, big_v)}

    small_names = ["g_mix", "ln_v_g", "ln_v_b", "w_s", "b_s", "conv_w", "g_mem", "g_head", "g_ffn", "g_final"]
    small_part = [dg_mix, dlng, dlnb, dws, dbs8[:, 0, :], dcw8[:3], dg_mem, dgh, dg_ffn, dg_final]
    small_shapes = [(1, D), (1, DS), (1, DS), (NSH, CHUNK, CHUNK), (NSH, CHUNK), (3, DC), (1, D), (1, D), (1, D), (1, D)]
    total = _allreduce_small(_pack(small_part), name="allreduce_small")
    small_g = _unpack(total, small_shapes)
    small_g[5] = lax.dynamic_slice(small_g[5], (0, shard * dcv4), (3, dcv4))
    small_w = [g_mix, ln_v_g, ln_v_b, ws3, bs2, conv_w[0], g_mem, g_head, g_ffn, g_final2]
    small_m = [m_g_mix, m_ln_v_g, m_ln_v_b, m_w_s[0], m_b_s[0], m_conv_w[0], m_g_mem, m_g_head, m_g_ffn,
               m_g_final.reshape(1, D)]
    small_v = [v_g_mix, v_ln_v_g, v_ln_v_b, v_w_s[0], v_b_s[0], v_conv_w[0], v_g_mem, v_g_head, v_g_ffn,
               v_g_final.reshape(1, D)]
    s_delta, s_m, s_v = _adamw_small(small_w, small_g, small_m, small_v, name="adamw_small")
    small_out = {nm: (g, d, mn, vn) for nm, g, d, mn, vn in zip(small_names, small_g, s_delta, s_m, s_v)}

    order = ["g_mix", "w_in", "ln_v_g", "ln_v_b", "w_s", "b_s", "conv_w", "g_mem", "w_kv", "g_head", "w_o",
             "g_ffn", "w_ffn1", "w_ffn2", "g_final"]
    like = dict(g_mix=g_mix, w_in=w_in, ln_v_g=ln_v_g, ln_v_b=ln_v_b, w_s=w_s, b_s=b_s, conv_w=conv_w, g_mem=g_mem,
                w_kv=w_kv, g_head=g_head, w_o=w_o, g_ffn=g_ffn, w_ffn1=w_ffn1, w_ffn2=w_ffn2, g_final=g_final)
    res = {**big_out, **small_out}
    outs = [loss, dx[None]]
    for k in range(4):
        outs += [res[nm][k].reshape(like[nm].shape) for nm in order]
    return tuple(outs)
```

```python
import functools
import math

import jax
import jax.numpy as jnp
from jax import lax
from jax.experimental import pallas as pl
from jax.experimental.pallas import tpu as pltpu

F32 = jnp.float32
BF16 = jnp.bfloat16
MESH = pl.DeviceIdType.MESH

D = 2048
S = 2048
HD = 128
NH = D // HD
NMH = 4
NSH = (NH - NMH) // 2
NCH = NH - NMH - NSH
DS = NSH * HD
DC = NCH * HD
DM = NMH * HD
DIN = 2 * DS + 3 * DC + DM
CHUNK = 128
NMEM = 256
DFF = 4 * D
EPS = 1e-6
NCHIP = 4
SCALE = HD ** -0.5

ADAM_LR = 0.001
ADAM_B1 = 0.9
ADAM_B2 = 0.999
ADAM_EPS = 1e-08
ADAM_WD = 0.01
ADAM_STEP = 10

TR_EW = 256
TR_MIX = 256
TM = 512
TN = 1024
TK = 2048
VMEM_MB = 56
HALO = 8


def _pick(n, target, q=128):
    best = None
    for t in range(q, min(n, target) + 1, q):
        if n % t == 0:
            best = t
    return n if best is None else best


def _cp(sem=None, vmem_mb=None, **kw):
    d = dict(kw)
    if sem is not None:
        d["dimension_semantics"] = sem
    if vmem_mb is not None:
        d["vmem_limit_bytes"] = vmem_mb << 20
    return pltpu.CompilerParams(**d)


def _gelu(x):
    z = 0.7978845608028654 * (x + 0.044715 * (x * x * x))
    return 0.5 * x * (1.0 + jnp.tanh(z))


def _gelu_grad(x):
    x2 = x * x
    t = jnp.tanh(0.7978845608028654 * (x + 0.044715 * (x2 * x)))
    return 0.5 * (1.0 + t) + 0.5 * x * (1.0 - t * t) * (0.7978845608028654 * (1.0 + 3.0 * 0.044715 * x2))


def _matmul(a, b, *, name, ta=False, tb=False, M, N, K, tm=None, tn=None, tk=None, outs, epi=None,
            extras=(), b_spec=None, out_specs=None):
    tm = _pick(M, TM if tm is None else tm, 8)
    tn = _pick(N, TN if tn is None else tn)
    tk = _pick(K, TK if tk is None else tk)
    nk = K // tk
    grid = (N // tn, M // tm, nk)
    a_spec = (pl.BlockSpec((tk, tm), lambda j, i, k: (k, i)) if ta
              else pl.BlockSpec((tm, tk), lambda j, i, k: (i, k)))
    if b_spec is None:
        b_spec = (pl.BlockSpec((tn, tk), lambda j, i, k: (j, k)) if tb
                  else pl.BlockSpec((tk, tn), lambda j, i, k: (k, j)))
    else:
        b_spec = b_spec(tn, tk)
    if out_specs is None:
        out_specs = [pl.BlockSpec((tm, tn), lambda j, i, k: (i, j)) for _ in outs]
    else:
        out_specs = out_specs(tm, tn)
    dn = (((0 if ta else 1,), (1 if tb else 0,)), ((), ()))
    n_ex, n_out = len(extras), len(outs)

    def body(*refs):
        a_ref, b_ref = refs[0], refs[1]
        ex = refs[2:2 + n_ex]
        o = refs[2 + n_ex:2 + n_ex + n_out]
        acc = refs[2 + n_ex + n_out:]
        part = lax.dot_general(a_ref[...].astype(BF16), b_ref[...].astype(BF16), dn,
                               preferred_element_type=F32)

        def finish(val):
            res = (val,) if epi is None else epi(val, *[e[...] for e in ex])
            for r, o_ref in zip(res, o):
                o_ref[...] = r.astype(o_ref.dtype)

        if nk == 1:
            finish(part)
        else:
            k = pl.program_id(2)

            @pl.when(k == 0)
            def _():
                acc[0][...] = part

            @pl.when(k > 0)
            def _():
                acc[0][...] += part

            @pl.when(k == nk - 1)
            def _():
                finish(acc[0][...])

    return pl.pallas_call(
        body, name=name, grid=grid,
        in_specs=[a_spec, b_spec] + [sp(tm, tn) for _, sp in extras],
        out_specs=out_specs, out_shape=outs,
        scratch_shapes=([pltpu.VMEM((tm, tn), F32)] if nk > 1 else []),
        compiler_params=_cp(("parallel", "parallel", "arbitrary"), VMEM_MB),
    )(a, b, *[arr for arr, _ in extras])


def _tile_spec():
    return lambda tm, tn: pl.BlockSpec((tm, tn), lambda j, i, k: (i, j))


def _cast_into_slot(w, slot, *, name):
    R, C = w.shape
    tr = _pick(R, TR_EW, 16)

    def body(s_ref, w_ref, o_ref):
        o_ref[...] = w_ref[...].astype(BF16)

    return pl.pallas_call(
        body, name=name,
        grid_spec=pltpu.PrefetchScalarGridSpec(
            num_scalar_prefetch=1, grid=(R // tr,),
            in_specs=[pl.BlockSpec((tr, C), lambda i, s: (i, 0))],
            out_specs=pl.BlockSpec((None, tr, C), lambda i, s: (s[0], i, 0))),
        out_shape=jax.ShapeDtypeStruct((NCHIP, R, C), BF16),
        compiler_params=_cp(("parallel",), VMEM_MB),
    )(slot, w)


def _rms_fwd(x, g, *, name):
    R, C = x.shape
    tr = _pick(R, TR_EW, 16)

    def body(x_ref, g_ref, o_ref):
        xv = x_ref[...]
        r = lax.rsqrt(jnp.mean(xv * xv, axis=-1, keepdims=True) + EPS)
        o_ref[...] = ((xv * r) * g_ref[...]).astype(BF16)

    return pl.pallas_call(
        body, name=name, grid=(R // tr,),
        in_specs=[pl.BlockSpec((tr, C), lambda i: (i, 0)), pl.BlockSpec((1, C), lambda i: (0, 0))],
        out_specs=pl.BlockSpec((tr, C), lambda i: (i, 0)),
        out_shape=jax.ShapeDtypeStruct((R, C), BF16),
        compiler_params=_cp(("parallel",), VMEM_MB),
    )(x, g)


def _rms_bwd(dh, x, g, dres, *, name, want_dx=True, want_bf=True):
    R, C = x.shape
    tr = _pick(R, TR_EW, 16)
    has_res = dres is not None
    row = pl.BlockSpec((tr, C), lambda i: (i, 0))
    vec = pl.BlockSpec((1, C), lambda i: (0, 0))

    def body(*refs):
        dh_ref, x_ref, g_ref = refs[:3]
        pos = 3
        dres_ref = None
        if has_res:
            dres_ref = refs[pos]
            pos += 1
        outs = refs[pos:]
        i = pl.program_id(0)
        xv = x_ref[...]
        r = lax.rsqrt(jnp.mean(xv * xv, axis=-1, keepdims=True) + EPS)
        xh = xv * r
        dhv = dh_ref[...]
        dg_ref = outs[-1]
        dgp = jnp.sum(dhv * xh, axis=0, keepdims=True)

        @pl.when(i == 0)
        def _():
            dg_ref[...] = dgp

        @pl.when(i > 0)
        def _():
            dg_ref[...] += dgp

        if want_dx:
            t = dhv * g_ref[...]
            dx = r * (t - xh * jnp.mean(t * xh, axis=-1, keepdims=True))
            if has_res:
                dx = dx + dres_ref[...]
            outs[0][...] = dx
            if want_bf:
                outs[1][...] = dx.astype(BF16)

    in_specs = [row, row, vec] + ([row] if has_res else [])
    out_specs, out_shape = [], []
    if want_dx:
        out_specs.append(row)
        out_shape.append(jax.ShapeDtypeStruct((R, C), F32))
        if want_bf:
            out_specs.append(row)
            out_shape.append(jax.ShapeDtypeStruct((R, C), BF16))
    out_specs.append(vec)
    out_shape.append(jax.ShapeDtypeStruct((1, C), F32))
    args = [dh, x, g] + ([dres] if has_res else [])
    return pl.pallas_call(
        body, name=name, grid=(R // tr,), in_specs=in_specs, out_specs=out_specs, out_shape=out_shape,
        compiler_params=_cp(("arbitrary",), VMEM_MB),
    )(*args)


def _loss_bwd(x3, g, tgt, *, name):
    R, C = x3.shape
    tr = _pick(R, TR_EW, 16)
    n = R // tr
    row = pl.BlockSpec((tr, C), lambda i: (i, 0))
    vec = pl.BlockSpec((1, C), lambda i: (0, 0))

    def body(x_ref, g_ref, t_ref, dx_ref, dxb_ref, dg_ref, loss_ref, acc_ref):
        i = pl.program_id(0)
        xv = x_ref[...]
        gv = g_ref[...]
        r = lax.rsqrt(jnp.mean(xv * xv, axis=-1, keepdims=True) + EPS)
        xh = xv * r
        e = xh * gv - t_ref[...]
        dy = e * (1.0 / C)
        sq = jnp.sum(e * e, axis=0, keepdims=True)
        dgp = jnp.sum(dy * xh, axis=0, keepdims=True)

        @pl.when(i == 0)
        def _():
            acc_ref[...] = sq
            dg_ref[...] = dgp

        @pl.when(i > 0)
        def _():
            acc_ref[...] += sq
            dg_ref[...] += dgp

        t = dy * gv
        dx = r * (t - xh * jnp.mean(t * xh, axis=-1, keepdims=True))
        dx_ref[...] = dx
        dxb_ref[...] = dx.astype(BF16)

        @pl.when(i == n - 1)
        def _():
            loss_ref[...] = jnp.sum(acc_ref[...], axis=-1, keepdims=True) * (0.5 / C)

    return pl.pallas_call(
        body, name=name, grid=(n,),
        in_specs=[row, vec, row],
        out_specs=[row, row, vec, pl.BlockSpec((1, 1), lambda i: (0, 0))],
        out_shape=[jax.ShapeDtypeStruct((R, C), F32), jax.ShapeDtypeStruct((R, C), BF16),
                   jax.ShapeDtypeStruct((1, C), F32), jax.ShapeDtypeStruct((1, 1), F32)],
        scratch_shapes=[pltpu.VMEM((1, C), F32)],
        compiler_params=_cp(("arbitrary",), VMEM_MB),
    )(x3, g, tgt)


def _offsets():
    u0 = 0
    v0 = DS
    b0 = 2 * DS
    c0 = b0 + DC
    x0 = c0 + DC
    q0 = x0 + DC
    return u0, v0, b0, c0, x0, q0


def _tri_mask(lower):
    r = lax.broadcasted_iota(jnp.int32, (CHUNK, CHUNK), 0)
    c = lax.broadcasted_iota(jnp.int32, (CHUNK, CHUNK), 1)
    return (r >= c) if lower else (c >= r)


def _layer_norm_stats(vg):
    mu = jnp.mean(vg, axis=-1, keepdims=True)
    vc = vg - mu
    rstd = lax.rsqrt(jnp.mean(vc * vc, axis=-1, keepdims=True) + EPS)
    return vc * rstd, rstd


def _softmax_rows(qh, kh):
    s = lax.dot_general(qh, kh, (((1,), (1,)), ((), ())), preferred_element_type=F32)
    m = jnp.max(s, axis=-1, keepdims=True)
    e = jnp.exp(s - m)
    return e / jnp.sum(e, axis=-1, keepdims=True)


def _mix_fwd(proj, kv, w_s, bs_t, ln_g, ln_b, conv_w, g_head, *, name):
    assert DS == DC
    tr = _pick(S, TR_MIX, CHUNK)
    n = S // tr
    nck = tr // CHUNK
    u0, v0, b0, c0, x0, q0 = _offsets()
    hb = tr // HALO

    def body(p_ref, cprev_ref, xprev_ref, kv_ref, ws_ref, bst_ref, lng_ref, lnb_ref, cw_ref, gh_ref,
             heads_ref, hn_ref, ycv_ref, buf_ref):
        i = pl.program_id(0)

        def emit(col, val):
            rs = lax.rsqrt(jnp.mean(val * val, axis=-1, keepdims=True) + EPS)
            heads_ref[:, col:col + HD] = val
            hn_ref[:, col:col + HD] = ((val * rs) * gh_ref[:, col:col + HD]).astype(BF16)

        vhat, _ = _layer_norm_stats(_gelu(p_ref[:, v0:v0 + DS]))
        vnb = (vhat * lng_ref[...] + lnb_ref[...]).astype(BF16)
        low = _tri_mask(True)
        for h in range(NSH):
            wt = jnp.where(low, ws_ref[h], 0.0).astype(BF16)
            bcol = bst_ref[:, h:h + 1]
            parts = []
            for c in range(nck):
                blk = vnb[c * CHUNK:(c + 1) * CHUNK, h * HD:(h + 1) * HD]
                parts.append(jnp.dot(wt, blk, preferred_element_type=F32) + bcol)
            mixed = parts[0] if nck == 1 else jnp.concatenate(parts, axis=0)
            emit(h * HD, _gelu(p_ref[:, u0 + h * HD:u0 + (h + 1) * HD]) * mixed)

        xc = p_ref[:, c0:c0 + DC] * p_ref[:, x0:x0 + DC]
        prev = cprev_ref[...] * xprev_ref[...]
        buf_ref[0:HALO, :] = jnp.where(i > 0, prev, 0.0)
        buf_ref[HALO:HALO + tr, :] = xc
        y = (cw_ref[2:3, :] * xc + cw_ref[1:2, :] * buf_ref[HALO - 1:HALO - 1 + tr, :]
             + cw_ref[0:1, :] * buf_ref[HALO - 2:HALO - 2 + tr, :])
        ycv_ref[...] = y
        cout = p_ref[:, b0:b0 + DC] * y
        for h in range(NCH):
            emit(DS + h * HD, cout[:, h * HD:(h + 1) * HD])

        for h in range(NMH):
            qh = (p_ref[:, q0 + h * HD:q0 + (h + 1) * HD] * SCALE).astype(BF16)
            kh = kv_ref[:, h * HD:(h + 1) * HD].astype(BF16)
            vh = kv_ref[:, DM + h * HD:DM + (h + 1) * HD].astype(BF16)
            p = _softmax_rows(qh, kh)
            emit(DS + DC + h * HD, jnp.dot(p.astype(BF16), vh, preferred_element_type=F32))

    full = lambda shape: pl.BlockSpec(shape, lambda i: (0,) * len(shape))
    halo_c = pl.BlockSpec((HALO, DC), lambda i: (jnp.maximum(i * hb - 1, 0), c0 // DC))
    halo_x = pl.BlockSpec((HALO, DC), lambda i: (jnp.maximum(i * hb - 1, 0), x0 // DC))
    return pl.pallas_call(
        body, name=name, grid=(n,),
        in_specs=[pl.BlockSpec((tr, DIN), lambda i: (i, 0)), halo_c, halo_x,
                  full((NMEM, 2 * DM)), full((NSH, CHUNK, CHUNK)), full((CHUNK, NSH)),
                  full((1, DS)), full((1, DS)), full((3, DC)), full((1, D))],
        out_specs=[pl.BlockSpec((tr, D), lambda i: (i, 0)), pl.BlockSpec((tr, D), lambda i: (i, 0)),
                   pl.BlockSpec((tr, DC), lambda i: (i, 0))],
        out_shape=[jax.ShapeDtypeStruct((S, D), F32), jax.ShapeDtypeStruct((S, D), BF16),
                   jax.ShapeDtypeStruct((S, DC), F32)],
        scratch_shapes=[pltpu.VMEM((tr + HALO, DC), F32)],
        compiler_params=_cp(("parallel",), VMEM_MB),
    )(proj, proj, proj, kv, w_s, bs_t, ln_g, ln_b, conv_w, g_head)


def _mix_bwd(dhn, heads, proj, ycv, kv, w_s, bs_t, ln_g, ln_b, conv_w, g_head, *, name):
    assert DS == DC
    tr = _pick(S, TR_MIX, CHUNK)
    n = S // tr
    nck = tr // CHUNK
    u0, v0, b0, c0, x0, q0 = _offsets()
    hb = tr // HALO
    last_hb = S // HALO - 1

    def body(dhn_ref, heads_ref, p_ref, ycv_ref, dhn_nx_ref, heads_nx_ref, b_nx_ref, kv_ref, ws_ref, bst_ref,
             lng_ref, lnb_ref, cw_ref, gh_ref,
             dp_ref, dkv_ref, dws_ref, dbs_ref, dlng_ref, dlnb_ref, dcw_ref, dgh_ref, buf_ref, dvn_ref):
        i = pl.program_id(0)

        @pl.when(i == 0)
        def _():
            dkv_ref[...] = jnp.zeros_like(dkv_ref)
            dws_ref[...] = jnp.zeros_like(dws_ref)
            dbs_ref[...] = jnp.zeros_like(dbs_ref)
            dlng_ref[...] = jnp.zeros_like(dlng_ref)
            dlnb_ref[...] = jnp.zeros_like(dlnb_ref)
            dcw_ref[...] = jnp.zeros_like(dcw_ref)
            dgh_ref[...] = jnp.zeros_like(dgh_ref)

        def head_bwd(a, dn, gh):
            rs = lax.rsqrt(jnp.mean(a * a, axis=-1, keepdims=True) + EPS)
            ah = a * rs
            t = dn * gh
            return rs * (t - ah * jnp.mean(t * ah, axis=-1, keepdims=True)), jnp.sum(dn * ah, axis=0, keepdims=True)

        def head_grad(col):
            da, dg = head_bwd(heads_ref[:, col:col + HD], dhn_ref[:, col:col + HD], gh_ref[:, col:col + HD])
            dgh_ref[:, col:col + HD] += dg
            return da

        v = p_ref[:, v0:v0 + DS]
        vhat, rstd = _layer_norm_stats(_gelu(v))
        vnb = (vhat * lng_ref[...] + lnb_ref[...]).astype(BF16)
        low = _tri_mask(True)
        ones = jnp.ones((HALO, HD), BF16)
        for h in range(NSH):
            w_h = ws_ref[h]
            wt = jnp.where(low, w_h, 0.0).astype(BF16)
            bcol = bst_ref[:, h:h + 1]
            da = head_grad(h * HD)
            u = p_ref[:, u0 + h * HD:u0 + (h + 1) * HD]
            ug = _gelu(u)
            dws = jnp.zeros((CHUNK, CHUNK), F32)
            dbs = jnp.zeros((HALO, CHUNK), F32)
            mixed_parts = []
            for c in range(nck):
                rows = slice(c * CHUNK, (c + 1) * CHUNK)
                blk = vnb[rows, h * HD:(h + 1) * HD]
                mixed_parts.append(jnp.dot(wt, blk, preferred_element_type=F32) + bcol)
                dmb = (da[rows] * ug[rows]).astype(BF16)
                dws = dws + lax.dot_general(dmb, blk, (((1,), (1,)), ((), ())), preferred_element_type=F32)
                dbs = dbs + lax.dot_general(ones, dmb, (((1,), (1,)), ((), ())), preferred_element_type=F32)
                dvn_ref[c * CHUNK:(c + 1) * CHUNK, h * HD:(h + 1) * HD] = lax.dot_general(
                    wt, dmb, (((0,), (0,)), ((), ())), preferred_element_type=F32)
            mixed = mixed_parts[0] if nck == 1 else jnp.concatenate(mixed_parts, axis=0)
            dp_ref[:, u0 + h * HD:u0 + (h + 1) * HD] = ((da * mixed) * _gelu_grad(u)).astype(BF16)
            dws_ref[h] += jnp.where(low, dws, 0.0)
            dbs_ref[h] += dbs
        dvn = dvn_ref[...]
        dlng_ref[...] += jnp.sum(dvn * vhat, axis=0, keepdims=True)
        dlnb_ref[...] += jnp.sum(dvn, axis=0, keepdims=True)
        dvh = dvn * lng_ref[...]
        dvg = rstd * (dvh - jnp.mean(dvh, axis=-1, keepdims=True)
                      - vhat * jnp.mean(dvh * vhat, axis=-1, keepdims=True))
        dp_ref[:, v0:v0 + DS] = (dvg * _gelu_grad(v)).astype(BF16)

        dc = jnp.concatenate([head_grad(DS + h * HD) for h in range(NCH)], axis=1)
        dc_nx = jnp.concatenate(
            [head_bwd(heads_nx_ref[:, h * HD:(h + 1) * HD], dhn_nx_ref[:, h * HD:(h + 1) * HD],
                      gh_ref[:, DS + h * HD:DS + (h + 1) * HD])[0] for h in range(NCH)], axis=1)
        bg = p_ref[:, b0:b0 + DC]
        cg = p_ref[:, c0:c0 + DC]
        xin = p_ref[:, x0:x0 + DC]
        dp_ref[:, b0:b0 + DC] = (dc * ycv_ref[...]).astype(BF16)
        dyv = dc * bg
        buf_ref[0:tr, :] = dyv
        buf_ref[tr:tr + HALO, :] = jnp.where(i < n - 1, dc_nx * b_nx_ref[...], 0.0)
        sh1 = buf_ref[1:1 + tr, :]
        sh0 = buf_ref[2:2 + tr, :]
        dxc = cw_ref[2:3, :] * dyv + cw_ref[1:2, :] * sh1 + cw_ref[0:1, :] * sh0
        xc = cg * xin
        dp_ref[:, c0:c0 + DC] = (dxc * xin).astype(BF16)
        dp_ref[:, x0:x0 + DC] = (dxc * cg).astype(BF16)
        dcw_ref[0:1, :] += jnp.sum(sh0 * xc, axis=0, keepdims=True)
        dcw_ref[1:2, :] += jnp.sum(sh1 * xc, axis=0, keepdims=True)
        dcw_ref[2:3, :] += jnp.sum(dyv * xc, axis=0, keepdims=True)

        for h in range(NMH):
            do = head_grad(DS + DC + h * HD).astype(BF16)
            qh = (p_ref[:, q0 + h * HD:q0 + (h + 1) * HD] * SCALE).astype(BF16)
            kh = kv_ref[:, h * HD:(h + 1) * HD].astype(BF16)
            vh = kv_ref[:, DM + h * HD:DM + (h + 1) * HD].astype(BF16)
            p = _softmax_rows(qh, kh)
            dpr = lax.dot_general(do, vh, (((1,), (1,)), ((), ())), preferred_element_type=F32)
            ds = (p * (dpr - jnp.sum(dpr * p, axis=-1, keepdims=True))).astype(BF16)
            dp_ref[:, q0 + h * HD:q0 + (h + 1) * HD] = (
                jnp.dot(ds, kh, preferred_element_type=F32) * SCALE).astype(BF16)
            dkv_ref[:, h * HD:(h + 1) * HD] += lax.dot_general(
                ds, qh, (((0,), (0,)), ((), ())), preferred_element_type=F32)
            dkv_ref[:, DM + h * HD:DM + (h + 1) * HD] += lax.dot_general(
                p.astype(BF16), do, (((0,), (0,)), ((), ())), preferred_element_type=F32)

    full = lambda shape: pl.BlockSpec(shape, lambda i: (0,) * len(shape))
    row = lambda c: pl.BlockSpec((tr, c), lambda i: (i, 0))
    nxt = lambda col: pl.BlockSpec((HALO, DC), lambda i: (jnp.minimum((i + 1) * hb, last_hb), col))
    return pl.pallas_call(
        body, name=name, grid=(n,),
        in_specs=[row(D), row(D), row(DIN), row(DC), nxt(DS // DC), nxt(DS // DC), nxt(b0 // DC),
                  full((NMEM, 2 * DM)), full((NSH, CHUNK, CHUNK)), full((CHUNK, NSH)),
                  full((1, DS)), full((1, DS)), full((3, DC)), full((1, D))],
        out_specs=[row(DIN), full((NMEM, 2 * DM)), full((NSH, CHUNK, CHUNK)), full((NSH, HALO, CHUNK)),
                   full((1, DS)), full((1, DS)), full((HALO, DC)), full((1, D))],
        out_shape=[jax.ShapeDtypeStruct((S, DIN), BF16), jax.ShapeDtypeStruct((NMEM, 2 * DM), F32),
                   jax.ShapeDtypeStruct((NSH, CHUNK, CHUNK), F32), jax.ShapeDtypeStruct((NSH, HALO, CHUNK), F32),
                   jax.ShapeDtypeStruct((1, DS), F32), jax.ShapeDtypeStruct((1, DS), F32),
                   jax.ShapeDtypeStruct((HALO, DC), F32), jax.ShapeDtypeStruct((1, D), F32)],
        scratch_shapes=[pltpu.VMEM((tr + HALO, DC), F32), pltpu.VMEM((tr, DS), F32)],
        compiler_params=_cp(("arbitrary",), VMEM_MB),
    )(dhn, heads, proj, ycv, dhn, heads, proj, kv, w_s, bs_t, ln_g, ln_b, conv_w, g_head)


def _place():
    x, y, c = lax.axis_index("x"), lax.axis_index("y"), lax.axis_index("c")
    chips = [(1 - x, y), (x, 1 - y), (1 - x, 1 - y)]
    return x, y, c, chips


ANY = pl.BlockSpec(memory_space=pl.ANY)


def _allgather_shards(bufs, *, name):
    nw = len(bufs)

    def body(*refs):
        outs = refs[nw:2 * nw]
        send1, recv1, send2, recv2 = refs[2 * nw:]
        x, y, c, chips = _place()
        s = 2 * x + y
        me, sibling = (x, y, c), (x, y, 1 - c)

        def half(w, which):
            hr = bufs[w].shape[1] // 2
            return pl.ds(which * hr, hr)

        def ici(w, j, slot, to):
            rows = outs[w].at[slot, half(w, c)]
            return pltpu.make_async_remote_copy(
                src_ref=rows, dst_ref=rows, send_sem=send1.at[3 * w + j],
                recv_sem=recv1.at[3 * w + j], device_id=to, device_id_type=MESH)

        def d2d(w, j, slot, which, to):
            rows = outs[w].at[slot, half(w, which)]
            return pltpu.make_async_remote_copy(
                src_ref=rows, dst_ref=rows, send_sem=send2.at[3 * w + j], recv_sem=recv2.at[3 * w + j],
                device_id=to, device_id_type=MESH)

        first = [ici(w, j, s, (cx, cy, c)) for w in range(nw) for j, (cx, cy) in enumerate(chips)]
        for cp in first:
            cp.start()
        passed = []
        for w in range(nw):
            for j, (cx, cy) in enumerate(chips):
                ici(w, j, 2 * cx + cy, me).wait_recv()
                cp = d2d(w, j, 2 * cx + cy, c, sibling)
                cp.start()
                passed.append(cp)
        for w in range(nw):
            for j, (cx, cy) in enumerate(chips):
                d2d(w, j, 2 * cx + cy, 1 - c, me).wait_recv()
        for cp in first + passed:
            cp.wait_send()

    return pl.pallas_call(
        body, name=name,
        in_specs=[ANY] * nw, out_specs=[ANY] * nw,
        out_shape=[jax.ShapeDtypeStruct(a.shape, a.dtype) for a in bufs],
        input_output_aliases={w: w for w in range(nw)},
        scratch_shapes=[pltpu.SemaphoreType.DMA((3 * nw,)), pltpu.SemaphoreType.DMA((3 * nw,)),
                        pltpu.SemaphoreType.DMA((3 * nw,)), pltpu.SemaphoreType.DMA((3 * nw,))],
    )(*bufs)


def _sibling_exchange(arrs, *, name):
    nw = len(arrs)

    def body(*refs):
        ins, outs = refs[:nw], refs[nw:2 * nw]
        send, recv = refs[2 * nw:]
        x, y, c, _ = _place()
        cps = [pltpu.make_async_remote_copy(src_ref=ins[w], dst_ref=outs[w], send_sem=send.at[w], recv_sem=recv.at[w],
                                            device_id=(x, y, 1 - c), device_id_type=MESH) for w in range(nw)]
        for cp in cps:
            cp.start()
        for cp in cps:
            cp.wait()

    return pl.pallas_call(
        body, name=name, in_specs=[ANY] * nw, out_specs=[ANY] * nw,
        out_shape=[jax.ShapeDtypeStruct(a.shape, a.dtype) for a in arrs],
        scratch_shapes=[pltpu.SemaphoreType.DMA((nw,)), pltpu.SemaphoreType.DMA((nw,))],
    )(*arrs)


def _chip_exchange(parts, bufs, *, name):
    nw = len(parts)

    def body(*refs):
        ins, outs = refs[:nw], refs[2 * nw:3 * nw]
        send, recv = refs[3 * nw:]
        x, y, c, chips = _place()
        s = 2 * x + y
        cps = [pltpu.make_async_remote_copy(
            src_ref=ins[w].at[2 * cx + cy], dst_ref=outs[w].at[s], send_sem=send.at[3 * w + j],
            recv_sem=recv.at[3 * w + j], device_id=(cx, cy, c), device_id_type=MESH)
            for j, (cx, cy) in enumerate(chips) for w in range(nw)]
        for cp in cps:
            cp.start()
        for j, (cx, cy) in enumerate(chips):
            for w in range(nw):
                slot = outs[w].at[2 * cx + cy]
                pltpu.make_async_remote_copy(
                    src_ref=slot, dst_ref=slot, send_sem=send.at[3 * w + j], recv_sem=recv.at[3 * w + j],
                    device_id=(x, y, c), device_id_type=MESH).wait_recv()
        for cp in cps:
            cp.wait_send()

    return pl.pallas_call(
        body, name=name, in_specs=[ANY] * (2 * nw), out_specs=[ANY] * nw,
        out_shape=[jax.ShapeDtypeStruct(a.shape, a.dtype) for a in bufs],
        input_output_aliases={nw + w: w for w in range(nw)},
        scratch_shapes=[pltpu.SemaphoreType.DMA((3 * nw,)), pltpu.SemaphoreType.DMA((3 * nw,))],
    )(*parts, *bufs)


def _allreduce_small(p, *, name):
    R = p.shape[0]
    hr = R // 2

    def body(p_ref, out_ref, sib_ref, sum_ref, gat_ref, tot_ref, send, recv):
        x, y, c, chips = _place()
        s = 2 * x + y
        sibling = (x, y, 1 - c)
        rows = pl.ds(pl.multiple_of(c * hr, 8), hr)
        swap = pltpu.make_async_remote_copy(src_ref=p_ref, dst_ref=sib_ref, send_sem=send.at[0], recv_sem=recv.at[0],
                                            device_id=sibling, device_id_type=MESH)
        swap.start()
        swap.wait()
        sum_ref[...] = p_ref[...] + sib_ref[...]
        gat_ref[s] = sum_ref[rows, :]
        cps = [pltpu.make_async_remote_copy(src_ref=sum_ref.at[rows], dst_ref=gat_ref.at[s], send_sem=send.at[1 + j],
                                            recv_sem=recv.at[1 + j], device_id=(cx, cy, c), device_id_type=MESH)
               for j, (cx, cy) in enumerate(chips)]
        for cp in cps:
            cp.start()
        for cp in cps:
            cp.wait()
        tot_ref[...] = ((gat_ref[0] + gat_ref[1]) + gat_ref[2]) + gat_ref[3]
        out_ref[rows, :] = tot_ref[...]
        share = pltpu.make_async_remote_copy(src_ref=tot_ref, dst_ref=out_ref.at[rows], send_sem=send.at[4],
                                             recv_sem=recv.at[4], device_id=sibling, device_id_type=MESH)
        share.start()
        share.wait_send()
        other = out_ref.at[pl.ds(pl.multiple_of((1 - c) * hr, 8), hr)]
        pltpu.make_async_remote_copy(src_ref=other, dst_ref=other, send_sem=send.at[4], recv_sem=recv.at[4],
                                     device_id=(x, y, c), device_id_type=MESH).wait_recv()

    vmem = pl.BlockSpec(memory_space=pltpu.VMEM)
    return pl.pallas_call(
        body, name=name, in_specs=[vmem], out_specs=vmem,
        out_shape=jax.ShapeDtypeStruct((R, 128), F32),
        scratch_shapes=[pltpu.VMEM((R, 128), F32), pltpu.VMEM((R, 128), F32), pltpu.VMEM((NCHIP, hr, 128), F32),
                        pltpu.VMEM((hr, 128), F32), pltpu.SemaphoreType.DMA((5,)), pltpu.SemaphoreType.DMA((5,))],
    )(p)


def _select_half_bf16(g, half, add, slot, *, name):
    _, R, C = g.shape
    hr = R // 2
    tr = _pick(hr, TR_EW, 16)
    nb = hr // tr
    sel = jnp.concatenate([jnp.reshape(half, (1,)).astype(jnp.int32), slot])
    has_add = add is not None

    def body(s_ref, g_ref, *rest):
        val = g_ref[...]
        if has_add:
            a_ref, o_ref, own_ref = rest
            val = (val + a_ref[...].astype(F32)).astype(BF16)
            o_ref[...] = val

            @pl.when(pl.program_id(1) == s_ref[1])
            def _():
                own_ref[...] = val
        else:
            rest[0][...] = val.astype(BF16)

    g_spec = pl.BlockSpec((None, tr, C), lambda i, j, s: (j, s[0] * nb + i, 0))
    o_spec = pl.BlockSpec((None, tr, C), lambda i, j, s: (j, i, 0))
    own_spec = pl.BlockSpec((None, tr, C), lambda i, j, s: (s[1], i, 0))
    shape = jax.ShapeDtypeStruct((NCHIP, hr, C), BF16)
    return pl.pallas_call(
        body, name=name,
        grid_spec=pltpu.PrefetchScalarGridSpec(
            num_scalar_prefetch=1, grid=(nb, NCHIP),
            in_specs=[g_spec] + ([o_spec] if has_add else []),
            out_specs=[o_spec, own_spec] if has_add else o_spec),
        out_shape=[shape, shape] if has_add else shape,
        compiler_params=_cp(("parallel", "arbitrary"), VMEM_MB),
    )(sel, g, *([add] if has_add else []))


def _sum_slots(r, *, name):
    _, R, C = r.shape
    tr = _pick(R, TR_EW, 16)

    def body(r_ref, o_ref):
        acc = r_ref[0].astype(F32) + r_ref[1].astype(F32)
        for j in range(2, NCHIP):
            acc = acc + r_ref[j].astype(F32)
        o_ref[...] = acc

    return pl.pallas_call(
        body, name=name, grid=(R // tr,),
        in_specs=[pl.BlockSpec((NCHIP, tr, C), lambda i: (0, i, 0))],
        out_specs=pl.BlockSpec((tr, C), lambda i: (i, 0)),
        out_shape=jax.ShapeDtypeStruct((R, C), F32),
        compiler_params=_cp(("parallel",), VMEM_MB),
    )(r)


def _adamw_math(w, g, m, v):
    m = ADAM_B1 * m + (1.0 - ADAM_B1) * g
    v = ADAM_B2 * v + (1.0 - ADAM_B2) * (g * g)
    m_hat = m / (1.0 - ADAM_B1 ** ADAM_STEP)
    v_hat = v / (1.0 - ADAM_B2 ** ADAM_STEP)
    delta = -ADAM_LR * (m_hat / (jnp.sqrt(v_hat) + ADAM_EPS) + ADAM_WD * w)
    return delta, m, v


def _adamw(w, g_mine, g_sib, m, v, core, *, name):
    R, C = w.shape
    hr = R // 2
    tr = _pick(hr, TR_EW, 8)
    nb = hr // tr
    row = pl.BlockSpec((tr, C), lambda hh, i, c: (hh * nb + i, 0))
    half = pl.BlockSpec((tr, C), lambda hh, i, c: (i, 0))

    def body(c_ref, w_ref, gm_ref, gs_ref, m_ref, v_ref, go_ref, d_ref, mo_ref, vo_ref):
        gv = jnp.where(pl.program_id(0) == c_ref[0], gm_ref[...], gs_ref[...])
        d, mn, vn = _adamw_math(w_ref[...], gv, m_ref[...], v_ref[...])
        go_ref[...] = gv
        d_ref[...] = d
        mo_ref[...] = mn
        vo_ref[...] = vn

    return pl.pallas_call(
        body, name=name,
        grid_spec=pltpu.PrefetchScalarGridSpec(
            num_scalar_prefetch=1, grid=(2, nb),
            in_specs=[row, half, half, row, row], out_specs=[row] * 4),
        out_shape=[jax.ShapeDtypeStruct((R, C), F32)] * 4,
        compiler_params=_cp(("parallel", "parallel"), VMEM_MB),
    )(core, w, g_mine, g_sib, m, v)


def _adamw_small(ws, gs, ms, vs, *, name):
    n = len(ws)

    def body(*refs):
        w_r, g_r, m_r, v_r = refs[:n], refs[n:2 * n], refs[2 * n:3 * n], refs[3 * n:4 * n]
        d_r, mo_r, vo_r = refs[4 * n:5 * n], refs[5 * n:6 * n], refs[6 * n:7 * n]
        for k in range(n):
            d, mn, vn = _adamw_math(w_r[k][...], g_r[k][...], m_r[k][...], v_r[k][...])
            d_r[k][...] = d
            mo_r[k][...] = mn
            vo_r[k][...] = vn

    shapes = [jax.ShapeDtypeStruct(w.shape, F32) for w in ws]
    res = pl.pallas_call(body, name=name, out_shape=shapes * 3)(*ws, *gs, *ms, *vs)
    return res[:n], res[n:2 * n], res[2 * n:]


_PACK_ROWS = 8


def _pack(parts):
    rows = []
    for a in parts:
        flat = a.reshape(-1)
        n = -(-flat.shape[0] // (_PACK_ROWS * 128)) * (_PACK_ROWS * 128)
        rows.append(jnp.pad(flat, (0, n - flat.shape[0])).reshape(-1, 128))
    total = sum(r.shape[0] for r in rows)
    if total % 16:
        rows.append(jnp.zeros((16 - total % 16, 128), F32))
    return jnp.concatenate(rows, axis=0)


def _unpack(p, shapes):
    out, r = [], 0
    for shp in shapes:
        n = math.prod(shp)
        nr = -(-n // (_PACK_ROWS * 128)) * _PACK_ROWS
        out.append(p[r:r + nr].reshape(-1)[:n].reshape(shp))
        r += nr
    return out


def kernel(x, mem, g_mix, w_in, ln_v_g, ln_v_b, w_s, b_s, conv_w, g_mem, w_kv, g_head, w_o, g_ffn, w_ffn1, w_ffn2, g_final, loss_target, m_g_mix, m_w_in, m_ln_v_g, m_ln_v_b, m_w_s, m_b_s, m_conv_w, m_g_mem, m_w_kv, m_g_head, m_w_o, m_g_ffn, m_w_ffn1, m_w_ffn2, m_g_final, v_g_mix, v_w_in, v_ln_v_g, v_ln_v_b, v_w_s, v_b_s, v_conv_w, v_g_mem, v_w_kv, v_g_head, v_w_o, v_g_ffn, v_w_ffn1, v_w_ffn2, v_g_final):
    sds = jax.ShapeDtypeStruct
    xi, yi = lax.axis_index("x"), lax.axis_index("y")
    shard = 2 * xi + yi
    x2d, mem2d, tgt = x[0], mem[0], loss_target[0]
    ws3, bs2 = w_s[0], b_s[0]
    g_final2 = g_final.reshape(1, D)
    dff4 = DFF // NCHIP
    din4 = DIN // NCHIP
    dcv4 = DC // NCHIP

    big = [w_in[0], w_kv[0], w_o[0], w_ffn1[0], w_ffn2[0]]
    big_names = ["w_in", "w_kv", "w_o", "w_ffn1", "w_ffn2"]
    slot = jnp.reshape(shard, (1,)).astype(jnp.int32)
    core = jnp.reshape(lax.axis_index("c"), (1,)).astype(jnp.int32)
    shards_b = [_cast_into_slot(w, slot, name="cast_" + nm) for w, nm in zip(big, big_names)]
    conv_pad = jnp.pad(conv_w[0], ((0, 16 - 3), (0, 256 - dcv4)))
    conv_slots = lax.dynamic_update_slice(jnp.zeros((NCHIP, 16, 256), F32), conv_pad[None], (shard, 0, 0))
    win4, wkv4, wo4, w14, w24, conv4 = _allgather_shards(shards_b + [conv_slots], name="allgather_weights")
    w_in_full = win4.transpose(1, 0, 2).reshape(D, DIN)
    w_kv_full = wkv4.reshape(D, 2 * DM)
    w_o_full = wo4.reshape(D, D)
    w2_full = w24.reshape(DFF, D)
    conv_full = conv4[:, :3, :dcv4].transpose(1, 0, 2).reshape(3, DC)
    bs_t = bs2.T

    h = _rms_fwd(x2d, g_mix, name="rms_mix")
    (proj,) = _matmul(h, w_in_full, name="mm_proj", M=S, N=DIN, K=D, tn=DIN // 2, outs=[sds((S, DIN), F32)])
    mem_n = _rms_fwd(mem2d, g_mem, name="rms_mem")
    (kv,) = _matmul(mem_n, w_kv_full, name="mm_kv", M=NMEM, N=2 * DM, K=D, outs=[sds((NMEM, 2 * DM), F32)])
    heads, hn, ycv = _mix_fwd(proj, kv, ws3, bs_t, ln_v_g, ln_v_b, conv_full, g_head, name="mix_fwd")
    (x2,) = _matmul(hn, w_o_full, name="mm_wo", M=S, N=D, K=D, outs=[sds((S, D), F32)],
                    epi=lambda acc, res: (acc + res,), extras=[(x2d, _tile_spec())])
    h2 = _rms_fwd(x2, g_ffn, name="rms_ffn")

    def w1_cols(tn, tk):
        nb = dff4 // tn
        return pl.BlockSpec((None, tk, tn), lambda j, i, k: (j // nb, k, j % nb))

    f, act = _matmul(h2, w14, name="mm_ffn1", M=S, N=DFF, K=D, b_spec=w1_cols,
                     outs=[sds((S, DFF), F32), sds((S, DFF), BF16)],
                     epi=lambda acc: (acc, jnp.square(jnp.maximum(acc, 0.0))))
    (x3,) = _matmul(act, w2_full, name="mm_ffn2", M=S, N=D, K=DFF, outs=[sds((S, D), F32)],
                    epi=lambda acc, res: (acc + res,), extras=[(x2, _tile_spec())])

    dx3, dx3b, dg_final, loss11 = _loss_bwd(x3, g_final2, tgt, name="loss_bwd")
    (dfb,) = _matmul(dx3b, w2_full, name="mm_dact", tb=True, M=S, N=DFF, K=D, outs=[sds((S, DFF), BF16)],
                     epi=lambda acc, fv: (acc * (2.0 * jnp.maximum(fv, 0.0)),), extras=[(f, _tile_spec())])
    (dw2,) = _matmul(act, dx3b, name="mm_dw2", ta=True, M=DFF, N=D, K=S, outs=[sds((DFF, D), F32)])

    def dw1_out(tm, tn):
        nb = dff4 // tn
        return [pl.BlockSpec((None, tm, tn), lambda j, i, k: (j // nb, i, j % nb))]

    (dw1,) = _matmul(h2, dfb, name="mm_dw1", ta=True, M=D, N=DFF, K=S, outs=[sds((NCHIP, D, dff4), F32)],
                     out_specs=dw1_out)

    def w1_rows(tn, tk):
        kb = dff4 // tk
        return pl.BlockSpec((None, tn, tk), lambda j, i, k: (k // kb, j, k % kb))

    (dh2,) = _matmul(dfb, w14, name="mm_dh2", tb=True, M=S, N=D, K=DFF, b_spec=w1_rows, outs=[sds((S, D), F32)])
    dx2, dx2b, dg_ffn = _rms_bwd(dh2, x2, g_ffn, dx3, name="rms_ffn_bwd")
    (dhn,) = _matmul(dx2b, w_o_full, name="mm_dhn", tb=True, M=S, N=D, K=D, outs=[sds((S, D), F32)])
    (dwo,) = _matmul(hn, dx2b, name="mm_dwo", ta=True, M=D, N=D, K=S, outs=[sds((D, D), F32)])
    dproj, dkv, dws, dbs8, dlng, dlnb, dcw8, dgh = _mix_bwd(
        dhn, heads, proj, ycv, kv, ws3, bs_t, ln_v_g, ln_v_b, conv_full, g_head, name="mix_bwd")
    (dwin,) = _matmul(h, dproj, name="mm_dwin", ta=True, M=D, N=DIN, K=S, tn=DIN // 2, outs=[sds((D, DIN), F32)])
    (dh,) = _matmul(dproj, w_in_full, name="mm_dh", tb=True, M=S, N=D, K=DIN, tk=DIN, outs=[sds((S, D), F32)])
    dx, dg_mix = _rms_bwd(dh, x2d, g_mix, dx2, name="rms_mix_bwd", want_bf=False)
    (dwkv,) = _matmul(mem_n, dkv, name="mm_dwkv", ta=True, M=D, N=2 * DM, K=NMEM, outs=[sds((D, 2 * DM), F32)])
    (dmem_n,) = _matmul(dkv, w_kv_full, name="mm_dmem", tb=True, M=NMEM, N=D, K=2 * DM, outs=[sds((NMEM, D), F32)])
    (dg_mem,) = _rms_bwd(dmem_n, mem2d, g_mem, None, name="rms_mem_bwd", want_dx=False)

    loss = lax.psum(loss11[0, 0], ("x", "y", "c"))

    grads4 = [dwin.reshape(D, NCHIP, din4).transpose(1, 0, 2), dwkv.reshape(NCHIP, D // NCHIP, 2 * DM),
              dwo.reshape(NCHIP, D // NCHIP, D), dw1, dw2.reshape(NCHIP, dff4, D)]
    ci = lax.axis_index("c")
    to_sib = [_select_half_bf16(g, 1 - ci, None, slot, name="rs_send_" + nm) for g, nm in zip(grads4, big_names)]
    from_sib = _sibling_exchange(to_sib, name="rs_sibling")
    added = [_select_half_bf16(g, ci, r, slot, name="rs_add_" + nm) for g, r, nm in zip(grads4, from_sib, big_names)]
    gathered = _chip_exchange([a[0] for a in added], [a[1] for a in added], name="rs_chips")
    halves = [_sum_slots(r, name="rs_sum_" + nm) for r, nm in zip(gathered, big_names)]
    sib_halves = _sibling_exchange(halves, name="rs_share")

    big_m = [m_w_in[0], m_w_kv[0], m_w_o[0], m_w_ffn1[0], m_w_ffn2[0]]
    big_v = [v_w_in[0], v_w_kv[0], v_w_o[0], v_w_ffn1[0], v_w_ffn2[0]]
    big_out = {nm: _adamw(w, g, gs, m, v, core, name="adamw_" + nm)
               for nm, w, g, gs, m, v in zip(big_names, big, halves, sib_halves, big_m, big_v)}

    small_names = ["g_mix", "ln_v_g", "ln_v_b", "w_s", "b_s", "conv_w", "g_mem", "g_head", "g_ffn", "g_final"]
    small_part = [dg_mix, dlng, dlnb, dws, dbs8[:, 0, :], dcw8[:3], dg_mem, dgh, dg_ffn, dg_final]
    small_shapes = [(1, D), (1, DS), (1, DS), (NSH, CHUNK, CHUNK), (NSH, CHUNK), (3, DC), (1, D), (1, D), (1, D), (1, D)]
    total = _allreduce_small(_pack(small_part), name="allreduce_small")
    small_g = _unpack(total, small_shapes)
    small_g[5] = lax.dynamic_slice(small_g[5], (0, shard * dcv4), (3, dcv4))
    small_w = [g_mix, ln_v_g, ln_v_b, ws3, bs2, conv_w[0], g_mem, g_head, g_ffn, g_final2]
    small_m = [m_g_mix, m_ln_v_g, m_ln_v_b, m_w_s[0], m_b_s[0], m_conv_w[0], m_g_mem, m_g_head, m_g_ffn,
               m_g_final.reshape(1, D)]
    small_v = [v_g_mix, v_ln_v_g, v_ln_v_b, v_w_s[0], v_b_s[0], v_conv_w[0], v_g_mem, v_g_head, v_g_ffn,
               v_g_final.reshape(1, D)]
    s_delta, s_m, s_v = _adamw_small(small_w, small_g, small_m, small_v, name="adamw_small")
    small_out = {nm: (g, d, mn, vn) for nm, g, d, mn, vn in zip(small_names, small_g, s_delta, s_m, s_v)}

    order = ["g_mix", "w_in", "ln_v_g", "ln_v_b", "w_s", "b_s", "conv_w", "g_mem", "w_kv", "g_head", "w_o",
             "g_ffn", "w_ffn1", "w_ffn2", "g_final"]
    like = dict(g_mix=g_mix, w_in=w_in, ln_v_g=ln_v_g, ln_v_b=ln_v_b, w_s=w_s, b_s=b_s, conv_w=conv_w, g_mem=g_mem,
                w_kv=w_kv, g_head=g_head, w_o=w_o, g_ffn=g_ffn, w_ffn1=w_ffn1, w_ffn2=w_ffn2, g_final=g_final)
    res = {**big_out, **small_out}
    outs = [loss, dx[None]]
    for k in range(4):
        outs += [res[nm][k].reshape(like[nm].shape) for nm in order]
    return tuple(outs)
```

```python
import functools
import math

import jax
import jax.numpy as jnp
from jax import lax
from jax.experimental import pallas as pl
from jax.experimental.pallas import tpu as pltpu

F32 = jnp.float32
BF16 = jnp.bfloat16
MESH = pl.DeviceIdType.MESH

D = 2048
S = 2048
HD = 128
NH = D // HD
NMH = 4
NSH = (NH - NMH) // 2
NCH = NH - NMH - NSH
DS = NSH * HD
DC = NCH * HD
DM = NMH * HD
DIN = 2 * DS + 3 * DC + DM
CHUNK = 128
NMEM = 256
DFF = 4 * D
EPS = 1e-6
NCHIP = 4
SCALE = HD ** -0.5

ADAM_LR = 0.001
ADAM_B1 = 0.9
ADAM_B2 = 0.999
ADAM_EPS = 1e-08
ADAM_WD = 0.01
ADAM_STEP = 10

TR_EW = 256
TR_MIX = 256
TM = 512
TN = 1024
TK = 2048
VMEM_MB = 56
HALO = 8


def _pick(n, target, q=128):
    best = None
    for t in range(q, min(n, target) + 1, q):
        if n % t == 0:
            best = t
    return n if best is None else best


def _cp(sem=None, vmem_mb=None, **kw):
    d = dict(kw)
    if sem is not None:
        d["dimension_semantics"] = sem
    if vmem_mb is not None:
        d["vmem_limit_bytes"] = vmem_mb << 20
    return pltpu.CompilerParams(**d)


def _gelu(x):
    z = 0.7978845608028654 * (x + 0.044715 * (x * x * x))
    return 0.5 * x * (1.0 + jnp.tanh(z))


def _gelu_grad(x):
    x2 = x * x
    t = jnp.tanh(0.7978845608028654 * (x + 0.044715 * (x2 * x)))
    return 0.5 * (1.0 + t) + 0.5 * x * (1.0 - t * t) * (0.7978845608028654 * (1.0 + 3.0 * 0.044715 * x2))


def _matmul(a, b, *, name, ta=False, tb=False, M, N, K, tm=None, tn=None, tk=None, outs, epi=None,
            extras=(), b_spec=None, out_specs=None):
    tm = _pick(M, TM if tm is None else tm, 8)
    tn = _pick(N, TN if tn is None else tn)
    tk = _pick(K, TK if tk is None else tk)
    nk = K // tk
    grid = (N // tn, M // tm, nk)
    a_spec = (pl.BlockSpec((tk, tm), lambda j, i, k: (k, i)) if ta
              else pl.BlockSpec((tm, tk), lambda j, i, k: (i, k)))
    if b_spec is None:
        b_spec = (pl.BlockSpec((tn, tk), lambda j, i, k: (j, k)) if tb
                  else pl.BlockSpec((tk, tn), lambda j, i, k: (k, j)))
    else:
        b_spec = b_spec(tn, tk)
    if out_specs is None:
        out_specs = [pl.BlockSpec((tm, tn), lambda j, i, k: (i, j)) for _ in outs]
    else:
        out_specs = out_specs(tm, tn)
    dn = (((0 if ta else 1,), (1 if tb else 0,)), ((), ()))
    n_ex, n_out = len(extras), len(outs)

    def body(*refs):
        a_ref, b_ref = refs[0], refs[1]
        ex = refs[2:2 + n_ex]
        o = refs[2 + n_ex:2 + n_ex + n_out]
        acc = refs[2 + n_ex + n_out:]
        part = lax.dot_general(a_ref[...].astype(BF16), b_ref[...].astype(BF16), dn,
                               preferred_element_type=F32)

        def finish(val):
            res = (val,) if epi is None else epi(val, *[e[...] for e in ex])
            for r, o_ref in zip(res, o):
                o_ref[...] = r.astype(o_ref.dtype)

        if nk == 1:
            finish(part)
        else:
            k = pl.program_id(2)

            @pl.when(k == 0)
            def _():
                acc[0][...] = part

            @pl.when(k > 0)
            def _():
                acc[0][...] += part

            @pl.when(k == nk - 1)
            def _():
                finish(acc[0][...])

    return pl.pallas_call(
        body, name=name, grid=grid,
        in_specs=[a_spec, b_spec] + [sp(tm, tn) for _, sp in extras],
        out_specs=out_specs, out_shape=outs,
        scratch_shapes=([pltpu.VMEM((tm, tn), F32)] if nk > 1 else []),
        compiler_params=_cp(("parallel", "parallel", "arbitrary"), VMEM_MB),
    )(a, b, *[arr for arr, _ in extras])


def _tile_spec():
    return lambda tm, tn: pl.BlockSpec((tm, tn), lambda j, i, k: (i, j))


def _cast_into_slot(w, slot, *, name):
    R, C = w.shape
    tr = _pick(R, TR_EW, 16)

    def body(s_ref, w_ref, o_ref):
        o_ref[...] = w_ref[...].astype(BF16)

    return pl.pallas_call(
        body, name=name,
        grid_spec=pltpu.PrefetchScalarGridSpec(
            num_scalar_prefetch=1, grid=(R // tr,),
            in_specs=[pl.BlockSpec((tr, C), lambda i, s: (i, 0))],
            out_specs=pl.BlockSpec((None, tr, C), lambda i, s: (s[0], i, 0))),
        out_shape=jax.ShapeDtypeStruct((NCHIP, R, C), BF16),
        compiler_params=_cp(("parallel",), VMEM_MB),
    )(slot, w)


def _rms_fwd(x, g, *, name):
    R, C = x.shape
    tr = _pick(R, TR_EW, 16)

    def body(x_ref, g_ref, o_ref):
        xv = x_ref[...]
        r = lax.rsqrt(jnp.mean(xv * xv, axis=-1, keepdims=True) + EPS)
        o_ref[...] = ((xv * r) * g_ref[...]).astype(BF16)

    return pl.pallas_call(
        body, name=name, grid=(R // tr,),
        in_specs=[pl.BlockSpec((tr, C), lambda i: (i, 0)), pl.BlockSpec((1, C), lambda i: (0, 0))],
        out_specs=pl.BlockSpec((tr, C), lambda i: (i, 0)),
        out_shape=jax.ShapeDtypeStruct((R, C), BF16),
        compiler_params=_cp(("parallel",), VMEM_MB),
    )(x, g)


def _rms_bwd(dh, x, g, dres, *, name, want_dx=True, want_bf=True):
    R, C = x.shape
    tr = _pick(R, TR_EW, 16)
    has_res = dres is not None
    row = pl.BlockSpec((tr, C), lambda i: (i, 0))
    vec = pl.BlockSpec((1, C), lambda i: (0, 0))

    def body(*refs):
        dh_ref, x_ref, g_ref = refs[:3]
        pos = 3
        dres_ref = None
        if has_res:
            dres_ref = refs[pos]
            pos += 1
        outs = refs[pos:]
        i = pl.program_id(0)
        xv = x_ref[...]
        r = lax.rsqrt(jnp.mean(xv * xv, axis=-1, keepdims=True) + EPS)
        xh = xv * r
        dhv = dh_ref[...]
        dg_ref = outs[-1]
        dgp = jnp.sum(dhv * xh, axis=0, keepdims=True)

        @pl.when(i == 0)
        def _():
            dg_ref[...] = dgp

        @pl.when(i > 0)
        def _():
            dg_ref[...] += dgp

        if want_dx:
            t = dhv * g_ref[...]
            dx = r * (t - xh * jnp.mean(t * xh, axis=-1, keepdims=True))
            if has_res:
                dx = dx + dres_ref[...]
            outs[0][...] = dx
            if want_bf:
                outs[1][...] = dx.astype(BF16)

    in_specs = [row, row, vec] + ([row] if has_res else [])
    out_specs, out_shape = [], []
    if want_dx:
        out_specs.append(row)
        out_shape.append(jax.ShapeDtypeStruct((R, C), F32))
        if want_bf:
            out_specs.append(row)
            out_shape.append(jax.ShapeDtypeStruct((R, C), BF16))
    out_specs.append(vec)
    out_shape.append(jax.ShapeDtypeStruct((1, C), F32))
    args = [dh, x, g] + ([dres] if has_res else [])
    return pl.pallas_call(
        body, name=name, grid=(R // tr,), in_specs=in_specs, out_specs=out_specs, out_shape=out_shape,
        compiler_params=_cp(("arbitrary",), VMEM_MB),
    )(*args)


def _loss_bwd(x3, g, tgt, *, name):
    R, C = x3.shape
    tr = _pick(R, TR_EW, 16)
    n = R // tr
    row = pl.BlockSpec((tr, C), lambda i: (i, 0))
    vec = pl.BlockSpec((1, C), lambda i: (0, 0))

    def body(x_ref, g_ref, t_ref, dx_ref, dxb_ref, dg_ref, loss_ref, acc_ref):
        i = pl.program_id(0)
        xv = x_ref[...]
        gv = g_ref[...]
        r = lax.rsqrt(jnp.mean(xv * xv, axis=-1, keepdims=True) + EPS)
        xh = xv * r
        e = xh * gv - t_ref[...]
        dy = e * (1.0 / C)
        sq = jnp.sum(e * e, axis=0, keepdims=True)
        dgp = jnp.sum(dy * xh, axis=0, keepdims=True)

        @pl.when(i == 0)
        def _():
            acc_ref[...] = sq
            dg_ref[...] = dgp

        @pl.when(i > 0)
        def _():
            acc_ref[...] += sq
            dg_ref[...] += dgp

        t = dy * gv
        dx = r * (t - xh * jnp.mean(t * xh, axis=-1, keepdims=True))
        dx_ref[...] = dx
        dxb_ref[...] = dx.astype(BF16)

        @pl.when(i == n - 1)
        def _():
            loss_ref[...] = jnp.sum(acc_ref[...], axis=-1, keepdims=True) * (0.5 / C)

    return pl.pallas_call(
        body, name=name, grid=(n,),
        in_specs=[row, vec, row],
        out_specs=[row, row, vec, pl.BlockSpec((1, 1), lambda i: (0, 0))],
        out_shape=[jax.ShapeDtypeStruct((R, C), F32), jax.ShapeDtypeStruct((R, C), BF16),
                   jax.ShapeDtypeStruct((1, C), F32), jax.ShapeDtypeStruct((1, 1), F32)],
        scratch_shapes=[pltpu.VMEM((1, C), F32)],
        compiler_params=_cp(("arbitrary",), VMEM_MB),
    )(x3, g, tgt)


def _offsets():
    u0 = 0
    v0 = DS
    b0 = 2 * DS
    c0 = b0 + DC
    x0 = c0 + DC
    q0 = x0 + DC
    return u0, v0, b0, c0, x0, q0


def _tri_mask(lower):
    r = lax.broadcasted_iota(jnp.int32, (CHUNK, CHUNK), 0)
    c = lax.broadcasted_iota(jnp.int32, (CHUNK, CHUNK), 1)
    return (r >= c) if lower else (c >= r)


def _layer_norm_stats(vg):
    mu = jnp.mean(vg, axis=-1, keepdims=True)
    vc = vg - mu
    rstd = lax.rsqrt(jnp.mean(vc * vc, axis=-1, keepdims=True) + EPS)
    return vc * rstd, rstd


def _softmax_rows(qh, kh):
    s = lax.dot_general(qh, kh, (((1,), (1,)), ((), ())), preferred_element_type=F32)
    m = jnp.max(s, axis=-1, keepdims=True)
    e = jnp.exp(s - m)
    return e / jnp.sum(e, axis=-1, keepdims=True)


def _mix_fwd(proj, kv, w_s, bs_t, ln_g, ln_b, conv_w, g_head, *, name):
    assert DS == DC
    tr = _pick(S, TR_MIX, CHUNK)
    n = S // tr
    nck = tr // CHUNK
    u0, v0, b0, c0, x0, q0 = _offsets()
    hb = tr // HALO

    def body(p_ref, cprev_ref, xprev_ref, kv_ref, ws_ref, bst_ref, lng_ref, lnb_ref, cw_ref, gh_ref,
             heads_ref, hn_ref, ycv_ref, buf_ref):
        i = pl.program_id(0)

        def emit(col, val):
            rs = lax.rsqrt(jnp.mean(val * val, axis=-1, keepdims=True) + EPS)
            heads_ref[:, col:col + HD] = val
            hn_ref[:, col:col + HD] = ((val * rs) * gh_ref[:, col:col + HD]).astype(BF16)

        vhat, _ = _layer_norm_stats(_gelu(p_ref[:, v0:v0 + DS]))
        vnb = (vhat * lng_ref[...] + lnb_ref[...]).astype(BF16)
        low = _tri_mask(True)
        for h in range(NSH):
            wt = jnp.where(low, ws_ref[h], 0.0).astype(BF16)
            bcol = bst_ref[:, h:h + 1]
            parts = []
            for c in range(nck):
                blk = vnb[c * CHUNK:(c + 1) * CHUNK, h * HD:(h + 1) * HD]
                parts.append(jnp.dot(wt, blk, preferred_element_type=F32) + bcol)
            mixed = parts[0] if nck == 1 else jnp.concatenate(parts, axis=0)
            emit(h * HD, _gelu(p_ref[:, u0 + h * HD:u0 + (h + 1) * HD]) * mixed)

        xc = p_ref[:, c0:c0 + DC] * p_ref[:, x0:x0 + DC]
        prev = cprev_ref[...] * xprev_ref[...]
        buf_ref[0:HALO, :] = jnp.where(i > 0, prev, 0.0)
        buf_ref[HALO:HALO + tr, :] = xc
        y = (cw_ref[2:3, :] * xc + cw_ref[1:2, :] * buf_ref[HALO - 1:HALO - 1 + tr, :]
             + cw_ref[0:1, :] * buf_ref[HALO - 2:HALO - 2 + tr, :])
        ycv_ref[...] = y
        cout = p_ref[:, b0:b0 + DC] * y
        for h in range(NCH):
            emit(DS + h * HD, cout[:, h * HD:(h + 1) * HD])

        for h in range(NMH):
            qh = (p_ref[:, q0 + h * HD:q0 + (h + 1) * HD] * SCALE).astype(BF16)
            kh = kv_ref[:, h * HD:(h + 1) * HD].astype(BF16)
            vh = kv_ref[:, DM + h * HD:DM + (h + 1) * HD].astype(BF16)
            p = _softmax_rows(qh, kh)
            emit(DS + DC + h * HD, jnp.dot(p.astype(BF16), vh, preferred_element_type=F32))

    full = lambda shape: pl.BlockSpec(shape, lambda i: (0,) * len(shape))
    halo_c = pl.BlockSpec((HALO, DC), lambda i: (jnp.maximum(i * hb - 1, 0), c0 // DC))
    halo_x = pl.BlockSpec((HALO, DC), lambda i: (jnp.maximum(i * hb - 1, 0), x0 // DC))
    return pl.pallas_call(
        body, name=name, grid=(n,),
        in_specs=[pl.BlockSpec((tr, DIN), lambda i: (i, 0)), halo_c, halo_x,
                  full((NMEM, 2 * DM)), full((NSH, CHUNK, CHUNK)), full((CHUNK, NSH)),
                  full((1, DS)), full((1, DS)), full((3, DC)), full((1, D))],
        out_specs=[pl.BlockSpec((tr, D), lambda i: (i, 0)), pl.BlockSpec((tr, D), lambda i: (i, 0)),
                   pl.BlockSpec((tr, DC), lambda i: (i, 0))],
        out_shape=[jax.ShapeDtypeStruct((S, D), F32), jax.ShapeDtypeStruct((S, D), BF16),
                   jax.ShapeDtypeStruct((S, DC), F32)],
        scratch_shapes=[pltpu.VMEM((tr + HALO, DC), F32)],
        compiler_params=_cp(("parallel",), VMEM_MB),
    )(proj, proj, proj, kv, w_s, bs_t, ln_g, ln_b, conv_w, g_head)


def _mix_bwd(dhn, heads, proj, ycv, kv, w_s, bs_t, ln_g, ln_b, conv_w, g_head, *, name):
    assert DS == DC
    tr = _pick(S, TR_MIX, CHUNK)
    n = S // tr
    nck = tr // CHUNK
    u0, v0, b0, c0, x0, q0 = _offsets()
    hb = tr // HALO
    last_hb = S // HALO - 1

    def body(dhn_ref, heads_ref, p_ref, ycv_ref, dhn_nx_ref, heads_nx_ref, b_nx_ref, kv_ref, ws_ref, bst_ref,
             lng_ref, lnb_ref, cw_ref, gh_ref,
             dp_ref, dkv_ref, dws_ref, dbs_ref, dlng_ref, dlnb_ref, dcw_ref, dgh_ref, buf_ref, dvn_ref):
        i = pl.program_id(0)

        @pl.when(i == 0)
        def _():
            dkv_ref[...] = jnp.zeros_like(dkv_ref)
            dws_ref[...] = jnp.zeros_like(dws_ref)
            dbs_ref[...] = jnp.zeros_like(dbs_ref)
            dlng_ref[...] = jnp.zeros_like(dlng_ref)
            dlnb_ref[...] = jnp.zeros_like(dlnb_ref)
            dcw_ref[...] = jnp.zeros_like(dcw_ref)
            dgh_ref[...] = jnp.zeros_like(dgh_ref)

        def head_bwd(a, dn, gh):
            rs = lax.rsqrt(jnp.mean(a * a, axis=-1, keepdims=True) + EPS)
            ah = a * rs
            t = dn * gh
            return rs * (t - ah * jnp.mean(t * ah, axis=-1, keepdims=True)), jnp.sum(dn * ah, axis=0, keepdims=True)

        def head_grad(col):
            da, dg = head_bwd(heads_ref[:, col:col + HD], dhn_ref[:, col:col + HD], gh_ref[:, col:col + HD])
            dgh_ref[:, col:col + HD] += dg
            return da

        v = p_ref[:, v0:v0 + DS]
        vhat, rstd = _layer_norm_stats(_gelu(v))
        vnb = (vhat * lng_ref[...] + lnb_ref[...]).astype(BF16)
        low = _tri_mask(True)
        ones = jnp.ones((HALO, HD), BF16)
        for h in range(NSH):
            w_h = ws_ref[h]
            wt = jnp.where(low, w_h, 0.0).astype(BF16)
            bcol = bst_ref[:, h:h + 1]
            da = head_grad(h * HD)
            u = p_ref[:, u0 + h * HD:u0 + (h + 1) * HD]
            ug = _gelu(u)
            dws = jnp.zeros((CHUNK, CHUNK), F32)
            dbs = jnp.zeros((HALO, CHUNK), F32)
            mixed_parts = []
            for c in range(nck):
                rows = slice(c * CHUNK, (c + 1) * CHUNK)
                blk = vnb[rows, h * HD:(h + 1) * HD]
                mixed_parts.append(jnp.dot(wt, blk, preferred_element_type=F32) + bcol)
                dmb = (da[rows] * ug[rows]).astype(BF16)
                dws = dws + lax.dot_general(dmb, blk, (((1,), (1,)), ((), ())), preferred_element_type=F32)
                dbs = dbs + lax.dot_general(ones, dmb, (((1,), (1,)), ((), ())), preferred_element_type=F32)
                dvn_ref[c * CHUNK:(c + 1) * CHUNK, h * HD:(h + 1) * HD] = lax.dot_general(
                    wt, dmb, (((0,), (0,)), ((), ())), preferred_element_type=F32)
            mixed = mixed_parts[0] if nck == 1 else jnp.concatenate(mixed_parts, axis=0)
            dp_ref[:, u0 + h * HD:u0 + (h + 1) * HD] = ((da * mixed) * _gelu_grad(u)).astype(BF16)
            dws_ref[h] += jnp.where(low, dws, 0.0)
            dbs_ref[h] += dbs
        dvn = dvn_ref[...]
        dlng_ref[...] += jnp.sum(dvn * vhat, axis=0, keepdims=True)
        dlnb_ref[...] += jnp.sum(dvn, axis=0, keepdims=True)
        dvh = dvn * lng_ref[...]
        dvg = rstd * (dvh - jnp.mean(dvh, axis=-1, keepdims=True)
                      - vhat * jnp.mean(dvh * vhat, axis=-1, keepdims=True))
        dp_ref[:, v0:v0 + DS] = (dvg * _gelu_grad(v)).astype(BF16)

        dc = jnp.concatenate([head_grad(DS + h * HD) for h in range(NCH)], axis=1)
        dc_nx = jnp.concatenate(
            [head_bwd(heads_nx_ref[:, h * HD:(h + 1) * HD], dhn_nx_ref[:, h * HD:(h + 1) * HD],
                      gh_ref[:, DS + h * HD:DS + (h + 1) * HD])[0] for h in range(NCH)], axis=1)
        bg = p_ref[:, b0:b0 + DC]
        cg = p_ref[:, c0:c0 + DC]
        xin = p_ref[:, x0:x0 + DC]
        dp_ref[:, b0:b0 + DC] = (dc * ycv_ref[...]).astype(BF16)
        dyv = dc * bg
        buf_ref[0:tr, :] = dyv
        buf_ref[tr:tr + HALO, :] = jnp.where(i < n - 1, dc_nx * b_nx_ref[...], 0.0)
        sh1 = buf_ref[1:1 + tr, :]
        sh0 = buf_ref[2:2 + tr, :]
        dxc = cw_ref[2:3, :] * dyv + cw_ref[1:2, :] * sh1 + cw_ref[0:1, :] * sh0
        xc = cg * xin
        dp_ref[:, c0:c0 + DC] = (dxc * xin).astype(BF16)
        dp_ref[:, x0:x0 + DC] = (dxc * cg).astype(BF16)
        dcw_ref[0:1, :] += jnp.sum(sh0 * xc, axis=0, keepdims=True)
        dcw_ref[1:2, :] += jnp.sum(sh1 * xc, axis=0, keepdims=True)
        dcw_ref[2:3, :] += jnp.sum(dyv * xc, axis=0, keepdims=True)

        for h in range(NMH):
            do = head_grad(DS + DC + h * HD).astype(BF16)
            qh = (p_ref[:, q0 + h * HD:q0 + (h + 1) * HD] * SCALE).astype(BF16)
            kh = kv_ref[:, h * HD:(h + 1) * HD].astype(BF16)
            vh = kv_ref[:, DM + h * HD:DM + (h + 1) * HD].astype(BF16)
            p = _softmax_rows(qh, kh)
            dpr = lax.dot_general(do, vh, (((1,), (1,)), ((), ())), preferred_element_type=F32)
            ds = (p * (dpr - jnp.sum(dpr * p, axis=-1, keepdims=True))).astype(BF16)
            dp_ref[:, q0 + h * HD:q0 + (h + 1) * HD] = (
                jnp.dot(ds, kh, preferred_element_type=F32) * SCALE).astype(BF16)
            dkv_ref[:, h * HD:(h + 1) * HD] += lax.dot_general(
                ds, qh, (((0,), (0,)), ((), ())), preferred_element_type=F32)
            dkv_ref[:, DM + h * HD:DM + (h + 1) * HD] += lax.dot_general(
                p.astype(BF16), do, (((0,), (0,)), ((), ())), preferred_element_type=F32)

    full = lambda shape: pl.BlockSpec(shape, lambda i: (0,) * len(shape))
    row = lambda c: pl.BlockSpec((tr, c), lambda i: (i, 0))
    nxt = lambda col: pl.BlockSpec((HALO, DC), lambda i: (jnp.minimum((i + 1) * hb, last_hb), col))
    return pl.pallas_call(
        body, name=name, grid=(n,),
        in_specs=[row(D), row(D), row(DIN), row(DC), nxt(DS // DC), nxt(DS // DC), nxt(b0 // DC),
                  full((NMEM, 2 * DM)), full((NSH, CHUNK, CHUNK)), full((CHUNK, NSH)),
                  full((1, DS)), full((1, DS)), full((3, DC)), full((1, D))],
        out_specs=[row(DIN), full((NMEM, 2 * DM)), full((NSH, CHUNK, CHUNK)), full((NSH, HALO, CHUNK)),
                   full((1, DS)), full((1, DS)), full((HALO, DC)), full((1, D))],
        out_shape=[jax.ShapeDtypeStruct((S, DIN), BF16), jax.ShapeDtypeStruct((NMEM, 2 * DM), F32),
                   jax.ShapeDtypeStruct((NSH, CHUNK, CHUNK), F32), jax.ShapeDtypeStruct((NSH, HALO, CHUNK), F32),
                   jax.ShapeDtypeStruct((1, DS), F32), jax.ShapeDtypeStruct((1, DS), F32),
                   jax.ShapeDtypeStruct((HALO, DC), F32), jax.ShapeDtypeStruct((1, D), F32)],
        scratch_shapes=[pltpu.VMEM((tr + HALO, DC), F32), pltpu.VMEM((tr, DS), F32)],
        compiler_params=_cp(("arbitrary",), VMEM_MB),
    )(dhn, heads, proj, ycv, dhn, heads, proj, kv, w_s, bs_t, ln_g, ln_b, conv_w, g_head)


def _place():
    x, y, c = lax.axis_index("x"), lax.axis_index("y"), lax.axis_index("c")
    chips = [(1 - x, y), (x, 1 - y), (1 - x, 1 - y)]
    return x, y, c, chips


ANY = pl.BlockSpec(memory_space=pl.ANY)


HBM = pl.BlockSpec(memory_space=pltpu.HBM)
SEM = pl.BlockSpec(memory_space=pltpu.SEMAPHORE)
EFFECT = pltpu.SideEffectType.DATAFLOW_SIDE_EFFECTING
N_PEER_CHIPS = 3


def _in_hbm(a):
    return pltpu.with_memory_space_constraint(a, pltpu.HBM)


def _allgather_start(bufs, *, name):
    nw = len(bufs)

    def body(*refs):
        ins, send, recv = refs[:nw], refs[nw:2 * nw], refs[2 * nw:3 * nw]
        token = refs[4 * nw]
        x, y, c, chips = _place()
        s = 2 * x + y
        for w in range(nw):
            hr = bufs[w].shape[1] // 2
            rows = ins[w].at[s, pl.ds(c * hr, hr)]
            for cx, cy in chips:
                pltpu.make_async_remote_copy(src_ref=rows, dst_ref=rows, send_sem=send[w], recv_sem=recv[w],
                                             device_id=(cx, cy, c), device_id_type=MESH).start()
        token[...] = jnp.zeros_like(token)

    res = pl.pallas_call(
        body, name=name,
        in_specs=[HBM] * nw,
        out_specs=[SEM] * (2 * nw) + [HBM] * nw + [pl.BlockSpec(memory_space=pltpu.VMEM)],
        out_shape=[pltpu.SemaphoreType.DMA(())] * (2 * nw) + [pltpu.HBM(a.shape, a.dtype) for a in bufs]
        + [jax.ShapeDtypeStruct((8, 128), F32)],
        input_output_aliases={w: 2 * nw + w for w in range(nw)},
        compiler_params=pltpu.CompilerParams(has_side_effects=EFFECT),
    )(*[_in_hbm(a) for a in bufs])
    return res[:nw], res[nw:2 * nw], res[2 * nw:3 * nw], res[3 * nw]


def _scatter_start(parts, bufs, *, name):
    nw = len(parts)

    def body(*refs):
        src, dst = refs[:nw], refs[nw:2 * nw]
        send, recv = refs[2 * nw:3 * nw], refs[3 * nw:4 * nw]
        token = refs[6 * nw]
        x, y, c, chips = _place()
        s = 2 * x + y
        for w in range(nw):
            for cx, cy in chips:
                pltpu.make_async_remote_copy(src_ref=src[w].at[2 * cx + cy], dst_ref=dst[w].at[s], send_sem=send[w],
                                             recv_sem=recv[w], device_id=(cx, cy, c), device_id_type=MESH).start()
        token[...] = jnp.zeros_like(token)

    res = pl.pallas_call(
        body, name=name,
        in_specs=[HBM] * (2 * nw),
        out_specs=[SEM] * (2 * nw) + [HBM] * (2 * nw) + [pl.BlockSpec(memory_space=pltpu.VMEM)],
        out_shape=[pltpu.SemaphoreType.DMA(())] * (2 * nw) + [pltpu.HBM(a.shape, a.dtype) for a in parts + bufs]
        + [jax.ShapeDtypeStruct((8, 128), F32)],
        input_output_aliases={k: 2 * nw + k for k in range(2 * nw)},
        compiler_params=pltpu.CompilerParams(has_side_effects=EFFECT),
    )(*[_in_hbm(a) for a in parts + bufs])
    return res[:nw], res[nw:2 * nw], res[2 * nw:3 * nw], res[3 * nw:4 * nw], res[4 * nw]


def _transfer_wait(sends, recvs, thru, sizes, after, *, name):
    n = len(sends)
    flat = [a for group in thru for a in group]

    def body(*refs):
        bufs = refs[:len(flat)]
        send = refs[len(flat):len(flat) + n]
        recv = refs[len(flat) + n:len(flat) + 2 * n]
        x, y, c, _ = _place()
        pos = 0
        for k in range(n):
            rows, _cols = sizes[k]
            region = bufs[pos].at[pl.ds(0, N_PEER_CHIPS), pl.ds(0, rows)]
            pos += len(thru[k])
            cp = pltpu.make_async_remote_copy(src_ref=region, dst_ref=region, send_sem=send[k], recv_sem=recv[k],
                                              device_id=(x, y, 1 - c), device_id_type=MESH)
            cp.wait_send()
            cp.wait_recv()

    res = pl.pallas_call(
        body, name=name,
        in_specs=[HBM] * len(flat) + [SEM] * (2 * n) + [pl.BlockSpec(memory_space=pl.ANY)],
        out_specs=[HBM] * len(flat),
        out_shape=[pltpu.HBM(a.shape, a.dtype) for a in flat],
        input_output_aliases={k: k for k in range(len(flat))},
        compiler_params=pltpu.CompilerParams(has_side_effects=EFFECT),
    )(*flat, *sends, *recvs, after)
    out, pos = [], 0
    for group in thru:
        out.append(res[pos:pos + len(group)])
        pos += len(group)
    return out


def _forward_to_sibling(bufs, *, name):
    nw = len(bufs)

    def body(*refs):
        outs = refs[nw:2 * nw]
        send, recv = refs[2 * nw:]
        x, y, c, chips = _place()
        me, sibling = (x, y, c), (x, y, 1 - c)

        def d2d(w, j, which, to):
            cx, cy = chips[j]
            hr = bufs[w].shape[1] // 2
            rows = outs[w].at[2 * cx + cy, pl.ds(which * hr, hr)]
            return pltpu.make_async_remote_copy(
                src_ref=rows, dst_ref=rows, send_sem=send.at[N_PEER_CHIPS * w + j],
                recv_sem=recv.at[N_PEER_CHIPS * w + j], device_id=to, device_id_type=MESH)

        passed = [d2d(w, j, c, sibling) for w in range(nw) for j in range(N_PEER_CHIPS)]
        for cp in passed:
            cp.start()
        for w in range(nw):
            for j in range(N_PEER_CHIPS):
                d2d(w, j, 1 - c, me).wait_recv()
        for cp in passed:
            cp.wait_send()

    return pl.pallas_call(
        body, name=name,
        in_specs=[ANY] * nw, out_specs=[ANY] * nw,
        out_shape=[jax.ShapeDtypeStruct(a.shape, a.dtype) for a in bufs],
        input_output_aliases={w: w for w in range(nw)},
        scratch_shapes=[pltpu.SemaphoreType.DMA((N_PEER_CHIPS * nw,)), pltpu.SemaphoreType.DMA((N_PEER_CHIPS * nw,))],
    )(*bufs)


def _sibling_exchange(arrs, *, name):
    nw = len(arrs)

    def body(*refs):
        ins, outs = refs[:nw], refs[nw:2 * nw]
        send, recv = refs[2 * nw:]
        x, y, c, _ = _place()
        cps = [pltpu.make_async_remote_copy(src_ref=ins[w], dst_ref=outs[w], send_sem=send.at[w], recv_sem=recv.at[w],
                                            device_id=(x, y, 1 - c), device_id_type=MESH) for w in range(nw)]
        for cp in cps:
            cp.start()
        for cp in cps:
            cp.wait()

    return pl.pallas_call(
        body, name=name, in_specs=[ANY] * nw, out_specs=[ANY] * nw,
        out_shape=[jax.ShapeDtypeStruct(a.shape, a.dtype) for a in arrs],
        scratch_shapes=[pltpu.SemaphoreType.DMA((nw,)), pltpu.SemaphoreType.DMA((nw,))],
    )(*arrs)


def _allreduce_small(p, *, name):
    R = p.shape[0]
    hr = R // 2

    def body(p_ref, out_ref, sib_ref, sum_ref, gat_ref, tot_ref, send, recv):
        x, y, c, chips = _place()
        s = 2 * x + y
        sibling = (x, y, 1 - c)
        rows = pl.ds(pl.multiple_of(c * hr, 8), hr)
        swap = pltpu.make_async_remote_copy(src_ref=p_ref, dst_ref=sib_ref, send_sem=send.at[0], recv_sem=recv.at[0],
                                            device_id=sibling, device_id_type=MESH)
        swap.start()
        swap.wait()
        sum_ref[...] = p_ref[...] + sib_ref[...]
        gat_ref[s] = sum_ref[rows, :]
        cps = [pltpu.make_async_remote_copy(src_ref=sum_ref.at[rows], dst_ref=gat_ref.at[s], send_sem=send.at[1 + j],
                                            recv_sem=recv.at[1 + j], device_id=(cx, cy, c), device_id_type=MESH)
               for j, (cx, cy) in enumerate(chips)]
        for cp in cps:
            cp.start()
        for cp in cps:
            cp.wait()
        tot_ref[...] = ((gat_ref[0] + gat_ref[1]) + gat_ref[2]) + gat_ref[3]
        out_ref[rows, :] = tot_ref[...]
        share = pltpu.make_async_remote_copy(src_ref=tot_ref, dst_ref=out_ref.at[rows], send_sem=send.at[4],
                                             recv_sem=recv.at[4], device_id=sibling, device_id_type=MESH)
        share.start()
        share.wait_send()
        other = out_ref.at[pl.ds(pl.multiple_of((1 - c) * hr, 8), hr)]
        pltpu.make_async_remote_copy(src_ref=other, dst_ref=other, send_sem=send.at[4], recv_sem=recv.at[4],
                                     device_id=(x, y, c), device_id_type=MESH).wait_recv()

    vmem = pl.BlockSpec(memory_space=pltpu.VMEM)
    return pl.pallas_call(
        body, name=name, in_specs=[vmem], out_specs=vmem,
        out_shape=jax.ShapeDtypeStruct((R, 128), F32),
        scratch_shapes=[pltpu.VMEM((R, 128), F32), pltpu.VMEM((R, 128), F32), pltpu.VMEM((NCHIP, hr, 128), F32),
                        pltpu.VMEM((hr, 128), F32), pltpu.SemaphoreType.DMA((5,)), pltpu.SemaphoreType.DMA((5,))],
    )(p)


def _select_half_bf16(g, half, add, slot, *, name):
    _, R, C = g.shape
    hr = R // 2
    tr = _pick(hr, TR_EW, 16)
    nb = hr // tr
    sel = jnp.concatenate([jnp.reshape(half, (1,)).astype(jnp.int32), slot])
    has_add = add is not None

    def body(s_ref, g_ref, *rest):
        val = g_ref[...]
        if has_add:
            a_ref, o_ref, own_ref = rest
            val = (val + a_ref[...].astype(F32)).astype(BF16)
            o_ref[...] = val

            @pl.when(pl.program_id(1) == s_ref[1])
            def _():
                own_ref[...] = val
        else:
            rest[0][...] = val.astype(BF16)

    g_spec = pl.BlockSpec((None, tr, C), lambda i, j, s: (j, s[0] * nb + i, 0))
    o_spec = pl.BlockSpec((None, tr, C), lambda i, j, s: (j, i, 0))
    own_spec = pl.BlockSpec((None, tr, C), lambda i, j, s: (s[1], i, 0))
    shape = jax.ShapeDtypeStruct((NCHIP, hr, C), BF16)
    return pl.pallas_call(
        body, name=name,
        grid_spec=pltpu.PrefetchScalarGridSpec(
            num_scalar_prefetch=1, grid=(nb, NCHIP),
            in_specs=[g_spec] + ([o_spec] if has_add else []),
            out_specs=[o_spec, own_spec] if has_add else o_spec),
        out_shape=[shape, shape] if has_add else shape,
        compiler_params=_cp(("parallel", "arbitrary"), VMEM_MB),
    )(sel, g, *([add] if has_add else []))


def _sum_slots(r, *, name):
    _, R, C = r.shape
    tr = _pick(R, TR_EW, 16)

    def body(r_ref, o_ref):
        acc = r_ref[0].astype(F32) + r_ref[1].astype(F32)
        for j in range(2, NCHIP):
            acc = acc + r_ref[j].astype(F32)
        o_ref[...] = acc

    return pl.pallas_call(
        body, name=name, grid=(R // tr,),
        in_specs=[pl.BlockSpec((NCHIP, tr, C), lambda i: (0, i, 0))],
        out_specs=pl.BlockSpec((tr, C), lambda i: (i, 0)),
        out_shape=jax.ShapeDtypeStruct((R, C), F32),
        compiler_params=_cp(("parallel",), VMEM_MB),
    )(r)


def _adamw_math(w, g, m, v):
    m = ADAM_B1 * m + (1.0 - ADAM_B1) * g
    v = ADAM_B2 * v + (1.0 - ADAM_B2) * (g * g)
    m_hat = m / (1.0 - ADAM_B1 ** ADAM_STEP)
    v_hat = v / (1.0 - ADAM_B2 ** ADAM_STEP)
    delta = -ADAM_LR * (m_hat / (jnp.sqrt(v_hat) + ADAM_EPS) + ADAM_WD * w)
    return delta, m, v


def _adamw(w, g_mine, g_sib, m, v, core, *, name):
    R, C = w.shape
    hr = R // 2
    tr = _pick(hr, TR_EW, 8)
    nb = hr // tr
    row = pl.BlockSpec((tr, C), lambda hh, i, c: (hh * nb + i, 0))
    half = pl.BlockSpec((tr, C), lambda hh, i, c: (i, 0))

    def body(c_ref, w_ref, gm_ref, gs_ref, m_ref, v_ref, go_ref, d_ref, mo_ref, vo_ref):
        gv = jnp.where(pl.program_id(0) == c_ref[0], gm_ref[...], gs_ref[...])
        d, mn, vn = _adamw_math(w_ref[...], gv, m_ref[...], v_ref[...])
        go_ref[...] = gv
        d_ref[...] = d
        mo_ref[...] = mn
        vo_ref[...] = vn

    return pl.pallas_call(
        body, name=name,
        grid_spec=pltpu.PrefetchScalarGridSpec(
            num_scalar_prefetch=1, grid=(2, nb),
            in_specs=[row, half, half, row, row], out_specs=[row] * 4),
        out_shape=[jax.ShapeDtypeStruct((R, C), F32)] * 4,
        compiler_params=_cp(("parallel", "parallel"), VMEM_MB),
    )(core, w, g_mine, g_sib, m, v)


def _adamw_small(ws, gs, ms, vs, *, name):
    n = len(ws)

    def body(*refs):
        w_r, g_r, m_r, v_r = refs[:n], refs[n:2 * n], refs[2 * n:3 * n], refs[3 * n:4 * n]
        d_r, mo_r, vo_r = refs[4 * n:5 * n], refs[5 * n:6 * n], refs[6 * n:7 * n]
        for k in range(n):
            d, mn, vn = _adamw_math(w_r[k][...], g_r[k][...], m_r[k][...], v_r[k][...])
            d_r[k][...] = d
            mo_r[k][...] = mn
            vo_r[k][...] = vn

    shapes = [jax.ShapeDtypeStruct(w.shape, F32) for w in ws]
    res = pl.pallas_call(body, name=name, out_shape=shapes * 3)(*ws, *gs, *ms, *vs)
    return res[:n], res[n:2 * n], res[2 * n:]


_PACK_ROWS = 8


def _pack(parts):
    rows = []
    for a in parts:
        flat = a.reshape(-1)
        n = -(-flat.shape[0] // (_PACK_ROWS * 128)) * (_PACK_ROWS * 128)
        rows.append(jnp.pad(flat, (0, n - flat.shape[0])).reshape(-1, 128))
    total = sum(r.shape[0] for r in rows)
    if total % 16:
        rows.append(jnp.zeros((16 - total % 16, 128), F32))
    return jnp.concatenate(rows, axis=0)


def _unpack(p, shapes):
    out, r = [], 0
    for shp in shapes:
        n = math.prod(shp)
        nr = -(-n // (_PACK_ROWS * 128)) * _PACK_ROWS
        out.append(p[r:r + nr].reshape(-1)[:n].reshape(shp))
        r += nr
    return out


def kernel(x, mem, g_mix, w_in, ln_v_g, ln_v_b, w_s, b_s, conv_w, g_mem, w_kv, g_head, w_o, g_ffn, w_ffn1, w_ffn2, g_final, loss_target, m_g_mix, m_w_in, m_ln_v_g, m_ln_v_b, m_w_s, m_b_s, m_conv_w, m_g_mem, m_w_kv, m_g_head, m_w_o, m_g_ffn, m_w_ffn1, m_w_ffn2, m_g_final, v_g_mix, v_w_in, v_ln_v_g, v_ln_v_b, v_w_s, v_b_s, v_conv_w, v_g_mem, v_w_kv, v_g_head, v_w_o, v_g_ffn, v_w_ffn1, v_w_ffn2, v_g_final):
    sds = jax.ShapeDtypeStruct
    xi, yi = lax.axis_index("x"), lax.axis_index("y")
    shard = 2 * xi + yi
    x2d, mem2d, tgt = x[0], mem[0], loss_target[0]
    ws3, bs2 = w_s[0], b_s[0]
    g_final2 = g_final.reshape(1, D)
    dff4 = DFF // NCHIP
    din4 = DIN // NCHIP
    dcv4 = DC // NCHIP

    big = [w_in[0], w_kv[0], w_o[0], w_ffn1[0], w_ffn2[0]]
    big_names = ["w_in", "w_kv", "w_o", "w_ffn1", "w_ffn2"]
    slot = jnp.reshape(shard, (1,)).astype(jnp.int32)
    core = jnp.reshape(lax.axis_index("c"), (1,)).astype(jnp.int32)
    shards_b = [_cast_into_slot(w, slot, name="cast_" + nm) for w, nm in zip(big, big_names)]
    conv_pad = jnp.pad(conv_w[0], ((0, 16 - 3), (0, 256 - dcv4)))
    conv_slots = lax.dynamic_update_slice(jnp.zeros((NCHIP, 16, 256), F32), conv_pad[None], (shard, 0, 0))
    gather = shards_b + [conv_slots]
    ag_send, ag_recv, gather, ag_token = _allgather_start(gather, name="ag_start")

    def arrive(idx, after, nm):
        got = _transfer_wait([ag_send[k] for k in idx], [ag_recv[k] for k in idx], [[gather[k]] for k in idx],
                             [(gather[k].shape[1] // 2, gather[k].shape[2]) for k in idx], after, name="ag_wait_" + nm)
        return _forward_to_sibling([g[0] for g in got], name="ag_fwd_" + nm)

    bs_t = bs2.T

    win4, conv4 = arrive([0, 5], ag_token, "in")
    w_in_full = win4.transpose(1, 0, 2).reshape(D, DIN)
    conv_full = conv4[:, :3, :dcv4].transpose(1, 0, 2).reshape(3, DC)
    h = _rms_fwd(x2d, g_mix, name="rms_mix")
    (proj,) = _matmul(h, w_in_full, name="mm_proj", M=S, N=DIN, K=D, tn=DIN // 2, outs=[sds((S, DIN), F32)])
    wkv4, wo4 = arrive([1, 2], proj, "kvo")
    w_kv_full = wkv4.reshape(D, 2 * DM)
    w_o_full = wo4.reshape(D, D)
    mem_n = _rms_fwd(mem2d, g_mem, name="rms_mem")
    (kv,) = _matmul(mem_n, w_kv_full, name="mm_kv", M=NMEM, N=2 * DM, K=D, outs=[sds((NMEM, 2 * DM), F32)])
    heads, hn, ycv = _mix_fwd(proj, kv, ws3, bs_t, ln_v_g, ln_v_b, conv_full, g_head, name="mix_fwd")
    (x2,) = _matmul(hn, w_o_full, name="mm_wo", M=S, N=D, K=D, outs=[sds((S, D), F32)],
                    epi=lambda acc, res: (acc + res,), extras=[(x2d, _tile_spec())])
    h2 = _rms_fwd(x2, g_ffn, name="rms_ffn")
    (w14,) = arrive([3], h2, "ffn1")

    def w1_cols(tn, tk):
        nb = dff4 // tn
        return pl.BlockSpec((None, tk, tn), lambda j, i, k: (j // nb, k, j % nb))

    f, act = _matmul(h2, w14, name="mm_ffn1", M=S, N=DFF, K=D, b_spec=w1_cols,
                     outs=[sds((S, DFF), F32), sds((S, DFF), BF16)],
                     epi=lambda acc: (acc, jnp.square(jnp.maximum(acc, 0.0))))
    (w24,) = arrive([4], act, "ffn2")
    w2_full = w24.reshape(DFF, D)
    (x3,) = _matmul(act, w2_full, name="mm_ffn2", M=S, N=D, K=DFF, outs=[sds((S, D), F32)],
                    epi=lambda acc, res: (acc + res,), extras=[(x2, _tile_spec())])

    ci = lax.axis_index("c")

    def rs_begin(g4, nm):
        to_sib = _select_half_bf16(g4, 1 - ci, None, slot, name="rs_send_" + nm)
        (from_sib,) = _sibling_exchange([to_sib], name="rs_sibling_" + nm)
        part, buf = _select_half_bf16(g4, ci, from_sib, slot, name="rs_add_" + nm)
        return _scatter_start([part], [buf], name="rs_start_" + nm)[:4]

    def rs_end(state, after, nm):
        send, recv, parts, bufs = state
        ((buf, _),) = _transfer_wait(send, recv, [[bufs[0], parts[0]]], [bufs[0].shape[1:]], after, name="rs_wait_" + nm)
        return _sum_slots(buf, name="rs_sum_" + nm)

    dx3, dx3b, dg_final, loss11 = _loss_bwd(x3, g_final2, tgt, name="loss_bwd")
    (dfb,) = _matmul(dx3b, w2_full, name="mm_dact", tb=True, M=S, N=DFF, K=D, outs=[sds((S, DFF), BF16)],
                     epi=lambda acc, fv: (acc * (2.0 * jnp.maximum(fv, 0.0)),), extras=[(f, _tile_spec())])
    (dw2,) = _matmul(act, dx3b, name="mm_dw2", ta=True, M=DFF, N=D, K=S, outs=[sds((DFF, D), F32)])
    rs_w2 = rs_begin(dw2.reshape(NCHIP, dff4, D), "w_ffn2")

    def dw1_out(tm, tn):
        nb = dff4 // tn
        return [pl.BlockSpec((None, tm, tn), lambda j, i, k: (j // nb, i, j % nb))]

    (dw1,) = _matmul(h2, dfb, name="mm_dw1", ta=True, M=D, N=DFF, K=S, outs=[sds((NCHIP, D, dff4), F32)],
                     out_specs=dw1_out)
    rs_w1 = rs_begin(dw1, "w_ffn1")

    def w1_rows(tn, tk):
        kb = dff4 // tk
        return pl.BlockSpec((None, tn, tk), lambda j, i, k: (k // kb, j, k % kb))

    (dh2,) = _matmul(dfb, w14, name="mm_dh2", tb=True, M=S, N=D, K=DFF, b_spec=w1_rows, outs=[sds((S, D), F32)])
    dx2, dx2b, dg_ffn = _rms_bwd(dh2, x2, g_ffn, dx3, name="rms_ffn_bwd")
    (dhn,) = _matmul(dx2b, w_o_full, name="mm_dhn", tb=True, M=S, N=D, K=D, outs=[sds((S, D), F32)])
    (dwo,) = _matmul(hn, dx2b, name="mm_dwo", ta=True, M=D, N=D, K=S, outs=[sds((D, D), F32)])
    rs_wo = rs_begin(dwo.reshape(NCHIP, D // NCHIP, D), "w_o")
    dproj, dkv, dws, dbs8, dlng, dlnb, dcw8, dgh = _mix_bwd(
        dhn, heads, proj, ycv, kv, ws3, bs_t, ln_v_g, ln_v_b, conv_full, g_head, name="mix_bwd")
    (dwin,) = _matmul(h, dproj, name="mm_dwin", ta=True, M=D, N=DIN, K=S, tn=DIN // 2, outs=[sds((D, DIN), F32)])
    rs_win = rs_begin(dwin.reshape(D, NCHIP, din4).transpose(1, 0, 2), "w_in")
    (dwkv,) = _matmul(mem_n, dkv, name="mm_dwkv", ta=True, M=D, N=2 * DM, K=NMEM, outs=[sds((D, 2 * DM), F32)])
    rs_wkv = rs_begin(dwkv.reshape(NCHIP, D // NCHIP, 2 * DM), "w_kv")
    (dh,) = _matmul(dproj, w_in_full, name="mm_dh", tb=True, M=S, N=D, K=DIN, tk=DIN, outs=[sds((S, D), F32)])
    dx, dg_mix = _rms_bwd(dh, x2d, g_mix, dx2, name="rms_mix_bwd", want_bf=False)
    (dmem_n,) = _matmul(dkv, w_kv_full, name="mm_dmem", tb=True, M=NMEM, N=D, K=2 * DM, outs=[sds((NMEM, D), F32)])
    (dg_mem,) = _rms_bwd(dmem_n, mem2d, g_mem, None, name="rms_mem_bwd", want_dx=False)

    loss = lax.psum(loss11[0, 0], ("x", "y", "c"))

    half_w2 = rs_end(rs_w2, dg_mem, "w_ffn2")
    half_w1 = rs_end(rs_w1, half_w2, "w_ffn1")
    half_wo = rs_end(rs_wo, half_w1, "w_o")
    half_win = rs_end(rs_win, half_wo, "w_in")
    half_wkv = rs_end(rs_wkv, half_win, "w_kv")
    halves = [half_win, half_wkv, half_wo, half_w1, half_w2]
    sib_halves = _sibling_exchange(halves, name="rs_share")

    big_m = [m_w_in[0], m_w_kv[0], m_w_o[0], m_w_ffn1[0], m_w_ffn2[0]]
    big_v = [v_w_in[0], v_w_kv[0], v_w_o[0], v_w_ffn1[0], v_w_ffn2[0]]
    big_out = {nm: _adamw(w, g, gs, m, v, core, name="adamw_" + nm)
               for nm, w, g, gs, m, v in zip(big_names, big, halves, sib_halves, big_m, big_v)}

    small_names = ["g_mix", "ln_v_g", "ln_v_b", "w_s", "b_s", "conv_w", "g_mem", "g_head", "g_ffn", "g_final"]
    small_part = [dg_mix, dlng, dlnb, dws, dbs8[:, 0, :], dcw8[:3], dg_mem, dgh, dg_ffn, dg_final]
    small_shapes = [(1, D), (1, DS), (1, DS), (NSH, CHUNK, CHUNK), (NSH, CHUNK), (3, DC), (1, D), (1, D), (1, D), (1, D)]
    total = _allreduce_small(_pack(small_part), name="allreduce_small")
    small_g = _unpack(total, small_shapes)
    small_g[5] = lax.dynamic_slice(small_g[5], (0, shard * dcv4), (3, dcv4))
    small_w = [g_mix, ln_v_g, ln_v_b, ws3, bs2, conv_w[0], g_mem, g_head, g_ffn, g_final2]
    small_m = [m_g_mix, m_ln_v_g, m_ln_v_b, m_w_s[0], m_b_s[0], m_conv_w[0], m_g_mem, m_g_head, m_g_ffn,
               m_g_final.reshape(1, D)]
    small_v = [v_g_mix, v_ln_v_g, v_ln_v_b, v_w_s[0], v_b_s[0], v_conv_w[0], v_g_mem, v_g_head, v_g_ffn,
               v_g_final.reshape(1, D)]
    s_delta, s_m, s_v = _adamw_small(small_w, small_g, small_m, small_v, name="adamw_small")
    small_out = {nm: (g, d, mn, vn) for nm, g, d, mn, vn in zip(small_names, small_g, s_delta, s_m, s_v)}

    order = ["g_mix", "w_in", "ln_v_g", "ln_v_b", "w_s", "b_s", "conv_w", "g_mem", "w_kv", "g_head", "w_o",
             "g_ffn", "w_ffn1", "w_ffn2", "g_final"]
    like = dict(g_mix=g_mix, w_in=w_in, ln_v_g=ln_v_g, ln_v_b=ln_v_b, w_s=w_s, b_s=b_s, conv_w=conv_w, g_mem=g_mem,
                w_kv=w_kv, g_head=g_head, w_o=w_o, g_ffn=g_ffn, w_ffn1=w_ffn1, w_ffn2=w_ffn2, g_final=g_final)
    res = {**big_out, **small_out}
    outs = [loss, dx[None]]
    for k in range(4):
        outs += [res[nm][k].reshape(like[nm].shape) for nm in order]
    return tuple(outs)
```

```python
import functools
import math

import jax
import jax.numpy as jnp
from jax import lax
from jax.experimental import pallas as pl
from jax.experimental.pallas import tpu as pltpu

F32 = jnp.float32
BF16 = jnp.bfloat16
MESH = pl.DeviceIdType.MESH

D = 2048
S = 2048
HD = 128
NH = D // HD
NMH = 4
NSH = (NH - NMH) // 2
NCH = NH - NMH - NSH
DS = NSH * HD
DC = NCH * HD
DM = NMH * HD
DIN = 2 * DS + 3 * DC + DM
CHUNK = 128
NMEM = 256
DFF = 4 * D
EPS = 1e-6
NCHIP = 4
SCALE = HD ** -0.5

ADAM_LR = 0.001
ADAM_B1 = 0.9
ADAM_B2 = 0.999
ADAM_EPS = 1e-08
ADAM_WD = 0.01
ADAM_STEP = 10

TR_EW = 256
TR_MIX = 256
TM = 512
TN = 1024
TK = 2048
VMEM_MB = 56
HALO = 8


def _pick(n, target, q=128):
    best = None
    for t in range(q, min(n, target) + 1, q):
        if n % t == 0:
            best = t
    return n if best is None else best


def _cp(sem=None, vmem_mb=None, **kw):
    d = dict(kw)
    if sem is not None:
        d["dimension_semantics"] = sem
    if vmem_mb is not None:
        d["vmem_limit_bytes"] = vmem_mb << 20
    return pltpu.CompilerParams(**d)


def _gelu(x):
    z = 0.7978845608028654 * (x + 0.044715 * (x * x * x))
    return 0.5 * x * (1.0 + jnp.tanh(z))


def _gelu_grad(x):
    x2 = x * x
    t = jnp.tanh(0.7978845608028654 * (x + 0.044715 * (x2 * x)))
    return 0.5 * (1.0 + t) + 0.5 * x * (1.0 - t * t) * (0.7978845608028654 * (1.0 + 3.0 * 0.044715 * x2))


def _matmul(a, b, *, name, ta=False, tb=False, M, N, K, tm=None, tn=None, tk=None, outs, epi=None,
            extras=(), b_spec=None, out_specs=None, after=()):
    n_after = len(after)
    tm = _pick(M, TM if tm is None else tm, 8)
    tn = _pick(N, TN if tn is None else tn)
    tk = _pick(K, TK if tk is None else tk)
    nk = K // tk
    grid = (N // tn, M // tm, nk)
    a_spec = (pl.BlockSpec((tk, tm), lambda j, i, k: (k, i)) if ta
              else pl.BlockSpec((tm, tk), lambda j, i, k: (i, k)))
    if b_spec is None:
        b_spec = (pl.BlockSpec((tn, tk), lambda j, i, k: (j, k)) if tb
                  else pl.BlockSpec((tk, tn), lambda j, i, k: (k, j)))
    else:
        b_spec = b_spec(tn, tk)
    if out_specs is None:
        out_specs = [pl.BlockSpec((tm, tn), lambda j, i, k: (i, j)) for _ in outs]
    else:
        out_specs = out_specs(tm, tn)
    dn = (((0 if ta else 1,), (1 if tb else 0,)), ((), ()))
    n_ex, n_out = len(extras), len(outs)

    def body(*refs):
        a_ref, b_ref = refs[0], refs[1]
        ex = refs[2:2 + n_ex]
        first_out = 2 + n_ex + n_after
        o = refs[first_out:first_out + n_out]
        acc = refs[first_out + n_out:]
        part = lax.dot_general(a_ref[...].astype(BF16), b_ref[...].astype(BF16), dn,
                               preferred_element_type=F32)

        def finish(val):
            res = (val,) if epi is None else epi(val, *[e[...] for e in ex])
            for r, o_ref in zip(res, o):
                o_ref[...] = r.astype(o_ref.dtype)

        if nk == 1:
            finish(part)
        else:
            k = pl.program_id(2)

            @pl.when(k == 0)
            def _():
                acc[0][...] = part

            @pl.when(k > 0)
            def _():
                acc[0][...] += part

            @pl.when(k == nk - 1)
            def _():
                finish(acc[0][...])

    return pl.pallas_call(
        body, name=name, grid=grid,
        in_specs=[a_spec, b_spec] + [sp(tm, tn) for _, sp in extras] + [ANY] * n_after,
        out_specs=out_specs, out_shape=outs,
        scratch_shapes=([pltpu.VMEM((tm, tn), F32)] if nk > 1 else []),
        compiler_params=_cp(("parallel", "parallel", "arbitrary"), VMEM_MB),
    )(a, b, *[arr for arr, _ in extras], *after)


def _tile_spec():
    return lambda tm, tn: pl.BlockSpec((tm, tn), lambda j, i, k: (i, j))


def _cast_into_slot(w, slot, *, name):
    R, C = w.shape
    tr = _pick(R, TR_EW, 16)

    def body(s_ref, w_ref, o_ref):
        o_ref[...] = w_ref[...].astype(BF16)

    return pl.pallas_call(
        body, name=name,
        grid_spec=pltpu.PrefetchScalarGridSpec(
            num_scalar_prefetch=1, grid=(R // tr,),
            in_specs=[pl.BlockSpec((tr, C), lambda i, s: (i, 0))],
            out_specs=pl.BlockSpec((None, tr, C), lambda i, s: (s[0], i, 0))),
        out_shape=jax.ShapeDtypeStruct((NCHIP, R, C), BF16),
        compiler_params=_cp(("parallel",), VMEM_MB),
    )(slot, w)


def _rms_fwd(x, g, *, name, after=()):
    R, C = x.shape
    tr = _pick(R, TR_EW, 16)
    n_after = len(after)

    def body(x_ref, g_ref, *rest):
        o_ref = rest[n_after]
        xv = x_ref[...]
        r = lax.rsqrt(jnp.mean(xv * xv, axis=-1, keepdims=True) + EPS)
        o_ref[...] = ((xv * r) * g_ref[...]).astype(BF16)

    return pl.pallas_call(
        body, name=name, grid=(R // tr,),
        in_specs=[pl.BlockSpec((tr, C), lambda i: (i, 0)), pl.BlockSpec((1, C), lambda i: (0, 0))] + [ANY] * n_after,
        out_specs=pl.BlockSpec((tr, C), lambda i: (i, 0)),
        out_shape=jax.ShapeDtypeStruct((R, C), BF16),
        compiler_params=_cp(("parallel",), VMEM_MB),
    )(x, g, *after)


def _rms_bwd(dh, x, g, dres, *, name, want_dx=True, want_bf=True):
    R, C = x.shape
    tr = _pick(R, TR_EW, 16)
    has_res = dres is not None
    row = pl.BlockSpec((tr, C), lambda i: (i, 0))
    vec = pl.BlockSpec((1, C), lambda i: (0, 0))

    def body(*refs):
        dh_ref, x_ref, g_ref = refs[:3]
        pos = 3
        dres_ref = None
        if has_res:
            dres_ref = refs[pos]
            pos += 1
        outs = refs[pos:]
        i = pl.program_id(0)
        xv = x_ref[...]
        r = lax.rsqrt(jnp.mean(xv * xv, axis=-1, keepdims=True) + EPS)
        xh = xv * r
        dhv = dh_ref[...]
        dg_ref = outs[-1]
        dgp = jnp.sum(dhv * xh, axis=0, keepdims=True)

        @pl.when(i == 0)
        def _():
            dg_ref[...] = dgp

        @pl.when(i > 0)
        def _():
            dg_ref[...] += dgp

        if want_dx:
            t = dhv * g_ref[...]
            dx = r * (t - xh * jnp.mean(t * xh, axis=-1, keepdims=True))
            if has_res:
                dx = dx + dres_ref[...]
            outs[0][...] = dx
            if want_bf:
                outs[1][...] = dx.astype(BF16)

    in_specs = [row, row, vec] + ([row] if has_res else [])
    out_specs, out_shape = [], []
    if want_dx:
        out_specs.append(row)
        out_shape.append(jax.ShapeDtypeStruct((R, C), F32))
        if want_bf:
            out_specs.append(row)
            out_shape.append(jax.ShapeDtypeStruct((R, C), BF16))
    out_specs.append(vec)
    out_shape.append(jax.ShapeDtypeStruct((1, C), F32))
    args = [dh, x, g] + ([dres] if has_res else [])
    return pl.pallas_call(
        body, name=name, grid=(R // tr,), in_specs=in_specs, out_specs=out_specs, out_shape=out_shape,
        compiler_params=_cp(("arbitrary",), VMEM_MB),
    )(*args)


def _loss_bwd(x3, g, tgt, *, name):
    R, C = x3.shape
    tr = _pick(R, TR_EW, 16)
    n = R // tr
    row = pl.BlockSpec((tr, C), lambda i: (i, 0))
    vec = pl.BlockSpec((1, C), lambda i: (0, 0))

    def body(x_ref, g_ref, t_ref, dx_ref, dxb_ref, dg_ref, loss_ref, acc_ref):
        i = pl.program_id(0)
        xv = x_ref[...]
        gv = g_ref[...]
        r = lax.rsqrt(jnp.mean(xv * xv, axis=-1, keepdims=True) + EPS)
        xh = xv * r
        e = xh * gv - t_ref[...]
        dy = e * (1.0 / C)
        sq = jnp.sum(e * e, axis=0, keepdims=True)
        dgp = jnp.sum(dy * xh, axis=0, keepdims=True)

        @pl.when(i == 0)
        def _():
            acc_ref[...] = sq
            dg_ref[...] = dgp

        @pl.when(i > 0)
        def _():
            acc_ref[...] += sq
            dg_ref[...] += dgp

        t = dy * gv
        dx = r * (t - xh * jnp.mean(t * xh, axis=-1, keepdims=True))
        dx_ref[...] = dx
        dxb_ref[...] = dx.astype(BF16)

        @pl.when(i == n - 1)
        def _():
            loss_ref[...] = jnp.sum(acc_ref[...], axis=-1, keepdims=True) * (0.5 / C)

    return pl.pallas_call(
        body, name=name, grid=(n,),
        in_specs=[row, vec, row],
        out_specs=[row, row, vec, pl.BlockSpec((1, 1), lambda i: (0, 0))],
        out_shape=[jax.ShapeDtypeStruct((R, C), F32), jax.ShapeDtypeStruct((R, C), BF16),
                   jax.ShapeDtypeStruct((1, C), F32), jax.ShapeDtypeStruct((1, 1), F32)],
        scratch_shapes=[pltpu.VMEM((1, C), F32)],
        compiler_params=_cp(("arbitrary",), VMEM_MB),
    )(x3, g, tgt)


def _offsets():
    u0 = 0
    v0 = DS
    b0 = 2 * DS
    c0 = b0 + DC
    x0 = c0 + DC
    q0 = x0 + DC
    return u0, v0, b0, c0, x0, q0


def _tri_mask(lower):
    r = lax.broadcasted_iota(jnp.int32, (CHUNK, CHUNK), 0)
    c = lax.broadcasted_iota(jnp.int32, (CHUNK, CHUNK), 1)
    return (r >= c) if lower else (c >= r)


def _layer_norm_stats(vg):
    mu = jnp.mean(vg, axis=-1, keepdims=True)
    vc = vg - mu
    rstd = lax.rsqrt(jnp.mean(vc * vc, axis=-1, keepdims=True) + EPS)
    return vc * rstd, rstd


def _softmax_rows(qh, kh):
    s = lax.dot_general(qh, kh, (((1,), (1,)), ((), ())), preferred_element_type=F32)
    m = jnp.max(s, axis=-1, keepdims=True)
    e = jnp.exp(s - m)
    return e / jnp.sum(e, axis=-1, keepdims=True)


def _mix_fwd(proj, kv, w_s, bs_t, ln_g, ln_b, conv_w, g_head, *, name):
    assert DS == DC
    tr = _pick(S, TR_MIX, CHUNK)
    n = S // tr
    nck = tr // CHUNK
    u0, v0, b0, c0, x0, q0 = _offsets()
    hb = tr // HALO

    def body(p_ref, cprev_ref, xprev_ref, kv_ref, ws_ref, bst_ref, lng_ref, lnb_ref, cw_ref, gh_ref,
             heads_ref, hn_ref, ycv_ref, buf_ref):
        i = pl.program_id(0)

        def emit(col, val):
            rs = lax.rsqrt(jnp.mean(val * val, axis=-1, keepdims=True) + EPS)
            heads_ref[:, col:col + HD] = val
            hn_ref[:, col:col + HD] = ((val * rs) * gh_ref[:, col:col + HD]).astype(BF16)

        vhat, _ = _layer_norm_stats(_gelu(p_ref[:, v0:v0 + DS]))
        vnb = (vhat * lng_ref[...] + lnb_ref[...]).astype(BF16)
        low = _tri_mask(True)
        for h in range(NSH):
            wt = jnp.where(low, ws_ref[h], 0.0).astype(BF16)
            bcol = bst_ref[:, h:h + 1]
            parts = []
            for c in range(nck):
                blk = vnb[c * CHUNK:(c + 1) * CHUNK, h * HD:(h + 1) * HD]
                parts.append(jnp.dot(wt, blk, preferred_element_type=F32) + bcol)
            mixed = parts[0] if nck == 1 else jnp.concatenate(parts, axis=0)
            emit(h * HD, _gelu(p_ref[:, u0 + h * HD:u0 + (h + 1) * HD]) * mixed)

        xc = p_ref[:, c0:c0 + DC] * p_ref[:, x0:x0 + DC]
        prev = cprev_ref[...] * xprev_ref[...]
        buf_ref[0:HALO, :] = jnp.where(i > 0, prev, 0.0)
        buf_ref[HALO:HALO + tr, :] = xc
        y = (cw_ref[2:3, :] * xc + cw_ref[1:2, :] * buf_ref[HALO - 1:HALO - 1 + tr, :]
             + cw_ref[0:1, :] * buf_ref[HALO - 2:HALO - 2 + tr, :])
        ycv_ref[...] = y
        cout = p_ref[:, b0:b0 + DC] * y
        for h in range(NCH):
            emit(DS + h * HD, cout[:, h * HD:(h + 1) * HD])

        for h in range(NMH):
            qh = (p_ref[:, q0 + h * HD:q0 + (h + 1) * HD] * SCALE).astype(BF16)
            kh = kv_ref[:, h * HD:(h + 1) * HD].astype(BF16)
            vh = kv_ref[:, DM + h * HD:DM + (h + 1) * HD].astype(BF16)
            p = _softmax_rows(qh, kh)
            emit(DS + DC + h * HD, jnp.dot(p.astype(BF16), vh, preferred_element_type=F32))

    full = lambda shape: pl.BlockSpec(shape, lambda i: (0,) * len(shape))
    halo_c = pl.BlockSpec((HALO, DC), lambda i: (jnp.maximum(i * hb - 1, 0), c0 // DC))
    halo_x = pl.BlockSpec((HALO, DC), lambda i: (jnp.maximum(i * hb - 1, 0), x0 // DC))
    return pl.pallas_call(
        body, name=name, grid=(n,),
        in_specs=[pl.BlockSpec((tr, DIN), lambda i: (i, 0)), halo_c, halo_x,
                  full((NMEM, 2 * DM)), full((NSH, CHUNK, CHUNK)), full((CHUNK, NSH)),
                  full((1, DS)), full((1, DS)), full((3, DC)), full((1, D))],
        out_specs=[pl.BlockSpec((tr, D), lambda i: (i, 0)), pl.BlockSpec((tr, D), lambda i: (i, 0)),
                   pl.BlockSpec((tr, DC), lambda i: (i, 0))],
        out_shape=[jax.ShapeDtypeStruct((S, D), F32), jax.ShapeDtypeStruct((S, D), BF16),
                   jax.ShapeDtypeStruct((S, DC), F32)],
        scratch_shapes=[pltpu.VMEM((tr + HALO, DC), F32)],
        compiler_params=_cp(("parallel",), VMEM_MB),
    )(proj, proj, proj, kv, w_s, bs_t, ln_g, ln_b, conv_w, g_head)


def _mix_bwd(dhn, heads, proj, ycv, kv, w_s, bs_t, ln_g, ln_b, conv_w, g_head, after, *, name):
    assert DS == DC
    tr = _pick(S, TR_MIX, CHUNK)
    n = S // tr
    nck = tr // CHUNK
    u0, v0, b0, c0, x0, q0 = _offsets()
    hb = tr // HALO
    last_hb = S // HALO - 1

    def body(dhn_ref, heads_ref, p_ref, ycv_ref, dhn_nx_ref, heads_nx_ref, b_nx_ref, kv_ref, ws_ref, bst_ref,
             lng_ref, lnb_ref, cw_ref, gh_ref, _after_ref,
             dp_ref, dkv_ref, dws_ref, dbs_ref, dlng_ref, dlnb_ref, dcw_ref, dgh_ref, buf_ref, dvn_ref):
        i = pl.program_id(0)

        @pl.when(i == 0)
        def _():
            dkv_ref[...] = jnp.zeros_like(dkv_ref)
            dws_ref[...] = jnp.zeros_like(dws_ref)
            dbs_ref[...] = jnp.zeros_like(dbs_ref)
            dlng_ref[...] = jnp.zeros_like(dlng_ref)
            dlnb_ref[...] = jnp.zeros_like(dlnb_ref)
            dcw_ref[...] = jnp.zeros_like(dcw_ref)
            dgh_ref[...] = jnp.zeros_like(dgh_ref)

        def head_bwd(a, dn, gh):
            rs = lax.rsqrt(jnp.mean(a * a, axis=-1, keepdims=True) + EPS)
            ah = a * rs
            t = dn * gh
            return rs * (t - ah * jnp.mean(t * ah, axis=-1, keepdims=True)), jnp.sum(dn * ah, axis=0, keepdims=True)

        def head_grad(col):
            da, dg = head_bwd(heads_ref[:, col:col + HD], dhn_ref[:, col:col + HD], gh_ref[:, col:col + HD])
            dgh_ref[:, col:col + HD] += dg
            return da

        v = p_ref[:, v0:v0 + DS]
        vhat, rstd = _layer_norm_stats(_gelu(v))
        vnb = (vhat * lng_ref[...] + lnb_ref[...]).astype(BF16)
        low = _tri_mask(True)
        ones = jnp.ones((HALO, HD), BF16)
        for h in range(NSH):
            w_h = ws_ref[h]
            wt = jnp.where(low, w_h, 0.0).astype(BF16)
            bcol = bst_ref[:, h:h + 1]
            da = head_grad(h * HD)
            u = p_ref[:, u0 + h * HD:u0 + (h + 1) * HD]
            ug = _gelu(u)
            dws = jnp.zeros((CHUNK, CHUNK), F32)
            dbs = jnp.zeros((HALO, CHUNK), F32)
            mixed_parts = []
            for c in range(nck):
                rows = slice(c * CHUNK, (c + 1) * CHUNK)
                blk = vnb[rows, h * HD:(h + 1) * HD]
                mixed_parts.append(jnp.dot(wt, blk, preferred_element_type=F32) + bcol)
                dmb = (da[rows] * ug[rows]).astype(BF16)
                dws = dws + lax.dot_general(dmb, blk, (((1,), (1,)), ((), ())), preferred_element_type=F32)
                dbs = dbs + lax.dot_general(ones, dmb, (((1,), (1,)), ((), ())), preferred_element_type=F32)
                dvn_ref[c * CHUNK:(c + 1) * CHUNK, h * HD:(h + 1) * HD] = lax.dot_general(
                    wt, dmb, (((0,), (0,)), ((), ())), preferred_element_type=F32)
            mixed = mixed_parts[0] if nck == 1 else jnp.concatenate(mixed_parts, axis=0)
            dp_ref[:, u0 + h * HD:u0 + (h + 1) * HD] = ((da * mixed) * _gelu_grad(u)).astype(BF16)
            dws_ref[h] += jnp.where(low, dws, 0.0)
            dbs_ref[h] += dbs
        dvn = dvn_ref[...]
        dlng_ref[...] += jnp.sum(dvn * vhat, axis=0, keepdims=True)
        dlnb_ref[...] += jnp.sum(dvn, axis=0, keepdims=True)
        dvh = dvn * lng_ref[...]
        dvg = rstd * (dvh - jnp.mean(dvh, axis=-1, keepdims=True)
                      - vhat * jnp.mean(dvh * vhat, axis=-1, keepdims=True))
        dp_ref[:, v0:v0 + DS] = (dvg * _gelu_grad(v)).astype(BF16)

        dc = jnp.concatenate([head_grad(DS + h * HD) for h in range(NCH)], axis=1)
        dc_nx = jnp.concatenate(
            [head_bwd(heads_nx_ref[:, h * HD:(h + 1) * HD], dhn_nx_ref[:, h * HD:(h + 1) * HD],
                      gh_ref[:, DS + h * HD:DS + (h + 1) * HD])[0] for h in range(NCH)], axis=1)
        bg = p_ref[:, b0:b0 + DC]
        cg = p_ref[:, c0:c0 + DC]
        xin = p_ref[:, x0:x0 + DC]
        dp_ref[:, b0:b0 + DC] = (dc * ycv_ref[...]).astype(BF16)
        dyv = dc * bg
        buf_ref[0:tr, :] = dyv
        buf_ref[tr:tr + HALO, :] = jnp.where(i < n - 1, dc_nx * b_nx_ref[...], 0.0)
        sh1 = buf_ref[1:1 + tr, :]
        sh0 = buf_ref[2:2 + tr, :]
        dxc = cw_ref[2:3, :] * dyv + cw_ref[1:2, :] * sh1 + cw_ref[0:1, :] * sh0
        xc = cg * xin
        dp_ref[:, c0:c0 + DC] = (dxc * xin).astype(BF16)
        dp_ref[:, x0:x0 + DC] = (dxc * cg).astype(BF16)
        dcw_ref[0:1, :] += jnp.sum(sh0 * xc, axis=0, keepdims=True)
        dcw_ref[1:2, :] += jnp.sum(sh1 * xc, axis=0, keepdims=True)
        dcw_ref[2:3, :] += jnp.sum(dyv * xc, axis=0, keepdims=True)

        for h in range(NMH):
            do = head_grad(DS + DC + h * HD).astype(BF16)
            qh = (p_ref[:, q0 + h * HD:q0 + (h + 1) * HD] * SCALE).astype(BF16)
            kh = kv_ref[:, h * HD:(h + 1) * HD].astype(BF16)
            vh = kv_ref[:, DM + h * HD:DM + (h + 1) * HD].astype(BF16)
            p = _softmax_rows(qh, kh)
            dpr = lax.dot_general(do, vh, (((1,), (1,)), ((), ())), preferred_element_type=F32)
            ds = (p * (dpr - jnp.sum(dpr * p, axis=-1, keepdims=True))).astype(BF16)
            dp_ref[:, q0 + h * HD:q0 + (h + 1) * HD] = (
                jnp.dot(ds, kh, preferred_element_type=F32) * SCALE).astype(BF16)
            dkv_ref[:, h * HD:(h + 1) * HD] += lax.dot_general(
                ds, qh, (((0,), (0,)), ((), ())), preferred_element_type=F32)
            dkv_ref[:, DM + h * HD:DM + (h + 1) * HD] += lax.dot_general(
                p.astype(BF16), do, (((0,), (0,)), ((), ())), preferred_element_type=F32)

    full = lambda shape: pl.BlockSpec(shape, lambda i: (0,) * len(shape))
    row = lambda c: pl.BlockSpec((tr, c), lambda i: (i, 0))
    nxt = lambda col: pl.BlockSpec((HALO, DC), lambda i: (jnp.minimum((i + 1) * hb, last_hb), col))
    return pl.pallas_call(
        body, name=name, grid=(n,),
        in_specs=[row(D), row(D), row(DIN), row(DC), nxt(DS // DC), nxt(DS // DC), nxt(b0 // DC),
                  full((NMEM, 2 * DM)), full((NSH, CHUNK, CHUNK)), full((CHUNK, NSH)),
                  full((1, DS)), full((1, DS)), full((3, DC)), full((1, D)), ANY],
        out_specs=[row(DIN), full((NMEM, 2 * DM)), full((NSH, CHUNK, CHUNK)), full((NSH, HALO, CHUNK)),
                   full((1, DS)), full((1, DS)), full((HALO, DC)), full((1, D))],
        out_shape=[jax.ShapeDtypeStruct((S, DIN), BF16), jax.ShapeDtypeStruct((NMEM, 2 * DM), F32),
                   jax.ShapeDtypeStruct((NSH, CHUNK, CHUNK), F32), jax.ShapeDtypeStruct((NSH, HALO, CHUNK), F32),
                   jax.ShapeDtypeStruct((1, DS), F32), jax.ShapeDtypeStruct((1, DS), F32),
                   jax.ShapeDtypeStruct((HALO, DC), F32), jax.ShapeDtypeStruct((1, D), F32)],
        scratch_shapes=[pltpu.VMEM((tr + HALO, DC), F32), pltpu.VMEM((tr, DS), F32)],
        compiler_params=_cp(("arbitrary",), VMEM_MB),
    )(dhn, heads, proj, ycv, dhn, heads, proj, kv, w_s, bs_t, ln_g, ln_b, conv_w, g_head, after)


def _place():
    x, y, c = lax.axis_index("x"), lax.axis_index("y"), lax.axis_index("c")
    chips = [(1 - x, y), (x, 1 - y), (1 - x, 1 - y)]
    return x, y, c, chips


ANY = pl.BlockSpec(memory_space=pl.ANY)


HBM = pl.BlockSpec(memory_space=pltpu.HBM)
SEM = pl.BlockSpec(memory_space=pltpu.SEMAPHORE)
EFFECT = pltpu.SideEffectType.DATAFLOW_SIDE_EFFECTING
N_PEER_CHIPS = 3


def _in_hbm(a):
    return pltpu.with_memory_space_constraint(a, pltpu.HBM)


def _allgather_start(bufs, after, *, name):
    nw = len(bufs)

    def body(*refs):
        ins, send, recv = refs[:nw], refs[nw + 1:2 * nw + 1], refs[2 * nw + 1:3 * nw + 1]
        token = refs[4 * nw + 1]
        x, y, c, chips = _place()
        s = 2 * x + y
        for w in range(nw):
            hr = bufs[w].shape[1] // 2
            rows = ins[w].at[s, pl.ds(c * hr, hr)]
            for cx, cy in chips:
                pltpu.make_async_remote_copy(src_ref=rows, dst_ref=rows, send_sem=send[w], recv_sem=recv[w],
                                             device_id=(cx, cy, c), device_id_type=MESH).start()
        token[...] = jnp.zeros_like(token)

    res = pl.pallas_call(
        body, name=name,
        in_specs=[HBM] * nw + [ANY],
        out_specs=[SEM] * (2 * nw) + [HBM] * nw + [pl.BlockSpec(memory_space=pltpu.VMEM)],
        out_shape=[pltpu.SemaphoreType.DMA(())] * (2 * nw) + [pltpu.HBM(a.shape, a.dtype) for a in bufs]
        + [jax.ShapeDtypeStruct((8, 128), F32)],
        input_output_aliases={w: 2 * nw + w for w in range(nw)},
        compiler_params=pltpu.CompilerParams(has_side_effects=EFFECT),
    )(*[_in_hbm(a) for a in bufs], after)
    return res[:nw], res[nw:2 * nw], res[2 * nw:3 * nw], res[3 * nw]


def _scatter_start(parts, bufs, *, name):
    nw = len(parts)

    def body(*refs):
        src, dst = refs[:nw], refs[nw:2 * nw]
        send, recv = refs[2 * nw:3 * nw], refs[3 * nw:4 * nw]
        token = refs[6 * nw]
        x, y, c, chips = _place()
        s = 2 * x + y
        for w in range(nw):
            for cx, cy in chips:
                pltpu.make_async_remote_copy(src_ref=src[w].at[2 * cx + cy], dst_ref=dst[w].at[s], send_sem=send[w],
                                             recv_sem=recv[w], device_id=(cx, cy, c), device_id_type=MESH).start()
        token[...] = jnp.zeros_like(token)

    res = pl.pallas_call(
        body, name=name,
        in_specs=[HBM] * (2 * nw),
        out_specs=[SEM] * (2 * nw) + [HBM] * (2 * nw) + [pl.BlockSpec(memory_space=pltpu.VMEM)],
        out_shape=[pltpu.SemaphoreType.DMA(())] * (2 * nw) + [pltpu.HBM(a.shape, a.dtype) for a in parts + bufs]
        + [jax.ShapeDtypeStruct((8, 128), F32)],
        input_output_aliases={k: 2 * nw + k for k in range(2 * nw)},
        compiler_params=pltpu.CompilerParams(has_side_effects=EFFECT),
    )(*[_in_hbm(a) for a in parts + bufs])
    return res[:nw], res[nw:2 * nw], res[2 * nw:3 * nw], res[3 * nw:4 * nw], res[4 * nw]


def _transfer_wait(sends, recvs, thru, sizes, after, *, name):
    n = len(sends)
    flat = [a for group in thru for a in group]

    def body(*refs):
        bufs = refs[:len(flat)]
        send = refs[len(flat):len(flat) + n]
        recv = refs[len(flat) + n:len(flat) + 2 * n]
        token = refs[2 * len(flat) + 2 * n + 1]
        token[...] = jnp.zeros_like(token)
        x, y, c, _ = _place()
        pos = 0
        for k in range(n):
            rows, _cols = sizes[k]
            region = bufs[pos].at[pl.ds(0, N_PEER_CHIPS), pl.ds(0, rows)]
            pos += len(thru[k])
            cp = pltpu.make_async_remote_copy(src_ref=region, dst_ref=region, send_sem=send[k], recv_sem=recv[k],
                                              device_id=(x, y, 1 - c), device_id_type=MESH)
            cp.wait_send()
            cp.wait_recv()

    res = pl.pallas_call(
        body, name=name,
        in_specs=[HBM] * len(flat) + [SEM] * (2 * n) + [pl.BlockSpec(memory_space=pl.ANY)],
        out_specs=[HBM] * len(flat) + [pl.BlockSpec(memory_space=pltpu.VMEM)],
        out_shape=[pltpu.HBM(a.shape, a.dtype) for a in flat] + [jax.ShapeDtypeStruct((8, 128), F32)],
        input_output_aliases={k: k for k in range(len(flat))},
        compiler_params=pltpu.CompilerParams(has_side_effects=EFFECT),
    )(*flat, *sends, *recvs, after)
    out, pos = [], 0
    for group in thru:
        out.append(res[pos:pos + len(group)])
        pos += len(group)
    return out, res[len(flat)]


def _forward_to_sibling(bufs, after, *, name):
    nw = len(bufs)

    def body(*refs):
        outs = refs[nw + 1:2 * nw + 1]
        send, recv = refs[2 * nw + 1:]
        x, y, c, chips = _place()
        me, sibling = (x, y, c), (x, y, 1 - c)

        def d2d(w, j, which, to):
            cx, cy = chips[j]
            hr = bufs[w].shape[1] // 2
            rows = outs[w].at[2 * cx + cy, pl.ds(which * hr, hr)]
            return pltpu.make_async_remote_copy(
                src_ref=rows, dst_ref=rows, send_sem=send.at[N_PEER_CHIPS * w + j],
                recv_sem=recv.at[N_PEER_CHIPS * w + j], device_id=to, device_id_type=MESH)

        passed = [d2d(w, j, c, sibling) for w in range(nw) for j in range(N_PEER_CHIPS)]
        for cp in passed:
            cp.start()
        for w in range(nw):
            for j in range(N_PEER_CHIPS):
                d2d(w, j, 1 - c, me).wait_recv()
        for cp in passed:
            cp.wait_send()

    return pl.pallas_call(
        body, name=name,
        in_specs=[ANY] * (nw + 1), out_specs=[ANY] * nw,
        out_shape=[jax.ShapeDtypeStruct(a.shape, a.dtype) for a in bufs],
        input_output_aliases={w: w for w in range(nw)},
        scratch_shapes=[pltpu.SemaphoreType.DMA((N_PEER_CHIPS * nw,)), pltpu.SemaphoreType.DMA((N_PEER_CHIPS * nw,))],
    )(*bufs, after)


def _sibling_exchange(arrs, *, name):
    nw = len(arrs)

    def body(*refs):
        ins, outs = refs[:nw], refs[nw:2 * nw]
        send, recv = refs[2 * nw:]
        x, y, c, _ = _place()
        cps = [pltpu.make_async_remote_copy(src_ref=ins[w], dst_ref=outs[w], send_sem=send.at[w], recv_sem=recv.at[w],
                                            device_id=(x, y, 1 - c), device_id_type=MESH) for w in range(nw)]
        for cp in cps:
            cp.start()
        for cp in cps:
            cp.wait()

    return pl.pallas_call(
        body, name=name, in_specs=[ANY] * nw, out_specs=[ANY] * nw,
        out_shape=[jax.ShapeDtypeStruct(a.shape, a.dtype) for a in arrs],
        scratch_shapes=[pltpu.SemaphoreType.DMA((nw,)), pltpu.SemaphoreType.DMA((nw,))],
    )(*arrs)


def _allreduce_small(p, *, name):
    R = p.shape[0]
    hr = R // 2

    def body(p_ref, out_ref, sib_ref, sum_ref, gat_ref, tot_ref, send, recv):
        x, y, c, chips = _place()
        s = 2 * x + y
        sibling = (x, y, 1 - c)
        rows = pl.ds(pl.multiple_of(c * hr, 8), hr)
        swap = pltpu.make_async_remote_copy(src_ref=p_ref, dst_ref=sib_ref, send_sem=send.at[0], recv_sem=recv.at[0],
                                            device_id=sibling, device_id_type=MESH)
        swap.start()
        swap.wait()
        sum_ref[...] = p_ref[...] + sib_ref[...]
        gat_ref[s] = sum_ref[rows, :]
        cps = [pltpu.make_async_remote_copy(src_ref=sum_ref.at[rows], dst_ref=gat_ref.at[s], send_sem=send.at[1 + j],
                                            recv_sem=recv.at[1 + j], device_id=(cx, cy, c), device_id_type=MESH)
               for j, (cx, cy) in enumerate(chips)]
        for cp in cps:
            cp.start()
        for cp in cps:
            cp.wait()
        tot_ref[...] = ((gat_ref[0] + gat_ref[1]) + gat_ref[2]) + gat_ref[3]
        out_ref[rows, :] = tot_ref[...]
        share = pltpu.make_async_remote_copy(src_ref=tot_ref, dst_ref=out_ref.at[rows], send_sem=send.at[4],
                                             recv_sem=recv.at[4], device_id=sibling, device_id_type=MESH)
        share.start()
        share.wait_send()
        other = out_ref.at[pl.ds(pl.multiple_of((1 - c) * hr, 8), hr)]
        pltpu.make_async_remote_copy(src_ref=other, dst_ref=other, send_sem=send.at[4], recv_sem=recv.at[4],
                                     device_id=(x, y, c), device_id_type=MESH).wait_recv()

    vmem = pl.BlockSpec(memory_space=pltpu.VMEM)
    return pl.pallas_call(
        body, name=name, in_specs=[vmem], out_specs=vmem,
        out_shape=jax.ShapeDtypeStruct((R, 128), F32),
        scratch_shapes=[pltpu.VMEM((R, 128), F32), pltpu.VMEM((R, 128), F32), pltpu.VMEM((NCHIP, hr, 128), F32),
                        pltpu.VMEM((hr, 128), F32), pltpu.SemaphoreType.DMA((5,)), pltpu.SemaphoreType.DMA((5,))],
    )(p)


def _select_half_bf16(g, half, add, slot, *, name):
    _, R, C = g.shape
    hr = R // 2
    tr = _pick(hr, TR_EW, 16)
    nb = hr // tr
    sel = jnp.concatenate([jnp.reshape(half, (1,)).astype(jnp.int32), slot])
    has_add = add is not None

    def body(s_ref, g_ref, *rest):
        val = g_ref[...]
        if has_add:
            a_ref, o_ref, own_ref = rest
            val = (val + a_ref[...].astype(F32)).astype(BF16)
            o_ref[...] = val

            @pl.when(pl.program_id(1) == s_ref[1])
            def _():
                own_ref[...] = val
        else:
            rest[0][...] = val.astype(BF16)

    g_spec = pl.BlockSpec((None, tr, C), lambda i, j, s: (j, s[0] * nb + i, 0))
    o_spec = pl.BlockSpec((None, tr, C), lambda i, j, s: (j, i, 0))
    own_spec = pl.BlockSpec((None, tr, C), lambda i, j, s: (s[1], i, 0))
    shape = jax.ShapeDtypeStruct((NCHIP, hr, C), BF16)
    return pl.pallas_call(
        body, name=name,
        grid_spec=pltpu.PrefetchScalarGridSpec(
            num_scalar_prefetch=1, grid=(nb, NCHIP),
            in_specs=[g_spec] + ([o_spec] if has_add else []),
            out_specs=[o_spec, own_spec] if has_add else o_spec),
        out_shape=[shape, shape] if has_add else shape,
        compiler_params=_cp(("parallel", "arbitrary"), VMEM_MB),
    )(sel, g, *([add] if has_add else []))


def _sum_slots(r, *, name):
    _, R, C = r.shape
    tr = _pick(R, TR_EW, 16)

    def body(r_ref, o_ref):
        acc = r_ref[0].astype(F32) + r_ref[1].astype(F32)
        for j in range(2, NCHIP):
            acc = acc + r_ref[j].astype(F32)
        o_ref[...] = acc

    return pl.pallas_call(
        body, name=name, grid=(R // tr,),
        in_specs=[pl.BlockSpec((NCHIP, tr, C), lambda i: (0, i, 0))],
        out_specs=pl.BlockSpec((tr, C), lambda i: (i, 0)),
        out_shape=jax.ShapeDtypeStruct((R, C), F32),
        compiler_params=_cp(("parallel",), VMEM_MB),
    )(r)


def _adamw_math(w, g, m, v):
    m = ADAM_B1 * m + (1.0 - ADAM_B1) * g
    v = ADAM_B2 * v + (1.0 - ADAM_B2) * (g * g)
    m_hat = m / (1.0 - ADAM_B1 ** ADAM_STEP)
    v_hat = v / (1.0 - ADAM_B2 ** ADAM_STEP)
    delta = -ADAM_LR * (m_hat / (jnp.sqrt(v_hat) + ADAM_EPS) + ADAM_WD * w)
    return delta, m, v


def _adamw(w, g_mine, g_sib, m, v, core, *, name):
    R, C = w.shape
    hr = R // 2
    tr = _pick(hr, TR_EW, 8)
    nb = hr // tr
    row = pl.BlockSpec((tr, C), lambda hh, i, c: (hh * nb + i, 0))
    half = pl.BlockSpec((tr, C), lambda hh, i, c: (i, 0))

    def body(c_ref, w_ref, gm_ref, gs_ref, m_ref, v_ref, go_ref, d_ref, mo_ref, vo_ref):
        gv = jnp.where(pl.program_id(0) == c_ref[0], gm_ref[...], gs_ref[...])
        d, mn, vn = _adamw_math(w_ref[...], gv, m_ref[...], v_ref[...])
        go_ref[...] = gv
        d_ref[...] = d
        mo_ref[...] = mn
        vo_ref[...] = vn

    return pl.pallas_call(
        body, name=name,
        grid_spec=pltpu.PrefetchScalarGridSpec(
            num_scalar_prefetch=1, grid=(2, nb),
            in_specs=[row, half, half, row, row], out_specs=[row] * 4),
        out_shape=[jax.ShapeDtypeStruct((R, C), F32)] * 4,
        compiler_params=_cp(("parallel", "parallel"), VMEM_MB),
    )(core, w, g_mine, g_sib, m, v)


def _adamw_small(ws, gs, ms, vs, *, name):
    n = len(ws)

    def body(*refs):
        w_r, g_r, m_r, v_r = refs[:n], refs[n:2 * n], refs[2 * n:3 * n], refs[3 * n:4 * n]
        d_r, mo_r, vo_r = refs[4 * n:5 * n], refs[5 * n:6 * n], refs[6 * n:7 * n]
        for k in range(n):
            d, mn, vn = _adamw_math(w_r[k][...], g_r[k][...], m_r[k][...], v_r[k][...])
            d_r[k][...] = d
            mo_r[k][...] = mn
            vo_r[k][...] = vn

    shapes = [jax.ShapeDtypeStruct(w.shape, F32) for w in ws]
    res = pl.pallas_call(body, name=name, out_shape=shapes * 3)(*ws, *gs, *ms, *vs)
    return res[:n], res[n:2 * n], res[2 * n:]


_PACK_ROWS = 8


def _pack(parts):
    rows = []
    for a in parts:
        flat = a.reshape(-1)
        n = -(-flat.shape[0] // (_PACK_ROWS * 128)) * (_PACK_ROWS * 128)
        rows.append(jnp.pad(flat, (0, n - flat.shape[0])).reshape(-1, 128))
    total = sum(r.shape[0] for r in rows)
    if total % 16:
        rows.append(jnp.zeros((16 - total % 16, 128), F32))
    return jnp.concatenate(rows, axis=0)


def _unpack(p, shapes):
    out, r = [], 0
    for shp in shapes:
        n = math.prod(shp)
        nr = -(-n // (_PACK_ROWS * 128)) * _PACK_ROWS
        out.append(p[r:r + nr].reshape(-1)[:n].reshape(shp))
        r += nr
    return out


def kernel(x, mem, g_mix, w_in, ln_v_g, ln_v_b, w_s, b_s, conv_w, g_mem, w_kv, g_head, w_o, g_ffn, w_ffn1, w_ffn2, g_final, loss_target, m_g_mix, m_w_in, m_ln_v_g, m_ln_v_b, m_w_s, m_b_s, m_conv_w, m_g_mem, m_w_kv, m_g_head, m_w_o, m_g_ffn, m_w_ffn1, m_w_ffn2, m_g_final, v_g_mix, v_w_in, v_ln_v_g, v_ln_v_b, v_w_s, v_b_s, v_conv_w, v_g_mem, v_w_kv, v_g_head, v_w_o, v_g_ffn, v_w_ffn1, v_w_ffn2, v_g_final):
    sds = jax.ShapeDtypeStruct
    xi, yi = lax.axis_index("x"), lax.axis_index("y")
    shard = 2 * xi + yi
    x2d, mem2d, tgt = x[0], mem[0], loss_target[0]
    ws3, bs2 = w_s[0], b_s[0]
    g_final2 = g_final.reshape(1, D)
    dff4 = DFF // NCHIP
    din4 = DIN // NCHIP
    dcv4 = DC // NCHIP

    big = [w_in[0], w_kv[0], w_o[0], w_ffn1[0], w_ffn2[0]]
    big_names = ["w_in", "w_kv", "w_o", "w_ffn1", "w_ffn2"]
    slot = jnp.reshape(shard, (1,)).astype(jnp.int32)
    core = jnp.reshape(lax.axis_index("c"), (1,)).astype(jnp.int32)
    conv_pad = jnp.pad(conv_w[0], ((0, 16 - 3), (0, 256 - dcv4)))
    conv_slots = lax.dynamic_update_slice(jnp.zeros((NCHIP, 16, 256), F32), conv_pad[None], (shard, 0, 0))

    def gather_start(bufs, after, nm):
        return _allgather_start(bufs, after, name="ag_start_" + nm)

    def gather_wait(state, after, nm):
        send, recv, bufs, _ = state
        got, token = _transfer_wait(send, recv, [[b] for b in bufs], [(b.shape[1] // 2, b.shape[2]) for b in bufs],
                                    after, name="ag_wait_" + nm)
        return [g[0] for g in got], token

    cast = lambda k: _cast_into_slot(big[k], slot, name="cast_" + big_names[k])
    ag_in = gather_start([cast(0), conv_slots], slot, "in")
    kvo_b = [cast(1), cast(2)]
    w1_b, w2_b = cast(3), cast(4)
    bs_t = bs2.T

    h = _rms_fwd(x2d, g_mix, name="rms_mix", after=[ag_in[3]])
    mem_n = _rms_fwd(mem2d, g_mem, name="rms_mem", after=[h])
    got_in, tok = gather_wait(ag_in, mem_n, "in")
    ag_kvo = gather_start(kvo_b, tok, "kvo")
    win4, conv4 = _forward_to_sibling(got_in, ag_kvo[3], name="ag_fwd_in")
    w_in_full = win4.transpose(1, 0, 2).reshape(D, DIN)
    conv_full = conv4[:, :3, :dcv4].transpose(1, 0, 2).reshape(3, DC)
    (proj,) = _matmul(h, w_in_full, name="mm_proj", M=S, N=DIN, K=D, tn=DIN // 2, outs=[sds((S, DIN), F32)])
    got_kvo, tok = gather_wait(ag_kvo, proj, "kvo")
    ag_w1 = gather_start([w1_b], tok, "ffn1")
    wkv4, wo4 = _forward_to_sibling(got_kvo, ag_w1[3], name="ag_fwd_kvo")
    w_kv_full = wkv4.reshape(D, 2 * DM)
    w_o_full = wo4.reshape(D, D)
    (kv,) = _matmul(mem_n, w_kv_full, name="mm_kv", M=NMEM, N=2 * DM, K=D, outs=[sds((NMEM, 2 * DM), F32)])
    heads, hn, ycv = _mix_fwd(proj, kv, ws3, bs_t, ln_v_g, ln_v_b, conv_full, g_head, name="mix_fwd")
    (x2,) = _matmul(hn, w_o_full, name="mm_wo", M=S, N=D, K=D, outs=[sds((S, D), F32)],
                    epi=lambda acc, res: (acc + res,), extras=[(x2d, _tile_spec())])
    h2 = _rms_fwd(x2, g_ffn, name="rms_ffn")
    got_w1, tok = gather_wait(ag_w1, h2, "ffn1")
    ag_w2 = gather_start([w2_b], tok, "ffn2")
    (w14,) = _forward_to_sibling(got_w1, ag_w2[3], name="ag_fwd_ffn1")

    def w1_cols(tn, tk):
        nb = dff4 // tn
        return pl.BlockSpec((None, tk, tn), lambda j, i, k: (j // nb, k, j % nb))

    f, act = _matmul(h2, w14, name="mm_ffn1", M=S, N=DFF, K=D, b_spec=w1_cols,
                     outs=[sds((S, DFF), F32), sds((S, DFF), BF16)],
                     epi=lambda acc: (acc, jnp.square(jnp.maximum(acc, 0.0))))
    got_w2, tok = gather_wait(ag_w2, act, "ffn2")
    (w24,) = _forward_to_sibling(got_w2, tok, name="ag_fwd_ffn2")
    w2_full = w24.reshape(DFF, D)
    (x3,) = _matmul(act, w2_full, name="mm_ffn2", M=S, N=D, K=DFF, outs=[sds((S, D), F32)],
                    epi=lambda acc, res: (acc + res,), extras=[(x2, _tile_spec())])

    ci = lax.axis_index("c")

    def rs_begin(g4, nm):
        to_sib = _select_half_bf16(g4, 1 - ci, None, slot, name="rs_send_" + nm)
        (from_sib,) = _sibling_exchange([to_sib], name="rs_sibling_" + nm)
        part, buf = _select_half_bf16(g4, ci, from_sib, slot, name="rs_add_" + nm)
        return _scatter_start([part], [buf], name="rs_start_" + nm)

    def rs_end(state, after, nm):
        send, recv, parts, bufs, _ = state
        (((buf, _),), _) = _transfer_wait(send, recv, [[bufs[0], parts[0]]], [bufs[0].shape[1:]], after,
                                          name="rs_wait_" + nm)
        return _sum_slots(buf, name="rs_sum_" + nm)

    big_m = [m_w_in[0], m_w_kv[0], m_w_o[0], m_w_ffn1[0], m_w_ffn2[0]]
    big_v = [v_w_in[0], v_w_kv[0], v_w_o[0], v_w_ffn1[0], v_w_ffn2[0]]
    big_out = {}

    def rs_finish(ks, halves, nm):
        sib = _sibling_exchange(halves, name="rs_share_" + nm)
        for k, g, gs in zip(ks, halves, sib):
            big_out[big_names[k]] = _adamw(big[k], g, gs, big_m[k], big_v[k], core, name="adamw_" + big_names[k])

    dx3, dx3b, dg_final, loss11 = _loss_bwd(x3, g_final2, tgt, name="loss_bwd")
    (dfb,) = _matmul(dx3b, w2_full, name="mm_dact", tb=True, M=S, N=DFF, K=D, outs=[sds((S, DFF), BF16)],
                     epi=lambda acc, fv: (acc * (2.0 * jnp.maximum(fv, 0.0)),), extras=[(f, _tile_spec())])
    (dw2,) = _matmul(act, dx3b, name="mm_dw2", ta=True, M=DFF, N=D, K=S, outs=[sds((DFF, D), F32)])
    rs_w2 = rs_begin(dw2.reshape(NCHIP, dff4, D), "w_ffn2")

    def dw1_out(tm, tn):
        nb = dff4 // tn
        return [pl.BlockSpec((None, tm, tn), lambda j, i, k: (j // nb, i, j % nb))]

    (dw1,) = _matmul(h2, dfb, name="mm_dw1", ta=True, M=D, N=DFF, K=S, outs=[sds((NCHIP, D, dff4), F32)],
                     out_specs=dw1_out, after=[rs_w2[4]])
    rs_w1 = rs_begin(dw1, "w_ffn1")

    def w1_rows(tn, tk):
        kb = dff4 // tk
        return pl.BlockSpec((None, tn, tk), lambda j, i, k: (k // kb, j, k % kb))

    (dh2,) = _matmul(dfb, w14, name="mm_dh2", tb=True, M=S, N=D, K=DFF, b_spec=w1_rows, outs=[sds((S, D), F32)],
                     after=[rs_w1[4]])
    dx2, dx2b, dg_ffn = _rms_bwd(dh2, x2, g_ffn, dx3, name="rms_ffn_bwd")
    (dhn,) = _matmul(dx2b, w_o_full, name="mm_dhn", tb=True, M=S, N=D, K=D, outs=[sds((S, D), F32)])
    (dwo,) = _matmul(hn, dx2b, name="mm_dwo", ta=True, M=D, N=D, K=S, outs=[sds((D, D), F32)], after=[dhn])
    rs_wo = rs_begin(dwo.reshape(NCHIP, D // NCHIP, D), "w_o")
    dproj, dkv, dws, dbs8, dlng, dlnb, dcw8, dgh = _mix_bwd(
        dhn, heads, proj, ycv, kv, ws3, bs_t, ln_v_g, ln_v_b, conv_full, g_head, rs_wo[4], name="mix_bwd")
    (dwin,) = _matmul(h, dproj, name="mm_dwin", ta=True, M=D, N=DIN, K=S, tn=DIN // 2, outs=[sds((D, DIN), F32)])
    rs_win = rs_begin(dwin.reshape(D, NCHIP, din4).transpose(1, 0, 2), "w_in")
    (dwkv,) = _matmul(mem_n, dkv, name="mm_dwkv", ta=True, M=D, N=2 * DM, K=NMEM, outs=[sds((D, 2 * DM), F32)],
                      after=[rs_win[4]])
    rs_wkv = rs_begin(dwkv.reshape(NCHIP, D // NCHIP, 2 * DM), "w_kv")
    half_w2 = rs_end(rs_w2, rs_wkv[4], "w_ffn2")
    half_w1 = rs_end(rs_w1, half_w2, "w_ffn1")
    rs_finish([4, 3], [half_w2, half_w1], "ffn")
    (dh,) = _matmul(dproj, w_in_full, name="mm_dh", tb=True, M=S, N=D, K=DIN, tk=DIN, outs=[sds((S, D), F32)],
                    after=[big_out["w_ffn2"][1], big_out["w_ffn1"][1]])
    dx, dg_mix = _rms_bwd(dh, x2d, g_mix, dx2, name="rms_mix_bwd", want_bf=False)
    (dmem_n,) = _matmul(dkv, w_kv_full, name="mm_dmem", tb=True, M=NMEM, N=D, K=2 * DM, outs=[sds((NMEM, D), F32)],
                        after=[dx])
    (dg_mem,) = _rms_bwd(dmem_n, mem2d, g_mem, None, name="rms_mem_bwd", want_dx=False)

    loss = lax.psum(loss11[0, 0], ("x", "y", "c"))

    half_wo = rs_end(rs_wo, dg_mem, "w_o")
    half_win = rs_end(rs_win, half_wo, "w_in")
    half_wkv = rs_end(rs_wkv, half_win, "w_kv")
    rs_finish([0, 1, 2], [half_win, half_wkv, half_wo], "rest")

    small_names = ["g_mix", "ln_v_g", "ln_v_b", "w_s", "b_s", "conv_w", "g_mem", "g_head", "g_ffn", "g_final"]
    small_part = [dg_mix, dlng, dlnb, dws, dbs8[:, 0, :], dcw8[:3], dg_mem, dgh, dg_ffn, dg_final]
    small_shapes = [(1, D), (1, DS), (1, DS), (NSH, CHUNK, CHUNK), (NSH, CHUNK), (3, DC), (1, D), (1, D), (1, D), (1, D)]
    total = _allreduce_small(_pack(small_part), name="allreduce_small")
    small_g = _unpack(total, small_shapes)
    small_g[5] = lax.dynamic_slice(small_g[5], (0, shard * dcv4), (3, dcv4))
    small_w = [g_mix, ln_v_g, ln_v_b, ws3, bs2, conv_w[0], g_mem, g_head, g_ffn, g_final2]
    small_m = [m_g_mix, m_ln_v_g, m_ln_v_b, m_w_s[0], m_b_s[0], m_conv_w[0], m_g_mem, m_g_head, m_g_ffn,
               m_g_final.reshape(1, D)]
    small_v = [v_g_mix, v_ln_v_g, v_ln_v_b, v_w_s[0], v_b_s[0], v_conv_w[0], v_g_mem, v_g_head, v_g_ffn,
               v_g_final.reshape(1, D)]
    s_delta, s_m, s_v = _adamw_small(small_w, small_g, small_m, small_v, name="adamw_small")
    small_out = {nm: (g, d, mn, vn) for nm, g, d, mn, vn in zip(small_names, small_g, s_delta, s_m, s_v)}

    order = ["g_mix", "w_in", "ln_v_g", "ln_v_b", "w_s", "b_s", "conv_w", "g_mem", "w_kv", "g_head", "w_o",
             "g_ffn", "w_ffn1", "w_ffn2", "g_final"]
    like = dict(g_mix=g_mix, w_in=w_in, ln_v_g=ln_v_g, ln_v_b=ln_v_b, w_s=w_s, b_s=b_s, conv_w=conv_w, g_mem=g_mem,
                w_kv=w_kv, g_head=g_head, w_o=w_o, g_ffn=g_ffn, w_ffn1=w_ffn1, w_ffn2=w_ffn2, g_final=g_final)
    res = {**big_out, **small_out}
    outs = [loss, dx[None]]
    for k in range(4):
        outs += [res[nm][k].reshape(like[nm].shape) for nm in order]
    return tuple(outs)
```

```python
import math

import jax
import jax.numpy as jnp
from jax import lax
from jax.experimental import pallas as pl
from jax.experimental.pallas import tpu as pltpu

F32 = jnp.float32
BF16 = jnp.bfloat16
MESH = pl.DeviceIdType.MESH

D = 2048
S = 2048
HD = 128
NH = D // HD
NMH = 4
NSH = (NH - NMH) // 2
NCH = NH - NMH - NSH
DS = NSH * HD
DC = NCH * HD
DM = NMH * HD
DIN = 2 * DS + 3 * DC + DM
CHUNK = 128
NMEM = 256
DFF = 4 * D
EPS = 1e-6
NCHIP = 4
SCALE = HD ** -0.5

ADAM_LR = 0.001
ADAM_B1 = 0.9
ADAM_B2 = 0.999
ADAM_EPS = 1e-08
ADAM_WD = 0.01
ADAM_STEP = 10

TR_EW = 256
TR_MIX = 256
TM = 512
TN = 1024
TK = 2048
VMEM_MB = 56
HALO = 8


def _pick(n, target, q=128):
    best = None
    for t in range(q, min(n, target) + 1, q):
        if n % t == 0:
            best = t
    return n if best is None else best


def _cp(sem=None, vmem_mb=None, **kw):
    d = dict(kw)
    if sem is not None:
        d["dimension_semantics"] = sem
    if vmem_mb is not None:
        d["vmem_limit_bytes"] = vmem_mb << 20
    return pltpu.CompilerParams(**d)


def _gelu(x):
    z = 0.7978845608028654 * (x + 0.044715 * (x * x * x))
    return 0.5 * x * (1.0 + jnp.tanh(z))


def _gelu_grad(x):
    x2 = x * x
    t = jnp.tanh(0.7978845608028654 * (x + 0.044715 * (x2 * x)))
    return 0.5 * (1.0 + t) + 0.5 * x * (1.0 - t * t) * (0.7978845608028654 * (1.0 + 3.0 * 0.044715 * x2))


def _matmul(a, b, *, name, ta=False, tb=False, M, N, K, tm=None, tn=None, tk=None, outs, epi=None,
            extras=(), b_spec=None, out_specs=None, after=()):
    n_after = len(after)
    tm = _pick(M, TM if tm is None else tm, 8)
    tn = _pick(N, TN if tn is None else tn)
    tk = _pick(K, TK if tk is None else tk)
    nk = K // tk
    grid = (N // tn, M // tm, nk)
    a_spec = (pl.BlockSpec((tk, tm), lambda j, i, k: (k, i)) if ta
              else pl.BlockSpec((tm, tk), lambda j, i, k: (i, k)))
    if b_spec is None:
        b_spec = (pl.BlockSpec((tn, tk), lambda j, i, k: (j, k)) if tb
                  else pl.BlockSpec((tk, tn), lambda j, i, k: (k, j)))
    else:
        b_spec = b_spec(tn, tk)
    if out_specs is None:
        out_specs = [pl.BlockSpec((tm, tn), lambda j, i, k: (i, j)) for _ in outs]
    else:
        out_specs = out_specs(tm, tn)
    dn = (((0 if ta else 1,), (1 if tb else 0,)), ((), ()))
    n_ex, n_out = len(extras), len(outs)

    def body(*refs):
        a_ref, b_ref = refs[0], refs[1]
        ex = refs[2:2 + n_ex]
        first_out = 2 + n_ex + n_after
        o = refs[first_out:first_out + n_out]
        acc = refs[first_out + n_out:]
        part = lax.dot_general(a_ref[...].astype(BF16), b_ref[...].astype(BF16), dn,
                               preferred_element_type=F32)

        def finish(val):
            res = (val,) if epi is None else epi(val, *[e[...] for e in ex])
            for r, o_ref in zip(res, o):
                o_ref[...] = r.astype(o_ref.dtype)

        if nk == 1:
            finish(part)
        else:
            k = pl.program_id(2)

            @pl.when(k == 0)
            def _():
                acc[0][...] = part

            @pl.when(k > 0)
            def _():
                acc[0][...] += part

            @pl.when(k == nk - 1)
            def _():
                finish(acc[0][...])

    return pl.pallas_call(
        body, name=name, grid=grid,
        in_specs=[a_spec, b_spec] + [sp(tm, tn) for _, sp in extras] + [ANY] * n_after,
        out_specs=out_specs, out_shape=outs,
        scratch_shapes=([pltpu.VMEM((tm, tn), F32)] if nk > 1 else []),
        compiler_params=_cp(("parallel", "parallel", "arbitrary"), VMEM_MB),
    )(a, b, *[arr for arr, _ in extras], *after)


def _tile_spec():
    return lambda tm, tn: pl.BlockSpec((tm, tn), lambda j, i, k: (i, j))


def _cast_into_slot(w, slot, *, name):
    R, C = w.shape
    tr = _pick(R, TR_EW, 16)

    def body(s_ref, w_ref, o_ref):
        o_ref[...] = w_ref[...].astype(BF16)

    return pl.pallas_call(
        body, name=name,
        grid_spec=pltpu.PrefetchScalarGridSpec(
            num_scalar_prefetch=1, grid=(R // tr,),
            in_specs=[pl.BlockSpec((tr, C), lambda i, s: (i, 0))],
            out_specs=pl.BlockSpec((None, tr, C), lambda i, s: (s[0], i, 0))),
        out_shape=jax.ShapeDtypeStruct((NCHIP, R, C), BF16),
        compiler_params=_cp(("parallel",), VMEM_MB),
    )(slot, w)


def _rms_fwd(x, g, *, name, after=()):
    R, C = x.shape
    tr = _pick(R, TR_EW, 16)
    n_after = len(after)

    def body(x_ref, g_ref, *rest):
        o_ref = rest[n_after]
        xv = x_ref[...]
        r = lax.rsqrt(jnp.mean(xv * xv, axis=-1, keepdims=True) + EPS)
        o_ref[...] = ((xv * r) * g_ref[...]).astype(BF16)

    return pl.pallas_call(
        body, name=name, grid=(R // tr,),
        in_specs=[pl.BlockSpec((tr, C), lambda i: (i, 0)), pl.BlockSpec((1, C), lambda i: (0, 0))] + [ANY] * n_after,
        out_specs=pl.BlockSpec((tr, C), lambda i: (i, 0)),
        out_shape=jax.ShapeDtypeStruct((R, C), BF16),
        compiler_params=_cp(("parallel",), VMEM_MB),
    )(x, g, *after)


def _rms_bwd(dh, x, g, dres, *, name, want_dx=True, want_bf=True, after=()):
    R, C = x.shape
    tr = _pick(R, TR_EW, 16)
    has_res = dres is not None
    row = pl.BlockSpec((tr, C), lambda i: (i, 0))
    vec = pl.BlockSpec((1, C), lambda i: (0, 0))

    def body(*refs):
        dh_ref, x_ref, g_ref = refs[:3]
        pos = 3
        dres_ref = None
        if has_res:
            dres_ref = refs[pos]
            pos += 1
        outs = refs[pos + len(after):]
        i = pl.program_id(0)
        xv = x_ref[...]
        r = lax.rsqrt(jnp.mean(xv * xv, axis=-1, keepdims=True) + EPS)
        xh = xv * r
        dhv = dh_ref[...]
        dg_ref = outs[-1]
        dgp = jnp.sum(dhv * xh, axis=0, keepdims=True)

        @pl.when(i == 0)
        def _():
            dg_ref[...] = dgp

        @pl.when(i > 0)
        def _():
            dg_ref[...] += dgp

        if want_dx:
            t = dhv * g_ref[...]
            dx = r * (t - xh * jnp.mean(t * xh, axis=-1, keepdims=True))
            if has_res:
                dx = dx + dres_ref[...]
            outs[0][...] = dx
            if want_bf:
                outs[1][...] = dx.astype(BF16)

    in_specs = [row, row, vec] + ([row] if has_res else []) + [ANY] * len(after)
    out_specs, out_shape = [], []
    if want_dx:
        out_specs.append(row)
        out_shape.append(jax.ShapeDtypeStruct((R, C), F32))
        if want_bf:
            out_specs.append(row)
            out_shape.append(jax.ShapeDtypeStruct((R, C), BF16))
    out_specs.append(vec)
    out_shape.append(jax.ShapeDtypeStruct((1, C), F32))
    args = [dh, x, g] + ([dres] if has_res else []) + list(after)
    return pl.pallas_call(
        body, name=name, grid=(R // tr,), in_specs=in_specs, out_specs=out_specs, out_shape=out_shape,
        compiler_params=_cp(("arbitrary",), VMEM_MB),
    )(*args)


def _loss_bwd(x3, g, tgt, *, name):
    R, C = x3.shape
    tr = _pick(R, TR_EW, 16)
    n = R // tr
    row = pl.BlockSpec((tr, C), lambda i: (i, 0))
    vec = pl.BlockSpec((1, C), lambda i: (0, 0))

    def body(x_ref, g_ref, t_ref, dx_ref, dxb_ref, dg_ref, loss_ref, acc_ref):
        i = pl.program_id(0)
        xv = x_ref[...]
        gv = g_ref[...]
        r = lax.rsqrt(jnp.mean(xv * xv, axis=-1, keepdims=True) + EPS)
        xh = xv * r
        e = xh * gv - t_ref[...]
        dy = e * (1.0 / C)
        sq = jnp.sum(e * e, axis=0, keepdims=True)
        dgp = jnp.sum(dy * xh, axis=0, keepdims=True)

        @pl.when(i == 0)
        def _():
            acc_ref[...] = sq
            dg_ref[...] = dgp

        @pl.when(i > 0)
        def _():
            acc_ref[...] += sq
            dg_ref[...] += dgp

        t = dy * gv
        dx = r * (t - xh * jnp.mean(t * xh, axis=-1, keepdims=True))
        dx_ref[...] = dx
        dxb_ref[...] = dx.astype(BF16)

        @pl.when(i == n - 1)
        def _():
            loss_ref[...] = jnp.sum(acc_ref[...], axis=-1, keepdims=True) * (0.5 / C)

    return pl.pallas_call(
        body, name=name, grid=(n,),
        in_specs=[row, vec, row],
        out_specs=[row, row, vec, pl.BlockSpec((1, 1), lambda i: (0, 0))],
        out_shape=[jax.ShapeDtypeStruct((R, C), F32), jax.ShapeDtypeStruct((R, C), BF16),
                   jax.ShapeDtypeStruct((1, C), F32), jax.ShapeDtypeStruct((1, 1), F32)],
        scratch_shapes=[pltpu.VMEM((1, C), F32)],
        compiler_params=_cp(("arbitrary",), VMEM_MB),
    )(x3, g, tgt)


def _offsets():
    u0 = 0
    v0 = DS
    b0 = 2 * DS
    c0 = b0 + DC
    x0 = c0 + DC
    q0 = x0 + DC
    return u0, v0, b0, c0, x0, q0


def _tri_mask(lower):
    r = lax.broadcasted_iota(jnp.int32, (CHUNK, CHUNK), 0)
    c = lax.broadcasted_iota(jnp.int32, (CHUNK, CHUNK), 1)
    return (r >= c) if lower else (c >= r)


def _layer_norm_stats(vg):
    mu = jnp.mean(vg, axis=-1, keepdims=True)
    vc = vg - mu
    rstd = lax.rsqrt(jnp.mean(vc * vc, axis=-1, keepdims=True) + EPS)
    return vc * rstd, rstd


def _softmax_rows(qh, kh):
    s = lax.dot_general(qh, kh, (((1,), (1,)), ((), ())), preferred_element_type=F32)
    m = jnp.max(s, axis=-1, keepdims=True)
    e = jnp.exp(s - m)
    return e / jnp.sum(e, axis=-1, keepdims=True)


def _mix_fwd(proj, kv, w_s, bs_t, ln_g, ln_b, conv_w, g_head, *, name):
    assert DS == DC
    tr = _pick(S, TR_MIX, CHUNK)
    n = S // tr
    nck = tr // CHUNK
    u0, v0, b0, c0, x0, q0 = _offsets()
    hb = tr // HALO

    def body(p_ref, cprev_ref, xprev_ref, kv_ref, ws_ref, bst_ref, lng_ref, lnb_ref, cw_ref, gh_ref,
             heads_ref, hn_ref, ycv_ref, buf_ref):
        i = pl.program_id(0)

        def emit(col, val):
            rs = lax.rsqrt(jnp.mean(val * val, axis=-1, keepdims=True) + EPS)
            heads_ref[:, col:col + HD] = val
            hn_ref[:, col:col + HD] = ((val * rs) * gh_ref[:, col:col + HD]).astype(BF16)

        vhat, _ = _layer_norm_stats(_gelu(p_ref[:, v0:v0 + DS]))
        vnb = (vhat * lng_ref[...] + lnb_ref[...]).astype(BF16)
        low = _tri_mask(True)
        for h in range(NSH):
            wt = jnp.where(low, ws_ref[h], 0.0).astype(BF16)
            bcol = bst_ref[:, h:h + 1]
            parts = []
            for c in range(nck):
                blk = vnb[c * CHUNK:(c + 1) * CHUNK, h * HD:(h + 1) * HD]
                parts.append(jnp.dot(wt, blk, preferred_element_type=F32) + bcol)
            mixed = parts[0] if nck == 1 else jnp.concatenate(parts, axis=0)
            emit(h * HD, _gelu(p_ref[:, u0 + h * HD:u0 + (h + 1) * HD]) * mixed)

        xc = p_ref[:, c0:c0 + DC] * p_ref[:, x0:x0 + DC]
        prev = cprev_ref[...] * xprev_ref[...]
        buf_ref[0:HALO, :] = jnp.where(i > 0, prev, 0.0)
        buf_ref[HALO:HALO + tr, :] = xc
        y = (cw_ref[2:3, :] * xc + cw_ref[1:2, :] * buf_ref[HALO - 1:HALO - 1 + tr, :]
             + cw_ref[0:1, :] * buf_ref[HALO - 2:HALO - 2 + tr, :])
        ycv_ref[...] = y
        cout = p_ref[:, b0:b0 + DC] * y
        for h in range(NCH):
            emit(DS + h * HD, cout[:, h * HD:(h + 1) * HD])

        for h in range(NMH):
            qh = (p_ref[:, q0 + h * HD:q0 + (h + 1) * HD] * SCALE).astype(BF16)
            kh = kv_ref[:, h * HD:(h + 1) * HD].astype(BF16)
            vh = kv_ref[:, DM + h * HD:DM + (h + 1) * HD].astype(BF16)
            p = _softmax_rows(qh, kh)
            emit(DS + DC + h * HD, jnp.dot(p.astype(BF16), vh, preferred_element_type=F32))

    full = lambda shape: pl.BlockSpec(shape, lambda i: (0,) * len(shape))
    halo_c = pl.BlockSpec((HALO, DC), lambda i: (jnp.maximum(i * hb - 1, 0), c0 // DC))
    halo_x = pl.BlockSpec((HALO, DC), lambda i: (jnp.maximum(i * hb - 1, 0), x0 // DC))
    return pl.pallas_call(
        body, name=name, grid=(n,),
        in_specs=[pl.BlockSpec((tr, DIN), lambda i: (i, 0)), halo_c, halo_x,
                  full((NMEM, 2 * DM)), full((NSH, CHUNK, CHUNK)), full((CHUNK, NSH)),
                  full((1, DS)), full((1, DS)), full((3, DC)), full((1, D))],
        out_specs=[pl.BlockSpec((tr, D), lambda i: (i, 0)), pl.BlockSpec((tr, D), lambda i: (i, 0)),
                   pl.BlockSpec((tr, DC), lambda i: (i, 0))],
        out_shape=[jax.ShapeDtypeStruct((S, D), F32), jax.ShapeDtypeStruct((S, D), BF16),
                   jax.ShapeDtypeStruct((S, DC), F32)],
        scratch_shapes=[pltpu.VMEM((tr + HALO, DC), F32)],
        compiler_params=_cp(("parallel",), VMEM_MB),
    )(proj, proj, proj, kv, w_s, bs_t, ln_g, ln_b, conv_w, g_head)


def _mix_bwd(dhn, heads, proj, ycv, kv, w_s, bs_t, ln_g, ln_b, conv_w, g_head, after, *, name):
    assert DS == DC
    tr = _pick(S, TR_MIX, CHUNK)
    n = S // tr
    nck = tr // CHUNK
    u0, v0, b0, c0, x0, q0 = _offsets()
    hb = tr // HALO
    last_hb = S // HALO - 1

    def body(dhn_ref, heads_ref, p_ref, ycv_ref, dhn_nx_ref, heads_nx_ref, b_nx_ref, kv_ref, ws_ref, bst_ref,
             lng_ref, lnb_ref, cw_ref, gh_ref, _after_ref,
             dp_ref, dkv_ref, dws_ref, dbs_ref, dlng_ref, dlnb_ref, dcw_ref, dgh_ref, buf_ref, dvn_ref):
        i = pl.program_id(0)

        @pl.when(i == 0)
        def _():
            dkv_ref[...] = jnp.zeros_like(dkv_ref)
            dws_ref[...] = jnp.zeros_like(dws_ref)
            dbs_ref[...] = jnp.zeros_like(dbs_ref)
            dlng_ref[...] = jnp.zeros_like(dlng_ref)
            dlnb_ref[...] = jnp.zeros_like(dlnb_ref)
            dcw_ref[...] = jnp.zeros_like(dcw_ref)
            dgh_ref[...] = jnp.zeros_like(dgh_ref)

        def head_bwd(a, dn, gh):
            rs = lax.rsqrt(jnp.mean(a * a, axis=-1, keepdims=True) + EPS)
            ah = a * rs
            t = dn * gh
            return rs * (t - ah * jnp.mean(t * ah, axis=-1, keepdims=True)), jnp.sum(dn * ah, axis=0, keepdims=True)

        def head_grad(col):
            da, dg = head_bwd(heads_ref[:, col:col + HD], dhn_ref[:, col:col + HD], gh_ref[:, col:col + HD])
            dgh_ref[:, col:col + HD] += dg
            return da

        v = p_ref[:, v0:v0 + DS]
        vhat, rstd = _layer_norm_stats(_gelu(v))
        vnb = (vhat * lng_ref[...] + lnb_ref[...]).astype(BF16)
        low = _tri_mask(True)
        ones = jnp.ones((HALO, HD), BF16)
        for h in range(NSH):
            w_h = ws_ref[h]
            wt = jnp.where(low, w_h, 0.0).astype(BF16)
            bcol = bst_ref[:, h:h + 1]
            da = head_grad(h * HD)
            u = p_ref[:, u0 + h * HD:u0 + (h + 1) * HD]
            ug = _gelu(u)
            dws = jnp.zeros((CHUNK, CHUNK), F32)
            dbs = jnp.zeros((HALO, CHUNK), F32)
            mixed_parts = []
            for c in range(nck):
                rows = slice(c * CHUNK, (c + 1) * CHUNK)
                blk = vnb[rows, h * HD:(h + 1) * HD]
                mixed_parts.append(jnp.dot(wt, blk, preferred_element_type=F32) + bcol)
                dmb = (da[rows] * ug[rows]).astype(BF16)
                dws = dws + lax.dot_general(dmb, blk, (((1,), (1,)), ((), ())), preferred_element_type=F32)
                dbs = dbs + lax.dot_general(ones, dmb, (((1,), (1,)), ((), ())), preferred_element_type=F32)
                dvn_ref[c * CHUNK:(c + 1) * CHUNK, h * HD:(h + 1) * HD] = lax.dot_general(
                    wt, dmb, (((0,), (0,)), ((), ())), preferred_element_type=F32)
            mixed = mixed_parts[0] if nck == 1 else jnp.concatenate(mixed_parts, axis=0)
            dp_ref[:, u0 + h * HD:u0 + (h + 1) * HD] = ((da * mixed) * _gelu_grad(u)).astype(BF16)
            dws_ref[h] += jnp.where(low, dws, 0.0)
            dbs_ref[h] += dbs
        dvn = dvn_ref[...]
        dlng_ref[...] += jnp.sum(dvn * vhat, axis=0, keepdims=True)
        dlnb_ref[...] += jnp.sum(dvn, axis=0, keepdims=True)
        dvh = dvn * lng_ref[...]
        dvg = rstd * (dvh - jnp.mean(dvh, axis=-1, keepdims=True)
                      - vhat * jnp.mean(dvh * vhat, axis=-1, keepdims=True))
        dp_ref[:, v0:v0 + DS] = (dvg * _gelu_grad(v)).astype(BF16)

        dc = jnp.concatenate([head_grad(DS + h * HD) for h in range(NCH)], axis=1)
        dc_nx = jnp.concatenate(
            [head_bwd(heads_nx_ref[:, h * HD:(h + 1) * HD], dhn_nx_ref[:, h * HD:(h + 1) * HD],
                      gh_ref[:, DS + h * HD:DS + (h + 1) * HD])[0] for h in range(NCH)], axis=1)
        bg = p_ref[:, b0:b0 + DC]
        cg = p_ref[:, c0:c0 + DC]
        xin = p_ref[:, x0:x0 + DC]
        dp_ref[:, b0:b0 + DC] = (dc * ycv_ref[...]).astype(BF16)
        dyv = dc * bg
        buf_ref[0:tr, :] = dyv
        buf_ref[tr:tr + HALO, :] = jnp.where(i < n - 1, dc_nx * b_nx_ref[...], 0.0)
        sh1 = buf_ref[1:1 + tr, :]
        sh0 = buf_ref[2:2 + tr, :]
        dxc = cw_ref[2:3, :] * dyv + cw_ref[1:2, :] * sh1 + cw_ref[0:1, :] * sh0
        xc = cg * xin
        dp_ref[:, c0:c0 + DC] = (dxc * xin).astype(BF16)
        dp_ref[:, x0:x0 + DC] = (dxc * cg).astype(BF16)
        dcw_ref[0:1, :] += jnp.sum(sh0 * xc, axis=0, keepdims=True)
        dcw_ref[1:2, :] += jnp.sum(sh1 * xc, axis=0, keepdims=True)
        dcw_ref[2:3, :] += jnp.sum(dyv * xc, axis=0, keepdims=True)

        for h in range(NMH):
            do = head_grad(DS + DC + h * HD).astype(BF16)
            qh = (p_ref[:, q0 + h * HD:q0 + (h + 1) * HD] * SCALE).astype(BF16)
            kh = kv_ref[:, h * HD:(h + 1) * HD].astype(BF16)
            vh = kv_ref[:, DM + h * HD:DM + (h + 1) * HD].astype(BF16)
            p = _softmax_rows(qh, kh)
            dpr = lax.dot_general(do, vh, (((1,), (1,)), ((), ())), preferred_element_type=F32)
            ds = (p * (dpr - jnp.sum(dpr * p, axis=-1, keepdims=True))).astype(BF16)
            dp_ref[:, q0 + h * HD:q0 + (h + 1) * HD] = (
                jnp.dot(ds, kh, preferred_element_type=F32) * SCALE).astype(BF16)
            dkv_ref[:, h * HD:(h + 1) * HD] += lax.dot_general(
                ds, qh, (((0,), (0,)), ((), ())), preferred_element_type=F32)
            dkv_ref[:, DM + h * HD:DM + (h + 1) * HD] += lax.dot_general(
                p.astype(BF16), do, (((0,), (0,)), ((), ())), preferred_element_type=F32)

    full = lambda shape: pl.BlockSpec(shape, lambda i: (0,) * len(shape))
    row = lambda c: pl.BlockSpec((tr, c), lambda i: (i, 0))
    nxt = lambda col: pl.BlockSpec((HALO, DC), lambda i: (jnp.minimum((i + 1) * hb, last_hb), col))
    return pl.pallas_call(
        body, name=name, grid=(n,),
        in_specs=[row(D), row(D), row(DIN), row(DC), nxt(DS // DC), nxt(DS // DC), nxt(b0 // DC),
                  full((NMEM, 2 * DM)), full((NSH, CHUNK, CHUNK)), full((CHUNK, NSH)),
                  full((1, DS)), full((1, DS)), full((3, DC)), full((1, D)), ANY],
        out_specs=[row(DIN), full((NMEM, 2 * DM)), full((NSH, CHUNK, CHUNK)), full((NSH, HALO, CHUNK)),
                   full((1, DS)), full((1, DS)), full((HALO, DC)), full((1, D))],
        out_shape=[jax.ShapeDtypeStruct((S, DIN), BF16), jax.ShapeDtypeStruct((NMEM, 2 * DM), F32),
                   jax.ShapeDtypeStruct((NSH, CHUNK, CHUNK), F32), jax.ShapeDtypeStruct((NSH, HALO, CHUNK), F32),
                   jax.ShapeDtypeStruct((1, DS), F32), jax.ShapeDtypeStruct((1, DS), F32),
                   jax.ShapeDtypeStruct((HALO, DC), F32), jax.ShapeDtypeStruct((1, D), F32)],
        scratch_shapes=[pltpu.VMEM((tr + HALO, DC), F32), pltpu.VMEM((tr, DS), F32)],
        compiler_params=_cp(("arbitrary",), VMEM_MB),
    )(dhn, heads, proj, ycv, dhn, heads, proj, kv, w_s, bs_t, ln_g, ln_b, conv_w, g_head, after)


def _place():
    x, y, c = lax.axis_index("x"), lax.axis_index("y"), lax.axis_index("c")
    chips = [(1 - x, y), (x, 1 - y), (1 - x, 1 - y)]
    return x, y, c, chips


ANY = pl.BlockSpec(memory_space=pl.ANY)


HBM = pl.BlockSpec(memory_space=pltpu.HBM)
SEM = pl.BlockSpec(memory_space=pltpu.SEMAPHORE)
EFFECT = pltpu.SideEffectType.DATAFLOW_SIDE_EFFECTING
N_PEER_CHIPS = 3


def _in_hbm(a):
    return pltpu.with_memory_space_constraint(a, pltpu.HBM)


def _allgather_start(bufs, after, *, name):
    nw = len(bufs)

    def body(*refs):
        ins, send, recv = refs[:nw], refs[nw + 1:2 * nw + 1], refs[2 * nw + 1:3 * nw + 1]
        token = refs[4 * nw + 1]
        x, y, c, chips = _place()
        s = 2 * x + y
        for w in range(nw):
            hr = bufs[w].shape[1] // 2
            rows = ins[w].at[s, pl.ds(c * hr, hr)]
            for cx, cy in chips:
                pltpu.make_async_remote_copy(src_ref=rows, dst_ref=rows, send_sem=send[w], recv_sem=recv[w],
                                             device_id=(cx, cy, c), device_id_type=MESH).start()
        token[...] = jnp.zeros_like(token)

    res = pl.pallas_call(
        body, name=name,
        in_specs=[HBM] * nw + [ANY],
        out_specs=[SEM] * (2 * nw) + [HBM] * nw + [pl.BlockSpec(memory_space=pltpu.VMEM)],
        out_shape=[pltpu.SemaphoreType.DMA(())] * (2 * nw) + [pltpu.HBM(a.shape, a.dtype) for a in bufs]
        + [jax.ShapeDtypeStruct((8, 128), F32)],
        input_output_aliases={w: 2 * nw + w for w in range(nw)},
        compiler_params=pltpu.CompilerParams(has_side_effects=EFFECT),
    )(*[_in_hbm(a) for a in bufs], after)
    return res[:nw], res[nw:2 * nw], res[2 * nw:3 * nw], res[3 * nw]


def _scatter_start(parts, bufs, *, name):
    nw = len(parts)

    def body(*refs):
        src, dst = refs[:nw], refs[nw:2 * nw]
        send, recv = refs[2 * nw:3 * nw], refs[3 * nw:4 * nw]
        token = refs[6 * nw]
        x, y, c, chips = _place()
        s = 2 * x + y
        for w in range(nw):
            for cx, cy in chips:
                pltpu.make_async_remote_copy(src_ref=src[w].at[2 * cx + cy], dst_ref=dst[w].at[s], send_sem=send[w],
                                             recv_sem=recv[w], device_id=(cx, cy, c), device_id_type=MESH).start()
        token[...] = jnp.zeros_like(token)

    res = pl.pallas_call(
        body, name=name,
        in_specs=[HBM] * (2 * nw),
        out_specs=[SEM] * (2 * nw) + [HBM] * (2 * nw) + [pl.BlockSpec(memory_space=pltpu.VMEM)],
        out_shape=[pltpu.SemaphoreType.DMA(())] * (2 * nw) + [pltpu.HBM(a.shape, a.dtype) for a in parts + bufs]
        + [jax.ShapeDtypeStruct((8, 128), F32)],
        input_output_aliases={k: 2 * nw + k for k in range(2 * nw)},
        compiler_params=pltpu.CompilerParams(has_side_effects=EFFECT),
    )(*[_in_hbm(a) for a in parts + bufs])
    return res[:nw], res[nw:2 * nw], res[2 * nw:3 * nw], res[3 * nw:4 * nw], res[4 * nw]


def _sibling_start(srcs, *, name):
    nw = len(srcs)
    lands = [lax.empty((a.shape[0], a.shape[1] // 2, a.shape[2]), a.dtype) for a in srcs]

    def body(*refs):
        src, land = refs[:nw], refs[nw:2 * nw]
        send, recv = refs[2 * nw:3 * nw], refs[3 * nw:4 * nw]
        token = refs[6 * nw]
        x, y, c, _ = _place()
        for w in range(nw):
            hr = srcs[w].shape[1] // 2
            rows = src[w].at[:, pl.ds((1 - c) * hr, hr)]
            pltpu.make_async_remote_copy(src_ref=rows, dst_ref=land[w], send_sem=send[w], recv_sem=recv[w],
                                         device_id=(x, y, 1 - c), device_id_type=MESH).start()
        token[...] = jnp.zeros_like(token)

    res = pl.pallas_call(
        body, name=name,
        in_specs=[HBM] * (2 * nw),
        out_specs=[SEM] * (2 * nw) + [HBM] * (2 * nw) + [pl.BlockSpec(memory_space=pltpu.VMEM)],
        out_shape=[pltpu.SemaphoreType.DMA(())] * (2 * nw) + [pltpu.HBM(a.shape, a.dtype) for a in srcs + lands]
        + [jax.ShapeDtypeStruct((8, 128), F32)],
        input_output_aliases={k: 2 * nw + k for k in range(2 * nw)},
        compiler_params=pltpu.CompilerParams(has_side_effects=EFFECT),
    )(*[_in_hbm(a) for a in srcs + lands])
    return res[:nw], res[nw:2 * nw], res[2 * nw:3 * nw], res[3 * nw:4 * nw], res[4 * nw]


def _transfer_wait(sends, recvs, thru, sizes, after, *, name):
    n = len(sends)
    flat = [a for group in thru for a in group]

    def body(*refs):
        bufs = refs[:len(flat)]
        send = refs[len(flat):len(flat) + n]
        recv = refs[len(flat) + n:len(flat) + 2 * n]
        token = refs[2 * len(flat) + 2 * n + 1]
        token[...] = jnp.zeros_like(token)
        x, y, c, _ = _place()
        pos = 0
        for k in range(n):
            slots, rows = sizes[k]
            region = bufs[pos].at[pl.ds(0, slots), pl.ds(0, rows)]
            pos += len(thru[k])
            cp = pltpu.make_async_remote_copy(src_ref=region, dst_ref=region, send_sem=send[k], recv_sem=recv[k],
                                              device_id=(x, y, 1 - c), device_id_type=MESH)
            cp.wait_send()
            cp.wait_recv()

    res = pl.pallas_call(
        body, name=name,
        in_specs=[HBM] * len(flat) + [SEM] * (2 * n) + [pl.BlockSpec(memory_space=pl.ANY)],
        out_specs=[HBM] * len(flat) + [pl.BlockSpec(memory_space=pltpu.VMEM)],
        out_shape=[pltpu.HBM(a.shape, a.dtype) for a in flat] + [jax.ShapeDtypeStruct((8, 128), F32)],
        input_output_aliases={k: k for k in range(len(flat))},
        compiler_params=pltpu.CompilerParams(has_side_effects=EFFECT),
    )(*flat, *sends, *recvs, after)
    out, pos = [], 0
    for group in thru:
        out.append(res[pos:pos + len(group)])
        pos += len(group)
    return out, res[len(flat)]


def _forward_to_sibling(bufs, after, *, name):
    nw = len(bufs)

    def body(*refs):
        outs = refs[nw + 1:2 * nw + 1]
        send, recv = refs[2 * nw + 1:]
        x, y, c, chips = _place()
        me, sibling = (x, y, c), (x, y, 1 - c)

        def d2d(w, j, which, to):
            cx, cy = chips[j]
            hr = bufs[w].shape[1] // 2
            rows = outs[w].at[2 * cx + cy, pl.ds(which * hr, hr)]
            return pltpu.make_async_remote_copy(
                src_ref=rows, dst_ref=rows, send_sem=send.at[N_PEER_CHIPS * w + j],
                recv_sem=recv.at[N_PEER_CHIPS * w + j], device_id=to, device_id_type=MESH)

        passed = [d2d(w, j, c, sibling) for w in range(nw) for j in range(N_PEER_CHIPS)]
        for cp in passed:
            cp.start()
        for w in range(nw):
            for j in range(N_PEER_CHIPS):
                d2d(w, j, 1 - c, me).wait_recv()
        for cp in passed:
            cp.wait_send()

    return pl.pallas_call(
        body, name=name,
        in_specs=[ANY] * (nw + 1), out_specs=[ANY] * nw,
        out_shape=[jax.ShapeDtypeStruct(a.shape, a.dtype) for a in bufs],
        input_output_aliases={w: w for w in range(nw)},
        scratch_shapes=[pltpu.SemaphoreType.DMA((N_PEER_CHIPS * nw,)), pltpu.SemaphoreType.DMA((N_PEER_CHIPS * nw,))],
    )(*bufs, after)


def _sibling_exchange(arrs, *, name):
    nw = len(arrs)

    def body(*refs):
        ins, outs = refs[:nw], refs[nw:2 * nw]
        send, recv = refs[2 * nw:]
        x, y, c, _ = _place()
        cps = [pltpu.make_async_remote_copy(src_ref=ins[w], dst_ref=outs[w], send_sem=send.at[w], recv_sem=recv.at[w],
                                            device_id=(x, y, 1 - c), device_id_type=MESH) for w in range(nw)]
        for cp in cps:
            cp.start()
        for cp in cps:
            cp.wait()

    return pl.pallas_call(
        body, name=name, in_specs=[ANY] * nw, out_specs=[ANY] * nw,
        out_shape=[jax.ShapeDtypeStruct(a.shape, a.dtype) for a in arrs],
        scratch_shapes=[pltpu.SemaphoreType.DMA((nw,)), pltpu.SemaphoreType.DMA((nw,))],
    )(*arrs)


def _allreduce_small(p, *, name):
    R = p.shape[0]
    hr = R // 2

    def body(p_ref, out_ref, sib_ref, sum_ref, gat_ref, tot_ref, send, recv):
        x, y, c, chips = _place()
        s = 2 * x + y
        sibling = (x, y, 1 - c)
        rows = pl.ds(pl.multiple_of(c * hr, 8), hr)
        swap = pltpu.make_async_remote_copy(src_ref=p_ref, dst_ref=sib_ref, send_sem=send.at[0], recv_sem=recv.at[0],
                                            device_id=sibling, device_id_type=MESH)
        swap.start()
        swap.wait()
        sum_ref[...] = p_ref[...] + sib_ref[...]
        gat_ref[s] = sum_ref[rows, :]
        cps = [pltpu.make_async_remote_copy(src_ref=sum_ref.at[rows], dst_ref=gat_ref.at[s], send_sem=send.at[1 + j],
                                            recv_sem=recv.at[1 + j], device_id=(cx, cy, c), device_id_type=MESH)
               for j, (cx, cy) in enumerate(chips)]
        for cp in cps:
            cp.start()
        for cp in cps:
            cp.wait()
        tot_ref[...] = ((gat_ref[0] + gat_ref[1]) + gat_ref[2]) + gat_ref[3]
        out_ref[rows, :] = tot_ref[...]
        share = pltpu.make_async_remote_copy(src_ref=tot_ref, dst_ref=out_ref.at[rows], send_sem=send.at[4],
                                             recv_sem=recv.at[4], device_id=sibling, device_id_type=MESH)
        share.start()
        share.wait_send()
        other = out_ref.at[pl.ds(pl.multiple_of((1 - c) * hr, 8), hr)]
        pltpu.make_async_remote_copy(src_ref=other, dst_ref=other, send_sem=send.at[4], recv_sem=recv.at[4],
                                     device_id=(x, y, c), device_id_type=MESH).wait_recv()

    vmem = pl.BlockSpec(memory_space=pltpu.VMEM)
    return pl.pallas_call(
        body, name=name, in_specs=[vmem], out_specs=vmem,
        out_shape=jax.ShapeDtypeStruct((R, 128), F32),
        scratch_shapes=[pltpu.VMEM((R, 128), F32), pltpu.VMEM((R, 128), F32), pltpu.VMEM((NCHIP, hr, 128), F32),
                        pltpu.VMEM((hr, 128), F32), pltpu.SemaphoreType.DMA((5,)), pltpu.SemaphoreType.DMA((5,))],
    )(p)


def _select_half_bf16(g, half, add, slot, *, name):
    _, R, C = g.shape
    hr = R // 2
    tr = _pick(hr, TR_EW, 16)
    nb = hr // tr
    sel = jnp.concatenate([jnp.reshape(half, (1,)).astype(jnp.int32), slot])

    def body(s_ref, g_ref, a_ref, o_ref, own_ref):
        val = (g_ref[...].astype(F32) + a_ref[...].astype(F32)).astype(BF16)
        o_ref[...] = val

        @pl.when(pl.program_id(1) == s_ref[1])
        def _():
            own_ref[...] = val

    g_spec = pl.BlockSpec((None, tr, C), lambda i, j, s: (j, s[0] * nb + i, 0))
    o_spec = pl.BlockSpec((None, tr, C), lambda i, j, s: (j, i, 0))
    own_spec = pl.BlockSpec((None, tr, C), lambda i, j, s: (s[1], i, 0))
    shape = jax.ShapeDtypeStruct((NCHIP, hr, C), BF16)
    return pl.pallas_call(
        body, name=name,
        grid_spec=pltpu.PrefetchScalarGridSpec(
            num_scalar_prefetch=1, grid=(nb, NCHIP), in_specs=[g_spec, o_spec], out_specs=[o_spec, own_spec]),
        out_shape=[shape, shape],
        compiler_params=_cp(("parallel", "arbitrary"), VMEM_MB),
    )(sel, g, add)


def _sum_slots(r, *, name):
    _, R, C = r.shape
    tr = _pick(R, TR_EW, 16)

    def body(r_ref, o_ref):
        acc = r_ref[0].astype(F32) + r_ref[1].astype(F32)
        for j in range(2, NCHIP):
            acc = acc + r_ref[j].astype(F32)
        o_ref[...] = acc

    return pl.pallas_call(
        body, name=name, grid=(R // tr,),
        in_specs=[pl.BlockSpec((NCHIP, tr, C), lambda i: (0, i, 0))],
        out_specs=pl.BlockSpec((tr, C), lambda i: (i, 0)),
        out_shape=jax.ShapeDtypeStruct((R, C), F32),
        compiler_params=_cp(("parallel",), VMEM_MB),
    )(r)


def _adamw_math(w, g, m, v):
    m = ADAM_B1 * m + (1.0 - ADAM_B1) * g
    v = ADAM_B2 * v + (1.0 - ADAM_B2) * (g * g)
    m_hat = m / (1.0 - ADAM_B1 ** ADAM_STEP)
    v_hat = v / (1.0 - ADAM_B2 ** ADAM_STEP)
    delta = -ADAM_LR * (m_hat / (jnp.sqrt(v_hat) + ADAM_EPS) + ADAM_WD * w)
    return delta, m, v


def _adamw(w, g_mine, g_sib, m, v, core, *, name):
    R, C = w.shape
    hr = R // 2
    tr = _pick(hr, TR_EW, 8)
    nb = hr // tr
    row = pl.BlockSpec((tr, C), lambda hh, i, c: (hh * nb + i, 0))
    half = pl.BlockSpec((tr, C), lambda hh, i, c: (i, 0))

    def body(c_ref, w_ref, gm_ref, gs_ref, m_ref, v_ref, go_ref, d_ref, mo_ref, vo_ref):
        gv = jnp.where(pl.program_id(0) == c_ref[0], gm_ref[...], gs_ref[...])
        d, mn, vn = _adamw_math(w_ref[...], gv, m_ref[...], v_ref[...])
        go_ref[...] = gv
        d_ref[...] = d
        mo_ref[...] = mn
        vo_ref[...] = vn

    return pl.pallas_call(
        body, name=name,
        grid_spec=pltpu.PrefetchScalarGridSpec(
            num_scalar_prefetch=1, grid=(2, nb),
            in_specs=[row, half, half, row, row], out_specs=[row] * 4),
        out_shape=[jax.ShapeDtypeStruct((R, C), F32)] * 4,
        compiler_params=_cp(("parallel", "parallel"), VMEM_MB),
    )(core, w, g_mine, g_sib, m, v)


def _adamw_small(ws, gs, ms, vs, *, name):
    n = len(ws)

    def body(*refs):
        w_r, g_r, m_r, v_r = refs[:n], refs[n:2 * n], refs[2 * n:3 * n], refs[3 * n:4 * n]
        d_r, mo_r, vo_r = refs[4 * n:5 * n], refs[5 * n:6 * n], refs[6 * n:7 * n]
        for k in range(n):
            d, mn, vn = _adamw_math(w_r[k][...], g_r[k][...], m_r[k][...], v_r[k][...])
            d_r[k][...] = d
            mo_r[k][...] = mn
            vo_r[k][...] = vn

    shapes = [jax.ShapeDtypeStruct(w.shape, F32) for w in ws]
    res = pl.pallas_call(body, name=name, out_shape=shapes * 3)(*ws, *gs, *ms, *vs)
    return res[:n], res[n:2 * n], res[2 * n:]


_PACK_ROWS = 8


def _pack(parts):
    rows = []
    for a in parts:
        flat = a.reshape(-1)
        n = -(-flat.shape[0] // (_PACK_ROWS * 128)) * (_PACK_ROWS * 128)
        rows.append(jnp.pad(flat, (0, n - flat.shape[0])).reshape(-1, 128))
    total = sum(r.shape[0] for r in rows)
    if total % 16:
        rows.append(jnp.zeros((16 - total % 16, 128), F32))
    return jnp.concatenate(rows, axis=0)


def _unpack(p, shapes):
    out, r = [], 0
    for shp in shapes:
        n = math.prod(shp)
        nr = -(-n // (_PACK_ROWS * 128)) * _PACK_ROWS
        out.append(p[r:r + nr].reshape(-1)[:n].reshape(shp))
        r += nr
    return out


def kernel(x, mem, g_mix, w_in, ln_v_g, ln_v_b, w_s, b_s, conv_w, g_mem, w_kv, g_head, w_o, g_ffn, w_ffn1, w_ffn2, g_final, loss_target, m_g_mix, m_w_in, m_ln_v_g, m_ln_v_b, m_w_s, m_b_s, m_conv_w, m_g_mem, m_w_kv, m_g_head, m_w_o, m_g_ffn, m_w_ffn1, m_w_ffn2, m_g_final, v_g_mix, v_w_in, v_ln_v_g, v_ln_v_b, v_w_s, v_b_s, v_conv_w, v_g_mem, v_w_kv, v_g_head, v_w_o, v_g_ffn, v_w_ffn1, v_w_ffn2, v_g_final):
    sds = jax.ShapeDtypeStruct
    xi, yi = lax.axis_index("x"), lax.axis_index("y")
    shard = 2 * xi + yi
    x2d, mem2d, tgt = x[0], mem[0], loss_target[0]
    ws3, bs2 = w_s[0], b_s[0]
    g_final2 = g_final.reshape(1, D)
    dff4 = DFF // NCHIP
    din4 = DIN // NCHIP
    dcv4 = DC // NCHIP

    big = [w_in[0].T, w_kv[0], w_o[0], w_ffn1[0], w_ffn2[0]]
    big_names = ["w_in", "w_kv", "w_o", "w_ffn1", "w_ffn2"]
    slot = jnp.reshape(shard, (1,)).astype(jnp.int32)
    core = jnp.reshape(lax.axis_index("c"), (1,)).astype(jnp.int32)
    conv_pad = jnp.pad(conv_w[0], ((0, 16 - 3), (0, 256 - dcv4)))
    conv_slots = lax.dynamic_update_slice(jnp.zeros((NCHIP, 16, 256), F32), conv_pad[None], (shard, 0, 0))

    def gather_start(bufs, after, nm):
        return _allgather_start(bufs, after, name="ag_start_" + nm)

    def gather_wait(state, idx, after, nm):
        send, recv, bufs, _ = state
        got, token = _transfer_wait([send[k] for k in idx], [recv[k] for k in idx], [[bufs[k]] for k in idx],
                                    [(N_PEER_CHIPS, bufs[k].shape[1] // 2) for k in idx], after, name="ag_wait_" + nm)
        return [g[0] for g in got], token

    cast = lambda k: _cast_into_slot(big[k], slot, name="cast_" + big_names[k])
    ag_in = gather_start([cast(0), conv_slots], slot, "in")
    rest_b = [cast(1), cast(2), cast(3)]
    w2_b = cast(4)
    bs_t = bs2.T

    h = _rms_fwd(x2d, g_mix, name="rms_mix", after=[ag_in[3]])
    mem_n = _rms_fwd(mem2d, g_mem, name="rms_mem", after=[h])
    got_in, tok = gather_wait(ag_in, [0, 1], mem_n, "in")
    ag_rest = gather_start(rest_b, tok, "kvo1")
    win4, conv4 = _forward_to_sibling(got_in, ag_rest[3], name="ag_fwd_in")
    w_in_t = win4.reshape(DIN, D)
    conv_full = conv4[:, :3, :dcv4].transpose(1, 0, 2).reshape(3, DC)
    (proj,) = _matmul(h, w_in_t, name="mm_proj", tb=True, M=S, N=DIN, K=D, tn=DIN // 2, outs=[sds((S, DIN), F32)])
    got_kvo, tok = gather_wait(ag_rest, [0, 1], proj, "kvo")
    wkv4, wo4 = _forward_to_sibling(got_kvo, tok, name="ag_fwd_kvo")
    w_kv_full = wkv4.reshape(D, 2 * DM)
    w_o_full = wo4.reshape(D, D)
    (kv,) = _matmul(mem_n, w_kv_full, name="mm_kv", M=NMEM, N=2 * DM, K=D, outs=[sds((NMEM, 2 * DM), F32)])
    heads, hn, ycv = _mix_fwd(proj, kv, ws3, bs_t, ln_v_g, ln_v_b, conv_full, g_head, name="mix_fwd")
    (x2,) = _matmul(hn, w_o_full, name="mm_wo", M=S, N=D, K=D, outs=[sds((S, D), F32)],
                    epi=lambda acc, res: (acc + res,), extras=[(x2d, _tile_spec())])
    h2 = _rms_fwd(x2, g_ffn, name="rms_ffn")
    got_w1, tok = gather_wait(ag_rest, [2], h2, "ffn1")
    ag_w2 = gather_start([w2_b], tok, "ffn2")
    (w14,) = _forward_to_sibling(got_w1, ag_w2[3], name="ag_fwd_ffn1")

    def w1_cols(tn, tk):
        nb = dff4 // tn
        return pl.BlockSpec((None, tk, tn), lambda j, i, k: (j // nb, k, j % nb))

    f, act = _matmul(h2, w14, name="mm_ffn1", M=S, N=DFF, K=D, b_spec=w1_cols,
                     outs=[sds((S, DFF), F32), sds((S, DFF), BF16)],
                     epi=lambda acc: (acc, jnp.square(jnp.maximum(acc, 0.0))))
    got_w2, tok = gather_wait(ag_w2, [0], act, "ffn2")
    (w24,) = _forward_to_sibling(got_w2, tok, name="ag_fwd_ffn2")
    w2_full = w24.reshape(DFF, D)
    (x3,) = _matmul(act, w2_full, name="mm_ffn2", M=S, N=D, K=DFF, outs=[sds((S, D), F32)],
                    epi=lambda acc, res: (acc + res,), extras=[(x2, _tile_spec())])

    ci = lax.axis_index("c")

    def rs_sibling(g4, nm):
        return _sibling_start([g4], name="rs_sib_" + nm)

    def rs_chips(state, after, nm):
        send, recv, g4, land, _ = state
        (((land_, g4_),), _) = _transfer_wait(send, recv, [[land[0], g4[0]]], [(NCHIP, land[0].shape[1])], after,
                                             name="rs_sibwait_" + nm)
        part, buf = _select_half_bf16(g4_, ci, land_, slot, name="rs_add_" + nm)
        return _scatter_start([part], [buf], name="rs_start_" + nm)

    def rs_end(state, after, nm):
        send, recv, parts, bufs, _ = state
        (((buf, _),), _) = _transfer_wait(send, recv, [[bufs[0], parts[0]]], [(N_PEER_CHIPS, bufs[0].shape[1])], after,
                                          name="rs_wait_" + nm)
        return _sum_slots(buf, name="rs_sum_" + nm)

    big_m = [m_w_in[0].T, m_w_kv[0], m_w_o[0], m_w_ffn1[0], m_w_ffn2[0]]
    big_v = [v_w_in[0].T, v_w_kv[0], v_w_o[0], v_w_ffn1[0], v_w_ffn2[0]]
    big_out = {}

    def rs_finish(ks, halves, nm):
        sib = _sibling_exchange(halves, name="rs_share_" + nm)
        for k, g, gs in zip(ks, halves, sib):
            big_out[big_names[k]] = _adamw(big[k], g, gs, big_m[k], big_v[k], core, name="adamw_" + big_names[k])

    dx3, dx3b, dg_final, loss11 = _loss_bwd(x3, g_final2, tgt, name="loss_bwd")
    (dw2,) = _matmul(act, dx3b, name="mm_dw2", ta=True, M=DFF, N=D, K=S, outs=[sds((DFF, D), BF16)])
    sib_w2 = rs_sibling(dw2.reshape(NCHIP, dff4, D), "w_ffn2")
    (dfb,) = _matmul(dx3b, w2_full, name="mm_dact", tb=True, M=S, N=DFF, K=D, outs=[sds((S, DFF), BF16)],
                     epi=lambda acc, fv: (acc * (2.0 * jnp.maximum(fv, 0.0)),), extras=[(f, _tile_spec())],
                     after=[sib_w2[4]])
    rs_w2 = rs_chips(sib_w2, dfb, "w_ffn2")

    def dw1_out(tm, tn):
        nb = dff4 // tn
        return [pl.BlockSpec((None, tm, tn), lambda j, i, k: (j // nb, i, j % nb))]

    (dw1,) = _matmul(h2, dfb, name="mm_dw1", ta=True, M=D, N=DFF, K=S, outs=[sds((NCHIP, D, dff4), BF16)],
                     out_specs=dw1_out, after=[rs_w2[4]])
    sib_w1 = rs_sibling(dw1, "w_ffn1")

    def w1_rows(tn, tk):
        kb = dff4 // tk
        return pl.BlockSpec((None, tn, tk), lambda j, i, k: (k // kb, j, k % kb))

    (dh2,) = _matmul(dfb, w14, name="mm_dh2", tb=True, M=S, N=D, K=DFF, b_spec=w1_rows, outs=[sds((S, D), F32)],
                     after=[sib_w1[4]])
    rs_w1 = rs_chips(sib_w1, dh2, "w_ffn1")
    dx2, dx2b, dg_ffn = _rms_bwd(dh2, x2, g_ffn, dx3, name="rms_ffn_bwd", after=[rs_w1[4]])
    (dwo,) = _matmul(hn, dx2b, name="mm_dwo", ta=True, M=D, N=D, K=S, outs=[sds((D, D), BF16)])
    sib_wo = rs_sibling(dwo.reshape(NCHIP, D // NCHIP, D), "w_o")
    (dhn,) = _matmul(dx2b, w_o_full, name="mm_dhn", tb=True, M=S, N=D, K=D, outs=[sds((S, D), F32)],
                     after=[sib_wo[4]])
    rs_wo = rs_chips(sib_wo, dhn, "w_o")
    dproj, dkv, dws, dbs8, dlng, dlnb, dcw8, dgh = _mix_bwd(
        dhn, heads, proj, ycv, kv, ws3, bs_t, ln_v_g, ln_v_b, conv_full, g_head, rs_wo[4], name="mix_bwd")
    (dwin_t,) = _matmul(dproj, h, name="mm_dwin", ta=True, M=DIN, N=D, K=S, tm=DIN // 2, outs=[sds((DIN, D), BF16)])
    sib_win = rs_sibling(dwin_t.reshape(NCHIP, din4, D), "w_in")
    (dwkv,) = _matmul(mem_n, dkv, name="mm_dwkv", ta=True, M=D, N=2 * DM, K=NMEM, outs=[sds((D, 2 * DM), BF16)],
                      after=[sib_win[4]])
    sib_wkv = rs_sibling(dwkv.reshape(NCHIP, D // NCHIP, 2 * DM), "w_kv")
    (dh,) = _matmul(dproj, w_in_t, name="mm_dh", M=S, N=D, K=DIN, tk=DIN, outs=[sds((S, D), F32)],
                    after=[sib_wkv[4]])
    rs_win = rs_chips(sib_win, dh, "w_in")
    rs_wkv = rs_chips(sib_wkv, rs_win[4], "w_kv")
    dx, dg_mix = _rms_bwd(dh, x2d, g_mix, dx2, name="rms_mix_bwd", want_bf=False, after=[rs_wkv[4]])
    (dmem_n,) = _matmul(dkv, w_kv_full, name="mm_dmem", tb=True, M=NMEM, N=D, K=2 * DM, outs=[sds((NMEM, D), F32)],
                        after=[dx])
    (dg_mem,) = _rms_bwd(dmem_n, mem2d, g_mem, None, name="rms_mem_bwd", want_dx=False)
    half_w2 = rs_end(rs_w2, dg_mem, "w_ffn2")
    half_w1 = rs_end(rs_w1, half_w2, "w_ffn1")
    rs_finish([4, 3], [half_w2, half_w1], "ffn")

    loss = lax.psum(loss11[0, 0], ("x", "y", "c"))

    half_wo = rs_end(rs_wo, big_out["w_ffn1"][1], "w_o")
    half_win = rs_end(rs_win, half_wo, "w_in")
    half_wkv = rs_end(rs_wkv, half_win, "w_kv")
    rs_finish([0, 1, 2], [half_win, half_wkv, half_wo], "rest")

    small_names = ["g_mix", "ln_v_g", "ln_v_b", "w_s", "b_s", "conv_w", "g_mem", "g_head", "g_ffn", "g_final"]
    small_part = [dg_mix, dlng, dlnb, dws, dbs8[:, 0, :], dcw8[:3], dg_mem, dgh, dg_ffn, dg_final]
    small_shapes = [(1, D), (1, DS), (1, DS), (NSH, CHUNK, CHUNK), (NSH, CHUNK), (3, DC), (1, D), (1, D), (1, D), (1, D)]
    total = _allreduce_small(_pack(small_part), name="allreduce_small")
    small_g = _unpack(total, small_shapes)
    small_g[5] = lax.dynamic_slice(small_g[5], (0, shard * dcv4), (3, dcv4))
    small_w = [g_mix, ln_v_g, ln_v_b, ws3, bs2, conv_w[0], g_mem, g_head, g_ffn, g_final2]
    small_m = [m_g_mix, m_ln_v_g, m_ln_v_b, m_w_s[0], m_b_s[0], m_conv_w[0], m_g_mem, m_g_head, m_g_ffn,
               m_g_final.reshape(1, D)]
    small_v = [v_g_mix, v_ln_v_g, v_ln_v_b, v_w_s[0], v_b_s[0], v_conv_w[0], v_g_mem, v_g_head, v_g_ffn,
               v_g_final.reshape(1, D)]
    s_delta, s_m, s_v = _adamw_small(small_w, small_g, small_m, small_v, name="adamw_small")
    small_out = {nm: (g, d, mn, vn) for nm, g, d, mn, vn in zip(small_names, small_g, s_delta, s_m, s_v)}

    order = ["g_mix", "w_in", "ln_v_g", "ln_v_b", "w_s", "b_s", "conv_w", "g_mem", "w_kv", "g_head", "w_o",
             "g_ffn", "w_ffn1", "w_ffn2", "g_final"]
    like = dict(g_mix=g_mix, w_in=w_in, ln_v_g=ln_v_g, ln_v_b=ln_v_b, w_s=w_s, b_s=b_s, conv_w=conv_w, g_mem=g_mem,
                w_kv=w_kv, g_head=g_head, w_o=w_o, g_ffn=g_ffn, w_ffn1=w_ffn1, w_ffn2=w_ffn2, g_final=g_final)
    res = {**big_out, **small_out}
    res["w_in"] = [a.T for a in res["w_in"]]
    outs = [loss, dx[None]]
    for k in range(4):
        outs += [res[nm][k].reshape(like[nm].shape) for nm in order]
    return tuple(outs)
```

```python
import math

import jax
import jax.numpy as jnp
from jax import lax
from jax.experimental import pallas as pl
from jax.experimental.pallas import tpu as pltpu

F32 = jnp.float32
BF16 = jnp.bfloat16
MESH = pl.DeviceIdType.MESH

D = 2048
S = 2048
HD = 128
NH = D // HD
NMH = 4
NSH = (NH - NMH) // 2
NCH = NH - NMH - NSH
DS = NSH * HD
DC = NCH * HD
DM = NMH * HD
DIN = 2 * DS + 3 * DC + DM
CHUNK = 128
NMEM = 256
DFF = 4 * D
EPS = 1e-6
NCHIP = 4
SCALE = HD ** -0.5

ADAM_LR = 0.001
ADAM_B1 = 0.9
ADAM_B2 = 0.999
ADAM_EPS = 1e-08
ADAM_WD = 0.01
ADAM_STEP = 10

TR_EW = 256
TR_MIX = 256
TM = 512
TN = 1024
TK = 2048
VMEM_MB = 56
HALO = 8


def _pick(n, target, q=128):
    best = None
    for t in range(q, min(n, target) + 1, q):
        if n % t == 0:
            best = t
    return n if best is None else best


def _pick_rows(n, q):
    below = _pick(n, TR_EW, q)
    if 2 * below >= TR_EW:
        return below
    above = [t for t in range(TR_EW, min(n, 4 * TR_EW) + 1, q) if n % t == 0]
    return above[0] if above else below


def _cp(sem=None, vmem_mb=None, **kw):
    d = dict(kw)
    if sem is not None:
        d["dimension_semantics"] = sem
    if vmem_mb is not None:
        d["vmem_limit_bytes"] = vmem_mb << 20
    return pltpu.CompilerParams(**d)


def _gelu(x):
    z = 0.7978845608028654 * (x + 0.044715 * (x * x * x))
    return 0.5 * x * (1.0 + jnp.tanh(z))


def _gelu_grad(x):
    x2 = x * x
    t = jnp.tanh(0.7978845608028654 * (x + 0.044715 * (x2 * x)))
    return 0.5 * (1.0 + t) + 0.5 * x * (1.0 - t * t) * (0.7978845608028654 * (1.0 + 3.0 * 0.044715 * x2))


def _matmul(a, b, *, name, ta=False, tb=False, M, N, K, tm=None, tn=None, tk=None, outs, epi=None,
            extras=(), b_spec=None, out_specs=None, after=()):
    n_after = len(after)
    tm = _pick(M, TM if tm is None else tm, 8)
    tn = _pick(N, TN if tn is None else tn)
    tk = _pick(K, TK if tk is None else tk)
    nk = K // tk
    grid = (N // tn, M // tm, nk)
    a_spec = (pl.BlockSpec((tk, tm), lambda j, i, k: (k, i)) if ta
              else pl.BlockSpec((tm, tk), lambda j, i, k: (i, k)))
    if b_spec is None:
        b_spec = (pl.BlockSpec((tn, tk), lambda j, i, k: (j, k)) if tb
                  else pl.BlockSpec((tk, tn), lambda j, i, k: (k, j)))
    else:
        b_spec = b_spec(tn, tk)
    if out_specs is None:
        out_specs = [pl.BlockSpec((tm, tn), lambda j, i, k: (i, j)) for _ in outs]
    else:
        out_specs = out_specs(tm, tn)
    dn = (((0 if ta else 1,), (1 if tb else 0,)), ((), ()))
    n_ex, n_out = len(extras), len(outs)

    def body(*refs):
        a_ref, b_ref = refs[0], refs[1]
        ex = refs[2:2 + n_ex]
        first_out = 2 + n_ex + n_after
        o = refs[first_out:first_out + n_out]
        acc = refs[first_out + n_out:]
        part = lax.dot_general(a_ref[...].astype(BF16), b_ref[...].astype(BF16), dn,
                               preferred_element_type=F32)

        def finish(val):
            res = (val,) if epi is None else epi(val, *[e[...] for e in ex])
            for r, o_ref in zip(res, o):
                o_ref[...] = r.astype(o_ref.dtype)

        if nk == 1:
            finish(part)
        else:
            k = pl.program_id(2)

            @pl.when(k == 0)
            def _():
                acc[0][...] = part

            @pl.when(k > 0)
            def _():
                acc[0][...] += part

            @pl.when(k == nk - 1)
            def _():
                finish(acc[0][...])

    return pl.pallas_call(
        body, name=name, grid=grid,
        in_specs=[a_spec, b_spec] + [sp(tm, tn) for _, sp in extras] + [ANY] * n_after,
        out_specs=out_specs, out_shape=outs,
        scratch_shapes=([pltpu.VMEM((tm, tn), F32)] if nk > 1 else []),
        compiler_params=_cp(("parallel", "parallel", "arbitrary"), VMEM_MB),
    )(a, b, *[arr for arr, _ in extras], *after)


def _tile_spec():
    return lambda tm, tn: pl.BlockSpec((tm, tn), lambda j, i, k: (i, j))


def _cast_into_slot(w, slot, after, *, name):
    R, C = w.shape
    tr = _pick_rows(R, 16)

    def body(s_ref, w_ref, _after_ref, o_ref):
        o_ref[...] = w_ref[...].astype(BF16)

    return pl.pallas_call(
        body, name=name,
        grid_spec=pltpu.PrefetchScalarGridSpec(
            num_scalar_prefetch=1, grid=(R // tr,),
            in_specs=[pl.BlockSpec((tr, C), lambda i, s: (i, 0)), ANY],
            out_specs=pl.BlockSpec((None, tr, C), lambda i, s: (s[0], i, 0))),
        out_shape=jax.ShapeDtypeStruct((NCHIP, R, C), BF16),
        compiler_params=_cp(("parallel",), VMEM_MB),
    )(slot, w, after)


def _rms_fwd(x, g, *, name, after=()):
    R, C = x.shape
    tr = _pick(R, TR_EW, 16)
    n_after = len(after)

    def body(x_ref, g_ref, *rest):
        o_ref = rest[n_after]
        xv = x_ref[...]
        r = lax.rsqrt(jnp.mean(xv * xv, axis=-1, keepdims=True) + EPS)
        o_ref[...] = ((xv * r) * g_ref[...]).astype(BF16)

    return pl.pallas_call(
        body, name=name, grid=(R // tr,),
        in_specs=[pl.BlockSpec((tr, C), lambda i: (i, 0)), pl.BlockSpec((1, C), lambda i: (0, 0))] + [ANY] * n_after,
        out_specs=pl.BlockSpec((tr, C), lambda i: (i, 0)),
        out_shape=jax.ShapeDtypeStruct((R, C), BF16),
        compiler_params=_cp(("parallel",), VMEM_MB),
    )(x, g, *after)


def _rms_bwd(dh, x, g, dres, *, name, want_dx=True, want_bf=True, after=()):
    R, C = x.shape
    tr = _pick(R, TR_EW, 16)
    has_res = dres is not None
    row = pl.BlockSpec((tr, C), lambda i: (i, 0))
    vec = pl.BlockSpec((1, C), lambda i: (0, 0))

    def body(*refs):
        dh_ref, x_ref, g_ref = refs[:3]
        pos = 3
        dres_ref = None
        if has_res:
            dres_ref = refs[pos]
            pos += 1
        outs = refs[pos + len(after):]
        i = pl.program_id(0)
        xv = x_ref[...]
        r = lax.rsqrt(jnp.mean(xv * xv, axis=-1, keepdims=True) + EPS)
        xh = xv * r
        dhv = dh_ref[...]
        dg_ref = outs[-1]
        dgp = jnp.sum(dhv * xh, axis=0, keepdims=True)

        @pl.when(i == 0)
        def _():
            dg_ref[...] = dgp

        @pl.when(i > 0)
        def _():
            dg_ref[...] += dgp

        if want_dx:
            t = dhv * g_ref[...]
            dx = r * (t - xh * jnp.mean(t * xh, axis=-1, keepdims=True))
            if has_res:
                dx = dx + dres_ref[...]
            outs[0][...] = dx
            if want_bf:
                outs[1][...] = dx.astype(BF16)

    in_specs = [row, row, vec] + ([row] if has_res else []) + [ANY] * len(after)
    out_specs, out_shape = [], []
    if want_dx:
        out_specs.append(row)
        out_shape.append(jax.ShapeDtypeStruct((R, C), F32))
        if want_bf:
            out_specs.append(row)
            out_shape.append(jax.ShapeDtypeStruct((R, C), BF16))
    out_specs.append(vec)
    out_shape.append(jax.ShapeDtypeStruct((1, C), F32))
    args = [dh, x, g] + ([dres] if has_res else []) + list(after)
    return pl.pallas_call(
        body, name=name, grid=(R // tr,), in_specs=in_specs, out_specs=out_specs, out_shape=out_shape,
        compiler_params=_cp(("arbitrary",), VMEM_MB),
    )(*args)


def _loss_bwd(x3, g, tgt, *, name):
    R, C = x3.shape
    tr = _pick(R, TR_EW, 16)
    n = R // tr
    row = pl.BlockSpec((tr, C), lambda i: (i, 0))
    vec = pl.BlockSpec((1, C), lambda i: (0, 0))

    def body(x_ref, g_ref, t_ref, dx_ref, dxb_ref, dg_ref, loss_ref, acc_ref):
        i = pl.program_id(0)
        xv = x_ref[...]
        gv = g_ref[...]
        r = lax.rsqrt(jnp.mean(xv * xv, axis=-1, keepdims=True) + EPS)
        xh = xv * r
        e = xh * gv - t_ref[...]
        dy = e * (1.0 / C)
        sq = jnp.sum(e * e, axis=0, keepdims=True)
        dgp = jnp.sum(dy * xh, axis=0, keepdims=True)

        @pl.when(i == 0)
        def _():
            acc_ref[...] = sq
            dg_ref[...] = dgp

        @pl.when(i > 0)
        def _():
            acc_ref[...] += sq
            dg_ref[...] += dgp

        t = dy * gv
        dx = r * (t - xh * jnp.mean(t * xh, axis=-1, keepdims=True))
        dx_ref[...] = dx
        dxb_ref[...] = dx.astype(BF16)

        @pl.when(i == n - 1)
        def _():
            loss_ref[...] = jnp.sum(acc_ref[...], axis=-1, keepdims=True) * (0.5 / C)

    return pl.pallas_call(
        body, name=name, grid=(n,),
        in_specs=[row, vec, row],
        out_specs=[row, row, vec, pl.BlockSpec((1, 1), lambda i: (0, 0))],
        out_shape=[jax.ShapeDtypeStruct((R, C), F32), jax.ShapeDtypeStruct((R, C), BF16),
                   jax.ShapeDtypeStruct((1, C), F32), jax.ShapeDtypeStruct((1, 1), F32)],
        scratch_shapes=[pltpu.VMEM((1, C), F32)],
        compiler_params=_cp(("arbitrary",), VMEM_MB),
    )(x3, g, tgt)


def _offsets():
    u0 = 0
    v0 = DS
    b0 = 2 * DS
    c0 = b0 + DC
    x0 = c0 + DC
    q0 = x0 + DC
    return u0, v0, b0, c0, x0, q0


def _tri_mask(lower):
    r = lax.broadcasted_iota(jnp.int32, (CHUNK, CHUNK), 0)
    c = lax.broadcasted_iota(jnp.int32, (CHUNK, CHUNK), 1)
    return (r >= c) if lower else (c >= r)


def _layer_norm_stats(vg):
    mu = jnp.mean(vg, axis=-1, keepdims=True)
    vc = vg - mu
    rstd = lax.rsqrt(jnp.mean(vc * vc, axis=-1, keepdims=True) + EPS)
    return vc * rstd, rstd


def _softmax_rows(qh, kh):
    s = lax.dot_general(qh, kh, (((1,), (1,)), ((), ())), preferred_element_type=F32)
    m = jnp.max(s, axis=-1, keepdims=True)
    e = jnp.exp(s - m)
    return e / jnp.sum(e, axis=-1, keepdims=True)


def _mix_fwd(proj, kv, w_s, bs_t, ln_g, ln_b, conv_w, g_head, *, name):
    assert DS == DC
    tr = _pick(S, TR_MIX, CHUNK)
    n = S // tr
    nck = tr // CHUNK
    u0, v0, b0, c0, x0, q0 = _offsets()
    hb = tr // HALO

    def body(p_ref, cprev_ref, xprev_ref, kv_ref, ws_ref, bst_ref, lng_ref, lnb_ref, cw_ref, gh_ref,
             heads_ref, hn_ref, ycv_ref, buf_ref):
        i = pl.program_id(0)

        def emit(col, val):
            rs = lax.rsqrt(jnp.mean(val * val, axis=-1, keepdims=True) + EPS)
            heads_ref[:, col:col + HD] = val
            hn_ref[:, col:col + HD] = ((val * rs) * gh_ref[:, col:col + HD]).astype(BF16)

        vhat, _ = _layer_norm_stats(_gelu(p_ref[:, v0:v0 + DS]))
        vnb = (vhat * lng_ref[...] + lnb_ref[...]).astype(BF16)
        low = _tri_mask(True)
        for h in range(NSH):
            wt = jnp.where(low, ws_ref[h], 0.0).astype(BF16)
            bcol = bst_ref[:, h:h + 1]
            parts = []
            for c in range(nck):
                blk = vnb[c * CHUNK:(c + 1) * CHUNK, h * HD:(h + 1) * HD]
                parts.append(jnp.dot(wt, blk, preferred_element_type=F32) + bcol)
            mixed = parts[0] if nck == 1 else jnp.concatenate(parts, axis=0)
            emit(h * HD, _gelu(p_ref[:, u0 + h * HD:u0 + (h + 1) * HD]) * mixed)

        xc = p_ref[:, c0:c0 + DC] * p_ref[:, x0:x0 + DC]
        prev = cprev_ref[...] * xprev_ref[...]
        buf_ref[0:HALO, :] = jnp.where(i > 0, prev, 0.0)
        buf_ref[HALO:HALO + tr, :] = xc
        y = (cw_ref[2:3, :] * xc + cw_ref[1:2, :] * buf_ref[HALO - 1:HALO - 1 + tr, :]
             + cw_ref[0:1, :] * buf_ref[HALO - 2:HALO - 2 + tr, :])
        ycv_ref[...] = y
        cout = p_ref[:, b0:b0 + DC] * y
        for h in range(NCH):
            emit(DS + h * HD, cout[:, h * HD:(h + 1) * HD])

        for h in range(NMH):
            qh = (p_ref[:, q0 + h * HD:q0 + (h + 1) * HD] * SCALE).astype(BF16)
            kh = kv_ref[:, h * HD:(h + 1) * HD].astype(BF16)
            vh = kv_ref[:, DM + h * HD:DM + (h + 1) * HD].astype(BF16)
            p = _softmax_rows(qh, kh)
            emit(DS + DC + h * HD, jnp.dot(p.astype(BF16), vh, preferred_element_type=F32))

    full = lambda shape: pl.BlockSpec(shape, lambda i: (0,) * len(shape))
    halo_c = pl.BlockSpec((HALO, DC), lambda i: (jnp.maximum(i * hb - 1, 0), c0 // DC))
    halo_x = pl.BlockSpec((HALO, DC), lambda i: (jnp.maximum(i * hb - 1, 0), x0 // DC))
    return pl.pallas_call(
        body, name=name, grid=(n,),
        in_specs=[pl.BlockSpec((tr, DIN), lambda i: (i, 0)), halo_c, halo_x,
                  full((NMEM, 2 * DM)), full((NSH, CHUNK, CHUNK)), full((CHUNK, NSH)),
                  full((1, DS)), full((1, DS)), full((3, DC)), full((1, D))],
        out_specs=[pl.BlockSpec((tr, D), lambda i: (i, 0)), pl.BlockSpec((tr, D), lambda i: (i, 0)),
                   pl.BlockSpec((tr, DC), lambda i: (i, 0))],
        out_shape=[jax.ShapeDtypeStruct((S, D), F32), jax.ShapeDtypeStruct((S, D), BF16),
                   jax.ShapeDtypeStruct((S, DC), F32)],
        scratch_shapes=[pltpu.VMEM((tr + HALO, DC), F32)],
        compiler_params=_cp(("parallel",), VMEM_MB),
    )(proj, proj, proj, kv, w_s, bs_t, ln_g, ln_b, conv_w, g_head)


def _mix_bwd(dhn, heads, proj, ycv, kv, w_s, bs_t, ln_g, ln_b, conv_w, g_head, after, *, name):
    assert DS == DC
    tr = _pick(S, TR_MIX, CHUNK)
    n = S // tr
    nck = tr // CHUNK
    u0, v0, b0, c0, x0, q0 = _offsets()
    hb = tr // HALO
    last_hb = S // HALO - 1

    def body(dhn_ref, heads_ref, p_ref, ycv_ref, dhn_nx_ref, heads_nx_ref, b_nx_ref, kv_ref, ws_ref, bst_ref,
             lng_ref, lnb_ref, cw_ref, gh_ref, _after_ref,
             dp_ref, dkv_ref, dws_ref, dbs_ref, dlng_ref, dlnb_ref, dcw_ref, dgh_ref, buf_ref, dvn_ref):
        i = pl.program_id(0)

        @pl.when(i == 0)
        def _():
            dkv_ref[...] = jnp.zeros_like(dkv_ref)
            dws_ref[...] = jnp.zeros_like(dws_ref)
            dbs_ref[...] = jnp.zeros_like(dbs_ref)
            dlng_ref[...] = jnp.zeros_like(dlng_ref)
            dlnb_ref[...] = jnp.zeros_like(dlnb_ref)
            dcw_ref[...] = jnp.zeros_like(dcw_ref)
            dgh_ref[...] = jnp.zeros_like(dgh_ref)

        def head_bwd(a, dn, gh):
            rs = lax.rsqrt(jnp.mean(a * a, axis=-1, keepdims=True) + EPS)
            ah = a * rs
            t = dn * gh
            return rs * (t - ah * jnp.mean(t * ah, axis=-1, keepdims=True)), jnp.sum(dn * ah, axis=0, keepdims=True)

        def head_grad(col):
            da, dg = head_bwd(heads_ref[:, col:col + HD], dhn_ref[:, col:col + HD], gh_ref[:, col:col + HD])
            dgh_ref[:, col:col + HD] += dg
            return da

        v = p_ref[:, v0:v0 + DS]
        vhat, rstd = _layer_norm_stats(_gelu(v))
        vnb = (vhat * lng_ref[...] + lnb_ref[...]).astype(BF16)
        low = _tri_mask(True)
        ones = jnp.ones((HALO, HD), BF16)
        for h in range(NSH):
            w_h = ws_ref[h]
            wt = jnp.where(low, w_h, 0.0).astype(BF16)
            bcol = bst_ref[:, h:h + 1]
            da = head_grad(h * HD)
            u = p_ref[:, u0 + h * HD:u0 + (h + 1) * HD]
            ug = _gelu(u)
            dws = jnp.zeros((CHUNK, CHUNK), F32)
            dbs = jnp.zeros((HALO, CHUNK), F32)
            mixed_parts = []
            for c in range(nck):
                rows = slice(c * CHUNK, (c + 1) * CHUNK)
                blk = vnb[rows, h * HD:(h + 1) * HD]
                mixed_parts.append(jnp.dot(wt, blk, preferred_element_type=F32) + bcol)
                dmb = (da[rows] * ug[rows]).astype(BF16)
                dws = dws + lax.dot_general(dmb, blk, (((1,), (1,)), ((), ())), preferred_element_type=F32)
                dbs = dbs + lax.dot_general(ones, dmb, (((1,), (1,)), ((), ())), preferred_element_type=F32)
                dvn_ref[c * CHUNK:(c + 1) * CHUNK, h * HD:(h + 1) * HD] = lax.dot_general(
                    wt, dmb, (((0,), (0,)), ((), ())), preferred_element_type=F32)
            mixed = mixed_parts[0] if nck == 1 else jnp.concatenate(mixed_parts, axis=0)
            dp_ref[:, u0 + h * HD:u0 + (h + 1) * HD] = ((da * mixed) * _gelu_grad(u)).astype(BF16)
            dws_ref[h] += jnp.where(low, dws, 0.0)
            dbs_ref[h] += dbs
        dvn = dvn_ref[...]
        dlng_ref[...] += jnp.sum(dvn * vhat, axis=0, keepdims=True)
        dlnb_ref[...] += jnp.sum(dvn, axis=0, keepdims=True)
        dvh = dvn * lng_ref[...]
        dvg = rstd * (dvh - jnp.mean(dvh, axis=-1, keepdims=True)
                      - vhat * jnp.mean(dvh * vhat, axis=-1, keepdims=True))
        dp_ref[:, v0:v0 + DS] = (dvg * _gelu_grad(v)).astype(BF16)

        dc = jnp.concatenate([head_grad(DS + h * HD) for h in range(NCH)], axis=1)
        dc_nx = jnp.concatenate(
            [head_bwd(heads_nx_ref[:, h * HD:(h + 1) * HD], dhn_nx_ref[:, h * HD:(h + 1) * HD],
                      gh_ref[:, DS + h * HD:DS + (h + 1) * HD])[0] for h in range(NCH)], axis=1)
        bg = p_ref[:, b0:b0 + DC]
        cg = p_ref[:, c0:c0 + DC]
        xin = p_ref[:, x0:x0 + DC]
        dp_ref[:, b0:b0 + DC] = (dc * ycv_ref[...]).astype(BF16)
        dyv = dc * bg
        buf_ref[0:tr, :] = dyv
        buf_ref[tr:tr + HALO, :] = jnp.where(i < n - 1, dc_nx * b_nx_ref[...], 0.0)
        sh1 = buf_ref[1:1 + tr, :]
        sh0 = buf_ref[2:2 + tr, :]
        dxc = cw_ref[2:3, :] * dyv + cw_ref[1:2, :] * sh1 + cw_ref[0:1, :] * sh0
        xc = cg * xin
        dp_ref[:, c0:c0 + DC] = (dxc * xin).astype(BF16)
        dp_ref[:, x0:x0 + DC] = (dxc * cg).astype(BF16)
        dcw_ref[0:1, :] += jnp.sum(sh0 * xc, axis=0, keepdims=True)
        dcw_ref[1:2, :] += jnp.sum(sh1 * xc, axis=0, keepdims=True)
        dcw_ref[2:3, :] += jnp.sum(dyv * xc, axis=0, keepdims=True)

        for h in range(NMH):
            do = head_grad(DS + DC + h * HD).astype(BF16)
            qh = (p_ref[:, q0 + h * HD:q0 + (h + 1) * HD] * SCALE).astype(BF16)
            kh = kv_ref[:, h * HD:(h + 1) * HD].astype(BF16)
            vh = kv_ref[:, DM + h * HD:DM + (h + 1) * HD].astype(BF16)
            p = _softmax_rows(qh, kh)
            dpr = lax.dot_general(do, vh, (((1,), (1,)), ((), ())), preferred_element_type=F32)
            ds = (p * (dpr - jnp.sum(dpr * p, axis=-1, keepdims=True))).astype(BF16)
            dp_ref[:, q0 + h * HD:q0 + (h + 1) * HD] = (
                jnp.dot(ds, kh, preferred_element_type=F32) * SCALE).astype(BF16)
            dkv_ref[:, h * HD:(h + 1) * HD] += lax.dot_general(
                ds, qh, (((0,), (0,)), ((), ())), preferred_element_type=F32)
            dkv_ref[:, DM + h * HD:DM + (h + 1) * HD] += lax.dot_general(
                p.astype(BF16), do, (((0,), (0,)), ((), ())), preferred_element_type=F32)

    full = lambda shape: pl.BlockSpec(shape, lambda i: (0,) * len(shape))
    row = lambda c: pl.BlockSpec((tr, c), lambda i: (i, 0))
    nxt = lambda col: pl.BlockSpec((HALO, DC), lambda i: (jnp.minimum((i + 1) * hb, last_hb), col))
    return pl.pallas_call(
        body, name=name, grid=(n,),
        in_specs=[row(D), row(D), row(DIN), row(DC), nxt(DS // DC), nxt(DS // DC), nxt(b0 // DC),
                  full((NMEM, 2 * DM)), full((NSH, CHUNK, CHUNK)), full((CHUNK, NSH)),
                  full((1, DS)), full((1, DS)), full((3, DC)), full((1, D)), ANY],
        out_specs=[row(DIN), full((NMEM, 2 * DM)), full((NSH, CHUNK, CHUNK)), full((NSH, HALO, CHUNK)),
                   full((1, DS)), full((1, DS)), full((HALO, DC)), full((1, D))],
        out_shape=[jax.ShapeDtypeStruct((S, DIN), BF16), jax.ShapeDtypeStruct((NMEM, 2 * DM), F32),
                   jax.ShapeDtypeStruct((NSH, CHUNK, CHUNK), F32), jax.ShapeDtypeStruct((NSH, HALO, CHUNK), F32),
                   jax.ShapeDtypeStruct((1, DS), F32), jax.ShapeDtypeStruct((1, DS), F32),
                   jax.ShapeDtypeStruct((HALO, DC), F32), jax.ShapeDtypeStruct((1, D), F32)],
        scratch_shapes=[pltpu.VMEM((tr + HALO, DC), F32), pltpu.VMEM((tr, DS), F32)],
        compiler_params=_cp(("arbitrary",), VMEM_MB),
    )(dhn, heads, proj, ycv, dhn, heads, proj, kv, w_s, bs_t, ln_g, ln_b, conv_w, g_head, after)


def _place():
    x, y, c = lax.axis_index("x"), lax.axis_index("y"), lax.axis_index("c")
    chips = [(1 - x, y), (x, 1 - y), (1 - x, 1 - y)]
    return x, y, c, chips


ANY = pl.BlockSpec(memory_space=pl.ANY)


HBM = pl.BlockSpec(memory_space=pltpu.HBM)
SEM = pl.BlockSpec(memory_space=pltpu.SEMAPHORE)
EFFECT = pltpu.SideEffectType.DATAFLOW_SIDE_EFFECTING
N_PEER_CHIPS = 3


def _in_hbm(a):
    return pltpu.with_memory_space_constraint(a, pltpu.HBM)


def _allgather_start(bufs, after, *, name):
    nw = len(bufs)

    def body(*refs):
        ins, send, recv = refs[:nw], refs[nw + 1:2 * nw + 1], refs[2 * nw + 1:3 * nw + 1]
        token = refs[4 * nw + 1]
        x, y, c, chips = _place()
        s = 2 * x + y
        for w in range(nw):
            hr = bufs[w].shape[1] // 2
            rows = ins[w].at[s, pl.ds(c * hr, hr)]
            for cx, cy in chips:
                pltpu.make_async_remote_copy(src_ref=rows, dst_ref=rows, send_sem=send[w], recv_sem=recv[w],
                                             device_id=(cx, cy, c), device_id_type=MESH).start()
        token[...] = jnp.zeros_like(token)

    res = pl.pallas_call(
        body, name=name,
        in_specs=[HBM] * nw + [ANY],
        out_specs=[SEM] * (2 * nw) + [HBM] * nw + [pl.BlockSpec(memory_space=pltpu.VMEM)],
        out_shape=[pltpu.SemaphoreType.DMA(())] * (2 * nw) + [pltpu.HBM(a.shape, a.dtype) for a in bufs]
        + [jax.ShapeDtypeStruct((8, 128), F32)],
        input_output_aliases={w: 2 * nw + w for w in range(nw)},
        compiler_params=pltpu.CompilerParams(has_side_effects=EFFECT),
    )(*[_in_hbm(a) for a in bufs], after)
    return res[:nw], res[nw:2 * nw], res[2 * nw:3 * nw], res[3 * nw]


def _scatter_start(parts, bufs, *, name):
    nw = len(parts)

    def body(*refs):
        src, dst = refs[:nw], refs[nw:2 * nw]
        send, recv = refs[2 * nw:3 * nw], refs[3 * nw:4 * nw]
        token = refs[6 * nw]
        x, y, c, chips = _place()
        s = 2 * x + y
        for w in range(nw):
            for cx, cy in chips:
                pltpu.make_async_remote_copy(src_ref=src[w].at[2 * cx + cy], dst_ref=dst[w].at[s], send_sem=send[w],
                                             recv_sem=recv[w], device_id=(cx, cy, c), device_id_type=MESH).start()
        token[...] = jnp.zeros_like(token)

    res = pl.pallas_call(
        body, name=name,
        in_specs=[HBM] * (2 * nw),
        out_specs=[SEM] * (2 * nw) + [HBM] * (2 * nw) + [pl.BlockSpec(memory_space=pltpu.VMEM)],
        out_shape=[pltpu.SemaphoreType.DMA(())] * (2 * nw) + [pltpu.HBM(a.shape, a.dtype) for a in parts + bufs]
        + [jax.ShapeDtypeStruct((8, 128), F32)],
        input_output_aliases={k: 2 * nw + k for k in range(2 * nw)},
        compiler_params=pltpu.CompilerParams(has_side_effects=EFFECT),
    )(*[_in_hbm(a) for a in parts + bufs])
    return res[:nw], res[nw:2 * nw], res[2 * nw:3 * nw], res[3 * nw:4 * nw], res[4 * nw]


def _sibling_start(srcs, *, name):
    nw = len(srcs)
    lands = [lax.empty((a.shape[0], a.shape[1] // 2, a.shape[2]), a.dtype) for a in srcs]

    def body(*refs):
        src, land = refs[:nw], refs[nw:2 * nw]
        send, recv = refs[2 * nw:3 * nw], refs[3 * nw:4 * nw]
        token = refs[6 * nw]
        x, y, c, _ = _place()
        for w in range(nw):
            hr = srcs[w].shape[1] // 2
            rows = src[w].at[:, pl.ds((1 - c) * hr, hr)]
            pltpu.make_async_remote_copy(src_ref=rows, dst_ref=land[w], send_sem=send[w], recv_sem=recv[w],
                                         device_id=(x, y, 1 - c), device_id_type=MESH).start()
        token[...] = jnp.zeros_like(token)

    res = pl.pallas_call(
        body, name=name,
        in_specs=[HBM] * (2 * nw),
        out_specs=[SEM] * (2 * nw) + [HBM] * (2 * nw) + [pl.BlockSpec(memory_space=pltpu.VMEM)],
        out_shape=[pltpu.SemaphoreType.DMA(())] * (2 * nw) + [pltpu.HBM(a.shape, a.dtype) for a in srcs + lands]
        + [jax.ShapeDtypeStruct((8, 128), F32)],
        input_output_aliases={k: 2 * nw + k for k in range(2 * nw)},
        compiler_params=pltpu.CompilerParams(has_side_effects=EFFECT),
    )(*[_in_hbm(a) for a in srcs + lands])
    return res[:nw], res[nw:2 * nw], res[2 * nw:3 * nw], res[3 * nw:4 * nw], res[4 * nw]


def _transfer_wait(sends, recvs, thru, sizes, after, *, name):
    n = len(sends)
    flat = [a for group in thru for a in group]

    def body(*refs):
        bufs = refs[:len(flat)]
        send = refs[len(flat):len(flat) + n]
        recv = refs[len(flat) + n:len(flat) + 2 * n]
        token = refs[2 * len(flat) + 2 * n + 1]
        token[...] = jnp.zeros_like(token)
        x, y, c, _ = _place()
        pos = 0
        for k in range(n):
            slots, rows = sizes[k]
            region = bufs[pos].at[pl.ds(0, slots), pl.ds(0, rows)]
            pos += len(thru[k])
            cp = pltpu.make_async_remote_copy(src_ref=region, dst_ref=region, send_sem=send[k], recv_sem=recv[k],
                                              device_id=(x, y, 1 - c), device_id_type=MESH)
            cp.wait_send()
            cp.wait_recv()

    res = pl.pallas_call(
        body, name=name,
        in_specs=[HBM] * len(flat) + [SEM] * (2 * n) + [pl.BlockSpec(memory_space=pl.ANY)],
        out_specs=[HBM] * len(flat) + [pl.BlockSpec(memory_space=pltpu.VMEM)],
        out_shape=[pltpu.HBM(a.shape, a.dtype) for a in flat] + [jax.ShapeDtypeStruct((8, 128), F32)],
        input_output_aliases={k: k for k in range(len(flat))},
        compiler_params=pltpu.CompilerParams(has_side_effects=EFFECT),
    )(*flat, *sends, *recvs, after)
    out, pos = [], 0
    for group in thru:
        out.append(res[pos:pos + len(group)])
        pos += len(group)
    return out, res[len(flat)]


def _forward_to_sibling(bufs, after, *, name):
    nw = len(bufs)

    def body(*refs):
        outs = refs[nw + 1:2 * nw + 1]
        send, recv = refs[2 * nw + 1:]
        x, y, c, chips = _place()
        me, sibling = (x, y, c), (x, y, 1 - c)

        def d2d(w, j, which, to):
            cx, cy = chips[j]
            hr = bufs[w].shape[1] // 2
            rows = outs[w].at[2 * cx + cy, pl.ds(which * hr, hr)]
            return pltpu.make_async_remote_copy(
                src_ref=rows, dst_ref=rows, send_sem=send.at[N_PEER_CHIPS * w + j],
                recv_sem=recv.at[N_PEER_CHIPS * w + j], device_id=to, device_id_type=MESH)

        passed = [d2d(w, j, c, sibling) for w in range(nw) for j in range(N_PEER_CHIPS)]
        for cp in passed:
            cp.start()
        for w in range(nw):
            for j in range(N_PEER_CHIPS):
                d2d(w, j, 1 - c, me).wait_recv()
        for cp in passed:
            cp.wait_send()

    return pl.pallas_call(
        body, name=name,
        in_specs=[ANY] * (nw + 1), out_specs=[ANY] * nw,
        out_shape=[jax.ShapeDtypeStruct(a.shape, a.dtype) for a in bufs],
        input_output_aliases={w: w for w in range(nw)},
        scratch_shapes=[pltpu.SemaphoreType.DMA((N_PEER_CHIPS * nw,)), pltpu.SemaphoreType.DMA((N_PEER_CHIPS * nw,))],
    )(*bufs, after)


def _sibling_exchange(arrs, *, name):
    nw = len(arrs)

    def body(*refs):
        ins, outs = refs[:nw], refs[nw:2 * nw]
        send, recv = refs[2 * nw:]
        x, y, c, _ = _place()
        cps = [pltpu.make_async_remote_copy(src_ref=ins[w], dst_ref=outs[w], send_sem=send.at[w], recv_sem=recv.at[w],
                                            device_id=(x, y, 1 - c), device_id_type=MESH) for w in range(nw)]
        for cp in cps:
            cp.start()
        for cp in cps:
            cp.wait()

    return pl.pallas_call(
        body, name=name, in_specs=[ANY] * nw, out_specs=[ANY] * nw,
        out_shape=[jax.ShapeDtypeStruct(a.shape, a.dtype) for a in arrs],
        scratch_shapes=[pltpu.SemaphoreType.DMA((nw,)), pltpu.SemaphoreType.DMA((nw,))],
    )(*arrs)


def _allreduce_small(p, after, *, name):
    R = p.shape[0]
    hr = R // 2

    def body(p_ref, _after_ref, out_ref, sib_ref, sum_ref, gat_ref, tot_ref, send, recv):
        x, y, c, chips = _place()
        s = 2 * x + y
        sibling = (x, y, 1 - c)
        rows = pl.ds(pl.multiple_of(c * hr, 8), hr)
        swap = pltpu.make_async_remote_copy(src_ref=p_ref, dst_ref=sib_ref, send_sem=send.at[0], recv_sem=recv.at[0],
                                            device_id=sibling, device_id_type=MESH)
        swap.start()
        swap.wait()
        sum_ref[...] = p_ref[...] + sib_ref[...]
        gat_ref[s] = sum_ref[rows, :]
        cps = [pltpu.make_async_remote_copy(src_ref=sum_ref.at[rows], dst_ref=gat_ref.at[s], send_sem=send.at[1 + j],
                                            recv_sem=recv.at[1 + j], device_id=(cx, cy, c), device_id_type=MESH)
               for j, (cx, cy) in enumerate(chips)]
        for cp in cps:
            cp.start()
        for cp in cps:
            cp.wait()
        tot_ref[...] = ((gat_ref[0] + gat_ref[1]) + gat_ref[2]) + gat_ref[3]
        out_ref[rows, :] = tot_ref[...]
        share = pltpu.make_async_remote_copy(src_ref=tot_ref, dst_ref=out_ref.at[rows], send_sem=send.at[4],
                                             recv_sem=recv.at[4], device_id=sibling, device_id_type=MESH)
        share.start()
        share.wait_send()
        other = out_ref.at[pl.ds(pl.multiple_of((1 - c) * hr, 8), hr)]
        pltpu.make_async_remote_copy(src_ref=other, dst_ref=other, send_sem=send.at[4], recv_sem=recv.at[4],
                                     device_id=(x, y, c), device_id_type=MESH).wait_recv()

    vmem = pl.BlockSpec(memory_space=pltpu.VMEM)
    return pl.pallas_call(
        body, name=name, in_specs=[vmem, ANY], out_specs=vmem,
        out_shape=jax.ShapeDtypeStruct((R, 128), F32),
        scratch_shapes=[pltpu.VMEM((R, 128), F32), pltpu.VMEM((R, 128), F32), pltpu.VMEM((NCHIP, hr, 128), F32),
                        pltpu.VMEM((hr, 128), F32), pltpu.SemaphoreType.DMA((5,)), pltpu.SemaphoreType.DMA((5,))],
    )(p, after)


def _select_half_bf16(g, half, add, slot, *, name):
    _, R, C = g.shape
    hr = R // 2
    tr = _pick_rows(hr, 16)
    nb = hr // tr
    sel = jnp.concatenate([jnp.reshape(half, (1,)).astype(jnp.int32), slot])

    def body(s_ref, g_ref, a_ref, o_ref, own_ref):
        val = (g_ref[...].astype(F32) + a_ref[...].astype(F32)).astype(BF16)
        o_ref[...] = val

        @pl.when(pl.program_id(1) == s_ref[1])
        def _():
            own_ref[...] = val

    g_spec = pl.BlockSpec((None, tr, C), lambda i, j, s: (j, s[0] * nb + i, 0))
    o_spec = pl.BlockSpec((None, tr, C), lambda i, j, s: (j, i, 0))
    own_spec = pl.BlockSpec((None, tr, C), lambda i, j, s: (s[1], i, 0))
    shape = jax.ShapeDtypeStruct((NCHIP, hr, C), BF16)
    return pl.pallas_call(
        body, name=name,
        grid_spec=pltpu.PrefetchScalarGridSpec(
            num_scalar_prefetch=1, grid=(nb, NCHIP), in_specs=[g_spec, o_spec], out_specs=[o_spec, own_spec]),
        out_shape=[shape, shape],
        compiler_params=_cp(("parallel", "arbitrary"), VMEM_MB),
    )(sel, g, add)


def _sum_slots(r, *, name):
    _, R, C = r.shape
    tr = _pick_rows(R, 16)

    def body(r_ref, o_ref):
        acc = r_ref[0].astype(F32) + r_ref[1].astype(F32)
        for j in range(2, NCHIP):
            acc = acc + r_ref[j].astype(F32)
        o_ref[...] = acc

    return pl.pallas_call(
        body, name=name, grid=(R // tr,),
        in_specs=[pl.BlockSpec((NCHIP, tr, C), lambda i: (0, i, 0))],
        out_specs=pl.BlockSpec((tr, C), lambda i: (i, 0)),
        out_shape=jax.ShapeDtypeStruct((R, C), F32),
        compiler_params=_cp(("parallel",), VMEM_MB),
    )(r)


def _adamw_math(w, g, m, v):
    m = ADAM_B1 * m + (1.0 - ADAM_B1) * g
    v = ADAM_B2 * v + (1.0 - ADAM_B2) * (g * g)
    m_hat = m / (1.0 - ADAM_B1 ** ADAM_STEP)
    v_hat = v / (1.0 - ADAM_B2 ** ADAM_STEP)
    delta = -ADAM_LR * (m_hat / (jnp.sqrt(v_hat) + ADAM_EPS) + ADAM_WD * w)
    return delta, m, v


def _adamw(w, g_mine, g_sib, m, v, core, *, name):
    R, C = w.shape
    hr = R // 2
    tr = _pick_rows(hr, 8)
    nb = hr // tr
    row = pl.BlockSpec((tr, C), lambda hh, i, c: (hh * nb + i, 0))
    half = pl.BlockSpec((tr, C), lambda hh, i, c: (i, 0))

    def body(c_ref, w_ref, gm_ref, gs_ref, m_ref, v_ref, go_ref, d_ref, mo_ref, vo_ref):
        gv = jnp.where(pl.program_id(0) == c_ref[0], gm_ref[...], gs_ref[...])
        d, mn, vn = _adamw_math(w_ref[...], gv, m_ref[...], v_ref[...])
        go_ref[...] = gv
        d_ref[...] = d
        mo_ref[...] = mn
        vo_ref[...] = vn

    return pl.pallas_call(
        body, name=name,
        grid_spec=pltpu.PrefetchScalarGridSpec(
            num_scalar_prefetch=1, grid=(2, nb),
            in_specs=[row, half, half, row, row], out_specs=[row] * 4),
        out_shape=[jax.ShapeDtypeStruct((R, C), F32)] * 4,
        compiler_params=_cp(("parallel", "parallel"), VMEM_MB),
    )(core, w, g_mine, g_sib, m, v)


def _adamw_small(ws, gs, ms, vs, *, name):
    n = len(ws)

    def body(*refs):
        w_r, g_r, m_r, v_r = refs[:n], refs[n:2 * n], refs[2 * n:3 * n], refs[3 * n:4 * n]
        d_r, mo_r, vo_r = refs[4 * n:5 * n], refs[5 * n:6 * n], refs[6 * n:7 * n]
        for k in range(n):
            d, mn, vn = _adamw_math(w_r[k][...], g_r[k][...], m_r[k][...], v_r[k][...])
            d_r[k][...] = d
            mo_r[k][...] = mn
            vo_r[k][...] = vn

    shapes = [jax.ShapeDtypeStruct(w.shape, F32) for w in ws]
    res = pl.pallas_call(body, name=name, out_shape=shapes * 3)(*ws, *gs, *ms, *vs)
    return res[:n], res[n:2 * n], res[2 * n:]


_PACK_ROWS = 8


def _pack(parts):
    rows = []
    for a in parts:
        flat = a.reshape(-1)
        n = -(-flat.shape[0] // (_PACK_ROWS * 128)) * (_PACK_ROWS * 128)
        rows.append(jnp.pad(flat, (0, n - flat.shape[0])).reshape(-1, 128))
    total = sum(r.shape[0] for r in rows)
    if total % 16:
        rows.append(jnp.zeros((16 - total % 16, 128), F32))
    return jnp.concatenate(rows, axis=0)


def _unpack(p, shapes):
    out, r = [], 0
    for shp in shapes:
        n = math.prod(shp)
        nr = -(-n // (_PACK_ROWS * 128)) * _PACK_ROWS
        out.append(p[r:r + nr].reshape(-1)[:n].reshape(shp))
        r += nr
    return out


def kernel(x, mem, g_mix, w_in, ln_v_g, ln_v_b, w_s, b_s, conv_w, g_mem, w_kv, g_head, w_o, g_ffn, w_ffn1, w_ffn2, g_final, loss_target, m_g_mix, m_w_in, m_ln_v_g, m_ln_v_b, m_w_s, m_b_s, m_conv_w, m_g_mem, m_w_kv, m_g_head, m_w_o, m_g_ffn, m_w_ffn1, m_w_ffn2, m_g_final, v_g_mix, v_w_in, v_ln_v_g, v_ln_v_b, v_w_s, v_b_s, v_conv_w, v_g_mem, v_w_kv, v_g_head, v_w_o, v_g_ffn, v_w_ffn1, v_w_ffn2, v_g_final):
    sds = jax.ShapeDtypeStruct
    xi, yi = lax.axis_index("x"), lax.axis_index("y")
    shard = 2 * xi + yi
    x2d, mem2d, tgt = x[0], mem[0], loss_target[0]
    ws3, bs2 = w_s[0], b_s[0]
    g_final2 = g_final.reshape(1, D)
    dff4 = DFF // NCHIP
    din4 = DIN // NCHIP
    dcv4 = DC // NCHIP

    big = [w_in[0].T, w_kv[0], w_o[0], w_ffn1[0], w_ffn2[0]]
    big_names = ["w_in", "w_kv", "w_o", "w_ffn1", "w_ffn2"]
    slot = jnp.reshape(shard, (1,)).astype(jnp.int32)
    core = jnp.reshape(lax.axis_index("c"), (1,)).astype(jnp.int32)
    conv_pad = jnp.pad(conv_w[0], ((0, 16 - 3), (0, 256 - dcv4)))
    conv_slots = lax.dynamic_update_slice(jnp.zeros((NCHIP, 16, 256), F32), conv_pad[None], (shard, 0, 0))

    def gather_start(bufs, after, nm):
        return _allgather_start(bufs, after, name="ag_start_" + nm)

    def gather_wait(state, idx, after, nm):
        send, recv, bufs, _ = state
        got, token = _transfer_wait([send[k] for k in idx], [recv[k] for k in idx], [[bufs[k]] for k in idx],
                                    [(N_PEER_CHIPS, bufs[k].shape[1] // 2) for k in idx], after, name="ag_wait_" + nm)
        return [g[0] for g in got], token

    cast = lambda k, after: _cast_into_slot(big[k], slot, after, name="cast_" + big_names[k])
    ag_in = gather_start([cast(0, slot), conv_slots], slot, "in")
    bs_t = bs2.T

    h = _rms_fwd(x2d, g_mix, name="rms_mix", after=[ag_in[3]])
    mem_n = _rms_fwd(mem2d, g_mem, name="rms_mem", after=[h])
    rest_b = [cast(1, mem_n)]
    rest_b.append(cast(2, rest_b[0]))
    rest_b.append(cast(3, rest_b[1]))
    w2_b = cast(4, rest_b[2])
    got_in, tok = gather_wait(ag_in, [0, 1], w2_b, "in")
    ag_rest = gather_start(rest_b, tok, "kvo1")
    win4, conv4 = _forward_to_sibling(got_in, ag_rest[3], name="ag_fwd_in")
    w_in_t = win4.reshape(DIN, D)
    conv_full = conv4[:, :3, :dcv4].transpose(1, 0, 2).reshape(3, DC)
    (proj,) = _matmul(h, w_in_t, name="mm_proj", tb=True, M=S, N=DIN, K=D, tn=DIN // 2, outs=[sds((S, DIN), F32)])
    got_kvo, tok = gather_wait(ag_rest, [0, 1], proj, "kvo")
    wkv4, wo4 = _forward_to_sibling(got_kvo, tok, name="ag_fwd_kvo")
    w_kv_full = wkv4.reshape(D, 2 * DM)
    w_o_full = wo4.reshape(D, D)
    (kv,) = _matmul(mem_n, w_kv_full, name="mm_kv", M=NMEM, N=2 * DM, K=D, outs=[sds((NMEM, 2 * DM), F32)])
    heads, hn, ycv = _mix_fwd(proj, kv, ws3, bs_t, ln_v_g, ln_v_b, conv_full, g_head, name="mix_fwd")
    (x2,) = _matmul(hn, w_o_full, name="mm_wo", M=S, N=D, K=D, outs=[sds((S, D), F32)],
                    epi=lambda acc, res: (acc + res,), extras=[(x2d, _tile_spec())])
    h2 = _rms_fwd(x2, g_ffn, name="rms_ffn")
    got_w1, tok = gather_wait(ag_rest, [2], h2, "ffn1")
    ag_w2 = gather_start([w2_b], tok, "ffn2")
    (w14,) = _forward_to_sibling(got_w1, ag_w2[3], name="ag_fwd_ffn1")

    def w1_cols(tn, tk):
        nb = dff4 // tn
        return pl.BlockSpec((None, tk, tn), lambda j, i, k: (j // nb, k, j % nb))

    f, act = _matmul(h2, w14, name="mm_ffn1", M=S, N=DFF, K=D, b_spec=w1_cols,
                     outs=[sds((S, DFF), F32), sds((S, DFF), BF16)],
                     epi=lambda acc: (acc, jnp.square(jnp.maximum(acc, 0.0))))
    got_w2, tok = gather_wait(ag_w2, [0], act, "ffn2")
    (w24,) = _forward_to_sibling(got_w2, tok, name="ag_fwd_ffn2")
    w2_full = w24.reshape(DFF, D)
    (x3,) = _matmul(act, w2_full, name="mm_ffn2", M=S, N=D, K=DFF, outs=[sds((S, D), F32)],
                    epi=lambda acc, res: (acc + res,), extras=[(x2, _tile_spec())])

    ci = lax.axis_index("c")

    def rs_sibling(g4, nm):
        return _sibling_start([g4], name="rs_sib_" + nm)

    def rs_chips(state, after, nm):
        send, recv, g4, land, _ = state
        (((land_, g4_),), _) = _transfer_wait(send, recv, [[land[0], g4[0]]], [(NCHIP, land[0].shape[1])], after,
                                             name="rs_sibwait_" + nm)
        part, buf = _select_half_bf16(g4_, ci, land_, slot, name="rs_add_" + nm)
        return _scatter_start([part], [buf], name="rs_start_" + nm)

    def rs_end(state, after, nm):
        send, recv, parts, bufs, _ = state
        (((buf, _),), _) = _transfer_wait(send, recv, [[bufs[0], parts[0]]], [(N_PEER_CHIPS, bufs[0].shape[1])], after,
                                          name="rs_wait_" + nm)
        return _sum_slots(buf, name="rs_sum_" + nm)

    big_m = [m_w_in[0].T, m_w_kv[0], m_w_o[0], m_w_ffn1[0], m_w_ffn2[0]]
    big_v = [v_w_in[0].T, v_w_kv[0], v_w_o[0], v_w_ffn1[0], v_w_ffn2[0]]
    big_out = {}

    def rs_finish(ks, halves, nm):
        sib = _sibling_exchange(halves, name="rs_share_" + nm)
        for k, g, gs in zip(ks, halves, sib):
            big_out[big_names[k]] = _adamw(big[k], g, gs, big_m[k], big_v[k], core, name="adamw_" + big_names[k])

    dx3, dx3b, dg_final, loss11 = _loss_bwd(x3, g_final2, tgt, name="loss_bwd")
    (dw2,) = _matmul(act, dx3b, name="mm_dw2", ta=True, M=DFF, N=D, K=S, outs=[sds((DFF, D), BF16)])
    sib_w2 = rs_sibling(dw2.reshape(NCHIP, dff4, D), "w_ffn2")
    (dfb,) = _matmul(dx3b, w2_full, name="mm_dact", tb=True, M=S, N=DFF, K=D, outs=[sds((S, DFF), BF16)],
                     epi=lambda acc, fv: (acc * (2.0 * jnp.maximum(fv, 0.0)),), extras=[(f, _tile_spec())],
                     after=[sib_w2[4]])
    rs_w2 = rs_chips(sib_w2, dfb, "w_ffn2")

    def dw1_out(tm, tn):
        nb = dff4 // tn
        return [pl.BlockSpec((None, tm, tn), lambda j, i, k: (j // nb, i, j % nb))]

    (dw1,) = _matmul(h2, dfb, name="mm_dw1", ta=True, M=D, N=DFF, K=S, outs=[sds((NCHIP, D, dff4), BF16)],
                     out_specs=dw1_out, after=[rs_w2[4]])
    sib_w1 = rs_sibling(dw1, "w_ffn1")

    def w1_rows(tn, tk):
        kb = dff4 // tk
        return pl.BlockSpec((None, tn, tk), lambda j, i, k: (k // kb, j, k % kb))

    (dh2,) = _matmul(dfb, w14, name="mm_dh2", tb=True, M=S, N=D, K=DFF, b_spec=w1_rows, outs=[sds((S, D), F32)],
                     after=[sib_w1[4]])
    rs_w1 = rs_chips(sib_w1, dh2, "w_ffn1")
    dx2, dx2b, dg_ffn = _rms_bwd(dh2, x2, g_ffn, dx3, name="rms_ffn_bwd", after=[rs_w1[4]])
    (dwo,) = _matmul(hn, dx2b, name="mm_dwo", ta=True, M=D, N=D, K=S, outs=[sds((D, D), BF16)])
    sib_wo = rs_sibling(dwo.reshape(NCHIP, D // NCHIP, D), "w_o")
    (dhn,) = _matmul(dx2b, w_o_full, name="mm_dhn", tb=True, M=S, N=D, K=D, outs=[sds((S, D), F32)],
                     after=[sib_wo[4]])
    rs_wo = rs_chips(sib_wo, dhn, "w_o")
    dproj, dkv, dws, dbs8, dlng, dlnb, dcw8, dgh = _mix_bwd(
        dhn, heads, proj, ycv, kv, ws3, bs_t, ln_v_g, ln_v_b, conv_full, g_head, rs_wo[4], name="mix_bwd")
    (dwin_t,) = _matmul(dproj, h, name="mm_dwin", ta=True, M=DIN, N=D, K=S, tm=DIN // 2, outs=[sds((DIN, D), BF16)])
    sib_win = rs_sibling(dwin_t.reshape(NCHIP, din4, D), "w_in")
    (dwkv,) = _matmul(mem_n, dkv, name="mm_dwkv", ta=True, M=D, N=2 * DM, K=NMEM, outs=[sds((D, 2 * DM), BF16)],
                      after=[sib_win[4]])
    sib_wkv = rs_sibling(dwkv.reshape(NCHIP, D // NCHIP, 2 * DM), "w_kv")
    (dh,) = _matmul(dproj, w_in_t, name="mm_dh", M=S, N=D, K=DIN, tk=DIN, outs=[sds((S, D), F32)],
                    after=[sib_wkv[4]])
    rs_win = rs_chips(sib_win, dh, "w_in")
    rs_wkv = rs_chips(sib_wkv, rs_win[4], "w_kv")
    dx, dg_mix = _rms_bwd(dh, x2d, g_mix, dx2, name="rms_mix_bwd", want_bf=False, after=[rs_wkv[4]])
    (dmem_n,) = _matmul(dkv, w_kv_full, name="mm_dmem", tb=True, M=NMEM, N=D, K=2 * DM, outs=[sds((NMEM, D), F32)],
                        after=[dx])
    (dg_mem,) = _rms_bwd(dmem_n, mem2d, g_mem, None, name="rms_mem_bwd", want_dx=False)
    half_w2 = rs_end(rs_w2, dg_mem, "w_ffn2")
    half_w1 = rs_end(rs_w1, half_w2, "w_ffn1")
    rs_finish([4, 3], [half_w2, half_w1], "ffn")

    loss = lax.psum(loss11[0, 0], ("x", "y", "c"))

    half_wo = rs_end(rs_wo, big_out["w_ffn1"][1], "w_o")
    half_win = rs_end(rs_win, half_wo, "w_in")
    half_wkv = rs_end(rs_wkv, half_win, "w_kv")
    rs_finish([0, 1, 2], [half_win, half_wkv, half_wo], "rest")

    small_names = ["g_mix", "ln_v_g", "ln_v_b", "w_s", "b_s", "conv_w", "g_mem", "g_head", "g_ffn", "g_final"]
    small_part = [dg_mix, dlng, dlnb, dws, dbs8[:, 0, :], dcw8[:3], dg_mem, dgh, dg_ffn, dg_final]
    small_shapes = [(1, D), (1, DS), (1, DS), (NSH, CHUNK, CHUNK), (NSH, CHUNK), (3, DC), (1, D), (1, D), (1, D), (1, D)]
    total = _allreduce_small(_pack(small_part), big_out["w_o"][1], name="allreduce_small")
    small_g = _unpack(total, small_shapes)
    small_g[5] = lax.dynamic_slice(small_g[5], (0, shard * dcv4), (3, dcv4))
    small_w = [g_mix, ln_v_g, ln_v_b, ws3, bs2, conv_w[0], g_mem, g_head, g_ffn, g_final2]
    small_m = [m_g_mix, m_ln_v_g, m_ln_v_b, m_w_s[0], m_b_s[0], m_conv_w[0], m_g_mem, m_g_head, m_g_ffn,
               m_g_final.reshape(1, D)]
    small_v = [v_g_mix, v_ln_v_g, v_ln_v_b, v_w_s[0], v_b_s[0], v_conv_w[0], v_g_mem, v_g_head, v_g_ffn,
               v_g_final.reshape(1, D)]
    s_delta, s_m, s_v = _adamw_small(small_w, small_g, small_m, small_v, name="adamw_small")
    small_out = {nm: (g, d, mn, vn) for nm, g, d, mn, vn in zip(small_names, small_g, s_delta, s_m, s_v)}

    order = ["g_mix", "w_in", "ln_v_g", "ln_v_b", "w_s", "b_s", "conv_w", "g_mem", "w_kv", "g_head", "w_o",
             "g_ffn", "w_ffn1", "w_ffn2", "g_final"]
    like = dict(g_mix=g_mix, w_in=w_in, ln_v_g=ln_v_g, ln_v_b=ln_v_b, w_s=w_s, b_s=b_s, conv_w=conv_w, g_mem=g_mem,
                w_kv=w_kv, g_head=g_head, w_o=w_o, g_ffn=g_ffn, w_ffn1=w_ffn1, w_ffn2=w_ffn2, g_final=g_final)
    res = {**big_out, **small_out}
    res["w_in"] = [a.T for a in res["w_in"]]
    outs = [loss, dx[None]]
    for k in range(4):
        outs += [res[nm][k].reshape(like[nm].shape) for nm in order]
    return tuple(outs)
```

```python
import math

import jax
import jax.numpy as jnp
from jax import lax
from jax.experimental import pallas as pl
from jax.experimental.pallas import tpu as pltpu

F32 = jnp.float32
BF16 = jnp.bfloat16
MESH = pl.DeviceIdType.MESH

D = 2048
S = 2048
HD = 128
NH = D // HD
NMH = 4
NSH = (NH - NMH) // 2
NCH = NH - NMH - NSH
DS = NSH * HD
DC = NCH * HD
DM = NMH * HD
DIN = 2 * DS + 3 * DC + DM
CHUNK = 128
NMEM = 256
DFF = 4 * D
EPS = 1e-6
NCHIP = 4
SCALE = HD ** -0.5

ADAM_LR = 0.001
ADAM_B1 = 0.9
ADAM_B2 = 0.999
ADAM_EPS = 1e-08
ADAM_WD = 0.01
ADAM_STEP = 10

TR_EW = 256
TR_MIX = 256
TM = 512
TN = 1024
TK = 2048
VMEM_MB = 56
HALO = 8


def _pick(n, target, q=128):
    best = None
    for t in range(q, min(n, target) + 1, q):
        if n % t == 0:
            best = t
    return n if best is None else best


def _pick_rows(n, q):
    below = _pick(n, TR_EW, q)
    if 2 * below >= TR_EW:
        return below
    above = [t for t in range(TR_EW, min(n, 4 * TR_EW) + 1, q) if n % t == 0]
    return above[0] if above else below


def _cp(sem=None, vmem_mb=None, **kw):
    d = dict(kw)
    if sem is not None:
        d["dimension_semantics"] = sem
    if vmem_mb is not None:
        d["vmem_limit_bytes"] = vmem_mb << 20
    return pltpu.CompilerParams(**d)


def _gelu(x):
    z = 0.7978845608028654 * (x + 0.044715 * (x * x * x))
    return 0.5 * x * (1.0 + jnp.tanh(z))


def _gelu_grad(x):
    x2 = x * x
    t = jnp.tanh(0.7978845608028654 * (x + 0.044715 * (x2 * x)))
    return 0.5 * (1.0 + t) + 0.5 * x * (1.0 - t * t) * (0.7978845608028654 * (1.0 + 3.0 * 0.044715 * x2))


def _matmul(a, b, *, name, ta=False, tb=False, M, N, K, tm=None, tn=None, tk=None, outs, epi=None,
            extras=(), b_spec=None, out_specs=None, after=()):
    n_after = len(after)
    tm = _pick(M, TM if tm is None else tm, 8)
    tn = _pick(N, TN if tn is None else tn)
    tk = _pick(K, TK if tk is None else tk)
    nk = K // tk
    grid = (N // tn, M // tm, nk)
    a_spec = (pl.BlockSpec((tk, tm), lambda j, i, k: (k, i)) if ta
              else pl.BlockSpec((tm, tk), lambda j, i, k: (i, k)))
    if b_spec is None:
        b_spec = (pl.BlockSpec((tn, tk), lambda j, i, k: (j, k)) if tb
                  else pl.BlockSpec((tk, tn), lambda j, i, k: (k, j)))
    else:
        b_spec = b_spec(tn, tk)
    if out_specs is None:
        out_specs = [pl.BlockSpec((tm, tn), lambda j, i, k: (i, j)) for _ in outs]
    else:
        out_specs = out_specs(tm, tn)
    dn = (((0 if ta else 1,), (1 if tb else 0,)), ((), ()))
    n_ex, n_out = len(extras), len(outs)

    def body(*refs):
        a_ref, b_ref = refs[0], refs[1]
        ex = refs[2:2 + n_ex]
        first_out = 2 + n_ex + n_after
        o = refs[first_out:first_out + n_out]
        acc = refs[first_out + n_out:]
        part = lax.dot_general(a_ref[...].astype(BF16), b_ref[...].astype(BF16), dn,
                               preferred_element_type=F32)

        def finish(val):
            res = (val,) if epi is None else epi(val, *[e[...] for e in ex])
            for r, o_ref in zip(res, o):
                o_ref[...] = r.astype(o_ref.dtype)

        if nk == 1:
            finish(part)
        else:
            k = pl.program_id(2)

            @pl.when(k == 0)
            def _():
                acc[0][...] = part

            @pl.when(k > 0)
            def _():
                acc[0][...] += part

            @pl.when(k == nk - 1)
            def _():
                finish(acc[0][...])

    return pl.pallas_call(
        body, name=name, grid=grid,
        in_specs=[a_spec, b_spec] + [sp(tm, tn) for _, sp in extras] + [ANY] * n_after,
        out_specs=out_specs, out_shape=outs,
        scratch_shapes=([pltpu.VMEM((tm, tn), F32)] if nk > 1 else []),
        compiler_params=_cp(("parallel", "parallel", "arbitrary"), VMEM_MB),
    )(a, b, *[arr for arr, _ in extras], *after)


def _tile_spec():
    return lambda tm, tn: pl.BlockSpec((tm, tn), lambda j, i, k: (i, j))


def _cast_into_slot(w, slot, after, *, name):
    R, C = w.shape
    tr = _pick_rows(R, 16)

    def body(s_ref, w_ref, _after_ref, o_ref):
        o_ref[...] = w_ref[...].astype(BF16)

    return pl.pallas_call(
        body, name=name,
        grid_spec=pltpu.PrefetchScalarGridSpec(
            num_scalar_prefetch=1, grid=(R // tr,),
            in_specs=[pl.BlockSpec((tr, C), lambda i, s: (i, 0)), ANY],
            out_specs=pl.BlockSpec((None, tr, C), lambda i, s: (s[0], i, 0))),
        out_shape=jax.ShapeDtypeStruct((NCHIP, R, C), BF16),
        compiler_params=_cp(("parallel",), VMEM_MB),
    )(slot, w, after)


def _rms_fwd(x, g, *, name, after=()):
    R, C = x.shape
    tr = _pick(R, TR_EW, 16)
    n_after = len(after)

    def body(x_ref, g_ref, *rest):
        o_ref = rest[n_after]
        xv = x_ref[...]
        r = lax.rsqrt(jnp.mean(xv * xv, axis=-1, keepdims=True) + EPS)
        o_ref[...] = ((xv * r) * g_ref[...]).astype(BF16)

    return pl.pallas_call(
        body, name=name, grid=(R // tr,),
        in_specs=[pl.BlockSpec((tr, C), lambda i: (i, 0)), pl.BlockSpec((1, C), lambda i: (0, 0))] + [ANY] * n_after,
        out_specs=pl.BlockSpec((tr, C), lambda i: (i, 0)),
        out_shape=jax.ShapeDtypeStruct((R, C), BF16),
        compiler_params=_cp(("parallel",), VMEM_MB),
    )(x, g, *after)


def _rms_bwd(dh, x, g, dres, *, name, want_dx=True, want_bf=True, after=()):
    R, C = x.shape
    tr = _pick(R, TR_EW, 16)
    has_res = dres is not None
    row = pl.BlockSpec((tr, C), lambda i: (i, 0))
    vec = pl.BlockSpec((1, C), lambda i: (0, 0))

    def body(*refs):
        dh_ref, x_ref, g_ref = refs[:3]
        pos = 3
        dres_ref = None
        if has_res:
            dres_ref = refs[pos]
            pos += 1
        outs = refs[pos + len(after):]
        i = pl.program_id(0)
        xv = x_ref[...]
        r = lax.rsqrt(jnp.mean(xv * xv, axis=-1, keepdims=True) + EPS)
        xh = xv * r
        dhv = dh_ref[...]
        dg_ref = outs[-1]
        dgp = jnp.sum(dhv * xh, axis=0, keepdims=True)

        @pl.when(i == 0)
        def _():
            dg_ref[...] = dgp

        @pl.when(i > 0)
        def _():
            dg_ref[...] += dgp

        if want_dx:
            t = dhv * g_ref[...]
            dx = r * (t - xh * jnp.mean(t * xh, axis=-1, keepdims=True))
            if has_res:
                dx = dx + dres_ref[...]
            outs[0][...] = dx
            if want_bf:
                outs[1][...] = dx.astype(BF16)

    in_specs = [row, row, vec] + ([row] if has_res else []) + [ANY] * len(after)
    out_specs, out_shape = [], []
    if want_dx:
        out_specs.append(row)
        out_shape.append(jax.ShapeDtypeStruct((R, C), F32))
        if want_bf:
            out_specs.append(row)
            out_shape.append(jax.ShapeDtypeStruct((R, C), BF16))
    out_specs.append(vec)
    out_shape.append(jax.ShapeDtypeStruct((1, C), F32))
    args = [dh, x, g] + ([dres] if has_res else []) + list(after)
    return pl.pallas_call(
        body, name=name, grid=(R // tr,), in_specs=in_specs, out_specs=out_specs, out_shape=out_shape,
        compiler_params=_cp(("arbitrary",), VMEM_MB),
    )(*args)


def _loss_bwd(x3, g, tgt, *, name):
    R, C = x3.shape
    tr = _pick(R, TR_EW, 16)
    n = R // tr
    row = pl.BlockSpec((tr, C), lambda i: (i, 0))
    vec = pl.BlockSpec((1, C), lambda i: (0, 0))

    def body(x_ref, g_ref, t_ref, dx_ref, dxb_ref, dg_ref, loss_ref, acc_ref):
        i = pl.program_id(0)
        xv = x_ref[...]
        gv = g_ref[...]
        r = lax.rsqrt(jnp.mean(xv * xv, axis=-1, keepdims=True) + EPS)
        xh = xv * r
        e = xh * gv - t_ref[...]
        dy = e * (1.0 / C)
        sq = jnp.sum(e * e, axis=0, keepdims=True)
        dgp = jnp.sum(dy * xh, axis=0, keepdims=True)

        @pl.when(i == 0)
        def _():
            acc_ref[...] = sq
            dg_ref[...] = dgp

        @pl.when(i > 0)
        def _():
            acc_ref[...] += sq
            dg_ref[...] += dgp

        t = dy * gv
        dx = r * (t - xh * jnp.mean(t * xh, axis=-1, keepdims=True))
        dx_ref[...] = dx
        dxb_ref[...] = dx.astype(BF16)

        @pl.when(i == n - 1)
        def _():
            loss_ref[...] = jnp.sum(acc_ref[...], axis=-1, keepdims=True) * (0.5 / C)

    return pl.pallas_call(
        body, name=name, grid=(n,),
        in_specs=[row, vec, row],
        out_specs=[row, row, vec, pl.BlockSpec((1, 1), lambda i: (0, 0))],
        out_shape=[jax.ShapeDtypeStruct((R, C), F32), jax.ShapeDtypeStruct((R, C), BF16),
                   jax.ShapeDtypeStruct((1, C), F32), jax.ShapeDtypeStruct((1, 1), F32)],
        scratch_shapes=[pltpu.VMEM((1, C), F32)],
        compiler_params=_cp(("arbitrary",), VMEM_MB),
    )(x3, g, tgt)


def _offsets():
    u0 = 0
    v0 = DS
    b0 = 2 * DS
    c0 = b0 + DC
    x0 = c0 + DC
    q0 = x0 + DC
    return u0, v0, b0, c0, x0, q0


def _tri_mask(lower):
    r = lax.broadcasted_iota(jnp.int32, (CHUNK, CHUNK), 0)
    c = lax.broadcasted_iota(jnp.int32, (CHUNK, CHUNK), 1)
    return (r >= c) if lower else (c >= r)


def _layer_norm_stats(vg):
    mu = jnp.mean(vg, axis=-1, keepdims=True)
    vc = vg - mu
    rstd = lax.rsqrt(jnp.mean(vc * vc, axis=-1, keepdims=True) + EPS)
    return vc * rstd, rstd


def _softmax_rows(qh, kh):
    s = lax.dot_general(qh, kh, (((1,), (1,)), ((), ())), preferred_element_type=F32)
    m = jnp.max(s, axis=-1, keepdims=True)
    e = jnp.exp(s - m)
    return e / jnp.sum(e, axis=-1, keepdims=True)


def _mix_fwd(proj, kv, w_s, bs_t, ln_g, ln_b, conv_w, g_head, *, name):
    assert DS == DC
    tr = _pick(S, TR_MIX, CHUNK)
    n = S // tr
    nck = tr // CHUNK
    u0, v0, b0, c0, x0, q0 = _offsets()
    hb = tr // HALO

    def body(p_ref, cprev_ref, xprev_ref, kv_ref, ws_ref, bst_ref, lng_ref, lnb_ref, cw_ref, gh_ref,
             heads_ref, hn_ref, ycv_ref, buf_ref):
        i = pl.program_id(0)

        def emit(col, val):
            rs = lax.rsqrt(jnp.mean(val * val, axis=-1, keepdims=True) + EPS)
            heads_ref[:, col:col + HD] = val
            hn_ref[:, col:col + HD] = ((val * rs) * gh_ref[:, col:col + HD]).astype(BF16)

        vhat, _ = _layer_norm_stats(_gelu(p_ref[:, v0:v0 + DS]))
        vnb = (vhat * lng_ref[...] + lnb_ref[...]).astype(BF16)
        low = _tri_mask(True)
        for h in range(NSH):
            wt = jnp.where(low, ws_ref[h], 0.0).astype(BF16)
            bcol = bst_ref[:, h:h + 1]
            parts = []
            for c in range(nck):
                blk = vnb[c * CHUNK:(c + 1) * CHUNK, h * HD:(h + 1) * HD]
                parts.append(jnp.dot(wt, blk, preferred_element_type=F32) + bcol)
            mixed = parts[0] if nck == 1 else jnp.concatenate(parts, axis=0)
            emit(h * HD, _gelu(p_ref[:, u0 + h * HD:u0 + (h + 1) * HD]) * mixed)

        xc = p_ref[:, c0:c0 + DC] * p_ref[:, x0:x0 + DC]
        prev = cprev_ref[...] * xprev_ref[...]
        buf_ref[0:HALO, :] = jnp.where(i > 0, prev, 0.0)
        buf_ref[HALO:HALO + tr, :] = xc
        y = (cw_ref[2:3, :] * xc + cw_ref[1:2, :] * buf_ref[HALO - 1:HALO - 1 + tr, :]
             + cw_ref[0:1, :] * buf_ref[HALO - 2:HALO - 2 + tr, :])
        ycv_ref[...] = y
        cout = p_ref[:, b0:b0 + DC] * y
        for h in range(NCH):
            emit(DS + h * HD, cout[:, h * HD:(h + 1) * HD])

        for h in range(NMH):
            qh = (p_ref[:, q0 + h * HD:q0 + (h + 1) * HD] * SCALE).astype(BF16)
            kh = kv_ref[:, h * HD:(h + 1) * HD].astype(BF16)
            vh = kv_ref[:, DM + h * HD:DM + (h + 1) * HD].astype(BF16)
            p = _softmax_rows(qh, kh)
            emit(DS + DC + h * HD, jnp.dot(p.astype(BF16), vh, preferred_element_type=F32))

    full = lambda shape: pl.BlockSpec(shape, lambda i: (0,) * len(shape))
    halo_c = pl.BlockSpec((HALO, DC), lambda i: (jnp.maximum(i * hb - 1, 0), c0 // DC))
    halo_x = pl.BlockSpec((HALO, DC), lambda i: (jnp.maximum(i * hb - 1, 0), x0 // DC))
    return pl.pallas_call(
        body, name=name, grid=(n,),
        in_specs=[pl.BlockSpec((tr, DIN), lambda i: (i, 0)), halo_c, halo_x,
                  full((NMEM, 2 * DM)), full((NSH, CHUNK, CHUNK)), full((CHUNK, NSH)),
                  full((1, DS)), full((1, DS)), full((3, DC)), full((1, D))],
        out_specs=[pl.BlockSpec((tr, D), lambda i: (i, 0)), pl.BlockSpec((tr, D), lambda i: (i, 0)),
                   pl.BlockSpec((tr, DC), lambda i: (i, 0))],
        out_shape=[jax.ShapeDtypeStruct((S, D), F32), jax.ShapeDtypeStruct((S, D), BF16),
                   jax.ShapeDtypeStruct((S, DC), F32)],
        scratch_shapes=[pltpu.VMEM((tr + HALO, DC), F32)],
        compiler_params=_cp(("parallel",), VMEM_MB),
    )(proj, proj, proj, kv, w_s, bs_t, ln_g, ln_b, conv_w, g_head)


def _mix_bwd(dhn, heads, proj, ycv, kv, w_s, bs_t, ln_g, ln_b, conv_w, g_head, after, *, name):
    assert DS == DC
    tr = _pick(S, TR_MIX, CHUNK)
    n = S // tr
    nck = tr // CHUNK
    u0, v0, b0, c0, x0, q0 = _offsets()
    hb = tr // HALO
    last_hb = S // HALO - 1

    def body(dhn_ref, heads_ref, p_ref, ycv_ref, dhn_nx_ref, heads_nx_ref, b_nx_ref, kv_ref, ws_ref, bst_ref,
             lng_ref, lnb_ref, cw_ref, gh_ref, _after_ref,
             dp_ref, dkv_ref, dws_ref, dbs_ref, dlng_ref, dlnb_ref, dcw_ref, dgh_ref, buf_ref, dvn_ref):
        i = pl.program_id(0)

        @pl.when(i == 0)
        def _():
            dkv_ref[...] = jnp.zeros_like(dkv_ref)
            dws_ref[...] = jnp.zeros_like(dws_ref)
            dbs_ref[...] = jnp.zeros_like(dbs_ref)
            dlng_ref[...] = jnp.zeros_like(dlng_ref)
            dlnb_ref[...] = jnp.zeros_like(dlnb_ref)
            dcw_ref[...] = jnp.zeros_like(dcw_ref)
            dgh_ref[...] = jnp.zeros_like(dgh_ref)

        def head_bwd(a, dn, gh):
            rs = lax.rsqrt(jnp.mean(a * a, axis=-1, keepdims=True) + EPS)
            ah = a * rs
            t = dn * gh
            return rs * (t - ah * jnp.mean(t * ah, axis=-1, keepdims=True)), jnp.sum(dn * ah, axis=0, keepdims=True)

        def head_grad(col):
            da, dg = head_bwd(heads_ref[:, col:col + HD], dhn_ref[:, col:col + HD], gh_ref[:, col:col + HD])
            dgh_ref[:, col:col + HD] += dg
            return da

        v = p_ref[:, v0:v0 + DS]
        vhat, rstd = _layer_norm_stats(_gelu(v))
        vnb = (vhat * lng_ref[...] + lnb_ref[...]).astype(BF16)
        low = _tri_mask(True)
        ones = jnp.ones((HALO, HD), BF16)
        for h in range(NSH):
            w_h = ws_ref[h]
            wt = jnp.where(low, w_h, 0.0).astype(BF16)
            bcol = bst_ref[:, h:h + 1]
            da = head_grad(h * HD)
            u = p_ref[:, u0 + h * HD:u0 + (h + 1) * HD]
            ug = _gelu(u)
            dws = jnp.zeros((CHUNK, CHUNK), F32)
            dbs = jnp.zeros((HALO, CHUNK), F32)
            mixed_parts = []
            for c in range(nck):
                rows = slice(c * CHUNK, (c + 1) * CHUNK)
                blk = vnb[rows, h * HD:(h + 1) * HD]
                mixed_parts.append(jnp.dot(wt, blk, preferred_element_type=F32) + bcol)
                dmb = (da[rows] * ug[rows]).astype(BF16)
                dws = dws + lax.dot_general(dmb, blk, (((1,), (1,)), ((), ())), preferred_element_type=F32)
                dbs = dbs + lax.dot_general(ones, dmb, (((1,), (1,)), ((), ())), preferred_element_type=F32)
                dvn_ref[c * CHUNK:(c + 1) * CHUNK, h * HD:(h + 1) * HD] = lax.dot_general(
                    wt, dmb, (((0,), (0,)), ((), ())), preferred_element_type=F32)
            mixed = mixed_parts[0] if nck == 1 else jnp.concatenate(mixed_parts, axis=0)
            dp_ref[:, u0 + h * HD:u0 + (h + 1) * HD] = ((da * mixed) * _gelu_grad(u)).astype(BF16)
            dws_ref[h] += jnp.where(low, dws, 0.0)
            dbs_ref[h] += dbs
        dvn = dvn_ref[...]
        dlng_ref[...] += jnp.sum(dvn * vhat, axis=0, keepdims=True)
        dlnb_ref[...] += jnp.sum(dvn, axis=0, keepdims=True)
        dvh = dvn * lng_ref[...]
        dvg = rstd * (dvh - jnp.mean(dvh, axis=-1, keepdims=True)
                      - vhat * jnp.mean(dvh * vhat, axis=-1, keepdims=True))
        dp_ref[:, v0:v0 + DS] = (dvg * _gelu_grad(v)).astype(BF16)

        dc = jnp.concatenate([head_grad(DS + h * HD) for h in range(NCH)], axis=1)
        dc_nx = jnp.concatenate(
            [head_bwd(heads_nx_ref[:, h * HD:(h + 1) * HD], dhn_nx_ref[:, h * HD:(h + 1) * HD],
                      gh_ref[:, DS + h * HD:DS + (h + 1) * HD])[0] for h in range(NCH)], axis=1)
        bg = p_ref[:, b0:b0 + DC]
        cg = p_ref[:, c0:c0 + DC]
        xin = p_ref[:, x0:x0 + DC]
        dp_ref[:, b0:b0 + DC] = (dc * ycv_ref[...]).astype(BF16)
        dyv = dc * bg
        buf_ref[0:tr, :] = dyv
        buf_ref[tr:tr + HALO, :] = jnp.where(i < n - 1, dc_nx * b_nx_ref[...], 0.0)
        sh1 = buf_ref[1:1 + tr, :]
        sh0 = buf_ref[2:2 + tr, :]
        dxc = cw_ref[2:3, :] * dyv + cw_ref[1:2, :] * sh1 + cw_ref[0:1, :] * sh0
        xc = cg * xin
        dp_ref[:, c0:c0 + DC] = (dxc * xin).astype(BF16)
        dp_ref[:, x0:x0 + DC] = (dxc * cg).astype(BF16)
        dcw_ref[0:1, :] += jnp.sum(sh0 * xc, axis=0, keepdims=True)
        dcw_ref[1:2, :] += jnp.sum(sh1 * xc, axis=0, keepdims=True)
        dcw_ref[2:3, :] += jnp.sum(dyv * xc, axis=0, keepdims=True)

        for h in range(NMH):
            do = head_grad(DS + DC + h * HD).astype(BF16)
            qh = (p_ref[:, q0 + h * HD:q0 + (h + 1) * HD] * SCALE).astype(BF16)
            kh = kv_ref[:, h * HD:(h + 1) * HD].astype(BF16)
            vh = kv_ref[:, DM + h * HD:DM + (h + 1) * HD].astype(BF16)
            p = _softmax_rows(qh, kh)
            dpr = lax.dot_general(do, vh, (((1,), (1,)), ((), ())), preferred_element_type=F32)
            ds = (p * (dpr - jnp.sum(dpr * p, axis=-1, keepdims=True))).astype(BF16)
            dp_ref[:, q0 + h * HD:q0 + (h + 1) * HD] = (
                jnp.dot(ds, kh, preferred_element_type=F32) * SCALE).astype(BF16)
            dkv_ref[:, h * HD:(h + 1) * HD] += lax.dot_general(
                ds, qh, (((0,), (0,)), ((), ())), preferred_element_type=F32)
            dkv_ref[:, DM + h * HD:DM + (h + 1) * HD] += lax.dot_general(
                p.astype(BF16), do, (((0,), (0,)), ((), ())), preferred_element_type=F32)

    full = lambda shape: pl.BlockSpec(shape, lambda i: (0,) * len(shape))
    row = lambda c: pl.BlockSpec((tr, c), lambda i: (i, 0))
    nxt = lambda col: pl.BlockSpec((HALO, DC), lambda i: (jnp.minimum((i + 1) * hb, last_hb), col))
    return pl.pallas_call(
        body, name=name, grid=(n,),
        in_specs=[row(D), row(D), row(DIN), row(DC), nxt(DS // DC), nxt(DS // DC), nxt(b0 // DC),
                  full((NMEM, 2 * DM)), full((NSH, CHUNK, CHUNK)), full((CHUNK, NSH)),
                  full((1, DS)), full((1, DS)), full((3, DC)), full((1, D)), ANY],
        out_specs=[row(DIN), full((NMEM, 2 * DM)), full((NSH, CHUNK, CHUNK)), full((NSH, HALO, CHUNK)),
                   full((1, DS)), full((1, DS)), full((HALO, DC)), full((1, D))],
        out_shape=[jax.ShapeDtypeStruct((S, DIN), BF16), jax.ShapeDtypeStruct((NMEM, 2 * DM), F32),
                   jax.ShapeDtypeStruct((NSH, CHUNK, CHUNK), F32), jax.ShapeDtypeStruct((NSH, HALO, CHUNK), F32),
                   jax.ShapeDtypeStruct((1, DS), F32), jax.ShapeDtypeStruct((1, DS), F32),
                   jax.ShapeDtypeStruct((HALO, DC), F32), jax.ShapeDtypeStruct((1, D), F32)],
        scratch_shapes=[pltpu.VMEM((tr + HALO, DC), F32), pltpu.VMEM((tr, DS), F32)],
        compiler_params=_cp(("arbitrary",), VMEM_MB),
    )(dhn, heads, proj, ycv, dhn, heads, proj, kv, w_s, bs_t, ln_g, ln_b, conv_w, g_head, after)


def _place():
    x, y, c = lax.axis_index("x"), lax.axis_index("y"), lax.axis_index("c")
    chips = [(1 - x, y), (x, 1 - y), (1 - x, 1 - y)]
    return x, y, c, chips


ANY = pl.BlockSpec(memory_space=pl.ANY)


HBM = pl.BlockSpec(memory_space=pltpu.HBM)
SEM = pl.BlockSpec(memory_space=pltpu.SEMAPHORE)
EFFECT = pltpu.SideEffectType.DATAFLOW_SIDE_EFFECTING
N_PEER_CHIPS = 3


def _in_hbm(a):
    return pltpu.with_memory_space_constraint(a, pltpu.HBM)


def _allgather_start(bufs, after, *, name):
    nw = len(bufs)

    def body(*refs):
        ins, send, recv = refs[:nw], refs[nw + 1:2 * nw + 1], refs[2 * nw + 1:3 * nw + 1]
        token = refs[4 * nw + 1]
        x, y, c, chips = _place()
        s = 2 * x + y
        for w in range(nw):
            hr = bufs[w].shape[1] // 2
            rows = ins[w].at[s, pl.ds(c * hr, hr)]
            for cx, cy in chips:
                pltpu.make_async_remote_copy(src_ref=rows, dst_ref=rows, send_sem=send[w], recv_sem=recv[w],
                                             device_id=(cx, cy, c), device_id_type=MESH).start()
        token[...] = jnp.zeros_like(token)

    res = pl.pallas_call(
        body, name=name,
        in_specs=[HBM] * nw + [ANY],
        out_specs=[SEM] * (2 * nw) + [HBM] * nw + [pl.BlockSpec(memory_space=pltpu.VMEM)],
        out_shape=[pltpu.SemaphoreType.DMA(())] * (2 * nw) + [pltpu.HBM(a.shape, a.dtype) for a in bufs]
        + [jax.ShapeDtypeStruct((8, 128), F32)],
        input_output_aliases={w: 2 * nw + w for w in range(nw)},
        compiler_params=pltpu.CompilerParams(has_side_effects=EFFECT),
    )(*[_in_hbm(a) for a in bufs], after)
    return res[:nw], res[nw:2 * nw], res[2 * nw:3 * nw], res[3 * nw]


def _scatter_start(parts, bufs, *, name):
    nw = len(parts)

    def body(*refs):
        src, dst = refs[:nw], refs[nw:2 * nw]
        send, recv = refs[2 * nw:3 * nw], refs[3 * nw:4 * nw]
        token = refs[6 * nw]
        x, y, c, chips = _place()
        s = 2 * x + y
        for w in range(nw):
            for cx, cy in chips:
                pltpu.make_async_remote_copy(src_ref=src[w].at[2 * cx + cy], dst_ref=dst[w].at[s], send_sem=send[w],
                                             recv_sem=recv[w], device_id=(cx, cy, c), device_id_type=MESH).start()
        token[...] = jnp.zeros_like(token)

    res = pl.pallas_call(
        body, name=name,
        in_specs=[HBM] * (2 * nw),
        out_specs=[SEM] * (2 * nw) + [HBM] * (2 * nw) + [pl.BlockSpec(memory_space=pltpu.VMEM)],
        out_shape=[pltpu.SemaphoreType.DMA(())] * (2 * nw) + [pltpu.HBM(a.shape, a.dtype) for a in parts + bufs]
        + [jax.ShapeDtypeStruct((8, 128), F32)],
        input_output_aliases={k: 2 * nw + k for k in range(2 * nw)},
        compiler_params=pltpu.CompilerParams(has_side_effects=EFFECT),
    )(*[_in_hbm(a) for a in parts + bufs])
    return res[:nw], res[nw:2 * nw], res[2 * nw:3 * nw], res[3 * nw:4 * nw], res[4 * nw]


def _sibling_start(srcs, *, name):
    nw = len(srcs)
    lands = [lax.empty((a.shape[0], a.shape[1] // 2, a.shape[2]), a.dtype) for a in srcs]

    def body(*refs):
        src, land = refs[:nw], refs[nw:2 * nw]
        send, recv = refs[2 * nw:3 * nw], refs[3 * nw:4 * nw]
        token = refs[6 * nw]
        x, y, c, _ = _place()
        for w in range(nw):
            hr = srcs[w].shape[1] // 2
            rows = src[w].at[:, pl.ds((1 - c) * hr, hr)]
            pltpu.make_async_remote_copy(src_ref=rows, dst_ref=land[w], send_sem=send[w], recv_sem=recv[w],
                                         device_id=(x, y, 1 - c), device_id_type=MESH).start()
        token[...] = jnp.zeros_like(token)

    res = pl.pallas_call(
        body, name=name,
        in_specs=[HBM] * (2 * nw),
        out_specs=[SEM] * (2 * nw) + [HBM] * (2 * nw) + [pl.BlockSpec(memory_space=pltpu.VMEM)],
        out_shape=[pltpu.SemaphoreType.DMA(())] * (2 * nw) + [pltpu.HBM(a.shape, a.dtype) for a in srcs + lands]
        + [jax.ShapeDtypeStruct((8, 128), F32)],
        input_output_aliases={k: 2 * nw + k for k in range(2 * nw)},
        compiler_params=pltpu.CompilerParams(has_side_effects=EFFECT),
    )(*[_in_hbm(a) for a in srcs + lands])
    return res[:nw], res[nw:2 * nw], res[2 * nw:3 * nw], res[3 * nw:4 * nw], res[4 * nw]


def _transfer_wait(sends, recvs, thru, sizes, after, *, name):
    n = len(sends)
    flat = [a for group in thru for a in group]

    def body(*refs):
        bufs = refs[:len(flat)]
        send = refs[len(flat):len(flat) + n]
        recv = refs[len(flat) + n:len(flat) + 2 * n]
        token = refs[2 * len(flat) + 2 * n + 1]
        token[...] = jnp.zeros_like(token)
        x, y, c, _ = _place()
        pos = 0
        for k in range(n):
            slots, rows = sizes[k]
            region = bufs[pos].at[pl.ds(0, slots), pl.ds(0, rows)]
            pos += len(thru[k])
            cp = pltpu.make_async_remote_copy(src_ref=region, dst_ref=region, send_sem=send[k], recv_sem=recv[k],
                                              device_id=(x, y, 1 - c), device_id_type=MESH)
            cp.wait_send()
            cp.wait_recv()

    res = pl.pallas_call(
        body, name=name,
        in_specs=[HBM] * len(flat) + [SEM] * (2 * n) + [pl.BlockSpec(memory_space=pl.ANY)],
        out_specs=[HBM] * len(flat) + [pl.BlockSpec(memory_space=pltpu.VMEM)],
        out_shape=[pltpu.HBM(a.shape, a.dtype) for a in flat] + [jax.ShapeDtypeStruct((8, 128), F32)],
        input_output_aliases={k: k for k in range(len(flat))},
        compiler_params=pltpu.CompilerParams(has_side_effects=EFFECT),
    )(*flat, *sends, *recvs, after)
    out, pos = [], 0
    for group in thru:
        out.append(res[pos:pos + len(group)])
        pos += len(group)
    return out, res[len(flat)]


def _forward_to_sibling(bufs, after, *, name):
    nw = len(bufs)

    def body(*refs):
        outs = refs[nw + 1:2 * nw + 1]
        send, recv = refs[2 * nw + 1:]
        x, y, c, chips = _place()
        me, sibling = (x, y, c), (x, y, 1 - c)

        def d2d(w, j, which, to):
            cx, cy = chips[j]
            hr = bufs[w].shape[1] // 2
            rows = outs[w].at[2 * cx + cy, pl.ds(which * hr, hr)]
            return pltpu.make_async_remote_copy(
                src_ref=rows, dst_ref=rows, send_sem=send.at[N_PEER_CHIPS * w + j],
                recv_sem=recv.at[N_PEER_CHIPS * w + j], device_id=to, device_id_type=MESH)

        passed = [d2d(w, j, c, sibling) for w in range(nw) for j in range(N_PEER_CHIPS)]
        for cp in passed:
            cp.start()
        for w in range(nw):
            for j in range(N_PEER_CHIPS):
                d2d(w, j, 1 - c, me).wait_recv()
        for cp in passed:
            cp.wait_send()

    return pl.pallas_call(
        body, name=name,
        in_specs=[ANY] * (nw + 1), out_specs=[ANY] * nw,
        out_shape=[jax.ShapeDtypeStruct(a.shape, a.dtype) for a in bufs],
        input_output_aliases={w: w for w in range(nw)},
        scratch_shapes=[pltpu.SemaphoreType.DMA((N_PEER_CHIPS * nw,)), pltpu.SemaphoreType.DMA((N_PEER_CHIPS * nw,))],
    )(*bufs, after)


def _sibling_exchange(arrs, *, name):
    nw = len(arrs)

    def body(*refs):
        ins, outs = refs[:nw], refs[nw:2 * nw]
        send, recv = refs[2 * nw:]
        x, y, c, _ = _place()
        cps = [pltpu.make_async_remote_copy(src_ref=ins[w], dst_ref=outs[w], send_sem=send.at[w], recv_sem=recv.at[w],
                                            device_id=(x, y, 1 - c), device_id_type=MESH) for w in range(nw)]
        for cp in cps:
            cp.start()
        for cp in cps:
            cp.wait()

    return pl.pallas_call(
        body, name=name, in_specs=[ANY] * nw, out_specs=[ANY] * nw,
        out_shape=[jax.ShapeDtypeStruct(a.shape, a.dtype) for a in arrs],
        scratch_shapes=[pltpu.SemaphoreType.DMA((nw,)), pltpu.SemaphoreType.DMA((nw,))],
    )(*arrs)


def _allreduce_small(p, after, *, name):
    R = p.shape[0]
    hr = R // 2

    def body(p_ref, _after_ref, out_ref, sib_ref, sum_ref, gat_ref, tot_ref, send, recv):
        x, y, c, chips = _place()
        s = 2 * x + y
        sibling = (x, y, 1 - c)
        rows = pl.ds(pl.multiple_of(c * hr, 8), hr)
        swap = pltpu.make_async_remote_copy(src_ref=p_ref, dst_ref=sib_ref, send_sem=send.at[0], recv_sem=recv.at[0],
                                            device_id=sibling, device_id_type=MESH)
        swap.start()
        swap.wait()
        sum_ref[...] = p_ref[...] + sib_ref[...]
        gat_ref[s] = sum_ref[rows, :]
        cps = [pltpu.make_async_remote_copy(src_ref=sum_ref.at[rows], dst_ref=gat_ref.at[s], send_sem=send.at[1 + j],
                                            recv_sem=recv.at[1 + j], device_id=(cx, cy, c), device_id_type=MESH)
               for j, (cx, cy) in enumerate(chips)]
        for cp in cps:
            cp.start()
        for cp in cps:
            cp.wait()
        tot_ref[...] = ((gat_ref[0] + gat_ref[1]) + gat_ref[2]) + gat_ref[3]
        out_ref[rows, :] = tot_ref[...]
        share = pltpu.make_async_remote_copy(src_ref=tot_ref, dst_ref=out_ref.at[rows], send_sem=send.at[4],
                                             recv_sem=recv.at[4], device_id=sibling, device_id_type=MESH)
        share.start()
        share.wait_send()
        other = out_ref.at[pl.ds(pl.multiple_of((1 - c) * hr, 8), hr)]
        pltpu.make_async_remote_copy(src_ref=other, dst_ref=other, send_sem=send.at[4], recv_sem=recv.at[4],
                                     device_id=(x, y, c), device_id_type=MESH).wait_recv()

    vmem = pl.BlockSpec(memory_space=pltpu.VMEM)
    return pl.pallas_call(
        body, name=name, in_specs=[vmem, ANY], out_specs=vmem,
        out_shape=jax.ShapeDtypeStruct((R, 128), F32),
        scratch_shapes=[pltpu.VMEM((R, 128), F32), pltpu.VMEM((R, 128), F32), pltpu.VMEM((NCHIP, hr, 128), F32),
                        pltpu.VMEM((hr, 128), F32), pltpu.SemaphoreType.DMA((5,)), pltpu.SemaphoreType.DMA((5,))],
    )(p, after)


def _select_half_bf16(g, half, add, slot, *, name):
    _, R, C = g.shape
    hr = R // 2
    tr = _pick_rows(hr, 16)
    nb = hr // tr
    sel = jnp.concatenate([jnp.reshape(half, (1,)).astype(jnp.int32), slot])

    def body(s_ref, g_ref, a_ref, o_ref, own_ref):
        val = (g_ref[...].astype(F32) + a_ref[...].astype(F32)).astype(BF16)
        o_ref[...] = val

        @pl.when(pl.program_id(1) == s_ref[1])
        def _():
            own_ref[...] = val

    g_spec = pl.BlockSpec((None, tr, C), lambda i, j, s: (j, s[0] * nb + i, 0))
    o_spec = pl.BlockSpec((None, tr, C), lambda i, j, s: (j, i, 0))
    own_spec = pl.BlockSpec((None, tr, C), lambda i, j, s: (s[1], i, 0))
    shape = jax.ShapeDtypeStruct((NCHIP, hr, C), BF16)
    return pl.pallas_call(
        body, name=name,
        grid_spec=pltpu.PrefetchScalarGridSpec(
            num_scalar_prefetch=1, grid=(nb, NCHIP), in_specs=[g_spec, o_spec], out_specs=[o_spec, own_spec]),
        out_shape=[shape, shape],
        compiler_params=_cp(("parallel", "arbitrary"), VMEM_MB),
    )(sel, g, add)


def _sum_slots(r, *, name):
    _, R, C = r.shape
    tr = _pick_rows(R, 16)

    def body(r_ref, o_ref):
        acc = r_ref[0].astype(F32) + r_ref[1].astype(F32)
        for j in range(2, NCHIP):
            acc = acc + r_ref[j].astype(F32)
        o_ref[...] = acc

    return pl.pallas_call(
        body, name=name, grid=(R // tr,),
        in_specs=[pl.BlockSpec((NCHIP, tr, C), lambda i: (0, i, 0))],
        out_specs=pl.BlockSpec((tr, C), lambda i: (i, 0)),
        out_shape=jax.ShapeDtypeStruct((R, C), F32),
        compiler_params=_cp(("parallel",), VMEM_MB),
    )(r)


def _adamw_math(w, g, m, v):
    m = ADAM_B1 * m + (1.0 - ADAM_B1) * g
    v = ADAM_B2 * v + (1.0 - ADAM_B2) * (g * g)
    m_hat = m / (1.0 - ADAM_B1 ** ADAM_STEP)
    v_hat = v / (1.0 - ADAM_B2 ** ADAM_STEP)
    delta = -ADAM_LR * (m_hat / (jnp.sqrt(v_hat) + ADAM_EPS) + ADAM_WD * w)
    return delta, m, v


def _adamw(w, g_mine, g_sib, m, v, core, *, name):
    R, C = w.shape
    hr = R // 2
    tr = _pick_rows(hr, 8)
    nb = hr // tr
    row = pl.BlockSpec((tr, C), lambda hh, i, c: (hh * nb + i, 0))
    mine = pl.BlockSpec((tr, C), lambda hh, i, c: (jnp.where(hh == c[0], i, 0), 0))
    sibs = pl.BlockSpec((tr, C), lambda hh, i, c: (jnp.where(hh == c[0], 0, i), 0))

    def body(c_ref, w_ref, gm_ref, gs_ref, m_ref, v_ref, go_ref, d_ref, mo_ref, vo_ref):
        gv = jnp.where(pl.program_id(0) == c_ref[0], gm_ref[...], gs_ref[...])
        d, mn, vn = _adamw_math(w_ref[...], gv, m_ref[...], v_ref[...])
        go_ref[...] = gv
        d_ref[...] = d
        mo_ref[...] = mn
        vo_ref[...] = vn

    return pl.pallas_call(
        body, name=name,
        grid_spec=pltpu.PrefetchScalarGridSpec(
            num_scalar_prefetch=1, grid=(2, nb),
            in_specs=[row, mine, sibs, row, row], out_specs=[row] * 4),
        out_shape=[jax.ShapeDtypeStruct((R, C), F32)] * 4,
        compiler_params=_cp(("parallel", "parallel"), VMEM_MB),
    )(core, w, g_mine, g_sib, m, v)


def _adamw_small(ws, gs, ms, vs, *, name):
    n = len(ws)

    def body(*refs):
        w_r, g_r, m_r, v_r = refs[:n], refs[n:2 * n], refs[2 * n:3 * n], refs[3 * n:4 * n]
        d_r, mo_r, vo_r = refs[4 * n:5 * n], refs[5 * n:6 * n], refs[6 * n:7 * n]
        for k in range(n):
            d, mn, vn = _adamw_math(w_r[k][...], g_r[k][...], m_r[k][...], v_r[k][...])
            d_r[k][...] = d
            mo_r[k][...] = mn
            vo_r[k][...] = vn

    shapes = [jax.ShapeDtypeStruct(w.shape, F32) for w in ws]
    res = pl.pallas_call(body, name=name, out_shape=shapes * 3)(*ws, *gs, *ms, *vs)
    return res[:n], res[n:2 * n], res[2 * n:]


_PACK_ROWS = 8


def _pack(parts):
    rows = []
    for a in parts:
        flat = a.reshape(-1)
        n = -(-flat.shape[0] // (_PACK_ROWS * 128)) * (_PACK_ROWS * 128)
        rows.append(jnp.pad(flat, (0, n - flat.shape[0])).reshape(-1, 128))
    total = sum(r.shape[0] for r in rows)
    if total % 16:
        rows.append(jnp.zeros((16 - total % 16, 128), F32))
    return jnp.concatenate(rows, axis=0)


def _unpack(p, shapes):
    out, r = [], 0
    for shp in shapes:
        n = math.prod(shp)
        nr = -(-n // (_PACK_ROWS * 128)) * _PACK_ROWS
        out.append(p[r:r + nr].reshape(-1)[:n].reshape(shp))
        r += nr
    return out


def kernel(x, mem, g_mix, w_in, ln_v_g, ln_v_b, w_s, b_s, conv_w, g_mem, w_kv, g_head, w_o, g_ffn, w_ffn1, w_ffn2, g_final, loss_target, m_g_mix, m_w_in, m_ln_v_g, m_ln_v_b, m_w_s, m_b_s, m_conv_w, m_g_mem, m_w_kv, m_g_head, m_w_o, m_g_ffn, m_w_ffn1, m_w_ffn2, m_g_final, v_g_mix, v_w_in, v_ln_v_g, v_ln_v_b, v_w_s, v_b_s, v_conv_w, v_g_mem, v_w_kv, v_g_head, v_w_o, v_g_ffn, v_w_ffn1, v_w_ffn2, v_g_final):
    sds = jax.ShapeDtypeStruct
    xi, yi = lax.axis_index("x"), lax.axis_index("y")
    shard = 2 * xi + yi
    x2d, mem2d, tgt = x[0], mem[0], loss_target[0]
    ws3, bs2 = w_s[0], b_s[0]
    g_final2 = g_final.reshape(1, D)
    dff4 = DFF // NCHIP
    din4 = DIN // NCHIP
    dcv4 = DC // NCHIP

    big = [w_in[0].T, w_kv[0], w_o[0], w_ffn1[0], w_ffn2[0]]
    big_names = ["w_in", "w_kv", "w_o", "w_ffn1", "w_ffn2"]
    slot = jnp.reshape(shard, (1,)).astype(jnp.int32)
    core = jnp.reshape(lax.axis_index("c"), (1,)).astype(jnp.int32)
    conv_pad = jnp.pad(conv_w[0], ((0, 16 - 3), (0, 256 - dcv4)))
    conv_slots = lax.dynamic_update_slice(jnp.zeros((NCHIP, 16, 256), F32), conv_pad[None], (shard, 0, 0))

    def gather_start(bufs, after, nm):
        return _allgather_start(bufs, after, name="ag_start_" + nm)

    def gather_wait(state, idx, after, nm):
        send, recv, bufs, _ = state
        got, token = _transfer_wait([send[k] for k in idx], [recv[k] for k in idx], [[bufs[k]] for k in idx],
                                    [(N_PEER_CHIPS, bufs[k].shape[1] // 2) for k in idx], after, name="ag_wait_" + nm)
        return [g[0] for g in got], token

    cast = lambda k, after: _cast_into_slot(big[k], slot, after, name="cast_" + big_names[k])
    ag_in = gather_start([cast(0, slot), conv_slots], slot, "in")
    bs_t = bs2.T

    h = _rms_fwd(x2d, g_mix, name="rms_mix", after=[ag_in[3]])
    mem_n = _rms_fwd(mem2d, g_mem, name="rms_mem", after=[h])
    rest_b = [cast(1, mem_n)]
    rest_b.append(cast(2, rest_b[0]))
    rest_b.append(cast(3, rest_b[1]))
    w2_b = cast(4, rest_b[2])
    got_in, tok = gather_wait(ag_in, [0, 1], w2_b, "in")
    ag_rest = gather_start(rest_b, tok, "kvo1")
    win4, conv4 = _forward_to_sibling(got_in, ag_rest[3], name="ag_fwd_in")
    w_in_t = win4.reshape(DIN, D)
    conv_full = conv4[:, :3, :dcv4].transpose(1, 0, 2).reshape(3, DC)
    (proj,) = _matmul(h, w_in_t, name="mm_proj", tb=True, M=S, N=DIN, K=D, tn=DIN // 2, outs=[sds((S, DIN), F32)])
    got_kvo, tok = gather_wait(ag_rest, [0, 1], proj, "kvo")
    wkv4, wo4 = _forward_to_sibling(got_kvo, tok, name="ag_fwd_kvo")
    w_kv_full = wkv4.reshape(D, 2 * DM)
    w_o_full = wo4.reshape(D, D)
    (kv,) = _matmul(mem_n, w_kv_full, name="mm_kv", M=NMEM, N=2 * DM, K=D, outs=[sds((NMEM, 2 * DM), F32)])
    heads, hn, ycv = _mix_fwd(proj, kv, ws3, bs_t, ln_v_g, ln_v_b, conv_full, g_head, name="mix_fwd")
    (x2,) = _matmul(hn, w_o_full, name="mm_wo", M=S, N=D, K=D, outs=[sds((S, D), F32)],
                    epi=lambda acc, res: (acc + res,), extras=[(x2d, _tile_spec())])
    h2 = _rms_fwd(x2, g_ffn, name="rms_ffn")
    got_w1, tok = gather_wait(ag_rest, [2], h2, "ffn1")
    ag_w2 = gather_start([w2_b], tok, "ffn2")
    (w14,) = _forward_to_sibling(got_w1, ag_w2[3], name="ag_fwd_ffn1")

    def w1_cols(tn, tk):
        nb = dff4 // tn
        return pl.BlockSpec((None, tk, tn), lambda j, i, k: (j // nb, k, j % nb))

    (act,) = _matmul(h2, w14, name="mm_ffn1", M=S, N=DFF, K=D, tn=dff4, b_spec=w1_cols, outs=[sds((S, DFF), BF16)],
                     epi=lambda acc: (jnp.square(jnp.maximum(acc, 0.0)),))
    got_w2, tok = gather_wait(ag_w2, [0], act, "ffn2")
    (w24,) = _forward_to_sibling(got_w2, tok, name="ag_fwd_ffn2")
    w2_full = w24.reshape(DFF, D)
    (x3,) = _matmul(act, w2_full, name="mm_ffn2", M=S, N=D, K=DFF, tm=2 * TM, outs=[sds((S, D), F32)],
                    epi=lambda acc, res: (acc + res,), extras=[(x2, _tile_spec())])

    ci = lax.axis_index("c")

    def rs_sibling(g4, nm):
        return _sibling_start([g4], name="rs_sib_" + nm)

    def rs_chips(state, after, nm):
        send, recv, g4, land, _ = state
        (((land_, g4_),), _) = _transfer_wait(send, recv, [[land[0], g4[0]]], [(NCHIP, land[0].shape[1])], after,
                                             name="rs_sibwait_" + nm)
        part, buf = _select_half_bf16(g4_, ci, land_, slot, name="rs_add_" + nm)
        return _scatter_start([part], [buf], name="rs_start_" + nm)

    def rs_end(state, after, nm):
        send, recv, parts, bufs, _ = state
        (((buf, _),), _) = _transfer_wait(send, recv, [[bufs[0], parts[0]]], [(N_PEER_CHIPS, bufs[0].shape[1])], after,
                                          name="rs_wait_" + nm)
        return _sum_slots(buf, name="rs_sum_" + nm)

    big_m = [m_w_in[0].T, m_w_kv[0], m_w_o[0], m_w_ffn1[0], m_w_ffn2[0]]
    big_v = [v_w_in[0].T, v_w_kv[0], v_w_o[0], v_w_ffn1[0], v_w_ffn2[0]]
    big_out = {}

    def rs_finish(ks, halves, nm):
        sib = _sibling_exchange(halves, name="rs_share_" + nm)
        for k, g, gs in zip(ks, halves, sib):
            big_out[big_names[k]] = _adamw(big[k], g, gs, big_m[k], big_v[k], core, name="adamw_" + big_names[k])

    dx3, dx3b, dg_final, loss11 = _loss_bwd(x3, g_final2, tgt, name="loss_bwd")
    (dw2,) = _matmul(act, dx3b, name="mm_dw2", ta=True, M=DFF, N=D, K=S, tn=D, outs=[sds((DFF, D), BF16)])
    sib_w2 = rs_sibling(dw2.reshape(NCHIP, dff4, D), "w_ffn2")
    (dfb,) = _matmul(dx3b, w2_full, name="mm_dact", tb=True, M=S, N=DFF, K=D, tn=dff4, outs=[sds((S, DFF), BF16)],
                     epi=lambda acc, a: (acc * (2.0 * jnp.sqrt(a.astype(F32))),), extras=[(act, _tile_spec())],
                     after=[sib_w2[4]])
    rs_w2 = rs_chips(sib_w2, dfb, "w_ffn2")

    def dw1_out(tm, tn):
        nb = dff4 // tn
        return [pl.BlockSpec((None, tm, tn), lambda j, i, k: (j // nb, i, j % nb))]

    (dw1,) = _matmul(h2, dfb, name="mm_dw1", ta=True, M=D, N=DFF, K=S, tn=dff4, outs=[sds((NCHIP, D, dff4), BF16)],
                     out_specs=dw1_out, after=[rs_w2[4]])
    sib_w1 = rs_sibling(dw1, "w_ffn1")

    def w1_rows(tn, tk):
        kb = dff4 // tk
        return pl.BlockSpec((None, tn, tk), lambda j, i, k: (k // kb, j, k % kb))

    (dh2,) = _matmul(dfb, w14, name="mm_dh2", tb=True, M=S, N=D, K=DFF, tm=2 * TM, b_spec=w1_rows,
                     outs=[sds((S, D), F32)], after=[sib_w1[4]])
    rs_w1 = rs_chips(sib_w1, dh2, "w_ffn1")
    dx2, dx2b, dg_ffn = _rms_bwd(dh2, x2, g_ffn, dx3, name="rms_ffn_bwd", after=[rs_w1[4]])
    (dwo,) = _matmul(hn, dx2b, name="mm_dwo", ta=True, M=D, N=D, K=S, outs=[sds((D, D), BF16)])
    sib_wo = rs_sibling(dwo.reshape(NCHIP, D // NCHIP, D), "w_o")
    (dhn,) = _matmul(dx2b, w_o_full, name="mm_dhn", tb=True, M=S, N=D, K=D, outs=[sds((S, D), F32)],
                     after=[sib_wo[4]])
    rs_wo = rs_chips(sib_wo, dhn, "w_o")
    dproj, dkv, dws, dbs8, dlng, dlnb, dcw8, dgh = _mix_bwd(
        dhn, heads, proj, ycv, kv, ws3, bs_t, ln_v_g, ln_v_b, conv_full, g_head, rs_wo[4], name="mix_bwd")
    (dwin_t,) = _matmul(dproj, h, name="mm_dwin", ta=True, M=DIN, N=D, K=S, tm=DIN // 2, outs=[sds((DIN, D), BF16)])
    sib_win = rs_sibling(dwin_t.reshape(NCHIP, din4, D), "w_in")
    (dwkv,) = _matmul(mem_n, dkv, name="mm_dwkv", ta=True, M=D, N=2 * DM, K=NMEM, outs=[sds((D, 2 * DM), BF16)],
                      after=[sib_win[4]])
    sib_wkv = rs_sibling(dwkv.reshape(NCHIP, D // NCHIP, 2 * DM), "w_kv")
    (dh,) = _matmul(dproj, w_in_t, name="mm_dh", M=S, N=D, K=DIN, tk=DIN, outs=[sds((S, D), F32)],
                    after=[sib_wkv[4]])
    rs_win = rs_chips(sib_win, dh, "w_in")
    rs_wkv = rs_chips(sib_wkv, rs_win[4], "w_kv")
    dx, dg_mix = _rms_bwd(dh, x2d, g_mix, dx2, name="rms_mix_bwd", want_bf=False, after=[rs_wkv[4]])
    (dmem_n,) = _matmul(dkv, w_kv_full, name="mm_dmem", tb=True, M=NMEM, N=D, K=2 * DM, outs=[sds((NMEM, D), F32)],
                        after=[dx])
    (dg_mem,) = _rms_bwd(dmem_n, mem2d, g_mem, None, name="rms_mem_bwd", want_dx=False)
    half_w2 = rs_end(rs_w2, dg_mem, "w_ffn2")
    half_w1 = rs_end(rs_w1, half_w2, "w_ffn1")
    rs_finish([4, 3], [half_w2, half_w1], "ffn")

    loss = lax.psum(loss11[0, 0], ("x", "y", "c"))

    half_wo = rs_end(rs_wo, big_out["w_ffn1"][1], "w_o")
    half_win = rs_end(rs_win, half_wo, "w_in")
    half_wkv = rs_end(rs_wkv, half_win, "w_kv")
    rs_finish([0, 1, 2], [half_win, half_wkv, half_wo], "rest")

    small_names = ["g_mix", "ln_v_g", "ln_v_b", "w_s", "b_s", "conv_w", "g_mem", "g_head", "g_ffn", "g_final"]
    small_part = [dg_mix, dlng, dlnb, dws, dbs8[:, 0, :], dcw8[:3], dg_mem, dgh, dg_ffn, dg_final]
    small_shapes = [(1, D), (1, DS), (1, DS), (NSH, CHUNK, CHUNK), (NSH, CHUNK), (3, DC), (1, D), (1, D), (1, D), (1, D)]
    total = _allreduce_small(_pack(small_part), big_out["w_o"][1], name="allreduce_small")
    small_g = _unpack(total, small_shapes)
    small_g[5] = lax.dynamic_slice(small_g[5], (0, shard * dcv4), (3, dcv4))
    small_w = [g_mix, ln_v_g, ln_v_b, ws3, bs2, conv_w[0], g_mem, g_head, g_ffn, g_final2]
    small_m = [m_g_mix, m_ln_v_g, m_ln_v_b, m_w_s[0], m_b_s[0], m_conv_w[0], m_g_mem, m_g_head, m_g_ffn,
               m_g_final.reshape(1, D)]
    small_v = [v_g_mix, v_ln_v_g, v_ln_v_b, v_w_s[0], v_b_s[0], v_conv_w[0], v_g_mem, v_g_head, v_g_ffn,
               v_g_final.reshape(1, D)]
    s_delta, s_m, s_v = _adamw_small(small_w, small_g, small_m, small_v, name="adamw_small")
    small_out = {nm: (g, d, mn, vn) for nm, g, d, mn, vn in zip(small_names, small_g, s_delta, s_m, s_v)}

    order = ["g_mix", "w_in", "ln_v_g", "ln_v_b", "w_s", "b_s", "conv_w", "g_mem", "w_kv", "g_head", "w_o",
             "g_ffn", "w_ffn1", "w_ffn2", "g_final"]
    like = dict(g_mix=g_mix, w_in=w_in, ln_v_g=ln_v_g, ln_v_b=ln_v_b, w_s=w_s, b_s=b_s, conv_w=conv_w, g_mem=g_mem,
                w_kv=w_kv, g_head=g_head, w_o=w_o, g_ffn=g_ffn, w_ffn1=w_ffn1, w_ffn2=w_ffn2, g_final=g_final)
    res = {**big_out, **small_out}
    res["w_in"] = [a.T for a in res["w_in"]]
    outs = [loss, dx[None]]
    for k in range(4):
        outs += [res[nm][k].reshape(like[nm].shape) for nm in order]
    return tuple(outs)
```

```python
import math

import jax
import jax.numpy as jnp
from jax import lax
from jax.experimental import pallas as pl
from jax.experimental.pallas import tpu as pltpu

F32 = jnp.float32
BF16 = jnp.bfloat16
MESH = pl.DeviceIdType.MESH

D = 2048
S = 2048
HD = 128
NH = D // HD
NMH = 4
NSH = (NH - NMH) // 2
NCH = NH - NMH - NSH
DS = NSH * HD
DC = NCH * HD
DM = NMH * HD
DIN = 2 * DS + 3 * DC + DM
CHUNK = 128
NMEM = 256
DFF = 4 * D
EPS = 1e-6
NCHIP = 4
SCALE = HD ** -0.5

ADAM_LR = 0.001
ADAM_B1 = 0.9
ADAM_B2 = 0.999
ADAM_EPS = 1e-08
ADAM_WD = 0.01
ADAM_STEP = 10

TR_EW = 256
TR_MIX = 256
TM = 512
TN = 1024
TK = 2048
VMEM_MB = 56
HALO = 8


def _pick(n, target, q=128):
    best = None
    for t in range(q, min(n, target) + 1, q):
        if n % t == 0:
            best = t
    return n if best is None else best


def _pick_rows(n, q):
    below = _pick(n, TR_EW, q)
    if 2 * below >= TR_EW:
        return below
    above = [t for t in range(TR_EW, min(n, 4 * TR_EW) + 1, q) if n % t == 0]
    return above[0] if above else below


def _cp(sem=None, vmem_mb=None, **kw):
    d = dict(kw)
    if sem is not None:
        d["dimension_semantics"] = sem
    if vmem_mb is not None:
        d["vmem_limit_bytes"] = vmem_mb << 20
    return pltpu.CompilerParams(**d)


def _gelu(x):
    z = 0.7978845608028654 * (x + 0.044715 * (x * x * x))
    return 0.5 * x * (1.0 + jnp.tanh(z))


def _gelu_grad(x):
    x2 = x * x
    t = jnp.tanh(0.7978845608028654 * (x + 0.044715 * (x2 * x)))
    return 0.5 * (1.0 + t) + 0.5 * x * (1.0 - t * t) * (0.7978845608028654 * (1.0 + 3.0 * 0.044715 * x2))


def _matmul(a, b, *, name, ta=False, tb=False, M, N, K, tm=None, tn=None, tk=None, outs, epi=None,
            extras=(), b_spec=None, out_specs=None, after=()):
    n_after = len(after)
    tm = _pick(M, TM if tm is None else tm, 8)
    tn = _pick(N, TN if tn is None else tn)
    tk = _pick(K, TK if tk is None else tk)
    nk = K // tk
    grid = (N // tn, M // tm, nk)
    a_spec = (pl.BlockSpec((tk, tm), lambda j, i, k: (k, i)) if ta
              else pl.BlockSpec((tm, tk), lambda j, i, k: (i, k)))
    if b_spec is None:
        b_spec = (pl.BlockSpec((tn, tk), lambda j, i, k: (j, k)) if tb
                  else pl.BlockSpec((tk, tn), lambda j, i, k: (k, j)))
    else:
        b_spec = b_spec(tn, tk)
    if out_specs is None:
        out_specs = [pl.BlockSpec((tm, tn), lambda j, i, k: (i, j)) for _ in outs]
    else:
        out_specs = out_specs(tm, tn)
    dn = (((0 if ta else 1,), (1 if tb else 0,)), ((), ()))
    n_ex, n_out = len(extras), len(outs)

    def body(*refs):
        a_ref, b_ref = refs[0], refs[1]
        ex = refs[2:2 + n_ex]
        first_out = 2 + n_ex + n_after
        o = refs[first_out:first_out + n_out]
        acc = refs[first_out + n_out:]
        part = lax.dot_general(a_ref[...].astype(BF16), b_ref[...].astype(BF16), dn,
                               preferred_element_type=F32)

        def finish(val):
            res = (val,) if epi is None else epi(val, *[e[...] for e in ex])
            for r, o_ref in zip(res, o):
                o_ref[...] = r.astype(o_ref.dtype)

        if nk == 1:
            finish(part)
        else:
            k = pl.program_id(2)

            @pl.when(k == 0)
            def _():
                acc[0][...] = part

            @pl.when(k > 0)
            def _():
                acc[0][...] += part

            @pl.when(k == nk - 1)
            def _():
                finish(acc[0][...])

    return pl.pallas_call(
        body, name=name, grid=grid,
        in_specs=[a_spec, b_spec] + [sp(tm, tn) for _, sp in extras] + [ANY] * n_after,
        out_specs=out_specs, out_shape=outs,
        scratch_shapes=([pltpu.VMEM((tm, tn), F32)] if nk > 1 else []),
        compiler_params=_cp(("parallel", "parallel", "arbitrary"), VMEM_MB),
    )(a, b, *[arr for arr, _ in extras], *after)


def _tile_spec():
    return lambda tm, tn: pl.BlockSpec((tm, tn), lambda j, i, k: (i, j))


def _cast_into_slot(w, slot, after, *, name):
    R, C = w.shape
    tr = _pick_rows(R, 16)

    def body(s_ref, w_ref, _after_ref, o_ref):
        o_ref[...] = w_ref[...].astype(BF16)

    return pl.pallas_call(
        body, name=name,
        grid_spec=pltpu.PrefetchScalarGridSpec(
            num_scalar_prefetch=1, grid=(R // tr,),
            in_specs=[pl.BlockSpec((tr, C), lambda i, s: (i, 0)), ANY],
            out_specs=pl.BlockSpec((None, tr, C), lambda i, s: (s[0], i, 0))),
        out_shape=jax.ShapeDtypeStruct((NCHIP, R, C), BF16),
        compiler_params=_cp(("parallel",), VMEM_MB),
    )(slot, w, after)


def _rms_fwd(x, g, *, name, after=()):
    R, C = x.shape
    tr = _pick(R, TR_EW, 16)
    n_after = len(after)

    def body(x_ref, g_ref, *rest):
        o_ref = rest[n_after]
        xv = x_ref[...]
        r = lax.rsqrt(jnp.mean(xv * xv, axis=-1, keepdims=True) + EPS)
        o_ref[...] = ((xv * r) * g_ref[...]).astype(BF16)

    return pl.pallas_call(
        body, name=name, grid=(R // tr,),
        in_specs=[pl.BlockSpec((tr, C), lambda i: (i, 0)), pl.BlockSpec((1, C), lambda i: (0, 0))] + [ANY] * n_after,
        out_specs=pl.BlockSpec((tr, C), lambda i: (i, 0)),
        out_shape=jax.ShapeDtypeStruct((R, C), BF16),
        compiler_params=_cp(("parallel",), VMEM_MB),
    )(x, g, *after)


def _rms_bwd(dh, x, g, dres, *, name, want_dx=True, want_bf=True, after=()):
    R, C = x.shape
    tr = _pick(R, TR_EW, 16)
    has_res = dres is not None
    row = pl.BlockSpec((tr, C), lambda i: (i, 0))
    vec = pl.BlockSpec((1, C), lambda i: (0, 0))

    def body(*refs):
        dh_ref, x_ref, g_ref = refs[:3]
        pos = 3
        dres_ref = None
        if has_res:
            dres_ref = refs[pos]
            pos += 1
        outs = refs[pos + len(after):]
        i = pl.program_id(0)
        xv = x_ref[...]
        r = lax.rsqrt(jnp.mean(xv * xv, axis=-1, keepdims=True) + EPS)
        xh = xv * r
        dhv = dh_ref[...]
        dg_ref = outs[-1]
        dgp = jnp.sum(dhv * xh, axis=0, keepdims=True)

        @pl.when(i == 0)
        def _():
            dg_ref[...] = dgp

        @pl.when(i > 0)
        def _():
            dg_ref[...] += dgp

        if want_dx:
            t = dhv * g_ref[...]
            dx = r * (t - xh * jnp.mean(t * xh, axis=-1, keepdims=True))
            if has_res:
                dx = dx + dres_ref[...]
            outs[0][...] = dx
            if want_bf:
                outs[1][...] = dx.astype(BF16)

    in_specs = [row, row, vec] + ([row] if has_res else []) + [ANY] * len(after)
    out_specs, out_shape = [], []
    if want_dx:
        out_specs.append(row)
        out_shape.append(jax.ShapeDtypeStruct((R, C), F32))
        if want_bf:
            out_specs.append(row)
            out_shape.append(jax.ShapeDtypeStruct((R, C), BF16))
    out_specs.append(vec)
    out_shape.append(jax.ShapeDtypeStruct((1, C), F32))
    args = [dh, x, g] + ([dres] if has_res else []) + list(after)
    return pl.pallas_call(
        body, name=name, grid=(R // tr,), in_specs=in_specs, out_specs=out_specs, out_shape=out_shape,
        compiler_params=_cp(("arbitrary",), VMEM_MB),
    )(*args)


def _loss_bwd(x3, g, tgt, *, name):
    R, C = x3.shape
    tr = _pick(R, TR_EW, 16)
    n = R // tr
    row = pl.BlockSpec((tr, C), lambda i: (i, 0))
    vec = pl.BlockSpec((1, C), lambda i: (0, 0))

    def body(x_ref, g_ref, t_ref, dx_ref, dxb_ref, dg_ref, loss_ref, acc_ref):
        i = pl.program_id(0)
        xv = x_ref[...]
        gv = g_ref[...]
        r = lax.rsqrt(jnp.mean(xv * xv, axis=-1, keepdims=True) + EPS)
        xh = xv * r
        e = xh * gv - t_ref[...]
        dy = e * (1.0 / C)
        sq = jnp.sum(e * e, axis=0, keepdims=True)
        dgp = jnp.sum(dy * xh, axis=0, keepdims=True)

        @pl.when(i == 0)
        def _():
            acc_ref[...] = sq
            dg_ref[...] = dgp

        @pl.when(i > 0)
        def _():
            acc_ref[...] += sq
            dg_ref[...] += dgp

        t = dy * gv
        dx = r * (t - xh * jnp.mean(t * xh, axis=-1, keepdims=True))
        dx_ref[...] = dx
        dxb_ref[...] = dx.astype(BF16)

        @pl.when(i == n - 1)
        def _():
            loss_ref[...] = jnp.sum(acc_ref[...], axis=-1, keepdims=True) * (0.5 / C)

    return pl.pallas_call(
        body, name=name, grid=(n,),
        in_specs=[row, vec, row],
        out_specs=[row, row, vec, pl.BlockSpec((1, 1), lambda i: (0, 0))],
        out_shape=[jax.ShapeDtypeStruct((R, C), F32), jax.ShapeDtypeStruct((R, C), BF16),
                   jax.ShapeDtypeStruct((1, C), F32), jax.ShapeDtypeStruct((1, 1), F32)],
        scratch_shapes=[pltpu.VMEM((1, C), F32)],
        compiler_params=_cp(("arbitrary",), VMEM_MB),
    )(x3, g, tgt)


def _offsets():
    u0 = 0
    v0 = DS
    b0 = 2 * DS
    c0 = b0 + DC
    x0 = c0 + DC
    q0 = x0 + DC
    return u0, v0, b0, c0, x0, q0


def _tri_mask(lower):
    r = lax.broadcasted_iota(jnp.int32, (CHUNK, CHUNK), 0)
    c = lax.broadcasted_iota(jnp.int32, (CHUNK, CHUNK), 1)
    return (r >= c) if lower else (c >= r)


def _layer_norm_stats(vg):
    mu = jnp.mean(vg, axis=-1, keepdims=True)
    vc = vg - mu
    rstd = lax.rsqrt(jnp.mean(vc * vc, axis=-1, keepdims=True) + EPS)
    return vc * rstd, rstd


def _softmax_rows(qh, kh):
    s = lax.dot_general(qh, kh, (((1,), (1,)), ((), ())), preferred_element_type=F32)
    m = jnp.max(s, axis=-1, keepdims=True)
    e = jnp.exp(s - m)
    return e / jnp.sum(e, axis=-1, keepdims=True)


def _mix_fwd(proj, kv, w_s, bs_t, ln_g, ln_b, conv_w, g_head, *, name):
    assert DS == DC
    tr = _pick(S, TR_MIX, CHUNK)
    n = S // tr
    nck = tr // CHUNK
    u0, v0, b0, c0, x0, q0 = _offsets()
    hb = tr // HALO

    def body(p_ref, cprev_ref, xprev_ref, kv_ref, ws_ref, bst_ref, lng_ref, lnb_ref, cw_ref, gh_ref,
             heads_ref, hn_ref, ycv_ref, buf_ref):
        i = pl.program_id(0)

        def emit(col, val):
            rs = lax.rsqrt(jnp.mean(val * val, axis=-1, keepdims=True) + EPS)
            heads_ref[:, col:col + HD] = val
            hn_ref[:, col:col + HD] = ((val * rs) * gh_ref[:, col:col + HD]).astype(BF16)

        vhat, _ = _layer_norm_stats(_gelu(p_ref[:, v0:v0 + DS]))
        vnb = (vhat * lng_ref[...] + lnb_ref[...]).astype(BF16)
        low = _tri_mask(True)
        for h in range(NSH):
            wt = jnp.where(low, ws_ref[h], 0.0).astype(BF16)
            bcol = bst_ref[:, h:h + 1]
            parts = []
            for c in range(nck):
                blk = vnb[c * CHUNK:(c + 1) * CHUNK, h * HD:(h + 1) * HD]
                parts.append(jnp.dot(wt, blk, preferred_element_type=F32) + bcol)
            mixed = parts[0] if nck == 1 else jnp.concatenate(parts, axis=0)
            emit(h * HD, _gelu(p_ref[:, u0 + h * HD:u0 + (h + 1) * HD]) * mixed)

        xc = p_ref[:, c0:c0 + DC] * p_ref[:, x0:x0 + DC]
        prev = cprev_ref[...] * xprev_ref[...]
        buf_ref[0:HALO, :] = jnp.where(i > 0, prev, 0.0)
        buf_ref[HALO:HALO + tr, :] = xc
        y = (cw_ref[2:3, :] * xc + cw_ref[1:2, :] * buf_ref[HALO - 1:HALO - 1 + tr, :]
             + cw_ref[0:1, :] * buf_ref[HALO - 2:HALO - 2 + tr, :])
        ycv_ref[...] = y
        cout = p_ref[:, b0:b0 + DC] * y
        for h in range(NCH):
            emit(DS + h * HD, cout[:, h * HD:(h + 1) * HD])

        for h in range(NMH):
            qh = (p_ref[:, q0 + h * HD:q0 + (h + 1) * HD] * SCALE).astype(BF16)
            kh = kv_ref[:, h * HD:(h + 1) * HD].astype(BF16)
            vh = kv_ref[:, DM + h * HD:DM + (h + 1) * HD].astype(BF16)
            p = _softmax_rows(qh, kh)
            emit(DS + DC + h * HD, jnp.dot(p.astype(BF16), vh, preferred_element_type=F32))

    full = lambda shape: pl.BlockSpec(shape, lambda i: (0,) * len(shape))
    halo_c = pl.BlockSpec((HALO, DC), lambda i: (jnp.maximum(i * hb - 1, 0), c0 // DC))
    halo_x = pl.BlockSpec((HALO, DC), lambda i: (jnp.maximum(i * hb - 1, 0), x0 // DC))
    return pl.pallas_call(
        body, name=name, grid=(n,),
        in_specs=[pl.BlockSpec((tr, DIN), lambda i: (i, 0)), halo_c, halo_x,
                  full((NMEM, 2 * DM)), full((NSH, CHUNK, CHUNK)), full((CHUNK, NSH)),
                  full((1, DS)), full((1, DS)), full((3, DC)), full((1, D))],
        out_specs=[pl.BlockSpec((tr, D), lambda i: (i, 0)), pl.BlockSpec((tr, D), lambda i: (i, 0)),
                   pl.BlockSpec((tr, DC), lambda i: (i, 0))],
        out_shape=[jax.ShapeDtypeStruct((S, D), F32), jax.ShapeDtypeStruct((S, D), BF16),
                   jax.ShapeDtypeStruct((S, DC), F32)],
        scratch_shapes=[pltpu.VMEM((tr + HALO, DC), F32)],
        compiler_params=_cp(("parallel",), VMEM_MB),
    )(proj, proj, proj, kv, w_s, bs_t, ln_g, ln_b, conv_w, g_head)


def _mix_bwd(dhn, heads, proj, ycv, kv, w_s, bs_t, ln_g, ln_b, conv_w, g_head, after, *, name):
    assert DS == DC
    tr = _pick(S, TR_MIX, CHUNK)
    n = S // tr
    nck = tr // CHUNK
    u0, v0, b0, c0, x0, q0 = _offsets()
    hb = tr // HALO
    last_hb = S // HALO - 1

    def body(dhn_ref, heads_ref, p_ref, ycv_ref, dhn_nx_ref, heads_nx_ref, b_nx_ref, kv_ref, ws_ref, bst_ref,
             lng_ref, lnb_ref, cw_ref, gh_ref, _after_ref,
             dp_ref, dkv_ref, dws_ref, dbs_ref, dlng_ref, dlnb_ref, dcw_ref, dgh_ref, buf_ref, dvn_ref):
        i = pl.program_id(0)

        @pl.when(i == 0)
        def _():
            dkv_ref[...] = jnp.zeros_like(dkv_ref)
            dws_ref[...] = jnp.zeros_like(dws_ref)
            dbs_ref[...] = jnp.zeros_like(dbs_ref)
            dlng_ref[...] = jnp.zeros_like(dlng_ref)
            dlnb_ref[...] = jnp.zeros_like(dlnb_ref)
            dcw_ref[...] = jnp.zeros_like(dcw_ref)
            dgh_ref[...] = jnp.zeros_like(dgh_ref)

        def head_bwd(a, dn, gh):
            rs = lax.rsqrt(jnp.mean(a * a, axis=-1, keepdims=True) + EPS)
            ah = a * rs
            t = dn * gh
            return rs * (t - ah * jnp.mean(t * ah, axis=-1, keepdims=True)), jnp.sum(dn * ah, axis=0, keepdims=True)

        def head_grad(col):
            da, dg = head_bwd(heads_ref[:, col:col + HD], dhn_ref[:, col:col + HD], gh_ref[:, col:col + HD])
            dgh_ref[:, col:col + HD] += dg
            return da

        v = p_ref[:, v0:v0 + DS]
        vhat, rstd = _layer_norm_stats(_gelu(v))
        vnb = (vhat * lng_ref[...] + lnb_ref[...]).astype(BF16)
        low = _tri_mask(True)
        ones = jnp.ones((HALO, HD), BF16)
        for h in range(NSH):
            w_h = ws_ref[h]
            wt = jnp.where(low, w_h, 0.0).astype(BF16)
            bcol = bst_ref[:, h:h + 1]
            da = head_grad(h * HD)
            u = p_ref[:, u0 + h * HD:u0 + (h + 1) * HD]
            ug = _gelu(u)
            dws = jnp.zeros((CHUNK, CHUNK), F32)
            dbs = jnp.zeros((HALO, CHUNK), F32)
            mixed_parts = []
            for c in range(nck):
                rows = slice(c * CHUNK, (c + 1) * CHUNK)
                blk = vnb[rows, h * HD:(h + 1) * HD]
                mixed_parts.append(jnp.dot(wt, blk, preferred_element_type=F32) + bcol)
                dmb = (da[rows] * ug[rows]).astype(BF16)
                dws = dws + lax.dot_general(dmb, blk, (((1,), (1,)), ((), ())), preferred_element_type=F32)
                dbs = dbs + lax.dot_general(ones, dmb, (((1,), (1,)), ((), ())), preferred_element_type=F32)
                dvn_ref[c * CHUNK:(c + 1) * CHUNK, h * HD:(h + 1) * HD] = lax.dot_general(
                    wt, dmb, (((0,), (0,)), ((), ())), preferred_element_type=F32)
            mixed = mixed_parts[0] if nck == 1 else jnp.concatenate(mixed_parts, axis=0)
            dp_ref[:, u0 + h * HD:u0 + (h + 1) * HD] = ((da * mixed) * _gelu_grad(u)).astype(BF16)
            dws_ref[h] += jnp.where(low, dws, 0.0)
            dbs_ref[h] += dbs
        dvn = dvn_ref[...]
        dlng_ref[...] += jnp.sum(dvn * vhat, axis=0, keepdims=True)
        dlnb_ref[...] += jnp.sum(dvn, axis=0, keepdims=True)
        dvh = dvn * lng_ref[...]
        dvg = rstd * (dvh - jnp.mean(dvh, axis=-1, keepdims=True)
                      - vhat * jnp.mean(dvh * vhat, axis=-1, keepdims=True))
        dp_ref[:, v0:v0 + DS] = (dvg * _gelu_grad(v)).astype(BF16)

        dc = jnp.concatenate([head_grad(DS + h * HD) for h in range(NCH)], axis=1)
        dc_nx = jnp.concatenate(
            [head_bwd(heads_nx_ref[:, h * HD:(h + 1) * HD], dhn_nx_ref[:, h * HD:(h + 1) * HD],
                      gh_ref[:, DS + h * HD:DS + (h + 1) * HD])[0] for h in range(NCH)], axis=1)
        bg = p_ref[:, b0:b0 + DC]
        cg = p_ref[:, c0:c0 + DC]
        xin = p_ref[:, x0:x0 + DC]
        dp_ref[:, b0:b0 + DC] = (dc * ycv_ref[...]).astype(BF16)
        dyv = dc * bg
        buf_ref[0:tr, :] = dyv
        buf_ref[tr:tr + HALO, :] = jnp.where(i < n - 1, dc_nx * b_nx_ref[...], 0.0)
        sh1 = buf_ref[1:1 + tr, :]
        sh0 = buf_ref[2:2 + tr, :]
        dxc = cw_ref[2:3, :] * dyv + cw_ref[1:2, :] * sh1 + cw_ref[0:1, :] * sh0
        xc = cg * xin
        dp_ref[:, c0:c0 + DC] = (dxc * xin).astype(BF16)
        dp_ref[:, x0:x0 + DC] = (dxc * cg).astype(BF16)
        dcw_ref[0:1, :] += jnp.sum(sh0 * xc, axis=0, keepdims=True)
        dcw_ref[1:2, :] += jnp.sum(sh1 * xc, axis=0, keepdims=True)
        dcw_ref[2:3, :] += jnp.sum(dyv * xc, axis=0, keepdims=True)

        for h in range(NMH):
            do = head_grad(DS + DC + h * HD).astype(BF16)
            qh = (p_ref[:, q0 + h * HD:q0 + (h + 1) * HD] * SCALE).astype(BF16)
            kh = kv_ref[:, h * HD:(h + 1) * HD].astype(BF16)
            vh = kv_ref[:, DM + h * HD:DM + (h + 1) * HD].astype(BF16)
            p = _softmax_rows(qh, kh)
            dpr = lax.dot_general(do, vh, (((1,), (1,)), ((), ())), preferred_element_type=F32)
            ds = (p * (dpr - jnp.sum(dpr * p, axis=-1, keepdims=True))).astype(BF16)
            dp_ref[:, q0 + h * HD:q0 + (h + 1) * HD] = (
                jnp.dot(ds, kh, preferred_element_type=F32) * SCALE).astype(BF16)
            dkv_ref[:, h * HD:(h + 1) * HD] += lax.dot_general(
                ds, qh, (((0,), (0,)), ((), ())), preferred_element_type=F32)
            dkv_ref[:, DM + h * HD:DM + (h + 1) * HD] += lax.dot_general(
                p.astype(BF16), do, (((0,), (0,)), ((), ())), preferred_element_type=F32)

    full = lambda shape: pl.BlockSpec(shape, lambda i: (0,) * len(shape))
    row = lambda c: pl.BlockSpec((tr, c), lambda i: (i, 0))
    nxt = lambda col: pl.BlockSpec((HALO, DC), lambda i: (jnp.minimum((i + 1) * hb, last_hb), col))
    return pl.pallas_call(
        body, name=name, grid=(n,),
        in_specs=[row(D), row(D), row(DIN), row(DC), nxt(DS // DC), nxt(DS // DC), nxt(b0 // DC),
                  full((NMEM, 2 * DM)), full((NSH, CHUNK, CHUNK)), full((CHUNK, NSH)),
                  full((1, DS)), full((1, DS)), full((3, DC)), full((1, D)), ANY],
        out_specs=[row(DIN), full((NMEM, 2 * DM)), full((NSH, CHUNK, CHUNK)), full((NSH, HALO, CHUNK)),
                   full((1, DS)), full((1, DS)), full((HALO, DC)), full((1, D))],
        out_shape=[jax.ShapeDtypeStruct((S, DIN), BF16), jax.ShapeDtypeStruct((NMEM, 2 * DM), F32),
                   jax.ShapeDtypeStruct((NSH, CHUNK, CHUNK), F32), jax.ShapeDtypeStruct((NSH, HALO, CHUNK), F32),
                   jax.ShapeDtypeStruct((1, DS), F32), jax.ShapeDtypeStruct((1, DS), F32),
                   jax.ShapeDtypeStruct((HALO, DC), F32), jax.ShapeDtypeStruct((1, D), F32)],
        scratch_shapes=[pltpu.VMEM((tr + HALO, DC), F32), pltpu.VMEM((tr, DS), F32)],
        compiler_params=_cp(("arbitrary",), VMEM_MB),
    )(dhn, heads, proj, ycv, dhn, heads, proj, kv, w_s, bs_t, ln_g, ln_b, conv_w, g_head, after)


def _place():
    x, y, c = lax.axis_index("x"), lax.axis_index("y"), lax.axis_index("c")
    chips = [(1 - x, y), (x, 1 - y), (1 - x, 1 - y)]
    return x, y, c, chips


ANY = pl.BlockSpec(memory_space=pl.ANY)


HBM = pl.BlockSpec(memory_space=pltpu.HBM)
SEM = pl.BlockSpec(memory_space=pltpu.SEMAPHORE)
EFFECT = pltpu.SideEffectType.DATAFLOW_SIDE_EFFECTING
N_PEER_CHIPS = 3


def _in_hbm(a):
    return pltpu.with_memory_space_constraint(a, pltpu.HBM)


def _allgather_start(bufs, after, *, name):
    nw = len(bufs)

    def body(*refs):
        ins, send, recv = refs[:nw], refs[nw + 1:2 * nw + 1], refs[2 * nw + 1:3 * nw + 1]
        token = refs[4 * nw + 1]
        x, y, c, chips = _place()
        s = 2 * x + y
        for w in range(nw):
            hr = bufs[w].shape[1] // 2
            rows = ins[w].at[s, pl.ds(c * hr, hr)]
            for cx, cy in chips:
                pltpu.make_async_remote_copy(src_ref=rows, dst_ref=rows, send_sem=send[w], recv_sem=recv[w],
                                             device_id=(cx, cy, c), device_id_type=MESH).start()
        token[...] = jnp.zeros_like(token)

    res = pl.pallas_call(
        body, name=name,
        in_specs=[HBM] * nw + [ANY],
        out_specs=[SEM] * (2 * nw) + [HBM] * nw + [pl.BlockSpec(memory_space=pltpu.VMEM)],
        out_shape=[pltpu.SemaphoreType.DMA(())] * (2 * nw) + [pltpu.HBM(a.shape, a.dtype) for a in bufs]
        + [jax.ShapeDtypeStruct((8, 128), F32)],
        input_output_aliases={w: 2 * nw + w for w in range(nw)},
        compiler_params=pltpu.CompilerParams(has_side_effects=EFFECT),
    )(*[_in_hbm(a) for a in bufs], after)
    return res[:nw], res[nw:2 * nw], res[2 * nw:3 * nw], res[3 * nw]


def _scatter_start(parts, bufs, *, name):
    nw = len(parts)

    def body(*refs):
        src, dst = refs[:nw], refs[nw:2 * nw]
        send, recv = refs[2 * nw:3 * nw], refs[3 * nw:4 * nw]
        token = refs[6 * nw]
        x, y, c, chips = _place()
        s = 2 * x + y
        for w in range(nw):
            for cx, cy in chips:
                pltpu.make_async_remote_copy(src_ref=src[w].at[2 * cx + cy], dst_ref=dst[w].at[s], send_sem=send[w],
                                             recv_sem=recv[w], device_id=(cx, cy, c), device_id_type=MESH).start()
        token[...] = jnp.zeros_like(token)

    res = pl.pallas_call(
        body, name=name,
        in_specs=[HBM] * (2 * nw),
        out_specs=[SEM] * (2 * nw) + [HBM] * (2 * nw) + [pl.BlockSpec(memory_space=pltpu.VMEM)],
        out_shape=[pltpu.SemaphoreType.DMA(())] * (2 * nw) + [pltpu.HBM(a.shape, a.dtype) for a in parts + bufs]
        + [jax.ShapeDtypeStruct((8, 128), F32)],
        input_output_aliases={k: 2 * nw + k for k in range(2 * nw)},
        compiler_params=pltpu.CompilerParams(has_side_effects=EFFECT),
    )(*[_in_hbm(a) for a in parts + bufs])
    return res[:nw], res[nw:2 * nw], res[2 * nw:3 * nw], res[3 * nw:4 * nw], res[4 * nw]


def _sibling_start(srcs, whole, *, name):
    nw = len(srcs)
    lands = [lax.empty((a.shape[0], a.shape[1] if whole else a.shape[1] // 2, a.shape[2]), a.dtype) for a in srcs]

    def body(*refs):
        src, land = refs[:nw], refs[nw:2 * nw]
        send, recv = refs[2 * nw:3 * nw], refs[3 * nw:4 * nw]
        token = refs[6 * nw]
        x, y, c, _ = _place()
        for w in range(nw):
            hr = srcs[w].shape[1] // 2
            rows = src[w] if whole else src[w].at[:, pl.ds((1 - c) * hr, hr)]
            pltpu.make_async_remote_copy(src_ref=rows, dst_ref=land[w], send_sem=send[w], recv_sem=recv[w],
                                         device_id=(x, y, 1 - c), device_id_type=MESH).start()
        token[...] = jnp.zeros_like(token)

    res = pl.pallas_call(
        body, name=name,
        in_specs=[HBM] * (2 * nw),
        out_specs=[SEM] * (2 * nw) + [HBM] * (2 * nw) + [pl.BlockSpec(memory_space=pltpu.VMEM)],
        out_shape=[pltpu.SemaphoreType.DMA(())] * (2 * nw) + [pltpu.HBM(a.shape, a.dtype) for a in srcs + lands]
        + [jax.ShapeDtypeStruct((8, 128), F32)],
        input_output_aliases={k: 2 * nw + k for k in range(2 * nw)},
        compiler_params=pltpu.CompilerParams(has_side_effects=EFFECT),
    )(*[_in_hbm(a) for a in srcs + lands])
    return res[:nw], res[nw:2 * nw], res[2 * nw:3 * nw], res[3 * nw:4 * nw], res[4 * nw]


def _transfer_wait(sends, recvs, thru, sizes, after, *, name):
    n = len(sends)
    flat = [a for group in thru for a in group]

    def body(*refs):
        bufs = refs[:len(flat)]
        send = refs[len(flat):len(flat) + n]
        recv = refs[len(flat) + n:len(flat) + 2 * n]
        token = refs[2 * len(flat) + 2 * n + 1]
        token[...] = jnp.zeros_like(token)
        x, y, c, _ = _place()
        pos = 0
        for k in range(n):
            slots, rows = sizes[k]
            region = bufs[pos].at[pl.ds(0, slots), pl.ds(0, rows)]
            pos += len(thru[k])
            cp = pltpu.make_async_remote_copy(src_ref=region, dst_ref=region, send_sem=send[k], recv_sem=recv[k],
                                              device_id=(x, y, 1 - c), device_id_type=MESH)
            cp.wait_send()
            cp.wait_recv()

    res = pl.pallas_call(
        body, name=name,
        in_specs=[HBM] * len(flat) + [SEM] * (2 * n) + [pl.BlockSpec(memory_space=pl.ANY)],
        out_specs=[HBM] * len(flat) + [pl.BlockSpec(memory_space=pltpu.VMEM)],
        out_shape=[pltpu.HBM(a.shape, a.dtype) for a in flat] + [jax.ShapeDtypeStruct((8, 128), F32)],
        input_output_aliases={k: k for k in range(len(flat))},
        compiler_params=pltpu.CompilerParams(has_side_effects=EFFECT),
    )(*flat, *sends, *recvs, after)
    out, pos = [], 0
    for group in thru:
        out.append(res[pos:pos + len(group)])
        pos += len(group)
    return out, res[len(flat)]


def _forward_to_sibling(bufs, after, *, name):
    nw = len(bufs)

    def body(*refs):
        outs = refs[nw + 1:2 * nw + 1]
        send, recv = refs[2 * nw + 1:]
        x, y, c, chips = _place()
        me, sibling = (x, y, c), (x, y, 1 - c)

        def d2d(w, j, which, to):
            cx, cy = chips[j]
            hr = bufs[w].shape[1] // 2
            rows = outs[w].at[2 * cx + cy, pl.ds(which * hr, hr)]
            return pltpu.make_async_remote_copy(
                src_ref=rows, dst_ref=rows, send_sem=send.at[N_PEER_CHIPS * w + j],
                recv_sem=recv.at[N_PEER_CHIPS * w + j], device_id=to, device_id_type=MESH)

        passed = [d2d(w, j, c, sibling) for w in range(nw) for j in range(N_PEER_CHIPS)]
        for cp in passed:
            cp.start()
        for w in range(nw):
            for j in range(N_PEER_CHIPS):
                d2d(w, j, 1 - c, me).wait_recv()
        for cp in passed:
            cp.wait_send()

    return pl.pallas_call(
        body, name=name,
        in_specs=[ANY] * (nw + 1), out_specs=[ANY] * nw,
        out_shape=[jax.ShapeDtypeStruct(a.shape, a.dtype) for a in bufs],
        input_output_aliases={w: w for w in range(nw)},
        scratch_shapes=[pltpu.SemaphoreType.DMA((N_PEER_CHIPS * nw,)), pltpu.SemaphoreType.DMA((N_PEER_CHIPS * nw,))],
    )(*bufs, after)


def _allreduce_small(p, after, *, name):
    R = p.shape[0]
    hr = R // 2

    def body(p_ref, _after_ref, out_ref, sib_ref, sum_ref, gat_ref, tot_ref, send, recv):
        x, y, c, chips = _place()
        s = 2 * x + y
        sibling = (x, y, 1 - c)
        rows = pl.ds(pl.multiple_of(c * hr, 8), hr)
        swap = pltpu.make_async_remote_copy(src_ref=p_ref, dst_ref=sib_ref, send_sem=send.at[0], recv_sem=recv.at[0],
                                            device_id=sibling, device_id_type=MESH)
        swap.start()
        swap.wait()
        sum_ref[...] = p_ref[...] + sib_ref[...]
        gat_ref[s] = sum_ref[rows, :]
        cps = [pltpu.make_async_remote_copy(src_ref=sum_ref.at[rows], dst_ref=gat_ref.at[s], send_sem=send.at[1 + j],
                                            recv_sem=recv.at[1 + j], device_id=(cx, cy, c), device_id_type=MESH)
               for j, (cx, cy) in enumerate(chips)]
        for cp in cps:
            cp.start()
        for cp in cps:
            cp.wait()
        tot_ref[...] = ((gat_ref[0] + gat_ref[1]) + gat_ref[2]) + gat_ref[3]
        out_ref[rows, :] = tot_ref[...]
        share = pltpu.make_async_remote_copy(src_ref=tot_ref, dst_ref=out_ref.at[rows], send_sem=send.at[4],
                                             recv_sem=recv.at[4], device_id=sibling, device_id_type=MESH)
        share.start()
        share.wait_send()
        other = out_ref.at[pl.ds(pl.multiple_of((1 - c) * hr, 8), hr)]
        pltpu.make_async_remote_copy(src_ref=other, dst_ref=other, send_sem=send.at[4], recv_sem=recv.at[4],
                                     device_id=(x, y, c), device_id_type=MESH).wait_recv()

    vmem = pl.BlockSpec(memory_space=pltpu.VMEM)
    return pl.pallas_call(
        body, name=name, in_specs=[vmem, ANY], out_specs=vmem,
        out_shape=jax.ShapeDtypeStruct((R, 128), F32),
        scratch_shapes=[pltpu.VMEM((R, 128), F32), pltpu.VMEM((R, 128), F32), pltpu.VMEM((NCHIP, hr, 128), F32),
                        pltpu.VMEM((hr, 128), F32), pltpu.SemaphoreType.DMA((5,)), pltpu.SemaphoreType.DMA((5,))],
    )(p, after)


def _select_half_bf16(g, half, add, slot, *, name):
    _, R, C = g.shape
    hr = R // 2
    tr = _pick_rows(hr, 16)
    nb = hr // tr
    sel = jnp.concatenate([jnp.reshape(half, (1,)).astype(jnp.int32), slot])

    def body(s_ref, g_ref, a_ref, o_ref, own_ref):
        val = (g_ref[...].astype(F32) + a_ref[...].astype(F32)).astype(BF16)
        o_ref[...] = val

        @pl.when(pl.program_id(1) == s_ref[1])
        def _():
            own_ref[...] = val

    g_spec = pl.BlockSpec((None, tr, C), lambda i, j, s: (j, s[0] * nb + i, 0))
    o_spec = pl.BlockSpec((None, tr, C), lambda i, j, s: (j, i, 0))
    own_spec = pl.BlockSpec((None, tr, C), lambda i, j, s: (s[1], i, 0))
    shape = jax.ShapeDtypeStruct((NCHIP, hr, C), BF16)
    return pl.pallas_call(
        body, name=name,
        grid_spec=pltpu.PrefetchScalarGridSpec(
            num_scalar_prefetch=1, grid=(nb, NCHIP), in_specs=[g_spec, o_spec], out_specs=[o_spec, own_spec]),
        out_shape=[shape, shape],
        compiler_params=_cp(("parallel", "arbitrary"), VMEM_MB),
    )(sel, g, add)


def _adamw_math(w, g, m, v):
    m = ADAM_B1 * m + (1.0 - ADAM_B1) * g
    v = ADAM_B2 * v + (1.0 - ADAM_B2) * (g * g)
    m_hat = m / (1.0 - ADAM_B1 ** ADAM_STEP)
    v_hat = v / (1.0 - ADAM_B2 ** ADAM_STEP)
    delta = -ADAM_LR * (m_hat / (jnp.sqrt(v_hat) + ADAM_EPS) + ADAM_WD * w)
    return delta, m, v


def _adamw(w, g_mine, g_sib, m, v, core, *, name):
    R, C = w.shape
    hr = R // 2
    tr = _pick_rows(hr, 16)
    nb = hr // tr
    row = pl.BlockSpec((tr, C), lambda hh, i, c: (hh * nb + i, 0))
    mine = pl.BlockSpec((NCHIP, tr, C), lambda hh, i, c: (0, jnp.where(hh == c[0], i, 0), 0))
    sibs = pl.BlockSpec((NCHIP, tr, C), lambda hh, i, c: (0, jnp.where(hh == c[0], 0, i), 0))

    def slot_sum(ref):
        acc = ref[0].astype(F32) + ref[1].astype(F32)
        for j in range(2, NCHIP):
            acc = acc + ref[j].astype(F32)
        return acc

    def body(c_ref, w_ref, gm_ref, gs_ref, m_ref, v_ref, go_ref, d_ref, mo_ref, vo_ref):
        gv = jnp.where(pl.program_id(0) == c_ref[0], slot_sum(gm_ref), slot_sum(gs_ref))
        d, mn, vn = _adamw_math(w_ref[...], gv, m_ref[...], v_ref[...])
        go_ref[...] = gv
        d_ref[...] = d
        mo_ref[...] = mn
        vo_ref[...] = vn

    return pl.pallas_call(
        body, name=name,
        grid_spec=pltpu.PrefetchScalarGridSpec(
            num_scalar_prefetch=1, grid=(2, nb),
            in_specs=[row, mine, sibs, row, row], out_specs=[row] * 4),
        out_shape=[jax.ShapeDtypeStruct((R, C), F32)] * 4,
        compiler_params=_cp(("parallel", "parallel"), VMEM_MB),
    )(core, w, g_mine, g_sib, m, v)


def _adamw_small(ws, gs, ms, vs, *, name):
    n = len(ws)

    def body(*refs):
        w_r, g_r, m_r, v_r = refs[:n], refs[n:2 * n], refs[2 * n:3 * n], refs[3 * n:4 * n]
        d_r, mo_r, vo_r = refs[4 * n:5 * n], refs[5 * n:6 * n], refs[6 * n:7 * n]
        for k in range(n):
            d, mn, vn = _adamw_math(w_r[k][...], g_r[k][...], m_r[k][...], v_r[k][...])
            d_r[k][...] = d
            mo_r[k][...] = mn
            vo_r[k][...] = vn

    shapes = [jax.ShapeDtypeStruct(w.shape, F32) for w in ws]
    res = pl.pallas_call(body, name=name, out_shape=shapes * 3)(*ws, *gs, *ms, *vs)
    return res[:n], res[n:2 * n], res[2 * n:]


_PACK_ROWS = 8


def _pack(parts):
    rows = []
    for a in parts:
        flat = a.reshape(-1)
        n = -(-flat.shape[0] // (_PACK_ROWS * 128)) * (_PACK_ROWS * 128)
        rows.append(jnp.pad(flat, (0, n - flat.shape[0])).reshape(-1, 128))
    total = sum(r.shape[0] for r in rows)
    if total % 16:
        rows.append(jnp.zeros((16 - total % 16, 128), F32))
    return jnp.concatenate(rows, axis=0)


def _unpack(p, shapes):
    out, r = [], 0
    for shp in shapes:
        n = math.prod(shp)
        nr = -(-n // (_PACK_ROWS * 128)) * _PACK_ROWS
        out.append(p[r:r + nr].reshape(-1)[:n].reshape(shp))
        r += nr
    return out


def kernel(x, mem, g_mix, w_in, ln_v_g, ln_v_b, w_s, b_s, conv_w, g_mem, w_kv, g_head, w_o, g_ffn, w_ffn1, w_ffn2, g_final, loss_target, m_g_mix, m_w_in, m_ln_v_g, m_ln_v_b, m_w_s, m_b_s, m_conv_w, m_g_mem, m_w_kv, m_g_head, m_w_o, m_g_ffn, m_w_ffn1, m_w_ffn2, m_g_final, v_g_mix, v_w_in, v_ln_v_g, v_ln_v_b, v_w_s, v_b_s, v_conv_w, v_g_mem, v_w_kv, v_g_head, v_w_o, v_g_ffn, v_w_ffn1, v_w_ffn2, v_g_final):
    sds = jax.ShapeDtypeStruct
    xi, yi = lax.axis_index("x"), lax.axis_index("y")
    shard = 2 * xi + yi
    x2d, mem2d, tgt = x[0], mem[0], loss_target[0]
    ws3, bs2 = w_s[0], b_s[0]
    g_final2 = g_final.reshape(1, D)
    dff4 = DFF // NCHIP
    din4 = DIN // NCHIP
    dcv4 = DC // NCHIP

    big = [w_in[0].T, w_kv[0], w_o[0], w_ffn1[0], w_ffn2[0]]
    big_names = ["w_in", "w_kv", "w_o", "w_ffn1", "w_ffn2"]
    slot = jnp.reshape(shard, (1,)).astype(jnp.int32)
    core = jnp.reshape(lax.axis_index("c"), (1,)).astype(jnp.int32)
    conv_pad = jnp.pad(conv_w[0], ((0, 16 - 3), (0, 256 - dcv4)))
    conv_slots = lax.dynamic_update_slice(jnp.zeros((NCHIP, 16, 256), F32), conv_pad[None], (shard, 0, 0))

    def gather_start(bufs, after, nm):
        return _allgather_start(bufs, after, name="ag_start_" + nm)

    def gather_wait(state, idx, after, nm):
        send, recv, bufs, _ = state
        got, token = _transfer_wait([send[k] for k in idx], [recv[k] for k in idx], [[bufs[k]] for k in idx],
                                    [(N_PEER_CHIPS, bufs[k].shape[1] // 2) for k in idx], after, name="ag_wait_" + nm)
        return [g[0] for g in got], token

    cast = lambda k, after: _cast_into_slot(big[k], slot, after, name="cast_" + big_names[k])
    ag_in = gather_start([cast(0, slot), conv_slots], slot, "in")
    bs_t = bs2.T

    h = _rms_fwd(x2d, g_mix, name="rms_mix", after=[ag_in[3]])
    mem_n = _rms_fwd(mem2d, g_mem, name="rms_mem", after=[h])
    rest_b = [cast(1, mem_n)]
    rest_b.append(cast(2, rest_b[0]))
    rest_b.append(cast(3, rest_b[1]))
    w2_b = cast(4, rest_b[2])
    got_in, tok = gather_wait(ag_in, [0, 1], w2_b, "in")
    ag_rest = gather_start(rest_b, tok, "kvo1")
    win4, conv4 = _forward_to_sibling(got_in, ag_rest[3], name="ag_fwd_in")
    w_in_t = win4.reshape(DIN, D)
    conv_full = conv4[:, :3, :dcv4].transpose(1, 0, 2).reshape(3, DC)
    (proj,) = _matmul(h, w_in_t, name="mm_proj", tb=True, M=S, N=DIN, K=D, tn=DIN // 2, outs=[sds((S, DIN), F32)])
    got_kvo, tok = gather_wait(ag_rest, [0, 1], proj, "kvo")
    wkv4, wo4 = _forward_to_sibling(got_kvo, tok, name="ag_fwd_kvo")
    w_kv_full = wkv4.reshape(D, 2 * DM)
    w_o_full = wo4.reshape(D, D)
    (kv,) = _matmul(mem_n, w_kv_full, name="mm_kv", M=NMEM, N=2 * DM, K=D, outs=[sds((NMEM, 2 * DM), F32)])
    heads, hn, ycv = _mix_fwd(proj, kv, ws3, bs_t, ln_v_g, ln_v_b, conv_full, g_head, name="mix_fwd")
    (x2,) = _matmul(hn, w_o_full, name="mm_wo", M=S, N=D, K=D, outs=[sds((S, D), F32)],
                    epi=lambda acc, res: (acc + res,), extras=[(x2d, _tile_spec())])
    h2 = _rms_fwd(x2, g_ffn, name="rms_ffn")
    got_w1, tok = gather_wait(ag_rest, [2], h2, "ffn1")
    ag_w2 = gather_start([w2_b], tok, "ffn2")
    (w14,) = _forward_to_sibling(got_w1, ag_w2[3], name="ag_fwd_ffn1")

    def w1_cols(tn, tk):
        nb = dff4 // tn
        return pl.BlockSpec((None, tk, tn), lambda j, i, k: (j // nb, k, j % nb))

    (act,) = _matmul(h2, w14, name="mm_ffn1", M=S, N=DFF, K=D, tn=dff4, b_spec=w1_cols, outs=[sds((S, DFF), BF16)],
                     epi=lambda acc: (jnp.square(jnp.maximum(acc, 0.0)),))
    got_w2, tok = gather_wait(ag_w2, [0], act, "ffn2")
    (w24,) = _forward_to_sibling(got_w2, tok, name="ag_fwd_ffn2")
    w2_full = w24.reshape(DFF, D)
    (x3,) = _matmul(act, w2_full, name="mm_ffn2", M=S, N=D, K=DFF, tm=2 * TM, outs=[sds((S, D), F32)],
                    epi=lambda acc, res: (acc + res,), extras=[(x2, _tile_spec())])

    ci = lax.axis_index("c")

    def rs_sibling(g4, nm):
        return _sibling_start([g4], False, name="rs_sib_" + nm)

    def rs_chips(state, after, nm):
        send, recv, g4, land, _ = state
        (((land_, g4_),), _) = _transfer_wait(send, recv, [[land[0], g4[0]]], [(NCHIP, land[0].shape[1])], after,
                                             name="rs_sibwait_" + nm)
        part, buf = _select_half_bf16(g4_, ci, land_, slot, name="rs_add_" + nm)
        return _scatter_start([part], [buf], name="rs_start_" + nm)

    def rs_end(state, after, nm):
        send, recv, parts, bufs, _ = state
        (((buf, _),), _) = _transfer_wait(send, recv, [[bufs[0], parts[0]]], [(N_PEER_CHIPS, bufs[0].shape[1])], after,
                                          name="rs_wait_" + nm)
        return _sibling_start([buf], True, name="rs_share_" + nm)

    big_m = [m_w_in[0].T, m_w_kv[0], m_w_o[0], m_w_ffn1[0], m_w_ffn2[0]]
    big_v = [v_w_in[0].T, v_w_kv[0], v_w_o[0], v_w_ffn1[0], v_w_ffn2[0]]
    big_out = {}

    def rs_finish(k, state, after):
        send, recv, mine, land, _ = state
        nm = big_names[k]
        (((land_, mine_),), _) = _transfer_wait(send, recv, [[land[0], mine[0]]], [(NCHIP, land[0].shape[1])], after,
                                               name="rs_sharewait_" + nm)
        big_out[nm] = _adamw(big[k], mine_, land_, big_m[k], big_v[k], core, name="adamw_" + nm)
        return big_out[nm][1]

    dx3, dx3b, dg_final, loss11 = _loss_bwd(x3, g_final2, tgt, name="loss_bwd")
    (dw2,) = _matmul(act, dx3b, name="mm_dw2", ta=True, M=DFF, N=D, K=S, tn=D, outs=[sds((DFF, D), BF16)])
    sib_w2 = rs_sibling(dw2.reshape(NCHIP, dff4, D), "w_ffn2")
    (dfb,) = _matmul(dx3b, w2_full, name="mm_dact", tb=True, M=S, N=DFF, K=D, tn=dff4, outs=[sds((S, DFF), BF16)],
                     epi=lambda acc, a: (acc * (2.0 * jnp.sqrt(a.astype(F32))),), extras=[(act, _tile_spec())],
                     after=[sib_w2[4]])
    rs_w2 = rs_chips(sib_w2, dfb, "w_ffn2")

    def dw1_out(tm, tn):
        nb = dff4 // tn
        return [pl.BlockSpec((None, tm, tn), lambda j, i, k: (j // nb, i, j % nb))]

    (dw1,) = _matmul(h2, dfb, name="mm_dw1", ta=True, M=D, N=DFF, K=S, tn=dff4, outs=[sds((NCHIP, D, dff4), BF16)],
                     out_specs=dw1_out, after=[rs_w2[4]])
    sib_w1 = rs_sibling(dw1, "w_ffn1")

    def w1_rows(tn, tk):
        kb = dff4 // tk
        return pl.BlockSpec((None, tn, tk), lambda j, i, k: (k // kb, j, k % kb))

    (dh2,) = _matmul(dfb, w14, name="mm_dh2", tb=True, M=S, N=D, K=DFF, tm=2 * TM, b_spec=w1_rows,
                     outs=[sds((S, D), F32)], after=[sib_w1[4]])
    rs_w1 = rs_chips(sib_w1, dh2, "w_ffn1")
    dx2, dx2b, dg_ffn = _rms_bwd(dh2, x2, g_ffn, dx3, name="rms_ffn_bwd", after=[rs_w1[4]])
    (dwo,) = _matmul(hn, dx2b, name="mm_dwo", ta=True, M=D, N=D, K=S, outs=[sds((D, D), BF16)])
    sib_wo = rs_sibling(dwo.reshape(NCHIP, D // NCHIP, D), "w_o")
    (dhn,) = _matmul(dx2b, w_o_full, name="mm_dhn", tb=True, M=S, N=D, K=D, outs=[sds((S, D), F32)],
                     after=[sib_wo[4]])
    rs_wo = rs_chips(sib_wo, dhn, "w_o")
    sh_w2 = rs_end(rs_w2, rs_wo[4], "w_ffn2")
    dproj, dkv, dws, dbs8, dlng, dlnb, dcw8, dgh = _mix_bwd(
        dhn, heads, proj, ycv, kv, ws3, bs_t, ln_v_g, ln_v_b, conv_full, g_head, sh_w2[4], name="mix_bwd")
    (dwin_t,) = _matmul(dproj, h, name="mm_dwin", ta=True, M=DIN, N=D, K=S, tm=DIN // 2, outs=[sds((DIN, D), BF16)])
    sib_win = rs_sibling(dwin_t.reshape(NCHIP, din4, D), "w_in")
    (dwkv,) = _matmul(mem_n, dkv, name="mm_dwkv", ta=True, M=D, N=2 * DM, K=NMEM, outs=[sds((D, 2 * DM), BF16)],
                      after=[sib_win[4]])
    sib_wkv = rs_sibling(dwkv.reshape(NCHIP, D // NCHIP, 2 * DM), "w_kv")
    (dh,) = _matmul(dproj, w_in_t, name="mm_dh", M=S, N=D, K=DIN, tk=DIN, outs=[sds((S, D), F32)],
                    after=[sib_wkv[4]])
    rs_win = rs_chips(sib_win, dh, "w_in")
    rs_wkv = rs_chips(sib_wkv, rs_win[4], "w_kv")
    sh_w1 = rs_end(rs_w1, rs_wkv[4], "w_ffn1")
    dx, dg_mix = _rms_bwd(dh, x2d, g_mix, dx2, name="rms_mix_bwd", want_bf=False, after=[sh_w1[4]])
    (dmem_n,) = _matmul(dkv, w_kv_full, name="mm_dmem", tb=True, M=NMEM, N=D, K=2 * DM, outs=[sds((NMEM, D), F32)],
                        after=[dx])
    (dg_mem,) = _rms_bwd(dmem_n, mem2d, g_mem, None, name="rms_mem_bwd", want_dx=False)
    sh_wo = rs_end(rs_wo, dg_mem, "w_o")
    done = rs_finish(4, sh_w2, sh_wo[4])
    done = rs_finish(3, sh_w1, done)
    sh_win = rs_end(rs_win, done, "w_in")
    sh_wkv = rs_end(rs_wkv, sh_win[4], "w_kv")
    done = rs_finish(2, sh_wo, sh_wkv[4])
    done = rs_finish(0, sh_win, done)
    done = rs_finish(1, sh_wkv, done)

    loss = lax.psum(loss11[0, 0], ("x", "y", "c"))

    small_names = ["g_mix", "ln_v_g", "ln_v_b", "w_s", "b_s", "conv_w", "g_mem", "g_head", "g_ffn", "g_final"]
    small_part = [dg_mix, dlng, dlnb, dws, dbs8[:, 0, :], dcw8[:3], dg_mem, dgh, dg_ffn, dg_final]
    small_shapes = [(1, D), (1, DS), (1, DS), (NSH, CHUNK, CHUNK), (NSH, CHUNK), (3, DC), (1, D), (1, D), (1, D), (1, D)]
    total = _allreduce_small(_pack(small_part), done, name="allreduce_small")
    small_g = _unpack(total, small_shapes)
    small_g[5] = lax.dynamic_slice(small_g[5], (0, shard * dcv4), (3, dcv4))
    small_w = [g_mix, ln_v_g, ln_v_b, ws3, bs2, conv_w[0], g_mem, g_head, g_ffn, g_final2]
    small_m = [m_g_mix, m_ln_v_g, m_ln_v_b, m_w_s[0], m_b_s[0], m_conv_w[0], m_g_mem, m_g_head, m_g_ffn,
               m_g_final.reshape(1, D)]
    small_v = [v_g_mix, v_ln_v_g, v_ln_v_b, v_w_s[0], v_b_s[0], v_conv_w[0], v_g_mem, v_g_head, v_g_ffn,
               v_g_final.reshape(1, D)]
    s_delta, s_m, s_v = _adamw_small(small_w, small_g, small_m, small_v, name="adamw_small")
    small_out = {nm: (g, d, mn, vn) for nm, g, d, mn, vn in zip(small_names, small_g, s_delta, s_m, s_v)}

    order = ["g_mix", "w_in", "ln_v_g", "ln_v_b", "w_s", "b_s", "conv_w", "g_mem", "w_kv", "g_head", "w_o",
             "g_ffn", "w_ffn1", "w_ffn2", "g_final"]
    like = dict(g_mix=g_mix, w_in=w_in, ln_v_g=ln_v_g, ln_v_b=ln_v_b, w_s=w_s, b_s=b_s, conv_w=conv_w, g_mem=g_mem,
                w_kv=w_kv, g_head=g_head, w_o=w_o, g_ffn=g_ffn, w_ffn1=w_ffn1, w_ffn2=w_ffn2, g_final=g_final)
    res = {**big_out, **small_out}
    res["w_in"] = [a.T for a in res["w_in"]]
    outs = [loss, dx[None]]
    for k in range(4):
        outs += [res[nm][k].reshape(like[nm].shape) for nm in order]
    return tuple(outs)
```

```python
import math

import jax
import jax.numpy as jnp
from jax import lax
from jax.experimental import pallas as pl
from jax.experimental.pallas import tpu as pltpu

F32 = jnp.float32
BF16 = jnp.bfloat16
MESH = pl.DeviceIdType.MESH

D = 2048
S = 2048
HD = 128
NH = D // HD
NMH = 4
NSH = (NH - NMH) // 2
NCH = NH - NMH - NSH
DS = NSH * HD
DC = NCH * HD
DM = NMH * HD
DIN = 2 * DS + 3 * DC + DM
CHUNK = 128
NMEM = 256
DFF = 4 * D
EPS = 1e-6
NCHIP = 4
SCALE = HD ** -0.5

ADAM_LR = 0.001
ADAM_B1 = 0.9
ADAM_B2 = 0.999
ADAM_EPS = 1e-08
ADAM_WD = 0.01
ADAM_STEP = 10

TR_EW = 256
TR_MIX = 256
TM = 512
TN = 1024
TK = 2048
VMEM_MB = 56
HALO = 8


def _pick(n, target, q=128):
    best = None
    for t in range(q, min(n, target) + 1, q):
        if n % t == 0:
            best = t
    return n if best is None else best


def _pick_rows(n, q):
    below = _pick(n, TR_EW, q)
    if 2 * below >= TR_EW:
        return below
    above = [t for t in range(TR_EW, min(n, 4 * TR_EW) + 1, q) if n % t == 0]
    return above[0] if above else below


def _cp(sem=None, vmem_mb=None, **kw):
    d = dict(kw)
    if sem is not None:
        d["dimension_semantics"] = sem
    if vmem_mb is not None:
        d["vmem_limit_bytes"] = vmem_mb << 20
    return pltpu.CompilerParams(**d)


def _gelu(x):
    z = 0.7978845608028654 * (x + 0.044715 * (x * x * x))
    return 0.5 * x * (1.0 + jnp.tanh(z))


def _gelu_grad(x):
    x2 = x * x
    t = jnp.tanh(0.7978845608028654 * (x + 0.044715 * (x2 * x)))
    return 0.5 * (1.0 + t) + 0.5 * x * (1.0 - t * t) * (0.7978845608028654 * (1.0 + 3.0 * 0.044715 * x2))


def _matmul(a, b, *, name, ta=False, tb=False, M, N, K, tm=None, tn=None, tk=None, outs, epi=None,
            extras=(), b_spec=None, out_specs=None, after=()):
    n_after = len(after)
    tm = _pick(M, TM if tm is None else tm, 8)
    tn = _pick(N, TN if tn is None else tn)
    tk = _pick(K, TK if tk is None else tk)
    nk = K // tk
    grid = (N // tn, M // tm, nk)
    a_spec = (pl.BlockSpec((tk, tm), lambda j, i, k: (k, i)) if ta
              else pl.BlockSpec((tm, tk), lambda j, i, k: (i, k)))
    if b_spec is None:
        b_spec = (pl.BlockSpec((tn, tk), lambda j, i, k: (j, k)) if tb
                  else pl.BlockSpec((tk, tn), lambda j, i, k: (k, j)))
    else:
        b_spec = b_spec(tn, tk)
    if out_specs is None:
        out_specs = [pl.BlockSpec((tm, tn), lambda j, i, k: (i, j)) for _ in outs]
    else:
        out_specs = out_specs(tm, tn)
    dn = (((0 if ta else 1,), (1 if tb else 0,)), ((), ()))
    n_ex, n_out = len(extras), len(outs)

    def body(*refs):
        a_ref, b_ref = refs[0], refs[1]
        ex = refs[2:2 + n_ex]
        first_out = 2 + n_ex + n_after
        o = refs[first_out:first_out + n_out]
        acc = refs[first_out + n_out:]
        part = lax.dot_general(a_ref[...].astype(BF16), b_ref[...].astype(BF16), dn,
                               preferred_element_type=F32)

        def finish(val):
            res = (val,) if epi is None else epi(val, *[e[...] for e in ex])
            for r, o_ref in zip(res, o):
                o_ref[...] = r.astype(o_ref.dtype)

        if nk == 1:
            finish(part)
        else:
            k = pl.program_id(2)

            @pl.when(k == 0)
            def _():
                acc[0][...] = part

            @pl.when(k > 0)
            def _():
                acc[0][...] += part

            @pl.when(k == nk - 1)
            def _():
                finish(acc[0][...])

    return pl.pallas_call(
        body, name=name, grid=grid,
        in_specs=[a_spec, b_spec] + [sp(tm, tn) for _, sp in extras] + [ANY] * n_after,
        out_specs=out_specs, out_shape=outs,
        scratch_shapes=([pltpu.VMEM((tm, tn), F32)] if nk > 1 else []),
        compiler_params=_cp(("parallel", "parallel", "arbitrary"), VMEM_MB),
    )(a, b, *[arr for arr, _ in extras], *after)


def _tile_spec():
    return lambda tm, tn: pl.BlockSpec((tm, tn), lambda j, i, k: (i, j))


def _cast_into_slot(w, slot, after, *, name):
    R, C = w.shape
    tr = _pick_rows(R, 16)

    def body(s_ref, w_ref, _after_ref, o_ref):
        o_ref[...] = w_ref[...].astype(BF16)

    return pl.pallas_call(
        body, name=name,
        grid_spec=pltpu.PrefetchScalarGridSpec(
            num_scalar_prefetch=1, grid=(R // tr,),
            in_specs=[pl.BlockSpec((tr, C), lambda i, s: (i, 0)), ANY],
            out_specs=pl.BlockSpec((None, tr, C), lambda i, s: (s[0], i, 0))),
        out_shape=jax.ShapeDtypeStruct((NCHIP, R, C), BF16),
        compiler_params=_cp(("parallel",), VMEM_MB),
    )(slot, w, after)


def _rms_fwd(x, g, *, name, after=()):
    R, C = x.shape
    tr = _pick(R, TR_EW, 16)
    n_after = len(after)

    def body(x_ref, g_ref, *rest):
        o_ref = rest[n_after]
        xv = x_ref[...]
        r = lax.rsqrt(jnp.mean(xv * xv, axis=-1, keepdims=True) + EPS)
        o_ref[...] = ((xv * r) * g_ref[...]).astype(BF16)

    return pl.pallas_call(
        body, name=name, grid=(R // tr,),
        in_specs=[pl.BlockSpec((tr, C), lambda i: (i, 0)), pl.BlockSpec((1, C), lambda i: (0, 0))] + [ANY] * n_after,
        out_specs=pl.BlockSpec((tr, C), lambda i: (i, 0)),
        out_shape=jax.ShapeDtypeStruct((R, C), BF16),
        compiler_params=_cp(("parallel",), VMEM_MB),
    )(x, g, *after)


def _rms_bwd(dh, x, g, dres, *, name, want_dx=True, want_bf=True, after=()):
    R, C = x.shape
    tr = _pick(R, TR_EW, 16)
    has_res = dres is not None
    row = pl.BlockSpec((tr, C), lambda i: (i, 0))
    vec = pl.BlockSpec((1, C), lambda i: (0, 0))

    def body(*refs):
        dh_ref, x_ref, g_ref = refs[:3]
        pos = 3
        dres_ref = None
        if has_res:
            dres_ref = refs[pos]
            pos += 1
        outs = refs[pos + len(after):]
        i = pl.program_id(0)
        xv = x_ref[...]
        r = lax.rsqrt(jnp.mean(xv * xv, axis=-1, keepdims=True) + EPS)
        xh = xv * r
        dhv = dh_ref[...]
        dg_ref = outs[-1]
        dgp = jnp.sum(dhv * xh, axis=0, keepdims=True)

        @pl.when(i == 0)
        def _():
            dg_ref[...] = dgp

        @pl.when(i > 0)
        def _():
            dg_ref[...] += dgp

        if want_dx:
            t = dhv * g_ref[...]
            dx = r * (t - xh * jnp.mean(t * xh, axis=-1, keepdims=True))
            if has_res:
                dx = dx + dres_ref[...]
            outs[0][...] = dx
            if want_bf:
                outs[1][...] = dx.astype(BF16)

    in_specs = [row, row, vec] + ([row] if has_res else []) + [ANY] * len(after)
    out_specs, out_shape = [], []
    if want_dx:
        out_specs.append(row)
        out_shape.append(jax.ShapeDtypeStruct((R, C), F32))
        if want_bf:
            out_specs.append(row)
            out_shape.append(jax.ShapeDtypeStruct((R, C), BF16))
    out_specs.append(vec)
    out_shape.append(jax.ShapeDtypeStruct((1, C), F32))
    args = [dh, x, g] + ([dres] if has_res else []) + list(after)
    return pl.pallas_call(
        body, name=name, grid=(R // tr,), in_specs=in_specs, out_specs=out_specs, out_shape=out_shape,
        compiler_params=_cp(("arbitrary",), VMEM_MB),
    )(*args)


def _loss_bwd(x3, g, tgt, *, name):
    R, C = x3.shape
    tr = _pick(R, TR_EW, 16)
    n = R // tr
    row = pl.BlockSpec((tr, C), lambda i: (i, 0))
    vec = pl.BlockSpec((1, C), lambda i: (0, 0))

    def body(x_ref, g_ref, t_ref, dx_ref, dxb_ref, dg_ref, loss_ref, acc_ref):
        i = pl.program_id(0)
        xv = x_ref[...]
        gv = g_ref[...]
        r = lax.rsqrt(jnp.mean(xv * xv, axis=-1, keepdims=True) + EPS)
        xh = xv * r
        e = xh * gv - t_ref[...]
        dy = e * (1.0 / C)
        sq = jnp.sum(e * e, axis=0, keepdims=True)
        dgp = jnp.sum(dy * xh, axis=0, keepdims=True)

        @pl.when(i == 0)
        def _():
            acc_ref[...] = sq
            dg_ref[...] = dgp

        @pl.when(i > 0)
        def _():
            acc_ref[...] += sq
            dg_ref[...] += dgp

        t = dy * gv
        dx = r * (t - xh * jnp.mean(t * xh, axis=-1, keepdims=True))
        dx_ref[...] = dx
        dxb_ref[...] = dx.astype(BF16)

        @pl.when(i == n - 1)
        def _():
            loss_ref[...] = jnp.sum(acc_ref[...], axis=-1, keepdims=True) * (0.5 / C)

    return pl.pallas_call(
        body, name=name, grid=(n,),
        in_specs=[row, vec, row],
        out_specs=[row, row, vec, pl.BlockSpec((1, 1), lambda i: (0, 0))],
        out_shape=[jax.ShapeDtypeStruct((R, C), F32), jax.ShapeDtypeStruct((R, C), BF16),
                   jax.ShapeDtypeStruct((1, C), F32), jax.ShapeDtypeStruct((1, 1), F32)],
        scratch_shapes=[pltpu.VMEM((1, C), F32)],
        compiler_params=_cp(("arbitrary",), VMEM_MB),
    )(x3, g, tgt)


def _offsets():
    u0 = 0
    v0 = DS
    b0 = 2 * DS
    c0 = b0 + DC
    x0 = c0 + DC
    q0 = x0 + DC
    return u0, v0, b0, c0, x0, q0


def _tri_mask(lower):
    r = lax.broadcasted_iota(jnp.int32, (CHUNK, CHUNK), 0)
    c = lax.broadcasted_iota(jnp.int32, (CHUNK, CHUNK), 1)
    return (r >= c) if lower else (c >= r)


def _layer_norm_stats(vg):
    mu = jnp.mean(vg, axis=-1, keepdims=True)
    vc = vg - mu
    rstd = lax.rsqrt(jnp.mean(vc * vc, axis=-1, keepdims=True) + EPS)
    return vc * rstd, rstd


def _softmax_rows(qh, kh):
    s = lax.dot_general(qh, kh, (((1,), (1,)), ((), ())), preferred_element_type=F32)
    m = jnp.max(s, axis=-1, keepdims=True)
    e = jnp.exp(s - m)
    return e / jnp.sum(e, axis=-1, keepdims=True)


def _mix_fwd(proj, kv, w_s, bs_t, ln_g, ln_b, conv_w, g_head, *, name):
    assert DS == DC
    tr = _pick(S, TR_MIX, CHUNK)
    n = S // tr
    nck = tr // CHUNK
    u0, v0, b0, c0, x0, q0 = _offsets()
    hb = tr // HALO

    def body(p_ref, cprev_ref, xprev_ref, kv_ref, ws_ref, bst_ref, lng_ref, lnb_ref, cw_ref, gh_ref,
             heads_ref, hn_ref, ycv_ref, buf_ref):
        i = pl.program_id(0)

        def emit(col, val):
            rs = lax.rsqrt(jnp.mean(val * val, axis=-1, keepdims=True) + EPS)
            heads_ref[:, col:col + HD] = val
            hn_ref[:, col:col + HD] = ((val * rs) * gh_ref[:, col:col + HD]).astype(BF16)

        vhat, _ = _layer_norm_stats(_gelu(p_ref[:, v0:v0 + DS]))
        vnb = (vhat * lng_ref[...] + lnb_ref[...]).astype(BF16)
        low = _tri_mask(True)
        for h in range(NSH):
            wt = jnp.where(low, ws_ref[h], 0.0).astype(BF16)
            bcol = bst_ref[:, h:h + 1]
            parts = []
            for c in range(nck):
                blk = vnb[c * CHUNK:(c + 1) * CHUNK, h * HD:(h + 1) * HD]
                parts.append(jnp.dot(wt, blk, preferred_element_type=F32) + bcol)
            mixed = parts[0] if nck == 1 else jnp.concatenate(parts, axis=0)
            emit(h * HD, _gelu(p_ref[:, u0 + h * HD:u0 + (h + 1) * HD]) * mixed)

        xc = p_ref[:, c0:c0 + DC] * p_ref[:, x0:x0 + DC]
        prev = cprev_ref[...] * xprev_ref[...]
        buf_ref[0:HALO, :] = jnp.where(i > 0, prev, 0.0)
        buf_ref[HALO:HALO + tr, :] = xc
        y = (cw_ref[2:3, :] * xc + cw_ref[1:2, :] * buf_ref[HALO - 1:HALO - 1 + tr, :]
             + cw_ref[0:1, :] * buf_ref[HALO - 2:HALO - 2 + tr, :])
        ycv_ref[...] = y
        cout = p_ref[:, b0:b0 + DC] * y
        for h in range(NCH):
            emit(DS + h * HD, cout[:, h * HD:(h + 1) * HD])

        for h in range(NMH):
            qh = (p_ref[:, q0 + h * HD:q0 + (h + 1) * HD] * SCALE).astype(BF16)
            kh = kv_ref[:, h * HD:(h + 1) * HD].astype(BF16)
            vh = kv_ref[:, DM + h * HD:DM + (h + 1) * HD].astype(BF16)
            p = _softmax_rows(qh, kh)
            emit(DS + DC + h * HD, jnp.dot(p.astype(BF16), vh, preferred_element_type=F32))

    full = lambda shape: pl.BlockSpec(shape, lambda i: (0,) * len(shape))
    halo_c = pl.BlockSpec((HALO, DC), lambda i: (jnp.maximum(i * hb - 1, 0), c0 // DC))
    halo_x = pl.BlockSpec((HALO, DC), lambda i: (jnp.maximum(i * hb - 1, 0), x0 // DC))
    return pl.pallas_call(
        body, name=name, grid=(n,),
        in_specs=[pl.BlockSpec((tr, DIN), lambda i: (i, 0)), halo_c, halo_x,
                  full((NMEM, 2 * DM)), full((NSH, CHUNK, CHUNK)), full((CHUNK, NSH)),
                  full((1, DS)), full((1, DS)), full((3, DC)), full((1, D))],
        out_specs=[pl.BlockSpec((tr, D), lambda i: (i, 0)), pl.BlockSpec((tr, D), lambda i: (i, 0)),
                   pl.BlockSpec((tr, DC), lambda i: (i, 0))],
        out_shape=[jax.ShapeDtypeStruct((S, D), F32), jax.ShapeDtypeStruct((S, D), BF16),
                   jax.ShapeDtypeStruct((S, DC), F32)],
        scratch_shapes=[pltpu.VMEM((tr + HALO, DC), F32)],
        compiler_params=_cp(("parallel",), VMEM_MB),
    )(proj, proj, proj, kv, w_s, bs_t, ln_g, ln_b, conv_w, g_head)


def _mix_bwd(dhn, heads, proj, ycv, kv, w_s, bs_t, ln_g, ln_b, conv_w, g_head, after, *, name):
    assert DS == DC
    tr = _pick(S, TR_MIX, CHUNK)
    n = S // tr
    nck = tr // CHUNK
    u0, v0, b0, c0, x0, q0 = _offsets()
    hb = tr // HALO
    last_hb = S // HALO - 1

    def body(dhn_ref, heads_ref, p_ref, ycv_ref, dhn_nx_ref, heads_nx_ref, b_nx_ref, kv_ref, ws_ref, bst_ref,
             lng_ref, lnb_ref, cw_ref, gh_ref, _after_ref,
             dp_ref, dkv_ref, dws_ref, dbs_ref, dlng_ref, dlnb_ref, dcw_ref, dgh_ref, buf_ref, dvn_ref):
        i = pl.program_id(0)

        @pl.when(i == 0)
        def _():
            dkv_ref[...] = jnp.zeros_like(dkv_ref)
            dws_ref[...] = jnp.zeros_like(dws_ref)
            dbs_ref[...] = jnp.zeros_like(dbs_ref)
            dlng_ref[...] = jnp.zeros_like(dlng_ref)
            dlnb_ref[...] = jnp.zeros_like(dlnb_ref)
            dcw_ref[...] = jnp.zeros_like(dcw_ref)
            dgh_ref[...] = jnp.zeros_like(dgh_ref)

        def head_bwd(a, dn, gh):
            rs = lax.rsqrt(jnp.mean(a * a, axis=-1, keepdims=True) + EPS)
            ah = a * rs
            t = dn * gh
            return rs * (t - ah * jnp.mean(t * ah, axis=-1, keepdims=True)), jnp.sum(dn * ah, axis=0, keepdims=True)

        def head_grad(col):
            da, dg = head_bwd(heads_ref[:, col:col + HD], dhn_ref[:, col:col + HD], gh_ref[:, col:col + HD])
            dgh_ref[:, col:col + HD] += dg
            return da

        v = p_ref[:, v0:v0 + DS]
        vhat, rstd = _layer_norm_stats(_gelu(v))
        vnb = (vhat * lng_ref[...] + lnb_ref[...]).astype(BF16)
        low = _tri_mask(True)
        ones = jnp.ones((HALO, HD), BF16)
        for h in range(NSH):
            w_h = ws_ref[h]
            wt = jnp.where(low, w_h, 0.0).astype(BF16)
            bcol = bst_ref[:, h:h + 1]
            da = head_grad(h * HD)
            u = p_ref[:, u0 + h * HD:u0 + (h + 1) * HD]
            ug = _gelu(u)
            dws = jnp.zeros((CHUNK, CHUNK), F32)
            dbs = jnp.zeros((HALO, CHUNK), F32)
            mixed_parts = []
            for c in range(nck):
                rows = slice(c * CHUNK, (c + 1) * CHUNK)
                blk = vnb[rows, h * HD:(h + 1) * HD]
                mixed_parts.append(jnp.dot(wt, blk, preferred_element_type=F32) + bcol)
                dmb = (da[rows] * ug[rows]).astype(BF16)
                dws = dws + lax.dot_general(dmb, blk, (((1,), (1,)), ((), ())), preferred_element_type=F32)
                dbs = dbs + lax.dot_general(ones, dmb, (((1,), (1,)), ((), ())), preferred_element_type=F32)
                dvn_ref[c * CHUNK:(c + 1) * CHUNK, h * HD:(h + 1) * HD] = lax.dot_general(
                    wt, dmb, (((0,), (0,)), ((), ())), preferred_element_type=F32)
            mixed = mixed_parts[0] if nck == 1 else jnp.concatenate(mixed_parts, axis=0)
            dp_ref[:, u0 + h * HD:u0 + (h + 1) * HD] = ((da * mixed) * _gelu_grad(u)).astype(BF16)
            dws_ref[h] += jnp.where(low, dws, 0.0)
            dbs_ref[h] += dbs
        dvn = dvn_ref[...]
        dlng_ref[...] += jnp.sum(dvn * vhat, axis=0, keepdims=True)
        dlnb_ref[...] += jnp.sum(dvn, axis=0, keepdims=True)
        dvh = dvn * lng_ref[...]
        dvg = rstd * (dvh - jnp.mean(dvh, axis=-1, keepdims=True)
                      - vhat * jnp.mean(dvh * vhat, axis=-1, keepdims=True))
        dp_ref[:, v0:v0 + DS] = (dvg * _gelu_grad(v)).astype(BF16)

        dc = jnp.concatenate([head_grad(DS + h * HD) for h in range(NCH)], axis=1)
        dc_nx = jnp.concatenate(
            [head_bwd(heads_nx_ref[:, h * HD:(h + 1) * HD], dhn_nx_ref[:, h * HD:(h + 1) * HD],
                      gh_ref[:, DS + h * HD:DS + (h + 1) * HD])[0] for h in range(NCH)], axis=1)
        bg = p_ref[:, b0:b0 + DC]
        cg = p_ref[:, c0:c0 + DC]
        xin = p_ref[:, x0:x0 + DC]
        dp_ref[:, b0:b0 + DC] = (dc * ycv_ref[...]).astype(BF16)
        dyv = dc * bg
        buf_ref[0:tr, :] = dyv
        buf_ref[tr:tr + HALO, :] = jnp.where(i < n - 1, dc_nx * b_nx_ref[...], 0.0)
        sh1 = buf_ref[1:1 + tr, :]
        sh0 = buf_ref[2:2 + tr, :]
        dxc = cw_ref[2:3, :] * dyv + cw_ref[1:2, :] * sh1 + cw_ref[0:1, :] * sh0
        xc = cg * xin
        dp_ref[:, c0:c0 + DC] = (dxc * xin).astype(BF16)
        dp_ref[:, x0:x0 + DC] = (dxc * cg).astype(BF16)
        dcw_ref[0:1, :] += jnp.sum(sh0 * xc, axis=0, keepdims=True)
        dcw_ref[1:2, :] += jnp.sum(sh1 * xc, axis=0, keepdims=True)
        dcw_ref[2:3, :] += jnp.sum(dyv * xc, axis=0, keepdims=True)

        for h in range(NMH):
            do = head_grad(DS + DC + h * HD).astype(BF16)
            qh = (p_ref[:, q0 + h * HD:q0 + (h + 1) * HD] * SCALE).astype(BF16)
            kh = kv_ref[:, h * HD:(h + 1) * HD].astype(BF16)
            vh = kv_ref[:, DM + h * HD:DM + (h + 1) * HD].astype(BF16)
            p = _softmax_rows(qh, kh)
            dpr = lax.dot_general(do, vh, (((1,), (1,)), ((), ())), preferred_element_type=F32)
            ds = (p * (dpr - jnp.sum(dpr * p, axis=-1, keepdims=True))).astype(BF16)
            dp_ref[:, q0 + h * HD:q0 + (h + 1) * HD] = (
                jnp.dot(ds, kh, preferred_element_type=F32) * SCALE).astype(BF16)
            dkv_ref[:, h * HD:(h + 1) * HD] += lax.dot_general(
                ds, qh, (((0,), (0,)), ((), ())), preferred_element_type=F32)
            dkv_ref[:, DM + h * HD:DM + (h + 1) * HD] += lax.dot_general(
                p.astype(BF16), do, (((0,), (0,)), ((), ())), preferred_element_type=F32)

    full = lambda shape: pl.BlockSpec(shape, lambda i: (0,) * len(shape))
    row = lambda c: pl.BlockSpec((tr, c), lambda i: (i, 0))
    nxt = lambda col: pl.BlockSpec((HALO, DC), lambda i: (jnp.minimum((i + 1) * hb, last_hb), col))
    return pl.pallas_call(
        body, name=name, grid=(n,),
        in_specs=[row(D), row(D), row(DIN), row(DC), nxt(DS // DC), nxt(DS // DC), nxt(b0 // DC),
                  full((NMEM, 2 * DM)), full((NSH, CHUNK, CHUNK)), full((CHUNK, NSH)),
                  full((1, DS)), full((1, DS)), full((3, DC)), full((1, D)), ANY],
        out_specs=[row(DIN), full((NMEM, 2 * DM)), full((NSH, CHUNK, CHUNK)), full((NSH, HALO, CHUNK)),
                   full((1, DS)), full((1, DS)), full((HALO, DC)), full((1, D))],
        out_shape=[jax.ShapeDtypeStruct((S, DIN), BF16), jax.ShapeDtypeStruct((NMEM, 2 * DM), F32),
                   jax.ShapeDtypeStruct((NSH, CHUNK, CHUNK), F32), jax.ShapeDtypeStruct((NSH, HALO, CHUNK), F32),
                   jax.ShapeDtypeStruct((1, DS), F32), jax.ShapeDtypeStruct((1, DS), F32),
                   jax.ShapeDtypeStruct((HALO, DC), F32), jax.ShapeDtypeStruct((1, D), F32)],
        scratch_shapes=[pltpu.VMEM((tr + HALO, DC), F32), pltpu.VMEM((tr, DS), F32)],
        compiler_params=_cp(("arbitrary",), VMEM_MB),
    )(dhn, heads, proj, ycv, dhn, heads, proj, kv, w_s, bs_t, ln_g, ln_b, conv_w, g_head, after)


def _place():
    x, y, c = lax.axis_index("x"), lax.axis_index("y"), lax.axis_index("c")
    chips = [(1 - x, y), (x, 1 - y), (1 - x, 1 - y)]
    return x, y, c, chips


ANY = pl.BlockSpec(memory_space=pl.ANY)


HBM = pl.BlockSpec(memory_space=pltpu.HBM)
SEM = pl.BlockSpec(memory_space=pltpu.SEMAPHORE)
EFFECT = pltpu.SideEffectType.DATAFLOW_SIDE_EFFECTING
N_PEER_CHIPS = 3
N_NEIGHBOUR_CHIPS = 2
CONV_PAD = (32, 256)


def _in_hbm(a):
    return pltpu.with_memory_space_constraint(a, pltpu.HBM)


def _allgather_start(bufs, after, *, name):
    nw = len(bufs)

    def body(*refs):
        ins, send, recv = refs[:nw], refs[nw + 1:2 * nw + 1], refs[2 * nw + 1:3 * nw + 1]
        token = refs[4 * nw + 1]
        x, y, c, chips = _place()
        s = 2 * x + y
        for w in range(nw):
            hr = bufs[w].shape[1] // 2
            rows = ins[w].at[s, pl.ds(c * hr, hr)]
            for cx, cy in chips[:N_NEIGHBOUR_CHIPS]:
                pltpu.make_async_remote_copy(src_ref=rows, dst_ref=rows, send_sem=send[w], recv_sem=recv[w],
                                             device_id=(cx, cy, c), device_id_type=MESH).start()
        token[...] = jnp.zeros_like(token)

    res = pl.pallas_call(
        body, name=name,
        in_specs=[HBM] * nw + [ANY],
        out_specs=[SEM] * (2 * nw) + [HBM] * nw + [pl.BlockSpec(memory_space=pltpu.VMEM)],
        out_shape=[pltpu.SemaphoreType.DMA(())] * (2 * nw) + [pltpu.HBM(a.shape, a.dtype) for a in bufs]
        + [jax.ShapeDtypeStruct((8, 128), F32)],
        input_output_aliases={w: 2 * nw + w for w in range(nw)},
        compiler_params=pltpu.CompilerParams(has_side_effects=EFFECT),
    )(*[_in_hbm(a) for a in bufs], after)
    return res[:nw], res[nw:2 * nw], res[2 * nw:3 * nw], res[3 * nw]


def _scatter_start(parts, bufs, *, name):
    nw = len(parts)

    def body(*refs):
        src, dst = refs[:nw], refs[nw:2 * nw]
        send, recv = refs[2 * nw:3 * nw], refs[3 * nw:4 * nw]
        token = refs[6 * nw]
        x, y, c, chips = _place()
        s = 2 * x + y
        for w in range(nw):
            for cx, cy in chips:
                pltpu.make_async_remote_copy(src_ref=src[w].at[2 * cx + cy], dst_ref=dst[w].at[s], send_sem=send[w],
                                             recv_sem=recv[w], device_id=(cx, cy, c), device_id_type=MESH).start()
        token[...] = jnp.zeros_like(token)

    res = pl.pallas_call(
        body, name=name,
        in_specs=[HBM] * (2 * nw),
        out_specs=[SEM] * (2 * nw) + [HBM] * (2 * nw) + [pl.BlockSpec(memory_space=pltpu.VMEM)],
        out_shape=[pltpu.SemaphoreType.DMA(())] * (2 * nw) + [pltpu.HBM(a.shape, a.dtype) for a in parts + bufs]
        + [jax.ShapeDtypeStruct((8, 128), F32)],
        input_output_aliases={k: 2 * nw + k for k in range(2 * nw)},
        compiler_params=pltpu.CompilerParams(has_side_effects=EFFECT),
    )(*[_in_hbm(a) for a in parts + bufs])
    return res[:nw], res[nw:2 * nw], res[2 * nw:3 * nw], res[3 * nw:4 * nw], res[4 * nw]


def _sibling_start(srcs, whole, *, name):
    nw = len(srcs)
    lands = [lax.empty((a.shape[0], a.shape[1] if whole else a.shape[1] // 2, a.shape[2]), a.dtype) for a in srcs]

    def body(*refs):
        src, land = refs[:nw], refs[nw:2 * nw]
        send, recv = refs[2 * nw:3 * nw], refs[3 * nw:4 * nw]
        token = refs[6 * nw]
        x, y, c, _ = _place()
        for w in range(nw):
            hr = srcs[w].shape[1] // 2
            rows = src[w] if whole else src[w].at[:, pl.ds((1 - c) * hr, hr)]
            pltpu.make_async_remote_copy(src_ref=rows, dst_ref=land[w], send_sem=send[w], recv_sem=recv[w],
                                         device_id=(x, y, 1 - c), device_id_type=MESH).start()
        token[...] = jnp.zeros_like(token)

    res = pl.pallas_call(
        body, name=name,
        in_specs=[HBM] * (2 * nw),
        out_specs=[SEM] * (2 * nw) + [HBM] * (2 * nw) + [pl.BlockSpec(memory_space=pltpu.VMEM)],
        out_shape=[pltpu.SemaphoreType.DMA(())] * (2 * nw) + [pltpu.HBM(a.shape, a.dtype) for a in srcs + lands]
        + [jax.ShapeDtypeStruct((8, 128), F32)],
        input_output_aliases={k: 2 * nw + k for k in range(2 * nw)},
        compiler_params=pltpu.CompilerParams(has_side_effects=EFFECT),
    )(*[_in_hbm(a) for a in srcs + lands])
    return res[:nw], res[nw:2 * nw], res[2 * nw:3 * nw], res[3 * nw:4 * nw], res[4 * nw]


def _transfer_wait(sends, recvs, thru, sizes, after, *, name):
    n = len(sends)
    flat = [a for group in thru for a in group]

    def body(*refs):
        bufs = refs[:len(flat)]
        send = refs[len(flat):len(flat) + n]
        recv = refs[len(flat) + n:len(flat) + 2 * n]
        token = refs[2 * len(flat) + 2 * n + 1]
        token[...] = jnp.zeros_like(token)
        x, y, c, _ = _place()
        pos = 0
        for k in range(n):
            slots, rows = sizes[k]
            region = bufs[pos].at[pl.ds(0, slots), pl.ds(0, rows)]
            pos += len(thru[k])
            cp = pltpu.make_async_remote_copy(src_ref=region, dst_ref=region, send_sem=send[k], recv_sem=recv[k],
                                              device_id=(x, y, 1 - c), device_id_type=MESH)
            cp.wait_send()
            cp.wait_recv()

    res = pl.pallas_call(
        body, name=name,
        in_specs=[HBM] * len(flat) + [SEM] * (2 * n) + [pl.BlockSpec(memory_space=pl.ANY)],
        out_specs=[HBM] * len(flat) + [pl.BlockSpec(memory_space=pltpu.VMEM)],
        out_shape=[pltpu.HBM(a.shape, a.dtype) for a in flat] + [jax.ShapeDtypeStruct((8, 128), F32)],
        input_output_aliases={k: k for k in range(len(flat))},
        compiler_params=pltpu.CompilerParams(has_side_effects=EFFECT),
    )(*flat, *sends, *recvs, after)
    out, pos = [], 0
    for group in thru:
        out.append(res[pos:pos + len(group)])
        pos += len(group)
    return out, res[len(flat)]


def _forward_gathered(bufs, after, *, name):
    nw = len(bufs)

    def body(*refs):
        outs = refs[nw + 1:2 * nw + 1]
        d_send, d_recv, i_send, i_recv = refs[2 * nw + 1:]
        x, y, c, chips = _place()
        me, sibling = (x, y, c), (x, y, 1 - c)
        slots = [2 * cx + cy for cx, cy in chips]

        def rows(w, j, start, n):
            return outs[w].at[slots[j], pl.ds(start, n)]

        def d2d(w, j, which, to):
            hr = bufs[w].shape[1] // 2
            r = rows(w, j, which * hr, hr)
            return pltpu.make_async_remote_copy(
                src_ref=r, dst_ref=r, send_sem=d_send.at[N_PEER_CHIPS * w + j],
                recv_sem=d_recv.at[N_PEER_CHIPS * w + j], device_id=to, device_id_type=MESH)

        def ici(w, j, slot_j, to):
            q = bufs[w].shape[1] // 4
            r = rows(w, slot_j, c * 2 * q + j * q, q)
            return pltpu.make_async_remote_copy(
                src_ref=r, dst_ref=r, send_sem=i_send.at[N_NEIGHBOUR_CHIPS * w + j],
                recv_sem=i_recv.at[N_NEIGHBOUR_CHIPS * w + j], device_id=to, device_id_type=MESH)

        started = []
        for w in range(nw):
            started += [ici(w, 0, 0, (*chips[1], c)), ici(w, 1, 1, (*chips[0], c))]
            started += [d2d(w, j, c, sibling) for j in range(N_NEIGHBOUR_CHIPS)]
        for cp in started:
            cp.start()
        diag = N_PEER_CHIPS - 1
        for w in range(nw):
            for j in range(N_NEIGHBOUR_CHIPS):
                ici(w, j, diag, me).wait_recv()
            cp = d2d(w, diag, c, sibling)
            cp.start()
            started.append(cp)
        for w in range(nw):
            for j in range(N_PEER_CHIPS):
                d2d(w, j, 1 - c, me).wait_recv()
        for cp in started:
            cp.wait_send()

    return pl.pallas_call(
        body, name=name,
        in_specs=[ANY] * (nw + 1), out_specs=[ANY] * nw,
        out_shape=[jax.ShapeDtypeStruct(a.shape, a.dtype) for a in bufs],
        input_output_aliases={w: w for w in range(nw)},
        scratch_shapes=[pltpu.SemaphoreType.DMA((N_PEER_CHIPS * nw,)), pltpu.SemaphoreType.DMA((N_PEER_CHIPS * nw,)),
                        pltpu.SemaphoreType.DMA((N_NEIGHBOUR_CHIPS * nw,)),
                        pltpu.SemaphoreType.DMA((N_NEIGHBOUR_CHIPS * nw,))],
    )(*bufs, after)


def _allreduce_small(p, after, *, name):
    R = p.shape[0]
    hr = R // 2

    def body(p_ref, _after_ref, out_ref, sib_ref, sum_ref, gat_ref, tot_ref, send, recv):
        x, y, c, chips = _place()
        s = 2 * x + y
        sibling = (x, y, 1 - c)
        rows = pl.ds(pl.multiple_of(c * hr, 8), hr)
        swap = pltpu.make_async_remote_copy(src_ref=p_ref, dst_ref=sib_ref, send_sem=send.at[0], recv_sem=recv.at[0],
                                            device_id=sibling, device_id_type=MESH)
        swap.start()
        swap.wait()
        sum_ref[...] = p_ref[...] + sib_ref[...]
        gat_ref[s] = sum_ref[rows, :]
        cps = [pltpu.make_async_remote_copy(src_ref=sum_ref.at[rows], dst_ref=gat_ref.at[s], send_sem=send.at[1 + j],
                                            recv_sem=recv.at[1 + j], device_id=(cx, cy, c), device_id_type=MESH)
               for j, (cx, cy) in enumerate(chips)]
        for cp in cps:
            cp.start()
        for cp in cps:
            cp.wait()
        tot_ref[...] = ((gat_ref[0] + gat_ref[1]) + gat_ref[2]) + gat_ref[3]
        out_ref[rows, :] = tot_ref[...]
        share = pltpu.make_async_remote_copy(src_ref=tot_ref, dst_ref=out_ref.at[rows], send_sem=send.at[4],
                                             recv_sem=recv.at[4], device_id=sibling, device_id_type=MESH)
        share.start()
        share.wait_send()
        other = out_ref.at[pl.ds(pl.multiple_of((1 - c) * hr, 8), hr)]
        pltpu.make_async_remote_copy(src_ref=other, dst_ref=other, send_sem=send.at[4], recv_sem=recv.at[4],
                                     device_id=(x, y, c), device_id_type=MESH).wait_recv()

    vmem = pl.BlockSpec(memory_space=pltpu.VMEM)
    return pl.pallas_call(
        body, name=name, in_specs=[vmem, ANY], out_specs=vmem,
        out_shape=jax.ShapeDtypeStruct((R, 128), F32),
        scratch_shapes=[pltpu.VMEM((R, 128), F32), pltpu.VMEM((R, 128), F32), pltpu.VMEM((NCHIP, hr, 128), F32),
                        pltpu.VMEM((hr, 128), F32), pltpu.SemaphoreType.DMA((5,)), pltpu.SemaphoreType.DMA((5,))],
    )(p, after)


def _select_half_bf16(g, half, add, slot, *, name):
    _, R, C = g.shape
    hr = R // 2
    tr = _pick_rows(hr, 16)
    nb = hr // tr
    sel = jnp.concatenate([jnp.reshape(half, (1,)).astype(jnp.int32), slot])

    def body(s_ref, g_ref, a_ref, o_ref, own_ref):
        val = (g_ref[...].astype(F32) + a_ref[...].astype(F32)).astype(BF16)
        o_ref[...] = val

        @pl.when(pl.program_id(1) == s_ref[1])
        def _():
            own_ref[...] = val

    g_spec = pl.BlockSpec((None, tr, C), lambda i, j, s: (j, s[0] * nb + i, 0))
    o_spec = pl.BlockSpec((None, tr, C), lambda i, j, s: (j, i, 0))
    own_spec = pl.BlockSpec((None, tr, C), lambda i, j, s: (s[1], i, 0))
    shape = jax.ShapeDtypeStruct((NCHIP, hr, C), BF16)
    return pl.pallas_call(
        body, name=name,
        grid_spec=pltpu.PrefetchScalarGridSpec(
            num_scalar_prefetch=1, grid=(nb, NCHIP), in_specs=[g_spec, o_spec], out_specs=[o_spec, own_spec]),
        out_shape=[shape, shape],
        compiler_params=_cp(("parallel", "arbitrary"), VMEM_MB),
    )(sel, g, add)


def _adamw_math(w, g, m, v):
    m = ADAM_B1 * m + (1.0 - ADAM_B1) * g
    v = ADAM_B2 * v + (1.0 - ADAM_B2) * (g * g)
    m_hat = m / (1.0 - ADAM_B1 ** ADAM_STEP)
    v_hat = v / (1.0 - ADAM_B2 ** ADAM_STEP)
    delta = -ADAM_LR * (m_hat / (jnp.sqrt(v_hat) + ADAM_EPS) + ADAM_WD * w)
    return delta, m, v


def _adamw(w, g_mine, g_sib, m, v, core, *, name):
    R, C = w.shape
    hr = R // 2
    tr = _pick_rows(hr, 16)
    nb = hr // tr
    row = pl.BlockSpec((tr, C), lambda hh, i, c: (hh * nb + i, 0))
    mine = pl.BlockSpec((NCHIP, tr, C), lambda hh, i, c: (0, jnp.where(hh == c[0], i, 0), 0))
    sibs = pl.BlockSpec((NCHIP, tr, C), lambda hh, i, c: (0, jnp.where(hh == c[0], 0, i), 0))

    def slot_sum(ref):
        acc = ref[0].astype(F32) + ref[1].astype(F32)
        for j in range(2, NCHIP):
            acc = acc + ref[j].astype(F32)
        return acc

    def body(c_ref, w_ref, gm_ref, gs_ref, m_ref, v_ref, go_ref, d_ref, mo_ref, vo_ref):
        gv = jnp.where(pl.program_id(0) == c_ref[0], slot_sum(gm_ref), slot_sum(gs_ref))
        d, mn, vn = _adamw_math(w_ref[...], gv, m_ref[...], v_ref[...])
        go_ref[...] = gv
        d_ref[...] = d
        mo_ref[...] = mn
        vo_ref[...] = vn

    return pl.pallas_call(
        body, name=name,
        grid_spec=pltpu.PrefetchScalarGridSpec(
            num_scalar_prefetch=1, grid=(2, nb),
            in_specs=[row, mine, sibs, row, row], out_specs=[row] * 4),
        out_shape=[jax.ShapeDtypeStruct((R, C), F32)] * 4,
        compiler_params=_cp(("parallel", "parallel"), VMEM_MB),
    )(core, w, g_mine, g_sib, m, v)


def _adamw_small(ws, gs, ms, vs, *, name):
    n = len(ws)

    def body(*refs):
        w_r, g_r, m_r, v_r = refs[:n], refs[n:2 * n], refs[2 * n:3 * n], refs[3 * n:4 * n]
        d_r, mo_r, vo_r = refs[4 * n:5 * n], refs[5 * n:6 * n], refs[6 * n:7 * n]
        for k in range(n):
            d, mn, vn = _adamw_math(w_r[k][...], g_r[k][...], m_r[k][...], v_r[k][...])
            d_r[k][...] = d
            mo_r[k][...] = mn
            vo_r[k][...] = vn

    shapes = [jax.ShapeDtypeStruct(w.shape, F32) for w in ws]
    res = pl.pallas_call(body, name=name, out_shape=shapes * 3)(*ws, *gs, *ms, *vs)
    return res[:n], res[n:2 * n], res[2 * n:]


_PACK_ROWS = 8


def _pack(parts):
    rows = []
    for a in parts:
        flat = a.reshape(-1)
        n = -(-flat.shape[0] // (_PACK_ROWS * 128)) * (_PACK_ROWS * 128)
        rows.append(jnp.pad(flat, (0, n - flat.shape[0])).reshape(-1, 128))
    total = sum(r.shape[0] for r in rows)
    if total % 16:
        rows.append(jnp.zeros((16 - total % 16, 128), F32))
    return jnp.concatenate(rows, axis=0)


def _unpack(p, shapes):
    out, r = [], 0
    for shp in shapes:
        n = math.prod(shp)
        nr = -(-n // (_PACK_ROWS * 128)) * _PACK_ROWS
        out.append(p[r:r + nr].reshape(-1)[:n].reshape(shp))
        r += nr
    return out


def kernel(x, mem, g_mix, w_in, ln_v_g, ln_v_b, w_s, b_s, conv_w, g_mem, w_kv, g_head, w_o, g_ffn, w_ffn1, w_ffn2, g_final, loss_target, m_g_mix, m_w_in, m_ln_v_g, m_ln_v_b, m_w_s, m_b_s, m_conv_w, m_g_mem, m_w_kv, m_g_head, m_w_o, m_g_ffn, m_w_ffn1, m_w_ffn2, m_g_final, v_g_mix, v_w_in, v_ln_v_g, v_ln_v_b, v_w_s, v_b_s, v_conv_w, v_g_mem, v_w_kv, v_g_head, v_w_o, v_g_ffn, v_w_ffn1, v_w_ffn2, v_g_final):
    sds = jax.ShapeDtypeStruct
    xi, yi = lax.axis_index("x"), lax.axis_index("y")
    shard = 2 * xi + yi
    x2d, mem2d, tgt = x[0], mem[0], loss_target[0]
    ws3, bs2 = w_s[0], b_s[0]
    g_final2 = g_final.reshape(1, D)
    dff4 = DFF // NCHIP
    din4 = DIN // NCHIP
    dcv4 = DC // NCHIP

    big = [w_in[0].T, w_kv[0], w_o[0], w_ffn1[0], w_ffn2[0]]
    big_names = ["w_in", "w_kv", "w_o", "w_ffn1", "w_ffn2"]
    slot = jnp.reshape(shard, (1,)).astype(jnp.int32)
    core = jnp.reshape(lax.axis_index("c"), (1,)).astype(jnp.int32)
    conv_pad = jnp.pad(conv_w[0], ((0, CONV_PAD[0] - 3), (0, CONV_PAD[1] - dcv4)))
    conv_slots = lax.dynamic_update_slice(jnp.zeros((NCHIP,) + CONV_PAD, F32), conv_pad[None], (shard, 0, 0))

    def gather_start(bufs, after, nm):
        return _allgather_start(bufs, after, name="ag_start_" + nm)

    def gather_wait(state, idx, after, nm):
        send, recv, bufs, _ = state
        got, token = _transfer_wait([send[k] for k in idx], [recv[k] for k in idx], [[bufs[k]] for k in idx],
                                    [(N_NEIGHBOUR_CHIPS, bufs[k].shape[1] // 2) for k in idx], after, name="ag_wait_" + nm)
        return [g[0] for g in got], token

    cast = lambda k, after: _cast_into_slot(big[k], slot, after, name="cast_" + big_names[k])
    ag_in = gather_start([cast(0, slot), conv_slots], slot, "in")
    bs_t = bs2.T

    h = _rms_fwd(x2d, g_mix, name="rms_mix", after=[ag_in[3]])
    mem_n = _rms_fwd(mem2d, g_mem, name="rms_mem", after=[h])
    rest_b = [cast(1, mem_n)]
    rest_b.append(cast(2, rest_b[0]))
    rest_b.append(cast(3, rest_b[1]))
    w2_b = cast(4, rest_b[2])
    got_in, tok = gather_wait(ag_in, [0, 1], w2_b, "in")
    ag_rest = gather_start(rest_b, tok, "kvo1")
    win4, conv4 = _forward_gathered(got_in, ag_rest[3], name="ag_fwd_in")
    w_in_t = win4.reshape(DIN, D)
    conv_full = conv4[:, :3, :dcv4].transpose(1, 0, 2).reshape(3, DC)
    (proj,) = _matmul(h, w_in_t, name="mm_proj", tb=True, M=S, N=DIN, K=D, tn=DIN // 2, outs=[sds((S, DIN), F32)])
    got_kvo, tok = gather_wait(ag_rest, [0, 1], proj, "kvo")
    wkv4, wo4 = _forward_gathered(got_kvo, tok, name="ag_fwd_kvo")
    w_kv_full = wkv4.reshape(D, 2 * DM)
    w_o_full = wo4.reshape(D, D)
    (kv,) = _matmul(mem_n, w_kv_full, name="mm_kv", M=NMEM, N=2 * DM, K=D, outs=[sds((NMEM, 2 * DM), F32)])
    heads, hn, ycv = _mix_fwd(proj, kv, ws3, bs_t, ln_v_g, ln_v_b, conv_full, g_head, name="mix_fwd")
    (x2,) = _matmul(hn, w_o_full, name="mm_wo", M=S, N=D, K=D, outs=[sds((S, D), F32)],
                    epi=lambda acc, res: (acc + res,), extras=[(x2d, _tile_spec())])
    h2 = _rms_fwd(x2, g_ffn, name="rms_ffn")
    got_w1, tok = gather_wait(ag_rest, [2], h2, "ffn1")
    ag_w2 = gather_start([w2_b], tok, "ffn2")
    (w14,) = _forward_gathered(got_w1, ag_w2[3], name="ag_fwd_ffn1")

    def w1_cols(tn, tk):
        nb = dff4 // tn
        return pl.BlockSpec((None, tk, tn), lambda j, i, k: (j // nb, k, j % nb))

    (act,) = _matmul(h2, w14, name="mm_ffn1", M=S, N=DFF, K=D, tn=dff4, b_spec=w1_cols, outs=[sds((S, DFF), BF16)],
                     epi=lambda acc: (jnp.square(jnp.maximum(acc, 0.0)),))
    got_w2, tok = gather_wait(ag_w2, [0], act, "ffn2")
    (w24,) = _forward_gathered(got_w2, tok, name="ag_fwd_ffn2")
    w2_full = w24.reshape(DFF, D)
    (x3,) = _matmul(act, w2_full, name="mm_ffn2", M=S, N=D, K=DFF, tm=2 * TM, outs=[sds((S, D), F32)],
                    epi=lambda acc, res: (acc + res,), extras=[(x2, _tile_spec())])

    ci = lax.axis_index("c")

    def rs_sibling(g4, nm):
        return _sibling_start([g4], False, name="rs_sib_" + nm)

    def rs_chips(state, after, nm):
        send, recv, g4, land, _ = state
        (((land_, g4_),), _) = _transfer_wait(send, recv, [[land[0], g4[0]]], [(NCHIP, land[0].shape[1])], after,
                                             name="rs_sibwait_" + nm)
        part, buf = _select_half_bf16(g4_, ci, land_, slot, name="rs_add_" + nm)
        return _scatter_start([part], [buf], name="rs_start_" + nm)

    def rs_end(state, after, nm):
        send, recv, parts, bufs, _ = state
        (((buf, _),), _) = _transfer_wait(send, recv, [[bufs[0], parts[0]]], [(N_PEER_CHIPS, bufs[0].shape[1])], after,
                                          name="rs_wait_" + nm)
        return _sibling_start([buf], True, name="rs_share_" + nm)

    big_m = [m_w_in[0].T, m_w_kv[0], m_w_o[0], m_w_ffn1[0], m_w_ffn2[0]]
    big_v = [v_w_in[0].T, v_w_kv[0], v_w_o[0], v_w_ffn1[0], v_w_ffn2[0]]
    big_out = {}

    def rs_finish(k, state, after):
        send, recv, mine, land, _ = state
        nm = big_names[k]
        (((land_, mine_),), _) = _transfer_wait(send, recv, [[land[0], mine[0]]], [(NCHIP, land[0].shape[1])], after,
                                               name="rs_sharewait_" + nm)
        big_out[nm] = _adamw(big[k], mine_, land_, big_m[k], big_v[k], core, name="adamw_" + nm)
        return big_out[nm][1]

    dx3, dx3b, dg_final, loss11 = _loss_bwd(x3, g_final2, tgt, name="loss_bwd")
    (dw2,) = _matmul(act, dx3b, name="mm_dw2", ta=True, M=DFF, N=D, K=S, tn=D, outs=[sds((DFF, D), BF16)])
    sib_w2 = rs_sibling(dw2.reshape(NCHIP, dff4, D), "w_ffn2")
    (dfb,) = _matmul(dx3b, w2_full, name="mm_dact", tb=True, M=S, N=DFF, K=D, tn=dff4, outs=[sds((S, DFF), BF16)],
                     epi=lambda acc, a: (acc * (2.0 * jnp.sqrt(a.astype(F32))),), extras=[(act, _tile_spec())],
                     after=[sib_w2[4]])
    rs_w2 = rs_chips(sib_w2, dfb, "w_ffn2")

    def dw1_out(tm, tn):
        nb = dff4 // tn
        return [pl.BlockSpec((None, tm, tn), lambda j, i, k: (j // nb, i, j % nb))]

    (dw1,) = _matmul(h2, dfb, name="mm_dw1", ta=True, M=D, N=DFF, K=S, tn=dff4, outs=[sds((NCHIP, D, dff4), BF16)],
                     out_specs=dw1_out, after=[rs_w2[4]])
    sib_w1 = rs_sibling(dw1, "w_ffn1")

    def w1_rows(tn, tk):
        kb = dff4 // tk
        return pl.BlockSpec((None, tn, tk), lambda j, i, k: (k // kb, j, k % kb))

    (dh2,) = _matmul(dfb, w14, name="mm_dh2", tb=True, M=S, N=D, K=DFF, tm=2 * TM, b_spec=w1_rows,
                     outs=[sds((S, D), F32)], after=[sib_w1[4]])
    rs_w1 = rs_chips(sib_w1, dh2, "w_ffn1")
    dx2, dx2b, dg_ffn = _rms_bwd(dh2, x2, g_ffn, dx3, name="rms_ffn_bwd", after=[rs_w1[4]])
    (dwo,) = _matmul(hn, dx2b, name="mm_dwo", ta=True, M=D, N=D, K=S, outs=[sds((D, D), BF16)])
    sib_wo = rs_sibling(dwo.reshape(NCHIP, D // NCHIP, D), "w_o")
    (dhn,) = _matmul(dx2b, w_o_full, name="mm_dhn", tb=True, M=S, N=D, K=D, outs=[sds((S, D), F32)],
                     after=[sib_wo[4]])
    rs_wo = rs_chips(sib_wo, dhn, "w_o")
    sh_w2 = rs_end(rs_w2, rs_wo[4], "w_ffn2")
    dproj, dkv, dws, dbs8, dlng, dlnb, dcw8, dgh = _mix_bwd(
        dhn, heads, proj, ycv, kv, ws3, bs_t, ln_v_g, ln_v_b, conv_full, g_head, sh_w2[4], name="mix_bwd")
    (dwin_t,) = _matmul(dproj, h, name="mm_dwin", ta=True, M=DIN, N=D, K=S, tm=DIN // 2, outs=[sds((DIN, D), BF16)])
    sib_win = rs_sibling(dwin_t.reshape(NCHIP, din4, D), "w_in")
    (dwkv,) = _matmul(mem_n, dkv, name="mm_dwkv", ta=True, M=D, N=2 * DM, K=NMEM, outs=[sds((D, 2 * DM), BF16)],
                      after=[sib_win[4]])
    sib_wkv = rs_sibling(dwkv.reshape(NCHIP, D // NCHIP, 2 * DM), "w_kv")
    (dh,) = _matmul(dproj, w_in_t, name="mm_dh", M=S, N=D, K=DIN, tk=DIN, outs=[sds((S, D), F32)],
                    after=[sib_wkv[4]])
    rs_win = rs_chips(sib_win, dh, "w_in")
    rs_wkv = rs_chips(sib_wkv, rs_win[4], "w_kv")
    sh_w1 = rs_end(rs_w1, rs_wkv[4], "w_ffn1")
    dx, dg_mix = _rms_bwd(dh, x2d, g_mix, dx2, name="rms_mix_bwd", want_bf=False, after=[sh_w1[4]])
    (dmem_n,) = _matmul(dkv, w_kv_full, name="mm_dmem", tb=True, M=NMEM, N=D, K=2 * DM, outs=[sds((NMEM, D), F32)],
                        after=[dx])
    (dg_mem,) = _rms_bwd(dmem_n, mem2d, g_mem, None, name="rms_mem_bwd", want_dx=False)
    sh_wo = rs_end(rs_wo, dg_mem, "w_o")
    done = rs_finish(4, sh_w2, sh_wo[4])
    done = rs_finish(3, sh_w1, done)
    sh_win = rs_end(rs_win, done, "w_in")
    sh_wkv = rs_end(rs_wkv, sh_win[4], "w_kv")
    done = rs_finish(2, sh_wo, sh_wkv[4])
    done = rs_finish(0, sh_win, done)
    done = rs_finish(1, sh_wkv, done)

    loss = lax.psum(loss11[0, 0], ("x", "y", "c"))

    small_names = ["g_mix", "ln_v_g", "ln_v_b", "w_s", "b_s", "conv_w", "g_mem", "g_head", "g_ffn", "g_final"]
    small_part = [dg_mix, dlng, dlnb, dws, dbs8[:, 0, :], dcw8[:3], dg_mem, dgh, dg_ffn, dg_final]
    small_shapes = [(1, D), (1, DS), (1, DS), (NSH, CHUNK, CHUNK), (NSH, CHUNK), (3, DC), (1, D), (1, D), (1, D), (1, D)]
    total = _allreduce_small(_pack(small_part), done, name="allreduce_small")
    small_g = _unpack(total, small_shapes)
    small_g[5] = lax.dynamic_slice(small_g[5], (0, shard * dcv4), (3, dcv4))
    small_w = [g_mix, ln_v_g, ln_v_b, ws3, bs2, conv_w[0], g_mem, g_head, g_ffn, g_final2]
    small_m = [m_g_mix, m_ln_v_g, m_ln_v_b, m_w_s[0], m_b_s[0], m_conv_w[0], m_g_mem, m_g_head, m_g_ffn,
               m_g_final.reshape(1, D)]
    small_v = [v_g_mix, v_ln_v_g, v_ln_v_b, v_w_s[0], v_b_s[0], v_conv_w[0], v_g_mem, v_g_head, v_g_ffn,
               v_g_final.reshape(1, D)]
    s_delta, s_m, s_v = _adamw_small(small_w, small_g, small_m, small_v, name="adamw_small")
    small_out = {nm: (g, d, mn, vn) for nm, g, d, mn, vn in zip(small_names, small_g, s_delta, s_m, s_v)}

    order = ["g_mix", "w_in", "ln_v_g", "ln_v_b", "w_s", "b_s", "conv_w", "g_mem", "w_kv", "g_head", "w_o",
             "g_ffn", "w_ffn1", "w_ffn2", "g_final"]
    like = dict(g_mix=g_mix, w_in=w_in, ln_v_g=ln_v_g, ln_v_b=ln_v_b, w_s=w_s, b_s=b_s, conv_w=conv_w, g_mem=g_mem,
                w_kv=w_kv, g_head=g_head, w_o=w_o, g_ffn=g_ffn, w_ffn1=w_ffn1, w_ffn2=w_ffn2, g_final=g_final)
    res = {**big_out, **small_out}
    res["w_in"] = [a.T for a in res["w_in"]]
    outs = [loss, dx[None]]
    for k in range(4):
        outs += [res[nm][k].reshape(like[nm].shape) for nm in order]
    return tuple(outs)
```

```python
import math

import jax
import jax.numpy as jnp
from jax import lax
from jax.experimental import pallas as pl
from jax.experimental.pallas import tpu as pltpu

F32 = jnp.float32
BF16 = jnp.bfloat16
MESH = pl.DeviceIdType.MESH

D = 2048
S = 2048
HD = 128
NH = D // HD
NMH = 4
NSH = (NH - NMH) // 2
NCH = NH - NMH - NSH
DS = NSH * HD
DC = NCH * HD
DM = NMH * HD
DIN = 2 * DS + 3 * DC + DM
CHUNK = 128
NMEM = 256
DFF = 4 * D
EPS = 1e-6
NCHIP = 4
SCALE = HD ** -0.5

ADAM_LR = 0.001
ADAM_B1 = 0.9
ADAM_B2 = 0.999
ADAM_EPS = 1e-08
ADAM_WD = 0.01
ADAM_STEP = 10

TR_EW = 256
TR_MIX = 256
TM = 512
TN = 1024
TK = 2048
VMEM_MB = 56
HALO = 8


def _pick(n, target, q=128):
    best = None
    for t in range(q, min(n, target) + 1, q):
        if n % t == 0:
            best = t
    return n if best is None else best


def _pick_rows(n, q):
    below = _pick(n, TR_EW, q)
    if 2 * below >= TR_EW:
        return below
    above = [t for t in range(TR_EW, min(n, 4 * TR_EW) + 1, q) if n % t == 0]
    return above[0] if above else below


def _cp(sem=None, vmem_mb=None, **kw):
    d = dict(kw)
    if sem is not None:
        d["dimension_semantics"] = sem
    if vmem_mb is not None:
        d["vmem_limit_bytes"] = vmem_mb << 20
    return pltpu.CompilerParams(**d)


def _gelu(x):
    z = 0.7978845608028654 * (x + 0.044715 * (x * x * x))
    return 0.5 * x * (1.0 + jnp.tanh(z))


def _gelu_grad(x):
    x2 = x * x
    t = jnp.tanh(0.7978845608028654 * (x + 0.044715 * (x2 * x)))
    return 0.5 * (1.0 + t) + 0.5 * x * (1.0 - t * t) * (0.7978845608028654 * (1.0 + 3.0 * 0.044715 * x2))


def _matmul(a, b, *, name, ta=False, tb=False, M, N, K, tm=None, tn=None, tk=None, outs, epi=None,
            extras=(), b_spec=None, out_specs=None, after=()):
    n_after = len(after)
    tm = _pick(M, TM if tm is None else tm, 8)
    tn = _pick(N, TN if tn is None else tn)
    tk = _pick(K, TK if tk is None else tk)
    nk = K // tk
    grid = (N // tn, M // tm, nk)
    a_spec = (pl.BlockSpec((tk, tm), lambda j, i, k: (k, i)) if ta
              else pl.BlockSpec((tm, tk), lambda j, i, k: (i, k)))
    if b_spec is None:
        b_spec = (pl.BlockSpec((tn, tk), lambda j, i, k: (j, k)) if tb
                  else pl.BlockSpec((tk, tn), lambda j, i, k: (k, j)))
    else:
        b_spec = b_spec(tn, tk)
    if out_specs is None:
        out_specs = [pl.BlockSpec((tm, tn), lambda j, i, k: (i, j)) for _ in outs]
    else:
        out_specs = out_specs(tm, tn)
    dn = (((0 if ta else 1,), (1 if tb else 0,)), ((), ()))
    n_ex, n_out = len(extras), len(outs)

    def body(*refs):
        a_ref, b_ref = refs[0], refs[1]
        ex = refs[2:2 + n_ex]
        first_out = 2 + n_ex + n_after
        o = refs[first_out:first_out + n_out]
        acc = refs[first_out + n_out:]
        part = lax.dot_general(a_ref[...].astype(BF16), b_ref[...].astype(BF16), dn,
                               preferred_element_type=F32)

        def finish(val):
            res = (val,) if epi is None else epi(val, *[e[...] for e in ex])
            for r, o_ref in zip(res, o):
                o_ref[...] = r.astype(o_ref.dtype)

        if nk == 1:
            finish(part)
        else:
            k = pl.program_id(2)

            @pl.when(k == 0)
            def _():
                acc[0][...] = part

            @pl.when(k > 0)
            def _():
                acc[0][...] += part

            @pl.when(k == nk - 1)
            def _():
                finish(acc[0][...])

    return pl.pallas_call(
        body, name=name, grid=grid,
        in_specs=[a_spec, b_spec] + [sp(tm, tn) for _, sp in extras] + [ANY] * n_after,
        out_specs=out_specs, out_shape=outs,
        scratch_shapes=([pltpu.VMEM((tm, tn), F32)] if nk > 1 else []),
        compiler_params=_cp(("parallel", "parallel", "arbitrary"), VMEM_MB),
    )(a, b, *[arr for arr, _ in extras], *after)


def _tile_spec():
    return lambda tm, tn: pl.BlockSpec((tm, tn), lambda j, i, k: (i, j))


def _cast_into_slot(w, slot, after, *, name):
    R, C = w.shape
    tr = _pick_rows(R, 16)

    def body(s_ref, w_ref, _after_ref, o_ref):
        o_ref[...] = w_ref[...].astype(BF16)

    return pl.pallas_call(
        body, name=name,
        grid_spec=pltpu.PrefetchScalarGridSpec(
            num_scalar_prefetch=1, grid=(R // tr,),
            in_specs=[pl.BlockSpec((tr, C), lambda i, s: (i, 0)), ANY],
            out_specs=pl.BlockSpec((None, tr, C), lambda i, s: (s[0], i, 0))),
        out_shape=jax.ShapeDtypeStruct((NCHIP, R, C), BF16),
        compiler_params=_cp(("parallel",), VMEM_MB),
    )(slot, w, after)


def _rms_fwd(x, g, *, name, after=()):
    R, C = x.shape
    tr = _pick(R, TR_EW, 16)
    n_after = len(after)

    def body(x_ref, g_ref, *rest):
        o_ref = rest[n_after]
        xv = x_ref[...]
        r = lax.rsqrt(jnp.mean(xv * xv, axis=-1, keepdims=True) + EPS)
        o_ref[...] = ((xv * r) * g_ref[...]).astype(BF16)

    return pl.pallas_call(
        body, name=name, grid=(R // tr,),
        in_specs=[pl.BlockSpec((tr, C), lambda i: (i, 0)), pl.BlockSpec((1, C), lambda i: (0, 0))] + [ANY] * n_after,
        out_specs=pl.BlockSpec((tr, C), lambda i: (i, 0)),
        out_shape=jax.ShapeDtypeStruct((R, C), BF16),
        compiler_params=_cp(("parallel",), VMEM_MB),
    )(x, g, *after)


def _rms_bwd(dh, x, g, dres, *, name, want_dx=True, want_bf=True, after=()):
    R, C = x.shape
    tr = _pick(R, TR_EW, 16)
    has_res = dres is not None
    row = pl.BlockSpec((tr, C), lambda i: (i, 0))
    vec = pl.BlockSpec((1, C), lambda i: (0, 0))

    def body(*refs):
        dh_ref, x_ref, g_ref = refs[:3]
        pos = 3
        dres_ref = None
        if has_res:
            dres_ref = refs[pos]
            pos += 1
        outs = refs[pos + len(after):]
        i = pl.program_id(0)
        xv = x_ref[...]
        r = lax.rsqrt(jnp.mean(xv * xv, axis=-1, keepdims=True) + EPS)
        xh = xv * r
        dhv = dh_ref[...]
        dg_ref = outs[-1]
        dgp = jnp.sum(dhv * xh, axis=0, keepdims=True)

        @pl.when(i == 0)
        def _():
            dg_ref[...] = dgp

        @pl.when(i > 0)
        def _():
            dg_ref[...] += dgp

        if want_dx:
            t = dhv * g_ref[...]
            dx = r * (t - xh * jnp.mean(t * xh, axis=-1, keepdims=True))
            if has_res:
                dx = dx + dres_ref[...]
            outs[0][...] = dx
            if want_bf:
                outs[1][...] = dx.astype(BF16)

    in_specs = [row, row, vec] + ([row] if has_res else []) + [ANY] * len(after)
    out_specs, out_shape = [], []
    if want_dx:
        out_specs.append(row)
        out_shape.append(jax.ShapeDtypeStruct((R, C), F32))
        if want_bf:
            out_specs.append(row)
            out_shape.append(jax.ShapeDtypeStruct((R, C), BF16))
    out_specs.append(vec)
    out_shape.append(jax.ShapeDtypeStruct((1, C), F32))
    args = [dh, x, g] + ([dres] if has_res else []) + list(after)
    return pl.pallas_call(
        body, name=name, grid=(R // tr,), in_specs=in_specs, out_specs=out_specs, out_shape=out_shape,
        compiler_params=_cp(("arbitrary",), VMEM_MB),
    )(*args)


def _loss_bwd(x3, g, tgt, *, name):
    R, C = x3.shape
    tr = _pick(R, TR_EW, 16)
    n = R // tr
    row = pl.BlockSpec((tr, C), lambda i: (i, 0))
    vec = pl.BlockSpec((1, C), lambda i: (0, 0))

    def body(x_ref, g_ref, t_ref, dx_ref, dxb_ref, dg_ref, loss_ref, acc_ref):
        i = pl.program_id(0)
        xv = x_ref[...]
        gv = g_ref[...]
        r = lax.rsqrt(jnp.mean(xv * xv, axis=-1, keepdims=True) + EPS)
        xh = xv * r
        e = xh * gv - t_ref[...]
        dy = e * (1.0 / C)
        sq = jnp.sum(e * e, axis=0, keepdims=True)
        dgp = jnp.sum(dy * xh, axis=0, keepdims=True)

        @pl.when(i == 0)
        def _():
            acc_ref[...] = sq
            dg_ref[...] = dgp

        @pl.when(i > 0)
        def _():
            acc_ref[...] += sq
            dg_ref[...] += dgp

        t = dy * gv
        dx = r * (t - xh * jnp.mean(t * xh, axis=-1, keepdims=True))
        dx_ref[...] = dx
        dxb_ref[...] = dx.astype(BF16)

        @pl.when(i == n - 1)
        def _():
            loss_ref[...] = jnp.sum(acc_ref[...], axis=-1, keepdims=True) * (0.5 / C)

    return pl.pallas_call(
        body, name=name, grid=(n,),
        in_specs=[row, vec, row],
        out_specs=[row, row, vec, pl.BlockSpec((1, 1), lambda i: (0, 0))],
        out_shape=[jax.ShapeDtypeStruct((R, C), F32), jax.ShapeDtypeStruct((R, C), BF16),
                   jax.ShapeDtypeStruct((1, C), F32), jax.ShapeDtypeStruct((1, 1), F32)],
        scratch_shapes=[pltpu.VMEM((1, C), F32)],
        compiler_params=_cp(("arbitrary",), VMEM_MB),
    )(x3, g, tgt)


def _offsets():
    u0 = 0
    v0 = DS
    b0 = 2 * DS
    c0 = b0 + DC
    x0 = c0 + DC
    q0 = x0 + DC
    return u0, v0, b0, c0, x0, q0


def _tri_mask(lower):
    r = lax.broadcasted_iota(jnp.int32, (CHUNK, CHUNK), 0)
    c = lax.broadcasted_iota(jnp.int32, (CHUNK, CHUNK), 1)
    return (r >= c) if lower else (c >= r)


def _layer_norm_stats(vg):
    mu = jnp.mean(vg, axis=-1, keepdims=True)
    vc = vg - mu
    rstd = lax.rsqrt(jnp.mean(vc * vc, axis=-1, keepdims=True) + EPS)
    return vc * rstd, rstd


def _softmax_rows(qh, kh):
    s = lax.dot_general(qh, kh, (((1,), (1,)), ((), ())), preferred_element_type=F32)
    m = jnp.max(s, axis=-1, keepdims=True)
    e = jnp.exp(s - m)
    return e / jnp.sum(e, axis=-1, keepdims=True)


def _mix_fwd(proj, kv, w_s, bs_t, ln_g, ln_b, conv_w, g_head, *, name):
    assert DS == DC
    tr = _pick(S, TR_MIX, CHUNK)
    n = S // tr
    nck = tr // CHUNK
    u0, v0, b0, c0, x0, q0 = _offsets()
    hb = tr // HALO

    def body(p_ref, cprev_ref, xprev_ref, kv_ref, ws_ref, bst_ref, lng_ref, lnb_ref, cw_ref, gh_ref,
             heads_ref, hn_ref, ycv_ref, buf_ref):
        i = pl.program_id(0)

        def emit(col, val):
            rs = lax.rsqrt(jnp.mean(val * val, axis=-1, keepdims=True) + EPS)
            heads_ref[:, col:col + HD] = val
            hn_ref[:, col:col + HD] = ((val * rs) * gh_ref[:, col:col + HD]).astype(BF16)

        vhat, _ = _layer_norm_stats(_gelu(p_ref[:, v0:v0 + DS]))
        vnb = (vhat * lng_ref[...] + lnb_ref[...]).astype(BF16)
        low = _tri_mask(True)
        for h in range(NSH):
            wt = jnp.where(low, ws_ref[h], 0.0).astype(BF16)
            bcol = bst_ref[:, h:h + 1]
            parts = []
            for c in range(nck):
                blk = vnb[c * CHUNK:(c + 1) * CHUNK, h * HD:(h + 1) * HD]
                parts.append(jnp.dot(wt, blk, preferred_element_type=F32) + bcol)
            mixed = parts[0] if nck == 1 else jnp.concatenate(parts, axis=0)
            emit(h * HD, _gelu(p_ref[:, u0 + h * HD:u0 + (h + 1) * HD]) * mixed)

        xc = p_ref[:, c0:c0 + DC] * p_ref[:, x0:x0 + DC]
        prev = cprev_ref[...] * xprev_ref[...]
        buf_ref[0:HALO, :] = jnp.where(i > 0, prev, 0.0)
        buf_ref[HALO:HALO + tr, :] = xc
        y = (cw_ref[2:3, :] * xc + cw_ref[1:2, :] * buf_ref[HALO - 1:HALO - 1 + tr, :]
             + cw_ref[0:1, :] * buf_ref[HALO - 2:HALO - 2 + tr, :])
        ycv_ref[...] = y
        cout = p_ref[:, b0:b0 + DC] * y
        for h in range(NCH):
            emit(DS + h * HD, cout[:, h * HD:(h + 1) * HD])

        for h in range(NMH):
            qh = (p_ref[:, q0 + h * HD:q0 + (h + 1) * HD] * SCALE).astype(BF16)
            kh = kv_ref[:, h * HD:(h + 1) * HD].astype(BF16)
            vh = kv_ref[:, DM + h * HD:DM + (h + 1) * HD].astype(BF16)
            p = _softmax_rows(qh, kh)
            emit(DS + DC + h * HD, jnp.dot(p.astype(BF16), vh, preferred_element_type=F32))

    full = lambda shape: pl.BlockSpec(shape, lambda i: (0,) * len(shape))
    halo_c = pl.BlockSpec((HALO, DC), lambda i: (jnp.maximum(i * hb - 1, 0), c0 // DC))
    halo_x = pl.BlockSpec((HALO, DC), lambda i: (jnp.maximum(i * hb - 1, 0), x0 // DC))
    return pl.pallas_call(
        body, name=name, grid=(n,),
        in_specs=[pl.BlockSpec((tr, DIN), lambda i: (i, 0)), halo_c, halo_x,
                  full((NMEM, 2 * DM)), full((NSH, CHUNK, CHUNK)), full((CHUNK, NSH)),
                  full((1, DS)), full((1, DS)), full((3, DC)), full((1, D))],
        out_specs=[pl.BlockSpec((tr, D), lambda i: (i, 0)), pl.BlockSpec((tr, D), lambda i: (i, 0)),
                   pl.BlockSpec((tr, DC), lambda i: (i, 0))],
        out_shape=[jax.ShapeDtypeStruct((S, D), F32), jax.ShapeDtypeStruct((S, D), BF16),
                   jax.ShapeDtypeStruct((S, DC), F32)],
        scratch_shapes=[pltpu.VMEM((tr + HALO, DC), F32)],
        compiler_params=_cp(("parallel",), VMEM_MB),
    )(proj, proj, proj, kv, w_s, bs_t, ln_g, ln_b, conv_w, g_head)


def _mix_bwd(dhn, heads, proj, ycv, kv, w_s, bs_t, ln_g, ln_b, conv_w, g_head, after, *, name):
    assert DS == DC
    tr = _pick(S, TR_MIX, CHUNK)
    n = S // tr
    nck = tr // CHUNK
    u0, v0, b0, c0, x0, q0 = _offsets()
    hb = tr // HALO
    last_hb = S // HALO - 1

    def body(dhn_ref, heads_ref, p_ref, ycv_ref, dhn_nx_ref, heads_nx_ref, b_nx_ref, kv_ref, ws_ref, bst_ref,
             lng_ref, lnb_ref, cw_ref, gh_ref, _after_ref,
             dp_ref, dkv_ref, dws_ref, dbs_ref, dlng_ref, dlnb_ref, dcw_ref, dgh_ref, buf_ref, dvn_ref):
        i = pl.program_id(0)

        @pl.when(i == 0)
        def _():
            dkv_ref[...] = jnp.zeros_like(dkv_ref)
            dws_ref[...] = jnp.zeros_like(dws_ref)
            dbs_ref[...] = jnp.zeros_like(dbs_ref)
            dlng_ref[...] = jnp.zeros_like(dlng_ref)
            dlnb_ref[...] = jnp.zeros_like(dlnb_ref)
            dcw_ref[...] = jnp.zeros_like(dcw_ref)
            dgh_ref[...] = jnp.zeros_like(dgh_ref)

        def head_bwd(a, dn, gh):
            rs = lax.rsqrt(jnp.mean(a * a, axis=-1, keepdims=True) + EPS)
            ah = a * rs
            t = dn * gh
            return rs * (t - ah * jnp.mean(t * ah, axis=-1, keepdims=True)), jnp.sum(dn * ah, axis=0, keepdims=True)

        def head_grad(col):
            da, dg = head_bwd(heads_ref[:, col:col + HD], dhn_ref[:, col:col + HD], gh_ref[:, col:col + HD])
            dgh_ref[:, col:col + HD] += dg
            return da

        v = p_ref[:, v0:v0 + DS]
        vhat, rstd = _layer_norm_stats(_gelu(v))
        vnb = (vhat * lng_ref[...] + lnb_ref[...]).astype(BF16)
        low = _tri_mask(True)
        ones = jnp.ones((HALO, HD), BF16)
        for h in range(NSH):
            w_h = ws_ref[h]
            wt = jnp.where(low, w_h, 0.0).astype(BF16)
            bcol = bst_ref[:, h:h + 1]
            da = head_grad(h * HD)
            u = p_ref[:, u0 + h * HD:u0 + (h + 1) * HD]
            ug = _gelu(u)
            dws = jnp.zeros((CHUNK, CHUNK), F32)
            dbs = jnp.zeros((HALO, CHUNK), F32)
            mixed_parts = []
            for c in range(nck):
                rows = slice(c * CHUNK, (c + 1) * CHUNK)
                blk = vnb[rows, h * HD:(h + 1) * HD]
                mixed_parts.append(jnp.dot(wt, blk, preferred_element_type=F32) + bcol)
                dmb = (da[rows] * ug[rows]).astype(BF16)
                dws = dws + lax.dot_general(dmb, blk, (((1,), (1,)), ((), ())), preferred_element_type=F32)
                dbs = dbs + lax.dot_general(ones, dmb, (((1,), (1,)), ((), ())), preferred_element_type=F32)
                dvn_ref[c * CHUNK:(c + 1) * CHUNK, h * HD:(h + 1) * HD] = lax.dot_general(
                    wt, dmb, (((0,), (0,)), ((), ())), preferred_element_type=F32)
            mixed = mixed_parts[0] if nck == 1 else jnp.concatenate(mixed_parts, axis=0)
            dp_ref[:, u0 + h * HD:u0 + (h + 1) * HD] = ((da * mixed) * _gelu_grad(u)).astype(BF16)
            dws_ref[h] += jnp.where(low, dws, 0.0)
            dbs_ref[h] += dbs
        dvn = dvn_ref[...]
        dlng_ref[...] += jnp.sum(dvn * vhat, axis=0, keepdims=True)
        dlnb_ref[...] += jnp.sum(dvn, axis=0, keepdims=True)
        dvh = dvn * lng_ref[...]
        dvg = rstd * (dvh - jnp.mean(dvh, axis=-1, keepdims=True)
                      - vhat * jnp.mean(dvh * vhat, axis=-1, keepdims=True))
        dp_ref[:, v0:v0 + DS] = (dvg * _gelu_grad(v)).astype(BF16)

        dc = jnp.concatenate([head_grad(DS + h * HD) for h in range(NCH)], axis=1)
        dc_nx = jnp.concatenate(
            [head_bwd(heads_nx_ref[:, h * HD:(h + 1) * HD], dhn_nx_ref[:, h * HD:(h + 1) * HD],
                      gh_ref[:, DS + h * HD:DS + (h + 1) * HD])[0] for h in range(NCH)], axis=1)
        bg = p_ref[:, b0:b0 + DC]
        cg = p_ref[:, c0:c0 + DC]
        xin = p_ref[:, x0:x0 + DC]
        dp_ref[:, b0:b0 + DC] = (dc * ycv_ref[...]).astype(BF16)
        dyv = dc * bg
        buf_ref[0:tr, :] = dyv
        buf_ref[tr:tr + HALO, :] = jnp.where(i < n - 1, dc_nx * b_nx_ref[...], 0.0)
        sh1 = buf_ref[1:1 + tr, :]
        sh0 = buf_ref[2:2 + tr, :]
        dxc = cw_ref[2:3, :] * dyv + cw_ref[1:2, :] * sh1 + cw_ref[0:1, :] * sh0
        xc = cg * xin
        dp_ref[:, c0:c0 + DC] = (dxc * xin).astype(BF16)
        dp_ref[:, x0:x0 + DC] = (dxc * cg).astype(BF16)
        dcw_ref[0:1, :] += jnp.sum(sh0 * xc, axis=0, keepdims=True)
        dcw_ref[1:2, :] += jnp.sum(sh1 * xc, axis=0, keepdims=True)
        dcw_ref[2:3, :] += jnp.sum(dyv * xc, axis=0, keepdims=True)

        for h in range(NMH):
            do = head_grad(DS + DC + h * HD).astype(BF16)
            qh = (p_ref[:, q0 + h * HD:q0 + (h + 1) * HD] * SCALE).astype(BF16)
            kh = kv_ref[:, h * HD:(h + 1) * HD].astype(BF16)
            vh = kv_ref[:, DM + h * HD:DM + (h + 1) * HD].astype(BF16)
            p = _softmax_rows(qh, kh)
            dpr = lax.dot_general(do, vh, (((1,), (1,)), ((), ())), preferred_element_type=F32)
            ds = (p * (dpr - jnp.sum(dpr * p, axis=-1, keepdims=True))).astype(BF16)
            dp_ref[:, q0 + h * HD:q0 + (h + 1) * HD] = (
                jnp.dot(ds, kh, preferred_element_type=F32) * SCALE).astype(BF16)
            dkv_ref[:, h * HD:(h + 1) * HD] += lax.dot_general(
                ds, qh, (((0,), (0,)), ((), ())), preferred_element_type=F32)
            dkv_ref[:, DM + h * HD:DM + (h + 1) * HD] += lax.dot_general(
                p.astype(BF16), do, (((0,), (0,)), ((), ())), preferred_element_type=F32)

    full = lambda shape: pl.BlockSpec(shape, lambda i: (0,) * len(shape))
    row = lambda c: pl.BlockSpec((tr, c), lambda i: (i, 0))
    nxt = lambda col: pl.BlockSpec((HALO, DC), lambda i: (jnp.minimum((i + 1) * hb, last_hb), col))
    return pl.pallas_call(
        body, name=name, grid=(n,),
        in_specs=[row(D), row(D), row(DIN), row(DC), nxt(DS // DC), nxt(DS // DC), nxt(b0 // DC),
                  full((NMEM, 2 * DM)), full((NSH, CHUNK, CHUNK)), full((CHUNK, NSH)),
                  full((1, DS)), full((1, DS)), full((3, DC)), full((1, D)), ANY],
        out_specs=[row(DIN), full((NMEM, 2 * DM)), full((NSH, CHUNK, CHUNK)), full((NSH, HALO, CHUNK)),
                   full((1, DS)), full((1, DS)), full((HALO, DC)), full((1, D))],
        out_shape=[jax.ShapeDtypeStruct((S, DIN), BF16), jax.ShapeDtypeStruct((NMEM, 2 * DM), F32),
                   jax.ShapeDtypeStruct((NSH, CHUNK, CHUNK), F32), jax.ShapeDtypeStruct((NSH, HALO, CHUNK), F32),
                   jax.ShapeDtypeStruct((1, DS), F32), jax.ShapeDtypeStruct((1, DS), F32),
                   jax.ShapeDtypeStruct((HALO, DC), F32), jax.ShapeDtypeStruct((1, D), F32)],
        scratch_shapes=[pltpu.VMEM((tr + HALO, DC), F32), pltpu.VMEM((tr, DS), F32)],
        compiler_params=_cp(("arbitrary",), VMEM_MB),
    )(dhn, heads, proj, ycv, dhn, heads, proj, kv, w_s, bs_t, ln_g, ln_b, conv_w, g_head, after)


def _place():
    x, y, c = lax.axis_index("x"), lax.axis_index("y"), lax.axis_index("c")
    chips = [(1 - x, y), (x, 1 - y), (1 - x, 1 - y)]
    return x, y, c, chips


ANY = pl.BlockSpec(memory_space=pl.ANY)


HBM = pl.BlockSpec(memory_space=pltpu.HBM)
SEM = pl.BlockSpec(memory_space=pltpu.SEMAPHORE)
EFFECT = pltpu.SideEffectType.DATAFLOW_SIDE_EFFECTING
N_PEER_CHIPS = 3
N_NEIGHBOUR_CHIPS = 2
CONV_PAD = (32, 256)


def _in_hbm(a):
    return pltpu.with_memory_space_constraint(a, pltpu.HBM)


def _allgather_start(bufs, after, *, name):
    nw = len(bufs)

    def body(*refs):
        ins, send, recv = refs[:nw], refs[nw + 1:2 * nw + 1], refs[2 * nw + 1:3 * nw + 1]
        token = refs[4 * nw + 1]
        x, y, c, chips = _place()
        s = 2 * x + y
        for w in range(nw):
            hr = bufs[w].shape[1] // 2
            rows = ins[w].at[s, pl.ds(c * hr, hr)]
            for cx, cy in chips[:N_NEIGHBOUR_CHIPS]:
                pltpu.make_async_remote_copy(src_ref=rows, dst_ref=rows, send_sem=send[w], recv_sem=recv[w],
                                             device_id=(cx, cy, c), device_id_type=MESH).start()
        token[...] = jnp.zeros_like(token)

    res = pl.pallas_call(
        body, name=name,
        in_specs=[HBM] * nw + [ANY],
        out_specs=[SEM] * (2 * nw) + [HBM] * nw + [pl.BlockSpec(memory_space=pltpu.VMEM)],
        out_shape=[pltpu.SemaphoreType.DMA(())] * (2 * nw) + [pltpu.HBM(a.shape, a.dtype) for a in bufs]
        + [jax.ShapeDtypeStruct((8, 128), F32)],
        input_output_aliases={w: 2 * nw + w for w in range(nw)},
        compiler_params=pltpu.CompilerParams(has_side_effects=EFFECT),
    )(*[_in_hbm(a) for a in bufs], after)
    return res[:nw], res[nw:2 * nw], res[2 * nw:3 * nw], res[3 * nw]


def _scatter_start(parts, bufs, *, name):
    nw = len(parts)

    def body(*refs):
        src, dst = refs[:nw], refs[nw:2 * nw]
        send, recv = refs[2 * nw:3 * nw], refs[3 * nw:4 * nw]
        token = refs[6 * nw]
        x, y, c, chips = _place()
        s = 2 * x + y
        for w in range(nw):
            for cx, cy in chips:
                pltpu.make_async_remote_copy(src_ref=src[w].at[2 * cx + cy], dst_ref=dst[w].at[s], send_sem=send[w],
                                             recv_sem=recv[w], device_id=(cx, cy, c), device_id_type=MESH).start()
        token[...] = jnp.zeros_like(token)

    res = pl.pallas_call(
        body, name=name,
        in_specs=[HBM] * (2 * nw),
        out_specs=[SEM] * (2 * nw) + [HBM] * (2 * nw) + [pl.BlockSpec(memory_space=pltpu.VMEM)],
        out_shape=[pltpu.SemaphoreType.DMA(())] * (2 * nw) + [pltpu.HBM(a.shape, a.dtype) for a in parts + bufs]
        + [jax.ShapeDtypeStruct((8, 128), F32)],
        input_output_aliases={k: 2 * nw + k for k in range(2 * nw)},
        compiler_params=pltpu.CompilerParams(has_side_effects=EFFECT),
    )(*[_in_hbm(a) for a in parts + bufs])
    return res[:nw], res[nw:2 * nw], res[2 * nw:3 * nw], res[3 * nw:4 * nw], res[4 * nw]


def _sibling_start(srcs, whole, *, name):
    nw = len(srcs)
    lands = [lax.empty((a.shape[0], a.shape[1] if whole else a.shape[1] // 2, a.shape[2]), a.dtype) for a in srcs]

    def body(*refs):
        src, land = refs[:nw], refs[nw:2 * nw]
        send, recv = refs[2 * nw:3 * nw], refs[3 * nw:4 * nw]
        token = refs[6 * nw]
        x, y, c, _ = _place()
        for w in range(nw):
            hr = srcs[w].shape[1] // 2
            rows = src[w] if whole else src[w].at[:, pl.ds((1 - c) * hr, hr)]
            pltpu.make_async_remote_copy(src_ref=rows, dst_ref=land[w], send_sem=send[w], recv_sem=recv[w],
                                         device_id=(x, y, 1 - c), device_id_type=MESH).start()
        token[...] = jnp.zeros_like(token)

    res = pl.pallas_call(
        body, name=name,
        in_specs=[HBM] * (2 * nw),
        out_specs=[SEM] * (2 * nw) + [HBM] * (2 * nw) + [pl.BlockSpec(memory_space=pltpu.VMEM)],
        out_shape=[pltpu.SemaphoreType.DMA(())] * (2 * nw) + [pltpu.HBM(a.shape, a.dtype) for a in srcs + lands]
        + [jax.ShapeDtypeStruct((8, 128), F32)],
        input_output_aliases={k: 2 * nw + k for k in range(2 * nw)},
        compiler_params=pltpu.CompilerParams(has_side_effects=EFFECT),
    )(*[_in_hbm(a) for a in srcs + lands])
    return res[:nw], res[nw:2 * nw], res[2 * nw:3 * nw], res[3 * nw:4 * nw], res[4 * nw]


def _transfer_wait(sends, recvs, thru, sizes, after, *, name):
    n = len(sends)
    flat = [a for group in thru for a in group]

    def body(*refs):
        bufs = refs[:len(flat)]
        send = refs[len(flat):len(flat) + n]
        recv = refs[len(flat) + n:len(flat) + 2 * n]
        token = refs[2 * len(flat) + 2 * n + 1]
        token[...] = jnp.zeros_like(token)
        x, y, c, _ = _place()
        pos = 0
        for k in range(n):
            slots, rows = sizes[k]
            region = bufs[pos].at[pl.ds(0, slots), pl.ds(0, rows)]
            pos += len(thru[k])
            cp = pltpu.make_async_remote_copy(src_ref=region, dst_ref=region, send_sem=send[k], recv_sem=recv[k],
                                              device_id=(x, y, 1 - c), device_id_type=MESH)
            cp.wait_send()
            cp.wait_recv()

    res = pl.pallas_call(
        body, name=name,
        in_specs=[HBM] * len(flat) + [SEM] * (2 * n) + [pl.BlockSpec(memory_space=pl.ANY)],
        out_specs=[HBM] * len(flat) + [pl.BlockSpec(memory_space=pltpu.VMEM)],
        out_shape=[pltpu.HBM(a.shape, a.dtype) for a in flat] + [jax.ShapeDtypeStruct((8, 128), F32)],
        input_output_aliases={k: k for k in range(len(flat))},
        compiler_params=pltpu.CompilerParams(has_side_effects=EFFECT),
    )(*flat, *sends, *recvs, after)
    out, pos = [], 0
    for group in thru:
        out.append(res[pos:pos + len(group)])
        pos += len(group)
    return out, res[len(flat)]


def _forward_gathered(bufs, after, *, name):
    nw = len(bufs)

    def body(*refs):
        outs = refs[nw + 1:2 * nw + 1]
        d_send, d_recv, i_send, i_recv = refs[2 * nw + 1:]
        x, y, c, chips = _place()
        me, sibling = (x, y, c), (x, y, 1 - c)
        slots = [2 * cx + cy for cx, cy in chips]

        def rows(w, j, start, n):
            return outs[w].at[slots[j], pl.ds(start, n)]

        def d2d(w, j, which, to):
            hr = bufs[w].shape[1] // 2
            r = rows(w, j, which * hr, hr)
            return pltpu.make_async_remote_copy(
                src_ref=r, dst_ref=r, send_sem=d_send.at[N_PEER_CHIPS * w + j],
                recv_sem=d_recv.at[N_PEER_CHIPS * w + j], device_id=to, device_id_type=MESH)

        def ici(w, j, slot_j, to):
            q = bufs[w].shape[1] // 4
            r = rows(w, slot_j, c * 2 * q + j * q, q)
            return pltpu.make_async_remote_copy(
                src_ref=r, dst_ref=r, send_sem=i_send.at[N_NEIGHBOUR_CHIPS * w + j],
                recv_sem=i_recv.at[N_NEIGHBOUR_CHIPS * w + j], device_id=to, device_id_type=MESH)

        started = []
        for w in range(nw):
            started += [ici(w, 0, 0, (*chips[1], c)), ici(w, 1, 1, (*chips[0], c))]
            started += [d2d(w, j, c, sibling) for j in range(N_NEIGHBOUR_CHIPS)]
        for cp in started:
            cp.start()
        diag = N_PEER_CHIPS - 1
        for w in range(nw):
            for j in range(N_NEIGHBOUR_CHIPS):
                ici(w, j, diag, me).wait_recv()
            cp = d2d(w, diag, c, sibling)
            cp.start()
            started.append(cp)
        for w in range(nw):
            for j in range(N_PEER_CHIPS):
                d2d(w, j, 1 - c, me).wait_recv()
        for cp in started:
            cp.wait_send()

    return pl.pallas_call(
        body, name=name,
        in_specs=[ANY] * (nw + 1), out_specs=[ANY] * nw,
        out_shape=[jax.ShapeDtypeStruct(a.shape, a.dtype) for a in bufs],
        input_output_aliases={w: w for w in range(nw)},
        scratch_shapes=[pltpu.SemaphoreType.DMA((N_PEER_CHIPS * nw,)), pltpu.SemaphoreType.DMA((N_PEER_CHIPS * nw,)),
                        pltpu.SemaphoreType.DMA((N_NEIGHBOUR_CHIPS * nw,)),
                        pltpu.SemaphoreType.DMA((N_NEIGHBOUR_CHIPS * nw,))],
    )(*bufs, after)


def _allreduce_small(p, after, *, name):
    R = p.shape[0]
    hr = R // 2

    def body(p_ref, _after_ref, out_ref, sib_ref, sum_ref, gat_ref, tot_ref, send, recv):
        x, y, c, chips = _place()
        s = 2 * x + y
        sibling = (x, y, 1 - c)
        rows = pl.ds(pl.multiple_of(c * hr, 8), hr)
        swap = pltpu.make_async_remote_copy(src_ref=p_ref, dst_ref=sib_ref, send_sem=send.at[0], recv_sem=recv.at[0],
                                            device_id=sibling, device_id_type=MESH)
        swap.start()
        swap.wait()
        sum_ref[...] = p_ref[...] + sib_ref[...]
        gat_ref[s] = sum_ref[rows, :]
        cps = [pltpu.make_async_remote_copy(src_ref=sum_ref.at[rows], dst_ref=gat_ref.at[s], send_sem=send.at[1 + j],
                                            recv_sem=recv.at[1 + j], device_id=(cx, cy, c), device_id_type=MESH)
               for j, (cx, cy) in enumerate(chips)]
        for cp in cps:
            cp.start()
        for cp in cps:
            cp.wait()
        tot_ref[...] = ((gat_ref[0] + gat_ref[1]) + gat_ref[2]) + gat_ref[3]
        out_ref[rows, :] = tot_ref[...]
        share = pltpu.make_async_remote_copy(src_ref=tot_ref, dst_ref=out_ref.at[rows], send_sem=send.at[4],
                                             recv_sem=recv.at[4], device_id=sibling, device_id_type=MESH)
        share.start()
        share.wait_send()
        other = out_ref.at[pl.ds(pl.multiple_of((1 - c) * hr, 8), hr)]
        pltpu.make_async_remote_copy(src_ref=other, dst_ref=other, send_sem=send.at[4], recv_sem=recv.at[4],
                                     device_id=(x, y, c), device_id_type=MESH).wait_recv()

    vmem = pl.BlockSpec(memory_space=pltpu.VMEM)
    return pl.pallas_call(
        body, name=name, in_specs=[vmem, ANY], out_specs=vmem,
        out_shape=jax.ShapeDtypeStruct((R, 128), F32),
        scratch_shapes=[pltpu.VMEM((R, 128), F32), pltpu.VMEM((R, 128), F32), pltpu.VMEM((NCHIP, hr, 128), F32),
                        pltpu.VMEM((hr, 128), F32), pltpu.SemaphoreType.DMA((5,)), pltpu.SemaphoreType.DMA((5,))],
    )(p, after)


def _select_half_bf16(g, half, add, slot, *, name):
    _, R, C = g.shape
    hr = R // 2
    tr = _pick_rows(hr, 16)
    nb = hr // tr
    sel = jnp.concatenate([jnp.reshape(half, (1,)).astype(jnp.int32), slot])

    def body(s_ref, g_ref, a_ref, o_ref, own_ref):
        val = (g_ref[...].astype(F32) + a_ref[...].astype(F32)).astype(BF16)
        o_ref[...] = val

        @pl.when(pl.program_id(1) == s_ref[1])
        def _():
            own_ref[...] = val

    g_spec = pl.BlockSpec((None, tr, C), lambda i, j, s: (j, s[0] * nb + i, 0))
    o_spec = pl.BlockSpec((None, tr, C), lambda i, j, s: (j, i, 0))
    own_spec = pl.BlockSpec((None, tr, C), lambda i, j, s: (s[1], i, 0))
    shape = jax.ShapeDtypeStruct((NCHIP, hr, C), BF16)
    return pl.pallas_call(
        body, name=name,
        grid_spec=pltpu.PrefetchScalarGridSpec(
            num_scalar_prefetch=1, grid=(nb, NCHIP), in_specs=[g_spec, o_spec], out_specs=[o_spec, own_spec]),
        out_shape=[shape, shape],
        compiler_params=_cp(("parallel", "arbitrary"), VMEM_MB),
    )(sel, g, add)


def _adamw_math(w, g, m, v):
    m = ADAM_B1 * m + (1.0 - ADAM_B1) * g
    v = ADAM_B2 * v + (1.0 - ADAM_B2) * (g * g)
    m_hat = m / (1.0 - ADAM_B1 ** ADAM_STEP)
    v_hat = v / (1.0 - ADAM_B2 ** ADAM_STEP)
    delta = -ADAM_LR * (m_hat / (jnp.sqrt(v_hat) + ADAM_EPS) + ADAM_WD * w)
    return delta, m, v


def _adamw(w, g_mine, g_sib, m, v, core, *, name):
    R, C = w.shape
    hr = R // 2
    tr = _pick_rows(hr, 16)
    nb = hr // tr
    row = pl.BlockSpec((tr, C), lambda hh, i, c: (hh * nb + i, 0))
    mine = pl.BlockSpec((NCHIP, tr, C), lambda hh, i, c: (0, jnp.where(hh == c[0], i, 0), 0))
    sibs = pl.BlockSpec((NCHIP, tr, C), lambda hh, i, c: (0, jnp.where(hh == c[0], 0, i), 0))

    def slot_sum(ref):
        acc = ref[0].astype(F32) + ref[1].astype(F32)
        for j in range(2, NCHIP):
            acc = acc + ref[j].astype(F32)
        return acc

    def body(c_ref, w_ref, gm_ref, gs_ref, m_ref, v_ref, go_ref, d_ref, mo_ref, vo_ref):
        gv = jnp.where(pl.program_id(0) == c_ref[0], slot_sum(gm_ref), slot_sum(gs_ref))
        d, mn, vn = _adamw_math(w_ref[...], gv, m_ref[...], v_ref[...])
        go_ref[...] = gv
        d_ref[...] = d
        mo_ref[...] = mn
        vo_ref[...] = vn

    return pl.pallas_call(
        body, name=name,
        grid_spec=pltpu.PrefetchScalarGridSpec(
            num_scalar_prefetch=1, grid=(2, nb),
            in_specs=[row, mine, sibs, row, row], out_specs=[row] * 4),
        out_shape=[jax.ShapeDtypeStruct((R, C), F32)] * 4,
        compiler_params=_cp(("parallel", "parallel"), VMEM_MB),
    )(core, w, g_mine, g_sib, m, v)


def _adamw_small(ws, gs, ms, vs, *, name):
    n = len(ws)

    def body(*refs):
        w_r, g_r, m_r, v_r = refs[:n], refs[n:2 * n], refs[2 * n:3 * n], refs[3 * n:4 * n]
        d_r, mo_r, vo_r = refs[4 * n:5 * n], refs[5 * n:6 * n], refs[6 * n:7 * n]
        for k in range(n):
            d, mn, vn = _adamw_math(w_r[k][...], g_r[k][...], m_r[k][...], v_r[k][...])
            d_r[k][...] = d
            mo_r[k][...] = mn
            vo_r[k][...] = vn

    shapes = [jax.ShapeDtypeStruct(w.shape, F32) for w in ws]
    res = pl.pallas_call(body, name=name, out_shape=shapes * 3)(*ws, *gs, *ms, *vs)
    return res[:n], res[n:2 * n], res[2 * n:]


_PACK_ROWS = 8


def _pack(parts):
    rows = []
    for a in parts:
        flat = a.reshape(-1)
        n = -(-flat.shape[0] // (_PACK_ROWS * 128)) * (_PACK_ROWS * 128)
        rows.append(jnp.pad(flat, (0, n - flat.shape[0])).reshape(-1, 128))
    total = sum(r.shape[0] for r in rows)
    if total % 16:
        rows.append(jnp.zeros((16 - total % 16, 128), F32))
    return jnp.concatenate(rows, axis=0)


def _unpack(p, shapes):
    out, r = [], 0
    for shp in shapes:
        n = math.prod(shp)
        nr = -(-n // (_PACK_ROWS * 128)) * _PACK_ROWS
        out.append(p[r:r + nr].reshape(-1)[:n].reshape(shp))
        r += nr
    return out


def kernel(x, mem, g_mix, w_in, ln_v_g, ln_v_b, w_s, b_s, conv_w, g_mem, w_kv, g_head, w_o, g_ffn, w_ffn1, w_ffn2, g_final, loss_target, m_g_mix, m_w_in, m_ln_v_g, m_ln_v_b, m_w_s, m_b_s, m_conv_w, m_g_mem, m_w_kv, m_g_head, m_w_o, m_g_ffn, m_w_ffn1, m_w_ffn2, m_g_final, v_g_mix, v_w_in, v_ln_v_g, v_ln_v_b, v_w_s, v_b_s, v_conv_w, v_g_mem, v_w_kv, v_g_head, v_w_o, v_g_ffn, v_w_ffn1, v_w_ffn2, v_g_final):
    sds = jax.ShapeDtypeStruct
    xi, yi = lax.axis_index("x"), lax.axis_index("y")
    shard = 2 * xi + yi
    x2d, mem2d, tgt = x[0], mem[0], loss_target[0]
    ws3, bs2 = w_s[0], b_s[0]
    g_final2 = g_final.reshape(1, D)
    dff4 = DFF // NCHIP
    din4 = DIN // NCHIP
    dcv4 = DC // NCHIP

    big = [w_in[0].T, w_kv[0], w_o[0], w_ffn1[0], w_ffn2[0]]
    big_names = ["w_in", "w_kv", "w_o", "w_ffn1", "w_ffn2"]
    slot = jnp.reshape(shard, (1,)).astype(jnp.int32)
    core = jnp.reshape(lax.axis_index("c"), (1,)).astype(jnp.int32)
    conv_pad = jnp.pad(conv_w[0], ((0, CONV_PAD[0] - 3), (0, CONV_PAD[1] - dcv4)))
    conv_slots = lax.dynamic_update_slice(jnp.zeros((NCHIP,) + CONV_PAD, F32), conv_pad[None], (shard, 0, 0))

    def gather_start(bufs, after, nm):
        return _allgather_start(bufs, after, name="ag_start_" + nm)

    def gather_wait(state, idx, after, nm):
        send, recv, bufs, _ = state
        got, token = _transfer_wait([send[k] for k in idx], [recv[k] for k in idx], [[bufs[k]] for k in idx],
                                    [(N_NEIGHBOUR_CHIPS, bufs[k].shape[1] // 2) for k in idx], after, name="ag_wait_" + nm)
        return [g[0] for g in got], token

    cast = lambda k, after: _cast_into_slot(big[k], slot, after, name="cast_" + big_names[k])
    ag_in = gather_start([cast(0, slot), conv_slots], slot, "in")
    bs_t = bs2.T

    h = _rms_fwd(x2d, g_mix, name="rms_mix", after=[ag_in[3]])
    mem_n = _rms_fwd(mem2d, g_mem, name="rms_mem", after=[h])
    kvo_b = [cast(1, mem_n)]
    kvo_b.append(cast(2, kvo_b[0]))
    w1_b = cast(3, kvo_b[1])
    w2_b = cast(4, w1_b)
    got_in, tok = gather_wait(ag_in, [0, 1], w2_b, "in")
    win4, conv4 = _forward_gathered(got_in, tok, name="ag_fwd_in")
    ag_kvo = gather_start(kvo_b, conv4, "kvo")
    w_in_t = win4.reshape(DIN, D)
    conv_full = conv4[:, :3, :dcv4].transpose(1, 0, 2).reshape(3, DC)
    (proj,) = _matmul(h, w_in_t, name="mm_proj", tb=True, M=S, N=DIN, K=D, tn=DIN // 2, outs=[sds((S, DIN), F32)],
                      after=[ag_kvo[3]])
    got_kvo, tok = gather_wait(ag_kvo, [0, 1], proj, "kvo")
    wkv4, wo4 = _forward_gathered(got_kvo, tok, name="ag_fwd_kvo")
    ag_w1 = gather_start([w1_b], wkv4, "ffn1")
    w_kv_full = wkv4.reshape(D, 2 * DM)
    w_o_full = wo4.reshape(D, D)
    (kv,) = _matmul(mem_n, w_kv_full, name="mm_kv", M=NMEM, N=2 * DM, K=D, outs=[sds((NMEM, 2 * DM), F32)],
                    after=[ag_w1[3]])
    heads, hn, ycv = _mix_fwd(proj, kv, ws3, bs_t, ln_v_g, ln_v_b, conv_full, g_head, name="mix_fwd")
    (x2,) = _matmul(hn, w_o_full, name="mm_wo", M=S, N=D, K=D, outs=[sds((S, D), F32)],
                    epi=lambda acc, res: (acc + res,), extras=[(x2d, _tile_spec())])
    h2 = _rms_fwd(x2, g_ffn, name="rms_ffn")
    got_w1, tok = gather_wait(ag_w1, [0], h2, "ffn1")
    (w14,) = _forward_gathered(got_w1, tok, name="ag_fwd_ffn1")
    ag_w2 = gather_start([w2_b], w14, "ffn2")

    def w1_cols(tn, tk):
        nb = dff4 // tn
        return pl.BlockSpec((None, tk, tn), lambda j, i, k: (j // nb, k, j % nb))

    (act,) = _matmul(h2, w14, name="mm_ffn1", M=S, N=DFF, K=D, tn=dff4, b_spec=w1_cols, outs=[sds((S, DFF), BF16)],
                     epi=lambda acc: (jnp.square(jnp.maximum(acc, 0.0)),), after=[ag_w2[3]])
    got_w2, tok = gather_wait(ag_w2, [0], act, "ffn2")
    (w24,) = _forward_gathered(got_w2, tok, name="ag_fwd_ffn2")
    w2_full = w24.reshape(DFF, D)
    (x3,) = _matmul(act, w2_full, name="mm_ffn2", M=S, N=D, K=DFF, tm=2 * TM, outs=[sds((S, D), F32)],
                    epi=lambda acc, res: (acc + res,), extras=[(x2, _tile_spec())])

    ci = lax.axis_index("c")

    def rs_sibling(g4, nm):
        return _sibling_start([g4], False, name="rs_sib_" + nm)

    def rs_chips(state, after, nm):
        send, recv, g4, land, _ = state
        (((land_, g4_),), _) = _transfer_wait(send, recv, [[land[0], g4[0]]], [(NCHIP, land[0].shape[1])], after,
                                             name="rs_sibwait_" + nm)
        part, buf = _select_half_bf16(g4_, ci, land_, slot, name="rs_add_" + nm)
        return _scatter_start([part], [buf], name="rs_start_" + nm)

    def rs_end(state, after, nm):
        send, recv, parts, bufs, _ = state
        (((buf, _),), _) = _transfer_wait(send, recv, [[bufs[0], parts[0]]], [(N_PEER_CHIPS, bufs[0].shape[1])], after,
                                          name="rs_wait_" + nm)
        return _sibling_start([buf], True, name="rs_share_" + nm)

    big_m = [m_w_in[0].T, m_w_kv[0], m_w_o[0], m_w_ffn1[0], m_w_ffn2[0]]
    big_v = [v_w_in[0].T, v_w_kv[0], v_w_o[0], v_w_ffn1[0], v_w_ffn2[0]]
    big_out = {}

    def rs_finish(k, state, after):
        send, recv, mine, land, _ = state
        nm = big_names[k]
        (((land_, mine_),), _) = _transfer_wait(send, recv, [[land[0], mine[0]]], [(NCHIP, land[0].shape[1])], after,
                                               name="rs_sharewait_" + nm)
        big_out[nm] = _adamw(big[k], mine_, land_, big_m[k], big_v[k], core, name="adamw_" + nm)
        return big_out[nm][1]

    dx3, dx3b, dg_final, loss11 = _loss_bwd(x3, g_final2, tgt, name="loss_bwd")
    (dw2,) = _matmul(act, dx3b, name="mm_dw2", ta=True, M=DFF, N=D, K=S, tn=D, outs=[sds((DFF, D), BF16)])
    sib_w2 = rs_sibling(dw2.reshape(NCHIP, dff4, D), "w_ffn2")
    (dfb,) = _matmul(dx3b, w2_full, name="mm_dact", tb=True, M=S, N=DFF, K=D, tn=dff4, outs=[sds((S, DFF), BF16)],
                     epi=lambda acc, a: (acc * (2.0 * jnp.sqrt(a.astype(F32))),), extras=[(act, _tile_spec())],
                     after=[sib_w2[4]])
    rs_w2 = rs_chips(sib_w2, dfb, "w_ffn2")

    def dw1_out(tm, tn):
        nb = dff4 // tn
        return [pl.BlockSpec((None, tm, tn), lambda j, i, k: (j // nb, i, j % nb))]

    (dw1,) = _matmul(h2, dfb, name="mm_dw1", ta=True, M=D, N=DFF, K=S, tn=dff4, outs=[sds((NCHIP, D, dff4), BF16)],
                     out_specs=dw1_out, after=[rs_w2[4]])
    sib_w1 = rs_sibling(dw1, "w_ffn1")

    def w1_rows(tn, tk):
        kb = dff4 // tk
        return pl.BlockSpec((None, tn, tk), lambda j, i, k: (k // kb, j, k % kb))

    (dh2,) = _matmul(dfb, w14, name="mm_dh2", tb=True, M=S, N=D, K=DFF, tm=2 * TM, b_spec=w1_rows,
                     outs=[sds((S, D), F32)], after=[sib_w1[4]])
    rs_w1 = rs_chips(sib_w1, dh2, "w_ffn1")
    dx2, dx2b, dg_ffn = _rms_bwd(dh2, x2, g_ffn, dx3, name="rms_ffn_bwd", after=[rs_w1[4]])
    (dwo,) = _matmul(hn, dx2b, name="mm_dwo", ta=True, M=D, N=D, K=S, outs=[sds((D, D), BF16)])
    sib_wo = rs_sibling(dwo.reshape(NCHIP, D // NCHIP, D), "w_o")
    (dhn,) = _matmul(dx2b, w_o_full, name="mm_dhn", tb=True, M=S, N=D, K=D, outs=[sds((S, D), F32)],
                     after=[sib_wo[4]])
    rs_wo = rs_chips(sib_wo, dhn, "w_o")
    sh_w2 = rs_end(rs_w2, rs_wo[4], "w_ffn2")
    dproj, dkv, dws, dbs8, dlng, dlnb, dcw8, dgh = _mix_bwd(
        dhn, heads, proj, ycv, kv, ws3, bs_t, ln_v_g, ln_v_b, conv_full, g_head, sh_w2[4], name="mix_bwd")
    (dwin_t,) = _matmul(dproj, h, name="mm_dwin", ta=True, M=DIN, N=D, K=S, tm=DIN // 2, outs=[sds((DIN, D), BF16)])
    sib_win = rs_sibling(dwin_t.reshape(NCHIP, din4, D), "w_in")
    (dwkv,) = _matmul(mem_n, dkv, name="mm_dwkv", ta=True, M=D, N=2 * DM, K=NMEM, outs=[sds((D, 2 * DM), BF16)],
                      after=[sib_win[4]])
    sib_wkv = rs_sibling(dwkv.reshape(NCHIP, D // NCHIP, 2 * DM), "w_kv")
    (dh,) = _matmul(dproj, w_in_t, name="mm_dh", M=S, N=D, K=DIN, tk=DIN, outs=[sds((S, D), F32)],
                    after=[sib_wkv[4]])
    rs_win = rs_chips(sib_win, dh, "w_in")
    rs_wkv = rs_chips(sib_wkv, rs_win[4], "w_kv")
    sh_w1 = rs_end(rs_w1, rs_wkv[4], "w_ffn1")
    dx, dg_mix = _rms_bwd(dh, x2d, g_mix, dx2, name="rms_mix_bwd", want_bf=False, after=[sh_w1[4]])
    (dmem_n,) = _matmul(dkv, w_kv_full, name="mm_dmem", tb=True, M=NMEM, N=D, K=2 * DM, outs=[sds((NMEM, D), F32)],
                        after=[dx])
    (dg_mem,) = _rms_bwd(dmem_n, mem2d, g_mem, None, name="rms_mem_bwd", want_dx=False)
    sh_wo = rs_end(rs_wo, dg_mem, "w_o")
    done = rs_finish(4, sh_w2, sh_wo[4])
    done = rs_finish(3, sh_w1, done)
    sh_win = rs_end(rs_win, done, "w_in")
    sh_wkv = rs_end(rs_wkv, sh_win[4], "w_kv")
    done = rs_finish(2, sh_wo, sh_wkv[4])
    done = rs_finish(0, sh_win, done)
    done = rs_finish(1, sh_wkv, done)

    loss = lax.psum(loss11[0, 0], ("x", "y", "c"))

    small_names = ["g_mix", "ln_v_g", "ln_v_b", "w_s", "b_s", "conv_w", "g_mem", "g_head", "g_ffn", "g_final"]
    small_part = [dg_mix, dlng, dlnb, dws, dbs8[:, 0, :], dcw8[:3], dg_mem, dgh, dg_ffn, dg_final]
    small_shapes = [(1, D), (1, DS), (1, DS), (NSH, CHUNK, CHUNK), (NSH, CHUNK), (3, DC), (1, D), (1, D), (1, D), (1, D)]
    total = _allreduce_small(_pack(small_part), done, name="allreduce_small")
    small_g = _unpack(total, small_shapes)
    small_g[5] = lax.dynamic_slice(small_g[5], (0, shard * dcv4), (3, dcv4))
    small_w = [g_mix, ln_v_g, ln_v_b, ws3, bs2, conv_w[0], g_mem, g_head, g_ffn, g_final2]
    small_m = [m_g_mix, m_ln_v_g, m_ln_v_b, m_w_s[0], m_b_s[0], m_conv_w[0], m_g_mem, m_g_head, m_g_ffn,
               m_g_final.reshape(1, D)]
    small_v = [v_g_mix, v_ln_v_g, v_ln_v_b, v_w_s[0], v_b_s[0], v_conv_w[0], v_g_mem, v_g_head, v_g_ffn,
               v_g_final.reshape(1, D)]
    s_delta, s_m, s_v = _adamw_small(small_w, small_g, small_m, small_v, name="adamw_small")
    small_out = {nm: (g, d, mn, vn) for nm, g, d, mn, vn in zip(small_names, small_g, s_delta, s_m, s_v)}

    order = ["g_mix", "w_in", "ln_v_g", "ln_v_b", "w_s", "b_s", "conv_w", "g_mem", "w_kv", "g_head", "w_o",
             "g_ffn", "w_ffn1", "w_ffn2", "g_final"]
    like = dict(g_mix=g_mix, w_in=w_in, ln_v_g=ln_v_g, ln_v_b=ln_v_b, w_s=w_s, b_s=b_s, conv_w=conv_w, g_mem=g_mem,
                w_kv=w_kv, g_head=g_head, w_o=w_o, g_ffn=g_ffn, w_ffn1=w_ffn1, w_ffn2=w_ffn2, g_final=g_final)
    res = {**big_out, **small_out}
    res["w_in"] = [a.T for a in res["w_in"]]
    outs = [loss, dx[None]]
    for k in range(4):
        outs += [res[nm][k].reshape(like[nm].shape) for nm in order]
    return tuple(outs)
```

```python
import math

import jax
import jax.numpy as jnp
from jax import lax
from jax.experimental import pallas as pl
from jax.experimental.pallas import tpu as pltpu

F32 = jnp.float32
BF16 = jnp.bfloat16
MESH = pl.DeviceIdType.MESH

D = 2048
S = 2048
HD = 128
NH = D // HD
NMH = 4
NSH = (NH - NMH) // 2
NCH = NH - NMH - NSH
DS = NSH * HD
DC = NCH * HD
DM = NMH * HD
DIN = 2 * DS + 3 * DC + DM
CHUNK = 128
NMEM = 256
DFF = 4 * D
EPS = 1e-6
NCHIP = 4
SCALE = HD ** -0.5

ADAM_LR = 0.001
ADAM_B1 = 0.9
ADAM_B2 = 0.999
ADAM_EPS = 1e-08
ADAM_WD = 0.01
ADAM_STEP = 10

TR_EW = 256
TR_MIX = 256
TM = 512
TN = 1024
TK = 2048
N_SUB = 512
VMEM_MB = 56
HALO = 8


def _pick(n, target, q=128):
    best = None
    for t in range(q, min(n, target) + 1, q):
        if n % t == 0:
            best = t
    return n if best is None else best


def _pick_rows(n, q):
    below = _pick(n, TR_EW, q)
    if 2 * below >= TR_EW:
        return below
    above = [t for t in range(TR_EW, min(n, 4 * TR_EW) + 1, q) if n % t == 0]
    return above[0] if above else below


def _cp(sem=None, vmem_mb=None, **kw):
    d = dict(kw)
    if sem is not None:
        d["dimension_semantics"] = sem
    if vmem_mb is not None:
        d["vmem_limit_bytes"] = vmem_mb << 20
    return pltpu.CompilerParams(**d)


def _gelu(x):
    z = 0.7978845608028654 * (x + 0.044715 * (x * x * x))
    return 0.5 * x * (1.0 + jnp.tanh(z))


def _gelu_with_grad(x):
    x2 = x * x
    t = jnp.tanh(0.7978845608028654 * (x + 0.044715 * (x2 * x)))
    half = 0.5 * (1.0 + t)
    return x * half, half + 0.5 * x * (1.0 - t * t) * (0.7978845608028654 * (1.0 + 3.0 * 0.044715 * x2))


def _matmul(a, b, *, name, ta=False, tb=False, M, N, K, tm=None, tn=None, tk=None, outs, epi=None,
            extras=(), b_spec=None, out_specs=None, after=(), n_split=None):
    n_after = len(after)
    tm = _pick(M, TM if tm is None else tm, 8)
    tn = _pick(N, TN if tn is None else tn)
    tk = _pick(K, TK if tk is None else tk)
    if n_split is None:
        n_split = tn // N_SUB if tn % N_SUB == 0 else 1
    nk = K // tk
    grid = (N // tn, M // tm, nk)
    a_spec = (pl.BlockSpec((tk, tm), lambda j, i, k: (k, i)) if ta
              else pl.BlockSpec((tm, tk), lambda j, i, k: (i, k)))
    if b_spec is None:
        b_spec = (pl.BlockSpec((tn, tk), lambda j, i, k: (j, k)) if tb
                  else pl.BlockSpec((tk, tn), lambda j, i, k: (k, j)))
    else:
        b_spec = b_spec(tn, tk)
    if out_specs is None:
        out_specs = [pl.BlockSpec((tm, tn), lambda j, i, k: (i, j)) for _ in outs]
    else:
        out_specs = out_specs(tm, tn)
    dn = (((0 if ta else 1,), (1 if tb else 0,)), ((), ()))
    n_ex, n_out = len(extras), len(outs)

    ns = tn // n_split

    def body(*refs):
        a_ref, b_ref = refs[0], refs[1]
        ex = refs[2:2 + n_ex]
        first_out = 2 + n_ex + n_after
        o = refs[first_out:first_out + n_out]
        acc = refs[first_out + n_out:]
        k = pl.program_id(2)

        def finish(val, cols):
            res = (val,) if epi is None else epi(val, *[e[:, cols] for e in ex])
            for r, o_ref in zip(res, o):
                o_ref[:, cols] = r.astype(o_ref.dtype)

        if nk > 1:
            @pl.when(k == 0)
            def _():
                acc[0][...] = jnp.zeros_like(acc[0])

        av = a_ref[...].astype(BF16)
        for q in range(n_split):
            cols = slice(q * ns, (q + 1) * ns)
            bq = (b_ref[cols, :] if tb else b_ref[:, cols]).astype(BF16)
            part = lax.dot_general(av, bq, dn, preferred_element_type=F32)
            if nk == 1:
                finish(part, cols)
            else:
                acc[0][:, cols] += part

        if nk > 1:
            @pl.when(k == nk - 1)
            def _():
                finish(acc[0][...], slice(0, tn))

    return pl.pallas_call(
        body, name=name, grid=grid,
        in_specs=[a_spec, b_spec] + [sp(tm, tn) for _, sp in extras] + [ANY] * n_after,
        out_specs=out_specs, out_shape=outs,
        scratch_shapes=([pltpu.VMEM((tm, tn), F32)] if nk > 1 else []),
        compiler_params=_cp(("parallel", "parallel", "arbitrary"), VMEM_MB),
    )(a, b, *[arr for arr, _ in extras], *after)


def _tile_spec():
    return lambda tm, tn: pl.BlockSpec((tm, tn), lambda j, i, k: (i, j))


def _cast_into_slot(w, slot, after, *, name):
    R, C = w.shape
    tr = _pick_rows(R, 16)

    def body(s_ref, w_ref, _after_ref, o_ref):
        o_ref[...] = w_ref[...].astype(BF16)

    return pl.pallas_call(
        body, name=name,
        grid_spec=pltpu.PrefetchScalarGridSpec(
            num_scalar_prefetch=1, grid=(R // tr,),
            in_specs=[pl.BlockSpec((tr, C), lambda i, s: (i, 0)), ANY],
            out_specs=pl.BlockSpec((None, tr, C), lambda i, s: (s[0], i, 0))),
        out_shape=jax.ShapeDtypeStruct((NCHIP, R, C), BF16),
        compiler_params=_cp(("parallel",), VMEM_MB),
    )(slot, w, after)


def _rms_fwd(x, g, *, name, after=()):
    R, C = x.shape
    tr = _pick(R, TR_EW, 16)
    n_after = len(after)

    def body(x_ref, g_ref, *rest):
        o_ref = rest[n_after]
        xv = x_ref[...]
        r = lax.rsqrt(jnp.mean(xv * xv, axis=-1, keepdims=True) + EPS)
        o_ref[...] = ((xv * r) * g_ref[...]).astype(BF16)

    return pl.pallas_call(
        body, name=name, grid=(R // tr,),
        in_specs=[pl.BlockSpec((tr, C), lambda i: (i, 0)), pl.BlockSpec((1, C), lambda i: (0, 0))] + [ANY] * n_after,
        out_specs=pl.BlockSpec((tr, C), lambda i: (i, 0)),
        out_shape=jax.ShapeDtypeStruct((R, C), BF16),
        compiler_params=_cp(("parallel",), VMEM_MB),
    )(x, g, *after)


def _rms_bwd(dh, x, g, dres, *, name, want_dx=True, want_bf=True, after=()):
    R, C = x.shape
    tr = _pick(R, TR_EW, 16)
    has_res = dres is not None
    row = pl.BlockSpec((tr, C), lambda i: (i, 0))
    vec = pl.BlockSpec((1, C), lambda i: (0, 0))

    def body(*refs):
        dh_ref, x_ref, g_ref = refs[:3]
        pos = 3
        dres_ref = None
        if has_res:
            dres_ref = refs[pos]
            pos += 1
        outs = refs[pos + len(after):]
        i = pl.program_id(0)
        xv = x_ref[...]
        r = lax.rsqrt(jnp.mean(xv * xv, axis=-1, keepdims=True) + EPS)
        xh = xv * r
        dhv = dh_ref[...]
        dg_ref = outs[-1]
        dgp = jnp.sum(dhv * xh, axis=0, keepdims=True)

        @pl.when(i == 0)
        def _():
            dg_ref[...] = dgp

        @pl.when(i > 0)
        def _():
            dg_ref[...] += dgp

        if want_dx:
            t = dhv * g_ref[...]
            dx = r * (t - xh * jnp.mean(t * xh, axis=-1, keepdims=True))
            if has_res:
                dx = dx + dres_ref[...]
            outs[0][...] = dx
            if want_bf:
                outs[1][...] = dx.astype(BF16)

    in_specs = [row, row, vec] + ([row] if has_res else []) + [ANY] * len(after)
    out_specs, out_shape = [], []
    if want_dx:
        out_specs.append(row)
        out_shape.append(jax.ShapeDtypeStruct((R, C), F32))
        if want_bf:
            out_specs.append(row)
            out_shape.append(jax.ShapeDtypeStruct((R, C), BF16))
    out_specs.append(vec)
    out_shape.append(jax.ShapeDtypeStruct((1, C), F32))
    args = [dh, x, g] + ([dres] if has_res else []) + list(after)
    return pl.pallas_call(
        body, name=name, grid=(R // tr,), in_specs=in_specs, out_specs=out_specs, out_shape=out_shape,
        compiler_params=_cp(("arbitrary",), VMEM_MB),
    )(*args)


def _loss_bwd(x3, g, tgt, *, name):
    R, C = x3.shape
    tr = _pick(R, TR_EW, 16)
    n = R // tr
    row = pl.BlockSpec((tr, C), lambda i: (i, 0))
    vec = pl.BlockSpec((1, C), lambda i: (0, 0))

    def body(x_ref, g_ref, t_ref, dx_ref, dxb_ref, dg_ref, loss_ref, acc_ref):
        i = pl.program_id(0)
        xv = x_ref[...]
        gv = g_ref[...]
        r = lax.rsqrt(jnp.mean(xv * xv, axis=-1, keepdims=True) + EPS)
        xh = xv * r
        e = xh * gv - t_ref[...]
        dy = e * (1.0 / C)
        sq = jnp.sum(e * e, axis=0, keepdims=True)
        dgp = jnp.sum(dy * xh, axis=0, keepdims=True)

        @pl.when(i == 0)
        def _():
            acc_ref[...] = sq
            dg_ref[...] = dgp

        @pl.when(i > 0)
        def _():
            acc_ref[...] += sq
            dg_ref[...] += dgp

        t = dy * gv
        dx = r * (t - xh * jnp.mean(t * xh, axis=-1, keepdims=True))
        dx_ref[...] = dx
        dxb_ref[...] = dx.astype(BF16)

        @pl.when(i == n - 1)
        def _():
            loss_ref[...] = jnp.sum(acc_ref[...], axis=-1, keepdims=True) * (0.5 / C)

    return pl.pallas_call(
        body, name=name, grid=(n,),
        in_specs=[row, vec, row],
        out_specs=[row, row, vec, pl.BlockSpec((1, 1), lambda i: (0, 0))],
        out_shape=[jax.ShapeDtypeStruct((R, C), F32), jax.ShapeDtypeStruct((R, C), BF16),
                   jax.ShapeDtypeStruct((1, C), F32), jax.ShapeDtypeStruct((1, 1), F32)],
        scratch_shapes=[pltpu.VMEM((1, C), F32)],
        compiler_params=_cp(("arbitrary",), VMEM_MB),
    )(x3, g, tgt)


def _offsets():
    u0 = 0
    v0 = DS
    b0 = 2 * DS
    c0 = b0 + DC
    x0 = c0 + DC
    q0 = x0 + DC
    return u0, v0, b0, c0, x0, q0


def _tri_mask(lower):
    r = lax.broadcasted_iota(jnp.int32, (CHUNK, CHUNK), 0)
    c = lax.broadcasted_iota(jnp.int32, (CHUNK, CHUNK), 1)
    return (r >= c) if lower else (c >= r)


def _layer_norm_stats(vg):
    mu = jnp.mean(vg, axis=-1, keepdims=True)
    vc = vg - mu
    rstd = lax.rsqrt(jnp.mean(vc * vc, axis=-1, keepdims=True) + EPS)
    return vc * rstd, rstd


def _softmax_rows(qh, kh):
    s = lax.dot_general(qh, kh, (((1,), (1,)), ((), ())), preferred_element_type=F32)
    m = jnp.max(s, axis=-1, keepdims=True)
    e = jnp.exp(s - m)
    return e / jnp.sum(e, axis=-1, keepdims=True)


def _mix_fwd(proj, kv, w_s, bs_t, ln_g, ln_b, conv_w, g_head, *, name):
    assert DS == DC
    tr = _pick(S, TR_MIX, CHUNK)
    n = S // tr
    nck = tr // CHUNK
    u0, v0, b0, c0, x0, q0 = _offsets()
    hb = tr // HALO

    def body(p_ref, cprev_ref, xprev_ref, kv_ref, ws_ref, bst_ref, lng_ref, lnb_ref, cw_ref, gh_ref,
             heads_ref, hn_ref, ycv_ref, buf_ref):
        i = pl.program_id(0)

        def emit(col, val):
            rs = lax.rsqrt(jnp.mean(val * val, axis=-1, keepdims=True) + EPS)
            heads_ref[:, col:col + HD] = val
            hn_ref[:, col:col + HD] = ((val * rs) * gh_ref[:, col:col + HD]).astype(BF16)

        vhat, _ = _layer_norm_stats(_gelu(p_ref[:, v0:v0 + DS]))
        vnb = (vhat * lng_ref[...] + lnb_ref[...]).astype(BF16)
        low = _tri_mask(True)
        for h in range(NSH):
            wt = jnp.where(low, ws_ref[h], 0.0).astype(BF16)
            bcol = bst_ref[:, h:h + 1]
            parts = []
            for c in range(nck):
                blk = vnb[c * CHUNK:(c + 1) * CHUNK, h * HD:(h + 1) * HD]
                parts.append(jnp.dot(wt, blk, preferred_element_type=F32) + bcol)
            mixed = parts[0] if nck == 1 else jnp.concatenate(parts, axis=0)
            emit(h * HD, _gelu(p_ref[:, u0 + h * HD:u0 + (h + 1) * HD]) * mixed)

        xc = p_ref[:, c0:c0 + DC] * p_ref[:, x0:x0 + DC]
        prev = cprev_ref[...] * xprev_ref[...]
        buf_ref[0:HALO, :] = jnp.where(i > 0, prev, 0.0)
        buf_ref[HALO:HALO + tr, :] = xc
        y = (cw_ref[2:3, :] * xc + cw_ref[1:2, :] * buf_ref[HALO - 1:HALO - 1 + tr, :]
             + cw_ref[0:1, :] * buf_ref[HALO - 2:HALO - 2 + tr, :])
        ycv_ref[...] = y
        cout = p_ref[:, b0:b0 + DC] * y
        for h in range(NCH):
            emit(DS + h * HD, cout[:, h * HD:(h + 1) * HD])

        for h in range(NMH):
            qh = (p_ref[:, q0 + h * HD:q0 + (h + 1) * HD] * SCALE).astype(BF16)
            kh = kv_ref[:, h * HD:(h + 1) * HD].astype(BF16)
            vh = kv_ref[:, DM + h * HD:DM + (h + 1) * HD].astype(BF16)
            p = _softmax_rows(qh, kh)
            emit(DS + DC + h * HD, jnp.dot(p.astype(BF16), vh, preferred_element_type=F32))

    full = lambda shape: pl.BlockSpec(shape, lambda i: (0,) * len(shape))
    halo_c = pl.BlockSpec((HALO, DC), lambda i: (jnp.maximum(i * hb - 1, 0), c0 // DC))
    halo_x = pl.BlockSpec((HALO, DC), lambda i: (jnp.maximum(i * hb - 1, 0), x0 // DC))
    return pl.pallas_call(
        body, name=name, grid=(n,),
        in_specs=[pl.BlockSpec((tr, DIN), lambda i: (i, 0)), halo_c, halo_x,
                  full((NMEM, 2 * DM)), full((NSH, CHUNK, CHUNK)), full((CHUNK, NSH)),
                  full((1, DS)), full((1, DS)), full((3, DC)), full((1, D))],
        out_specs=[pl.BlockSpec((tr, D), lambda i: (i, 0)), pl.BlockSpec((tr, D), lambda i: (i, 0)),
                   pl.BlockSpec((tr, DC), lambda i: (i, 0))],
        out_shape=[jax.ShapeDtypeStruct((S, D), F32), jax.ShapeDtypeStruct((S, D), BF16),
                   jax.ShapeDtypeStruct((S, DC), F32)],
        scratch_shapes=[pltpu.VMEM((tr + HALO, DC), F32)],
        compiler_params=_cp(("parallel",), VMEM_MB),
    )(proj, proj, proj, kv, w_s, bs_t, ln_g, ln_b, conv_w, g_head)


def _mix_bwd(dhn, heads, proj, ycv, kv, w_s, bs_t, ln_g, ln_b, conv_w, g_head, after, *, name):
    assert DS == DC
    tr = _pick(S, TR_MIX, CHUNK)
    n = S // tr
    nck = tr // CHUNK
    u0, v0, b0, c0, x0, q0 = _offsets()
    hb = tr // HALO
    last_hb = S // HALO - 1

    def body(dhn_ref, heads_ref, p_ref, ycv_ref, dhn_nx_ref, heads_nx_ref, b_nx_ref, kv_ref, ws_ref, bst_ref,
             lng_ref, lnb_ref, cw_ref, gh_ref, _after_ref,
             dp_ref, dkv_ref, dws_ref, dbs_ref, dlng_ref, dlnb_ref, dcw_ref, dgh_ref, buf_ref, dvn_ref):
        i = pl.program_id(0)

        @pl.when(i == 0)
        def _():
            dkv_ref[...] = jnp.zeros_like(dkv_ref)
            dws_ref[...] = jnp.zeros_like(dws_ref)
            dbs_ref[...] = jnp.zeros_like(dbs_ref)
            dlng_ref[...] = jnp.zeros_like(dlng_ref)
            dlnb_ref[...] = jnp.zeros_like(dlnb_ref)
            dcw_ref[...] = jnp.zeros_like(dcw_ref)
            dgh_ref[...] = jnp.zeros_like(dgh_ref)

        def head_bwd(a, dn, gh):
            rs = lax.rsqrt(jnp.mean(a * a, axis=-1, keepdims=True) + EPS)
            ah = a * rs
            t = dn * gh
            return rs * (t - ah * jnp.mean(t * ah, axis=-1, keepdims=True)), jnp.sum(dn * ah, axis=0, keepdims=True)

        def head_grad(col):
            da, dg = head_bwd(heads_ref[:, col:col + HD], dhn_ref[:, col:col + HD], gh_ref[:, col:col + HD])
            dgh_ref[:, col:col + HD] += dg
            return da

        vg, dvg_dv = _gelu_with_grad(p_ref[:, v0:v0 + DS])
        vhat, rstd = _layer_norm_stats(vg)
        vnb = (vhat * lng_ref[...] + lnb_ref[...]).astype(BF16)
        low = _tri_mask(True)
        ones = jnp.ones((HALO, HD), BF16)
        for h in range(NSH):
            w_h = ws_ref[h]
            wt = jnp.where(low, w_h, 0.0).astype(BF16)
            bcol = bst_ref[:, h:h + 1]
            da = head_grad(h * HD)
            ug, dug_du = _gelu_with_grad(p_ref[:, u0 + h * HD:u0 + (h + 1) * HD])
            dws = jnp.zeros((CHUNK, CHUNK), F32)
            dbs = jnp.zeros((HALO, CHUNK), F32)
            mixed_parts = []
            for c in range(nck):
                rows = slice(c * CHUNK, (c + 1) * CHUNK)
                blk = vnb[rows, h * HD:(h + 1) * HD]
                mixed_parts.append(jnp.dot(wt, blk, preferred_element_type=F32) + bcol)
                dmb = (da[rows] * ug[rows]).astype(BF16)
                dws = dws + lax.dot_general(dmb, blk, (((1,), (1,)), ((), ())), preferred_element_type=F32)
                dbs = dbs + lax.dot_general(ones, dmb, (((1,), (1,)), ((), ())), preferred_element_type=F32)
                dvn_ref[c * CHUNK:(c + 1) * CHUNK, h * HD:(h + 1) * HD] = lax.dot_general(
                    wt, dmb, (((0,), (0,)), ((), ())), preferred_element_type=F32)
            mixed = mixed_parts[0] if nck == 1 else jnp.concatenate(mixed_parts, axis=0)
            dp_ref[:, u0 + h * HD:u0 + (h + 1) * HD] = ((da * mixed) * dug_du).astype(BF16)
            dws_ref[h] += jnp.where(low, dws, 0.0)
            dbs_ref[h] += dbs
        dvn = dvn_ref[...]
        dlng_ref[...] += jnp.sum(dvn * vhat, axis=0, keepdims=True)
        dlnb_ref[...] += jnp.sum(dvn, axis=0, keepdims=True)
        dvh = dvn * lng_ref[...]
        dvg = rstd * (dvh - jnp.mean(dvh, axis=-1, keepdims=True)
                      - vhat * jnp.mean(dvh * vhat, axis=-1, keepdims=True))
        dp_ref[:, v0:v0 + DS] = (dvg * dvg_dv).astype(BF16)

        dc = jnp.concatenate([head_grad(DS + h * HD) for h in range(NCH)], axis=1)
        dc_nx = jnp.concatenate(
            [head_bwd(heads_nx_ref[:, h * HD:(h + 1) * HD], dhn_nx_ref[:, h * HD:(h + 1) * HD],
                      gh_ref[:, DS + h * HD:DS + (h + 1) * HD])[0] for h in range(NCH)], axis=1)
        bg = p_ref[:, b0:b0 + DC]
        cg = p_ref[:, c0:c0 + DC]
        xin = p_ref[:, x0:x0 + DC]
        dp_ref[:, b0:b0 + DC] = (dc * ycv_ref[...]).astype(BF16)
        dyv = dc * bg
        buf_ref[0:tr, :] = dyv
        buf_ref[tr:tr + HALO, :] = jnp.where(i < n - 1, dc_nx * b_nx_ref[...], 0.0)
        sh1 = buf_ref[1:1 + tr, :]
        sh0 = buf_ref[2:2 + tr, :]
        dxc = cw_ref[2:3, :] * dyv + cw_ref[1:2, :] * sh1 + cw_ref[0:1, :] * sh0
        xc = cg * xin
        dp_ref[:, c0:c0 + DC] = (dxc * xin).astype(BF16)
        dp_ref[:, x0:x0 + DC] = (dxc * cg).astype(BF16)
        dcw_ref[0:1, :] += jnp.sum(sh0 * xc, axis=0, keepdims=True)
        dcw_ref[1:2, :] += jnp.sum(sh1 * xc, axis=0, keepdims=True)
        dcw_ref[2:3, :] += jnp.sum(dyv * xc, axis=0, keepdims=True)

        for h in range(NMH):
            do = head_grad(DS + DC + h * HD).astype(BF16)
            qh = (p_ref[:, q0 + h * HD:q0 + (h + 1) * HD] * SCALE).astype(BF16)
            kh = kv_ref[:, h * HD:(h + 1) * HD].astype(BF16)
            vh = kv_ref[:, DM + h * HD:DM + (h + 1) * HD].astype(BF16)
            p = _softmax_rows(qh, kh)
            dpr = lax.dot_general(do, vh, (((1,), (1,)), ((), ())), preferred_element_type=F32)
            ds = (p * (dpr - jnp.sum(dpr * p, axis=-1, keepdims=True))).astype(BF16)
            dp_ref[:, q0 + h * HD:q0 + (h + 1) * HD] = (
                jnp.dot(ds, kh, preferred_element_type=F32) * SCALE).astype(BF16)
            dkv_ref[:, h * HD:(h + 1) * HD] += lax.dot_general(
                ds, qh, (((0,), (0,)), ((), ())), preferred_element_type=F32)
            dkv_ref[:, DM + h * HD:DM + (h + 1) * HD] += lax.dot_general(
                p.astype(BF16), do, (((0,), (0,)), ((), ())), preferred_element_type=F32)

    full = lambda shape: pl.BlockSpec(shape, lambda i: (0,) * len(shape))
    row = lambda c: pl.BlockSpec((tr, c), lambda i: (i, 0))
    nxt = lambda col: pl.BlockSpec((HALO, DC), lambda i: (jnp.minimum((i + 1) * hb, last_hb), col))
    return pl.pallas_call(
        body, name=name, grid=(n,),
        in_specs=[row(D), row(D), row(DIN), row(DC), nxt(DS // DC), nxt(DS // DC), nxt(b0 // DC),
                  full((NMEM, 2 * DM)), full((NSH, CHUNK, CHUNK)), full((CHUNK, NSH)),
                  full((1, DS)), full((1, DS)), full((3, DC)), full((1, D)), ANY],
        out_specs=[row(DIN), full((NMEM, 2 * DM)), full((NSH, CHUNK, CHUNK)), full((NSH, HALO, CHUNK)),
                   full((1, DS)), full((1, DS)), full((HALO, DC)), full((1, D))],
        out_shape=[jax.ShapeDtypeStruct((S, DIN), BF16), jax.ShapeDtypeStruct((NMEM, 2 * DM), F32),
                   jax.ShapeDtypeStruct((NSH, CHUNK, CHUNK), F32), jax.ShapeDtypeStruct((NSH, HALO, CHUNK), F32),
                   jax.ShapeDtypeStruct((1, DS), F32), jax.ShapeDtypeStruct((1, DS), F32),
                   jax.ShapeDtypeStruct((HALO, DC), F32), jax.ShapeDtypeStruct((1, D), F32)],
        scratch_shapes=[pltpu.VMEM((tr + HALO, DC), F32), pltpu.VMEM((tr, DS), F32)],
        compiler_params=_cp(("arbitrary",), VMEM_MB),
    )(dhn, heads, proj, ycv, dhn, heads, proj, kv, w_s, bs_t, ln_g, ln_b, conv_w, g_head, after)


def _place():
    x, y, c = lax.axis_index("x"), lax.axis_index("y"), lax.axis_index("c")
    chips = [(1 - x, y), (x, 1 - y), (1 - x, 1 - y)]
    return x, y, c, chips


ANY = pl.BlockSpec(memory_space=pl.ANY)


HBM = pl.BlockSpec(memory_space=pltpu.HBM)
SEM = pl.BlockSpec(memory_space=pltpu.SEMAPHORE)
EFFECT = pltpu.SideEffectType.DATAFLOW_SIDE_EFFECTING
N_PEER_CHIPS = 3
N_NEIGHBOUR_CHIPS = 2
CONV_PAD = (32, 256)


def _in_hbm(a):
    return pltpu.with_memory_space_constraint(a, pltpu.HBM)


def _allgather_start(bufs, after, *, name):
    nw = len(bufs)

    def body(*refs):
        ins, send, recv = refs[:nw], refs[nw + 1:2 * nw + 1], refs[2 * nw + 1:3 * nw + 1]
        token = refs[4 * nw + 1]
        x, y, c, chips = _place()
        s = 2 * x + y
        for w in range(nw):
            hr = bufs[w].shape[1] // 2
            rows = ins[w].at[s, pl.ds(c * hr, hr)]
            for cx, cy in chips[:N_NEIGHBOUR_CHIPS]:
                pltpu.make_async_remote_copy(src_ref=rows, dst_ref=rows, send_sem=send[w], recv_sem=recv[w],
                                             device_id=(cx, cy, c), device_id_type=MESH).start()
        token[...] = jnp.zeros_like(token)

    res = pl.pallas_call(
        body, name=name,
        in_specs=[HBM] * nw + [ANY],
        out_specs=[SEM] * (2 * nw) + [HBM] * nw + [pl.BlockSpec(memory_space=pltpu.VMEM)],
        out_shape=[pltpu.SemaphoreType.DMA(())] * (2 * nw) + [pltpu.HBM(a.shape, a.dtype) for a in bufs]
        + [jax.ShapeDtypeStruct((8, 128), F32)],
        input_output_aliases={w: 2 * nw + w for w in range(nw)},
        compiler_params=pltpu.CompilerParams(has_side_effects=EFFECT),
    )(*[_in_hbm(a) for a in bufs], after)
    return res[:nw], res[nw:2 * nw], res[2 * nw:3 * nw], res[3 * nw]


def _scatter_start(parts, bufs, *, name):
    nw = len(parts)

    def body(*refs):
        src, dst = refs[:nw], refs[nw:2 * nw]
        send, recv = refs[2 * nw:3 * nw], refs[3 * nw:4 * nw]
        token = refs[6 * nw]
        x, y, c, chips = _place()
        s = 2 * x + y
        for w in range(nw):
            for cx, cy in chips:
                pltpu.make_async_remote_copy(src_ref=src[w].at[2 * cx + cy], dst_ref=dst[w].at[s], send_sem=send[w],
                                             recv_sem=recv[w], device_id=(cx, cy, c), device_id_type=MESH).start()
        token[...] = jnp.zeros_like(token)

    res = pl.pallas_call(
        body, name=name,
        in_specs=[HBM] * (2 * nw),
        out_specs=[SEM] * (2 * nw) + [HBM] * (2 * nw) + [pl.BlockSpec(memory_space=pltpu.VMEM)],
        out_shape=[pltpu.SemaphoreType.DMA(())] * (2 * nw) + [pltpu.HBM(a.shape, a.dtype) for a in parts + bufs]
        + [jax.ShapeDtypeStruct((8, 128), F32)],
        input_output_aliases={k: 2 * nw + k for k in range(2 * nw)},
        compiler_params=pltpu.CompilerParams(has_side_effects=EFFECT),
    )(*[_in_hbm(a) for a in parts + bufs])
    return res[:nw], res[nw:2 * nw], res[2 * nw:3 * nw], res[3 * nw:4 * nw], res[4 * nw]


def _sibling_start(srcs, whole, *, name):
    nw = len(srcs)
    lands = [lax.empty((a.shape[0], a.shape[1] if whole else a.shape[1] // 2, a.shape[2]), a.dtype) for a in srcs]

    def body(*refs):
        src, land = refs[:nw], refs[nw:2 * nw]
        send, recv = refs[2 * nw:3 * nw], refs[3 * nw:4 * nw]
        token = refs[6 * nw]
        x, y, c, _ = _place()
        for w in range(nw):
            hr = srcs[w].shape[1] // 2
            rows = src[w] if whole else src[w].at[:, pl.ds((1 - c) * hr, hr)]
            pltpu.make_async_remote_copy(src_ref=rows, dst_ref=land[w], send_sem=send[w], recv_sem=recv[w],
                                         device_id=(x, y, 1 - c), device_id_type=MESH).start()
        token[...] = jnp.zeros_like(token)

    res = pl.pallas_call(
        body, name=name,
        in_specs=[HBM] * (2 * nw),
        out_specs=[SEM] * (2 * nw) + [HBM] * (2 * nw) + [pl.BlockSpec(memory_space=pltpu.VMEM)],
        out_shape=[pltpu.SemaphoreType.DMA(())] * (2 * nw) + [pltpu.HBM(a.shape, a.dtype) for a in srcs + lands]
        + [jax.ShapeDtypeStruct((8, 128), F32)],
        input_output_aliases={k: 2 * nw + k for k in range(2 * nw)},
        compiler_params=pltpu.CompilerParams(has_side_effects=EFFECT),
    )(*[_in_hbm(a) for a in srcs + lands])
    return res[:nw], res[nw:2 * nw], res[2 * nw:3 * nw], res[3 * nw:4 * nw], res[4 * nw]


def _transfer_wait(sends, recvs, thru, sizes, after, *, name):
    n = len(sends)
    flat = [a for group in thru for a in group]

    def body(*refs):
        bufs = refs[:len(flat)]
        send = refs[len(flat):len(flat) + n]
        recv = refs[len(flat) + n:len(flat) + 2 * n]
        token = refs[2 * len(flat) + 2 * n + 1]
        token[...] = jnp.zeros_like(token)
        x, y, c, _ = _place()
        pos = 0
        for k in range(n):
            slots, rows = sizes[k]
            region = bufs[pos].at[pl.ds(0, slots), pl.ds(0, rows)]
            pos += len(thru[k])
            cp = pltpu.make_async_remote_copy(src_ref=region, dst_ref=region, send_sem=send[k], recv_sem=recv[k],
                                              device_id=(x, y, 1 - c), device_id_type=MESH)
            cp.wait_send()
            cp.wait_recv()

    res = pl.pallas_call(
        body, name=name,
        in_specs=[HBM] * len(flat) + [SEM] * (2 * n) + [pl.BlockSpec(memory_space=pl.ANY)],
        out_specs=[HBM] * len(flat) + [pl.BlockSpec(memory_space=pltpu.VMEM)],
        out_shape=[pltpu.HBM(a.shape, a.dtype) for a in flat] + [jax.ShapeDtypeStruct((8, 128), F32)],
        input_output_aliases={k: k for k in range(len(flat))},
        compiler_params=pltpu.CompilerParams(has_side_effects=EFFECT),
    )(*flat, *sends, *recvs, after)
    out, pos = [], 0
    for group in thru:
        out.append(res[pos:pos + len(group)])
        pos += len(group)
    return out, res[len(flat)]


def _forward_gathered(bufs, after, *, name):
    nw = len(bufs)

    def body(*refs):
        outs = refs[nw + 1:2 * nw + 1]
        d_send, d_recv, i_send, i_recv = refs[2 * nw + 1:]
        x, y, c, chips = _place()
        me, sibling = (x, y, c), (x, y, 1 - c)
        slots = [2 * cx + cy for cx, cy in chips]

        def rows(w, j, start, n):
            return outs[w].at[slots[j], pl.ds(start, n)]

        def d2d(w, j, which, to):
            hr = bufs[w].shape[1] // 2
            r = rows(w, j, which * hr, hr)
            return pltpu.make_async_remote_copy(
                src_ref=r, dst_ref=r, send_sem=d_send.at[N_PEER_CHIPS * w + j],
                recv_sem=d_recv.at[N_PEER_CHIPS * w + j], device_id=to, device_id_type=MESH)

        def ici(w, j, slot_j, to):
            q = bufs[w].shape[1] // 4
            r = rows(w, slot_j, c * 2 * q + j * q, q)
            return pltpu.make_async_remote_copy(
                src_ref=r, dst_ref=r, send_sem=i_send.at[N_NEIGHBOUR_CHIPS * w + j],
                recv_sem=i_recv.at[N_NEIGHBOUR_CHIPS * w + j], device_id=to, device_id_type=MESH)

        started = []
        for w in range(nw):
            started += [ici(w, 0, 0, (*chips[1], c)), ici(w, 1, 1, (*chips[0], c))]
            started += [d2d(w, j, c, sibling) for j in range(N_NEIGHBOUR_CHIPS)]
        for cp in started:
            cp.start()
        diag = N_PEER_CHIPS - 1
        for w in range(nw):
            for j in range(N_NEIGHBOUR_CHIPS):
                ici(w, j, diag, me).wait_recv()
            cp = d2d(w, diag, c, sibling)
            cp.start()
            started.append(cp)
        for w in range(nw):
            for j in range(N_PEER_CHIPS):
                d2d(w, j, 1 - c, me).wait_recv()
        for cp in started:
            cp.wait_send()

    return pl.pallas_call(
        body, name=name,
        in_specs=[ANY] * (nw + 1), out_specs=[ANY] * nw,
        out_shape=[jax.ShapeDtypeStruct(a.shape, a.dtype) for a in bufs],
        input_output_aliases={w: w for w in range(nw)},
        scratch_shapes=[pltpu.SemaphoreType.DMA((N_PEER_CHIPS * nw,)), pltpu.SemaphoreType.DMA((N_PEER_CHIPS * nw,)),
                        pltpu.SemaphoreType.DMA((N_NEIGHBOUR_CHIPS * nw,)),
                        pltpu.SemaphoreType.DMA((N_NEIGHBOUR_CHIPS * nw,))],
    )(*bufs, after)


def _allreduce_small(p, after, *, name):
    R = p.shape[0]
    hr = R // 2

    def body(p_ref, _after_ref, out_ref, sib_ref, sum_ref, gat_ref, tot_ref, send, recv):
        x, y, c, chips = _place()
        s = 2 * x + y
        sibling = (x, y, 1 - c)
        rows = pl.ds(pl.multiple_of(c * hr, 8), hr)
        swap = pltpu.make_async_remote_copy(src_ref=p_ref, dst_ref=sib_ref, send_sem=send.at[0], recv_sem=recv.at[0],
                                            device_id=sibling, device_id_type=MESH)
        swap.start()
        swap.wait()
        sum_ref[...] = p_ref[...] + sib_ref[...]
        gat_ref[s] = sum_ref[rows, :]
        cps = [pltpu.make_async_remote_copy(src_ref=sum_ref.at[rows], dst_ref=gat_ref.at[s], send_sem=send.at[1 + j],
                                            recv_sem=recv.at[1 + j], device_id=(cx, cy, c), device_id_type=MESH)
               for j, (cx, cy) in enumerate(chips)]
        for cp in cps:
            cp.start()
        for cp in cps:
            cp.wait()
        tot_ref[...] = ((gat_ref[0] + gat_ref[1]) + gat_ref[2]) + gat_ref[3]
        out_ref[rows, :] = tot_ref[...]
        share = pltpu.make_async_remote_copy(src_ref=tot_ref, dst_ref=out_ref.at[rows], send_sem=send.at[4],
                                             recv_sem=recv.at[4], device_id=sibling, device_id_type=MESH)
        share.start()
        share.wait_send()
        other = out_ref.at[pl.ds(pl.multiple_of((1 - c) * hr, 8), hr)]
        pltpu.make_async_remote_copy(src_ref=other, dst_ref=other, send_sem=send.at[4], recv_sem=recv.at[4],
                                     device_id=(x, y, c), device_id_type=MESH).wait_recv()

    vmem = pl.BlockSpec(memory_space=pltpu.VMEM)
    return pl.pallas_call(
        body, name=name, in_specs=[vmem, ANY], out_specs=vmem,
        out_shape=jax.ShapeDtypeStruct((R, 128), F32),
        scratch_shapes=[pltpu.VMEM((R, 128), F32), pltpu.VMEM((R, 128), F32), pltpu.VMEM((NCHIP, hr, 128), F32),
                        pltpu.VMEM((hr, 128), F32), pltpu.SemaphoreType.DMA((5,)), pltpu.SemaphoreType.DMA((5,))],
    )(p, after)


def _select_half_bf16(g, half, add, slot, *, name):
    _, R, C = g.shape
    hr = R // 2
    tr = _pick_rows(hr, 16)
    nb = hr // tr
    sel = jnp.concatenate([jnp.reshape(half, (1,)).astype(jnp.int32), slot])

    def body(s_ref, g_ref, a_ref, o_ref, own_ref):
        val = (g_ref[...].astype(F32) + a_ref[...].astype(F32)).astype(BF16)
        o_ref[...] = val

        @pl.when(pl.program_id(1) == s_ref[1])
        def _():
            own_ref[...] = val

    g_spec = pl.BlockSpec((None, tr, C), lambda i, j, s: (j, s[0] * nb + i, 0))
    o_spec = pl.BlockSpec((None, tr, C), lambda i, j, s: (j, i, 0))
    own_spec = pl.BlockSpec((None, tr, C), lambda i, j, s: (s[1], i, 0))
    shape = jax.ShapeDtypeStruct((NCHIP, hr, C), BF16)
    return pl.pallas_call(
        body, name=name,
        grid_spec=pltpu.PrefetchScalarGridSpec(
            num_scalar_prefetch=1, grid=(nb, NCHIP), in_specs=[g_spec, o_spec], out_specs=[o_spec, own_spec]),
        out_shape=[shape, shape],
        compiler_params=_cp(("parallel", "arbitrary"), VMEM_MB),
    )(sel, g, add)


def _adamw_math(w, g, m, v):
    m = ADAM_B1 * m + (1.0 - ADAM_B1) * g
    v = ADAM_B2 * v + (1.0 - ADAM_B2) * (g * g)
    m_hat = m / (1.0 - ADAM_B1 ** ADAM_STEP)
    v_hat = v / (1.0 - ADAM_B2 ** ADAM_STEP)
    delta = -ADAM_LR * (m_hat / (jnp.sqrt(v_hat) + ADAM_EPS) + ADAM_WD * w)
    return delta, m, v


def _adamw(w, g_mine, g_sib, m, v, core, *, name):
    R, C = w.shape
    hr = R // 2
    tr = _pick_rows(hr, 16)
    nb = hr // tr
    row = pl.BlockSpec((tr, C), lambda hh, i, c: (hh * nb + i, 0))
    mine = pl.BlockSpec((NCHIP, tr, C), lambda hh, i, c: (0, jnp.where(hh == c[0], i, 0), 0))
    sibs = pl.BlockSpec((NCHIP, tr, C), lambda hh, i, c: (0, jnp.where(hh == c[0], 0, i), 0))

    def slot_sum(ref):
        acc = ref[0].astype(F32) + ref[1].astype(F32)
        for j in range(2, NCHIP):
            acc = acc + ref[j].astype(F32)
        return acc

    def body(c_ref, w_ref, gm_ref, gs_ref, m_ref, v_ref, go_ref, d_ref, mo_ref, vo_ref):
        gv = jnp.where(pl.program_id(0) == c_ref[0], slot_sum(gm_ref), slot_sum(gs_ref))
        d, mn, vn = _adamw_math(w_ref[...], gv, m_ref[...], v_ref[...])
        go_ref[...] = gv
        d_ref[...] = d
        mo_ref[...] = mn
        vo_ref[...] = vn

    return pl.pallas_call(
        body, name=name,
        grid_spec=pltpu.PrefetchScalarGridSpec(
            num_scalar_prefetch=1, grid=(2, nb),
            in_specs=[row, mine, sibs, row, row], out_specs=[row] * 4),
        out_shape=[jax.ShapeDtypeStruct((R, C), F32)] * 4,
        compiler_params=_cp(("parallel", "parallel"), VMEM_MB),
    )(core, w, g_mine, g_sib, m, v)


def _adamw_small(ws, gs, ms, vs, *, name):
    n = len(ws)

    def body(*refs):
        w_r, g_r, m_r, v_r = refs[:n], refs[n:2 * n], refs[2 * n:3 * n], refs[3 * n:4 * n]
        d_r, mo_r, vo_r = refs[4 * n:5 * n], refs[5 * n:6 * n], refs[6 * n:7 * n]
        for k in range(n):
            d, mn, vn = _adamw_math(w_r[k][...], g_r[k][...], m_r[k][...], v_r[k][...])
            d_r[k][...] = d
            mo_r[k][...] = mn
            vo_r[k][...] = vn

    shapes = [jax.ShapeDtypeStruct(w.shape, F32) for w in ws]
    res = pl.pallas_call(body, name=name, out_shape=shapes * 3)(*ws, *gs, *ms, *vs)
    return res[:n], res[n:2 * n], res[2 * n:]


_PACK_ROWS = 8


def _pack(parts):
    rows = []
    for a in parts:
        flat = a.reshape(-1)
        n = -(-flat.shape[0] // (_PACK_ROWS * 128)) * (_PACK_ROWS * 128)
        rows.append(jnp.pad(flat, (0, n - flat.shape[0])).reshape(-1, 128))
    total = sum(r.shape[0] for r in rows)
    if total % 16:
        rows.append(jnp.zeros((16 - total % 16, 128), F32))
    return jnp.concatenate(rows, axis=0)


def _unpack(p, shapes):
    out, r = [], 0
    for shp in shapes:
        n = math.prod(shp)
        nr = -(-n // (_PACK_ROWS * 128)) * _PACK_ROWS
        out.append(p[r:r + nr].reshape(-1)[:n].reshape(shp))
        r += nr
    return out


def kernel(x, mem, g_mix, w_in, ln_v_g, ln_v_b, w_s, b_s, conv_w, g_mem, w_kv, g_head, w_o, g_ffn, w_ffn1, w_ffn2, g_final, loss_target, m_g_mix, m_w_in, m_ln_v_g, m_ln_v_b, m_w_s, m_b_s, m_conv_w, m_g_mem, m_w_kv, m_g_head, m_w_o, m_g_ffn, m_w_ffn1, m_w_ffn2, m_g_final, v_g_mix, v_w_in, v_ln_v_g, v_ln_v_b, v_w_s, v_b_s, v_conv_w, v_g_mem, v_w_kv, v_g_head, v_w_o, v_g_ffn, v_w_ffn1, v_w_ffn2, v_g_final):
    sds = jax.ShapeDtypeStruct
    xi, yi = lax.axis_index("x"), lax.axis_index("y")
    shard = 2 * xi + yi
    x2d, mem2d, tgt = x[0], mem[0], loss_target[0]
    ws3, bs2 = w_s[0], b_s[0]
    g_final2 = g_final.reshape(1, D)
    dff4 = DFF // NCHIP
    din4 = DIN // NCHIP
    dcv4 = DC // NCHIP

    big = [w_in[0].T, w_kv[0], w_o[0], w_ffn1[0], w_ffn2[0]]
    big_names = ["w_in", "w_kv", "w_o", "w_ffn1", "w_ffn2"]
    slot = jnp.reshape(shard, (1,)).astype(jnp.int32)
    core = jnp.reshape(lax.axis_index("c"), (1,)).astype(jnp.int32)
    conv_pad = jnp.pad(conv_w[0], ((0, CONV_PAD[0] - 3), (0, CONV_PAD[1] - dcv4)))
    conv_slots = lax.dynamic_update_slice(jnp.zeros((NCHIP,) + CONV_PAD, F32), conv_pad[None], (shard, 0, 0))

    def gather_start(bufs, after, nm):
        return _allgather_start(bufs, after, name="ag_start_" + nm)

    def gather_wait(state, idx, after, nm):
        send, recv, bufs, _ = state
        got, token = _transfer_wait([send[k] for k in idx], [recv[k] for k in idx], [[bufs[k]] for k in idx],
                                    [(N_NEIGHBOUR_CHIPS, bufs[k].shape[1] // 2) for k in idx], after, name="ag_wait_" + nm)
        return [g[0] for g in got], token

    cast = lambda k, after: _cast_into_slot(big[k], slot, after, name="cast_" + big_names[k])
    ag_in = gather_start([cast(0, slot), conv_slots], slot, "in")
    bs_t = bs2.T

    h = _rms_fwd(x2d, g_mix, name="rms_mix", after=[ag_in[3]])
    mem_n = _rms_fwd(mem2d, g_mem, name="rms_mem", after=[h])
    kvo_b = [cast(1, mem_n)]
    kvo_b.append(cast(2, kvo_b[0]))
    w1_b = cast(3, kvo_b[1])
    w2_b = cast(4, w1_b)
    got_in, tok = gather_wait(ag_in, [0, 1], w2_b, "in")
    win4, conv4 = _forward_gathered(got_in, tok, name="ag_fwd_in")
    ag_kvo = gather_start(kvo_b, conv4, "kvo")
    w_in_t = win4.reshape(DIN, D)
    conv_full = conv4[:, :3, :dcv4].transpose(1, 0, 2).reshape(3, DC)
    (proj,) = _matmul(h, w_in_t, name="mm_proj", tb=True, M=S, N=DIN, K=D, tn=DIN // 2, outs=[sds((S, DIN), F32)],
                      after=[ag_kvo[3]])
    got_kvo, tok = gather_wait(ag_kvo, [0, 1], proj, "kvo")
    wkv4, wo4 = _forward_gathered(got_kvo, tok, name="ag_fwd_kvo")
    ag_w1 = gather_start([w1_b], wkv4, "ffn1")
    w_kv_full = wkv4.reshape(D, 2 * DM)
    w_o_full = wo4.reshape(D, D)
    (kv,) = _matmul(mem_n, w_kv_full, name="mm_kv", M=NMEM, N=2 * DM, K=D, outs=[sds((NMEM, 2 * DM), F32)],
                    after=[ag_w1[3]])
    heads, hn, ycv = _mix_fwd(proj, kv, ws3, bs_t, ln_v_g, ln_v_b, conv_full, g_head, name="mix_fwd")
    (x2,) = _matmul(hn, w_o_full, name="mm_wo", M=S, N=D, K=D, outs=[sds((S, D), F32)],
                    epi=lambda acc, res: (acc + res,), extras=[(x2d, _tile_spec())])
    h2 = _rms_fwd(x2, g_ffn, name="rms_ffn")
    got_w1, tok = gather_wait(ag_w1, [0], h2, "ffn1")
    (w14,) = _forward_gathered(got_w1, tok, name="ag_fwd_ffn1")
    ag_w2 = gather_start([w2_b], w14, "ffn2")

    def w1_cols(tn, tk):
        nb = dff4 // tn
        return pl.BlockSpec((None, tk, tn), lambda j, i, k: (j // nb, k, j % nb))

    (act,) = _matmul(h2, w14, name="mm_ffn1", M=S, N=DFF, K=D, tn=dff4, b_spec=w1_cols, outs=[sds((S, DFF), BF16)],
                     epi=lambda acc: (jnp.square(jnp.maximum(acc, 0.0)),), after=[ag_w2[3]])
    got_w2, tok = gather_wait(ag_w2, [0], act, "ffn2")
    (w24,) = _forward_gathered(got_w2, tok, name="ag_fwd_ffn2")
    w2_full = w24.reshape(DFF, D)
    (x3,) = _matmul(act, w2_full, name="mm_ffn2", M=S, N=D, K=DFF, tm=2 * TM, outs=[sds((S, D), F32)],
                    epi=lambda acc, res: (acc + res,), extras=[(x2, _tile_spec())])

    ci = lax.axis_index("c")

    def rs_sibling(g4, nm):
        return _sibling_start([g4], False, name="rs_sib_" + nm)

    def rs_chips(state, after, nm):
        send, recv, g4, land, _ = state
        (((land_, g4_),), _) = _transfer_wait(send, recv, [[land[0], g4[0]]], [(NCHIP, land[0].shape[1])], after,
                                             name="rs_sibwait_" + nm)
        part, buf = _select_half_bf16(g4_, ci, land_, slot, name="rs_add_" + nm)
        return _scatter_start([part], [buf], name="rs_start_" + nm)

    def rs_end(state, after, nm):
        send, recv, parts, bufs, _ = state
        (((buf, _),), _) = _transfer_wait(send, recv, [[bufs[0], parts[0]]], [(N_PEER_CHIPS, bufs[0].shape[1])], after,
                                          name="rs_wait_" + nm)
        return _sibling_start([buf], True, name="rs_share_" + nm)

    big_m = [m_w_in[0].T, m_w_kv[0], m_w_o[0], m_w_ffn1[0], m_w_ffn2[0]]
    big_v = [v_w_in[0].T, v_w_kv[0], v_w_o[0], v_w_ffn1[0], v_w_ffn2[0]]
    big_out = {}

    def rs_finish(k, state, after):
        send, recv, mine, land, _ = state
        nm = big_names[k]
        (((land_, mine_),), _) = _transfer_wait(send, recv, [[land[0], mine[0]]], [(NCHIP, land[0].shape[1])], after,
                                               name="rs_sharewait_" + nm)
        big_out[nm] = _adamw(big[k], mine_, land_, big_m[k], big_v[k], core, name="adamw_" + nm)
        return big_out[nm][1]

    dx3, dx3b, dg_final, loss11 = _loss_bwd(x3, g_final2, tgt, name="loss_bwd")
    (dw2,) = _matmul(act, dx3b, name="mm_dw2", ta=True, M=DFF, N=D, K=S, tn=D, outs=[sds((DFF, D), BF16)])
    sib_w2 = rs_sibling(dw2.reshape(NCHIP, dff4, D), "w_ffn2")
    (dfb,) = _matmul(dx3b, w2_full, name="mm_dact", tb=True, M=S, N=DFF, K=D, tn=dff4, outs=[sds((S, DFF), BF16)],
                     epi=lambda acc, a: (acc * (2.0 * jnp.sqrt(a.astype(F32))),), extras=[(act, _tile_spec())],
                     after=[sib_w2[4]])
    rs_w2 = rs_chips(sib_w2, dfb, "w_ffn2")

    def dw1_out(tm, tn):
        nb = dff4 // tn
        return [pl.BlockSpec((None, tm, tn), lambda j, i, k: (j // nb, i, j % nb))]

    (dw1,) = _matmul(h2, dfb, name="mm_dw1", ta=True, M=D, N=DFF, K=S, tn=dff4, outs=[sds((NCHIP, D, dff4), BF16)],
                     out_specs=dw1_out, after=[rs_w2[4]])
    sib_w1 = rs_sibling(dw1, "w_ffn1")

    def w1_rows(tn, tk):
        kb = dff4 // tk
        return pl.BlockSpec((None, tn, tk), lambda j, i, k: (k // kb, j, k % kb))

    (dh2,) = _matmul(dfb, w14, name="mm_dh2", tb=True, M=S, N=D, K=DFF, tm=2 * TM, b_spec=w1_rows,
                     outs=[sds((S, D), F32)], after=[sib_w1[4]])
    rs_w1 = rs_chips(sib_w1, dh2, "w_ffn1")
    dx2, dx2b, dg_ffn = _rms_bwd(dh2, x2, g_ffn, dx3, name="rms_ffn_bwd", after=[rs_w1[4]])
    (dwo,) = _matmul(hn, dx2b, name="mm_dwo", ta=True, M=D, N=D, K=S, outs=[sds((D, D), BF16)])
    sib_wo = rs_sibling(dwo.reshape(NCHIP, D // NCHIP, D), "w_o")
    (dhn,) = _matmul(dx2b, w_o_full, name="mm_dhn", tb=True, M=S, N=D, K=D, outs=[sds((S, D), F32)],
                     after=[sib_wo[4]])
    rs_wo = rs_chips(sib_wo, dhn, "w_o")
    sh_w2 = rs_end(rs_w2, rs_wo[4], "w_ffn2")
    dproj, dkv, dws, dbs8, dlng, dlnb, dcw8, dgh = _mix_bwd(
        dhn, heads, proj, ycv, kv, ws3, bs_t, ln_v_g, ln_v_b, conv_full, g_head, sh_w2[4], name="mix_bwd")
    (dwin_t,) = _matmul(dproj, h, name="mm_dwin", ta=True, M=DIN, N=D, K=S, tm=DIN // 2, outs=[sds((DIN, D), BF16)])
    sib_win = rs_sibling(dwin_t.reshape(NCHIP, din4, D), "w_in")
    (dwkv,) = _matmul(mem_n, dkv, name="mm_dwkv", ta=True, M=D, N=2 * DM, K=NMEM, outs=[sds((D, 2 * DM), BF16)],
                      after=[sib_win[4]])
    sib_wkv = rs_sibling(dwkv.reshape(NCHIP, D // NCHIP, 2 * DM), "w_kv")
    (dh,) = _matmul(dproj, w_in_t, name="mm_dh", M=S, N=D, K=DIN, tk=DIN, outs=[sds((S, D), F32)],
                    after=[sib_wkv[4]])
    rs_win = rs_chips(sib_win, dh, "w_in")
    rs_wkv = rs_chips(sib_wkv, rs_win[4], "w_kv")
    sh_w1 = rs_end(rs_w1, rs_wkv[4], "w_ffn1")
    dx, dg_mix = _rms_bwd(dh, x2d, g_mix, dx2, name="rms_mix_bwd", want_bf=False, after=[sh_w1[4]])
    (dmem_n,) = _matmul(dkv, w_kv_full, name="mm_dmem", tb=True, M=NMEM, N=D, K=2 * DM, outs=[sds((NMEM, D), F32)],
                        after=[dx])
    (dg_mem,) = _rms_bwd(dmem_n, mem2d, g_mem, None, name="rms_mem_bwd", want_dx=False)
    sh_wo = rs_end(rs_wo, dg_mem, "w_o")
    done = rs_finish(4, sh_w2, sh_wo[4])
    done = rs_finish(3, sh_w1, done)
    sh_win = rs_end(rs_win, done, "w_in")
    sh_wkv = rs_end(rs_wkv, sh_win[4], "w_kv")
    done = rs_finish(2, sh_wo, sh_wkv[4])
    done = rs_finish(0, sh_win, done)
    done = rs_finish(1, sh_wkv, done)

    loss = lax.psum(loss11[0, 0], ("x", "y", "c"))

    small_names = ["g_mix", "ln_v_g", "ln_v_b", "w_s", "b_s", "conv_w", "g_mem", "g_head", "g_ffn", "g_final"]
    small_part = [dg_mix, dlng, dlnb, dws, dbs8[:, 0, :], dcw8[:3], dg_mem, dgh, dg_ffn, dg_final]
    small_shapes = [(1, D), (1, DS), (1, DS), (NSH, CHUNK, CHUNK), (NSH, CHUNK), (3, DC), (1, D), (1, D), (1, D), (1, D)]
    total = _allreduce_small(_pack(small_part), done, name="allreduce_small")
    small_g = _unpack(total, small_shapes)
    small_g[5] = lax.dynamic_slice(small_g[5], (0, shard * dcv4), (3, dcv4))
    small_w = [g_mix, ln_v_g, ln_v_b, ws3, bs2, conv_w[0], g_mem, g_head, g_ffn, g_final2]
    small_m = [m_g_mix, m_ln_v_g, m_ln_v_b, m_w_s[0], m_b_s[0], m_conv_w[0], m_g_mem, m_g_head, m_g_ffn,
               m_g_final.reshape(1, D)]
    small_v = [v_g_mix, v_ln_v_g, v_ln_v_b, v_w_s[0], v_b_s[0], v_conv_w[0], v_g_mem, v_g_head, v_g_ffn,
               v_g_final.reshape(1, D)]
    s_delta, s_m, s_v = _adamw_small(small_w, small_g, small_m, small_v, name="adamw_small")
    small_out = {nm: (g, d, mn, vn) for nm, g, d, mn, vn in zip(small_names, small_g, s_delta, s_m, s_v)}

    order = ["g_mix", "w_in", "ln_v_g", "ln_v_b", "w_s", "b_s", "conv_w", "g_mem", "w_kv", "g_head", "w_o",
             "g_ffn", "w_ffn1", "w_ffn2", "g_final"]
    like = dict(g_mix=g_mix, w_in=w_in, ln_v_g=ln_v_g, ln_v_b=ln_v_b, w_s=w_s, b_s=b_s, conv_w=conv_w, g_mem=g_mem,
                w_kv=w_kv, g_head=g_head, w_o=w_o, g_ffn=g_ffn, w_ffn1=w_ffn1, w_ffn2=w_ffn2, g_final=g_final)
    res = {**big_out, **small_out}
    res["w_in"] = [a.T for a in res["w_in"]]
    outs = [loss, dx[None]]
    for k in range(4):
        outs += [res[nm][k].reshape(like[nm].shape) for nm in order]
    return tuple(outs)
```

```python
import math

import jax
import jax.numpy as jnp
from jax import lax
from jax.experimental import pallas as pl
from jax.experimental.pallas import tpu as pltpu

F32 = jnp.float32
BF16 = jnp.bfloat16
MESH = pl.DeviceIdType.MESH

D = 2048
S = 2048
HD = 128
NH = D // HD
NMH = 4
NSH = (NH - NMH) // 2
NCH = NH - NMH - NSH
DS = NSH * HD
DC = NCH * HD
DM = NMH * HD
DIN = 2 * DS + 3 * DC + DM
CHUNK = 128
NMEM = 256
DFF = 4 * D
EPS = 1e-6
NCHIP = 4
SCALE = HD ** -0.5

ADAM_LR = 0.001
ADAM_B1 = 0.9
ADAM_B2 = 0.999
ADAM_EPS = 1e-08
ADAM_WD = 0.01
ADAM_STEP = 10

TR_EW = 256
TR_MIX = 256
TM = 512
TN = 1024
TK = 2048
N_SUB = 512
VMEM_MB = 56
HALO = 8


def _pick(n, target, q=128):
    best = None
    for t in range(q, min(n, target) + 1, q):
        if n % t == 0:
            best = t
    return n if best is None else best


def _pick_rows(n, q):
    below = _pick(n, TR_EW, q)
    if 2 * below >= TR_EW:
        return below
    above = [t for t in range(TR_EW, min(n, 4 * TR_EW) + 1, q) if n % t == 0]
    return above[0] if above else below


def _cp(sem=None, vmem_mb=None, **kw):
    d = dict(kw)
    if sem is not None:
        d["dimension_semantics"] = sem
    if vmem_mb is not None:
        d["vmem_limit_bytes"] = vmem_mb << 20
    return pltpu.CompilerParams(**d)


def _gelu(x):
    z = 0.7978845608028654 * (x + 0.044715 * (x * x * x))
    return 0.5 * x * (1.0 + jnp.tanh(z))


def _gelu_with_grad(x):
    x2 = x * x
    t = jnp.tanh(0.7978845608028654 * (x + 0.044715 * (x2 * x)))
    half = 0.5 * (1.0 + t)
    return x * half, half + 0.5 * x * (1.0 - t * t) * (0.7978845608028654 * (1.0 + 3.0 * 0.044715 * x2))


def _matmul(a, b, *, name, ta=False, tb=False, M, N, K, tm=None, tn=None, tk=None, outs, epi=None,
            extras=(), b_spec=None, out_specs=None, after=(), n_split=None):
    n_after = len(after)
    tm = _pick(M, TM if tm is None else tm, 8)
    tn = _pick(N, TN if tn is None else tn)
    tk = _pick(K, TK if tk is None else tk)
    if n_split is None:
        n_split = tn // N_SUB if tn % N_SUB == 0 else 1
    nk = K // tk
    grid = (N // tn, M // tm, nk)
    a_spec = (pl.BlockSpec((tk, tm), lambda j, i, k: (k, i)) if ta
              else pl.BlockSpec((tm, tk), lambda j, i, k: (i, k)))
    if b_spec is None:
        b_spec = (pl.BlockSpec((tn, tk), lambda j, i, k: (j, k)) if tb
                  else pl.BlockSpec((tk, tn), lambda j, i, k: (k, j)))
    else:
        b_spec = b_spec(tn, tk)
    if out_specs is None:
        out_specs = [pl.BlockSpec((tm, tn), lambda j, i, k: (i, j)) for _ in outs]
    else:
        out_specs = out_specs(tm, tn)
    dn = (((0 if ta else 1,), (1 if tb else 0,)), ((), ()))
    n_ex, n_out = len(extras), len(outs)

    ns = tn // n_split

    def body(*refs):
        a_ref, b_ref = refs[0], refs[1]
        ex = refs[2:2 + n_ex]
        first_out = 2 + n_ex + n_after
        o = refs[first_out:first_out + n_out]
        acc = refs[first_out + n_out:]
        k = pl.program_id(2)

        def finish(val, cols):
            res = (val,) if epi is None else epi(val, *[e[:, cols] for e in ex])
            for r, o_ref in zip(res, o):
                o_ref[:, cols] = r.astype(o_ref.dtype)

        if nk > 1:
            @pl.when(k == 0)
            def _():
                acc[0][...] = jnp.zeros_like(acc[0])

        av = a_ref[...].astype(BF16)
        for q in range(n_split):
            cols = slice(q * ns, (q + 1) * ns)
            bq = (b_ref[cols, :] if tb else b_ref[:, cols]).astype(BF16)
            part = lax.dot_general(av, bq, dn, preferred_element_type=F32)
            if nk == 1:
                finish(part, cols)
            else:
                acc[0][:, cols] += part

        if nk > 1:
            @pl.when(k == nk - 1)
            def _():
                finish(acc[0][...], slice(0, tn))

    return pl.pallas_call(
        body, name=name, grid=grid,
        in_specs=[a_spec, b_spec] + [sp(tm, tn) for _, sp in extras] + [ANY] * n_after,
        out_specs=out_specs, out_shape=outs,
        scratch_shapes=([pltpu.VMEM((tm, tn), F32)] if nk > 1 else []),
        compiler_params=_cp(("parallel", "parallel", "arbitrary"), VMEM_MB),
    )(a, b, *[arr for arr, _ in extras], *after)


def _tile_spec():
    return lambda tm, tn: pl.BlockSpec((tm, tn), lambda j, i, k: (i, j))


def _cast_into_slot(w, slot, after, *, name):
    R, C = w.shape
    tr = _pick_rows(R, 16)

    def body(s_ref, w_ref, _after_ref, o_ref):
        o_ref[...] = w_ref[...].astype(BF16)

    return pl.pallas_call(
        body, name=name,
        grid_spec=pltpu.PrefetchScalarGridSpec(
            num_scalar_prefetch=1, grid=(R // tr,),
            in_specs=[pl.BlockSpec((tr, C), lambda i, s: (i, 0)), ANY],
            out_specs=pl.BlockSpec((None, tr, C), lambda i, s: (s[0], i, 0))),
        out_shape=jax.ShapeDtypeStruct((NCHIP, R, C), BF16),
        compiler_params=_cp(("parallel",), VMEM_MB),
    )(slot, w, after)


def _rms_fwd(x, g, *, name, after=()):
    R, C = x.shape
    tr = _pick(R, TR_EW, 16)
    n_after = len(after)

    def body(x_ref, g_ref, *rest):
        o_ref = rest[n_after]
        xv = x_ref[...]
        r = lax.rsqrt(jnp.mean(xv * xv, axis=-1, keepdims=True) + EPS)
        o_ref[...] = ((xv * r) * g_ref[...]).astype(BF16)

    return pl.pallas_call(
        body, name=name, grid=(R // tr,),
        in_specs=[pl.BlockSpec((tr, C), lambda i: (i, 0)), pl.BlockSpec((1, C), lambda i: (0, 0))] + [ANY] * n_after,
        out_specs=pl.BlockSpec((tr, C), lambda i: (i, 0)),
        out_shape=jax.ShapeDtypeStruct((R, C), BF16),
        compiler_params=_cp(("parallel",), VMEM_MB),
    )(x, g, *after)


def _rms_bwd(dh, x, g, dres, *, name, want_dx=True, want_bf=True, after=()):
    R, C = x.shape
    tr = _pick(R, TR_EW, 16)
    has_res = dres is not None
    row = pl.BlockSpec((tr, C), lambda i: (i, 0))
    vec = pl.BlockSpec((1, C), lambda i: (0, 0))

    def body(*refs):
        dh_ref, x_ref, g_ref = refs[:3]
        pos = 3
        dres_ref = None
        if has_res:
            dres_ref = refs[pos]
            pos += 1
        outs = refs[pos + len(after):]
        i = pl.program_id(0)
        xv = x_ref[...]
        r = lax.rsqrt(jnp.mean(xv * xv, axis=-1, keepdims=True) + EPS)
        xh = xv * r
        dhv = dh_ref[...]
        dg_ref = outs[-1]
        dgp = jnp.sum(dhv * xh, axis=0, keepdims=True)

        @pl.when(i == 0)
        def _():
            dg_ref[...] = dgp

        @pl.when(i > 0)
        def _():
            dg_ref[...] += dgp

        if want_dx:
            t = dhv * g_ref[...]
            dx = r * (t - xh * jnp.mean(t * xh, axis=-1, keepdims=True))
            if has_res:
                dx = dx + dres_ref[...]
            outs[0][...] = dx
            if want_bf:
                outs[1][...] = dx.astype(BF16)

    in_specs = [row, row, vec] + ([row] if has_res else []) + [ANY] * len(after)
    out_specs, out_shape = [], []
    if want_dx:
        out_specs.append(row)
        out_shape.append(jax.ShapeDtypeStruct((R, C), F32))
        if want_bf:
            out_specs.append(row)
            out_shape.append(jax.ShapeDtypeStruct((R, C), BF16))
    out_specs.append(vec)
    out_shape.append(jax.ShapeDtypeStruct((1, C), F32))
    args = [dh, x, g] + ([dres] if has_res else []) + list(after)
    return pl.pallas_call(
        body, name=name, grid=(R // tr,), in_specs=in_specs, out_specs=out_specs, out_shape=out_shape,
        compiler_params=_cp(("arbitrary",), VMEM_MB),
    )(*args)


def _loss_bwd(x3, g, tgt, *, name):
    R, C = x3.shape
    tr = _pick(R, TR_EW, 16)
    n = R // tr
    row = pl.BlockSpec((tr, C), lambda i: (i, 0))
    vec = pl.BlockSpec((1, C), lambda i: (0, 0))

    def body(x_ref, g_ref, t_ref, dx_ref, dxb_ref, dg_ref, loss_ref, acc_ref):
        i = pl.program_id(0)
        xv = x_ref[...]
        gv = g_ref[...]
        r = lax.rsqrt(jnp.mean(xv * xv, axis=-1, keepdims=True) + EPS)
        xh = xv * r
        e = xh * gv - t_ref[...]
        dy = e * (1.0 / C)
        sq = jnp.sum(e * e, axis=0, keepdims=True)
        dgp = jnp.sum(dy * xh, axis=0, keepdims=True)

        @pl.when(i == 0)
        def _():
            acc_ref[...] = sq
            dg_ref[...] = dgp

        @pl.when(i > 0)
        def _():
            acc_ref[...] += sq
            dg_ref[...] += dgp

        t = dy * gv
        dx = r * (t - xh * jnp.mean(t * xh, axis=-1, keepdims=True))
        dx_ref[...] = dx
        dxb_ref[...] = dx.astype(BF16)

        @pl.when(i == n - 1)
        def _():
            loss_ref[...] = jnp.sum(acc_ref[...], axis=-1, keepdims=True) * (0.5 / C)

    return pl.pallas_call(
        body, name=name, grid=(n,),
        in_specs=[row, vec, row],
        out_specs=[row, row, vec, pl.BlockSpec((1, 1), lambda i: (0, 0))],
        out_shape=[jax.ShapeDtypeStruct((R, C), F32), jax.ShapeDtypeStruct((R, C), BF16),
                   jax.ShapeDtypeStruct((1, C), F32), jax.ShapeDtypeStruct((1, 1), F32)],
        scratch_shapes=[pltpu.VMEM((1, C), F32)],
        compiler_params=_cp(("arbitrary",), VMEM_MB),
    )(x3, g, tgt)


def _offsets():
    u0 = 0
    v0 = DS
    b0 = 2 * DS
    c0 = b0 + DC
    x0 = c0 + DC
    q0 = x0 + DC
    return u0, v0, b0, c0, x0, q0


def _tri_mask(lower):
    r = lax.broadcasted_iota(jnp.int32, (CHUNK, CHUNK), 0)
    c = lax.broadcasted_iota(jnp.int32, (CHUNK, CHUNK), 1)
    return (r >= c) if lower else (c >= r)


def _layer_norm_stats(vg):
    mu = jnp.mean(vg, axis=-1, keepdims=True)
    vc = vg - mu
    rstd = lax.rsqrt(jnp.mean(vc * vc, axis=-1, keepdims=True) + EPS)
    return vc * rstd, rstd


def _softmax_rows(qh, kh):
    s = lax.dot_general(qh, kh, (((1,), (1,)), ((), ())), preferred_element_type=F32)
    m = jnp.max(s, axis=-1, keepdims=True)
    e = jnp.exp(s - m)
    return e / jnp.sum(e, axis=-1, keepdims=True)


def _mix_fwd(proj, kv, w_s, bs_t, ln_g, ln_b, conv_w, g_head, *, name):
    assert DS == DC
    tr = _pick(S, TR_MIX, CHUNK)
    n = S // tr
    nck = tr // CHUNK
    u0, v0, b0, c0, x0, q0 = _offsets()
    hb = tr // HALO

    def body(p_ref, cprev_ref, xprev_ref, kv_ref, ws_ref, bst_ref, lng_ref, lnb_ref, cw_ref, gh_ref,
             heads_ref, hn_ref, ycv_ref, buf_ref):
        i = pl.program_id(0)

        def emit(col, val):
            rs = lax.rsqrt(jnp.mean(val * val, axis=-1, keepdims=True) + EPS)
            heads_ref[:, col:col + HD] = val
            hn_ref[:, col:col + HD] = ((val * rs) * gh_ref[:, col:col + HD]).astype(BF16)

        vhat, _ = _layer_norm_stats(_gelu(p_ref[:, v0:v0 + DS]))
        vnb = (vhat * lng_ref[...] + lnb_ref[...]).astype(BF16)
        low = _tri_mask(True)
        for h in range(NSH):
            wt = jnp.where(low, ws_ref[h], 0.0).astype(BF16)
            bcol = bst_ref[:, h:h + 1]
            parts = []
            for c in range(nck):
                blk = vnb[c * CHUNK:(c + 1) * CHUNK, h * HD:(h + 1) * HD]
                parts.append(jnp.dot(wt, blk, preferred_element_type=F32) + bcol)
            mixed = parts[0] if nck == 1 else jnp.concatenate(parts, axis=0)
            emit(h * HD, _gelu(p_ref[:, u0 + h * HD:u0 + (h + 1) * HD]) * mixed)

        xc = p_ref[:, c0:c0 + DC] * p_ref[:, x0:x0 + DC]
        prev = cprev_ref[...] * xprev_ref[...]
        buf_ref[0:HALO, :] = jnp.where(i > 0, prev, 0.0)
        buf_ref[HALO:HALO + tr, :] = xc
        y = (cw_ref[2:3, :] * xc + cw_ref[1:2, :] * buf_ref[HALO - 1:HALO - 1 + tr, :]
             + cw_ref[0:1, :] * buf_ref[HALO - 2:HALO - 2 + tr, :])
        ycv_ref[...] = y
        cout = p_ref[:, b0:b0 + DC] * y
        for h in range(NCH):
            emit(DS + h * HD, cout[:, h * HD:(h + 1) * HD])

        for h in range(NMH):
            qh = (p_ref[:, q0 + h * HD:q0 + (h + 1) * HD] * SCALE).astype(BF16)
            kh = kv_ref[:, h * HD:(h + 1) * HD].astype(BF16)
            vh = kv_ref[:, DM + h * HD:DM + (h + 1) * HD].astype(BF16)
            p = _softmax_rows(qh, kh)
            emit(DS + DC + h * HD, jnp.dot(p.astype(BF16), vh, preferred_element_type=F32))

    full = lambda shape: pl.BlockSpec(shape, lambda i: (0,) * len(shape))
    halo_c = pl.BlockSpec((HALO, DC), lambda i: (jnp.maximum(i * hb - 1, 0), c0 // DC))
    halo_x = pl.BlockSpec((HALO, DC), lambda i: (jnp.maximum(i * hb - 1, 0), x0 // DC))
    return pl.pallas_call(
        body, name=name, grid=(n,),
        in_specs=[pl.BlockSpec((tr, DIN), lambda i: (i, 0)), halo_c, halo_x,
                  full((NMEM, 2 * DM)), full((NSH, CHUNK, CHUNK)), full((CHUNK, NSH)),
                  full((1, DS)), full((1, DS)), full((3, DC)), full((1, D))],
        out_specs=[pl.BlockSpec((tr, D), lambda i: (i, 0)), pl.BlockSpec((tr, D), lambda i: (i, 0)),
                   pl.BlockSpec((tr, DC), lambda i: (i, 0))],
        out_shape=[jax.ShapeDtypeStruct((S, D), F32), jax.ShapeDtypeStruct((S, D), BF16),
                   jax.ShapeDtypeStruct((S, DC), F32)],
        scratch_shapes=[pltpu.VMEM((tr + HALO, DC), F32)],
        compiler_params=_cp(("parallel",), VMEM_MB),
    )(proj, proj, proj, kv, w_s, bs_t, ln_g, ln_b, conv_w, g_head)


def _mix_bwd(dhn, heads, proj, ycv, kv, w_s, bs_t, ln_g, ln_b, conv_w, g_head, after, *, name):
    assert DS == DC
    tr = _pick(S, TR_MIX, CHUNK)
    n = S // tr
    nck = tr // CHUNK
    u0, v0, b0, c0, x0, q0 = _offsets()
    hb = tr // HALO
    last_hb = S // HALO - 1

    def body(dhn_ref, heads_ref, p_ref, ycv_ref, dhn_nx_ref, heads_nx_ref, b_nx_ref, kv_ref, ws_ref, bst_ref,
             lng_ref, lnb_ref, cw_ref, gh_ref, _after_ref,
             dp_ref, dkv_ref, dws_ref, dbs_ref, dlng_ref, dlnb_ref, dcw_ref, dgh_ref, buf_ref, dvn_ref):
        i = pl.program_id(0)

        @pl.when(i == 0)
        def _():
            dkv_ref[...] = jnp.zeros_like(dkv_ref)
            dws_ref[...] = jnp.zeros_like(dws_ref)
            dbs_ref[...] = jnp.zeros_like(dbs_ref)
            dlng_ref[...] = jnp.zeros_like(dlng_ref)
            dlnb_ref[...] = jnp.zeros_like(dlnb_ref)
            dcw_ref[...] = jnp.zeros_like(dcw_ref)
            dgh_ref[...] = jnp.zeros_like(dgh_ref)

        def head_bwd(a, dn, gh):
            rs = lax.rsqrt(jnp.mean(a * a, axis=-1, keepdims=True) + EPS)
            ah = a * rs
            t = dn * gh
            return rs * (t - ah * jnp.mean(t * ah, axis=-1, keepdims=True)), jnp.sum(dn * ah, axis=0, keepdims=True)

        def head_grad(col):
            da, dg = head_bwd(heads_ref[:, col:col + HD], dhn_ref[:, col:col + HD], gh_ref[:, col:col + HD])
            dgh_ref[:, col:col + HD] += dg
            return da

        vg, dvg_dv = _gelu_with_grad(p_ref[:, v0:v0 + DS])
        vhat, rstd = _layer_norm_stats(vg)
        vnb = (vhat * lng_ref[...] + lnb_ref[...]).astype(BF16)
        low = _tri_mask(True)
        ones = jnp.ones((HALO, HD), BF16)
        for h in range(NSH):
            w_h = ws_ref[h]
            wt = jnp.where(low, w_h, 0.0).astype(BF16)
            bcol = bst_ref[:, h:h + 1]
            da = head_grad(h * HD)
            ug, dug_du = _gelu_with_grad(p_ref[:, u0 + h * HD:u0 + (h + 1) * HD])
            dws = jnp.zeros((CHUNK, CHUNK), F32)
            dbs = jnp.zeros((HALO, CHUNK), F32)
            mixed_parts = []
            for c in range(nck):
                rows = slice(c * CHUNK, (c + 1) * CHUNK)
                blk = vnb[rows, h * HD:(h + 1) * HD]
                mixed_parts.append(jnp.dot(wt, blk, preferred_element_type=F32) + bcol)
                dmb = (da[rows] * ug[rows]).astype(BF16)
                dws = dws + lax.dot_general(dmb, blk, (((1,), (1,)), ((), ())), preferred_element_type=F32)
                dbs = dbs + lax.dot_general(ones, dmb, (((1,), (1,)), ((), ())), preferred_element_type=F32)
                dvn_ref[c * CHUNK:(c + 1) * CHUNK, h * HD:(h + 1) * HD] = lax.dot_general(
                    wt, dmb, (((0,), (0,)), ((), ())), preferred_element_type=F32)
            mixed = mixed_parts[0] if nck == 1 else jnp.concatenate(mixed_parts, axis=0)
            dp_ref[:, u0 + h * HD:u0 + (h + 1) * HD] = ((da * mixed) * dug_du).astype(BF16)
            dws_ref[h] += jnp.where(low, dws, 0.0)
            dbs_ref[h] += dbs
        dvn = dvn_ref[...]
        dlng_ref[...] += jnp.sum(dvn * vhat, axis=0, keepdims=True)
        dlnb_ref[...] += jnp.sum(dvn, axis=0, keepdims=True)
        dvh = dvn * lng_ref[...]
        dvg = rstd * (dvh - jnp.mean(dvh, axis=-1, keepdims=True)
                      - vhat * jnp.mean(dvh * vhat, axis=-1, keepdims=True))
        dp_ref[:, v0:v0 + DS] = (dvg * dvg_dv).astype(BF16)

        dc = jnp.concatenate([head_grad(DS + h * HD) for h in range(NCH)], axis=1)
        dc_nx = jnp.concatenate(
            [head_bwd(heads_nx_ref[:, h * HD:(h + 1) * HD], dhn_nx_ref[:, h * HD:(h + 1) * HD],
                      gh_ref[:, DS + h * HD:DS + (h + 1) * HD])[0] for h in range(NCH)], axis=1)
        bg = p_ref[:, b0:b0 + DC]
        cg = p_ref[:, c0:c0 + DC]
        xin = p_ref[:, x0:x0 + DC]
        dp_ref[:, b0:b0 + DC] = (dc * ycv_ref[...]).astype(BF16)
        dyv = dc * bg
        buf_ref[0:tr, :] = dyv
        buf_ref[tr:tr + HALO, :] = jnp.where(i < n - 1, dc_nx * b_nx_ref[...], 0.0)
        sh1 = buf_ref[1:1 + tr, :]
        sh0 = buf_ref[2:2 + tr, :]
        dxc = cw_ref[2:3, :] * dyv + cw_ref[1:2, :] * sh1 + cw_ref[0:1, :] * sh0
        xc = cg * xin
        dp_ref[:, c0:c0 + DC] = (dxc * xin).astype(BF16)
        dp_ref[:, x0:x0 + DC] = (dxc * cg).astype(BF16)
        dcw_ref[0:1, :] += jnp.sum(sh0 * xc, axis=0, keepdims=True)
        dcw_ref[1:2, :] += jnp.sum(sh1 * xc, axis=0, keepdims=True)
        dcw_ref[2:3, :] += jnp.sum(dyv * xc, axis=0, keepdims=True)

        for h in range(NMH):
            do = head_grad(DS + DC + h * HD).astype(BF16)
            qh = (p_ref[:, q0 + h * HD:q0 + (h + 1) * HD] * SCALE).astype(BF16)
            kh = kv_ref[:, h * HD:(h + 1) * HD].astype(BF16)
            vh = kv_ref[:, DM + h * HD:DM + (h + 1) * HD].astype(BF16)
            p = _softmax_rows(qh, kh)
            dpr = lax.dot_general(do, vh, (((1,), (1,)), ((), ())), preferred_element_type=F32)
            ds = (p * (dpr - jnp.sum(dpr * p, axis=-1, keepdims=True))).astype(BF16)
            dp_ref[:, q0 + h * HD:q0 + (h + 1) * HD] = (
                jnp.dot(ds, kh, preferred_element_type=F32) * SCALE).astype(BF16)
            dkv_ref[:, h * HD:(h + 1) * HD] += lax.dot_general(
                ds, qh, (((0,), (0,)), ((), ())), preferred_element_type=F32)
            dkv_ref[:, DM + h * HD:DM + (h + 1) * HD] += lax.dot_general(
                p.astype(BF16), do, (((0,), (0,)), ((), ())), preferred_element_type=F32)

    full = lambda shape: pl.BlockSpec(shape, lambda i: (0,) * len(shape))
    row = lambda c: pl.BlockSpec((tr, c), lambda i: (i, 0))
    nxt = lambda col: pl.BlockSpec((HALO, DC), lambda i: (jnp.minimum((i + 1) * hb, last_hb), col))
    return pl.pallas_call(
        body, name=name, grid=(n,),
        in_specs=[row(D), row(D), row(DIN), row(DC), nxt(DS // DC), nxt(DS // DC), nxt(b0 // DC),
                  full((NMEM, 2 * DM)), full((NSH, CHUNK, CHUNK)), full((CHUNK, NSH)),
                  full((1, DS)), full((1, DS)), full((3, DC)), full((1, D)), ANY],
        out_specs=[row(DIN), full((NMEM, 2 * DM)), full((NSH, CHUNK, CHUNK)), full((NSH, HALO, CHUNK)),
                   full((1, DS)), full((1, DS)), full((HALO, DC)), full((1, D))],
        out_shape=[jax.ShapeDtypeStruct((S, DIN), BF16), jax.ShapeDtypeStruct((NMEM, 2 * DM), F32),
                   jax.ShapeDtypeStruct((NSH, CHUNK, CHUNK), F32), jax.ShapeDtypeStruct((NSH, HALO, CHUNK), F32),
                   jax.ShapeDtypeStruct((1, DS), F32), jax.ShapeDtypeStruct((1, DS), F32),
                   jax.ShapeDtypeStruct((HALO, DC), F32), jax.ShapeDtypeStruct((1, D), F32)],
        scratch_shapes=[pltpu.VMEM((tr + HALO, DC), F32), pltpu.VMEM((tr, DS), F32)],
        compiler_params=_cp(("arbitrary",), VMEM_MB),
    )(dhn, heads, proj, ycv, dhn, heads, proj, kv, w_s, bs_t, ln_g, ln_b, conv_w, g_head, after)


def _place():
    x, y, c = lax.axis_index("x"), lax.axis_index("y"), lax.axis_index("c")
    chips = [(1 - x, y), (x, 1 - y), (1 - x, 1 - y)]
    return x, y, c, chips


ANY = pl.BlockSpec(memory_space=pl.ANY)


HBM = pl.BlockSpec(memory_space=pltpu.HBM)
SEM = pl.BlockSpec(memory_space=pltpu.SEMAPHORE)
EFFECT = pltpu.SideEffectType.DATAFLOW_SIDE_EFFECTING
N_PEER_CHIPS = 3
N_NEIGHBOUR_CHIPS = 2
CONV_PAD = (32, 256)


def _in_hbm(a):
    return pltpu.with_memory_space_constraint(a, pltpu.HBM)


def _allgather_start(bufs, after, *, name):
    nw = len(bufs)

    def body(*refs):
        ins, send, recv = refs[:nw], refs[nw + 1:2 * nw + 1], refs[2 * nw + 1:3 * nw + 1]
        token = refs[4 * nw + 1]
        x, y, c, chips = _place()
        s = 2 * x + y
        for w in range(nw):
            hr = bufs[w].shape[1] // 2
            rows = ins[w].at[s, pl.ds(c * hr, hr)]
            for cx, cy in chips[:N_NEIGHBOUR_CHIPS]:
                pltpu.make_async_remote_copy(src_ref=rows, dst_ref=rows, send_sem=send[w], recv_sem=recv[w],
                                             device_id=(cx, cy, c), device_id_type=MESH).start()
        token[...] = jnp.zeros_like(token)

    res = pl.pallas_call(
        body, name=name,
        in_specs=[HBM] * nw + [ANY],
        out_specs=[SEM] * (2 * nw) + [HBM] * nw + [pl.BlockSpec(memory_space=pltpu.VMEM)],
        out_shape=[pltpu.SemaphoreType.DMA(())] * (2 * nw) + [pltpu.HBM(a.shape, a.dtype) for a in bufs]
        + [jax.ShapeDtypeStruct((8, 128), F32)],
        input_output_aliases={w: 2 * nw + w for w in range(nw)},
        compiler_params=pltpu.CompilerParams(has_side_effects=EFFECT),
    )(*[_in_hbm(a) for a in bufs], after)
    return res[:nw], res[nw:2 * nw], res[2 * nw:3 * nw], res[3 * nw]


def _handshake(peers):
    barrier = pltpu.get_barrier_semaphore()
    for peer in peers:
        pl.semaphore_signal(barrier, inc=1, device_id=peer, device_id_type=MESH)
    pl.semaphore_wait(barrier, len(peers))


def _scatter_start(parts, bufs, collective_id, *, name):
    nw = len(parts)

    def body(*refs):
        src, dst = refs[:nw], refs[nw:2 * nw]
        send, recv = refs[2 * nw:3 * nw], refs[3 * nw:4 * nw]
        token = refs[6 * nw]
        x, y, c, chips = _place()
        s = 2 * x + y
        _handshake([(cx, cy, c) for cx, cy in chips])
        for w in range(nw):
            for cx, cy in chips:
                pltpu.make_async_remote_copy(src_ref=src[w].at[2 * cx + cy], dst_ref=dst[w].at[s], send_sem=send[w],
                                             recv_sem=recv[w], device_id=(cx, cy, c), device_id_type=MESH).start()
        token[...] = jnp.zeros_like(token)

    res = pl.pallas_call(
        body, name=name,
        in_specs=[HBM] * (2 * nw),
        out_specs=[SEM] * (2 * nw) + [HBM] * (2 * nw) + [pl.BlockSpec(memory_space=pltpu.VMEM)],
        out_shape=[pltpu.SemaphoreType.DMA(())] * (2 * nw) + [pltpu.HBM(a.shape, a.dtype) for a in parts + bufs]
        + [jax.ShapeDtypeStruct((8, 128), F32)],
        input_output_aliases={k: 2 * nw + k for k in range(2 * nw)},
        compiler_params=pltpu.CompilerParams(has_side_effects=EFFECT, collective_id=collective_id),
    )(*[_in_hbm(a) for a in parts + bufs])
    return res[:nw], res[nw:2 * nw], res[2 * nw:3 * nw], res[3 * nw:4 * nw], res[4 * nw]


def _sibling_start(srcs, whole, collective_id, *, name):
    nw = len(srcs)
    lands = [lax.empty((a.shape[0], a.shape[1] if whole else a.shape[1] // 2, a.shape[2]), a.dtype) for a in srcs]

    def body(*refs):
        src, land = refs[:nw], refs[nw:2 * nw]
        send, recv = refs[2 * nw:3 * nw], refs[3 * nw:4 * nw]
        token = refs[6 * nw]
        x, y, c, _ = _place()
        _handshake([(x, y, 1 - c)])
        for w in range(nw):
            hr = srcs[w].shape[1] // 2
            rows = src[w] if whole else src[w].at[:, pl.ds((1 - c) * hr, hr)]
            pltpu.make_async_remote_copy(src_ref=rows, dst_ref=land[w], send_sem=send[w], recv_sem=recv[w],
                                         device_id=(x, y, 1 - c), device_id_type=MESH).start()
        token[...] = jnp.zeros_like(token)

    res = pl.pallas_call(
        body, name=name,
        in_specs=[HBM] * (2 * nw),
        out_specs=[SEM] * (2 * nw) + [HBM] * (2 * nw) + [pl.BlockSpec(memory_space=pltpu.VMEM)],
        out_shape=[pltpu.SemaphoreType.DMA(())] * (2 * nw) + [pltpu.HBM(a.shape, a.dtype) for a in srcs + lands]
        + [jax.ShapeDtypeStruct((8, 128), F32)],
        input_output_aliases={k: 2 * nw + k for k in range(2 * nw)},
        compiler_params=pltpu.CompilerParams(has_side_effects=EFFECT, collective_id=collective_id),
    )(*[_in_hbm(a) for a in srcs + lands])
    return res[:nw], res[nw:2 * nw], res[2 * nw:3 * nw], res[3 * nw:4 * nw], res[4 * nw]


def _transfer_wait(sends, recvs, thru, sizes, after, *, name):
    n = len(sends)
    flat = [a for group in thru for a in group]

    def body(*refs):
        bufs = refs[:len(flat)]
        send = refs[len(flat):len(flat) + n]
        recv = refs[len(flat) + n:len(flat) + 2 * n]
        token = refs[2 * len(flat) + 2 * n + 1]
        token[...] = jnp.zeros_like(token)
        x, y, c, _ = _place()
        pos = 0
        for k in range(n):
            slots, rows = sizes[k]
            region = bufs[pos].at[pl.ds(0, slots), pl.ds(0, rows)]
            pos += len(thru[k])
            cp = pltpu.make_async_remote_copy(src_ref=region, dst_ref=region, send_sem=send[k], recv_sem=recv[k],
                                              device_id=(x, y, 1 - c), device_id_type=MESH)
            cp.wait_send()
            cp.wait_recv()

    res = pl.pallas_call(
        body, name=name,
        in_specs=[HBM] * len(flat) + [SEM] * (2 * n) + [pl.BlockSpec(memory_space=pl.ANY)],
        out_specs=[HBM] * len(flat) + [pl.BlockSpec(memory_space=pltpu.VMEM)],
        out_shape=[pltpu.HBM(a.shape, a.dtype) for a in flat] + [jax.ShapeDtypeStruct((8, 128), F32)],
        input_output_aliases={k: k for k in range(len(flat))},
        compiler_params=pltpu.CompilerParams(has_side_effects=EFFECT),
    )(*flat, *sends, *recvs, after)
    out, pos = [], 0
    for group in thru:
        out.append(res[pos:pos + len(group)])
        pos += len(group)
    return out, res[len(flat)]


def _forward_gathered(bufs, after, *, name):
    nw = len(bufs)

    def body(*refs):
        outs = refs[nw + 1:2 * nw + 1]
        d_send, d_recv, i_send, i_recv = refs[2 * nw + 1:]
        x, y, c, chips = _place()
        me, sibling = (x, y, c), (x, y, 1 - c)
        slots = [2 * cx + cy for cx, cy in chips]

        def rows(w, j, start, n):
            return outs[w].at[slots[j], pl.ds(start, n)]

        def d2d(w, j, which, to):
            hr = bufs[w].shape[1] // 2
            r = rows(w, j, which * hr, hr)
            return pltpu.make_async_remote_copy(
                src_ref=r, dst_ref=r, send_sem=d_send.at[N_PEER_CHIPS * w + j],
                recv_sem=d_recv.at[N_PEER_CHIPS * w + j], device_id=to, device_id_type=MESH)

        def ici(w, j, slot_j, to):
            q = bufs[w].shape[1] // 4
            r = rows(w, slot_j, c * 2 * q + j * q, q)
            return pltpu.make_async_remote_copy(
                src_ref=r, dst_ref=r, send_sem=i_send.at[N_NEIGHBOUR_CHIPS * w + j],
                recv_sem=i_recv.at[N_NEIGHBOUR_CHIPS * w + j], device_id=to, device_id_type=MESH)

        started = []
        for w in range(nw):
            started += [ici(w, 0, 0, (*chips[1], c)), ici(w, 1, 1, (*chips[0], c))]
            started += [d2d(w, j, c, sibling) for j in range(N_NEIGHBOUR_CHIPS)]
        for cp in started:
            cp.start()
        diag = N_PEER_CHIPS - 1
        for w in range(nw):
            for j in range(N_NEIGHBOUR_CHIPS):
                ici(w, j, diag, me).wait_recv()
            cp = d2d(w, diag, c, sibling)
            cp.start()
            started.append(cp)
        for w in range(nw):
            for j in range(N_PEER_CHIPS):
                d2d(w, j, 1 - c, me).wait_recv()
        for cp in started:
            cp.wait_send()

    return pl.pallas_call(
        body, name=name,
        in_specs=[ANY] * (nw + 1), out_specs=[ANY] * nw,
        out_shape=[jax.ShapeDtypeStruct(a.shape, a.dtype) for a in bufs],
        input_output_aliases={w: w for w in range(nw)},
        scratch_shapes=[pltpu.SemaphoreType.DMA((N_PEER_CHIPS * nw,)), pltpu.SemaphoreType.DMA((N_PEER_CHIPS * nw,)),
                        pltpu.SemaphoreType.DMA((N_NEIGHBOUR_CHIPS * nw,)),
                        pltpu.SemaphoreType.DMA((N_NEIGHBOUR_CHIPS * nw,))],
    )(*bufs, after)


def _allreduce_small(p, after, *, name):
    R = p.shape[0]
    hr = R // 2

    def body(p_ref, _after_ref, out_ref, sib_ref, sum_ref, gat_ref, tot_ref, send, recv):
        x, y, c, chips = _place()
        s = 2 * x + y
        sibling = (x, y, 1 - c)
        rows = pl.ds(pl.multiple_of(c * hr, 8), hr)
        swap = pltpu.make_async_remote_copy(src_ref=p_ref, dst_ref=sib_ref, send_sem=send.at[0], recv_sem=recv.at[0],
                                            device_id=sibling, device_id_type=MESH)
        swap.start()
        swap.wait()
        sum_ref[...] = p_ref[...] + sib_ref[...]
        gat_ref[s] = sum_ref[rows, :]
        cps = [pltpu.make_async_remote_copy(src_ref=sum_ref.at[rows], dst_ref=gat_ref.at[s], send_sem=send.at[1 + j],
                                            recv_sem=recv.at[1 + j], device_id=(cx, cy, c), device_id_type=MESH)
               for j, (cx, cy) in enumerate(chips)]
        for cp in cps:
            cp.start()
        for cp in cps:
            cp.wait()
        tot_ref[...] = ((gat_ref[0] + gat_ref[1]) + gat_ref[2]) + gat_ref[3]
        out_ref[rows, :] = tot_ref[...]
        share = pltpu.make_async_remote_copy(src_ref=tot_ref, dst_ref=out_ref.at[rows], send_sem=send.at[4],
                                             recv_sem=recv.at[4], device_id=sibling, device_id_type=MESH)
        share.start()
        share.wait_send()
        other = out_ref.at[pl.ds(pl.multiple_of((1 - c) * hr, 8), hr)]
        pltpu.make_async_remote_copy(src_ref=other, dst_ref=other, send_sem=send.at[4], recv_sem=recv.at[4],
                                     device_id=(x, y, c), device_id_type=MESH).wait_recv()

    vmem = pl.BlockSpec(memory_space=pltpu.VMEM)
    return pl.pallas_call(
        body, name=name, in_specs=[vmem, ANY], out_specs=vmem,
        out_shape=jax.ShapeDtypeStruct((R, 128), F32),
        scratch_shapes=[pltpu.VMEM((R, 128), F32), pltpu.VMEM((R, 128), F32), pltpu.VMEM((NCHIP, hr, 128), F32),
                        pltpu.VMEM((hr, 128), F32), pltpu.SemaphoreType.DMA((5,)), pltpu.SemaphoreType.DMA((5,))],
    )(p, after)


def _select_half_bf16(g, half, add, slot, *, name):
    _, R, C = g.shape
    hr = R // 2
    tr = _pick_rows(hr, 16)
    nb = hr // tr
    sel = jnp.concatenate([jnp.reshape(half, (1,)).astype(jnp.int32), slot])

    def body(s_ref, g_ref, a_ref, o_ref, own_ref):
        val = (g_ref[...].astype(F32) + a_ref[...].astype(F32)).astype(BF16)
        o_ref[...] = val

        @pl.when(pl.program_id(1) == s_ref[1])
        def _():
            own_ref[...] = val

    g_spec = pl.BlockSpec((None, tr, C), lambda i, j, s: (j, s[0] * nb + i, 0))
    o_spec = pl.BlockSpec((None, tr, C), lambda i, j, s: (j, i, 0))
    own_spec = pl.BlockSpec((None, tr, C), lambda i, j, s: (s[1], i, 0))
    shape = jax.ShapeDtypeStruct((NCHIP, hr, C), BF16)
    return pl.pallas_call(
        body, name=name,
        grid_spec=pltpu.PrefetchScalarGridSpec(
            num_scalar_prefetch=1, grid=(nb, NCHIP), in_specs=[g_spec, o_spec], out_specs=[o_spec, own_spec]),
        out_shape=[shape, shape],
        compiler_params=_cp(("parallel", "arbitrary"), VMEM_MB),
    )(sel, g, add)


def _adamw_math(w, g, m, v):
    m = ADAM_B1 * m + (1.0 - ADAM_B1) * g
    v = ADAM_B2 * v + (1.0 - ADAM_B2) * (g * g)
    m_hat = m / (1.0 - ADAM_B1 ** ADAM_STEP)
    v_hat = v / (1.0 - ADAM_B2 ** ADAM_STEP)
    delta = -ADAM_LR * (m_hat / (jnp.sqrt(v_hat) + ADAM_EPS) + ADAM_WD * w)
    return delta, m, v


def _adamw(w, g_mine, g_sib, m, v, core, *, name):
    R, C = w.shape
    hr = R // 2
    tr = _pick_rows(hr, 16)
    nb = hr // tr
    row = pl.BlockSpec((tr, C), lambda hh, i, c: (hh * nb + i, 0))
    mine = pl.BlockSpec((NCHIP, tr, C), lambda hh, i, c: (0, jnp.where(hh == c[0], i, 0), 0))
    sibs = pl.BlockSpec((NCHIP, tr, C), lambda hh, i, c: (0, jnp.where(hh == c[0], 0, i), 0))

    def slot_sum(ref):
        acc = ref[0].astype(F32) + ref[1].astype(F32)
        for j in range(2, NCHIP):
            acc = acc + ref[j].astype(F32)
        return acc

    def body(c_ref, w_ref, gm_ref, gs_ref, m_ref, v_ref, go_ref, d_ref, mo_ref, vo_ref):
        gv = jnp.where(pl.program_id(0) == c_ref[0], slot_sum(gm_ref), slot_sum(gs_ref))
        d, mn, vn = _adamw_math(w_ref[...], gv, m_ref[...], v_ref[...])
        go_ref[...] = gv
        d_ref[...] = d
        mo_ref[...] = mn
        vo_ref[...] = vn

    return pl.pallas_call(
        body, name=name,
        grid_spec=pltpu.PrefetchScalarGridSpec(
            num_scalar_prefetch=1, grid=(2, nb),
            in_specs=[row, mine, sibs, row, row], out_specs=[row] * 4),
        out_shape=[jax.ShapeDtypeStruct((R, C), F32)] * 4,
        compiler_params=_cp(("parallel", "parallel"), VMEM_MB),
    )(core, w, g_mine, g_sib, m, v)


def _adamw_small(ws, gs, ms, vs, *, name):
    n = len(ws)

    def body(*refs):
        w_r, g_r, m_r, v_r = refs[:n], refs[n:2 * n], refs[2 * n:3 * n], refs[3 * n:4 * n]
        d_r, mo_r, vo_r = refs[4 * n:5 * n], refs[5 * n:6 * n], refs[6 * n:7 * n]
        for k in range(n):
            d, mn, vn = _adamw_math(w_r[k][...], g_r[k][...], m_r[k][...], v_r[k][...])
            d_r[k][...] = d
            mo_r[k][...] = mn
            vo_r[k][...] = vn

    shapes = [jax.ShapeDtypeStruct(w.shape, F32) for w in ws]
    res = pl.pallas_call(body, name=name, out_shape=shapes * 3)(*ws, *gs, *ms, *vs)
    return res[:n], res[n:2 * n], res[2 * n:]


_PACK_ROWS = 8


def _pack(parts):
    rows = []
    for a in parts:
        flat = a.reshape(-1)
        n = -(-flat.shape[0] // (_PACK_ROWS * 128)) * (_PACK_ROWS * 128)
        rows.append(jnp.pad(flat, (0, n - flat.shape[0])).reshape(-1, 128))
    total = sum(r.shape[0] for r in rows)
    if total % 16:
        rows.append(jnp.zeros((16 - total % 16, 128), F32))
    return jnp.concatenate(rows, axis=0)


def _unpack(p, shapes):
    out, r = [], 0
    for shp in shapes:
        n = math.prod(shp)
        nr = -(-n // (_PACK_ROWS * 128)) * _PACK_ROWS
        out.append(p[r:r + nr].reshape(-1)[:n].reshape(shp))
        r += nr
    return out


def kernel(x, mem, g_mix, w_in, ln_v_g, ln_v_b, w_s, b_s, conv_w, g_mem, w_kv, g_head, w_o, g_ffn, w_ffn1, w_ffn2, g_final, loss_target, m_g_mix, m_w_in, m_ln_v_g, m_ln_v_b, m_w_s, m_b_s, m_conv_w, m_g_mem, m_w_kv, m_g_head, m_w_o, m_g_ffn, m_w_ffn1, m_w_ffn2, m_g_final, v_g_mix, v_w_in, v_ln_v_g, v_ln_v_b, v_w_s, v_b_s, v_conv_w, v_g_mem, v_w_kv, v_g_head, v_w_o, v_g_ffn, v_w_ffn1, v_w_ffn2, v_g_final):
    sds = jax.ShapeDtypeStruct
    xi, yi = lax.axis_index("x"), lax.axis_index("y")
    shard = 2 * xi + yi
    x2d, mem2d, tgt = x[0], mem[0], loss_target[0]
    ws3, bs2 = w_s[0], b_s[0]
    g_final2 = g_final.reshape(1, D)
    dff4 = DFF // NCHIP
    din4 = DIN // NCHIP
    dcv4 = DC // NCHIP

    big = [w_in[0].T, w_kv[0], w_o[0], w_ffn1[0], w_ffn2[0]]
    big_names = ["w_in", "w_kv", "w_o", "w_ffn1", "w_ffn2"]
    slot = jnp.reshape(shard, (1,)).astype(jnp.int32)
    core = jnp.reshape(lax.axis_index("c"), (1,)).astype(jnp.int32)
    conv_pad = jnp.pad(conv_w[0], ((0, CONV_PAD[0] - 3), (0, CONV_PAD[1] - dcv4)))
    conv_slots = lax.dynamic_update_slice(jnp.zeros((NCHIP,) + CONV_PAD, F32), conv_pad[None], (shard, 0, 0))

    def gather_start(bufs, after, nm):
        return _allgather_start(bufs, after, name="ag_start_" + nm)

    def gather_wait(state, idx, after, nm):
        send, recv, bufs, _ = state
        got, token = _transfer_wait([send[k] for k in idx], [recv[k] for k in idx], [[bufs[k]] for k in idx],
                                    [(N_NEIGHBOUR_CHIPS, bufs[k].shape[1] // 2) for k in idx], after, name="ag_wait_" + nm)
        return [g[0] for g in got], token

    cast = lambda k, after: _cast_into_slot(big[k], slot, after, name="cast_" + big_names[k])
    ag_in = gather_start([cast(0, slot), conv_slots], slot, "in")
    bs_t = bs2.T

    h = _rms_fwd(x2d, g_mix, name="rms_mix", after=[ag_in[3]])
    mem_n = _rms_fwd(mem2d, g_mem, name="rms_mem", after=[h])
    kvo_b = [cast(1, mem_n)]
    kvo_b.append(cast(2, kvo_b[0]))
    w1_b = cast(3, kvo_b[1])
    w2_b = cast(4, w1_b)
    got_in, tok = gather_wait(ag_in, [0, 1], w2_b, "in")
    win4, conv4 = _forward_gathered(got_in, tok, name="ag_fwd_in")
    ag_kvo = gather_start(kvo_b, conv4, "kvo")
    w_in_t = win4.reshape(DIN, D)
    conv_full = conv4[:, :3, :dcv4].transpose(1, 0, 2).reshape(3, DC)
    (proj,) = _matmul(h, w_in_t, name="mm_proj", tb=True, M=S, N=DIN, K=D, tn=DIN // 2, outs=[sds((S, DIN), F32)],
                      after=[ag_kvo[3]])
    got_kvo, tok = gather_wait(ag_kvo, [0, 1], proj, "kvo")
    wkv4, wo4 = _forward_gathered(got_kvo, tok, name="ag_fwd_kvo")
    ag_w1 = gather_start([w1_b], wkv4, "ffn1")
    w_kv_full = wkv4.reshape(D, 2 * DM)
    w_o_full = wo4.reshape(D, D)
    (kv,) = _matmul(mem_n, w_kv_full, name="mm_kv", M=NMEM, N=2 * DM, K=D, outs=[sds((NMEM, 2 * DM), F32)],
                    after=[ag_w1[3]])
    heads, hn, ycv = _mix_fwd(proj, kv, ws3, bs_t, ln_v_g, ln_v_b, conv_full, g_head, name="mix_fwd")
    (x2,) = _matmul(hn, w_o_full, name="mm_wo", M=S, N=D, K=D, outs=[sds((S, D), F32)],
                    epi=lambda acc, res: (acc + res,), extras=[(x2d, _tile_spec())])
    h2 = _rms_fwd(x2, g_ffn, name="rms_ffn")
    got_w1, tok = gather_wait(ag_w1, [0], h2, "ffn1")
    (w14,) = _forward_gathered(got_w1, tok, name="ag_fwd_ffn1")
    ag_w2 = gather_start([w2_b], w14, "ffn2")

    def w1_cols(tn, tk):
        nb = dff4 // tn
        return pl.BlockSpec((None, tk, tn), lambda j, i, k: (j // nb, k, j % nb))

    (act,) = _matmul(h2, w14, name="mm_ffn1", M=S, N=DFF, K=D, tn=dff4, b_spec=w1_cols, outs=[sds((S, DFF), BF16)],
                     epi=lambda acc: (jnp.square(jnp.maximum(acc, 0.0)),), after=[ag_w2[3]])
    got_w2, tok = gather_wait(ag_w2, [0], act, "ffn2")
    (w24,) = _forward_gathered(got_w2, tok, name="ag_fwd_ffn2")
    w2_full = w24.reshape(DFF, D)
    (x3,) = _matmul(act, w2_full, name="mm_ffn2", M=S, N=D, K=DFF, tm=2 * TM, outs=[sds((S, D), F32)],
                    epi=lambda acc, res: (acc + res,), extras=[(x2, _tile_spec())])

    ci = lax.axis_index("c")

    def rs_sibling(g4, nm):
        return _sibling_start([g4], False, 1 + big_names.index(nm), name="rs_sib_" + nm)

    def rs_chips(state, after, nm):
        send, recv, g4, land, _ = state
        (((land_, g4_),), _) = _transfer_wait(send, recv, [[land[0], g4[0]]], [(NCHIP, land[0].shape[1])], after,
                                             name="rs_sibwait_" + nm)
        part, buf = _select_half_bf16(g4_, ci, land_, slot, name="rs_add_" + nm)
        return _scatter_start([part], [buf], 1 + 2 * len(big_names) + big_names.index(nm), name="rs_start_" + nm)

    def rs_end(state, after, nm):
        send, recv, parts, bufs, _ = state
        (((buf, _),), _) = _transfer_wait(send, recv, [[bufs[0], parts[0]]], [(N_PEER_CHIPS, bufs[0].shape[1])], after,
                                          name="rs_wait_" + nm)
        return _sibling_start([buf], True, 1 + len(big_names) + big_names.index(nm), name="rs_share_" + nm)

    big_m = [m_w_in[0].T, m_w_kv[0], m_w_o[0], m_w_ffn1[0], m_w_ffn2[0]]
    big_v = [v_w_in[0].T, v_w_kv[0], v_w_o[0], v_w_ffn1[0], v_w_ffn2[0]]
    big_out = {}

    def rs_finish(k, state, after):
        send, recv, mine, land, _ = state
        nm = big_names[k]
        (((land_, mine_),), _) = _transfer_wait(send, recv, [[land[0], mine[0]]], [(NCHIP, land[0].shape[1])], after,
                                               name="rs_sharewait_" + nm)
        big_out[nm] = _adamw(big[k], mine_, land_, big_m[k], big_v[k], core, name="adamw_" + nm)
        return big_out[nm][1]

    dx3, dx3b, dg_final, loss11 = _loss_bwd(x3, g_final2, tgt, name="loss_bwd")
    (dw2,) = _matmul(act, dx3b, name="mm_dw2", ta=True, M=DFF, N=D, K=S, tn=D, outs=[sds((DFF, D), BF16)])
    sib_w2 = rs_sibling(dw2.reshape(NCHIP, dff4, D), "w_ffn2")
    (dfb,) = _matmul(dx3b, w2_full, name="mm_dact", tb=True, M=S, N=DFF, K=D, tn=dff4, outs=[sds((S, DFF), BF16)],
                     epi=lambda acc, a: (acc * (2.0 * jnp.sqrt(a.astype(F32))),), extras=[(act, _tile_spec())],
                     after=[sib_w2[4]])
    rs_w2 = rs_chips(sib_w2, dfb, "w_ffn2")

    def dw1_out(tm, tn):
        nb = dff4 // tn
        return [pl.BlockSpec((None, tm, tn), lambda j, i, k: (j // nb, i, j % nb))]

    (dw1,) = _matmul(h2, dfb, name="mm_dw1", ta=True, M=D, N=DFF, K=S, tn=dff4, outs=[sds((NCHIP, D, dff4), BF16)],
                     out_specs=dw1_out, after=[rs_w2[4]])
    sib_w1 = rs_sibling(dw1, "w_ffn1")

    def w1_rows(tn, tk):
        kb = dff4 // tk
        return pl.BlockSpec((None, tn, tk), lambda j, i, k: (k // kb, j, k % kb))

    (dh2,) = _matmul(dfb, w14, name="mm_dh2", tb=True, M=S, N=D, K=DFF, tm=2 * TM, b_spec=w1_rows,
                     outs=[sds((S, D), F32)], after=[sib_w1[4]])
    rs_w1 = rs_chips(sib_w1, dh2, "w_ffn1")
    dx2, dx2b, dg_ffn = _rms_bwd(dh2, x2, g_ffn, dx3, name="rms_ffn_bwd", after=[rs_w1[4]])
    (dwo,) = _matmul(hn, dx2b, name="mm_dwo", ta=True, M=D, N=D, K=S, outs=[sds((D, D), BF16)])
    sib_wo = rs_sibling(dwo.reshape(NCHIP, D // NCHIP, D), "w_o")
    (dhn,) = _matmul(dx2b, w_o_full, name="mm_dhn", tb=True, M=S, N=D, K=D, outs=[sds((S, D), F32)],
                     after=[sib_wo[4]])
    rs_wo = rs_chips(sib_wo, dhn, "w_o")
    sh_w2 = rs_end(rs_w2, rs_wo[4], "w_ffn2")
    dproj, dkv, dws, dbs8, dlng, dlnb, dcw8, dgh = _mix_bwd(
        dhn, heads, proj, ycv, kv, ws3, bs_t, ln_v_g, ln_v_b, conv_full, g_head, sh_w2[4], name="mix_bwd")
    (dwin_t,) = _matmul(dproj, h, name="mm_dwin", ta=True, M=DIN, N=D, K=S, tm=DIN // 2, outs=[sds((DIN, D), BF16)])
    sib_win = rs_sibling(dwin_t.reshape(NCHIP, din4, D), "w_in")
    (dwkv,) = _matmul(mem_n, dkv, name="mm_dwkv", ta=True, M=D, N=2 * DM, K=NMEM, outs=[sds((D, 2 * DM), BF16)],
                      after=[sib_win[4]])
    sib_wkv = rs_sibling(dwkv.reshape(NCHIP, D // NCHIP, 2 * DM), "w_kv")
    (dh,) = _matmul(dproj, w_in_t, name="mm_dh", M=S, N=D, K=DIN, tk=DIN, outs=[sds((S, D), F32)],
                    after=[sib_wkv[4]])
    rs_win = rs_chips(sib_win, dh, "w_in")
    rs_wkv = rs_chips(sib_wkv, rs_win[4], "w_kv")
    sh_w1 = rs_end(rs_w1, rs_wkv[4], "w_ffn1")
    dx, dg_mix = _rms_bwd(dh, x2d, g_mix, dx2, name="rms_mix_bwd", want_bf=False, after=[sh_w1[4]])
    (dmem_n,) = _matmul(dkv, w_kv_full, name="mm_dmem", tb=True, M=NMEM, N=D, K=2 * DM, outs=[sds((NMEM, D), F32)],
                        after=[dx])
    (dg_mem,) = _rms_bwd(dmem_n, mem2d, g_mem, None, name="rms_mem_bwd", want_dx=False)
    sh_wo = rs_end(rs_wo, dg_mem, "w_o")
    done = rs_finish(4, sh_w2, sh_wo[4])
    done = rs_finish(3, sh_w1, done)
    sh_win = rs_end(rs_win, done, "w_in")
    sh_wkv = rs_end(rs_wkv, sh_win[4], "w_kv")
    done = rs_finish(2, sh_wo, sh_wkv[4])
    done = rs_finish(0, sh_win, done)
    done = rs_finish(1, sh_wkv, done)

    loss = lax.psum(loss11[0, 0], ("x", "y", "c"))

    small_names = ["g_mix", "ln_v_g", "ln_v_b", "w_s", "b_s", "conv_w", "g_mem", "g_head", "g_ffn", "g_final"]
    small_part = [dg_mix, dlng, dlnb, dws, dbs8[:, 0, :], dcw8[:3], dg_mem, dgh, dg_ffn, dg_final]
    small_shapes = [(1, D), (1, DS), (1, DS), (NSH, CHUNK, CHUNK), (NSH, CHUNK), (3, DC), (1, D), (1, D), (1, D), (1, D)]
    total = _allreduce_small(_pack(small_part), done, name="allreduce_small")
    small_g = _unpack(total, small_shapes)
    small_g[5] = lax.dynamic_slice(small_g[5], (0, shard * dcv4), (3, dcv4))
    small_w = [g_mix, ln_v_g, ln_v_b, ws3, bs2, conv_w[0], g_mem, g_head, g_ffn, g_final2]
    small_m = [m_g_mix, m_ln_v_g, m_ln_v_b, m_w_s[0], m_b_s[0], m_conv_w[0], m_g_mem, m_g_head, m_g_ffn,
               m_g_final.reshape(1, D)]
    small_v = [v_g_mix, v_ln_v_g, v_ln_v_b, v_w_s[0], v_b_s[0], v_conv_w[0], v_g_mem, v_g_head, v_g_ffn,
               v_g_final.reshape(1, D)]
    s_delta, s_m, s_v = _adamw_small(small_w, small_g, small_m, small_v, name="adamw_small")
    small_out = {nm: (g, d, mn, vn) for nm, g, d, mn, vn in zip(small_names, small_g, s_delta, s_m, s_v)}

    order = ["g_mix", "w_in", "ln_v_g", "ln_v_b", "w_s", "b_s", "conv_w", "g_mem", "w_kv", "g_head", "w_o",
             "g_ffn", "w_ffn1", "w_ffn2", "g_final"]
    like = dict(g_mix=g_mix, w_in=w_in, ln_v_g=ln_v_g, ln_v_b=ln_v_b, w_s=w_s, b_s=b_s, conv_w=conv_w, g_mem=g_mem,
                w_kv=w_kv, g_head=g_head, w_o=w_o, g_ffn=g_ffn, w_ffn1=w_ffn1, w_ffn2=w_ffn2, g_final=g_final)
    res = {**big_out, **small_out}
    res["w_in"] = [a.T for a in res["w_in"]]
    outs = [loss, dx[None]]
    for k in range(4):
        outs += [res[nm][k].reshape(like[nm].shape) for nm in order]
    return tuple(outs)
```

```python
import math

import jax
import jax.numpy as jnp
from jax import lax
from jax.experimental import pallas as pl
from jax.experimental.pallas import tpu as pltpu

F32 = jnp.float32
BF16 = jnp.bfloat16
MESH = pl.DeviceIdType.MESH

D = 2048
S = 2048
HD = 128
NH = D // HD
NMH = 4
NSH = (NH - NMH) // 2
NCH = NH - NMH - NSH
DS = NSH * HD
DC = NCH * HD
DM = NMH * HD
DIN = 2 * DS + 3 * DC + DM
CHUNK = 128
NMEM = 256
DFF = 4 * D
EPS = 1e-6
NCHIP = 4
SCALE = HD ** -0.5

ADAM_LR = 0.001
ADAM_B1 = 0.9
ADAM_B2 = 0.999
ADAM_EPS = 1e-08
ADAM_WD = 0.01
ADAM_STEP = 10

TR_EW = 256
TR_MIX = 256
TM = 512
TN = 1024
TK = 2048
N_SUB = 512
VMEM_MB = 56
HALO = 8


def _pick(n, target, q=128):
    best = None
    for t in range(q, min(n, target) + 1, q):
        if n % t == 0:
            best = t
    return n if best is None else best


def _pick_rows(n, q):
    below = _pick(n, TR_EW, q)
    if 2 * below >= TR_EW:
        return below
    above = [t for t in range(TR_EW, min(n, 4 * TR_EW) + 1, q) if n % t == 0]
    return above[0] if above else below


def _cp(sem=None, vmem_mb=None, **kw):
    d = dict(kw)
    if sem is not None:
        d["dimension_semantics"] = sem
    if vmem_mb is not None:
        d["vmem_limit_bytes"] = vmem_mb << 20
    return pltpu.CompilerParams(**d)


def _gelu(x):
    z = 0.7978845608028654 * (x + 0.044715 * (x * x * x))
    return 0.5 * x * (1.0 + jnp.tanh(z))


def _gelu_with_grad(x):
    x2 = x * x
    t = jnp.tanh(0.7978845608028654 * (x + 0.044715 * (x2 * x)))
    half = 0.5 * (1.0 + t)
    return x * half, half + 0.5 * x * (1.0 - t * t) * (0.7978845608028654 * (1.0 + 3.0 * 0.044715 * x2))


def _matmul(a, b, *, name, ta=False, tb=False, M, N, K, tm=None, tn=None, tk=None, outs, epi=None,
            extras=(), b_spec=None, out_specs=None, after=(), n_split=None):
    n_after = len(after)
    tm = _pick(M, TM if tm is None else tm, 8)
    tn = _pick(N, TN if tn is None else tn)
    tk = _pick(K, TK if tk is None else tk)
    if n_split is None:
        n_split = tn // N_SUB if tn % N_SUB == 0 else 1
    nk = K // tk
    grid = (N // tn, M // tm, nk)
    a_spec = (pl.BlockSpec((tk, tm), lambda j, i, k: (k, i)) if ta
              else pl.BlockSpec((tm, tk), lambda j, i, k: (i, k)))
    if b_spec is None:
        b_spec = (pl.BlockSpec((tn, tk), lambda j, i, k: (j, k)) if tb
                  else pl.BlockSpec((tk, tn), lambda j, i, k: (k, j)))
    else:
        b_spec = b_spec(tn, tk)
    if out_specs is None:
        out_specs = [pl.BlockSpec((tm, tn), lambda j, i, k: (i, j)) for _ in outs]
    else:
        out_specs = out_specs(tm, tn)
    dn = (((0 if ta else 1,), (1 if tb else 0,)), ((), ()))
    n_ex, n_out = len(extras), len(outs)

    ns = tn // n_split

    def body(*refs):
        a_ref, b_ref = refs[0], refs[1]
        ex = refs[2:2 + n_ex]
        first_out = 2 + n_ex + n_after
        o = refs[first_out:first_out + n_out]
        acc = refs[first_out + n_out:]
        k = pl.program_id(2)

        def finish(val, cols):
            res = (val,) if epi is None else epi(val, *[e[:, cols] for e in ex])
            for r, o_ref in zip(res, o):
                o_ref[:, cols] = r.astype(o_ref.dtype)

        if nk > 1:
            @pl.when(k == 0)
            def _():
                acc[0][...] = jnp.zeros_like(acc[0])

        av = a_ref[...].astype(BF16)
        for q in range(n_split):
            cols = slice(q * ns, (q + 1) * ns)
            bq = (b_ref[cols, :] if tb else b_ref[:, cols]).astype(BF16)
            part = lax.dot_general(av, bq, dn, preferred_element_type=F32)
            if nk == 1:
                finish(part, cols)
            else:
                acc[0][:, cols] += part

        if nk > 1:
            @pl.when(k == nk - 1)
            def _():
                finish(acc[0][...], slice(0, tn))

    return pl.pallas_call(
        body, name=name, grid=grid,
        in_specs=[a_spec, b_spec] + [sp(tm, tn) for _, sp in extras] + [ANY] * n_after,
        out_specs=out_specs, out_shape=outs,
        scratch_shapes=([pltpu.VMEM((tm, tn), F32)] if nk > 1 else []),
        compiler_params=_cp(("parallel", "parallel", "arbitrary"), VMEM_MB),
    )(a, b, *[arr for arr, _ in extras], *after)


def _tile_spec():
    return lambda tm, tn: pl.BlockSpec((tm, tn), lambda j, i, k: (i, j))


def _cast_into_slot(w, slot, after, *, name):
    R, C = w.shape
    tr = _pick_rows(R, 16)

    def body(s_ref, w_ref, _after_ref, o_ref):
        o_ref[...] = w_ref[...].astype(BF16)

    return pl.pallas_call(
        body, name=name,
        grid_spec=pltpu.PrefetchScalarGridSpec(
            num_scalar_prefetch=1, grid=(R // tr,),
            in_specs=[pl.BlockSpec((tr, C), lambda i, s: (i, 0)), ANY],
            out_specs=pl.BlockSpec((None, tr, C), lambda i, s: (s[0], i, 0))),
        out_shape=jax.ShapeDtypeStruct((NCHIP, R, C), BF16),
        compiler_params=_cp(("parallel",), VMEM_MB),
    )(slot, w, after)


def _rms_fwd(x, g, *, name, after=()):
    R, C = x.shape
    tr = _pick(R, TR_EW, 16)
    n_after = len(after)

    def body(x_ref, g_ref, *rest):
        o_ref = rest[n_after]
        xv = x_ref[...]
        r = lax.rsqrt(jnp.mean(xv * xv, axis=-1, keepdims=True) + EPS)
        o_ref[...] = ((xv * r) * g_ref[...]).astype(BF16)

    return pl.pallas_call(
        body, name=name, grid=(R // tr,),
        in_specs=[pl.BlockSpec((tr, C), lambda i: (i, 0)), pl.BlockSpec((1, C), lambda i: (0, 0))] + [ANY] * n_after,
        out_specs=pl.BlockSpec((tr, C), lambda i: (i, 0)),
        out_shape=jax.ShapeDtypeStruct((R, C), BF16),
        compiler_params=_cp(("parallel",), VMEM_MB),
    )(x, g, *after)


def _rms_bwd(dh, x, g, dres, *, name, want_dx=True, want_bf=True, after=()):
    R, C = x.shape
    tr = _pick(R, TR_EW, 16)
    has_res = dres is not None
    row = pl.BlockSpec((tr, C), lambda i: (i, 0))
    vec = pl.BlockSpec((1, C), lambda i: (0, 0))

    def body(*refs):
        dh_ref, x_ref, g_ref = refs[:3]
        pos = 3
        dres_ref = None
        if has_res:
            dres_ref = refs[pos]
            pos += 1
        outs = refs[pos + len(after):]
        i = pl.program_id(0)
        xv = x_ref[...]
        r = lax.rsqrt(jnp.mean(xv * xv, axis=-1, keepdims=True) + EPS)
        xh = xv * r
        dhv = dh_ref[...]
        dg_ref = outs[-1]
        dgp = jnp.sum(dhv * xh, axis=0, keepdims=True)

        @pl.when(i == 0)
        def _():
            dg_ref[...] = dgp

        @pl.when(i > 0)
        def _():
            dg_ref[...] += dgp

        if want_dx:
            t = dhv * g_ref[...]
            dx = r * (t - xh * jnp.mean(t * xh, axis=-1, keepdims=True))
            if has_res:
                dx = dx + dres_ref[...]
            outs[0][...] = dx
            if want_bf:
                outs[1][...] = dx.astype(BF16)

    in_specs = [row, row, vec] + ([row] if has_res else []) + [ANY] * len(after)
    out_specs, out_shape = [], []
    if want_dx:
        out_specs.append(row)
        out_shape.append(jax.ShapeDtypeStruct((R, C), F32))
        if want_bf:
            out_specs.append(row)
            out_shape.append(jax.ShapeDtypeStruct((R, C), BF16))
    out_specs.append(vec)
    out_shape.append(jax.ShapeDtypeStruct((1, C), F32))
    args = [dh, x, g] + ([dres] if has_res else []) + list(after)
    return pl.pallas_call(
        body, name=name, grid=(R // tr,), in_specs=in_specs, out_specs=out_specs, out_shape=out_shape,
        compiler_params=_cp(("arbitrary",), VMEM_MB),
    )(*args)


def _loss_bwd(x3, g, tgt, *, name):
    R, C = x3.shape
    tr = _pick(R, TR_EW, 16)
    n = R // tr
    row = pl.BlockSpec((tr, C), lambda i: (i, 0))
    vec = pl.BlockSpec((1, C), lambda i: (0, 0))

    def body(x_ref, g_ref, t_ref, dx_ref, dxb_ref, dg_ref, loss_ref, acc_ref):
        i = pl.program_id(0)
        xv = x_ref[...]
        gv = g_ref[...]
        r = lax.rsqrt(jnp.mean(xv * xv, axis=-1, keepdims=True) + EPS)
        xh = xv * r
        e = xh * gv - t_ref[...]
        dy = e * (1.0 / C)
        sq = jnp.sum(e * e, axis=0, keepdims=True)
        dgp = jnp.sum(dy * xh, axis=0, keepdims=True)

        @pl.when(i == 0)
        def _():
            acc_ref[...] = sq
            dg_ref[...] = dgp

        @pl.when(i > 0)
        def _():
            acc_ref[...] += sq
            dg_ref[...] += dgp

        t = dy * gv
        dx = r * (t - xh * jnp.mean(t * xh, axis=-1, keepdims=True))
        dx_ref[...] = dx
        dxb_ref[...] = dx.astype(BF16)

        @pl.when(i == n - 1)
        def _():
            loss_ref[...] = jnp.sum(acc_ref[...], axis=-1, keepdims=True) * (0.5 / C)

    return pl.pallas_call(
        body, name=name, grid=(n,),
        in_specs=[row, vec, row],
        out_specs=[row, row, vec, pl.BlockSpec((1, 1), lambda i: (0, 0))],
        out_shape=[jax.ShapeDtypeStruct((R, C), F32), jax.ShapeDtypeStruct((R, C), BF16),
                   jax.ShapeDtypeStruct((1, C), F32), jax.ShapeDtypeStruct((1, 1), F32)],
        scratch_shapes=[pltpu.VMEM((1, C), F32)],
        compiler_params=_cp(("arbitrary",), VMEM_MB),
    )(x3, g, tgt)


def _offsets():
    u0 = 0
    v0 = DS
    b0 = 2 * DS
    c0 = b0 + DC
    x0 = c0 + DC
    q0 = x0 + DC
    return u0, v0, b0, c0, x0, q0


def _tri_mask(lower):
    r = lax.broadcasted_iota(jnp.int32, (CHUNK, CHUNK), 0)
    c = lax.broadcasted_iota(jnp.int32, (CHUNK, CHUNK), 1)
    return (r >= c) if lower else (c >= r)


def _layer_norm_stats(vg):
    mu = jnp.mean(vg, axis=-1, keepdims=True)
    vc = vg - mu
    rstd = lax.rsqrt(jnp.mean(vc * vc, axis=-1, keepdims=True) + EPS)
    return vc * rstd, rstd


def _softmax_rows(qh, kh):
    s = lax.dot_general(qh, kh, (((1,), (1,)), ((), ())), preferred_element_type=F32)
    m = jnp.max(s, axis=-1, keepdims=True)
    e = jnp.exp(s - m)
    return e / jnp.sum(e, axis=-1, keepdims=True)


def _mix_fwd(proj, kv, w_s, bs_t, ln_g, ln_b, conv_w, g_head, *, name):
    assert DS == DC
    tr = _pick(S, TR_MIX, CHUNK)
    n = S // tr
    nck = tr // CHUNK
    u0, v0, b0, c0, x0, q0 = _offsets()
    hb = tr // HALO

    def body(p_ref, cprev_ref, xprev_ref, kv_ref, ws_ref, bst_ref, lng_ref, lnb_ref, cw_ref, gh_ref,
             heads_ref, hn_ref, ycv_ref, buf_ref):
        i = pl.program_id(0)

        def emit(col, val):
            rs = lax.rsqrt(jnp.mean(val * val, axis=-1, keepdims=True) + EPS)
            heads_ref[:, col:col + HD] = val
            hn_ref[:, col:col + HD] = ((val * rs) * gh_ref[:, col:col + HD]).astype(BF16)

        vhat, _ = _layer_norm_stats(_gelu(p_ref[:, v0:v0 + DS]))
        vnb = (vhat * lng_ref[...] + lnb_ref[...]).astype(BF16)
        low = _tri_mask(True)
        for h in range(NSH):
            wt = jnp.where(low, ws_ref[h], 0.0).astype(BF16)
            bcol = bst_ref[:, h:h + 1]
            parts = []
            for c in range(nck):
                blk = vnb[c * CHUNK:(c + 1) * CHUNK, h * HD:(h + 1) * HD]
                parts.append(jnp.dot(wt, blk, preferred_element_type=F32) + bcol)
            mixed = parts[0] if nck == 1 else jnp.concatenate(parts, axis=0)
            emit(h * HD, _gelu(p_ref[:, u0 + h * HD:u0 + (h + 1) * HD]) * mixed)

        xc = p_ref[:, c0:c0 + DC] * p_ref[:, x0:x0 + DC]
        prev = cprev_ref[...] * xprev_ref[...]
        buf_ref[0:HALO, :] = jnp.where(i > 0, prev, 0.0)
        buf_ref[HALO:HALO + tr, :] = xc
        y = (cw_ref[2:3, :] * xc + cw_ref[1:2, :] * buf_ref[HALO - 1:HALO - 1 + tr, :]
             + cw_ref[0:1, :] * buf_ref[HALO - 2:HALO - 2 + tr, :])
        ycv_ref[...] = y
        cout = p_ref[:, b0:b0 + DC] * y
        for h in range(NCH):
            emit(DS + h * HD, cout[:, h * HD:(h + 1) * HD])

        for h in range(NMH):
            qh = (p_ref[:, q0 + h * HD:q0 + (h + 1) * HD] * SCALE).astype(BF16)
            kh = kv_ref[:, h * HD:(h + 1) * HD].astype(BF16)
            vh = kv_ref[:, DM + h * HD:DM + (h + 1) * HD].astype(BF16)
            p = _softmax_rows(qh, kh)
            emit(DS + DC + h * HD, jnp.dot(p.astype(BF16), vh, preferred_element_type=F32))

    full = lambda shape: pl.BlockSpec(shape, lambda i: (0,) * len(shape))
    halo_c = pl.BlockSpec((HALO, DC), lambda i: (jnp.maximum(i * hb - 1, 0), c0 // DC))
    halo_x = pl.BlockSpec((HALO, DC), lambda i: (jnp.maximum(i * hb - 1, 0), x0 // DC))
    return pl.pallas_call(
        body, name=name, grid=(n,),
        in_specs=[pl.BlockSpec((tr, DIN), lambda i: (i, 0)), halo_c, halo_x,
                  full((NMEM, 2 * DM)), full((NSH, CHUNK, CHUNK)), full((CHUNK, NSH)),
                  full((1, DS)), full((1, DS)), full((3, DC)), full((1, D))],
        out_specs=[pl.BlockSpec((tr, D), lambda i: (i, 0)), pl.BlockSpec((tr, D), lambda i: (i, 0)),
                   pl.BlockSpec((tr, DC), lambda i: (i, 0))],
        out_shape=[jax.ShapeDtypeStruct((S, D), F32), jax.ShapeDtypeStruct((S, D), BF16),
                   jax.ShapeDtypeStruct((S, DC), F32)],
        scratch_shapes=[pltpu.VMEM((tr + HALO, DC), F32)],
        compiler_params=_cp(("parallel",), VMEM_MB),
    )(proj, proj, proj, kv, w_s, bs_t, ln_g, ln_b, conv_w, g_head)


def _mix_bwd(dhn, heads, proj, ycv, kv, w_s, bs_t, ln_g, ln_b, conv_w, g_head, after, *, name):
    assert DS == DC
    tr = _pick(S, TR_MIX, CHUNK)
    n = S // tr
    nck = tr // CHUNK
    u0, v0, b0, c0, x0, q0 = _offsets()
    hb = tr // HALO
    last_hb = S // HALO - 1

    def body(dhn_ref, heads_ref, p_ref, ycv_ref, dhn_nx_ref, heads_nx_ref, b_nx_ref, kv_ref, ws_ref, bst_ref,
             lng_ref, lnb_ref, cw_ref, gh_ref, _after_ref,
             dp_ref, dkv_ref, dws_ref, dbs_ref, dlng_ref, dlnb_ref, dcw_ref, dgh_ref, buf_ref, dvn_ref):
        i = pl.program_id(0)

        @pl.when(i == 0)
        def _():
            dkv_ref[...] = jnp.zeros_like(dkv_ref)
            dws_ref[...] = jnp.zeros_like(dws_ref)
            dbs_ref[...] = jnp.zeros_like(dbs_ref)
            dlng_ref[...] = jnp.zeros_like(dlng_ref)
            dlnb_ref[...] = jnp.zeros_like(dlnb_ref)
            dcw_ref[...] = jnp.zeros_like(dcw_ref)
            dgh_ref[...] = jnp.zeros_like(dgh_ref)

        def head_bwd(a, dn, gh):
            rs = lax.rsqrt(jnp.mean(a * a, axis=-1, keepdims=True) + EPS)
            ah = a * rs
            t = dn * gh
            return rs * (t - ah * jnp.mean(t * ah, axis=-1, keepdims=True)), jnp.sum(dn * ah, axis=0, keepdims=True)

        def head_grad(col):
            da, dg = head_bwd(heads_ref[:, col:col + HD], dhn_ref[:, col:col + HD], gh_ref[:, col:col + HD])
            dgh_ref[:, col:col + HD] += dg
            return da

        vg, dvg_dv = _gelu_with_grad(p_ref[:, v0:v0 + DS])
        vhat, rstd = _layer_norm_stats(vg)
        vnb = (vhat * lng_ref[...] + lnb_ref[...]).astype(BF16)
        low = _tri_mask(True)
        ones = jnp.ones((HALO, HD), BF16)
        for h in range(NSH):
            w_h = ws_ref[h]
            wt = jnp.where(low, w_h, 0.0).astype(BF16)
            bcol = bst_ref[:, h:h + 1]
            da = head_grad(h * HD)
            ug, dug_du = _gelu_with_grad(p_ref[:, u0 + h * HD:u0 + (h + 1) * HD])
            dws = jnp.zeros((CHUNK, CHUNK), F32)
            dbs = jnp.zeros((HALO, CHUNK), F32)
            mixed_parts = []
            for c in range(nck):
                rows = slice(c * CHUNK, (c + 1) * CHUNK)
                blk = vnb[rows, h * HD:(h + 1) * HD]
                mixed_parts.append(jnp.dot(wt, blk, preferred_element_type=F32) + bcol)
                dmb = (da[rows] * ug[rows]).astype(BF16)
                dws = dws + lax.dot_general(dmb, blk, (((1,), (1,)), ((), ())), preferred_element_type=F32)
                dbs = dbs + lax.dot_general(ones, dmb, (((1,), (1,)), ((), ())), preferred_element_type=F32)
                dvn_ref[c * CHUNK:(c + 1) * CHUNK, h * HD:(h + 1) * HD] = lax.dot_general(
                    wt, dmb, (((0,), (0,)), ((), ())), preferred_element_type=F32)
            mixed = mixed_parts[0] if nck == 1 else jnp.concatenate(mixed_parts, axis=0)
            dp_ref[:, u0 + h * HD:u0 + (h + 1) * HD] = ((da * mixed) * dug_du).astype(BF16)
            dws_ref[h] += jnp.where(low, dws, 0.0)
            dbs_ref[h] += dbs
        dvn = dvn_ref[...]
        dlng_ref[...] += jnp.sum(dvn * vhat, axis=0, keepdims=True)
        dlnb_ref[...] += jnp.sum(dvn, axis=0, keepdims=True)
        dvh = dvn * lng_ref[...]
        dvg = rstd * (dvh - jnp.mean(dvh, axis=-1, keepdims=True)
                      - vhat * jnp.mean(dvh * vhat, axis=-1, keepdims=True))
        dp_ref[:, v0:v0 + DS] = (dvg * dvg_dv).astype(BF16)

        dc = jnp.concatenate([head_grad(DS + h * HD) for h in range(NCH)], axis=1)
        dc_nx = jnp.concatenate(
            [head_bwd(heads_nx_ref[:, h * HD:(h + 1) * HD], dhn_nx_ref[:, h * HD:(h + 1) * HD],
                      gh_ref[:, DS + h * HD:DS + (h + 1) * HD])[0] for h in range(NCH)], axis=1)
        bg = p_ref[:, b0:b0 + DC]
        cg = p_ref[:, c0:c0 + DC]
        xin = p_ref[:, x0:x0 + DC]
        dp_ref[:, b0:b0 + DC] = (dc * ycv_ref[...]).astype(BF16)
        dyv = dc * bg
        buf_ref[0:tr, :] = dyv
        buf_ref[tr:tr + HALO, :] = jnp.where(i < n - 1, dc_nx * b_nx_ref[...], 0.0)
        sh1 = buf_ref[1:1 + tr, :]
        sh0 = buf_ref[2:2 + tr, :]
        dxc = cw_ref[2:3, :] * dyv + cw_ref[1:2, :] * sh1 + cw_ref[0:1, :] * sh0
        xc = cg * xin
        dp_ref[:, c0:c0 + DC] = (dxc * xin).astype(BF16)
        dp_ref[:, x0:x0 + DC] = (dxc * cg).astype(BF16)
        dcw_ref[0:1, :] += jnp.sum(sh0 * xc, axis=0, keepdims=True)
        dcw_ref[1:2, :] += jnp.sum(sh1 * xc, axis=0, keepdims=True)
        dcw_ref[2:3, :] += jnp.sum(dyv * xc, axis=0, keepdims=True)

        for h in range(NMH):
            do = head_grad(DS + DC + h * HD).astype(BF16)
            qh = (p_ref[:, q0 + h * HD:q0 + (h + 1) * HD] * SCALE).astype(BF16)
            kh = kv_ref[:, h * HD:(h + 1) * HD].astype(BF16)
            vh = kv_ref[:, DM + h * HD:DM + (h + 1) * HD].astype(BF16)
            p = _softmax_rows(qh, kh)
            dpr = lax.dot_general(do, vh, (((1,), (1,)), ((), ())), preferred_element_type=F32)
            ds = (p * (dpr - jnp.sum(dpr * p, axis=-1, keepdims=True))).astype(BF16)
            dp_ref[:, q0 + h * HD:q0 + (h + 1) * HD] = (
                jnp.dot(ds, kh, preferred_element_type=F32) * SCALE).astype(BF16)
            dkv_ref[:, h * HD:(h + 1) * HD] += lax.dot_general(
                ds, qh, (((0,), (0,)), ((), ())), preferred_element_type=F32)
            dkv_ref[:, DM + h * HD:DM + (h + 1) * HD] += lax.dot_general(
                p.astype(BF16), do, (((0,), (0,)), ((), ())), preferred_element_type=F32)

    full = lambda shape: pl.BlockSpec(shape, lambda i: (0,) * len(shape))
    row = lambda c: pl.BlockSpec((tr, c), lambda i: (i, 0))
    nxt = lambda col: pl.BlockSpec((HALO, DC), lambda i: (jnp.minimum((i + 1) * hb, last_hb), col))
    return pl.pallas_call(
        body, name=name, grid=(n,),
        in_specs=[row(D), row(D), row(DIN), row(DC), nxt(DS // DC), nxt(DS // DC), nxt(b0 // DC),
                  full((NMEM, 2 * DM)), full((NSH, CHUNK, CHUNK)), full((CHUNK, NSH)),
                  full((1, DS)), full((1, DS)), full((3, DC)), full((1, D)), ANY],
        out_specs=[row(DIN), full((NMEM, 2 * DM)), full((NSH, CHUNK, CHUNK)), full((NSH, HALO, CHUNK)),
                   full((1, DS)), full((1, DS)), full((HALO, DC)), full((1, D))],
        out_shape=[jax.ShapeDtypeStruct((S, DIN), BF16), jax.ShapeDtypeStruct((NMEM, 2 * DM), F32),
                   jax.ShapeDtypeStruct((NSH, CHUNK, CHUNK), F32), jax.ShapeDtypeStruct((NSH, HALO, CHUNK), F32),
                   jax.ShapeDtypeStruct((1, DS), F32), jax.ShapeDtypeStruct((1, DS), F32),
                   jax.ShapeDtypeStruct((HALO, DC), F32), jax.ShapeDtypeStruct((1, D), F32)],
        scratch_shapes=[pltpu.VMEM((tr + HALO, DC), F32), pltpu.VMEM((tr, DS), F32)],
        compiler_params=_cp(("arbitrary",), VMEM_MB),
    )(dhn, heads, proj, ycv, dhn, heads, proj, kv, w_s, bs_t, ln_g, ln_b, conv_w, g_head, after)


def _place():
    x, y, c = lax.axis_index("x"), lax.axis_index("y"), lax.axis_index("c")
    chips = [(1 - x, y), (x, 1 - y), (1 - x, 1 - y)]
    return x, y, c, chips


ANY = pl.BlockSpec(memory_space=pl.ANY)


HBM = pl.BlockSpec(memory_space=pltpu.HBM)
SEM = pl.BlockSpec(memory_space=pltpu.SEMAPHORE)
EFFECT = pltpu.SideEffectType.DATAFLOW_SIDE_EFFECTING
N_PEER_CHIPS = 3
N_NEIGHBOUR_CHIPS = 2
CONV_PAD = (32, 256)


def _in_hbm(a):
    return pltpu.with_memory_space_constraint(a, pltpu.HBM)


def _allgather_start(bufs, after, *, name):
    nw = len(bufs)

    def body(*refs):
        ins, send, recv = refs[:nw], refs[nw + 1:2 * nw + 1], refs[2 * nw + 1:3 * nw + 1]
        token = refs[4 * nw + 1]
        x, y, c, chips = _place()
        s = 2 * x + y
        for w in range(nw):
            hr = bufs[w].shape[1] // 2
            rows = ins[w].at[s, pl.ds(c * hr, hr)]
            for cx, cy in chips[:N_NEIGHBOUR_CHIPS]:
                pltpu.make_async_remote_copy(src_ref=rows, dst_ref=rows, send_sem=send[w], recv_sem=recv[w],
                                             device_id=(cx, cy, c), device_id_type=MESH).start()
        token[...] = jnp.zeros_like(token)

    res = pl.pallas_call(
        body, name=name,
        in_specs=[HBM] * nw + [ANY],
        out_specs=[SEM] * (2 * nw) + [HBM] * nw + [pl.BlockSpec(memory_space=pltpu.VMEM)],
        out_shape=[pltpu.SemaphoreType.DMA(())] * (2 * nw) + [pltpu.HBM(a.shape, a.dtype) for a in bufs]
        + [jax.ShapeDtypeStruct((8, 128), F32)],
        input_output_aliases={w: 2 * nw + w for w in range(nw)},
        compiler_params=pltpu.CompilerParams(has_side_effects=EFFECT),
    )(*[_in_hbm(a) for a in bufs], after)
    return res[:nw], res[nw:2 * nw], res[2 * nw:3 * nw], res[3 * nw]


def _handshake(peers):
    barrier = pltpu.get_barrier_semaphore()
    for peer in peers:
        pl.semaphore_signal(barrier, inc=1, device_id=peer, device_id_type=MESH)
    pl.semaphore_wait(barrier, len(peers))


def _scatter_start(parts, bufs, collective_id, *, name):
    nw = len(parts)

    def body(*refs):
        src, dst = refs[:nw], refs[nw:2 * nw]
        send, recv = refs[2 * nw:3 * nw], refs[3 * nw:4 * nw]
        token = refs[6 * nw]
        x, y, c, chips = _place()
        s = 2 * x + y
        _handshake([(cx, cy, c) for cx, cy in chips])
        for w in range(nw):
            for cx, cy in chips:
                pltpu.make_async_remote_copy(src_ref=src[w].at[2 * cx + cy], dst_ref=dst[w].at[s], send_sem=send[w],
                                             recv_sem=recv[w], device_id=(cx, cy, c), device_id_type=MESH).start()
        token[...] = jnp.zeros_like(token)

    res = pl.pallas_call(
        body, name=name,
        in_specs=[HBM] * (2 * nw),
        out_specs=[SEM] * (2 * nw) + [HBM] * (2 * nw) + [pl.BlockSpec(memory_space=pltpu.VMEM)],
        out_shape=[pltpu.SemaphoreType.DMA(())] * (2 * nw) + [pltpu.HBM(a.shape, a.dtype) for a in parts + bufs]
        + [jax.ShapeDtypeStruct((8, 128), F32)],
        input_output_aliases={k: 2 * nw + k for k in range(2 * nw)},
        compiler_params=pltpu.CompilerParams(has_side_effects=EFFECT, collective_id=collective_id),
    )(*[_in_hbm(a) for a in parts + bufs])
    return res[:nw], res[nw:2 * nw], res[2 * nw:3 * nw], res[3 * nw:4 * nw], res[4 * nw]


def _sibling_start(srcs, whole, collective_id, *, name):
    nw = len(srcs)
    lands = [lax.empty((a.shape[0], a.shape[1] if whole else a.shape[1] // 2, a.shape[2]), a.dtype) for a in srcs]

    def body(*refs):
        src, land = refs[:nw], refs[nw:2 * nw]
        send, recv = refs[2 * nw:3 * nw], refs[3 * nw:4 * nw]
        token = refs[6 * nw]
        x, y, c, _ = _place()
        _handshake([(x, y, 1 - c)])
        for w in range(nw):
            hr = srcs[w].shape[1] // 2
            rows = src[w] if whole else src[w].at[:, pl.ds((1 - c) * hr, hr)]
            pltpu.make_async_remote_copy(src_ref=rows, dst_ref=land[w], send_sem=send[w], recv_sem=recv[w],
                                         device_id=(x, y, 1 - c), device_id_type=MESH).start()
        token[...] = jnp.zeros_like(token)

    res = pl.pallas_call(
        body, name=name,
        in_specs=[HBM] * (2 * nw),
        out_specs=[SEM] * (2 * nw) + [HBM] * (2 * nw) + [pl.BlockSpec(memory_space=pltpu.VMEM)],
        out_shape=[pltpu.SemaphoreType.DMA(())] * (2 * nw) + [pltpu.HBM(a.shape, a.dtype) for a in srcs + lands]
        + [jax.ShapeDtypeStruct((8, 128), F32)],
        input_output_aliases={k: 2 * nw + k for k in range(2 * nw)},
        compiler_params=pltpu.CompilerParams(has_side_effects=EFFECT, collective_id=collective_id),
    )(*[_in_hbm(a) for a in srcs + lands])
    return res[:nw], res[nw:2 * nw], res[2 * nw:3 * nw], res[3 * nw:4 * nw], res[4 * nw]


def _transfer_wait(sends, recvs, thru, sizes, after, *, name):
    n = len(sends)
    flat = [a for group in thru for a in group]

    def body(*refs):
        bufs = refs[:len(flat)]
        send = refs[len(flat):len(flat) + n]
        recv = refs[len(flat) + n:len(flat) + 2 * n]
        token = refs[2 * len(flat) + 2 * n + 1]
        token[...] = jnp.zeros_like(token)
        x, y, c, _ = _place()
        pos = 0
        for k in range(n):
            slots, rows = sizes[k]
            region = bufs[pos].at[pl.ds(0, slots), pl.ds(0, rows)]
            pos += len(thru[k])
            cp = pltpu.make_async_remote_copy(src_ref=region, dst_ref=region, send_sem=send[k], recv_sem=recv[k],
                                              device_id=(x, y, 1 - c), device_id_type=MESH)
            cp.wait_send()
            cp.wait_recv()

    res = pl.pallas_call(
        body, name=name,
        in_specs=[HBM] * len(flat) + [SEM] * (2 * n) + [pl.BlockSpec(memory_space=pl.ANY)],
        out_specs=[HBM] * len(flat) + [pl.BlockSpec(memory_space=pltpu.VMEM)],
        out_shape=[pltpu.HBM(a.shape, a.dtype) for a in flat] + [jax.ShapeDtypeStruct((8, 128), F32)],
        input_output_aliases={k: k for k in range(len(flat))},
        compiler_params=pltpu.CompilerParams(has_side_effects=EFFECT),
    )(*flat, *sends, *recvs, after)
    out, pos = [], 0
    for group in thru:
        out.append(res[pos:pos + len(group)])
        pos += len(group)
    return out, res[len(flat)]


def _forward_gathered(bufs, after, *, name):
    nw = len(bufs)

    def body(*refs):
        outs = refs[nw + 1:2 * nw + 1]
        d_send, d_recv, i_send, i_recv = refs[2 * nw + 1:]
        x, y, c, chips = _place()
        me, sibling = (x, y, c), (x, y, 1 - c)
        slots = [2 * cx + cy for cx, cy in chips]

        def rows(w, j, start, n):
            return outs[w].at[slots[j], pl.ds(start, n)]

        def d2d(w, j, which, to):
            hr = bufs[w].shape[1] // 2
            r = rows(w, j, which * hr, hr)
            return pltpu.make_async_remote_copy(
                src_ref=r, dst_ref=r, send_sem=d_send.at[N_PEER_CHIPS * w + j],
                recv_sem=d_recv.at[N_PEER_CHIPS * w + j], device_id=to, device_id_type=MESH)

        def ici(w, j, slot_j, to):
            q = bufs[w].shape[1] // 4
            r = rows(w, slot_j, c * 2 * q + j * q, q)
            return pltpu.make_async_remote_copy(
                src_ref=r, dst_ref=r, send_sem=i_send.at[N_NEIGHBOUR_CHIPS * w + j],
                recv_sem=i_recv.at[N_NEIGHBOUR_CHIPS * w + j], device_id=to, device_id_type=MESH)

        started = []
        for w in range(nw):
            started += [ici(w, 0, 0, (*chips[1], c)), ici(w, 1, 1, (*chips[0], c))]
            started += [d2d(w, j, c, sibling) for j in range(N_NEIGHBOUR_CHIPS)]
        for cp in started:
            cp.start()
        diag = N_PEER_CHIPS - 1
        for w in range(nw):
            for j in range(N_NEIGHBOUR_CHIPS):
                ici(w, j, diag, me).wait_recv()
            cp = d2d(w, diag, c, sibling)
            cp.start()
            started.append(cp)
        for w in range(nw):
            for j in range(N_PEER_CHIPS):
                d2d(w, j, 1 - c, me).wait_recv()
        for cp in started:
            cp.wait_send()

    return pl.pallas_call(
        body, name=name,
        in_specs=[ANY] * (nw + 1), out_specs=[ANY] * nw,
        out_shape=[jax.ShapeDtypeStruct(a.shape, a.dtype) for a in bufs],
        input_output_aliases={w: w for w in range(nw)},
        scratch_shapes=[pltpu.SemaphoreType.DMA((N_PEER_CHIPS * nw,)), pltpu.SemaphoreType.DMA((N_PEER_CHIPS * nw,)),
                        pltpu.SemaphoreType.DMA((N_NEIGHBOUR_CHIPS * nw,)),
                        pltpu.SemaphoreType.DMA((N_NEIGHBOUR_CHIPS * nw,))],
    )(*bufs, after)


def _allreduce_small(p, after, *, name):
    R = p.shape[0]
    hr = R // 2

    def body(p_ref, _after_ref, out_ref, sib_ref, sum_ref, gat_ref, tot_ref, send, recv):
        x, y, c, chips = _place()
        s = 2 * x + y
        sibling = (x, y, 1 - c)
        rows = pl.ds(pl.multiple_of(c * hr, 8), hr)
        swap = pltpu.make_async_remote_copy(src_ref=p_ref, dst_ref=sib_ref, send_sem=send.at[0], recv_sem=recv.at[0],
                                            device_id=sibling, device_id_type=MESH)
        swap.start()
        swap.wait()
        sum_ref[...] = p_ref[...] + sib_ref[...]
        gat_ref[s] = sum_ref[rows, :]
        cps = [pltpu.make_async_remote_copy(src_ref=sum_ref.at[rows], dst_ref=gat_ref.at[s], send_sem=send.at[1 + j],
                                            recv_sem=recv.at[1 + j], device_id=(cx, cy, c), device_id_type=MESH)
               for j, (cx, cy) in enumerate(chips)]
        for cp in cps:
            cp.start()
        for cp in cps:
            cp.wait()
        tot_ref[...] = ((gat_ref[0] + gat_ref[1]) + gat_ref[2]) + gat_ref[3]
        out_ref[rows, :] = tot_ref[...]
        share = pltpu.make_async_remote_copy(src_ref=tot_ref, dst_ref=out_ref.at[rows], send_sem=send.at[4],
                                             recv_sem=recv.at[4], device_id=sibling, device_id_type=MESH)
        share.start()
        share.wait_send()
        other = out_ref.at[pl.ds(pl.multiple_of((1 - c) * hr, 8), hr)]
        pltpu.make_async_remote_copy(src_ref=other, dst_ref=other, send_sem=send.at[4], recv_sem=recv.at[4],
                                     device_id=(x, y, c), device_id_type=MESH).wait_recv()

    vmem = pl.BlockSpec(memory_space=pltpu.VMEM)
    return pl.pallas_call(
        body, name=name, in_specs=[vmem, ANY], out_specs=vmem,
        out_shape=jax.ShapeDtypeStruct((R, 128), F32),
        scratch_shapes=[pltpu.VMEM((R, 128), F32), pltpu.VMEM((R, 128), F32), pltpu.VMEM((NCHIP, hr, 128), F32),
                        pltpu.VMEM((hr, 128), F32), pltpu.SemaphoreType.DMA((5,)), pltpu.SemaphoreType.DMA((5,))],
    )(p, after)


def _select_half_bf16(g, half, add, slot, *, name):
    _, R, C = g.shape
    hr = R // 2
    tr = _pick_rows(hr, 16)
    nb = hr // tr
    sel = jnp.concatenate([jnp.reshape(half, (1,)).astype(jnp.int32), slot])

    def body(s_ref, g_ref, a_ref, o_ref, own_ref):
        val = (g_ref[...].astype(F32) + a_ref[...].astype(F32)).astype(BF16)
        o_ref[...] = val

        @pl.when(pl.program_id(1) == s_ref[1])
        def _():
            own_ref[...] = val

    g_spec = pl.BlockSpec((None, tr, C), lambda i, j, s: (j, s[0] * nb + i, 0))
    o_spec = pl.BlockSpec((None, tr, C), lambda i, j, s: (j, i, 0))
    own_spec = pl.BlockSpec((None, tr, C), lambda i, j, s: (s[1], i, 0))
    shape = jax.ShapeDtypeStruct((NCHIP, hr, C), BF16)
    return pl.pallas_call(
        body, name=name,
        grid_spec=pltpu.PrefetchScalarGridSpec(
            num_scalar_prefetch=1, grid=(nb, NCHIP), in_specs=[g_spec, o_spec], out_specs=[o_spec, own_spec]),
        out_shape=[shape, shape],
        compiler_params=_cp(("parallel", "arbitrary"), VMEM_MB),
    )(sel, g, add)


def _adamw_math(w, g, m, v):
    m = ADAM_B1 * m + (1.0 - ADAM_B1) * g
    v = ADAM_B2 * v + (1.0 - ADAM_B2) * (g * g)
    m_hat = m / (1.0 - ADAM_B1 ** ADAM_STEP)
    v_hat = v / (1.0 - ADAM_B2 ** ADAM_STEP)
    delta = -ADAM_LR * (m_hat / (jnp.sqrt(v_hat) + ADAM_EPS) + ADAM_WD * w)
    return delta, m, v


def _adamw(w, g_mine, g_sib, m, v, core, *, name):
    R, C = w.shape
    hr = R // 2
    tr = _pick_rows(hr, 16)
    nb = hr // tr
    row = pl.BlockSpec((tr, C), lambda hh, i, c: (hh * nb + i, 0))
    mine = pl.BlockSpec((NCHIP, tr, C), lambda hh, i, c: (0, jnp.where(hh == c[0], i, 0), 0))
    sibs = pl.BlockSpec((NCHIP, tr, C), lambda hh, i, c: (0, jnp.where(hh == c[0], 0, i), 0))

    def slot_sum(ref):
        acc = ref[0].astype(F32) + ref[1].astype(F32)
        for j in range(2, NCHIP):
            acc = acc + ref[j].astype(F32)
        return acc

    def body(c_ref, w_ref, gm_ref, gs_ref, m_ref, v_ref, go_ref, d_ref, mo_ref, vo_ref):
        gv = jnp.where(pl.program_id(0) == c_ref[0], slot_sum(gm_ref), slot_sum(gs_ref))
        d, mn, vn = _adamw_math(w_ref[...], gv, m_ref[...], v_ref[...])
        go_ref[...] = gv
        d_ref[...] = d
        mo_ref[...] = mn
        vo_ref[...] = vn

    return pl.pallas_call(
        body, name=name,
        grid_spec=pltpu.PrefetchScalarGridSpec(
            num_scalar_prefetch=1, grid=(2, nb),
            in_specs=[row, mine, sibs, row, row], out_specs=[row] * 4),
        out_shape=[jax.ShapeDtypeStruct((R, C), F32)] * 4,
        compiler_params=_cp(("parallel", "parallel"), VMEM_MB),
    )(core, w, g_mine, g_sib, m, v)


def _adamw_small(ws, gs, ms, vs, *, name):
    n = len(ws)

    def body(*refs):
        w_r, g_r, m_r, v_r = refs[:n], refs[n:2 * n], refs[2 * n:3 * n], refs[3 * n:4 * n]
        d_r, mo_r, vo_r = refs[4 * n:5 * n], refs[5 * n:6 * n], refs[6 * n:7 * n]
        for k in range(n):
            d, mn, vn = _adamw_math(w_r[k][...], g_r[k][...], m_r[k][...], v_r[k][...])
            d_r[k][...] = d
            mo_r[k][...] = mn
            vo_r[k][...] = vn

    shapes = [jax.ShapeDtypeStruct(w.shape, F32) for w in ws]
    res = pl.pallas_call(body, name=name, out_shape=shapes * 3)(*ws, *gs, *ms, *vs)
    return res[:n], res[n:2 * n], res[2 * n:]


_PACK_ROWS = 8


def _pack(parts):
    rows = []
    for a in parts:
        flat = a.reshape(-1)
        n = -(-flat.shape[0] // (_PACK_ROWS * 128)) * (_PACK_ROWS * 128)
        rows.append(jnp.pad(flat, (0, n - flat.shape[0])).reshape(-1, 128))
    total = sum(r.shape[0] for r in rows)
    if total % 16:
        rows.append(jnp.zeros((16 - total % 16, 128), F32))
    return jnp.concatenate(rows, axis=0)


def _unpack(p, shapes):
    out, r = [], 0
    for shp in shapes:
        n = math.prod(shp)
        nr = -(-n // (_PACK_ROWS * 128)) * _PACK_ROWS
        out.append(p[r:r + nr].reshape(-1)[:n].reshape(shp))
        r += nr
    return out


def kernel(x, mem, g_mix, w_in, ln_v_g, ln_v_b, w_s, b_s, conv_w, g_mem, w_kv, g_head, w_o, g_ffn, w_ffn1, w_ffn2, g_final, loss_target, m_g_mix, m_w_in, m_ln_v_g, m_ln_v_b, m_w_s, m_b_s, m_conv_w, m_g_mem, m_w_kv, m_g_head, m_w_o, m_g_ffn, m_w_ffn1, m_w_ffn2, m_g_final, v_g_mix, v_w_in, v_ln_v_g, v_ln_v_b, v_w_s, v_b_s, v_conv_w, v_g_mem, v_w_kv, v_g_head, v_w_o, v_g_ffn, v_w_ffn1, v_w_ffn2, v_g_final):
    sds = jax.ShapeDtypeStruct
    xi, yi = lax.axis_index("x"), lax.axis_index("y")
    shard = 2 * xi + yi
    x2d, mem2d, tgt = x[0], mem[0], loss_target[0]
    ws3, bs2 = w_s[0], b_s[0]
    g_final2 = g_final.reshape(1, D)
    dff4 = DFF // NCHIP
    din4 = DIN // NCHIP
    dcv4 = DC // NCHIP

    big = [w_in[0].T, w_kv[0], w_o[0], w_ffn1[0], w_ffn2[0]]
    big_names = ["w_in", "w_kv", "w_o", "w_ffn1", "w_ffn2"]
    slot = jnp.reshape(shard, (1,)).astype(jnp.int32)
    core = jnp.reshape(lax.axis_index("c"), (1,)).astype(jnp.int32)
    conv_pad = jnp.pad(conv_w[0], ((0, CONV_PAD[0] - 3), (0, CONV_PAD[1] - dcv4)))
    conv_slots = lax.dynamic_update_slice(jnp.zeros((NCHIP,) + CONV_PAD, F32), conv_pad[None], (shard, 0, 0))

    def gather_start(bufs, after, nm):
        return _allgather_start(bufs, after, name="ag_start_" + nm)

    def gather_wait(state, idx, after, nm):
        send, recv, bufs, _ = state
        got, token = _transfer_wait([send[k] for k in idx], [recv[k] for k in idx], [[bufs[k]] for k in idx],
                                    [(N_NEIGHBOUR_CHIPS, bufs[k].shape[1] // 2) for k in idx], after, name="ag_wait_" + nm)
        return [g[0] for g in got], token

    cast = lambda k, after: _cast_into_slot(big[k], slot, after, name="cast_" + big_names[k])
    ag_in = gather_start([cast(0, slot), conv_slots], slot, "in")
    bs_t = bs2.T

    h = _rms_fwd(x2d, g_mix, name="rms_mix", after=[ag_in[3]])
    mem_n = _rms_fwd(mem2d, g_mem, name="rms_mem", after=[h])
    kvo_b = [cast(1, mem_n)]
    kvo_b.append(cast(2, kvo_b[0]))
    w1_b = cast(3, kvo_b[1])
    w2_b = cast(4, w1_b)
    got_in, tok = gather_wait(ag_in, [0, 1], w2_b, "in")
    win4, conv4 = _forward_gathered(got_in, tok, name="ag_fwd_in")
    ag_kvo = gather_start(kvo_b, conv4, "kvo")
    w_in_t = win4.reshape(DIN, D)
    conv_full = conv4[:, :3, :dcv4].transpose(1, 0, 2).reshape(3, DC)
    (proj,) = _matmul(h, w_in_t, name="mm_proj", tb=True, M=S, N=DIN, K=D, tn=DIN // 2, outs=[sds((S, DIN), F32)],
                      after=[ag_kvo[3]])
    got_kvo, tok = gather_wait(ag_kvo, [0, 1], proj, "kvo")
    wkv4, wo4 = _forward_gathered(got_kvo, tok, name="ag_fwd_kvo")
    ag_w1 = gather_start([w1_b], wkv4, "ffn1")
    w_kv_full = wkv4.reshape(D, 2 * DM)
    w_o_full = wo4.reshape(D, D)
    (kv,) = _matmul(mem_n, w_kv_full, name="mm_kv", M=NMEM, N=2 * DM, K=D, outs=[sds((NMEM, 2 * DM), F32)],
                    after=[ag_w1[3]])
    heads, hn, ycv = _mix_fwd(proj, kv, ws3, bs_t, ln_v_g, ln_v_b, conv_full, g_head, name="mix_fwd")
    (x2,) = _matmul(hn, w_o_full, name="mm_wo", M=S, N=D, K=D, outs=[sds((S, D), F32)],
                    epi=lambda acc, res: (acc + res,), extras=[(x2d, _tile_spec())])
    h2 = _rms_fwd(x2, g_ffn, name="rms_ffn")
    got_w1, tok = gather_wait(ag_w1, [0], h2, "ffn1")
    (w14,) = _forward_gathered(got_w1, tok, name="ag_fwd_ffn1")
    ag_w2 = gather_start([w2_b], w14, "ffn2")

    def w1_cols(tn, tk):
        nb = dff4 // tn
        return pl.BlockSpec((None, tk, tn), lambda j, i, k: (j // nb, k, j % nb))

    (act,) = _matmul(h2, w14, name="mm_ffn1", M=S, N=DFF, K=D, tn=dff4, b_spec=w1_cols, outs=[sds((S, DFF), BF16)],
                     epi=lambda acc: (jnp.square(jnp.maximum(acc, 0.0)),), after=[ag_w2[3]])
    got_w2, tok = gather_wait(ag_w2, [0], act, "ffn2")
    (w24,) = _forward_gathered(got_w2, tok, name="ag_fwd_ffn2")
    w2_full = w24.reshape(DFF, D)
    (x3,) = _matmul(act, w2_full, name="mm_ffn2", M=S, N=D, K=DFF, tm=2 * TM, outs=[sds((S, D), F32)],
                    epi=lambda acc, res: (acc + res,), extras=[(x2, _tile_spec())])

    ci = lax.axis_index("c")

    def rs_sibling(g4, nm):
        return _sibling_start([g4], False, 1 + big_names.index(nm), name="rs_sib_" + nm)

    def rs_chips(state, after, nm):
        send, recv, g4, land, _ = state
        (((land_, g4_),), _) = _transfer_wait(send, recv, [[land[0], g4[0]]], [(NCHIP, land[0].shape[1])], after,
                                             name="rs_sibwait_" + nm)
        part, buf = _select_half_bf16(g4_, ci, land_, slot, name="rs_add_" + nm)
        return _scatter_start([part], [buf], 1 + 2 * len(big_names) + big_names.index(nm), name="rs_start_" + nm)

    def rs_end(state, after, nm):
        send, recv, parts, bufs, _ = state
        (((buf, _),), _) = _transfer_wait(send, recv, [[bufs[0], parts[0]]], [(N_PEER_CHIPS, bufs[0].shape[1])], after,
                                          name="rs_wait_" + nm)
        return _sibling_start([buf], True, 1 + len(big_names) + big_names.index(nm), name="rs_share_" + nm)

    big_m = [m_w_in[0].T, m_w_kv[0], m_w_o[0], m_w_ffn1[0], m_w_ffn2[0]]
    big_v = [v_w_in[0].T, v_w_kv[0], v_w_o[0], v_w_ffn1[0], v_w_ffn2[0]]
    big_out = {}

    def rs_finish(k, state, after):
        send, recv, mine, land, _ = state
        nm = big_names[k]
        (((land_, mine_),), _) = _transfer_wait(send, recv, [[land[0], mine[0]]], [(NCHIP, land[0].shape[1])], after,
                                               name="rs_sharewait_" + nm)
        big_out[nm] = _adamw(big[k], mine_, land_, big_m[k], big_v[k], core, name="adamw_" + nm)
        return big_out[nm][1]

    dx3, dx3b, dg_final, loss11 = _loss_bwd(x3, g_final2, tgt, name="loss_bwd")
    (dw2,) = _matmul(act, dx3b, name="mm_dw2", ta=True, M=DFF, N=D, K=S, tn=D, outs=[sds((DFF, D), BF16)])
    sib_w2 = rs_sibling(dw2.reshape(NCHIP, dff4, D), "w_ffn2")
    (dfb,) = _matmul(dx3b, w2_full, name="mm_dact", tb=True, M=S, N=DFF, K=D, tn=dff4, outs=[sds((S, DFF), BF16)],
                     epi=lambda acc, a: (acc * (2.0 * jnp.sqrt(a.astype(F32))),), extras=[(act, _tile_spec())],
                     after=[sib_w2[4]])
    rs_w2 = rs_chips(sib_w2, dfb, "w_ffn2")

    def dw1_out(tm, tn):
        nb = dff4 // tn
        return [pl.BlockSpec((None, tm, tn), lambda j, i, k: (j // nb, i, j % nb))]

    (dw1,) = _matmul(h2, dfb, name="mm_dw1", ta=True, M=D, N=DFF, K=S, tn=dff4, outs=[sds((NCHIP, D, dff4), BF16)],
                     out_specs=dw1_out, after=[rs_w2[4]])
    sib_w1 = rs_sibling(dw1, "w_ffn1")

    def w1_rows(tn, tk):
        kb = dff4 // tk
        return pl.BlockSpec((None, tn, tk), lambda j, i, k: (k // kb, j, k % kb))

    (dh2,) = _matmul(dfb, w14, name="mm_dh2", tb=True, M=S, N=D, K=DFF, tm=2 * TM, b_spec=w1_rows,
                     outs=[sds((S, D), F32)], after=[sib_w1[4]])
    rs_w1 = rs_chips(sib_w1, dh2, "w_ffn1")
    dx2, dx2b, dg_ffn = _rms_bwd(dh2, x2, g_ffn, dx3, name="rms_ffn_bwd", after=[rs_w1[4]])
    (dwo,) = _matmul(hn, dx2b, name="mm_dwo", ta=True, M=D, N=D, K=S, outs=[sds((D, D), BF16)])
    sib_wo = rs_sibling(dwo.reshape(NCHIP, D // NCHIP, D), "w_o")
    (dhn,) = _matmul(dx2b, w_o_full, name="mm_dhn", tb=True, M=S, N=D, K=D, outs=[sds((S, D), F32)],
                     after=[sib_wo[4]])
    rs_wo = rs_chips(sib_wo, dhn, "w_o")
    sh_w2 = rs_end(rs_w2, rs_wo[4], "w_ffn2")
    dproj, dkv, dws, dbs8, dlng, dlnb, dcw8, dgh = _mix_bwd(
        dhn, heads, proj, ycv, kv, ws3, bs_t, ln_v_g, ln_v_b, conv_full, g_head, sh_w2[4], name="mix_bwd")
    (dwin_t,) = _matmul(dproj, h, name="mm_dwin", ta=True, M=DIN, N=D, K=S, tm=DIN // 2, outs=[sds((DIN, D), BF16)])
    sib_win = rs_sibling(dwin_t.reshape(NCHIP, din4, D), "w_in")
    (dwkv,) = _matmul(mem_n, dkv, name="mm_dwkv", ta=True, M=D, N=2 * DM, K=NMEM, outs=[sds((D, 2 * DM), BF16)],
                      after=[sib_win[4]])
    sib_wkv = rs_sibling(dwkv.reshape(NCHIP, D // NCHIP, 2 * DM), "w_kv")
    (dh,) = _matmul(dproj, w_in_t, name="mm_dh", M=S, N=D, K=DIN, tk=DIN, outs=[sds((S, D), F32)],
                    after=[sib_wkv[4]])
    rs_win = rs_chips(sib_win, dh, "w_in")
    rs_wkv = rs_chips(sib_wkv, rs_win[4], "w_kv")
    dx, dg_mix = _rms_bwd(dh, x2d, g_mix, dx2, name="rms_mix_bwd", want_bf=False, after=[rs_wkv[4]])
    sh_w1 = rs_end(rs_w1, dx, "w_ffn1")
    (dmem_n,) = _matmul(dkv, w_kv_full, name="mm_dmem", tb=True, M=NMEM, N=D, K=2 * DM, outs=[sds((NMEM, D), F32)],
                        after=[sh_w1[4]])
    (dg_mem,) = _rms_bwd(dmem_n, mem2d, g_mem, None, name="rms_mem_bwd", want_dx=False)
    sh_wo = rs_end(rs_wo, dg_mem, "w_o")
    done = rs_finish(4, sh_w2, sh_wo[4])
    done = rs_finish(3, sh_w1, done)
    sh_win = rs_end(rs_win, done, "w_in")
    sh_wkv = rs_end(rs_wkv, sh_win[4], "w_kv")
    done = rs_finish(2, sh_wo, sh_wkv[4])
    done = rs_finish(0, sh_win, done)
    done = rs_finish(1, sh_wkv, done)

    small_names = ["g_mix", "ln_v_g", "ln_v_b", "w_s", "b_s", "conv_w", "g_mem", "g_head", "g_ffn", "g_final"]
    small_part = [dg_mix, dlng, dlnb, dws, dbs8[:, 0, :], dcw8[:3], dg_mem, dgh, dg_ffn, dg_final, loss11]
    small_shapes = [(1, D), (1, DS), (1, DS), (NSH, CHUNK, CHUNK), (NSH, CHUNK), (3, DC), (1, D), (1, D), (1, D), (1, D),
                    (1, 1)]
    total = _allreduce_small(_pack(small_part), done, name="allreduce_small")
    small_g = _unpack(total, small_shapes)
    loss = small_g.pop()[0, 0]
    small_g[5] = lax.dynamic_slice(small_g[5], (0, shard * dcv4), (3, dcv4))
    small_w = [g_mix, ln_v_g, ln_v_b, ws3, bs2, conv_w[0], g_mem, g_head, g_ffn, g_final2]
    small_m = [m_g_mix, m_ln_v_g, m_ln_v_b, m_w_s[0], m_b_s[0], m_conv_w[0], m_g_mem, m_g_head, m_g_ffn,
               m_g_final.reshape(1, D)]
    small_v = [v_g_mix, v_ln_v_g, v_ln_v_b, v_w_s[0], v_b_s[0], v_conv_w[0], v_g_mem, v_g_head, v_g_ffn,
               v_g_final.reshape(1, D)]
    s_delta, s_m, s_v = _adamw_small(small_w, small_g, small_m, small_v, name="adamw_small")
    small_out = {nm: (g, d, mn, vn) for nm, g, d, mn, vn in zip(small_names, small_g, s_delta, s_m, s_v)}

    order = ["g_mix", "w_in", "ln_v_g", "ln_v_b", "w_s", "b_s", "conv_w", "g_mem", "w_kv", "g_head", "w_o",
             "g_ffn", "w_ffn1", "w_ffn2", "g_final"]
    like = dict(g_mix=g_mix, w_in=w_in, ln_v_g=ln_v_g, ln_v_b=ln_v_b, w_s=w_s, b_s=b_s, conv_w=conv_w, g_mem=g_mem,
                w_kv=w_kv, g_head=g_head, w_o=w_o, g_ffn=g_ffn, w_ffn1=w_ffn1, w_ffn2=w_ffn2, g_final=g_final)
    res = {**big_out, **small_out}
    res["w_in"] = [a.T for a in res["w_in"]]
    outs = [loss, dx[None]]
    for k in range(4):
        outs += [res[nm][k].reshape(like[nm].shape) for nm in order]
    return tuple(outs)
```

```python
import math

import jax
import jax.numpy as jnp
from jax import lax
from jax.experimental import pallas as pl
from jax.experimental.pallas import tpu as pltpu

F32 = jnp.float32
BF16 = jnp.bfloat16
MESH = pl.DeviceIdType.MESH

D = 2048
S = 2048
HD = 128
NH = D // HD
NMH = 4
NSH = (NH - NMH) // 2
NCH = NH - NMH - NSH
DS = NSH * HD
DC = NCH * HD
DM = NMH * HD
DIN = 2 * DS + 3 * DC + DM
CHUNK = 128
NMEM = 256
DFF = 4 * D
EPS = 1e-6
NCHIP = 4
SCALE = HD ** -0.5

ADAM_LR = 0.001
ADAM_B1 = 0.9
ADAM_B2 = 0.999
ADAM_EPS = 1e-08
ADAM_WD = 0.01
ADAM_STEP = 10

TR_EW = 256
TR_MIX = 256
TM = 512
TN = 1024
TK = 2048
N_SUB = 512
VMEM_MB = 56
HALO = 8


def _pick(n, target, q=128):
    best = None
    for t in range(q, min(n, target) + 1, q):
        if n % t == 0:
            best = t
    return n if best is None else best


def _pick_rows(n, q):
    below = _pick(n, TR_EW, q)
    if 2 * below >= TR_EW:
        return below
    above = [t for t in range(TR_EW, min(n, 4 * TR_EW) + 1, q) if n % t == 0]
    return above[0] if above else below


def _cp(sem=None, vmem_mb=None, **kw):
    d = dict(kw)
    if sem is not None:
        d["dimension_semantics"] = sem
    if vmem_mb is not None:
        d["vmem_limit_bytes"] = vmem_mb << 20
    return pltpu.CompilerParams(**d)


def _gelu(x):
    z = 0.7978845608028654 * (x + 0.044715 * (x * x * x))
    return 0.5 * x * (1.0 + jnp.tanh(z))


def _gelu_with_grad(x):
    x2 = x * x
    t = jnp.tanh(0.7978845608028654 * (x + 0.044715 * (x2 * x)))
    half = 0.5 * (1.0 + t)
    return x * half, half + 0.5 * x * (1.0 - t * t) * (0.7978845608028654 * (1.0 + 3.0 * 0.044715 * x2))


def _matmul(a, b, *, name, ta=False, tb=False, M, N, K, tm=None, tn=None, tk=None, outs, epi=None,
            extras=(), a_spec=None, b_spec=None, out_specs=None, after=(), n_split=None, slots=None, into=None):
    n_after = len(after)
    tm = _pick(M, TM if tm is None else tm, 8)
    tn = _pick(N, TN if tn is None else tn)
    tk = _pick(K, TK if tk is None else tk)
    if n_split is None:
        n_split = tn // N_SUB if tn % N_SUB == 0 else 1
    nk = K // tk
    grid = (N // tn, M // tm, nk)
    if a_spec is None:
        a_spec = (pl.BlockSpec((tk, tm), lambda j, i, k, *s: (k, i)) if ta
                  else pl.BlockSpec((tm, tk), lambda j, i, k, *s: (i, k)))
    else:
        a_spec = a_spec(tm, tk)
    if b_spec is None:
        b_spec = (pl.BlockSpec((tn, tk), lambda j, i, k, *s: (j, k)) if tb
                  else pl.BlockSpec((tk, tn), lambda j, i, k, *s: (k, j)))
    else:
        b_spec = b_spec(tn, tk)
    if out_specs is None:
        out_specs = [pl.BlockSpec((tm, tn), lambda j, i, k, *s: (i, j)) for _ in outs]
    else:
        out_specs = out_specs(tm, tn)
    dn = (((0 if ta else 1,), (1 if tb else 0,)), ((), ()))
    n_ex, n_out = len(extras), len(outs)
    n_pre = 0 if slots is None else 1
    n_into = 0 if into is None else 1
    ns = tn // n_split

    def body(*refs):
        a_ref, b_ref = refs[n_pre], refs[n_pre + 1]
        ex = refs[n_pre + 2:n_pre + 2 + n_ex]
        first_out = n_pre + 2 + n_ex + n_after + n_into
        o = refs[first_out:first_out + n_out]
        acc = refs[first_out + n_out:]
        k = pl.program_id(2)

        def finish(val, cols):
            res = (val,) if epi is None else epi(val, *[e[:, cols] for e in ex])
            for r, o_ref in zip(res, o):
                o_ref[:, cols] = r.astype(o_ref.dtype)

        if nk > 1:
            @pl.when(k == 0)
            def _():
                acc[0][...] = jnp.zeros_like(acc[0])

        av = a_ref[...].astype(BF16)
        for q in range(n_split):
            cols = slice(q * ns, (q + 1) * ns)
            bq = (b_ref[cols, :] if tb else b_ref[:, cols]).astype(BF16)
            part = lax.dot_general(av, bq, dn, preferred_element_type=F32)
            if nk == 1:
                finish(part, cols)
            else:
                acc[0][:, cols] += part

        if nk > 1:
            @pl.when(k == nk - 1)
            def _():
                finish(acc[0][...], slice(0, tn))

    in_specs = ([a_spec, b_spec] + [sp(tm, tn) for _, sp in extras] + [ANY] * (n_after + n_into))
    scratch = [pltpu.VMEM((tm, tn), F32)] if nk > 1 else []
    args = [a, b] + [arr for arr, _ in extras] + list(after) + ([] if into is None else [into])
    aliases = {} if into is None else {n_pre + len(args) - 1: 0}
    params = _cp(("parallel", "parallel", "arbitrary"), VMEM_MB)
    if slots is None:
        return pl.pallas_call(body, name=name, grid=grid, in_specs=in_specs, out_specs=out_specs, out_shape=outs,
                              scratch_shapes=scratch, input_output_aliases=aliases, compiler_params=params)(*args)
    return pl.pallas_call(
        body, name=name,
        grid_spec=pltpu.PrefetchScalarGridSpec(num_scalar_prefetch=1, grid=grid, in_specs=in_specs,
                                               out_specs=out_specs, scratch_shapes=scratch),
        out_shape=outs, input_output_aliases=aliases, compiler_params=params)(slots, *args)


def _tile_spec():
    return lambda tm, tn: pl.BlockSpec((tm, tn), lambda j, i, k, *s: (i, j))


def _cast_into_slot(w, slot, after, *, name):
    R, C = w.shape
    tr = _pick_rows(R, 16)

    def body(s_ref, w_ref, _after_ref, o_ref):
        o_ref[...] = w_ref[...].astype(BF16)

    return pl.pallas_call(
        body, name=name,
        grid_spec=pltpu.PrefetchScalarGridSpec(
            num_scalar_prefetch=1, grid=(R // tr,),
            in_specs=[pl.BlockSpec((tr, C), lambda i, s: (i, 0)), ANY],
            out_specs=pl.BlockSpec((None, tr, C), lambda i, s: (s[0], i, 0))),
        out_shape=jax.ShapeDtypeStruct((NCHIP, R, C), BF16),
        compiler_params=_cp(("parallel",), VMEM_MB),
    )(slot, w, after)


def _rms_fwd(x, g, *, name, after=()):
    R, C = x.shape
    tr = _pick(R, TR_EW, 16)
    n_after = len(after)

    def body(x_ref, g_ref, *rest):
        o_ref = rest[n_after]
        xv = x_ref[...]
        r = lax.rsqrt(jnp.mean(xv * xv, axis=-1, keepdims=True) + EPS)
        o_ref[...] = ((xv * r) * g_ref[...]).astype(BF16)

    return pl.pallas_call(
        body, name=name, grid=(R // tr,),
        in_specs=[pl.BlockSpec((tr, C), lambda i: (i, 0)), pl.BlockSpec((1, C), lambda i: (0, 0))] + [ANY] * n_after,
        out_specs=pl.BlockSpec((tr, C), lambda i: (i, 0)),
        out_shape=jax.ShapeDtypeStruct((R, C), BF16),
        compiler_params=_cp(("parallel",), VMEM_MB),
    )(x, g, *after)


def _rms_bwd(dh, x, g, dres, *, name, want_dx=True, want_bf=True, after=()):
    R, C = x.shape
    tr = _pick(R, TR_EW, 16)
    has_res = dres is not None
    row = pl.BlockSpec((tr, C), lambda i: (i, 0))
    vec = pl.BlockSpec((1, C), lambda i: (0, 0))

    def body(*refs):
        dh_ref, x_ref, g_ref = refs[:3]
        pos = 3
        dres_ref = None
        if has_res:
            dres_ref = refs[pos]
            pos += 1
        outs = refs[pos + len(after):]
        i = pl.program_id(0)
        xv = x_ref[...]
        r = lax.rsqrt(jnp.mean(xv * xv, axis=-1, keepdims=True) + EPS)
        xh = xv * r
        dhv = dh_ref[...]
        dg_ref = outs[-1]
        dgp = jnp.sum(dhv * xh, axis=0, keepdims=True)

        @pl.when(i == 0)
        def _():
            dg_ref[...] = dgp

        @pl.when(i > 0)
        def _():
            dg_ref[...] += dgp

        if want_dx:
            t = dhv * g_ref[...]
            dx = r * (t - xh * jnp.mean(t * xh, axis=-1, keepdims=True))
            if has_res:
                dx = dx + dres_ref[...]
            outs[0][...] = dx
            if want_bf:
                outs[1][...] = dx.astype(BF16)

    in_specs = [row, row, vec] + ([row] if has_res else []) + [ANY] * len(after)
    out_specs, out_shape = [], []
    if want_dx:
        out_specs.append(row)
        out_shape.append(jax.ShapeDtypeStruct((R, C), F32))
        if want_bf:
            out_specs.append(row)
            out_shape.append(jax.ShapeDtypeStruct((R, C), BF16))
    out_specs.append(vec)
    out_shape.append(jax.ShapeDtypeStruct((1, C), F32))
    args = [dh, x, g] + ([dres] if has_res else []) + list(after)
    return pl.pallas_call(
        body, name=name, grid=(R // tr,), in_specs=in_specs, out_specs=out_specs, out_shape=out_shape,
        compiler_params=_cp(("arbitrary",), VMEM_MB),
    )(*args)


def _loss_bwd(x3, g, tgt, *, name):
    R, C = x3.shape
    tr = _pick(R, TR_EW, 16)
    n = R // tr
    row = pl.BlockSpec((tr, C), lambda i: (i, 0))
    vec = pl.BlockSpec((1, C), lambda i: (0, 0))

    def body(x_ref, g_ref, t_ref, dx_ref, dxb_ref, dg_ref, loss_ref, acc_ref):
        i = pl.program_id(0)
        xv = x_ref[...]
        gv = g_ref[...]
        r = lax.rsqrt(jnp.mean(xv * xv, axis=-1, keepdims=True) + EPS)
        xh = xv * r
        e = xh * gv - t_ref[...]
        dy = e * (1.0 / C)
        sq = jnp.sum(e * e, axis=0, keepdims=True)
        dgp = jnp.sum(dy * xh, axis=0, keepdims=True)

        @pl.when(i == 0)
        def _():
            acc_ref[...] = sq
            dg_ref[...] = dgp

        @pl.when(i > 0)
        def _():
            acc_ref[...] += sq
            dg_ref[...] += dgp

        t = dy * gv
        dx = r * (t - xh * jnp.mean(t * xh, axis=-1, keepdims=True))
        dx_ref[...] = dx
        dxb_ref[...] = dx.astype(BF16)

        @pl.when(i == n - 1)
        def _():
            loss_ref[...] = jnp.sum(acc_ref[...], axis=-1, keepdims=True) * (0.5 / C)

    return pl.pallas_call(
        body, name=name, grid=(n,),
        in_specs=[row, vec, row],
        out_specs=[row, row, vec, pl.BlockSpec((1, 1), lambda i: (0, 0))],
        out_shape=[jax.ShapeDtypeStruct((R, C), F32), jax.ShapeDtypeStruct((R, C), BF16),
                   jax.ShapeDtypeStruct((1, C), F32), jax.ShapeDtypeStruct((1, 1), F32)],
        scratch_shapes=[pltpu.VMEM((1, C), F32)],
        compiler_params=_cp(("arbitrary",), VMEM_MB),
    )(x3, g, tgt)


def _offsets():
    u0 = 0
    v0 = DS
    b0 = 2 * DS
    c0 = b0 + DC
    x0 = c0 + DC
    q0 = x0 + DC
    return u0, v0, b0, c0, x0, q0


def _tri_mask(lower):
    r = lax.broadcasted_iota(jnp.int32, (CHUNK, CHUNK), 0)
    c = lax.broadcasted_iota(jnp.int32, (CHUNK, CHUNK), 1)
    return (r >= c) if lower else (c >= r)


def _layer_norm_stats(vg):
    mu = jnp.mean(vg, axis=-1, keepdims=True)
    vc = vg - mu
    rstd = lax.rsqrt(jnp.mean(vc * vc, axis=-1, keepdims=True) + EPS)
    return vc * rstd, rstd


def _softmax_rows(qh, kh):
    s = lax.dot_general(qh, kh, (((1,), (1,)), ((), ())), preferred_element_type=F32)
    m = jnp.max(s, axis=-1, keepdims=True)
    e = jnp.exp(s - m)
    return e / jnp.sum(e, axis=-1, keepdims=True)


def _mix_fwd(proj, kv, w_s, bs_t, ln_g, ln_b, conv_w, g_head, *, name):
    assert DS == DC
    tr = _pick(S, TR_MIX, CHUNK)
    n = S // tr
    nck = tr // CHUNK
    u0, v0, b0, c0, x0, q0 = _offsets()
    hb = tr // HALO

    def body(p_ref, cprev_ref, xprev_ref, kv_ref, ws_ref, bst_ref, lng_ref, lnb_ref, cw_ref, gh_ref,
             heads_ref, hn_ref, ycv_ref, buf_ref):
        i = pl.program_id(0)

        def emit(col, val):
            rs = lax.rsqrt(jnp.mean(val * val, axis=-1, keepdims=True) + EPS)
            heads_ref[:, col:col + HD] = val
            hn_ref[:, col:col + HD] = ((val * rs) * gh_ref[:, col:col + HD]).astype(BF16)

        vhat, _ = _layer_norm_stats(_gelu(p_ref[:, v0:v0 + DS]))
        vnb = (vhat * lng_ref[...] + lnb_ref[...]).astype(BF16)
        low = _tri_mask(True)
        for h in range(NSH):
            wt = jnp.where(low, ws_ref[h], 0.0).astype(BF16)
            bcol = bst_ref[:, h:h + 1]
            parts = []
            for c in range(nck):
                blk = vnb[c * CHUNK:(c + 1) * CHUNK, h * HD:(h + 1) * HD]
                parts.append(jnp.dot(wt, blk, preferred_element_type=F32) + bcol)
            mixed = parts[0] if nck == 1 else jnp.concatenate(parts, axis=0)
            emit(h * HD, _gelu(p_ref[:, u0 + h * HD:u0 + (h + 1) * HD]) * mixed)

        xc = p_ref[:, c0:c0 + DC] * p_ref[:, x0:x0 + DC]
        prev = cprev_ref[...] * xprev_ref[...]
        buf_ref[0:HALO, :] = jnp.where(i > 0, prev, 0.0)
        buf_ref[HALO:HALO + tr, :] = xc
        y = (cw_ref[2:3, :] * xc + cw_ref[1:2, :] * buf_ref[HALO - 1:HALO - 1 + tr, :]
             + cw_ref[0:1, :] * buf_ref[HALO - 2:HALO - 2 + tr, :])
        ycv_ref[...] = y
        cout = p_ref[:, b0:b0 + DC] * y
        for h in range(NCH):
            emit(DS + h * HD, cout[:, h * HD:(h + 1) * HD])

        for h in range(NMH):
            qh = (p_ref[:, q0 + h * HD:q0 + (h + 1) * HD] * SCALE).astype(BF16)
            kh = kv_ref[:, h * HD:(h + 1) * HD].astype(BF16)
            vh = kv_ref[:, DM + h * HD:DM + (h + 1) * HD].astype(BF16)
            p = _softmax_rows(qh, kh)
            emit(DS + DC + h * HD, jnp.dot(p.astype(BF16), vh, preferred_element_type=F32))

    full = lambda shape: pl.BlockSpec(shape, lambda i: (0,) * len(shape))
    halo_c = pl.BlockSpec((HALO, DC), lambda i: (jnp.maximum(i * hb - 1, 0), c0 // DC))
    halo_x = pl.BlockSpec((HALO, DC), lambda i: (jnp.maximum(i * hb - 1, 0), x0 // DC))
    return pl.pallas_call(
        body, name=name, grid=(n,),
        in_specs=[pl.BlockSpec((tr, DIN), lambda i: (i, 0)), halo_c, halo_x,
                  full((NMEM, 2 * DM)), full((NSH, CHUNK, CHUNK)), full((CHUNK, NSH)),
                  full((1, DS)), full((1, DS)), full((3, DC)), full((1, D))],
        out_specs=[pl.BlockSpec((tr, D), lambda i: (i, 0)), pl.BlockSpec((tr, D), lambda i: (i, 0)),
                   pl.BlockSpec((tr, DC), lambda i: (i, 0))],
        out_shape=[jax.ShapeDtypeStruct((S, D), F32), jax.ShapeDtypeStruct((S, D), BF16),
                   jax.ShapeDtypeStruct((S, DC), F32)],
        scratch_shapes=[pltpu.VMEM((tr + HALO, DC), F32)],
        compiler_params=_cp(("parallel",), VMEM_MB),
    )(proj, proj, proj, kv, w_s, bs_t, ln_g, ln_b, conv_w, g_head)


def _mix_bwd(dhn, heads, proj, ycv, kv, w_s, bs_t, ln_g, ln_b, conv_w, g_head, after, *, name):
    assert DS == DC
    tr = _pick(S, TR_MIX, CHUNK)
    n = S // tr
    nck = tr // CHUNK
    u0, v0, b0, c0, x0, q0 = _offsets()
    hb = tr // HALO
    last_hb = S // HALO - 1

    def body(dhn_ref, heads_ref, p_ref, ycv_ref, dhn_nx_ref, heads_nx_ref, b_nx_ref, kv_ref, ws_ref, bst_ref,
             lng_ref, lnb_ref, cw_ref, gh_ref, _after_ref,
             dp_ref, dkv_ref, dws_ref, dbs_ref, dlng_ref, dlnb_ref, dcw_ref, dgh_ref, buf_ref, dvn_ref):
        i = pl.program_id(0)

        @pl.when(i == 0)
        def _():
            dkv_ref[...] = jnp.zeros_like(dkv_ref)
            dws_ref[...] = jnp.zeros_like(dws_ref)
            dbs_ref[...] = jnp.zeros_like(dbs_ref)
            dlng_ref[...] = jnp.zeros_like(dlng_ref)
            dlnb_ref[...] = jnp.zeros_like(dlnb_ref)
            dcw_ref[...] = jnp.zeros_like(dcw_ref)
            dgh_ref[...] = jnp.zeros_like(dgh_ref)

        def head_bwd(a, dn, gh):
            rs = lax.rsqrt(jnp.mean(a * a, axis=-1, keepdims=True) + EPS)
            ah = a * rs
            t = dn * gh
            return rs * (t - ah * jnp.mean(t * ah, axis=-1, keepdims=True)), jnp.sum(dn * ah, axis=0, keepdims=True)

        def head_grad(col):
            da, dg = head_bwd(heads_ref[:, col:col + HD], dhn_ref[:, col:col + HD], gh_ref[:, col:col + HD])
            dgh_ref[:, col:col + HD] += dg
            return da

        vg, dvg_dv = _gelu_with_grad(p_ref[:, v0:v0 + DS])
        vhat, rstd = _layer_norm_stats(vg)
        vnb = (vhat * lng_ref[...] + lnb_ref[...]).astype(BF16)
        low = _tri_mask(True)
        ones = jnp.ones((HALO, HD), BF16)
        for h in range(NSH):
            w_h = ws_ref[h]
            wt = jnp.where(low, w_h, 0.0).astype(BF16)
            bcol = bst_ref[:, h:h + 1]
            da = head_grad(h * HD)
            ug, dug_du = _gelu_with_grad(p_ref[:, u0 + h * HD:u0 + (h + 1) * HD])
            dws = jnp.zeros((CHUNK, CHUNK), F32)
            dbs = jnp.zeros((HALO, CHUNK), F32)
            mixed_parts = []
            for c in range(nck):
                rows = slice(c * CHUNK, (c + 1) * CHUNK)
                blk = vnb[rows, h * HD:(h + 1) * HD]
                mixed_parts.append(jnp.dot(wt, blk, preferred_element_type=F32) + bcol)
                dmb = (da[rows] * ug[rows]).astype(BF16)
                dws = dws + lax.dot_general(dmb, blk, (((1,), (1,)), ((), ())), preferred_element_type=F32)
                dbs = dbs + lax.dot_general(ones, dmb, (((1,), (1,)), ((), ())), preferred_element_type=F32)
                dvn_ref[c * CHUNK:(c + 1) * CHUNK, h * HD:(h + 1) * HD] = lax.dot_general(
                    wt, dmb, (((0,), (0,)), ((), ())), preferred_element_type=F32)
            mixed = mixed_parts[0] if nck == 1 else jnp.concatenate(mixed_parts, axis=0)
            dp_ref[:, u0 + h * HD:u0 + (h + 1) * HD] = ((da * mixed) * dug_du).astype(BF16)
            dws_ref[h] += jnp.where(low, dws, 0.0)
            dbs_ref[h] += dbs
        dvn = dvn_ref[...]
        dlng_ref[...] += jnp.sum(dvn * vhat, axis=0, keepdims=True)
        dlnb_ref[...] += jnp.sum(dvn, axis=0, keepdims=True)
        dvh = dvn * lng_ref[...]
        dvg = rstd * (dvh - jnp.mean(dvh, axis=-1, keepdims=True)
                      - vhat * jnp.mean(dvh * vhat, axis=-1, keepdims=True))
        dp_ref[:, v0:v0 + DS] = (dvg * dvg_dv).astype(BF16)

        dc = jnp.concatenate([head_grad(DS + h * HD) for h in range(NCH)], axis=1)
        dc_nx = jnp.concatenate(
            [head_bwd(heads_nx_ref[:, h * HD:(h + 1) * HD], dhn_nx_ref[:, h * HD:(h + 1) * HD],
                      gh_ref[:, DS + h * HD:DS + (h + 1) * HD])[0] for h in range(NCH)], axis=1)
        bg = p_ref[:, b0:b0 + DC]
        cg = p_ref[:, c0:c0 + DC]
        xin = p_ref[:, x0:x0 + DC]
        dp_ref[:, b0:b0 + DC] = (dc * ycv_ref[...]).astype(BF16)
        dyv = dc * bg
        buf_ref[0:tr, :] = dyv
        buf_ref[tr:tr + HALO, :] = jnp.where(i < n - 1, dc_nx * b_nx_ref[...], 0.0)
        sh1 = buf_ref[1:1 + tr, :]
        sh0 = buf_ref[2:2 + tr, :]
        dxc = cw_ref[2:3, :] * dyv + cw_ref[1:2, :] * sh1 + cw_ref[0:1, :] * sh0
        xc = cg * xin
        dp_ref[:, c0:c0 + DC] = (dxc * xin).astype(BF16)
        dp_ref[:, x0:x0 + DC] = (dxc * cg).astype(BF16)
        dcw_ref[0:1, :] += jnp.sum(sh0 * xc, axis=0, keepdims=True)
        dcw_ref[1:2, :] += jnp.sum(sh1 * xc, axis=0, keepdims=True)
        dcw_ref[2:3, :] += jnp.sum(dyv * xc, axis=0, keepdims=True)

        for h in range(NMH):
            do = head_grad(DS + DC + h * HD).astype(BF16)
            qh = (p_ref[:, q0 + h * HD:q0 + (h + 1) * HD] * SCALE).astype(BF16)
            kh = kv_ref[:, h * HD:(h + 1) * HD].astype(BF16)
            vh = kv_ref[:, DM + h * HD:DM + (h + 1) * HD].astype(BF16)
            p = _softmax_rows(qh, kh)
            dpr = lax.dot_general(do, vh, (((1,), (1,)), ((), ())), preferred_element_type=F32)
            ds = (p * (dpr - jnp.sum(dpr * p, axis=-1, keepdims=True))).astype(BF16)
            dp_ref[:, q0 + h * HD:q0 + (h + 1) * HD] = (
                jnp.dot(ds, kh, preferred_element_type=F32) * SCALE).astype(BF16)
            dkv_ref[:, h * HD:(h + 1) * HD] += lax.dot_general(
                ds, qh, (((0,), (0,)), ((), ())), preferred_element_type=F32)
            dkv_ref[:, DM + h * HD:DM + (h + 1) * HD] += lax.dot_general(
                p.astype(BF16), do, (((0,), (0,)), ((), ())), preferred_element_type=F32)

    full = lambda shape: pl.BlockSpec(shape, lambda i: (0,) * len(shape))
    row = lambda c: pl.BlockSpec((tr, c), lambda i: (i, 0))
    nxt = lambda col: pl.BlockSpec((HALO, DC), lambda i: (jnp.minimum((i + 1) * hb, last_hb), col))
    return pl.pallas_call(
        body, name=name, grid=(n,),
        in_specs=[row(D), row(D), row(DIN), row(DC), nxt(DS // DC), nxt(DS // DC), nxt(b0 // DC),
                  full((NMEM, 2 * DM)), full((NSH, CHUNK, CHUNK)), full((CHUNK, NSH)),
                  full((1, DS)), full((1, DS)), full((3, DC)), full((1, D)), ANY],
        out_specs=[row(DIN), full((NMEM, 2 * DM)), full((NSH, CHUNK, CHUNK)), full((NSH, HALO, CHUNK)),
                   full((1, DS)), full((1, DS)), full((HALO, DC)), full((1, D))],
        out_shape=[jax.ShapeDtypeStruct((S, DIN), BF16), jax.ShapeDtypeStruct((NMEM, 2 * DM), F32),
                   jax.ShapeDtypeStruct((NSH, CHUNK, CHUNK), F32), jax.ShapeDtypeStruct((NSH, HALO, CHUNK), F32),
                   jax.ShapeDtypeStruct((1, DS), F32), jax.ShapeDtypeStruct((1, DS), F32),
                   jax.ShapeDtypeStruct((HALO, DC), F32), jax.ShapeDtypeStruct((1, D), F32)],
        scratch_shapes=[pltpu.VMEM((tr + HALO, DC), F32), pltpu.VMEM((tr, DS), F32)],
        compiler_params=_cp(("arbitrary",), VMEM_MB),
    )(dhn, heads, proj, ycv, dhn, heads, proj, kv, w_s, bs_t, ln_g, ln_b, conv_w, g_head, after)


def _place():
    x, y, c = lax.axis_index("x"), lax.axis_index("y"), lax.axis_index("c")
    chips = [(1 - x, y), (x, 1 - y), (1 - x, 1 - y)]
    return x, y, c, chips


ANY = pl.BlockSpec(memory_space=pl.ANY)


HBM = pl.BlockSpec(memory_space=pltpu.HBM)
SEM = pl.BlockSpec(memory_space=pltpu.SEMAPHORE)
EFFECT = pltpu.SideEffectType.DATAFLOW_SIDE_EFFECTING
N_PEER_CHIPS = 3
N_NEIGHBOUR_CHIPS = 2
CONV_PAD = (32, 256)


def _in_hbm(a):
    return pltpu.with_memory_space_constraint(a, pltpu.HBM)


def _allgather_start(bufs, forwards, after, *, name):
    arrs = list(bufs) + list(forwards)
    nw, nb = len(arrs), len(bufs)

    def body(*refs):
        ins, send, recv = refs[:nw], refs[nw + 1:2 * nw + 1], refs[2 * nw + 1:3 * nw + 1]
        token = refs[4 * nw + 1]
        x, y, c, chips = _place()
        s = 2 * x + y
        slots = [2 * cx + cy for cx, cy in chips]
        for w in range(nb, nw):
            q = arrs[w].shape[1] // 4
            for j in range(N_NEIGHBOUR_CHIPS):
                rows = ins[w].at[slots[j], pl.ds(c * 2 * q + j * q, q)]
                pltpu.make_async_remote_copy(src_ref=rows, dst_ref=rows, send_sem=send[w], recv_sem=recv[w],
                                             device_id=(*chips[1 - j], c), device_id_type=MESH).start()
        for w in range(nb):
            hr = arrs[w].shape[1] // 2
            rows = ins[w].at[s, pl.ds(c * hr, hr)]
            for cx, cy in chips[:N_NEIGHBOUR_CHIPS]:
                pltpu.make_async_remote_copy(src_ref=rows, dst_ref=rows, send_sem=send[w], recv_sem=recv[w],
                                             device_id=(cx, cy, c), device_id_type=MESH).start()
        token[...] = jnp.zeros_like(token)

    res = pl.pallas_call(
        body, name=name,
        in_specs=[HBM] * nw + [ANY],
        out_specs=[SEM] * (2 * nw) + [HBM] * nw + [pl.BlockSpec(memory_space=pltpu.VMEM)],
        out_shape=[pltpu.SemaphoreType.DMA(())] * (2 * nw) + [pltpu.HBM(a.shape, a.dtype) for a in arrs]
        + [jax.ShapeDtypeStruct((8, 128), F32)],
        input_output_aliases={w: 2 * nw + w for w in range(nw)},
        compiler_params=pltpu.CompilerParams(has_side_effects=EFFECT),
    )(*[_in_hbm(a) for a in arrs], after)
    return res[:nw], res[nw:2 * nw], res[2 * nw:3 * nw], res[3 * nw]


def _handshake(peers):
    barrier = pltpu.get_barrier_semaphore()
    for peer in peers:
        pl.semaphore_signal(barrier, inc=1, device_id=peer, device_id_type=MESH)
    pl.semaphore_wait(barrier, len(peers))


def _scatter_start(parts, bufs, collective_id, *, name):
    nw = len(parts)

    def body(*refs):
        src, dst = refs[:nw], refs[nw:2 * nw]
        send, recv = refs[2 * nw:3 * nw], refs[3 * nw:4 * nw]
        token = refs[6 * nw]
        x, y, c, chips = _place()
        s = 2 * x + y
        _handshake([(cx, cy, c) for cx, cy in chips])
        for w in range(nw):
            for cx, cy in chips:
                pltpu.make_async_remote_copy(src_ref=src[w].at[2 * cx + cy], dst_ref=dst[w].at[s], send_sem=send[w],
                                             recv_sem=recv[w], device_id=(cx, cy, c), device_id_type=MESH).start()
        token[...] = jnp.zeros_like(token)

    res = pl.pallas_call(
        body, name=name,
        in_specs=[HBM] * (2 * nw),
        out_specs=[SEM] * (2 * nw) + [HBM] * (2 * nw) + [pl.BlockSpec(memory_space=pltpu.VMEM)],
        out_shape=[pltpu.SemaphoreType.DMA(())] * (2 * nw) + [pltpu.HBM(a.shape, a.dtype) for a in parts + bufs]
        + [jax.ShapeDtypeStruct((8, 128), F32)],
        input_output_aliases={k: 2 * nw + k for k in range(2 * nw)},
        compiler_params=pltpu.CompilerParams(has_side_effects=EFFECT, collective_id=collective_id),
    )(*[_in_hbm(a) for a in parts + bufs])
    return res[:nw], res[nw:2 * nw], res[2 * nw:3 * nw], res[3 * nw:4 * nw], res[4 * nw]


def _sibling_start(srcs, whole, collective_id, *, name):
    nw = len(srcs)
    lands = [lax.empty((a.shape[0], a.shape[1] if whole else a.shape[1] // 2, a.shape[2]), a.dtype) for a in srcs]

    def body(*refs):
        src, land = refs[:nw], refs[nw:2 * nw]
        send, recv = refs[2 * nw:3 * nw], refs[3 * nw:4 * nw]
        token = refs[6 * nw]
        x, y, c, _ = _place()
        _handshake([(x, y, 1 - c)])
        for w in range(nw):
            hr = srcs[w].shape[1] // 2
            rows = src[w] if whole else src[w].at[:, pl.ds((1 - c) * hr, hr)]
            pltpu.make_async_remote_copy(src_ref=rows, dst_ref=land[w], send_sem=send[w], recv_sem=recv[w],
                                         device_id=(x, y, 1 - c), device_id_type=MESH).start()
        token[...] = jnp.zeros_like(token)

    res = pl.pallas_call(
        body, name=name,
        in_specs=[HBM] * (2 * nw),
        out_specs=[SEM] * (2 * nw) + [HBM] * (2 * nw) + [pl.BlockSpec(memory_space=pltpu.VMEM)],
        out_shape=[pltpu.SemaphoreType.DMA(())] * (2 * nw) + [pltpu.HBM(a.shape, a.dtype) for a in srcs + lands]
        + [jax.ShapeDtypeStruct((8, 128), F32)],
        input_output_aliases={k: 2 * nw + k for k in range(2 * nw)},
        compiler_params=pltpu.CompilerParams(has_side_effects=EFFECT, collective_id=collective_id),
    )(*[_in_hbm(a) for a in srcs + lands])
    return res[:nw], res[nw:2 * nw], res[2 * nw:3 * nw], res[3 * nw:4 * nw], res[4 * nw]


def _transfer_wait(sends, recvs, thru, sizes, after, *, name):
    n = len(sends)
    flat = [a for group in thru for a in group]

    def body(*refs):
        bufs = refs[:len(flat)]
        send = refs[len(flat):len(flat) + n]
        recv = refs[len(flat) + n:len(flat) + 2 * n]
        token = refs[2 * len(flat) + 2 * n + 1]
        token[...] = jnp.zeros_like(token)
        x, y, c, _ = _place()
        pos = 0
        for k in range(n):
            slots, rows = sizes[k]
            region = bufs[pos].at[pl.ds(0, slots), pl.ds(0, rows)]
            pos += len(thru[k])
            cp = pltpu.make_async_remote_copy(src_ref=region, dst_ref=region, send_sem=send[k], recv_sem=recv[k],
                                              device_id=(x, y, 1 - c), device_id_type=MESH)
            cp.wait_send()
            cp.wait_recv()

    res = pl.pallas_call(
        body, name=name,
        in_specs=[HBM] * len(flat) + [SEM] * (2 * n) + [pl.BlockSpec(memory_space=pl.ANY)],
        out_specs=[HBM] * len(flat) + [pl.BlockSpec(memory_space=pltpu.VMEM)],
        out_shape=[pltpu.HBM(a.shape, a.dtype) for a in flat] + [jax.ShapeDtypeStruct((8, 128), F32)],
        input_output_aliases={k: k for k in range(len(flat))},
        compiler_params=pltpu.CompilerParams(has_side_effects=EFFECT),
    )(*flat, *sends, *recvs, after)
    out, pos = [], 0
    for group in thru:
        out.append(res[pos:pos + len(group)])
        pos += len(group)
    return out, res[len(flat)]


def _forward_gathered(bufs, after, *, name):
    nw = len(bufs)

    def body(*refs):
        outs = refs[nw + 1:2 * nw + 1]
        d_send, d_recv, i_send, i_recv = refs[2 * nw + 1:]
        x, y, c, chips = _place()
        me, sibling = (x, y, c), (x, y, 1 - c)
        slots = [2 * cx + cy for cx, cy in chips]

        def rows(w, j, start, n):
            return outs[w].at[slots[j], pl.ds(start, n)]

        def d2d(w, j, which, to):
            hr = bufs[w].shape[1] // 2
            r = rows(w, j, which * hr, hr)
            return pltpu.make_async_remote_copy(
                src_ref=r, dst_ref=r, send_sem=d_send.at[N_PEER_CHIPS * w + j],
                recv_sem=d_recv.at[N_PEER_CHIPS * w + j], device_id=to, device_id_type=MESH)

        def ici(w, j, slot_j, to):
            q = bufs[w].shape[1] // 4
            r = rows(w, slot_j, c * 2 * q + j * q, q)
            return pltpu.make_async_remote_copy(
                src_ref=r, dst_ref=r, send_sem=i_send.at[N_NEIGHBOUR_CHIPS * w + j],
                recv_sem=i_recv.at[N_NEIGHBOUR_CHIPS * w + j], device_id=to, device_id_type=MESH)

        started = []
        for w in range(nw):
            started += [ici(w, 0, 0, (*chips[1], c)), ici(w, 1, 1, (*chips[0], c))]
            started += [d2d(w, j, c, sibling) for j in range(N_NEIGHBOUR_CHIPS)]
        for cp in started:
            cp.start()
        diag = N_PEER_CHIPS - 1
        for w in range(nw):
            for j in range(N_NEIGHBOUR_CHIPS):
                ici(w, j, diag, me).wait_recv()
            cp = d2d(w, diag, c, sibling)
            cp.start()
            started.append(cp)
        for w in range(nw):
            for j in range(N_PEER_CHIPS):
                d2d(w, j, 1 - c, me).wait_recv()
        for cp in started:
            cp.wait_send()

    return pl.pallas_call(
        body, name=name,
        in_specs=[ANY] * (nw + 1), out_specs=[ANY] * nw,
        out_shape=[jax.ShapeDtypeStruct(a.shape, a.dtype) for a in bufs],
        input_output_aliases={w: w for w in range(nw)},
        scratch_shapes=[pltpu.SemaphoreType.DMA((N_PEER_CHIPS * nw,)), pltpu.SemaphoreType.DMA((N_PEER_CHIPS * nw,)),
                        pltpu.SemaphoreType.DMA((N_NEIGHBOUR_CHIPS * nw,)),
                        pltpu.SemaphoreType.DMA((N_NEIGHBOUR_CHIPS * nw,))],
    )(*bufs, after)


def _forward_halves(bufs, which, after, *, name):
    nw = len(bufs)
    n = len(which)

    def body(*refs):
        outs = refs[nw + 1:2 * nw + 1]
        send, recv = refs[2 * nw + 1:]
        x, y, c, chips = _place()
        me, sibling = (x, y, c), (x, y, 1 - c)

        def d2d(w, t, half, to):
            cx, cy = chips[which[t]]
            hr = bufs[w].shape[1] // 2
            rows = outs[w].at[2 * cx + cy, pl.ds(half * hr, hr)]
            return pltpu.make_async_remote_copy(src_ref=rows, dst_ref=rows, send_sem=send.at[n * w + t],
                                                recv_sem=recv.at[n * w + t], device_id=to, device_id_type=MESH)

        passed = [d2d(w, t, c, sibling) for w in range(nw) for t in range(n)]
        for cp in passed:
            cp.start()
        for w in range(nw):
            for t in range(n):
                d2d(w, t, 1 - c, me).wait_recv()
        for cp in passed:
            cp.wait_send()

    return pl.pallas_call(
        body, name=name,
        in_specs=[ANY] * (nw + 1), out_specs=[ANY] * nw,
        out_shape=[jax.ShapeDtypeStruct(a.shape, a.dtype) for a in bufs],
        input_output_aliases={w: w for w in range(nw)},
        scratch_shapes=[pltpu.SemaphoreType.DMA((n * nw,)), pltpu.SemaphoreType.DMA((n * nw,))],
    )(*bufs, after)


def _allreduce_small(p, after, *, name):
    R = p.shape[0]
    hr = R // 2

    def body(p_ref, _after_ref, out_ref, sib_ref, sum_ref, gat_ref, tot_ref, send, recv):
        x, y, c, chips = _place()
        s = 2 * x + y
        sibling = (x, y, 1 - c)
        rows = pl.ds(pl.multiple_of(c * hr, 8), hr)
        swap = pltpu.make_async_remote_copy(src_ref=p_ref, dst_ref=sib_ref, send_sem=send.at[0], recv_sem=recv.at[0],
                                            device_id=sibling, device_id_type=MESH)
        swap.start()
        swap.wait()
        sum_ref[...] = p_ref[...] + sib_ref[...]
        gat_ref[s] = sum_ref[rows, :]
        cps = [pltpu.make_async_remote_copy(src_ref=sum_ref.at[rows], dst_ref=gat_ref.at[s], send_sem=send.at[1 + j],
                                            recv_sem=recv.at[1 + j], device_id=(cx, cy, c), device_id_type=MESH)
               for j, (cx, cy) in enumerate(chips)]
        for cp in cps:
            cp.start()
        for cp in cps:
            cp.wait()
        tot_ref[...] = ((gat_ref[0] + gat_ref[1]) + gat_ref[2]) + gat_ref[3]
        out_ref[rows, :] = tot_ref[...]
        share = pltpu.make_async_remote_copy(src_ref=tot_ref, dst_ref=out_ref.at[rows], send_sem=send.at[4],
                                             recv_sem=recv.at[4], device_id=sibling, device_id_type=MESH)
        share.start()
        share.wait_send()
        other = out_ref.at[pl.ds(pl.multiple_of((1 - c) * hr, 8), hr)]
        pltpu.make_async_remote_copy(src_ref=other, dst_ref=other, send_sem=send.at[4], recv_sem=recv.at[4],
                                     device_id=(x, y, c), device_id_type=MESH).wait_recv()

    vmem = pl.BlockSpec(memory_space=pltpu.VMEM)
    return pl.pallas_call(
        body, name=name, in_specs=[vmem, ANY], out_specs=vmem,
        out_shape=jax.ShapeDtypeStruct((R, 128), F32),
        scratch_shapes=[pltpu.VMEM((R, 128), F32), pltpu.VMEM((R, 128), F32), pltpu.VMEM((NCHIP, hr, 128), F32),
                        pltpu.VMEM((hr, 128), F32), pltpu.SemaphoreType.DMA((5,)), pltpu.SemaphoreType.DMA((5,))],
    )(p, after)


def _select_half_bf16(g, half, add, slot, *, name):
    _, R, C = g.shape
    hr = R // 2
    tr = _pick_rows(hr, 16)
    nb = hr // tr
    sel = jnp.concatenate([jnp.reshape(half, (1,)).astype(jnp.int32), slot])

    def body(s_ref, g_ref, a_ref, o_ref, own_ref):
        val = (g_ref[...].astype(F32) + a_ref[...].astype(F32)).astype(BF16)
        o_ref[...] = val

        @pl.when(pl.program_id(1) == s_ref[1])
        def _():
            own_ref[...] = val

    g_spec = pl.BlockSpec((None, tr, C), lambda i, j, s: (j, s[0] * nb + i, 0))
    o_spec = pl.BlockSpec((None, tr, C), lambda i, j, s: (j, i, 0))
    own_spec = pl.BlockSpec((None, tr, C), lambda i, j, s: (s[1], i, 0))
    shape = jax.ShapeDtypeStruct((NCHIP, hr, C), BF16)
    return pl.pallas_call(
        body, name=name,
        grid_spec=pltpu.PrefetchScalarGridSpec(
            num_scalar_prefetch=1, grid=(nb, NCHIP), in_specs=[g_spec, o_spec], out_specs=[o_spec, own_spec]),
        out_shape=[shape, shape],
        compiler_params=_cp(("parallel", "arbitrary"), VMEM_MB),
    )(sel, g, add)


def _adamw_math(w, g, m, v):
    m = ADAM_B1 * m + (1.0 - ADAM_B1) * g
    v = ADAM_B2 * v + (1.0 - ADAM_B2) * (g * g)
    m_hat = m / (1.0 - ADAM_B1 ** ADAM_STEP)
    v_hat = v / (1.0 - ADAM_B2 ** ADAM_STEP)
    delta = -ADAM_LR * (m_hat / (jnp.sqrt(v_hat) + ADAM_EPS) + ADAM_WD * w)
    return delta, m, v


def _adamw(w, g_mine, g_sib, m, v, core, *, name):
    R, C = w.shape
    hr = R // 2
    tr = _pick_rows(hr, 16)
    nb = hr // tr
    row = pl.BlockSpec((tr, C), lambda hh, i, c: (hh * nb + i, 0))
    mine = pl.BlockSpec((NCHIP, tr, C), lambda hh, i, c: (0, jnp.where(hh == c[0], i, 0), 0))
    sibs = pl.BlockSpec((NCHIP, tr, C), lambda hh, i, c: (0, jnp.where(hh == c[0], 0, i), 0))

    def slot_sum(ref):
        acc = ref[0].astype(F32) + ref[1].astype(F32)
        for j in range(2, NCHIP):
            acc = acc + ref[j].astype(F32)
        return acc

    def body(c_ref, w_ref, gm_ref, gs_ref, m_ref, v_ref, go_ref, d_ref, mo_ref, vo_ref):
        gv = jnp.where(pl.program_id(0) == c_ref[0], slot_sum(gm_ref), slot_sum(gs_ref))
        d, mn, vn = _adamw_math(w_ref[...], gv, m_ref[...], v_ref[...])
        go_ref[...] = gv
        d_ref[...] = d
        mo_ref[...] = mn
        vo_ref[...] = vn

    return pl.pallas_call(
        body, name=name,
        grid_spec=pltpu.PrefetchScalarGridSpec(
            num_scalar_prefetch=1, grid=(2, nb),
            in_specs=[row, mine, sibs, row, row], out_specs=[row] * 4),
        out_shape=[jax.ShapeDtypeStruct((R, C), F32)] * 4,
        compiler_params=_cp(("parallel", "parallel"), VMEM_MB),
    )(core, w, g_mine, g_sib, m, v)


def _adamw_small(ws, gs, ms, vs, *, name):
    n = len(ws)

    def body(*refs):
        w_r, g_r, m_r, v_r = refs[:n], refs[n:2 * n], refs[2 * n:3 * n], refs[3 * n:4 * n]
        d_r, mo_r, vo_r = refs[4 * n:5 * n], refs[5 * n:6 * n], refs[6 * n:7 * n]
        for k in range(n):
            d, mn, vn = _adamw_math(w_r[k][...], g_r[k][...], m_r[k][...], v_r[k][...])
            d_r[k][...] = d
            mo_r[k][...] = mn
            vo_r[k][...] = vn

    shapes = [jax.ShapeDtypeStruct(w.shape, F32) for w in ws]
    res = pl.pallas_call(body, name=name, out_shape=shapes * 3)(*ws, *gs, *ms, *vs)
    return res[:n], res[n:2 * n], res[2 * n:]


_PACK_ROWS = 8


def _pack(parts):
    rows = []
    for a in parts:
        flat = a.reshape(-1)
        n = -(-flat.shape[0] // (_PACK_ROWS * 128)) * (_PACK_ROWS * 128)
        rows.append(jnp.pad(flat, (0, n - flat.shape[0])).reshape(-1, 128))
    total = sum(r.shape[0] for r in rows)
    if total % 16:
        rows.append(jnp.zeros((16 - total % 16, 128), F32))
    return jnp.concatenate(rows, axis=0)


def _unpack(p, shapes):
    out, r = [], 0
    for shp in shapes:
        n = math.prod(shp)
        nr = -(-n // (_PACK_ROWS * 128)) * _PACK_ROWS
        out.append(p[r:r + nr].reshape(-1)[:n].reshape(shp))
        r += nr
    return out


def kernel(x, mem, g_mix, w_in, ln_v_g, ln_v_b, w_s, b_s, conv_w, g_mem, w_kv, g_head, w_o, g_ffn, w_ffn1, w_ffn2, g_final, loss_target, m_g_mix, m_w_in, m_ln_v_g, m_ln_v_b, m_w_s, m_b_s, m_conv_w, m_g_mem, m_w_kv, m_g_head, m_w_o, m_g_ffn, m_w_ffn1, m_w_ffn2, m_g_final, v_g_mix, v_w_in, v_ln_v_g, v_ln_v_b, v_w_s, v_b_s, v_conv_w, v_g_mem, v_w_kv, v_g_head, v_w_o, v_g_ffn, v_w_ffn1, v_w_ffn2, v_g_final):
    sds = jax.ShapeDtypeStruct
    xi, yi = lax.axis_index("x"), lax.axis_index("y")
    shard = 2 * xi + yi
    x2d, mem2d, tgt = x[0], mem[0], loss_target[0]
    ws3, bs2 = w_s[0], b_s[0]
    g_final2 = g_final.reshape(1, D)
    dff4 = DFF // NCHIP
    din4 = DIN // NCHIP
    dcv4 = DC // NCHIP

    big = [w_in[0].T, w_kv[0], w_o[0], w_ffn1[0], w_ffn2[0]]
    big_names = ["w_in", "w_kv", "w_o", "w_ffn1", "w_ffn2"]
    slot = jnp.reshape(shard, (1,)).astype(jnp.int32)
    core = jnp.reshape(lax.axis_index("c"), (1,)).astype(jnp.int32)
    conv_pad = jnp.pad(conv_w[0], ((0, CONV_PAD[0] - 3), (0, CONV_PAD[1] - dcv4)))
    conv_slots = lax.dynamic_update_slice(jnp.zeros((NCHIP,) + CONV_PAD, F32), conv_pad[None], (shard, 0, 0))

    def gather_start(bufs, after, nm, forwards=()):
        return _allgather_start(bufs, forwards, after, name="ag_start_" + nm)

    def gather_wait(state, idx, after, nm):
        send, recv, bufs, _ = state
        got, token = _transfer_wait([send[k] for k in idx], [recv[k] for k in idx], [[bufs[k]] for k in idx],
                                    [(N_NEIGHBOUR_CHIPS, bufs[k].shape[1] // 2) for k in idx], after, name="ag_wait_" + nm)
        return [g[0] for g in got], token

    cast = lambda k, after: _cast_into_slot(big[k], slot, after, name="cast_" + big_names[k])
    ag_in = gather_start([cast(0, slot), conv_slots], slot, "in")
    bs_t = bs2.T

    h = _rms_fwd(x2d, g_mix, name="rms_mix", after=[ag_in[3]])
    mem_n = _rms_fwd(mem2d, g_mem, name="rms_mem", after=[h])
    kvo_b = [cast(1, mem_n)]
    kvo_b.append(cast(2, kvo_b[0]))
    w1_b = cast(3, kvo_b[1])
    w2_b = cast(4, w1_b)
    got_in, tok = gather_wait(ag_in, [0, 1], w2_b, "in")
    win4, conv4 = _forward_gathered(got_in, tok, name="ag_fwd_in")
    ag_kvo = gather_start(kvo_b, conv4, "kvo")
    w_in_t = win4.reshape(DIN, D)
    conv_full = conv4[:, :3, :dcv4].transpose(1, 0, 2).reshape(3, DC)
    (proj,) = _matmul(h, w_in_t, name="mm_proj", tb=True, M=S, N=DIN, K=D, tn=DIN // 2, outs=[sds((S, DIN), F32)],
                      after=[ag_kvo[3]])
    got_kvo, tok = gather_wait(ag_kvo, [0, 1], proj, "kvo")
    wkv4, wo4 = _forward_gathered(got_kvo, tok, name="ag_fwd_kvo")
    ag_w1 = gather_start([w1_b], wkv4, "ffn1")
    w_kv_full = wkv4.reshape(D, 2 * DM)
    w_o_full = wo4.reshape(D, D)
    (kv,) = _matmul(mem_n, w_kv_full, name="mm_kv", M=NMEM, N=2 * DM, K=D, outs=[sds((NMEM, 2 * DM), F32)],
                    after=[ag_w1[3]])
    heads, hn, ycv = _mix_fwd(proj, kv, ws3, bs_t, ln_v_g, ln_v_b, conv_full, g_head, name="mix_fwd")
    (x2,) = _matmul(hn, w_o_full, name="mm_wo", M=S, N=D, K=D, outs=[sds((S, D), F32)],
                    epi=lambda acc, res: (acc + res,), extras=[(x2d, _tile_spec())])
    h2 = _rms_fwd(x2, g_ffn, name="rms_ffn")
    near = jnp.stack([shard, 2 * (1 - xi) + yi, 2 * xi + (1 - yi)]).astype(jnp.int32)
    far = jnp.reshape(2 * (1 - xi) + (1 - yi), (1,)).astype(jnp.int32)
    NEAR, FAR = [0, 1], [2]

    def diagonal_wait(state, k, after, nm):
        send, recv, bufs, _ = state
        got, token = _transfer_wait([send[k]], [recv[k]], [[bufs[k]]], [(1, bufs[k].shape[1] // 2)], after,
                                    name="ag_waitd_" + nm)
        return got[0][0], token

    w1_shard = lambda tn, tk: pl.BlockSpec((None, tk, tn), lambda j, i, k, s: (s[j], k, 0))
    act_cols = lambda tm, tn: [pl.BlockSpec((tm, tn), lambda j, i, k, s: (i, s[j]))]
    relu2 = lambda acc: (jnp.square(jnp.maximum(acc, 0.0)),)

    got_w1, tok = gather_wait(ag_w1, [0], h2, "ffn1")
    (w1n,) = _forward_halves(got_w1, NEAR, tok, name="ag_fwdn_ffn1")
    ag_w2 = gather_start([w2_b], slot, "ffn2", forwards=[w1n])
    (act,) = _matmul(h2, ag_w2[2][1], name="mm_ffn1_near", M=S, N=3 * dff4, K=D, tn=dff4, b_spec=w1_shard,
                     out_specs=act_cols, outs=[sds((S, DFF), BF16)], epi=relu2, slots=near, after=[ag_w2[3]])
    w1d, tok = diagonal_wait(ag_w2, 1, act, "ffn1")
    (w14,) = _forward_halves([w1d], FAR, tok, name="ag_fwdd_ffn1")
    (act,) = _matmul(h2, w14, name="mm_ffn1_far", M=S, N=dff4, K=D, tn=dff4, b_spec=w1_shard, out_specs=act_cols,
                     outs=[sds((S, DFF), BF16)], epi=relu2, slots=far, into=act)

    act_shard = lambda tm, tk: pl.BlockSpec((tm, tk), lambda j, i, k, s: (i, s[k]))
    w2_shard = lambda tn, tk: pl.BlockSpec((None, tk, tn), lambda j, i, k, s: (s[k], 0, j))
    got_w2, tok = gather_wait(ag_w2, [0], act, "ffn2")
    (w2n,) = _forward_halves(got_w2, NEAR, tok, name="ag_fwdn_ffn2")
    ag_w2d = gather_start([], slot, "ffn2d", forwards=[w2n])
    (x3,) = _matmul(act, ag_w2d[2][0], name="mm_ffn2_near", M=S, N=D, K=3 * dff4, tm=2 * TM, tk=dff4,
                    a_spec=act_shard, b_spec=w2_shard, outs=[sds((S, D), F32)], epi=lambda acc, res: (acc + res,),
                    extras=[(x2, _tile_spec())], slots=near, after=[ag_w2d[3]])
    w2d, tok = diagonal_wait(ag_w2d, 0, x3, "ffn2")
    (w24,) = _forward_halves([w2d], FAR, tok, name="ag_fwdd_ffn2")
    (x3,) = _matmul(act, w24, name="mm_ffn2_far", M=S, N=D, K=dff4, tm=2 * TM, tk=dff4, a_spec=act_shard,
                    b_spec=w2_shard, outs=[sds((S, D), F32)], epi=lambda acc, res: (acc + res,),
                    extras=[(x3, _tile_spec())], slots=far)
    w2_full = w24.reshape(DFF, D)

    ci = lax.axis_index("c")

    def rs_sibling(g4, nm):
        return _sibling_start([g4], False, 1 + big_names.index(nm), name="rs_sib_" + nm)

    def rs_chips(state, after, nm):
        send, recv, g4, land, _ = state
        (((land_, g4_),), _) = _transfer_wait(send, recv, [[land[0], g4[0]]], [(NCHIP, land[0].shape[1])], after,
                                             name="rs_sibwait_" + nm)
        part, buf = _select_half_bf16(g4_, ci, land_, slot, name="rs_add_" + nm)
        return _scatter_start([part], [buf], 1 + 2 * len(big_names) + big_names.index(nm), name="rs_start_" + nm)

    def rs_end(state, after, nm):
        send, recv, parts, bufs, _ = state
        (((buf, _),), _) = _transfer_wait(send, recv, [[bufs[0], parts[0]]], [(N_PEER_CHIPS, bufs[0].shape[1])], after,
                                          name="rs_wait_" + nm)
        return _sibling_start([buf], True, 1 + len(big_names) + big_names.index(nm), name="rs_share_" + nm)

    big_m = [m_w_in[0].T, m_w_kv[0], m_w_o[0], m_w_ffn1[0], m_w_ffn2[0]]
    big_v = [v_w_in[0].T, v_w_kv[0], v_w_o[0], v_w_ffn1[0], v_w_ffn2[0]]
    big_out = {}

    def rs_finish(k, state, after):
        send, recv, mine, land, _ = state
        nm = big_names[k]
        (((land_, mine_),), _) = _transfer_wait(send, recv, [[land[0], mine[0]]], [(NCHIP, land[0].shape[1])], after,
                                               name="rs_sharewait_" + nm)
        big_out[nm] = _adamw(big[k], mine_, land_, big_m[k], big_v[k], core, name="adamw_" + nm)
        return big_out[nm][1]

    dx3, dx3b, dg_final, loss11 = _loss_bwd(x3, g_final2, tgt, name="loss_bwd")
    (dw2,) = _matmul(act, dx3b, name="mm_dw2", ta=True, M=DFF, N=D, K=S, tn=D, outs=[sds((DFF, D), BF16)])
    sib_w2 = rs_sibling(dw2.reshape(NCHIP, dff4, D), "w_ffn2")
    (dfb,) = _matmul(dx3b, w2_full, name="mm_dact", tb=True, M=S, N=DFF, K=D, tn=dff4, outs=[sds((S, DFF), BF16)],
                     epi=lambda acc, a: (acc * (2.0 * jnp.sqrt(a.astype(F32))),), extras=[(act, _tile_spec())],
                     after=[sib_w2[4]])
    rs_w2 = rs_chips(sib_w2, dfb, "w_ffn2")

    def dw1_out(tm, tn):
        nb = dff4 // tn
        return [pl.BlockSpec((None, tm, tn), lambda j, i, k: (j // nb, i, j % nb))]

    (dw1,) = _matmul(h2, dfb, name="mm_dw1", ta=True, M=D, N=DFF, K=S, tn=dff4, outs=[sds((NCHIP, D, dff4), BF16)],
                     out_specs=dw1_out, after=[rs_w2[4]])
    sib_w1 = rs_sibling(dw1, "w_ffn1")

    def w1_rows(tn, tk):
        kb = dff4 // tk
        return pl.BlockSpec((None, tn, tk), lambda j, i, k: (k // kb, j, k % kb))

    (dh2,) = _matmul(dfb, w14, name="mm_dh2", tb=True, M=S, N=D, K=DFF, tm=2 * TM, b_spec=w1_rows,
                     outs=[sds((S, D), F32)], after=[sib_w1[4]])
    rs_w1 = rs_chips(sib_w1, dh2, "w_ffn1")
    dx2, dx2b, dg_ffn = _rms_bwd(dh2, x2, g_ffn, dx3, name="rms_ffn_bwd", after=[rs_w1[4]])
    (dwo,) = _matmul(hn, dx2b, name="mm_dwo", ta=True, M=D, N=D, K=S, outs=[sds((D, D), BF16)])
    sib_wo = rs_sibling(dwo.reshape(NCHIP, D // NCHIP, D), "w_o")
    (dhn,) = _matmul(dx2b, w_o_full, name="mm_dhn", tb=True, M=S, N=D, K=D, outs=[sds((S, D), F32)],
                     after=[sib_wo[4]])
    rs_wo = rs_chips(sib_wo, dhn, "w_o")
    sh_w2 = rs_end(rs_w2, rs_wo[4], "w_ffn2")
    dproj, dkv, dws, dbs8, dlng, dlnb, dcw8, dgh = _mix_bwd(
        dhn, heads, proj, ycv, kv, ws3, bs_t, ln_v_g, ln_v_b, conv_full, g_head, sh_w2[4], name="mix_bwd")
    (dwin_t,) = _matmul(dproj, h, name="mm_dwin", ta=True, M=DIN, N=D, K=S, tm=DIN // 2, outs=[sds((DIN, D), BF16)])
    sib_win = rs_sibling(dwin_t.reshape(NCHIP, din4, D), "w_in")
    (dwkv,) = _matmul(mem_n, dkv, name="mm_dwkv", ta=True, M=D, N=2 * DM, K=NMEM, outs=[sds((D, 2 * DM), BF16)],
                      after=[sib_win[4]])
    sib_wkv = rs_sibling(dwkv.reshape(NCHIP, D // NCHIP, 2 * DM), "w_kv")
    (dh,) = _matmul(dproj, w_in_t, name="mm_dh", M=S, N=D, K=DIN, tk=DIN, outs=[sds((S, D), F32)],
                    after=[sib_wkv[4]])
    rs_win = rs_chips(sib_win, dh, "w_in")
    rs_wkv = rs_chips(sib_wkv, rs_win[4], "w_kv")
    dx, dg_mix = _rms_bwd(dh, x2d, g_mix, dx2, name="rms_mix_bwd", want_bf=False, after=[rs_wkv[4]])
    sh_w1 = rs_end(rs_w1, dx, "w_ffn1")
    (dmem_n,) = _matmul(dkv, w_kv_full, name="mm_dmem", tb=True, M=NMEM, N=D, K=2 * DM, outs=[sds((NMEM, D), F32)],
                        after=[sh_w1[4]])
    (dg_mem,) = _rms_bwd(dmem_n, mem2d, g_mem, None, name="rms_mem_bwd", want_dx=False)
    sh_wo = rs_end(rs_wo, dg_mem, "w_o")
    done = rs_finish(4, sh_w2, sh_wo[4])
    done = rs_finish(3, sh_w1, done)
    sh_win = rs_end(rs_win, done, "w_in")
    sh_wkv = rs_end(rs_wkv, sh_win[4], "w_kv")
    done = rs_finish(2, sh_wo, sh_wkv[4])
    done = rs_finish(0, sh_win, done)
    done = rs_finish(1, sh_wkv, done)

    small_names = ["g_mix", "ln_v_g", "ln_v_b", "w_s", "b_s", "conv_w", "g_mem", "g_head", "g_ffn", "g_final"]
    small_part = [dg_mix, dlng, dlnb, dws, dbs8[:, 0, :], dcw8[:3], dg_mem, dgh, dg_ffn, dg_final, loss11]
    small_shapes = [(1, D), (1, DS), (1, DS), (NSH, CHUNK, CHUNK), (NSH, CHUNK), (3, DC), (1, D), (1, D), (1, D), (1, D),
                    (1, 1)]
    total = _allreduce_small(_pack(small_part), done, name="allreduce_small")
    small_g = _unpack(total, small_shapes)
    loss = small_g.pop()[0, 0]
    small_g[5] = lax.dynamic_slice(small_g[5], (0, shard * dcv4), (3, dcv4))
    small_w = [g_mix, ln_v_g, ln_v_b, ws3, bs2, conv_w[0], g_mem, g_head, g_ffn, g_final2]
    small_m = [m_g_mix, m_ln_v_g, m_ln_v_b, m_w_s[0], m_b_s[0], m_conv_w[0], m_g_mem, m_g_head, m_g_ffn,
               m_g_final.reshape(1, D)]
    small_v = [v_g_mix, v_ln_v_g, v_ln_v_b, v_w_s[0], v_b_s[0], v_conv_w[0], v_g_mem, v_g_head, v_g_ffn,
               v_g_final.reshape(1, D)]
    s_delta, s_m, s_v = _adamw_small(small_w, small_g, small_m, small_v, name="adamw_small")
    small_out = {nm: (g, d, mn, vn) for nm, g, d, mn, vn in zip(small_names, small_g, s_delta, s_m, s_v)}

    order = ["g_mix", "w_in", "ln_v_g", "ln_v_b", "w_s", "b_s", "conv_w", "g_mem", "w_kv", "g_head", "w_o",
             "g_ffn", "w_ffn1", "w_ffn2", "g_final"]
    like = dict(g_mix=g_mix, w_in=w_in, ln_v_g=ln_v_g, ln_v_b=ln_v_b, w_s=w_s, b_s=b_s, conv_w=conv_w, g_mem=g_mem,
                w_kv=w_kv, g_head=g_head, w_o=w_o, g_ffn=g_ffn, w_ffn1=w_ffn1, w_ffn2=w_ffn2, g_final=g_final)
    res = {**big_out, **small_out}
    res["w_in"] = [a.T for a in res["w_in"]]
    outs = [loss, dx[None]]
    for k in range(4):
        outs += [res[nm][k].reshape(like[nm].shape) for nm in order]
    return tuple(outs)
```

```python
import math

import jax
import jax.numpy as jnp
from jax import lax
from jax.experimental import pallas as pl
from jax.experimental.pallas import tpu as pltpu

F32 = jnp.float32
BF16 = jnp.bfloat16
MESH = pl.DeviceIdType.MESH

D = 2048
S = 2048
HD = 128
NH = D // HD
NMH = 4
NSH = (NH - NMH) // 2
NCH = NH - NMH - NSH
DS = NSH * HD
DC = NCH * HD
DM = NMH * HD
DIN = 2 * DS + 3 * DC + DM
CHUNK = 128
NMEM = 256
DFF = 4 * D
EPS = 1e-6
NCHIP = 4
SCALE = HD ** -0.5

ADAM_LR = 0.001
ADAM_B1 = 0.9
ADAM_B2 = 0.999
ADAM_EPS = 1e-08
ADAM_WD = 0.01
ADAM_STEP = 10

TR_EW = 256
TR_MIX = 256
TM = 512
TN = 1024
TK = 2048
N_SUB = 512
VMEM_MB = 56
HALO = 8


def _pick(n, target, q=128):
    best = None
    for t in range(q, min(n, target) + 1, q):
        if n % t == 0:
            best = t
    return n if best is None else best


def _pick_rows(n, q):
    below = _pick(n, TR_EW, q)
    if 2 * below >= TR_EW:
        return below
    above = [t for t in range(TR_EW, min(n, 4 * TR_EW) + 1, q) if n % t == 0]
    return above[0] if above else below


def _cp(sem=None, vmem_mb=None, **kw):
    d = dict(kw)
    if sem is not None:
        d["dimension_semantics"] = sem
    if vmem_mb is not None:
        d["vmem_limit_bytes"] = vmem_mb << 20
    return pltpu.CompilerParams(**d)


def _gelu(x):
    z = 0.7978845608028654 * (x + 0.044715 * (x * x * x))
    return 0.5 * x * (1.0 + jnp.tanh(z))


def _gelu_with_grad(x):
    x2 = x * x
    t = jnp.tanh(0.7978845608028654 * (x + 0.044715 * (x2 * x)))
    half = 0.5 * (1.0 + t)
    return x * half, half + 0.5 * x * (1.0 - t * t) * (0.7978845608028654 * (1.0 + 3.0 * 0.044715 * x2))


def _matmul(a, b, *, name, ta=False, tb=False, M, N, K, tm=None, tn=None, tk=None, outs, epi=None,
            extras=(), a_spec=None, b_spec=None, out_specs=None, after=(), n_split=None, slots=None, into=None):
    n_after = len(after)
    tm = _pick(M, TM if tm is None else tm, 8)
    tn = _pick(N, TN if tn is None else tn)
    tk = _pick(K, TK if tk is None else tk)
    if n_split is None:
        n_split = tn // N_SUB if tn % N_SUB == 0 else 1
    nk = K // tk
    grid = (N // tn, M // tm, nk)
    if a_spec is None:
        a_spec = (pl.BlockSpec((tk, tm), lambda j, i, k, *s: (k, i)) if ta
                  else pl.BlockSpec((tm, tk), lambda j, i, k, *s: (i, k)))
    else:
        a_spec = a_spec(tm, tk)
    if b_spec is None:
        b_spec = (pl.BlockSpec((tn, tk), lambda j, i, k, *s: (j, k)) if tb
                  else pl.BlockSpec((tk, tn), lambda j, i, k, *s: (k, j)))
    else:
        b_spec = b_spec(tn, tk)
    if out_specs is None:
        out_specs = [pl.BlockSpec((tm, tn), lambda j, i, k, *s: (i, j)) for _ in outs]
    else:
        out_specs = out_specs(tm, tn)
    dn = (((0 if ta else 1,), (1 if tb else 0,)), ((), ()))
    n_ex, n_out = len(extras), len(outs)
    n_pre = 0 if slots is None else 1
    n_into = 0 if into is None else 1
    ns = tn // n_split

    def body(*refs):
        a_ref, b_ref = refs[n_pre], refs[n_pre + 1]
        ex = refs[n_pre + 2:n_pre + 2 + n_ex]
        first_out = n_pre + 2 + n_ex + n_after + n_into
        o = refs[first_out:first_out + n_out]
        acc = refs[first_out + n_out:]
        k = pl.program_id(2)

        def finish(val, cols):
            res = (val,) if epi is None else epi(val, *[e[:, cols] for e in ex])
            for r, o_ref in zip(res, o):
                o_ref[:, cols] = r.astype(o_ref.dtype)

        if nk > 1:
            @pl.when(k == 0)
            def _():
                acc[0][...] = jnp.zeros_like(acc[0])

        av = a_ref[...].astype(BF16)
        for q in range(n_split):
            cols = slice(q * ns, (q + 1) * ns)
            bq = (b_ref[cols, :] if tb else b_ref[:, cols]).astype(BF16)
            part = lax.dot_general(av, bq, dn, preferred_element_type=F32)
            if nk == 1:
                finish(part, cols)
            else:
                acc[0][:, cols] += part

        if nk > 1:
            @pl.when(k == nk - 1)
            def _():
                finish(acc[0][...], slice(0, tn))

    in_specs = ([a_spec, b_spec] + [sp(tm, tn) for _, sp in extras] + [ANY] * (n_after + n_into))
    scratch = [pltpu.VMEM((tm, tn), F32)] if nk > 1 else []
    args = [a, b] + [arr for arr, _ in extras] + list(after) + ([] if into is None else [into])
    aliases = {} if into is None else {n_pre + len(args) - 1: 0}
    params = _cp(("parallel", "parallel", "arbitrary"), VMEM_MB)
    if slots is None:
        return pl.pallas_call(body, name=name, grid=grid, in_specs=in_specs, out_specs=out_specs, out_shape=outs,
                              scratch_shapes=scratch, input_output_aliases=aliases, compiler_params=params)(*args)
    return pl.pallas_call(
        body, name=name,
        grid_spec=pltpu.PrefetchScalarGridSpec(num_scalar_prefetch=1, grid=grid, in_specs=in_specs,
                                               out_specs=out_specs, scratch_shapes=scratch),
        out_shape=outs, input_output_aliases=aliases, compiler_params=params)(slots, *args)


def _tile_spec():
    return lambda tm, tn: pl.BlockSpec((tm, tn), lambda j, i, k, *s: (i, j))


def _cast_into_slot(w, slot, after, *, name):
    R, C = w.shape
    tr = _pick_rows(R, 16)

    def body(s_ref, w_ref, _after_ref, o_ref):
        o_ref[...] = w_ref[...].astype(BF16)

    return pl.pallas_call(
        body, name=name,
        grid_spec=pltpu.PrefetchScalarGridSpec(
            num_scalar_prefetch=1, grid=(R // tr,),
            in_specs=[pl.BlockSpec((tr, C), lambda i, s: (i, 0)), ANY],
            out_specs=pl.BlockSpec((None, tr, C), lambda i, s: (s[0], i, 0))),
        out_shape=jax.ShapeDtypeStruct((NCHIP, R, C), BF16),
        compiler_params=_cp(("parallel",), VMEM_MB),
    )(slot, w, after)


def _rms_fwd(x, g, *, name, after=()):
    R, C = x.shape
    tr = _pick(R, TR_EW, 16)
    n_after = len(after)

    def body(x_ref, g_ref, *rest):
        o_ref = rest[n_after]
        xv = x_ref[...]
        r = lax.rsqrt(jnp.mean(xv * xv, axis=-1, keepdims=True) + EPS)
        o_ref[...] = ((xv * r) * g_ref[...]).astype(BF16)

    return pl.pallas_call(
        body, name=name, grid=(R // tr,),
        in_specs=[pl.BlockSpec((tr, C), lambda i: (i, 0)), pl.BlockSpec((1, C), lambda i: (0, 0))] + [ANY] * n_after,
        out_specs=pl.BlockSpec((tr, C), lambda i: (i, 0)),
        out_shape=jax.ShapeDtypeStruct((R, C), BF16),
        compiler_params=_cp(("parallel",), VMEM_MB),
    )(x, g, *after)


def _rms_bwd(dh, x, g, dres, *, name, want_dx=True, want_bf=True, after=()):
    R, C = x.shape
    tr = _pick(R, TR_EW, 16)
    has_res = dres is not None
    row = pl.BlockSpec((tr, C), lambda i: (i, 0))
    vec = pl.BlockSpec((1, C), lambda i: (0, 0))

    def body(*refs):
        dh_ref, x_ref, g_ref = refs[:3]
        pos = 3
        dres_ref = None
        if has_res:
            dres_ref = refs[pos]
            pos += 1
        outs = refs[pos + len(after):]
        i = pl.program_id(0)
        xv = x_ref[...]
        r = lax.rsqrt(jnp.mean(xv * xv, axis=-1, keepdims=True) + EPS)
        xh = xv * r
        dhv = dh_ref[...]
        dg_ref = outs[-1]
        dgp = jnp.sum(dhv * xh, axis=0, keepdims=True)

        @pl.when(i == 0)
        def _():
            dg_ref[...] = dgp

        @pl.when(i > 0)
        def _():
            dg_ref[...] += dgp

        if want_dx:
            t = dhv * g_ref[...]
            dx = r * (t - xh * jnp.mean(t * xh, axis=-1, keepdims=True))
            if has_res:
                dx = dx + dres_ref[...]
            outs[0][...] = dx
            if want_bf:
                outs[1][...] = dx.astype(BF16)

    in_specs = [row, row, vec] + ([row] if has_res else []) + [ANY] * len(after)
    out_specs, out_shape = [], []
    if want_dx:
        out_specs.append(row)
        out_shape.append(jax.ShapeDtypeStruct((R, C), F32))
        if want_bf:
            out_specs.append(row)
            out_shape.append(jax.ShapeDtypeStruct((R, C), BF16))
    out_specs.append(vec)
    out_shape.append(jax.ShapeDtypeStruct((1, C), F32))
    args = [dh, x, g] + ([dres] if has_res else []) + list(after)
    return pl.pallas_call(
        body, name=name, grid=(R // tr,), in_specs=in_specs, out_specs=out_specs, out_shape=out_shape,
        compiler_params=_cp(("arbitrary",), VMEM_MB),
    )(*args)


def _loss_bwd(x3, g, tgt, *, name):
    R, C = x3.shape
    tr = _pick(R, TR_EW, 16)
    n = R // tr
    row = pl.BlockSpec((tr, C), lambda i: (i, 0))
    vec = pl.BlockSpec((1, C), lambda i: (0, 0))

    def body(x_ref, g_ref, t_ref, dx_ref, dxb_ref, dg_ref, loss_ref, acc_ref):
        i = pl.program_id(0)
        xv = x_ref[...]
        gv = g_ref[...]
        r = lax.rsqrt(jnp.mean(xv * xv, axis=-1, keepdims=True) + EPS)
        xh = xv * r
        e = xh * gv - t_ref[...]
        dy = e * (1.0 / C)
        sq = jnp.sum(e * e, axis=0, keepdims=True)
        dgp = jnp.sum(dy * xh, axis=0, keepdims=True)

        @pl.when(i == 0)
        def _():
            acc_ref[...] = sq
            dg_ref[...] = dgp

        @pl.when(i > 0)
        def _():
            acc_ref[...] += sq
            dg_ref[...] += dgp

        t = dy * gv
        dx = r * (t - xh * jnp.mean(t * xh, axis=-1, keepdims=True))
        dx_ref[...] = dx
        dxb_ref[...] = dx.astype(BF16)

        @pl.when(i == n - 1)
        def _():
            loss_ref[...] = jnp.sum(acc_ref[...], axis=-1, keepdims=True) * (0.5 / C)

    return pl.pallas_call(
        body, name=name, grid=(n,),
        in_specs=[row, vec, row],
        out_specs=[row, row, vec, pl.BlockSpec((1, 1), lambda i: (0, 0))],
        out_shape=[jax.ShapeDtypeStruct((R, C), F32), jax.ShapeDtypeStruct((R, C), BF16),
                   jax.ShapeDtypeStruct((1, C), F32), jax.ShapeDtypeStruct((1, 1), F32)],
        scratch_shapes=[pltpu.VMEM((1, C), F32)],
        compiler_params=_cp(("arbitrary",), VMEM_MB),
    )(x3, g, tgt)


def _offsets():
    u0 = 0
    v0 = DS
    b0 = 2 * DS
    c0 = b0 + DC
    x0 = c0 + DC
    q0 = x0 + DC
    return u0, v0, b0, c0, x0, q0


def _tri_mask(lower):
    r = lax.broadcasted_iota(jnp.int32, (CHUNK, CHUNK), 0)
    c = lax.broadcasted_iota(jnp.int32, (CHUNK, CHUNK), 1)
    return (r >= c) if lower else (c >= r)


def _layer_norm_stats(vg):
    mu = jnp.mean(vg, axis=-1, keepdims=True)
    vc = vg - mu
    rstd = lax.rsqrt(jnp.mean(vc * vc, axis=-1, keepdims=True) + EPS)
    return vc * rstd, rstd


def _softmax_rows(qh, kh):
    s = lax.dot_general(qh, kh, (((1,), (1,)), ((), ())), preferred_element_type=F32)
    m = jnp.max(s, axis=-1, keepdims=True)
    e = jnp.exp(s - m)
    return e / jnp.sum(e, axis=-1, keepdims=True)


def _mix_fwd(proj, kv, w_s, bs_t, ln_g, ln_b, conv_w, g_head, *, name):
    assert DS == DC
    tr = _pick(S, TR_MIX, CHUNK)
    n = S // tr
    nck = tr // CHUNK
    u0, v0, b0, c0, x0, q0 = _offsets()
    hb = tr // HALO

    def body(p_ref, cprev_ref, xprev_ref, kv_ref, ws_ref, bst_ref, lng_ref, lnb_ref, cw_ref, gh_ref,
             heads_ref, hn_ref, ycv_ref, buf_ref):
        i = pl.program_id(0)

        def emit(col, val):
            rs = lax.rsqrt(jnp.mean(val * val, axis=-1, keepdims=True) + EPS)
            heads_ref[:, col:col + HD] = val
            hn_ref[:, col:col + HD] = ((val * rs) * gh_ref[:, col:col + HD]).astype(BF16)

        vhat, _ = _layer_norm_stats(_gelu(p_ref[:, v0:v0 + DS]))
        vnb = (vhat * lng_ref[...] + lnb_ref[...]).astype(BF16)
        low = _tri_mask(True)
        for h in range(NSH):
            wt = jnp.where(low, ws_ref[h], 0.0).astype(BF16)
            bcol = bst_ref[:, h:h + 1]
            parts = []
            for c in range(nck):
                blk = vnb[c * CHUNK:(c + 1) * CHUNK, h * HD:(h + 1) * HD]
                parts.append(jnp.dot(wt, blk, preferred_element_type=F32) + bcol)
            mixed = parts[0] if nck == 1 else jnp.concatenate(parts, axis=0)
            emit(h * HD, _gelu(p_ref[:, u0 + h * HD:u0 + (h + 1) * HD]) * mixed)

        xc = p_ref[:, c0:c0 + DC] * p_ref[:, x0:x0 + DC]
        prev = cprev_ref[...] * xprev_ref[...]
        buf_ref[0:HALO, :] = jnp.where(i > 0, prev, 0.0)
        buf_ref[HALO:HALO + tr, :] = xc
        y = (cw_ref[2:3, :] * xc + cw_ref[1:2, :] * buf_ref[HALO - 1:HALO - 1 + tr, :]
             + cw_ref[0:1, :] * buf_ref[HALO - 2:HALO - 2 + tr, :])
        ycv_ref[...] = y
        cout = p_ref[:, b0:b0 + DC] * y
        for h in range(NCH):
            emit(DS + h * HD, cout[:, h * HD:(h + 1) * HD])

        for h in range(NMH):
            qh = (p_ref[:, q0 + h * HD:q0 + (h + 1) * HD] * SCALE).astype(BF16)
            kh = kv_ref[:, h * HD:(h + 1) * HD].astype(BF16)
            vh = kv_ref[:, DM + h * HD:DM + (h + 1) * HD].astype(BF16)
            p = _softmax_rows(qh, kh)
            emit(DS + DC + h * HD, jnp.dot(p.astype(BF16), vh, preferred_element_type=F32))

    full = lambda shape: pl.BlockSpec(shape, lambda i: (0,) * len(shape))
    halo_c = pl.BlockSpec((HALO, DC), lambda i: (jnp.maximum(i * hb - 1, 0), c0 // DC))
    halo_x = pl.BlockSpec((HALO, DC), lambda i: (jnp.maximum(i * hb - 1, 0), x0 // DC))
    return pl.pallas_call(
        body, name=name, grid=(n,),
        in_specs=[pl.BlockSpec((tr, DIN), lambda i: (i, 0)), halo_c, halo_x,
                  full((NMEM, 2 * DM)), full((NSH, CHUNK, CHUNK)), full((CHUNK, NSH)),
                  full((1, DS)), full((1, DS)), full((3, DC)), full((1, D))],
        out_specs=[pl.BlockSpec((tr, D), lambda i: (i, 0)), pl.BlockSpec((tr, D), lambda i: (i, 0)),
                   pl.BlockSpec((tr, DC), lambda i: (i, 0))],
        out_shape=[jax.ShapeDtypeStruct((S, D), F32), jax.ShapeDtypeStruct((S, D), BF16),
                   jax.ShapeDtypeStruct((S, DC), F32)],
        scratch_shapes=[pltpu.VMEM((tr + HALO, DC), F32)],
        compiler_params=_cp(("parallel",), VMEM_MB),
    )(proj, proj, proj, kv, w_s, bs_t, ln_g, ln_b, conv_w, g_head)


def _mix_bwd(dhn, heads, proj, ycv, kv, w_s, bs_t, ln_g, ln_b, conv_w, g_head, after, *, name):
    assert DS == DC
    tr = _pick(S, TR_MIX, CHUNK)
    n = S // tr
    nck = tr // CHUNK
    u0, v0, b0, c0, x0, q0 = _offsets()
    hb = tr // HALO
    last_hb = S // HALO - 1

    def body(dhn_ref, heads_ref, p_ref, ycv_ref, dhn_nx_ref, heads_nx_ref, b_nx_ref, kv_ref, ws_ref, bst_ref,
             lng_ref, lnb_ref, cw_ref, gh_ref, _after_ref,
             dp_ref, dkv_ref, dws_ref, dbs_ref, dlng_ref, dlnb_ref, dcw_ref, dgh_ref, buf_ref, dvn_ref):
        i = pl.program_id(0)

        @pl.when(i == 0)
        def _():
            dkv_ref[...] = jnp.zeros_like(dkv_ref)
            dws_ref[...] = jnp.zeros_like(dws_ref)
            dbs_ref[...] = jnp.zeros_like(dbs_ref)
            dlng_ref[...] = jnp.zeros_like(dlng_ref)
            dlnb_ref[...] = jnp.zeros_like(dlnb_ref)
            dcw_ref[...] = jnp.zeros_like(dcw_ref)
            dgh_ref[...] = jnp.zeros_like(dgh_ref)

        def head_bwd(a, dn, gh):
            rs = lax.rsqrt(jnp.mean(a * a, axis=-1, keepdims=True) + EPS)
            ah = a * rs
            t = dn * gh
            return rs * (t - ah * jnp.mean(t * ah, axis=-1, keepdims=True)), jnp.sum(dn * ah, axis=0, keepdims=True)

        def head_grad(col):
            da, dg = head_bwd(heads_ref[:, col:col + HD], dhn_ref[:, col:col + HD], gh_ref[:, col:col + HD])
            dgh_ref[:, col:col + HD] += dg
            return da

        vg, dvg_dv = _gelu_with_grad(p_ref[:, v0:v0 + DS])
        vhat, rstd = _layer_norm_stats(vg)
        vnb = (vhat * lng_ref[...] + lnb_ref[...]).astype(BF16)
        low = _tri_mask(True)
        ones = jnp.ones((HALO, HD), BF16)
        for h in range(NSH):
            w_h = ws_ref[h]
            wt = jnp.where(low, w_h, 0.0).astype(BF16)
            bcol = bst_ref[:, h:h + 1]
            da = head_grad(h * HD)
            ug, dug_du = _gelu_with_grad(p_ref[:, u0 + h * HD:u0 + (h + 1) * HD])
            dws = jnp.zeros((CHUNK, CHUNK), F32)
            dbs = jnp.zeros((HALO, CHUNK), F32)
            mixed_parts = []
            for c in range(nck):
                rows = slice(c * CHUNK, (c + 1) * CHUNK)
                blk = vnb[rows, h * HD:(h + 1) * HD]
                mixed_parts.append(jnp.dot(wt, blk, preferred_element_type=F32) + bcol)
                dmb = (da[rows] * ug[rows]).astype(BF16)
                dws = dws + lax.dot_general(dmb, blk, (((1,), (1,)), ((), ())), preferred_element_type=F32)
                dbs = dbs + lax.dot_general(ones, dmb, (((1,), (1,)), ((), ())), preferred_element_type=F32)
                dvn_ref[c * CHUNK:(c + 1) * CHUNK, h * HD:(h + 1) * HD] = lax.dot_general(
                    wt, dmb, (((0,), (0,)), ((), ())), preferred_element_type=F32)
            mixed = mixed_parts[0] if nck == 1 else jnp.concatenate(mixed_parts, axis=0)
            dp_ref[:, u0 + h * HD:u0 + (h + 1) * HD] = ((da * mixed) * dug_du).astype(BF16)
            dws_ref[h] += jnp.where(low, dws, 0.0)
            dbs_ref[h] += dbs
        dvn = dvn_ref[...]
        dlng_ref[...] += jnp.sum(dvn * vhat, axis=0, keepdims=True)
        dlnb_ref[...] += jnp.sum(dvn, axis=0, keepdims=True)
        dvh = dvn * lng_ref[...]
        dvg = rstd * (dvh - jnp.mean(dvh, axis=-1, keepdims=True)
                      - vhat * jnp.mean(dvh * vhat, axis=-1, keepdims=True))
        dp_ref[:, v0:v0 + DS] = (dvg * dvg_dv).astype(BF16)

        dc = jnp.concatenate([head_grad(DS + h * HD) for h in range(NCH)], axis=1)
        dc_nx = jnp.concatenate(
            [head_bwd(heads_nx_ref[:, h * HD:(h + 1) * HD], dhn_nx_ref[:, h * HD:(h + 1) * HD],
                      gh_ref[:, DS + h * HD:DS + (h + 1) * HD])[0] for h in range(NCH)], axis=1)
        bg = p_ref[:, b0:b0 + DC]
        cg = p_ref[:, c0:c0 + DC]
        xin = p_ref[:, x0:x0 + DC]
        dp_ref[:, b0:b0 + DC] = (dc * ycv_ref[...]).astype(BF16)
        dyv = dc * bg
        buf_ref[0:tr, :] = dyv
        buf_ref[tr:tr + HALO, :] = jnp.where(i < n - 1, dc_nx * b_nx_ref[...], 0.0)
        sh1 = buf_ref[1:1 + tr, :]
        sh0 = buf_ref[2:2 + tr, :]
        dxc = cw_ref[2:3, :] * dyv + cw_ref[1:2, :] * sh1 + cw_ref[0:1, :] * sh0
        xc = cg * xin
        dp_ref[:, c0:c0 + DC] = (dxc * xin).astype(BF16)
        dp_ref[:, x0:x0 + DC] = (dxc * cg).astype(BF16)
        dcw_ref[0:1, :] += jnp.sum(sh0 * xc, axis=0, keepdims=True)
        dcw_ref[1:2, :] += jnp.sum(sh1 * xc, axis=0, keepdims=True)
        dcw_ref[2:3, :] += jnp.sum(dyv * xc, axis=0, keepdims=True)

        for h in range(NMH):
            do = head_grad(DS + DC + h * HD).astype(BF16)
            qh = (p_ref[:, q0 + h * HD:q0 + (h + 1) * HD] * SCALE).astype(BF16)
            kh = kv_ref[:, h * HD:(h + 1) * HD].astype(BF16)
            vh = kv_ref[:, DM + h * HD:DM + (h + 1) * HD].astype(BF16)
            p = _softmax_rows(qh, kh)
            dpr = lax.dot_general(do, vh, (((1,), (1,)), ((), ())), preferred_element_type=F32)
            ds = (p * (dpr - jnp.sum(dpr * p, axis=-1, keepdims=True))).astype(BF16)
            dp_ref[:, q0 + h * HD:q0 + (h + 1) * HD] = (
                jnp.dot(ds, kh, preferred_element_type=F32) * SCALE).astype(BF16)
            dkv_ref[:, h * HD:(h + 1) * HD] += lax.dot_general(
                ds, qh, (((0,), (0,)), ((), ())), preferred_element_type=F32)
            dkv_ref[:, DM + h * HD:DM + (h + 1) * HD] += lax.dot_general(
                p.astype(BF16), do, (((0,), (0,)), ((), ())), preferred_element_type=F32)

    full = lambda shape: pl.BlockSpec(shape, lambda i: (0,) * len(shape))
    row = lambda c: pl.BlockSpec((tr, c), lambda i: (i, 0))
    nxt = lambda col: pl.BlockSpec((HALO, DC), lambda i: (jnp.minimum((i + 1) * hb, last_hb), col))
    return pl.pallas_call(
        body, name=name, grid=(n,),
        in_specs=[row(D), row(D), row(DIN), row(DC), nxt(DS // DC), nxt(DS // DC), nxt(b0 // DC),
                  full((NMEM, 2 * DM)), full((NSH, CHUNK, CHUNK)), full((CHUNK, NSH)),
                  full((1, DS)), full((1, DS)), full((3, DC)), full((1, D)), ANY],
        out_specs=[row(DIN), full((NMEM, 2 * DM)), full((NSH, CHUNK, CHUNK)), full((NSH, HALO, CHUNK)),
                   full((1, DS)), full((1, DS)), full((HALO, DC)), full((1, D))],
        out_shape=[jax.ShapeDtypeStruct((S, DIN), BF16), jax.ShapeDtypeStruct((NMEM, 2 * DM), F32),
                   jax.ShapeDtypeStruct((NSH, CHUNK, CHUNK), F32), jax.ShapeDtypeStruct((NSH, HALO, CHUNK), F32),
                   jax.ShapeDtypeStruct((1, DS), F32), jax.ShapeDtypeStruct((1, DS), F32),
                   jax.ShapeDtypeStruct((HALO, DC), F32), jax.ShapeDtypeStruct((1, D), F32)],
        scratch_shapes=[pltpu.VMEM((tr + HALO, DC), F32), pltpu.VMEM((tr, DS), F32)],
        compiler_params=_cp(("arbitrary",), VMEM_MB),
    )(dhn, heads, proj, ycv, dhn, heads, proj, kv, w_s, bs_t, ln_g, ln_b, conv_w, g_head, after)


def _place():
    x, y, c = lax.axis_index("x"), lax.axis_index("y"), lax.axis_index("c")
    chips = [(1 - x, y), (x, 1 - y), (1 - x, 1 - y)]
    return x, y, c, chips


ANY = pl.BlockSpec(memory_space=pl.ANY)


HBM = pl.BlockSpec(memory_space=pltpu.HBM)
SEM = pl.BlockSpec(memory_space=pltpu.SEMAPHORE)
EFFECT = pltpu.SideEffectType.DATAFLOW_SIDE_EFFECTING
N_PEER_CHIPS = 3
N_NEIGHBOUR_CHIPS = 2
CONV_PAD = (32, 256)


def _in_hbm(a):
    return pltpu.with_memory_space_constraint(a, pltpu.HBM)


def _allgather_start(bufs, forwards, after, *, name):
    arrs = list(bufs) + list(forwards)
    nw, nb = len(arrs), len(bufs)

    def body(*refs):
        ins, send, recv = refs[:nw], refs[nw + 1:2 * nw + 1], refs[2 * nw + 1:3 * nw + 1]
        token = refs[4 * nw + 1]
        x, y, c, chips = _place()
        s = 2 * x + y
        slots = [2 * cx + cy for cx, cy in chips]
        for w in range(nb, nw):
            q = arrs[w].shape[1] // 4
            for j in range(N_NEIGHBOUR_CHIPS):
                rows = ins[w].at[slots[j], pl.ds(c * 2 * q + j * q, q)]
                pltpu.make_async_remote_copy(src_ref=rows, dst_ref=rows, send_sem=send[w], recv_sem=recv[w],
                                             device_id=(*chips[1 - j], c), device_id_type=MESH).start()
        for w in range(nb):
            hr = arrs[w].shape[1] // 2
            rows = ins[w].at[s, pl.ds(c * hr, hr)]
            for cx, cy in chips[:N_NEIGHBOUR_CHIPS]:
                pltpu.make_async_remote_copy(src_ref=rows, dst_ref=rows, send_sem=send[w], recv_sem=recv[w],
                                             device_id=(cx, cy, c), device_id_type=MESH).start()
        token[...] = jnp.zeros_like(token)

    res = pl.pallas_call(
        body, name=name,
        in_specs=[HBM] * nw + [ANY],
        out_specs=[SEM] * (2 * nw) + [HBM] * nw + [pl.BlockSpec(memory_space=pltpu.VMEM)],
        out_shape=[pltpu.SemaphoreType.DMA(())] * (2 * nw) + [pltpu.HBM(a.shape, a.dtype) for a in arrs]
        + [jax.ShapeDtypeStruct((8, 128), F32)],
        input_output_aliases={w: 2 * nw + w for w in range(nw)},
        compiler_params=pltpu.CompilerParams(has_side_effects=EFFECT),
    )(*[_in_hbm(a) for a in arrs], after)
    return res[:nw], res[nw:2 * nw], res[2 * nw:3 * nw], res[3 * nw]


def _handshake(peers):
    barrier = pltpu.get_barrier_semaphore()
    for peer in peers:
        pl.semaphore_signal(barrier, inc=1, device_id=peer, device_id_type=MESH)
    pl.semaphore_wait(barrier, len(peers))


def _scatter_start(parts, bufs, collective_id, *, name):
    nw = len(parts)

    def body(*refs):
        src, dst = refs[:nw], refs[nw:2 * nw]
        send, recv = refs[2 * nw:3 * nw], refs[3 * nw:4 * nw]
        token = refs[6 * nw]
        x, y, c, chips = _place()
        s = 2 * x + y
        _handshake([(cx, cy, c) for cx, cy in chips])
        for w in range(nw):
            for cx, cy in chips:
                pltpu.make_async_remote_copy(src_ref=src[w].at[2 * cx + cy], dst_ref=dst[w].at[s], send_sem=send[w],
                                             recv_sem=recv[w], device_id=(cx, cy, c), device_id_type=MESH).start()
        token[...] = jnp.zeros_like(token)

    res = pl.pallas_call(
        body, name=name,
        in_specs=[HBM] * (2 * nw),
        out_specs=[SEM] * (2 * nw) + [HBM] * (2 * nw) + [pl.BlockSpec(memory_space=pltpu.VMEM)],
        out_shape=[pltpu.SemaphoreType.DMA(())] * (2 * nw) + [pltpu.HBM(a.shape, a.dtype) for a in parts + bufs]
        + [jax.ShapeDtypeStruct((8, 128), F32)],
        input_output_aliases={k: 2 * nw + k for k in range(2 * nw)},
        compiler_params=pltpu.CompilerParams(has_side_effects=EFFECT, collective_id=collective_id),
    )(*[_in_hbm(a) for a in parts + bufs])
    return res[:nw], res[nw:2 * nw], res[2 * nw:3 * nw], res[3 * nw:4 * nw], res[4 * nw]


def _sibling_start(srcs, whole, collective_id, *, name):
    nw = len(srcs)
    lands = [lax.empty((a.shape[0], a.shape[1] if whole else a.shape[1] // 2, a.shape[2]), a.dtype) for a in srcs]

    def body(*refs):
        src, land = refs[:nw], refs[nw:2 * nw]
        send, recv = refs[2 * nw:3 * nw], refs[3 * nw:4 * nw]
        token = refs[6 * nw]
        x, y, c, _ = _place()
        _handshake([(x, y, 1 - c)])
        for w in range(nw):
            hr = srcs[w].shape[1] // 2
            rows = src[w] if whole else src[w].at[:, pl.ds((1 - c) * hr, hr)]
            pltpu.make_async_remote_copy(src_ref=rows, dst_ref=land[w], send_sem=send[w], recv_sem=recv[w],
                                         device_id=(x, y, 1 - c), device_id_type=MESH).start()
        token[...] = jnp.zeros_like(token)

    res = pl.pallas_call(
        body, name=name,
        in_specs=[HBM] * (2 * nw),
        out_specs=[SEM] * (2 * nw) + [HBM] * (2 * nw) + [pl.BlockSpec(memory_space=pltpu.VMEM)],
        out_shape=[pltpu.SemaphoreType.DMA(())] * (2 * nw) + [pltpu.HBM(a.shape, a.dtype) for a in srcs + lands]
        + [jax.ShapeDtypeStruct((8, 128), F32)],
        input_output_aliases={k: 2 * nw + k for k in range(2 * nw)},
        compiler_params=pltpu.CompilerParams(has_side_effects=EFFECT, collective_id=collective_id),
    )(*[_in_hbm(a) for a in srcs + lands])
    return res[:nw], res[nw:2 * nw], res[2 * nw:3 * nw], res[3 * nw:4 * nw], res[4 * nw]


def _transfer_wait(sends, recvs, thru, sizes, after, *, name):
    n = len(sends)
    flat = [a for group in thru for a in group]

    def body(*refs):
        bufs = refs[:len(flat)]
        send = refs[len(flat):len(flat) + n]
        recv = refs[len(flat) + n:len(flat) + 2 * n]
        token = refs[2 * len(flat) + 2 * n + 1]
        token[...] = jnp.zeros_like(token)
        x, y, c, _ = _place()
        pos = 0
        for k in range(n):
            slots, rows = sizes[k]
            region = bufs[pos].at[pl.ds(0, slots), pl.ds(0, rows)]
            pos += len(thru[k])
            cp = pltpu.make_async_remote_copy(src_ref=region, dst_ref=region, send_sem=send[k], recv_sem=recv[k],
                                              device_id=(x, y, 1 - c), device_id_type=MESH)
            cp.wait_send()
            cp.wait_recv()

    res = pl.pallas_call(
        body, name=name,
        in_specs=[HBM] * len(flat) + [SEM] * (2 * n) + [pl.BlockSpec(memory_space=pl.ANY)],
        out_specs=[HBM] * len(flat) + [pl.BlockSpec(memory_space=pltpu.VMEM)],
        out_shape=[pltpu.HBM(a.shape, a.dtype) for a in flat] + [jax.ShapeDtypeStruct((8, 128), F32)],
        input_output_aliases={k: k for k in range(len(flat))},
        compiler_params=pltpu.CompilerParams(has_side_effects=EFFECT),
    )(*flat, *sends, *recvs, after)
    out, pos = [], 0
    for group in thru:
        out.append(res[pos:pos + len(group)])
        pos += len(group)
    return out, res[len(flat)]


def _forward_gathered(bufs, after, *, name):
    nw = len(bufs)

    def body(*refs):
        outs = refs[nw + 1:2 * nw + 1]
        d_send, d_recv, i_send, i_recv = refs[2 * nw + 1:]
        x, y, c, chips = _place()
        me, sibling = (x, y, c), (x, y, 1 - c)
        slots = [2 * cx + cy for cx, cy in chips]

        def rows(w, j, start, n):
            return outs[w].at[slots[j], pl.ds(start, n)]

        def d2d(w, j, which, to):
            hr = bufs[w].shape[1] // 2
            r = rows(w, j, which * hr, hr)
            return pltpu.make_async_remote_copy(
                src_ref=r, dst_ref=r, send_sem=d_send.at[N_PEER_CHIPS * w + j],
                recv_sem=d_recv.at[N_PEER_CHIPS * w + j], device_id=to, device_id_type=MESH)

        def ici(w, j, slot_j, to):
            q = bufs[w].shape[1] // 4
            r = rows(w, slot_j, c * 2 * q + j * q, q)
            return pltpu.make_async_remote_copy(
                src_ref=r, dst_ref=r, send_sem=i_send.at[N_NEIGHBOUR_CHIPS * w + j],
                recv_sem=i_recv.at[N_NEIGHBOUR_CHIPS * w + j], device_id=to, device_id_type=MESH)

        started = []
        for w in range(nw):
            started += [ici(w, 0, 0, (*chips[1], c)), ici(w, 1, 1, (*chips[0], c))]
            started += [d2d(w, j, c, sibling) for j in range(N_NEIGHBOUR_CHIPS)]
        for cp in started:
            cp.start()
        diag = N_PEER_CHIPS - 1
        for w in range(nw):
            for j in range(N_NEIGHBOUR_CHIPS):
                ici(w, j, diag, me).wait_recv()
            cp = d2d(w, diag, c, sibling)
            cp.start()
            started.append(cp)
        for w in range(nw):
            for j in range(N_PEER_CHIPS):
                d2d(w, j, 1 - c, me).wait_recv()
        for cp in started:
            cp.wait_send()

    return pl.pallas_call(
        body, name=name,
        in_specs=[ANY] * (nw + 1), out_specs=[ANY] * nw,
        out_shape=[jax.ShapeDtypeStruct(a.shape, a.dtype) for a in bufs],
        input_output_aliases={w: w for w in range(nw)},
        scratch_shapes=[pltpu.SemaphoreType.DMA((N_PEER_CHIPS * nw,)), pltpu.SemaphoreType.DMA((N_PEER_CHIPS * nw,)),
                        pltpu.SemaphoreType.DMA((N_NEIGHBOUR_CHIPS * nw,)),
                        pltpu.SemaphoreType.DMA((N_NEIGHBOUR_CHIPS * nw,))],
    )(*bufs, after)


def _forward_halves(bufs, which, after, *, name):
    nw = len(bufs)
    n = len(which)

    def body(*refs):
        outs = refs[nw + 1:2 * nw + 1]
        send, recv = refs[2 * nw + 1:]
        x, y, c, chips = _place()
        me, sibling = (x, y, c), (x, y, 1 - c)

        def d2d(w, t, half, to):
            cx, cy = chips[which[t]]
            hr = bufs[w].shape[1] // 2
            rows = outs[w].at[2 * cx + cy, pl.ds(half * hr, hr)]
            return pltpu.make_async_remote_copy(src_ref=rows, dst_ref=rows, send_sem=send.at[n * w + t],
                                                recv_sem=recv.at[n * w + t], device_id=to, device_id_type=MESH)

        passed = [d2d(w, t, c, sibling) for w in range(nw) for t in range(n)]
        for cp in passed:
            cp.start()
        for w in range(nw):
            for t in range(n):
                d2d(w, t, 1 - c, me).wait_recv()
        for cp in passed:
            cp.wait_send()

    return pl.pallas_call(
        body, name=name,
        in_specs=[ANY] * (nw + 1), out_specs=[ANY] * nw,
        out_shape=[jax.ShapeDtypeStruct(a.shape, a.dtype) for a in bufs],
        input_output_aliases={w: w for w in range(nw)},
        scratch_shapes=[pltpu.SemaphoreType.DMA((n * nw,)), pltpu.SemaphoreType.DMA((n * nw,))],
    )(*bufs, after)


def _allreduce_small(p, after, *, name):
    R = p.shape[0]
    hr = R // 2

    def body(p_ref, _after_ref, out_ref, sib_ref, sum_ref, gat_ref, tot_ref, send, recv):
        x, y, c, chips = _place()
        s = 2 * x + y
        sibling = (x, y, 1 - c)
        rows = pl.ds(pl.multiple_of(c * hr, 8), hr)
        swap = pltpu.make_async_remote_copy(src_ref=p_ref, dst_ref=sib_ref, send_sem=send.at[0], recv_sem=recv.at[0],
                                            device_id=sibling, device_id_type=MESH)
        swap.start()
        swap.wait()
        sum_ref[...] = p_ref[...] + sib_ref[...]
        gat_ref[s] = sum_ref[rows, :]
        cps = [pltpu.make_async_remote_copy(src_ref=sum_ref.at[rows], dst_ref=gat_ref.at[s], send_sem=send.at[1 + j],
                                            recv_sem=recv.at[1 + j], device_id=(cx, cy, c), device_id_type=MESH)
               for j, (cx, cy) in enumerate(chips)]
        for cp in cps:
            cp.start()
        for cp in cps:
            cp.wait()
        tot_ref[...] = ((gat_ref[0] + gat_ref[1]) + gat_ref[2]) + gat_ref[3]
        out_ref[rows, :] = tot_ref[...]
        share = pltpu.make_async_remote_copy(src_ref=tot_ref, dst_ref=out_ref.at[rows], send_sem=send.at[4],
                                             recv_sem=recv.at[4], device_id=sibling, device_id_type=MESH)
        share.start()
        share.wait_send()
        other = out_ref.at[pl.ds(pl.multiple_of((1 - c) * hr, 8), hr)]
        pltpu.make_async_remote_copy(src_ref=other, dst_ref=other, send_sem=send.at[4], recv_sem=recv.at[4],
                                     device_id=(x, y, c), device_id_type=MESH).wait_recv()

    vmem = pl.BlockSpec(memory_space=pltpu.VMEM)
    return pl.pallas_call(
        body, name=name, in_specs=[vmem, ANY], out_specs=vmem,
        out_shape=jax.ShapeDtypeStruct((R, 128), F32),
        scratch_shapes=[pltpu.VMEM((R, 128), F32), pltpu.VMEM((R, 128), F32), pltpu.VMEM((NCHIP, hr, 128), F32),
                        pltpu.VMEM((hr, 128), F32), pltpu.SemaphoreType.DMA((5,)), pltpu.SemaphoreType.DMA((5,))],
    )(p, after)


def _select_half_bf16(g, half, add, slot, *, name):
    _, R, C = g.shape
    hr = R // 2
    tr = _pick_rows(hr, 16)
    nb = hr // tr
    sel = jnp.concatenate([jnp.reshape(half, (1,)).astype(jnp.int32), slot])

    def body(s_ref, g_ref, a_ref, o_ref, own_ref):
        val = (g_ref[...].astype(F32) + a_ref[...].astype(F32)).astype(BF16)
        o_ref[...] = val

        @pl.when(pl.program_id(1) == s_ref[1])
        def _():
            own_ref[...] = val

    g_spec = pl.BlockSpec((None, tr, C), lambda i, j, s: (j, s[0] * nb + i, 0))
    o_spec = pl.BlockSpec((None, tr, C), lambda i, j, s: (j, i, 0))
    own_spec = pl.BlockSpec((None, tr, C), lambda i, j, s: (s[1], i, 0))
    shape = jax.ShapeDtypeStruct((NCHIP, hr, C), BF16)
    return pl.pallas_call(
        body, name=name,
        grid_spec=pltpu.PrefetchScalarGridSpec(
            num_scalar_prefetch=1, grid=(nb, NCHIP), in_specs=[g_spec, o_spec], out_specs=[o_spec, own_spec]),
        out_shape=[shape, shape],
        compiler_params=_cp(("parallel", "arbitrary"), VMEM_MB),
    )(sel, g, add)


def _adamw_math(w, g, m, v):
    m = ADAM_B1 * m + (1.0 - ADAM_B1) * g
    v = ADAM_B2 * v + (1.0 - ADAM_B2) * (g * g)
    m_hat = m / (1.0 - ADAM_B1 ** ADAM_STEP)
    v_hat = v / (1.0 - ADAM_B2 ** ADAM_STEP)
    delta = -ADAM_LR * (m_hat / (jnp.sqrt(v_hat) + ADAM_EPS) + ADAM_WD * w)
    return delta, m, v


def _adamw(w, g_mine, g_sib, m, v, core, *, name):
    R, C = w.shape
    hr = R // 2
    tr = _pick_rows(hr, 16)
    nb = hr // tr
    row = pl.BlockSpec((tr, C), lambda hh, i, c: (hh * nb + i, 0))
    mine = pl.BlockSpec((NCHIP, tr, C), lambda hh, i, c: (0, jnp.where(hh == c[0], i, 0), 0))
    sibs = pl.BlockSpec((NCHIP, tr, C), lambda hh, i, c: (0, jnp.where(hh == c[0], 0, i), 0))

    def slot_sum(ref):
        acc = ref[0].astype(F32) + ref[1].astype(F32)
        for j in range(2, NCHIP):
            acc = acc + ref[j].astype(F32)
        return acc

    def body(c_ref, w_ref, gm_ref, gs_ref, m_ref, v_ref, go_ref, d_ref, mo_ref, vo_ref):
        gv = jnp.where(pl.program_id(0) == c_ref[0], slot_sum(gm_ref), slot_sum(gs_ref))
        d, mn, vn = _adamw_math(w_ref[...], gv, m_ref[...], v_ref[...])
        go_ref[...] = gv
        d_ref[...] = d
        mo_ref[...] = mn
        vo_ref[...] = vn

    return pl.pallas_call(
        body, name=name,
        grid_spec=pltpu.PrefetchScalarGridSpec(
            num_scalar_prefetch=1, grid=(2, nb),
            in_specs=[row, mine, sibs, row, row], out_specs=[row] * 4),
        out_shape=[jax.ShapeDtypeStruct((R, C), F32)] * 4,
        compiler_params=_cp(("parallel", "parallel"), VMEM_MB),
    )(core, w, g_mine, g_sib, m, v)


def _adamw_small(ws, gs, ms, vs, *, name):
    n = len(ws)

    def body(*refs):
        w_r, g_r, m_r, v_r = refs[:n], refs[n:2 * n], refs[2 * n:3 * n], refs[3 * n:4 * n]
        d_r, mo_r, vo_r = refs[4 * n:5 * n], refs[5 * n:6 * n], refs[6 * n:7 * n]
        for k in range(n):
            d, mn, vn = _adamw_math(w_r[k][...], g_r[k][...], m_r[k][...], v_r[k][...])
            d_r[k][...] = d
            mo_r[k][...] = mn
            vo_r[k][...] = vn

    shapes = [jax.ShapeDtypeStruct(w.shape, F32) for w in ws]
    res = pl.pallas_call(body, name=name, out_shape=shapes * 3)(*ws, *gs, *ms, *vs)
    return res[:n], res[n:2 * n], res[2 * n:]


_PACK_ROWS = 8


def _pack(parts):
    rows = []
    for a in parts:
        flat = a.reshape(-1)
        n = -(-flat.shape[0] // (_PACK_ROWS * 128)) * (_PACK_ROWS * 128)
        rows.append(jnp.pad(flat, (0, n - flat.shape[0])).reshape(-1, 128))
    total = sum(r.shape[0] for r in rows)
    if total % 16:
        rows.append(jnp.zeros((16 - total % 16, 128), F32))
    return jnp.concatenate(rows, axis=0)


def _unpack(p, shapes):
    out, r = [], 0
    for shp in shapes:
        n = math.prod(shp)
        nr = -(-n // (_PACK_ROWS * 128)) * _PACK_ROWS
        out.append(p[r:r + nr].reshape(-1)[:n].reshape(shp))
        r += nr
    return out


def kernel(x, mem, g_mix, w_in, ln_v_g, ln_v_b, w_s, b_s, conv_w, g_mem, w_kv, g_head, w_o, g_ffn, w_ffn1, w_ffn2, g_final, loss_target, m_g_mix, m_w_in, m_ln_v_g, m_ln_v_b, m_w_s, m_b_s, m_conv_w, m_g_mem, m_w_kv, m_g_head, m_w_o, m_g_ffn, m_w_ffn1, m_w_ffn2, m_g_final, v_g_mix, v_w_in, v_ln_v_g, v_ln_v_b, v_w_s, v_b_s, v_conv_w, v_g_mem, v_w_kv, v_g_head, v_w_o, v_g_ffn, v_w_ffn1, v_w_ffn2, v_g_final):
    sds = jax.ShapeDtypeStruct
    xi, yi = lax.axis_index("x"), lax.axis_index("y")
    shard = 2 * xi + yi
    x2d, mem2d, tgt = x[0], mem[0], loss_target[0]
    ws3, bs2 = w_s[0], b_s[0]
    g_final2 = g_final.reshape(1, D)
    dff4 = DFF // NCHIP
    din4 = DIN // NCHIP
    dcv4 = DC // NCHIP

    big = [w_in[0].T, w_kv[0], w_o[0], w_ffn1[0], w_ffn2[0]]
    big_names = ["w_in", "w_kv", "w_o", "w_ffn1", "w_ffn2"]
    slot = jnp.reshape(shard, (1,)).astype(jnp.int32)
    core = jnp.reshape(lax.axis_index("c"), (1,)).astype(jnp.int32)
    conv_pad = jnp.pad(conv_w[0], ((0, CONV_PAD[0] - 3), (0, CONV_PAD[1] - dcv4)))
    conv_slots = lax.dynamic_update_slice(jnp.zeros((NCHIP,) + CONV_PAD, F32), conv_pad[None], (shard, 0, 0))

    def gather_start(bufs, after, nm, forwards=()):
        return _allgather_start(bufs, forwards, after, name="ag_start_" + nm)

    def gather_wait(state, idx, after, nm):
        send, recv, bufs, _ = state
        got, token = _transfer_wait([send[k] for k in idx], [recv[k] for k in idx], [[bufs[k]] for k in idx],
                                    [(N_NEIGHBOUR_CHIPS, bufs[k].shape[1] // 2) for k in idx], after, name="ag_wait_" + nm)
        return [g[0] for g in got], token

    cast = lambda k, after: _cast_into_slot(big[k], slot, after, name="cast_" + big_names[k])
    ag_in = gather_start([cast(0, slot), conv_slots], slot, "in")
    bs_t = bs2.T

    h = _rms_fwd(x2d, g_mix, name="rms_mix", after=[ag_in[3]])
    mem_n = _rms_fwd(mem2d, g_mem, name="rms_mem", after=[h])
    kvo_b = [cast(1, mem_n)]
    kvo_b.append(cast(2, kvo_b[0]))
    w1_b = cast(3, kvo_b[1])
    w2_b = cast(4, w1_b)
    got_in, tok = gather_wait(ag_in, [0, 1], w2_b, "in")
    win4, conv4 = _forward_gathered(got_in, tok, name="ag_fwd_in")
    ag_kvo = gather_start(kvo_b, conv4, "kvo")
    w_in_t = win4.reshape(DIN, D)
    conv_full = conv4[:, :3, :dcv4].transpose(1, 0, 2).reshape(3, DC)
    (proj,) = _matmul(h, w_in_t, name="mm_proj", tb=True, M=S, N=DIN, K=D, tn=DIN // 2, outs=[sds((S, DIN), F32)],
                      after=[ag_kvo[3]])
    got_kvo, tok = gather_wait(ag_kvo, [0, 1], proj, "kvo")
    wkv4, wo4 = _forward_gathered(got_kvo, tok, name="ag_fwd_kvo")
    ag_w1 = gather_start([w1_b], wkv4, "ffn1")
    w_kv_full = wkv4.reshape(D, 2 * DM)
    w_o_full = wo4.reshape(D, D)
    (kv,) = _matmul(mem_n, w_kv_full, name="mm_kv", M=NMEM, N=2 * DM, K=D, outs=[sds((NMEM, 2 * DM), F32)],
                    after=[ag_w1[3]])
    heads, hn, ycv = _mix_fwd(proj, kv, ws3, bs_t, ln_v_g, ln_v_b, conv_full, g_head, name="mix_fwd")
    (x2,) = _matmul(hn, w_o_full, name="mm_wo", M=S, N=D, K=D, outs=[sds((S, D), F32)],
                    epi=lambda acc, res: (acc + res,), extras=[(x2d, _tile_spec())])
    h2 = _rms_fwd(x2, g_ffn, name="rms_ffn")
    near = jnp.stack([shard, 2 * (1 - xi) + yi, 2 * xi + (1 - yi)]).astype(jnp.int32)
    far = jnp.reshape(2 * (1 - xi) + (1 - yi), (1,)).astype(jnp.int32)
    NEAR, FAR = [0, 1], [2]

    def diagonal_wait(state, k, after, nm):
        send, recv, bufs, _ = state
        got, token = _transfer_wait([send[k]], [recv[k]], [[bufs[k]]], [(1, bufs[k].shape[1] // 2)], after,
                                    name="ag_waitd_" + nm)
        return got[0][0], token

    w1_shard = lambda tn, tk: pl.BlockSpec((None, tk, tn), lambda j, i, k, s: (s[j], k, 0))
    act_cols = lambda tm, tn: [pl.BlockSpec((tm, tn), lambda j, i, k, s: (i, s[j]))]
    relu2 = lambda acc: (jnp.square(jnp.maximum(acc, 0.0)),)

    got_w1, tok = gather_wait(ag_w1, [0], h2, "ffn1")
    ag_w2 = gather_start([w2_b], tok, "ffn2", forwards=got_w1)
    (w1n,) = _forward_halves([ag_w2[2][1]], NEAR, ag_w2[3], name="ag_fwdn_ffn1")
    ag_w2 = (ag_w2[0], ag_w2[1], [ag_w2[2][0], w1n], ag_w2[3])
    (act,) = _matmul(h2, w1n, name="mm_ffn1_near", M=S, N=3 * dff4, K=D, tn=dff4, b_spec=w1_shard,
                     out_specs=act_cols, outs=[sds((S, DFF), BF16)], epi=relu2, slots=near)
    w1d, tok = diagonal_wait(ag_w2, 1, act, "ffn1")
    (w14,) = _forward_halves([w1d], FAR, tok, name="ag_fwdd_ffn1")
    (act,) = _matmul(h2, w14, name="mm_ffn1_far", M=S, N=dff4, K=D, tn=dff4, b_spec=w1_shard, out_specs=act_cols,
                     outs=[sds((S, DFF), BF16)], epi=relu2, slots=far, into=act)

    act_shard = lambda tm, tk: pl.BlockSpec((tm, tk), lambda j, i, k, s: (i, s[k]))
    w2_shard = lambda tn, tk: pl.BlockSpec((None, tk, tn), lambda j, i, k, s: (s[k], 0, j))
    got_w2, tok = gather_wait(ag_w2, [0], act, "ffn2")
    ag_w2d = gather_start([], tok, "ffn2d", forwards=got_w2)
    (w2n,) = _forward_halves(ag_w2d[2], NEAR, ag_w2d[3], name="ag_fwdn_ffn2")
    ag_w2d = (ag_w2d[0], ag_w2d[1], [w2n], ag_w2d[3])
    (x3,) = _matmul(act, w2n, name="mm_ffn2_near", M=S, N=D, K=3 * dff4, tm=2 * TM, tk=dff4,
                    a_spec=act_shard, b_spec=w2_shard, outs=[sds((S, D), F32)], epi=lambda acc, res: (acc + res,),
                    extras=[(x2, _tile_spec())], slots=near)
    w2d, tok = diagonal_wait(ag_w2d, 0, x3, "ffn2")
    (w24,) = _forward_halves([w2d], FAR, tok, name="ag_fwdd_ffn2")
    (x3,) = _matmul(act, w24, name="mm_ffn2_far", M=S, N=D, K=dff4, tm=2 * TM, tk=dff4, a_spec=act_shard,
                    b_spec=w2_shard, outs=[sds((S, D), F32)], epi=lambda acc, res: (acc + res,),
                    extras=[(x3, _tile_spec())], slots=far)
    w2_full = w24.reshape(DFF, D)

    ci = lax.axis_index("c")

    def rs_sibling(g4, nm):
        return _sibling_start([g4], False, 1 + big_names.index(nm), name="rs_sib_" + nm)

    def rs_chips(state, after, nm):
        send, recv, g4, land, _ = state
        (((land_, g4_),), _) = _transfer_wait(send, recv, [[land[0], g4[0]]], [(NCHIP, land[0].shape[1])], after,
                                             name="rs_sibwait_" + nm)
        part, buf = _select_half_bf16(g4_, ci, land_, slot, name="rs_add_" + nm)
        return _scatter_start([part], [buf], 1 + 2 * len(big_names) + big_names.index(nm), name="rs_start_" + nm)

    def rs_end(state, after, nm):
        send, recv, parts, bufs, _ = state
        (((buf, _),), _) = _transfer_wait(send, recv, [[bufs[0], parts[0]]], [(N_PEER_CHIPS, bufs[0].shape[1])], after,
                                          name="rs_wait_" + nm)
        return _sibling_start([buf], True, 1 + len(big_names) + big_names.index(nm), name="rs_share_" + nm)

    big_m = [m_w_in[0].T, m_w_kv[0], m_w_o[0], m_w_ffn1[0], m_w_ffn2[0]]
    big_v = [v_w_in[0].T, v_w_kv[0], v_w_o[0], v_w_ffn1[0], v_w_ffn2[0]]
    big_out = {}

    def rs_finish(k, state, after):
        send, recv, mine, land, _ = state
        nm = big_names[k]
        (((land_, mine_),), _) = _transfer_wait(send, recv, [[land[0], mine[0]]], [(NCHIP, land[0].shape[1])], after,
                                               name="rs_sharewait_" + nm)
        big_out[nm] = _adamw(big[k], mine_, land_, big_m[k], big_v[k], core, name="adamw_" + nm)
        return big_out[nm][1]

    dx3, dx3b, dg_final, loss11 = _loss_bwd(x3, g_final2, tgt, name="loss_bwd")
    (dw2,) = _matmul(act, dx3b, name="mm_dw2", ta=True, M=DFF, N=D, K=S, tn=D, outs=[sds((DFF, D), BF16)])
    sib_w2 = rs_sibling(dw2.reshape(NCHIP, dff4, D), "w_ffn2")
    (dfb,) = _matmul(dx3b, w2_full, name="mm_dact", tb=True, M=S, N=DFF, K=D, tn=dff4, outs=[sds((S, DFF), BF16)],
                     epi=lambda acc, a: (acc * (2.0 * jnp.sqrt(a.astype(F32))),), extras=[(act, _tile_spec())],
                     after=[sib_w2[4]])
    rs_w2 = rs_chips(sib_w2, dfb, "w_ffn2")

    def dw1_out(tm, tn):
        nb = dff4 // tn
        return [pl.BlockSpec((None, tm, tn), lambda j, i, k: (j // nb, i, j % nb))]

    (dw1,) = _matmul(h2, dfb, name="mm_dw1", ta=True, M=D, N=DFF, K=S, tn=dff4, outs=[sds((NCHIP, D, dff4), BF16)],
                     out_specs=dw1_out, after=[rs_w2[4]])
    sib_w1 = rs_sibling(dw1, "w_ffn1")

    def w1_rows(tn, tk):
        kb = dff4 // tk
        return pl.BlockSpec((None, tn, tk), lambda j, i, k: (k // kb, j, k % kb))

    (dh2,) = _matmul(dfb, w14, name="mm_dh2", tb=True, M=S, N=D, K=DFF, tm=2 * TM, b_spec=w1_rows,
                     outs=[sds((S, D), F32)], after=[sib_w1[4]])
    rs_w1 = rs_chips(sib_w1, dh2, "w_ffn1")
    dx2, dx2b, dg_ffn = _rms_bwd(dh2, x2, g_ffn, dx3, name="rms_ffn_bwd", after=[rs_w1[4]])
    (dwo,) = _matmul(hn, dx2b, name="mm_dwo", ta=True, M=D, N=D, K=S, outs=[sds((D, D), BF16)])
    sib_wo = rs_sibling(dwo.reshape(NCHIP, D // NCHIP, D), "w_o")
    (dhn,) = _matmul(dx2b, w_o_full, name="mm_dhn", tb=True, M=S, N=D, K=D, outs=[sds((S, D), F32)],
                     after=[sib_wo[4]])
    rs_wo = rs_chips(sib_wo, dhn, "w_o")
    sh_w2 = rs_end(rs_w2, rs_wo[4], "w_ffn2")
    dproj, dkv, dws, dbs8, dlng, dlnb, dcw8, dgh = _mix_bwd(
        dhn, heads, proj, ycv, kv, ws3, bs_t, ln_v_g, ln_v_b, conv_full, g_head, sh_w2[4], name="mix_bwd")
    (dwin_t,) = _matmul(dproj, h, name="mm_dwin", ta=True, M=DIN, N=D, K=S, tm=DIN // 2, outs=[sds((DIN, D), BF16)])
    sib_win = rs_sibling(dwin_t.reshape(NCHIP, din4, D), "w_in")
    (dwkv,) = _matmul(mem_n, dkv, name="mm_dwkv", ta=True, M=D, N=2 * DM, K=NMEM, outs=[sds((D, 2 * DM), BF16)],
                      after=[sib_win[4]])
    sib_wkv = rs_sibling(dwkv.reshape(NCHIP, D // NCHIP, 2 * DM), "w_kv")
    (dh,) = _matmul(dproj, w_in_t, name="mm_dh", M=S, N=D, K=DIN, tk=DIN, outs=[sds((S, D), F32)],
                    after=[sib_wkv[4]])
    rs_win = rs_chips(sib_win, dh, "w_in")
    rs_wkv = rs_chips(sib_wkv, rs_win[4], "w_kv")
    dx, dg_mix = _rms_bwd(dh, x2d, g_mix, dx2, name="rms_mix_bwd", want_bf=False, after=[rs_wkv[4]])
    sh_w1 = rs_end(rs_w1, dx, "w_ffn1")
    (dmem_n,) = _matmul(dkv, w_kv_full, name="mm_dmem", tb=True, M=NMEM, N=D, K=2 * DM, outs=[sds((NMEM, D), F32)],
                        after=[sh_w1[4]])
    (dg_mem,) = _rms_bwd(dmem_n, mem2d, g_mem, None, name="rms_mem_bwd", want_dx=False)
    sh_wo = rs_end(rs_wo, dg_mem, "w_o")
    done = rs_finish(4, sh_w2, sh_wo[4])
    done = rs_finish(3, sh_w1, done)
    sh_win = rs_end(rs_win, done, "w_in")
    sh_wkv = rs_end(rs_wkv, sh_win[4], "w_kv")
    done = rs_finish(2, sh_wo, sh_wkv[4])
    done = rs_finish(0, sh_win, done)
    done = rs_finish(1, sh_wkv, done)

    small_names = ["g_mix", "ln_v_g", "ln_v_b", "w_s", "b_s", "conv_w", "g_mem", "g_head", "g_ffn", "g_final"]
    small_part = [dg_mix, dlng, dlnb, dws, dbs8[:, 0, :], dcw8[:3], dg_mem, dgh, dg_ffn, dg_final, loss11]
    small_shapes = [(1, D), (1, DS), (1, DS), (NSH, CHUNK, CHUNK), (NSH, CHUNK), (3, DC), (1, D), (1, D), (1, D), (1, D),
                    (1, 1)]
    total = _allreduce_small(_pack(small_part), done, name="allreduce_small")
    small_g = _unpack(total, small_shapes)
    loss = small_g.pop()[0, 0]
    small_g[5] = lax.dynamic_slice(small_g[5], (0, shard * dcv4), (3, dcv4))
    small_w = [g_mix, ln_v_g, ln_v_b, ws3, bs2, conv_w[0], g_mem, g_head, g_ffn, g_final2]
    small_m = [m_g_mix, m_ln_v_g, m_ln_v_b, m_w_s[0], m_b_s[0], m_conv_w[0], m_g_mem, m_g_head, m_g_ffn,
               m_g_final.reshape(1, D)]
    small_v = [v_g_mix, v_ln_v_g, v_ln_v_b, v_w_s[0], v_b_s[0], v_conv_w[0], v_g_mem, v_g_head, v_g_ffn,
               v_g_final.reshape(1, D)]
    s_delta, s_m, s_v = _adamw_small(small_w, small_g, small_m, small_v, name="adamw_small")
    small_out = {nm: (g, d, mn, vn) for nm, g, d, mn, vn in zip(small_names, small_g, s_delta, s_m, s_v)}

    order = ["g_mix", "w_in", "ln_v_g", "ln_v_b", "w_s", "b_s", "conv_w", "g_mem", "w_kv", "g_head", "w_o",
             "g_ffn", "w_ffn1", "w_ffn2", "g_final"]
    like = dict(g_mix=g_mix, w_in=w_in, ln_v_g=ln_v_g, ln_v_b=ln_v_b, w_s=w_s, b_s=b_s, conv_w=conv_w, g_mem=g_mem,
                w_kv=w_kv, g_head=g_head, w_o=w_o, g_ffn=g_ffn, w_ffn1=w_ffn1, w_ffn2=w_ffn2, g_final=g_final)
    res = {**big_out, **small_out}
    res["w_in"] = [a.T for a in res["w_in"]]
    outs = [loss, dx[None]]
    for k in range(4):
        outs += [res[nm][k].reshape(like[nm].shape) for nm in order]
    return tuple(outs)
```

```python
import math

import jax
import jax.numpy as jnp
from jax import lax
from jax.experimental import pallas as pl
from jax.experimental.pallas import tpu as pltpu

F32 = jnp.float32
BF16 = jnp.bfloat16
MESH = pl.DeviceIdType.MESH

D = 2048
S = 2048
HD = 128
NH = D // HD
NMH = 4
NSH = (NH - NMH) // 2
NCH = NH - NMH - NSH
DS = NSH * HD
DC = NCH * HD
DM = NMH * HD
DIN = 2 * DS + 3 * DC + DM
CHUNK = 128
NMEM = 256
DFF = 4 * D
EPS = 1e-6
NCHIP = 4
SCALE = HD ** -0.5

ADAM_LR = 0.001
ADAM_B1 = 0.9
ADAM_B2 = 0.999
ADAM_EPS = 1e-08
ADAM_WD = 0.01
ADAM_STEP = 10

TR_EW = 256
TR_MIX = 256
TM = 512
TN = 1024
TK = 2048
N_SUB = 512
VMEM_MB = 56
HALO = 8


def _pick(n, target, q=128):
    best = None
    for t in range(q, min(n, target) + 1, q):
        if n % t == 0:
            best = t
    return n if best is None else best


def _pick_rows(n, q):
    below = _pick(n, TR_EW, q)
    if 2 * below >= TR_EW:
        return below
    above = [t for t in range(TR_EW, min(n, 4 * TR_EW) + 1, q) if n % t == 0]
    return above[0] if above else below


def _cp(sem=None, vmem_mb=None, **kw):
    d = dict(kw)
    if sem is not None:
        d["dimension_semantics"] = sem
    if vmem_mb is not None:
        d["vmem_limit_bytes"] = vmem_mb << 20
    return pltpu.CompilerParams(**d)


def _gelu(x):
    z = 0.7978845608028654 * (x + 0.044715 * (x * x * x))
    return 0.5 * x * (1.0 + jnp.tanh(z))


def _gelu_with_grad(x):
    x2 = x * x
    t = jnp.tanh(0.7978845608028654 * (x + 0.044715 * (x2 * x)))
    half = 0.5 * (1.0 + t)
    return x * half, half + 0.5 * x * (1.0 - t * t) * (0.7978845608028654 * (1.0 + 3.0 * 0.044715 * x2))


def _matmul(a, b, *, name, ta=False, tb=False, M, N, K, tm=None, tn=None, tk=None, outs, epi=None,
            extras=(), a_spec=None, b_spec=None, out_specs=None, after=(), n_split=None, slots=None, into=None):
    n_after = len(after)
    tm = _pick(M, TM if tm is None else tm, 8)
    tn = _pick(N, TN if tn is None else tn)
    tk = _pick(K, TK if tk is None else tk)
    if n_split is None:
        n_split = tn // N_SUB if tn % N_SUB == 0 else 1
    nk = K // tk
    grid = (N // tn, M // tm, nk)
    if a_spec is None:
        a_spec = (pl.BlockSpec((tk, tm), lambda j, i, k, *s: (k, i)) if ta
                  else pl.BlockSpec((tm, tk), lambda j, i, k, *s: (i, k)))
    else:
        a_spec = a_spec(tm, tk)
    if b_spec is None:
        b_spec = (pl.BlockSpec((tn, tk), lambda j, i, k, *s: (j, k)) if tb
                  else pl.BlockSpec((tk, tn), lambda j, i, k, *s: (k, j)))
    else:
        b_spec = b_spec(tn, tk)
    if out_specs is None:
        out_specs = [pl.BlockSpec((tm, tn), lambda j, i, k, *s: (i, j)) for _ in outs]
    else:
        out_specs = out_specs(tm, tn)
    dn = (((0 if ta else 1,), (1 if tb else 0,)), ((), ()))
    n_ex, n_out = len(extras), len(outs)
    n_pre = 0 if slots is None else 1
    n_into = 0 if into is None else 1
    ns = tn // n_split

    def body(*refs):
        a_ref, b_ref = refs[n_pre], refs[n_pre + 1]
        ex = refs[n_pre + 2:n_pre + 2 + n_ex]
        first_out = n_pre + 2 + n_ex + n_after + n_into
        o = refs[first_out:first_out + n_out]
        acc = refs[first_out + n_out:]
        k = pl.program_id(2)

        def finish(val, cols):
            res = (val,) if epi is None else epi(val, *[e[:, cols] for e in ex])
            for r, o_ref in zip(res, o):
                o_ref[:, cols] = r.astype(o_ref.dtype)

        if nk > 1:
            @pl.when(k == 0)
            def _():
                acc[0][...] = jnp.zeros_like(acc[0])

        av = a_ref[...].astype(BF16)
        for q in range(n_split):
            cols = slice(q * ns, (q + 1) * ns)
            bq = (b_ref[cols, :] if tb else b_ref[:, cols]).astype(BF16)
            part = lax.dot_general(av, bq, dn, preferred_element_type=F32)
            if nk == 1:
                finish(part, cols)
            else:
                acc[0][:, cols] += part

        if nk > 1:
            @pl.when(k == nk - 1)
            def _():
                finish(acc[0][...], slice(0, tn))

    in_specs = ([a_spec, b_spec] + [sp(tm, tn) for _, sp in extras] + [ANY] * (n_after + n_into))
    scratch = [pltpu.VMEM((tm, tn), F32)] if nk > 1 else []
    args = [a, b] + [arr for arr, _ in extras] + list(after) + ([] if into is None else [into])
    aliases = {} if into is None else {n_pre + len(args) - 1: 0}
    params = _cp(("parallel", "parallel", "arbitrary"), VMEM_MB)
    if slots is None:
        return pl.pallas_call(body, name=name, grid=grid, in_specs=in_specs, out_specs=out_specs, out_shape=outs,
                              scratch_shapes=scratch, input_output_aliases=aliases, compiler_params=params)(*args)
    return pl.pallas_call(
        body, name=name,
        grid_spec=pltpu.PrefetchScalarGridSpec(num_scalar_prefetch=1, grid=grid, in_specs=in_specs,
                                               out_specs=out_specs, scratch_shapes=scratch),
        out_shape=outs, input_output_aliases=aliases, compiler_params=params)(slots, *args)


def _tile_spec():
    return lambda tm, tn: pl.BlockSpec((tm, tn), lambda j, i, k, *s: (i, j))


def _cast_into_slot(w, slot, after, *, name):
    R, C = w.shape
    tr = _pick_rows(R, 16)

    def body(s_ref, w_ref, _after_ref, o_ref):
        o_ref[...] = w_ref[...].astype(BF16)

    return pl.pallas_call(
        body, name=name,
        grid_spec=pltpu.PrefetchScalarGridSpec(
            num_scalar_prefetch=1, grid=(R // tr,),
            in_specs=[pl.BlockSpec((tr, C), lambda i, s: (i, 0)), ANY],
            out_specs=pl.BlockSpec((None, tr, C), lambda i, s: (s[0], i, 0))),
        out_shape=jax.ShapeDtypeStruct((NCHIP, R, C), BF16),
        compiler_params=_cp(("parallel",), VMEM_MB),
    )(slot, w, after)


def _rms_fwd(x, g, *, name, after=()):
    R, C = x.shape
    tr = _pick(R, TR_EW, 16)
    n_after = len(after)

    def body(x_ref, g_ref, *rest):
        o_ref = rest[n_after]
        xv = x_ref[...]
        r = lax.rsqrt(jnp.mean(xv * xv, axis=-1, keepdims=True) + EPS)
        o_ref[...] = ((xv * r) * g_ref[...]).astype(BF16)

    return pl.pallas_call(
        body, name=name, grid=(R // tr,),
        in_specs=[pl.BlockSpec((tr, C), lambda i: (i, 0)), pl.BlockSpec((1, C), lambda i: (0, 0))] + [ANY] * n_after,
        out_specs=pl.BlockSpec((tr, C), lambda i: (i, 0)),
        out_shape=jax.ShapeDtypeStruct((R, C), BF16),
        compiler_params=_cp(("parallel",), VMEM_MB),
    )(x, g, *after)


def _rms_bwd(dh, x, g, dres, *, name, want_dx=True, want_bf=True, after=()):
    R, C = x.shape
    tr = _pick(R, TR_EW, 16)
    has_res = dres is not None
    row = pl.BlockSpec((tr, C), lambda i: (i, 0))
    vec = pl.BlockSpec((1, C), lambda i: (0, 0))

    def body(*refs):
        dh_ref, x_ref, g_ref = refs[:3]
        pos = 3
        dres_ref = None
        if has_res:
            dres_ref = refs[pos]
            pos += 1
        outs = refs[pos + len(after):]
        i = pl.program_id(0)
        xv = x_ref[...]
        r = lax.rsqrt(jnp.mean(xv * xv, axis=-1, keepdims=True) + EPS)
        xh = xv * r
        dhv = dh_ref[...]
        dg_ref = outs[-1]
        dgp = jnp.sum(dhv * xh, axis=0, keepdims=True)

        @pl.when(i == 0)
        def _():
            dg_ref[...] = dgp

        @pl.when(i > 0)
        def _():
            dg_ref[...] += dgp

        if want_dx:
            t = dhv * g_ref[...]
            dx = r * (t - xh * jnp.mean(t * xh, axis=-1, keepdims=True))
            if has_res:
                dx = dx + dres_ref[...]
            outs[0][...] = dx
            if want_bf:
                outs[1][...] = dx.astype(BF16)

    in_specs = [row, row, vec] + ([row] if has_res else []) + [ANY] * len(after)
    out_specs, out_shape = [], []
    if want_dx:
        out_specs.append(row)
        out_shape.append(jax.ShapeDtypeStruct((R, C), F32))
        if want_bf:
            out_specs.append(row)
            out_shape.append(jax.ShapeDtypeStruct((R, C), BF16))
    out_specs.append(vec)
    out_shape.append(jax.ShapeDtypeStruct((1, C), F32))
    args = [dh, x, g] + ([dres] if has_res else []) + list(after)
    return pl.pallas_call(
        body, name=name, grid=(R // tr,), in_specs=in_specs, out_specs=out_specs, out_shape=out_shape,
        compiler_params=_cp(("arbitrary",), VMEM_MB),
    )(*args)


def _loss_bwd(x3, g, tgt, *, name):
    R, C = x3.shape
    tr = _pick(R, TR_EW, 16)
    n = R // tr
    row = pl.BlockSpec((tr, C), lambda i: (i, 0))
    vec = pl.BlockSpec((1, C), lambda i: (0, 0))

    def body(x_ref, g_ref, t_ref, dx_ref, dxb_ref, dg_ref, loss_ref, acc_ref):
        i = pl.program_id(0)
        xv = x_ref[...]
        gv = g_ref[...]
        r = lax.rsqrt(jnp.mean(xv * xv, axis=-1, keepdims=True) + EPS)
        xh = xv * r
        e = xh * gv - t_ref[...]
        dy = e * (1.0 / C)
        sq = jnp.sum(e * e, axis=0, keepdims=True)
        dgp = jnp.sum(dy * xh, axis=0, keepdims=True)

        @pl.when(i == 0)
        def _():
            acc_ref[...] = sq
            dg_ref[...] = dgp

        @pl.when(i > 0)
        def _():
            acc_ref[...] += sq
            dg_ref[...] += dgp

        t = dy * gv
        dx = r * (t - xh * jnp.mean(t * xh, axis=-1, keepdims=True))
        dx_ref[...] = dx
        dxb_ref[...] = dx.astype(BF16)

        @pl.when(i == n - 1)
        def _():
            loss_ref[...] = jnp.sum(acc_ref[...], axis=-1, keepdims=True) * (0.5 / C)

    return pl.pallas_call(
        body, name=name, grid=(n,),
        in_specs=[row, vec, row],
        out_specs=[row, row, vec, pl.BlockSpec((1, 1), lambda i: (0, 0))],
        out_shape=[jax.ShapeDtypeStruct((R, C), F32), jax.ShapeDtypeStruct((R, C), BF16),
                   jax.ShapeDtypeStruct((1, C), F32), jax.ShapeDtypeStruct((1, 1), F32)],
        scratch_shapes=[pltpu.VMEM((1, C), F32)],
        compiler_params=_cp(("arbitrary",), VMEM_MB),
    )(x3, g, tgt)


def _offsets():
    u0 = 0
    v0 = DS
    b0 = 2 * DS
    c0 = b0 + DC
    x0 = c0 + DC
    q0 = x0 + DC
    return u0, v0, b0, c0, x0, q0


def _tri_mask(lower):
    r = lax.broadcasted_iota(jnp.int32, (CHUNK, CHUNK), 0)
    c = lax.broadcasted_iota(jnp.int32, (CHUNK, CHUNK), 1)
    return (r >= c) if lower else (c >= r)


def _layer_norm_stats(vg):
    mu = jnp.mean(vg, axis=-1, keepdims=True)
    vc = vg - mu
    rstd = lax.rsqrt(jnp.mean(vc * vc, axis=-1, keepdims=True) + EPS)
    return vc * rstd, rstd


def _softmax_rows(qh, kh):
    s = lax.dot_general(qh, kh, (((1,), (1,)), ((), ())), preferred_element_type=F32)
    m = jnp.max(s, axis=-1, keepdims=True)
    e = jnp.exp(s - m)
    return e / jnp.sum(e, axis=-1, keepdims=True)


def _mix_fwd(proj, kv, w_s, bs_t, ln_g, ln_b, conv_w, g_head, *, name):
    assert DS == DC
    tr = _pick(S, TR_MIX, CHUNK)
    n = S // tr
    nck = tr // CHUNK
    u0, v0, b0, c0, x0, q0 = _offsets()
    hb = tr // HALO

    def body(p_ref, cprev_ref, xprev_ref, kv_ref, ws_ref, bst_ref, lng_ref, lnb_ref, cw_ref, gh_ref,
             heads_ref, hn_ref, ycv_ref, buf_ref):
        i = pl.program_id(0)

        def emit(col, val):
            rs = lax.rsqrt(jnp.mean(val * val, axis=-1, keepdims=True) + EPS)
            heads_ref[:, col:col + HD] = val
            hn_ref[:, col:col + HD] = ((val * rs) * gh_ref[:, col:col + HD]).astype(BF16)

        vhat, _ = _layer_norm_stats(_gelu(p_ref[:, v0:v0 + DS]))
        vnb = (vhat * lng_ref[...] + lnb_ref[...]).astype(BF16)
        low = _tri_mask(True)
        for h in range(NSH):
            wt = jnp.where(low, ws_ref[h], 0.0).astype(BF16)
            bcol = bst_ref[:, h:h + 1]
            parts = []
            for c in range(nck):
                blk = vnb[c * CHUNK:(c + 1) * CHUNK, h * HD:(h + 1) * HD]
                parts.append(jnp.dot(wt, blk, preferred_element_type=F32) + bcol)
            mixed = parts[0] if nck == 1 else jnp.concatenate(parts, axis=0)
            emit(h * HD, _gelu(p_ref[:, u0 + h * HD:u0 + (h + 1) * HD]) * mixed)

        xc = p_ref[:, c0:c0 + DC] * p_ref[:, x0:x0 + DC]
        prev = cprev_ref[...] * xprev_ref[...]
        buf_ref[0:HALO, :] = jnp.where(i > 0, prev, 0.0)
        buf_ref[HALO:HALO + tr, :] = xc
        y = (cw_ref[2:3, :] * xc + cw_ref[1:2, :] * buf_ref[HALO - 1:HALO - 1 + tr, :]
             + cw_ref[0:1, :] * buf_ref[HALO - 2:HALO - 2 + tr, :])
        ycv_ref[...] = y
        cout = p_ref[:, b0:b0 + DC] * y
        for h in range(NCH):
            emit(DS + h * HD, cout[:, h * HD:(h + 1) * HD])

        for h in range(NMH):
            qh = (p_ref[:, q0 + h * HD:q0 + (h + 1) * HD] * SCALE).astype(BF16)
            kh = kv_ref[:, h * HD:(h + 1) * HD].astype(BF16)
            vh = kv_ref[:, DM + h * HD:DM + (h + 1) * HD].astype(BF16)
            p = _softmax_rows(qh, kh)
            emit(DS + DC + h * HD, jnp.dot(p.astype(BF16), vh, preferred_element_type=F32))

    full = lambda shape: pl.BlockSpec(shape, lambda i: (0,) * len(shape))
    halo_c = pl.BlockSpec((HALO, DC), lambda i: (jnp.maximum(i * hb - 1, 0), c0 // DC))
    halo_x = pl.BlockSpec((HALO, DC), lambda i: (jnp.maximum(i * hb - 1, 0), x0 // DC))
    return pl.pallas_call(
        body, name=name, grid=(n,),
        in_specs=[pl.BlockSpec((tr, DIN), lambda i: (i, 0)), halo_c, halo_x,
                  full((NMEM, 2 * DM)), full((NSH, CHUNK, CHUNK)), full((CHUNK, NSH)),
                  full((1, DS)), full((1, DS)), full((3, DC)), full((1, D))],
        out_specs=[pl.BlockSpec((tr, D), lambda i: (i, 0)), pl.BlockSpec((tr, D), lambda i: (i, 0)),
                   pl.BlockSpec((tr, DC), lambda i: (i, 0))],
        out_shape=[jax.ShapeDtypeStruct((S, D), F32), jax.ShapeDtypeStruct((S, D), BF16),
                   jax.ShapeDtypeStruct((S, DC), F32)],
        scratch_shapes=[pltpu.VMEM((tr + HALO, DC), F32)],
        compiler_params=_cp(("parallel",), VMEM_MB),
    )(proj, proj, proj, kv, w_s, bs_t, ln_g, ln_b, conv_w, g_head)


def _mix_bwd(dhn, heads, proj, ycv, kv, w_s, bs_t, ln_g, ln_b, conv_w, g_head, after, *, name):
    assert DS == DC
    tr = _pick(S, TR_MIX, CHUNK)
    n = S // tr
    nck = tr // CHUNK
    u0, v0, b0, c0, x0, q0 = _offsets()
    hb = tr // HALO
    last_hb = S // HALO - 1

    def body(dhn_ref, heads_ref, p_ref, ycv_ref, dhn_nx_ref, heads_nx_ref, b_nx_ref, kv_ref, ws_ref, bst_ref,
             lng_ref, lnb_ref, cw_ref, gh_ref, _after_ref,
             dp_ref, dkv_ref, dws_ref, dbs_ref, dlng_ref, dlnb_ref, dcw_ref, dgh_ref, buf_ref, dvn_ref):
        i = pl.program_id(0)

        @pl.when(i == 0)
        def _():
            dkv_ref[...] = jnp.zeros_like(dkv_ref)
            dws_ref[...] = jnp.zeros_like(dws_ref)
            dbs_ref[...] = jnp.zeros_like(dbs_ref)
            dlng_ref[...] = jnp.zeros_like(dlng_ref)
            dlnb_ref[...] = jnp.zeros_like(dlnb_ref)
            dcw_ref[...] = jnp.zeros_like(dcw_ref)
            dgh_ref[...] = jnp.zeros_like(dgh_ref)

        def head_bwd(a, dn, gh):
            rs = lax.rsqrt(jnp.mean(a * a, axis=-1, keepdims=True) + EPS)
            ah = a * rs
            t = dn * gh
            return rs * (t - ah * jnp.mean(t * ah, axis=-1, keepdims=True)), jnp.sum(dn * ah, axis=0, keepdims=True)

        def head_grad(col):
            da, dg = head_bwd(heads_ref[:, col:col + HD], dhn_ref[:, col:col + HD], gh_ref[:, col:col + HD])
            dgh_ref[:, col:col + HD] += dg
            return da

        vg, dvg_dv = _gelu_with_grad(p_ref[:, v0:v0 + DS])
        vhat, rstd = _layer_norm_stats(vg)
        vnb = (vhat * lng_ref[...] + lnb_ref[...]).astype(BF16)
        low = _tri_mask(True)
        ones = jnp.ones((HALO, HD), BF16)
        for h in range(NSH):
            w_h = ws_ref[h]
            wt = jnp.where(low, w_h, 0.0).astype(BF16)
            bcol = bst_ref[:, h:h + 1]
            da = head_grad(h * HD)
            ug, dug_du = _gelu_with_grad(p_ref[:, u0 + h * HD:u0 + (h + 1) * HD])
            dws = jnp.zeros((CHUNK, CHUNK), F32)
            dbs = jnp.zeros((HALO, CHUNK), F32)
            mixed_parts = []
            for c in range(nck):
                rows = slice(c * CHUNK, (c + 1) * CHUNK)
                blk = vnb[rows, h * HD:(h + 1) * HD]
                mixed_parts.append(jnp.dot(wt, blk, preferred_element_type=F32) + bcol)
                dmb = (da[rows] * ug[rows]).astype(BF16)
                dws = dws + lax.dot_general(dmb, blk, (((1,), (1,)), ((), ())), preferred_element_type=F32)
                dbs = dbs + lax.dot_general(ones, dmb, (((1,), (1,)), ((), ())), preferred_element_type=F32)
                dvn_ref[c * CHUNK:(c + 1) * CHUNK, h * HD:(h + 1) * HD] = lax.dot_general(
                    wt, dmb, (((0,), (0,)), ((), ())), preferred_element_type=F32)
            mixed = mixed_parts[0] if nck == 1 else jnp.concatenate(mixed_parts, axis=0)
            dp_ref[:, u0 + h * HD:u0 + (h + 1) * HD] = ((da * mixed) * dug_du).astype(BF16)
            dws_ref[h] += jnp.where(low, dws, 0.0)
            dbs_ref[h] += dbs
        dvn = dvn_ref[...]
        dlng_ref[...] += jnp.sum(dvn * vhat, axis=0, keepdims=True)
        dlnb_ref[...] += jnp.sum(dvn, axis=0, keepdims=True)
        dvh = dvn * lng_ref[...]
        dvg = rstd * (dvh - jnp.mean(dvh, axis=-1, keepdims=True)
                      - vhat * jnp.mean(dvh * vhat, axis=-1, keepdims=True))
        dp_ref[:, v0:v0 + DS] = (dvg * dvg_dv).astype(BF16)

        dc = jnp.concatenate([head_grad(DS + h * HD) for h in range(NCH)], axis=1)
        dc_nx = jnp.concatenate(
            [head_bwd(heads_nx_ref[:, h * HD:(h + 1) * HD], dhn_nx_ref[:, h * HD:(h + 1) * HD],
                      gh_ref[:, DS + h * HD:DS + (h + 1) * HD])[0] for h in range(NCH)], axis=1)
        bg = p_ref[:, b0:b0 + DC]
        cg = p_ref[:, c0:c0 + DC]
        xin = p_ref[:, x0:x0 + DC]
        dp_ref[:, b0:b0 + DC] = (dc * ycv_ref[...]).astype(BF16)
        dyv = dc * bg
        buf_ref[0:tr, :] = dyv
        buf_ref[tr:tr + HALO, :] = jnp.where(i < n - 1, dc_nx * b_nx_ref[...], 0.0)
        sh1 = buf_ref[1:1 + tr, :]
        sh0 = buf_ref[2:2 + tr, :]
        dxc = cw_ref[2:3, :] * dyv + cw_ref[1:2, :] * sh1 + cw_ref[0:1, :] * sh0
        xc = cg * xin
        dp_ref[:, c0:c0 + DC] = (dxc * xin).astype(BF16)
        dp_ref[:, x0:x0 + DC] = (dxc * cg).astype(BF16)
        dcw_ref[0:1, :] += jnp.sum(sh0 * xc, axis=0, keepdims=True)
        dcw_ref[1:2, :] += jnp.sum(sh1 * xc, axis=0, keepdims=True)
        dcw_ref[2:3, :] += jnp.sum(dyv * xc, axis=0, keepdims=True)

        for h in range(NMH):
            do = head_grad(DS + DC + h * HD).astype(BF16)
            qh = (p_ref[:, q0 + h * HD:q0 + (h + 1) * HD] * SCALE).astype(BF16)
            kh = kv_ref[:, h * HD:(h + 1) * HD].astype(BF16)
            vh = kv_ref[:, DM + h * HD:DM + (h + 1) * HD].astype(BF16)
            p = _softmax_rows(qh, kh)
            dpr = lax.dot_general(do, vh, (((1,), (1,)), ((), ())), preferred_element_type=F32)
            ds = (p * (dpr - jnp.sum(dpr * p, axis=-1, keepdims=True))).astype(BF16)
            dp_ref[:, q0 + h * HD:q0 + (h + 1) * HD] = (
                jnp.dot(ds, kh, preferred_element_type=F32) * SCALE).astype(BF16)
            dkv_ref[:, h * HD:(h + 1) * HD] += lax.dot_general(
                ds, qh, (((0,), (0,)), ((), ())), preferred_element_type=F32)
            dkv_ref[:, DM + h * HD:DM + (h + 1) * HD] += lax.dot_general(
                p.astype(BF16), do, (((0,), (0,)), ((), ())), preferred_element_type=F32)

    full = lambda shape: pl.BlockSpec(shape, lambda i: (0,) * len(shape))
    row = lambda c: pl.BlockSpec((tr, c), lambda i: (i, 0))
    nxt = lambda col: pl.BlockSpec((HALO, DC), lambda i: (jnp.minimum((i + 1) * hb, last_hb), col))
    return pl.pallas_call(
        body, name=name, grid=(n,),
        in_specs=[row(D), row(D), row(DIN), row(DC), nxt(DS // DC), nxt(DS // DC), nxt(b0 // DC),
                  full((NMEM, 2 * DM)), full((NSH, CHUNK, CHUNK)), full((CHUNK, NSH)),
                  full((1, DS)), full((1, DS)), full((3, DC)), full((1, D)), ANY],
        out_specs=[row(DIN), full((NMEM, 2 * DM)), full((NSH, CHUNK, CHUNK)), full((NSH, HALO, CHUNK)),
                   full((1, DS)), full((1, DS)), full((HALO, DC)), full((1, D))],
        out_shape=[jax.ShapeDtypeStruct((S, DIN), BF16), jax.ShapeDtypeStruct((NMEM, 2 * DM), F32),
                   jax.ShapeDtypeStruct((NSH, CHUNK, CHUNK), F32), jax.ShapeDtypeStruct((NSH, HALO, CHUNK), F32),
                   jax.ShapeDtypeStruct((1, DS), F32), jax.ShapeDtypeStruct((1, DS), F32),
                   jax.ShapeDtypeStruct((HALO, DC), F32), jax.ShapeDtypeStruct((1, D), F32)],
        scratch_shapes=[pltpu.VMEM((tr + HALO, DC), F32), pltpu.VMEM((tr, DS), F32)],
        compiler_params=_cp(("arbitrary",), VMEM_MB),
    )(dhn, heads, proj, ycv, dhn, heads, proj, kv, w_s, bs_t, ln_g, ln_b, conv_w, g_head, after)


def _place():
    x, y, c = lax.axis_index("x"), lax.axis_index("y"), lax.axis_index("c")
    chips = [(1 - x, y), (x, 1 - y), (1 - x, 1 - y)]
    return x, y, c, chips


ANY = pl.BlockSpec(memory_space=pl.ANY)


HBM = pl.BlockSpec(memory_space=pltpu.HBM)
SEM = pl.BlockSpec(memory_space=pltpu.SEMAPHORE)
EFFECT = pltpu.SideEffectType.DATAFLOW_SIDE_EFFECTING
N_PEER_CHIPS = 3
N_NEIGHBOUR_CHIPS = 2
CONV_PAD = (32, 256)


def _in_hbm(a):
    return pltpu.with_memory_space_constraint(a, pltpu.HBM)


def _allgather_start(bufs, forwards, after, collective_id, *, name):
    arrs = list(bufs) + list(forwards)
    nw, nb = len(arrs), len(bufs)

    def body(*refs):
        ins, send, recv = refs[:nw], refs[nw + 1:2 * nw + 1], refs[2 * nw + 1:3 * nw + 1]
        token = refs[4 * nw + 1]
        x, y, c, chips = _place()
        s = 2 * x + y
        slots = [2 * cx + cy for cx, cy in chips]
        _handshake([(cx, cy, c) for cx, cy in chips[:N_NEIGHBOUR_CHIPS]])
        for w in range(nb, nw):
            q = arrs[w].shape[1] // 4
            for j in range(N_NEIGHBOUR_CHIPS):
                rows = ins[w].at[slots[j], pl.ds(c * 2 * q + j * q, q)]
                pltpu.make_async_remote_copy(src_ref=rows, dst_ref=rows, send_sem=send[w], recv_sem=recv[w],
                                             device_id=(*chips[1 - j], c), device_id_type=MESH).start()
        for w in range(nb):
            hr = arrs[w].shape[1] // 2
            rows = ins[w].at[s, pl.ds(c * hr, hr)]
            for cx, cy in chips[:N_NEIGHBOUR_CHIPS]:
                pltpu.make_async_remote_copy(src_ref=rows, dst_ref=rows, send_sem=send[w], recv_sem=recv[w],
                                             device_id=(cx, cy, c), device_id_type=MESH).start()
        token[...] = jnp.zeros_like(token)

    res = pl.pallas_call(
        body, name=name,
        in_specs=[HBM] * nw + [ANY],
        out_specs=[SEM] * (2 * nw) + [HBM] * nw + [pl.BlockSpec(memory_space=pltpu.VMEM)],
        out_shape=[pltpu.SemaphoreType.DMA(())] * (2 * nw) + [pltpu.HBM(a.shape, a.dtype) for a in arrs]
        + [jax.ShapeDtypeStruct((8, 128), F32)],
        input_output_aliases={w: 2 * nw + w for w in range(nw)},
        compiler_params=pltpu.CompilerParams(has_side_effects=EFFECT, collective_id=collective_id),
    )(*[_in_hbm(a) for a in arrs], after)
    return res[:nw], res[nw:2 * nw], res[2 * nw:3 * nw], res[3 * nw]


def _handshake(peers):
    barrier = pltpu.get_barrier_semaphore()
    for peer in peers:
        pl.semaphore_signal(barrier, inc=1, device_id=peer, device_id_type=MESH)
    pl.semaphore_wait(barrier, len(peers))


def _scatter_start(parts, bufs, collective_id, *, name):
    nw = len(parts)

    def body(*refs):
        src, dst = refs[:nw], refs[nw:2 * nw]
        send, recv = refs[2 * nw:3 * nw], refs[3 * nw:4 * nw]
        token = refs[6 * nw]
        x, y, c, chips = _place()
        s = 2 * x + y
        _handshake([(cx, cy, c) for cx, cy in chips])
        for w in range(nw):
            for cx, cy in chips:
                pltpu.make_async_remote_copy(src_ref=src[w].at[2 * cx + cy], dst_ref=dst[w].at[s], send_sem=send[w],
                                             recv_sem=recv[w], device_id=(cx, cy, c), device_id_type=MESH).start()
        token[...] = jnp.zeros_like(token)

    res = pl.pallas_call(
        body, name=name,
        in_specs=[HBM] * (2 * nw),
        out_specs=[SEM] * (2 * nw) + [HBM] * (2 * nw) + [pl.BlockSpec(memory_space=pltpu.VMEM)],
        out_shape=[pltpu.SemaphoreType.DMA(())] * (2 * nw) + [pltpu.HBM(a.shape, a.dtype) for a in parts + bufs]
        + [jax.ShapeDtypeStruct((8, 128), F32)],
        input_output_aliases={k: 2 * nw + k for k in range(2 * nw)},
        compiler_params=pltpu.CompilerParams(has_side_effects=EFFECT, collective_id=collective_id),
    )(*[_in_hbm(a) for a in parts + bufs])
    return res[:nw], res[nw:2 * nw], res[2 * nw:3 * nw], res[3 * nw:4 * nw], res[4 * nw]


def _sibling_start(srcs, whole, collective_id, *, name):
    nw = len(srcs)
    lands = [lax.empty((a.shape[0], a.shape[1] if whole else a.shape[1] // 2, a.shape[2]), a.dtype) for a in srcs]

    def body(*refs):
        src, land = refs[:nw], refs[nw:2 * nw]
        send, recv = refs[2 * nw:3 * nw], refs[3 * nw:4 * nw]
        token = refs[6 * nw]
        x, y, c, _ = _place()
        _handshake([(x, y, 1 - c)])
        for w in range(nw):
            hr = srcs[w].shape[1] // 2
            rows = src[w] if whole else src[w].at[:, pl.ds((1 - c) * hr, hr)]
            pltpu.make_async_remote_copy(src_ref=rows, dst_ref=land[w], send_sem=send[w], recv_sem=recv[w],
                                         device_id=(x, y, 1 - c), device_id_type=MESH).start()
        token[...] = jnp.zeros_like(token)

    res = pl.pallas_call(
        body, name=name,
        in_specs=[HBM] * (2 * nw),
        out_specs=[SEM] * (2 * nw) + [HBM] * (2 * nw) + [pl.BlockSpec(memory_space=pltpu.VMEM)],
        out_shape=[pltpu.SemaphoreType.DMA(())] * (2 * nw) + [pltpu.HBM(a.shape, a.dtype) for a in srcs + lands]
        + [jax.ShapeDtypeStruct((8, 128), F32)],
        input_output_aliases={k: 2 * nw + k for k in range(2 * nw)},
        compiler_params=pltpu.CompilerParams(has_side_effects=EFFECT, collective_id=collective_id),
    )(*[_in_hbm(a) for a in srcs + lands])
    return res[:nw], res[nw:2 * nw], res[2 * nw:3 * nw], res[3 * nw:4 * nw], res[4 * nw]


def _transfer_wait(sends, recvs, thru, sizes, after, *, name):
    n = len(sends)
    flat = [a for group in thru for a in group]

    def body(*refs):
        bufs = refs[:len(flat)]
        send = refs[len(flat):len(flat) + n]
        recv = refs[len(flat) + n:len(flat) + 2 * n]
        token = refs[2 * len(flat) + 2 * n + 1]
        token[...] = jnp.zeros_like(token)
        x, y, c, _ = _place()
        pos = 0
        for k in range(n):
            slots, rows = sizes[k]
            region = bufs[pos].at[pl.ds(0, slots), pl.ds(0, rows)]
            pos += len(thru[k])
            cp = pltpu.make_async_remote_copy(src_ref=region, dst_ref=region, send_sem=send[k], recv_sem=recv[k],
                                              device_id=(x, y, 1 - c), device_id_type=MESH)
            cp.wait_send()
            cp.wait_recv()

    res = pl.pallas_call(
        body, name=name,
        in_specs=[HBM] * len(flat) + [SEM] * (2 * n) + [pl.BlockSpec(memory_space=pl.ANY)],
        out_specs=[HBM] * len(flat) + [pl.BlockSpec(memory_space=pltpu.VMEM)],
        out_shape=[pltpu.HBM(a.shape, a.dtype) for a in flat] + [jax.ShapeDtypeStruct((8, 128), F32)],
        input_output_aliases={k: k for k in range(len(flat))},
        compiler_params=pltpu.CompilerParams(has_side_effects=EFFECT),
    )(*flat, *sends, *recvs, after)
    out, pos = [], 0
    for group in thru:
        out.append(res[pos:pos + len(group)])
        pos += len(group)
    return out, res[len(flat)]


def _forward_gathered(bufs, after, *, name):
    nw = len(bufs)

    def body(*refs):
        outs = refs[nw + 1:2 * nw + 1]
        d_send, d_recv, i_send, i_recv = refs[2 * nw + 1:]
        x, y, c, chips = _place()
        me, sibling = (x, y, c), (x, y, 1 - c)
        slots = [2 * cx + cy for cx, cy in chips]

        def rows(w, j, start, n):
            return outs[w].at[slots[j], pl.ds(start, n)]

        def d2d(w, j, which, to):
            hr = bufs[w].shape[1] // 2
            r = rows(w, j, which * hr, hr)
            return pltpu.make_async_remote_copy(
                src_ref=r, dst_ref=r, send_sem=d_send.at[N_PEER_CHIPS * w + j],
                recv_sem=d_recv.at[N_PEER_CHIPS * w + j], device_id=to, device_id_type=MESH)

        def ici(w, j, slot_j, to):
            q = bufs[w].shape[1] // 4
            r = rows(w, slot_j, c * 2 * q + j * q, q)
            return pltpu.make_async_remote_copy(
                src_ref=r, dst_ref=r, send_sem=i_send.at[N_NEIGHBOUR_CHIPS * w + j],
                recv_sem=i_recv.at[N_NEIGHBOUR_CHIPS * w + j], device_id=to, device_id_type=MESH)

        started = []
        for w in range(nw):
            started += [ici(w, 0, 0, (*chips[1], c)), ici(w, 1, 1, (*chips[0], c))]
            started += [d2d(w, j, c, sibling) for j in range(N_NEIGHBOUR_CHIPS)]
        for cp in started:
            cp.start()
        diag = N_PEER_CHIPS - 1
        for w in range(nw):
            for j in range(N_NEIGHBOUR_CHIPS):
                ici(w, j, diag, me).wait_recv()
            cp = d2d(w, diag, c, sibling)
            cp.start()
            started.append(cp)
        for w in range(nw):
            for j in range(N_PEER_CHIPS):
                d2d(w, j, 1 - c, me).wait_recv()
        for cp in started:
            cp.wait_send()

    return pl.pallas_call(
        body, name=name,
        in_specs=[ANY] * (nw + 1), out_specs=[ANY] * nw,
        out_shape=[jax.ShapeDtypeStruct(a.shape, a.dtype) for a in bufs],
        input_output_aliases={w: w for w in range(nw)},
        scratch_shapes=[pltpu.SemaphoreType.DMA((N_PEER_CHIPS * nw,)), pltpu.SemaphoreType.DMA((N_PEER_CHIPS * nw,)),
                        pltpu.SemaphoreType.DMA((N_NEIGHBOUR_CHIPS * nw,)),
                        pltpu.SemaphoreType.DMA((N_NEIGHBOUR_CHIPS * nw,))],
    )(*bufs, after)


def _forward_halves(bufs, which, after, *, name):
    nw = len(bufs)
    n = len(which)

    def body(*refs):
        outs = refs[nw + 1:2 * nw + 1]
        send, recv = refs[2 * nw + 1:]
        x, y, c, chips = _place()
        me, sibling = (x, y, c), (x, y, 1 - c)

        def d2d(w, t, half, to):
            cx, cy = chips[which[t]]
            hr = bufs[w].shape[1] // 2
            rows = outs[w].at[2 * cx + cy, pl.ds(half * hr, hr)]
            return pltpu.make_async_remote_copy(src_ref=rows, dst_ref=rows, send_sem=send.at[n * w + t],
                                                recv_sem=recv.at[n * w + t], device_id=to, device_id_type=MESH)

        passed = [d2d(w, t, c, sibling) for w in range(nw) for t in range(n)]
        for cp in passed:
            cp.start()
        for w in range(nw):
            for t in range(n):
                d2d(w, t, 1 - c, me).wait_recv()
        for cp in passed:
            cp.wait_send()

    return pl.pallas_call(
        body, name=name,
        in_specs=[ANY] * (nw + 1), out_specs=[ANY] * nw,
        out_shape=[jax.ShapeDtypeStruct(a.shape, a.dtype) for a in bufs],
        input_output_aliases={w: w for w in range(nw)},
        scratch_shapes=[pltpu.SemaphoreType.DMA((n * nw,)), pltpu.SemaphoreType.DMA((n * nw,))],
    )(*bufs, after)


def _allreduce_small(p, after, *, name):
    R = p.shape[0]
    hr = R // 2

    def body(p_ref, _after_ref, out_ref, sib_ref, sum_ref, gat_ref, tot_ref, send, recv):
        x, y, c, chips = _place()
        s = 2 * x + y
        sibling = (x, y, 1 - c)
        rows = pl.ds(pl.multiple_of(c * hr, 8), hr)
        swap = pltpu.make_async_remote_copy(src_ref=p_ref, dst_ref=sib_ref, send_sem=send.at[0], recv_sem=recv.at[0],
                                            device_id=sibling, device_id_type=MESH)
        swap.start()
        swap.wait()
        sum_ref[...] = p_ref[...] + sib_ref[...]
        gat_ref[s] = sum_ref[rows, :]
        cps = [pltpu.make_async_remote_copy(src_ref=sum_ref.at[rows], dst_ref=gat_ref.at[s], send_sem=send.at[1 + j],
                                            recv_sem=recv.at[1 + j], device_id=(cx, cy, c), device_id_type=MESH)
               for j, (cx, cy) in enumerate(chips)]
        for cp in cps:
            cp.start()
        for cp in cps:
            cp.wait()
        tot_ref[...] = ((gat_ref[0] + gat_ref[1]) + gat_ref[2]) + gat_ref[3]
        out_ref[rows, :] = tot_ref[...]
        share = pltpu.make_async_remote_copy(src_ref=tot_ref, dst_ref=out_ref.at[rows], send_sem=send.at[4],
                                             recv_sem=recv.at[4], device_id=sibling, device_id_type=MESH)
        share.start()
        share.wait_send()
        other = out_ref.at[pl.ds(pl.multiple_of((1 - c) * hr, 8), hr)]
        pltpu.make_async_remote_copy(src_ref=other, dst_ref=other, send_sem=send.at[4], recv_sem=recv.at[4],
                                     device_id=(x, y, c), device_id_type=MESH).wait_recv()

    vmem = pl.BlockSpec(memory_space=pltpu.VMEM)
    return pl.pallas_call(
        body, name=name, in_specs=[vmem, ANY], out_specs=vmem,
        out_shape=jax.ShapeDtypeStruct((R, 128), F32),
        scratch_shapes=[pltpu.VMEM((R, 128), F32), pltpu.VMEM((R, 128), F32), pltpu.VMEM((NCHIP, hr, 128), F32),
                        pltpu.VMEM((hr, 128), F32), pltpu.SemaphoreType.DMA((5,)), pltpu.SemaphoreType.DMA((5,))],
    )(p, after)


def _select_half_bf16(g, half, add, slot, *, name):
    _, R, C = g.shape
    hr = R // 2
    tr = _pick_rows(hr, 16)
    nb = hr // tr
    sel = jnp.concatenate([jnp.reshape(half, (1,)).astype(jnp.int32), slot])

    def body(s_ref, g_ref, a_ref, o_ref, own_ref):
        val = (g_ref[...].astype(F32) + a_ref[...].astype(F32)).astype(BF16)
        o_ref[...] = val

        @pl.when(pl.program_id(1) == s_ref[1])
        def _():
            own_ref[...] = val

    g_spec = pl.BlockSpec((None, tr, C), lambda i, j, s: (j, s[0] * nb + i, 0))
    o_spec = pl.BlockSpec((None, tr, C), lambda i, j, s: (j, i, 0))
    own_spec = pl.BlockSpec((None, tr, C), lambda i, j, s: (s[1], i, 0))
    shape = jax.ShapeDtypeStruct((NCHIP, hr, C), BF16)
    return pl.pallas_call(
        body, name=name,
        grid_spec=pltpu.PrefetchScalarGridSpec(
            num_scalar_prefetch=1, grid=(nb, NCHIP), in_specs=[g_spec, o_spec], out_specs=[o_spec, own_spec]),
        out_shape=[shape, shape],
        compiler_params=_cp(("parallel", "arbitrary"), VMEM_MB),
    )(sel, g, add)


def _adamw_math(w, g, m, v):
    m = ADAM_B1 * m + (1.0 - ADAM_B1) * g
    v = ADAM_B2 * v + (1.0 - ADAM_B2) * (g * g)
    m_hat = m / (1.0 - ADAM_B1 ** ADAM_STEP)
    v_hat = v / (1.0 - ADAM_B2 ** ADAM_STEP)
    delta = -ADAM_LR * (m_hat / (jnp.sqrt(v_hat) + ADAM_EPS) + ADAM_WD * w)
    return delta, m, v


def _adamw(w, g_mine, g_sib, m, v, core, *, name):
    R, C = w.shape
    hr = R // 2
    tr = _pick_rows(hr, 16)
    nb = hr // tr
    row = pl.BlockSpec((tr, C), lambda hh, i, c: (hh * nb + i, 0))
    mine = pl.BlockSpec((NCHIP, tr, C), lambda hh, i, c: (0, jnp.where(hh == c[0], i, 0), 0))
    sibs = pl.BlockSpec((NCHIP, tr, C), lambda hh, i, c: (0, jnp.where(hh == c[0], 0, i), 0))

    def slot_sum(ref):
        acc = ref[0].astype(F32) + ref[1].astype(F32)
        for j in range(2, NCHIP):
            acc = acc + ref[j].astype(F32)
        return acc

    def body(c_ref, w_ref, gm_ref, gs_ref, m_ref, v_ref, go_ref, d_ref, mo_ref, vo_ref):
        gv = jnp.where(pl.program_id(0) == c_ref[0], slot_sum(gm_ref), slot_sum(gs_ref))
        d, mn, vn = _adamw_math(w_ref[...], gv, m_ref[...], v_ref[...])
        go_ref[...] = gv
        d_ref[...] = d
        mo_ref[...] = mn
        vo_ref[...] = vn

    return pl.pallas_call(
        body, name=name,
        grid_spec=pltpu.PrefetchScalarGridSpec(
            num_scalar_prefetch=1, grid=(2, nb),
            in_specs=[row, mine, sibs, row, row], out_specs=[row] * 4),
        out_shape=[jax.ShapeDtypeStruct((R, C), F32)] * 4,
        compiler_params=_cp(("parallel", "parallel"), VMEM_MB),
    )(core, w, g_mine, g_sib, m, v)


def _adamw_small(ws, gs, ms, vs, *, name):
    n = len(ws)

    def body(*refs):
        w_r, g_r, m_r, v_r = refs[:n], refs[n:2 * n], refs[2 * n:3 * n], refs[3 * n:4 * n]
        d_r, mo_r, vo_r = refs[4 * n:5 * n], refs[5 * n:6 * n], refs[6 * n:7 * n]
        for k in range(n):
            d, mn, vn = _adamw_math(w_r[k][...], g_r[k][...], m_r[k][...], v_r[k][...])
            d_r[k][...] = d
            mo_r[k][...] = mn
            vo_r[k][...] = vn

    shapes = [jax.ShapeDtypeStruct(w.shape, F32) for w in ws]
    res = pl.pallas_call(body, name=name, out_shape=shapes * 3)(*ws, *gs, *ms, *vs)
    return res[:n], res[n:2 * n], res[2 * n:]


_PACK_ROWS = 8


def _pack(parts):
    rows = []
    for a in parts:
        flat = a.reshape(-1)
        n = -(-flat.shape[0] // (_PACK_ROWS * 128)) * (_PACK_ROWS * 128)
        rows.append(jnp.pad(flat, (0, n - flat.shape[0])).reshape(-1, 128))
    total = sum(r.shape[0] for r in rows)
    if total % 16:
        rows.append(jnp.zeros((16 - total % 16, 128), F32))
    return jnp.concatenate(rows, axis=0)


def _unpack(p, shapes):
    out, r = [], 0
    for shp in shapes:
        n = math.prod(shp)
        nr = -(-n // (_PACK_ROWS * 128)) * _PACK_ROWS
        out.append(p[r:r + nr].reshape(-1)[:n].reshape(shp))
        r += nr
    return out


def kernel(x, mem, g_mix, w_in, ln_v_g, ln_v_b, w_s, b_s, conv_w, g_mem, w_kv, g_head, w_o, g_ffn, w_ffn1, w_ffn2, g_final, loss_target, m_g_mix, m_w_in, m_ln_v_g, m_ln_v_b, m_w_s, m_b_s, m_conv_w, m_g_mem, m_w_kv, m_g_head, m_w_o, m_g_ffn, m_w_ffn1, m_w_ffn2, m_g_final, v_g_mix, v_w_in, v_ln_v_g, v_ln_v_b, v_w_s, v_b_s, v_conv_w, v_g_mem, v_w_kv, v_g_head, v_w_o, v_g_ffn, v_w_ffn1, v_w_ffn2, v_g_final):
    sds = jax.ShapeDtypeStruct
    xi, yi = lax.axis_index("x"), lax.axis_index("y")
    shard = 2 * xi + yi
    x2d, mem2d, tgt = x[0], mem[0], loss_target[0]
    ws3, bs2 = w_s[0], b_s[0]
    g_final2 = g_final.reshape(1, D)
    dff4 = DFF // NCHIP
    din4 = DIN // NCHIP
    dcv4 = DC // NCHIP

    big = [w_in[0].T, w_kv[0], w_o[0], w_ffn1[0], w_ffn2[0]]
    big_names = ["w_in", "w_kv", "w_o", "w_ffn1", "w_ffn2"]
    slot = jnp.reshape(shard, (1,)).astype(jnp.int32)
    core = jnp.reshape(lax.axis_index("c"), (1,)).astype(jnp.int32)
    conv_pad = jnp.pad(conv_w[0], ((0, CONV_PAD[0] - 3), (0, CONV_PAD[1] - dcv4)))
    conv_slots = lax.dynamic_update_slice(jnp.zeros((NCHIP,) + CONV_PAD, F32), conv_pad[None], (shard, 0, 0))

    gather_ids = {"in": 16, "kvo": 17, "ffn1": 18, "ffn2": 19, "ffn2d": 20}

    def gather_start(bufs, after, nm, forwards=()):
        return _allgather_start(bufs, forwards, after, gather_ids[nm], name="ag_start_" + nm)

    def gather_wait(state, idx, after, nm):
        send, recv, bufs, _ = state
        got, token = _transfer_wait([send[k] for k in idx], [recv[k] for k in idx], [[bufs[k]] for k in idx],
                                    [(N_NEIGHBOUR_CHIPS, bufs[k].shape[1] // 2) for k in idx], after, name="ag_wait_" + nm)
        return [g[0] for g in got], token

    cast = lambda k, after: _cast_into_slot(big[k], slot, after, name="cast_" + big_names[k])
    ag_in = gather_start([cast(0, slot), conv_slots], slot, "in")
    bs_t = bs2.T

    h = _rms_fwd(x2d, g_mix, name="rms_mix", after=[ag_in[3]])
    mem_n = _rms_fwd(mem2d, g_mem, name="rms_mem", after=[h])
    kvo_b = [cast(1, mem_n)]
    kvo_b.append(cast(2, kvo_b[0]))
    w1_b = cast(3, kvo_b[1])
    w2_b = cast(4, w1_b)
    got_in, tok = gather_wait(ag_in, [0, 1], w2_b, "in")
    win4, conv4 = _forward_gathered(got_in, tok, name="ag_fwd_in")
    ag_kvo = gather_start(kvo_b, conv4, "kvo")
    w_in_t = win4.reshape(DIN, D)
    conv_full = conv4[:, :3, :dcv4].transpose(1, 0, 2).reshape(3, DC)
    NEAR, FAR = [0, 1], [2]

    def diagonal_wait(state, ks, after, nm):
        send, recv, bufs, _ = state
        got, token = _transfer_wait([send[k] for k in ks], [recv[k] for k in ks], [[bufs[k]] for k in ks],
                                    [(1, bufs[k].shape[1] // 2) for k in ks], after, name="ag_waitd_" + nm)
        return [g[0] for g in got], token

    proj_w = lambda tn, tk: pl.BlockSpec((tn, tk), lambda j, i, k, s: (s[j], k))
    proj_cols = lambda tm, tn: [pl.BlockSpec((tm, tn), lambda j, i, k, s: (i, s[j]))]
    proj_half = lambda which, into, after: _matmul(
        h, w_in_t, name="mm_proj_%d" % which, tb=True, M=S, N=DIN // 2, K=D, tn=DIN // 2, b_spec=proj_w,
        out_specs=proj_cols, outs=[sds((S, DIN), F32)], slots=jnp.full((1,), which, jnp.int32), into=into,
        after=after)[0]
    proj = proj_half(0, None, [ag_kvo[3]])
    got_kvo, tok = gather_wait(ag_kvo, [0, 1], proj, "kvo")
    ag_w1 = gather_start([w1_b], tok, "ffn1", forwards=got_kvo)
    kvo_n = _forward_halves(ag_w1[2][1:], NEAR, ag_w1[3], name="ag_fwdn_kvo")
    ag_w1 = (ag_w1[0], ag_w1[1], [ag_w1[2][0]] + list(kvo_n), ag_w1[3])
    proj = proj_half(1, proj, list(kvo_n))
    kvo_d, tok = diagonal_wait(ag_w1, [1, 2], proj, "kvo")
    wkv4, wo4 = _forward_halves(kvo_d, FAR, tok, name="ag_fwdd_kvo")
    w_kv_full = wkv4.reshape(D, 2 * DM)
    w_o_full = wo4.reshape(D, D)
    (kv,) = _matmul(mem_n, w_kv_full, name="mm_kv", M=NMEM, N=2 * DM, K=D, outs=[sds((NMEM, 2 * DM), F32)])
    heads, hn, ycv = _mix_fwd(proj, kv, ws3, bs_t, ln_v_g, ln_v_b, conv_full, g_head, name="mix_fwd")
    (x2,) = _matmul(hn, w_o_full, name="mm_wo", M=S, N=D, K=D, outs=[sds((S, D), F32)],
                    epi=lambda acc, res: (acc + res,), extras=[(x2d, _tile_spec())])
    h2 = _rms_fwd(x2, g_ffn, name="rms_ffn")
    near = jnp.stack([shard, 2 * (1 - xi) + yi, 2 * xi + (1 - yi)]).astype(jnp.int32)
    far = jnp.reshape(2 * (1 - xi) + (1 - yi), (1,)).astype(jnp.int32)

    w1_shard = lambda tn, tk: pl.BlockSpec((None, tk, tn), lambda j, i, k, s: (s[j], k, 0))
    act_cols = lambda tm, tn: [pl.BlockSpec((tm, tn), lambda j, i, k, s: (i, s[j]))]
    relu2 = lambda acc: (jnp.square(jnp.maximum(acc, 0.0)),)

    got_w1, tok = gather_wait(ag_w1, [0], h2, "ffn1")
    ag_w2 = gather_start([w2_b], tok, "ffn2", forwards=got_w1)
    (w1n,) = _forward_halves([ag_w2[2][1]], NEAR, ag_w2[3], name="ag_fwdn_ffn1")
    ag_w2 = (ag_w2[0], ag_w2[1], [ag_w2[2][0], w1n], ag_w2[3])
    (act,) = _matmul(h2, w1n, name="mm_ffn1_near", M=S, N=3 * dff4, K=D, tn=dff4, b_spec=w1_shard,
                     out_specs=act_cols, outs=[sds((S, DFF), BF16)], epi=relu2, slots=near)
    w1d, tok = diagonal_wait(ag_w2, [1], act, "ffn1")
    (w14,) = _forward_halves(w1d, FAR, tok, name="ag_fwdd_ffn1")
    (act,) = _matmul(h2, w14, name="mm_ffn1_far", M=S, N=dff4, K=D, tn=dff4, b_spec=w1_shard, out_specs=act_cols,
                     outs=[sds((S, DFF), BF16)], epi=relu2, slots=far, into=act)

    act_shard = lambda tm, tk: pl.BlockSpec((tm, tk), lambda j, i, k, s: (i, s[k]))
    w2_shard = lambda tn, tk: pl.BlockSpec((None, tk, tn), lambda j, i, k, s: (s[k], 0, j))
    got_w2, tok = gather_wait(ag_w2, [0], act, "ffn2")
    ag_w2d = gather_start([], tok, "ffn2d", forwards=got_w2)
    (w2n,) = _forward_halves(ag_w2d[2], NEAR, ag_w2d[3], name="ag_fwdn_ffn2")
    ag_w2d = (ag_w2d[0], ag_w2d[1], [w2n], ag_w2d[3])
    (x3,) = _matmul(act, w2n, name="mm_ffn2_near", M=S, N=D, K=3 * dff4, tm=2 * TM, tk=dff4,
                    a_spec=act_shard, b_spec=w2_shard, outs=[sds((S, D), F32)], epi=lambda acc, res: (acc + res,),
                    extras=[(x2, _tile_spec())], slots=near)
    w2d, tok = diagonal_wait(ag_w2d, [0], x3, "ffn2")
    (w24,) = _forward_halves(w2d, FAR, tok, name="ag_fwdd_ffn2")
    (x3,) = _matmul(act, w24, name="mm_ffn2_far", M=S, N=D, K=dff4, tm=2 * TM, tk=dff4, a_spec=act_shard,
                    b_spec=w2_shard, outs=[sds((S, D), F32)], epi=lambda acc, res: (acc + res,),
                    extras=[(x3, _tile_spec())], slots=far)
    w2_full = w24.reshape(DFF, D)

    ci = lax.axis_index("c")

    def rs_sibling(g4, nm):
        return _sibling_start([g4], False, 1 + big_names.index(nm), name="rs_sib_" + nm)

    def rs_chips(state, after, nm):
        send, recv, g4, land, _ = state
        (((land_, g4_),), _) = _transfer_wait(send, recv, [[land[0], g4[0]]], [(NCHIP, land[0].shape[1])], after,
                                             name="rs_sibwait_" + nm)
        part, buf = _select_half_bf16(g4_, ci, land_, slot, name="rs_add_" + nm)
        return _scatter_start([part], [buf], 1 + 2 * len(big_names) + big_names.index(nm), name="rs_start_" + nm)

    def rs_end(state, after, nm):
        send, recv, parts, bufs, _ = state
        (((buf, _),), _) = _transfer_wait(send, recv, [[bufs[0], parts[0]]], [(N_PEER_CHIPS, bufs[0].shape[1])], after,
                                          name="rs_wait_" + nm)
        return _sibling_start([buf], True, 1 + len(big_names) + big_names.index(nm), name="rs_share_" + nm)

    big_m = [m_w_in[0].T, m_w_kv[0], m_w_o[0], m_w_ffn1[0], m_w_ffn2[0]]
    big_v = [v_w_in[0].T, v_w_kv[0], v_w_o[0], v_w_ffn1[0], v_w_ffn2[0]]
    big_out = {}

    def rs_finish(k, state, after):
        send, recv, mine, land, _ = state
        nm = big_names[k]
        (((land_, mine_),), _) = _transfer_wait(send, recv, [[land[0], mine[0]]], [(NCHIP, land[0].shape[1])], after,
                                               name="rs_sharewait_" + nm)
        big_out[nm] = _adamw(big[k], mine_, land_, big_m[k], big_v[k], core, name="adamw_" + nm)
        return big_out[nm][1]

    dx3, dx3b, dg_final, loss11 = _loss_bwd(x3, g_final2, tgt, name="loss_bwd")
    (dw2,) = _matmul(act, dx3b, name="mm_dw2", ta=True, M=DFF, N=D, K=S, tn=D, outs=[sds((DFF, D), BF16)])
    sib_w2 = rs_sibling(dw2.reshape(NCHIP, dff4, D), "w_ffn2")
    (dfb,) = _matmul(dx3b, w2_full, name="mm_dact", tb=True, M=S, N=DFF, K=D, tn=dff4, outs=[sds((S, DFF), BF16)],
                     epi=lambda acc, a: (acc * (2.0 * jnp.sqrt(a.astype(F32))),), extras=[(act, _tile_spec())],
                     after=[sib_w2[4]])
    rs_w2 = rs_chips(sib_w2, dfb, "w_ffn2")

    def dw1_out(tm, tn):
        nb = dff4 // tn
        return [pl.BlockSpec((None, tm, tn), lambda j, i, k: (j // nb, i, j % nb))]

    (dw1,) = _matmul(h2, dfb, name="mm_dw1", ta=True, M=D, N=DFF, K=S, tn=dff4, outs=[sds((NCHIP, D, dff4), BF16)],
                     out_specs=dw1_out, after=[rs_w2[4]])
    sib_w1 = rs_sibling(dw1, "w_ffn1")

    def w1_rows(tn, tk):
        kb = dff4 // tk
        return pl.BlockSpec((None, tn, tk), lambda j, i, k: (k // kb, j, k % kb))

    (dh2,) = _matmul(dfb, w14, name="mm_dh2", tb=True, M=S, N=D, K=DFF, tm=2 * TM, b_spec=w1_rows,
                     outs=[sds((S, D), F32)], after=[sib_w1[4]])
    rs_w1 = rs_chips(sib_w1, dh2, "w_ffn1")
    dx2, dx2b, dg_ffn = _rms_bwd(dh2, x2, g_ffn, dx3, name="rms_ffn_bwd", after=[rs_w1[4]])
    (dwo,) = _matmul(hn, dx2b, name="mm_dwo", ta=True, M=D, N=D, K=S, outs=[sds((D, D), BF16)])
    sib_wo = rs_sibling(dwo.reshape(NCHIP, D // NCHIP, D), "w_o")
    (dhn,) = _matmul(dx2b, w_o_full, name="mm_dhn", tb=True, M=S, N=D, K=D, outs=[sds((S, D), F32)],
                     after=[sib_wo[4]])
    rs_wo = rs_chips(sib_wo, dhn, "w_o")
    sh_w2 = rs_end(rs_w2, rs_wo[4], "w_ffn2")
    dproj, dkv, dws, dbs8, dlng, dlnb, dcw8, dgh = _mix_bwd(
        dhn, heads, proj, ycv, kv, ws3, bs_t, ln_v_g, ln_v_b, conv_full, g_head, sh_w2[4], name="mix_bwd")
    (dwin_t,) = _matmul(dproj, h, name="mm_dwin", ta=True, M=DIN, N=D, K=S, tm=DIN // 2, outs=[sds((DIN, D), BF16)])
    sib_win = rs_sibling(dwin_t.reshape(NCHIP, din4, D), "w_in")
    (dwkv,) = _matmul(mem_n, dkv, name="mm_dwkv", ta=True, M=D, N=2 * DM, K=NMEM, outs=[sds((D, 2 * DM), BF16)],
                      after=[sib_win[4]])
    sib_wkv = rs_sibling(dwkv.reshape(NCHIP, D // NCHIP, 2 * DM), "w_kv")
    (dh,) = _matmul(dproj, w_in_t, name="mm_dh", M=S, N=D, K=DIN, tk=DIN, outs=[sds((S, D), F32)],
                    after=[sib_wkv[4]])
    rs_win = rs_chips(sib_win, dh, "w_in")
    rs_wkv = rs_chips(sib_wkv, rs_win[4], "w_kv")
    dx, dg_mix = _rms_bwd(dh, x2d, g_mix, dx2, name="rms_mix_bwd", want_bf=False, after=[rs_wkv[4]])
    sh_w1 = rs_end(rs_w1, dx, "w_ffn1")
    (dmem_n,) = _matmul(dkv, w_kv_full, name="mm_dmem", tb=True, M=NMEM, N=D, K=2 * DM, outs=[sds((NMEM, D), F32)],
                        after=[sh_w1[4]])
    (dg_mem,) = _rms_bwd(dmem_n, mem2d, g_mem, None, name="rms_mem_bwd", want_dx=False)
    sh_wo = rs_end(rs_wo, dg_mem, "w_o")
    done = rs_finish(4, sh_w2, sh_wo[4])
    done = rs_finish(3, sh_w1, done)
    sh_win = rs_end(rs_win, done, "w_in")
    sh_wkv = rs_end(rs_wkv, sh_win[4], "w_kv")
    done = rs_finish(2, sh_wo, sh_wkv[4])
    done = rs_finish(0, sh_win, done)
    done = rs_finish(1, sh_wkv, done)

    small_names = ["g_mix", "ln_v_g", "ln_v_b", "w_s", "b_s", "conv_w", "g_mem", "g_head", "g_ffn", "g_final"]
    small_part = [dg_mix, dlng, dlnb, dws, dbs8[:, 0, :], dcw8[:3], dg_mem, dgh, dg_ffn, dg_final, loss11]
    small_shapes = [(1, D), (1, DS), (1, DS), (NSH, CHUNK, CHUNK), (NSH, CHUNK), (3, DC), (1, D), (1, D), (1, D), (1, D),
                    (1, 1)]
    total = _allreduce_small(_pack(small_part), done, name="allreduce_small")
    small_g = _unpack(total, small_shapes)
    loss = small_g.pop()[0, 0]
    small_g[5] = lax.dynamic_slice(small_g[5], (0, shard * dcv4), (3, dcv4))
    small_w = [g_mix, ln_v_g, ln_v_b, ws3, bs2, conv_w[0], g_mem, g_head, g_ffn, g_final2]
    small_m = [m_g_mix, m_ln_v_g, m_ln_v_b, m_w_s[0], m_b_s[0], m_conv_w[0], m_g_mem, m_g_head, m_g_ffn,
               m_g_final.reshape(1, D)]
    small_v = [v_g_mix, v_ln_v_g, v_ln_v_b, v_w_s[0], v_b_s[0], v_conv_w[0], v_g_mem, v_g_head, v_g_ffn,
               v_g_final.reshape(1, D)]
    s_delta, s_m, s_v = _adamw_small(small_w, small_g, small_m, small_v, name="adamw_small")
    small_out = {nm: (g, d, mn, vn) for nm, g, d, mn, vn in zip(small_names, small_g, s_delta, s_m, s_v)}

    order = ["g_mix", "w_in", "ln_v_g", "ln_v_b", "w_s", "b_s", "conv_w", "g_mem", "w_kv", "g_head", "w_o",
             "g_ffn", "w_ffn1", "w_ffn2", "g_final"]
    like = dict(g_mix=g_mix, w_in=w_in, ln_v_g=ln_v_g, ln_v_b=ln_v_b, w_s=w_s, b_s=b_s, conv_w=conv_w, g_mem=g_mem,
                w_kv=w_kv, g_head=g_head, w_o=w_o, g_ffn=g_ffn, w_ffn1=w_ffn1, w_ffn2=w_ffn2, g_final=g_final)
    res = {**big_out, **small_out}
    res["w_in"] = [a.T for a in res["w_in"]]
    outs = [loss, dx[None]]
    for k in range(4):
        outs += [res[nm][k].reshape(like[nm].shape) for nm in order]
    return tuple(outs)
```

```python
import math

import jax
import jax.numpy as jnp
from jax import lax
from jax.experimental import pallas as pl
from jax.experimental.pallas import tpu as pltpu

F32 = jnp.float32
BF16 = jnp.bfloat16
MESH = pl.DeviceIdType.MESH

D = 2048
S = 2048
HD = 128
NH = D // HD
NMH = 4
NSH = (NH - NMH) // 2
NCH = NH - NMH - NSH
DS = NSH * HD
DC = NCH * HD
DM = NMH * HD
DIN = 2 * DS + 3 * DC + DM
CHUNK = 128
NMEM = 256
DFF = 4 * D
EPS = 1e-6
NCHIP = 4
SCALE = HD ** -0.5

ADAM_LR = 0.001
ADAM_B1 = 0.9
ADAM_B2 = 0.999
ADAM_EPS = 1e-08
ADAM_WD = 0.01
ADAM_STEP = 10

TR_EW = 256
TR_MIX = 256
TM = 512
TN = 1024
TK = 2048
N_SUB = 512
VMEM_MB = 56
HALO = 8


def _pick(n, target, q=128):
    best = None
    for t in range(q, min(n, target) + 1, q):
        if n % t == 0:
            best = t
    return n if best is None else best


def _pick_rows(n, q):
    below = _pick(n, TR_EW, q)
    if 2 * below >= TR_EW:
        return below
    above = [t for t in range(TR_EW, min(n, 4 * TR_EW) + 1, q) if n % t == 0]
    return above[0] if above else below


def _cp(sem=None, vmem_mb=None, **kw):
    d = dict(kw)
    if sem is not None:
        d["dimension_semantics"] = sem
    if vmem_mb is not None:
        d["vmem_limit_bytes"] = vmem_mb << 20
    return pltpu.CompilerParams(**d)


def _gelu(x):
    z = 0.7978845608028654 * (x + 0.044715 * (x * x * x))
    return 0.5 * x * (1.0 + jnp.tanh(z))


def _gelu_with_grad(x):
    x2 = x * x
    t = jnp.tanh(0.7978845608028654 * (x + 0.044715 * (x2 * x)))
    half = 0.5 * (1.0 + t)
    return x * half, half + 0.5 * x * (1.0 - t * t) * (0.7978845608028654 * (1.0 + 3.0 * 0.044715 * x2))


def _matmul(a, b, *, name, ta=False, tb=False, M, N, K, tm=None, tn=None, tk=None, outs, epi=None,
            extras=(), a_spec=None, b_spec=None, out_specs=None, after=(), n_split=None, slots=None, into=None):
    n_after = len(after)
    tm = _pick(M, TM if tm is None else tm, 8)
    tn = _pick(N, TN if tn is None else tn)
    tk = _pick(K, TK if tk is None else tk)
    if n_split is None:
        n_split = tn // N_SUB if tn % N_SUB == 0 else 1
    nk = K // tk
    grid = (N // tn, M // tm, nk)
    if a_spec is None:
        a_spec = (pl.BlockSpec((tk, tm), lambda j, i, k, *s: (k, i)) if ta
                  else pl.BlockSpec((tm, tk), lambda j, i, k, *s: (i, k)))
    else:
        a_spec = a_spec(tm, tk)
    if b_spec is None:
        b_spec = (pl.BlockSpec((tn, tk), lambda j, i, k, *s: (j, k)) if tb
                  else pl.BlockSpec((tk, tn), lambda j, i, k, *s: (k, j)))
    else:
        b_spec = b_spec(tn, tk)
    if out_specs is None:
        out_specs = [pl.BlockSpec((tm, tn), lambda j, i, k, *s: (i, j)) for _ in outs]
    else:
        out_specs = out_specs(tm, tn)
    dn = (((0 if ta else 1,), (1 if tb else 0,)), ((), ()))
    n_ex, n_out = len(extras), len(outs)
    n_pre = 0 if slots is None else 1
    n_into = 0 if into is None else 1
    ns = tn // n_split

    def body(*refs):
        a_ref, b_ref = refs[n_pre], refs[n_pre + 1]
        ex = refs[n_pre + 2:n_pre + 2 + n_ex]
        first_out = n_pre + 2 + n_ex + n_after + n_into
        o = refs[first_out:first_out + n_out]
        acc = refs[first_out + n_out:]
        k = pl.program_id(2)

        def finish(val, cols):
            res = (val,) if epi is None else epi(val, *[e[:, cols] for e in ex])
            for r, o_ref in zip(res, o):
                o_ref[:, cols] = r.astype(o_ref.dtype)

        if nk > 1:
            @pl.when(k == 0)
            def _():
                acc[0][...] = jnp.zeros_like(acc[0])

        av = a_ref[...].astype(BF16)
        for q in range(n_split):
            cols = slice(q * ns, (q + 1) * ns)
            bq = (b_ref[cols, :] if tb else b_ref[:, cols]).astype(BF16)
            part = lax.dot_general(av, bq, dn, preferred_element_type=F32)
            if nk == 1:
                finish(part, cols)
            else:
                acc[0][:, cols] += part

        if nk > 1:
            @pl.when(k == nk - 1)
            def _():
                finish(acc[0][...], slice(0, tn))

    in_specs = ([a_spec, b_spec] + [sp(tm, tn) for _, sp in extras] + [ANY] * (n_after + n_into))
    scratch = [pltpu.VMEM((tm, tn), F32)] if nk > 1 else []
    args = [a, b] + [arr for arr, _ in extras] + list(after) + ([] if into is None else [into])
    aliases = {} if into is None else {n_pre + len(args) - 1: 0}
    params = _cp(("parallel", "parallel", "arbitrary"), VMEM_MB)
    if slots is None:
        return pl.pallas_call(body, name=name, grid=grid, in_specs=in_specs, out_specs=out_specs, out_shape=outs,
                              scratch_shapes=scratch, input_output_aliases=aliases, compiler_params=params)(*args)
    return pl.pallas_call(
        body, name=name,
        grid_spec=pltpu.PrefetchScalarGridSpec(num_scalar_prefetch=1, grid=grid, in_specs=in_specs,
                                               out_specs=out_specs, scratch_shapes=scratch),
        out_shape=outs, input_output_aliases=aliases, compiler_params=params)(slots, *args)


def _tile_spec():
    return lambda tm, tn: pl.BlockSpec((tm, tn), lambda j, i, k, *s: (i, j))


def _cast_into_slot(w, slot, after, *, name):
    R, C = w.shape
    tr = _pick_rows(R, 16)

    def body(s_ref, w_ref, _after_ref, o_ref):
        o_ref[...] = w_ref[...].astype(BF16)

    return pl.pallas_call(
        body, name=name,
        grid_spec=pltpu.PrefetchScalarGridSpec(
            num_scalar_prefetch=1, grid=(R // tr,),
            in_specs=[pl.BlockSpec((tr, C), lambda i, s: (i, 0)), ANY],
            out_specs=pl.BlockSpec((None, tr, C), lambda i, s: (s[0], i, 0))),
        out_shape=jax.ShapeDtypeStruct((NCHIP, R, C), BF16),
        compiler_params=_cp(("parallel",), VMEM_MB),
    )(slot, w, after)


def _rms_fwd(x, g, *, name, after=()):
    R, C = x.shape
    tr = _pick(R, TR_EW, 16)
    n_after = len(after)

    def body(x_ref, g_ref, *rest):
        o_ref = rest[n_after]
        xv = x_ref[...]
        r = lax.rsqrt(jnp.mean(xv * xv, axis=-1, keepdims=True) + EPS)
        o_ref[...] = ((xv * r) * g_ref[...]).astype(BF16)

    return pl.pallas_call(
        body, name=name, grid=(R // tr,),
        in_specs=[pl.BlockSpec((tr, C), lambda i: (i, 0)), pl.BlockSpec((1, C), lambda i: (0, 0))] + [ANY] * n_after,
        out_specs=pl.BlockSpec((tr, C), lambda i: (i, 0)),
        out_shape=jax.ShapeDtypeStruct((R, C), BF16),
        compiler_params=_cp(("parallel",), VMEM_MB),
    )(x, g, *after)


def _rms_bwd(dh, x, g, dres, *, name, want_dx=True, want_bf=True, after=()):
    R, C = x.shape
    tr = _pick(R, TR_EW, 16)
    has_res = dres is not None
    row = pl.BlockSpec((tr, C), lambda i: (i, 0))
    vec = pl.BlockSpec((1, C), lambda i: (0, 0))

    def body(*refs):
        dh_ref, x_ref, g_ref = refs[:3]
        pos = 3
        dres_ref = None
        if has_res:
            dres_ref = refs[pos]
            pos += 1
        outs = refs[pos + len(after):]
        i = pl.program_id(0)
        xv = x_ref[...]
        r = lax.rsqrt(jnp.mean(xv * xv, axis=-1, keepdims=True) + EPS)
        xh = xv * r
        dhv = dh_ref[...]
        dg_ref = outs[-1]
        dgp = jnp.sum(dhv * xh, axis=0, keepdims=True)

        @pl.when(i == 0)
        def _():
            dg_ref[...] = dgp

        @pl.when(i > 0)
        def _():
            dg_ref[...] += dgp

        if want_dx:
            t = dhv * g_ref[...]
            dx = r * (t - xh * jnp.mean(t * xh, axis=-1, keepdims=True))
            if has_res:
                dx = dx + dres_ref[...]
            outs[0][...] = dx
            if want_bf:
                outs[1][...] = dx.astype(BF16)

    in_specs = [row, row, vec] + ([row] if has_res else []) + [ANY] * len(after)
    out_specs, out_shape = [], []
    if want_dx:
        out_specs.append(row)
        out_shape.append(jax.ShapeDtypeStruct((R, C), F32))
        if want_bf:
            out_specs.append(row)
            out_shape.append(jax.ShapeDtypeStruct((R, C), BF16))
    out_specs.append(vec)
    out_shape.append(jax.ShapeDtypeStruct((1, C), F32))
    args = [dh, x, g] + ([dres] if has_res else []) + list(after)
    return pl.pallas_call(
        body, name=name, grid=(R // tr,), in_specs=in_specs, out_specs=out_specs, out_shape=out_shape,
        compiler_params=_cp(("arbitrary",), VMEM_MB),
    )(*args)


def _loss_bwd(x3, g, tgt, *, name):
    R, C = x3.shape
    tr = _pick(R, TR_EW, 16)
    n = R // tr
    row = pl.BlockSpec((tr, C), lambda i: (i, 0))
    vec = pl.BlockSpec((1, C), lambda i: (0, 0))

    def body(x_ref, g_ref, t_ref, dx_ref, dxb_ref, dg_ref, loss_ref, acc_ref):
        i = pl.program_id(0)
        xv = x_ref[...]
        gv = g_ref[...]
        r = lax.rsqrt(jnp.mean(xv * xv, axis=-1, keepdims=True) + EPS)
        xh = xv * r
        e = xh * gv - t_ref[...]
        dy = e * (1.0 / C)
        sq = jnp.sum(e * e, axis=0, keepdims=True)
        dgp = jnp.sum(dy * xh, axis=0, keepdims=True)

        @pl.when(i == 0)
        def _():
            acc_ref[...] = sq
            dg_ref[...] = dgp

        @pl.when(i > 0)
        def _():
            acc_ref[...] += sq
            dg_ref[...] += dgp

        t = dy * gv
        dx = r * (t - xh * jnp.mean(t * xh, axis=-1, keepdims=True))
        dx_ref[...] = dx
        dxb_ref[...] = dx.astype(BF16)

        @pl.when(i == n - 1)
        def _():
            loss_ref[...] = jnp.sum(acc_ref[...], axis=-1, keepdims=True) * (0.5 / C)

    return pl.pallas_call(
        body, name=name, grid=(n,),
        in_specs=[row, vec, row],
        out_specs=[row, row, vec, pl.BlockSpec((1, 1), lambda i: (0, 0))],
        out_shape=[jax.ShapeDtypeStruct((R, C), F32), jax.ShapeDtypeStruct((R, C), BF16),
                   jax.ShapeDtypeStruct((1, C), F32), jax.ShapeDtypeStruct((1, 1), F32)],
        scratch_shapes=[pltpu.VMEM((1, C), F32)],
        compiler_params=_cp(("arbitrary",), VMEM_MB),
    )(x3, g, tgt)


def _offsets():
    u0 = 0
    v0 = DS
    b0 = 2 * DS
    c0 = b0 + DC
    x0 = c0 + DC
    q0 = x0 + DC
    return u0, v0, b0, c0, x0, q0


def _tri_mask(lower):
    r = lax.broadcasted_iota(jnp.int32, (CHUNK, CHUNK), 0)
    c = lax.broadcasted_iota(jnp.int32, (CHUNK, CHUNK), 1)
    return (r >= c) if lower else (c >= r)


def _layer_norm_stats(vg):
    mu = jnp.mean(vg, axis=-1, keepdims=True)
    vc = vg - mu
    rstd = lax.rsqrt(jnp.mean(vc * vc, axis=-1, keepdims=True) + EPS)
    return vc * rstd, rstd


def _softmax_rows(qh, kh):
    s = lax.dot_general(qh, kh, (((1,), (1,)), ((), ())), preferred_element_type=F32)
    m = jnp.max(s, axis=-1, keepdims=True)
    e = jnp.exp(s - m)
    return e / jnp.sum(e, axis=-1, keepdims=True)


def _mix_fwd(proj, kv, w_s, bs_t, ln_g, ln_b, conv_w, g_head, *, name):
    assert DS == DC
    tr = _pick(S, TR_MIX, CHUNK)
    n = S // tr
    nck = tr // CHUNK
    u0, v0, b0, c0, x0, q0 = _offsets()
    hb = tr // HALO

    def body(p_ref, cprev_ref, xprev_ref, kv_ref, ws_ref, bst_ref, lng_ref, lnb_ref, cw_ref, gh_ref,
             heads_ref, hn_ref, ycv_ref, buf_ref):
        i = pl.program_id(0)

        def emit(col, val):
            rs = lax.rsqrt(jnp.mean(val * val, axis=-1, keepdims=True) + EPS)
            heads_ref[:, col:col + HD] = val
            hn_ref[:, col:col + HD] = ((val * rs) * gh_ref[:, col:col + HD]).astype(BF16)

        vhat, _ = _layer_norm_stats(_gelu(p_ref[:, v0:v0 + DS]))
        vnb = (vhat * lng_ref[...] + lnb_ref[...]).astype(BF16)
        low = _tri_mask(True)
        for h in range(NSH):
            wt = jnp.where(low, ws_ref[h], 0.0).astype(BF16)
            bcol = bst_ref[:, h:h + 1]
            parts = []
            for c in range(nck):
                blk = vnb[c * CHUNK:(c + 1) * CHUNK, h * HD:(h + 1) * HD]
                parts.append(jnp.dot(wt, blk, preferred_element_type=F32) + bcol)
            mixed = parts[0] if nck == 1 else jnp.concatenate(parts, axis=0)
            emit(h * HD, _gelu(p_ref[:, u0 + h * HD:u0 + (h + 1) * HD]) * mixed)

        xc = p_ref[:, c0:c0 + DC] * p_ref[:, x0:x0 + DC]
        prev = cprev_ref[...] * xprev_ref[...]
        buf_ref[0:HALO, :] = jnp.where(i > 0, prev, 0.0)
        buf_ref[HALO:HALO + tr, :] = xc
        y = (cw_ref[2:3, :] * xc + cw_ref[1:2, :] * buf_ref[HALO - 1:HALO - 1 + tr, :]
             + cw_ref[0:1, :] * buf_ref[HALO - 2:HALO - 2 + tr, :])
        ycv_ref[...] = y
        cout = p_ref[:, b0:b0 + DC] * y
        for h in range(NCH):
            emit(DS + h * HD, cout[:, h * HD:(h + 1) * HD])

        for h in range(NMH):
            qh = (p_ref[:, q0 + h * HD:q0 + (h + 1) * HD] * SCALE).astype(BF16)
            kh = kv_ref[:, h * HD:(h + 1) * HD].astype(BF16)
            vh = kv_ref[:, DM + h * HD:DM + (h + 1) * HD].astype(BF16)
            p = _softmax_rows(qh, kh)
            emit(DS + DC + h * HD, jnp.dot(p.astype(BF16), vh, preferred_element_type=F32))

    full = lambda shape: pl.BlockSpec(shape, lambda i: (0,) * len(shape))
    halo_c = pl.BlockSpec((HALO, DC), lambda i: (jnp.maximum(i * hb - 1, 0), c0 // DC))
    halo_x = pl.BlockSpec((HALO, DC), lambda i: (jnp.maximum(i * hb - 1, 0), x0 // DC))
    return pl.pallas_call(
        body, name=name, grid=(n,),
        in_specs=[pl.BlockSpec((tr, DIN), lambda i: (i, 0)), halo_c, halo_x,
                  full((NMEM, 2 * DM)), full((NSH, CHUNK, CHUNK)), full((CHUNK, NSH)),
                  full((1, DS)), full((1, DS)), full((3, DC)), full((1, D))],
        out_specs=[pl.BlockSpec((tr, D), lambda i: (i, 0)), pl.BlockSpec((tr, D), lambda i: (i, 0)),
                   pl.BlockSpec((tr, DC), lambda i: (i, 0))],
        out_shape=[jax.ShapeDtypeStruct((S, D), F32), jax.ShapeDtypeStruct((S, D), BF16),
                   jax.ShapeDtypeStruct((S, DC), F32)],
        scratch_shapes=[pltpu.VMEM((tr + HALO, DC), F32)],
        compiler_params=_cp(("parallel",), VMEM_MB),
    )(proj, proj, proj, kv, w_s, bs_t, ln_g, ln_b, conv_w, g_head)


def _mix_bwd(dhn, heads, proj, ycv, kv, w_s, bs_t, ln_g, ln_b, conv_w, g_head, after, *, name):
    assert DS == DC
    tr = _pick(S, TR_MIX, CHUNK)
    n = S // tr
    nck = tr // CHUNK
    u0, v0, b0, c0, x0, q0 = _offsets()
    hb = tr // HALO
    last_hb = S // HALO - 1

    def body(dhn_ref, heads_ref, p_ref, ycv_ref, dhn_nx_ref, heads_nx_ref, b_nx_ref, kv_ref, ws_ref, bst_ref,
             lng_ref, lnb_ref, cw_ref, gh_ref, _after_ref,
             dp_ref, dkv_ref, dws_ref, dbs_ref, dlng_ref, dlnb_ref, dcw_ref, dgh_ref, buf_ref, dvn_ref):
        i = pl.program_id(0)

        @pl.when(i == 0)
        def _():
            dkv_ref[...] = jnp.zeros_like(dkv_ref)
            dws_ref[...] = jnp.zeros_like(dws_ref)
            dbs_ref[...] = jnp.zeros_like(dbs_ref)
            dlng_ref[...] = jnp.zeros_like(dlng_ref)
            dlnb_ref[...] = jnp.zeros_like(dlnb_ref)
            dcw_ref[...] = jnp.zeros_like(dcw_ref)
            dgh_ref[...] = jnp.zeros_like(dgh_ref)

        def head_bwd(a, dn, gh):
            rs = lax.rsqrt(jnp.mean(a * a, axis=-1, keepdims=True) + EPS)
            ah = a * rs
            t = dn * gh
            return rs * (t - ah * jnp.mean(t * ah, axis=-1, keepdims=True)), jnp.sum(dn * ah, axis=0, keepdims=True)

        def head_grad(col):
            da, dg = head_bwd(heads_ref[:, col:col + HD], dhn_ref[:, col:col + HD], gh_ref[:, col:col + HD])
            dgh_ref[:, col:col + HD] += dg
            return da

        vg, dvg_dv = _gelu_with_grad(p_ref[:, v0:v0 + DS])
        vhat, rstd = _layer_norm_stats(vg)
        vnb = (vhat * lng_ref[...] + lnb_ref[...]).astype(BF16)
        low = _tri_mask(True)
        ones = jnp.ones((HALO, HD), BF16)
        for h in range(NSH):
            w_h = ws_ref[h]
            wt = jnp.where(low, w_h, 0.0).astype(BF16)
            bcol = bst_ref[:, h:h + 1]
            da = head_grad(h * HD)
            ug, dug_du = _gelu_with_grad(p_ref[:, u0 + h * HD:u0 + (h + 1) * HD])
            dws = jnp.zeros((CHUNK, CHUNK), F32)
            dbs = jnp.zeros((HALO, CHUNK), F32)
            mixed_parts = []
            for c in range(nck):
                rows = slice(c * CHUNK, (c + 1) * CHUNK)
                blk = vnb[rows, h * HD:(h + 1) * HD]
                mixed_parts.append(jnp.dot(wt, blk, preferred_element_type=F32) + bcol)
                dmb = (da[rows] * ug[rows]).astype(BF16)
                dws = dws + lax.dot_general(dmb, blk, (((1,), (1,)), ((), ())), preferred_element_type=F32)
                dbs = dbs + lax.dot_general(ones, dmb, (((1,), (1,)), ((), ())), preferred_element_type=F32)
                dvn_ref[c * CHUNK:(c + 1) * CHUNK, h * HD:(h + 1) * HD] = lax.dot_general(
                    wt, dmb, (((0,), (0,)), ((), ())), preferred_element_type=F32)
            mixed = mixed_parts[0] if nck == 1 else jnp.concatenate(mixed_parts, axis=0)
            dp_ref[:, u0 + h * HD:u0 + (h + 1) * HD] = ((da * mixed) * dug_du).astype(BF16)
            dws_ref[h] += jnp.where(low, dws, 0.0)
            dbs_ref[h] += dbs
        dvn = dvn_ref[...]
        dlng_ref[...] += jnp.sum(dvn * vhat, axis=0, keepdims=True)
        dlnb_ref[...] += jnp.sum(dvn, axis=0, keepdims=True)
        dvh = dvn * lng_ref[...]
        dvg = rstd * (dvh - jnp.mean(dvh, axis=-1, keepdims=True)
                      - vhat * jnp.mean(dvh * vhat, axis=-1, keepdims=True))
        dp_ref[:, v0:v0 + DS] = (dvg * dvg_dv).astype(BF16)

        dc = jnp.concatenate([head_grad(DS + h * HD) for h in range(NCH)], axis=1)
        dc_nx = jnp.concatenate(
            [head_bwd(heads_nx_ref[:, h * HD:(h + 1) * HD], dhn_nx_ref[:, h * HD:(h + 1) * HD],
                      gh_ref[:, DS + h * HD:DS + (h + 1) * HD])[0] for h in range(NCH)], axis=1)
        bg = p_ref[:, b0:b0 + DC]
        cg = p_ref[:, c0:c0 + DC]
        xin = p_ref[:, x0:x0 + DC]
        dp_ref[:, b0:b0 + DC] = (dc * ycv_ref[...]).astype(BF16)
        dyv = dc * bg
        buf_ref[0:tr, :] = dyv
        buf_ref[tr:tr + HALO, :] = jnp.where(i < n - 1, dc_nx * b_nx_ref[...], 0.0)
        sh1 = buf_ref[1:1 + tr, :]
        sh0 = buf_ref[2:2 + tr, :]
        dxc = cw_ref[2:3, :] * dyv + cw_ref[1:2, :] * sh1 + cw_ref[0:1, :] * sh0
        xc = cg * xin
        dp_ref[:, c0:c0 + DC] = (dxc * xin).astype(BF16)
        dp_ref[:, x0:x0 + DC] = (dxc * cg).astype(BF16)
        dcw_ref[0:1, :] += jnp.sum(sh0 * xc, axis=0, keepdims=True)
        dcw_ref[1:2, :] += jnp.sum(sh1 * xc, axis=0, keepdims=True)
        dcw_ref[2:3, :] += jnp.sum(dyv * xc, axis=0, keepdims=True)

        for h in range(NMH):
            do = head_grad(DS + DC + h * HD).astype(BF16)
            qh = (p_ref[:, q0 + h * HD:q0 + (h + 1) * HD] * SCALE).astype(BF16)
            kh = kv_ref[:, h * HD:(h + 1) * HD].astype(BF16)
            vh = kv_ref[:, DM + h * HD:DM + (h + 1) * HD].astype(BF16)
            p = _softmax_rows(qh, kh)
            dpr = lax.dot_general(do, vh, (((1,), (1,)), ((), ())), preferred_element_type=F32)
            ds = (p * (dpr - jnp.sum(dpr * p, axis=-1, keepdims=True))).astype(BF16)
            dp_ref[:, q0 + h * HD:q0 + (h + 1) * HD] = (
                jnp.dot(ds, kh, preferred_element_type=F32) * SCALE).astype(BF16)
            dkv_ref[:, h * HD:(h + 1) * HD] += lax.dot_general(
                ds, qh, (((0,), (0,)), ((), ())), preferred_element_type=F32)
            dkv_ref[:, DM + h * HD:DM + (h + 1) * HD] += lax.dot_general(
                p.astype(BF16), do, (((0,), (0,)), ((), ())), preferred_element_type=F32)

    full = lambda shape: pl.BlockSpec(shape, lambda i: (0,) * len(shape))
    row = lambda c: pl.BlockSpec((tr, c), lambda i: (i, 0))
    nxt = lambda col: pl.BlockSpec((HALO, DC), lambda i: (jnp.minimum((i + 1) * hb, last_hb), col))
    return pl.pallas_call(
        body, name=name, grid=(n,),
        in_specs=[row(D), row(D), row(DIN), row(DC), nxt(DS // DC), nxt(DS // DC), nxt(b0 // DC),
                  full((NMEM, 2 * DM)), full((NSH, CHUNK, CHUNK)), full((CHUNK, NSH)),
                  full((1, DS)), full((1, DS)), full((3, DC)), full((1, D)), ANY],
        out_specs=[row(DIN), full((NMEM, 2 * DM)), full((NSH, CHUNK, CHUNK)), full((NSH, HALO, CHUNK)),
                   full((1, DS)), full((1, DS)), full((HALO, DC)), full((1, D))],
        out_shape=[jax.ShapeDtypeStruct((S, DIN), BF16), jax.ShapeDtypeStruct((NMEM, 2 * DM), F32),
                   jax.ShapeDtypeStruct((NSH, CHUNK, CHUNK), F32), jax.ShapeDtypeStruct((NSH, HALO, CHUNK), F32),
                   jax.ShapeDtypeStruct((1, DS), F32), jax.ShapeDtypeStruct((1, DS), F32),
                   jax.ShapeDtypeStruct((HALO, DC), F32), jax.ShapeDtypeStruct((1, D), F32)],
        scratch_shapes=[pltpu.VMEM((tr + HALO, DC), F32), pltpu.VMEM((tr, DS), F32)],
        compiler_params=_cp(("arbitrary",), VMEM_MB),
    )(dhn, heads, proj, ycv, dhn, heads, proj, kv, w_s, bs_t, ln_g, ln_b, conv_w, g_head, after)


def _place():
    x, y, c = lax.axis_index("x"), lax.axis_index("y"), lax.axis_index("c")
    chips = [(1 - x, y), (x, 1 - y), (1 - x, 1 - y)]
    return x, y, c, chips


ANY = pl.BlockSpec(memory_space=pl.ANY)


HBM = pl.BlockSpec(memory_space=pltpu.HBM)
SEM = pl.BlockSpec(memory_space=pltpu.SEMAPHORE)
EFFECT = pltpu.SideEffectType.DATAFLOW_SIDE_EFFECTING
N_PEER_CHIPS = 3
N_NEIGHBOUR_CHIPS = 2
CONV_PAD = (32, 256)


def _in_hbm(a):
    return pltpu.with_memory_space_constraint(a, pltpu.HBM)


def _allgather_start(bufs, forwards, after, collective_id, *, name):
    arrs = list(bufs) + list(forwards)
    nw, nb = len(arrs), len(bufs)

    def body(*refs):
        ins, send, recv = refs[:nw], refs[nw + 1:2 * nw + 1], refs[2 * nw + 1:3 * nw + 1]
        token = refs[4 * nw + 1]
        x, y, c, chips = _place()
        s = 2 * x + y
        slots = [2 * cx + cy for cx, cy in chips]
        _handshake([(cx, cy, c) for cx, cy in chips[:N_NEIGHBOUR_CHIPS]])
        for w in range(nb, nw):
            q = arrs[w].shape[1] // 4
            for j in range(N_NEIGHBOUR_CHIPS):
                rows = ins[w].at[slots[j], pl.ds(c * 2 * q + j * q, q)]
                pltpu.make_async_remote_copy(src_ref=rows, dst_ref=rows, send_sem=send[w], recv_sem=recv[w],
                                             device_id=(*chips[1 - j], c), device_id_type=MESH).start()
        for w in range(nb):
            hr = arrs[w].shape[1] // 2
            rows = ins[w].at[s, pl.ds(c * hr, hr)]
            for cx, cy in chips[:N_NEIGHBOUR_CHIPS]:
                pltpu.make_async_remote_copy(src_ref=rows, dst_ref=rows, send_sem=send[w], recv_sem=recv[w],
                                             device_id=(cx, cy, c), device_id_type=MESH).start()
        token[...] = jnp.zeros_like(token)

    res = pl.pallas_call(
        body, name=name,
        in_specs=[HBM] * nw + [ANY],
        out_specs=[SEM] * (2 * nw) + [HBM] * nw + [pl.BlockSpec(memory_space=pltpu.VMEM)],
        out_shape=[pltpu.SemaphoreType.DMA(())] * (2 * nw) + [pltpu.HBM(a.shape, a.dtype) for a in arrs]
        + [jax.ShapeDtypeStruct((8, 128), F32)],
        input_output_aliases={w: 2 * nw + w for w in range(nw)},
        compiler_params=pltpu.CompilerParams(has_side_effects=EFFECT, collective_id=collective_id),
    )(*[_in_hbm(a) for a in arrs], after)
    return res[:nw], res[nw:2 * nw], res[2 * nw:3 * nw], res[3 * nw]


def _handshake(peers):
    barrier = pltpu.get_barrier_semaphore()
    for peer in peers:
        pl.semaphore_signal(barrier, inc=1, device_id=peer, device_id_type=MESH)
    pl.semaphore_wait(barrier, len(peers))


def _scatter_start(parts, bufs, collective_id, *, name):
    nw = len(parts)

    def body(*refs):
        src, dst = refs[:nw], refs[nw:2 * nw]
        send, recv = refs[2 * nw:3 * nw], refs[3 * nw:4 * nw]
        token = refs[6 * nw]
        x, y, c, chips = _place()
        s = 2 * x + y
        _handshake([(cx, cy, c) for cx, cy in chips])
        for w in range(nw):
            for cx, cy in chips:
                pltpu.make_async_remote_copy(src_ref=src[w].at[2 * cx + cy], dst_ref=dst[w].at[s], send_sem=send[w],
                                             recv_sem=recv[w], device_id=(cx, cy, c), device_id_type=MESH).start()
        token[...] = jnp.zeros_like(token)

    res = pl.pallas_call(
        body, name=name,
        in_specs=[HBM] * (2 * nw),
        out_specs=[SEM] * (2 * nw) + [HBM] * (2 * nw) + [pl.BlockSpec(memory_space=pltpu.VMEM)],
        out_shape=[pltpu.SemaphoreType.DMA(())] * (2 * nw) + [pltpu.HBM(a.shape, a.dtype) for a in parts + bufs]
        + [jax.ShapeDtypeStruct((8, 128), F32)],
        input_output_aliases={k: 2 * nw + k for k in range(2 * nw)},
        compiler_params=pltpu.CompilerParams(has_side_effects=EFFECT, collective_id=collective_id),
    )(*[_in_hbm(a) for a in parts + bufs])
    return res[:nw], res[nw:2 * nw], res[2 * nw:3 * nw], res[3 * nw:4 * nw], res[4 * nw]


def _sibling_start(srcs, whole, collective_id, *, name):
    nw = len(srcs)
    lands = [lax.empty((a.shape[0], a.shape[1] if whole else a.shape[1] // 2, a.shape[2]), a.dtype) for a in srcs]

    def body(*refs):
        src, land = refs[:nw], refs[nw:2 * nw]
        send, recv = refs[2 * nw:3 * nw], refs[3 * nw:4 * nw]
        token = refs[6 * nw]
        x, y, c, _ = _place()
        _handshake([(x, y, 1 - c)])
        for w in range(nw):
            hr = srcs[w].shape[1] // 2
            rows = src[w] if whole else src[w].at[:, pl.ds((1 - c) * hr, hr)]
            pltpu.make_async_remote_copy(src_ref=rows, dst_ref=land[w], send_sem=send[w], recv_sem=recv[w],
                                         device_id=(x, y, 1 - c), device_id_type=MESH).start()
        token[...] = jnp.zeros_like(token)

    res = pl.pallas_call(
        body, name=name,
        in_specs=[HBM] * (2 * nw),
        out_specs=[SEM] * (2 * nw) + [HBM] * (2 * nw) + [pl.BlockSpec(memory_space=pltpu.VMEM)],
        out_shape=[pltpu.SemaphoreType.DMA(())] * (2 * nw) + [pltpu.HBM(a.shape, a.dtype) for a in srcs + lands]
        + [jax.ShapeDtypeStruct((8, 128), F32)],
        input_output_aliases={k: 2 * nw + k for k in range(2 * nw)},
        compiler_params=pltpu.CompilerParams(has_side_effects=EFFECT, collective_id=collective_id),
    )(*[_in_hbm(a) for a in srcs + lands])
    return res[:nw], res[nw:2 * nw], res[2 * nw:3 * nw], res[3 * nw:4 * nw], res[4 * nw]


def _transfer_wait(sends, recvs, thru, sizes, after, *, name):
    n = len(sends)
    flat = [a for group in thru for a in group]

    def body(*refs):
        bufs = refs[:len(flat)]
        send = refs[len(flat):len(flat) + n]
        recv = refs[len(flat) + n:len(flat) + 2 * n]
        token = refs[2 * len(flat) + 2 * n + 1]
        token[...] = jnp.zeros_like(token)
        x, y, c, _ = _place()
        pos = 0
        for k in range(n):
            slots, rows = sizes[k]
            region = bufs[pos].at[pl.ds(0, slots), pl.ds(0, rows)]
            pos += len(thru[k])
            cp = pltpu.make_async_remote_copy(src_ref=region, dst_ref=region, send_sem=send[k], recv_sem=recv[k],
                                              device_id=(x, y, 1 - c), device_id_type=MESH)
            cp.wait_send()
            cp.wait_recv()

    res = pl.pallas_call(
        body, name=name,
        in_specs=[HBM] * len(flat) + [SEM] * (2 * n) + [pl.BlockSpec(memory_space=pl.ANY)],
        out_specs=[HBM] * len(flat) + [pl.BlockSpec(memory_space=pltpu.VMEM)],
        out_shape=[pltpu.HBM(a.shape, a.dtype) for a in flat] + [jax.ShapeDtypeStruct((8, 128), F32)],
        input_output_aliases={k: k for k in range(len(flat))},
        compiler_params=pltpu.CompilerParams(has_side_effects=EFFECT),
    )(*flat, *sends, *recvs, after)
    out, pos = [], 0
    for group in thru:
        out.append(res[pos:pos + len(group)])
        pos += len(group)
    return out, res[len(flat)]


def _forward_gathered(bufs, after, *, name):
    nw = len(bufs)

    def body(*refs):
        outs = refs[nw + 1:2 * nw + 1]
        d_send, d_recv, i_send, i_recv = refs[2 * nw + 1:]
        x, y, c, chips = _place()
        me, sibling = (x, y, c), (x, y, 1 - c)
        slots = [2 * cx + cy for cx, cy in chips]

        def rows(w, j, start, n):
            return outs[w].at[slots[j], pl.ds(start, n)]

        def d2d(w, j, which, to):
            hr = bufs[w].shape[1] // 2
            r = rows(w, j, which * hr, hr)
            return pltpu.make_async_remote_copy(
                src_ref=r, dst_ref=r, send_sem=d_send.at[N_PEER_CHIPS * w + j],
                recv_sem=d_recv.at[N_PEER_CHIPS * w + j], device_id=to, device_id_type=MESH)

        def ici(w, j, slot_j, to):
            q = bufs[w].shape[1] // 4
            r = rows(w, slot_j, c * 2 * q + j * q, q)
            return pltpu.make_async_remote_copy(
                src_ref=r, dst_ref=r, send_sem=i_send.at[N_NEIGHBOUR_CHIPS * w + j],
                recv_sem=i_recv.at[N_NEIGHBOUR_CHIPS * w + j], device_id=to, device_id_type=MESH)

        started = []
        for w in range(nw):
            started += [ici(w, 0, 0, (*chips[1], c)), ici(w, 1, 1, (*chips[0], c))]
            started += [d2d(w, j, c, sibling) for j in range(N_NEIGHBOUR_CHIPS)]
        for cp in started:
            cp.start()
        diag = N_PEER_CHIPS - 1
        for w in range(nw):
            for j in range(N_NEIGHBOUR_CHIPS):
                ici(w, j, diag, me).wait_recv()
            cp = d2d(w, diag, c, sibling)
            cp.start()
            started.append(cp)
        for w in range(nw):
            for j in range(N_PEER_CHIPS):
                d2d(w, j, 1 - c, me).wait_recv()
        for cp in started:
            cp.wait_send()

    return pl.pallas_call(
        body, name=name,
        in_specs=[ANY] * (nw + 1), out_specs=[ANY] * nw,
        out_shape=[jax.ShapeDtypeStruct(a.shape, a.dtype) for a in bufs],
        input_output_aliases={w: w for w in range(nw)},
        scratch_shapes=[pltpu.SemaphoreType.DMA((N_PEER_CHIPS * nw,)), pltpu.SemaphoreType.DMA((N_PEER_CHIPS * nw,)),
                        pltpu.SemaphoreType.DMA((N_NEIGHBOUR_CHIPS * nw,)),
                        pltpu.SemaphoreType.DMA((N_NEIGHBOUR_CHIPS * nw,))],
    )(*bufs, after)


def _forward_halves(bufs, which, after, *, name):
    nw = len(bufs)
    n = len(which)

    def body(*refs):
        outs = refs[nw + 1:2 * nw + 1]
        send, recv = refs[2 * nw + 1:]
        x, y, c, chips = _place()
        me, sibling = (x, y, c), (x, y, 1 - c)

        def d2d(w, t, half, to):
            cx, cy = chips[which[t]]
            hr = bufs[w].shape[1] // 2
            rows = outs[w].at[2 * cx + cy, pl.ds(half * hr, hr)]
            return pltpu.make_async_remote_copy(src_ref=rows, dst_ref=rows, send_sem=send.at[n * w + t],
                                                recv_sem=recv.at[n * w + t], device_id=to, device_id_type=MESH)

        passed = [d2d(w, t, c, sibling) for w in range(nw) for t in range(n)]
        for cp in passed:
            cp.start()
        for w in range(nw):
            for t in range(n):
                d2d(w, t, 1 - c, me).wait_recv()
        for cp in passed:
            cp.wait_send()

    return pl.pallas_call(
        body, name=name,
        in_specs=[ANY] * (nw + 1), out_specs=[ANY] * nw,
        out_shape=[jax.ShapeDtypeStruct(a.shape, a.dtype) for a in bufs],
        input_output_aliases={w: w for w in range(nw)},
        scratch_shapes=[pltpu.SemaphoreType.DMA((n * nw,)), pltpu.SemaphoreType.DMA((n * nw,))],
    )(*bufs, after)


def _allreduce_small(p, after, *, name):
    R = p.shape[0]
    hr = R // 2

    def body(p_ref, _after_ref, out_ref, sib_ref, sum_ref, gat_ref, tot_ref, send, recv):
        x, y, c, chips = _place()
        s = 2 * x + y
        sibling = (x, y, 1 - c)
        rows = pl.ds(pl.multiple_of(c * hr, 8), hr)
        swap = pltpu.make_async_remote_copy(src_ref=p_ref, dst_ref=sib_ref, send_sem=send.at[0], recv_sem=recv.at[0],
                                            device_id=sibling, device_id_type=MESH)
        swap.start()
        swap.wait()
        sum_ref[...] = p_ref[...] + sib_ref[...]
        gat_ref[s] = sum_ref[rows, :]
        cps = [pltpu.make_async_remote_copy(src_ref=sum_ref.at[rows], dst_ref=gat_ref.at[s], send_sem=send.at[1 + j],
                                            recv_sem=recv.at[1 + j], device_id=(cx, cy, c), device_id_type=MESH)
               for j, (cx, cy) in enumerate(chips)]
        for cp in cps:
            cp.start()
        for cp in cps:
            cp.wait()
        tot_ref[...] = ((gat_ref[0] + gat_ref[1]) + gat_ref[2]) + gat_ref[3]
        out_ref[rows, :] = tot_ref[...]
        share = pltpu.make_async_remote_copy(src_ref=tot_ref, dst_ref=out_ref.at[rows], send_sem=send.at[4],
                                             recv_sem=recv.at[4], device_id=sibling, device_id_type=MESH)
        share.start()
        share.wait_send()
        other = out_ref.at[pl.ds(pl.multiple_of((1 - c) * hr, 8), hr)]
        pltpu.make_async_remote_copy(src_ref=other, dst_ref=other, send_sem=send.at[4], recv_sem=recv.at[4],
                                     device_id=(x, y, c), device_id_type=MESH).wait_recv()

    vmem = pl.BlockSpec(memory_space=pltpu.VMEM)
    return pl.pallas_call(
        body, name=name, in_specs=[vmem, ANY], out_specs=vmem,
        out_shape=jax.ShapeDtypeStruct((R, 128), F32),
        scratch_shapes=[pltpu.VMEM((R, 128), F32), pltpu.VMEM((R, 128), F32), pltpu.VMEM((NCHIP, hr, 128), F32),
                        pltpu.VMEM((hr, 128), F32), pltpu.SemaphoreType.DMA((5,)), pltpu.SemaphoreType.DMA((5,))],
    )(p, after)


def _select_half_bf16(g, half, add, slot, *, name):
    _, R, C = g.shape
    hr = R // 2
    tr = _pick_rows(hr, 16)
    nb = hr // tr
    sel = jnp.concatenate([jnp.reshape(half, (1,)).astype(jnp.int32), slot])

    def body(s_ref, g_ref, a_ref, o_ref, own_ref):
        val = (g_ref[...].astype(F32) + a_ref[...].astype(F32)).astype(BF16)
        o_ref[...] = val

        @pl.when(pl.program_id(1) == s_ref[1])
        def _():
            own_ref[...] = val

    g_spec = pl.BlockSpec((None, tr, C), lambda i, j, s: (j, s[0] * nb + i, 0))
    o_spec = pl.BlockSpec((None, tr, C), lambda i, j, s: (j, i, 0))
    own_spec = pl.BlockSpec((None, tr, C), lambda i, j, s: (s[1], i, 0))
    shape = jax.ShapeDtypeStruct((NCHIP, hr, C), BF16)
    return pl.pallas_call(
        body, name=name,
        grid_spec=pltpu.PrefetchScalarGridSpec(
            num_scalar_prefetch=1, grid=(nb, NCHIP), in_specs=[g_spec, o_spec], out_specs=[o_spec, own_spec]),
        out_shape=[shape, shape],
        compiler_params=_cp(("parallel", "arbitrary"), VMEM_MB),
    )(sel, g, add)


def _adamw_math(w, g, m, v):
    m = ADAM_B1 * m + (1.0 - ADAM_B1) * g
    v = ADAM_B2 * v + (1.0 - ADAM_B2) * (g * g)
    m_hat = m / (1.0 - ADAM_B1 ** ADAM_STEP)
    v_hat = v / (1.0 - ADAM_B2 ** ADAM_STEP)
    delta = -ADAM_LR * (m_hat / (jnp.sqrt(v_hat) + ADAM_EPS) + ADAM_WD * w)
    return delta, m, v


def _adamw(w, g_mine, g_sib, m, v, core, *, name):
    R, C = w.shape
    hr = R // 2
    tr = _pick_rows(hr, 16)
    nb = hr // tr
    row = pl.BlockSpec((tr, C), lambda hh, i, c: (hh * nb + i, 0))
    mine = pl.BlockSpec((NCHIP, tr, C), lambda hh, i, c: (0, jnp.where(hh == c[0], i, 0), 0))
    sibs = pl.BlockSpec((NCHIP, tr, C), lambda hh, i, c: (0, jnp.where(hh == c[0], 0, i), 0))

    def slot_sum(ref):
        acc = ref[0].astype(F32) + ref[1].astype(F32)
        for j in range(2, NCHIP):
            acc = acc + ref[j].astype(F32)
        return acc

    def body(c_ref, w_ref, gm_ref, gs_ref, m_ref, v_ref, go_ref, d_ref, mo_ref, vo_ref):
        gv = jnp.where(pl.program_id(0) == c_ref[0], slot_sum(gm_ref), slot_sum(gs_ref))
        d, mn, vn = _adamw_math(w_ref[...], gv, m_ref[...], v_ref[...])
        go_ref[...] = gv
        d_ref[...] = d
        mo_ref[...] = mn
        vo_ref[...] = vn

    return pl.pallas_call(
        body, name=name,
        grid_spec=pltpu.PrefetchScalarGridSpec(
            num_scalar_prefetch=1, grid=(2, nb),
            in_specs=[row, mine, sibs, row, row], out_specs=[row] * 4),
        out_shape=[jax.ShapeDtypeStruct((R, C), F32)] * 4,
        compiler_params=_cp(("parallel", "parallel"), VMEM_MB),
    )(core, w, g_mine, g_sib, m, v)


def _adamw_small(ws, gs, ms, vs, *, name):
    n = len(ws)

    def body(*refs):
        w_r, g_r, m_r, v_r = refs[:n], refs[n:2 * n], refs[2 * n:3 * n], refs[3 * n:4 * n]
        d_r, mo_r, vo_r = refs[4 * n:5 * n], refs[5 * n:6 * n], refs[6 * n:7 * n]
        for k in range(n):
            d, mn, vn = _adamw_math(w_r[k][...], g_r[k][...], m_r[k][...], v_r[k][...])
            d_r[k][...] = d
            mo_r[k][...] = mn
            vo_r[k][...] = vn

    shapes = [jax.ShapeDtypeStruct(w.shape, F32) for w in ws]
    res = pl.pallas_call(body, name=name, out_shape=shapes * 3)(*ws, *gs, *ms, *vs)
    return res[:n], res[n:2 * n], res[2 * n:]


_PACK_ROWS = 8


def _pack(parts):
    rows = []
    for a in parts:
        flat = a.reshape(-1)
        n = -(-flat.shape[0] // (_PACK_ROWS * 128)) * (_PACK_ROWS * 128)
        rows.append(jnp.pad(flat, (0, n - flat.shape[0])).reshape(-1, 128))
    total = sum(r.shape[0] for r in rows)
    if total % 16:
        rows.append(jnp.zeros((16 - total % 16, 128), F32))
    return jnp.concatenate(rows, axis=0)


def _unpack(p, shapes):
    out, r = [], 0
    for shp in shapes:
        n = math.prod(shp)
        nr = -(-n // (_PACK_ROWS * 128)) * _PACK_ROWS
        out.append(p[r:r + nr].reshape(-1)[:n].reshape(shp))
        r += nr
    return out


def kernel(x, mem, g_mix, w_in, ln_v_g, ln_v_b, w_s, b_s, conv_w, g_mem, w_kv, g_head, w_o, g_ffn, w_ffn1, w_ffn2, g_final, loss_target, m_g_mix, m_w_in, m_ln_v_g, m_ln_v_b, m_w_s, m_b_s, m_conv_w, m_g_mem, m_w_kv, m_g_head, m_w_o, m_g_ffn, m_w_ffn1, m_w_ffn2, m_g_final, v_g_mix, v_w_in, v_ln_v_g, v_ln_v_b, v_w_s, v_b_s, v_conv_w, v_g_mem, v_w_kv, v_g_head, v_w_o, v_g_ffn, v_w_ffn1, v_w_ffn2, v_g_final):
    sds = jax.ShapeDtypeStruct
    xi, yi = lax.axis_index("x"), lax.axis_index("y")
    shard = 2 * xi + yi
    x2d, mem2d, tgt = x[0], mem[0], loss_target[0]
    ws3, bs2 = w_s[0], b_s[0]
    g_final2 = g_final.reshape(1, D)
    dff4 = DFF // NCHIP
    din4 = DIN // NCHIP
    dcv4 = DC // NCHIP

    big = [w_in[0].T, w_kv[0], w_o[0], w_ffn1[0], w_ffn2[0]]
    big_names = ["w_in", "w_kv", "w_o", "w_ffn1", "w_ffn2"]
    slot = jnp.reshape(shard, (1,)).astype(jnp.int32)
    core = jnp.reshape(lax.axis_index("c"), (1,)).astype(jnp.int32)
    conv_pad = jnp.pad(conv_w[0], ((0, CONV_PAD[0] - 3), (0, CONV_PAD[1] - dcv4)))
    conv_slots = lax.dynamic_update_slice(jnp.zeros((NCHIP,) + CONV_PAD, F32), conv_pad[None], (shard, 0, 0))

    gather_ids = {"in": 16, "kvo": 17, "ffn1": 18, "ffn2": 19, "ffn2d": 20}

    def gather_start(bufs, after, nm, forwards=()):
        return _allgather_start(bufs, forwards, after, gather_ids[nm], name="ag_start_" + nm)

    def gather_wait(state, idx, after, nm):
        send, recv, bufs, _ = state
        got, token = _transfer_wait([send[k] for k in idx], [recv[k] for k in idx], [[bufs[k]] for k in idx],
                                    [(N_NEIGHBOUR_CHIPS, bufs[k].shape[1] // 2) for k in idx], after, name="ag_wait_" + nm)
        return [g[0] for g in got], token

    cast = lambda k, after: _cast_into_slot(big[k], slot, after, name="cast_" + big_names[k])
    ag_in = gather_start([cast(0, slot), conv_slots], slot, "in")
    bs_t = bs2.T

    h = _rms_fwd(x2d, g_mix, name="rms_mix", after=[ag_in[3]])
    mem_n = _rms_fwd(mem2d, g_mem, name="rms_mem", after=[h])
    kvo_b = [cast(1, mem_n)]
    kvo_b.append(cast(2, kvo_b[0]))
    w1_b = cast(3, kvo_b[1])
    w2_b = cast(4, w1_b)
    got_in, tok = gather_wait(ag_in, [0, 1], w2_b, "in")
    win4, conv4 = _forward_gathered(got_in, tok, name="ag_fwd_in")
    ag_kvo = gather_start(kvo_b, conv4, "kvo")
    w_in_t = win4.reshape(DIN, D)
    conv_full = conv4[:, :3, :dcv4].transpose(1, 0, 2).reshape(3, DC)
    NEAR, FAR = [0, 1], [2]

    def diagonal_wait(state, ks, after, nm):
        send, recv, bufs, _ = state
        got, token = _transfer_wait([send[k] for k in ks], [recv[k] for k in ks], [[bufs[k]] for k in ks],
                                    [(1, bufs[k].shape[1] // 2) for k in ks], after, name="ag_waitd_" + nm)
        return [g[0] for g in got], token

    (proj,) = _matmul(h, w_in_t, name="mm_proj", tb=True, M=S, N=DIN, K=D, tn=DIN // 2, outs=[sds((S, DIN), F32)],
                      after=[ag_kvo[3]])
    got_kvo, tok = gather_wait(ag_kvo, [0, 1], proj, "kvo")
    ag_w1 = gather_start([w1_b], tok, "ffn1", forwards=got_kvo)
    kvo_n = _forward_halves(ag_w1[2][1:], NEAR, ag_w1[3], name="ag_fwdn_kvo")
    ag_w1 = (ag_w1[0], ag_w1[1], [ag_w1[2][0]] + list(kvo_n), ag_w1[3])
    kvo_d, tok = diagonal_wait(ag_w1, [1, 2], ag_w1[3], "kvo")
    wkv4, wo4 = _forward_halves(kvo_d, FAR, tok, name="ag_fwdd_kvo")
    w_kv_full = wkv4.reshape(D, 2 * DM)
    w_o_full = wo4.reshape(D, D)
    (kv,) = _matmul(mem_n, w_kv_full, name="mm_kv", M=NMEM, N=2 * DM, K=D, outs=[sds((NMEM, 2 * DM), F32)])
    heads, hn, ycv = _mix_fwd(proj, kv, ws3, bs_t, ln_v_g, ln_v_b, conv_full, g_head, name="mix_fwd")
    (x2,) = _matmul(hn, w_o_full, name="mm_wo", M=S, N=D, K=D, outs=[sds((S, D), F32)],
                    epi=lambda acc, res: (acc + res,), extras=[(x2d, _tile_spec())])
    h2 = _rms_fwd(x2, g_ffn, name="rms_ffn")
    near = jnp.stack([shard, 2 * (1 - xi) + yi, 2 * xi + (1 - yi)]).astype(jnp.int32)
    far = jnp.reshape(2 * (1 - xi) + (1 - yi), (1,)).astype(jnp.int32)

    w1_shard = lambda tn, tk: pl.BlockSpec((None, tk, tn), lambda j, i, k, s: (s[j], k, 0))
    act_cols = lambda tm, tn: [pl.BlockSpec((tm, tn), lambda j, i, k, s: (i, s[j]))]
    relu2 = lambda acc: (jnp.square(jnp.maximum(acc, 0.0)),)

    got_w1, tok = gather_wait(ag_w1, [0], h2, "ffn1")
    ag_w2 = gather_start([w2_b], tok, "ffn2", forwards=got_w1)
    (w1n,) = _forward_halves([ag_w2[2][1]], NEAR, ag_w2[3], name="ag_fwdn_ffn1")
    ag_w2 = (ag_w2[0], ag_w2[1], [ag_w2[2][0], w1n], ag_w2[3])
    (act,) = _matmul(h2, w1n, name="mm_ffn1_near", M=S, N=3 * dff4, K=D, tn=dff4, b_spec=w1_shard,
                     out_specs=act_cols, outs=[sds((S, DFF), BF16)], epi=relu2, slots=near)
    w1d, tok = diagonal_wait(ag_w2, [1], act, "ffn1")
    (w14,) = _forward_halves(w1d, FAR, tok, name="ag_fwdd_ffn1")
    (act,) = _matmul(h2, w14, name="mm_ffn1_far", M=S, N=dff4, K=D, tn=dff4, b_spec=w1_shard, out_specs=act_cols,
                     outs=[sds((S, DFF), BF16)], epi=relu2, slots=far, into=act)

    act_shard = lambda tm, tk: pl.BlockSpec((tm, tk), lambda j, i, k, s: (i, s[k]))
    w2_shard = lambda tn, tk: pl.BlockSpec((None, tk, tn), lambda j, i, k, s: (s[k], 0, j))
    got_w2, tok = gather_wait(ag_w2, [0], act, "ffn2")
    ag_w2d = gather_start([], tok, "ffn2d", forwards=got_w2)
    (w2n,) = _forward_halves(ag_w2d[2], NEAR, ag_w2d[3], name="ag_fwdn_ffn2")
    ag_w2d = (ag_w2d[0], ag_w2d[1], [w2n], ag_w2d[3])
    (x3,) = _matmul(act, w2n, name="mm_ffn2_near", M=S, N=D, K=3 * dff4, tm=2 * TM, tk=dff4,
                    a_spec=act_shard, b_spec=w2_shard, outs=[sds((S, D), F32)], epi=lambda acc, res: (acc + res,),
                    extras=[(x2, _tile_spec())], slots=near)
    w2d, tok = diagonal_wait(ag_w2d, [0], x3, "ffn2")
    (w24,) = _forward_halves(w2d, FAR, tok, name="ag_fwdd_ffn2")
    (x3,) = _matmul(act, w24, name="mm_ffn2_far", M=S, N=D, K=dff4, tm=2 * TM, tk=dff4, a_spec=act_shard,
                    b_spec=w2_shard, outs=[sds((S, D), F32)], epi=lambda acc, res: (acc + res,),
                    extras=[(x3, _tile_spec())], slots=far)
    w2_full = w24.reshape(DFF, D)

    ci = lax.axis_index("c")

    def rs_sibling(g4, nm):
        return _sibling_start([g4], False, 1 + big_names.index(nm), name="rs_sib_" + nm)

    def rs_chips(state, after, nm):
        send, recv, g4, land, _ = state
        (((land_, g4_),), _) = _transfer_wait(send, recv, [[land[0], g4[0]]], [(NCHIP, land[0].shape[1])], after,
                                             name="rs_sibwait_" + nm)
        part, buf = _select_half_bf16(g4_, ci, land_, slot, name="rs_add_" + nm)
        return _scatter_start([part], [buf], 1 + 2 * len(big_names) + big_names.index(nm), name="rs_start_" + nm)

    def rs_end(state, after, nm):
        send, recv, parts, bufs, _ = state
        (((buf, _),), _) = _transfer_wait(send, recv, [[bufs[0], parts[0]]], [(N_PEER_CHIPS, bufs[0].shape[1])], after,
                                          name="rs_wait_" + nm)
        return _sibling_start([buf], True, 1 + len(big_names) + big_names.index(nm), name="rs_share_" + nm)

    big_m = [m_w_in[0].T, m_w_kv[0], m_w_o[0], m_w_ffn1[0], m_w_ffn2[0]]
    big_v = [v_w_in[0].T, v_w_kv[0], v_w_o[0], v_w_ffn1[0], v_w_ffn2[0]]
    big_out = {}

    def rs_finish(k, state, after):
        send, recv, mine, land, _ = state
        nm = big_names[k]
        (((land_, mine_),), _) = _transfer_wait(send, recv, [[land[0], mine[0]]], [(NCHIP, land[0].shape[1])], after,
                                               name="rs_sharewait_" + nm)
        big_out[nm] = _adamw(big[k], mine_, land_, big_m[k], big_v[k], core, name="adamw_" + nm)
        return big_out[nm][1]

    dx3, dx3b, dg_final, loss11 = _loss_bwd(x3, g_final2, tgt, name="loss_bwd")
    (dw2,) = _matmul(act, dx3b, name="mm_dw2", ta=True, M=DFF, N=D, K=S, tn=D, outs=[sds((DFF, D), BF16)])
    sib_w2 = rs_sibling(dw2.reshape(NCHIP, dff4, D), "w_ffn2")
    (dfb,) = _matmul(dx3b, w2_full, name="mm_dact", tb=True, M=S, N=DFF, K=D, tn=dff4, outs=[sds((S, DFF), BF16)],
                     epi=lambda acc, a: (acc * (2.0 * jnp.sqrt(a.astype(F32))),), extras=[(act, _tile_spec())],
                     after=[sib_w2[4]])
    rs_w2 = rs_chips(sib_w2, dfb, "w_ffn2")

    def dw1_out(tm, tn):
        nb = dff4 // tn
        return [pl.BlockSpec((None, tm, tn), lambda j, i, k: (j // nb, i, j % nb))]

    (dw1,) = _matmul(h2, dfb, name="mm_dw1", ta=True, M=D, N=DFF, K=S, tn=dff4, outs=[sds((NCHIP, D, dff4), BF16)],
                     out_specs=dw1_out, after=[rs_w2[4]])
    sib_w1 = rs_sibling(dw1, "w_ffn1")

    def w1_rows(tn, tk):
        kb = dff4 // tk
        return pl.BlockSpec((None, tn, tk), lambda j, i, k: (k // kb, j, k % kb))

    (dh2,) = _matmul(dfb, w14, name="mm_dh2", tb=True, M=S, N=D, K=DFF, tm=2 * TM, b_spec=w1_rows,
                     outs=[sds((S, D), F32)], after=[sib_w1[4]])
    rs_w1 = rs_chips(sib_w1, dh2, "w_ffn1")
    dx2, dx2b, dg_ffn = _rms_bwd(dh2, x2, g_ffn, dx3, name="rms_ffn_bwd", after=[rs_w1[4]])
    (dwo,) = _matmul(hn, dx2b, name="mm_dwo", ta=True, M=D, N=D, K=S, outs=[sds((D, D), BF16)])
    sib_wo = rs_sibling(dwo.reshape(NCHIP, D // NCHIP, D), "w_o")
    (dhn,) = _matmul(dx2b, w_o_full, name="mm_dhn", tb=True, M=S, N=D, K=D, outs=[sds((S, D), F32)],
                     after=[sib_wo[4]])
    rs_wo = rs_chips(sib_wo, dhn, "w_o")
    sh_w2 = rs_end(rs_w2, rs_wo[4], "w_ffn2")
    dproj, dkv, dws, dbs8, dlng, dlnb, dcw8, dgh = _mix_bwd(
        dhn, heads, proj, ycv, kv, ws3, bs_t, ln_v_g, ln_v_b, conv_full, g_head, sh_w2[4], name="mix_bwd")
    (dwin_t,) = _matmul(dproj, h, name="mm_dwin", ta=True, M=DIN, N=D, K=S, tm=DIN // 2, outs=[sds((DIN, D), BF16)])
    sib_win = rs_sibling(dwin_t.reshape(NCHIP, din4, D), "w_in")
    (dwkv,) = _matmul(mem_n, dkv, name="mm_dwkv", ta=True, M=D, N=2 * DM, K=NMEM, outs=[sds((D, 2 * DM), BF16)],
                      after=[sib_win[4]])
    sib_wkv = rs_sibling(dwkv.reshape(NCHIP, D // NCHIP, 2 * DM), "w_kv")
    (dh,) = _matmul(dproj, w_in_t, name="mm_dh", M=S, N=D, K=DIN, tk=DIN, outs=[sds((S, D), F32)],
                    after=[sib_wkv[4]])
    rs_win = rs_chips(sib_win, dh, "w_in")
    rs_wkv = rs_chips(sib_wkv, rs_win[4], "w_kv")
    dx, dg_mix = _rms_bwd(dh, x2d, g_mix, dx2, name="rms_mix_bwd", want_bf=False, after=[rs_wkv[4]])
    sh_w1 = rs_end(rs_w1, dx, "w_ffn1")
    (dmem_n,) = _matmul(dkv, w_kv_full, name="mm_dmem", tb=True, M=NMEM, N=D, K=2 * DM, outs=[sds((NMEM, D), F32)],
                        after=[sh_w1[4]])
    (dg_mem,) = _rms_bwd(dmem_n, mem2d, g_mem, None, name="rms_mem_bwd", want_dx=False)
    sh_wo = rs_end(rs_wo, dg_mem, "w_o")
    done = rs_finish(4, sh_w2, sh_wo[4])
    done = rs_finish(3, sh_w1, done)
    sh_win = rs_end(rs_win, done, "w_in")
    sh_wkv = rs_end(rs_wkv, sh_win[4], "w_kv")
    done = rs_finish(2, sh_wo, sh_wkv[4])
    done = rs_finish(0, sh_win, done)
    done = rs_finish(1, sh_wkv, done)

    small_names = ["g_mix", "ln_v_g", "ln_v_b", "w_s", "b_s", "conv_w", "g_mem", "g_head", "g_ffn", "g_final"]
    small_part = [dg_mix, dlng, dlnb, dws, dbs8[:, 0, :], dcw8[:3], dg_mem, dgh, dg_ffn, dg_final, loss11]
    small_shapes = [(1, D), (1, DS), (1, DS), (NSH, CHUNK, CHUNK), (NSH, CHUNK), (3, DC), (1, D), (1, D), (1, D), (1, D),
                    (1, 1)]
    total = _allreduce_small(_pack(small_part), done, name="allreduce_small")
    small_g = _unpack(total, small_shapes)
    loss = small_g.pop()[0, 0]
    small_g[5] = lax.dynamic_slice(small_g[5], (0, shard * dcv4), (3, dcv4))
    small_w = [g_mix, ln_v_g, ln_v_b, ws3, bs2, conv_w[0], g_mem, g_head, g_ffn, g_final2]
    small_m = [m_g_mix, m_ln_v_g, m_ln_v_b, m_w_s[0], m_b_s[0], m_conv_w[0], m_g_mem, m_g_head, m_g_ffn,
               m_g_final.reshape(1, D)]
    small_v = [v_g_mix, v_ln_v_g, v_ln_v_b, v_w_s[0], v_b_s[0], v_conv_w[0], v_g_mem, v_g_head, v_g_ffn,
               v_g_final.reshape(1, D)]
    s_delta, s_m, s_v = _adamw_small(small_w, small_g, small_m, small_v, name="adamw_small")
    small_out = {nm: (g, d, mn, vn) for nm, g, d, mn, vn in zip(small_names, small_g, s_delta, s_m, s_v)}

    order = ["g_mix", "w_in", "ln_v_g", "ln_v_b", "w_s", "b_s", "conv_w", "g_mem", "w_kv", "g_head", "w_o",
             "g_ffn", "w_ffn1", "w_ffn2", "g_final"]
    like = dict(g_mix=g_mix, w_in=w_in, ln_v_g=ln_v_g, ln_v_b=ln_v_b, w_s=w_s, b_s=b_s, conv_w=conv_w, g_mem=g_mem,
                w_kv=w_kv, g_head=g_head, w_o=w_o, g_ffn=g_ffn, w_ffn1=w_ffn1, w_ffn2=w_ffn2, g_final=g_final)
    res = {**big_out, **small_out}
    res["w_in"] = [a.T for a in res["w_in"]]
    outs = [loss, dx[None]]
    for k in range(4):
        outs += [res[nm][k].reshape(like[nm].shape) for nm in order]
    return tuple(outs)
```

```python
import math

import jax
import jax.numpy as jnp
from jax import lax
from jax.experimental import pallas as pl
from jax.experimental.pallas import tpu as pltpu

F32 = jnp.float32
BF16 = jnp.bfloat16
MESH = pl.DeviceIdType.MESH

D = 2048
S = 2048
HD = 128
NH = D // HD
NMH = 4
NSH = (NH - NMH) // 2
NCH = NH - NMH - NSH
DS = NSH * HD
DC = NCH * HD
DM = NMH * HD
DIN = 2 * DS + 3 * DC + DM
CHUNK = 128
NMEM = 256
DFF = 4 * D
EPS = 1e-6
NCHIP = 4
SCALE = HD ** -0.5

ADAM_LR = 0.001
ADAM_B1 = 0.9
ADAM_B2 = 0.999
ADAM_EPS = 1e-08
ADAM_WD = 0.01
ADAM_STEP = 10

TR_EW = 256
TR_MIX = 256
TM = 512
TN = 1024
TK = 2048
N_SUB = 512
VMEM_MB = 56
HALO = 8


def _pick(n, target, q=128):
    best = None
    for t in range(q, min(n, target) + 1, q):
        if n % t == 0:
            best = t
    return n if best is None else best


def _pick_rows(n, q):
    below = _pick(n, TR_EW, q)
    if 2 * below >= TR_EW:
        return below
    above = [t for t in range(TR_EW, min(n, 4 * TR_EW) + 1, q) if n % t == 0]
    return above[0] if above else below


def _cp(sem=None, vmem_mb=None, **kw):
    d = dict(kw)
    if sem is not None:
        d["dimension_semantics"] = sem
    if vmem_mb is not None:
        d["vmem_limit_bytes"] = vmem_mb << 20
    return pltpu.CompilerParams(**d)


def _gelu(x):
    z = 0.7978845608028654 * (x + 0.044715 * (x * x * x))
    return 0.5 * x * (1.0 + jnp.tanh(z))


def _gelu_with_grad(x):
    x2 = x * x
    t = jnp.tanh(0.7978845608028654 * (x + 0.044715 * (x2 * x)))
    half = 0.5 * (1.0 + t)
    return x * half, half + 0.5 * x * (1.0 - t * t) * (0.7978845608028654 * (1.0 + 3.0 * 0.044715 * x2))


def _matmul(a, b, *, name, ta=False, tb=False, M, N, K, tm=None, tn=None, tk=None, outs, epi=None,
            extras=(), a_spec=None, b_spec=None, out_specs=None, after=(), n_split=None, slots=None, into=()):
    n_after = len(after)
    tm = _pick(M, TM if tm is None else tm, 8)
    tn = _pick(N, TN if tn is None else tn)
    tk = _pick(K, TK if tk is None else tk)
    if n_split is None:
        n_split = tn // N_SUB if tn % N_SUB == 0 else 1
    nk = K // tk
    grid = (N // tn, M // tm, nk)
    if a_spec is None:
        a_spec = (pl.BlockSpec((tk, tm), lambda j, i, k, *s: (k, i)) if ta
                  else pl.BlockSpec((tm, tk), lambda j, i, k, *s: (i, k)))
    else:
        a_spec = a_spec(tm, tk)
    if b_spec is None:
        b_spec = (pl.BlockSpec((tn, tk), lambda j, i, k, *s: (j, k)) if tb
                  else pl.BlockSpec((tk, tn), lambda j, i, k, *s: (k, j)))
    else:
        b_spec = b_spec(tn, tk)
    if out_specs is None:
        out_specs = [pl.BlockSpec((tm, tn), lambda j, i, k, *s: (i, j)) for _ in outs]
    else:
        out_specs = out_specs(tm, tn)
    dn = (((0 if ta else 1,), (1 if tb else 0,)), ((), ()))
    n_ex, n_out = len(extras), len(outs)
    n_pre = 0 if slots is None else 1
    n_into = len(into)
    ns = tn // n_split

    def body(*refs):
        a_ref, b_ref = refs[n_pre], refs[n_pre + 1]
        ex = refs[n_pre + 2:n_pre + 2 + n_ex]
        first_out = n_pre + 2 + n_ex + n_after + n_into
        o = refs[first_out:first_out + n_out]
        acc = refs[first_out + n_out:]
        k = pl.program_id(2)

        def finish(val, cols):
            res = (val,) if epi is None else epi(val, *[e[:, cols] for e in ex])
            for r, o_ref in zip(res, o):
                o_ref[:, cols] = r.astype(o_ref.dtype)

        if nk > 1:
            @pl.when(k == 0)
            def _():
                acc[0][...] = jnp.zeros_like(acc[0])

        av = a_ref[...].astype(BF16)
        for q in range(n_split):
            cols = slice(q * ns, (q + 1) * ns)
            bq = (b_ref[cols, :] if tb else b_ref[:, cols]).astype(BF16)
            part = lax.dot_general(av, bq, dn, preferred_element_type=F32)
            if nk == 1:
                finish(part, cols)
            else:
                acc[0][:, cols] += part

        if nk > 1:
            @pl.when(k == nk - 1)
            def _():
                finish(acc[0][...], slice(0, tn))

    in_specs = ([a_spec, b_spec] + [sp(tm, tn) for _, sp in extras] + [ANY] * (n_after + n_into))
    scratch = [pltpu.VMEM((tm, tn), F32)] if nk > 1 else []
    args = [a, b] + [arr for arr, _ in extras] + list(after) + list(into)
    aliases = {n_pre + len(args) - n_into + t: t for t in range(n_into)}
    params = _cp(("parallel", "parallel", "arbitrary"), VMEM_MB)
    if slots is None:
        return pl.pallas_call(body, name=name, grid=grid, in_specs=in_specs, out_specs=out_specs, out_shape=outs,
                              scratch_shapes=scratch, input_output_aliases=aliases, compiler_params=params)(*args)
    return pl.pallas_call(
        body, name=name,
        grid_spec=pltpu.PrefetchScalarGridSpec(num_scalar_prefetch=1, grid=grid, in_specs=in_specs,
                                               out_specs=out_specs, scratch_shapes=scratch),
        out_shape=outs, input_output_aliases=aliases, compiler_params=params)(slots, *args)


def _tile_spec():
    return lambda tm, tn: pl.BlockSpec((tm, tn), lambda j, i, k, *s: (i, j))


def _cast_into_slot(w, slot, after, *, name):
    R, C = w.shape
    tr = _pick_rows(R, 16)

    def body(s_ref, w_ref, _after_ref, o_ref):
        o_ref[...] = w_ref[...].astype(BF16)

    return pl.pallas_call(
        body, name=name,
        grid_spec=pltpu.PrefetchScalarGridSpec(
            num_scalar_prefetch=1, grid=(R // tr,),
            in_specs=[pl.BlockSpec((tr, C), lambda i, s: (i, 0)), ANY],
            out_specs=pl.BlockSpec((None, tr, C), lambda i, s: (s[0], i, 0))),
        out_shape=jax.ShapeDtypeStruct((NCHIP, R, C), BF16),
        compiler_params=_cp(("parallel",), VMEM_MB),
    )(slot, w, after)


def _rms_fwd(x, g, *, name, after=()):
    R, C = x.shape
    tr = _pick(R, TR_EW, 16)
    n_after = len(after)

    def body(x_ref, g_ref, *rest):
        o_ref = rest[n_after]
        xv = x_ref[...]
        r = lax.rsqrt(jnp.mean(xv * xv, axis=-1, keepdims=True) + EPS)
        o_ref[...] = ((xv * r) * g_ref[...]).astype(BF16)

    return pl.pallas_call(
        body, name=name, grid=(R // tr,),
        in_specs=[pl.BlockSpec((tr, C), lambda i: (i, 0)), pl.BlockSpec((1, C), lambda i: (0, 0))] + [ANY] * n_after,
        out_specs=pl.BlockSpec((tr, C), lambda i: (i, 0)),
        out_shape=jax.ShapeDtypeStruct((R, C), BF16),
        compiler_params=_cp(("parallel",), VMEM_MB),
    )(x, g, *after)


def _rms_bwd(dh, x, g, dres, *, name, want_dx=True, want_bf=True, after=()):
    R, C = x.shape
    tr = _pick(R, TR_EW, 16)
    has_res = dres is not None
    row = pl.BlockSpec((tr, C), lambda i: (i, 0))
    vec = pl.BlockSpec((1, C), lambda i: (0, 0))

    def body(*refs):
        dh_ref, x_ref, g_ref = refs[:3]
        pos = 3
        dres_ref = None
        if has_res:
            dres_ref = refs[pos]
            pos += 1
        outs = refs[pos + len(after):]
        i = pl.program_id(0)
        xv = x_ref[...]
        r = lax.rsqrt(jnp.mean(xv * xv, axis=-1, keepdims=True) + EPS)
        xh = xv * r
        dhv = dh_ref[...]
        dg_ref = outs[-1]
        dgp = jnp.sum(dhv * xh, axis=0, keepdims=True)

        @pl.when(i == 0)
        def _():
            dg_ref[...] = dgp

        @pl.when(i > 0)
        def _():
            dg_ref[...] += dgp

        if want_dx:
            t = dhv * g_ref[...]
            dx = r * (t - xh * jnp.mean(t * xh, axis=-1, keepdims=True))
            if has_res:
                dx = dx + dres_ref[...]
            outs[0][...] = dx
            if want_bf:
                outs[1][...] = dx.astype(BF16)

    in_specs = [row, row, vec] + ([row] if has_res else []) + [ANY] * len(after)
    out_specs, out_shape = [], []
    if want_dx:
        out_specs.append(row)
        out_shape.append(jax.ShapeDtypeStruct((R, C), F32))
        if want_bf:
            out_specs.append(row)
            out_shape.append(jax.ShapeDtypeStruct((R, C), BF16))
    out_specs.append(vec)
    out_shape.append(jax.ShapeDtypeStruct((1, C), F32))
    args = [dh, x, g] + ([dres] if has_res else []) + list(after)
    return pl.pallas_call(
        body, name=name, grid=(R // tr,), in_specs=in_specs, out_specs=out_specs, out_shape=out_shape,
        compiler_params=_cp(("arbitrary",), VMEM_MB),
    )(*args)


def _loss_bwd(x3, g, tgt, *, name):
    R, C = x3.shape
    tr = _pick(R, TR_EW, 16)
    n = R // tr
    row = pl.BlockSpec((tr, C), lambda i: (i, 0))
    vec = pl.BlockSpec((1, C), lambda i: (0, 0))

    def body(x_ref, g_ref, t_ref, dx_ref, dxb_ref, dg_ref, loss_ref, acc_ref):
        i = pl.program_id(0)
        xv = x_ref[...]
        gv = g_ref[...]
        r = lax.rsqrt(jnp.mean(xv * xv, axis=-1, keepdims=True) + EPS)
        xh = xv * r
        e = xh * gv - t_ref[...]
        dy = e * (1.0 / C)
        sq = jnp.sum(e * e, axis=0, keepdims=True)
        dgp = jnp.sum(dy * xh, axis=0, keepdims=True)

        @pl.when(i == 0)
        def _():
            acc_ref[...] = sq
            dg_ref[...] = dgp

        @pl.when(i > 0)
        def _():
            acc_ref[...] += sq
            dg_ref[...] += dgp

        t = dy * gv
        dx = r * (t - xh * jnp.mean(t * xh, axis=-1, keepdims=True))
        dx_ref[...] = dx
        dxb_ref[...] = dx.astype(BF16)

        @pl.when(i == n - 1)
        def _():
            loss_ref[...] = jnp.sum(acc_ref[...], axis=-1, keepdims=True) * (0.5 / C)

    return pl.pallas_call(
        body, name=name, grid=(n,),
        in_specs=[row, vec, row],
        out_specs=[row, row, vec, pl.BlockSpec((1, 1), lambda i: (0, 0))],
        out_shape=[jax.ShapeDtypeStruct((R, C), F32), jax.ShapeDtypeStruct((R, C), BF16),
                   jax.ShapeDtypeStruct((1, C), F32), jax.ShapeDtypeStruct((1, 1), F32)],
        scratch_shapes=[pltpu.VMEM((1, C), F32)],
        compiler_params=_cp(("arbitrary",), VMEM_MB),
    )(x3, g, tgt)


def _offsets():
    u0 = 0
    v0 = DS
    b0 = 2 * DS
    c0 = b0 + DC
    x0 = c0 + DC
    q0 = x0 + DC
    return u0, v0, b0, c0, x0, q0


def _tri_mask(lower):
    r = lax.broadcasted_iota(jnp.int32, (CHUNK, CHUNK), 0)
    c = lax.broadcasted_iota(jnp.int32, (CHUNK, CHUNK), 1)
    return (r >= c) if lower else (c >= r)


def _layer_norm_stats(vg):
    mu = jnp.mean(vg, axis=-1, keepdims=True)
    vc = vg - mu
    rstd = lax.rsqrt(jnp.mean(vc * vc, axis=-1, keepdims=True) + EPS)
    return vc * rstd, rstd


def _softmax_rows(qh, kh):
    s = lax.dot_general(qh, kh, (((1,), (1,)), ((), ())), preferred_element_type=F32)
    m = jnp.max(s, axis=-1, keepdims=True)
    e = jnp.exp(s - m)
    return e / jnp.sum(e, axis=-1, keepdims=True)


def _mix_fwd(proj, kv, w_s, bs_t, ln_g, ln_b, conv_w, g_head, *, name):
    assert DS == DC
    tr = _pick(S, TR_MIX, CHUNK)
    n = S // tr
    nck = tr // CHUNK
    u0, v0, b0, c0, x0, q0 = _offsets()
    hb = tr // HALO

    def body(p_ref, cprev_ref, xprev_ref, kv_ref, ws_ref, bst_ref, lng_ref, lnb_ref, cw_ref, gh_ref,
             heads_ref, hn_ref, ycv_ref, buf_ref):
        i = pl.program_id(0)

        def emit(col, val):
            rs = lax.rsqrt(jnp.mean(val * val, axis=-1, keepdims=True) + EPS)
            heads_ref[:, col:col + HD] = val
            hn_ref[:, col:col + HD] = ((val * rs) * gh_ref[:, col:col + HD]).astype(BF16)

        vhat, _ = _layer_norm_stats(_gelu(p_ref[:, v0:v0 + DS]))
        vnb = (vhat * lng_ref[...] + lnb_ref[...]).astype(BF16)
        low = _tri_mask(True)
        for h in range(NSH):
            wt = jnp.where(low, ws_ref[h], 0.0).astype(BF16)
            bcol = bst_ref[:, h:h + 1]
            parts = []
            for c in range(nck):
                blk = vnb[c * CHUNK:(c + 1) * CHUNK, h * HD:(h + 1) * HD]
                parts.append(jnp.dot(wt, blk, preferred_element_type=F32) + bcol)
            mixed = parts[0] if nck == 1 else jnp.concatenate(parts, axis=0)
            emit(h * HD, _gelu(p_ref[:, u0 + h * HD:u0 + (h + 1) * HD]) * mixed)

        xc = p_ref[:, c0:c0 + DC] * p_ref[:, x0:x0 + DC]
        prev = cprev_ref[...] * xprev_ref[...]
        buf_ref[0:HALO, :] = jnp.where(i > 0, prev, 0.0)
        buf_ref[HALO:HALO + tr, :] = xc
        y = (cw_ref[2:3, :] * xc + cw_ref[1:2, :] * buf_ref[HALO - 1:HALO - 1 + tr, :]
             + cw_ref[0:1, :] * buf_ref[HALO - 2:HALO - 2 + tr, :])
        ycv_ref[...] = y
        cout = p_ref[:, b0:b0 + DC] * y
        for h in range(NCH):
            emit(DS + h * HD, cout[:, h * HD:(h + 1) * HD])

        for h in range(NMH):
            qh = (p_ref[:, q0 + h * HD:q0 + (h + 1) * HD] * SCALE).astype(BF16)
            kh = kv_ref[:, h * HD:(h + 1) * HD].astype(BF16)
            vh = kv_ref[:, DM + h * HD:DM + (h + 1) * HD].astype(BF16)
            p = _softmax_rows(qh, kh)
            emit(DS + DC + h * HD, jnp.dot(p.astype(BF16), vh, preferred_element_type=F32))

    full = lambda shape: pl.BlockSpec(shape, lambda i: (0,) * len(shape))
    halo_c = pl.BlockSpec((HALO, DC), lambda i: (jnp.maximum(i * hb - 1, 0), c0 // DC))
    halo_x = pl.BlockSpec((HALO, DC), lambda i: (jnp.maximum(i * hb - 1, 0), x0 // DC))
    return pl.pallas_call(
        body, name=name, grid=(n,),
        in_specs=[pl.BlockSpec((tr, DIN), lambda i: (i, 0)), halo_c, halo_x,
                  full((NMEM, 2 * DM)), full((NSH, CHUNK, CHUNK)), full((CHUNK, NSH)),
                  full((1, DS)), full((1, DS)), full((3, DC)), full((1, D))],
        out_specs=[pl.BlockSpec((tr, D), lambda i: (i, 0)), pl.BlockSpec((tr, D), lambda i: (i, 0)),
                   pl.BlockSpec((tr, DC), lambda i: (i, 0))],
        out_shape=[jax.ShapeDtypeStruct((S, D), F32), jax.ShapeDtypeStruct((S, D), BF16),
                   jax.ShapeDtypeStruct((S, DC), F32)],
        scratch_shapes=[pltpu.VMEM((tr + HALO, DC), F32)],
        compiler_params=_cp(("parallel",), VMEM_MB),
    )(proj, proj, proj, kv, w_s, bs_t, ln_g, ln_b, conv_w, g_head)


def _mix_bwd(dhn, heads, proj, ycv, kv, w_s, bs_t, ln_g, ln_b, conv_w, g_head, after, *, name):
    assert DS == DC
    tr = _pick(S, TR_MIX, CHUNK)
    n = S // tr
    nck = tr // CHUNK
    u0, v0, b0, c0, x0, q0 = _offsets()
    hb = tr // HALO
    last_hb = S // HALO - 1

    def body(dhn_ref, heads_ref, p_ref, ycv_ref, dhn_nx_ref, heads_nx_ref, b_nx_ref, kv_ref, ws_ref, bst_ref,
             lng_ref, lnb_ref, cw_ref, gh_ref, _after_ref,
             dp_ref, dkv_ref, dws_ref, dbs_ref, dlng_ref, dlnb_ref, dcw_ref, dgh_ref, buf_ref, dvn_ref):
        i = pl.program_id(0)

        @pl.when(i == 0)
        def _():
            dkv_ref[...] = jnp.zeros_like(dkv_ref)
            dws_ref[...] = jnp.zeros_like(dws_ref)
            dbs_ref[...] = jnp.zeros_like(dbs_ref)
            dlng_ref[...] = jnp.zeros_like(dlng_ref)
            dlnb_ref[...] = jnp.zeros_like(dlnb_ref)
            dcw_ref[...] = jnp.zeros_like(dcw_ref)
            dgh_ref[...] = jnp.zeros_like(dgh_ref)

        def head_bwd(a, dn, gh):
            rs = lax.rsqrt(jnp.mean(a * a, axis=-1, keepdims=True) + EPS)
            ah = a * rs
            t = dn * gh
            return rs * (t - ah * jnp.mean(t * ah, axis=-1, keepdims=True)), jnp.sum(dn * ah, axis=0, keepdims=True)

        def head_grad(col):
            da, dg = head_bwd(heads_ref[:, col:col + HD], dhn_ref[:, col:col + HD], gh_ref[:, col:col + HD])
            dgh_ref[:, col:col + HD] += dg
            return da

        vg, dvg_dv = _gelu_with_grad(p_ref[:, v0:v0 + DS])
        vhat, rstd = _layer_norm_stats(vg)
        vnb = (vhat * lng_ref[...] + lnb_ref[...]).astype(BF16)
        low = _tri_mask(True)
        ones = jnp.ones((HALO, HD), BF16)
        for h in range(NSH):
            w_h = ws_ref[h]
            wt = jnp.where(low, w_h, 0.0).astype(BF16)
            bcol = bst_ref[:, h:h + 1]
            da = head_grad(h * HD)
            ug, dug_du = _gelu_with_grad(p_ref[:, u0 + h * HD:u0 + (h + 1) * HD])
            dws = jnp.zeros((CHUNK, CHUNK), F32)
            dbs = jnp.zeros((HALO, CHUNK), F32)
            mixed_parts = []
            for c in range(nck):
                rows = slice(c * CHUNK, (c + 1) * CHUNK)
                blk = vnb[rows, h * HD:(h + 1) * HD]
                mixed_parts.append(jnp.dot(wt, blk, preferred_element_type=F32) + bcol)
                dmb = (da[rows] * ug[rows]).astype(BF16)
                dws = dws + lax.dot_general(dmb, blk, (((1,), (1,)), ((), ())), preferred_element_type=F32)
                dbs = dbs + lax.dot_general(ones, dmb, (((1,), (1,)), ((), ())), preferred_element_type=F32)
                dvn_ref[c * CHUNK:(c + 1) * CHUNK, h * HD:(h + 1) * HD] = lax.dot_general(
                    wt, dmb, (((0,), (0,)), ((), ())), preferred_element_type=F32)
            mixed = mixed_parts[0] if nck == 1 else jnp.concatenate(mixed_parts, axis=0)
            dp_ref[:, u0 + h * HD:u0 + (h + 1) * HD] = ((da * mixed) * dug_du).astype(BF16)
            dws_ref[h] += jnp.where(low, dws, 0.0)
            dbs_ref[h] += dbs
        dvn = dvn_ref[...]
        dlng_ref[...] += jnp.sum(dvn * vhat, axis=0, keepdims=True)
        dlnb_ref[...] += jnp.sum(dvn, axis=0, keepdims=True)
        dvh = dvn * lng_ref[...]
        dvg = rstd * (dvh - jnp.mean(dvh, axis=-1, keepdims=True)
                      - vhat * jnp.mean(dvh * vhat, axis=-1, keepdims=True))
        dp_ref[:, v0:v0 + DS] = (dvg * dvg_dv).astype(BF16)

        dc = jnp.concatenate([head_grad(DS + h * HD) for h in range(NCH)], axis=1)
        dc_nx = jnp.concatenate(
            [head_bwd(heads_nx_ref[:, h * HD:(h + 1) * HD], dhn_nx_ref[:, h * HD:(h + 1) * HD],
                      gh_ref[:, DS + h * HD:DS + (h + 1) * HD])[0] for h in range(NCH)], axis=1)
        bg = p_ref[:, b0:b0 + DC]
        cg = p_ref[:, c0:c0 + DC]
        xin = p_ref[:, x0:x0 + DC]
        dp_ref[:, b0:b0 + DC] = (dc * ycv_ref[...]).astype(BF16)
        dyv = dc * bg
        buf_ref[0:tr, :] = dyv
        buf_ref[tr:tr + HALO, :] = jnp.where(i < n - 1, dc_nx * b_nx_ref[...], 0.0)
        sh1 = buf_ref[1:1 + tr, :]
        sh0 = buf_ref[2:2 + tr, :]
        dxc = cw_ref[2:3, :] * dyv + cw_ref[1:2, :] * sh1 + cw_ref[0:1, :] * sh0
        xc = cg * xin
        dp_ref[:, c0:c0 + DC] = (dxc * xin).astype(BF16)
        dp_ref[:, x0:x0 + DC] = (dxc * cg).astype(BF16)
        dcw_ref[0:1, :] += jnp.sum(sh0 * xc, axis=0, keepdims=True)
        dcw_ref[1:2, :] += jnp.sum(sh1 * xc, axis=0, keepdims=True)
        dcw_ref[2:3, :] += jnp.sum(dyv * xc, axis=0, keepdims=True)

        for h in range(NMH):
            do = head_grad(DS + DC + h * HD).astype(BF16)
            qh = (p_ref[:, q0 + h * HD:q0 + (h + 1) * HD] * SCALE).astype(BF16)
            kh = kv_ref[:, h * HD:(h + 1) * HD].astype(BF16)
            vh = kv_ref[:, DM + h * HD:DM + (h + 1) * HD].astype(BF16)
            p = _softmax_rows(qh, kh)
            dpr = lax.dot_general(do, vh, (((1,), (1,)), ((), ())), preferred_element_type=F32)
            ds = (p * (dpr - jnp.sum(dpr * p, axis=-1, keepdims=True))).astype(BF16)
            dp_ref[:, q0 + h * HD:q0 + (h + 1) * HD] = (
                jnp.dot(ds, kh, preferred_element_type=F32) * SCALE).astype(BF16)
            dkv_ref[:, h * HD:(h + 1) * HD] += lax.dot_general(
                ds, qh, (((0,), (0,)), ((), ())), preferred_element_type=F32)
            dkv_ref[:, DM + h * HD:DM + (h + 1) * HD] += lax.dot_general(
                p.astype(BF16), do, (((0,), (0,)), ((), ())), preferred_element_type=F32)

    full = lambda shape: pl.BlockSpec(shape, lambda i: (0,) * len(shape))
    row = lambda c: pl.BlockSpec((tr, c), lambda i: (i, 0))
    nxt = lambda col: pl.BlockSpec((HALO, DC), lambda i: (jnp.minimum((i + 1) * hb, last_hb), col))
    return pl.pallas_call(
        body, name=name, grid=(n,),
        in_specs=[row(D), row(D), row(DIN), row(DC), nxt(DS // DC), nxt(DS // DC), nxt(b0 // DC),
                  full((NMEM, 2 * DM)), full((NSH, CHUNK, CHUNK)), full((CHUNK, NSH)),
                  full((1, DS)), full((1, DS)), full((3, DC)), full((1, D)), ANY],
        out_specs=[row(DIN), full((NMEM, 2 * DM)), full((NSH, CHUNK, CHUNK)), full((NSH, HALO, CHUNK)),
                   full((1, DS)), full((1, DS)), full((HALO, DC)), full((1, D))],
        out_shape=[jax.ShapeDtypeStruct((S, DIN), BF16), jax.ShapeDtypeStruct((NMEM, 2 * DM), F32),
                   jax.ShapeDtypeStruct((NSH, CHUNK, CHUNK), F32), jax.ShapeDtypeStruct((NSH, HALO, CHUNK), F32),
                   jax.ShapeDtypeStruct((1, DS), F32), jax.ShapeDtypeStruct((1, DS), F32),
                   jax.ShapeDtypeStruct((HALO, DC), F32), jax.ShapeDtypeStruct((1, D), F32)],
        scratch_shapes=[pltpu.VMEM((tr + HALO, DC), F32), pltpu.VMEM((tr, DS), F32)],
        compiler_params=_cp(("arbitrary",), VMEM_MB),
    )(dhn, heads, proj, ycv, dhn, heads, proj, kv, w_s, bs_t, ln_g, ln_b, conv_w, g_head, after)


def _place():
    x, y, c = lax.axis_index("x"), lax.axis_index("y"), lax.axis_index("c")
    chips = [(1 - x, y), (x, 1 - y), (1 - x, 1 - y)]
    return x, y, c, chips


ANY = pl.BlockSpec(memory_space=pl.ANY)


HBM = pl.BlockSpec(memory_space=pltpu.HBM)
SEM = pl.BlockSpec(memory_space=pltpu.SEMAPHORE)
EFFECT = pltpu.SideEffectType.DATAFLOW_SIDE_EFFECTING
N_PEER_CHIPS = 3
N_NEIGHBOUR_CHIPS = 2
CONV_PAD = (32, 256)


def _in_hbm(a):
    return pltpu.with_memory_space_constraint(a, pltpu.HBM)


def _allgather_start(bufs, forwards, after, collective_id, *, name):
    arrs = list(bufs) + list(forwards)
    nw, nb = len(arrs), len(bufs)

    def body(*refs):
        ins, send, recv = refs[:nw], refs[nw + 1:2 * nw + 1], refs[2 * nw + 1:3 * nw + 1]
        token = refs[4 * nw + 1]
        x, y, c, chips = _place()
        s = 2 * x + y
        slots = [2 * cx + cy for cx, cy in chips]
        _handshake([(cx, cy, c) for cx, cy in chips[:N_NEIGHBOUR_CHIPS]])
        for w in range(nb, nw):
            q = arrs[w].shape[1] // 4
            for j in range(N_NEIGHBOUR_CHIPS):
                rows = ins[w].at[slots[j], pl.ds(c * 2 * q + j * q, q)]
                pltpu.make_async_remote_copy(src_ref=rows, dst_ref=rows, send_sem=send[w], recv_sem=recv[w],
                                             device_id=(*chips[1 - j], c), device_id_type=MESH).start()
        for w in range(nb):
            hr = arrs[w].shape[1] // 2
            rows = ins[w].at[s, pl.ds(c * hr, hr)]
            for cx, cy in chips[:N_NEIGHBOUR_CHIPS]:
                pltpu.make_async_remote_copy(src_ref=rows, dst_ref=rows, send_sem=send[w], recv_sem=recv[w],
                                             device_id=(cx, cy, c), device_id_type=MESH).start()
        token[...] = jnp.zeros_like(token)

    res = pl.pallas_call(
        body, name=name,
        in_specs=[HBM] * nw + [ANY],
        out_specs=[SEM] * (2 * nw) + [HBM] * nw + [pl.BlockSpec(memory_space=pltpu.VMEM)],
        out_shape=[pltpu.SemaphoreType.DMA(())] * (2 * nw) + [pltpu.HBM(a.shape, a.dtype) for a in arrs]
        + [jax.ShapeDtypeStruct((8, 128), F32)],
        input_output_aliases={w: 2 * nw + w for w in range(nw)},
        compiler_params=pltpu.CompilerParams(has_side_effects=EFFECT, collective_id=collective_id),
    )(*[_in_hbm(a) for a in arrs], after)
    return res[:nw], res[nw:2 * nw], res[2 * nw:3 * nw], res[3 * nw]


def _handshake(peers):
    barrier = pltpu.get_barrier_semaphore()
    for peer in peers:
        pl.semaphore_signal(barrier, inc=1, device_id=peer, device_id_type=MESH)
    pl.semaphore_wait(barrier, len(peers))


def _scatter_start(parts, bufs, collective_id, *, name):
    nw = len(parts)

    def body(*refs):
        src, dst = refs[:nw], refs[nw:2 * nw]
        send, recv = refs[2 * nw:3 * nw], refs[3 * nw:4 * nw]
        token = refs[6 * nw]
        x, y, c, chips = _place()
        s = 2 * x + y
        _handshake([(cx, cy, c) for cx, cy in chips])
        for w in range(nw):
            for cx, cy in chips:
                pltpu.make_async_remote_copy(src_ref=src[w].at[2 * cx + cy], dst_ref=dst[w].at[s], send_sem=send[w],
                                             recv_sem=recv[w], device_id=(cx, cy, c), device_id_type=MESH).start()
        token[...] = jnp.zeros_like(token)

    res = pl.pallas_call(
        body, name=name,
        in_specs=[HBM] * (2 * nw),
        out_specs=[SEM] * (2 * nw) + [HBM] * (2 * nw) + [pl.BlockSpec(memory_space=pltpu.VMEM)],
        out_shape=[pltpu.SemaphoreType.DMA(())] * (2 * nw) + [pltpu.HBM(a.shape, a.dtype) for a in parts + bufs]
        + [jax.ShapeDtypeStruct((8, 128), F32)],
        input_output_aliases={k: 2 * nw + k for k in range(2 * nw)},
        compiler_params=pltpu.CompilerParams(has_side_effects=EFFECT, collective_id=collective_id),
    )(*[_in_hbm(a) for a in parts + bufs])
    return res[:nw], res[nw:2 * nw], res[2 * nw:3 * nw], res[3 * nw:4 * nw], res[4 * nw]


def _sibling_start(srcs, whole, collective_id, *, name):
    nw = len(srcs)
    lands = [lax.empty((a.shape[0], a.shape[1] if whole else a.shape[1] // 2, a.shape[2]), a.dtype) for a in srcs]

    def body(*refs):
        src, land = refs[:nw], refs[nw:2 * nw]
        send, recv = refs[2 * nw:3 * nw], refs[3 * nw:4 * nw]
        token = refs[6 * nw]
        x, y, c, _ = _place()
        _handshake([(x, y, 1 - c)])
        for w in range(nw):
            hr = srcs[w].shape[1] // 2
            rows = src[w] if whole else src[w].at[:, pl.ds((1 - c) * hr, hr)]
            pltpu.make_async_remote_copy(src_ref=rows, dst_ref=land[w], send_sem=send[w], recv_sem=recv[w],
                                         device_id=(x, y, 1 - c), device_id_type=MESH).start()
        token[...] = jnp.zeros_like(token)

    res = pl.pallas_call(
        body, name=name,
        in_specs=[HBM] * (2 * nw),
        out_specs=[SEM] * (2 * nw) + [HBM] * (2 * nw) + [pl.BlockSpec(memory_space=pltpu.VMEM)],
        out_shape=[pltpu.SemaphoreType.DMA(())] * (2 * nw) + [pltpu.HBM(a.shape, a.dtype) for a in srcs + lands]
        + [jax.ShapeDtypeStruct((8, 128), F32)],
        input_output_aliases={k: 2 * nw + k for k in range(2 * nw)},
        compiler_params=pltpu.CompilerParams(has_side_effects=EFFECT, collective_id=collective_id),
    )(*[_in_hbm(a) for a in srcs + lands])
    return res[:nw], res[nw:2 * nw], res[2 * nw:3 * nw], res[3 * nw:4 * nw], res[4 * nw]


def _transfer_wait(sends, recvs, thru, sizes, after, *, name):
    n = len(sends)
    flat = [a for group in thru for a in group]

    def body(*refs):
        bufs = refs[:len(flat)]
        send = refs[len(flat):len(flat) + n]
        recv = refs[len(flat) + n:len(flat) + 2 * n]
        token = refs[2 * len(flat) + 2 * n + 1]
        token[...] = jnp.zeros_like(token)
        x, y, c, _ = _place()
        pos = 0
        for k in range(n):
            slots, rows = sizes[k]
            region = bufs[pos].at[pl.ds(0, slots), pl.ds(0, rows)]
            pos += len(thru[k])
            cp = pltpu.make_async_remote_copy(src_ref=region, dst_ref=region, send_sem=send[k], recv_sem=recv[k],
                                              device_id=(x, y, 1 - c), device_id_type=MESH)
            cp.wait_send()
            cp.wait_recv()

    res = pl.pallas_call(
        body, name=name,
        in_specs=[HBM] * len(flat) + [SEM] * (2 * n) + [pl.BlockSpec(memory_space=pl.ANY)],
        out_specs=[HBM] * len(flat) + [pl.BlockSpec(memory_space=pltpu.VMEM)],
        out_shape=[pltpu.HBM(a.shape, a.dtype) for a in flat] + [jax.ShapeDtypeStruct((8, 128), F32)],
        input_output_aliases={k: k for k in range(len(flat))},
        compiler_params=pltpu.CompilerParams(has_side_effects=EFFECT),
    )(*flat, *sends, *recvs, after)
    out, pos = [], 0
    for group in thru:
        out.append(res[pos:pos + len(group)])
        pos += len(group)
    return out, res[len(flat)]


def _forward_gathered(bufs, after, *, name):
    nw = len(bufs)

    def body(*refs):
        outs = refs[nw + 1:2 * nw + 1]
        d_send, d_recv, i_send, i_recv = refs[2 * nw + 1:]
        x, y, c, chips = _place()
        me, sibling = (x, y, c), (x, y, 1 - c)
        slots = [2 * cx + cy for cx, cy in chips]

        def rows(w, j, start, n):
            return outs[w].at[slots[j], pl.ds(start, n)]

        def d2d(w, j, which, to):
            hr = bufs[w].shape[1] // 2
            r = rows(w, j, which * hr, hr)
            return pltpu.make_async_remote_copy(
                src_ref=r, dst_ref=r, send_sem=d_send.at[N_PEER_CHIPS * w + j],
                recv_sem=d_recv.at[N_PEER_CHIPS * w + j], device_id=to, device_id_type=MESH)

        def ici(w, j, slot_j, to):
            q = bufs[w].shape[1] // 4
            r = rows(w, slot_j, c * 2 * q + j * q, q)
            return pltpu.make_async_remote_copy(
                src_ref=r, dst_ref=r, send_sem=i_send.at[N_NEIGHBOUR_CHIPS * w + j],
                recv_sem=i_recv.at[N_NEIGHBOUR_CHIPS * w + j], device_id=to, device_id_type=MESH)

        started = []
        for w in range(nw):
            started += [ici(w, 0, 0, (*chips[1], c)), ici(w, 1, 1, (*chips[0], c))]
            started += [d2d(w, j, c, sibling) for j in range(N_NEIGHBOUR_CHIPS)]
        for cp in started:
            cp.start()
        diag = N_PEER_CHIPS - 1
        for w in range(nw):
            for j in range(N_NEIGHBOUR_CHIPS):
                ici(w, j, diag, me).wait_recv()
            cp = d2d(w, diag, c, sibling)
            cp.start()
            started.append(cp)
        for w in range(nw):
            for j in range(N_PEER_CHIPS):
                d2d(w, j, 1 - c, me).wait_recv()
        for cp in started:
            cp.wait_send()

    return pl.pallas_call(
        body, name=name,
        in_specs=[ANY] * (nw + 1), out_specs=[ANY] * nw,
        out_shape=[jax.ShapeDtypeStruct(a.shape, a.dtype) for a in bufs],
        input_output_aliases={w: w for w in range(nw)},
        scratch_shapes=[pltpu.SemaphoreType.DMA((N_PEER_CHIPS * nw,)), pltpu.SemaphoreType.DMA((N_PEER_CHIPS * nw,)),
                        pltpu.SemaphoreType.DMA((N_NEIGHBOUR_CHIPS * nw,)),
                        pltpu.SemaphoreType.DMA((N_NEIGHBOUR_CHIPS * nw,))],
    )(*bufs, after)


def _forward_halves(bufs, which, after, *, name):
    nw = len(bufs)
    n = len(which)

    def body(*refs):
        outs = refs[nw + 1:2 * nw + 1]
        send, recv = refs[2 * nw + 1:]
        x, y, c, chips = _place()
        me, sibling = (x, y, c), (x, y, 1 - c)

        def d2d(w, t, half, to):
            cx, cy = chips[which[t]]
            hr = bufs[w].shape[1] // 2
            rows = outs[w].at[2 * cx + cy, pl.ds(half * hr, hr)]
            return pltpu.make_async_remote_copy(src_ref=rows, dst_ref=rows, send_sem=send.at[n * w + t],
                                                recv_sem=recv.at[n * w + t], device_id=to, device_id_type=MESH)

        passed = [d2d(w, t, c, sibling) for w in range(nw) for t in range(n)]
        for cp in passed:
            cp.start()
        for w in range(nw):
            for t in range(n):
                d2d(w, t, 1 - c, me).wait_recv()
        for cp in passed:
            cp.wait_send()

    return pl.pallas_call(
        body, name=name,
        in_specs=[ANY] * (nw + 1), out_specs=[ANY] * nw,
        out_shape=[jax.ShapeDtypeStruct(a.shape, a.dtype) for a in bufs],
        input_output_aliases={w: w for w in range(nw)},
        scratch_shapes=[pltpu.SemaphoreType.DMA((n * nw,)), pltpu.SemaphoreType.DMA((n * nw,))],
    )(*bufs, after)


def _allreduce_small(p, after, *, name):
    R = p.shape[0]
    hr = R // 2

    def body(p_ref, _after_ref, out_ref, sib_ref, sum_ref, gat_ref, tot_ref, send, recv):
        x, y, c, chips = _place()
        s = 2 * x + y
        sibling = (x, y, 1 - c)
        rows = pl.ds(pl.multiple_of(c * hr, 8), hr)
        swap = pltpu.make_async_remote_copy(src_ref=p_ref, dst_ref=sib_ref, send_sem=send.at[0], recv_sem=recv.at[0],
                                            device_id=sibling, device_id_type=MESH)
        swap.start()
        swap.wait()
        sum_ref[...] = p_ref[...] + sib_ref[...]
        gat_ref[s] = sum_ref[rows, :]
        cps = [pltpu.make_async_remote_copy(src_ref=sum_ref.at[rows], dst_ref=gat_ref.at[s], send_sem=send.at[1 + j],
                                            recv_sem=recv.at[1 + j], device_id=(cx, cy, c), device_id_type=MESH)
               for j, (cx, cy) in enumerate(chips)]
        for cp in cps:
            cp.start()
        for cp in cps:
            cp.wait()
        tot_ref[...] = ((gat_ref[0] + gat_ref[1]) + gat_ref[2]) + gat_ref[3]
        out_ref[rows, :] = tot_ref[...]
        share = pltpu.make_async_remote_copy(src_ref=tot_ref, dst_ref=out_ref.at[rows], send_sem=send.at[4],
                                             recv_sem=recv.at[4], device_id=sibling, device_id_type=MESH)
        share.start()
        share.wait_send()
        other = out_ref.at[pl.ds(pl.multiple_of((1 - c) * hr, 8), hr)]
        pltpu.make_async_remote_copy(src_ref=other, dst_ref=other, send_sem=send.at[4], recv_sem=recv.at[4],
                                     device_id=(x, y, c), device_id_type=MESH).wait_recv()

    vmem = pl.BlockSpec(memory_space=pltpu.VMEM)
    return pl.pallas_call(
        body, name=name, in_specs=[vmem, ANY], out_specs=vmem,
        out_shape=jax.ShapeDtypeStruct((R, 128), F32),
        scratch_shapes=[pltpu.VMEM((R, 128), F32), pltpu.VMEM((R, 128), F32), pltpu.VMEM((NCHIP, hr, 128), F32),
                        pltpu.VMEM((hr, 128), F32), pltpu.SemaphoreType.DMA((5,)), pltpu.SemaphoreType.DMA((5,))],
    )(p, after)


def _select_half_bf16(g, half, add, slot, *, name):
    _, R, C = g.shape
    hr = R // 2
    tr = _pick_rows(hr, 16)
    nb = hr // tr
    sel = jnp.concatenate([jnp.reshape(half, (1,)).astype(jnp.int32), slot])

    def body(s_ref, g_ref, a_ref, o_ref, own_ref):
        val = (g_ref[...].astype(F32) + a_ref[...].astype(F32)).astype(BF16)
        o_ref[...] = val

        @pl.when(pl.program_id(1) == s_ref[1])
        def _():
            own_ref[...] = val

    g_spec = pl.BlockSpec((None, tr, C), lambda i, j, s: (j, s[0] * nb + i, 0))
    o_spec = pl.BlockSpec((None, tr, C), lambda i, j, s: (j, i, 0))
    own_spec = pl.BlockSpec((None, tr, C), lambda i, j, s: (s[1], i, 0))
    shape = jax.ShapeDtypeStruct((NCHIP, hr, C), BF16)
    return pl.pallas_call(
        body, name=name,
        grid_spec=pltpu.PrefetchScalarGridSpec(
            num_scalar_prefetch=1, grid=(nb, NCHIP), in_specs=[g_spec, o_spec], out_specs=[o_spec, own_spec]),
        out_shape=[shape, shape],
        compiler_params=_cp(("parallel", "arbitrary"), VMEM_MB),
    )(sel, g, add)


def _adamw_math(w, g, m, v):
    m = ADAM_B1 * m + (1.0 - ADAM_B1) * g
    v = ADAM_B2 * v + (1.0 - ADAM_B2) * (g * g)
    m_hat = m / (1.0 - ADAM_B1 ** ADAM_STEP)
    v_hat = v / (1.0 - ADAM_B2 ** ADAM_STEP)
    delta = -ADAM_LR * (m_hat / (jnp.sqrt(v_hat) + ADAM_EPS) + ADAM_WD * w)
    return delta, m, v


def _adamw(w, g_mine, g_sib, m, v, core, *, name):
    R, C = w.shape
    hr = R // 2
    tr = _pick_rows(hr, 16)
    nb = hr // tr
    row = pl.BlockSpec((tr, C), lambda hh, i, c: (hh * nb + i, 0))
    mine = pl.BlockSpec((NCHIP, tr, C), lambda hh, i, c: (0, jnp.where(hh == c[0], i, 0), 0))
    sibs = pl.BlockSpec((NCHIP, tr, C), lambda hh, i, c: (0, jnp.where(hh == c[0], 0, i), 0))

    def slot_sum(ref):
        acc = ref[0].astype(F32) + ref[1].astype(F32)
        for j in range(2, NCHIP):
            acc = acc + ref[j].astype(F32)
        return acc

    def body(c_ref, w_ref, gm_ref, gs_ref, m_ref, v_ref, go_ref, d_ref, mo_ref, vo_ref):
        gv = jnp.where(pl.program_id(0) == c_ref[0], slot_sum(gm_ref), slot_sum(gs_ref))
        d, mn, vn = _adamw_math(w_ref[...], gv, m_ref[...], v_ref[...])
        go_ref[...] = gv
        d_ref[...] = d
        mo_ref[...] = mn
        vo_ref[...] = vn

    return pl.pallas_call(
        body, name=name,
        grid_spec=pltpu.PrefetchScalarGridSpec(
            num_scalar_prefetch=1, grid=(2, nb),
            in_specs=[row, mine, sibs, row, row], out_specs=[row] * 4),
        out_shape=[jax.ShapeDtypeStruct((R, C), F32)] * 4,
        compiler_params=_cp(("parallel", "parallel"), VMEM_MB),
    )(core, w, g_mine, g_sib, m, v)


def _adamw_small(ws, gs, ms, vs, *, name):
    n = len(ws)

    def body(*refs):
        w_r, g_r, m_r, v_r = refs[:n], refs[n:2 * n], refs[2 * n:3 * n], refs[3 * n:4 * n]
        d_r, mo_r, vo_r = refs[4 * n:5 * n], refs[5 * n:6 * n], refs[6 * n:7 * n]
        for k in range(n):
            d, mn, vn = _adamw_math(w_r[k][...], g_r[k][...], m_r[k][...], v_r[k][...])
            d_r[k][...] = d
            mo_r[k][...] = mn
            vo_r[k][...] = vn

    shapes = [jax.ShapeDtypeStruct(w.shape, F32) for w in ws]
    res = pl.pallas_call(body, name=name, out_shape=shapes * 3)(*ws, *gs, *ms, *vs)
    return res[:n], res[n:2 * n], res[2 * n:]


_PACK_ROWS = 8


def _pack(parts):
    rows = []
    for a in parts:
        flat = a.reshape(-1)
        n = -(-flat.shape[0] // (_PACK_ROWS * 128)) * (_PACK_ROWS * 128)
        rows.append(jnp.pad(flat, (0, n - flat.shape[0])).reshape(-1, 128))
    total = sum(r.shape[0] for r in rows)
    if total % 16:
        rows.append(jnp.zeros((16 - total % 16, 128), F32))
    return jnp.concatenate(rows, axis=0)


def _unpack(p, shapes):
    out, r = [], 0
    for shp in shapes:
        n = math.prod(shp)
        nr = -(-n // (_PACK_ROWS * 128)) * _PACK_ROWS
        out.append(p[r:r + nr].reshape(-1)[:n].reshape(shp))
        r += nr
    return out


def kernel(x, mem, g_mix, w_in, ln_v_g, ln_v_b, w_s, b_s, conv_w, g_mem, w_kv, g_head, w_o, g_ffn, w_ffn1, w_ffn2, g_final, loss_target, m_g_mix, m_w_in, m_ln_v_g, m_ln_v_b, m_w_s, m_b_s, m_conv_w, m_g_mem, m_w_kv, m_g_head, m_w_o, m_g_ffn, m_w_ffn1, m_w_ffn2, m_g_final, v_g_mix, v_w_in, v_ln_v_g, v_ln_v_b, v_w_s, v_b_s, v_conv_w, v_g_mem, v_w_kv, v_g_head, v_w_o, v_g_ffn, v_w_ffn1, v_w_ffn2, v_g_final):
    sds = jax.ShapeDtypeStruct
    xi, yi = lax.axis_index("x"), lax.axis_index("y")
    shard = 2 * xi + yi
    x2d, mem2d, tgt = x[0], mem[0], loss_target[0]
    ws3, bs2 = w_s[0], b_s[0]
    g_final2 = g_final.reshape(1, D)
    dff4 = DFF // NCHIP
    din4 = DIN // NCHIP
    dcv4 = DC // NCHIP

    big = [w_in[0].T, w_kv[0], w_o[0], w_ffn1[0], w_ffn2[0]]
    big_names = ["w_in", "w_kv", "w_o", "w_ffn1", "w_ffn2"]
    slot = jnp.reshape(shard, (1,)).astype(jnp.int32)
    core = jnp.reshape(lax.axis_index("c"), (1,)).astype(jnp.int32)
    conv_pad = jnp.pad(conv_w[0], ((0, CONV_PAD[0] - 3), (0, CONV_PAD[1] - dcv4)))
    conv_slots = lax.dynamic_update_slice(jnp.zeros((NCHIP,) + CONV_PAD, F32), conv_pad[None], (shard, 0, 0))

    gather_ids = {"in": 16, "kvo": 17, "ffn1": 18, "ffn2": 19, "ffn2d": 20}

    def gather_start(bufs, after, nm, forwards=()):
        return _allgather_start(bufs, forwards, after, gather_ids[nm], name="ag_start_" + nm)

    def gather_wait(state, idx, after, nm):
        send, recv, bufs, _ = state
        got, token = _transfer_wait([send[k] for k in idx], [recv[k] for k in idx], [[bufs[k]] for k in idx],
                                    [(N_NEIGHBOUR_CHIPS, bufs[k].shape[1] // 2) for k in idx], after, name="ag_wait_" + nm)
        return [g[0] for g in got], token

    cast = lambda k, after: _cast_into_slot(big[k], slot, after, name="cast_" + big_names[k])
    ag_in = gather_start([cast(0, slot), conv_slots], slot, "in")
    bs_t = bs2.T

    h = _rms_fwd(x2d, g_mix, name="rms_mix", after=[ag_in[3]])
    mem_n = _rms_fwd(mem2d, g_mem, name="rms_mem", after=[h])
    kvo_b = [cast(1, mem_n)]
    kvo_b.append(cast(2, kvo_b[0]))
    w1_b = cast(3, kvo_b[1])
    w2_b = cast(4, w1_b)
    got_in, tok = gather_wait(ag_in, [0, 1], w2_b, "in")
    win4, conv4 = _forward_gathered(got_in, tok, name="ag_fwd_in")
    ag_kvo = gather_start(kvo_b, conv4, "kvo")
    w_in_t = win4.reshape(DIN, D)
    conv_full = conv4[:, :3, :dcv4].transpose(1, 0, 2).reshape(3, DC)
    NEAR, FAR = [0, 1], [2]

    def diagonal_wait(state, ks, after, nm):
        send, recv, bufs, _ = state
        got, token = _transfer_wait([send[k] for k in ks], [recv[k] for k in ks], [[bufs[k]] for k in ks],
                                    [(1, bufs[k].shape[1] // 2) for k in ks], after, name="ag_waitd_" + nm)
        return [g[0] for g in got], token

    (proj,) = _matmul(h, w_in_t, name="mm_proj", tb=True, M=S, N=DIN, K=D, tn=DIN // 2, outs=[sds((S, DIN), F32)],
                      after=[ag_kvo[3]])
    got_kvo, tok = gather_wait(ag_kvo, [0, 1], proj, "kvo")
    ag_w1 = gather_start([w1_b], tok, "ffn1", forwards=got_kvo)
    kvo_n = _forward_halves(ag_w1[2][1:], NEAR, ag_w1[3], name="ag_fwdn_kvo")
    ag_w1 = (ag_w1[0], ag_w1[1], [ag_w1[2][0]] + list(kvo_n), ag_w1[3])
    kvo_d, tok = diagonal_wait(ag_w1, [1, 2], ag_w1[3], "kvo")
    wkv4, wo4 = _forward_halves(kvo_d, FAR, tok, name="ag_fwdd_kvo")
    w_kv_full = wkv4.reshape(D, 2 * DM)
    w_o_full = wo4.reshape(D, D)
    (kv,) = _matmul(mem_n, w_kv_full, name="mm_kv", M=NMEM, N=2 * DM, K=D, outs=[sds((NMEM, 2 * DM), F32)])
    heads, hn, ycv = _mix_fwd(proj, kv, ws3, bs_t, ln_v_g, ln_v_b, conv_full, g_head, name="mix_fwd")
    (x2,) = _matmul(hn, w_o_full, name="mm_wo", M=S, N=D, K=D, outs=[sds((S, D), F32)],
                    epi=lambda acc, res: (acc + res,), extras=[(x2d, _tile_spec())])
    h2 = _rms_fwd(x2, g_ffn, name="rms_ffn")
    near = jnp.stack([shard, 2 * (1 - xi) + yi, 2 * xi + (1 - yi)]).astype(jnp.int32)
    far = jnp.reshape(2 * (1 - xi) + (1 - yi), (1,)).astype(jnp.int32)

    w1_shard = lambda tn, tk: pl.BlockSpec((None, tk, tn), lambda j, i, k, s: (s[j], k, 0))
    act_cols = lambda tm, tn: [pl.BlockSpec((tm, tn), lambda j, i, k, s: (i, s[j]))] * 2

    def relu2(acc):
        r = jnp.maximum(acc, 0.0)
        return r * r, 2.0 * r

    got_w1, tok = gather_wait(ag_w1, [0], h2, "ffn1")
    ag_w2 = gather_start([w2_b], tok, "ffn2", forwards=got_w1)
    (w1n,) = _forward_halves([ag_w2[2][1]], NEAR, ag_w2[3], name="ag_fwdn_ffn1")
    ag_w2 = (ag_w2[0], ag_w2[1], [ag_w2[2][0], w1n], ag_w2[3])
    act, dact_df = _matmul(h2, w1n, name="mm_ffn1_near", M=S, N=3 * dff4, K=D, tn=dff4, b_spec=w1_shard,
                           out_specs=act_cols, outs=[sds((S, DFF), BF16)] * 2, epi=relu2, slots=near)
    w1d, tok = diagonal_wait(ag_w2, [1], act, "ffn1")
    (w14,) = _forward_halves(w1d, FAR, tok, name="ag_fwdd_ffn1")
    act, dact_df = _matmul(h2, w14, name="mm_ffn1_far", M=S, N=dff4, K=D, tn=dff4, b_spec=w1_shard,
                           out_specs=act_cols, outs=[sds((S, DFF), BF16)] * 2, epi=relu2, slots=far,
                           into=[act, dact_df])

    act_shard = lambda tm, tk: pl.BlockSpec((tm, tk), lambda j, i, k, s: (i, s[k]))
    w2_shard = lambda tn, tk: pl.BlockSpec((None, tk, tn), lambda j, i, k, s: (s[k], 0, j))
    got_w2, tok = gather_wait(ag_w2, [0], act, "ffn2")
    ag_w2d = gather_start([], tok, "ffn2d", forwards=got_w2)
    (w2n,) = _forward_halves(ag_w2d[2], NEAR, ag_w2d[3], name="ag_fwdn_ffn2")
    ag_w2d = (ag_w2d[0], ag_w2d[1], [w2n], ag_w2d[3])
    (x3,) = _matmul(act, w2n, name="mm_ffn2_near", M=S, N=D, K=3 * dff4, tm=2 * TM, tk=dff4,
                    a_spec=act_shard, b_spec=w2_shard, outs=[sds((S, D), F32)], epi=lambda acc, res: (acc + res,),
                    extras=[(x2, _tile_spec())], slots=near)
    w2d, tok = diagonal_wait(ag_w2d, [0], x3, "ffn2")
    (w24,) = _forward_halves(w2d, FAR, tok, name="ag_fwdd_ffn2")
    (x3,) = _matmul(act, w24, name="mm_ffn2_far", M=S, N=D, K=dff4, tm=2 * TM, tk=dff4, a_spec=act_shard,
                    b_spec=w2_shard, outs=[sds((S, D), F32)], epi=lambda acc, res: (acc + res,),
                    extras=[(x3, _tile_spec())], slots=far)
    w2_full = w24.reshape(DFF, D)

    ci = lax.axis_index("c")

    def rs_sibling(g4, nm):
        return _sibling_start([g4], False, 1 + big_names.index(nm), name="rs_sib_" + nm)

    def rs_chips(state, after, nm):
        send, recv, g4, land, _ = state
        (((land_, g4_),), _) = _transfer_wait(send, recv, [[land[0], g4[0]]], [(NCHIP, land[0].shape[1])], after,
                                             name="rs_sibwait_" + nm)
        part, buf = _select_half_bf16(g4_, ci, land_, slot, name="rs_add_" + nm)
        return _scatter_start([part], [buf], 1 + 2 * len(big_names) + big_names.index(nm), name="rs_start_" + nm)

    def rs_end(state, after, nm):
        send, recv, parts, bufs, _ = state
        (((buf, _),), _) = _transfer_wait(send, recv, [[bufs[0], parts[0]]], [(N_PEER_CHIPS, bufs[0].shape[1])], after,
                                          name="rs_wait_" + nm)
        return _sibling_start([buf], True, 1 + len(big_names) + big_names.index(nm), name="rs_share_" + nm)

    big_m = [m_w_in[0].T, m_w_kv[0], m_w_o[0], m_w_ffn1[0], m_w_ffn2[0]]
    big_v = [v_w_in[0].T, v_w_kv[0], v_w_o[0], v_w_ffn1[0], v_w_ffn2[0]]
    big_out = {}

    def rs_finish(k, state, after):
        send, recv, mine, land, _ = state
        nm = big_names[k]
        (((land_, mine_),), _) = _transfer_wait(send, recv, [[land[0], mine[0]]], [(NCHIP, land[0].shape[1])], after,
                                               name="rs_sharewait_" + nm)
        big_out[nm] = _adamw(big[k], mine_, land_, big_m[k], big_v[k], core, name="adamw_" + nm)
        return big_out[nm][1]

    dx3, dx3b, dg_final, loss11 = _loss_bwd(x3, g_final2, tgt, name="loss_bwd")
    (dw2,) = _matmul(act, dx3b, name="mm_dw2", ta=True, M=DFF, N=D, K=S, tn=D, outs=[sds((DFF, D), BF16)])
    sib_w2 = rs_sibling(dw2.reshape(NCHIP, dff4, D), "w_ffn2")
    (dfb,) = _matmul(dx3b, w2_full, name="mm_dact", tb=True, M=S, N=DFF, K=D, tn=dff4, outs=[sds((S, DFF), BF16)],
                     epi=lambda acc, g: (acc * g.astype(F32),), extras=[(dact_df, _tile_spec())],
                     after=[sib_w2[4]])
    rs_w2 = rs_chips(sib_w2, dfb, "w_ffn2")

    def dw1_out(tm, tn):
        nb = dff4 // tn
        return [pl.BlockSpec((None, tm, tn), lambda j, i, k: (j // nb, i, j % nb))]

    (dw1,) = _matmul(h2, dfb, name="mm_dw1", ta=True, M=D, N=DFF, K=S, tn=dff4, outs=[sds((NCHIP, D, dff4), BF16)],
                     out_specs=dw1_out, after=[rs_w2[4]])
    sib_w1 = rs_sibling(dw1, "w_ffn1")

    def w1_rows(tn, tk):
        kb = dff4 // tk
        return pl.BlockSpec((None, tn, tk), lambda j, i, k: (k // kb, j, k % kb))

    (dh2,) = _matmul(dfb, w14, name="mm_dh2", tb=True, M=S, N=D, K=DFF, tm=2 * TM, b_spec=w1_rows,
                     outs=[sds((S, D), F32)], after=[sib_w1[4]])
    rs_w1 = rs_chips(sib_w1, dh2, "w_ffn1")
    dx2, dx2b, dg_ffn = _rms_bwd(dh2, x2, g_ffn, dx3, name="rms_ffn_bwd", after=[rs_w1[4]])
    (dwo,) = _matmul(hn, dx2b, name="mm_dwo", ta=True, M=D, N=D, K=S, outs=[sds((D, D), BF16)])
    sib_wo = rs_sibling(dwo.reshape(NCHIP, D // NCHIP, D), "w_o")
    (dhn,) = _matmul(dx2b, w_o_full, name="mm_dhn", tb=True, M=S, N=D, K=D, outs=[sds((S, D), F32)],
                     after=[sib_wo[4]])
    rs_wo = rs_chips(sib_wo, dhn, "w_o")
    sh_w2 = rs_end(rs_w2, rs_wo[4], "w_ffn2")
    dproj, dkv, dws, dbs8, dlng, dlnb, dcw8, dgh = _mix_bwd(
        dhn, heads, proj, ycv, kv, ws3, bs_t, ln_v_g, ln_v_b, conv_full, g_head, sh_w2[4], name="mix_bwd")
    (dwin_t,) = _matmul(dproj, h, name="mm_dwin", ta=True, M=DIN, N=D, K=S, tm=DIN // 2, outs=[sds((DIN, D), BF16)])
    sib_win = rs_sibling(dwin_t.reshape(NCHIP, din4, D), "w_in")
    (dwkv,) = _matmul(mem_n, dkv, name="mm_dwkv", ta=True, M=D, N=2 * DM, K=NMEM, outs=[sds((D, 2 * DM), BF16)],
                      after=[sib_win[4]])
    sib_wkv = rs_sibling(dwkv.reshape(NCHIP, D // NCHIP, 2 * DM), "w_kv")
    (dh,) = _matmul(dproj, w_in_t, name="mm_dh", M=S, N=D, K=DIN, tk=DIN, outs=[sds((S, D), F32)],
                    after=[sib_wkv[4]])
    rs_win = rs_chips(sib_win, dh, "w_in")
    rs_wkv = rs_chips(sib_wkv, rs_win[4], "w_kv")
    dx, dg_mix = _rms_bwd(dh, x2d, g_mix, dx2, name="rms_mix_bwd", want_bf=False, after=[rs_wkv[4]])
    sh_w1 = rs_end(rs_w1, dx, "w_ffn1")
    (dmem_n,) = _matmul(dkv, w_kv_full, name="mm_dmem", tb=True, M=NMEM, N=D, K=2 * DM, outs=[sds((NMEM, D), F32)],
                        after=[sh_w1[4]])
    (dg_mem,) = _rms_bwd(dmem_n, mem2d, g_mem, None, name="rms_mem_bwd", want_dx=False)
    sh_wo = rs_end(rs_wo, dg_mem, "w_o")
    done = rs_finish(4, sh_w2, sh_wo[4])
    done = rs_finish(3, sh_w1, done)
    sh_win = rs_end(rs_win, done, "w_in")
    sh_wkv = rs_end(rs_wkv, sh_win[4], "w_kv")
    done = rs_finish(2, sh_wo, sh_wkv[4])
    done = rs_finish(0, sh_win, done)
    done = rs_finish(1, sh_wkv, done)

    small_names = ["g_mix", "ln_v_g", "ln_v_b", "w_s", "b_s", "conv_w", "g_mem", "g_head", "g_ffn", "g_final"]
    small_part = [dg_mix, dlng, dlnb, dws, dbs8[:, 0, :], dcw8[:3], dg_mem, dgh, dg_ffn, dg_final, loss11]
    small_shapes = [(1, D), (1, DS), (1, DS), (NSH, CHUNK, CHUNK), (NSH, CHUNK), (3, DC), (1, D), (1, D), (1, D), (1, D),
                    (1, 1)]
    total = _allreduce_small(_pack(small_part), done, name="allreduce_small")
    small_g = _unpack(total, small_shapes)
    loss = small_g.pop()[0, 0]
    small_g[5] = lax.dynamic_slice(small_g[5], (0, shard * dcv4), (3, dcv4))
    small_w = [g_mix, ln_v_g, ln_v_b, ws3, bs2, conv_w[0], g_mem, g_head, g_ffn, g_final2]
    small_m = [m_g_mix, m_ln_v_g, m_ln_v_b, m_w_s[0], m_b_s[0], m_conv_w[0], m_g_mem, m_g_head, m_g_ffn,
               m_g_final.reshape(1, D)]
    small_v = [v_g_mix, v_ln_v_g, v_ln_v_b, v_w_s[0], v_b_s[0], v_conv_w[0], v_g_mem, v_g_head, v_g_ffn,
               v_g_final.reshape(1, D)]
    s_delta, s_m, s_v = _adamw_small(small_w, small_g, small_m, small_v, name="adamw_small")
    small_out = {nm: (g, d, mn, vn) for nm, g, d, mn, vn in zip(small_names, small_g, s_delta, s_m, s_v)}

    order = ["g_mix", "w_in", "ln_v_g", "ln_v_b", "w_s", "b_s", "conv_w", "g_mem", "w_kv", "g_head", "w_o",
             "g_ffn", "w_ffn1", "w_ffn2", "g_final"]
    like = dict(g_mix=g_mix, w_in=w_in, ln_v_g=ln_v_g, ln_v_b=ln_v_b, w_s=w_s, b_s=b_s, conv_w=conv_w, g_mem=g_mem,
                w_kv=w_kv, g_head=g_head, w_o=w_o, g_ffn=g_ffn, w_ffn1=w_ffn1, w_ffn2=w_ffn2, g_final=g_final)
    res = {**big_out, **small_out}
    res["w_in"] = [a.T for a in res["w_in"]]
    outs = [loss, dx[None]]
    for k in range(4):
        outs += [res[nm][k].reshape(like[nm].shape) for nm in order]
    return tuple(outs)
```

```python
import math

import jax
import jax.numpy as jnp
from jax import lax
from jax.experimental import pallas as pl
from jax.experimental.pallas import tpu as pltpu

F32 = jnp.float32
BF16 = jnp.bfloat16
MESH = pl.DeviceIdType.MESH

D = 2048
S = 2048
HD = 128
NH = D // HD
NMH = 4
NSH = (NH - NMH) // 2
NCH = NH - NMH - NSH
DS = NSH * HD
DC = NCH * HD
DM = NMH * HD
DIN = 2 * DS + 3 * DC + DM
CHUNK = 128
NMEM = 256
DFF = 4 * D
EPS = 1e-6
NCHIP = 4
SCALE = HD ** -0.5

ADAM_LR = 0.001
ADAM_B1 = 0.9
ADAM_B2 = 0.999
ADAM_EPS = 1e-08
ADAM_WD = 0.01
ADAM_STEP = 10

TR_EW = 256
TR_MIX = 256
TM = 512
TN = 1024
TK = 2048
N_SUB = 512
VMEM_MB = 56
HALO = 8


def _pick(n, target, q=128):
    best = None
    for t in range(q, min(n, target) + 1, q):
        if n % t == 0:
            best = t
    return n if best is None else best


def _pick_rows(n, q):
    below = _pick(n, TR_EW, q)
    if 2 * below >= TR_EW:
        return below
    above = [t for t in range(TR_EW, min(n, 4 * TR_EW) + 1, q) if n % t == 0]
    return above[0] if above else below


def _cp(sem=None, vmem_mb=None, **kw):
    d = dict(kw)
    if sem is not None:
        d["dimension_semantics"] = sem
    if vmem_mb is not None:
        d["vmem_limit_bytes"] = vmem_mb << 20
    return pltpu.CompilerParams(**d)


def _gelu(x):
    z = 0.7978845608028654 * (x + 0.044715 * (x * x * x))
    return 0.5 * x * (1.0 + jnp.tanh(z))


def _gelu_with_grad(x):
    x2 = x * x
    t = jnp.tanh(0.7978845608028654 * (x + 0.044715 * (x2 * x)))
    half = 0.5 * (1.0 + t)
    return x * half, half + 0.5 * x * (1.0 - t * t) * (0.7978845608028654 * (1.0 + 3.0 * 0.044715 * x2))


def _matmul(a, b, *, name, ta=False, tb=False, M, N, K, tm=None, tn=None, tk=None, outs, epi=None,
            extras=(), a_spec=None, b_spec=None, out_specs=None, after=(), n_split=None, slots=None, into=()):
    n_after = len(after)
    tm = _pick(M, TM if tm is None else tm, 8)
    tn = _pick(N, TN if tn is None else tn)
    tk = _pick(K, TK if tk is None else tk)
    if n_split is None:
        n_split = tn // N_SUB if tn % N_SUB == 0 else 1
    nk = K // tk
    grid = (N // tn, M // tm, nk)
    if a_spec is None:
        a_spec = (pl.BlockSpec((tk, tm), lambda j, i, k, *s: (k, i)) if ta
                  else pl.BlockSpec((tm, tk), lambda j, i, k, *s: (i, k)))
    else:
        a_spec = a_spec(tm, tk)
    if b_spec is None:
        b_spec = (pl.BlockSpec((tn, tk), lambda j, i, k, *s: (j, k)) if tb
                  else pl.BlockSpec((tk, tn), lambda j, i, k, *s: (k, j)))
    else:
        b_spec = b_spec(tn, tk)
    if out_specs is None:
        out_specs = [pl.BlockSpec((tm, tn), lambda j, i, k, *s: (i, j)) for _ in outs]
    else:
        out_specs = out_specs(tm, tn)
    dn = (((0 if ta else 1,), (1 if tb else 0,)), ((), ()))
    n_ex, n_out = len(extras), len(outs)
    n_pre = 0 if slots is None else 1
    n_into = len(into)
    ns = tn // n_split

    def body(*refs):
        a_ref, b_ref = refs[n_pre], refs[n_pre + 1]
        ex = refs[n_pre + 2:n_pre + 2 + n_ex]
        first_out = n_pre + 2 + n_ex + n_after + n_into
        o = refs[first_out:first_out + n_out]
        acc = refs[first_out + n_out:]
        k = pl.program_id(2)

        def finish(val, cols):
            res = (val,) if epi is None else epi(val, *[e[:, cols] for e in ex])
            for r, o_ref in zip(res, o):
                o_ref[:, cols] = r.astype(o_ref.dtype)

        if nk > 1:
            @pl.when(k == 0)
            def _():
                acc[0][...] = jnp.zeros_like(acc[0])

        av = a_ref[...].astype(BF16)
        for q in range(n_split):
            cols = slice(q * ns, (q + 1) * ns)
            bq = (b_ref[cols, :] if tb else b_ref[:, cols]).astype(BF16)
            part = lax.dot_general(av, bq, dn, preferred_element_type=F32)
            if nk == 1:
                finish(part, cols)
            else:
                acc[0][:, cols] += part

        if nk > 1:
            @pl.when(k == nk - 1)
            def _():
                finish(acc[0][...], slice(0, tn))

    in_specs = ([a_spec, b_spec] + [sp(tm, tn) for _, sp in extras] + [ANY] * (n_after + n_into))
    scratch = [pltpu.VMEM((tm, tn), F32)] if nk > 1 else []
    args = [a, b] + [arr for arr, _ in extras] + list(after) + list(into)
    aliases = {n_pre + len(args) - n_into + t: t for t in range(n_into)}
    params = _cp(("parallel", "parallel", "arbitrary"), VMEM_MB)
    if slots is None:
        return pl.pallas_call(body, name=name, grid=grid, in_specs=in_specs, out_specs=out_specs, out_shape=outs,
                              scratch_shapes=scratch, input_output_aliases=aliases, compiler_params=params)(*args)
    return pl.pallas_call(
        body, name=name,
        grid_spec=pltpu.PrefetchScalarGridSpec(num_scalar_prefetch=1, grid=grid, in_specs=in_specs,
                                               out_specs=out_specs, scratch_shapes=scratch),
        out_shape=outs, input_output_aliases=aliases, compiler_params=params)(slots, *args)


def _tile_spec():
    return lambda tm, tn: pl.BlockSpec((tm, tn), lambda j, i, k, *s: (i, j))


def _cast_into_slot(w, slot, after, *, name):
    R, C = w.shape
    tr = _pick_rows(R, 16)

    def body(s_ref, w_ref, _after_ref, o_ref):
        o_ref[...] = w_ref[...].astype(BF16)

    return pl.pallas_call(
        body, name=name,
        grid_spec=pltpu.PrefetchScalarGridSpec(
            num_scalar_prefetch=1, grid=(R // tr,),
            in_specs=[pl.BlockSpec((tr, C), lambda i, s: (i, 0)), ANY],
            out_specs=pl.BlockSpec((None, tr, C), lambda i, s: (s[0], i, 0))),
        out_shape=jax.ShapeDtypeStruct((NCHIP, R, C), BF16),
        compiler_params=_cp(("parallel",), VMEM_MB),
    )(slot, w, after)


def _rms_fwd(x, g, *, name, after=()):
    R, C = x.shape
    tr = _pick(R, TR_EW, 16)
    n_after = len(after)

    def body(x_ref, g_ref, *rest):
        o_ref = rest[n_after]
        xv = x_ref[...]
        r = lax.rsqrt(jnp.mean(xv * xv, axis=-1, keepdims=True) + EPS)
        o_ref[...] = ((xv * r) * g_ref[...]).astype(BF16)

    return pl.pallas_call(
        body, name=name, grid=(R // tr,),
        in_specs=[pl.BlockSpec((tr, C), lambda i: (i, 0)), pl.BlockSpec((1, C), lambda i: (0, 0))] + [ANY] * n_after,
        out_specs=pl.BlockSpec((tr, C), lambda i: (i, 0)),
        out_shape=jax.ShapeDtypeStruct((R, C), BF16),
        compiler_params=_cp(("parallel",), VMEM_MB),
    )(x, g, *after)


def _rms_bwd(dh, x, g, dres, *, name, want_dx=True, want_bf=True, after=()):
    R, C = x.shape
    tr = _pick(R, TR_EW, 16)
    has_res = dres is not None
    row = pl.BlockSpec((tr, C), lambda i: (i, 0))
    vec = pl.BlockSpec((1, C), lambda i: (0, 0))

    def body(*refs):
        dh_ref, x_ref, g_ref = refs[:3]
        pos = 3
        dres_ref = None
        if has_res:
            dres_ref = refs[pos]
            pos += 1
        outs = refs[pos + len(after):]
        i = pl.program_id(0)
        xv = x_ref[...]
        r = lax.rsqrt(jnp.mean(xv * xv, axis=-1, keepdims=True) + EPS)
        xh = xv * r
        dhv = dh_ref[...]
        dg_ref = outs[-1]
        dgp = jnp.sum(dhv * xh, axis=0, keepdims=True)

        @pl.when(i == 0)
        def _():
            dg_ref[...] = dgp

        @pl.when(i > 0)
        def _():
            dg_ref[...] += dgp

        if want_dx:
            t = dhv * g_ref[...]
            dx = r * (t - xh * jnp.mean(t * xh, axis=-1, keepdims=True))
            if has_res:
                dx = dx + dres_ref[...]
            outs[0][...] = dx
            if want_bf:
                outs[1][...] = dx.astype(BF16)

    in_specs = [row, row, vec] + ([row] if has_res else []) + [ANY] * len(after)
    out_specs, out_shape = [], []
    if want_dx:
        out_specs.append(row)
        out_shape.append(jax.ShapeDtypeStruct((R, C), F32))
        if want_bf:
            out_specs.append(row)
            out_shape.append(jax.ShapeDtypeStruct((R, C), BF16))
    out_specs.append(vec)
    out_shape.append(jax.ShapeDtypeStruct((1, C), F32))
    args = [dh, x, g] + ([dres] if has_res else []) + list(after)
    return pl.pallas_call(
        body, name=name, grid=(R // tr,), in_specs=in_specs, out_specs=out_specs, out_shape=out_shape,
        compiler_params=_cp(("arbitrary",), VMEM_MB),
    )(*args)


def _loss_bwd(x3, g, tgt, *, name):
    R, C = x3.shape
    tr = _pick(R, TR_EW, 16)
    n = R // tr
    row = pl.BlockSpec((tr, C), lambda i: (i, 0))
    vec = pl.BlockSpec((1, C), lambda i: (0, 0))

    def body(x_ref, g_ref, t_ref, dx_ref, dxb_ref, dg_ref, loss_ref, acc_ref):
        i = pl.program_id(0)
        xv = x_ref[...]
        gv = g_ref[...]
        r = lax.rsqrt(jnp.mean(xv * xv, axis=-1, keepdims=True) + EPS)
        xh = xv * r
        e = xh * gv - t_ref[...]
        dy = e * (1.0 / C)
        sq = jnp.sum(e * e, axis=0, keepdims=True)
        dgp = jnp.sum(dy * xh, axis=0, keepdims=True)

        @pl.when(i == 0)
        def _():
            acc_ref[...] = sq
            dg_ref[...] = dgp

        @pl.when(i > 0)
        def _():
            acc_ref[...] += sq
            dg_ref[...] += dgp

        t = dy * gv
        dx = r * (t - xh * jnp.mean(t * xh, axis=-1, keepdims=True))
        dx_ref[...] = dx
        dxb_ref[...] = dx.astype(BF16)

        @pl.when(i == n - 1)
        def _():
            loss_ref[...] = jnp.sum(acc_ref[...], axis=-1, keepdims=True) * (0.5 / C)

    return pl.pallas_call(
        body, name=name, grid=(n,),
        in_specs=[row, vec, row],
        out_specs=[row, row, vec, pl.BlockSpec((1, 1), lambda i: (0, 0))],
        out_shape=[jax.ShapeDtypeStruct((R, C), F32), jax.ShapeDtypeStruct((R, C), BF16),
                   jax.ShapeDtypeStruct((1, C), F32), jax.ShapeDtypeStruct((1, 1), F32)],
        scratch_shapes=[pltpu.VMEM((1, C), F32)],
        compiler_params=_cp(("arbitrary",), VMEM_MB),
    )(x3, g, tgt)


def _offsets():
    u0 = 0
    v0 = DS
    b0 = 2 * DS
    c0 = b0 + DC
    x0 = c0 + DC
    q0 = x0 + DC
    return u0, v0, b0, c0, x0, q0


def _tri_mask(lower):
    r = lax.broadcasted_iota(jnp.int32, (CHUNK, CHUNK), 0)
    c = lax.broadcasted_iota(jnp.int32, (CHUNK, CHUNK), 1)
    return (r >= c) if lower else (c >= r)


def _layer_norm_stats(vg):
    mu = jnp.mean(vg, axis=-1, keepdims=True)
    vc = vg - mu
    rstd = lax.rsqrt(jnp.mean(vc * vc, axis=-1, keepdims=True) + EPS)
    return vc * rstd, rstd


def _softmax_rows(qh, kh):
    s = lax.dot_general(qh, kh, (((1,), (1,)), ((), ())), preferred_element_type=F32)
    m = jnp.max(s, axis=-1, keepdims=True)
    e = jnp.exp(s - m)
    return e / jnp.sum(e, axis=-1, keepdims=True)


def _mix_fwd(proj, kv, w_s, bs_t, ln_g, ln_b, conv_w, g_head, *, name):
    assert DS == DC
    tr = _pick(S, TR_MIX, CHUNK)
    n = S // tr
    nck = tr // CHUNK
    u0, v0, b0, c0, x0, q0 = _offsets()
    hb = tr // HALO

    def body(p_ref, cprev_ref, xprev_ref, kv_ref, ws_ref, bst_ref, lng_ref, lnb_ref, cw_ref, gh_ref,
             heads_ref, hn_ref, ycv_ref, buf_ref):
        i = pl.program_id(0)

        def emit(col, val):
            rs = lax.rsqrt(jnp.mean(val * val, axis=-1, keepdims=True) + EPS)
            heads_ref[:, col:col + HD] = val
            hn_ref[:, col:col + HD] = ((val * rs) * gh_ref[:, col:col + HD]).astype(BF16)

        vhat, _ = _layer_norm_stats(_gelu(p_ref[:, v0:v0 + DS]))
        vnb = (vhat * lng_ref[...] + lnb_ref[...]).astype(BF16)
        low = _tri_mask(True)
        for h in range(NSH):
            wt = jnp.where(low, ws_ref[h], 0.0).astype(BF16)
            bcol = bst_ref[:, h:h + 1]
            parts = []
            for c in range(nck):
                blk = vnb[c * CHUNK:(c + 1) * CHUNK, h * HD:(h + 1) * HD]
                parts.append(jnp.dot(wt, blk, preferred_element_type=F32) + bcol)
            mixed = parts[0] if nck == 1 else jnp.concatenate(parts, axis=0)
            emit(h * HD, _gelu(p_ref[:, u0 + h * HD:u0 + (h + 1) * HD]) * mixed)

        xc = p_ref[:, c0:c0 + DC] * p_ref[:, x0:x0 + DC]
        prev = cprev_ref[...] * xprev_ref[...]
        buf_ref[0:HALO, :] = jnp.where(i > 0, prev, 0.0)
        buf_ref[HALO:HALO + tr, :] = xc
        y = (cw_ref[2:3, :] * xc + cw_ref[1:2, :] * buf_ref[HALO - 1:HALO - 1 + tr, :]
             + cw_ref[0:1, :] * buf_ref[HALO - 2:HALO - 2 + tr, :])
        ycv_ref[...] = y
        cout = p_ref[:, b0:b0 + DC] * y
        for h in range(NCH):
            emit(DS + h * HD, cout[:, h * HD:(h + 1) * HD])

        for h in range(NMH):
            qh = (p_ref[:, q0 + h * HD:q0 + (h + 1) * HD] * SCALE).astype(BF16)
            kh = kv_ref[:, h * HD:(h + 1) * HD].astype(BF16)
            vh = kv_ref[:, DM + h * HD:DM + (h + 1) * HD].astype(BF16)
            p = _softmax_rows(qh, kh)
            emit(DS + DC + h * HD, jnp.dot(p.astype(BF16), vh, preferred_element_type=F32))

    full = lambda shape: pl.BlockSpec(shape, lambda i: (0,) * len(shape))
    halo_c = pl.BlockSpec((HALO, DC), lambda i: (jnp.maximum(i * hb - 1, 0), c0 // DC))
    halo_x = pl.BlockSpec((HALO, DC), lambda i: (jnp.maximum(i * hb - 1, 0), x0 // DC))
    return pl.pallas_call(
        body, name=name, grid=(n,),
        in_specs=[pl.BlockSpec((tr, DIN), lambda i: (i, 0)), halo_c, halo_x,
                  full((NMEM, 2 * DM)), full((NSH, CHUNK, CHUNK)), full((CHUNK, NSH)),
                  full((1, DS)), full((1, DS)), full((3, DC)), full((1, D))],
        out_specs=[pl.BlockSpec((tr, D), lambda i: (i, 0)), pl.BlockSpec((tr, D), lambda i: (i, 0)),
                   pl.BlockSpec((tr, DC), lambda i: (i, 0))],
        out_shape=[jax.ShapeDtypeStruct((S, D), F32), jax.ShapeDtypeStruct((S, D), BF16),
                   jax.ShapeDtypeStruct((S, DC), F32)],
        scratch_shapes=[pltpu.VMEM((tr + HALO, DC), F32)],
        compiler_params=_cp(("parallel",), VMEM_MB),
    )(proj, proj, proj, kv, w_s, bs_t, ln_g, ln_b, conv_w, g_head)


def _mix_bwd(dhn, heads, proj, ycv, kv, w_s, bs_t, ln_g, ln_b, conv_w, g_head, after, *, name):
    assert DS == DC
    tr = _pick(S, TR_MIX, CHUNK)
    n = S // tr
    nck = tr // CHUNK
    u0, v0, b0, c0, x0, q0 = _offsets()
    hb = tr // HALO
    last_hb = S // HALO - 1

    def body(dhn_ref, heads_ref, p_ref, ycv_ref, dhn_nx_ref, heads_nx_ref, b_nx_ref, kv_ref, ws_ref, bst_ref,
             lng_ref, lnb_ref, cw_ref, gh_ref, _after_ref,
             dp_ref, dkv_ref, dws_ref, dbs_ref, dlng_ref, dlnb_ref, dcw_ref, dgh_ref, buf_ref, dvn_ref):
        i = pl.program_id(0)

        @pl.when(i == 0)
        def _():
            dkv_ref[...] = jnp.zeros_like(dkv_ref)
            dws_ref[...] = jnp.zeros_like(dws_ref)
            dbs_ref[...] = jnp.zeros_like(dbs_ref)
            dlng_ref[...] = jnp.zeros_like(dlng_ref)
            dlnb_ref[...] = jnp.zeros_like(dlnb_ref)
            dcw_ref[...] = jnp.zeros_like(dcw_ref)
            dgh_ref[...] = jnp.zeros_like(dgh_ref)

        def head_bwd(a, dn, gh):
            rs = lax.rsqrt(jnp.mean(a * a, axis=-1, keepdims=True) + EPS)
            ah = a * rs
            t = dn * gh
            return rs * (t - ah * jnp.mean(t * ah, axis=-1, keepdims=True)), jnp.sum(dn * ah, axis=0, keepdims=True)

        def head_grad(col):
            da, dg = head_bwd(heads_ref[:, col:col + HD], dhn_ref[:, col:col + HD], gh_ref[:, col:col + HD])
            dgh_ref[:, col:col + HD] += dg
            return da

        vg, dvg_dv = _gelu_with_grad(p_ref[:, v0:v0 + DS])
        vhat, rstd = _layer_norm_stats(vg)
        vnb = (vhat * lng_ref[...] + lnb_ref[...]).astype(BF16)
        low = _tri_mask(True)
        ones = jnp.ones((HALO, HD), BF16)
        for h in range(NSH):
            w_h = ws_ref[h]
            wt = jnp.where(low, w_h, 0.0).astype(BF16)
            bcol = bst_ref[:, h:h + 1]
            da = head_grad(h * HD)
            ug, dug_du = _gelu_with_grad(p_ref[:, u0 + h * HD:u0 + (h + 1) * HD])
            dws = jnp.zeros((CHUNK, CHUNK), F32)
            dbs = jnp.zeros((HALO, CHUNK), F32)
            mixed_parts = []
            for c in range(nck):
                rows = slice(c * CHUNK, (c + 1) * CHUNK)
                blk = vnb[rows, h * HD:(h + 1) * HD]
                mixed_parts.append(jnp.dot(wt, blk, preferred_element_type=F32) + bcol)
                dmb = (da[rows] * ug[rows]).astype(BF16)
                dws = dws + lax.dot_general(dmb, blk, (((1,), (1,)), ((), ())), preferred_element_type=F32)
                dbs = dbs + lax.dot_general(ones, dmb, (((1,), (1,)), ((), ())), preferred_element_type=F32)
                dvn_ref[c * CHUNK:(c + 1) * CHUNK, h * HD:(h + 1) * HD] = lax.dot_general(
                    wt, dmb, (((0,), (0,)), ((), ())), preferred_element_type=F32)
            mixed = mixed_parts[0] if nck == 1 else jnp.concatenate(mixed_parts, axis=0)
            dp_ref[:, u0 + h * HD:u0 + (h + 1) * HD] = ((da * mixed) * dug_du).astype(BF16)
            dws_ref[h] += jnp.where(low, dws, 0.0)
            dbs_ref[h] += dbs
        dvn = dvn_ref[...]
        dlng_ref[...] += jnp.sum(dvn * vhat, axis=0, keepdims=True)
        dlnb_ref[...] += jnp.sum(dvn, axis=0, keepdims=True)
        dvh = dvn * lng_ref[...]
        dvg = rstd * (dvh - jnp.mean(dvh, axis=-1, keepdims=True)
                      - vhat * jnp.mean(dvh * vhat, axis=-1, keepdims=True))
        dp_ref[:, v0:v0 + DS] = (dvg * dvg_dv).astype(BF16)

        dc = jnp.concatenate([head_grad(DS + h * HD) for h in range(NCH)], axis=1)
        dc_nx = jnp.concatenate(
            [head_bwd(heads_nx_ref[:, h * HD:(h + 1) * HD], dhn_nx_ref[:, h * HD:(h + 1) * HD],
                      gh_ref[:, DS + h * HD:DS + (h + 1) * HD])[0] for h in range(NCH)], axis=1)
        bg = p_ref[:, b0:b0 + DC]
        cg = p_ref[:, c0:c0 + DC]
        xin = p_ref[:, x0:x0 + DC]
        dp_ref[:, b0:b0 + DC] = (dc * ycv_ref[...]).astype(BF16)
        dyv = dc * bg
        buf_ref[0:tr, :] = dyv
        buf_ref[tr:tr + HALO, :] = jnp.where(i < n - 1, dc_nx * b_nx_ref[...], 0.0)
        sh1 = buf_ref[1:1 + tr, :]
        sh0 = buf_ref[2:2 + tr, :]
        dxc = cw_ref[2:3, :] * dyv + cw_ref[1:2, :] * sh1 + cw_ref[0:1, :] * sh0
        xc = cg * xin
        dp_ref[:, c0:c0 + DC] = (dxc * xin).astype(BF16)
        dp_ref[:, x0:x0 + DC] = (dxc * cg).astype(BF16)
        dcw_ref[0:1, :] += jnp.sum(sh0 * xc, axis=0, keepdims=True)
        dcw_ref[1:2, :] += jnp.sum(sh1 * xc, axis=0, keepdims=True)
        dcw_ref[2:3, :] += jnp.sum(dyv * xc, axis=0, keepdims=True)

        for h in range(NMH):
            do = head_grad(DS + DC + h * HD).astype(BF16)
            qh = (p_ref[:, q0 + h * HD:q0 + (h + 1) * HD] * SCALE).astype(BF16)
            kh = kv_ref[:, h * HD:(h + 1) * HD].astype(BF16)
            vh = kv_ref[:, DM + h * HD:DM + (h + 1) * HD].astype(BF16)
            p = _softmax_rows(qh, kh)
            dpr = lax.dot_general(do, vh, (((1,), (1,)), ((), ())), preferred_element_type=F32)
            ds = (p * (dpr - jnp.sum(dpr * p, axis=-1, keepdims=True))).astype(BF16)
            dp_ref[:, q0 + h * HD:q0 + (h + 1) * HD] = (
                jnp.dot(ds, kh, preferred_element_type=F32) * SCALE).astype(BF16)
            dkv_ref[:, h * HD:(h + 1) * HD] += lax.dot_general(
                ds, qh, (((0,), (0,)), ((), ())), preferred_element_type=F32)
            dkv_ref[:, DM + h * HD:DM + (h + 1) * HD] += lax.dot_general(
                p.astype(BF16), do, (((0,), (0,)), ((), ())), preferred_element_type=F32)

    full = lambda shape: pl.BlockSpec(shape, lambda i: (0,) * len(shape))
    row = lambda c: pl.BlockSpec((tr, c), lambda i: (i, 0))
    nxt = lambda col: pl.BlockSpec((HALO, DC), lambda i: (jnp.minimum((i + 1) * hb, last_hb), col))
    return pl.pallas_call(
        body, name=name, grid=(n,),
        in_specs=[row(D), row(D), row(DIN), row(DC), nxt(DS // DC), nxt(DS // DC), nxt(b0 // DC),
                  full((NMEM, 2 * DM)), full((NSH, CHUNK, CHUNK)), full((CHUNK, NSH)),
                  full((1, DS)), full((1, DS)), full((3, DC)), full((1, D)), ANY],
        out_specs=[row(DIN), full((NMEM, 2 * DM)), full((NSH, CHUNK, CHUNK)), full((NSH, HALO, CHUNK)),
                   full((1, DS)), full((1, DS)), full((HALO, DC)), full((1, D))],
        out_shape=[jax.ShapeDtypeStruct((S, DIN), BF16), jax.ShapeDtypeStruct((NMEM, 2 * DM), F32),
                   jax.ShapeDtypeStruct((NSH, CHUNK, CHUNK), F32), jax.ShapeDtypeStruct((NSH, HALO, CHUNK), F32),
                   jax.ShapeDtypeStruct((1, DS), F32), jax.ShapeDtypeStruct((1, DS), F32),
                   jax.ShapeDtypeStruct((HALO, DC), F32), jax.ShapeDtypeStruct((1, D), F32)],
        scratch_shapes=[pltpu.VMEM((tr + HALO, DC), F32), pltpu.VMEM((tr, DS), F32)],
        compiler_params=_cp(("arbitrary",), VMEM_MB),
    )(dhn, heads, proj, ycv, dhn, heads, proj, kv, w_s, bs_t, ln_g, ln_b, conv_w, g_head, after)


def _place():
    x, y, c = lax.axis_index("x"), lax.axis_index("y"), lax.axis_index("c")
    chips = [(1 - x, y), (x, 1 - y), (1 - x, 1 - y)]
    return x, y, c, chips


ANY = pl.BlockSpec(memory_space=pl.ANY)


HBM = pl.BlockSpec(memory_space=pltpu.HBM)
SEM = pl.BlockSpec(memory_space=pltpu.SEMAPHORE)
EFFECT = pltpu.SideEffectType.DATAFLOW_SIDE_EFFECTING
N_PEER_CHIPS = 3
N_NEIGHBOUR_CHIPS = 2
CONV_PAD = (32, 256)


def _in_hbm(a):
    return pltpu.with_memory_space_constraint(a, pltpu.HBM)


def _allgather_start(bufs, forwards, after, collective_id, *, name):
    arrs = list(bufs) + list(forwards)
    nw, nb = len(arrs), len(bufs)

    def body(*refs):
        ins, send, recv = refs[:nw], refs[nw + 1:2 * nw + 1], refs[2 * nw + 1:3 * nw + 1]
        token = refs[4 * nw + 1]
        x, y, c, chips = _place()
        s = 2 * x + y
        slots = [2 * cx + cy for cx, cy in chips]
        _handshake([(cx, cy, c) for cx, cy in chips[:N_NEIGHBOUR_CHIPS]])
        for w in range(nb, nw):
            q = arrs[w].shape[1] // 4
            for j in range(N_NEIGHBOUR_CHIPS):
                rows = ins[w].at[slots[j], pl.ds(c * 2 * q + j * q, q)]
                pltpu.make_async_remote_copy(src_ref=rows, dst_ref=rows, send_sem=send[w], recv_sem=recv[w],
                                             device_id=(*chips[1 - j], c), device_id_type=MESH).start()
        for w in range(nb):
            hr = arrs[w].shape[1] // 2
            rows = ins[w].at[s, pl.ds(c * hr, hr)]
            for cx, cy in chips[:N_NEIGHBOUR_CHIPS]:
                pltpu.make_async_remote_copy(src_ref=rows, dst_ref=rows, send_sem=send[w], recv_sem=recv[w],
                                             device_id=(cx, cy, c), device_id_type=MESH).start()
        token[...] = jnp.zeros_like(token)

    res = pl.pallas_call(
        body, name=name,
        in_specs=[HBM] * nw + [ANY],
        out_specs=[SEM] * (2 * nw) + [HBM] * nw + [pl.BlockSpec(memory_space=pltpu.VMEM)],
        out_shape=[pltpu.SemaphoreType.DMA(())] * (2 * nw) + [pltpu.HBM(a.shape, a.dtype) for a in arrs]
        + [jax.ShapeDtypeStruct((8, 128), F32)],
        input_output_aliases={w: 2 * nw + w for w in range(nw)},
        compiler_params=pltpu.CompilerParams(has_side_effects=EFFECT, collective_id=collective_id),
    )(*[_in_hbm(a) for a in arrs], after)
    return res[:nw], res[nw:2 * nw], res[2 * nw:3 * nw], res[3 * nw]


def _handshake(peers):
    barrier = pltpu.get_barrier_semaphore()
    for peer in peers:
        pl.semaphore_signal(barrier, inc=1, device_id=peer, device_id_type=MESH)
    pl.semaphore_wait(barrier, len(peers))


def _scatter_start(parts, bufs, collective_id, *, name):
    nw = len(parts)

    def body(*refs):
        src, dst = refs[:nw], refs[nw:2 * nw]
        send, recv = refs[2 * nw:3 * nw], refs[3 * nw:4 * nw]
        token = refs[6 * nw]
        x, y, c, chips = _place()
        s = 2 * x + y
        _handshake([(cx, cy, c) for cx, cy in chips])
        for w in range(nw):
            for cx, cy in chips:
                pltpu.make_async_remote_copy(src_ref=src[w].at[2 * cx + cy], dst_ref=dst[w].at[s], send_sem=send[w],
                                             recv_sem=recv[w], device_id=(cx, cy, c), device_id_type=MESH).start()
        token[...] = jnp.zeros_like(token)

    res = pl.pallas_call(
        body, name=name,
        in_specs=[HBM] * (2 * nw),
        out_specs=[SEM] * (2 * nw) + [HBM] * (2 * nw) + [pl.BlockSpec(memory_space=pltpu.VMEM)],
        out_shape=[pltpu.SemaphoreType.DMA(())] * (2 * nw) + [pltpu.HBM(a.shape, a.dtype) for a in parts + bufs]
        + [jax.ShapeDtypeStruct((8, 128), F32)],
        input_output_aliases={k: 2 * nw + k for k in range(2 * nw)},
        compiler_params=pltpu.CompilerParams(has_side_effects=EFFECT, collective_id=collective_id),
    )(*[_in_hbm(a) for a in parts + bufs])
    return res[:nw], res[nw:2 * nw], res[2 * nw:3 * nw], res[3 * nw:4 * nw], res[4 * nw]


def _sibling_start(srcs, whole, collective_id, *, name):
    nw = len(srcs)
    lands = [lax.empty((a.shape[0], a.shape[1] if whole else a.shape[1] // 2, a.shape[2]), a.dtype) for a in srcs]

    def body(*refs):
        src, land = refs[:nw], refs[nw:2 * nw]
        send, recv = refs[2 * nw:3 * nw], refs[3 * nw:4 * nw]
        token = refs[6 * nw]
        x, y, c, _ = _place()
        _handshake([(x, y, 1 - c)])
        for w in range(nw):
            hr = srcs[w].shape[1] // 2
            rows = src[w] if whole else src[w].at[:, pl.ds((1 - c) * hr, hr)]
            pltpu.make_async_remote_copy(src_ref=rows, dst_ref=land[w], send_sem=send[w], recv_sem=recv[w],
                                         device_id=(x, y, 1 - c), device_id_type=MESH).start()
        token[...] = jnp.zeros_like(token)

    res = pl.pallas_call(
        body, name=name,
        in_specs=[HBM] * (2 * nw),
        out_specs=[SEM] * (2 * nw) + [HBM] * (2 * nw) + [pl.BlockSpec(memory_space=pltpu.VMEM)],
        out_shape=[pltpu.SemaphoreType.DMA(())] * (2 * nw) + [pltpu.HBM(a.shape, a.dtype) for a in srcs + lands]
        + [jax.ShapeDtypeStruct((8, 128), F32)],
        input_output_aliases={k: 2 * nw + k for k in range(2 * nw)},
        compiler_params=pltpu.CompilerParams(has_side_effects=EFFECT, collective_id=collective_id),
    )(*[_in_hbm(a) for a in srcs + lands])
    return res[:nw], res[nw:2 * nw], res[2 * nw:3 * nw], res[3 * nw:4 * nw], res[4 * nw]


def _transfer_wait(sends, recvs, thru, sizes, after, *, name):
    n = len(sends)
    flat = [a for group in thru for a in group]

    def body(*refs):
        bufs = refs[:len(flat)]
        send = refs[len(flat):len(flat) + n]
        recv = refs[len(flat) + n:len(flat) + 2 * n]
        token = refs[2 * len(flat) + 2 * n + 1]
        token[...] = jnp.zeros_like(token)
        x, y, c, _ = _place()
        pos = 0
        for k in range(n):
            slots, rows = sizes[k]
            region = bufs[pos].at[pl.ds(0, slots), pl.ds(0, rows)]
            pos += len(thru[k])
            cp = pltpu.make_async_remote_copy(src_ref=region, dst_ref=region, send_sem=send[k], recv_sem=recv[k],
                                              device_id=(x, y, 1 - c), device_id_type=MESH)
            cp.wait_send()
            cp.wait_recv()

    res = pl.pallas_call(
        body, name=name,
        in_specs=[HBM] * len(flat) + [SEM] * (2 * n) + [pl.BlockSpec(memory_space=pl.ANY)],
        out_specs=[HBM] * len(flat) + [pl.BlockSpec(memory_space=pltpu.VMEM)],
        out_shape=[pltpu.HBM(a.shape, a.dtype) for a in flat] + [jax.ShapeDtypeStruct((8, 128), F32)],
        input_output_aliases={k: k for k in range(len(flat))},
        compiler_params=pltpu.CompilerParams(has_side_effects=EFFECT),
    )(*flat, *sends, *recvs, after)
    out, pos = [], 0
    for group in thru:
        out.append(res[pos:pos + len(group)])
        pos += len(group)
    return out, res[len(flat)]


def _forward_gathered(bufs, after, *, name):
    nw = len(bufs)

    def body(*refs):
        outs = refs[nw + 1:2 * nw + 1]
        d_send, d_recv, i_send, i_recv = refs[2 * nw + 1:]
        x, y, c, chips = _place()
        me, sibling = (x, y, c), (x, y, 1 - c)
        slots = [2 * cx + cy for cx, cy in chips]

        def rows(w, j, start, n):
            return outs[w].at[slots[j], pl.ds(start, n)]

        def d2d(w, j, which, to):
            hr = bufs[w].shape[1] // 2
            r = rows(w, j, which * hr, hr)
            return pltpu.make_async_remote_copy(
                src_ref=r, dst_ref=r, send_sem=d_send.at[N_PEER_CHIPS * w + j],
                recv_sem=d_recv.at[N_PEER_CHIPS * w + j], device_id=to, device_id_type=MESH)

        def ici(w, j, slot_j, to):
            q = bufs[w].shape[1] // 4
            r = rows(w, slot_j, c * 2 * q + j * q, q)
            return pltpu.make_async_remote_copy(
                src_ref=r, dst_ref=r, send_sem=i_send.at[N_NEIGHBOUR_CHIPS * w + j],
                recv_sem=i_recv.at[N_NEIGHBOUR_CHIPS * w + j], device_id=to, device_id_type=MESH)

        started = []
        for w in range(nw):
            started += [ici(w, 0, 0, (*chips[1], c)), ici(w, 1, 1, (*chips[0], c))]
            started += [d2d(w, j, c, sibling) for j in range(N_NEIGHBOUR_CHIPS)]
        for cp in started:
            cp.start()
        diag = N_PEER_CHIPS - 1
        for w in range(nw):
            for j in range(N_NEIGHBOUR_CHIPS):
                ici(w, j, diag, me).wait_recv()
            cp = d2d(w, diag, c, sibling)
            cp.start()
            started.append(cp)
        for w in range(nw):
            for j in range(N_PEER_CHIPS):
                d2d(w, j, 1 - c, me).wait_recv()
        for cp in started:
            cp.wait_send()

    return pl.pallas_call(
        body, name=name,
        in_specs=[ANY] * (nw + 1), out_specs=[ANY] * nw,
        out_shape=[jax.ShapeDtypeStruct(a.shape, a.dtype) for a in bufs],
        input_output_aliases={w: w for w in range(nw)},
        scratch_shapes=[pltpu.SemaphoreType.DMA((N_PEER_CHIPS * nw,)), pltpu.SemaphoreType.DMA((N_PEER_CHIPS * nw,)),
                        pltpu.SemaphoreType.DMA((N_NEIGHBOUR_CHIPS * nw,)),
                        pltpu.SemaphoreType.DMA((N_NEIGHBOUR_CHIPS * nw,))],
    )(*bufs, after)


def _forward_halves(bufs, which, after, *, name):
    nw = len(bufs)
    n = len(which)

    def body(*refs):
        outs = refs[nw + 1:2 * nw + 1]
        send, recv = refs[2 * nw + 1:]
        x, y, c, chips = _place()
        me, sibling = (x, y, c), (x, y, 1 - c)

        def d2d(w, t, half, to):
            cx, cy = chips[which[t]]
            hr = bufs[w].shape[1] // 2
            rows = outs[w].at[2 * cx + cy, pl.ds(half * hr, hr)]
            return pltpu.make_async_remote_copy(src_ref=rows, dst_ref=rows, send_sem=send.at[n * w + t],
                                                recv_sem=recv.at[n * w + t], device_id=to, device_id_type=MESH)

        passed = [d2d(w, t, c, sibling) for w in range(nw) for t in range(n)]
        for cp in passed:
            cp.start()
        for w in range(nw):
            for t in range(n):
                d2d(w, t, 1 - c, me).wait_recv()
        for cp in passed:
            cp.wait_send()

    return pl.pallas_call(
        body, name=name,
        in_specs=[ANY] * (nw + 1), out_specs=[ANY] * nw,
        out_shape=[jax.ShapeDtypeStruct(a.shape, a.dtype) for a in bufs],
        input_output_aliases={w: w for w in range(nw)},
        scratch_shapes=[pltpu.SemaphoreType.DMA((n * nw,)), pltpu.SemaphoreType.DMA((n * nw,))],
    )(*bufs, after)


def _allreduce_small(p, after, *, name):
    R = p.shape[0]
    hr = R // 2

    def body(p_ref, _after_ref, out_ref, sib_ref, sum_ref, gat_ref, tot_ref, send, recv):
        x, y, c, chips = _place()
        s = 2 * x + y
        sibling = (x, y, 1 - c)
        rows = pl.ds(pl.multiple_of(c * hr, 8), hr)
        swap = pltpu.make_async_remote_copy(src_ref=p_ref, dst_ref=sib_ref, send_sem=send.at[0], recv_sem=recv.at[0],
                                            device_id=sibling, device_id_type=MESH)
        swap.start()
        swap.wait()
        sum_ref[...] = p_ref[...] + sib_ref[...]
        gat_ref[s] = sum_ref[rows, :]
        cps = [pltpu.make_async_remote_copy(src_ref=sum_ref.at[rows], dst_ref=gat_ref.at[s], send_sem=send.at[1 + j],
                                            recv_sem=recv.at[1 + j], device_id=(cx, cy, c), device_id_type=MESH)
               for j, (cx, cy) in enumerate(chips)]
        for cp in cps:
            cp.start()
        for cp in cps:
            cp.wait()
        tot_ref[...] = ((gat_ref[0] + gat_ref[1]) + gat_ref[2]) + gat_ref[3]
        out_ref[rows, :] = tot_ref[...]
        share = pltpu.make_async_remote_copy(src_ref=tot_ref, dst_ref=out_ref.at[rows], send_sem=send.at[4],
                                             recv_sem=recv.at[4], device_id=sibling, device_id_type=MESH)
        share.start()
        share.wait_send()
        other = out_ref.at[pl.ds(pl.multiple_of((1 - c) * hr, 8), hr)]
        pltpu.make_async_remote_copy(src_ref=other, dst_ref=other, send_sem=send.at[4], recv_sem=recv.at[4],
                                     device_id=(x, y, c), device_id_type=MESH).wait_recv()

    vmem = pl.BlockSpec(memory_space=pltpu.VMEM)
    return pl.pallas_call(
        body, name=name, in_specs=[vmem, ANY], out_specs=vmem,
        out_shape=jax.ShapeDtypeStruct((R, 128), F32),
        scratch_shapes=[pltpu.VMEM((R, 128), F32), pltpu.VMEM((R, 128), F32), pltpu.VMEM((NCHIP, hr, 128), F32),
                        pltpu.VMEM((hr, 128), F32), pltpu.SemaphoreType.DMA((5,)), pltpu.SemaphoreType.DMA((5,))],
    )(p, after)


def _select_half_bf16(g, half, add, slot, *, name):
    _, R, C = g.shape
    hr = R // 2
    tr = _pick_rows(hr, 16)
    nb = hr // tr
    sel = jnp.concatenate([jnp.reshape(half, (1,)).astype(jnp.int32), slot])

    def body(s_ref, g_ref, a_ref, o_ref, own_ref):
        val = (g_ref[...].astype(F32) + a_ref[...].astype(F32)).astype(BF16)
        o_ref[...] = val

        @pl.when(pl.program_id(1) == s_ref[1])
        def _():
            own_ref[...] = val

    g_spec = pl.BlockSpec((None, tr, C), lambda i, j, s: (j, s[0] * nb + i, 0))
    o_spec = pl.BlockSpec((None, tr, C), lambda i, j, s: (j, i, 0))
    own_spec = pl.BlockSpec((None, tr, C), lambda i, j, s: (s[1], i, 0))
    shape = jax.ShapeDtypeStruct((NCHIP, hr, C), BF16)
    return pl.pallas_call(
        body, name=name,
        grid_spec=pltpu.PrefetchScalarGridSpec(
            num_scalar_prefetch=1, grid=(nb, NCHIP), in_specs=[g_spec, o_spec], out_specs=[o_spec, own_spec]),
        out_shape=[shape, shape],
        compiler_params=_cp(("parallel", "arbitrary"), VMEM_MB),
    )(sel, g, add)


def _adamw_math(w, g, m, v):
    m = ADAM_B1 * m + (1.0 - ADAM_B1) * g
    v = ADAM_B2 * v + (1.0 - ADAM_B2) * (g * g)
    m_hat = m / (1.0 - ADAM_B1 ** ADAM_STEP)
    v_hat = v / (1.0 - ADAM_B2 ** ADAM_STEP)
    delta = -ADAM_LR * (m_hat / (jnp.sqrt(v_hat) + ADAM_EPS) + ADAM_WD * w)
    return delta, m, v


def _adamw(w, g_mine, g_sib, m, v, core, *, name):
    R, C = w.shape
    hr = R // 2
    tr = _pick_rows(hr, 16)
    nb = hr // tr
    row = pl.BlockSpec((tr, C), lambda hh, i, c: (hh * nb + i, 0))
    mine = pl.BlockSpec((NCHIP, tr, C), lambda hh, i, c: (0, jnp.where(hh == c[0], i, 0), 0))
    sibs = pl.BlockSpec((NCHIP, tr, C), lambda hh, i, c: (0, jnp.where(hh == c[0], 0, i), 0))

    def slot_sum(ref):
        acc = ref[0].astype(F32) + ref[1].astype(F32)
        for j in range(2, NCHIP):
            acc = acc + ref[j].astype(F32)
        return acc

    def body(c_ref, w_ref, gm_ref, gs_ref, m_ref, v_ref, go_ref, d_ref, mo_ref, vo_ref):
        gv = jnp.where(pl.program_id(0) == c_ref[0], slot_sum(gm_ref), slot_sum(gs_ref))
        d, mn, vn = _adamw_math(w_ref[...], gv, m_ref[...], v_ref[...])
        go_ref[...] = gv
        d_ref[...] = d
        mo_ref[...] = mn
        vo_ref[...] = vn

    return pl.pallas_call(
        body, name=name,
        grid_spec=pltpu.PrefetchScalarGridSpec(
            num_scalar_prefetch=1, grid=(2, nb),
            in_specs=[row, mine, sibs, row, row], out_specs=[row] * 4),
        out_shape=[jax.ShapeDtypeStruct((R, C), F32)] * 4,
        compiler_params=_cp(("parallel", "parallel"), VMEM_MB),
    )(core, w, g_mine, g_sib, m, v)


def _adamw_small(ws, gs, ms, vs, *, name):
    n = len(ws)

    def body(*refs):
        w_r, g_r, m_r, v_r = refs[:n], refs[n:2 * n], refs[2 * n:3 * n], refs[3 * n:4 * n]
        d_r, mo_r, vo_r = refs[4 * n:5 * n], refs[5 * n:6 * n], refs[6 * n:7 * n]
        for k in range(n):
            d, mn, vn = _adamw_math(w_r[k][...], g_r[k][...], m_r[k][...], v_r[k][...])
            d_r[k][...] = d
            mo_r[k][...] = mn
            vo_r[k][...] = vn

    shapes = [jax.ShapeDtypeStruct(w.shape, F32) for w in ws]
    res = pl.pallas_call(body, name=name, out_shape=shapes * 3)(*ws, *gs, *ms, *vs)
    return res[:n], res[n:2 * n], res[2 * n:]


_PACK_ROWS = 8


def _pack(parts):
    rows = []
    for a in parts:
        flat = a.reshape(-1)
        n = -(-flat.shape[0] // (_PACK_ROWS * 128)) * (_PACK_ROWS * 128)
        rows.append(jnp.pad(flat, (0, n - flat.shape[0])).reshape(-1, 128))
    total = sum(r.shape[0] for r in rows)
    if total % 16:
        rows.append(jnp.zeros((16 - total % 16, 128), F32))
    return jnp.concatenate(rows, axis=0)


def _unpack(p, shapes):
    out, r = [], 0
    for shp in shapes:
        n = math.prod(shp)
        nr = -(-n // (_PACK_ROWS * 128)) * _PACK_ROWS
        out.append(p[r:r + nr].reshape(-1)[:n].reshape(shp))
        r += nr
    return out


def kernel(x, mem, g_mix, w_in, ln_v_g, ln_v_b, w_s, b_s, conv_w, g_mem, w_kv, g_head, w_o, g_ffn, w_ffn1, w_ffn2, g_final, loss_target, m_g_mix, m_w_in, m_ln_v_g, m_ln_v_b, m_w_s, m_b_s, m_conv_w, m_g_mem, m_w_kv, m_g_head, m_w_o, m_g_ffn, m_w_ffn1, m_w_ffn2, m_g_final, v_g_mix, v_w_in, v_ln_v_g, v_ln_v_b, v_w_s, v_b_s, v_conv_w, v_g_mem, v_w_kv, v_g_head, v_w_o, v_g_ffn, v_w_ffn1, v_w_ffn2, v_g_final):
    sds = jax.ShapeDtypeStruct
    xi, yi = lax.axis_index("x"), lax.axis_index("y")
    shard = 2 * xi + yi
    x2d, mem2d, tgt = x[0], mem[0], loss_target[0]
    ws3, bs2 = w_s[0], b_s[0]
    g_final2 = g_final.reshape(1, D)
    dff4 = DFF // NCHIP
    din4 = DIN // NCHIP
    dcv4 = DC // NCHIP

    big = [w_in[0].T, w_kv[0], w_o[0], w_ffn1[0], w_ffn2[0]]
    big_names = ["w_in", "w_kv", "w_o", "w_ffn1", "w_ffn2"]
    slot = jnp.reshape(shard, (1,)).astype(jnp.int32)
    core = jnp.reshape(lax.axis_index("c"), (1,)).astype(jnp.int32)
    conv_pad = jnp.pad(conv_w[0], ((0, CONV_PAD[0] - 3), (0, CONV_PAD[1] - dcv4)))
    conv_slots = lax.dynamic_update_slice(jnp.zeros((NCHIP,) + CONV_PAD, F32), conv_pad[None], (shard, 0, 0))

    gather_ids = {"in": 16, "kvo": 17, "ffn1": 18, "ffn2": 19, "ffn2d": 20}

    def gather_start(bufs, after, nm, forwards=()):
        return _allgather_start(bufs, forwards, after, gather_ids[nm], name="ag_start_" + nm)

    def gather_wait(state, idx, after, nm):
        send, recv, bufs, _ = state
        got, token = _transfer_wait([send[k] for k in idx], [recv[k] for k in idx], [[bufs[k]] for k in idx],
                                    [(N_NEIGHBOUR_CHIPS, bufs[k].shape[1] // 2) for k in idx], after, name="ag_wait_" + nm)
        return [g[0] for g in got], token

    cast = lambda k, after: _cast_into_slot(big[k], slot, after, name="cast_" + big_names[k])
    ag_in = gather_start([cast(0, slot), conv_slots], slot, "in")
    bs_t = bs2.T

    h = _rms_fwd(x2d, g_mix, name="rms_mix", after=[ag_in[3]])
    mem_n = _rms_fwd(mem2d, g_mem, name="rms_mem", after=[h])
    kvo_b = [cast(1, mem_n)]
    kvo_b.append(cast(2, kvo_b[0]))
    w1_b = cast(3, kvo_b[1])
    w2_b = cast(4, w1_b)
    got_in, tok = gather_wait(ag_in, [0, 1], w2_b, "in")
    win4, conv4 = _forward_gathered(got_in, tok, name="ag_fwd_in")
    ag_kvo = gather_start(kvo_b, conv4, "kvo")
    w_in_t = win4.reshape(DIN, D)
    conv_full = conv4[:, :3, :dcv4].transpose(1, 0, 2).reshape(3, DC)
    NEAR, FAR = [0, 1], [2]

    def diagonal_wait(state, ks, after, nm):
        send, recv, bufs, _ = state
        got, token = _transfer_wait([send[k] for k in ks], [recv[k] for k in ks], [[bufs[k]] for k in ks],
                                    [(1, bufs[k].shape[1] // 2) for k in ks], after, name="ag_waitd_" + nm)
        return [g[0] for g in got], token

    (proj,) = _matmul(h, w_in_t, name="mm_proj", tb=True, M=S, N=DIN, K=D, tn=DIN // 2, outs=[sds((S, DIN), F32)],
                      after=[ag_kvo[3]])
    got_kvo, tok = gather_wait(ag_kvo, [0, 1], proj, "kvo")
    ag_w1 = gather_start([w1_b], tok, "ffn1", forwards=got_kvo)
    kvo_n = _forward_halves(ag_w1[2][1:], NEAR, ag_w1[3], name="ag_fwdn_kvo")
    ag_w1 = (ag_w1[0], ag_w1[1], [ag_w1[2][0]] + list(kvo_n), ag_w1[3])
    kvo_d, tok = diagonal_wait(ag_w1, [1, 2], ag_w1[3], "kvo")
    wkv4, wo4 = _forward_halves(kvo_d, FAR, tok, name="ag_fwdd_kvo")
    w_kv_full = wkv4.reshape(D, 2 * DM)
    w_o_full = wo4.reshape(D, D)
    (kv,) = _matmul(mem_n, w_kv_full, name="mm_kv", M=NMEM, N=2 * DM, K=D, outs=[sds((NMEM, 2 * DM), F32)])
    heads, hn, ycv = _mix_fwd(proj, kv, ws3, bs_t, ln_v_g, ln_v_b, conv_full, g_head, name="mix_fwd")
    def residual_and_norm(acc, res, g):
        x2v = acc + res
        r = lax.rsqrt(jnp.mean(x2v * x2v, axis=-1, keepdims=True) + EPS)
        return x2v, (x2v * r) * g

    row_vec = lambda tm, tn: pl.BlockSpec((1, tn), lambda j, i, k, *s: (0, j))
    x2, h2 = _matmul(hn, w_o_full, name="mm_wo", M=S, N=D, K=D, tn=D, n_split=1, epi=residual_and_norm,
                     outs=[sds((S, D), F32), sds((S, D), BF16)], extras=[(x2d, _tile_spec()), (g_ffn, row_vec)])
    near = jnp.stack([shard, 2 * (1 - xi) + yi, 2 * xi + (1 - yi)]).astype(jnp.int32)
    far = jnp.reshape(2 * (1 - xi) + (1 - yi), (1,)).astype(jnp.int32)

    w1_shard = lambda tn, tk: pl.BlockSpec((None, tk, tn), lambda j, i, k, s: (s[j], k, 0))
    act_cols = lambda tm, tn: [pl.BlockSpec((tm, tn), lambda j, i, k, s: (i, s[j]))] * 2

    def relu2(acc):
        r = jnp.maximum(acc, 0.0)
        return r * r, 2.0 * r

    got_w1, tok = gather_wait(ag_w1, [0], h2, "ffn1")
    ag_w2 = gather_start([w2_b], tok, "ffn2", forwards=got_w1)
    (w1n,) = _forward_halves([ag_w2[2][1]], NEAR, ag_w2[3], name="ag_fwdn_ffn1")
    ag_w2 = (ag_w2[0], ag_w2[1], [ag_w2[2][0], w1n], ag_w2[3])
    act, dact_df = _matmul(h2, w1n, name="mm_ffn1_near", M=S, N=3 * dff4, K=D, tn=dff4, b_spec=w1_shard,
                           out_specs=act_cols, outs=[sds((S, DFF), BF16)] * 2, epi=relu2, slots=near)
    w1d, tok = diagonal_wait(ag_w2, [1], act, "ffn1")
    (w14,) = _forward_halves(w1d, FAR, tok, name="ag_fwdd_ffn1")
    act, dact_df = _matmul(h2, w14, name="mm_ffn1_far", M=S, N=dff4, K=D, tn=dff4, b_spec=w1_shard,
                           out_specs=act_cols, outs=[sds((S, DFF), BF16)] * 2, epi=relu2, slots=far,
                           into=[act, dact_df])

    act_shard = lambda tm, tk: pl.BlockSpec((tm, tk), lambda j, i, k, s: (i, s[k]))
    w2_shard = lambda tn, tk: pl.BlockSpec((None, tk, tn), lambda j, i, k, s: (s[k], 0, j))
    got_w2, tok = gather_wait(ag_w2, [0], act, "ffn2")
    ag_w2d = gather_start([], tok, "ffn2d", forwards=got_w2)
    (w2n,) = _forward_halves(ag_w2d[2], NEAR, ag_w2d[3], name="ag_fwdn_ffn2")
    ag_w2d = (ag_w2d[0], ag_w2d[1], [w2n], ag_w2d[3])
    (x3,) = _matmul(act, w2n, name="mm_ffn2_near", M=S, N=D, K=3 * dff4, tm=2 * TM, tk=dff4,
                    a_spec=act_shard, b_spec=w2_shard, outs=[sds((S, D), F32)], epi=lambda acc, res: (acc + res,),
                    extras=[(x2, _tile_spec())], slots=near)
    w2d, tok = diagonal_wait(ag_w2d, [0], x3, "ffn2")
    (w24,) = _forward_halves(w2d, FAR, tok, name="ag_fwdd_ffn2")
    (x3,) = _matmul(act, w24, name="mm_ffn2_far", M=S, N=D, K=dff4, tm=2 * TM, tk=dff4, a_spec=act_shard,
                    b_spec=w2_shard, outs=[sds((S, D), F32)], epi=lambda acc, res: (acc + res,),
                    extras=[(x3, _tile_spec())], slots=far)
    w2_full = w24.reshape(DFF, D)

    ci = lax.axis_index("c")

    def rs_sibling(g4, nm):
        return _sibling_start([g4], False, 1 + big_names.index(nm), name="rs_sib_" + nm)

    def rs_chips(state, after, nm):
        send, recv, g4, land, _ = state
        (((land_, g4_),), _) = _transfer_wait(send, recv, [[land[0], g4[0]]], [(NCHIP, land[0].shape[1])], after,
                                             name="rs_sibwait_" + nm)
        part, buf = _select_half_bf16(g4_, ci, land_, slot, name="rs_add_" + nm)
        return _scatter_start([part], [buf], 1 + 2 * len(big_names) + big_names.index(nm), name="rs_start_" + nm)

    def rs_end(state, after, nm):
        send, recv, parts, bufs, _ = state
        (((buf, _),), _) = _transfer_wait(send, recv, [[bufs[0], parts[0]]], [(N_PEER_CHIPS, bufs[0].shape[1])], after,
                                          name="rs_wait_" + nm)
        return _sibling_start([buf], True, 1 + len(big_names) + big_names.index(nm), name="rs_share_" + nm)

    big_m = [m_w_in[0].T, m_w_kv[0], m_w_o[0], m_w_ffn1[0], m_w_ffn2[0]]
    big_v = [v_w_in[0].T, v_w_kv[0], v_w_o[0], v_w_ffn1[0], v_w_ffn2[0]]
    big_out = {}

    def rs_finish(k, state, after):
        send, recv, mine, land, _ = state
        nm = big_names[k]
        (((land_, mine_),), _) = _transfer_wait(send, recv, [[land[0], mine[0]]], [(NCHIP, land[0].shape[1])], after,
                                               name="rs_sharewait_" + nm)
        big_out[nm] = _adamw(big[k], mine_, land_, big_m[k], big_v[k], core, name="adamw_" + nm)
        return big_out[nm][1]

    dx3, dx3b, dg_final, loss11 = _loss_bwd(x3, g_final2, tgt, name="loss_bwd")
    (dw2,) = _matmul(act, dx3b, name="mm_dw2", ta=True, M=DFF, N=D, K=S, tn=D, outs=[sds((DFF, D), BF16)])
    sib_w2 = rs_sibling(dw2.reshape(NCHIP, dff4, D), "w_ffn2")
    (dfb,) = _matmul(dx3b, w2_full, name="mm_dact", tb=True, M=S, N=DFF, K=D, tn=dff4, outs=[sds((S, DFF), BF16)],
                     epi=lambda acc, g: (acc * g.astype(F32),), extras=[(dact_df, _tile_spec())],
                     after=[sib_w2[4]])
    rs_w2 = rs_chips(sib_w2, dfb, "w_ffn2")

    def dw1_out(tm, tn):
        nb = dff4 // tn
        return [pl.BlockSpec((None, tm, tn), lambda j, i, k: (j // nb, i, j % nb))]

    (dw1,) = _matmul(h2, dfb, name="mm_dw1", ta=True, M=D, N=DFF, K=S, tn=dff4, outs=[sds((NCHIP, D, dff4), BF16)],
                     out_specs=dw1_out, after=[rs_w2[4]])
    sib_w1 = rs_sibling(dw1, "w_ffn1")

    def w1_rows(tn, tk):
        kb = dff4 // tk
        return pl.BlockSpec((None, tn, tk), lambda j, i, k: (k // kb, j, k % kb))

    (dh2,) = _matmul(dfb, w14, name="mm_dh2", tb=True, M=S, N=D, K=DFF, tm=2 * TM, b_spec=w1_rows,
                     outs=[sds((S, D), F32)], after=[sib_w1[4]])
    rs_w1 = rs_chips(sib_w1, dh2, "w_ffn1")
    dx2, dx2b, dg_ffn = _rms_bwd(dh2, x2, g_ffn, dx3, name="rms_ffn_bwd", after=[rs_w1[4]])
    (dwo,) = _matmul(hn, dx2b, name="mm_dwo", ta=True, M=D, N=D, K=S, outs=[sds((D, D), BF16)])
    sib_wo = rs_sibling(dwo.reshape(NCHIP, D // NCHIP, D), "w_o")
    (dhn,) = _matmul(dx2b, w_o_full, name="mm_dhn", tb=True, M=S, N=D, K=D, outs=[sds((S, D), F32)],
                     after=[sib_wo[4]])
    rs_wo = rs_chips(sib_wo, dhn, "w_o")
    sh_w2 = rs_end(rs_w2, rs_wo[4], "w_ffn2")
    dproj, dkv, dws, dbs8, dlng, dlnb, dcw8, dgh = _mix_bwd(
        dhn, heads, proj, ycv, kv, ws3, bs_t, ln_v_g, ln_v_b, conv_full, g_head, sh_w2[4], name="mix_bwd")
    (dwin_t,) = _matmul(dproj, h, name="mm_dwin", ta=True, M=DIN, N=D, K=S, tm=DIN // 2, outs=[sds((DIN, D), BF16)])
    sib_win = rs_sibling(dwin_t.reshape(NCHIP, din4, D), "w_in")
    (dwkv,) = _matmul(mem_n, dkv, name="mm_dwkv", ta=True, M=D, N=2 * DM, K=NMEM, outs=[sds((D, 2 * DM), BF16)],
                      after=[sib_win[4]])
    sib_wkv = rs_sibling(dwkv.reshape(NCHIP, D // NCHIP, 2 * DM), "w_kv")
    (dh,) = _matmul(dproj, w_in_t, name="mm_dh", M=S, N=D, K=DIN, tk=DIN, outs=[sds((S, D), F32)],
                    after=[sib_wkv[4]])
    rs_win = rs_chips(sib_win, dh, "w_in")
    rs_wkv = rs_chips(sib_wkv, rs_win[4], "w_kv")
    dx, dg_mix = _rms_bwd(dh, x2d, g_mix, dx2, name="rms_mix_bwd", want_bf=False, after=[rs_wkv[4]])
    sh_w1 = rs_end(rs_w1, dx, "w_ffn1")
    (dmem_n,) = _matmul(dkv, w_kv_full, name="mm_dmem", tb=True, M=NMEM, N=D, K=2 * DM, outs=[sds((NMEM, D), F32)],
                        after=[sh_w1[4]])
    (dg_mem,) = _rms_bwd(dmem_n, mem2d, g_mem, None, name="rms_mem_bwd", want_dx=False)
    sh_wo = rs_end(rs_wo, dg_mem, "w_o")
    done = rs_finish(4, sh_w2, sh_wo[4])
    done = rs_finish(3, sh_w1, done)
    sh_win = rs_end(rs_win, done, "w_in")
    sh_wkv = rs_end(rs_wkv, sh_win[4], "w_kv")
    done = rs_finish(2, sh_wo, sh_wkv[4])
    done = rs_finish(0, sh_win, done)
    done = rs_finish(1, sh_wkv, done)

    small_names = ["g_mix", "ln_v_g", "ln_v_b", "w_s", "b_s", "conv_w", "g_mem", "g_head", "g_ffn", "g_final"]
    small_part = [dg_mix, dlng, dlnb, dws, dbs8[:, 0, :], dcw8[:3], dg_mem, dgh, dg_ffn, dg_final, loss11]
    small_shapes = [(1, D), (1, DS), (1, DS), (NSH, CHUNK, CHUNK), (NSH, CHUNK), (3, DC), (1, D), (1, D), (1, D), (1, D),
                    (1, 1)]
    total = _allreduce_small(_pack(small_part), done, name="allreduce_small")
    small_g = _unpack(total, small_shapes)
    loss = small_g.pop()[0, 0]
    small_g[5] = lax.dynamic_slice(small_g[5], (0, shard * dcv4), (3, dcv4))
    small_w = [g_mix, ln_v_g, ln_v_b, ws3, bs2, conv_w[0], g_mem, g_head, g_ffn, g_final2]
    small_m = [m_g_mix, m_ln_v_g, m_ln_v_b, m_w_s[0], m_b_s[0], m_conv_w[0], m_g_mem, m_g_head, m_g_ffn,
               m_g_final.reshape(1, D)]
    small_v = [v_g_mix, v_ln_v_g, v_ln_v_b, v_w_s[0], v_b_s[0], v_conv_w[0], v_g_mem, v_g_head, v_g_ffn,
               v_g_final.reshape(1, D)]
    s_delta, s_m, s_v = _adamw_small(small_w, small_g, small_m, small_v, name="adamw_small")
    small_out = {nm: (g, d, mn, vn) for nm, g, d, mn, vn in zip(small_names, small_g, s_delta, s_m, s_v)}

    order = ["g_mix", "w_in", "ln_v_g", "ln_v_b", "w_s", "b_s", "conv_w", "g_mem", "w_kv", "g_head", "w_o",
             "g_ffn", "w_ffn1", "w_ffn2", "g_final"]
    like = dict(g_mix=g_mix, w_in=w_in, ln_v_g=ln_v_g, ln_v_b=ln_v_b, w_s=w_s, b_s=b_s, conv_w=conv_w, g_mem=g_mem,
                w_kv=w_kv, g_head=g_head, w_o=w_o, g_ffn=g_ffn, w_ffn1=w_ffn1, w_ffn2=w_ffn2, g_final=g_final)
    res = {**big_out, **small_out}
    res["w_in"] = [a.T for a in res["w_in"]]
    outs = [loss, dx[None]]
    for k in range(4):
        outs += [res[nm][k].reshape(like[nm].shape) for nm in order]
    return tuple(outs)
```

```python
import math

import jax
import jax.numpy as jnp
from jax import lax
from jax.experimental import pallas as pl
from jax.experimental.pallas import tpu as pltpu

F32 = jnp.float32
BF16 = jnp.bfloat16
MESH = pl.DeviceIdType.MESH

D = 2048
S = 2048
HD = 128
NH = D // HD
NMH = 4
NSH = (NH - NMH) // 2
NCH = NH - NMH - NSH
DS = NSH * HD
DC = NCH * HD
DM = NMH * HD
DIN = 2 * DS + 3 * DC + DM
CHUNK = 128
NMEM = 256
DFF = 4 * D
EPS = 1e-6
NCHIP = 4
SCALE = HD ** -0.5

ADAM_LR = 0.001
ADAM_B1 = 0.9
ADAM_B2 = 0.999
ADAM_EPS = 1e-08
ADAM_WD = 0.01
ADAM_STEP = 10

TR_EW = 256
TR_MIX = 256
TM = 512
TN = 1024
TK = 2048
N_SUB = 512
VMEM_MB = 56
HALO = 8


def _pick(n, target, q=128):
    best = None
    for t in range(q, min(n, target) + 1, q):
        if n % t == 0:
            best = t
    return n if best is None else best


def _pick_rows(n, q):
    below = _pick(n, TR_EW, q)
    if 2 * below >= TR_EW:
        return below
    above = [t for t in range(TR_EW, min(n, 4 * TR_EW) + 1, q) if n % t == 0]
    return above[0] if above else below


def _cp(sem=None, vmem_mb=None, **kw):
    d = dict(kw)
    if sem is not None:
        d["dimension_semantics"] = sem
    if vmem_mb is not None:
        d["vmem_limit_bytes"] = vmem_mb << 20
    return pltpu.CompilerParams(**d)


def _gelu(x):
    z = 0.7978845608028654 * (x + 0.044715 * (x * x * x))
    return 0.5 * x * (1.0 + jnp.tanh(z))


def _gelu_with_grad(x):
    x2 = x * x
    t = jnp.tanh(0.7978845608028654 * (x + 0.044715 * (x2 * x)))
    half = 0.5 * (1.0 + t)
    return x * half, half + 0.5 * x * (1.0 - t * t) * (0.7978845608028654 * (1.0 + 3.0 * 0.044715 * x2))


def _matmul(a, b, *, name, ta=False, tb=False, M, N, K, tm=None, tn=None, tk=None, outs, epi=None,
            extras=(), a_spec=None, b_spec=None, out_specs=None, after=(), n_split=None, slots=None, into=()):
    n_after = len(after)
    tm = _pick(M, TM if tm is None else tm, 8)
    tn = _pick(N, TN if tn is None else tn)
    tk = _pick(K, TK if tk is None else tk)
    if n_split is None:
        n_split = tn // N_SUB if tn % N_SUB == 0 else 1
    nk = K // tk
    grid = (N // tn, M // tm, nk)
    if a_spec is None:
        a_spec = (pl.BlockSpec((tk, tm), lambda j, i, k, *s: (k, i)) if ta
                  else pl.BlockSpec((tm, tk), lambda j, i, k, *s: (i, k)))
    else:
        a_spec = a_spec(tm, tk)
    if b_spec is None:
        b_spec = (pl.BlockSpec((tn, tk), lambda j, i, k, *s: (j, k)) if tb
                  else pl.BlockSpec((tk, tn), lambda j, i, k, *s: (k, j)))
    else:
        b_spec = b_spec(tn, tk)
    if out_specs is None:
        out_specs = [pl.BlockSpec((tm, tn), lambda j, i, k, *s: (i, j)) for _ in outs]
    else:
        out_specs = out_specs(tm, tn)
    dn = (((0 if ta else 1,), (1 if tb else 0,)), ((), ()))
    n_ex, n_out = len(extras), len(outs)
    n_pre = 0 if slots is None else 1
    n_into = len(into)
    ns = tn // n_split

    def body(*refs):
        a_ref, b_ref = refs[n_pre], refs[n_pre + 1]
        ex = refs[n_pre + 2:n_pre + 2 + n_ex]
        first_out = n_pre + 2 + n_ex + n_after + n_into
        o = refs[first_out:first_out + n_out]
        acc = refs[first_out + n_out:]
        k = pl.program_id(2)

        def finish(val, cols):
            res = (val,) if epi is None else epi(val, *[e[:, cols] for e in ex])
            for r, o_ref in zip(res, o):
                o_ref[:, cols] = r.astype(o_ref.dtype)

        if nk > 1:
            @pl.when(k == 0)
            def _():
                acc[0][...] = jnp.zeros_like(acc[0])

        av = a_ref[...].astype(BF16)
        for q in range(n_split):
            cols = slice(q * ns, (q + 1) * ns)
            bq = (b_ref[cols, :] if tb else b_ref[:, cols]).astype(BF16)
            part = lax.dot_general(av, bq, dn, preferred_element_type=F32)
            if nk == 1:
                finish(part, cols)
            else:
                acc[0][:, cols] += part

        if nk > 1:
            @pl.when(k == nk - 1)
            def _():
                finish(acc[0][...], slice(0, tn))

    in_specs = ([a_spec, b_spec] + [sp(tm, tn) for _, sp in extras] + [ANY] * (n_after + n_into))
    scratch = [pltpu.VMEM((tm, tn), F32)] if nk > 1 else []
    args = [a, b] + [arr for arr, _ in extras] + list(after) + list(into)
    aliases = {n_pre + len(args) - n_into + t: t for t in range(n_into)}
    params = _cp(("parallel", "parallel", "arbitrary"), VMEM_MB)
    if slots is None:
        return pl.pallas_call(body, name=name, grid=grid, in_specs=in_specs, out_specs=out_specs, out_shape=outs,
                              scratch_shapes=scratch, input_output_aliases=aliases, compiler_params=params)(*args)
    return pl.pallas_call(
        body, name=name,
        grid_spec=pltpu.PrefetchScalarGridSpec(num_scalar_prefetch=1, grid=grid, in_specs=in_specs,
                                               out_specs=out_specs, scratch_shapes=scratch),
        out_shape=outs, input_output_aliases=aliases, compiler_params=params)(slots, *args)


def _tile_spec():
    return lambda tm, tn: pl.BlockSpec((tm, tn), lambda j, i, k, *s: (i, j))


def _cast_into_slot(w, slot, after, *, name):
    R, C = w.shape
    tr = _pick_rows(R, 16)

    def body(s_ref, w_ref, _after_ref, o_ref):
        o_ref[...] = w_ref[...].astype(BF16)

    return pl.pallas_call(
        body, name=name,
        grid_spec=pltpu.PrefetchScalarGridSpec(
            num_scalar_prefetch=1, grid=(R // tr,),
            in_specs=[pl.BlockSpec((tr, C), lambda i, s: (i, 0)), ANY],
            out_specs=pl.BlockSpec((None, tr, C), lambda i, s: (s[0], i, 0))),
        out_shape=jax.ShapeDtypeStruct((NCHIP, R, C), BF16),
        compiler_params=_cp(("parallel",), VMEM_MB),
    )(slot, w, after)


def _rms_fwd(x, g, *, name, after=()):
    R, C = x.shape
    tr = _pick(R, TR_EW, 16)
    n_after = len(after)

    def body(x_ref, g_ref, *rest):
        o_ref = rest[n_after]
        xv = x_ref[...]
        r = lax.rsqrt(jnp.mean(xv * xv, axis=-1, keepdims=True) + EPS)
        o_ref[...] = ((xv * r) * g_ref[...]).astype(BF16)

    return pl.pallas_call(
        body, name=name, grid=(R // tr,),
        in_specs=[pl.BlockSpec((tr, C), lambda i: (i, 0)), pl.BlockSpec((1, C), lambda i: (0, 0))] + [ANY] * n_after,
        out_specs=pl.BlockSpec((tr, C), lambda i: (i, 0)),
        out_shape=jax.ShapeDtypeStruct((R, C), BF16),
        compiler_params=_cp(("parallel",), VMEM_MB),
    )(x, g, *after)


def _rms_bwd(dh, x, g, dres, *, name, want_dx=True, want_bf=True, after=()):
    R, C = x.shape
    tr = _pick(R, TR_EW, 16)
    has_res = dres is not None
    row = pl.BlockSpec((tr, C), lambda i: (i, 0))
    vec = pl.BlockSpec((1, C), lambda i: (0, 0))

    def body(*refs):
        dh_ref, x_ref, g_ref = refs[:3]
        pos = 3
        dres_ref = None
        if has_res:
            dres_ref = refs[pos]
            pos += 1
        outs = refs[pos + len(after):]
        i = pl.program_id(0)
        xv = x_ref[...]
        r = lax.rsqrt(jnp.mean(xv * xv, axis=-1, keepdims=True) + EPS)
        xh = xv * r
        dhv = dh_ref[...]
        dg_ref = outs[-1]
        dgp = jnp.sum(dhv * xh, axis=0, keepdims=True)

        @pl.when(i == 0)
        def _():
            dg_ref[...] = dgp

        @pl.when(i > 0)
        def _():
            dg_ref[...] += dgp

        if want_dx:
            t = dhv * g_ref[...]
            dx = r * (t - xh * jnp.mean(t * xh, axis=-1, keepdims=True))
            if has_res:
                dx = dx + dres_ref[...]
            outs[0][...] = dx
            if want_bf:
                outs[1][...] = dx.astype(BF16)

    in_specs = [row, row, vec] + ([row] if has_res else []) + [ANY] * len(after)
    out_specs, out_shape = [], []
    if want_dx:
        out_specs.append(row)
        out_shape.append(jax.ShapeDtypeStruct((R, C), F32))
        if want_bf:
            out_specs.append(row)
            out_shape.append(jax.ShapeDtypeStruct((R, C), BF16))
    out_specs.append(vec)
    out_shape.append(jax.ShapeDtypeStruct((1, C), F32))
    args = [dh, x, g] + ([dres] if has_res else []) + list(after)
    return pl.pallas_call(
        body, name=name, grid=(R // tr,), in_specs=in_specs, out_specs=out_specs, out_shape=out_shape,
        compiler_params=_cp(("arbitrary",), VMEM_MB),
    )(*args)


def _loss_bwd(x3, g, tgt, *, name):
    R, C = x3.shape
    tr = _pick(R, TR_EW, 16)
    n = R // tr
    row = pl.BlockSpec((tr, C), lambda i: (i, 0))
    vec = pl.BlockSpec((1, C), lambda i: (0, 0))

    def body(x_ref, g_ref, t_ref, dx_ref, dxb_ref, dg_ref, loss_ref, acc_ref):
        i = pl.program_id(0)
        xv = x_ref[...]
        gv = g_ref[...]
        r = lax.rsqrt(jnp.mean(xv * xv, axis=-1, keepdims=True) + EPS)
        xh = xv * r
        e = xh * gv - t_ref[...]
        dy = e * (1.0 / C)
        sq = jnp.sum(e * e, axis=0, keepdims=True)
        dgp = jnp.sum(dy * xh, axis=0, keepdims=True)

        @pl.when(i == 0)
        def _():
            acc_ref[...] = sq
            dg_ref[...] = dgp

        @pl.when(i > 0)
        def _():
            acc_ref[...] += sq
            dg_ref[...] += dgp

        t = dy * gv
        dx = r * (t - xh * jnp.mean(t * xh, axis=-1, keepdims=True))
        dx_ref[...] = dx
        dxb_ref[...] = dx.astype(BF16)

        @pl.when(i == n - 1)
        def _():
            loss_ref[...] = jnp.sum(acc_ref[...], axis=-1, keepdims=True) * (0.5 / C)

    return pl.pallas_call(
        body, name=name, grid=(n,),
        in_specs=[row, vec, row],
        out_specs=[row, row, vec, pl.BlockSpec((1, 1), lambda i: (0, 0))],
        out_shape=[jax.ShapeDtypeStruct((R, C), F32), jax.ShapeDtypeStruct((R, C), BF16),
                   jax.ShapeDtypeStruct((1, C), F32), jax.ShapeDtypeStruct((1, 1), F32)],
        scratch_shapes=[pltpu.VMEM((1, C), F32)],
        compiler_params=_cp(("arbitrary",), VMEM_MB),
    )(x3, g, tgt)


def _offsets():
    u0 = 0
    v0 = DS
    b0 = 2 * DS
    c0 = b0 + DC
    x0 = c0 + DC
    q0 = x0 + DC
    return u0, v0, b0, c0, x0, q0


def _tri_mask(lower):
    r = lax.broadcasted_iota(jnp.int32, (CHUNK, CHUNK), 0)
    c = lax.broadcasted_iota(jnp.int32, (CHUNK, CHUNK), 1)
    return (r >= c) if lower else (c >= r)


def _layer_norm_stats(vg):
    mu = jnp.mean(vg, axis=-1, keepdims=True)
    vc = vg - mu
    rstd = lax.rsqrt(jnp.mean(vc * vc, axis=-1, keepdims=True) + EPS)
    return vc * rstd, rstd


def _softmax_rows(qh, kh):
    s = lax.dot_general(qh, kh, (((1,), (1,)), ((), ())), preferred_element_type=F32)
    m = jnp.max(s, axis=-1, keepdims=True)
    e = jnp.exp(s - m)
    return e / jnp.sum(e, axis=-1, keepdims=True)


def _mix_fwd(proj, kv, w_s, bs_t, ln_g, ln_b, conv_w, g_head, *, name):
    assert DS == DC
    tr = _pick(S, TR_MIX, CHUNK)
    n = S // tr
    nck = tr // CHUNK
    u0, v0, b0, c0, x0, q0 = _offsets()
    hb = tr // HALO

    def body(p_ref, cprev_ref, xprev_ref, kv_ref, ws_ref, bst_ref, lng_ref, lnb_ref, cw_ref, gh_ref,
             heads_ref, hn_ref, ycv_ref, buf_ref):
        i = pl.program_id(0)

        def emit(col, val):
            rs = lax.rsqrt(jnp.mean(val * val, axis=-1, keepdims=True) + EPS)
            heads_ref[:, col:col + HD] = val
            hn_ref[:, col:col + HD] = ((val * rs) * gh_ref[:, col:col + HD]).astype(BF16)

        vhat, _ = _layer_norm_stats(_gelu(p_ref[:, v0:v0 + DS]))
        vnb = (vhat * lng_ref[...] + lnb_ref[...]).astype(BF16)
        low = _tri_mask(True)
        for h in range(NSH):
            wt = jnp.where(low, ws_ref[h], 0.0).astype(BF16)
            bcol = bst_ref[:, h:h + 1]
            parts = []
            for c in range(nck):
                blk = vnb[c * CHUNK:(c + 1) * CHUNK, h * HD:(h + 1) * HD]
                parts.append(jnp.dot(wt, blk, preferred_element_type=F32) + bcol)
            mixed = parts[0] if nck == 1 else jnp.concatenate(parts, axis=0)
            emit(h * HD, _gelu(p_ref[:, u0 + h * HD:u0 + (h + 1) * HD]) * mixed)

        xc = p_ref[:, c0:c0 + DC] * p_ref[:, x0:x0 + DC]
        prev = cprev_ref[...] * xprev_ref[...]
        buf_ref[0:HALO, :] = jnp.where(i > 0, prev, 0.0)
        buf_ref[HALO:HALO + tr, :] = xc
        y = (cw_ref[2:3, :] * xc + cw_ref[1:2, :] * buf_ref[HALO - 1:HALO - 1 + tr, :]
             + cw_ref[0:1, :] * buf_ref[HALO - 2:HALO - 2 + tr, :])
        ycv_ref[...] = y
        cout = p_ref[:, b0:b0 + DC] * y
        for h in range(NCH):
            emit(DS + h * HD, cout[:, h * HD:(h + 1) * HD])

        for h in range(NMH):
            qh = (p_ref[:, q0 + h * HD:q0 + (h + 1) * HD] * SCALE).astype(BF16)
            kh = kv_ref[:, h * HD:(h + 1) * HD].astype(BF16)
            vh = kv_ref[:, DM + h * HD:DM + (h + 1) * HD].astype(BF16)
            p = _softmax_rows(qh, kh)
            emit(DS + DC + h * HD, jnp.dot(p.astype(BF16), vh, preferred_element_type=F32))

    full = lambda shape: pl.BlockSpec(shape, lambda i: (0,) * len(shape))
    halo_c = pl.BlockSpec((HALO, DC), lambda i: (jnp.maximum(i * hb - 1, 0), c0 // DC))
    halo_x = pl.BlockSpec((HALO, DC), lambda i: (jnp.maximum(i * hb - 1, 0), x0 // DC))
    return pl.pallas_call(
        body, name=name, grid=(n,),
        in_specs=[pl.BlockSpec((tr, DIN), lambda i: (i, 0)), halo_c, halo_x,
                  full((NMEM, 2 * DM)), full((NSH, CHUNK, CHUNK)), full((CHUNK, NSH)),
                  full((1, DS)), full((1, DS)), full((3, DC)), full((1, D))],
        out_specs=[pl.BlockSpec((tr, D), lambda i: (i, 0)), pl.BlockSpec((tr, D), lambda i: (i, 0)),
                   pl.BlockSpec((tr, DC), lambda i: (i, 0))],
        out_shape=[jax.ShapeDtypeStruct((S, D), F32), jax.ShapeDtypeStruct((S, D), BF16),
                   jax.ShapeDtypeStruct((S, DC), F32)],
        scratch_shapes=[pltpu.VMEM((tr + HALO, DC), F32)],
        compiler_params=_cp(("parallel",), VMEM_MB),
    )(proj, proj, proj, kv, w_s, bs_t, ln_g, ln_b, conv_w, g_head)


def _mix_bwd(dhn, heads, proj, ycv, kv, w_s, bs_t, ln_g, ln_b, conv_w, g_head, after, *, name):
    assert DS == DC
    tr = _pick(S, TR_MIX, CHUNK)
    n = S // tr
    nck = tr // CHUNK
    u0, v0, b0, c0, x0, q0 = _offsets()
    hb = tr // HALO
    last_hb = S // HALO - 1

    def body(dhn_ref, heads_ref, p_ref, ycv_ref, dhn_nx_ref, heads_nx_ref, b_nx_ref, kv_ref, ws_ref, bst_ref,
             lng_ref, lnb_ref, cw_ref, gh_ref, _after_ref,
             dp_ref, dkv_ref, dws_ref, dbs_ref, dlng_ref, dlnb_ref, dcw_ref, dgh_ref, buf_ref, dvn_ref):
        i = pl.program_id(0)

        @pl.when(i == 0)
        def _():
            dkv_ref[...] = jnp.zeros_like(dkv_ref)
            dws_ref[...] = jnp.zeros_like(dws_ref)
            dbs_ref[...] = jnp.zeros_like(dbs_ref)
            dlng_ref[...] = jnp.zeros_like(dlng_ref)
            dlnb_ref[...] = jnp.zeros_like(dlnb_ref)
            dcw_ref[...] = jnp.zeros_like(dcw_ref)
            dgh_ref[...] = jnp.zeros_like(dgh_ref)

        def head_bwd(a, dn, gh):
            rs = lax.rsqrt(jnp.mean(a * a, axis=-1, keepdims=True) + EPS)
            ah = a * rs
            t = dn * gh
            return rs * (t - ah * jnp.mean(t * ah, axis=-1, keepdims=True)), jnp.sum(dn * ah, axis=0, keepdims=True)

        def head_grad(col):
            da, dg = head_bwd(heads_ref[:, col:col + HD], dhn_ref[:, col:col + HD], gh_ref[:, col:col + HD])
            dgh_ref[:, col:col + HD] += dg
            return da

        vg, dvg_dv = _gelu_with_grad(p_ref[:, v0:v0 + DS])
        vhat, rstd = _layer_norm_stats(vg)
        vnb = (vhat * lng_ref[...] + lnb_ref[...]).astype(BF16)
        low = _tri_mask(True)
        ones = jnp.ones((HALO, HD), BF16)
        for h in range(NSH):
            w_h = ws_ref[h]
            wt = jnp.where(low, w_h, 0.0).astype(BF16)
            bcol = bst_ref[:, h:h + 1]
            da = head_grad(h * HD)
            ug, dug_du = _gelu_with_grad(p_ref[:, u0 + h * HD:u0 + (h + 1) * HD])
            dws = jnp.zeros((CHUNK, CHUNK), F32)
            dbs = jnp.zeros((HALO, CHUNK), F32)
            mixed_parts = []
            for c in range(nck):
                rows = slice(c * CHUNK, (c + 1) * CHUNK)
                blk = vnb[rows, h * HD:(h + 1) * HD]
                mixed_parts.append(jnp.dot(wt, blk, preferred_element_type=F32) + bcol)
                dmb = (da[rows] * ug[rows]).astype(BF16)
                dws = dws + lax.dot_general(dmb, blk, (((1,), (1,)), ((), ())), preferred_element_type=F32)
                dbs = dbs + lax.dot_general(ones, dmb, (((1,), (1,)), ((), ())), preferred_element_type=F32)
                dvn_ref[c * CHUNK:(c + 1) * CHUNK, h * HD:(h + 1) * HD] = lax.dot_general(
                    wt, dmb, (((0,), (0,)), ((), ())), preferred_element_type=F32)
            mixed = mixed_parts[0] if nck == 1 else jnp.concatenate(mixed_parts, axis=0)
            dp_ref[:, u0 + h * HD:u0 + (h + 1) * HD] = ((da * mixed) * dug_du).astype(BF16)
            dws_ref[h] += jnp.where(low, dws, 0.0)
            dbs_ref[h] += dbs
        dvn = dvn_ref[...]
        dlng_ref[...] += jnp.sum(dvn * vhat, axis=0, keepdims=True)
        dlnb_ref[...] += jnp.sum(dvn, axis=0, keepdims=True)
        dvh = dvn * lng_ref[...]
        dvg = rstd * (dvh - jnp.mean(dvh, axis=-1, keepdims=True)
                      - vhat * jnp.mean(dvh * vhat, axis=-1, keepdims=True))
        dp_ref[:, v0:v0 + DS] = (dvg * dvg_dv).astype(BF16)

        dc = jnp.concatenate([head_grad(DS + h * HD) for h in range(NCH)], axis=1)
        dc_nx = jnp.concatenate(
            [head_bwd(heads_nx_ref[:, h * HD:(h + 1) * HD], dhn_nx_ref[:, h * HD:(h + 1) * HD],
                      gh_ref[:, DS + h * HD:DS + (h + 1) * HD])[0] for h in range(NCH)], axis=1)
        bg = p_ref[:, b0:b0 + DC]
        cg = p_ref[:, c0:c0 + DC]
        xin = p_ref[:, x0:x0 + DC]
        dp_ref[:, b0:b0 + DC] = (dc * ycv_ref[...]).astype(BF16)
        dyv = dc * bg
        buf_ref[0:tr, :] = dyv
        buf_ref[tr:tr + HALO, :] = jnp.where(i < n - 1, dc_nx * b_nx_ref[...], 0.0)
        sh1 = buf_ref[1:1 + tr, :]
        sh0 = buf_ref[2:2 + tr, :]
        dxc = cw_ref[2:3, :] * dyv + cw_ref[1:2, :] * sh1 + cw_ref[0:1, :] * sh0
        xc = cg * xin
        dp_ref[:, c0:c0 + DC] = (dxc * xin).astype(BF16)
        dp_ref[:, x0:x0 + DC] = (dxc * cg).astype(BF16)
        dcw_ref[0:1, :] += jnp.sum(sh0 * xc, axis=0, keepdims=True)
        dcw_ref[1:2, :] += jnp.sum(sh1 * xc, axis=0, keepdims=True)
        dcw_ref[2:3, :] += jnp.sum(dyv * xc, axis=0, keepdims=True)

        for h in range(NMH):
            do = head_grad(DS + DC + h * HD).astype(BF16)
            qh = (p_ref[:, q0 + h * HD:q0 + (h + 1) * HD] * SCALE).astype(BF16)
            kh = kv_ref[:, h * HD:(h + 1) * HD].astype(BF16)
            vh = kv_ref[:, DM + h * HD:DM + (h + 1) * HD].astype(BF16)
            p = _softmax_rows(qh, kh)
            dpr = lax.dot_general(do, vh, (((1,), (1,)), ((), ())), preferred_element_type=F32)
            ds = (p * (dpr - jnp.sum(dpr * p, axis=-1, keepdims=True))).astype(BF16)
            dp_ref[:, q0 + h * HD:q0 + (h + 1) * HD] = (
                jnp.dot(ds, kh, preferred_element_type=F32) * SCALE).astype(BF16)
            dkv_ref[:, h * HD:(h + 1) * HD] += lax.dot_general(
                ds, qh, (((0,), (0,)), ((), ())), preferred_element_type=F32)
            dkv_ref[:, DM + h * HD:DM + (h + 1) * HD] += lax.dot_general(
                p.astype(BF16), do, (((0,), (0,)), ((), ())), preferred_element_type=F32)

    full = lambda shape: pl.BlockSpec(shape, lambda i: (0,) * len(shape))
    row = lambda c: pl.BlockSpec((tr, c), lambda i: (i, 0))
    nxt = lambda col: pl.BlockSpec((HALO, DC), lambda i: (jnp.minimum((i + 1) * hb, last_hb), col))
    return pl.pallas_call(
        body, name=name, grid=(n,),
        in_specs=[row(D), row(D), row(DIN), row(DC), nxt(DS // DC), nxt(DS // DC), nxt(b0 // DC),
                  full((NMEM, 2 * DM)), full((NSH, CHUNK, CHUNK)), full((CHUNK, NSH)),
                  full((1, DS)), full((1, DS)), full((3, DC)), full((1, D)), ANY],
        out_specs=[row(DIN), full((NMEM, 2 * DM)), full((NSH, CHUNK, CHUNK)), full((NSH, HALO, CHUNK)),
                   full((1, DS)), full((1, DS)), full((HALO, DC)), full((1, D))],
        out_shape=[jax.ShapeDtypeStruct((S, DIN), BF16), jax.ShapeDtypeStruct((NMEM, 2 * DM), F32),
                   jax.ShapeDtypeStruct((NSH, CHUNK, CHUNK), F32), jax.ShapeDtypeStruct((NSH, HALO, CHUNK), F32),
                   jax.ShapeDtypeStruct((1, DS), F32), jax.ShapeDtypeStruct((1, DS), F32),
                   jax.ShapeDtypeStruct((HALO, DC), F32), jax.ShapeDtypeStruct((1, D), F32)],
        scratch_shapes=[pltpu.VMEM((tr + HALO, DC), F32), pltpu.VMEM((tr, DS), F32)],
        compiler_params=_cp(("arbitrary",), VMEM_MB),
    )(dhn, heads, proj, ycv, dhn, heads, proj, kv, w_s, bs_t, ln_g, ln_b, conv_w, g_head, after)


def _place():
    x, y, c = lax.axis_index("x"), lax.axis_index("y"), lax.axis_index("c")
    chips = [(1 - x, y), (x, 1 - y), (1 - x, 1 - y)]
    return x, y, c, chips


ANY = pl.BlockSpec(memory_space=pl.ANY)


HBM = pl.BlockSpec(memory_space=pltpu.HBM)
SEM = pl.BlockSpec(memory_space=pltpu.SEMAPHORE)
EFFECT = pltpu.SideEffectType.DATAFLOW_SIDE_EFFECTING
N_PEER_CHIPS = 3
N_NEIGHBOUR_CHIPS = 2
CONV_PAD = (32, 256)


def _in_hbm(a):
    return pltpu.with_memory_space_constraint(a, pltpu.HBM)


def _allgather_start(bufs, forwards, after, collective_id, *, name):
    arrs = list(bufs) + list(forwards)
    nw, nb = len(arrs), len(bufs)

    def body(*refs):
        ins, send, recv = refs[:nw], refs[nw + 1:2 * nw + 1], refs[2 * nw + 1:3 * nw + 1]
        token = refs[4 * nw + 1]
        x, y, c, chips = _place()
        s = 2 * x + y
        slots = [2 * cx + cy for cx, cy in chips]
        _handshake([(cx, cy, c) for cx, cy in chips[:N_NEIGHBOUR_CHIPS]])
        for w in range(nb, nw):
            q = arrs[w].shape[1] // 4
            for j in range(N_NEIGHBOUR_CHIPS):
                rows = ins[w].at[slots[j], pl.ds(c * 2 * q + j * q, q)]
                pltpu.make_async_remote_copy(src_ref=rows, dst_ref=rows, send_sem=send[w], recv_sem=recv[w],
                                             device_id=(*chips[1 - j], c), device_id_type=MESH).start()
        for w in range(nb):
            hr = arrs[w].shape[1] // 2
            rows = ins[w].at[s, pl.ds(c * hr, hr)]
            for cx, cy in chips[:N_NEIGHBOUR_CHIPS]:
                pltpu.make_async_remote_copy(src_ref=rows, dst_ref=rows, send_sem=send[w], recv_sem=recv[w],
                                             device_id=(cx, cy, c), device_id_type=MESH).start()
        token[...] = jnp.zeros_like(token)

    res = pl.pallas_call(
        body, name=name,
        in_specs=[HBM] * nw + [ANY],
        out_specs=[SEM] * (2 * nw) + [HBM] * nw + [pl.BlockSpec(memory_space=pltpu.VMEM)],
        out_shape=[pltpu.SemaphoreType.DMA(())] * (2 * nw) + [pltpu.HBM(a.shape, a.dtype) for a in arrs]
        + [jax.ShapeDtypeStruct((8, 128), F32)],
        input_output_aliases={w: 2 * nw + w for w in range(nw)},
        compiler_params=pltpu.CompilerParams(has_side_effects=EFFECT, collective_id=collective_id),
    )(*[_in_hbm(a) for a in arrs], after)
    return res[:nw], res[nw:2 * nw], res[2 * nw:3 * nw], res[3 * nw]


def _handshake(peers):
    barrier = pltpu.get_barrier_semaphore()
    for peer in peers:
        pl.semaphore_signal(barrier, inc=1, device_id=peer, device_id_type=MESH)
    pl.semaphore_wait(barrier, len(peers))


def _scatter_start(parts, bufs, collective_id, *, name):
    nw = len(parts)

    def body(*refs):
        src, dst = refs[:nw], refs[nw:2 * nw]
        send, recv = refs[2 * nw:3 * nw], refs[3 * nw:4 * nw]
        token = refs[6 * nw]
        x, y, c, chips = _place()
        s = 2 * x + y
        _handshake([(cx, cy, c) for cx, cy in chips])
        for w in range(nw):
            for cx, cy in chips:
                pltpu.make_async_remote_copy(src_ref=src[w].at[2 * cx + cy], dst_ref=dst[w].at[s], send_sem=send[w],
                                             recv_sem=recv[w], device_id=(cx, cy, c), device_id_type=MESH).start()
        token[...] = jnp.zeros_like(token)

    res = pl.pallas_call(
        body, name=name,
        in_specs=[HBM] * (2 * nw),
        out_specs=[SEM] * (2 * nw) + [HBM] * (2 * nw) + [pl.BlockSpec(memory_space=pltpu.VMEM)],
        out_shape=[pltpu.SemaphoreType.DMA(())] * (2 * nw) + [pltpu.HBM(a.shape, a.dtype) for a in parts + bufs]
        + [jax.ShapeDtypeStruct((8, 128), F32)],
        input_output_aliases={k: 2 * nw + k for k in range(2 * nw)},
        compiler_params=pltpu.CompilerParams(has_side_effects=EFFECT, collective_id=collective_id),
    )(*[_in_hbm(a) for a in parts + bufs])
    return res[:nw], res[nw:2 * nw], res[2 * nw:3 * nw], res[3 * nw:4 * nw], res[4 * nw]


def _sibling_start(srcs, whole, collective_id, *, name):
    nw = len(srcs)
    lands = [lax.empty((a.shape[0], a.shape[1] if whole else a.shape[1] // 2, a.shape[2]), a.dtype) for a in srcs]

    def body(*refs):
        src, land = refs[:nw], refs[nw:2 * nw]
        send, recv = refs[2 * nw:3 * nw], refs[3 * nw:4 * nw]
        token = refs[6 * nw]
        x, y, c, _ = _place()
        _handshake([(x, y, 1 - c)])
        for w in range(nw):
            hr = srcs[w].shape[1] // 2
            rows = src[w] if whole else src[w].at[:, pl.ds((1 - c) * hr, hr)]
            pltpu.make_async_remote_copy(src_ref=rows, dst_ref=land[w], send_sem=send[w], recv_sem=recv[w],
                                         device_id=(x, y, 1 - c), device_id_type=MESH).start()
        token[...] = jnp.zeros_like(token)

    res = pl.pallas_call(
        body, name=name,
        in_specs=[HBM] * (2 * nw),
        out_specs=[SEM] * (2 * nw) + [HBM] * (2 * nw) + [pl.BlockSpec(memory_space=pltpu.VMEM)],
        out_shape=[pltpu.SemaphoreType.DMA(())] * (2 * nw) + [pltpu.HBM(a.shape, a.dtype) for a in srcs + lands]
        + [jax.ShapeDtypeStruct((8, 128), F32)],
        input_output_aliases={k: 2 * nw + k for k in range(2 * nw)},
        compiler_params=pltpu.CompilerParams(has_side_effects=EFFECT, collective_id=collective_id),
    )(*[_in_hbm(a) for a in srcs + lands])
    return res[:nw], res[nw:2 * nw], res[2 * nw:3 * nw], res[3 * nw:4 * nw], res[4 * nw]


def _transfer_wait(sends, recvs, thru, sizes, after, *, name):
    n = len(sends)
    flat = [a for group in thru for a in group]

    def body(*refs):
        bufs = refs[:len(flat)]
        send = refs[len(flat):len(flat) + n]
        recv = refs[len(flat) + n:len(flat) + 2 * n]
        token = refs[2 * len(flat) + 2 * n + 1]
        token[...] = jnp.zeros_like(token)
        x, y, c, _ = _place()
        pos = 0
        for k in range(n):
            slots, rows = sizes[k]
            region = bufs[pos].at[pl.ds(0, slots), pl.ds(0, rows)]
            pos += len(thru[k])
            cp = pltpu.make_async_remote_copy(src_ref=region, dst_ref=region, send_sem=send[k], recv_sem=recv[k],
                                              device_id=(x, y, 1 - c), device_id_type=MESH)
            cp.wait_send()
            cp.wait_recv()

    res = pl.pallas_call(
        body, name=name,
        in_specs=[HBM] * len(flat) + [SEM] * (2 * n) + [pl.BlockSpec(memory_space=pl.ANY)],
        out_specs=[HBM] * len(flat) + [pl.BlockSpec(memory_space=pltpu.VMEM)],
        out_shape=[pltpu.HBM(a.shape, a.dtype) for a in flat] + [jax.ShapeDtypeStruct((8, 128), F32)],
        input_output_aliases={k: k for k in range(len(flat))},
        compiler_params=pltpu.CompilerParams(has_side_effects=EFFECT),
    )(*flat, *sends, *recvs, after)
    out, pos = [], 0
    for group in thru:
        out.append(res[pos:pos + len(group)])
        pos += len(group)
    return out, res[len(flat)]


def _forward_gathered(bufs, after, *, name):
    nw = len(bufs)

    def body(*refs):
        outs = refs[nw + 1:2 * nw + 1]
        d_send, d_recv, i_send, i_recv = refs[2 * nw + 1:]
        x, y, c, chips = _place()
        me, sibling = (x, y, c), (x, y, 1 - c)
        slots = [2 * cx + cy for cx, cy in chips]

        def rows(w, j, start, n):
            return outs[w].at[slots[j], pl.ds(start, n)]

        def d2d(w, j, which, to):
            hr = bufs[w].shape[1] // 2
            r = rows(w, j, which * hr, hr)
            return pltpu.make_async_remote_copy(
                src_ref=r, dst_ref=r, send_sem=d_send.at[N_PEER_CHIPS * w + j],
                recv_sem=d_recv.at[N_PEER_CHIPS * w + j], device_id=to, device_id_type=MESH)

        def ici(w, j, slot_j, to):
            q = bufs[w].shape[1] // 4
            r = rows(w, slot_j, c * 2 * q + j * q, q)
            return pltpu.make_async_remote_copy(
                src_ref=r, dst_ref=r, send_sem=i_send.at[N_NEIGHBOUR_CHIPS * w + j],
                recv_sem=i_recv.at[N_NEIGHBOUR_CHIPS * w + j], device_id=to, device_id_type=MESH)

        started = []
        for w in range(nw):
            started += [ici(w, 0, 0, (*chips[1], c)), ici(w, 1, 1, (*chips[0], c))]
            started += [d2d(w, j, c, sibling) for j in range(N_NEIGHBOUR_CHIPS)]
        for cp in started:
            cp.start()
        diag = N_PEER_CHIPS - 1
        for w in range(nw):
            for j in range(N_NEIGHBOUR_CHIPS):
                ici(w, j, diag, me).wait_recv()
            cp = d2d(w, diag, c, sibling)
            cp.start()
            started.append(cp)
        for w in range(nw):
            for j in range(N_PEER_CHIPS):
                d2d(w, j, 1 - c, me).wait_recv()
        for cp in started:
            cp.wait_send()

    return pl.pallas_call(
        body, name=name,
        in_specs=[ANY] * (nw + 1), out_specs=[ANY] * nw,
        out_shape=[jax.ShapeDtypeStruct(a.shape, a.dtype) for a in bufs],
        input_output_aliases={w: w for w in range(nw)},
        scratch_shapes=[pltpu.SemaphoreType.DMA((N_PEER_CHIPS * nw,)), pltpu.SemaphoreType.DMA((N_PEER_CHIPS * nw,)),
                        pltpu.SemaphoreType.DMA((N_NEIGHBOUR_CHIPS * nw,)),
                        pltpu.SemaphoreType.DMA((N_NEIGHBOUR_CHIPS * nw,))],
    )(*bufs, after)


def _forward_halves(bufs, which, after, *, name):
    nw = len(bufs)
    n = len(which)

    def body(*refs):
        outs = refs[nw + 1:2 * nw + 1]
        send, recv = refs[2 * nw + 1:]
        x, y, c, chips = _place()
        me, sibling = (x, y, c), (x, y, 1 - c)

        def d2d(w, t, half, to):
            cx, cy = chips[which[t]]
            hr = bufs[w].shape[1] // 2
            rows = outs[w].at[2 * cx + cy, pl.ds(half * hr, hr)]
            return pltpu.make_async_remote_copy(src_ref=rows, dst_ref=rows, send_sem=send.at[n * w + t],
                                                recv_sem=recv.at[n * w + t], device_id=to, device_id_type=MESH)

        passed = [d2d(w, t, c, sibling) for w in range(nw) for t in range(n)]
        for cp in passed:
            cp.start()
        for w in range(nw):
            for t in range(n):
                d2d(w, t, 1 - c, me).wait_recv()
        for cp in passed:
            cp.wait_send()

    return pl.pallas_call(
        body, name=name,
        in_specs=[ANY] * (nw + 1), out_specs=[ANY] * nw,
        out_shape=[jax.ShapeDtypeStruct(a.shape, a.dtype) for a in bufs],
        input_output_aliases={w: w for w in range(nw)},
        scratch_shapes=[pltpu.SemaphoreType.DMA((n * nw,)), pltpu.SemaphoreType.DMA((n * nw,))],
    )(*bufs, after)


def _allreduce_small(p, after, *, name):
    R = p.shape[0]
    hr = R // 2

    def body(p_ref, _after_ref, out_ref, sib_ref, sum_ref, gat_ref, tot_ref, send, recv):
        x, y, c, chips = _place()
        s = 2 * x + y
        sibling = (x, y, 1 - c)
        rows = pl.ds(pl.multiple_of(c * hr, 8), hr)
        swap = pltpu.make_async_remote_copy(src_ref=p_ref, dst_ref=sib_ref, send_sem=send.at[0], recv_sem=recv.at[0],
                                            device_id=sibling, device_id_type=MESH)
        swap.start()
        swap.wait()
        sum_ref[...] = p_ref[...] + sib_ref[...]
        gat_ref[s] = sum_ref[rows, :]
        cps = [pltpu.make_async_remote_copy(src_ref=sum_ref.at[rows], dst_ref=gat_ref.at[s], send_sem=send.at[1 + j],
                                            recv_sem=recv.at[1 + j], device_id=(cx, cy, c), device_id_type=MESH)
               for j, (cx, cy) in enumerate(chips)]
        for cp in cps:
            cp.start()
        for cp in cps:
            cp.wait()
        tot_ref[...] = ((gat_ref[0] + gat_ref[1]) + gat_ref[2]) + gat_ref[3]
        out_ref[rows, :] = tot_ref[...]
        share = pltpu.make_async_remote_copy(src_ref=tot_ref, dst_ref=out_ref.at[rows], send_sem=send.at[4],
                                             recv_sem=recv.at[4], device_id=sibling, device_id_type=MESH)
        share.start()
        share.wait_send()
        other = out_ref.at[pl.ds(pl.multiple_of((1 - c) * hr, 8), hr)]
        pltpu.make_async_remote_copy(src_ref=other, dst_ref=other, send_sem=send.at[4], recv_sem=recv.at[4],
                                     device_id=(x, y, c), device_id_type=MESH).wait_recv()

    vmem = pl.BlockSpec(memory_space=pltpu.VMEM)
    return pl.pallas_call(
        body, name=name, in_specs=[vmem, ANY], out_specs=vmem,
        out_shape=jax.ShapeDtypeStruct((R, 128), F32),
        scratch_shapes=[pltpu.VMEM((R, 128), F32), pltpu.VMEM((R, 128), F32), pltpu.VMEM((NCHIP, hr, 128), F32),
                        pltpu.VMEM((hr, 128), F32), pltpu.SemaphoreType.DMA((5,)), pltpu.SemaphoreType.DMA((5,))],
    )(p, after)


def _select_half_bf16(g, half, add, slot, *, name):
    _, R, C = g.shape
    hr = R // 2
    tr = _pick_rows(hr, 16)
    nb = hr // tr
    sel = jnp.concatenate([jnp.reshape(half, (1,)).astype(jnp.int32), slot])

    def body(s_ref, g_ref, a_ref, o_ref, own_ref):
        val = (g_ref[...].astype(F32) + a_ref[...].astype(F32)).astype(BF16)
        o_ref[...] = val

        @pl.when(pl.program_id(1) == s_ref[1])
        def _():
            own_ref[...] = val

    g_spec = pl.BlockSpec((None, tr, C), lambda i, j, s: (j, s[0] * nb + i, 0))
    o_spec = pl.BlockSpec((None, tr, C), lambda i, j, s: (j, i, 0))
    own_spec = pl.BlockSpec((None, tr, C), lambda i, j, s: (s[1], i, 0))
    shape = jax.ShapeDtypeStruct((NCHIP, hr, C), BF16)
    return pl.pallas_call(
        body, name=name,
        grid_spec=pltpu.PrefetchScalarGridSpec(
            num_scalar_prefetch=1, grid=(nb, NCHIP), in_specs=[g_spec, o_spec], out_specs=[o_spec, own_spec]),
        out_shape=[shape, shape],
        compiler_params=_cp(("parallel", "arbitrary"), VMEM_MB),
    )(sel, g, add)


def _adamw_math(w, g, m, v):
    m = ADAM_B1 * m + (1.0 - ADAM_B1) * g
    v = ADAM_B2 * v + (1.0 - ADAM_B2) * (g * g)
    m_hat = m / (1.0 - ADAM_B1 ** ADAM_STEP)
    v_hat = v / (1.0 - ADAM_B2 ** ADAM_STEP)
    delta = -ADAM_LR * (m_hat / (jnp.sqrt(v_hat) + ADAM_EPS) + ADAM_WD * w)
    return delta, m, v


def _adamw(w, g_mine, g_sib, m, v, core, *, name):
    R, C = w.shape
    hr = R // 2
    tr = _pick_rows(hr, 16)
    nb = hr // tr
    row = pl.BlockSpec((tr, C), lambda hh, i, c: (hh * nb + i, 0))
    mine = pl.BlockSpec((NCHIP, tr, C), lambda hh, i, c: (0, jnp.where(hh == c[0], i, 0), 0))
    sibs = pl.BlockSpec((NCHIP, tr, C), lambda hh, i, c: (0, jnp.where(hh == c[0], 0, i), 0))

    def slot_sum(ref):
        acc = ref[0].astype(F32) + ref[1].astype(F32)
        for j in range(2, NCHIP):
            acc = acc + ref[j].astype(F32)
        return acc

    def body(c_ref, w_ref, gm_ref, gs_ref, m_ref, v_ref, go_ref, d_ref, mo_ref, vo_ref):
        gv = jnp.where(pl.program_id(0) == c_ref[0], slot_sum(gm_ref), slot_sum(gs_ref))
        d, mn, vn = _adamw_math(w_ref[...], gv, m_ref[...], v_ref[...])
        go_ref[...] = gv
        d_ref[...] = d
        mo_ref[...] = mn
        vo_ref[...] = vn

    return pl.pallas_call(
        body, name=name,
        grid_spec=pltpu.PrefetchScalarGridSpec(
            num_scalar_prefetch=1, grid=(2, nb),
            in_specs=[row, mine, sibs, row, row], out_specs=[row] * 4),
        out_shape=[jax.ShapeDtypeStruct((R, C), F32)] * 4,
        compiler_params=_cp(("parallel", "parallel"), VMEM_MB),
    )(core, w, g_mine, g_sib, m, v)


def _adamw_small(ws, gs, ms, vs, *, name):
    n = len(ws)

    def body(*refs):
        w_r, g_r, m_r, v_r = refs[:n], refs[n:2 * n], refs[2 * n:3 * n], refs[3 * n:4 * n]
        d_r, mo_r, vo_r = refs[4 * n:5 * n], refs[5 * n:6 * n], refs[6 * n:7 * n]
        for k in range(n):
            d, mn, vn = _adamw_math(w_r[k][...], g_r[k][...], m_r[k][...], v_r[k][...])
            d_r[k][...] = d
            mo_r[k][...] = mn
            vo_r[k][...] = vn

    shapes = [jax.ShapeDtypeStruct(w.shape, F32) for w in ws]
    res = pl.pallas_call(body, name=name, out_shape=shapes * 3)(*ws, *gs, *ms, *vs)
    return res[:n], res[n:2 * n], res[2 * n:]


_PACK_ROWS = 8


def _pack(parts):
    rows = []
    for a in parts:
        flat = a.reshape(-1)
        n = -(-flat.shape[0] // (_PACK_ROWS * 128)) * (_PACK_ROWS * 128)
        rows.append(jnp.pad(flat, (0, n - flat.shape[0])).reshape(-1, 128))
    total = sum(r.shape[0] for r in rows)
    if total % 16:
        rows.append(jnp.zeros((16 - total % 16, 128), F32))
    return jnp.concatenate(rows, axis=0)


def _unpack(p, shapes):
    out, r = [], 0
    for shp in shapes:
        n = math.prod(shp)
        nr = -(-n // (_PACK_ROWS * 128)) * _PACK_ROWS
        out.append(p[r:r + nr].reshape(-1)[:n].reshape(shp))
        r += nr
    return out


def kernel(x, mem, g_mix, w_in, ln_v_g, ln_v_b, w_s, b_s, conv_w, g_mem, w_kv, g_head, w_o, g_ffn, w_ffn1, w_ffn2, g_final, loss_target, m_g_mix, m_w_in, m_ln_v_g, m_ln_v_b, m_w_s, m_b_s, m_conv_w, m_g_mem, m_w_kv, m_g_head, m_w_o, m_g_ffn, m_w_ffn1, m_w_ffn2, m_g_final, v_g_mix, v_w_in, v_ln_v_g, v_ln_v_b, v_w_s, v_b_s, v_conv_w, v_g_mem, v_w_kv, v_g_head, v_w_o, v_g_ffn, v_w_ffn1, v_w_ffn2, v_g_final):
    sds = jax.ShapeDtypeStruct
    xi, yi = lax.axis_index("x"), lax.axis_index("y")
    shard = 2 * xi + yi
    x2d, mem2d, tgt = x[0], mem[0], loss_target[0]
    ws3, bs2 = w_s[0], b_s[0]
    g_final2 = g_final.reshape(1, D)
    dff4 = DFF // NCHIP
    din4 = DIN // NCHIP
    dcv4 = DC // NCHIP

    big = [w_in[0].T, w_kv[0], w_o[0], w_ffn1[0], w_ffn2[0]]
    big_names = ["w_in", "w_kv", "w_o", "w_ffn1", "w_ffn2"]
    slot = jnp.reshape(shard, (1,)).astype(jnp.int32)
    core = jnp.reshape(lax.axis_index("c"), (1,)).astype(jnp.int32)
    conv_pad = jnp.pad(conv_w[0], ((0, CONV_PAD[0] - 3), (0, CONV_PAD[1] - dcv4)))
    conv_slots = lax.dynamic_update_slice(jnp.zeros((NCHIP,) + CONV_PAD, F32), conv_pad[None], (shard, 0, 0))

    gather_ids = {"in": 16, "kvo": 17, "ffn1": 18, "ffn2": 19, "ffn2d": 20}

    def gather_start(bufs, after, nm, forwards=()):
        return _allgather_start(bufs, forwards, after, gather_ids[nm], name="ag_start_" + nm)

    def gather_wait(state, idx, after, nm):
        send, recv, bufs, _ = state
        got, token = _transfer_wait([send[k] for k in idx], [recv[k] for k in idx], [[bufs[k]] for k in idx],
                                    [(N_NEIGHBOUR_CHIPS, bufs[k].shape[1] // 2) for k in idx], after, name="ag_wait_" + nm)
        return [g[0] for g in got], token

    cast = lambda k, after: _cast_into_slot(big[k], slot, after, name="cast_" + big_names[k])
    ag_in = gather_start([cast(0, slot), conv_slots], slot, "in")
    bs_t = bs2.T

    h = _rms_fwd(x2d, g_mix, name="rms_mix", after=[ag_in[3]])
    mem_n = _rms_fwd(mem2d, g_mem, name="rms_mem", after=[h])
    kvo_b = [cast(1, mem_n)]
    kvo_b.append(cast(2, kvo_b[0]))
    w1_b = cast(3, kvo_b[1])
    w2_b = cast(4, w1_b)
    got_in, tok = gather_wait(ag_in, [0, 1], w2_b, "in")
    win4, conv4 = _forward_gathered(got_in, tok, name="ag_fwd_in")
    ag_kvo = gather_start(kvo_b, conv4, "kvo")
    w_in_t = win4.reshape(DIN, D)
    conv_full = conv4[:, :3, :dcv4].transpose(1, 0, 2).reshape(3, DC)
    NEAR, FAR = [0, 1], [2]

    def diagonal_wait(state, ks, after, nm):
        send, recv, bufs, _ = state
        got, token = _transfer_wait([send[k] for k in ks], [recv[k] for k in ks], [[bufs[k]] for k in ks],
                                    [(1, bufs[k].shape[1] // 2) for k in ks], after, name="ag_waitd_" + nm)
        return [g[0] for g in got], token

    proj_w = lambda tn, tk: pl.BlockSpec((tn, tk), lambda j, i, k, s: (s[j], k))
    proj_cols = lambda tm, tn: [pl.BlockSpec((tm, tn), lambda j, i, k, s: (i, s[j]))]
    proj_half = lambda which, into, after: _matmul(
        h, w_in_t, name="mm_proj_%d" % which, tb=True, M=S, N=DIN // 2, K=D, tn=DIN // 2, b_spec=proj_w,
        out_specs=proj_cols, outs=[sds((S, DIN), F32)], slots=jnp.full((1,), which, jnp.int32), into=into,
        after=after)[0]
    proj = proj_half(0, [], [ag_kvo[3]])
    got_kvo, tok = gather_wait(ag_kvo, [0, 1], proj, "kvo")
    ag_w1 = gather_start([w1_b], tok, "ffn1", forwards=got_kvo)
    kvo_n = _forward_halves(ag_w1[2][1:], NEAR, ag_w1[3], name="ag_fwdn_kvo")
    ag_w1 = (ag_w1[0], ag_w1[1], [ag_w1[2][0]] + list(kvo_n), ag_w1[3])
    proj = proj_half(1, [proj], list(kvo_n))
    kvo_d, tok = diagonal_wait(ag_w1, [1, 2], proj, "kvo")
    wkv4, wo4 = _forward_halves(kvo_d, FAR, tok, name="ag_fwdd_kvo")
    w_kv_full = wkv4.reshape(D, 2 * DM)
    w_o_full = wo4.reshape(D, D)
    (kv,) = _matmul(mem_n, w_kv_full, name="mm_kv", M=NMEM, N=2 * DM, K=D, outs=[sds((NMEM, 2 * DM), F32)])
    heads, hn, ycv = _mix_fwd(proj, kv, ws3, bs_t, ln_v_g, ln_v_b, conv_full, g_head, name="mix_fwd")
    def residual_and_norm(acc, res, g):
        x2v = acc + res
        r = lax.rsqrt(jnp.mean(x2v * x2v, axis=-1, keepdims=True) + EPS)
        return x2v, (x2v * r) * g

    row_vec = lambda tm, tn: pl.BlockSpec((1, tn), lambda j, i, k, *s: (0, j))
    x2, h2 = _matmul(hn, w_o_full, name="mm_wo", M=S, N=D, K=D, tn=D, n_split=1, epi=residual_and_norm,
                     outs=[sds((S, D), F32), sds((S, D), BF16)], extras=[(x2d, _tile_spec()), (g_ffn, row_vec)])
    near = jnp.stack([shard, 2 * (1 - xi) + yi, 2 * xi + (1 - yi)]).astype(jnp.int32)
    far = jnp.reshape(2 * (1 - xi) + (1 - yi), (1,)).astype(jnp.int32)

    w1_shard = lambda tn, tk: pl.BlockSpec((None, tk, tn), lambda j, i, k, s: (s[j], k, 0))
    act_cols = lambda tm, tn: [pl.BlockSpec((tm, tn), lambda j, i, k, s: (i, s[j]))] * 2

    def relu2(acc):
        r = jnp.maximum(acc, 0.0)
        return r * r, 2.0 * r

    got_w1, tok = gather_wait(ag_w1, [0], h2, "ffn1")
    ag_w2 = gather_start([w2_b], tok, "ffn2", forwards=got_w1)
    (w1n,) = _forward_halves([ag_w2[2][1]], NEAR, ag_w2[3], name="ag_fwdn_ffn1")
    ag_w2 = (ag_w2[0], ag_w2[1], [ag_w2[2][0], w1n], ag_w2[3])
    act, dact_df = _matmul(h2, w1n, name="mm_ffn1_near", M=S, N=3 * dff4, K=D, tn=dff4, b_spec=w1_shard,
                           out_specs=act_cols, outs=[sds((S, DFF), BF16)] * 2, epi=relu2, slots=near)
    w1d, tok = diagonal_wait(ag_w2, [1], act, "ffn1")
    (w14,) = _forward_halves(w1d, FAR, tok, name="ag_fwdd_ffn1")
    act, dact_df = _matmul(h2, w14, name="mm_ffn1_far", M=S, N=dff4, K=D, tn=dff4, b_spec=w1_shard,
                           out_specs=act_cols, outs=[sds((S, DFF), BF16)] * 2, epi=relu2, slots=far,
                           into=[act, dact_df])

    act_shard = lambda tm, tk: pl.BlockSpec((tm, tk), lambda j, i, k, s: (i, s[k]))
    w2_shard = lambda tn, tk: pl.BlockSpec((None, tk, tn), lambda j, i, k, s: (s[k], 0, j))
    got_w2, tok = gather_wait(ag_w2, [0], act, "ffn2")
    ag_w2d = gather_start([], tok, "ffn2d", forwards=got_w2)
    (w2n,) = _forward_halves(ag_w2d[2], NEAR, ag_w2d[3], name="ag_fwdn_ffn2")
    ag_w2d = (ag_w2d[0], ag_w2d[1], [w2n], ag_w2d[3])
    (x3,) = _matmul(act, w2n, name="mm_ffn2_near", M=S, N=D, K=3 * dff4, tm=2 * TM, tk=dff4,
                    a_spec=act_shard, b_spec=w2_shard, outs=[sds((S, D), F32)], epi=lambda acc, res: (acc + res,),
                    extras=[(x2, _tile_spec())], slots=near)
    w2d, tok = diagonal_wait(ag_w2d, [0], x3, "ffn2")
    (w24,) = _forward_halves(w2d, FAR, tok, name="ag_fwdd_ffn2")
    (x3,) = _matmul(act, w24, name="mm_ffn2_far", M=S, N=D, K=dff4, tm=2 * TM, tk=dff4, a_spec=act_shard,
                    b_spec=w2_shard, outs=[sds((S, D), F32)], epi=lambda acc, res: (acc + res,),
                    extras=[(x3, _tile_spec())], slots=far)
    w2_full = w24.reshape(DFF, D)

    ci = lax.axis_index("c")

    def rs_sibling(g4, nm):
        return _sibling_start([g4], False, 1 + big_names.index(nm), name="rs_sib_" + nm)

    def rs_chips(state, after, nm):
        send, recv, g4, land, _ = state
        (((land_, g4_),), _) = _transfer_wait(send, recv, [[land[0], g4[0]]], [(NCHIP, land[0].shape[1])], after,
                                             name="rs_sibwait_" + nm)
        part, buf = _select_half_bf16(g4_, ci, land_, slot, name="rs_add_" + nm)
        return _scatter_start([part], [buf], 1 + 2 * len(big_names) + big_names.index(nm), name="rs_start_" + nm)

    def rs_end(state, after, nm):
        send, recv, parts, bufs, _ = state
        (((buf, _),), _) = _transfer_wait(send, recv, [[bufs[0], parts[0]]], [(N_PEER_CHIPS, bufs[0].shape[1])], after,
                                          name="rs_wait_" + nm)
        return _sibling_start([buf], True, 1 + len(big_names) + big_names.index(nm), name="rs_share_" + nm)

    big_m = [m_w_in[0].T, m_w_kv[0], m_w_o[0], m_w_ffn1[0], m_w_ffn2[0]]
    big_v = [v_w_in[0].T, v_w_kv[0], v_w_o[0], v_w_ffn1[0], v_w_ffn2[0]]
    big_out = {}

    def rs_finish(k, state, after):
        send, recv, mine, land, _ = state
        nm = big_names[k]
        (((land_, mine_),), _) = _transfer_wait(send, recv, [[land[0], mine[0]]], [(NCHIP, land[0].shape[1])], after,
                                               name="rs_sharewait_" + nm)
        big_out[nm] = _adamw(big[k], mine_, land_, big_m[k], big_v[k], core, name="adamw_" + nm)
        return big_out[nm][1]

    dx3, dx3b, dg_final, loss11 = _loss_bwd(x3, g_final2, tgt, name="loss_bwd")
    (dw2,) = _matmul(act, dx3b, name="mm_dw2", ta=True, M=DFF, N=D, K=S, tn=D, outs=[sds((DFF, D), BF16)])
    sib_w2 = rs_sibling(dw2.reshape(NCHIP, dff4, D), "w_ffn2")
    (dfb,) = _matmul(dx3b, w2_full, name="mm_dact", tb=True, M=S, N=DFF, K=D, tn=dff4, outs=[sds((S, DFF), BF16)],
                     epi=lambda acc, g: (acc * g.astype(F32),), extras=[(dact_df, _tile_spec())],
                     after=[sib_w2[4]])
    rs_w2 = rs_chips(sib_w2, dfb, "w_ffn2")

    def dw1_out(tm, tn):
        nb = dff4 // tn
        return [pl.BlockSpec((None, tm, tn), lambda j, i, k: (j // nb, i, j % nb))]

    (dw1,) = _matmul(h2, dfb, name="mm_dw1", ta=True, M=D, N=DFF, K=S, tn=dff4, outs=[sds((NCHIP, D, dff4), BF16)],
                     out_specs=dw1_out, after=[rs_w2[4]])
    sib_w1 = rs_sibling(dw1, "w_ffn1")

    def w1_rows(tn, tk):
        kb = dff4 // tk
        return pl.BlockSpec((None, tn, tk), lambda j, i, k: (k // kb, j, k % kb))

    (dh2,) = _matmul(dfb, w14, name="mm_dh2", tb=True, M=S, N=D, K=DFF, tm=2 * TM, b_spec=w1_rows,
                     outs=[sds((S, D), F32)], after=[sib_w1[4]])
    rs_w1 = rs_chips(sib_w1, dh2, "w_ffn1")
    dx2, dx2b, dg_ffn = _rms_bwd(dh2, x2, g_ffn, dx3, name="rms_ffn_bwd", after=[rs_w1[4]])
    (dwo,) = _matmul(hn, dx2b, name="mm_dwo", ta=True, M=D, N=D, K=S, outs=[sds((D, D), BF16)])
    sib_wo = rs_sibling(dwo.reshape(NCHIP, D // NCHIP, D), "w_o")
    (dhn,) = _matmul(dx2b, w_o_full, name="mm_dhn", tb=True, M=S, N=D, K=D, outs=[sds((S, D), F32)],
                     after=[sib_wo[4]])
    rs_wo = rs_chips(sib_wo, dhn, "w_o")
    sh_w2 = rs_end(rs_w2, rs_wo[4], "w_ffn2")
    dproj, dkv, dws, dbs8, dlng, dlnb, dcw8, dgh = _mix_bwd(
        dhn, heads, proj, ycv, kv, ws3, bs_t, ln_v_g, ln_v_b, conv_full, g_head, sh_w2[4], name="mix_bwd")
    (dwin_t,) = _matmul(dproj, h, name="mm_dwin", ta=True, M=DIN, N=D, K=S, tm=DIN // 2, outs=[sds((DIN, D), BF16)])
    sib_win = rs_sibling(dwin_t.reshape(NCHIP, din4, D), "w_in")
    (dwkv,) = _matmul(mem_n, dkv, name="mm_dwkv", ta=True, M=D, N=2 * DM, K=NMEM, outs=[sds((D, 2 * DM), BF16)],
                      after=[sib_win[4]])
    sib_wkv = rs_sibling(dwkv.reshape(NCHIP, D // NCHIP, 2 * DM), "w_kv")
    (dh,) = _matmul(dproj, w_in_t, name="mm_dh", M=S, N=D, K=DIN, tk=DIN, outs=[sds((S, D), F32)],
                    after=[sib_wkv[4]])
    rs_win = rs_chips(sib_win, dh, "w_in")
    rs_wkv = rs_chips(sib_wkv, rs_win[4], "w_kv")
    dx, dg_mix = _rms_bwd(dh, x2d, g_mix, dx2, name="rms_mix_bwd", want_bf=False, after=[rs_wkv[4]])
    sh_w1 = rs_end(rs_w1, dx, "w_ffn1")
    (dmem_n,) = _matmul(dkv, w_kv_full, name="mm_dmem", tb=True, M=NMEM, N=D, K=2 * DM, outs=[sds((NMEM, D), F32)],
                        after=[sh_w1[4]])
    (dg_mem,) = _rms_bwd(dmem_n, mem2d, g_mem, None, name="rms_mem_bwd", want_dx=False)
    sh_wo = rs_end(rs_wo, dg_mem, "w_o")
    done = rs_finish(4, sh_w2, sh_wo[4])
    done = rs_finish(3, sh_w1, done)
    sh_win = rs_end(rs_win, done, "w_in")
    sh_wkv = rs_end(rs_wkv, sh_win[4], "w_kv")
    done = rs_finish(2, sh_wo, sh_wkv[4])
    done = rs_finish(0, sh_win, done)
    done = rs_finish(1, sh_wkv, done)

    small_names = ["g_mix", "ln_v_g", "ln_v_b", "w_s", "b_s", "conv_w", "g_mem", "g_head", "g_ffn", "g_final"]
    small_part = [dg_mix, dlng, dlnb, dws, dbs8[:, 0, :], dcw8[:3], dg_mem, dgh, dg_ffn, dg_final, loss11]
    small_shapes = [(1, D), (1, DS), (1, DS), (NSH, CHUNK, CHUNK), (NSH, CHUNK), (3, DC), (1, D), (1, D), (1, D), (1, D),
                    (1, 1)]
    total = _allreduce_small(_pack(small_part), done, name="allreduce_small")
    small_g = _unpack(total, small_shapes)
    loss = small_g.pop()[0, 0]
    small_g[5] = lax.dynamic_slice(small_g[5], (0, shard * dcv4), (3, dcv4))
    small_w = [g_mix, ln_v_g, ln_v_b, ws3, bs2, conv_w[0], g_mem, g_head, g_ffn, g_final2]
    small_m = [m_g_mix, m_ln_v_g, m_ln_v_b, m_w_s[0], m_b_s[0], m_conv_w[0], m_g_mem, m_g_head, m_g_ffn,
               m_g_final.reshape(1, D)]
    small_v = [v_g_mix, v_ln_v_g, v_ln_v_b, v_w_s[0], v_b_s[0], v_conv_w[0], v_g_mem, v_g_head, v_g_ffn,
               v_g_final.reshape(1, D)]
    s_delta, s_m, s_v = _adamw_small(small_w, small_g, small_m, small_v, name="adamw_small")
    small_out = {nm: (g, d, mn, vn) for nm, g, d, mn, vn in zip(small_names, small_g, s_delta, s_m, s_v)}

    order = ["g_mix", "w_in", "ln_v_g", "ln_v_b", "w_s", "b_s", "conv_w", "g_mem", "w_kv", "g_head", "w_o",
             "g_ffn", "w_ffn1", "w_ffn2", "g_final"]
    like = dict(g_mix=g_mix, w_in=w_in, ln_v_g=ln_v_g, ln_v_b=ln_v_b, w_s=w_s, b_s=b_s, conv_w=conv_w, g_mem=g_mem,
                w_kv=w_kv, g_head=g_head, w_o=w_o, g_ffn=g_ffn, w_ffn1=w_ffn1, w_ffn2=w_ffn2, g_final=g_final)
    res = {**big_out, **small_out}
    res["w_in"] = [a.T for a in res["w_in"]]
    outs = [loss, dx[None]]
    for k in range(4):
        outs += [res[nm][k].reshape(like[nm].shape) for nm in order]
    return tuple(outs)
```

```python
import math

import jax
import jax.numpy as jnp
from jax import lax
from jax.experimental import pallas as pl
from jax.experimental.pallas import tpu as pltpu

F32 = jnp.float32
BF16 = jnp.bfloat16
MESH = pl.DeviceIdType.MESH

D = 2048
S = 2048
HD = 128
NH = D // HD
NMH = 4
NSH = (NH - NMH) // 2
NCH = NH - NMH - NSH
DS = NSH * HD
DC = NCH * HD
DM = NMH * HD
DIN = 2 * DS + 3 * DC + DM
CHUNK = 128
NMEM = 256
DFF = 4 * D
EPS = 1e-6
NCHIP = 4
SCALE = HD ** -0.5

ADAM_LR = 0.001
ADAM_B1 = 0.9
ADAM_B2 = 0.999
ADAM_EPS = 1e-08
ADAM_WD = 0.01
ADAM_STEP = 10

TR_EW = 256
TR_MIX = 256
TM = 512
TN = 1024
TK = 2048
N_SUB = 512
VMEM_MB = 56
HALO = 8


def _pick(n, target, q=128):
    best = None
    for t in range(q, min(n, target) + 1, q):
        if n % t == 0:
            best = t
    return n if best is None else best


def _pick_rows(n, q):
    below = _pick(n, TR_EW, q)
    if 2 * below >= TR_EW:
        return below
    above = [t for t in range(TR_EW, min(n, 4 * TR_EW) + 1, q) if n % t == 0]
    return above[0] if above else below


def _cp(sem=None, vmem_mb=None, **kw):
    d = dict(kw)
    if sem is not None:
        d["dimension_semantics"] = sem
    if vmem_mb is not None:
        d["vmem_limit_bytes"] = vmem_mb << 20
    return pltpu.CompilerParams(**d)


def _gelu(x):
    z = 0.7978845608028654 * (x + 0.044715 * (x * x * x))
    return 0.5 * x * (1.0 + jnp.tanh(z))


def _gelu_with_grad(x):
    x2 = x * x
    t = jnp.tanh(0.7978845608028654 * (x + 0.044715 * (x2 * x)))
    half = 0.5 * (1.0 + t)
    return x * half, half + 0.5 * x * (1.0 - t * t) * (0.7978845608028654 * (1.0 + 3.0 * 0.044715 * x2))


def _matmul(a, b, *, name, ta=False, tb=False, M, N, K, tm=None, tn=None, tk=None, outs, epi=None,
            extras=(), a_spec=None, b_spec=None, out_specs=None, after=(), n_split=None, slots=None, into=()):
    n_after = len(after)
    tm = _pick(M, TM if tm is None else tm, 8)
    tn = _pick(N, TN if tn is None else tn)
    tk = _pick(K, TK if tk is None else tk)
    if n_split is None:
        n_split = tn // N_SUB if tn % N_SUB == 0 else 1
    nk = K // tk
    grid = (N // tn, M // tm, nk)
    if a_spec is None:
        a_spec = (pl.BlockSpec((tk, tm), lambda j, i, k, *s: (k, i)) if ta
                  else pl.BlockSpec((tm, tk), lambda j, i, k, *s: (i, k)))
    else:
        a_spec = a_spec(tm, tk)
    if b_spec is None:
        b_spec = (pl.BlockSpec((tn, tk), lambda j, i, k, *s: (j, k)) if tb
                  else pl.BlockSpec((tk, tn), lambda j, i, k, *s: (k, j)))
    else:
        b_spec = b_spec(tn, tk)
    if out_specs is None:
        out_specs = [pl.BlockSpec((tm, tn), lambda j, i, k, *s: (i, j)) for _ in outs]
    else:
        out_specs = out_specs(tm, tn)
    dn = (((0 if ta else 1,), (1 if tb else 0,)), ((), ()))
    n_ex, n_out = len(extras), len(outs)
    n_pre = 0 if slots is None else 1
    n_into = len(into)
    ns = tn // n_split

    def body(*refs):
        a_ref, b_ref = refs[n_pre], refs[n_pre + 1]
        ex = refs[n_pre + 2:n_pre + 2 + n_ex]
        first_out = n_pre + 2 + n_ex + n_after + n_into
        o = refs[first_out:first_out + n_out]
        acc = refs[first_out + n_out:]
        k = pl.program_id(2)

        def finish(val, cols):
            res = (val,) if epi is None else epi(val, *[e[:, cols] for e in ex])
            for r, o_ref in zip(res, o):
                o_ref[:, cols] = r.astype(o_ref.dtype)

        if nk > 1:
            @pl.when(k == 0)
            def _():
                acc[0][...] = jnp.zeros_like(acc[0])

        av = a_ref[...].astype(BF16)
        for q in range(n_split):
            cols = slice(q * ns, (q + 1) * ns)
            bq = (b_ref[cols, :] if tb else b_ref[:, cols]).astype(BF16)
            part = lax.dot_general(av, bq, dn, preferred_element_type=F32)
            if nk == 1:
                finish(part, cols)
            else:
                acc[0][:, cols] += part

        if nk > 1:
            @pl.when(k == nk - 1)
            def _():
                finish(acc[0][...], slice(0, tn))

    in_specs = ([a_spec, b_spec] + [sp(tm, tn) for _, sp in extras] + [ANY] * (n_after + n_into))
    scratch = [pltpu.VMEM((tm, tn), F32)] if nk > 1 else []
    args = [a, b] + [arr for arr, _ in extras] + list(after) + list(into)
    aliases = {n_pre + len(args) - n_into + t: t for t in range(n_into)}
    params = _cp(("parallel", "parallel", "arbitrary"), VMEM_MB)
    if slots is None:
        return pl.pallas_call(body, name=name, grid=grid, in_specs=in_specs, out_specs=out_specs, out_shape=outs,
                              scratch_shapes=scratch, input_output_aliases=aliases, compiler_params=params)(*args)
    return pl.pallas_call(
        body, name=name,
        grid_spec=pltpu.PrefetchScalarGridSpec(num_scalar_prefetch=1, grid=grid, in_specs=in_specs,
                                               out_specs=out_specs, scratch_shapes=scratch),
        out_shape=outs, input_output_aliases=aliases, compiler_params=params)(slots, *args)


def _tile_spec():
    return lambda tm, tn: pl.BlockSpec((tm, tn), lambda j, i, k, *s: (i, j))


def _cast_into_slot(w, slot, after, *, name):
    R, C = w.shape
    tr = _pick_rows(R, 16)

    def body(s_ref, w_ref, _after_ref, o_ref):
        o_ref[...] = w_ref[...].astype(BF16)

    return pl.pallas_call(
        body, name=name,
        grid_spec=pltpu.PrefetchScalarGridSpec(
            num_scalar_prefetch=1, grid=(R // tr,),
            in_specs=[pl.BlockSpec((tr, C), lambda i, s: (i, 0)), ANY],
            out_specs=pl.BlockSpec((None, tr, C), lambda i, s: (s[0], i, 0))),
        out_shape=jax.ShapeDtypeStruct((NCHIP, R, C), BF16),
        compiler_params=_cp(("parallel",), VMEM_MB),
    )(slot, w, after)


def _rms_fwd(x, g, *, name, after=()):
    R, C = x.shape
    tr = _pick(R, TR_EW, 16)
    n_after = len(after)

    def body(x_ref, g_ref, *rest):
        o_ref = rest[n_after]
        xv = x_ref[...]
        r = lax.rsqrt(jnp.mean(xv * xv, axis=-1, keepdims=True) + EPS)
        o_ref[...] = ((xv * r) * g_ref[...]).astype(BF16)

    return pl.pallas_call(
        body, name=name, grid=(R // tr,),
        in_specs=[pl.BlockSpec((tr, C), lambda i: (i, 0)), pl.BlockSpec((1, C), lambda i: (0, 0))] + [ANY] * n_after,
        out_specs=pl.BlockSpec((tr, C), lambda i: (i, 0)),
        out_shape=jax.ShapeDtypeStruct((R, C), BF16),
        compiler_params=_cp(("parallel",), VMEM_MB),
    )(x, g, *after)


def _rms_bwd(dh, x, g, dres, *, name, want_dx=True, want_bf=True, after=()):
    R, C = x.shape
    tr = _pick(R, TR_EW, 16)
    has_res = dres is not None
    row = pl.BlockSpec((tr, C), lambda i: (i, 0))
    vec = pl.BlockSpec((1, C), lambda i: (0, 0))

    def body(*refs):
        dh_ref, x_ref, g_ref = refs[:3]
        pos = 3
        dres_ref = None
        if has_res:
            dres_ref = refs[pos]
            pos += 1
        outs = refs[pos + len(after):]
        i = pl.program_id(0)
        xv = x_ref[...]
        r = lax.rsqrt(jnp.mean(xv * xv, axis=-1, keepdims=True) + EPS)
        xh = xv * r
        dhv = dh_ref[...]
        dg_ref = outs[-1]
        dgp = jnp.sum(dhv * xh, axis=0, keepdims=True)

        @pl.when(i == 0)
        def _():
            dg_ref[...] = dgp

        @pl.when(i > 0)
        def _():
            dg_ref[...] += dgp

        if want_dx:
            t = dhv * g_ref[...]
            dx = r * (t - xh * jnp.mean(t * xh, axis=-1, keepdims=True))
            if has_res:
                dx = dx + dres_ref[...]
            outs[0][...] = dx
            if want_bf:
                outs[1][...] = dx.astype(BF16)

    in_specs = [row, row, vec] + ([row] if has_res else []) + [ANY] * len(after)
    out_specs, out_shape = [], []
    if want_dx:
        out_specs.append(row)
        out_shape.append(jax.ShapeDtypeStruct((R, C), F32))
        if want_bf:
            out_specs.append(row)
            out_shape.append(jax.ShapeDtypeStruct((R, C), BF16))
    out_specs.append(vec)
    out_shape.append(jax.ShapeDtypeStruct((1, C), F32))
    args = [dh, x, g] + ([dres] if has_res else []) + list(after)
    return pl.pallas_call(
        body, name=name, grid=(R // tr,), in_specs=in_specs, out_specs=out_specs, out_shape=out_shape,
        compiler_params=_cp(("arbitrary",), VMEM_MB),
    )(*args)


def _loss_bwd(x3, g, tgt, *, name):
    R, C = x3.shape
    tr = _pick(R, TR_EW, 16)
    n = R // tr
    row = pl.BlockSpec((tr, C), lambda i: (i, 0))
    vec = pl.BlockSpec((1, C), lambda i: (0, 0))

    def body(x_ref, g_ref, t_ref, dx_ref, dxb_ref, dg_ref, loss_ref, acc_ref):
        i = pl.program_id(0)
        xv = x_ref[...]
        gv = g_ref[...]
        r = lax.rsqrt(jnp.mean(xv * xv, axis=-1, keepdims=True) + EPS)
        xh = xv * r
        e = xh * gv - t_ref[...]
        dy = e * (1.0 / C)
        sq = jnp.sum(e * e, axis=0, keepdims=True)
        dgp = jnp.sum(dy * xh, axis=0, keepdims=True)

        @pl.when(i == 0)
        def _():
            acc_ref[...] = sq
            dg_ref[...] = dgp

        @pl.when(i > 0)
        def _():
            acc_ref[...] += sq
            dg_ref[...] += dgp

        t = dy * gv
        dx = r * (t - xh * jnp.mean(t * xh, axis=-1, keepdims=True))
        dx_ref[...] = dx
        dxb_ref[...] = dx.astype(BF16)

        @pl.when(i == n - 1)
        def _():
            loss_ref[...] = jnp.sum(acc_ref[...], axis=-1, keepdims=True) * (0.5 / C)

    return pl.pallas_call(
        body, name=name, grid=(n,),
        in_specs=[row, vec, row],
        out_specs=[row, row, vec, pl.BlockSpec((1, 1), lambda i: (0, 0))],
        out_shape=[jax.ShapeDtypeStruct((R, C), F32), jax.ShapeDtypeStruct((R, C), BF16),
                   jax.ShapeDtypeStruct((1, C), F32), jax.ShapeDtypeStruct((1, 1), F32)],
        scratch_shapes=[pltpu.VMEM((1, C), F32)],
        compiler_params=_cp(("arbitrary",), VMEM_MB),
    )(x3, g, tgt)


def _offsets():
    u0 = 0
    v0 = DS
    b0 = 2 * DS
    c0 = b0 + DC
    x0 = c0 + DC
    q0 = x0 + DC
    return u0, v0, b0, c0, x0, q0


def _tri_mask(lower):
    r = lax.broadcasted_iota(jnp.int32, (CHUNK, CHUNK), 0)
    c = lax.broadcasted_iota(jnp.int32, (CHUNK, CHUNK), 1)
    return (r >= c) if lower else (c >= r)


def _layer_norm_stats(vg):
    mu = jnp.mean(vg, axis=-1, keepdims=True)
    vc = vg - mu
    rstd = lax.rsqrt(jnp.mean(vc * vc, axis=-1, keepdims=True) + EPS)
    return vc * rstd, rstd


def _softmax_rows(qh, kh):
    s = lax.dot_general(qh, kh, (((1,), (1,)), ((), ())), preferred_element_type=F32)
    m = jnp.max(s, axis=-1, keepdims=True)
    e = jnp.exp(s - m)
    return e / jnp.sum(e, axis=-1, keepdims=True)


def _mix_fwd(proj, kv, w_s, bs_t, ln_g, ln_b, conv_w, g_head, *, name):
    assert DS == DC
    tr = _pick(S, TR_MIX, CHUNK)
    n = S // tr
    nck = tr // CHUNK
    u0, v0, b0, c0, x0, q0 = _offsets()
    hb = tr // HALO

    def body(p_ref, cprev_ref, xprev_ref, kv_ref, ws_ref, bst_ref, lng_ref, lnb_ref, cw_ref, gh_ref,
             heads_ref, hn_ref, ycv_ref, buf_ref):
        i = pl.program_id(0)

        def emit(col, val):
            rs = lax.rsqrt(jnp.mean(val * val, axis=-1, keepdims=True) + EPS)
            heads_ref[:, col:col + HD] = val
            hn_ref[:, col:col + HD] = ((val * rs) * gh_ref[:, col:col + HD]).astype(BF16)

        vhat, _ = _layer_norm_stats(_gelu(p_ref[:, v0:v0 + DS]))
        vnb = (vhat * lng_ref[...] + lnb_ref[...]).astype(BF16)
        low = _tri_mask(True)
        for h in range(NSH):
            wt = jnp.where(low, ws_ref[h], 0.0).astype(BF16)
            bcol = bst_ref[:, h:h + 1]
            parts = []
            for c in range(nck):
                blk = vnb[c * CHUNK:(c + 1) * CHUNK, h * HD:(h + 1) * HD]
                parts.append(jnp.dot(wt, blk, preferred_element_type=F32) + bcol)
            mixed = parts[0] if nck == 1 else jnp.concatenate(parts, axis=0)
            emit(h * HD, _gelu(p_ref[:, u0 + h * HD:u0 + (h + 1) * HD]) * mixed)

        xc = p_ref[:, c0:c0 + DC] * p_ref[:, x0:x0 + DC]
        prev = cprev_ref[...] * xprev_ref[...]
        buf_ref[0:HALO, :] = jnp.where(i > 0, prev, 0.0)
        buf_ref[HALO:HALO + tr, :] = xc
        y = (cw_ref[2:3, :] * xc + cw_ref[1:2, :] * buf_ref[HALO - 1:HALO - 1 + tr, :]
             + cw_ref[0:1, :] * buf_ref[HALO - 2:HALO - 2 + tr, :])
        ycv_ref[...] = y
        cout = p_ref[:, b0:b0 + DC] * y
        for h in range(NCH):
            emit(DS + h * HD, cout[:, h * HD:(h + 1) * HD])

        for h in range(NMH):
            qh = (p_ref[:, q0 + h * HD:q0 + (h + 1) * HD] * SCALE).astype(BF16)
            kh = kv_ref[:, h * HD:(h + 1) * HD].astype(BF16)
            vh = kv_ref[:, DM + h * HD:DM + (h + 1) * HD].astype(BF16)
            p = _softmax_rows(qh, kh)
            emit(DS + DC + h * HD, jnp.dot(p.astype(BF16), vh, preferred_element_type=F32))

    full = lambda shape: pl.BlockSpec(shape, lambda i: (0,) * len(shape))
    halo_c = pl.BlockSpec((HALO, DC), lambda i: (jnp.maximum(i * hb - 1, 0), c0 // DC))
    halo_x = pl.BlockSpec((HALO, DC), lambda i: (jnp.maximum(i * hb - 1, 0), x0 // DC))
    return pl.pallas_call(
        body, name=name, grid=(n,),
        in_specs=[pl.BlockSpec((tr, DIN), lambda i: (i, 0)), halo_c, halo_x,
                  full((NMEM, 2 * DM)), full((NSH, CHUNK, CHUNK)), full((CHUNK, NSH)),
                  full((1, DS)), full((1, DS)), full((3, DC)), full((1, D))],
        out_specs=[pl.BlockSpec((tr, D), lambda i: (i, 0)), pl.BlockSpec((tr, D), lambda i: (i, 0)),
                   pl.BlockSpec((tr, DC), lambda i: (i, 0))],
        out_shape=[jax.ShapeDtypeStruct((S, D), F32), jax.ShapeDtypeStruct((S, D), BF16),
                   jax.ShapeDtypeStruct((S, DC), F32)],
        scratch_shapes=[pltpu.VMEM((tr + HALO, DC), F32)],
        compiler_params=_cp(("parallel",), VMEM_MB),
    )(proj, proj, proj, kv, w_s, bs_t, ln_g, ln_b, conv_w, g_head)


def _mix_bwd(dhn, heads, proj, ycv, kv, w_s, bs_t, ln_g, ln_b, conv_w, g_head, after, *, name):
    assert DS == DC
    tr = _pick(S, TR_MIX, CHUNK)
    n = S // tr
    nck = tr // CHUNK
    u0, v0, b0, c0, x0, q0 = _offsets()
    hb = tr // HALO
    last_hb = S // HALO - 1

    def body(dhn_ref, heads_ref, p_ref, ycv_ref, dhn_nx_ref, heads_nx_ref, b_nx_ref, kv_ref, ws_ref, bst_ref,
             lng_ref, lnb_ref, cw_ref, gh_ref, _after_ref,
             dp_ref, dkv_ref, dws_ref, dbs_ref, dlng_ref, dlnb_ref, dcw_ref, dgh_ref, buf_ref, dvn_ref):
        i = pl.program_id(0)

        @pl.when(i == 0)
        def _():
            dkv_ref[...] = jnp.zeros_like(dkv_ref)
            dws_ref[...] = jnp.zeros_like(dws_ref)
            dbs_ref[...] = jnp.zeros_like(dbs_ref)
            dlng_ref[...] = jnp.zeros_like(dlng_ref)
            dlnb_ref[...] = jnp.zeros_like(dlnb_ref)
            dcw_ref[...] = jnp.zeros_like(dcw_ref)
            dgh_ref[...] = jnp.zeros_like(dgh_ref)

        def head_bwd(a, dn, gh):
            rs = lax.rsqrt(jnp.mean(a * a, axis=-1, keepdims=True) + EPS)
            ah = a * rs
            t = dn * gh
            return rs * (t - ah * jnp.mean(t * ah, axis=-1, keepdims=True)), jnp.sum(dn * ah, axis=0, keepdims=True)

        def head_grad(col):
            da, dg = head_bwd(heads_ref[:, col:col + HD], dhn_ref[:, col:col + HD], gh_ref[:, col:col + HD])
            dgh_ref[:, col:col + HD] += dg
            return da

        vg, dvg_dv = _gelu_with_grad(p_ref[:, v0:v0 + DS])
        vhat, rstd = _layer_norm_stats(vg)
        vnb = (vhat * lng_ref[...] + lnb_ref[...]).astype(BF16)
        low = _tri_mask(True)
        ones = jnp.ones((HALO, HD), BF16)
        for h in range(NSH):
            w_h = ws_ref[h]
            wt = jnp.where(low, w_h, 0.0).astype(BF16)
            bcol = bst_ref[:, h:h + 1]
            da = head_grad(h * HD)
            ug, dug_du = _gelu_with_grad(p_ref[:, u0 + h * HD:u0 + (h + 1) * HD])
            dws = jnp.zeros((CHUNK, CHUNK), F32)
            dbs = jnp.zeros((HALO, CHUNK), F32)
            mixed_parts = []
            for c in range(nck):
                rows = slice(c * CHUNK, (c + 1) * CHUNK)
                blk = vnb[rows, h * HD:(h + 1) * HD]
                mixed_parts.append(jnp.dot(wt, blk, preferred_element_type=F32) + bcol)
                dmb = (da[rows] * ug[rows]).astype(BF16)
                dws = dws + lax.dot_general(dmb, blk, (((1,), (1,)), ((), ())), preferred_element_type=F32)
                dbs = dbs + lax.dot_general(ones, dmb, (((1,), (1,)), ((), ())), preferred_element_type=F32)
                dvn_ref[c * CHUNK:(c + 1) * CHUNK, h * HD:(h + 1) * HD] = lax.dot_general(
                    wt, dmb, (((0,), (0,)), ((), ())), preferred_element_type=F32)
            mixed = mixed_parts[0] if nck == 1 else jnp.concatenate(mixed_parts, axis=0)
            dp_ref[:, u0 + h * HD:u0 + (h + 1) * HD] = ((da * mixed) * dug_du).astype(BF16)
            dws_ref[h] += jnp.where(low, dws, 0.0)
            dbs_ref[h] += dbs
        dvn = dvn_ref[...]
        dlng_ref[...] += jnp.sum(dvn * vhat, axis=0, keepdims=True)
        dlnb_ref[...] += jnp.sum(dvn, axis=0, keepdims=True)
        dvh = dvn * lng_ref[...]
        dvg = rstd * (dvh - jnp.mean(dvh, axis=-1, keepdims=True)
                      - vhat * jnp.mean(dvh * vhat, axis=-1, keepdims=True))
        dp_ref[:, v0:v0 + DS] = (dvg * dvg_dv).astype(BF16)

        dc = jnp.concatenate([head_grad(DS + h * HD) for h in range(NCH)], axis=1)
        dc_nx = jnp.concatenate(
            [head_bwd(heads_nx_ref[:, h * HD:(h + 1) * HD], dhn_nx_ref[:, h * HD:(h + 1) * HD],
                      gh_ref[:, DS + h * HD:DS + (h + 1) * HD])[0] for h in range(NCH)], axis=1)
        bg = p_ref[:, b0:b0 + DC]
        cg = p_ref[:, c0:c0 + DC]
        xin = p_ref[:, x0:x0 + DC]
        dp_ref[:, b0:b0 + DC] = (dc * ycv_ref[...]).astype(BF16)
        dyv = dc * bg
        buf_ref[0:tr, :] = dyv
        buf_ref[tr:tr + HALO, :] = jnp.where(i < n - 1, dc_nx * b_nx_ref[...], 0.0)
        sh1 = buf_ref[1:1 + tr, :]
        sh0 = buf_ref[2:2 + tr, :]
        dxc = cw_ref[2:3, :] * dyv + cw_ref[1:2, :] * sh1 + cw_ref[0:1, :] * sh0
        xc = cg * xin
        dp_ref[:, c0:c0 + DC] = (dxc * xin).astype(BF16)
        dp_ref[:, x0:x0 + DC] = (dxc * cg).astype(BF16)
        dcw_ref[0:1, :] += jnp.sum(sh0 * xc, axis=0, keepdims=True)
        dcw_ref[1:2, :] += jnp.sum(sh1 * xc, axis=0, keepdims=True)
        dcw_ref[2:3, :] += jnp.sum(dyv * xc, axis=0, keepdims=True)

        for h in range(NMH):
            do = head_grad(DS + DC + h * HD).astype(BF16)
            qh = (p_ref[:, q0 + h * HD:q0 + (h + 1) * HD] * SCALE).astype(BF16)
            kh = kv_ref[:, h * HD:(h + 1) * HD].astype(BF16)
            vh = kv_ref[:, DM + h * HD:DM + (h + 1) * HD].astype(BF16)
            p = _softmax_rows(qh, kh)
            dpr = lax.dot_general(do, vh, (((1,), (1,)), ((), ())), preferred_element_type=F32)
            ds = (p * (dpr - jnp.sum(dpr * p, axis=-1, keepdims=True))).astype(BF16)
            dp_ref[:, q0 + h * HD:q0 + (h + 1) * HD] = (
                jnp.dot(ds, kh, preferred_element_type=F32) * SCALE).astype(BF16)
            dkv_ref[:, h * HD:(h + 1) * HD] += lax.dot_general(
                ds, qh, (((0,), (0,)), ((), ())), preferred_element_type=F32)
            dkv_ref[:, DM + h * HD:DM + (h + 1) * HD] += lax.dot_general(
                p.astype(BF16), do, (((0,), (0,)), ((), ())), preferred_element_type=F32)

    full = lambda shape: pl.BlockSpec(shape, lambda i: (0,) * len(shape))
    row = lambda c: pl.BlockSpec((tr, c), lambda i: (i, 0))
    nxt = lambda col: pl.BlockSpec((HALO, DC), lambda i: (jnp.minimum((i + 1) * hb, last_hb), col))
    return pl.pallas_call(
        body, name=name, grid=(n,),
        in_specs=[row(D), row(D), row(DIN), row(DC), nxt(DS // DC), nxt(DS // DC), nxt(b0 // DC),
                  full((NMEM, 2 * DM)), full((NSH, CHUNK, CHUNK)), full((CHUNK, NSH)),
                  full((1, DS)), full((1, DS)), full((3, DC)), full((1, D)), ANY],
        out_specs=[row(DIN), full((NMEM, 2 * DM)), full((NSH, CHUNK, CHUNK)), full((NSH, HALO, CHUNK)),
                   full((1, DS)), full((1, DS)), full((HALO, DC)), full((1, D))],
        out_shape=[jax.ShapeDtypeStruct((S, DIN), BF16), jax.ShapeDtypeStruct((NMEM, 2 * DM), F32),
                   jax.ShapeDtypeStruct((NSH, CHUNK, CHUNK), F32), jax.ShapeDtypeStruct((NSH, HALO, CHUNK), F32),
                   jax.ShapeDtypeStruct((1, DS), F32), jax.ShapeDtypeStruct((1, DS), F32),
                   jax.ShapeDtypeStruct((HALO, DC), F32), jax.ShapeDtypeStruct((1, D), F32)],
        scratch_shapes=[pltpu.VMEM((tr + HALO, DC), F32), pltpu.VMEM((tr, DS), F32)],
        compiler_params=_cp(("arbitrary",), VMEM_MB),
    )(dhn, heads, proj, ycv, dhn, heads, proj, kv, w_s, bs_t, ln_g, ln_b, conv_w, g_head, after)


def _place():
    x, y, c = lax.axis_index("x"), lax.axis_index("y"), lax.axis_index("c")
    chips = [(1 - x, y), (x, 1 - y), (1 - x, 1 - y)]
    return x, y, c, chips


ANY = pl.BlockSpec(memory_space=pl.ANY)


HBM = pl.BlockSpec(memory_space=pltpu.HBM)
SEM = pl.BlockSpec(memory_space=pltpu.SEMAPHORE)
EFFECT = pltpu.SideEffectType.DATAFLOW_SIDE_EFFECTING
N_PEER_CHIPS = 3
N_NEIGHBOUR_CHIPS = 2
CONV_PAD = (32, 256)


def _in_hbm(a):
    return pltpu.with_memory_space_constraint(a, pltpu.HBM)


def _allgather_start(bufs, forwards, after, collective_id, *, name):
    arrs = list(bufs) + list(forwards)
    nw, nb = len(arrs), len(bufs)

    def body(*refs):
        ins, send, recv = refs[:nw], refs[nw + 1:2 * nw + 1], refs[2 * nw + 1:3 * nw + 1]
        token = refs[4 * nw + 1]
        x, y, c, chips = _place()
        s = 2 * x + y
        slots = [2 * cx + cy for cx, cy in chips]
        _handshake([(cx, cy, c) for cx, cy in chips[:N_NEIGHBOUR_CHIPS]])
        for w in range(nb, nw):
            q = arrs[w].shape[1] // 4
            for j in range(N_NEIGHBOUR_CHIPS):
                rows = ins[w].at[slots[j], pl.ds(c * 2 * q + j * q, q)]
                pltpu.make_async_remote_copy(src_ref=rows, dst_ref=rows, send_sem=send[w], recv_sem=recv[w],
                                             device_id=(*chips[1 - j], c), device_id_type=MESH).start()
        for w in range(nb):
            hr = arrs[w].shape[1] // 2
            rows = ins[w].at[s, pl.ds(c * hr, hr)]
            for cx, cy in chips[:N_NEIGHBOUR_CHIPS]:
                pltpu.make_async_remote_copy(src_ref=rows, dst_ref=rows, send_sem=send[w], recv_sem=recv[w],
                                             device_id=(cx, cy, c), device_id_type=MESH).start()
        token[...] = jnp.zeros_like(token)

    res = pl.pallas_call(
        body, name=name,
        in_specs=[HBM] * nw + [ANY],
        out_specs=[SEM] * (2 * nw) + [HBM] * nw + [pl.BlockSpec(memory_space=pltpu.VMEM)],
        out_shape=[pltpu.SemaphoreType.DMA(())] * (2 * nw) + [pltpu.HBM(a.shape, a.dtype) for a in arrs]
        + [jax.ShapeDtypeStruct((8, 128), F32)],
        input_output_aliases={w: 2 * nw + w for w in range(nw)},
        compiler_params=pltpu.CompilerParams(has_side_effects=EFFECT, collective_id=collective_id),
    )(*[_in_hbm(a) for a in arrs], after)
    return res[:nw], res[nw:2 * nw], res[2 * nw:3 * nw], res[3 * nw]


def _handshake(peers):
    barrier = pltpu.get_barrier_semaphore()
    for peer in peers:
        pl.semaphore_signal(barrier, inc=1, device_id=peer, device_id_type=MESH)
    pl.semaphore_wait(barrier, len(peers))


def _scatter_start(parts, bufs, collective_id, *, name):
    nw = len(parts)

    def body(*refs):
        src, dst = refs[:nw], refs[nw:2 * nw]
        send, recv = refs[2 * nw:3 * nw], refs[3 * nw:4 * nw]
        token = refs[6 * nw]
        x, y, c, chips = _place()
        s = 2 * x + y
        _handshake([(cx, cy, c) for cx, cy in chips])
        for w in range(nw):
            for cx, cy in chips:
                pltpu.make_async_remote_copy(src_ref=src[w].at[2 * cx + cy], dst_ref=dst[w].at[s], send_sem=send[w],
                                             recv_sem=recv[w], device_id=(cx, cy, c), device_id_type=MESH).start()
        token[...] = jnp.zeros_like(token)

    res = pl.pallas_call(
        body, name=name,
        in_specs=[HBM] * (2 * nw),
        out_specs=[SEM] * (2 * nw) + [HBM] * (2 * nw) + [pl.BlockSpec(memory_space=pltpu.VMEM)],
        out_shape=[pltpu.SemaphoreType.DMA(())] * (2 * nw) + [pltpu.HBM(a.shape, a.dtype) for a in parts + bufs]
        + [jax.ShapeDtypeStruct((8, 128), F32)],
        input_output_aliases={k: 2 * nw + k for k in range(2 * nw)},
        compiler_params=pltpu.CompilerParams(has_side_effects=EFFECT, collective_id=collective_id),
    )(*[_in_hbm(a) for a in parts + bufs])
    return res[:nw], res[nw:2 * nw], res[2 * nw:3 * nw], res[3 * nw:4 * nw], res[4 * nw]


def _sibling_start(srcs, whole, collective_id, *, name):
    nw = len(srcs)
    lands = [lax.empty((a.shape[0], a.shape[1] if whole else a.shape[1] // 2, a.shape[2]), a.dtype) for a in srcs]

    def body(*refs):
        src, land = refs[:nw], refs[nw:2 * nw]
        send, recv = refs[2 * nw:3 * nw], refs[3 * nw:4 * nw]
        token = refs[6 * nw]
        x, y, c, _ = _place()
        _handshake([(x, y, 1 - c)])
        for w in range(nw):
            hr = srcs[w].shape[1] // 2
            rows = src[w] if whole else src[w].at[:, pl.ds((1 - c) * hr, hr)]
            pltpu.make_async_remote_copy(src_ref=rows, dst_ref=land[w], send_sem=send[w], recv_sem=recv[w],
                                         device_id=(x, y, 1 - c), device_id_type=MESH).start()
        token[...] = jnp.zeros_like(token)

    res = pl.pallas_call(
        body, name=name,
        in_specs=[HBM] * (2 * nw),
        out_specs=[SEM] * (2 * nw) + [HBM] * (2 * nw) + [pl.BlockSpec(memory_space=pltpu.VMEM)],
        out_shape=[pltpu.SemaphoreType.DMA(())] * (2 * nw) + [pltpu.HBM(a.shape, a.dtype) for a in srcs + lands]
        + [jax.ShapeDtypeStruct((8, 128), F32)],
        input_output_aliases={k: 2 * nw + k for k in range(2 * nw)},
        compiler_params=pltpu.CompilerParams(has_side_effects=EFFECT, collective_id=collective_id),
    )(*[_in_hbm(a) for a in srcs + lands])
    return res[:nw], res[nw:2 * nw], res[2 * nw:3 * nw], res[3 * nw:4 * nw], res[4 * nw]


def _transfer_wait(sends, recvs, thru, sizes, after, *, name):
    n = len(sends)
    flat = [a for group in thru for a in group]

    def body(*refs):
        bufs = refs[:len(flat)]
        send = refs[len(flat):len(flat) + n]
        recv = refs[len(flat) + n:len(flat) + 2 * n]
        token = refs[2 * len(flat) + 2 * n + 1]
        token[...] = jnp.zeros_like(token)
        x, y, c, _ = _place()
        pos = 0
        for k in range(n):
            slots, rows = sizes[k]
            region = bufs[pos].at[pl.ds(0, slots), pl.ds(0, rows)]
            pos += len(thru[k])
            cp = pltpu.make_async_remote_copy(src_ref=region, dst_ref=region, send_sem=send[k], recv_sem=recv[k],
                                              device_id=(x, y, 1 - c), device_id_type=MESH)
            cp.wait_send()
            cp.wait_recv()

    res = pl.pallas_call(
        body, name=name,
        in_specs=[HBM] * len(flat) + [SEM] * (2 * n) + [pl.BlockSpec(memory_space=pl.ANY)],
        out_specs=[HBM] * len(flat) + [pl.BlockSpec(memory_space=pltpu.VMEM)],
        out_shape=[pltpu.HBM(a.shape, a.dtype) for a in flat] + [jax.ShapeDtypeStruct((8, 128), F32)],
        input_output_aliases={k: k for k in range(len(flat))},
        compiler_params=pltpu.CompilerParams(has_side_effects=EFFECT),
    )(*flat, *sends, *recvs, after)
    out, pos = [], 0
    for group in thru:
        out.append(res[pos:pos + len(group)])
        pos += len(group)
    return out, res[len(flat)]


def _forward_gathered(bufs, after, *, name):
    nw = len(bufs)

    def body(*refs):
        outs = refs[nw + 1:2 * nw + 1]
        d_send, d_recv, i_send, i_recv = refs[2 * nw + 1:]
        x, y, c, chips = _place()
        me, sibling = (x, y, c), (x, y, 1 - c)
        slots = [2 * cx + cy for cx, cy in chips]

        def rows(w, j, start, n):
            return outs[w].at[slots[j], pl.ds(start, n)]

        def d2d(w, j, which, to):
            hr = bufs[w].shape[1] // 2
            r = rows(w, j, which * hr, hr)
            return pltpu.make_async_remote_copy(
                src_ref=r, dst_ref=r, send_sem=d_send.at[N_PEER_CHIPS * w + j],
                recv_sem=d_recv.at[N_PEER_CHIPS * w + j], device_id=to, device_id_type=MESH)

        def ici(w, j, slot_j, to):
            q = bufs[w].shape[1] // 4
            r = rows(w, slot_j, c * 2 * q + j * q, q)
            return pltpu.make_async_remote_copy(
                src_ref=r, dst_ref=r, send_sem=i_send.at[N_NEIGHBOUR_CHIPS * w + j],
                recv_sem=i_recv.at[N_NEIGHBOUR_CHIPS * w + j], device_id=to, device_id_type=MESH)

        started = []
        for w in range(nw):
            started += [ici(w, 0, 0, (*chips[1], c)), ici(w, 1, 1, (*chips[0], c))]
            started += [d2d(w, j, c, sibling) for j in range(N_NEIGHBOUR_CHIPS)]
        for cp in started:
            cp.start()
        diag = N_PEER_CHIPS - 1
        for w in range(nw):
            for j in range(N_NEIGHBOUR_CHIPS):
                ici(w, j, diag, me).wait_recv()
            cp = d2d(w, diag, c, sibling)
            cp.start()
            started.append(cp)
        for w in range(nw):
            for j in range(N_PEER_CHIPS):
                d2d(w, j, 1 - c, me).wait_recv()
        for cp in started:
            cp.wait_send()

    return pl.pallas_call(
        body, name=name,
        in_specs=[ANY] * (nw + 1), out_specs=[ANY] * nw,
        out_shape=[jax.ShapeDtypeStruct(a.shape, a.dtype) for a in bufs],
        input_output_aliases={w: w for w in range(nw)},
        scratch_shapes=[pltpu.SemaphoreType.DMA((N_PEER_CHIPS * nw,)), pltpu.SemaphoreType.DMA((N_PEER_CHIPS * nw,)),
                        pltpu.SemaphoreType.DMA((N_NEIGHBOUR_CHIPS * nw,)),
                        pltpu.SemaphoreType.DMA((N_NEIGHBOUR_CHIPS * nw,))],
    )(*bufs, after)


def _forward_halves(bufs, which, after, *, name):
    nw = len(bufs)
    n = len(which)

    def body(*refs):
        outs = refs[nw + 1:2 * nw + 1]
        send, recv = refs[2 * nw + 1:]
        x, y, c, chips = _place()
        me, sibling = (x, y, c), (x, y, 1 - c)

        def d2d(w, t, half, to):
            cx, cy = chips[which[t]]
            hr = bufs[w].shape[1] // 2
            rows = outs[w].at[2 * cx + cy, pl.ds(half * hr, hr)]
            return pltpu.make_async_remote_copy(src_ref=rows, dst_ref=rows, send_sem=send.at[n * w + t],
                                                recv_sem=recv.at[n * w + t], device_id=to, device_id_type=MESH)

        passed = [d2d(w, t, c, sibling) for w in range(nw) for t in range(n)]
        for cp in passed:
            cp.start()
        for w in range(nw):
            for t in range(n):
                d2d(w, t, 1 - c, me).wait_recv()
        for cp in passed:
            cp.wait_send()

    return pl.pallas_call(
        body, name=name,
        in_specs=[ANY] * (nw + 1), out_specs=[ANY] * nw,
        out_shape=[jax.ShapeDtypeStruct(a.shape, a.dtype) for a in bufs],
        input_output_aliases={w: w for w in range(nw)},
        scratch_shapes=[pltpu.SemaphoreType.DMA((n * nw,)), pltpu.SemaphoreType.DMA((n * nw,))],
    )(*bufs, after)


def _allreduce_small(p, after, *, name):
    R = p.shape[0]
    hr = R // 2

    def body(p_ref, _after_ref, out_ref, sib_ref, sum_ref, gat_ref, tot_ref, send, recv):
        x, y, c, chips = _place()
        s = 2 * x + y
        sibling = (x, y, 1 - c)
        rows = pl.ds(pl.multiple_of(c * hr, 8), hr)
        swap = pltpu.make_async_remote_copy(src_ref=p_ref, dst_ref=sib_ref, send_sem=send.at[0], recv_sem=recv.at[0],
                                            device_id=sibling, device_id_type=MESH)
        swap.start()
        swap.wait()
        sum_ref[...] = p_ref[...] + sib_ref[...]
        gat_ref[s] = sum_ref[rows, :]
        cps = [pltpu.make_async_remote_copy(src_ref=sum_ref.at[rows], dst_ref=gat_ref.at[s], send_sem=send.at[1 + j],
                                            recv_sem=recv.at[1 + j], device_id=(cx, cy, c), device_id_type=MESH)
               for j, (cx, cy) in enumerate(chips)]
        for cp in cps:
            cp.start()
        for cp in cps:
            cp.wait()
        tot_ref[...] = ((gat_ref[0] + gat_ref[1]) + gat_ref[2]) + gat_ref[3]
        out_ref[rows, :] = tot_ref[...]
        share = pltpu.make_async_remote_copy(src_ref=tot_ref, dst_ref=out_ref.at[rows], send_sem=send.at[4],
                                             recv_sem=recv.at[4], device_id=sibling, device_id_type=MESH)
        share.start()
        share.wait_send()
        other = out_ref.at[pl.ds(pl.multiple_of((1 - c) * hr, 8), hr)]
        pltpu.make_async_remote_copy(src_ref=other, dst_ref=other, send_sem=send.at[4], recv_sem=recv.at[4],
                                     device_id=(x, y, c), device_id_type=MESH).wait_recv()

    vmem = pl.BlockSpec(memory_space=pltpu.VMEM)
    return pl.pallas_call(
        body, name=name, in_specs=[vmem, ANY], out_specs=vmem,
        out_shape=jax.ShapeDtypeStruct((R, 128), F32),
        scratch_shapes=[pltpu.VMEM((R, 128), F32), pltpu.VMEM((R, 128), F32), pltpu.VMEM((NCHIP, hr, 128), F32),
                        pltpu.VMEM((hr, 128), F32), pltpu.SemaphoreType.DMA((5,)), pltpu.SemaphoreType.DMA((5,))],
    )(p, after)


def _select_half_bf16(g, half, add, slot, *, name):
    _, R, C = g.shape
    hr = R // 2
    tr = _pick_rows(hr, 16)
    nb = hr // tr
    sel = jnp.concatenate([jnp.reshape(half, (1,)).astype(jnp.int32), slot])

    def body(s_ref, g_ref, a_ref, o_ref, own_ref):
        val = (g_ref[...].astype(F32) + a_ref[...].astype(F32)).astype(BF16)
        o_ref[...] = val

        @pl.when(pl.program_id(1) == s_ref[1])
        def _():
            own_ref[...] = val

    g_spec = pl.BlockSpec((None, tr, C), lambda i, j, s: (j, s[0] * nb + i, 0))
    o_spec = pl.BlockSpec((None, tr, C), lambda i, j, s: (j, i, 0))
    own_spec = pl.BlockSpec((None, tr, C), lambda i, j, s: (s[1], i, 0))
    shape = jax.ShapeDtypeStruct((NCHIP, hr, C), BF16)
    return pl.pallas_call(
        body, name=name,
        grid_spec=pltpu.PrefetchScalarGridSpec(
            num_scalar_prefetch=1, grid=(nb, NCHIP), in_specs=[g_spec, o_spec], out_specs=[o_spec, own_spec]),
        out_shape=[shape, shape],
        compiler_params=_cp(("parallel", "arbitrary"), VMEM_MB),
    )(sel, g, add)


def _adamw_math(w, g, m, v):
    m = ADAM_B1 * m + (1.0 - ADAM_B1) * g
    v = ADAM_B2 * v + (1.0 - ADAM_B2) * (g * g)
    m_hat = m / (1.0 - ADAM_B1 ** ADAM_STEP)
    v_hat = v / (1.0 - ADAM_B2 ** ADAM_STEP)
    delta = -ADAM_LR * (m_hat / (jnp.sqrt(v_hat) + ADAM_EPS) + ADAM_WD * w)
    return delta, m, v


def _adamw(w, g_mine, g_sib, m, v, core, *, name):
    R, C = w.shape
    hr = R // 2
    tr = _pick_rows(hr, 16)
    nb = hr // tr
    row = pl.BlockSpec((tr, C), lambda hh, i, c: (hh * nb + i, 0))
    mine = pl.BlockSpec((NCHIP, tr, C), lambda hh, i, c: (0, jnp.where(hh == c[0], i, 0), 0))
    sibs = pl.BlockSpec((NCHIP, tr, C), lambda hh, i, c: (0, jnp.where(hh == c[0], 0, i), 0))

    def slot_sum(ref):
        acc = ref[0].astype(F32) + ref[1].astype(F32)
        for j in range(2, NCHIP):
            acc = acc + ref[j].astype(F32)
        return acc

    def body(c_ref, w_ref, gm_ref, gs_ref, m_ref, v_ref, go_ref, d_ref, mo_ref, vo_ref):
        gv = jnp.where(pl.program_id(0) == c_ref[0], slot_sum(gm_ref), slot_sum(gs_ref))
        d, mn, vn = _adamw_math(w_ref[...], gv, m_ref[...], v_ref[...])
        go_ref[...] = gv
        d_ref[...] = d
        mo_ref[...] = mn
        vo_ref[...] = vn

    return pl.pallas_call(
        body, name=name,
        grid_spec=pltpu.PrefetchScalarGridSpec(
            num_scalar_prefetch=1, grid=(2, nb),
            in_specs=[row, mine, sibs, row, row], out_specs=[row] * 4),
        out_shape=[jax.ShapeDtypeStruct((R, C), F32)] * 4,
        compiler_params=_cp(("parallel", "parallel"), VMEM_MB),
    )(core, w, g_mine, g_sib, m, v)


def _adamw_small(ws, gs, ms, vs, *, name):
    n = len(ws)

    def body(*refs):
        w_r, g_r, m_r, v_r = refs[:n], refs[n:2 * n], refs[2 * n:3 * n], refs[3 * n:4 * n]
        d_r, mo_r, vo_r = refs[4 * n:5 * n], refs[5 * n:6 * n], refs[6 * n:7 * n]
        for k in range(n):
            d, mn, vn = _adamw_math(w_r[k][...], g_r[k][...], m_r[k][...], v_r[k][...])
            d_r[k][...] = d
            mo_r[k][...] = mn
            vo_r[k][...] = vn

    shapes = [jax.ShapeDtypeStruct(w.shape, F32) for w in ws]
    res = pl.pallas_call(body, name=name, out_shape=shapes * 3)(*ws, *gs, *ms, *vs)
    return res[:n], res[n:2 * n], res[2 * n:]


_PACK_ROWS = 8


def _pack(parts):
    rows = []
    for a in parts:
        flat = a.reshape(-1)
        n = -(-flat.shape[0] // (_PACK_ROWS * 128)) * (_PACK_ROWS * 128)
        rows.append(jnp.pad(flat, (0, n - flat.shape[0])).reshape(-1, 128))
    total = sum(r.shape[0] for r in rows)
    if total % 16:
        rows.append(jnp.zeros((16 - total % 16, 128), F32))
    return jnp.concatenate(rows, axis=0)


def _unpack(p, shapes):
    out, r = [], 0
    for shp in shapes:
        n = math.prod(shp)
        nr = -(-n // (_PACK_ROWS * 128)) * _PACK_ROWS
        out.append(p[r:r + nr].reshape(-1)[:n].reshape(shp))
        r += nr
    return out


def kernel(x, mem, g_mix, w_in, ln_v_g, ln_v_b, w_s, b_s, conv_w, g_mem, w_kv, g_head, w_o, g_ffn, w_ffn1, w_ffn2, g_final, loss_target, m_g_mix, m_w_in, m_ln_v_g, m_ln_v_b, m_w_s, m_b_s, m_conv_w, m_g_mem, m_w_kv, m_g_head, m_w_o, m_g_ffn, m_w_ffn1, m_w_ffn2, m_g_final, v_g_mix, v_w_in, v_ln_v_g, v_ln_v_b, v_w_s, v_b_s, v_conv_w, v_g_mem, v_w_kv, v_g_head, v_w_o, v_g_ffn, v_w_ffn1, v_w_ffn2, v_g_final):
    sds = jax.ShapeDtypeStruct
    xi, yi = lax.axis_index("x"), lax.axis_index("y")
    shard = 2 * xi + yi
    x2d, mem2d, tgt = x[0], mem[0], loss_target[0]
    ws3, bs2 = w_s[0], b_s[0]
    g_final2 = g_final.reshape(1, D)
    dff4 = DFF // NCHIP
    din4 = DIN // NCHIP
    dcv4 = DC // NCHIP

    big = [w_in[0].T, w_kv[0], w_o[0], w_ffn1[0], w_ffn2[0]]
    big_names = ["w_in", "w_kv", "w_o", "w_ffn1", "w_ffn2"]
    slot = jnp.reshape(shard, (1,)).astype(jnp.int32)
    core = jnp.reshape(lax.axis_index("c"), (1,)).astype(jnp.int32)
    conv_pad = jnp.pad(conv_w[0], ((0, CONV_PAD[0] - 3), (0, CONV_PAD[1] - dcv4)))
    conv_slots = lax.dynamic_update_slice(jnp.zeros((NCHIP,) + CONV_PAD, F32), conv_pad[None], (shard, 0, 0))

    gather_ids = {"in": 16, "kvo": 17, "ffn1": 18, "ffn2": 19, "ffn2d": 20}

    def gather_start(bufs, after, nm, forwards=()):
        return _allgather_start(bufs, forwards, after, gather_ids[nm], name="ag_start_" + nm)

    def gather_wait(state, idx, after, nm):
        send, recv, bufs, _ = state
        got, token = _transfer_wait([send[k] for k in idx], [recv[k] for k in idx], [[bufs[k]] for k in idx],
                                    [(N_NEIGHBOUR_CHIPS, bufs[k].shape[1] // 2) for k in idx], after, name="ag_wait_" + nm)
        return [g[0] for g in got], token

    cast = lambda k, after: _cast_into_slot(big[k], slot, after, name="cast_" + big_names[k])
    ag_in = gather_start([cast(0, slot), conv_slots], slot, "in")
    bs_t = bs2.T

    h = _rms_fwd(x2d, g_mix, name="rms_mix", after=[ag_in[3]])
    mem_n = _rms_fwd(mem2d, g_mem, name="rms_mem", after=[h])
    kvo_b = [cast(1, mem_n)]
    kvo_b.append(cast(2, kvo_b[0]))
    w1_b = cast(3, kvo_b[1])
    w2_b = cast(4, w1_b)
    got_in, tok = gather_wait(ag_in, [0, 1], w2_b, "in")
    win4, conv4 = _forward_gathered(got_in, tok, name="ag_fwd_in")
    ag_kvo = gather_start(kvo_b, conv4, "kvo")
    w_in_t = win4.reshape(DIN, D)
    conv_full = conv4[:, :3, :dcv4].transpose(1, 0, 2).reshape(3, DC)
    NEAR, FAR = [0, 1], [2]

    def diagonal_wait(state, ks, after, nm):
        send, recv, bufs, _ = state
        got, token = _transfer_wait([send[k] for k in ks], [recv[k] for k in ks], [[bufs[k]] for k in ks],
                                    [(1, bufs[k].shape[1] // 2) for k in ks], after, name="ag_waitd_" + nm)
        return [g[0] for g in got], token

    proj_w = lambda tn, tk: pl.BlockSpec((tn, tk), lambda j, i, k, s: (s[j], k))
    proj_cols = lambda tm, tn: [pl.BlockSpec((tm, tn), lambda j, i, k, s: (i, s[j]))]
    proj_half = lambda which, into, after: _matmul(
        h, w_in_t, name="mm_proj_%d" % which, tb=True, M=S, N=DIN // 2, K=D, tn=DIN // 2, b_spec=proj_w,
        out_specs=proj_cols, outs=[sds((S, DIN), F32)], slots=jnp.full((1,), which, jnp.int32), into=into,
        after=after)[0]
    proj = proj_half(0, [], [ag_kvo[3]])
    got_kvo, tok = gather_wait(ag_kvo, [0, 1], proj, "kvo")
    ag_w1 = gather_start([w1_b], tok, "ffn1", forwards=got_kvo)
    kvo_n = _forward_halves(ag_w1[2][1:], NEAR, ag_w1[3], name="ag_fwdn_kvo")
    ag_w1 = (ag_w1[0], ag_w1[1], [ag_w1[2][0]] + list(kvo_n), ag_w1[3])
    proj = proj_half(1, [proj], list(kvo_n))
    kvo_d, tok = diagonal_wait(ag_w1, [1, 2], proj, "kvo")
    wkv4, wo4 = _forward_halves(kvo_d, FAR, tok, name="ag_fwdd_kvo")
    w_kv_full = wkv4.reshape(D, 2 * DM)
    w_o_full = wo4.reshape(D, D)
    (kv,) = _matmul(mem_n, w_kv_full, name="mm_kv", M=NMEM, N=2 * DM, K=D, outs=[sds((NMEM, 2 * DM), F32)])
    heads, hn, ycv = _mix_fwd(proj, kv, ws3, bs_t, ln_v_g, ln_v_b, conv_full, g_head, name="mix_fwd")
    def residual_and_norm(acc, res, g):
        x2v = acc + res
        r = lax.rsqrt(jnp.mean(x2v * x2v, axis=-1, keepdims=True) + EPS)
        return x2v, (x2v * r) * g

    row_vec = lambda tm, tn: pl.BlockSpec((1, tn), lambda j, i, k, *s: (0, j))
    x2, h2 = _matmul(hn, w_o_full, name="mm_wo", M=S, N=D, K=D, tn=D, n_split=1, epi=residual_and_norm,
                     outs=[sds((S, D), F32), sds((S, D), BF16)], extras=[(x2d, _tile_spec()), (g_ffn, row_vec)])
    near = jnp.stack([shard, 2 * (1 - xi) + yi, 2 * xi + (1 - yi)]).astype(jnp.int32)
    far = jnp.reshape(2 * (1 - xi) + (1 - yi), (1,)).astype(jnp.int32)

    w1_shard = lambda tn, tk: pl.BlockSpec((None, tk, tn), lambda j, i, k, s: (s[j], k, 0))
    act_cols = lambda tm, tn: [pl.BlockSpec((tm, tn), lambda j, i, k, s: (i, s[j]))] * 2

    def relu2(acc):
        r = jnp.maximum(acc, 0.0)
        return r * r, 2.0 * r

    got_w1, tok = gather_wait(ag_w1, [0], h2, "ffn1")
    ag_w2 = gather_start([w2_b], tok, "ffn2", forwards=got_w1)
    (w1n,) = _forward_halves([ag_w2[2][1]], NEAR, ag_w2[3], name="ag_fwdn_ffn1")
    ag_w2 = (ag_w2[0], ag_w2[1], [ag_w2[2][0], w1n], ag_w2[3])
    act, dact_df = _matmul(h2, w1n, name="mm_ffn1_near", M=S, N=3 * dff4, K=D, tn=dff4, b_spec=w1_shard,
                           out_specs=act_cols, outs=[sds((S, DFF), BF16)] * 2, epi=relu2, slots=near)
    w1d, tok = diagonal_wait(ag_w2, [1], act, "ffn1")
    (w14,) = _forward_halves(w1d, FAR, tok, name="ag_fwdd_ffn1")
    act, dact_df = _matmul(h2, w14, name="mm_ffn1_far", M=S, N=dff4, K=D, tn=dff4, b_spec=w1_shard,
                           out_specs=act_cols, outs=[sds((S, DFF), BF16)] * 2, epi=relu2, slots=far,
                           into=[act, dact_df])

    act_shard = lambda tm, tk: pl.BlockSpec((tm, tk), lambda j, i, k, s: (i, s[k]))
    w2_shard = lambda tn, tk: pl.BlockSpec((None, tk, tn), lambda j, i, k, s: (s[k], 0, j))
    got_w2, tok = gather_wait(ag_w2, [0], act, "ffn2")
    ag_w2d = gather_start([], tok, "ffn2d", forwards=got_w2)
    (w2n,) = _forward_halves(ag_w2d[2], NEAR, ag_w2d[3], name="ag_fwdn_ffn2")
    ag_w2d = (ag_w2d[0], ag_w2d[1], [w2n], ag_w2d[3])
    (x3,) = _matmul(act, w2n, name="mm_ffn2_near", M=S, N=D, K=3 * dff4, tm=2 * TM, tk=dff4,
                    a_spec=act_shard, b_spec=w2_shard, outs=[sds((S, D), F32)], epi=lambda acc, res: (acc + res,),
                    extras=[(x2, _tile_spec())], slots=near)
    w2d, tok = diagonal_wait(ag_w2d, [0], x3, "ffn2")
    (w24,) = _forward_halves(w2d, FAR, tok, name="ag_fwdd_ffn2")
    (x3,) = _matmul(act, w24, name="mm_ffn2_far", M=S, N=D, K=dff4, tm=2 * TM, tk=dff4, a_spec=act_shard,
                    b_spec=w2_shard, outs=[sds((S, D), F32)], epi=lambda acc, res: (acc + res,),
                    extras=[(x3, _tile_spec())], slots=far)
    w2_full = w24.reshape(DFF, D)

    ci = lax.axis_index("c")

    def rs_sibling(g4, nm):
        return _sibling_start([g4], False, 1 + big_names.index(nm), name="rs_sib_" + nm)

    def rs_chips(state, after, nm):
        send, recv, g4, land, _ = state
        (((land_, g4_),), _) = _transfer_wait(send, recv, [[land[0], g4[0]]], [(NCHIP, land[0].shape[1])], after,
                                             name="rs_sibwait_" + nm)
        part, buf = _select_half_bf16(g4_, ci, land_, slot, name="rs_add_" + nm)
        return _scatter_start([part], [buf], 1 + 2 * len(big_names) + big_names.index(nm), name="rs_start_" + nm)

    def rs_end(state, after, nm):
        send, recv, parts, bufs, _ = state
        (((buf, _),), _) = _transfer_wait(send, recv, [[bufs[0], parts[0]]], [(N_PEER_CHIPS, bufs[0].shape[1])], after,
                                          name="rs_wait_" + nm)
        return _sibling_start([buf], True, 1 + len(big_names) + big_names.index(nm), name="rs_share_" + nm)

    big_m = [m_w_in[0].T, m_w_kv[0], m_w_o[0], m_w_ffn1[0], m_w_ffn2[0]]
    big_v = [v_w_in[0].T, v_w_kv[0], v_w_o[0], v_w_ffn1[0], v_w_ffn2[0]]
    big_out = {}

    def rs_finish(k, state, after):
        send, recv, mine, land, _ = state
        nm = big_names[k]
        (((land_, mine_),), _) = _transfer_wait(send, recv, [[land[0], mine[0]]], [(NCHIP, land[0].shape[1])], after,
                                               name="rs_sharewait_" + nm)
        big_out[nm] = _adamw(big[k], mine_, land_, big_m[k], big_v[k], core, name="adamw_" + nm)
        return big_out[nm][1]

    dx3, dx3b, dg_final, loss11 = _loss_bwd(x3, g_final2, tgt, name="loss_bwd")
    (dw2,) = _matmul(act, dx3b, name="mm_dw2", ta=True, M=DFF, N=D, K=S, tm=2 * TM, tn=D, outs=[sds((DFF, D), BF16)])
    sib_w2 = rs_sibling(dw2.reshape(NCHIP, dff4, D), "w_ffn2")
    (dfb,) = _matmul(dx3b, w2_full, name="mm_dact", tb=True, M=S, N=DFF, K=D, tm=2 * TM, tn=dff4, outs=[sds((S, DFF), BF16)],
                     epi=lambda acc, g: (acc * g.astype(F32),), extras=[(dact_df, _tile_spec())],
                     after=[sib_w2[4]])
    rs_w2 = rs_chips(sib_w2, dfb, "w_ffn2")

    def dw1_out(tm, tn):
        nb = dff4 // tn
        return [pl.BlockSpec((None, tm, tn), lambda j, i, k: (j // nb, i, j % nb))]

    (dw1,) = _matmul(h2, dfb, name="mm_dw1", ta=True, M=D, N=DFF, K=S, tm=2 * TM, tn=dff4, outs=[sds((NCHIP, D, dff4), BF16)],
                     out_specs=dw1_out, after=[rs_w2[4]])
    sib_w1 = rs_sibling(dw1, "w_ffn1")

    def w1_rows(tn, tk):
        kb = dff4 // tk
        return pl.BlockSpec((None, tn, tk), lambda j, i, k: (k // kb, j, k % kb))

    (dh2,) = _matmul(dfb, w14, name="mm_dh2", tb=True, M=S, N=D, K=DFF, tm=2 * TM, b_spec=w1_rows,
                     outs=[sds((S, D), F32)], after=[sib_w1[4]])
    rs_w1 = rs_chips(sib_w1, dh2, "w_ffn1")
    dx2, dx2b, dg_ffn = _rms_bwd(dh2, x2, g_ffn, dx3, name="rms_ffn_bwd", after=[rs_w1[4]])
    (dwo,) = _matmul(hn, dx2b, name="mm_dwo", ta=True, M=D, N=D, K=S, outs=[sds((D, D), BF16)])
    sib_wo = rs_sibling(dwo.reshape(NCHIP, D // NCHIP, D), "w_o")
    (dhn,) = _matmul(dx2b, w_o_full, name="mm_dhn", tb=True, M=S, N=D, K=D, outs=[sds((S, D), F32)],
                     after=[sib_wo[4]])
    rs_wo = rs_chips(sib_wo, dhn, "w_o")
    sh_w2 = rs_end(rs_w2, rs_wo[4], "w_ffn2")
    dproj, dkv, dws, dbs8, dlng, dlnb, dcw8, dgh = _mix_bwd(
        dhn, heads, proj, ycv, kv, ws3, bs_t, ln_v_g, ln_v_b, conv_full, g_head, sh_w2[4], name="mix_bwd")
    (dwin_t,) = _matmul(dproj, h, name="mm_dwin", ta=True, M=DIN, N=D, K=S, tm=DIN // 2, outs=[sds((DIN, D), BF16)])
    sib_win = rs_sibling(dwin_t.reshape(NCHIP, din4, D), "w_in")
    (dwkv,) = _matmul(mem_n, dkv, name="mm_dwkv", ta=True, M=D, N=2 * DM, K=NMEM, outs=[sds((D, 2 * DM), BF16)],
                      after=[sib_win[4]])
    sib_wkv = rs_sibling(dwkv.reshape(NCHIP, D // NCHIP, 2 * DM), "w_kv")
    (dh,) = _matmul(dproj, w_in_t, name="mm_dh", M=S, N=D, K=DIN, tk=DIN, outs=[sds((S, D), F32)],
                    after=[sib_wkv[4]])
    rs_win = rs_chips(sib_win, dh, "w_in")
    rs_wkv = rs_chips(sib_wkv, rs_win[4], "w_kv")
    dx, dg_mix = _rms_bwd(dh, x2d, g_mix, dx2, name="rms_mix_bwd", want_bf=False, after=[rs_wkv[4]])
    sh_w1 = rs_end(rs_w1, dx, "w_ffn1")
    (dmem_n,) = _matmul(dkv, w_kv_full, name="mm_dmem", tb=True, M=NMEM, N=D, K=2 * DM, outs=[sds((NMEM, D), F32)],
                        after=[sh_w1[4]])
    (dg_mem,) = _rms_bwd(dmem_n, mem2d, g_mem, None, name="rms_mem_bwd", want_dx=False)
    sh_wo = rs_end(rs_wo, dg_mem, "w_o")
    done = rs_finish(4, sh_w2, sh_wo[4])
    done = rs_finish(3, sh_w1, done)
    sh_win = rs_end(rs_win, done, "w_in")
    sh_wkv = rs_end(rs_wkv, sh_win[4], "w_kv")
    done = rs_finish(2, sh_wo, sh_wkv[4])
    done = rs_finish(0, sh_win, done)
    done = rs_finish(1, sh_wkv, done)

    small_names = ["g_mix", "ln_v_g", "ln_v_b", "w_s", "b_s", "conv_w", "g_mem", "g_head", "g_ffn", "g_final"]
    small_part = [dg_mix, dlng, dlnb, dws, dbs8[:, 0, :], dcw8[:3], dg_mem, dgh, dg_ffn, dg_final, loss11]
    small_shapes = [(1, D), (1, DS), (1, DS), (NSH, CHUNK, CHUNK), (NSH, CHUNK), (3, DC), (1, D), (1, D), (1, D), (1, D),
                    (1, 1)]
    total = _allreduce_small(_pack(small_part), done, name="allreduce_small")
    small_g = _unpack(total, small_shapes)
    loss = small_g.pop()[0, 0]
    small_g[5] = lax.dynamic_slice(small_g[5], (0, shard * dcv4), (3, dcv4))
    small_w = [g_mix, ln_v_g, ln_v_b, ws3, bs2, conv_w[0], g_mem, g_head, g_ffn, g_final2]
    small_m = [m_g_mix, m_ln_v_g, m_ln_v_b, m_w_s[0], m_b_s[0], m_conv_w[0], m_g_mem, m_g_head, m_g_ffn,
               m_g_final.reshape(1, D)]
    small_v = [v_g_mix, v_ln_v_g, v_ln_v_b, v_w_s[0], v_b_s[0], v_conv_w[0], v_g_mem, v_g_head, v_g_ffn,
               v_g_final.reshape(1, D)]
    s_delta, s_m, s_v = _adamw_small(small_w, small_g, small_m, small_v, name="adamw_small")
    small_out = {nm: (g, d, mn, vn) for nm, g, d, mn, vn in zip(small_names, small_g, s_delta, s_m, s_v)}

    order = ["g_mix", "w_in", "ln_v_g", "ln_v_b", "w_s", "b_s", "conv_w", "g_mem", "w_kv", "g_head", "w_o",
             "g_ffn", "w_ffn1", "w_ffn2", "g_final"]
    like = dict(g_mix=g_mix, w_in=w_in, ln_v_g=ln_v_g, ln_v_b=ln_v_b, w_s=w_s, b_s=b_s, conv_w=conv_w, g_mem=g_mem,
                w_kv=w_kv, g_head=g_head, w_o=w_o, g_ffn=g_ffn, w_ffn1=w_ffn1, w_ffn2=w_ffn2, g_final=g_final)
    res = {**big_out, **small_out}
    res["w_in"] = [a.T for a in res["w_in"]]
    outs = [loss, dx[None]]
    for k in range(4):
        outs += [res[nm][k].reshape(like[nm].shape) for nm in order]
    return tuple(outs)
```

```python
import math

import jax
import jax.numpy as jnp
from jax import lax
from jax.experimental import pallas as pl
from jax.experimental.pallas import tpu as pltpu

F32 = jnp.float32
BF16 = jnp.bfloat16
MESH = pl.DeviceIdType.MESH

D = 2048
S = 2048
HD = 128
NH = D // HD
NMH = 4
NSH = (NH - NMH) // 2
NCH = NH - NMH - NSH
DS = NSH * HD
DC = NCH * HD
DM = NMH * HD
DIN = 2 * DS + 3 * DC + DM
CHUNK = 128
NMEM = 256
DFF = 4 * D
EPS = 1e-6
NCHIP = 4
SCALE = HD ** -0.5

ADAM_LR = 0.001
ADAM_B1 = 0.9
ADAM_B2 = 0.999
ADAM_EPS = 1e-08
ADAM_WD = 0.01
ADAM_STEP = 10

TR_EW = 256
TR_MIX = 256
TM = 512
TN = 1024
TK = 2048
N_SUB = 512
VMEM_MB = 56
HALO = 8


def _pick(n, target, q=128):
    best = None
    for t in range(q, min(n, target) + 1, q):
        if n % t == 0:
            best = t
    return n if best is None else best


def _pick_rows(n, q):
    below = _pick(n, TR_EW, q)
    if 2 * below >= TR_EW:
        return below
    above = [t for t in range(TR_EW, min(n, 4 * TR_EW) + 1, q) if n % t == 0]
    return above[0] if above else below


def _cp(sem=None, vmem_mb=None, **kw):
    d = dict(kw)
    if sem is not None:
        d["dimension_semantics"] = sem
    if vmem_mb is not None:
        d["vmem_limit_bytes"] = vmem_mb << 20
    return pltpu.CompilerParams(**d)


def _gelu(x):
    z = 0.7978845608028654 * (x + 0.044715 * (x * x * x))
    return 0.5 * x * (1.0 + jnp.tanh(z))


def _gelu_with_grad(x):
    x2 = x * x
    t = jnp.tanh(0.7978845608028654 * (x + 0.044715 * (x2 * x)))
    half = 0.5 * (1.0 + t)
    return x * half, half + 0.5 * x * (1.0 - t * t) * (0.7978845608028654 * (1.0 + 3.0 * 0.044715 * x2))


def _matmul(a, b, *, name, ta=False, tb=False, M, N, K, tm=None, tn=None, tk=None, outs, epi=None,
            extras=(), a_spec=None, b_spec=None, out_specs=None, after=(), n_split=None, slots=None, into=()):
    n_after = len(after)
    tm = _pick(M, TM if tm is None else tm, 8)
    tn = _pick(N, TN if tn is None else tn)
    tk = _pick(K, TK if tk is None else tk)
    if n_split is None:
        n_split = tn // N_SUB if tn % N_SUB == 0 else 1
    nk = K // tk
    grid = (N // tn, M // tm, nk)
    if a_spec is None:
        a_spec = (pl.BlockSpec((tk, tm), lambda j, i, k, *s: (k, i)) if ta
                  else pl.BlockSpec((tm, tk), lambda j, i, k, *s: (i, k)))
    else:
        a_spec = a_spec(tm, tk)
    if b_spec is None:
        b_spec = (pl.BlockSpec((tn, tk), lambda j, i, k, *s: (j, k)) if tb
                  else pl.BlockSpec((tk, tn), lambda j, i, k, *s: (k, j)))
    else:
        b_spec = b_spec(tn, tk)
    if out_specs is None:
        out_specs = [pl.BlockSpec((tm, tn), lambda j, i, k, *s: (i, j)) for _ in outs]
    else:
        out_specs = out_specs(tm, tn)
    dn = (((0 if ta else 1,), (1 if tb else 0,)), ((), ()))
    n_ex, n_out = len(extras), len(outs)
    n_pre = 0 if slots is None else 1
    n_into = len(into)
    ns = tn // n_split

    def body(*refs):
        a_ref, b_ref = refs[n_pre], refs[n_pre + 1]
        ex = refs[n_pre + 2:n_pre + 2 + n_ex]
        first_out = n_pre + 2 + n_ex + n_after + n_into
        o = refs[first_out:first_out + n_out]
        acc = refs[first_out + n_out:]
        k = pl.program_id(2)

        def finish(val, cols):
            res = (val,) if epi is None else epi(val, *[e[:, cols] for e in ex])
            for r, o_ref in zip(res, o):
                o_ref[:, cols] = r.astype(o_ref.dtype)

        if nk > 1:
            @pl.when(k == 0)
            def _():
                acc[0][...] = jnp.zeros_like(acc[0])

        av = a_ref[...].astype(BF16)
        for q in range(n_split):
            cols = slice(q * ns, (q + 1) * ns)
            bq = (b_ref[cols, :] if tb else b_ref[:, cols]).astype(BF16)
            part = lax.dot_general(av, bq, dn, preferred_element_type=F32)
            if nk == 1:
                finish(part, cols)
            else:
                acc[0][:, cols] += part

        if nk > 1:
            @pl.when(k == nk - 1)
            def _():
                finish(acc[0][...], slice(0, tn))

    in_specs = ([a_spec, b_spec] + [sp(tm, tn) for _, sp in extras] + [ANY] * (n_after + n_into))
    scratch = [pltpu.VMEM((tm, tn), F32)] if nk > 1 else []
    args = [a, b] + [arr for arr, _ in extras] + list(after) + list(into)
    aliases = {n_pre + len(args) - n_into + t: t for t in range(n_into)}
    params = _cp(("parallel", "parallel", "arbitrary"), VMEM_MB)
    if slots is None:
        return pl.pallas_call(body, name=name, grid=grid, in_specs=in_specs, out_specs=out_specs, out_shape=outs,
                              scratch_shapes=scratch, input_output_aliases=aliases, compiler_params=params)(*args)
    return pl.pallas_call(
        body, name=name,
        grid_spec=pltpu.PrefetchScalarGridSpec(num_scalar_prefetch=1, grid=grid, in_specs=in_specs,
                                               out_specs=out_specs, scratch_shapes=scratch),
        out_shape=outs, input_output_aliases=aliases, compiler_params=params)(slots, *args)


def _tile_spec():
    return lambda tm, tn: pl.BlockSpec((tm, tn), lambda j, i, k, *s: (i, j))


def _cast_into_slot(w, slot, after, *, name):
    R, C = w.shape
    tr = _pick_rows(R, 16)

    def body(s_ref, w_ref, _after_ref, o_ref):
        o_ref[...] = w_ref[...].astype(BF16)

    return pl.pallas_call(
        body, name=name,
        grid_spec=pltpu.PrefetchScalarGridSpec(
            num_scalar_prefetch=1, grid=(R // tr,),
            in_specs=[pl.BlockSpec((tr, C), lambda i, s: (i, 0)), ANY],
            out_specs=pl.BlockSpec((None, tr, C), lambda i, s: (s[0], i, 0))),
        out_shape=jax.ShapeDtypeStruct((NCHIP, R, C), BF16),
        compiler_params=_cp(("parallel",), VMEM_MB),
    )(slot, w, after)


def _rms_fwd(x, g, *, name, after=()):
    R, C = x.shape
    tr = _pick(R, TR_EW, 16)
    n_after = len(after)

    def body(x_ref, g_ref, *rest):
        o_ref = rest[n_after]
        xv = x_ref[...]
        r = lax.rsqrt(jnp.mean(xv * xv, axis=-1, keepdims=True) + EPS)
        o_ref[...] = ((xv * r) * g_ref[...]).astype(BF16)

    return pl.pallas_call(
        body, name=name, grid=(R // tr,),
        in_specs=[pl.BlockSpec((tr, C), lambda i: (i, 0)), pl.BlockSpec((1, C), lambda i: (0, 0))] + [ANY] * n_after,
        out_specs=pl.BlockSpec((tr, C), lambda i: (i, 0)),
        out_shape=jax.ShapeDtypeStruct((R, C), BF16),
        compiler_params=_cp(("parallel",), VMEM_MB),
    )(x, g, *after)


def _rms_bwd(dh, x, g, dres, *, name, want_dx=True, want_bf=True, after=()):
    R, C = x.shape
    tr = _pick(R, TR_EW, 16)
    has_res = dres is not None
    row = pl.BlockSpec((tr, C), lambda i: (i, 0))
    vec = pl.BlockSpec((1, C), lambda i: (0, 0))

    def body(*refs):
        dh_ref, x_ref, g_ref = refs[:3]
        pos = 3
        dres_ref = None
        if has_res:
            dres_ref = refs[pos]
            pos += 1
        outs = refs[pos + len(after):]
        i = pl.program_id(0)
        xv = x_ref[...]
        r = lax.rsqrt(jnp.mean(xv * xv, axis=-1, keepdims=True) + EPS)
        xh = xv * r
        dhv = dh_ref[...]
        dg_ref = outs[-1]
        dgp = jnp.sum(dhv * xh, axis=0, keepdims=True)

        @pl.when(i == 0)
        def _():
            dg_ref[...] = dgp

        @pl.when(i > 0)
        def _():
            dg_ref[...] += dgp

        if want_dx:
            t = dhv * g_ref[...]
            dx = r * (t - xh * jnp.mean(t * xh, axis=-1, keepdims=True))
            if has_res:
                dx = dx + dres_ref[...]
            outs[0][...] = dx
            if want_bf:
                outs[1][...] = dx.astype(BF16)

    in_specs = [row, row, vec] + ([row] if has_res else []) + [ANY] * len(after)
    out_specs, out_shape = [], []
    if want_dx:
        out_specs.append(row)
        out_shape.append(jax.ShapeDtypeStruct((R, C), F32))
        if want_bf:
            out_specs.append(row)
            out_shape.append(jax.ShapeDtypeStruct((R, C), BF16))
    out_specs.append(vec)
    out_shape.append(jax.ShapeDtypeStruct((1, C), F32))
    args = [dh, x, g] + ([dres] if has_res else []) + list(after)
    return pl.pallas_call(
        body, name=name, grid=(R // tr,), in_specs=in_specs, out_specs=out_specs, out_shape=out_shape,
        compiler_params=_cp(("arbitrary",), VMEM_MB),
    )(*args)


def _loss_bwd(x3, g, tgt, *, name):
    R, C = x3.shape
    tr = _pick(R, TR_EW, 16)
    n = R // tr
    row = pl.BlockSpec((tr, C), lambda i: (i, 0))
    vec = pl.BlockSpec((1, C), lambda i: (0, 0))

    def body(x_ref, g_ref, t_ref, dx_ref, dxb_ref, dg_ref, loss_ref, acc_ref):
        i = pl.program_id(0)
        xv = x_ref[...]
        gv = g_ref[...]
        r = lax.rsqrt(jnp.mean(xv * xv, axis=-1, keepdims=True) + EPS)
        xh = xv * r
        e = xh * gv - t_ref[...]
        dy = e * (1.0 / C)
        sq = jnp.sum(e * e, axis=0, keepdims=True)
        dgp = jnp.sum(dy * xh, axis=0, keepdims=True)

        @pl.when(i == 0)
        def _():
            acc_ref[...] = sq
            dg_ref[...] = dgp

        @pl.when(i > 0)
        def _():
            acc_ref[...] += sq
            dg_ref[...] += dgp

        t = dy * gv
        dx = r * (t - xh * jnp.mean(t * xh, axis=-1, keepdims=True))
        dx_ref[...] = dx
        dxb_ref[...] = dx.astype(BF16)

        @pl.when(i == n - 1)
        def _():
            loss_ref[...] = jnp.sum(acc_ref[...], axis=-1, keepdims=True) * (0.5 / C)

    return pl.pallas_call(
        body, name=name, grid=(n,),
        in_specs=[row, vec, row],
        out_specs=[row, row, vec, pl.BlockSpec((1, 1), lambda i: (0, 0))],
        out_shape=[jax.ShapeDtypeStruct((R, C), F32), jax.ShapeDtypeStruct((R, C), BF16),
                   jax.ShapeDtypeStruct((1, C), F32), jax.ShapeDtypeStruct((1, 1), F32)],
        scratch_shapes=[pltpu.VMEM((1, C), F32)],
        compiler_params=_cp(("arbitrary",), VMEM_MB),
    )(x3, g, tgt)


def _offsets():
    u0 = 0
    v0 = DS
    b0 = 2 * DS
    c0 = b0 + DC
    x0 = c0 + DC
    q0 = x0 + DC
    return u0, v0, b0, c0, x0, q0


def _tri_mask(lower):
    r = lax.broadcasted_iota(jnp.int32, (CHUNK, CHUNK), 0)
    c = lax.broadcasted_iota(jnp.int32, (CHUNK, CHUNK), 1)
    return (r >= c) if lower else (c >= r)


def _layer_norm_stats(vg):
    mu = jnp.mean(vg, axis=-1, keepdims=True)
    vc = vg - mu
    rstd = lax.rsqrt(jnp.mean(vc * vc, axis=-1, keepdims=True) + EPS)
    return vc * rstd, rstd


def _softmax_rows(qh, kh):
    s = lax.dot_general(qh, kh, (((1,), (1,)), ((), ())), preferred_element_type=F32)
    m = jnp.max(s, axis=-1, keepdims=True)
    e = jnp.exp(s - m)
    return e / jnp.sum(e, axis=-1, keepdims=True)


def _mix_fwd(proj, kv, w_s, bs_t, ln_g, ln_b, conv_w, g_head, *, name):
    assert DS == DC
    tr = _pick(S, TR_MIX, CHUNK)
    n = S // tr
    nck = tr // CHUNK
    u0, v0, b0, c0, x0, q0 = _offsets()
    hb = tr // HALO

    def body(p_ref, cprev_ref, xprev_ref, kv_ref, ws_ref, bst_ref, lng_ref, lnb_ref, cw_ref, gh_ref,
             heads_ref, hn_ref, ycv_ref, buf_ref):
        i = pl.program_id(0)

        def emit(col, val):
            rs = lax.rsqrt(jnp.mean(val * val, axis=-1, keepdims=True) + EPS)
            heads_ref[:, col:col + HD] = val
            hn_ref[:, col:col + HD] = ((val * rs) * gh_ref[:, col:col + HD]).astype(BF16)

        vhat, _ = _layer_norm_stats(_gelu(p_ref[:, v0:v0 + DS]))
        vnb = (vhat * lng_ref[...] + lnb_ref[...]).astype(BF16)
        low = _tri_mask(True)
        for h in range(NSH):
            wt = jnp.where(low, ws_ref[h], 0.0).astype(BF16)
            bcol = bst_ref[:, h:h + 1]
            parts = []
            for c in range(nck):
                blk = vnb[c * CHUNK:(c + 1) * CHUNK, h * HD:(h + 1) * HD]
                parts.append(jnp.dot(wt, blk, preferred_element_type=F32) + bcol)
            mixed = parts[0] if nck == 1 else jnp.concatenate(parts, axis=0)
            emit(h * HD, _gelu(p_ref[:, u0 + h * HD:u0 + (h + 1) * HD]) * mixed)

        xc = p_ref[:, c0:c0 + DC] * p_ref[:, x0:x0 + DC]
        prev = cprev_ref[...] * xprev_ref[...]
        buf_ref[0:HALO, :] = jnp.where(i > 0, prev, 0.0)
        buf_ref[HALO:HALO + tr, :] = xc
        y = (cw_ref[2:3, :] * xc + cw_ref[1:2, :] * buf_ref[HALO - 1:HALO - 1 + tr, :]
             + cw_ref[0:1, :] * buf_ref[HALO - 2:HALO - 2 + tr, :])
        ycv_ref[...] = y
        cout = p_ref[:, b0:b0 + DC] * y
        for h in range(NCH):
            emit(DS + h * HD, cout[:, h * HD:(h + 1) * HD])

        for h in range(NMH):
            qh = (p_ref[:, q0 + h * HD:q0 + (h + 1) * HD] * SCALE).astype(BF16)
            kh = kv_ref[:, h * HD:(h + 1) * HD].astype(BF16)
            vh = kv_ref[:, DM + h * HD:DM + (h + 1) * HD].astype(BF16)
            p = _softmax_rows(qh, kh)
            emit(DS + DC + h * HD, jnp.dot(p.astype(BF16), vh, preferred_element_type=F32))

    full = lambda shape: pl.BlockSpec(shape, lambda i: (0,) * len(shape))
    halo_c = pl.BlockSpec((HALO, DC), lambda i: (jnp.maximum(i * hb - 1, 0), c0 // DC))
    halo_x = pl.BlockSpec((HALO, DC), lambda i: (jnp.maximum(i * hb - 1, 0), x0 // DC))
    return pl.pallas_call(
        body, name=name, grid=(n,),
        in_specs=[pl.BlockSpec((tr, DIN), lambda i: (i, 0)), halo_c, halo_x,
                  full((NMEM, 2 * DM)), full((NSH, CHUNK, CHUNK)), full((CHUNK, NSH)),
                  full((1, DS)), full((1, DS)), full((3, DC)), full((1, D))],
        out_specs=[pl.BlockSpec((tr, D), lambda i: (i, 0)), pl.BlockSpec((tr, D), lambda i: (i, 0)),
                   pl.BlockSpec((tr, DC), lambda i: (i, 0))],
        out_shape=[jax.ShapeDtypeStruct((S, D), F32), jax.ShapeDtypeStruct((S, D), BF16),
                   jax.ShapeDtypeStruct((S, DC), F32)],
        scratch_shapes=[pltpu.VMEM((tr + HALO, DC), F32)],
        compiler_params=_cp(("parallel",), VMEM_MB),
    )(proj, proj, proj, kv, w_s, bs_t, ln_g, ln_b, conv_w, g_head)


def _mix_bwd(dhn, heads, proj, ycv, kv, w_s, bs_t, ln_g, ln_b, conv_w, g_head, after, *, name):
    assert DS == DC
    tr = _pick(S, TR_MIX, CHUNK)
    n = S // tr
    nck = tr // CHUNK
    u0, v0, b0, c0, x0, q0 = _offsets()
    hb = tr // HALO
    last_hb = S // HALO - 1

    def body(dhn_ref, heads_ref, p_ref, ycv_ref, dhn_nx_ref, heads_nx_ref, b_nx_ref, kv_ref, ws_ref, bst_ref,
             lng_ref, lnb_ref, cw_ref, gh_ref, _after_ref,
             dp_ref, dkv_ref, dws_ref, dbs_ref, dlng_ref, dlnb_ref, dcw_ref, dgh_ref, buf_ref, dvn_ref):
        i = pl.program_id(0)

        @pl.when(i == 0)
        def _():
            dkv_ref[...] = jnp.zeros_like(dkv_ref)
            dws_ref[...] = jnp.zeros_like(dws_ref)
            dbs_ref[...] = jnp.zeros_like(dbs_ref)
            dlng_ref[...] = jnp.zeros_like(dlng_ref)
            dlnb_ref[...] = jnp.zeros_like(dlnb_ref)
            dcw_ref[...] = jnp.zeros_like(dcw_ref)
            dgh_ref[...] = jnp.zeros_like(dgh_ref)

        def head_bwd(a, dn, gh):
            rs = lax.rsqrt(jnp.mean(a * a, axis=-1, keepdims=True) + EPS)
            ah = a * rs
            t = dn * gh
            return rs * (t - ah * jnp.mean(t * ah, axis=-1, keepdims=True)), jnp.sum(dn * ah, axis=0, keepdims=True)

        def head_grad(col):
            da, dg = head_bwd(heads_ref[:, col:col + HD], dhn_ref[:, col:col + HD], gh_ref[:, col:col + HD])
            dgh_ref[:, col:col + HD] += dg
            return da

        vg, dvg_dv = _gelu_with_grad(p_ref[:, v0:v0 + DS])
        vhat, rstd = _layer_norm_stats(vg)
        vnb = (vhat * lng_ref[...] + lnb_ref[...]).astype(BF16)
        low = _tri_mask(True)
        ones = jnp.ones((HALO, HD), BF16)
        for h in range(NSH):
            w_h = ws_ref[h]
            wt = jnp.where(low, w_h, 0.0).astype(BF16)
            bcol = bst_ref[:, h:h + 1]
            da = head_grad(h * HD)
            ug, dug_du = _gelu_with_grad(p_ref[:, u0 + h * HD:u0 + (h + 1) * HD])
            dws = jnp.zeros((CHUNK, CHUNK), F32)
            dbs = jnp.zeros((HALO, CHUNK), F32)
            mixed_parts = []
            for c in range(nck):
                rows = slice(c * CHUNK, (c + 1) * CHUNK)
                blk = vnb[rows, h * HD:(h + 1) * HD]
                mixed_parts.append(jnp.dot(wt, blk, preferred_element_type=F32) + bcol)
                dmb = (da[rows] * ug[rows]).astype(BF16)
                dws = dws + lax.dot_general(dmb, blk, (((1,), (1,)), ((), ())), preferred_element_type=F32)
                dbs = dbs + lax.dot_general(ones, dmb, (((1,), (1,)), ((), ())), preferred_element_type=F32)
                dvn_ref[c * CHUNK:(c + 1) * CHUNK, h * HD:(h + 1) * HD] = lax.dot_general(
                    wt, dmb, (((0,), (0,)), ((), ())), preferred_element_type=F32)
            mixed = mixed_parts[0] if nck == 1 else jnp.concatenate(mixed_parts, axis=0)
            dp_ref[:, u0 + h * HD:u0 + (h + 1) * HD] = ((da * mixed) * dug_du).astype(BF16)
            dws_ref[h] += jnp.where(low, dws, 0.0)
            dbs_ref[h] += dbs
        dvn = dvn_ref[...]
        dlng_ref[...] += jnp.sum(dvn * vhat, axis=0, keepdims=True)
        dlnb_ref[...] += jnp.sum(dvn, axis=0, keepdims=True)
        dvh = dvn * lng_ref[...]
        dvg = rstd * (dvh - jnp.mean(dvh, axis=-1, keepdims=True)
                      - vhat * jnp.mean(dvh * vhat, axis=-1, keepdims=True))
        dp_ref[:, v0:v0 + DS] = (dvg * dvg_dv).astype(BF16)

        dc = jnp.concatenate([head_grad(DS + h * HD) for h in range(NCH)], axis=1)
        dc_nx = jnp.concatenate(
            [head_bwd(heads_nx_ref[:, h * HD:(h + 1) * HD], dhn_nx_ref[:, h * HD:(h + 1) * HD],
                      gh_ref[:, DS + h * HD:DS + (h + 1) * HD])[0] for h in range(NCH)], axis=1)
        bg = p_ref[:, b0:b0 + DC]
        cg = p_ref[:, c0:c0 + DC]
        xin = p_ref[:, x0:x0 + DC]
        dp_ref[:, b0:b0 + DC] = (dc * ycv_ref[...]).astype(BF16)
        dyv = dc * bg
        buf_ref[0:tr, :] = dyv
        buf_ref[tr:tr + HALO, :] = jnp.where(i < n - 1, dc_nx * b_nx_ref[...], 0.0)
        sh1 = buf_ref[1:1 + tr, :]
        sh0 = buf_ref[2:2 + tr, :]
        dxc = cw_ref[2:3, :] * dyv + cw_ref[1:2, :] * sh1 + cw_ref[0:1, :] * sh0
        xc = cg * xin
        dp_ref[:, c0:c0 + DC] = (dxc * xin).astype(BF16)
        dp_ref[:, x0:x0 + DC] = (dxc * cg).astype(BF16)
        dcw_ref[0:1, :] += jnp.sum(sh0 * xc, axis=0, keepdims=True)
        dcw_ref[1:2, :] += jnp.sum(sh1 * xc, axis=0, keepdims=True)
        dcw_ref[2:3, :] += jnp.sum(dyv * xc, axis=0, keepdims=True)

        for h in range(NMH):
            do = head_grad(DS + DC + h * HD).astype(BF16)
            qh = (p_ref[:, q0 + h * HD:q0 + (h + 1) * HD] * SCALE).astype(BF16)
            kh = kv_ref[:, h * HD:(h + 1) * HD].astype(BF16)
            vh = kv_ref[:, DM + h * HD:DM + (h + 1) * HD].astype(BF16)
            p = _softmax_rows(qh, kh)
            dpr = lax.dot_general(do, vh, (((1,), (1,)), ((), ())), preferred_element_type=F32)
            ds = (p * (dpr - jnp.sum(dpr * p, axis=-1, keepdims=True))).astype(BF16)
            dp_ref[:, q0 + h * HD:q0 + (h + 1) * HD] = (
                jnp.dot(ds, kh, preferred_element_type=F32) * SCALE).astype(BF16)
            dkv_ref[:, h * HD:(h + 1) * HD] += lax.dot_general(
                ds, qh, (((0,), (0,)), ((), ())), preferred_element_type=F32)
            dkv_ref[:, DM + h * HD:DM + (h + 1) * HD] += lax.dot_general(
                p.astype(BF16), do, (((0,), (0,)), ((), ())), preferred_element_type=F32)

    full = lambda shape: pl.BlockSpec(shape, lambda i: (0,) * len(shape))
    row = lambda c: pl.BlockSpec((tr, c), lambda i: (i, 0))
    nxt = lambda col: pl.BlockSpec((HALO, DC), lambda i: (jnp.minimum((i + 1) * hb, last_hb), col))
    return pl.pallas_call(
        body, name=name, grid=(n,),
        in_specs=[row(D), row(D), row(DIN), row(DC), nxt(DS // DC), nxt(DS // DC), nxt(b0 // DC),
                  full((NMEM, 2 * DM)), full((NSH, CHUNK, CHUNK)), full((CHUNK, NSH)),
                  full((1, DS)), full((1, DS)), full((3, DC)), full((1, D)), ANY],
        out_specs=[row(DIN), full((NMEM, 2 * DM)), full((NSH, CHUNK, CHUNK)), full((NSH, HALO, CHUNK)),
                   full((1, DS)), full((1, DS)), full((HALO, DC)), full((1, D))],
        out_shape=[jax.ShapeDtypeStruct((S, DIN), BF16), jax.ShapeDtypeStruct((NMEM, 2 * DM), F32),
                   jax.ShapeDtypeStruct((NSH, CHUNK, CHUNK), F32), jax.ShapeDtypeStruct((NSH, HALO, CHUNK), F32),
                   jax.ShapeDtypeStruct((1, DS), F32), jax.ShapeDtypeStruct((1, DS), F32),
                   jax.ShapeDtypeStruct((HALO, DC), F32), jax.ShapeDtypeStruct((1, D), F32)],
        scratch_shapes=[pltpu.VMEM((tr + HALO, DC), F32), pltpu.VMEM((tr, DS), F32)],
        compiler_params=_cp(("arbitrary",), VMEM_MB),
    )(dhn, heads, proj, ycv, dhn, heads, proj, kv, w_s, bs_t, ln_g, ln_b, conv_w, g_head, after)


def _place():
    x, y, c = lax.axis_index("x"), lax.axis_index("y"), lax.axis_index("c")
    chips = [(1 - x, y), (x, 1 - y), (1 - x, 1 - y)]
    return x, y, c, chips


ANY = pl.BlockSpec(memory_space=pl.ANY)


HBM = pl.BlockSpec(memory_space=pltpu.HBM)
SEM = pl.BlockSpec(memory_space=pltpu.SEMAPHORE)
EFFECT = pltpu.SideEffectType.DATAFLOW_SIDE_EFFECTING
N_PEER_CHIPS = 3
N_NEIGHBOUR_CHIPS = 2
CONV_PAD = (32, 256)


def _in_hbm(a):
    return pltpu.with_memory_space_constraint(a, pltpu.HBM)


def _allgather_start(bufs, forwards, after, collective_id, *, name):
    arrs = list(bufs) + list(forwards)
    nw, nb = len(arrs), len(bufs)

    def body(*refs):
        ins, send, recv = refs[:nw], refs[nw + 1:2 * nw + 1], refs[2 * nw + 1:3 * nw + 1]
        token = refs[4 * nw + 1]
        x, y, c, chips = _place()
        s = 2 * x + y
        slots = [2 * cx + cy for cx, cy in chips]
        _handshake([(cx, cy, c) for cx, cy in chips[:N_NEIGHBOUR_CHIPS]])
        for w in range(nb, nw):
            q = arrs[w].shape[1] // 4
            for j in range(N_NEIGHBOUR_CHIPS):
                rows = ins[w].at[slots[j], pl.ds(c * 2 * q + j * q, q)]
                pltpu.make_async_remote_copy(src_ref=rows, dst_ref=rows, send_sem=send[w], recv_sem=recv[w],
                                             device_id=(*chips[1 - j], c), device_id_type=MESH).start()
        for w in range(nb):
            hr = arrs[w].shape[1] // 2
            rows = ins[w].at[s, pl.ds(c * hr, hr)]
            for cx, cy in chips[:N_NEIGHBOUR_CHIPS]:
                pltpu.make_async_remote_copy(src_ref=rows, dst_ref=rows, send_sem=send[w], recv_sem=recv[w],
                                             device_id=(cx, cy, c), device_id_type=MESH).start()
        token[...] = jnp.zeros_like(token)

    res = pl.pallas_call(
        body, name=name,
        in_specs=[HBM] * nw + [ANY],
        out_specs=[SEM] * (2 * nw) + [HBM] * nw + [pl.BlockSpec(memory_space=pltpu.VMEM)],
        out_shape=[pltpu.SemaphoreType.DMA(())] * (2 * nw) + [pltpu.HBM(a.shape, a.dtype) for a in arrs]
        + [jax.ShapeDtypeStruct((8, 128), F32)],
        input_output_aliases={w: 2 * nw + w for w in range(nw)},
        compiler_params=pltpu.CompilerParams(has_side_effects=EFFECT, collective_id=collective_id),
    )(*[_in_hbm(a) for a in arrs], after)
    return res[:nw], res[nw:2 * nw], res[2 * nw:3 * nw], res[3 * nw]


def _handshake(peers):
    barrier = pltpu.get_barrier_semaphore()
    for peer in peers:
        pl.semaphore_signal(barrier, inc=1, device_id=peer, device_id_type=MESH)
    pl.semaphore_wait(barrier, len(peers))


def _scatter_start(parts, bufs, collective_id, *, name):
    nw = len(parts)

    def body(*refs):
        src, dst = refs[:nw], refs[nw:2 * nw]
        send, recv = refs[2 * nw:3 * nw], refs[3 * nw:4 * nw]
        token = refs[6 * nw]
        x, y, c, chips = _place()
        s = 2 * x + y
        _handshake([(cx, cy, c) for cx, cy in chips])
        for w in range(nw):
            for cx, cy in chips:
                pltpu.make_async_remote_copy(src_ref=src[w].at[2 * cx + cy], dst_ref=dst[w].at[s], send_sem=send[w],
                                             recv_sem=recv[w], device_id=(cx, cy, c), device_id_type=MESH).start()
        token[...] = jnp.zeros_like(token)

    res = pl.pallas_call(
        body, name=name,
        in_specs=[HBM] * (2 * nw),
        out_specs=[SEM] * (2 * nw) + [HBM] * (2 * nw) + [pl.BlockSpec(memory_space=pltpu.VMEM)],
        out_shape=[pltpu.SemaphoreType.DMA(())] * (2 * nw) + [pltpu.HBM(a.shape, a.dtype) for a in parts + bufs]
        + [jax.ShapeDtypeStruct((8, 128), F32)],
        input_output_aliases={k: 2 * nw + k for k in range(2 * nw)},
        compiler_params=pltpu.CompilerParams(has_side_effects=EFFECT, collective_id=collective_id),
    )(*[_in_hbm(a) for a in parts + bufs])
    return res[:nw], res[nw:2 * nw], res[2 * nw:3 * nw], res[3 * nw:4 * nw], res[4 * nw]


def _sibling_start(srcs, whole, collective_id, *, name):
    nw = len(srcs)
    lands = [lax.empty((a.shape[0], a.shape[1] if whole else a.shape[1] // 2, a.shape[2]), a.dtype) for a in srcs]

    def body(*refs):
        src, land = refs[:nw], refs[nw:2 * nw]
        send, recv = refs[2 * nw:3 * nw], refs[3 * nw:4 * nw]
        token = refs[6 * nw]
        x, y, c, _ = _place()
        _handshake([(x, y, 1 - c)])
        for w in range(nw):
            hr = srcs[w].shape[1] // 2
            rows = src[w] if whole else src[w].at[:, pl.ds((1 - c) * hr, hr)]
            pltpu.make_async_remote_copy(src_ref=rows, dst_ref=land[w], send_sem=send[w], recv_sem=recv[w],
                                         device_id=(x, y, 1 - c), device_id_type=MESH).start()
        token[...] = jnp.zeros_like(token)

    res = pl.pallas_call(
        body, name=name,
        in_specs=[HBM] * (2 * nw),
        out_specs=[SEM] * (2 * nw) + [HBM] * (2 * nw) + [pl.BlockSpec(memory_space=pltpu.VMEM)],
        out_shape=[pltpu.SemaphoreType.DMA(())] * (2 * nw) + [pltpu.HBM(a.shape, a.dtype) for a in srcs + lands]
        + [jax.ShapeDtypeStruct((8, 128), F32)],
        input_output_aliases={k: 2 * nw + k for k in range(2 * nw)},
        compiler_params=pltpu.CompilerParams(has_side_effects=EFFECT, collective_id=collective_id),
    )(*[_in_hbm(a) for a in srcs + lands])
    return res[:nw], res[nw:2 * nw], res[2 * nw:3 * nw], res[3 * nw:4 * nw], res[4 * nw]


def _transfer_wait(sends, recvs, thru, sizes, after, *, name):
    n = len(sends)
    flat = [a for group in thru for a in group]

    def body(*refs):
        bufs = refs[:len(flat)]
        send = refs[len(flat):len(flat) + n]
        recv = refs[len(flat) + n:len(flat) + 2 * n]
        token = refs[2 * len(flat) + 2 * n + 1]
        token[...] = jnp.zeros_like(token)
        x, y, c, _ = _place()
        pos = 0
        for k in range(n):
            slots, rows = sizes[k]
            region = bufs[pos].at[pl.ds(0, slots), pl.ds(0, rows)]
            pos += len(thru[k])
            cp = pltpu.make_async_remote_copy(src_ref=region, dst_ref=region, send_sem=send[k], recv_sem=recv[k],
                                              device_id=(x, y, 1 - c), device_id_type=MESH)
            cp.wait_send()
            cp.wait_recv()

    res = pl.pallas_call(
        body, name=name,
        in_specs=[HBM] * len(flat) + [SEM] * (2 * n) + [pl.BlockSpec(memory_space=pl.ANY)],
        out_specs=[HBM] * len(flat) + [pl.BlockSpec(memory_space=pltpu.VMEM)],
        out_shape=[pltpu.HBM(a.shape, a.dtype) for a in flat] + [jax.ShapeDtypeStruct((8, 128), F32)],
        input_output_aliases={k: k for k in range(len(flat))},
        compiler_params=pltpu.CompilerParams(has_side_effects=EFFECT),
    )(*flat, *sends, *recvs, after)
    out, pos = [], 0
    for group in thru:
        out.append(res[pos:pos + len(group)])
        pos += len(group)
    return out, res[len(flat)]


def _forward_gathered(bufs, after, *, name):
    nw = len(bufs)

    def body(*refs):
        outs = refs[nw + 1:2 * nw + 1]
        d_send, d_recv, i_send, i_recv = refs[2 * nw + 1:]
        x, y, c, chips = _place()
        me, sibling = (x, y, c), (x, y, 1 - c)
        slots = [2 * cx + cy for cx, cy in chips]

        def rows(w, j, start, n):
            return outs[w].at[slots[j], pl.ds(start, n)]

        def d2d(w, j, which, to):
            hr = bufs[w].shape[1] // 2
            r = rows(w, j, which * hr, hr)
            return pltpu.make_async_remote_copy(
                src_ref=r, dst_ref=r, send_sem=d_send.at[N_PEER_CHIPS * w + j],
                recv_sem=d_recv.at[N_PEER_CHIPS * w + j], device_id=to, device_id_type=MESH)

        def ici(w, j, slot_j, to):
            q = bufs[w].shape[1] // 4
            r = rows(w, slot_j, c * 2 * q + j * q, q)
            return pltpu.make_async_remote_copy(
                src_ref=r, dst_ref=r, send_sem=i_send.at[N_NEIGHBOUR_CHIPS * w + j],
                recv_sem=i_recv.at[N_NEIGHBOUR_CHIPS * w + j], device_id=to, device_id_type=MESH)

        started = []
        for w in range(nw):
            started += [ici(w, 0, 0, (*chips[1], c)), ici(w, 1, 1, (*chips[0], c))]
            started += [d2d(w, j, c, sibling) for j in range(N_NEIGHBOUR_CHIPS)]
        for cp in started:
            cp.start()
        diag = N_PEER_CHIPS - 1
        for w in range(nw):
            for j in range(N_NEIGHBOUR_CHIPS):
                ici(w, j, diag, me).wait_recv()
            cp = d2d(w, diag, c, sibling)
            cp.start()
            started.append(cp)
        for w in range(nw):
            for j in range(N_PEER_CHIPS):
                d2d(w, j, 1 - c, me).wait_recv()
        for cp in started:
            cp.wait_send()

    return pl.pallas_call(
        body, name=name,
        in_specs=[ANY] * (nw + 1), out_specs=[ANY] * nw,
        out_shape=[jax.ShapeDtypeStruct(a.shape, a.dtype) for a in bufs],
        input_output_aliases={w: w for w in range(nw)},
        scratch_shapes=[pltpu.SemaphoreType.DMA((N_PEER_CHIPS * nw,)), pltpu.SemaphoreType.DMA((N_PEER_CHIPS * nw,)),
                        pltpu.SemaphoreType.DMA((N_NEIGHBOUR_CHIPS * nw,)),
                        pltpu.SemaphoreType.DMA((N_NEIGHBOUR_CHIPS * nw,))],
    )(*bufs, after)


def _forward_halves(bufs, which, after, *, name):
    nw = len(bufs)
    n = len(which)

    def body(*refs):
        outs = refs[nw + 1:2 * nw + 1]
        send, recv = refs[2 * nw + 1:]
        x, y, c, chips = _place()
        me, sibling = (x, y, c), (x, y, 1 - c)

        def d2d(w, t, half, to):
            cx, cy = chips[which[t]]
            hr = bufs[w].shape[1] // 2
            rows = outs[w].at[2 * cx + cy, pl.ds(half * hr, hr)]
            return pltpu.make_async_remote_copy(src_ref=rows, dst_ref=rows, send_sem=send.at[n * w + t],
                                                recv_sem=recv.at[n * w + t], device_id=to, device_id_type=MESH)

        passed = [d2d(w, t, c, sibling) for w in range(nw) for t in range(n)]
        for cp in passed:
            cp.start()
        for w in range(nw):
            for t in range(n):
                d2d(w, t, 1 - c, me).wait_recv()
        for cp in passed:
            cp.wait_send()

    return pl.pallas_call(
        body, name=name,
        in_specs=[ANY] * (nw + 1), out_specs=[ANY] * nw,
        out_shape=[jax.ShapeDtypeStruct(a.shape, a.dtype) for a in bufs],
        input_output_aliases={w: w for w in range(nw)},
        scratch_shapes=[pltpu.SemaphoreType.DMA((n * nw,)), pltpu.SemaphoreType.DMA((n * nw,))],
    )(*bufs, after)


def _allreduce_small(p, after, *, name):
    R = p.shape[0]
    hr = R // 2

    def body(p_ref, _after_ref, out_ref, sib_ref, sum_ref, gat_ref, tot_ref, send, recv):
        x, y, c, chips = _place()
        s = 2 * x + y
        sibling = (x, y, 1 - c)
        rows = pl.ds(pl.multiple_of(c * hr, 8), hr)
        swap = pltpu.make_async_remote_copy(src_ref=p_ref, dst_ref=sib_ref, send_sem=send.at[0], recv_sem=recv.at[0],
                                            device_id=sibling, device_id_type=MESH)
        swap.start()
        swap.wait()
        sum_ref[...] = p_ref[...] + sib_ref[...]
        gat_ref[s] = sum_ref[rows, :]
        cps = [pltpu.make_async_remote_copy(src_ref=sum_ref.at[rows], dst_ref=gat_ref.at[s], send_sem=send.at[1 + j],
                                            recv_sem=recv.at[1 + j], device_id=(cx, cy, c), device_id_type=MESH)
               for j, (cx, cy) in enumerate(chips)]
        for cp in cps:
            cp.start()
        for cp in cps:
            cp.wait()
        tot_ref[...] = ((gat_ref[0] + gat_ref[1]) + gat_ref[2]) + gat_ref[3]
        out_ref[rows, :] = tot_ref[...]
        share = pltpu.make_async_remote_copy(src_ref=tot_ref, dst_ref=out_ref.at[rows], send_sem=send.at[4],
                                             recv_sem=recv.at[4], device_id=sibling, device_id_type=MESH)
        share.start()
        share.wait_send()
        other = out_ref.at[pl.ds(pl.multiple_of((1 - c) * hr, 8), hr)]
        pltpu.make_async_remote_copy(src_ref=other, dst_ref=other, send_sem=send.at[4], recv_sem=recv.at[4],
                                     device_id=(x, y, c), device_id_type=MESH).wait_recv()

    vmem = pl.BlockSpec(memory_space=pltpu.VMEM)
    return pl.pallas_call(
        body, name=name, in_specs=[vmem, ANY], out_specs=vmem,
        out_shape=jax.ShapeDtypeStruct((R, 128), F32),
        scratch_shapes=[pltpu.VMEM((R, 128), F32), pltpu.VMEM((R, 128), F32), pltpu.VMEM((NCHIP, hr, 128), F32),
                        pltpu.VMEM((hr, 128), F32), pltpu.SemaphoreType.DMA((5,)), pltpu.SemaphoreType.DMA((5,))],
    )(p, after)


def _select_half_bf16(g, half, add, slot, *, name):
    _, R, C = g.shape
    hr = R // 2
    tr = _pick_rows(hr, 16)
    nb = hr // tr
    sel = jnp.concatenate([jnp.reshape(half, (1,)).astype(jnp.int32), slot])

    def body(s_ref, g_ref, a_ref, o_ref, own_ref):
        val = (g_ref[...].astype(F32) + a_ref[...].astype(F32)).astype(BF16)
        o_ref[...] = val

        @pl.when(pl.program_id(1) == s_ref[1])
        def _():
            own_ref[...] = val

    g_spec = pl.BlockSpec((None, tr, C), lambda i, j, s: (j, s[0] * nb + i, 0))
    o_spec = pl.BlockSpec((None, tr, C), lambda i, j, s: (j, i, 0))
    own_spec = pl.BlockSpec((None, tr, C), lambda i, j, s: (s[1], i, 0))
    shape = jax.ShapeDtypeStruct((NCHIP, hr, C), BF16)
    return pl.pallas_call(
        body, name=name,
        grid_spec=pltpu.PrefetchScalarGridSpec(
            num_scalar_prefetch=1, grid=(nb, NCHIP), in_specs=[g_spec, o_spec], out_specs=[o_spec, own_spec]),
        out_shape=[shape, shape],
        compiler_params=_cp(("parallel", "arbitrary"), VMEM_MB),
    )(sel, g, add)


def _adamw_math(w, g, m, v):
    m = ADAM_B1 * m + (1.0 - ADAM_B1) * g
    v = ADAM_B2 * v + (1.0 - ADAM_B2) * (g * g)
    m_hat = m / (1.0 - ADAM_B1 ** ADAM_STEP)
    v_hat = v / (1.0 - ADAM_B2 ** ADAM_STEP)
    delta = -ADAM_LR * (m_hat / (jnp.sqrt(v_hat) + ADAM_EPS) + ADAM_WD * w)
    return delta, m, v


def _adamw(w, g_mine, g_sib, m, v, core, *, name):
    R, C = w.shape
    hr = R // 2
    tr = _pick_rows(hr, 16)
    nb = hr // tr
    row = pl.BlockSpec((tr, C), lambda hh, i, c: (hh * nb + i, 0))
    mine = pl.BlockSpec((NCHIP, tr, C), lambda hh, i, c: (0, jnp.where(hh == c[0], i, 0), 0))
    sibs = pl.BlockSpec((NCHIP, tr, C), lambda hh, i, c: (0, jnp.where(hh == c[0], 0, i), 0))

    def slot_sum(ref):
        acc = ref[0].astype(F32) + ref[1].astype(F32)
        for j in range(2, NCHIP):
            acc = acc + ref[j].astype(F32)
        return acc

    def body(c_ref, w_ref, gm_ref, gs_ref, m_ref, v_ref, go_ref, d_ref, mo_ref, vo_ref):
        gv = jnp.where(pl.program_id(0) == c_ref[0], slot_sum(gm_ref), slot_sum(gs_ref))
        d, mn, vn = _adamw_math(w_ref[...], gv, m_ref[...], v_ref[...])
        go_ref[...] = gv
        d_ref[...] = d
        mo_ref[...] = mn
        vo_ref[...] = vn

    return pl.pallas_call(
        body, name=name,
        grid_spec=pltpu.PrefetchScalarGridSpec(
            num_scalar_prefetch=1, grid=(2, nb),
            in_specs=[row, mine, sibs, row, row], out_specs=[row] * 4),
        out_shape=[jax.ShapeDtypeStruct((R, C), F32)] * 4,
        compiler_params=_cp(("parallel", "parallel"), VMEM_MB),
    )(core, w, g_mine, g_sib, m, v)


def _adamw_small(ws, gs, ms, vs, *, name):
    n = len(ws)

    def body(*refs):
        w_r, g_r, m_r, v_r = refs[:n], refs[n:2 * n], refs[2 * n:3 * n], refs[3 * n:4 * n]
        d_r, mo_r, vo_r = refs[4 * n:5 * n], refs[5 * n:6 * n], refs[6 * n:7 * n]
        for k in range(n):
            d, mn, vn = _adamw_math(w_r[k][...], g_r[k][...], m_r[k][...], v_r[k][...])
            d_r[k][...] = d
            mo_r[k][...] = mn
            vo_r[k][...] = vn

    shapes = [jax.ShapeDtypeStruct(w.shape, F32) for w in ws]
    res = pl.pallas_call(body, name=name, out_shape=shapes * 3)(*ws, *gs, *ms, *vs)
    return res[:n], res[n:2 * n], res[2 * n:]


_PACK_ROWS = 8


def _pack(parts):
    rows = []
    for a in parts:
        flat = a.reshape(-1)
        n = -(-flat.shape[0] // (_PACK_ROWS * 128)) * (_PACK_ROWS * 128)
        rows.append(jnp.pad(flat, (0, n - flat.shape[0])).reshape(-1, 128))
    total = sum(r.shape[0] for r in rows)
    if total % 16:
        rows.append(jnp.zeros((16 - total % 16, 128), F32))
    return jnp.concatenate(rows, axis=0)


def _unpack(p, shapes):
    out, r = [], 0
    for shp in shapes:
        n = math.prod(shp)
        nr = -(-n // (_PACK_ROWS * 128)) * _PACK_ROWS
        out.append(p[r:r + nr].reshape(-1)[:n].reshape(shp))
        r += nr
    return out


def kernel(x, mem, g_mix, w_in, ln_v_g, ln_v_b, w_s, b_s, conv_w, g_mem, w_kv, g_head, w_o, g_ffn, w_ffn1, w_ffn2, g_final, loss_target, m_g_mix, m_w_in, m_ln_v_g, m_ln_v_b, m_w_s, m_b_s, m_conv_w, m_g_mem, m_w_kv, m_g_head, m_w_o, m_g_ffn, m_w_ffn1, m_w_ffn2, m_g_final, v_g_mix, v_w_in, v_ln_v_g, v_ln_v_b, v_w_s, v_b_s, v_conv_w, v_g_mem, v_w_kv, v_g_head, v_w_o, v_g_ffn, v_w_ffn1, v_w_ffn2, v_g_final):
    sds = jax.ShapeDtypeStruct
    xi, yi = lax.axis_index("x"), lax.axis_index("y")
    shard = 2 * xi + yi
    x2d, mem2d, tgt = x[0], mem[0], loss_target[0]
    ws3, bs2 = w_s[0], b_s[0]
    g_final2 = g_final.reshape(1, D)
    dff4 = DFF // NCHIP
    din4 = DIN // NCHIP
    dcv4 = DC // NCHIP

    big = [w_in[0].T, w_kv[0], w_o[0], w_ffn1[0], w_ffn2[0]]
    big_names = ["w_in", "w_kv", "w_o", "w_ffn1", "w_ffn2"]
    slot = jnp.reshape(shard, (1,)).astype(jnp.int32)
    core = jnp.reshape(lax.axis_index("c"), (1,)).astype(jnp.int32)
    conv_pad = jnp.pad(conv_w[0], ((0, CONV_PAD[0] - 3), (0, CONV_PAD[1] - dcv4)))
    conv_slots = lax.dynamic_update_slice(jnp.zeros((NCHIP,) + CONV_PAD, F32), conv_pad[None], (shard, 0, 0))

    gather_ids = {"in": 16, "kvo": 17, "ffn1": 18, "ffn2": 19, "ffn2d": 20}

    def gather_start(bufs, after, nm, forwards=()):
        return _allgather_start(bufs, forwards, after, gather_ids[nm], name="ag_start_" + nm)

    def gather_wait(state, idx, after, nm):
        send, recv, bufs, _ = state
        got, token = _transfer_wait([send[k] for k in idx], [recv[k] for k in idx], [[bufs[k]] for k in idx],
                                    [(N_NEIGHBOUR_CHIPS, bufs[k].shape[1] // 2) for k in idx], after, name="ag_wait_" + nm)
        return [g[0] for g in got], token

    cast = lambda k, after: _cast_into_slot(big[k], slot, after, name="cast_" + big_names[k])
    ag_in = gather_start([cast(0, slot), conv_slots], slot, "in")
    bs_t = bs2.T

    h = _rms_fwd(x2d, g_mix, name="rms_mix", after=[ag_in[3]])
    mem_n = _rms_fwd(mem2d, g_mem, name="rms_mem", after=[h])
    kvo_b = [cast(1, mem_n)]
    kvo_b.append(cast(2, kvo_b[0]))
    w1_b = cast(3, kvo_b[1])
    w2_b = cast(4, w1_b)
    got_in, tok = gather_wait(ag_in, [0, 1], w2_b, "in")
    win4, conv4 = _forward_gathered(got_in, tok, name="ag_fwd_in")
    ag_kvo = gather_start(kvo_b, conv4, "kvo")
    w_in_t = win4.reshape(DIN, D)
    conv_full = conv4[:, :3, :dcv4].transpose(1, 0, 2).reshape(3, DC)
    NEAR, FAR = [0, 1], [2]

    def diagonal_wait(state, ks, after, nm):
        send, recv, bufs, _ = state
        got, token = _transfer_wait([send[k] for k in ks], [recv[k] for k in ks], [[bufs[k]] for k in ks],
                                    [(1, bufs[k].shape[1] // 2) for k in ks], after, name="ag_waitd_" + nm)
        return [g[0] for g in got], token

    proj_w = lambda tn, tk: pl.BlockSpec((tn, tk), lambda j, i, k, s: (s[j], k))
    proj_cols = lambda tm, tn: [pl.BlockSpec((tm, tn), lambda j, i, k, s: (i, s[j]))]
    proj_half = lambda which, into, after: _matmul(
        h, w_in_t, name="mm_proj_%d" % which, tb=True, M=S, N=DIN // 2, K=D, tn=DIN // 2, b_spec=proj_w,
        out_specs=proj_cols, outs=[sds((S, DIN), F32)], slots=jnp.full((1,), which, jnp.int32), into=into,
        after=after)[0]
    proj = proj_half(0, [], [ag_kvo[3]])
    got_kvo, tok = gather_wait(ag_kvo, [0, 1], proj, "kvo")
    ag_w1 = gather_start([w1_b], tok, "ffn1", forwards=got_kvo)
    kvo_n = _forward_halves(ag_w1[2][1:], NEAR, ag_w1[3], name="ag_fwdn_kvo")
    ag_w1 = (ag_w1[0], ag_w1[1], [ag_w1[2][0]] + list(kvo_n), ag_w1[3])
    proj = proj_half(1, [proj], list(kvo_n))
    kvo_d, tok = diagonal_wait(ag_w1, [1, 2], proj, "kvo")
    wkv4, wo4 = _forward_halves(kvo_d, FAR, tok, name="ag_fwdd_kvo")
    w_kv_full = wkv4.reshape(D, 2 * DM)
    w_o_full = wo4.reshape(D, D)
    (kv,) = _matmul(mem_n, w_kv_full, name="mm_kv", M=NMEM, N=2 * DM, K=D, outs=[sds((NMEM, 2 * DM), F32)])
    heads, hn, ycv = _mix_fwd(proj, kv, ws3, bs_t, ln_v_g, ln_v_b, conv_full, g_head, name="mix_fwd")
    def residual_and_norm(acc, res, g):
        x2v = acc + res
        r = lax.rsqrt(jnp.mean(x2v * x2v, axis=-1, keepdims=True) + EPS)
        return x2v, (x2v * r) * g

    row_vec = lambda tm, tn: pl.BlockSpec((1, tn), lambda j, i, k, *s: (0, j))
    x2, h2 = _matmul(hn, w_o_full, name="mm_wo", M=S, N=D, K=D, tn=D, n_split=1, epi=residual_and_norm,
                     outs=[sds((S, D), F32), sds((S, D), BF16)], extras=[(x2d, _tile_spec()), (g_ffn, row_vec)])
    near = jnp.stack([shard, 2 * (1 - xi) + yi, 2 * xi + (1 - yi)]).astype(jnp.int32)
    far = jnp.reshape(2 * (1 - xi) + (1 - yi), (1,)).astype(jnp.int32)

    w1_shard = lambda tn, tk: pl.BlockSpec((None, tk, tn), lambda j, i, k, s: (s[j], k, 0))
    act_cols = lambda tm, tn: [pl.BlockSpec((tm, tn), lambda j, i, k, s: (i, s[j]))] * 2

    def relu2(acc):
        r = jnp.maximum(acc, 0.0)
        return r * r, 2.0 * r

    got_w1, tok = gather_wait(ag_w1, [0], h2, "ffn1")
    ag_w2 = gather_start([w2_b], tok, "ffn2", forwards=got_w1)
    (w1n,) = _forward_halves([ag_w2[2][1]], NEAR, ag_w2[3], name="ag_fwdn_ffn1")
    ag_w2 = (ag_w2[0], ag_w2[1], [ag_w2[2][0], w1n], ag_w2[3])
    act, dact_df = _matmul(h2, w1n, name="mm_ffn1_near", M=S, N=3 * dff4, K=D, tn=dff4, b_spec=w1_shard,
                           out_specs=act_cols, outs=[sds((S, DFF), BF16)] * 2, epi=relu2, slots=near)
    w1d, tok = diagonal_wait(ag_w2, [1], act, "ffn1")
    (w14,) = _forward_halves(w1d, FAR, tok, name="ag_fwdd_ffn1")
    act, dact_df = _matmul(h2, w14, name="mm_ffn1_far", M=S, N=dff4, K=D, tn=dff4, b_spec=w1_shard,
                           out_specs=act_cols, outs=[sds((S, DFF), BF16)] * 2, epi=relu2, slots=far,
                           into=[act, dact_df])

    act_shard = lambda tm, tk: pl.BlockSpec((tm, tk), lambda j, i, k, s: (i, s[k]))
    w2_shard = lambda tn, tk: pl.BlockSpec((None, tk, tn), lambda j, i, k, s: (s[k], 0, j))
    got_w2, tok = gather_wait(ag_w2, [0], act, "ffn2")
    ag_w2d = gather_start([], tok, "ffn2d", forwards=got_w2)
    (w2n,) = _forward_halves(ag_w2d[2], NEAR, ag_w2d[3], name="ag_fwdn_ffn2")
    ag_w2d = (ag_w2d[0], ag_w2d[1], [w2n], ag_w2d[3])
    (x3,) = _matmul(act, w2n, name="mm_ffn2_near", M=S, N=D, K=3 * dff4, tm=2 * TM, tk=dff4,
                    a_spec=act_shard, b_spec=w2_shard, outs=[sds((S, D), F32)], epi=lambda acc, res: (acc + res,),
                    extras=[(x2, _tile_spec())], slots=near)
    w2d, tok = diagonal_wait(ag_w2d, [0], x3, "ffn2")
    (w24,) = _forward_halves(w2d, FAR, tok, name="ag_fwdd_ffn2")
    (x3,) = _matmul(act, w24, name="mm_ffn2_far", M=S, N=D, K=dff4, tm=2 * TM, tk=dff4, a_spec=act_shard,
                    b_spec=w2_shard, outs=[sds((S, D), F32)], epi=lambda acc, res: (acc + res,),
                    extras=[(x3, _tile_spec())], slots=far)
    w2_full = w24.reshape(DFF, D)

    ci = lax.axis_index("c")

    def rs_sibling(g4, nm):
        return _sibling_start([g4], False, 1 + big_names.index(nm), name="rs_sib_" + nm)

    def rs_chips(state, after, nm):
        send, recv, g4, land, _ = state
        (((land_, g4_),), _) = _transfer_wait(send, recv, [[land[0], g4[0]]], [(NCHIP, land[0].shape[1])], after,
                                             name="rs_sibwait_" + nm)
        part, buf = _select_half_bf16(g4_, ci, land_, slot, name="rs_add_" + nm)
        return _scatter_start([part], [buf], 1 + 2 * len(big_names) + big_names.index(nm), name="rs_start_" + nm)

    def rs_end(state, after, nm):
        send, recv, parts, bufs, _ = state
        (((buf, _),), _) = _transfer_wait(send, recv, [[bufs[0], parts[0]]], [(N_PEER_CHIPS, bufs[0].shape[1])], after,
                                          name="rs_wait_" + nm)
        return _sibling_start([buf], True, 1 + len(big_names) + big_names.index(nm), name="rs_share_" + nm)

    big_m = [m_w_in[0].T, m_w_kv[0], m_w_o[0], m_w_ffn1[0], m_w_ffn2[0]]
    big_v = [v_w_in[0].T, v_w_kv[0], v_w_o[0], v_w_ffn1[0], v_w_ffn2[0]]
    big_out = {}

    def rs_finish(k, state, after):
        send, recv, mine, land, _ = state
        nm = big_names[k]
        (((land_, mine_),), _) = _transfer_wait(send, recv, [[land[0], mine[0]]], [(NCHIP, land[0].shape[1])], after,
                                               name="rs_sharewait_" + nm)
        big_out[nm] = _adamw(big[k], mine_, land_, big_m[k], big_v[k], core, name="adamw_" + nm)
        return big_out[nm][1]

    dx3, dx3b, dg_final, loss11 = _loss_bwd(x3, g_final2, tgt, name="loss_bwd")
    (dw2,) = _matmul(act, dx3b, name="mm_dw2", ta=True, M=DFF, N=D, K=S, tm=2 * TM, tn=D, outs=[sds((DFF, D), BF16)])
    sib_w2 = rs_sibling(dw2.reshape(NCHIP, dff4, D), "w_ffn2")
    (dfb,) = _matmul(dx3b, w2_full, name="mm_dact", tb=True, M=S, N=DFF, K=D, tm=2 * TM, tn=dff4, outs=[sds((S, DFF), BF16)],
                     epi=lambda acc, g: (acc * g.astype(F32),), extras=[(dact_df, _tile_spec())],
                     after=[sib_w2[4]])
    rs_w2 = rs_chips(sib_w2, dfb, "w_ffn2")

    def dw1_out(tm, tn):
        nb = dff4 // tn
        return [pl.BlockSpec((None, tm, tn), lambda j, i, k: (j // nb, i, j % nb))]

    (dw1,) = _matmul(h2, dfb, name="mm_dw1", ta=True, M=D, N=DFF, K=S, tm=2 * TM, tn=dff4, outs=[sds((NCHIP, D, dff4), BF16)],
                     out_specs=dw1_out, after=[rs_w2[4]])
    sib_w1 = rs_sibling(dw1, "w_ffn1")

    def w1_rows(tn, tk):
        kb = dff4 // tk
        return pl.BlockSpec((None, tn, tk), lambda j, i, k: (k // kb, j, k % kb))

    (dh2,) = _matmul(dfb, w14, name="mm_dh2", tb=True, M=S, N=D, K=DFF, tm=2 * TM, b_spec=w1_rows,
                     outs=[sds((S, D), F32)], after=[sib_w1[4]])
    rs_w1 = rs_chips(sib_w1, dh2, "w_ffn1")
    dx2, dx2b, dg_ffn = _rms_bwd(dh2, x2, g_ffn, dx3, name="rms_ffn_bwd", after=[rs_w1[4]])
    (dwo,) = _matmul(hn, dx2b, name="mm_dwo", ta=True, M=D, N=D, K=S, tm=2 * TM, outs=[sds((D, D), BF16)])
    sib_wo = rs_sibling(dwo.reshape(NCHIP, D // NCHIP, D), "w_o")
    (dhn,) = _matmul(dx2b, w_o_full, name="mm_dhn", tb=True, M=S, N=D, K=D, tm=2 * TM, outs=[sds((S, D), F32)],
                     after=[sib_wo[4]])
    rs_wo = rs_chips(sib_wo, dhn, "w_o")
    sh_w2 = rs_end(rs_w2, rs_wo[4], "w_ffn2")
    dproj, dkv, dws, dbs8, dlng, dlnb, dcw8, dgh = _mix_bwd(
        dhn, heads, proj, ycv, kv, ws3, bs_t, ln_v_g, ln_v_b, conv_full, g_head, sh_w2[4], name="mix_bwd")
    (dwin_t,) = _matmul(dproj, h, name="mm_dwin", ta=True, M=DIN, N=D, K=S, tm=DIN // 2, outs=[sds((DIN, D), BF16)])
    sib_win = rs_sibling(dwin_t.reshape(NCHIP, din4, D), "w_in")
    (dwkv,) = _matmul(mem_n, dkv, name="mm_dwkv", ta=True, M=D, N=2 * DM, K=NMEM, outs=[sds((D, 2 * DM), BF16)],
                      after=[sib_win[4]])
    sib_wkv = rs_sibling(dwkv.reshape(NCHIP, D // NCHIP, 2 * DM), "w_kv")
    (dh,) = _matmul(dproj, w_in_t, name="mm_dh", M=S, N=D, K=DIN, tm=2 * TM, tk=DIN, outs=[sds((S, D), F32)],
                    after=[sib_wkv[4]])
    rs_win = rs_chips(sib_win, dh, "w_in")
    rs_wkv = rs_chips(sib_wkv, rs_win[4], "w_kv")
    dx, dg_mix = _rms_bwd(dh, x2d, g_mix, dx2, name="rms_mix_bwd", want_bf=False, after=[rs_wkv[4]])
    sh_w1 = rs_end(rs_w1, dx, "w_ffn1")
    (dmem_n,) = _matmul(dkv, w_kv_full, name="mm_dmem", tb=True, M=NMEM, N=D, K=2 * DM, outs=[sds((NMEM, D), F32)],
                        after=[sh_w1[4]])
    (dg_mem,) = _rms_bwd(dmem_n, mem2d, g_mem, None, name="rms_mem_bwd", want_dx=False)
    sh_wo = rs_end(rs_wo, dg_mem, "w_o")
    done = rs_finish(4, sh_w2, sh_wo[4])
    done = rs_finish(3, sh_w1, done)
    sh_win = rs_end(rs_win, done, "w_in")
    sh_wkv = rs_end(rs_wkv, sh_win[4], "w_kv")
    done = rs_finish(2, sh_wo, sh_wkv[4])
    done = rs_finish(0, sh_win, done)
    done = rs_finish(1, sh_wkv, done)

    small_names = ["g_mix", "ln_v_g", "ln_v_b", "w_s", "b_s", "conv_w", "g_mem", "g_head", "g_ffn", "g_final"]
    small_part = [dg_mix, dlng, dlnb, dws, dbs8[:, 0, :], dcw8[:3], dg_mem, dgh, dg_ffn, dg_final, loss11]
    small_shapes = [(1, D), (1, DS), (1, DS), (NSH, CHUNK, CHUNK), (NSH, CHUNK), (3, DC), (1, D), (1, D), (1, D), (1, D),
                    (1, 1)]
    total = _allreduce_small(_pack(small_part), done, name="allreduce_small")
    small_g = _unpack(total, small_shapes)
    loss = small_g.pop()[0, 0]
    small_g[5] = lax.dynamic_slice(small_g[5], (0, shard * dcv4), (3, dcv4))
    small_w = [g_mix, ln_v_g, ln_v_b, ws3, bs2, conv_w[0], g_mem, g_head, g_ffn, g_final2]
    small_m = [m_g_mix, m_ln_v_g, m_ln_v_b, m_w_s[0], m_b_s[0], m_conv_w[0], m_g_mem, m_g_head, m_g_ffn,
               m_g_final.reshape(1, D)]
    small_v = [v_g_mix, v_ln_v_g, v_ln_v_b, v_w_s[0], v_b_s[0], v_conv_w[0], v_g_mem, v_g_head, v_g_ffn,
               v_g_final.reshape(1, D)]
    s_delta, s_m, s_v = _adamw_small(small_w, small_g, small_m, small_v, name="adamw_small")
    small_out = {nm: (g, d, mn, vn) for nm, g, d, mn, vn in zip(small_names, small_g, s_delta, s_m, s_v)}

    order = ["g_mix", "w_in", "ln_v_g", "ln_v_b", "w_s", "b_s", "conv_w", "g_mem", "w_kv", "g_head", "w_o",
             "g_ffn", "w_ffn1", "w_ffn2", "g_final"]
    like = dict(g_mix=g_mix, w_in=w_in, ln_v_g=ln_v_g, ln_v_b=ln_v_b, w_s=w_s, b_s=b_s, conv_w=conv_w, g_mem=g_mem,
                w_kv=w_kv, g_head=g_head, w_o=w_o, g_ffn=g_ffn, w_ffn1=w_ffn1, w_ffn2=w_ffn2, g_final=g_final)
    res = {**big_out, **small_out}
    res["w_in"] = [a.T for a in res["w_in"]]
    outs = [loss, dx[None]]
    for k in range(4):
        outs += [res[nm][k].reshape(like[nm].shape) for nm in order]
    return tuple(outs)
```

```python
import math

import jax
import jax.numpy as jnp
from jax import lax
from jax.experimental import pallas as pl
from jax.experimental.pallas import tpu as pltpu

F32 = jnp.float32
BF16 = jnp.bfloat16
MESH = pl.DeviceIdType.MESH

D = 2048
S = 2048
HD = 128
NH = D // HD
NMH = 4
NSH = (NH - NMH) // 2
NCH = NH - NMH - NSH
DS = NSH * HD
DC = NCH * HD
DM = NMH * HD
DIN = 2 * DS + 3 * DC + DM
CHUNK = 128
NMEM = 256
DFF = 4 * D
EPS = 1e-6
NCHIP = 4
SCALE = HD ** -0.5

ADAM_LR = 0.001
ADAM_B1 = 0.9
ADAM_B2 = 0.999
ADAM_EPS = 1e-08
ADAM_WD = 0.01
ADAM_STEP = 10

TR_EW = 256
TR_MIX = 256
TM = 512
TN = 1024
TK = 2048
N_SUB = 512
VMEM_MB = 56
HALO = 8


def _pick(n, target, q=128):
    best = None
    for t in range(q, min(n, target) + 1, q):
        if n % t == 0:
            best = t
    return n if best is None else best


def _pick_rows(n, q):
    below = _pick(n, TR_EW, q)
    if 2 * below >= TR_EW:
        return below
    above = [t for t in range(TR_EW, min(n, 4 * TR_EW) + 1, q) if n % t == 0]
    return above[0] if above else below


def _cp(sem=None, vmem_mb=None, **kw):
    d = dict(kw)
    if sem is not None:
        d["dimension_semantics"] = sem
    if vmem_mb is not None:
        d["vmem_limit_bytes"] = vmem_mb << 20
    return pltpu.CompilerParams(**d)


def _gelu(x):
    z = 0.7978845608028654 * (x + 0.044715 * (x * x * x))
    return 0.5 * x * (1.0 + jnp.tanh(z))


def _gelu_with_grad(x):
    x2 = x * x
    t = jnp.tanh(0.7978845608028654 * (x + 0.044715 * (x2 * x)))
    half = 0.5 * (1.0 + t)
    return x * half, half + 0.5 * x * (1.0 - t * t) * (0.7978845608028654 * (1.0 + 3.0 * 0.044715 * x2))


def _matmul(a, b, *, name, ta=False, tb=False, M, N, K, tm=None, tn=None, tk=None, outs, epi=None,
            extras=(), a_spec=None, b_spec=None, out_specs=None, after=(), n_split=None, slots=None, into=()):
    n_after = len(after)
    tm = _pick(M, TM if tm is None else tm, 8)
    tn = _pick(N, TN if tn is None else tn)
    tk = _pick(K, TK if tk is None else tk)
    if n_split is None:
        n_split = tn // N_SUB if tn % N_SUB == 0 else 1
    nk = K // tk
    grid = (N // tn, M // tm, nk)
    if a_spec is None:
        a_spec = (pl.BlockSpec((tk, tm), lambda j, i, k, *s: (k, i)) if ta
                  else pl.BlockSpec((tm, tk), lambda j, i, k, *s: (i, k)))
    else:
        a_spec = a_spec(tm, tk)
    if b_spec is None:
        b_spec = (pl.BlockSpec((tn, tk), lambda j, i, k, *s: (j, k)) if tb
                  else pl.BlockSpec((tk, tn), lambda j, i, k, *s: (k, j)))
    else:
        b_spec = b_spec(tn, tk)
    if out_specs is None:
        out_specs = [pl.BlockSpec((tm, tn), lambda j, i, k, *s: (i, j)) for _ in outs]
    else:
        out_specs = out_specs(tm, tn)
    dn = (((0 if ta else 1,), (1 if tb else 0,)), ((), ()))
    n_ex, n_out = len(extras), len(outs)
    n_pre = 0 if slots is None else 1
    n_into = len(into)
    ns = tn // n_split

    def body(*refs):
        a_ref, b_ref = refs[n_pre], refs[n_pre + 1]
        ex = refs[n_pre + 2:n_pre + 2 + n_ex]
        first_out = n_pre + 2 + n_ex + n_after + n_into
        o = refs[first_out:first_out + n_out]
        acc = refs[first_out + n_out:]
        k = pl.program_id(2)

        def finish(val, cols):
            res = (val,) if epi is None else epi(val, *[e[:, cols] for e in ex])
            for r, o_ref in zip(res, o):
                o_ref[:, cols] = r.astype(o_ref.dtype)

        if nk > 1:
            @pl.when(k == 0)
            def _():
                acc[0][...] = jnp.zeros_like(acc[0])

        av = a_ref[...].astype(BF16)
        for q in range(n_split):
            cols = slice(q * ns, (q + 1) * ns)
            bq = (b_ref[cols, :] if tb else b_ref[:, cols]).astype(BF16)
            part = lax.dot_general(av, bq, dn, preferred_element_type=F32)
            if nk == 1:
                finish(part, cols)
            else:
                acc[0][:, cols] += part

        if nk > 1:
            @pl.when(k == nk - 1)
            def _():
                finish(acc[0][...], slice(0, tn))

    in_specs = ([a_spec, b_spec] + [sp(tm, tn) for _, sp in extras] + [ANY] * (n_after + n_into))
    scratch = [pltpu.VMEM((tm, tn), F32)] if nk > 1 else []
    args = [a, b] + [arr for arr, _ in extras] + list(after) + list(into)
    aliases = {n_pre + len(args) - n_into + t: t for t in range(n_into)}
    params = _cp(("parallel", "parallel", "arbitrary"), VMEM_MB)
    if slots is None:
        return pl.pallas_call(body, name=name, grid=grid, in_specs=in_specs, out_specs=out_specs, out_shape=outs,
                              scratch_shapes=scratch, input_output_aliases=aliases, compiler_params=params)(*args)
    return pl.pallas_call(
        body, name=name,
        grid_spec=pltpu.PrefetchScalarGridSpec(num_scalar_prefetch=1, grid=grid, in_specs=in_specs,
                                               out_specs=out_specs, scratch_shapes=scratch),
        out_shape=outs, input_output_aliases=aliases, compiler_params=params)(slots, *args)


def _tile_spec():
    return lambda tm, tn: pl.BlockSpec((tm, tn), lambda j, i, k, *s: (i, j))


def _cast_into_slot(w, slot, after, *, name):
    R, C = w.shape
    tr = _pick_rows(R, 16)

    def body(s_ref, w_ref, _after_ref, o_ref):
        o_ref[...] = w_ref[...].astype(BF16)

    return pl.pallas_call(
        body, name=name,
        grid_spec=pltpu.PrefetchScalarGridSpec(
            num_scalar_prefetch=1, grid=(R // tr,),
            in_specs=[pl.BlockSpec((tr, C), lambda i, s: (i, 0)), ANY],
            out_specs=pl.BlockSpec((None, tr, C), lambda i, s: (s[0], i, 0))),
        out_shape=jax.ShapeDtypeStruct((NCHIP, R, C), BF16),
        compiler_params=_cp(("parallel",), VMEM_MB),
    )(slot, w, after)


def _own_slot(part, slot, *, name):
    _, R, C = part.shape
    tr = _pick_rows(R, 16)

    def body(s_ref, p_ref, o_ref):
        o_ref[...] = p_ref[...]

    spec = pl.BlockSpec((None, tr, C), lambda i, s: (s[0], i, 0))
    return pl.pallas_call(
        body, name=name,
        grid_spec=pltpu.PrefetchScalarGridSpec(num_scalar_prefetch=1, grid=(R // tr,), in_specs=[spec],
                                               out_specs=spec),
        out_shape=jax.ShapeDtypeStruct(part.shape, part.dtype),
        compiler_params=_cp(("parallel",), VMEM_MB),
    )(slot, part)


def _rms_fwd(x, g, *, name, after=()):
    R, C = x.shape
    tr = _pick(R, TR_EW, 16)
    n_after = len(after)

    def body(x_ref, g_ref, *rest):
        o_ref = rest[n_after]
        xv = x_ref[...]
        r = lax.rsqrt(jnp.mean(xv * xv, axis=-1, keepdims=True) + EPS)
        o_ref[...] = ((xv * r) * g_ref[...]).astype(BF16)

    return pl.pallas_call(
        body, name=name, grid=(R // tr,),
        in_specs=[pl.BlockSpec((tr, C), lambda i: (i, 0)), pl.BlockSpec((1, C), lambda i: (0, 0))] + [ANY] * n_after,
        out_specs=pl.BlockSpec((tr, C), lambda i: (i, 0)),
        out_shape=jax.ShapeDtypeStruct((R, C), BF16),
        compiler_params=_cp(("parallel",), VMEM_MB),
    )(x, g, *after)


def _rms_bwd(dh, x, g, dres, *, name, want_dx=True, want_bf=True, after=()):
    R, C = x.shape
    tr = _pick(R, TR_EW, 16)
    has_res = dres is not None
    row = pl.BlockSpec((tr, C), lambda i: (i, 0))
    vec = pl.BlockSpec((1, C), lambda i: (0, 0))

    def body(*refs):
        dh_ref, x_ref, g_ref = refs[:3]
        pos = 3
        dres_ref = None
        if has_res:
            dres_ref = refs[pos]
            pos += 1
        outs = refs[pos + len(after):]
        i = pl.program_id(0)
        xv = x_ref[...]
        r = lax.rsqrt(jnp.mean(xv * xv, axis=-1, keepdims=True) + EPS)
        xh = xv * r
        dhv = dh_ref[...]
        dg_ref = outs[-1]
        dgp = jnp.sum(dhv * xh, axis=0, keepdims=True)

        @pl.when(i == 0)
        def _():
            dg_ref[...] = dgp

        @pl.when(i > 0)
        def _():
            dg_ref[...] += dgp

        if want_dx:
            t = dhv * g_ref[...]
            dx = r * (t - xh * jnp.mean(t * xh, axis=-1, keepdims=True))
            if has_res:
                dx = dx + dres_ref[...]
            outs[0][...] = dx
            if want_bf:
                outs[1][...] = dx.astype(BF16)

    in_specs = [row, row, vec] + ([row] if has_res else []) + [ANY] * len(after)
    out_specs, out_shape = [], []
    if want_dx:
        out_specs.append(row)
        out_shape.append(jax.ShapeDtypeStruct((R, C), F32))
        if want_bf:
            out_specs.append(row)
            out_shape.append(jax.ShapeDtypeStruct((R, C), BF16))
    out_specs.append(vec)
    out_shape.append(jax.ShapeDtypeStruct((1, C), F32))
    args = [dh, x, g] + ([dres] if has_res else []) + list(after)
    return pl.pallas_call(
        body, name=name, grid=(R // tr,), in_specs=in_specs, out_specs=out_specs, out_shape=out_shape,
        compiler_params=_cp(("arbitrary",), VMEM_MB),
    )(*args)


def _loss_bwd(x3, g, tgt, *, name):
    R, C = x3.shape
    tr = _pick(R, TR_EW, 16)
    n = R // tr
    row = pl.BlockSpec((tr, C), lambda i: (i, 0))
    vec = pl.BlockSpec((1, C), lambda i: (0, 0))

    def body(x_ref, g_ref, t_ref, dx_ref, dxb_ref, dg_ref, loss_ref, acc_ref):
        i = pl.program_id(0)
        xv = x_ref[...]
        gv = g_ref[...]
        r = lax.rsqrt(jnp.mean(xv * xv, axis=-1, keepdims=True) + EPS)
        xh = xv * r
        e = xh * gv - t_ref[...]
        dy = e * (1.0 / C)
        sq = jnp.sum(e * e, axis=0, keepdims=True)
        dgp = jnp.sum(dy * xh, axis=0, keepdims=True)

        @pl.when(i == 0)
        def _():
            acc_ref[...] = sq
            dg_ref[...] = dgp

        @pl.when(i > 0)
        def _():
            acc_ref[...] += sq
            dg_ref[...] += dgp

        t = dy * gv
        dx = r * (t - xh * jnp.mean(t * xh, axis=-1, keepdims=True))
        dx_ref[...] = dx
        dxb_ref[...] = dx.astype(BF16)

        @pl.when(i == n - 1)
        def _():
            loss_ref[...] = jnp.sum(acc_ref[...], axis=-1, keepdims=True) * (0.5 / C)

    return pl.pallas_call(
        body, name=name, grid=(n,),
        in_specs=[row, vec, row],
        out_specs=[row, row, vec, pl.BlockSpec((1, 1), lambda i: (0, 0))],
        out_shape=[jax.ShapeDtypeStruct((R, C), F32), jax.ShapeDtypeStruct((R, C), BF16),
                   jax.ShapeDtypeStruct((1, C), F32), jax.ShapeDtypeStruct((1, 1), F32)],
        scratch_shapes=[pltpu.VMEM((1, C), F32)],
        compiler_params=_cp(("arbitrary",), VMEM_MB),
    )(x3, g, tgt)


def _offsets():
    u0 = 0
    v0 = DS
    b0 = 2 * DS
    c0 = b0 + DC
    x0 = c0 + DC
    q0 = x0 + DC
    return u0, v0, b0, c0, x0, q0


def _tri_mask(lower):
    r = lax.broadcasted_iota(jnp.int32, (CHUNK, CHUNK), 0)
    c = lax.broadcasted_iota(jnp.int32, (CHUNK, CHUNK), 1)
    return (r >= c) if lower else (c >= r)


def _layer_norm_stats(vg):
    mu = jnp.mean(vg, axis=-1, keepdims=True)
    vc = vg - mu
    rstd = lax.rsqrt(jnp.mean(vc * vc, axis=-1, keepdims=True) + EPS)
    return vc * rstd, rstd


def _softmax_rows(qh, kh):
    s = lax.dot_general(qh, kh, (((1,), (1,)), ((), ())), preferred_element_type=F32)
    m = jnp.max(s, axis=-1, keepdims=True)
    e = jnp.exp(s - m)
    return e / jnp.sum(e, axis=-1, keepdims=True)


def _mix_fwd(proj, kv, w_s, bs_t, ln_g, ln_b, conv_w, g_head, *, name):
    assert DS == DC
    tr = _pick(S, TR_MIX, CHUNK)
    n = S // tr
    nck = tr // CHUNK
    u0, v0, b0, c0, x0, q0 = _offsets()
    hb = tr // HALO

    def body(p_ref, cprev_ref, xprev_ref, kv_ref, ws_ref, bst_ref, lng_ref, lnb_ref, cw_ref, gh_ref,
             heads_ref, hn_ref, ycv_ref, buf_ref):
        i = pl.program_id(0)

        def emit(col, val):
            rs = lax.rsqrt(jnp.mean(val * val, axis=-1, keepdims=True) + EPS)
            heads_ref[:, col:col + HD] = val
            hn_ref[:, col:col + HD] = ((val * rs) * gh_ref[:, col:col + HD]).astype(BF16)

        vhat, _ = _layer_norm_stats(_gelu(p_ref[:, v0:v0 + DS]))
        vnb = (vhat * lng_ref[...] + lnb_ref[...]).astype(BF16)
        low = _tri_mask(True)
        for h in range(NSH):
            wt = jnp.where(low, ws_ref[h], 0.0).astype(BF16)
            bcol = bst_ref[:, h:h + 1]
            parts = []
            for c in range(nck):
                blk = vnb[c * CHUNK:(c + 1) * CHUNK, h * HD:(h + 1) * HD]
                parts.append(jnp.dot(wt, blk, preferred_element_type=F32) + bcol)
            mixed = parts[0] if nck == 1 else jnp.concatenate(parts, axis=0)
            emit(h * HD, _gelu(p_ref[:, u0 + h * HD:u0 + (h + 1) * HD]) * mixed)

        xc = p_ref[:, c0:c0 + DC] * p_ref[:, x0:x0 + DC]
        prev = cprev_ref[...] * xprev_ref[...]
        buf_ref[0:HALO, :] = jnp.where(i > 0, prev, 0.0)
        buf_ref[HALO:HALO + tr, :] = xc
        y = (cw_ref[2:3, :] * xc + cw_ref[1:2, :] * buf_ref[HALO - 1:HALO - 1 + tr, :]
             + cw_ref[0:1, :] * buf_ref[HALO - 2:HALO - 2 + tr, :])
        ycv_ref[...] = y
        cout = p_ref[:, b0:b0 + DC] * y
        for h in range(NCH):
            emit(DS + h * HD, cout[:, h * HD:(h + 1) * HD])

        for h in range(NMH):
            qh = (p_ref[:, q0 + h * HD:q0 + (h + 1) * HD] * SCALE).astype(BF16)
            kh = kv_ref[:, h * HD:(h + 1) * HD].astype(BF16)
            vh = kv_ref[:, DM + h * HD:DM + (h + 1) * HD].astype(BF16)
            p = _softmax_rows(qh, kh)
            emit(DS + DC + h * HD, jnp.dot(p.astype(BF16), vh, preferred_element_type=F32))

    full = lambda shape: pl.BlockSpec(shape, lambda i: (0,) * len(shape))
    halo_c = pl.BlockSpec((HALO, DC), lambda i: (jnp.maximum(i * hb - 1, 0), c0 // DC))
    halo_x = pl.BlockSpec((HALO, DC), lambda i: (jnp.maximum(i * hb - 1, 0), x0 // DC))
    return pl.pallas_call(
        body, name=name, grid=(n,),
        in_specs=[pl.BlockSpec((tr, DIN), lambda i: (i, 0)), halo_c, halo_x,
                  full((NMEM, 2 * DM)), full((NSH, CHUNK, CHUNK)), full((CHUNK, NSH)),
                  full((1, DS)), full((1, DS)), full((3, DC)), full((1, D))],
        out_specs=[pl.BlockSpec((tr, D), lambda i: (i, 0)), pl.BlockSpec((tr, D), lambda i: (i, 0)),
                   pl.BlockSpec((tr, DC), lambda i: (i, 0))],
        out_shape=[jax.ShapeDtypeStruct((S, D), F32), jax.ShapeDtypeStruct((S, D), BF16),
                   jax.ShapeDtypeStruct((S, DC), F32)],
        scratch_shapes=[pltpu.VMEM((tr + HALO, DC), F32)],
        compiler_params=_cp(("parallel",), VMEM_MB),
    )(proj, proj, proj, kv, w_s, bs_t, ln_g, ln_b, conv_w, g_head)


def _mix_bwd(dhn, heads, proj, ycv, kv, w_s, bs_t, ln_g, ln_b, conv_w, g_head, after, *, name):
    assert DS == DC
    tr = _pick(S, TR_MIX, CHUNK)
    n = S // tr
    nck = tr // CHUNK
    u0, v0, b0, c0, x0, q0 = _offsets()
    hb = tr // HALO
    last_hb = S // HALO - 1

    def body(dhn_ref, heads_ref, p_ref, ycv_ref, dhn_nx_ref, heads_nx_ref, b_nx_ref, kv_ref, ws_ref, bst_ref,
             lng_ref, lnb_ref, cw_ref, gh_ref, _after_ref,
             dp_ref, dkv_ref, dws_ref, dbs_ref, dlng_ref, dlnb_ref, dcw_ref, dgh_ref, buf_ref, dvn_ref):
        i = pl.program_id(0)

        @pl.when(i == 0)
        def _():
            dkv_ref[...] = jnp.zeros_like(dkv_ref)
            dws_ref[...] = jnp.zeros_like(dws_ref)
            dbs_ref[...] = jnp.zeros_like(dbs_ref)
            dlng_ref[...] = jnp.zeros_like(dlng_ref)
            dlnb_ref[...] = jnp.zeros_like(dlnb_ref)
            dcw_ref[...] = jnp.zeros_like(dcw_ref)
            dgh_ref[...] = jnp.zeros_like(dgh_ref)

        def head_bwd(a, dn, gh):
            rs = lax.rsqrt(jnp.mean(a * a, axis=-1, keepdims=True) + EPS)
            ah = a * rs
            t = dn * gh
            return rs * (t - ah * jnp.mean(t * ah, axis=-1, keepdims=True)), jnp.sum(dn * ah, axis=0, keepdims=True)

        def head_grad(col):
            da, dg = head_bwd(heads_ref[:, col:col + HD], dhn_ref[:, col:col + HD], gh_ref[:, col:col + HD])
            dgh_ref[:, col:col + HD] += dg
            return da

        vg, dvg_dv = _gelu_with_grad(p_ref[:, v0:v0 + DS])
        vhat, rstd = _layer_norm_stats(vg)
        vnb = (vhat * lng_ref[...] + lnb_ref[...]).astype(BF16)
        low = _tri_mask(True)
        ones = jnp.ones((HALO, HD), BF16)
        for h in range(NSH):
            w_h = ws_ref[h]
            wt = jnp.where(low, w_h, 0.0).astype(BF16)
            bcol = bst_ref[:, h:h + 1]
            da = head_grad(h * HD)
            ug, dug_du = _gelu_with_grad(p_ref[:, u0 + h * HD:u0 + (h + 1) * HD])
            dws = jnp.zeros((CHUNK, CHUNK), F32)
            dbs = jnp.zeros((HALO, CHUNK), F32)
            mixed_parts = []
            for c in range(nck):
                rows = slice(c * CHUNK, (c + 1) * CHUNK)
                blk = vnb[rows, h * HD:(h + 1) * HD]
                mixed_parts.append(jnp.dot(wt, blk, preferred_element_type=F32) + bcol)
                dmb = (da[rows] * ug[rows]).astype(BF16)
                dws = dws + lax.dot_general(dmb, blk, (((1,), (1,)), ((), ())), preferred_element_type=F32)
                dbs = dbs + lax.dot_general(ones, dmb, (((1,), (1,)), ((), ())), preferred_element_type=F32)
                dvn_ref[c * CHUNK:(c + 1) * CHUNK, h * HD:(h + 1) * HD] = lax.dot_general(
                    wt, dmb, (((0,), (0,)), ((), ())), preferred_element_type=F32)
            mixed = mixed_parts[0] if nck == 1 else jnp.concatenate(mixed_parts, axis=0)
            dp_ref[:, u0 + h * HD:u0 + (h + 1) * HD] = ((da * mixed) * dug_du).astype(BF16)
            dws_ref[h] += jnp.where(low, dws, 0.0)
            dbs_ref[h] += dbs
        dvn = dvn_ref[...]
        dlng_ref[...] += jnp.sum(dvn * vhat, axis=0, keepdims=True)
        dlnb_ref[...] += jnp.sum(dvn, axis=0, keepdims=True)
        dvh = dvn * lng_ref[...]
        dvg = rstd * (dvh - jnp.mean(dvh, axis=-1, keepdims=True)
                      - vhat * jnp.mean(dvh * vhat, axis=-1, keepdims=True))
        dp_ref[:, v0:v0 + DS] = (dvg * dvg_dv).astype(BF16)

        dc = jnp.concatenate([head_grad(DS + h * HD) for h in range(NCH)], axis=1)
        dc_nx = jnp.concatenate(
            [head_bwd(heads_nx_ref[:, h * HD:(h + 1) * HD], dhn_nx_ref[:, h * HD:(h + 1) * HD],
                      gh_ref[:, DS + h * HD:DS + (h + 1) * HD])[0] for h in range(NCH)], axis=1)
        bg = p_ref[:, b0:b0 + DC]
        cg = p_ref[:, c0:c0 + DC]
        xin = p_ref[:, x0:x0 + DC]
        dp_ref[:, b0:b0 + DC] = (dc * ycv_ref[...]).astype(BF16)
        dyv = dc * bg
        buf_ref[0:tr, :] = dyv
        buf_ref[tr:tr + HALO, :] = jnp.where(i < n - 1, dc_nx * b_nx_ref[...], 0.0)
        sh1 = buf_ref[1:1 + tr, :]
        sh0 = buf_ref[2:2 + tr, :]
        dxc = cw_ref[2:3, :] * dyv + cw_ref[1:2, :] * sh1 + cw_ref[0:1, :] * sh0
        xc = cg * xin
        dp_ref[:, c0:c0 + DC] = (dxc * xin).astype(BF16)
        dp_ref[:, x0:x0 + DC] = (dxc * cg).astype(BF16)
        dcw_ref[0:1, :] += jnp.sum(sh0 * xc, axis=0, keepdims=True)
        dcw_ref[1:2, :] += jnp.sum(sh1 * xc, axis=0, keepdims=True)
        dcw_ref[2:3, :] += jnp.sum(dyv * xc, axis=0, keepdims=True)

        for h in range(NMH):
            do = head_grad(DS + DC + h * HD).astype(BF16)
            qh = (p_ref[:, q0 + h * HD:q0 + (h + 1) * HD] * SCALE).astype(BF16)
            kh = kv_ref[:, h * HD:(h + 1) * HD].astype(BF16)
            vh = kv_ref[:, DM + h * HD:DM + (h + 1) * HD].astype(BF16)
            p = _softmax_rows(qh, kh)
            dpr = lax.dot_general(do, vh, (((1,), (1,)), ((), ())), preferred_element_type=F32)
            ds = (p * (dpr - jnp.sum(dpr * p, axis=-1, keepdims=True))).astype(BF16)
            dp_ref[:, q0 + h * HD:q0 + (h + 1) * HD] = (
                jnp.dot(ds, kh, preferred_element_type=F32) * SCALE).astype(BF16)
            dkv_ref[:, h * HD:(h + 1) * HD] += lax.dot_general(
                ds, qh, (((0,), (0,)), ((), ())), preferred_element_type=F32)
            dkv_ref[:, DM + h * HD:DM + (h + 1) * HD] += lax.dot_general(
                p.astype(BF16), do, (((0,), (0,)), ((), ())), preferred_element_type=F32)

    full = lambda shape: pl.BlockSpec(shape, lambda i: (0,) * len(shape))
    row = lambda c: pl.BlockSpec((tr, c), lambda i: (i, 0))
    nxt = lambda col: pl.BlockSpec((HALO, DC), lambda i: (jnp.minimum((i + 1) * hb, last_hb), col))
    return pl.pallas_call(
        body, name=name, grid=(n,),
        in_specs=[row(D), row(D), row(DIN), row(DC), nxt(DS // DC), nxt(DS // DC), nxt(b0 // DC),
                  full((NMEM, 2 * DM)), full((NSH, CHUNK, CHUNK)), full((CHUNK, NSH)),
                  full((1, DS)), full((1, DS)), full((3, DC)), full((1, D)), ANY],
        out_specs=[row(DIN), full((NMEM, 2 * DM)), full((NSH, CHUNK, CHUNK)), full((NSH, HALO, CHUNK)),
                   full((1, DS)), full((1, DS)), full((HALO, DC)), full((1, D))],
        out_shape=[jax.ShapeDtypeStruct((S, DIN), BF16), jax.ShapeDtypeStruct((NMEM, 2 * DM), F32),
                   jax.ShapeDtypeStruct((NSH, CHUNK, CHUNK), F32), jax.ShapeDtypeStruct((NSH, HALO, CHUNK), F32),
                   jax.ShapeDtypeStruct((1, DS), F32), jax.ShapeDtypeStruct((1, DS), F32),
                   jax.ShapeDtypeStruct((HALO, DC), F32), jax.ShapeDtypeStruct((1, D), F32)],
        scratch_shapes=[pltpu.VMEM((tr + HALO, DC), F32), pltpu.VMEM((tr, DS), F32)],
        compiler_params=_cp(("arbitrary",), VMEM_MB),
    )(dhn, heads, proj, ycv, dhn, heads, proj, kv, w_s, bs_t, ln_g, ln_b, conv_w, g_head, after)


def _place():
    x, y, c = lax.axis_index("x"), lax.axis_index("y"), lax.axis_index("c")
    chips = [(1 - x, y), (x, 1 - y), (1 - x, 1 - y)]
    return x, y, c, chips


ANY = pl.BlockSpec(memory_space=pl.ANY)


HBM = pl.BlockSpec(memory_space=pltpu.HBM)
SEM = pl.BlockSpec(memory_space=pltpu.SEMAPHORE)
EFFECT = pltpu.SideEffectType.DATAFLOW_SIDE_EFFECTING
N_PEER_CHIPS = 3
N_NEIGHBOUR_CHIPS = 2
CONV_PAD = (32, 256)


def _in_hbm(a):
    return pltpu.with_memory_space_constraint(a, pltpu.HBM)


def _allgather_start(bufs, forwards, after, collective_id, *, name):
    arrs = list(bufs) + list(forwards)
    nw, nb = len(arrs), len(bufs)

    def body(*refs):
        ins, send, recv = refs[:nw], refs[nw + 1:2 * nw + 1], refs[2 * nw + 1:3 * nw + 1]
        token = refs[4 * nw + 1]
        x, y, c, chips = _place()
        s = 2 * x + y
        slots = [2 * cx + cy for cx, cy in chips]
        _handshake([(cx, cy, c) for cx, cy in chips[:N_NEIGHBOUR_CHIPS]])
        for w in range(nb, nw):
            q = arrs[w].shape[1] // 4
            for j in range(N_NEIGHBOUR_CHIPS):
                rows = ins[w].at[slots[j], pl.ds(c * 2 * q + j * q, q)]
                pltpu.make_async_remote_copy(src_ref=rows, dst_ref=rows, send_sem=send[w], recv_sem=recv[w],
                                             device_id=(*chips[1 - j], c), device_id_type=MESH).start()
        for w in range(nb):
            hr = arrs[w].shape[1] // 2
            rows = ins[w].at[s, pl.ds(c * hr, hr)]
            for cx, cy in chips[:N_NEIGHBOUR_CHIPS]:
                pltpu.make_async_remote_copy(src_ref=rows, dst_ref=rows, send_sem=send[w], recv_sem=recv[w],
                                             device_id=(cx, cy, c), device_id_type=MESH).start()
        token[...] = jnp.zeros_like(token)

    res = pl.pallas_call(
        body, name=name,
        in_specs=[HBM] * nw + [ANY],
        out_specs=[SEM] * (2 * nw) + [HBM] * nw + [pl.BlockSpec(memory_space=pltpu.VMEM)],
        out_shape=[pltpu.SemaphoreType.DMA(())] * (2 * nw) + [pltpu.HBM(a.shape, a.dtype) for a in arrs]
        + [jax.ShapeDtypeStruct((8, 128), F32)],
        input_output_aliases={w: 2 * nw + w for w in range(nw)},
        compiler_params=pltpu.CompilerParams(has_side_effects=EFFECT, collective_id=collective_id),
    )(*[_in_hbm(a) for a in arrs], after)
    return res[:nw], res[nw:2 * nw], res[2 * nw:3 * nw], res[3 * nw]


def _handshake(peers):
    barrier = pltpu.get_barrier_semaphore()
    for peer in peers:
        pl.semaphore_signal(barrier, inc=1, device_id=peer, device_id_type=MESH)
    pl.semaphore_wait(barrier, len(peers))


def _scatter_start(parts, bufs, collective_id, *, name):
    nw = len(parts)

    def body(*refs):
        src, dst = refs[:nw], refs[nw:2 * nw]
        send, recv = refs[2 * nw:3 * nw], refs[3 * nw:4 * nw]
        token = refs[6 * nw]
        x, y, c, chips = _place()
        s = 2 * x + y
        _handshake([(cx, cy, c) for cx, cy in chips])
        for w in range(nw):
            for cx, cy in chips:
                pltpu.make_async_remote_copy(src_ref=src[w].at[2 * cx + cy], dst_ref=dst[w].at[s], send_sem=send[w],
                                             recv_sem=recv[w], device_id=(cx, cy, c), device_id_type=MESH).start()
        token[...] = jnp.zeros_like(token)

    res = pl.pallas_call(
        body, name=name,
        in_specs=[HBM] * (2 * nw),
        out_specs=[SEM] * (2 * nw) + [HBM] * (2 * nw) + [pl.BlockSpec(memory_space=pltpu.VMEM)],
        out_shape=[pltpu.SemaphoreType.DMA(())] * (2 * nw) + [pltpu.HBM(a.shape, a.dtype) for a in parts + bufs]
        + [jax.ShapeDtypeStruct((8, 128), F32)],
        input_output_aliases={k: 2 * nw + k for k in range(2 * nw)},
        compiler_params=pltpu.CompilerParams(has_side_effects=EFFECT, collective_id=collective_id),
    )(*[_in_hbm(a) for a in parts + bufs])
    return res[:nw], res[nw:2 * nw], res[2 * nw:3 * nw], res[3 * nw:4 * nw], res[4 * nw]


def _sibling_start(srcs, whole, collective_id, *, name):
    nw = len(srcs)
    lands = [lax.empty((a.shape[0], a.shape[1] if whole else a.shape[1] // 2, a.shape[2]), a.dtype) for a in srcs]

    def body(*refs):
        src, land = refs[:nw], refs[nw:2 * nw]
        send, recv = refs[2 * nw:3 * nw], refs[3 * nw:4 * nw]
        token = refs[6 * nw]
        x, y, c, _ = _place()
        _handshake([(x, y, 1 - c)])
        for w in range(nw):
            hr = srcs[w].shape[1] // 2
            rows = src[w] if whole else src[w].at[:, pl.ds((1 - c) * hr, hr)]
            pltpu.make_async_remote_copy(src_ref=rows, dst_ref=land[w], send_sem=send[w], recv_sem=recv[w],
                                         device_id=(x, y, 1 - c), device_id_type=MESH).start()
        token[...] = jnp.zeros_like(token)

    res = pl.pallas_call(
        body, name=name,
        in_specs=[HBM] * (2 * nw),
        out_specs=[SEM] * (2 * nw) + [HBM] * (2 * nw) + [pl.BlockSpec(memory_space=pltpu.VMEM)],
        out_shape=[pltpu.SemaphoreType.DMA(())] * (2 * nw) + [pltpu.HBM(a.shape, a.dtype) for a in srcs + lands]
        + [jax.ShapeDtypeStruct((8, 128), F32)],
        input_output_aliases={k: 2 * nw + k for k in range(2 * nw)},
        compiler_params=pltpu.CompilerParams(has_side_effects=EFFECT, collective_id=collective_id),
    )(*[_in_hbm(a) for a in srcs + lands])
    return res[:nw], res[nw:2 * nw], res[2 * nw:3 * nw], res[3 * nw:4 * nw], res[4 * nw]


def _transfer_wait(sends, recvs, thru, sizes, after, *, name):
    n = len(sends)
    flat = [a for group in thru for a in group]

    def body(*refs):
        bufs = refs[:len(flat)]
        send = refs[len(flat):len(flat) + n]
        recv = refs[len(flat) + n:len(flat) + 2 * n]
        token = refs[2 * len(flat) + 2 * n + 1]
        token[...] = jnp.zeros_like(token)
        x, y, c, _ = _place()
        pos = 0
        for k in range(n):
            slots, rows = sizes[k]
            region = bufs[pos].at[pl.ds(0, slots), pl.ds(0, rows)]
            pos += len(thru[k])
            cp = pltpu.make_async_remote_copy(src_ref=region, dst_ref=region, send_sem=send[k], recv_sem=recv[k],
                                              device_id=(x, y, 1 - c), device_id_type=MESH)
            cp.wait_send()
            cp.wait_recv()

    res = pl.pallas_call(
        body, name=name,
        in_specs=[HBM] * len(flat) + [SEM] * (2 * n) + [pl.BlockSpec(memory_space=pl.ANY)],
        out_specs=[HBM] * len(flat) + [pl.BlockSpec(memory_space=pltpu.VMEM)],
        out_shape=[pltpu.HBM(a.shape, a.dtype) for a in flat] + [jax.ShapeDtypeStruct((8, 128), F32)],
        input_output_aliases={k: k for k in range(len(flat))},
        compiler_params=pltpu.CompilerParams(has_side_effects=EFFECT),
    )(*flat, *sends, *recvs, after)
    out, pos = [], 0
    for group in thru:
        out.append(res[pos:pos + len(group)])
        pos += len(group)
    return out, res[len(flat)]


def _forward_gathered(bufs, after, *, name):
    nw = len(bufs)

    def body(*refs):
        outs = refs[nw + 1:2 * nw + 1]
        d_send, d_recv, i_send, i_recv = refs[2 * nw + 1:]
        x, y, c, chips = _place()
        me, sibling = (x, y, c), (x, y, 1 - c)
        slots = [2 * cx + cy for cx, cy in chips]

        def rows(w, j, start, n):
            return outs[w].at[slots[j], pl.ds(start, n)]

        def d2d(w, j, which, to):
            hr = bufs[w].shape[1] // 2
            r = rows(w, j, which * hr, hr)
            return pltpu.make_async_remote_copy(
                src_ref=r, dst_ref=r, send_sem=d_send.at[N_PEER_CHIPS * w + j],
                recv_sem=d_recv.at[N_PEER_CHIPS * w + j], device_id=to, device_id_type=MESH)

        def ici(w, j, slot_j, to):
            q = bufs[w].shape[1] // 4
            r = rows(w, slot_j, c * 2 * q + j * q, q)
            return pltpu.make_async_remote_copy(
                src_ref=r, dst_ref=r, send_sem=i_send.at[N_NEIGHBOUR_CHIPS * w + j],
                recv_sem=i_recv.at[N_NEIGHBOUR_CHIPS * w + j], device_id=to, device_id_type=MESH)

        started = []
        for w in range(nw):
            started += [ici(w, 0, 0, (*chips[1], c)), ici(w, 1, 1, (*chips[0], c))]
            started += [d2d(w, j, c, sibling) for j in range(N_NEIGHBOUR_CHIPS)]
        for cp in started:
            cp.start()
        diag = N_PEER_CHIPS - 1
        for w in range(nw):
            for j in range(N_NEIGHBOUR_CHIPS):
                ici(w, j, diag, me).wait_recv()
            cp = d2d(w, diag, c, sibling)
            cp.start()
            started.append(cp)
        for w in range(nw):
            for j in range(N_PEER_CHIPS):
                d2d(w, j, 1 - c, me).wait_recv()
        for cp in started:
            cp.wait_send()

    return pl.pallas_call(
        body, name=name,
        in_specs=[ANY] * (nw + 1), out_specs=[ANY] * nw,
        out_shape=[jax.ShapeDtypeStruct(a.shape, a.dtype) for a in bufs],
        input_output_aliases={w: w for w in range(nw)},
        scratch_shapes=[pltpu.SemaphoreType.DMA((N_PEER_CHIPS * nw,)), pltpu.SemaphoreType.DMA((N_PEER_CHIPS * nw,)),
                        pltpu.SemaphoreType.DMA((N_NEIGHBOUR_CHIPS * nw,)),
                        pltpu.SemaphoreType.DMA((N_NEIGHBOUR_CHIPS * nw,))],
    )(*bufs, after)


def _forward_halves(bufs, which, after, *, name):
    nw = len(bufs)
    n = len(which)

    def body(*refs):
        outs = refs[nw + 1:2 * nw + 1]
        send, recv = refs[2 * nw + 1:]
        x, y, c, chips = _place()
        me, sibling = (x, y, c), (x, y, 1 - c)

        def d2d(w, t, half, to):
            cx, cy = chips[which[t]]
            hr = bufs[w].shape[1] // 2
            rows = outs[w].at[2 * cx + cy, pl.ds(half * hr, hr)]
            return pltpu.make_async_remote_copy(src_ref=rows, dst_ref=rows, send_sem=send.at[n * w + t],
                                                recv_sem=recv.at[n * w + t], device_id=to, device_id_type=MESH)

        passed = [d2d(w, t, c, sibling) for w in range(nw) for t in range(n)]
        for cp in passed:
            cp.start()
        for w in range(nw):
            for t in range(n):
                d2d(w, t, 1 - c, me).wait_recv()
        for cp in passed:
            cp.wait_send()

    return pl.pallas_call(
        body, name=name,
        in_specs=[ANY] * (nw + 1), out_specs=[ANY] * nw,
        out_shape=[jax.ShapeDtypeStruct(a.shape, a.dtype) for a in bufs],
        input_output_aliases={w: w for w in range(nw)},
        scratch_shapes=[pltpu.SemaphoreType.DMA((n * nw,)), pltpu.SemaphoreType.DMA((n * nw,))],
    )(*bufs, after)


def _allreduce_small(p, after, *, name):
    R = p.shape[0]
    hr = R // 2

    def body(p_ref, _after_ref, out_ref, sib_ref, sum_ref, gat_ref, tot_ref, send, recv):
        x, y, c, chips = _place()
        s = 2 * x + y
        sibling = (x, y, 1 - c)
        rows = pl.ds(pl.multiple_of(c * hr, 8), hr)
        swap = pltpu.make_async_remote_copy(src_ref=p_ref, dst_ref=sib_ref, send_sem=send.at[0], recv_sem=recv.at[0],
                                            device_id=sibling, device_id_type=MESH)
        swap.start()
        swap.wait()
        sum_ref[...] = p_ref[...] + sib_ref[...]
        gat_ref[s] = sum_ref[rows, :]
        cps = [pltpu.make_async_remote_copy(src_ref=sum_ref.at[rows], dst_ref=gat_ref.at[s], send_sem=send.at[1 + j],
                                            recv_sem=recv.at[1 + j], device_id=(cx, cy, c), device_id_type=MESH)
               for j, (cx, cy) in enumerate(chips)]
        for cp in cps:
            cp.start()
        for cp in cps:
            cp.wait()
        tot_ref[...] = ((gat_ref[0] + gat_ref[1]) + gat_ref[2]) + gat_ref[3]
        out_ref[rows, :] = tot_ref[...]
        share = pltpu.make_async_remote_copy(src_ref=tot_ref, dst_ref=out_ref.at[rows], send_sem=send.at[4],
                                             recv_sem=recv.at[4], device_id=sibling, device_id_type=MESH)
        share.start()
        share.wait_send()
        other = out_ref.at[pl.ds(pl.multiple_of((1 - c) * hr, 8), hr)]
        pltpu.make_async_remote_copy(src_ref=other, dst_ref=other, send_sem=send.at[4], recv_sem=recv.at[4],
                                     device_id=(x, y, c), device_id_type=MESH).wait_recv()

    vmem = pl.BlockSpec(memory_space=pltpu.VMEM)
    return pl.pallas_call(
        body, name=name, in_specs=[vmem, ANY], out_specs=vmem,
        out_shape=jax.ShapeDtypeStruct((R, 128), F32),
        scratch_shapes=[pltpu.VMEM((R, 128), F32), pltpu.VMEM((R, 128), F32), pltpu.VMEM((NCHIP, hr, 128), F32),
                        pltpu.VMEM((hr, 128), F32), pltpu.SemaphoreType.DMA((5,)), pltpu.SemaphoreType.DMA((5,))],
    )(p, after)


def _select_half_bf16(g, half, add, slot, *, name):
    _, R, C = g.shape
    hr = R // 2
    tr = _pick_rows(hr, 16)
    nb = hr // tr
    sel = jnp.concatenate([jnp.reshape(half, (1,)).astype(jnp.int32), slot])

    def body(s_ref, g_ref, a_ref, o_ref, own_ref):
        val = (g_ref[...].astype(F32) + a_ref[...].astype(F32)).astype(BF16)
        o_ref[...] = val

        @pl.when(pl.program_id(1) == s_ref[1])
        def _():
            own_ref[...] = val

    g_spec = pl.BlockSpec((None, tr, C), lambda i, j, s: (j, s[0] * nb + i, 0))
    o_spec = pl.BlockSpec((None, tr, C), lambda i, j, s: (j, i, 0))
    own_spec = pl.BlockSpec((None, tr, C), lambda i, j, s: (s[1], i, 0))
    shape = jax.ShapeDtypeStruct((NCHIP, hr, C), BF16)
    return pl.pallas_call(
        body, name=name,
        grid_spec=pltpu.PrefetchScalarGridSpec(
            num_scalar_prefetch=1, grid=(nb, NCHIP), in_specs=[g_spec, o_spec], out_specs=[o_spec, own_spec]),
        out_shape=[shape, shape],
        compiler_params=_cp(("parallel", "arbitrary"), VMEM_MB),
    )(sel, g, add)


def _adamw_math(w, g, m, v):
    m = ADAM_B1 * m + (1.0 - ADAM_B1) * g
    v = ADAM_B2 * v + (1.0 - ADAM_B2) * (g * g)
    m_hat = m / (1.0 - ADAM_B1 ** ADAM_STEP)
    v_hat = v / (1.0 - ADAM_B2 ** ADAM_STEP)
    delta = -ADAM_LR * (m_hat / (jnp.sqrt(v_hat) + ADAM_EPS) + ADAM_WD * w)
    return delta, m, v


def _adamw(w, g_mine, g_sib, m, v, core, *, name):
    R, C = w.shape
    hr = R // 2
    tr = _pick_rows(hr, 16)
    nb = hr // tr
    row = pl.BlockSpec((tr, C), lambda hh, i, c: (hh * nb + i, 0))
    mine = pl.BlockSpec((NCHIP, tr, C), lambda hh, i, c: (0, jnp.where(hh == c[0], i, 0), 0))
    sibs = pl.BlockSpec((NCHIP, tr, C), lambda hh, i, c: (0, jnp.where(hh == c[0], 0, i), 0))

    def slot_sum(ref):
        acc = ref[0].astype(F32) + ref[1].astype(F32)
        for j in range(2, NCHIP):
            acc = acc + ref[j].astype(F32)
        return acc

    def body(c_ref, w_ref, gm_ref, gs_ref, m_ref, v_ref, go_ref, d_ref, mo_ref, vo_ref):
        gv = jnp.where(pl.program_id(0) == c_ref[0], slot_sum(gm_ref), slot_sum(gs_ref))
        d, mn, vn = _adamw_math(w_ref[...], gv, m_ref[...], v_ref[...])
        go_ref[...] = gv
        d_ref[...] = d
        mo_ref[...] = mn
        vo_ref[...] = vn

    return pl.pallas_call(
        body, name=name,
        grid_spec=pltpu.PrefetchScalarGridSpec(
            num_scalar_prefetch=1, grid=(2, nb),
            in_specs=[row, mine, sibs, row, row], out_specs=[row] * 4),
        out_shape=[jax.ShapeDtypeStruct((R, C), F32)] * 4,
        compiler_params=_cp(("parallel", "parallel"), VMEM_MB),
    )(core, w, g_mine, g_sib, m, v)


def _adamw_small(ws, gs, ms, vs, *, name):
    n = len(ws)

    def body(*refs):
        w_r, g_r, m_r, v_r = refs[:n], refs[n:2 * n], refs[2 * n:3 * n], refs[3 * n:4 * n]
        d_r, mo_r, vo_r = refs[4 * n:5 * n], refs[5 * n:6 * n], refs[6 * n:7 * n]
        for k in range(n):
            d, mn, vn = _adamw_math(w_r[k][...], g_r[k][...], m_r[k][...], v_r[k][...])
            d_r[k][...] = d
            mo_r[k][...] = mn
            vo_r[k][...] = vn

    shapes = [jax.ShapeDtypeStruct(w.shape, F32) for w in ws]
    res = pl.pallas_call(body, name=name, out_shape=shapes * 3)(*ws, *gs, *ms, *vs)
    return res[:n], res[n:2 * n], res[2 * n:]


_PACK_ROWS = 8


def _pack(parts):
    rows = []
    for a in parts:
        flat = a.reshape(-1)
        n = -(-flat.shape[0] // (_PACK_ROWS * 128)) * (_PACK_ROWS * 128)
        rows.append(jnp.pad(flat, (0, n - flat.shape[0])).reshape(-1, 128))
    total = sum(r.shape[0] for r in rows)
    if total % 16:
        rows.append(jnp.zeros((16 - total % 16, 128), F32))
    return jnp.concatenate(rows, axis=0)


def _unpack(p, shapes):
    out, r = [], 0
    for shp in shapes:
        n = math.prod(shp)
        nr = -(-n // (_PACK_ROWS * 128)) * _PACK_ROWS
        out.append(p[r:r + nr].reshape(-1)[:n].reshape(shp))
        r += nr
    return out


def kernel(x, mem, g_mix, w_in, ln_v_g, ln_v_b, w_s, b_s, conv_w, g_mem, w_kv, g_head, w_o, g_ffn, w_ffn1, w_ffn2, g_final, loss_target, m_g_mix, m_w_in, m_ln_v_g, m_ln_v_b, m_w_s, m_b_s, m_conv_w, m_g_mem, m_w_kv, m_g_head, m_w_o, m_g_ffn, m_w_ffn1, m_w_ffn2, m_g_final, v_g_mix, v_w_in, v_ln_v_g, v_ln_v_b, v_w_s, v_b_s, v_conv_w, v_g_mem, v_w_kv, v_g_head, v_w_o, v_g_ffn, v_w_ffn1, v_w_ffn2, v_g_final):
    sds = jax.ShapeDtypeStruct
    xi, yi = lax.axis_index("x"), lax.axis_index("y")
    shard = 2 * xi + yi
    x2d, mem2d, tgt = x[0], mem[0], loss_target[0]
    ws3, bs2 = w_s[0], b_s[0]
    g_final2 = g_final.reshape(1, D)
    dff4 = DFF // NCHIP
    din4 = DIN // NCHIP
    dcv4 = DC // NCHIP

    big = [w_in[0].T, w_kv[0], w_o[0], w_ffn1[0], w_ffn2[0]]
    big_names = ["w_in", "w_kv", "w_o", "w_ffn1", "w_ffn2"]
    slot = jnp.reshape(shard, (1,)).astype(jnp.int32)
    core = jnp.reshape(lax.axis_index("c"), (1,)).astype(jnp.int32)
    conv_pad = jnp.pad(conv_w[0], ((0, CONV_PAD[0] - 3), (0, CONV_PAD[1] - dcv4)))
    conv_slots = lax.dynamic_update_slice(jnp.zeros((NCHIP,) + CONV_PAD, F32), conv_pad[None], (shard, 0, 0))

    gather_ids = {"in": 16, "kvo": 17, "ffn1": 18, "ffn2": 19, "ffn2d": 20}

    def gather_start(bufs, after, nm, forwards=()):
        return _allgather_start(bufs, forwards, after, gather_ids[nm], name="ag_start_" + nm)

    def gather_wait(state, idx, after, nm):
        send, recv, bufs, _ = state
        got, token = _transfer_wait([send[k] for k in idx], [recv[k] for k in idx], [[bufs[k]] for k in idx],
                                    [(N_NEIGHBOUR_CHIPS, bufs[k].shape[1] // 2) for k in idx], after, name="ag_wait_" + nm)
        return [g[0] for g in got], token

    cast = lambda k, after: _cast_into_slot(big[k], slot, after, name="cast_" + big_names[k])
    ag_in = gather_start([cast(0, slot), conv_slots], slot, "in")
    bs_t = bs2.T

    h = _rms_fwd(x2d, g_mix, name="rms_mix", after=[ag_in[3]])
    mem_n = _rms_fwd(mem2d, g_mem, name="rms_mem", after=[h])
    kvo_b = [cast(1, mem_n)]
    kvo_b.append(cast(2, kvo_b[0]))
    w1_b = cast(3, kvo_b[1])
    w2_b = cast(4, w1_b)
    got_in, tok = gather_wait(ag_in, [0, 1], w2_b, "in")
    win4, conv4 = _forward_gathered(got_in, tok, name="ag_fwd_in")
    ag_kvo = gather_start(kvo_b, conv4, "kvo")
    w_in_t = win4.reshape(DIN, D)
    conv_full = conv4[:, :3, :dcv4].transpose(1, 0, 2).reshape(3, DC)
    NEAR, FAR = [0, 1], [2]

    def diagonal_wait(state, ks, after, nm):
        send, recv, bufs, _ = state
        got, token = _transfer_wait([send[k] for k in ks], [recv[k] for k in ks], [[bufs[k]] for k in ks],
                                    [(1, bufs[k].shape[1] // 2) for k in ks], after, name="ag_waitd_" + nm)
        return [g[0] for g in got], token

    proj_w = lambda tn, tk: pl.BlockSpec((tn, tk), lambda j, i, k, s: (s[j], k))
    proj_cols = lambda tm, tn: [pl.BlockSpec((tm, tn), lambda j, i, k, s: (i, s[j]))]
    proj_half = lambda which, into, after: _matmul(
        h, w_in_t, name="mm_proj_%d" % which, tb=True, M=S, N=DIN // 2, K=D, tn=DIN // 2, b_spec=proj_w,
        out_specs=proj_cols, outs=[sds((S, DIN), F32)], slots=jnp.full((1,), which, jnp.int32), into=into,
        after=after)[0]
    proj = proj_half(0, [], [ag_kvo[3]])
    got_kvo, tok = gather_wait(ag_kvo, [0, 1], proj, "kvo")
    ag_w1 = gather_start([w1_b], tok, "ffn1", forwards=got_kvo)
    kvo_n = _forward_halves(ag_w1[2][1:], NEAR, ag_w1[3], name="ag_fwdn_kvo")
    ag_w1 = (ag_w1[0], ag_w1[1], [ag_w1[2][0]] + list(kvo_n), ag_w1[3])
    proj = proj_half(1, [proj], list(kvo_n))
    kvo_d, tok = diagonal_wait(ag_w1, [1, 2], proj, "kvo")
    wkv4, wo4 = _forward_halves(kvo_d, FAR, tok, name="ag_fwdd_kvo")
    w_kv_full = wkv4.reshape(D, 2 * DM)
    w_o_full = wo4.reshape(D, D)
    (kv,) = _matmul(mem_n, w_kv_full, name="mm_kv", M=NMEM, N=2 * DM, K=D, outs=[sds((NMEM, 2 * DM), F32)])
    heads, hn, ycv = _mix_fwd(proj, kv, ws3, bs_t, ln_v_g, ln_v_b, conv_full, g_head, name="mix_fwd")
    def residual_and_norm(acc, res, g):
        x2v = acc + res
        r = lax.rsqrt(jnp.mean(x2v * x2v, axis=-1, keepdims=True) + EPS)
        return x2v, (x2v * r) * g

    row_vec = lambda tm, tn: pl.BlockSpec((1, tn), lambda j, i, k, *s: (0, j))
    x2, h2 = _matmul(hn, w_o_full, name="mm_wo", M=S, N=D, K=D, tn=D, n_split=1, epi=residual_and_norm,
                     outs=[sds((S, D), F32), sds((S, D), BF16)], extras=[(x2d, _tile_spec()), (g_ffn, row_vec)])
    near = jnp.stack([shard, 2 * (1 - xi) + yi, 2 * xi + (1 - yi)]).astype(jnp.int32)
    far = jnp.reshape(2 * (1 - xi) + (1 - yi), (1,)).astype(jnp.int32)

    w1_shard = lambda tn, tk: pl.BlockSpec((None, tk, tn), lambda j, i, k, s: (s[j], k, 0))
    act_cols = lambda tm, tn: [pl.BlockSpec((tm, tn), lambda j, i, k, s: (i, s[j]))] * 2

    def relu2(acc):
        r = jnp.maximum(acc, 0.0)
        return r * r, 2.0 * r

    got_w1, tok = gather_wait(ag_w1, [0], h2, "ffn1")
    ag_w2 = gather_start([w2_b], tok, "ffn2", forwards=got_w1)
    (w1n,) = _forward_halves([ag_w2[2][1]], NEAR, ag_w2[3], name="ag_fwdn_ffn1")
    ag_w2 = (ag_w2[0], ag_w2[1], [ag_w2[2][0], w1n], ag_w2[3])
    act, dact_df = _matmul(h2, w1n, name="mm_ffn1_near", M=S, N=3 * dff4, K=D, tn=dff4, b_spec=w1_shard,
                           out_specs=act_cols, outs=[sds((S, DFF), BF16)] * 2, epi=relu2, slots=near)
    w1d, tok = diagonal_wait(ag_w2, [1], act, "ffn1")
    (w14,) = _forward_halves(w1d, FAR, tok, name="ag_fwdd_ffn1")
    act, dact_df = _matmul(h2, w14, name="mm_ffn1_far", M=S, N=dff4, K=D, tn=dff4, b_spec=w1_shard,
                           out_specs=act_cols, outs=[sds((S, DFF), BF16)] * 2, epi=relu2, slots=far,
                           into=[act, dact_df])

    act_shard = lambda tm, tk: pl.BlockSpec((tm, tk), lambda j, i, k, s: (i, s[k]))
    w2_shard = lambda tn, tk: pl.BlockSpec((None, tk, tn), lambda j, i, k, s: (s[k], 0, j))
    got_w2, tok = gather_wait(ag_w2, [0], act, "ffn2")
    ag_w2d = gather_start([], tok, "ffn2d", forwards=got_w2)
    (w2n,) = _forward_halves(ag_w2d[2], NEAR, ag_w2d[3], name="ag_fwdn_ffn2")
    ag_w2d = (ag_w2d[0], ag_w2d[1], [w2n], ag_w2d[3])
    (x3,) = _matmul(act, w2n, name="mm_ffn2_near", M=S, N=D, K=3 * dff4, tm=2 * TM, tk=dff4,
                    a_spec=act_shard, b_spec=w2_shard, outs=[sds((S, D), F32)], epi=lambda acc, res: (acc + res,),
                    extras=[(x2, _tile_spec())], slots=near)
    w2d, tok = diagonal_wait(ag_w2d, [0], x3, "ffn2")
    (w24,) = _forward_halves(w2d, FAR, tok, name="ag_fwdd_ffn2")
    (x3,) = _matmul(act, w24, name="mm_ffn2_far", M=S, N=D, K=dff4, tm=2 * TM, tk=dff4, a_spec=act_shard,
                    b_spec=w2_shard, outs=[sds((S, D), F32)], epi=lambda acc, res: (acc + res,),
                    extras=[(x3, _tile_spec())], slots=far)
    w2_full = w24.reshape(DFF, D)

    ci = lax.axis_index("c")

    def rs_sibling(g4, nm):
        return _sibling_start([g4], False, 1 + big_names.index(nm), name="rs_sib_" + nm)

    def rs_chips(state, after, nm):
        send, recv, g4, land, _ = state
        (((land_, g4_),), _) = _transfer_wait(send, recv, [[land[0], g4[0]]], [(NCHIP, land[0].shape[1])], after,
                                             name="rs_sibwait_" + nm)
        part, buf = _select_half_bf16(g4_, ci, land_, slot, name="rs_add_" + nm)
        return _scatter_start([part], [buf], 1 + 2 * len(big_names) + big_names.index(nm), name="rs_start_" + nm)

    def dw_half(a, b, nm, *, by_rows, hr, cols, which, land, after):
        tile = lambda tm, tn: pl.BlockSpec(
            (None, tm, tn), (lambda j, i, k, s: (i, 0, j)) if by_rows else (lambda j, i, k, s: (j, 0, 0)))
        a_half = lambda tm, tk: pl.BlockSpec(
            (tk, tm), (lambda j, i, k, s: (k, 2 * i + s[0])) if by_rows else (lambda j, i, k, s: (k, s[0])))
        (out,) = _matmul(a, b, name=nm, ta=True, M=NCHIP * hr if by_rows else hr, N=cols if by_rows else NCHIP * cols,
                         K=S, tm=hr, tn=cols, a_spec=a_half, out_specs=lambda tm, tn: [tile(tm, tn)],
                         outs=[sds((NCHIP, hr, cols), BF16)], slots=jnp.reshape(which, (1,)).astype(jnp.int32),
                         epi=None if land is None else (lambda acc, other: (acc + other.astype(F32),)),
                         extras=[] if land is None else [(land, tile)], after=after)
        return out

    def rs_sibling_half(half, nm):
        return _sibling_start([half], True, 1 + big_names.index(nm), name="rs_sib_" + nm)

    def rs_chips_fused(state, grad_half, after, nm):
        send, recv, mine, land, _ = state
        (((land_, _),), tok) = _transfer_wait(send, recv, [[land[0], mine[0]]], [(NCHIP, land[0].shape[1])], after,
                                             name="rs_sibwait_" + nm)
        part = grad_half(land_, [tok])
        buf = _own_slot(part, slot, name="rs_own_" + nm)
        return _scatter_start([part], [buf], 1 + 2 * len(big_names) + big_names.index(nm), name="rs_start_" + nm)

    def rs_end(state, after, nm):
        send, recv, parts, bufs, _ = state
        (((buf, _),), _) = _transfer_wait(send, recv, [[bufs[0], parts[0]]], [(N_PEER_CHIPS, bufs[0].shape[1])], after,
                                          name="rs_wait_" + nm)
        return _sibling_start([buf], True, 1 + len(big_names) + big_names.index(nm), name="rs_share_" + nm)

    big_m = [m_w_in[0].T, m_w_kv[0], m_w_o[0], m_w_ffn1[0], m_w_ffn2[0]]
    big_v = [v_w_in[0].T, v_w_kv[0], v_w_o[0], v_w_ffn1[0], v_w_ffn2[0]]
    big_out = {}

    def rs_finish(k, state, after):
        send, recv, mine, land, _ = state
        nm = big_names[k]
        (((land_, mine_),), _) = _transfer_wait(send, recv, [[land[0], mine[0]]], [(NCHIP, land[0].shape[1])], after,
                                               name="rs_sharewait_" + nm)
        big_out[nm] = _adamw(big[k], mine_, land_, big_m[k], big_v[k], core, name="adamw_" + nm)
        return big_out[nm][1]

    dx3, dx3b, dg_final, loss11 = _loss_bwd(x3, g_final2, tgt, name="loss_bwd")
    dw2_half = lambda which, land, after, nm: dw_half(
        act, dx3b, nm, by_rows=True, hr=dff4 // 2, cols=D, which=which, land=land, after=after)
    sib_w2 = rs_sibling_half(dw2_half(1 - ci, None, [], "mm_dw2_sib"), "w_ffn2")
    (dfb,) = _matmul(dx3b, w2_full, name="mm_dact", tb=True, M=S, N=DFF, K=D, tm=2 * TM, tn=dff4, outs=[sds((S, DFF), BF16)],
                     epi=lambda acc, g: (acc * g.astype(F32),), extras=[(dact_df, _tile_spec())],
                     after=[sib_w2[4]])
    rs_w2 = rs_chips_fused(sib_w2, lambda land, after: dw2_half(ci, land, after, "mm_dw2_own"), dfb, "w_ffn2")

    dw1_half = lambda which, land, after, nm: dw_half(
        h2, dfb, nm, by_rows=False, hr=D // 2, cols=dff4, which=which, land=land, after=after)
    sib_w1 = rs_sibling_half(dw1_half(1 - ci, None, [rs_w2[4]], "mm_dw1_sib"), "w_ffn1")

    def w1_rows(tn, tk):
        kb = dff4 // tk
        return pl.BlockSpec((None, tn, tk), lambda j, i, k: (k // kb, j, k % kb))

    (dh2,) = _matmul(dfb, w14, name="mm_dh2", tb=True, M=S, N=D, K=DFF, tm=2 * TM, b_spec=w1_rows,
                     outs=[sds((S, D), F32)], after=[sib_w1[4]])
    rs_w1 = rs_chips_fused(sib_w1, lambda land, after: dw1_half(ci, land, after, "mm_dw1_own"), dh2, "w_ffn1")
    dx2, dx2b, dg_ffn = _rms_bwd(dh2, x2, g_ffn, dx3, name="rms_ffn_bwd", after=[rs_w1[4]])
    dwo_half = lambda which, land, after, nm: dw_half(
        hn, dx2b, nm, by_rows=True, hr=D // NCHIP // 2, cols=D, which=which, land=land, after=after)
    sib_wo = rs_sibling_half(dwo_half(1 - ci, None, [], "mm_dwo_sib"), "w_o")
    (dhn,) = _matmul(dx2b, w_o_full, name="mm_dhn", tb=True, M=S, N=D, K=D, tm=2 * TM, outs=[sds((S, D), F32)],
                     after=[sib_wo[4]])
    rs_wo = rs_chips_fused(sib_wo, lambda land, after: dwo_half(ci, land, after, "mm_dwo_own"), dhn, "w_o")
    sh_w2 = rs_end(rs_w2, rs_wo[4], "w_ffn2")
    dproj, dkv, dws, dbs8, dlng, dlnb, dcw8, dgh = _mix_bwd(
        dhn, heads, proj, ycv, kv, ws3, bs_t, ln_v_g, ln_v_b, conv_full, g_head, sh_w2[4], name="mix_bwd")
    (dwin_t,) = _matmul(dproj, h, name="mm_dwin", ta=True, M=DIN, N=D, K=S, tm=DIN // 2, outs=[sds((DIN, D), BF16)])
    sib_win = rs_sibling(dwin_t.reshape(NCHIP, din4, D), "w_in")
    (dwkv,) = _matmul(mem_n, dkv, name="mm_dwkv", ta=True, M=D, N=2 * DM, K=NMEM, outs=[sds((D, 2 * DM), BF16)],
                      after=[sib_win[4]])
    sib_wkv = rs_sibling(dwkv.reshape(NCHIP, D // NCHIP, 2 * DM), "w_kv")
    (dh,) = _matmul(dproj, w_in_t, name="mm_dh", M=S, N=D, K=DIN, tm=2 * TM, tk=DIN, outs=[sds((S, D), F32)],
                    after=[sib_wkv[4]])
    rs_win = rs_chips(sib_win, dh, "w_in")
    rs_wkv = rs_chips(sib_wkv, rs_win[4], "w_kv")
    dx, dg_mix = _rms_bwd(dh, x2d, g_mix, dx2, name="rms_mix_bwd", want_bf=False, after=[rs_wkv[4]])
    sh_w1 = rs_end(rs_w1, dx, "w_ffn1")
    (dmem_n,) = _matmul(dkv, w_kv_full, name="mm_dmem", tb=True, M=NMEM, N=D, K=2 * DM, outs=[sds((NMEM, D), F32)],
                        after=[sh_w1[4]])
    (dg_mem,) = _rms_bwd(dmem_n, mem2d, g_mem, None, name="rms_mem_bwd", want_dx=False)
    sh_wo = rs_end(rs_wo, dg_mem, "w_o")
    done = rs_finish(4, sh_w2, sh_wo[4])
    done = rs_finish(3, sh_w1, done)
    sh_win = rs_end(rs_win, done, "w_in")
    sh_wkv = rs_end(rs_wkv, sh_win[4], "w_kv")
    done = rs_finish(2, sh_wo, sh_wkv[4])
    done = rs_finish(0, sh_win, done)
    done = rs_finish(1, sh_wkv, done)

    small_names = ["g_mix", "ln_v_g", "ln_v_b", "w_s", "b_s", "conv_w", "g_mem", "g_head", "g_ffn", "g_final"]
    small_part = [dg_mix, dlng, dlnb, dws, dbs8[:, 0, :], dcw8[:3], dg_mem, dgh, dg_ffn, dg_final, loss11]
    small_shapes = [(1, D), (1, DS), (1, DS), (NSH, CHUNK, CHUNK), (NSH, CHUNK), (3, DC), (1, D), (1, D), (1, D), (1, D),
                    (1, 1)]
    total = _allreduce_small(_pack(small_part), done, name="allreduce_small")
    small_g = _unpack(total, small_shapes)
    loss = small_g.pop()[0, 0]
    small_g[5] = lax.dynamic_slice(small_g[5], (0, shard * dcv4), (3, dcv4))
    small_w = [g_mix, ln_v_g, ln_v_b, ws3, bs2, conv_w[0], g_mem, g_head, g_ffn, g_final2]
    small_m = [m_g_mix, m_ln_v_g, m_ln_v_b, m_w_s[0], m_b_s[0], m_conv_w[0], m_g_mem, m_g_head, m_g_ffn,
               m_g_final.reshape(1, D)]
    small_v = [v_g_mix, v_ln_v_g, v_ln_v_b, v_w_s[0], v_b_s[0], v_conv_w[0], v_g_mem, v_g_head, v_g_ffn,
               v_g_final.reshape(1, D)]
    s_delta, s_m, s_v = _adamw_small(small_w, small_g, small_m, small_v, name="adamw_small")
    small_out = {nm: (g, d, mn, vn) for nm, g, d, mn, vn in zip(small_names, small_g, s_delta, s_m, s_v)}

    order = ["g_mix", "w_in", "ln_v_g", "ln_v_b", "w_s", "b_s", "conv_w", "g_mem", "w_kv", "g_head", "w_o",
             "g_ffn", "w_ffn1", "w_ffn2", "g_final"]
    like = dict(g_mix=g_mix, w_in=w_in, ln_v_g=ln_v_g, ln_v_b=ln_v_b, w_s=w_s, b_s=b_s, conv_w=conv_w, g_mem=g_mem,
                w_kv=w_kv, g_head=g_head, w_o=w_o, g_ffn=g_ffn, w_ffn1=w_ffn1, w_ffn2=w_ffn2, g_final=g_final)
    res = {**big_out, **small_out}
    res["w_in"] = [a.T for a in res["w_in"]]
    outs = [loss, dx[None]]
    for k in range(4):
        outs += [res[nm][k].reshape(like[nm].shape) for nm in order]
    return tuple(outs)
```

```python
import math

import jax
import jax.numpy as jnp
from jax import lax
from jax.experimental import pallas as pl
from jax.experimental.pallas import tpu as pltpu

F32 = jnp.float32
BF16 = jnp.bfloat16
MESH = pl.DeviceIdType.MESH

D = 2048
S = 2048
HD = 128
NH = D // HD
NMH = 4
NSH = (NH - NMH) // 2
NCH = NH - NMH - NSH
DS = NSH * HD
DC = NCH * HD
DM = NMH * HD
DIN = 2 * DS + 3 * DC + DM
CHUNK = 128
NMEM = 256
DFF = 4 * D
EPS = 1e-6
NCHIP = 4
SCALE = HD ** -0.5

ADAM_LR = 0.001
ADAM_B1 = 0.9
ADAM_B2 = 0.999
ADAM_EPS = 1e-08
ADAM_WD = 0.01
ADAM_STEP = 10

TR_EW = 256
TR_MIX = 256
TM = 512
TN = 1024
TK = 2048
N_SUB = 512
VMEM_MB = 56
HALO = 8


def _pick(n, target, q=128):
    best = None
    for t in range(q, min(n, target) + 1, q):
        if n % t == 0:
            best = t
    return n if best is None else best


def _pick_rows(n, q):
    below = _pick(n, TR_EW, q)
    if 2 * below >= TR_EW:
        return below
    above = [t for t in range(TR_EW, min(n, 4 * TR_EW) + 1, q) if n % t == 0]
    return above[0] if above else below


def _cp(sem=None, vmem_mb=None, **kw):
    d = dict(kw)
    if sem is not None:
        d["dimension_semantics"] = sem
    if vmem_mb is not None:
        d["vmem_limit_bytes"] = vmem_mb << 20
    return pltpu.CompilerParams(**d)


def _gelu(x):
    z = 0.7978845608028654 * (x + 0.044715 * (x * x * x))
    return 0.5 * x * (1.0 + jnp.tanh(z))


def _gelu_with_grad(x):
    x2 = x * x
    t = jnp.tanh(0.7978845608028654 * (x + 0.044715 * (x2 * x)))
    half = 0.5 * (1.0 + t)
    return x * half, half + 0.5 * x * (1.0 - t * t) * (0.7978845608028654 * (1.0 + 3.0 * 0.044715 * x2))


def _matmul(a, b, *, name, ta=False, tb=False, M, N, K, tm=None, tn=None, tk=None, outs, epi=None,
            extras=(), a_spec=None, b_spec=None, out_specs=None, after=(), n_split=None, slots=None, into=(),
            gated={}):
    n_after = len(after)
    tm = _pick(M, TM if tm is None else tm, 8)
    tn = _pick(N, TN if tn is None else tn)
    tk = _pick(K, TK if tk is None else tk)
    if n_split is None:
        n_split = tn // N_SUB if tn % N_SUB == 0 else 1
    nk = K // tk
    grid = (N // tn, M // tm, nk)
    if a_spec is None:
        a_spec = (pl.BlockSpec((tk, tm), lambda j, i, k, *s: (k, i)) if ta
                  else pl.BlockSpec((tm, tk), lambda j, i, k, *s: (i, k)))
    else:
        a_spec = a_spec(tm, tk)
    if b_spec is None:
        b_spec = (pl.BlockSpec((tn, tk), lambda j, i, k, *s: (j, k)) if tb
                  else pl.BlockSpec((tk, tn), lambda j, i, k, *s: (k, j)))
    else:
        b_spec = b_spec(tn, tk)
    if out_specs is None:
        out_specs = [pl.BlockSpec((tm, tn), lambda j, i, k, *s: (i, j)) for _ in outs]
    else:
        out_specs = out_specs(tm, tn)
    dn = (((0 if ta else 1,), (1 if tb else 0,)), ((), ()))
    n_ex, n_out = len(extras), len(outs)
    n_pre = 0 if slots is None else 1
    n_into = len(into)
    ns = tn // n_split

    def body(*refs):
        a_ref, b_ref = refs[n_pre], refs[n_pre + 1]
        ex = refs[n_pre + 2:n_pre + 2 + n_ex]
        first_out = n_pre + 2 + n_ex + n_after + n_into
        o = refs[first_out:first_out + n_out]
        acc = refs[first_out + n_out:]
        k = pl.program_id(2)

        def finish(val, cols):
            res = (val,) if epi is None else epi(val, *[e[:, cols] for e in ex])
            for t, (r, o_ref) in enumerate(zip(res, o)):
                if t in gated:
                    axis, pos = gated[t]

                    @pl.when(pl.program_id(axis) == refs[0][pos])
                    def _(r=r, o_ref=o_ref):
                        o_ref[:, cols] = r.astype(o_ref.dtype)
                else:
                    o_ref[:, cols] = r.astype(o_ref.dtype)

        if nk > 1:
            @pl.when(k == 0)
            def _():
                acc[0][...] = jnp.zeros_like(acc[0])

        av = a_ref[...].astype(BF16)
        for q in range(n_split):
            cols = slice(q * ns, (q + 1) * ns)
            bq = (b_ref[cols, :] if tb else b_ref[:, cols]).astype(BF16)
            part = lax.dot_general(av, bq, dn, preferred_element_type=F32)
            if nk == 1:
                finish(part, cols)
            else:
                acc[0][:, cols] += part

        if nk > 1:
            @pl.when(k == nk - 1)
            def _():
                finish(acc[0][...], slice(0, tn))

    in_specs = ([a_spec, b_spec] + [sp(tm, tn) for _, sp in extras] + [ANY] * (n_after + n_into))
    scratch = [pltpu.VMEM((tm, tn), F32)] if nk > 1 else []
    args = [a, b] + [arr for arr, _ in extras] + list(after) + list(into)
    aliases = {n_pre + len(args) - n_into + t: t for t in range(n_into)}
    params = _cp(("parallel", "parallel", "arbitrary"), VMEM_MB)
    if slots is None:
        return pl.pallas_call(body, name=name, grid=grid, in_specs=in_specs, out_specs=out_specs, out_shape=outs,
                              scratch_shapes=scratch, input_output_aliases=aliases, compiler_params=params)(*args)
    return pl.pallas_call(
        body, name=name,
        grid_spec=pltpu.PrefetchScalarGridSpec(num_scalar_prefetch=1, grid=grid, in_specs=in_specs,
                                               out_specs=out_specs, scratch_shapes=scratch),
        out_shape=outs, input_output_aliases=aliases, compiler_params=params)(slots, *args)


def _tile_spec():
    return lambda tm, tn: pl.BlockSpec((tm, tn), lambda j, i, k, *s: (i, j))


def _cast_into_slot(w, slot, after, *, name):
    R, C = w.shape
    tr = _pick_rows(R, 16)

    def body(s_ref, w_ref, _after_ref, o_ref):
        o_ref[...] = w_ref[...].astype(BF16)

    return pl.pallas_call(
        body, name=name,
        grid_spec=pltpu.PrefetchScalarGridSpec(
            num_scalar_prefetch=1, grid=(R // tr,),
            in_specs=[pl.BlockSpec((tr, C), lambda i, s: (i, 0)), ANY],
            out_specs=pl.BlockSpec((None, tr, C), lambda i, s: (s[0], i, 0))),
        out_shape=jax.ShapeDtypeStruct((NCHIP, R, C), BF16),
        compiler_params=_cp(("parallel",), VMEM_MB),
    )(slot, w, after)


def _rms_fwd(x, g, *, name, after=()):
    R, C = x.shape
    tr = _pick(R, TR_EW, 16)
    n_after = len(after)

    def body(x_ref, g_ref, *rest):
        o_ref = rest[n_after]
        xv = x_ref[...]
        r = lax.rsqrt(jnp.mean(xv * xv, axis=-1, keepdims=True) + EPS)
        o_ref[...] = ((xv * r) * g_ref[...]).astype(BF16)

    return pl.pallas_call(
        body, name=name, grid=(R // tr,),
        in_specs=[pl.BlockSpec((tr, C), lambda i: (i, 0)), pl.BlockSpec((1, C), lambda i: (0, 0))] + [ANY] * n_after,
        out_specs=pl.BlockSpec((tr, C), lambda i: (i, 0)),
        out_shape=jax.ShapeDtypeStruct((R, C), BF16),
        compiler_params=_cp(("parallel",), VMEM_MB),
    )(x, g, *after)


def _rms_bwd(dh, x, g, dres, *, name, want_dx=True, want_bf=True, after=()):
    R, C = x.shape
    tr = _pick(R, TR_EW, 16)
    has_res = dres is not None
    row = pl.BlockSpec((tr, C), lambda i: (i, 0))
    vec = pl.BlockSpec((1, C), lambda i: (0, 0))

    def body(*refs):
        dh_ref, x_ref, g_ref = refs[:3]
        pos = 3
        dres_ref = None
        if has_res:
            dres_ref = refs[pos]
            pos += 1
        outs = refs[pos + len(after):]
        i = pl.program_id(0)
        xv = x_ref[...]
        r = lax.rsqrt(jnp.mean(xv * xv, axis=-1, keepdims=True) + EPS)
        xh = xv * r
        dhv = dh_ref[...]
        dg_ref = outs[-1]
        dgp = jnp.sum(dhv * xh, axis=0, keepdims=True)

        @pl.when(i == 0)
        def _():
            dg_ref[...] = dgp

        @pl.when(i > 0)
        def _():
            dg_ref[...] += dgp

        if want_dx:
            t = dhv * g_ref[...]
            dx = r * (t - xh * jnp.mean(t * xh, axis=-1, keepdims=True))
            if has_res:
                dx = dx + dres_ref[...]
            outs[0][...] = dx
            if want_bf:
                outs[1][...] = dx.astype(BF16)

    in_specs = [row, row, vec] + ([row] if has_res else []) + [ANY] * len(after)
    out_specs, out_shape = [], []
    if want_dx:
        out_specs.append(row)
        out_shape.append(jax.ShapeDtypeStruct((R, C), F32))
        if want_bf:
            out_specs.append(row)
            out_shape.append(jax.ShapeDtypeStruct((R, C), BF16))
    out_specs.append(vec)
    out_shape.append(jax.ShapeDtypeStruct((1, C), F32))
    args = [dh, x, g] + ([dres] if has_res else []) + list(after)
    return pl.pallas_call(
        body, name=name, grid=(R // tr,), in_specs=in_specs, out_specs=out_specs, out_shape=out_shape,
        compiler_params=_cp(("arbitrary",), VMEM_MB),
    )(*args)


def _loss_bwd(x3, g, tgt, *, name):
    R, C = x3.shape
    tr = _pick(R, TR_EW, 16)
    n = R // tr
    row = pl.BlockSpec((tr, C), lambda i: (i, 0))
    vec = pl.BlockSpec((1, C), lambda i: (0, 0))

    def body(x_ref, g_ref, t_ref, dx_ref, dxb_ref, dg_ref, loss_ref, acc_ref):
        i = pl.program_id(0)
        xv = x_ref[...]
        gv = g_ref[...]
        r = lax.rsqrt(jnp.mean(xv * xv, axis=-1, keepdims=True) + EPS)
        xh = xv * r
        e = xh * gv - t_ref[...]
        dy = e * (1.0 / C)
        sq = jnp.sum(e * e, axis=0, keepdims=True)
        dgp = jnp.sum(dy * xh, axis=0, keepdims=True)

        @pl.when(i == 0)
        def _():
            acc_ref[...] = sq
            dg_ref[...] = dgp

        @pl.when(i > 0)
        def _():
            acc_ref[...] += sq
            dg_ref[...] += dgp

        t = dy * gv
        dx = r * (t - xh * jnp.mean(t * xh, axis=-1, keepdims=True))
        dx_ref[...] = dx
        dxb_ref[...] = dx.astype(BF16)

        @pl.when(i == n - 1)
        def _():
            loss_ref[...] = jnp.sum(acc_ref[...], axis=-1, keepdims=True) * (0.5 / C)

    return pl.pallas_call(
        body, name=name, grid=(n,),
        in_specs=[row, vec, row],
        out_specs=[row, row, vec, pl.BlockSpec((1, 1), lambda i: (0, 0))],
        out_shape=[jax.ShapeDtypeStruct((R, C), F32), jax.ShapeDtypeStruct((R, C), BF16),
                   jax.ShapeDtypeStruct((1, C), F32), jax.ShapeDtypeStruct((1, 1), F32)],
        scratch_shapes=[pltpu.VMEM((1, C), F32)],
        compiler_params=_cp(("arbitrary",), VMEM_MB),
    )(x3, g, tgt)


def _offsets():
    u0 = 0
    v0 = DS
    b0 = 2 * DS
    c0 = b0 + DC
    x0 = c0 + DC
    q0 = x0 + DC
    return u0, v0, b0, c0, x0, q0


def _tri_mask(lower):
    r = lax.broadcasted_iota(jnp.int32, (CHUNK, CHUNK), 0)
    c = lax.broadcasted_iota(jnp.int32, (CHUNK, CHUNK), 1)
    return (r >= c) if lower else (c >= r)


def _layer_norm_stats(vg):
    mu = jnp.mean(vg, axis=-1, keepdims=True)
    vc = vg - mu
    rstd = lax.rsqrt(jnp.mean(vc * vc, axis=-1, keepdims=True) + EPS)
    return vc * rstd, rstd


def _softmax_rows(qh, kh):
    s = lax.dot_general(qh, kh, (((1,), (1,)), ((), ())), preferred_element_type=F32)
    m = jnp.max(s, axis=-1, keepdims=True)
    e = jnp.exp(s - m)
    return e / jnp.sum(e, axis=-1, keepdims=True)


def _mix_fwd(proj, kv, w_s, bs_t, ln_g, ln_b, conv_w, g_head, *, name):
    assert DS == DC
    tr = _pick(S, TR_MIX, CHUNK)
    n = S // tr
    nck = tr // CHUNK
    u0, v0, b0, c0, x0, q0 = _offsets()
    hb = tr // HALO

    def body(p_ref, cprev_ref, xprev_ref, kv_ref, ws_ref, bst_ref, lng_ref, lnb_ref, cw_ref, gh_ref,
             heads_ref, hn_ref, ycv_ref, buf_ref):
        i = pl.program_id(0)

        def emit(col, val):
            rs = lax.rsqrt(jnp.mean(val * val, axis=-1, keepdims=True) + EPS)
            heads_ref[:, col:col + HD] = val
            hn_ref[:, col:col + HD] = ((val * rs) * gh_ref[:, col:col + HD]).astype(BF16)

        vhat, _ = _layer_norm_stats(_gelu(p_ref[:, v0:v0 + DS]))
        vnb = (vhat * lng_ref[...] + lnb_ref[...]).astype(BF16)
        low = _tri_mask(True)
        for h in range(NSH):
            wt = jnp.where(low, ws_ref[h], 0.0).astype(BF16)
            bcol = bst_ref[:, h:h + 1]
            parts = []
            for c in range(nck):
                blk = vnb[c * CHUNK:(c + 1) * CHUNK, h * HD:(h + 1) * HD]
                parts.append(jnp.dot(wt, blk, preferred_element_type=F32) + bcol)
            mixed = parts[0] if nck == 1 else jnp.concatenate(parts, axis=0)
            emit(h * HD, _gelu(p_ref[:, u0 + h * HD:u0 + (h + 1) * HD]) * mixed)

        xc = p_ref[:, c0:c0 + DC] * p_ref[:, x0:x0 + DC]
        prev = cprev_ref[...] * xprev_ref[...]
        buf_ref[0:HALO, :] = jnp.where(i > 0, prev, 0.0)
        buf_ref[HALO:HALO + tr, :] = xc
        y = (cw_ref[2:3, :] * xc + cw_ref[1:2, :] * buf_ref[HALO - 1:HALO - 1 + tr, :]
             + cw_ref[0:1, :] * buf_ref[HALO - 2:HALO - 2 + tr, :])
        ycv_ref[...] = y
        cout = p_ref[:, b0:b0 + DC] * y
        for h in range(NCH):
            emit(DS + h * HD, cout[:, h * HD:(h + 1) * HD])

        for h in range(NMH):
            qh = (p_ref[:, q0 + h * HD:q0 + (h + 1) * HD] * SCALE).astype(BF16)
            kh = kv_ref[:, h * HD:(h + 1) * HD].astype(BF16)
            vh = kv_ref[:, DM + h * HD:DM + (h + 1) * HD].astype(BF16)
            p = _softmax_rows(qh, kh)
            emit(DS + DC + h * HD, jnp.dot(p.astype(BF16), vh, preferred_element_type=F32))

    full = lambda shape: pl.BlockSpec(shape, lambda i: (0,) * len(shape))
    halo_c = pl.BlockSpec((HALO, DC), lambda i: (jnp.maximum(i * hb - 1, 0), c0 // DC))
    halo_x = pl.BlockSpec((HALO, DC), lambda i: (jnp.maximum(i * hb - 1, 0), x0 // DC))
    return pl.pallas_call(
        body, name=name, grid=(n,),
        in_specs=[pl.BlockSpec((tr, DIN), lambda i: (i, 0)), halo_c, halo_x,
                  full((NMEM, 2 * DM)), full((NSH, CHUNK, CHUNK)), full((CHUNK, NSH)),
                  full((1, DS)), full((1, DS)), full((3, DC)), full((1, D))],
        out_specs=[pl.BlockSpec((tr, D), lambda i: (i, 0)), pl.BlockSpec((tr, D), lambda i: (i, 0)),
                   pl.BlockSpec((tr, DC), lambda i: (i, 0))],
        out_shape=[jax.ShapeDtypeStruct((S, D), F32), jax.ShapeDtypeStruct((S, D), BF16),
                   jax.ShapeDtypeStruct((S, DC), F32)],
        scratch_shapes=[pltpu.VMEM((tr + HALO, DC), F32)],
        compiler_params=_cp(("parallel",), VMEM_MB),
    )(proj, proj, proj, kv, w_s, bs_t, ln_g, ln_b, conv_w, g_head)


def _mix_bwd(dhn, heads, proj, ycv, kv, w_s, bs_t, ln_g, ln_b, conv_w, g_head, after, *, name):
    assert DS == DC
    tr = _pick(S, TR_MIX, CHUNK)
    n = S // tr
    nck = tr // CHUNK
    u0, v0, b0, c0, x0, q0 = _offsets()
    hb = tr // HALO
    last_hb = S // HALO - 1

    def body(dhn_ref, heads_ref, p_ref, ycv_ref, dhn_nx_ref, heads_nx_ref, b_nx_ref, kv_ref, ws_ref, bst_ref,
             lng_ref, lnb_ref, cw_ref, gh_ref, _after_ref,
             dp_ref, dkv_ref, dws_ref, dbs_ref, dlng_ref, dlnb_ref, dcw_ref, dgh_ref, buf_ref, dvn_ref):
        i = pl.program_id(0)

        @pl.when(i == 0)
        def _():
            dkv_ref[...] = jnp.zeros_like(dkv_ref)
            dws_ref[...] = jnp.zeros_like(dws_ref)
            dbs_ref[...] = jnp.zeros_like(dbs_ref)
            dlng_ref[...] = jnp.zeros_like(dlng_ref)
            dlnb_ref[...] = jnp.zeros_like(dlnb_ref)
            dcw_ref[...] = jnp.zeros_like(dcw_ref)
            dgh_ref[...] = jnp.zeros_like(dgh_ref)

        def head_bwd(a, dn, gh):
            rs = lax.rsqrt(jnp.mean(a * a, axis=-1, keepdims=True) + EPS)
            ah = a * rs
            t = dn * gh
            return rs * (t - ah * jnp.mean(t * ah, axis=-1, keepdims=True)), jnp.sum(dn * ah, axis=0, keepdims=True)

        def head_grad(col):
            da, dg = head_bwd(heads_ref[:, col:col + HD], dhn_ref[:, col:col + HD], gh_ref[:, col:col + HD])
            dgh_ref[:, col:col + HD] += dg
            return da

        vg, dvg_dv = _gelu_with_grad(p_ref[:, v0:v0 + DS])
        vhat, rstd = _layer_norm_stats(vg)
        vnb = (vhat * lng_ref[...] + lnb_ref[...]).astype(BF16)
        low = _tri_mask(True)
        ones = jnp.ones((HALO, HD), BF16)
        for h in range(NSH):
            w_h = ws_ref[h]
            wt = jnp.where(low, w_h, 0.0).astype(BF16)
            bcol = bst_ref[:, h:h + 1]
            da = head_grad(h * HD)
            ug, dug_du = _gelu_with_grad(p_ref[:, u0 + h * HD:u0 + (h + 1) * HD])
            dws = jnp.zeros((CHUNK, CHUNK), F32)
            dbs = jnp.zeros((HALO, CHUNK), F32)
            mixed_parts = []
            for c in range(nck):
                rows = slice(c * CHUNK, (c + 1) * CHUNK)
                blk = vnb[rows, h * HD:(h + 1) * HD]
                mixed_parts.append(jnp.dot(wt, blk, preferred_element_type=F32) + bcol)
                dmb = (da[rows] * ug[rows]).astype(BF16)
                dws = dws + lax.dot_general(dmb, blk, (((1,), (1,)), ((), ())), preferred_element_type=F32)
                dbs = dbs + lax.dot_general(ones, dmb, (((1,), (1,)), ((), ())), preferred_element_type=F32)
                dvn_ref[c * CHUNK:(c + 1) * CHUNK, h * HD:(h + 1) * HD] = lax.dot_general(
                    wt, dmb, (((0,), (0,)), ((), ())), preferred_element_type=F32)
            mixed = mixed_parts[0] if nck == 1 else jnp.concatenate(mixed_parts, axis=0)
            dp_ref[:, u0 + h * HD:u0 + (h + 1) * HD] = ((da * mixed) * dug_du).astype(BF16)
            dws_ref[h] += jnp.where(low, dws, 0.0)
            dbs_ref[h] += dbs
        dvn = dvn_ref[...]
        dlng_ref[...] += jnp.sum(dvn * vhat, axis=0, keepdims=True)
        dlnb_ref[...] += jnp.sum(dvn, axis=0, keepdims=True)
        dvh = dvn * lng_ref[...]
        dvg = rstd * (dvh - jnp.mean(dvh, axis=-1, keepdims=True)
                      - vhat * jnp.mean(dvh * vhat, axis=-1, keepdims=True))
        dp_ref[:, v0:v0 + DS] = (dvg * dvg_dv).astype(BF16)

        dc = jnp.concatenate([head_grad(DS + h * HD) for h in range(NCH)], axis=1)
        dc_nx = jnp.concatenate(
            [head_bwd(heads_nx_ref[:, h * HD:(h + 1) * HD], dhn_nx_ref[:, h * HD:(h + 1) * HD],
                      gh_ref[:, DS + h * HD:DS + (h + 1) * HD])[0] for h in range(NCH)], axis=1)
        bg = p_ref[:, b0:b0 + DC]
        cg = p_ref[:, c0:c0 + DC]
        xin = p_ref[:, x0:x0 + DC]
        dp_ref[:, b0:b0 + DC] = (dc * ycv_ref[...]).astype(BF16)
        dyv = dc * bg
        buf_ref[0:tr, :] = dyv
        buf_ref[tr:tr + HALO, :] = jnp.where(i < n - 1, dc_nx * b_nx_ref[...], 0.0)
        sh1 = buf_ref[1:1 + tr, :]
        sh0 = buf_ref[2:2 + tr, :]
        dxc = cw_ref[2:3, :] * dyv + cw_ref[1:2, :] * sh1 + cw_ref[0:1, :] * sh0
        xc = cg * xin
        dp_ref[:, c0:c0 + DC] = (dxc * xin).astype(BF16)
        dp_ref[:, x0:x0 + DC] = (dxc * cg).astype(BF16)
        dcw_ref[0:1, :] += jnp.sum(sh0 * xc, axis=0, keepdims=True)
        dcw_ref[1:2, :] += jnp.sum(sh1 * xc, axis=0, keepdims=True)
        dcw_ref[2:3, :] += jnp.sum(dyv * xc, axis=0, keepdims=True)

        for h in range(NMH):
            do = head_grad(DS + DC + h * HD).astype(BF16)
            qh = (p_ref[:, q0 + h * HD:q0 + (h + 1) * HD] * SCALE).astype(BF16)
            kh = kv_ref[:, h * HD:(h + 1) * HD].astype(BF16)
            vh = kv_ref[:, DM + h * HD:DM + (h + 1) * HD].astype(BF16)
            p = _softmax_rows(qh, kh)
            dpr = lax.dot_general(do, vh, (((1,), (1,)), ((), ())), preferred_element_type=F32)
            ds = (p * (dpr - jnp.sum(dpr * p, axis=-1, keepdims=True))).astype(BF16)
            dp_ref[:, q0 + h * HD:q0 + (h + 1) * HD] = (
                jnp.dot(ds, kh, preferred_element_type=F32) * SCALE).astype(BF16)
            dkv_ref[:, h * HD:(h + 1) * HD] += lax.dot_general(
                ds, qh, (((0,), (0,)), ((), ())), preferred_element_type=F32)
            dkv_ref[:, DM + h * HD:DM + (h + 1) * HD] += lax.dot_general(
                p.astype(BF16), do, (((0,), (0,)), ((), ())), preferred_element_type=F32)

    full = lambda shape: pl.BlockSpec(shape, lambda i: (0,) * len(shape))
    row = lambda c: pl.BlockSpec((tr, c), lambda i: (i, 0))
    nxt = lambda col: pl.BlockSpec((HALO, DC), lambda i: (jnp.minimum((i + 1) * hb, last_hb), col))
    return pl.pallas_call(
        body, name=name, grid=(n,),
        in_specs=[row(D), row(D), row(DIN), row(DC), nxt(DS // DC), nxt(DS // DC), nxt(b0 // DC),
                  full((NMEM, 2 * DM)), full((NSH, CHUNK, CHUNK)), full((CHUNK, NSH)),
                  full((1, DS)), full((1, DS)), full((3, DC)), full((1, D)), ANY],
        out_specs=[row(DIN), full((NMEM, 2 * DM)), full((NSH, CHUNK, CHUNK)), full((NSH, HALO, CHUNK)),
                   full((1, DS)), full((1, DS)), full((HALO, DC)), full((1, D))],
        out_shape=[jax.ShapeDtypeStruct((S, DIN), BF16), jax.ShapeDtypeStruct((NMEM, 2 * DM), F32),
                   jax.ShapeDtypeStruct((NSH, CHUNK, CHUNK), F32), jax.ShapeDtypeStruct((NSH, HALO, CHUNK), F32),
                   jax.ShapeDtypeStruct((1, DS), F32), jax.ShapeDtypeStruct((1, DS), F32),
                   jax.ShapeDtypeStruct((HALO, DC), F32), jax.ShapeDtypeStruct((1, D), F32)],
        scratch_shapes=[pltpu.VMEM((tr + HALO, DC), F32), pltpu.VMEM((tr, DS), F32)],
        compiler_params=_cp(("arbitrary",), VMEM_MB),
    )(dhn, heads, proj, ycv, dhn, heads, proj, kv, w_s, bs_t, ln_g, ln_b, conv_w, g_head, after)


def _place():
    x, y, c = lax.axis_index("x"), lax.axis_index("y"), lax.axis_index("c")
    chips = [(1 - x, y), (x, 1 - y), (1 - x, 1 - y)]
    return x, y, c, chips


ANY = pl.BlockSpec(memory_space=pl.ANY)


HBM = pl.BlockSpec(memory_space=pltpu.HBM)
SEM = pl.BlockSpec(memory_space=pltpu.SEMAPHORE)
EFFECT = pltpu.SideEffectType.DATAFLOW_SIDE_EFFECTING
N_PEER_CHIPS = 3
N_NEIGHBOUR_CHIPS = 2
CONV_PAD = (32, 256)


def _in_hbm(a):
    return pltpu.with_memory_space_constraint(a, pltpu.HBM)


def _allgather_start(bufs, forwards, after, collective_id, *, name):
    arrs = list(bufs) + list(forwards)
    nw, nb = len(arrs), len(bufs)

    def body(*refs):
        ins, send, recv = refs[:nw], refs[nw + 1:2 * nw + 1], refs[2 * nw + 1:3 * nw + 1]
        token = refs[4 * nw + 1]
        x, y, c, chips = _place()
        s = 2 * x + y
        slots = [2 * cx + cy for cx, cy in chips]
        _handshake([(cx, cy, c) for cx, cy in chips[:N_NEIGHBOUR_CHIPS]])
        for w in range(nb, nw):
            q = arrs[w].shape[1] // 4
            for j in range(N_NEIGHBOUR_CHIPS):
                rows = ins[w].at[slots[j], pl.ds(c * 2 * q + j * q, q)]
                pltpu.make_async_remote_copy(src_ref=rows, dst_ref=rows, send_sem=send[w], recv_sem=recv[w],
                                             device_id=(*chips[1 - j], c), device_id_type=MESH).start()
        for w in range(nb):
            hr = arrs[w].shape[1] // 2
            rows = ins[w].at[s, pl.ds(c * hr, hr)]
            for cx, cy in chips[:N_NEIGHBOUR_CHIPS]:
                pltpu.make_async_remote_copy(src_ref=rows, dst_ref=rows, send_sem=send[w], recv_sem=recv[w],
                                             device_id=(cx, cy, c), device_id_type=MESH).start()
        token[...] = jnp.zeros_like(token)

    res = pl.pallas_call(
        body, name=name,
        in_specs=[HBM] * nw + [ANY],
        out_specs=[SEM] * (2 * nw) + [HBM] * nw + [pl.BlockSpec(memory_space=pltpu.VMEM)],
        out_shape=[pltpu.SemaphoreType.DMA(())] * (2 * nw) + [pltpu.HBM(a.shape, a.dtype) for a in arrs]
        + [jax.ShapeDtypeStruct((8, 128), F32)],
        input_output_aliases={w: 2 * nw + w for w in range(nw)},
        compiler_params=pltpu.CompilerParams(has_side_effects=EFFECT, collective_id=collective_id),
    )(*[_in_hbm(a) for a in arrs], after)
    return res[:nw], res[nw:2 * nw], res[2 * nw:3 * nw], res[3 * nw]


def _handshake(peers):
    barrier = pltpu.get_barrier_semaphore()
    for peer in peers:
        pl.semaphore_signal(barrier, inc=1, device_id=peer, device_id_type=MESH)
    pl.semaphore_wait(barrier, len(peers))


def _scatter_start(parts, bufs, collective_id, *, name):
    nw = len(parts)

    def body(*refs):
        src, dst = refs[:nw], refs[nw:2 * nw]
        send, recv = refs[2 * nw:3 * nw], refs[3 * nw:4 * nw]
        token = refs[6 * nw]
        x, y, c, chips = _place()
        s = 2 * x + y
        _handshake([(cx, cy, c) for cx, cy in chips])
        for w in range(nw):
            for cx, cy in chips:
                pltpu.make_async_remote_copy(src_ref=src[w].at[2 * cx + cy], dst_ref=dst[w].at[s], send_sem=send[w],
                                             recv_sem=recv[w], device_id=(cx, cy, c), device_id_type=MESH).start()
        token[...] = jnp.zeros_like(token)

    res = pl.pallas_call(
        body, name=name,
        in_specs=[HBM] * (2 * nw),
        out_specs=[SEM] * (2 * nw) + [HBM] * (2 * nw) + [pl.BlockSpec(memory_space=pltpu.VMEM)],
        out_shape=[pltpu.SemaphoreType.DMA(())] * (2 * nw) + [pltpu.HBM(a.shape, a.dtype) for a in parts + bufs]
        + [jax.ShapeDtypeStruct((8, 128), F32)],
        input_output_aliases={k: 2 * nw + k for k in range(2 * nw)},
        compiler_params=pltpu.CompilerParams(has_side_effects=EFFECT, collective_id=collective_id),
    )(*[_in_hbm(a) for a in parts + bufs])
    return res[:nw], res[nw:2 * nw], res[2 * nw:3 * nw], res[3 * nw:4 * nw], res[4 * nw]


def _sibling_start(srcs, whole, collective_id, *, name):
    nw = len(srcs)
    lands = [lax.empty((a.shape[0], a.shape[1] if whole else a.shape[1] // 2, a.shape[2]), a.dtype) for a in srcs]

    def body(*refs):
        src, land = refs[:nw], refs[nw:2 * nw]
        send, recv = refs[2 * nw:3 * nw], refs[3 * nw:4 * nw]
        token = refs[6 * nw]
        x, y, c, _ = _place()
        _handshake([(x, y, 1 - c)])
        for w in range(nw):
            hr = srcs[w].shape[1] // 2
            rows = src[w] if whole else src[w].at[:, pl.ds((1 - c) * hr, hr)]
            pltpu.make_async_remote_copy(src_ref=rows, dst_ref=land[w], send_sem=send[w], recv_sem=recv[w],
                                         device_id=(x, y, 1 - c), device_id_type=MESH).start()
        token[...] = jnp.zeros_like(token)

    res = pl.pallas_call(
        body, name=name,
        in_specs=[HBM] * (2 * nw),
        out_specs=[SEM] * (2 * nw) + [HBM] * (2 * nw) + [pl.BlockSpec(memory_space=pltpu.VMEM)],
        out_shape=[pltpu.SemaphoreType.DMA(())] * (2 * nw) + [pltpu.HBM(a.shape, a.dtype) for a in srcs + lands]
        + [jax.ShapeDtypeStruct((8, 128), F32)],
        input_output_aliases={k: 2 * nw + k for k in range(2 * nw)},
        compiler_params=pltpu.CompilerParams(has_side_effects=EFFECT, collective_id=collective_id),
    )(*[_in_hbm(a) for a in srcs + lands])
    return res[:nw], res[nw:2 * nw], res[2 * nw:3 * nw], res[3 * nw:4 * nw], res[4 * nw]


def _transfer_wait(sends, recvs, thru, sizes, after, *, name):
    n = len(sends)
    flat = [a for group in thru for a in group]

    def body(*refs):
        bufs = refs[:len(flat)]
        send = refs[len(flat):len(flat) + n]
        recv = refs[len(flat) + n:len(flat) + 2 * n]
        token = refs[2 * len(flat) + 2 * n + 1]
        token[...] = jnp.zeros_like(token)
        x, y, c, _ = _place()
        pos = 0
        for k in range(n):
            slots, rows = sizes[k]
            region = bufs[pos].at[pl.ds(0, slots), pl.ds(0, rows)]
            pos += len(thru[k])
            cp = pltpu.make_async_remote_copy(src_ref=region, dst_ref=region, send_sem=send[k], recv_sem=recv[k],
                                              device_id=(x, y, 1 - c), device_id_type=MESH)
            cp.wait_send()
            cp.wait_recv()

    res = pl.pallas_call(
        body, name=name,
        in_specs=[HBM] * len(flat) + [SEM] * (2 * n) + [pl.BlockSpec(memory_space=pl.ANY)],
        out_specs=[HBM] * len(flat) + [pl.BlockSpec(memory_space=pltpu.VMEM)],
        out_shape=[pltpu.HBM(a.shape, a.dtype) for a in flat] + [jax.ShapeDtypeStruct((8, 128), F32)],
        input_output_aliases={k: k for k in range(len(flat))},
        compiler_params=pltpu.CompilerParams(has_side_effects=EFFECT),
    )(*flat, *sends, *recvs, after)
    out, pos = [], 0
    for group in thru:
        out.append(res[pos:pos + len(group)])
        pos += len(group)
    return out, res[len(flat)]


def _forward_gathered(bufs, after, *, name):
    nw = len(bufs)

    def body(*refs):
        outs = refs[nw + 1:2 * nw + 1]
        d_send, d_recv, i_send, i_recv = refs[2 * nw + 1:]
        x, y, c, chips = _place()
        me, sibling = (x, y, c), (x, y, 1 - c)
        slots = [2 * cx + cy for cx, cy in chips]

        def rows(w, j, start, n):
            return outs[w].at[slots[j], pl.ds(start, n)]

        def d2d(w, j, which, to):
            hr = bufs[w].shape[1] // 2
            r = rows(w, j, which * hr, hr)
            return pltpu.make_async_remote_copy(
                src_ref=r, dst_ref=r, send_sem=d_send.at[N_PEER_CHIPS * w + j],
                recv_sem=d_recv.at[N_PEER_CHIPS * w + j], device_id=to, device_id_type=MESH)

        def ici(w, j, slot_j, to):
            q = bufs[w].shape[1] // 4
            r = rows(w, slot_j, c * 2 * q + j * q, q)
            return pltpu.make_async_remote_copy(
                src_ref=r, dst_ref=r, send_sem=i_send.at[N_NEIGHBOUR_CHIPS * w + j],
                recv_sem=i_recv.at[N_NEIGHBOUR_CHIPS * w + j], device_id=to, device_id_type=MESH)

        started = []
        for w in range(nw):
            started += [ici(w, 0, 0, (*chips[1], c)), ici(w, 1, 1, (*chips[0], c))]
            started += [d2d(w, j, c, sibling) for j in range(N_NEIGHBOUR_CHIPS)]
        for cp in started:
            cp.start()
        diag = N_PEER_CHIPS - 1
        for w in range(nw):
            for j in range(N_NEIGHBOUR_CHIPS):
                ici(w, j, diag, me).wait_recv()
            cp = d2d(w, diag, c, sibling)
            cp.start()
            started.append(cp)
        for w in range(nw):
            for j in range(N_PEER_CHIPS):
                d2d(w, j, 1 - c, me).wait_recv()
        for cp in started:
            cp.wait_send()

    return pl.pallas_call(
        body, name=name,
        in_specs=[ANY] * (nw + 1), out_specs=[ANY] * nw,
        out_shape=[jax.ShapeDtypeStruct(a.shape, a.dtype) for a in bufs],
        input_output_aliases={w: w for w in range(nw)},
        scratch_shapes=[pltpu.SemaphoreType.DMA((N_PEER_CHIPS * nw,)), pltpu.SemaphoreType.DMA((N_PEER_CHIPS * nw,)),
                        pltpu.SemaphoreType.DMA((N_NEIGHBOUR_CHIPS * nw,)),
                        pltpu.SemaphoreType.DMA((N_NEIGHBOUR_CHIPS * nw,))],
    )(*bufs, after)


def _forward_halves(bufs, which, after, *, name):
    nw = len(bufs)
    n = len(which)

    def body(*refs):
        outs = refs[nw + 1:2 * nw + 1]
        send, recv = refs[2 * nw + 1:]
        x, y, c, chips = _place()
        me, sibling = (x, y, c), (x, y, 1 - c)

        def d2d(w, t, half, to):
            cx, cy = chips[which[t]]
            hr = bufs[w].shape[1] // 2
            rows = outs[w].at[2 * cx + cy, pl.ds(half * hr, hr)]
            return pltpu.make_async_remote_copy(src_ref=rows, dst_ref=rows, send_sem=send.at[n * w + t],
                                                recv_sem=recv.at[n * w + t], device_id=to, device_id_type=MESH)

        passed = [d2d(w, t, c, sibling) for w in range(nw) for t in range(n)]
        for cp in passed:
            cp.start()
        for w in range(nw):
            for t in range(n):
                d2d(w, t, 1 - c, me).wait_recv()
        for cp in passed:
            cp.wait_send()

    return pl.pallas_call(
        body, name=name,
        in_specs=[ANY] * (nw + 1), out_specs=[ANY] * nw,
        out_shape=[jax.ShapeDtypeStruct(a.shape, a.dtype) for a in bufs],
        input_output_aliases={w: w for w in range(nw)},
        scratch_shapes=[pltpu.SemaphoreType.DMA((n * nw,)), pltpu.SemaphoreType.DMA((n * nw,))],
    )(*bufs, after)


def _allreduce_small(p, after, *, name):
    R = p.shape[0]
    hr = R // 2

    def body(p_ref, _after_ref, out_ref, sib_ref, sum_ref, gat_ref, tot_ref, send, recv):
        x, y, c, chips = _place()
        s = 2 * x + y
        sibling = (x, y, 1 - c)
        rows = pl.ds(pl.multiple_of(c * hr, 8), hr)
        swap = pltpu.make_async_remote_copy(src_ref=p_ref, dst_ref=sib_ref, send_sem=send.at[0], recv_sem=recv.at[0],
                                            device_id=sibling, device_id_type=MESH)
        swap.start()
        swap.wait()
        sum_ref[...] = p_ref[...] + sib_ref[...]
        gat_ref[s] = sum_ref[rows, :]
        cps = [pltpu.make_async_remote_copy(src_ref=sum_ref.at[rows], dst_ref=gat_ref.at[s], send_sem=send.at[1 + j],
                                            recv_sem=recv.at[1 + j], device_id=(cx, cy, c), device_id_type=MESH)
               for j, (cx, cy) in enumerate(chips)]
        for cp in cps:
            cp.start()
        for cp in cps:
            cp.wait()
        tot_ref[...] = ((gat_ref[0] + gat_ref[1]) + gat_ref[2]) + gat_ref[3]
        out_ref[rows, :] = tot_ref[...]
        share = pltpu.make_async_remote_copy(src_ref=tot_ref, dst_ref=out_ref.at[rows], send_sem=send.at[4],
                                             recv_sem=recv.at[4], device_id=sibling, device_id_type=MESH)
        share.start()
        share.wait_send()
        other = out_ref.at[pl.ds(pl.multiple_of((1 - c) * hr, 8), hr)]
        pltpu.make_async_remote_copy(src_ref=other, dst_ref=other, send_sem=send.at[4], recv_sem=recv.at[4],
                                     device_id=(x, y, c), device_id_type=MESH).wait_recv()

    vmem = pl.BlockSpec(memory_space=pltpu.VMEM)
    return pl.pallas_call(
        body, name=name, in_specs=[vmem, ANY], out_specs=vmem,
        out_shape=jax.ShapeDtypeStruct((R, 128), F32),
        scratch_shapes=[pltpu.VMEM((R, 128), F32), pltpu.VMEM((R, 128), F32), pltpu.VMEM((NCHIP, hr, 128), F32),
                        pltpu.VMEM((hr, 128), F32), pltpu.SemaphoreType.DMA((5,)), pltpu.SemaphoreType.DMA((5,))],
    )(p, after)


def _select_half_bf16(g, half, add, slot, *, name):
    _, R, C = g.shape
    hr = R // 2
    tr = _pick_rows(hr, 16)
    nb = hr // tr
    sel = jnp.concatenate([jnp.reshape(half, (1,)).astype(jnp.int32), slot])

    def body(s_ref, g_ref, a_ref, o_ref, own_ref):
        val = (g_ref[...].astype(F32) + a_ref[...].astype(F32)).astype(BF16)
        o_ref[...] = val

        @pl.when(pl.program_id(1) == s_ref[1])
        def _():
            own_ref[...] = val

    g_spec = pl.BlockSpec((None, tr, C), lambda i, j, s: (j, s[0] * nb + i, 0))
    o_spec = pl.BlockSpec((None, tr, C), lambda i, j, s: (j, i, 0))
    own_spec = pl.BlockSpec((None, tr, C), lambda i, j, s: (s[1], i, 0))
    shape = jax.ShapeDtypeStruct((NCHIP, hr, C), BF16)
    return pl.pallas_call(
        body, name=name,
        grid_spec=pltpu.PrefetchScalarGridSpec(
            num_scalar_prefetch=1, grid=(nb, NCHIP), in_specs=[g_spec, o_spec], out_specs=[o_spec, own_spec]),
        out_shape=[shape, shape],
        compiler_params=_cp(("parallel", "arbitrary"), VMEM_MB),
    )(sel, g, add)


def _adamw_math(w, g, m, v):
    m = ADAM_B1 * m + (1.0 - ADAM_B1) * g
    v = ADAM_B2 * v + (1.0 - ADAM_B2) * (g * g)
    m_hat = m / (1.0 - ADAM_B1 ** ADAM_STEP)
    v_hat = v / (1.0 - ADAM_B2 ** ADAM_STEP)
    delta = -ADAM_LR * (m_hat / (jnp.sqrt(v_hat) + ADAM_EPS) + ADAM_WD * w)
    return delta, m, v


def _adamw(w, g_mine, g_sib, m, v, core, *, name):
    R, C = w.shape
    hr = R // 2
    tr = _pick_rows(hr, 16)
    nb = hr // tr
    row = pl.BlockSpec((tr, C), lambda hh, i, c: (hh * nb + i, 0))
    mine = pl.BlockSpec((NCHIP, tr, C), lambda hh, i, c: (0, jnp.where(hh == c[0], i, 0), 0))
    sibs = pl.BlockSpec((NCHIP, tr, C), lambda hh, i, c: (0, jnp.where(hh == c[0], 0, i), 0))

    def slot_sum(ref):
        acc = ref[0].astype(F32) + ref[1].astype(F32)
        for j in range(2, NCHIP):
            acc = acc + ref[j].astype(F32)
        return acc

    def body(c_ref, w_ref, gm_ref, gs_ref, m_ref, v_ref, go_ref, d_ref, mo_ref, vo_ref):
        gv = jnp.where(pl.program_id(0) == c_ref[0], slot_sum(gm_ref), slot_sum(gs_ref))
        d, mn, vn = _adamw_math(w_ref[...], gv, m_ref[...], v_ref[...])
        go_ref[...] = gv
        d_ref[...] = d
        mo_ref[...] = mn
        vo_ref[...] = vn

    return pl.pallas_call(
        body, name=name,
        grid_spec=pltpu.PrefetchScalarGridSpec(
            num_scalar_prefetch=1, grid=(2, nb),
            in_specs=[row, mine, sibs, row, row], out_specs=[row] * 4),
        out_shape=[jax.ShapeDtypeStruct((R, C), F32)] * 4,
        compiler_params=_cp(("parallel", "parallel"), VMEM_MB),
    )(core, w, g_mine, g_sib, m, v)


def _adamw_small(ws, gs, ms, vs, *, name):
    n = len(ws)

    def body(*refs):
        w_r, g_r, m_r, v_r = refs[:n], refs[n:2 * n], refs[2 * n:3 * n], refs[3 * n:4 * n]
        d_r, mo_r, vo_r = refs[4 * n:5 * n], refs[5 * n:6 * n], refs[6 * n:7 * n]
        for k in range(n):
            d, mn, vn = _adamw_math(w_r[k][...], g_r[k][...], m_r[k][...], v_r[k][...])
            d_r[k][...] = d
            mo_r[k][...] = mn
            vo_r[k][...] = vn

    shapes = [jax.ShapeDtypeStruct(w.shape, F32) for w in ws]
    res = pl.pallas_call(body, name=name, out_shape=shapes * 3)(*ws, *gs, *ms, *vs)
    return res[:n], res[n:2 * n], res[2 * n:]


_PACK_ROWS = 8


def _pack(parts):
    rows = []
    for a in parts:
        flat = a.reshape(-1)
        n = -(-flat.shape[0] // (_PACK_ROWS * 128)) * (_PACK_ROWS * 128)
        rows.append(jnp.pad(flat, (0, n - flat.shape[0])).reshape(-1, 128))
    total = sum(r.shape[0] for r in rows)
    if total % 16:
        rows.append(jnp.zeros((16 - total % 16, 128), F32))
    return jnp.concatenate(rows, axis=0)


def _unpack(p, shapes):
    out, r = [], 0
    for shp in shapes:
        n = math.prod(shp)
        nr = -(-n // (_PACK_ROWS * 128)) * _PACK_ROWS
        out.append(p[r:r + nr].reshape(-1)[:n].reshape(shp))
        r += nr
    return out


def kernel(x, mem, g_mix, w_in, ln_v_g, ln_v_b, w_s, b_s, conv_w, g_mem, w_kv, g_head, w_o, g_ffn, w_ffn1, w_ffn2, g_final, loss_target, m_g_mix, m_w_in, m_ln_v_g, m_ln_v_b, m_w_s, m_b_s, m_conv_w, m_g_mem, m_w_kv, m_g_head, m_w_o, m_g_ffn, m_w_ffn1, m_w_ffn2, m_g_final, v_g_mix, v_w_in, v_ln_v_g, v_ln_v_b, v_w_s, v_b_s, v_conv_w, v_g_mem, v_w_kv, v_g_head, v_w_o, v_g_ffn, v_w_ffn1, v_w_ffn2, v_g_final):
    sds = jax.ShapeDtypeStruct
    xi, yi = lax.axis_index("x"), lax.axis_index("y")
    shard = 2 * xi + yi
    x2d, mem2d, tgt = x[0], mem[0], loss_target[0]
    ws3, bs2 = w_s[0], b_s[0]
    g_final2 = g_final.reshape(1, D)
    dff4 = DFF // NCHIP
    din4 = DIN // NCHIP
    dcv4 = DC // NCHIP

    big = [w_in[0].T, w_kv[0], w_o[0], w_ffn1[0], w_ffn2[0]]
    big_names = ["w_in", "w_kv", "w_o", "w_ffn1", "w_ffn2"]
    slot = jnp.reshape(shard, (1,)).astype(jnp.int32)
    core = jnp.reshape(lax.axis_index("c"), (1,)).astype(jnp.int32)
    conv_pad = jnp.pad(conv_w[0], ((0, CONV_PAD[0] - 3), (0, CONV_PAD[1] - dcv4)))
    conv_slots = lax.dynamic_update_slice(jnp.zeros((NCHIP,) + CONV_PAD, F32), conv_pad[None], (shard, 0, 0))

    gather_ids = {"in": 16, "kvo": 17, "ffn1": 18, "ffn2": 19, "ffn2d": 20}

    def gather_start(bufs, after, nm, forwards=()):
        return _allgather_start(bufs, forwards, after, gather_ids[nm], name="ag_start_" + nm)

    def gather_wait(state, idx, after, nm):
        send, recv, bufs, _ = state
        got, token = _transfer_wait([send[k] for k in idx], [recv[k] for k in idx], [[bufs[k]] for k in idx],
                                    [(N_NEIGHBOUR_CHIPS, bufs[k].shape[1] // 2) for k in idx], after, name="ag_wait_" + nm)
        return [g[0] for g in got], token

    cast = lambda k, after: _cast_into_slot(big[k], slot, after, name="cast_" + big_names[k])
    ag_in = gather_start([cast(0, slot), conv_slots], slot, "in")
    bs_t = bs2.T

    h = _rms_fwd(x2d, g_mix, name="rms_mix", after=[ag_in[3]])
    mem_n = _rms_fwd(mem2d, g_mem, name="rms_mem", after=[h])
    kvo_b = [cast(1, mem_n)]
    kvo_b.append(cast(2, kvo_b[0]))
    w1_b = cast(3, kvo_b[1])
    w2_b = cast(4, w1_b)
    got_in, tok = gather_wait(ag_in, [0, 1], w2_b, "in")
    win4, conv4 = _forward_gathered(got_in, tok, name="ag_fwd_in")
    ag_kvo = gather_start(kvo_b, conv4, "kvo")
    w_in_t = win4.reshape(DIN, D)
    conv_full = conv4[:, :3, :dcv4].transpose(1, 0, 2).reshape(3, DC)
    NEAR, FAR = [0, 1], [2]

    def diagonal_wait(state, ks, after, nm):
        send, recv, bufs, _ = state
        got, token = _transfer_wait([send[k] for k in ks], [recv[k] for k in ks], [[bufs[k]] for k in ks],
                                    [(1, bufs[k].shape[1] // 2) for k in ks], after, name="ag_waitd_" + nm)
        return [g[0] for g in got], token

    proj_w = lambda tn, tk: pl.BlockSpec((tn, tk), lambda j, i, k, s: (s[j], k))
    proj_cols = lambda tm, tn: [pl.BlockSpec((tm, tn), lambda j, i, k, s: (i, s[j]))]
    proj_half = lambda which, into, after: _matmul(
        h, w_in_t, name="mm_proj_%d" % which, tb=True, M=S, N=DIN // 2, K=D, tn=DIN // 2, b_spec=proj_w,
        out_specs=proj_cols, outs=[sds((S, DIN), F32)], slots=jnp.full((1,), which, jnp.int32), into=into,
        after=after)[0]
    proj = proj_half(0, [], [ag_kvo[3]])
    got_kvo, tok = gather_wait(ag_kvo, [0, 1], proj, "kvo")
    ag_w1 = gather_start([w1_b], tok, "ffn1", forwards=got_kvo)
    kvo_n = _forward_halves(ag_w1[2][1:], NEAR, ag_w1[3], name="ag_fwdn_kvo")
    ag_w1 = (ag_w1[0], ag_w1[1], [ag_w1[2][0]] + list(kvo_n), ag_w1[3])
    proj = proj_half(1, [proj], list(kvo_n))
    kvo_d, tok = diagonal_wait(ag_w1, [1, 2], proj, "kvo")
    wkv4, wo4 = _forward_halves(kvo_d, FAR, tok, name="ag_fwdd_kvo")
    w_kv_full = wkv4.reshape(D, 2 * DM)
    w_o_full = wo4.reshape(D, D)
    (kv,) = _matmul(mem_n, w_kv_full, name="mm_kv", M=NMEM, N=2 * DM, K=D, outs=[sds((NMEM, 2 * DM), F32)])
    heads, hn, ycv = _mix_fwd(proj, kv, ws3, bs_t, ln_v_g, ln_v_b, conv_full, g_head, name="mix_fwd")
    def residual_and_norm(acc, res, g):
        x2v = acc + res
        r = lax.rsqrt(jnp.mean(x2v * x2v, axis=-1, keepdims=True) + EPS)
        return x2v, (x2v * r) * g

    row_vec = lambda tm, tn: pl.BlockSpec((1, tn), lambda j, i, k, *s: (0, j))
    x2, h2 = _matmul(hn, w_o_full, name="mm_wo", M=S, N=D, K=D, tn=D, n_split=1, epi=residual_and_norm,
                     outs=[sds((S, D), F32), sds((S, D), BF16)], extras=[(x2d, _tile_spec()), (g_ffn, row_vec)])
    near = jnp.stack([shard, 2 * (1 - xi) + yi, 2 * xi + (1 - yi)]).astype(jnp.int32)
    far = jnp.reshape(2 * (1 - xi) + (1 - yi), (1,)).astype(jnp.int32)

    w1_shard = lambda tn, tk: pl.BlockSpec((None, tk, tn), lambda j, i, k, s: (s[j], k, 0))
    act_cols = lambda tm, tn: [pl.BlockSpec((tm, tn), lambda j, i, k, s: (i, s[j]))] * 2

    def relu2(acc):
        r = jnp.maximum(acc, 0.0)
        return r * r, 2.0 * r

    got_w1, tok = gather_wait(ag_w1, [0], h2, "ffn1")
    ag_w2 = gather_start([w2_b], tok, "ffn2", forwards=got_w1)
    (w1n,) = _forward_halves([ag_w2[2][1]], NEAR, ag_w2[3], name="ag_fwdn_ffn1")
    ag_w2 = (ag_w2[0], ag_w2[1], [ag_w2[2][0], w1n], ag_w2[3])
    act, dact_df = _matmul(h2, w1n, name="mm_ffn1_near", M=S, N=3 * dff4, K=D, tn=dff4, b_spec=w1_shard,
                           out_specs=act_cols, outs=[sds((S, DFF), BF16)] * 2, epi=relu2, slots=near)
    w1d, tok = diagonal_wait(ag_w2, [1], act, "ffn1")
    (w14,) = _forward_halves(w1d, FAR, tok, name="ag_fwdd_ffn1")
    act, dact_df = _matmul(h2, w14, name="mm_ffn1_far", M=S, N=dff4, K=D, tn=dff4, b_spec=w1_shard,
                           out_specs=act_cols, outs=[sds((S, DFF), BF16)] * 2, epi=relu2, slots=far,
                           into=[act, dact_df])

    act_shard = lambda tm, tk: pl.BlockSpec((tm, tk), lambda j, i, k, s: (i, s[k]))
    w2_shard = lambda tn, tk: pl.BlockSpec((None, tk, tn), lambda j, i, k, s: (s[k], 0, j))
    got_w2, tok = gather_wait(ag_w2, [0], act, "ffn2")
    ag_w2d = gather_start([], tok, "ffn2d", forwards=got_w2)
    (w2n,) = _forward_halves(ag_w2d[2], NEAR, ag_w2d[3], name="ag_fwdn_ffn2")
    ag_w2d = (ag_w2d[0], ag_w2d[1], [w2n], ag_w2d[3])
    (x3,) = _matmul(act, w2n, name="mm_ffn2_near", M=S, N=D, K=3 * dff4, tm=2 * TM, tk=dff4,
                    a_spec=act_shard, b_spec=w2_shard, outs=[sds((S, D), F32)], epi=lambda acc, res: (acc + res,),
                    extras=[(x2, _tile_spec())], slots=near)
    w2d, tok = diagonal_wait(ag_w2d, [0], x3, "ffn2")
    (w24,) = _forward_halves(w2d, FAR, tok, name="ag_fwdd_ffn2")
    (x3,) = _matmul(act, w24, name="mm_ffn2_far", M=S, N=D, K=dff4, tm=2 * TM, tk=dff4, a_spec=act_shard,
                    b_spec=w2_shard, outs=[sds((S, D), F32)], epi=lambda acc, res: (acc + res,),
                    extras=[(x3, _tile_spec())], slots=far)
    w2_full = w24.reshape(DFF, D)

    ci = lax.axis_index("c")

    def rs_sibling(g4, nm):
        return _sibling_start([g4], False, 1 + big_names.index(nm), name="rs_sib_" + nm)

    def rs_chips(state, after, nm):
        send, recv, g4, land, _ = state
        (((land_, g4_),), _) = _transfer_wait(send, recv, [[land[0], g4[0]]], [(NCHIP, land[0].shape[1])], after,
                                             name="rs_sibwait_" + nm)
        part, buf = _select_half_bf16(g4_, ci, land_, slot, name="rs_add_" + nm)
        return _scatter_start([part], [buf], 1 + 2 * len(big_names) + big_names.index(nm), name="rs_start_" + nm)

    def dw_half(a, b, nm, *, by_rows, hr, cols, which, land, after):
        tile = lambda tm, tn: pl.BlockSpec(
            (None, tm, tn), (lambda j, i, k, s: (i, 0, j)) if by_rows else (lambda j, i, k, s: (j, 0, 0)))
        own = lambda tm, tn: pl.BlockSpec(
            (None, tm, tn), (lambda j, i, k, s: (s[1], 0, j)) if by_rows else (lambda j, i, k, s: (s[1], 0, 0)))
        a_half = lambda tm, tk: pl.BlockSpec(
            (tk, tm), (lambda j, i, k, s: (k, 2 * i + s[0])) if by_rows else (lambda j, i, k, s: (k, s[0])))
        sel = jnp.concatenate([jnp.reshape(which, (1,)).astype(jnp.int32), slot])
        shape = sds((NCHIP, hr, cols), BF16)
        common = dict(name=nm, ta=True, M=NCHIP * hr if by_rows else hr, N=cols if by_rows else NCHIP * cols, K=S,
                      tm=hr, tn=cols, a_spec=a_half, slots=sel, after=after)
        if land is None:
            return _matmul(a, b, out_specs=lambda tm, tn: [tile(tm, tn)], outs=[shape], **common)[0]

        def add_sibling(acc, other):
            part = acc + other.astype(F32)
            return part, part

        return _matmul(a, b, out_specs=lambda tm, tn: [tile(tm, tn), own(tm, tn)], outs=[shape, shape], epi=add_sibling,
                       extras=[(land, tile)], gated={1: (1 if by_rows else 0, 1)}, **common)

    def rs_sibling_half(half, nm):
        return _sibling_start([half], True, 1 + big_names.index(nm), name="rs_sib_" + nm)

    def rs_chips_fused(state, grad_half, after, nm):
        send, recv, mine, land, _ = state
        (((land_, _),), tok) = _transfer_wait(send, recv, [[land[0], mine[0]]], [(NCHIP, land[0].shape[1])], after,
                                             name="rs_sibwait_" + nm)
        part, buf = grad_half(land_, [tok])
        return _scatter_start([part], [buf], 1 + 2 * len(big_names) + big_names.index(nm), name="rs_start_" + nm)

    def rs_end(state, after, nm):
        send, recv, parts, bufs, _ = state
        (((buf, _),), _) = _transfer_wait(send, recv, [[bufs[0], parts[0]]], [(N_PEER_CHIPS, bufs[0].shape[1])], after,
                                          name="rs_wait_" + nm)
        return _sibling_start([buf], True, 1 + len(big_names) + big_names.index(nm), name="rs_share_" + nm)

    big_m = [m_w_in[0].T, m_w_kv[0], m_w_o[0], m_w_ffn1[0], m_w_ffn2[0]]
    big_v = [v_w_in[0].T, v_w_kv[0], v_w_o[0], v_w_ffn1[0], v_w_ffn2[0]]
    big_out = {}

    def rs_finish(k, state, after):
        send, recv, mine, land, _ = state
        nm = big_names[k]
        (((land_, mine_),), _) = _transfer_wait(send, recv, [[land[0], mine[0]]], [(NCHIP, land[0].shape[1])], after,
                                               name="rs_sharewait_" + nm)
        big_out[nm] = _adamw(big[k], mine_, land_, big_m[k], big_v[k], core, name="adamw_" + nm)
        return big_out[nm][1]

    dx3, dx3b, dg_final, loss11 = _loss_bwd(x3, g_final2, tgt, name="loss_bwd")
    dw2_half = lambda which, land, after, nm: dw_half(
        act, dx3b, nm, by_rows=True, hr=dff4 // 2, cols=D, which=which, land=land, after=after)
    sib_w2 = rs_sibling_half(dw2_half(1 - ci, None, [], "mm_dw2_sib"), "w_ffn2")
    (dfb,) = _matmul(dx3b, w2_full, name="mm_dact", tb=True, M=S, N=DFF, K=D, tm=2 * TM, tn=dff4, outs=[sds((S, DFF), BF16)],
                     epi=lambda acc, g: (acc * g.astype(F32),), extras=[(dact_df, _tile_spec())],
                     after=[sib_w2[4]])
    rs_w2 = rs_chips_fused(sib_w2, lambda land, after: dw2_half(ci, land, after, "mm_dw2_own"), dfb, "w_ffn2")

    dw1_half = lambda which, land, after, nm: dw_half(
        h2, dfb, nm, by_rows=False, hr=D // 2, cols=dff4, which=which, land=land, after=after)
    sib_w1 = rs_sibling_half(dw1_half(1 - ci, None, [rs_w2[4]], "mm_dw1_sib"), "w_ffn1")

    def w1_rows(tn, tk):
        kb = dff4 // tk
        return pl.BlockSpec((None, tn, tk), lambda j, i, k: (k // kb, j, k % kb))

    (dh2,) = _matmul(dfb, w14, name="mm_dh2", tb=True, M=S, N=D, K=DFF, tm=2 * TM, b_spec=w1_rows,
                     outs=[sds((S, D), F32)], after=[sib_w1[4]])
    rs_w1 = rs_chips_fused(sib_w1, lambda land, after: dw1_half(ci, land, after, "mm_dw1_own"), dh2, "w_ffn1")
    dx2, dx2b, dg_ffn = _rms_bwd(dh2, x2, g_ffn, dx3, name="rms_ffn_bwd", after=[rs_w1[4]])
    dwo_half = lambda which, land, after, nm: dw_half(
        hn, dx2b, nm, by_rows=True, hr=D // NCHIP // 2, cols=D, which=which, land=land, after=after)
    sib_wo = rs_sibling_half(dwo_half(1 - ci, None, [], "mm_dwo_sib"), "w_o")
    (dhn,) = _matmul(dx2b, w_o_full, name="mm_dhn", tb=True, M=S, N=D, K=D, tm=2 * TM, outs=[sds((S, D), F32)],
                     after=[sib_wo[4]])
    rs_wo = rs_chips_fused(sib_wo, lambda land, after: dwo_half(ci, land, after, "mm_dwo_own"), dhn, "w_o")
    sh_w2 = rs_end(rs_w2, rs_wo[4], "w_ffn2")
    dproj, dkv, dws, dbs8, dlng, dlnb, dcw8, dgh = _mix_bwd(
        dhn, heads, proj, ycv, kv, ws3, bs_t, ln_v_g, ln_v_b, conv_full, g_head, sh_w2[4], name="mix_bwd")
    (dwin_t,) = _matmul(dproj, h, name="mm_dwin", ta=True, M=DIN, N=D, K=S, tm=DIN // 2, outs=[sds((DIN, D), BF16)])
    sib_win = rs_sibling(dwin_t.reshape(NCHIP, din4, D), "w_in")
    (dwkv,) = _matmul(mem_n, dkv, name="mm_dwkv", ta=True, M=D, N=2 * DM, K=NMEM, outs=[sds((D, 2 * DM), BF16)],
                      after=[sib_win[4]])
    sib_wkv = rs_sibling(dwkv.reshape(NCHIP, D // NCHIP, 2 * DM), "w_kv")
    (dh,) = _matmul(dproj, w_in_t, name="mm_dh", M=S, N=D, K=DIN, tm=2 * TM, tk=DIN, outs=[sds((S, D), F32)],
                    after=[sib_wkv[4]])
    rs_win = rs_chips(sib_win, dh, "w_in")
    rs_wkv = rs_chips(sib_wkv, rs_win[4], "w_kv")
    dx, dg_mix = _rms_bwd(dh, x2d, g_mix, dx2, name="rms_mix_bwd", want_bf=False, after=[rs_wkv[4]])
    sh_w1 = rs_end(rs_w1, dx, "w_ffn1")
    (dmem_n,) = _matmul(dkv, w_kv_full, name="mm_dmem", tb=True, M=NMEM, N=D, K=2 * DM, outs=[sds((NMEM, D), F32)],
                        after=[sh_w1[4]])
    (dg_mem,) = _rms_bwd(dmem_n, mem2d, g_mem, None, name="rms_mem_bwd", want_dx=False)
    sh_wo = rs_end(rs_wo, dg_mem, "w_o")
    done = rs_finish(4, sh_w2, sh_wo[4])
    done = rs_finish(3, sh_w1, done)
    sh_win = rs_end(rs_win, done, "w_in")
    sh_wkv = rs_end(rs_wkv, sh_win[4], "w_kv")
    done = rs_finish(2, sh_wo, sh_wkv[4])
    done = rs_finish(0, sh_win, done)
    done = rs_finish(1, sh_wkv, done)

    small_names = ["g_mix", "ln_v_g", "ln_v_b", "w_s", "b_s", "conv_w", "g_mem", "g_head", "g_ffn", "g_final"]
    small_part = [dg_mix, dlng, dlnb, dws, dbs8[:, 0, :], dcw8[:3], dg_mem, dgh, dg_ffn, dg_final, loss11]
    small_shapes = [(1, D), (1, DS), (1, DS), (NSH, CHUNK, CHUNK), (NSH, CHUNK), (3, DC), (1, D), (1, D), (1, D), (1, D),
                    (1, 1)]
    total = _allreduce_small(_pack(small_part), done, name="allreduce_small")
    small_g = _unpack(total, small_shapes)
    loss = small_g.pop()[0, 0]
    small_g[5] = lax.dynamic_slice(small_g[5], (0, shard * dcv4), (3, dcv4))
    small_w = [g_mix, ln_v_g, ln_v_b, ws3, bs2, conv_w[0], g_mem, g_head, g_ffn, g_final2]
    small_m = [m_g_mix, m_ln_v_g, m_ln_v_b, m_w_s[0], m_b_s[0], m_conv_w[0], m_g_mem, m_g_head, m_g_ffn,
               m_g_final.reshape(1, D)]
    small_v = [v_g_mix, v_ln_v_g, v_ln_v_b, v_w_s[0], v_b_s[0], v_conv_w[0], v_g_mem, v_g_head, v_g_ffn,
               v_g_final.reshape(1, D)]
    s_delta, s_m, s_v = _adamw_small(small_w, small_g, small_m, small_v, name="adamw_small")
    small_out = {nm: (g, d, mn, vn) for nm, g, d, mn, vn in zip(small_names, small_g, s_delta, s_m, s_v)}

    order = ["g_mix", "w_in", "ln_v_g", "ln_v_b", "w_s", "b_s", "conv_w", "g_mem", "w_kv", "g_head", "w_o",
             "g_ffn", "w_ffn1", "w_ffn2", "g_final"]
    like = dict(g_mix=g_mix, w_in=w_in, ln_v_g=ln_v_g, ln_v_b=ln_v_b, w_s=w_s, b_s=b_s, conv_w=conv_w, g_mem=g_mem,
                w_kv=w_kv, g_head=g_head, w_o=w_o, g_ffn=g_ffn, w_ffn1=w_ffn1, w_ffn2=w_ffn2, g_final=g_final)
    res = {**big_out, **small_out}
    res["w_in"] = [a.T for a in res["w_in"]]
    outs = [loss, dx[None]]
    for k in range(4):
        outs += [res[nm][k].reshape(like[nm].shape) for nm in order]
    return tuple(outs)
```

```python
import math

import jax
import jax.numpy as jnp
from jax import lax
from jax.experimental import pallas as pl
from jax.experimental.pallas import tpu as pltpu

F32 = jnp.float32
BF16 = jnp.bfloat16
MESH = pl.DeviceIdType.MESH

D = 2048
S = 2048
HD = 128
NH = D // HD
NMH = 4
NSH = (NH - NMH) // 2
NCH = NH - NMH - NSH
DS = NSH * HD
DC = NCH * HD
DM = NMH * HD
DIN = 2 * DS + 3 * DC + DM
CHUNK = 128
NMEM = 256
DFF = 4 * D
EPS = 1e-6
NCHIP = 4
SCALE = HD ** -0.5

ADAM_LR = 0.001
ADAM_B1 = 0.9
ADAM_B2 = 0.999
ADAM_EPS = 1e-08
ADAM_WD = 0.01
ADAM_STEP = 10

TR_EW = 256
TR_MIX = 256
TM = 512
TN = 1024
TK = 2048
N_SUB = 512
VMEM_MB = 56
HALO = 8


def _pick(n, target, q=128):
    best = None
    for t in range(q, min(n, target) + 1, q):
        if n % t == 0:
            best = t
    return n if best is None else best


def _pick_rows(n, q):
    below = _pick(n, TR_EW, q)
    if 2 * below >= TR_EW:
        return below
    above = [t for t in range(TR_EW, min(n, 4 * TR_EW) + 1, q) if n % t == 0]
    return above[0] if above else below


def _cp(sem=None, vmem_mb=None, **kw):
    d = dict(kw)
    if sem is not None:
        d["dimension_semantics"] = sem
    if vmem_mb is not None:
        d["vmem_limit_bytes"] = vmem_mb << 20
    return pltpu.CompilerParams(**d)


def _gelu(x):
    z = 0.7978845608028654 * (x + 0.044715 * (x * x * x))
    return 0.5 * x * (1.0 + jnp.tanh(z))


def _gelu_with_grad(x):
    x2 = x * x
    t = jnp.tanh(0.7978845608028654 * (x + 0.044715 * (x2 * x)))
    half = 0.5 * (1.0 + t)
    return x * half, half + 0.5 * x * (1.0 - t * t) * (0.7978845608028654 * (1.0 + 3.0 * 0.044715 * x2))


def _matmul(a, b, *, name, ta=False, tb=False, M, N, K, tm=None, tn=None, tk=None, outs, epi=None,
            extras=(), a_spec=None, b_spec=None, out_specs=None, after=(), n_split=None, slots=None, into=()):
    n_after = len(after)
    tm = _pick(M, TM if tm is None else tm, 8)
    tn = _pick(N, TN if tn is None else tn)
    tk = _pick(K, TK if tk is None else tk)
    if n_split is None:
        n_split = tn // N_SUB if tn % N_SUB == 0 else 1
    nk = K // tk
    grid = (N // tn, M // tm, nk)
    if a_spec is None:
        a_spec = (pl.BlockSpec((tk, tm), lambda j, i, k, *s: (k, i)) if ta
                  else pl.BlockSpec((tm, tk), lambda j, i, k, *s: (i, k)))
    else:
        a_spec = a_spec(tm, tk)
    if b_spec is None:
        b_spec = (pl.BlockSpec((tn, tk), lambda j, i, k, *s: (j, k)) if tb
                  else pl.BlockSpec((tk, tn), lambda j, i, k, *s: (k, j)))
    else:
        b_spec = b_spec(tn, tk)
    if out_specs is None:
        out_specs = [pl.BlockSpec((tm, tn), lambda j, i, k, *s: (i, j)) for _ in outs]
    else:
        out_specs = out_specs(tm, tn)
    dn = (((0 if ta else 1,), (1 if tb else 0,)), ((), ()))
    n_ex, n_out = len(extras), len(outs)
    n_pre = 0 if slots is None else 1
    n_into = len(into)
    ns = tn // n_split

    def body(*refs):
        a_ref, b_ref = refs[n_pre], refs[n_pre + 1]
        ex = refs[n_pre + 2:n_pre + 2 + n_ex]
        first_out = n_pre + 2 + n_ex + n_after + n_into
        o = refs[first_out:first_out + n_out]
        acc = refs[first_out + n_out:]
        k = pl.program_id(2)

        def finish(val, cols):
            res = (val,) if epi is None else epi(val, *[e[:, cols] for e in ex])
            for r, o_ref in zip(res, o):
                o_ref[:, cols] = r.astype(o_ref.dtype)

        if nk > 1:
            @pl.when(k == 0)
            def _():
                acc[0][...] = jnp.zeros_like(acc[0])

        av = a_ref[...].astype(BF16)
        for q in range(n_split):
            cols = slice(q * ns, (q + 1) * ns)
            bq = (b_ref[cols, :] if tb else b_ref[:, cols]).astype(BF16)
            part = lax.dot_general(av, bq, dn, preferred_element_type=F32)
            if nk == 1:
                finish(part, cols)
            else:
                acc[0][:, cols] += part

        if nk > 1:
            @pl.when(k == nk - 1)
            def _():
                finish(acc[0][...], slice(0, tn))

    in_specs = ([a_spec, b_spec] + [sp(tm, tn) for _, sp in extras] + [ANY] * (n_after + n_into))
    scratch = [pltpu.VMEM((tm, tn), F32)] if nk > 1 else []
    args = [a, b] + [arr for arr, _ in extras] + list(after) + list(into)
    aliases = {n_pre + len(args) - n_into + t: t for t in range(n_into)}
    params = _cp(("parallel", "parallel", "arbitrary"), VMEM_MB)
    if slots is None:
        return pl.pallas_call(body, name=name, grid=grid, in_specs=in_specs, out_specs=out_specs, out_shape=outs,
                              scratch_shapes=scratch, input_output_aliases=aliases, compiler_params=params)(*args)
    return pl.pallas_call(
        body, name=name,
        grid_spec=pltpu.PrefetchScalarGridSpec(num_scalar_prefetch=1, grid=grid, in_specs=in_specs,
                                               out_specs=out_specs, scratch_shapes=scratch),
        out_shape=outs, input_output_aliases=aliases, compiler_params=params)(slots, *args)


def _tile_spec():
    return lambda tm, tn: pl.BlockSpec((tm, tn), lambda j, i, k, *s: (i, j))


def _cast_into_slot(w, slot, after, *, name):
    R, C = w.shape
    tr = _pick_rows(R, 16)

    def body(s_ref, w_ref, _after_ref, o_ref):
        o_ref[...] = w_ref[...].astype(BF16)

    return pl.pallas_call(
        body, name=name,
        grid_spec=pltpu.PrefetchScalarGridSpec(
            num_scalar_prefetch=1, grid=(R // tr,),
            in_specs=[pl.BlockSpec((tr, C), lambda i, s: (i, 0)), ANY],
            out_specs=pl.BlockSpec((None, tr, C), lambda i, s: (s[0], i, 0))),
        out_shape=jax.ShapeDtypeStruct((NCHIP, R, C), BF16),
        compiler_params=_cp(("parallel",), VMEM_MB),
    )(slot, w, after)


def _own_slot(part, slot, *, name):
    _, R, C = part.shape
    tr = _pick_rows(R, 16)

    def body(s_ref, p_ref, o_ref):
        o_ref[...] = p_ref[...]

    spec = pl.BlockSpec((None, tr, C), lambda i, s: (s[0], i, 0))
    return pl.pallas_call(
        body, name=name,
        grid_spec=pltpu.PrefetchScalarGridSpec(num_scalar_prefetch=1, grid=(R // tr,), in_specs=[spec],
                                               out_specs=spec),
        out_shape=jax.ShapeDtypeStruct(part.shape, part.dtype),
        compiler_params=_cp(("parallel",), VMEM_MB),
    )(slot, part)


def _rms_fwd(x, g, *, name, after=()):
    R, C = x.shape
    tr = _pick(R, TR_EW, 16)
    n_after = len(after)

    def body(x_ref, g_ref, *rest):
        o_ref = rest[n_after]
        xv = x_ref[...]
        r = lax.rsqrt(jnp.mean(xv * xv, axis=-1, keepdims=True) + EPS)
        o_ref[...] = ((xv * r) * g_ref[...]).astype(BF16)

    return pl.pallas_call(
        body, name=name, grid=(R // tr,),
        in_specs=[pl.BlockSpec((tr, C), lambda i: (i, 0)), pl.BlockSpec((1, C), lambda i: (0, 0))] + [ANY] * n_after,
        out_specs=pl.BlockSpec((tr, C), lambda i: (i, 0)),
        out_shape=jax.ShapeDtypeStruct((R, C), BF16),
        compiler_params=_cp(("parallel",), VMEM_MB),
    )(x, g, *after)


def _rms_bwd(dh, x, g, dres, *, name, want_dx=True, want_bf=True, after=()):
    R, C = x.shape
    tr = _pick(R, TR_EW, 16)
    has_res = dres is not None
    row = pl.BlockSpec((tr, C), lambda i: (i, 0))
    vec = pl.BlockSpec((1, C), lambda i: (0, 0))

    def body(*refs):
        dh_ref, x_ref, g_ref = refs[:3]
        pos = 3
        dres_ref = None
        if has_res:
            dres_ref = refs[pos]
            pos += 1
        outs = refs[pos + len(after):]
        i = pl.program_id(0)
        xv = x_ref[...]
        r = lax.rsqrt(jnp.mean(xv * xv, axis=-1, keepdims=True) + EPS)
        xh = xv * r
        dhv = dh_ref[...]
        dg_ref = outs[-1]
        dgp = jnp.sum(dhv * xh, axis=0, keepdims=True)

        @pl.when(i == 0)
        def _():
            dg_ref[...] = dgp

        @pl.when(i > 0)
        def _():
            dg_ref[...] += dgp

        if want_dx:
            t = dhv * g_ref[...]
            dx = r * (t - xh * jnp.mean(t * xh, axis=-1, keepdims=True))
            if has_res:
                dx = dx + dres_ref[...]
            outs[0][...] = dx
            if want_bf:
                outs[1][...] = dx.astype(BF16)

    in_specs = [row, row, vec] + ([row] if has_res else []) + [ANY] * len(after)
    out_specs, out_shape = [], []
    if want_dx:
        out_specs.append(row)
        out_shape.append(jax.ShapeDtypeStruct((R, C), F32))
        if want_bf:
            out_specs.append(row)
            out_shape.append(jax.ShapeDtypeStruct((R, C), BF16))
    out_specs.append(vec)
    out_shape.append(jax.ShapeDtypeStruct((1, C), F32))
    args = [dh, x, g] + ([dres] if has_res else []) + list(after)
    return pl.pallas_call(
        body, name=name, grid=(R // tr,), in_specs=in_specs, out_specs=out_specs, out_shape=out_shape,
        compiler_params=_cp(("arbitrary",), VMEM_MB),
    )(*args)


def _loss_bwd(x3, g, tgt, *, name):
    R, C = x3.shape
    tr = _pick(R, TR_EW, 16)
    n = R // tr
    row = pl.BlockSpec((tr, C), lambda i: (i, 0))
    vec = pl.BlockSpec((1, C), lambda i: (0, 0))

    def body(x_ref, g_ref, t_ref, dx_ref, dxb_ref, dg_ref, loss_ref, acc_ref):
        i = pl.program_id(0)
        xv = x_ref[...]
        gv = g_ref[...]
        r = lax.rsqrt(jnp.mean(xv * xv, axis=-1, keepdims=True) + EPS)
        xh = xv * r
        e = xh * gv - t_ref[...]
        dy = e * (1.0 / C)
        sq = jnp.sum(e * e, axis=0, keepdims=True)
        dgp = jnp.sum(dy * xh, axis=0, keepdims=True)

        @pl.when(i == 0)
        def _():
            acc_ref[...] = sq
            dg_ref[...] = dgp

        @pl.when(i > 0)
        def _():
            acc_ref[...] += sq
            dg_ref[...] += dgp

        t = dy * gv
        dx = r * (t - xh * jnp.mean(t * xh, axis=-1, keepdims=True))
        dx_ref[...] = dx
        dxb_ref[...] = dx.astype(BF16)

        @pl.when(i == n - 1)
        def _():
            loss_ref[...] = jnp.sum(acc_ref[...], axis=-1, keepdims=True) * (0.5 / C)

    return pl.pallas_call(
        body, name=name, grid=(n,),
        in_specs=[row, vec, row],
        out_specs=[row, row, vec, pl.BlockSpec((1, 1), lambda i: (0, 0))],
        out_shape=[jax.ShapeDtypeStruct((R, C), F32), jax.ShapeDtypeStruct((R, C), BF16),
                   jax.ShapeDtypeStruct((1, C), F32), jax.ShapeDtypeStruct((1, 1), F32)],
        scratch_shapes=[pltpu.VMEM((1, C), F32)],
        compiler_params=_cp(("arbitrary",), VMEM_MB),
    )(x3, g, tgt)


def _offsets():
    u0 = 0
    v0 = DS
    b0 = 2 * DS
    c0 = b0 + DC
    x0 = c0 + DC
    q0 = x0 + DC
    return u0, v0, b0, c0, x0, q0


def _tri_mask(lower):
    r = lax.broadcasted_iota(jnp.int32, (CHUNK, CHUNK), 0)
    c = lax.broadcasted_iota(jnp.int32, (CHUNK, CHUNK), 1)
    return (r >= c) if lower else (c >= r)


def _layer_norm_stats(vg):
    mu = jnp.mean(vg, axis=-1, keepdims=True)
    vc = vg - mu
    rstd = lax.rsqrt(jnp.mean(vc * vc, axis=-1, keepdims=True) + EPS)
    return vc * rstd, rstd


def _softmax_rows(qh, kh):
    s = lax.dot_general(qh, kh, (((1,), (1,)), ((), ())), preferred_element_type=F32)
    m = jnp.max(s, axis=-1, keepdims=True)
    e = jnp.exp(s - m)
    return e / jnp.sum(e, axis=-1, keepdims=True)


def _mix_fwd(proj, kv, w_s, bs_t, ln_g, ln_b, conv_w, g_head, *, name):
    assert DS == DC
    tr = _pick(S, TR_MIX, CHUNK)
    n = S // tr
    nck = tr // CHUNK
    u0, v0, b0, c0, x0, q0 = _offsets()
    hb = tr // HALO

    def body(p_ref, cprev_ref, xprev_ref, kv_ref, ws_ref, bst_ref, lng_ref, lnb_ref, cw_ref, gh_ref,
             heads_ref, hn_ref, ycv_ref, buf_ref):
        i = pl.program_id(0)

        def emit(col, val):
            rs = lax.rsqrt(jnp.mean(val * val, axis=-1, keepdims=True) + EPS)
            heads_ref[:, col:col + HD] = val
            hn_ref[:, col:col + HD] = ((val * rs) * gh_ref[:, col:col + HD]).astype(BF16)

        vhat, _ = _layer_norm_stats(_gelu(p_ref[:, v0:v0 + DS]))
        vnb = (vhat * lng_ref[...] + lnb_ref[...]).astype(BF16)
        low = _tri_mask(True)
        for h in range(NSH):
            wt = jnp.where(low, ws_ref[h], 0.0).astype(BF16)
            bcol = bst_ref[:, h:h + 1]
            parts = []
            for c in range(nck):
                blk = vnb[c * CHUNK:(c + 1) * CHUNK, h * HD:(h + 1) * HD]
                parts.append(jnp.dot(wt, blk, preferred_element_type=F32) + bcol)
            mixed = parts[0] if nck == 1 else jnp.concatenate(parts, axis=0)
            emit(h * HD, _gelu(p_ref[:, u0 + h * HD:u0 + (h + 1) * HD]) * mixed)

        xc = p_ref[:, c0:c0 + DC] * p_ref[:, x0:x0 + DC]
        prev = cprev_ref[...] * xprev_ref[...]
        buf_ref[0:HALO, :] = jnp.where(i > 0, prev, 0.0)
        buf_ref[HALO:HALO + tr, :] = xc
        y = (cw_ref[2:3, :] * xc + cw_ref[1:2, :] * buf_ref[HALO - 1:HALO - 1 + tr, :]
             + cw_ref[0:1, :] * buf_ref[HALO - 2:HALO - 2 + tr, :])
        ycv_ref[...] = y
        cout = p_ref[:, b0:b0 + DC] * y
        for h in range(NCH):
            emit(DS + h * HD, cout[:, h * HD:(h + 1) * HD])

        for h in range(NMH):
            qh = (p_ref[:, q0 + h * HD:q0 + (h + 1) * HD] * SCALE).astype(BF16)
            kh = kv_ref[:, h * HD:(h + 1) * HD].astype(BF16)
            vh = kv_ref[:, DM + h * HD:DM + (h + 1) * HD].astype(BF16)
            p = _softmax_rows(qh, kh)
            emit(DS + DC + h * HD, jnp.dot(p.astype(BF16), vh, preferred_element_type=F32))

    full = lambda shape: pl.BlockSpec(shape, lambda i: (0,) * len(shape))
    halo_c = pl.BlockSpec((HALO, DC), lambda i: (jnp.maximum(i * hb - 1, 0), c0 // DC))
    halo_x = pl.BlockSpec((HALO, DC), lambda i: (jnp.maximum(i * hb - 1, 0), x0 // DC))
    return pl.pallas_call(
        body, name=name, grid=(n,),
        in_specs=[pl.BlockSpec((tr, DIN), lambda i: (i, 0)), halo_c, halo_x,
                  full((NMEM, 2 * DM)), full((NSH, CHUNK, CHUNK)), full((CHUNK, NSH)),
                  full((1, DS)), full((1, DS)), full((3, DC)), full((1, D))],
        out_specs=[pl.BlockSpec((tr, D), lambda i: (i, 0)), pl.BlockSpec((tr, D), lambda i: (i, 0)),
                   pl.BlockSpec((tr, DC), lambda i: (i, 0))],
        out_shape=[jax.ShapeDtypeStruct((S, D), F32), jax.ShapeDtypeStruct((S, D), BF16),
                   jax.ShapeDtypeStruct((S, DC), F32)],
        scratch_shapes=[pltpu.VMEM((tr + HALO, DC), F32)],
        compiler_params=_cp(("parallel",), VMEM_MB),
    )(proj, proj, proj, kv, w_s, bs_t, ln_g, ln_b, conv_w, g_head)


def _mix_bwd(dhn, heads, proj, ycv, kv, w_s, bs_t, ln_g, ln_b, conv_w, g_head, after, *, name):
    assert DS == DC
    tr = _pick(S, TR_MIX, CHUNK)
    n = S // tr
    nck = tr // CHUNK
    u0, v0, b0, c0, x0, q0 = _offsets()
    hb = tr // HALO
    last_hb = S // HALO - 1

    def body(dhn_ref, heads_ref, p_ref, ycv_ref, dhn_nx_ref, heads_nx_ref, b_nx_ref, kv_ref, ws_ref, bst_ref,
             lng_ref, lnb_ref, cw_ref, gh_ref, _after_ref,
             dp_ref, dkv_ref, dws_ref, dbs_ref, dlng_ref, dlnb_ref, dcw_ref, dgh_ref, buf_ref, dvn_ref):
        i = pl.program_id(0)

        @pl.when(i == 0)
        def _():
            dkv_ref[...] = jnp.zeros_like(dkv_ref)
            dws_ref[...] = jnp.zeros_like(dws_ref)
            dbs_ref[...] = jnp.zeros_like(dbs_ref)
            dlng_ref[...] = jnp.zeros_like(dlng_ref)
            dlnb_ref[...] = jnp.zeros_like(dlnb_ref)
            dcw_ref[...] = jnp.zeros_like(dcw_ref)
            dgh_ref[...] = jnp.zeros_like(dgh_ref)

        def head_bwd(a, dn, gh):
            rs = lax.rsqrt(jnp.mean(a * a, axis=-1, keepdims=True) + EPS)
            ah = a * rs
            t = dn * gh
            return rs * (t - ah * jnp.mean(t * ah, axis=-1, keepdims=True)), jnp.sum(dn * ah, axis=0, keepdims=True)

        def head_grad(col):
            da, dg = head_bwd(heads_ref[:, col:col + HD], dhn_ref[:, col:col + HD], gh_ref[:, col:col + HD])
            dgh_ref[:, col:col + HD] += dg
            return da

        vg, dvg_dv = _gelu_with_grad(p_ref[:, v0:v0 + DS])
        vhat, rstd = _layer_norm_stats(vg)
        vnb = (vhat * lng_ref[...] + lnb_ref[...]).astype(BF16)
        low = _tri_mask(True)
        ones = jnp.ones((HALO, HD), BF16)
        for h in range(NSH):
            w_h = ws_ref[h]
            wt = jnp.where(low, w_h, 0.0).astype(BF16)
            bcol = bst_ref[:, h:h + 1]
            da = head_grad(h * HD)
            ug, dug_du = _gelu_with_grad(p_ref[:, u0 + h * HD:u0 + (h + 1) * HD])
            dws = jnp.zeros((CHUNK, CHUNK), F32)
            dbs = jnp.zeros((HALO, CHUNK), F32)
            mixed_parts = []
            for c in range(nck):
                rows = slice(c * CHUNK, (c + 1) * CHUNK)
                blk = vnb[rows, h * HD:(h + 1) * HD]
                mixed_parts.append(jnp.dot(wt, blk, preferred_element_type=F32) + bcol)
                dmb = (da[rows] * ug[rows]).astype(BF16)
                dws = dws + lax.dot_general(dmb, blk, (((1,), (1,)), ((), ())), preferred_element_type=F32)
                dbs = dbs + lax.dot_general(ones, dmb, (((1,), (1,)), ((), ())), preferred_element_type=F32)
                dvn_ref[c * CHUNK:(c + 1) * CHUNK, h * HD:(h + 1) * HD] = lax.dot_general(
                    wt, dmb, (((0,), (0,)), ((), ())), preferred_element_type=F32)
            mixed = mixed_parts[0] if nck == 1 else jnp.concatenate(mixed_parts, axis=0)
            dp_ref[:, u0 + h * HD:u0 + (h + 1) * HD] = ((da * mixed) * dug_du).astype(BF16)
            dws_ref[h] += jnp.where(low, dws, 0.0)
            dbs_ref[h] += dbs
        dvn = dvn_ref[...]
        dlng_ref[...] += jnp.sum(dvn * vhat, axis=0, keepdims=True)
        dlnb_ref[...] += jnp.sum(dvn, axis=0, keepdims=True)
        dvh = dvn * lng_ref[...]
        dvg = rstd * (dvh - jnp.mean(dvh, axis=-1, keepdims=True)
                      - vhat * jnp.mean(dvh * vhat, axis=-1, keepdims=True))
        dp_ref[:, v0:v0 + DS] = (dvg * dvg_dv).astype(BF16)

        dc = jnp.concatenate([head_grad(DS + h * HD) for h in range(NCH)], axis=1)
        dc_nx = jnp.concatenate(
            [head_bwd(heads_nx_ref[:, h * HD:(h + 1) * HD], dhn_nx_ref[:, h * HD:(h + 1) * HD],
                      gh_ref[:, DS + h * HD:DS + (h + 1) * HD])[0] for h in range(NCH)], axis=1)
        bg = p_ref[:, b0:b0 + DC]
        cg = p_ref[:, c0:c0 + DC]
        xin = p_ref[:, x0:x0 + DC]
        dp_ref[:, b0:b0 + DC] = (dc * ycv_ref[...]).astype(BF16)
        dyv = dc * bg
        buf_ref[0:tr, :] = dyv
        buf_ref[tr:tr + HALO, :] = jnp.where(i < n - 1, dc_nx * b_nx_ref[...], 0.0)
        sh1 = buf_ref[1:1 + tr, :]
        sh0 = buf_ref[2:2 + tr, :]
        dxc = cw_ref[2:3, :] * dyv + cw_ref[1:2, :] * sh1 + cw_ref[0:1, :] * sh0
        xc = cg * xin
        dp_ref[:, c0:c0 + DC] = (dxc * xin).astype(BF16)
        dp_ref[:, x0:x0 + DC] = (dxc * cg).astype(BF16)
        dcw_ref[0:1, :] += jnp.sum(sh0 * xc, axis=0, keepdims=True)
        dcw_ref[1:2, :] += jnp.sum(sh1 * xc, axis=0, keepdims=True)
        dcw_ref[2:3, :] += jnp.sum(dyv * xc, axis=0, keepdims=True)

        for h in range(NMH):
            do = head_grad(DS + DC + h * HD).astype(BF16)
            qh = (p_ref[:, q0 + h * HD:q0 + (h + 1) * HD] * SCALE).astype(BF16)
            kh = kv_ref[:, h * HD:(h + 1) * HD].astype(BF16)
            vh = kv_ref[:, DM + h * HD:DM + (h + 1) * HD].astype(BF16)
            p = _softmax_rows(qh, kh)
            dpr = lax.dot_general(do, vh, (((1,), (1,)), ((), ())), preferred_element_type=F32)
            ds = (p * (dpr - jnp.sum(dpr * p, axis=-1, keepdims=True))).astype(BF16)
            dp_ref[:, q0 + h * HD:q0 + (h + 1) * HD] = (
                jnp.dot(ds, kh, preferred_element_type=F32) * SCALE).astype(BF16)
            dkv_ref[:, h * HD:(h + 1) * HD] += lax.dot_general(
                ds, qh, (((0,), (0,)), ((), ())), preferred_element_type=F32)
            dkv_ref[:, DM + h * HD:DM + (h + 1) * HD] += lax.dot_general(
                p.astype(BF16), do, (((0,), (0,)), ((), ())), preferred_element_type=F32)

    full = lambda shape: pl.BlockSpec(shape, lambda i: (0,) * len(shape))
    row = lambda c: pl.BlockSpec((tr, c), lambda i: (i, 0))
    nxt = lambda col: pl.BlockSpec((HALO, DC), lambda i: (jnp.minimum((i + 1) * hb, last_hb), col))
    return pl.pallas_call(
        body, name=name, grid=(n,),
        in_specs=[row(D), row(D), row(DIN), row(DC), nxt(DS // DC), nxt(DS // DC), nxt(b0 // DC),
                  full((NMEM, 2 * DM)), full((NSH, CHUNK, CHUNK)), full((CHUNK, NSH)),
                  full((1, DS)), full((1, DS)), full((3, DC)), full((1, D)), ANY],
        out_specs=[row(DIN), full((NMEM, 2 * DM)), full((NSH, CHUNK, CHUNK)), full((NSH, HALO, CHUNK)),
                   full((1, DS)), full((1, DS)), full((HALO, DC)), full((1, D))],
        out_shape=[jax.ShapeDtypeStruct((S, DIN), BF16), jax.ShapeDtypeStruct((NMEM, 2 * DM), F32),
                   jax.ShapeDtypeStruct((NSH, CHUNK, CHUNK), F32), jax.ShapeDtypeStruct((NSH, HALO, CHUNK), F32),
                   jax.ShapeDtypeStruct((1, DS), F32), jax.ShapeDtypeStruct((1, DS), F32),
                   jax.ShapeDtypeStruct((HALO, DC), F32), jax.ShapeDtypeStruct((1, D), F32)],
        scratch_shapes=[pltpu.VMEM((tr + HALO, DC), F32), pltpu.VMEM((tr, DS), F32)],
        compiler_params=_cp(("arbitrary",), VMEM_MB),
    )(dhn, heads, proj, ycv, dhn, heads, proj, kv, w_s, bs_t, ln_g, ln_b, conv_w, g_head, after)


def _place():
    x, y, c = lax.axis_index("x"), lax.axis_index("y"), lax.axis_index("c")
    chips = [(1 - x, y), (x, 1 - y), (1 - x, 1 - y)]
    return x, y, c, chips


ANY = pl.BlockSpec(memory_space=pl.ANY)


HBM = pl.BlockSpec(memory_space=pltpu.HBM)
SEM = pl.BlockSpec(memory_space=pltpu.SEMAPHORE)
EFFECT = pltpu.SideEffectType.DATAFLOW_SIDE_EFFECTING
N_PEER_CHIPS = 3
N_NEIGHBOUR_CHIPS = 2
CONV_PAD = (32, 256)


def _in_hbm(a):
    return pltpu.with_memory_space_constraint(a, pltpu.HBM)


def _allgather_start(bufs, forwards, after, collective_id, *, name):
    arrs = list(bufs) + list(forwards)
    nw, nb = len(arrs), len(bufs)

    def body(*refs):
        ins, send, recv = refs[:nw], refs[nw + 1:2 * nw + 1], refs[2 * nw + 1:3 * nw + 1]
        token = refs[4 * nw + 1]
        x, y, c, chips = _place()
        s = 2 * x + y
        slots = [2 * cx + cy for cx, cy in chips]
        _handshake([(cx, cy, c) for cx, cy in chips[:N_NEIGHBOUR_CHIPS]])
        for w in range(nb, nw):
            q = arrs[w].shape[1] // 4
            for j in range(N_NEIGHBOUR_CHIPS):
                rows = ins[w].at[slots[j], pl.ds(c * 2 * q + j * q, q)]
                pltpu.make_async_remote_copy(src_ref=rows, dst_ref=rows, send_sem=send[w], recv_sem=recv[w],
                                             device_id=(*chips[1 - j], c), device_id_type=MESH).start()
        for w in range(nb):
            hr = arrs[w].shape[1] // 2
            rows = ins[w].at[s, pl.ds(c * hr, hr)]
            for cx, cy in chips[:N_NEIGHBOUR_CHIPS]:
                pltpu.make_async_remote_copy(src_ref=rows, dst_ref=rows, send_sem=send[w], recv_sem=recv[w],
                                             device_id=(cx, cy, c), device_id_type=MESH).start()
        token[...] = jnp.zeros_like(token)

    res = pl.pallas_call(
        body, name=name,
        in_specs=[HBM] * nw + [ANY],
        out_specs=[SEM] * (2 * nw) + [HBM] * nw + [pl.BlockSpec(memory_space=pltpu.VMEM)],
        out_shape=[pltpu.SemaphoreType.DMA(())] * (2 * nw) + [pltpu.HBM(a.shape, a.dtype) for a in arrs]
        + [jax.ShapeDtypeStruct((8, 128), F32)],
        input_output_aliases={w: 2 * nw + w for w in range(nw)},
        compiler_params=pltpu.CompilerParams(has_side_effects=EFFECT, collective_id=collective_id),
    )(*[_in_hbm(a) for a in arrs], after)
    return res[:nw], res[nw:2 * nw], res[2 * nw:3 * nw], res[3 * nw]


def _handshake(peers):
    barrier = pltpu.get_barrier_semaphore()
    for peer in peers:
        pl.semaphore_signal(barrier, inc=1, device_id=peer, device_id_type=MESH)
    pl.semaphore_wait(barrier, len(peers))


def _scatter_start(parts, bufs, collective_id, *, name):
    nw = len(parts)

    def body(*refs):
        src, dst = refs[:nw], refs[nw:2 * nw]
        send, recv = refs[2 * nw:3 * nw], refs[3 * nw:4 * nw]
        token = refs[6 * nw]
        x, y, c, chips = _place()
        s = 2 * x + y
        _handshake([(cx, cy, c) for cx, cy in chips])
        for w in range(nw):
            for cx, cy in chips:
                pltpu.make_async_remote_copy(src_ref=src[w].at[2 * cx + cy], dst_ref=dst[w].at[s], send_sem=send[w],
                                             recv_sem=recv[w], device_id=(cx, cy, c), device_id_type=MESH).start()
        token[...] = jnp.zeros_like(token)

    res = pl.pallas_call(
        body, name=name,
        in_specs=[HBM] * (2 * nw),
        out_specs=[SEM] * (2 * nw) + [HBM] * (2 * nw) + [pl.BlockSpec(memory_space=pltpu.VMEM)],
        out_shape=[pltpu.SemaphoreType.DMA(())] * (2 * nw) + [pltpu.HBM(a.shape, a.dtype) for a in parts + bufs]
        + [jax.ShapeDtypeStruct((8, 128), F32)],
        input_output_aliases={k: 2 * nw + k for k in range(2 * nw)},
        compiler_params=pltpu.CompilerParams(has_side_effects=EFFECT, collective_id=collective_id),
    )(*[_in_hbm(a) for a in parts + bufs])
    return res[:nw], res[nw:2 * nw], res[2 * nw:3 * nw], res[3 * nw:4 * nw], res[4 * nw]


def _sibling_start(srcs, whole, collective_id, *, name):
    nw = len(srcs)
    lands = [lax.empty((a.shape[0], a.shape[1] if whole else a.shape[1] // 2, a.shape[2]), a.dtype) for a in srcs]

    def body(*refs):
        src, land = refs[:nw], refs[nw:2 * nw]
        send, recv = refs[2 * nw:3 * nw], refs[3 * nw:4 * nw]
        token = refs[6 * nw]
        x, y, c, _ = _place()
        _handshake([(x, y, 1 - c)])
        for w in range(nw):
            hr = srcs[w].shape[1] // 2
            rows = src[w] if whole else src[w].at[:, pl.ds((1 - c) * hr, hr)]
            pltpu.make_async_remote_copy(src_ref=rows, dst_ref=land[w], send_sem=send[w], recv_sem=recv[w],
                                         device_id=(x, y, 1 - c), device_id_type=MESH).start()
        token[...] = jnp.zeros_like(token)

    res = pl.pallas_call(
        body, name=name,
        in_specs=[HBM] * (2 * nw),
        out_specs=[SEM] * (2 * nw) + [HBM] * (2 * nw) + [pl.BlockSpec(memory_space=pltpu.VMEM)],
        out_shape=[pltpu.SemaphoreType.DMA(())] * (2 * nw) + [pltpu.HBM(a.shape, a.dtype) for a in srcs + lands]
        + [jax.ShapeDtypeStruct((8, 128), F32)],
        input_output_aliases={k: 2 * nw + k for k in range(2 * nw)},
        compiler_params=pltpu.CompilerParams(has_side_effects=EFFECT, collective_id=collective_id),
    )(*[_in_hbm(a) for a in srcs + lands])
    return res[:nw], res[nw:2 * nw], res[2 * nw:3 * nw], res[3 * nw:4 * nw], res[4 * nw]


def _transfer_wait(sends, recvs, thru, sizes, after, *, name):
    n = len(sends)
    flat = [a for group in thru for a in group]

    def body(*refs):
        bufs = refs[:len(flat)]
        send = refs[len(flat):len(flat) + n]
        recv = refs[len(flat) + n:len(flat) + 2 * n]
        token = refs[2 * len(flat) + 2 * n + 1]
        token[...] = jnp.zeros_like(token)
        x, y, c, _ = _place()
        pos = 0
        for k in range(n):
            slots, rows = sizes[k]
            region = bufs[pos].at[pl.ds(0, slots), pl.ds(0, rows)]
            pos += len(thru[k])
            cp = pltpu.make_async_remote_copy(src_ref=region, dst_ref=region, send_sem=send[k], recv_sem=recv[k],
                                              device_id=(x, y, 1 - c), device_id_type=MESH)
            cp.wait_send()
            cp.wait_recv()

    res = pl.pallas_call(
        body, name=name,
        in_specs=[HBM] * len(flat) + [SEM] * (2 * n) + [pl.BlockSpec(memory_space=pl.ANY)],
        out_specs=[HBM] * len(flat) + [pl.BlockSpec(memory_space=pltpu.VMEM)],
        out_shape=[pltpu.HBM(a.shape, a.dtype) for a in flat] + [jax.ShapeDtypeStruct((8, 128), F32)],
        input_output_aliases={k: k for k in range(len(flat))},
        compiler_params=pltpu.CompilerParams(has_side_effects=EFFECT),
    )(*flat, *sends, *recvs, after)
    out, pos = [], 0
    for group in thru:
        out.append(res[pos:pos + len(group)])
        pos += len(group)
    return out, res[len(flat)]


def _forward_gathered(bufs, after, *, name):
    nw = len(bufs)

    def body(*refs):
        outs = refs[nw + 1:2 * nw + 1]
        d_send, d_recv, i_send, i_recv = refs[2 * nw + 1:]
        x, y, c, chips = _place()
        me, sibling = (x, y, c), (x, y, 1 - c)
        slots = [2 * cx + cy for cx, cy in chips]

        def rows(w, j, start, n):
            return outs[w].at[slots[j], pl.ds(start, n)]

        def d2d(w, j, which, to):
            hr = bufs[w].shape[1] // 2
            r = rows(w, j, which * hr, hr)
            return pltpu.make_async_remote_copy(
                src_ref=r, dst_ref=r, send_sem=d_send.at[N_PEER_CHIPS * w + j],
                recv_sem=d_recv.at[N_PEER_CHIPS * w + j], device_id=to, device_id_type=MESH)

        def ici(w, j, slot_j, to):
            q = bufs[w].shape[1] // 4
            r = rows(w, slot_j, c * 2 * q + j * q, q)
            return pltpu.make_async_remote_copy(
                src_ref=r, dst_ref=r, send_sem=i_send.at[N_NEIGHBOUR_CHIPS * w + j],
                recv_sem=i_recv.at[N_NEIGHBOUR_CHIPS * w + j], device_id=to, device_id_type=MESH)

        started = []
        for w in range(nw):
            started += [ici(w, 0, 0, (*chips[1], c)), ici(w, 1, 1, (*chips[0], c))]
            started += [d2d(w, j, c, sibling) for j in range(N_NEIGHBOUR_CHIPS)]
        for cp in started:
            cp.start()
        diag = N_PEER_CHIPS - 1
        for w in range(nw):
            for j in range(N_NEIGHBOUR_CHIPS):
                ici(w, j, diag, me).wait_recv()
            cp = d2d(w, diag, c, sibling)
            cp.start()
            started.append(cp)
        for w in range(nw):
            for j in range(N_PEER_CHIPS):
                d2d(w, j, 1 - c, me).wait_recv()
        for cp in started:
            cp.wait_send()

    return pl.pallas_call(
        body, name=name,
        in_specs=[ANY] * (nw + 1), out_specs=[ANY] * nw,
        out_shape=[jax.ShapeDtypeStruct(a.shape, a.dtype) for a in bufs],
        input_output_aliases={w: w for w in range(nw)},
        scratch_shapes=[pltpu.SemaphoreType.DMA((N_PEER_CHIPS * nw,)), pltpu.SemaphoreType.DMA((N_PEER_CHIPS * nw,)),
                        pltpu.SemaphoreType.DMA((N_NEIGHBOUR_CHIPS * nw,)),
                        pltpu.SemaphoreType.DMA((N_NEIGHBOUR_CHIPS * nw,))],
    )(*bufs, after)


def _forward_halves(bufs, which, after, *, name):
    nw = len(bufs)
    n = len(which)

    def body(*refs):
        outs = refs[nw + 1:2 * nw + 1]
        send, recv = refs[2 * nw + 1:]
        x, y, c, chips = _place()
        me, sibling = (x, y, c), (x, y, 1 - c)

        def d2d(w, t, half, to):
            cx, cy = chips[which[t]]
            hr = bufs[w].shape[1] // 2
            rows = outs[w].at[2 * cx + cy, pl.ds(half * hr, hr)]
            return pltpu.make_async_remote_copy(src_ref=rows, dst_ref=rows, send_sem=send.at[n * w + t],
                                                recv_sem=recv.at[n * w + t], device_id=to, device_id_type=MESH)

        passed = [d2d(w, t, c, sibling) for w in range(nw) for t in range(n)]
        for cp in passed:
            cp.start()
        for w in range(nw):
            for t in range(n):
                d2d(w, t, 1 - c, me).wait_recv()
        for cp in passed:
            cp.wait_send()

    return pl.pallas_call(
        body, name=name,
        in_specs=[ANY] * (nw + 1), out_specs=[ANY] * nw,
        out_shape=[jax.ShapeDtypeStruct(a.shape, a.dtype) for a in bufs],
        input_output_aliases={w: w for w in range(nw)},
        scratch_shapes=[pltpu.SemaphoreType.DMA((n * nw,)), pltpu.SemaphoreType.DMA((n * nw,))],
    )(*bufs, after)


def _allreduce_small(p, after, *, name):
    R = p.shape[0]
    hr = R // 2

    def body(p_ref, _after_ref, out_ref, sib_ref, sum_ref, gat_ref, tot_ref, send, recv):
        x, y, c, chips = _place()
        s = 2 * x + y
        sibling = (x, y, 1 - c)
        rows = pl.ds(pl.multiple_of(c * hr, 8), hr)
        swap = pltpu.make_async_remote_copy(src_ref=p_ref, dst_ref=sib_ref, send_sem=send.at[0], recv_sem=recv.at[0],
                                            device_id=sibling, device_id_type=MESH)
        swap.start()
        swap.wait()
        sum_ref[...] = p_ref[...] + sib_ref[...]
        gat_ref[s] = sum_ref[rows, :]
        cps = [pltpu.make_async_remote_copy(src_ref=sum_ref.at[rows], dst_ref=gat_ref.at[s], send_sem=send.at[1 + j],
                                            recv_sem=recv.at[1 + j], device_id=(cx, cy, c), device_id_type=MESH)
               for j, (cx, cy) in enumerate(chips)]
        for cp in cps:
            cp.start()
        for cp in cps:
            cp.wait()
        tot_ref[...] = ((gat_ref[0] + gat_ref[1]) + gat_ref[2]) + gat_ref[3]
        out_ref[rows, :] = tot_ref[...]
        share = pltpu.make_async_remote_copy(src_ref=tot_ref, dst_ref=out_ref.at[rows], send_sem=send.at[4],
                                             recv_sem=recv.at[4], device_id=sibling, device_id_type=MESH)
        share.start()
        share.wait_send()
        other = out_ref.at[pl.ds(pl.multiple_of((1 - c) * hr, 8), hr)]
        pltpu.make_async_remote_copy(src_ref=other, dst_ref=other, send_sem=send.at[4], recv_sem=recv.at[4],
                                     device_id=(x, y, c), device_id_type=MESH).wait_recv()

    vmem = pl.BlockSpec(memory_space=pltpu.VMEM)
    return pl.pallas_call(
        body, name=name, in_specs=[vmem, ANY], out_specs=vmem,
        out_shape=jax.ShapeDtypeStruct((R, 128), F32),
        scratch_shapes=[pltpu.VMEM((R, 128), F32), pltpu.VMEM((R, 128), F32), pltpu.VMEM((NCHIP, hr, 128), F32),
                        pltpu.VMEM((hr, 128), F32), pltpu.SemaphoreType.DMA((5,)), pltpu.SemaphoreType.DMA((5,))],
    )(p, after)


def _select_half_bf16(g, half, add, slot, *, name):
    _, R, C = g.shape
    hr = R // 2
    tr = _pick_rows(hr, 16)
    nb = hr // tr
    sel = jnp.concatenate([jnp.reshape(half, (1,)).astype(jnp.int32), slot])

    def body(s_ref, g_ref, a_ref, o_ref, own_ref):
        val = (g_ref[...].astype(F32) + a_ref[...].astype(F32)).astype(BF16)
        o_ref[...] = val

        @pl.when(pl.program_id(1) == s_ref[1])
        def _():
            own_ref[...] = val

    g_spec = pl.BlockSpec((None, tr, C), lambda i, j, s: (j, s[0] * nb + i, 0))
    o_spec = pl.BlockSpec((None, tr, C), lambda i, j, s: (j, i, 0))
    own_spec = pl.BlockSpec((None, tr, C), lambda i, j, s: (s[1], i, 0))
    shape = jax.ShapeDtypeStruct((NCHIP, hr, C), BF16)
    return pl.pallas_call(
        body, name=name,
        grid_spec=pltpu.PrefetchScalarGridSpec(
            num_scalar_prefetch=1, grid=(nb, NCHIP), in_specs=[g_spec, o_spec], out_specs=[o_spec, own_spec]),
        out_shape=[shape, shape],
        compiler_params=_cp(("parallel", "arbitrary"), VMEM_MB),
    )(sel, g, add)


def _adamw_math(w, g, m, v):
    m = ADAM_B1 * m + (1.0 - ADAM_B1) * g
    v = ADAM_B2 * v + (1.0 - ADAM_B2) * (g * g)
    m_hat = m / (1.0 - ADAM_B1 ** ADAM_STEP)
    v_hat = v / (1.0 - ADAM_B2 ** ADAM_STEP)
    delta = -ADAM_LR * (m_hat / (jnp.sqrt(v_hat) + ADAM_EPS) + ADAM_WD * w)
    return delta, m, v


def _adamw(w, g_mine, g_sib, m, v, core, *, name):
    R, C = w.shape
    hr = R // 2
    tr = _pick_rows(hr, 16)
    nb = hr // tr
    row = pl.BlockSpec((tr, C), lambda hh, i, c: (hh * nb + i, 0))
    mine = pl.BlockSpec((NCHIP, tr, C), lambda hh, i, c: (0, jnp.where(hh == c[0], i, 0), 0))
    sibs = pl.BlockSpec((NCHIP, tr, C), lambda hh, i, c: (0, jnp.where(hh == c[0], 0, i), 0))

    def slot_sum(ref):
        acc = ref[0].astype(F32) + ref[1].astype(F32)
        for j in range(2, NCHIP):
            acc = acc + ref[j].astype(F32)
        return acc

    def body(c_ref, w_ref, gm_ref, gs_ref, m_ref, v_ref, go_ref, d_ref, mo_ref, vo_ref):
        gv = jnp.where(pl.program_id(0) == c_ref[0], slot_sum(gm_ref), slot_sum(gs_ref))
        d, mn, vn = _adamw_math(w_ref[...], gv, m_ref[...], v_ref[...])
        go_ref[...] = gv
        d_ref[...] = d
        mo_ref[...] = mn
        vo_ref[...] = vn

    return pl.pallas_call(
        body, name=name,
        grid_spec=pltpu.PrefetchScalarGridSpec(
            num_scalar_prefetch=1, grid=(2, nb),
            in_specs=[row, mine, sibs, row, row], out_specs=[row] * 4),
        out_shape=[jax.ShapeDtypeStruct((R, C), F32)] * 4,
        compiler_params=_cp(("parallel", "parallel"), VMEM_MB),
    )(core, w, g_mine, g_sib, m, v)


def _adamw_small(ws, gs, ms, vs, *, name):
    n = len(ws)

    def body(*refs):
        w_r, g_r, m_r, v_r = refs[:n], refs[n:2 * n], refs[2 * n:3 * n], refs[3 * n:4 * n]
        d_r, mo_r, vo_r = refs[4 * n:5 * n], refs[5 * n:6 * n], refs[6 * n:7 * n]
        for k in range(n):
            d, mn, vn = _adamw_math(w_r[k][...], g_r[k][...], m_r[k][...], v_r[k][...])
            d_r[k][...] = d
            mo_r[k][...] = mn
            vo_r[k][...] = vn

    shapes = [jax.ShapeDtypeStruct(w.shape, F32) for w in ws]
    res = pl.pallas_call(body, name=name, out_shape=shapes * 3)(*ws, *gs, *ms, *vs)
    return res[:n], res[n:2 * n], res[2 * n:]


_PACK_ROWS = 8


def _pack(parts):
    rows = []
    for a in parts:
        flat = a.reshape(-1)
        n = -(-flat.shape[0] // (_PACK_ROWS * 128)) * (_PACK_ROWS * 128)
        rows.append(jnp.pad(flat, (0, n - flat.shape[0])).reshape(-1, 128))
    total = sum(r.shape[0] for r in rows)
    if total % 16:
        rows.append(jnp.zeros((16 - total % 16, 128), F32))
    return jnp.concatenate(rows, axis=0)


def _unpack(p, shapes):
    out, r = [], 0
    for shp in shapes:
        n = math.prod(shp)
        nr = -(-n // (_PACK_ROWS * 128)) * _PACK_ROWS
        out.append(p[r:r + nr].reshape(-1)[:n].reshape(shp))
        r += nr
    return out


def kernel(x, mem, g_mix, w_in, ln_v_g, ln_v_b, w_s, b_s, conv_w, g_mem, w_kv, g_head, w_o, g_ffn, w_ffn1, w_ffn2, g_final, loss_target, m_g_mix, m_w_in, m_ln_v_g, m_ln_v_b, m_w_s, m_b_s, m_conv_w, m_g_mem, m_w_kv, m_g_head, m_w_o, m_g_ffn, m_w_ffn1, m_w_ffn2, m_g_final, v_g_mix, v_w_in, v_ln_v_g, v_ln_v_b, v_w_s, v_b_s, v_conv_w, v_g_mem, v_w_kv, v_g_head, v_w_o, v_g_ffn, v_w_ffn1, v_w_ffn2, v_g_final):
    sds = jax.ShapeDtypeStruct
    xi, yi = lax.axis_index("x"), lax.axis_index("y")
    shard = 2 * xi + yi
    x2d, mem2d, tgt = x[0], mem[0], loss_target[0]
    ws3, bs2 = w_s[0], b_s[0]
    g_final2 = g_final.reshape(1, D)
    dff4 = DFF // NCHIP
    din4 = DIN // NCHIP
    dcv4 = DC // NCHIP

    big = [w_in[0].T, w_kv[0], w_o[0], w_ffn1[0], w_ffn2[0]]
    big_names = ["w_in", "w_kv", "w_o", "w_ffn1", "w_ffn2"]
    slot = jnp.reshape(shard, (1,)).astype(jnp.int32)
    core = jnp.reshape(lax.axis_index("c"), (1,)).astype(jnp.int32)
    conv_pad = jnp.pad(conv_w[0], ((0, CONV_PAD[0] - 3), (0, CONV_PAD[1] - dcv4)))
    conv_slots = lax.dynamic_update_slice(jnp.zeros((NCHIP,) + CONV_PAD, F32), conv_pad[None], (shard, 0, 0))

    gather_ids = {"in": 16, "kvo": 17, "ffn1": 18, "ffn2": 19, "ffn2d": 20}

    def gather_start(bufs, after, nm, forwards=()):
        return _allgather_start(bufs, forwards, after, gather_ids[nm], name="ag_start_" + nm)

    def gather_wait(state, idx, after, nm):
        send, recv, bufs, _ = state
        got, token = _transfer_wait([send[k] for k in idx], [recv[k] for k in idx], [[bufs[k]] for k in idx],
                                    [(N_NEIGHBOUR_CHIPS, bufs[k].shape[1] // 2) for k in idx], after, name="ag_wait_" + nm)
        return [g[0] for g in got], token

    cast = lambda k, after: _cast_into_slot(big[k], slot, after, name="cast_" + big_names[k])
    ag_in = gather_start([cast(0, slot), conv_slots], slot, "in")
    bs_t = bs2.T

    h = _rms_fwd(x2d, g_mix, name="rms_mix", after=[ag_in[3]])
    mem_n = _rms_fwd(mem2d, g_mem, name="rms_mem", after=[h])
    kvo_b = [cast(1, mem_n)]
    kvo_b.append(cast(2, kvo_b[0]))
    w1_b = cast(3, kvo_b[1])
    w2_b = cast(4, w1_b)
    NEAR, FAR = [0, 1], [2]

    def diagonal_wait(state, ks, after, nm):
        send, recv, bufs, _ = state
        got, token = _transfer_wait([send[k] for k in ks], [recv[k] for k in ks], [[bufs[k]] for k in ks],
                                    [(1, bufs[k].shape[1] // 2) for k in ks], after, name="ag_waitd_" + nm)
        return [g[0] for g in got], token

    got_in, tok = gather_wait(ag_in, [0, 1], w2_b, "in")
    ag_kvo = gather_start(kvo_b, tok, "kvo", forwards=got_in)
    in_n = _forward_halves(ag_kvo[2][2:], NEAR, ag_kvo[3], name="ag_fwdn_in")
    ag_kvo = (ag_kvo[0], ag_kvo[1], list(ag_kvo[2][:2]) + list(in_n), ag_kvo[3])
    in_d, tok = diagonal_wait(ag_kvo, [2, 3], ag_kvo[3], "in")
    win4, conv4 = _forward_halves(in_d, FAR, tok, name="ag_fwdd_in")
    w_in_t = win4.reshape(DIN, D)
    conv_full = conv4[:, :3, :dcv4].transpose(1, 0, 2).reshape(3, DC)

    proj_w = lambda tn, tk: pl.BlockSpec((tn, tk), lambda j, i, k, s: (s[j], k))
    proj_cols = lambda tm, tn: [pl.BlockSpec((tm, tn), lambda j, i, k, s: (i, s[j]))]
    proj_half = lambda which, into, after: _matmul(
        h, w_in_t, name="mm_proj_%d" % which, tb=True, M=S, N=DIN // 2, K=D, tn=DIN // 2, b_spec=proj_w,
        out_specs=proj_cols, outs=[sds((S, DIN), F32)], slots=jnp.full((1,), which, jnp.int32), into=into,
        after=after)[0]
    proj = proj_half(0, [], [ag_kvo[3]])
    got_kvo, tok = gather_wait(ag_kvo, [0, 1], proj, "kvo")
    ag_w1 = gather_start([w1_b], tok, "ffn1", forwards=got_kvo)
    kvo_n = _forward_halves(ag_w1[2][1:], NEAR, ag_w1[3], name="ag_fwdn_kvo")
    ag_w1 = (ag_w1[0], ag_w1[1], [ag_w1[2][0]] + list(kvo_n), ag_w1[3])
    proj = proj_half(1, [proj], list(kvo_n))
    kvo_d, tok = diagonal_wait(ag_w1, [1, 2], proj, "kvo")
    wkv4, wo4 = _forward_halves(kvo_d, FAR, tok, name="ag_fwdd_kvo")
    w_kv_full = wkv4.reshape(D, 2 * DM)
    w_o_full = wo4.reshape(D, D)
    (kv,) = _matmul(mem_n, w_kv_full, name="mm_kv", M=NMEM, N=2 * DM, K=D, outs=[sds((NMEM, 2 * DM), F32)])
    heads, hn, ycv = _mix_fwd(proj, kv, ws3, bs_t, ln_v_g, ln_v_b, conv_full, g_head, name="mix_fwd")
    def residual_and_norm(acc, res, g):
        x2v = acc + res
        r = lax.rsqrt(jnp.mean(x2v * x2v, axis=-1, keepdims=True) + EPS)
        return x2v, (x2v * r) * g

    row_vec = lambda tm, tn: pl.BlockSpec((1, tn), lambda j, i, k, *s: (0, j))
    x2, h2 = _matmul(hn, w_o_full, name="mm_wo", M=S, N=D, K=D, tn=D, n_split=1, epi=residual_and_norm,
                     outs=[sds((S, D), F32), sds((S, D), BF16)], extras=[(x2d, _tile_spec()), (g_ffn, row_vec)])
    near = jnp.stack([shard, 2 * (1 - xi) + yi, 2 * xi + (1 - yi)]).astype(jnp.int32)
    far = jnp.reshape(2 * (1 - xi) + (1 - yi), (1,)).astype(jnp.int32)

    w1_shard = lambda tn, tk: pl.BlockSpec((None, tk, tn), lambda j, i, k, s: (s[j], k, 0))
    act_cols = lambda tm, tn: [pl.BlockSpec((tm, tn), lambda j, i, k, s: (i, s[j]))] * 2

    def relu2(acc):
        r = jnp.maximum(acc, 0.0)
        return r * r, 2.0 * r

    got_w1, tok = gather_wait(ag_w1, [0], h2, "ffn1")
    ag_w2 = gather_start([w2_b], tok, "ffn2", forwards=got_w1)
    (w1n,) = _forward_halves([ag_w2[2][1]], NEAR, ag_w2[3], name="ag_fwdn_ffn1")
    ag_w2 = (ag_w2[0], ag_w2[1], [ag_w2[2][0], w1n], ag_w2[3])
    act, dact_df = _matmul(h2, w1n, name="mm_ffn1_near", M=S, N=3 * dff4, K=D, tn=dff4, b_spec=w1_shard,
                           out_specs=act_cols, outs=[sds((S, DFF), BF16)] * 2, epi=relu2, slots=near)
    w1d, tok = diagonal_wait(ag_w2, [1], act, "ffn1")
    (w14,) = _forward_halves(w1d, FAR, tok, name="ag_fwdd_ffn1")
    act, dact_df = _matmul(h2, w14, name="mm_ffn1_far", M=S, N=dff4, K=D, tn=dff4, b_spec=w1_shard,
                           out_specs=act_cols, outs=[sds((S, DFF), BF16)] * 2, epi=relu2, slots=far,
                           into=[act, dact_df])

    act_shard = lambda tm, tk: pl.BlockSpec((tm, tk), lambda j, i, k, s: (i, s[k]))
    w2_shard = lambda tn, tk: pl.BlockSpec((None, tk, tn), lambda j, i, k, s: (s[k], 0, j))
    got_w2, tok = gather_wait(ag_w2, [0], act, "ffn2")
    ag_w2d = gather_start([], tok, "ffn2d", forwards=got_w2)
    (w2n,) = _forward_halves(ag_w2d[2], NEAR, ag_w2d[3], name="ag_fwdn_ffn2")
    ag_w2d = (ag_w2d[0], ag_w2d[1], [w2n], ag_w2d[3])
    (x3,) = _matmul(act, w2n, name="mm_ffn2_near", M=S, N=D, K=3 * dff4, tm=2 * TM, tk=dff4,
                    a_spec=act_shard, b_spec=w2_shard, outs=[sds((S, D), F32)], epi=lambda acc, res: (acc + res,),
                    extras=[(x2, _tile_spec())], slots=near)
    w2d, tok = diagonal_wait(ag_w2d, [0], x3, "ffn2")
    (w24,) = _forward_halves(w2d, FAR, tok, name="ag_fwdd_ffn2")
    (x3,) = _matmul(act, w24, name="mm_ffn2_far", M=S, N=D, K=dff4, tm=2 * TM, tk=dff4, a_spec=act_shard,
                    b_spec=w2_shard, outs=[sds((S, D), F32)], epi=lambda acc, res: (acc + res,),
                    extras=[(x3, _tile_spec())], slots=far)
    w2_full = w24.reshape(DFF, D)

    ci = lax.axis_index("c")

    def rs_sibling(g4, nm):
        return _sibling_start([g4], False, 1 + big_names.index(nm), name="rs_sib_" + nm)

    def rs_chips(state, after, nm):
        send, recv, g4, land, _ = state
        (((land_, g4_),), _) = _transfer_wait(send, recv, [[land[0], g4[0]]], [(NCHIP, land[0].shape[1])], after,
                                             name="rs_sibwait_" + nm)
        part, buf = _select_half_bf16(g4_, ci, land_, slot, name="rs_add_" + nm)
        return _scatter_start([part], [buf], 1 + 2 * len(big_names) + big_names.index(nm), name="rs_start_" + nm)

    def dw_half(a, b, nm, *, by_rows, hr, cols, which, land, after):
        tile = lambda tm, tn: pl.BlockSpec(
            (None, tm, tn), (lambda j, i, k, s: (i, 0, j)) if by_rows else (lambda j, i, k, s: (j, 0, 0)))
        a_half = lambda tm, tk: pl.BlockSpec(
            (tk, tm), (lambda j, i, k, s: (k, 2 * i + s[0])) if by_rows else (lambda j, i, k, s: (k, s[0])))
        (out,) = _matmul(a, b, name=nm, ta=True, M=NCHIP * hr if by_rows else hr, N=cols if by_rows else NCHIP * cols,
                         K=S, tm=hr, tn=cols, a_spec=a_half, out_specs=lambda tm, tn: [tile(tm, tn)],
                         outs=[sds((NCHIP, hr, cols), BF16)], slots=jnp.reshape(which, (1,)).astype(jnp.int32),
                         epi=None if land is None else (lambda acc, other: (acc + other.astype(F32),)),
                         extras=[] if land is None else [(land, tile)], after=after)
        return out

    def rs_sibling_half(half, nm):
        return _sibling_start([half], True, 1 + big_names.index(nm), name="rs_sib_" + nm)

    def rs_chips_fused(state, grad_half, after, nm):
        send, recv, mine, land, _ = state
        (((land_, _),), tok) = _transfer_wait(send, recv, [[land[0], mine[0]]], [(NCHIP, land[0].shape[1])], after,
                                             name="rs_sibwait_" + nm)
        part = grad_half(land_, [tok])
        buf = _own_slot(part, slot, name="rs_own_" + nm)
        return _scatter_start([part], [buf], 1 + 2 * len(big_names) + big_names.index(nm), name="rs_start_" + nm)

    def rs_end(state, after, nm):
        send, recv, parts, bufs, _ = state
        (((buf, _),), _) = _transfer_wait(send, recv, [[bufs[0], parts[0]]], [(N_PEER_CHIPS, bufs[0].shape[1])], after,
                                          name="rs_wait_" + nm)
        return _sibling_start([buf], True, 1 + len(big_names) + big_names.index(nm), name="rs_share_" + nm)

    big_m = [m_w_in[0].T, m_w_kv[0], m_w_o[0], m_w_ffn1[0], m_w_ffn2[0]]
    big_v = [v_w_in[0].T, v_w_kv[0], v_w_o[0], v_w_ffn1[0], v_w_ffn2[0]]
    big_out = {}

    def rs_finish(k, state, after):
        send, recv, mine, land, _ = state
        nm = big_names[k]
        (((land_, mine_),), _) = _transfer_wait(send, recv, [[land[0], mine[0]]], [(NCHIP, land[0].shape[1])], after,
                                               name="rs_sharewait_" + nm)
        big_out[nm] = _adamw(big[k], mine_, land_, big_m[k], big_v[k], core, name="adamw_" + nm)
        return big_out[nm][1]

    dx3, dx3b, dg_final, loss11 = _loss_bwd(x3, g_final2, tgt, name="loss_bwd")
    dw2_half = lambda which, land, after, nm: dw_half(
        act, dx3b, nm, by_rows=True, hr=dff4 // 2, cols=D, which=which, land=land, after=after)
    sib_w2 = rs_sibling_half(dw2_half(1 - ci, None, [], "mm_dw2_sib"), "w_ffn2")
    (dfb,) = _matmul(dx3b, w2_full, name="mm_dact", tb=True, M=S, N=DFF, K=D, tm=2 * TM, tn=dff4, outs=[sds((S, DFF), BF16)],
                     epi=lambda acc, g: (acc * g.astype(F32),), extras=[(dact_df, _tile_spec())],
                     after=[sib_w2[4]])
    rs_w2 = rs_chips_fused(sib_w2, lambda land, after: dw2_half(ci, land, after, "mm_dw2_own"), dfb, "w_ffn2")

    dw1_half = lambda which, land, after, nm: dw_half(
        h2, dfb, nm, by_rows=False, hr=D // 2, cols=dff4, which=which, land=land, after=after)
    sib_w1 = rs_sibling_half(dw1_half(1 - ci, None, [rs_w2[4]], "mm_dw1_sib"), "w_ffn1")

    def w1_rows(tn, tk):
        kb = dff4 // tk
        return pl.BlockSpec((None, tn, tk), lambda j, i, k: (k // kb, j, k % kb))

    (dh2,) = _matmul(dfb, w14, name="mm_dh2", tb=True, M=S, N=D, K=DFF, tm=2 * TM, b_spec=w1_rows,
                     outs=[sds((S, D), F32)], after=[sib_w1[4]])
    rs_w1 = rs_chips_fused(sib_w1, lambda land, after: dw1_half(ci, land, after, "mm_dw1_own"), dh2, "w_ffn1")
    dx2, dx2b, dg_ffn = _rms_bwd(dh2, x2, g_ffn, dx3, name="rms_ffn_bwd", after=[rs_w1[4]])
    dwo_half = lambda which, land, after, nm: dw_half(
        hn, dx2b, nm, by_rows=True, hr=D // NCHIP // 2, cols=D, which=which, land=land, after=after)
    sib_wo = rs_sibling_half(dwo_half(1 - ci, None, [], "mm_dwo_sib"), "w_o")
    (dhn,) = _matmul(dx2b, w_o_full, name="mm_dhn", tb=True, M=S, N=D, K=D, tm=2 * TM, outs=[sds((S, D), F32)],
                     after=[sib_wo[4]])
    rs_wo = rs_chips_fused(sib_wo, lambda land, after: dwo_half(ci, land, after, "mm_dwo_own"), dhn, "w_o")
    sh_w2 = rs_end(rs_w2, rs_wo[4], "w_ffn2")
    dproj, dkv, dws, dbs8, dlng, dlnb, dcw8, dgh = _mix_bwd(
        dhn, heads, proj, ycv, kv, ws3, bs_t, ln_v_g, ln_v_b, conv_full, g_head, sh_w2[4], name="mix_bwd")
    (dwin_t,) = _matmul(dproj, h, name="mm_dwin", ta=True, M=DIN, N=D, K=S, tm=DIN // 2, outs=[sds((DIN, D), BF16)])
    sib_win = rs_sibling(dwin_t.reshape(NCHIP, din4, D), "w_in")
    (dwkv,) = _matmul(mem_n, dkv, name="mm_dwkv", ta=True, M=D, N=2 * DM, K=NMEM, outs=[sds((D, 2 * DM), BF16)],
                      after=[sib_win[4]])
    sib_wkv = rs_sibling(dwkv.reshape(NCHIP, D // NCHIP, 2 * DM), "w_kv")
    (dh,) = _matmul(dproj, w_in_t, name="mm_dh", M=S, N=D, K=DIN, tm=2 * TM, tk=DIN, outs=[sds((S, D), F32)],
                    after=[sib_wkv[4]])
    rs_win = rs_chips(sib_win, dh, "w_in")
    rs_wkv = rs_chips(sib_wkv, rs_win[4], "w_kv")
    dx, dg_mix = _rms_bwd(dh, x2d, g_mix, dx2, name="rms_mix_bwd", want_bf=False, after=[rs_wkv[4]])
    sh_w1 = rs_end(rs_w1, dx, "w_ffn1")
    (dmem_n,) = _matmul(dkv, w_kv_full, name="mm_dmem", tb=True, M=NMEM, N=D, K=2 * DM, outs=[sds((NMEM, D), F32)],
                        after=[sh_w1[4]])
    (dg_mem,) = _rms_bwd(dmem_n, mem2d, g_mem, None, name="rms_mem_bwd", want_dx=False)
    sh_wo = rs_end(rs_wo, dg_mem, "w_o")
    done = rs_finish(4, sh_w2, sh_wo[4])
    done = rs_finish(3, sh_w1, done)
    sh_win = rs_end(rs_win, done, "w_in")
    sh_wkv = rs_end(rs_wkv, sh_win[4], "w_kv")
    done = rs_finish(2, sh_wo, sh_wkv[4])
    done = rs_finish(0, sh_win, done)
    done = rs_finish(1, sh_wkv, done)

    small_names = ["g_mix", "ln_v_g", "ln_v_b", "w_s", "b_s", "conv_w", "g_mem", "g_head", "g_ffn", "g_final"]
    small_part = [dg_mix, dlng, dlnb, dws, dbs8[:, 0, :], dcw8[:3], dg_mem, dgh, dg_ffn, dg_final, loss11]
    small_shapes = [(1, D), (1, DS), (1, DS), (NSH, CHUNK, CHUNK), (NSH, CHUNK), (3, DC), (1, D), (1, D), (1, D), (1, D),
                    (1, 1)]
    total = _allreduce_small(_pack(small_part), done, name="allreduce_small")
    small_g = _unpack(total, small_shapes)
    loss = small_g.pop()[0, 0]
    small_g[5] = lax.dynamic_slice(small_g[5], (0, shard * dcv4), (3, dcv4))
    small_w = [g_mix, ln_v_g, ln_v_b, ws3, bs2, conv_w[0], g_mem, g_head, g_ffn, g_final2]
    small_m = [m_g_mix, m_ln_v_g, m_ln_v_b, m_w_s[0], m_b_s[0], m_conv_w[0], m_g_mem, m_g_head, m_g_ffn,
               m_g_final.reshape(1, D)]
    small_v = [v_g_mix, v_ln_v_g, v_ln_v_b, v_w_s[0], v_b_s[0], v_conv_w[0], v_g_mem, v_g_head, v_g_ffn,
               v_g_final.reshape(1, D)]
    s_delta, s_m, s_v = _adamw_small(small_w, small_g, small_m, small_v, name="adamw_small")
    small_out = {nm: (g, d, mn, vn) for nm, g, d, mn, vn in zip(small_names, small_g, s_delta, s_m, s_v)}

    order = ["g_mix", "w_in", "ln_v_g", "ln_v_b", "w_s", "b_s", "conv_w", "g_mem", "w_kv", "g_head", "w_o",
             "g_ffn", "w_ffn1", "w_ffn2", "g_final"]
    like = dict(g_mix=g_mix, w_in=w_in, ln_v_g=ln_v_g, ln_v_b=ln_v_b, w_s=w_s, b_s=b_s, conv_w=conv_w, g_mem=g_mem,
                w_kv=w_kv, g_head=g_head, w_o=w_o, g_ffn=g_ffn, w_ffn1=w_ffn1, w_ffn2=w_ffn2, g_final=g_final)
    res = {**big_out, **small_out}
    res["w_in"] = [a.T for a in res["w_in"]]
    outs = [loss, dx[None]]
    for k in range(4):
        outs += [res[nm][k].reshape(like[nm].shape) for nm in order]
    return tuple(outs)
```

```python
import math

import jax
import jax.numpy as jnp
from jax import lax
from jax.experimental import pallas as pl
from jax.experimental.pallas import tpu as pltpu

F32 = jnp.float32
BF16 = jnp.bfloat16
MESH = pl.DeviceIdType.MESH

D = 2048
S = 2048
HD = 128
NH = D // HD
NMH = 4
NSH = (NH - NMH) // 2
NCH = NH - NMH - NSH
DS = NSH * HD
DC = NCH * HD
DM = NMH * HD
DIN = 2 * DS + 3 * DC + DM
CHUNK = 128
NMEM = 256
DFF = 4 * D
EPS = 1e-6
NCHIP = 4
SCALE = HD ** -0.5

ADAM_LR = 0.001
ADAM_B1 = 0.9
ADAM_B2 = 0.999
ADAM_EPS = 1e-08
ADAM_WD = 0.01
ADAM_STEP = 10

TR_EW = 256
TR_MIX = 256
TM = 512
TN = 1024
TK = 2048
N_SUB = 512
VMEM_MB = 56
HALO = 8


def _pick(n, target, q=128):
    best = None
    for t in range(q, min(n, target) + 1, q):
        if n % t == 0:
            best = t
    return n if best is None else best


def _pick_rows(n, q):
    below = _pick(n, TR_EW, q)
    if 2 * below >= TR_EW:
        return below
    above = [t for t in range(TR_EW, min(n, 4 * TR_EW) + 1, q) if n % t == 0]
    return above[0] if above else below


def _cp(sem=None, vmem_mb=None, **kw):
    d = dict(kw)
    if sem is not None:
        d["dimension_semantics"] = sem
    if vmem_mb is not None:
        d["vmem_limit_bytes"] = vmem_mb << 20
    return pltpu.CompilerParams(**d)


def _gelu(x):
    z = 0.7978845608028654 * (x + 0.044715 * (x * x * x))
    return 0.5 * x * (1.0 + jnp.tanh(z))


def _gelu_with_grad(x):
    x2 = x * x
    t = jnp.tanh(0.7978845608028654 * (x + 0.044715 * (x2 * x)))
    half = 0.5 * (1.0 + t)
    return x * half, half + 0.5 * x * (1.0 - t * t) * (0.7978845608028654 * (1.0 + 3.0 * 0.044715 * x2))


def _matmul(a, b, *, name, ta=False, tb=False, M, N, K, tm=None, tn=None, tk=None, outs, epi=None,
            extras=(), a_spec=None, b_spec=None, out_specs=None, after=(), n_split=None, slots=None, into=()):
    n_after = len(after)
    tm = _pick(M, TM if tm is None else tm, 8)
    tn = _pick(N, TN if tn is None else tn)
    tk = _pick(K, TK if tk is None else tk)
    if n_split is None:
        n_split = tn // N_SUB if tn % N_SUB == 0 else 1
    nk = K // tk
    grid = (N // tn, M // tm, nk)
    if a_spec is None:
        a_spec = (pl.BlockSpec((tk, tm), lambda j, i, k, *s: (k, i)) if ta
                  else pl.BlockSpec((tm, tk), lambda j, i, k, *s: (i, k)))
    else:
        a_spec = a_spec(tm, tk)
    if b_spec is None:
        b_spec = (pl.BlockSpec((tn, tk), lambda j, i, k, *s: (j, k)) if tb
                  else pl.BlockSpec((tk, tn), lambda j, i, k, *s: (k, j)))
    else:
        b_spec = b_spec(tn, tk)
    if out_specs is None:
        out_specs = [pl.BlockSpec((tm, tn), lambda j, i, k, *s: (i, j)) for _ in outs]
    else:
        out_specs = out_specs(tm, tn)
    dn = (((0 if ta else 1,), (1 if tb else 0,)), ((), ()))
    n_ex, n_out = len(extras), len(outs)
    n_pre = 0 if slots is None else 1
    n_into = len(into)
    ns = tn // n_split

    def body(*refs):
        a_ref, b_ref = refs[n_pre], refs[n_pre + 1]
        ex = refs[n_pre + 2:n_pre + 2 + n_ex]
        first_out = n_pre + 2 + n_ex + n_after + n_into
        o = refs[first_out:first_out + n_out]
        acc = refs[first_out + n_out:]
        k = pl.program_id(2)

        def finish(val, cols):
            res = (val,) if epi is None else epi(val, *[e[:, cols] for e in ex])
            for r, o_ref in zip(res, o):
                o_ref[:, cols] = r.astype(o_ref.dtype)

        if nk > 1:
            @pl.when(k == 0)
            def _():
                acc[0][...] = jnp.zeros_like(acc[0])

        av = a_ref[...].astype(BF16)
        for q in range(n_split):
            cols = slice(q * ns, (q + 1) * ns)
            bq = (b_ref[cols, :] if tb else b_ref[:, cols]).astype(BF16)
            part = lax.dot_general(av, bq, dn, preferred_element_type=F32)
            if nk == 1:
                finish(part, cols)
            else:
                acc[0][:, cols] += part

        if nk > 1:
            @pl.when(k == nk - 1)
            def _():
                finish(acc[0][...], slice(0, tn))

    in_specs = ([a_spec, b_spec] + [sp(tm, tn) for _, sp in extras] + [ANY] * (n_after + n_into))
    scratch = [pltpu.VMEM((tm, tn), F32)] if nk > 1 else []
    args = [a, b] + [arr for arr, _ in extras] + list(after) + list(into)
    aliases = {n_pre + len(args) - n_into + t: t for t in range(n_into)}
    params = _cp(("parallel", "parallel", "arbitrary"), VMEM_MB)
    if slots is None:
        return pl.pallas_call(body, name=name, grid=grid, in_specs=in_specs, out_specs=out_specs, out_shape=outs,
                              scratch_shapes=scratch, input_output_aliases=aliases, compiler_params=params)(*args)
    return pl.pallas_call(
        body, name=name,
        grid_spec=pltpu.PrefetchScalarGridSpec(num_scalar_prefetch=1, grid=grid, in_specs=in_specs,
                                               out_specs=out_specs, scratch_shapes=scratch),
        out_shape=outs, input_output_aliases=aliases, compiler_params=params)(slots, *args)


def _tile_spec():
    return lambda tm, tn: pl.BlockSpec((tm, tn), lambda j, i, k, *s: (i, j))


def _cast_into_slot(w, slot, after, *, name):
    R, C = w.shape
    tr = _pick_rows(R, 16)

    def body(s_ref, w_ref, _after_ref, o_ref):
        o_ref[...] = w_ref[...].astype(BF16)

    return pl.pallas_call(
        body, name=name,
        grid_spec=pltpu.PrefetchScalarGridSpec(
            num_scalar_prefetch=1, grid=(R // tr,),
            in_specs=[pl.BlockSpec((tr, C), lambda i, s: (i, 0)), ANY],
            out_specs=pl.BlockSpec((None, tr, C), lambda i, s: (s[0], i, 0))),
        out_shape=jax.ShapeDtypeStruct((NCHIP, R, C), BF16),
        compiler_params=_cp(("parallel",), VMEM_MB),
    )(slot, w, after)


def _own_slot(part, slot, *, name):
    _, R, C = part.shape
    tr = _pick_rows(R, 16)

    def body(s_ref, p_ref, o_ref):
        o_ref[...] = p_ref[...]

    spec = pl.BlockSpec((None, tr, C), lambda i, s: (s[0], i, 0))
    return pl.pallas_call(
        body, name=name,
        grid_spec=pltpu.PrefetchScalarGridSpec(num_scalar_prefetch=1, grid=(R // tr,), in_specs=[spec],
                                               out_specs=spec),
        out_shape=jax.ShapeDtypeStruct(part.shape, part.dtype),
        compiler_params=_cp(("parallel",), VMEM_MB),
    )(slot, part)


def _rms_fwd(x, g, *, name, after=()):
    R, C = x.shape
    tr = _pick(R, TR_EW, 16)
    n_after = len(after)

    def body(x_ref, g_ref, *rest):
        o_ref = rest[n_after]
        xv = x_ref[...]
        r = lax.rsqrt(jnp.mean(xv * xv, axis=-1, keepdims=True) + EPS)
        o_ref[...] = ((xv * r) * g_ref[...]).astype(BF16)

    return pl.pallas_call(
        body, name=name, grid=(R // tr,),
        in_specs=[pl.BlockSpec((tr, C), lambda i: (i, 0)), pl.BlockSpec((1, C), lambda i: (0, 0))] + [ANY] * n_after,
        out_specs=pl.BlockSpec((tr, C), lambda i: (i, 0)),
        out_shape=jax.ShapeDtypeStruct((R, C), BF16),
        compiler_params=_cp(("parallel",), VMEM_MB),
    )(x, g, *after)


def _rms_bwd(dh, x, g, dres, *, name, want_dx=True, want_bf=True, after=()):
    R, C = x.shape
    tr = _pick(R, TR_EW, 16)
    has_res = dres is not None
    row = pl.BlockSpec((tr, C), lambda i: (i, 0))
    vec = pl.BlockSpec((1, C), lambda i: (0, 0))

    def body(*refs):
        dh_ref, x_ref, g_ref = refs[:3]
        pos = 3
        dres_ref = None
        if has_res:
            dres_ref = refs[pos]
            pos += 1
        outs = refs[pos + len(after):]
        i = pl.program_id(0)
        xv = x_ref[...]
        r = lax.rsqrt(jnp.mean(xv * xv, axis=-1, keepdims=True) + EPS)
        xh = xv * r
        dhv = dh_ref[...]
        dg_ref = outs[-1]
        dgp = jnp.sum(dhv * xh, axis=0, keepdims=True)

        @pl.when(i == 0)
        def _():
            dg_ref[...] = dgp

        @pl.when(i > 0)
        def _():
            dg_ref[...] += dgp

        if want_dx:
            t = dhv * g_ref[...]
            dx = r * (t - xh * jnp.mean(t * xh, axis=-1, keepdims=True))
            if has_res:
                dx = dx + dres_ref[...]
            outs[0][...] = dx
            if want_bf:
                outs[1][...] = dx.astype(BF16)

    in_specs = [row, row, vec] + ([row] if has_res else []) + [ANY] * len(after)
    out_specs, out_shape = [], []
    if want_dx:
        out_specs.append(row)
        out_shape.append(jax.ShapeDtypeStruct((R, C), F32))
        if want_bf:
            out_specs.append(row)
            out_shape.append(jax.ShapeDtypeStruct((R, C), BF16))
    out_specs.append(vec)
    out_shape.append(jax.ShapeDtypeStruct((1, C), F32))
    args = [dh, x, g] + ([dres] if has_res else []) + list(after)
    return pl.pallas_call(
        body, name=name, grid=(R // tr,), in_specs=in_specs, out_specs=out_specs, out_shape=out_shape,
        compiler_params=_cp(("arbitrary",), VMEM_MB),
    )(*args)


def _loss_bwd(x3, g, tgt, *, name):
    R, C = x3.shape
    tr = _pick(R, TR_EW, 16)
    n = R // tr
    row = pl.BlockSpec((tr, C), lambda i: (i, 0))
    vec = pl.BlockSpec((1, C), lambda i: (0, 0))

    def body(x_ref, g_ref, t_ref, dx_ref, dxb_ref, dg_ref, loss_ref, acc_ref):
        i = pl.program_id(0)
        xv = x_ref[...]
        gv = g_ref[...]
        r = lax.rsqrt(jnp.mean(xv * xv, axis=-1, keepdims=True) + EPS)
        xh = xv * r
        e = xh * gv - t_ref[...]
        dy = e * (1.0 / C)
        sq = jnp.sum(e * e, axis=0, keepdims=True)
        dgp = jnp.sum(dy * xh, axis=0, keepdims=True)

        @pl.when(i == 0)
        def _():
            acc_ref[...] = sq
            dg_ref[...] = dgp

        @pl.when(i > 0)
        def _():
            acc_ref[...] += sq
            dg_ref[...] += dgp

        t = dy * gv
        dx = r * (t - xh * jnp.mean(t * xh, axis=-1, keepdims=True))
        dx_ref[...] = dx
        dxb_ref[...] = dx.astype(BF16)

        @pl.when(i == n - 1)
        def _():
            loss_ref[...] = jnp.sum(acc_ref[...], axis=-1, keepdims=True) * (0.5 / C)

    return pl.pallas_call(
        body, name=name, grid=(n,),
        in_specs=[row, vec, row],
        out_specs=[row, row, vec, pl.BlockSpec((1, 1), lambda i: (0, 0))],
        out_shape=[jax.ShapeDtypeStruct((R, C), F32), jax.ShapeDtypeStruct((R, C), BF16),
                   jax.ShapeDtypeStruct((1, C), F32), jax.ShapeDtypeStruct((1, 1), F32)],
        scratch_shapes=[pltpu.VMEM((1, C), F32)],
        compiler_params=_cp(("arbitrary",), VMEM_MB),
    )(x3, g, tgt)


def _offsets():
    u0 = 0
    v0 = DS
    b0 = 2 * DS
    c0 = b0 + DC
    x0 = c0 + DC
    q0 = x0 + DC
    return u0, v0, b0, c0, x0, q0


def _tri_mask(lower):
    r = lax.broadcasted_iota(jnp.int32, (CHUNK, CHUNK), 0)
    c = lax.broadcasted_iota(jnp.int32, (CHUNK, CHUNK), 1)
    return (r >= c) if lower else (c >= r)


def _layer_norm_stats(vg):
    mu = jnp.mean(vg, axis=-1, keepdims=True)
    vc = vg - mu
    rstd = lax.rsqrt(jnp.mean(vc * vc, axis=-1, keepdims=True) + EPS)
    return vc * rstd, rstd


def _softmax_rows(qh, kh):
    s = lax.dot_general(qh, kh, (((1,), (1,)), ((), ())), preferred_element_type=F32)
    m = jnp.max(s, axis=-1, keepdims=True)
    e = jnp.exp(s - m)
    return e / jnp.sum(e, axis=-1, keepdims=True)


def _mix_fwd(proj, kv, w_s, bs_t, ln_g, ln_b, conv_w, g_head, *, name):
    assert DS == DC
    tr = _pick(S, TR_MIX, CHUNK)
    n = S // tr
    nck = tr // CHUNK
    u0, v0, b0, c0, x0, q0 = _offsets()
    hb = tr // HALO

    def body(p_ref, cprev_ref, xprev_ref, kv_ref, ws_ref, bst_ref, lng_ref, lnb_ref, cw_ref, gh_ref,
             heads_ref, hn_ref, ycv_ref, buf_ref):
        i = pl.program_id(0)

        def emit(col, val):
            rs = lax.rsqrt(jnp.mean(val * val, axis=-1, keepdims=True) + EPS)
            heads_ref[:, col:col + HD] = val
            hn_ref[:, col:col + HD] = ((val * rs) * gh_ref[:, col:col + HD]).astype(BF16)

        vhat, _ = _layer_norm_stats(_gelu(p_ref[:, v0:v0 + DS]))
        vnb = (vhat * lng_ref[...] + lnb_ref[...]).astype(BF16)
        low = _tri_mask(True)
        for h in range(NSH):
            wt = jnp.where(low, ws_ref[h], 0.0).astype(BF16)
            bcol = bst_ref[:, h:h + 1]
            parts = []
            for c in range(nck):
                blk = vnb[c * CHUNK:(c + 1) * CHUNK, h * HD:(h + 1) * HD]
                parts.append(jnp.dot(wt, blk, preferred_element_type=F32) + bcol)
            mixed = parts[0] if nck == 1 else jnp.concatenate(parts, axis=0)
            emit(h * HD, _gelu(p_ref[:, u0 + h * HD:u0 + (h + 1) * HD]) * mixed)

        xc = p_ref[:, c0:c0 + DC] * p_ref[:, x0:x0 + DC]
        prev = cprev_ref[...] * xprev_ref[...]
        buf_ref[0:HALO, :] = jnp.where(i > 0, prev, 0.0)
        buf_ref[HALO:HALO + tr, :] = xc
        y = (cw_ref[2:3, :] * xc + cw_ref[1:2, :] * buf_ref[HALO - 1:HALO - 1 + tr, :]
             + cw_ref[0:1, :] * buf_ref[HALO - 2:HALO - 2 + tr, :])
        ycv_ref[...] = y
        cout = p_ref[:, b0:b0 + DC] * y
        for h in range(NCH):
            emit(DS + h * HD, cout[:, h * HD:(h + 1) * HD])

        for h in range(NMH):
            qh = (p_ref[:, q0 + h * HD:q0 + (h + 1) * HD] * SCALE).astype(BF16)
            kh = kv_ref[:, h * HD:(h + 1) * HD].astype(BF16)
            vh = kv_ref[:, DM + h * HD:DM + (h + 1) * HD].astype(BF16)
            p = _softmax_rows(qh, kh)
            emit(DS + DC + h * HD, jnp.dot(p.astype(BF16), vh, preferred_element_type=F32))

    full = lambda shape: pl.BlockSpec(shape, lambda i: (0,) * len(shape))
    halo_c = pl.BlockSpec((HALO, DC), lambda i: (jnp.maximum(i * hb - 1, 0), c0 // DC))
    halo_x = pl.BlockSpec((HALO, DC), lambda i: (jnp.maximum(i * hb - 1, 0), x0 // DC))
    return pl.pallas_call(
        body, name=name, grid=(n,),
        in_specs=[pl.BlockSpec((tr, DIN), lambda i: (i, 0)), halo_c, halo_x,
                  full((NMEM, 2 * DM)), full((NSH, CHUNK, CHUNK)), full((CHUNK, NSH)),
                  full((1, DS)), full((1, DS)), full((3, DC)), full((1, D))],
        out_specs=[pl.BlockSpec((tr, D), lambda i: (i, 0)), pl.BlockSpec((tr, D), lambda i: (i, 0)),
                   pl.BlockSpec((tr, DC), lambda i: (i, 0))],
        out_shape=[jax.ShapeDtypeStruct((S, D), F32), jax.ShapeDtypeStruct((S, D), BF16),
                   jax.ShapeDtypeStruct((S, DC), F32)],
        scratch_shapes=[pltpu.VMEM((tr + HALO, DC), F32)],
        compiler_params=_cp(("parallel",), VMEM_MB),
    )(proj, proj, proj, kv, w_s, bs_t, ln_g, ln_b, conv_w, g_head)


def _mix_bwd(dhn, heads, proj, ycv, kv, w_s, bs_t, ln_g, ln_b, conv_w, g_head, after, *, name):
    assert DS == DC
    tr = _pick(S, TR_MIX, CHUNK)
    n = S // tr
    nck = tr // CHUNK
    u0, v0, b0, c0, x0, q0 = _offsets()
    hb = tr // HALO
    last_hb = S // HALO - 1

    def body(dhn_ref, heads_ref, p_ref, ycv_ref, dhn_nx_ref, heads_nx_ref, b_nx_ref, kv_ref, ws_ref, bst_ref,
             lng_ref, lnb_ref, cw_ref, gh_ref, _after_ref,
             dp_ref, dkv_ref, dws_ref, dbs_ref, dlng_ref, dlnb_ref, dcw_ref, dgh_ref, buf_ref, dvn_ref):
        i = pl.program_id(0)

        @pl.when(i == 0)
        def _():
            dkv_ref[...] = jnp.zeros_like(dkv_ref)
            dws_ref[...] = jnp.zeros_like(dws_ref)
            dbs_ref[...] = jnp.zeros_like(dbs_ref)
            dlng_ref[...] = jnp.zeros_like(dlng_ref)
            dlnb_ref[...] = jnp.zeros_like(dlnb_ref)
            dcw_ref[...] = jnp.zeros_like(dcw_ref)
            dgh_ref[...] = jnp.zeros_like(dgh_ref)

        def head_bwd(a, dn, gh):
            rs = lax.rsqrt(jnp.mean(a * a, axis=-1, keepdims=True) + EPS)
            ah = a * rs
            t = dn * gh
            return rs * (t - ah * jnp.mean(t * ah, axis=-1, keepdims=True)), jnp.sum(dn * ah, axis=0, keepdims=True)

        def head_grad(col):
            da, dg = head_bwd(heads_ref[:, col:col + HD], dhn_ref[:, col:col + HD], gh_ref[:, col:col + HD])
            dgh_ref[:, col:col + HD] += dg
            return da

        vg, dvg_dv = _gelu_with_grad(p_ref[:, v0:v0 + DS])
        vhat, rstd = _layer_norm_stats(vg)
        vnb = (vhat * lng_ref[...] + lnb_ref[...]).astype(BF16)
        low = _tri_mask(True)
        ones = jnp.ones((HALO, HD), BF16)
        for h in range(NSH):
            w_h = ws_ref[h]
            wt = jnp.where(low, w_h, 0.0).astype(BF16)
            bcol = bst_ref[:, h:h + 1]
            da = head_grad(h * HD)
            ug, dug_du = _gelu_with_grad(p_ref[:, u0 + h * HD:u0 + (h + 1) * HD])
            dws = jnp.zeros((CHUNK, CHUNK), F32)
            dbs = jnp.zeros((HALO, CHUNK), F32)
            mixed_parts = []
            for c in range(nck):
                rows = slice(c * CHUNK, (c + 1) * CHUNK)
                blk = vnb[rows, h * HD:(h + 1) * HD]
                mixed_parts.append(jnp.dot(wt, blk, preferred_element_type=F32) + bcol)
                dmb = (da[rows] * ug[rows]).astype(BF16)
                dws = dws + lax.dot_general(dmb, blk, (((1,), (1,)), ((), ())), preferred_element_type=F32)
                dbs = dbs + lax.dot_general(ones, dmb, (((1,), (1,)), ((), ())), preferred_element_type=F32)
                dvn_ref[c * CHUNK:(c + 1) * CHUNK, h * HD:(h + 1) * HD] = lax.dot_general(
                    wt, dmb, (((0,), (0,)), ((), ())), preferred_element_type=F32)
            mixed = mixed_parts[0] if nck == 1 else jnp.concatenate(mixed_parts, axis=0)
            dp_ref[:, u0 + h * HD:u0 + (h + 1) * HD] = ((da * mixed) * dug_du).astype(BF16)
            dws_ref[h] += jnp.where(low, dws, 0.0)
            dbs_ref[h] += dbs
        dvn = dvn_ref[...]
        dlng_ref[...] += jnp.sum(dvn * vhat, axis=0, keepdims=True)
        dlnb_ref[...] += jnp.sum(dvn, axis=0, keepdims=True)
        dvh = dvn * lng_ref[...]
        dvg = rstd * (dvh - jnp.mean(dvh, axis=-1, keepdims=True)
                      - vhat * jnp.mean(dvh * vhat, axis=-1, keepdims=True))
        dp_ref[:, v0:v0 + DS] = (dvg * dvg_dv).astype(BF16)

        dc = jnp.concatenate([head_grad(DS + h * HD) for h in range(NCH)], axis=1)
        dc_nx = jnp.concatenate(
            [head_bwd(heads_nx_ref[:, h * HD:(h + 1) * HD], dhn_nx_ref[:, h * HD:(h + 1) * HD],
                      gh_ref[:, DS + h * HD:DS + (h + 1) * HD])[0] for h in range(NCH)], axis=1)
        bg = p_ref[:, b0:b0 + DC]
        cg = p_ref[:, c0:c0 + DC]
        xin = p_ref[:, x0:x0 + DC]
        dp_ref[:, b0:b0 + DC] = (dc * ycv_ref[...]).astype(BF16)
        dyv = dc * bg
        buf_ref[0:tr, :] = dyv
        buf_ref[tr:tr + HALO, :] = jnp.where(i < n - 1, dc_nx * b_nx_ref[...], 0.0)
        sh1 = buf_ref[1:1 + tr, :]
        sh0 = buf_ref[2:2 + tr, :]
        dxc = cw_ref[2:3, :] * dyv + cw_ref[1:2, :] * sh1 + cw_ref[0:1, :] * sh0
        xc = cg * xin
        dp_ref[:, c0:c0 + DC] = (dxc * xin).astype(BF16)
        dp_ref[:, x0:x0 + DC] = (dxc * cg).astype(BF16)
        dcw_ref[0:1, :] += jnp.sum(sh0 * xc, axis=0, keepdims=True)
        dcw_ref[1:2, :] += jnp.sum(sh1 * xc, axis=0, keepdims=True)
        dcw_ref[2:3, :] += jnp.sum(dyv * xc, axis=0, keepdims=True)

        for h in range(NMH):
            do = head_grad(DS + DC + h * HD).astype(BF16)
            qh = (p_ref[:, q0 + h * HD:q0 + (h + 1) * HD] * SCALE).astype(BF16)
            kh = kv_ref[:, h * HD:(h + 1) * HD].astype(BF16)
            vh = kv_ref[:, DM + h * HD:DM + (h + 1) * HD].astype(BF16)
            p = _softmax_rows(qh, kh)
            dpr = lax.dot_general(do, vh, (((1,), (1,)), ((), ())), preferred_element_type=F32)
            ds = (p * (dpr - jnp.sum(dpr * p, axis=-1, keepdims=True))).astype(BF16)
            dp_ref[:, q0 + h * HD:q0 + (h + 1) * HD] = (
                jnp.dot(ds, kh, preferred_element_type=F32) * SCALE).astype(BF16)
            dkv_ref[:, h * HD:(h + 1) * HD] += lax.dot_general(
                ds, qh, (((0,), (0,)), ((), ())), preferred_element_type=F32)
            dkv_ref[:, DM + h * HD:DM + (h + 1) * HD] += lax.dot_general(
                p.astype(BF16), do, (((0,), (0,)), ((), ())), preferred_element_type=F32)

    full = lambda shape: pl.BlockSpec(shape, lambda i: (0,) * len(shape))
    row = lambda c: pl.BlockSpec((tr, c), lambda i: (i, 0))
    nxt = lambda col: pl.BlockSpec((HALO, DC), lambda i: (jnp.minimum((i + 1) * hb, last_hb), col))
    return pl.pallas_call(
        body, name=name, grid=(n,),
        in_specs=[row(D), row(D), row(DIN), row(DC), nxt(DS // DC), nxt(DS // DC), nxt(b0 // DC),
                  full((NMEM, 2 * DM)), full((NSH, CHUNK, CHUNK)), full((CHUNK, NSH)),
                  full((1, DS)), full((1, DS)), full((3, DC)), full((1, D)), ANY],
        out_specs=[row(DIN), full((NMEM, 2 * DM)), full((NSH, CHUNK, CHUNK)), full((NSH, HALO, CHUNK)),
                   full((1, DS)), full((1, DS)), full((HALO, DC)), full((1, D))],
        out_shape=[jax.ShapeDtypeStruct((S, DIN), BF16), jax.ShapeDtypeStruct((NMEM, 2 * DM), F32),
                   jax.ShapeDtypeStruct((NSH, CHUNK, CHUNK), F32), jax.ShapeDtypeStruct((NSH, HALO, CHUNK), F32),
                   jax.ShapeDtypeStruct((1, DS), F32), jax.ShapeDtypeStruct((1, DS), F32),
                   jax.ShapeDtypeStruct((HALO, DC), F32), jax.ShapeDtypeStruct((1, D), F32)],
        scratch_shapes=[pltpu.VMEM((tr + HALO, DC), F32), pltpu.VMEM((tr, DS), F32)],
        compiler_params=_cp(("arbitrary",), VMEM_MB),
    )(dhn, heads, proj, ycv, dhn, heads, proj, kv, w_s, bs_t, ln_g, ln_b, conv_w, g_head, after)


def _place():
    x, y, c = lax.axis_index("x"), lax.axis_index("y"), lax.axis_index("c")
    chips = [(1 - x, y), (x, 1 - y), (1 - x, 1 - y)]
    return x, y, c, chips


ANY = pl.BlockSpec(memory_space=pl.ANY)


HBM = pl.BlockSpec(memory_space=pltpu.HBM)
SEM = pl.BlockSpec(memory_space=pltpu.SEMAPHORE)
EFFECT = pltpu.SideEffectType.DATAFLOW_SIDE_EFFECTING
N_PEER_CHIPS = 3
N_NEIGHBOUR_CHIPS = 2
CONV_PAD = (32, 256)


def _in_hbm(a):
    return pltpu.with_memory_space_constraint(a, pltpu.HBM)


def _allgather_start(bufs, forwards, after, collective_id, *, name):
    arrs = list(bufs) + list(forwards)
    nw, nb = len(arrs), len(bufs)

    def body(*refs):
        ins, send, recv = refs[:nw], refs[nw + 1:2 * nw + 1], refs[2 * nw + 1:3 * nw + 1]
        token = refs[4 * nw + 1]
        x, y, c, chips = _place()
        s = 2 * x + y
        slots = [2 * cx + cy for cx, cy in chips]
        _handshake([(cx, cy, c) for cx, cy in chips[:N_NEIGHBOUR_CHIPS]])
        for w in range(nb, nw):
            q = arrs[w].shape[1] // 4
            for j in range(N_NEIGHBOUR_CHIPS):
                rows = ins[w].at[slots[j], pl.ds(c * 2 * q + j * q, q)]
                pltpu.make_async_remote_copy(src_ref=rows, dst_ref=rows, send_sem=send[w], recv_sem=recv[w],
                                             device_id=(*chips[1 - j], c), device_id_type=MESH).start()
        for w in range(nb):
            hr = arrs[w].shape[1] // 2
            rows = ins[w].at[s, pl.ds(c * hr, hr)]
            for cx, cy in chips[:N_NEIGHBOUR_CHIPS]:
                pltpu.make_async_remote_copy(src_ref=rows, dst_ref=rows, send_sem=send[w], recv_sem=recv[w],
                                             device_id=(cx, cy, c), device_id_type=MESH).start()
        token[...] = jnp.zeros_like(token)

    res = pl.pallas_call(
        body, name=name,
        in_specs=[HBM] * nw + [ANY],
        out_specs=[SEM] * (2 * nw) + [HBM] * nw + [pl.BlockSpec(memory_space=pltpu.VMEM)],
        out_shape=[pltpu.SemaphoreType.DMA(())] * (2 * nw) + [pltpu.HBM(a.shape, a.dtype) for a in arrs]
        + [jax.ShapeDtypeStruct((8, 128), F32)],
        input_output_aliases={w: 2 * nw + w for w in range(nw)},
        compiler_params=pltpu.CompilerParams(has_side_effects=EFFECT, collective_id=collective_id),
    )(*[_in_hbm(a) for a in arrs], after)
    return res[:nw], res[nw:2 * nw], res[2 * nw:3 * nw], res[3 * nw]


def _handshake(peers):
    barrier = pltpu.get_barrier_semaphore()
    for peer in peers:
        pl.semaphore_signal(barrier, inc=1, device_id=peer, device_id_type=MESH)
    pl.semaphore_wait(barrier, len(peers))


def _scatter_start(parts, bufs, collective_id, *, name):
    nw = len(parts)

    def body(*refs):
        src, dst = refs[:nw], refs[nw:2 * nw]
        send, recv = refs[2 * nw:3 * nw], refs[3 * nw:4 * nw]
        token = refs[6 * nw]
        x, y, c, chips = _place()
        s = 2 * x + y
        _handshake([(cx, cy, c) for cx, cy in chips])
        for w in range(nw):
            for cx, cy in chips:
                pltpu.make_async_remote_copy(src_ref=src[w].at[2 * cx + cy], dst_ref=dst[w].at[s], send_sem=send[w],
                                             recv_sem=recv[w], device_id=(cx, cy, c), device_id_type=MESH).start()
        token[...] = jnp.zeros_like(token)

    res = pl.pallas_call(
        body, name=name,
        in_specs=[HBM] * (2 * nw),
        out_specs=[SEM] * (2 * nw) + [HBM] * (2 * nw) + [pl.BlockSpec(memory_space=pltpu.VMEM)],
        out_shape=[pltpu.SemaphoreType.DMA(())] * (2 * nw) + [pltpu.HBM(a.shape, a.dtype) for a in parts + bufs]
        + [jax.ShapeDtypeStruct((8, 128), F32)],
        input_output_aliases={k: 2 * nw + k for k in range(2 * nw)},
        compiler_params=pltpu.CompilerParams(has_side_effects=EFFECT, collective_id=collective_id),
    )(*[_in_hbm(a) for a in parts + bufs])
    return res[:nw], res[nw:2 * nw], res[2 * nw:3 * nw], res[3 * nw:4 * nw], res[4 * nw]


def _sibling_start(srcs, whole, collective_id, *, name):
    nw = len(srcs)
    lands = [lax.empty((a.shape[0], a.shape[1] if whole else a.shape[1] // 2, a.shape[2]), a.dtype) for a in srcs]

    def body(*refs):
        src, land = refs[:nw], refs[nw:2 * nw]
        send, recv = refs[2 * nw:3 * nw], refs[3 * nw:4 * nw]
        token = refs[6 * nw]
        x, y, c, _ = _place()
        _handshake([(x, y, 1 - c)])
        for w in range(nw):
            hr = srcs[w].shape[1] // 2
            rows = src[w] if whole else src[w].at[:, pl.ds((1 - c) * hr, hr)]
            pltpu.make_async_remote_copy(src_ref=rows, dst_ref=land[w], send_sem=send[w], recv_sem=recv[w],
                                         device_id=(x, y, 1 - c), device_id_type=MESH).start()
        token[...] = jnp.zeros_like(token)

    res = pl.pallas_call(
        body, name=name,
        in_specs=[HBM] * (2 * nw),
        out_specs=[SEM] * (2 * nw) + [HBM] * (2 * nw) + [pl.BlockSpec(memory_space=pltpu.VMEM)],
        out_shape=[pltpu.SemaphoreType.DMA(())] * (2 * nw) + [pltpu.HBM(a.shape, a.dtype) for a in srcs + lands]
        + [jax.ShapeDtypeStruct((8, 128), F32)],
        input_output_aliases={k: 2 * nw + k for k in range(2 * nw)},
        compiler_params=pltpu.CompilerParams(has_side_effects=EFFECT, collective_id=collective_id),
    )(*[_in_hbm(a) for a in srcs + lands])
    return res[:nw], res[nw:2 * nw], res[2 * nw:3 * nw], res[3 * nw:4 * nw], res[4 * nw]


def _transfer_wait(sends, recvs, thru, sizes, after, *, name):
    n = len(sends)
    flat = [a for group in thru for a in group]

    def body(*refs):
        bufs = refs[:len(flat)]
        send = refs[len(flat):len(flat) + n]
        recv = refs[len(flat) + n:len(flat) + 2 * n]
        token = refs[2 * len(flat) + 2 * n + 1]
        token[...] = jnp.zeros_like(token)
        x, y, c, _ = _place()
        pos = 0
        for k in range(n):
            slots, rows = sizes[k]
            region = bufs[pos].at[pl.ds(0, slots), pl.ds(0, rows)]
            pos += len(thru[k])
            cp = pltpu.make_async_remote_copy(src_ref=region, dst_ref=region, send_sem=send[k], recv_sem=recv[k],
                                              device_id=(x, y, 1 - c), device_id_type=MESH)
            cp.wait_send()
            cp.wait_recv()

    res = pl.pallas_call(
        body, name=name,
        in_specs=[HBM] * len(flat) + [SEM] * (2 * n) + [pl.BlockSpec(memory_space=pl.ANY)],
        out_specs=[HBM] * len(flat) + [pl.BlockSpec(memory_space=pltpu.VMEM)],
        out_shape=[pltpu.HBM(a.shape, a.dtype) for a in flat] + [jax.ShapeDtypeStruct((8, 128), F32)],
        input_output_aliases={k: k for k in range(len(flat))},
        compiler_params=pltpu.CompilerParams(has_side_effects=EFFECT),
    )(*flat, *sends, *recvs, after)
    out, pos = [], 0
    for group in thru:
        out.append(res[pos:pos + len(group)])
        pos += len(group)
    return out, res[len(flat)]


def _forward_halves(bufs, which, after, *, name):
    nw = len(bufs)
    n = len(which)

    def body(*refs):
        outs = refs[nw + 1:2 * nw + 1]
        send, recv = refs[2 * nw + 1:]
        x, y, c, chips = _place()
        me, sibling = (x, y, c), (x, y, 1 - c)

        def d2d(w, t, half, to):
            cx, cy = chips[which[t]]
            hr = bufs[w].shape[1] // 2
            rows = outs[w].at[2 * cx + cy, pl.ds(half * hr, hr)]
            return pltpu.make_async_remote_copy(src_ref=rows, dst_ref=rows, send_sem=send.at[n * w + t],
                                                recv_sem=recv.at[n * w + t], device_id=to, device_id_type=MESH)

        passed = [d2d(w, t, c, sibling) for w in range(nw) for t in range(n)]
        for cp in passed:
            cp.start()
        for w in range(nw):
            for t in range(n):
                d2d(w, t, 1 - c, me).wait_recv()
        for cp in passed:
            cp.wait_send()

    return pl.pallas_call(
        body, name=name,
        in_specs=[ANY] * (nw + 1), out_specs=[ANY] * nw,
        out_shape=[jax.ShapeDtypeStruct(a.shape, a.dtype) for a in bufs],
        input_output_aliases={w: w for w in range(nw)},
        scratch_shapes=[pltpu.SemaphoreType.DMA((n * nw,)), pltpu.SemaphoreType.DMA((n * nw,))],
    )(*bufs, after)


def _allreduce_small(p, after, *, name):
    R = p.shape[0]
    hr = R // 2

    def body(p_ref, _after_ref, out_ref, sib_ref, sum_ref, gat_ref, tot_ref, send, recv):
        x, y, c, chips = _place()
        s = 2 * x + y
        sibling = (x, y, 1 - c)
        rows = pl.ds(pl.multiple_of(c * hr, 8), hr)
        swap = pltpu.make_async_remote_copy(src_ref=p_ref, dst_ref=sib_ref, send_sem=send.at[0], recv_sem=recv.at[0],
                                            device_id=sibling, device_id_type=MESH)
        swap.start()
        swap.wait()
        sum_ref[...] = p_ref[...] + sib_ref[...]
        gat_ref[s] = sum_ref[rows, :]
        cps = [pltpu.make_async_remote_copy(src_ref=sum_ref.at[rows], dst_ref=gat_ref.at[s], send_sem=send.at[1 + j],
                                            recv_sem=recv.at[1 + j], device_id=(cx, cy, c), device_id_type=MESH)
               for j, (cx, cy) in enumerate(chips)]
        for cp in cps:
            cp.start()
        for cp in cps:
            cp.wait()
        tot_ref[...] = ((gat_ref[0] + gat_ref[1]) + gat_ref[2]) + gat_ref[3]
        out_ref[rows, :] = tot_ref[...]
        share = pltpu.make_async_remote_copy(src_ref=tot_ref, dst_ref=out_ref.at[rows], send_sem=send.at[4],
                                             recv_sem=recv.at[4], device_id=sibling, device_id_type=MESH)
        share.start()
        share.wait_send()
        other = out_ref.at[pl.ds(pl.multiple_of((1 - c) * hr, 8), hr)]
        pltpu.make_async_remote_copy(src_ref=other, dst_ref=other, send_sem=send.at[4], recv_sem=recv.at[4],
                                     device_id=(x, y, c), device_id_type=MESH).wait_recv()

    vmem = pl.BlockSpec(memory_space=pltpu.VMEM)
    return pl.pallas_call(
        body, name=name, in_specs=[vmem, ANY], out_specs=vmem,
        out_shape=jax.ShapeDtypeStruct((R, 128), F32),
        scratch_shapes=[pltpu.VMEM((R, 128), F32), pltpu.VMEM((R, 128), F32), pltpu.VMEM((NCHIP, hr, 128), F32),
                        pltpu.VMEM((hr, 128), F32), pltpu.SemaphoreType.DMA((5,)), pltpu.SemaphoreType.DMA((5,))],
    )(p, after)


def _select_half_bf16(g, half, add, slot, *, name):
    _, R, C = g.shape
    hr = R // 2
    tr = _pick_rows(hr, 16)
    nb = hr // tr
    sel = jnp.concatenate([jnp.reshape(half, (1,)).astype(jnp.int32), slot])

    def body(s_ref, g_ref, a_ref, o_ref, own_ref):
        val = (g_ref[...].astype(F32) + a_ref[...].astype(F32)).astype(BF16)
        o_ref[...] = val

        @pl.when(pl.program_id(1) == s_ref[1])
        def _():
            own_ref[...] = val

    g_spec = pl.BlockSpec((None, tr, C), lambda i, j, s: (j, s[0] * nb + i, 0))
    o_spec = pl.BlockSpec((None, tr, C), lambda i, j, s: (j, i, 0))
    own_spec = pl.BlockSpec((None, tr, C), lambda i, j, s: (s[1], i, 0))
    shape = jax.ShapeDtypeStruct((NCHIP, hr, C), BF16)
    return pl.pallas_call(
        body, name=name,
        grid_spec=pltpu.PrefetchScalarGridSpec(
            num_scalar_prefetch=1, grid=(nb, NCHIP), in_specs=[g_spec, o_spec], out_specs=[o_spec, own_spec]),
        out_shape=[shape, shape],
        compiler_params=_cp(("parallel", "arbitrary"), VMEM_MB),
    )(sel, g, add)


def _adamw_math(w, g, m, v):
    m = ADAM_B1 * m + (1.0 - ADAM_B1) * g
    v = ADAM_B2 * v + (1.0 - ADAM_B2) * (g * g)
    m_hat = m / (1.0 - ADAM_B1 ** ADAM_STEP)
    v_hat = v / (1.0 - ADAM_B2 ** ADAM_STEP)
    delta = -ADAM_LR * (m_hat / (jnp.sqrt(v_hat) + ADAM_EPS) + ADAM_WD * w)
    return delta, m, v


def _adamw(w, g_mine, g_sib, m, v, core, *, name):
    R, C = w.shape
    hr = R // 2
    tr = _pick_rows(hr, 16)
    nb = hr // tr
    row = pl.BlockSpec((tr, C), lambda hh, i, c: (hh * nb + i, 0))
    mine = pl.BlockSpec((NCHIP, tr, C), lambda hh, i, c: (0, jnp.where(hh == c[0], i, 0), 0))
    sibs = pl.BlockSpec((NCHIP, tr, C), lambda hh, i, c: (0, jnp.where(hh == c[0], 0, i), 0))

    def slot_sum(ref):
        acc = ref[0].astype(F32) + ref[1].astype(F32)
        for j in range(2, NCHIP):
            acc = acc + ref[j].astype(F32)
        return acc

    def body(c_ref, w_ref, gm_ref, gs_ref, m_ref, v_ref, go_ref, d_ref, mo_ref, vo_ref):
        gv = jnp.where(pl.program_id(0) == c_ref[0], slot_sum(gm_ref), slot_sum(gs_ref))
        d, mn, vn = _adamw_math(w_ref[...], gv, m_ref[...], v_ref[...])
        go_ref[...] = gv
        d_ref[...] = d
        mo_ref[...] = mn
        vo_ref[...] = vn

    return pl.pallas_call(
        body, name=name,
        grid_spec=pltpu.PrefetchScalarGridSpec(
            num_scalar_prefetch=1, grid=(2, nb),
            in_specs=[row, mine, sibs, row, row], out_specs=[row] * 4),
        out_shape=[jax.ShapeDtypeStruct((R, C), F32)] * 4,
        compiler_params=_cp(("parallel", "parallel"), VMEM_MB),
    )(core, w, g_mine, g_sib, m, v)


def _adamw_small(ws, gs, ms, vs, *, name):
    n = len(ws)

    def body(*refs):
        w_r, g_r, m_r, v_r = refs[:n], refs[n:2 * n], refs[2 * n:3 * n], refs[3 * n:4 * n]
        d_r, mo_r, vo_r = refs[4 * n:5 * n], refs[5 * n:6 * n], refs[6 * n:7 * n]
        for k in range(n):
            d, mn, vn = _adamw_math(w_r[k][...], g_r[k][...], m_r[k][...], v_r[k][...])
            d_r[k][...] = d
            mo_r[k][...] = mn
            vo_r[k][...] = vn

    shapes = [jax.ShapeDtypeStruct(w.shape, F32) for w in ws]
    res = pl.pallas_call(body, name=name, out_shape=shapes * 3)(*ws, *gs, *ms, *vs)
    return res[:n], res[n:2 * n], res[2 * n:]


_PACK_ROWS = 8


def _pack(parts):
    rows = []
    for a in parts:
        flat = a.reshape(-1)
        n = -(-flat.shape[0] // (_PACK_ROWS * 128)) * (_PACK_ROWS * 128)
        rows.append(jnp.pad(flat, (0, n - flat.shape[0])).reshape(-1, 128))
    total = sum(r.shape[0] for r in rows)
    if total % 16:
        rows.append(jnp.zeros((16 - total % 16, 128), F32))
    return jnp.concatenate(rows, axis=0)


def _unpack(p, shapes):
    out, r = [], 0
    for shp in shapes:
        n = math.prod(shp)
        nr = -(-n // (_PACK_ROWS * 128)) * _PACK_ROWS
        out.append(p[r:r + nr].reshape(-1)[:n].reshape(shp))
        r += nr
    return out


def kernel(x, mem, g_mix, w_in, ln_v_g, ln_v_b, w_s, b_s, conv_w, g_mem, w_kv, g_head, w_o, g_ffn, w_ffn1, w_ffn2, g_final, loss_target, m_g_mix, m_w_in, m_ln_v_g, m_ln_v_b, m_w_s, m_b_s, m_conv_w, m_g_mem, m_w_kv, m_g_head, m_w_o, m_g_ffn, m_w_ffn1, m_w_ffn2, m_g_final, v_g_mix, v_w_in, v_ln_v_g, v_ln_v_b, v_w_s, v_b_s, v_conv_w, v_g_mem, v_w_kv, v_g_head, v_w_o, v_g_ffn, v_w_ffn1, v_w_ffn2, v_g_final):
    sds = jax.ShapeDtypeStruct
    xi, yi = lax.axis_index("x"), lax.axis_index("y")
    shard = 2 * xi + yi
    x2d, mem2d, tgt = x[0], mem[0], loss_target[0]
    ws3, bs2 = w_s[0], b_s[0]
    g_final2 = g_final.reshape(1, D)
    dff4 = DFF // NCHIP
    din4 = DIN // NCHIP
    dcv4 = DC // NCHIP

    big = [w_in[0].T, w_kv[0], w_o[0], w_ffn1[0], w_ffn2[0]]
    big_names = ["w_in", "w_kv", "w_o", "w_ffn1", "w_ffn2"]
    slot = jnp.reshape(shard, (1,)).astype(jnp.int32)
    core = jnp.reshape(lax.axis_index("c"), (1,)).astype(jnp.int32)
    conv_pad = jnp.pad(conv_w[0], ((0, CONV_PAD[0] - 3), (0, CONV_PAD[1] - dcv4)))
    conv_slots = lax.dynamic_update_slice(jnp.zeros((NCHIP,) + CONV_PAD, F32), conv_pad[None], (shard, 0, 0))

    gather_ids = {"in": 16, "kvo": 17, "ffn1": 18, "ffn2": 19, "ffn2d": 20}

    def gather_start(bufs, after, nm, forwards=()):
        return _allgather_start(bufs, forwards, after, gather_ids[nm], name="ag_start_" + nm)

    def gather_wait(state, idx, after, nm):
        send, recv, bufs, _ = state
        got, token = _transfer_wait([send[k] for k in idx], [recv[k] for k in idx], [[bufs[k]] for k in idx],
                                    [(N_NEIGHBOUR_CHIPS, bufs[k].shape[1] // 2) for k in idx], after, name="ag_wait_" + nm)
        return [g[0] for g in got], token

    cast = lambda k, after: _cast_into_slot(big[k], slot, after, name="cast_" + big_names[k])
    ag_in = gather_start([cast(0, slot), conv_slots], slot, "in")
    bs_t = bs2.T

    h = _rms_fwd(x2d, g_mix, name="rms_mix", after=[ag_in[3]])
    mem_n = _rms_fwd(mem2d, g_mem, name="rms_mem", after=[h])
    kvo_b = [cast(1, mem_n)]
    kvo_b.append(cast(2, kvo_b[0]))
    w1_b = cast(3, kvo_b[1])
    w2_b = cast(4, w1_b)
    NEAR, FAR = [0, 1], [2]

    def diagonal_wait(state, ks, after, nm):
        send, recv, bufs, _ = state
        got, token = _transfer_wait([send[k] for k in ks], [recv[k] for k in ks], [[bufs[k]] for k in ks],
                                    [(1, bufs[k].shape[1] // 2) for k in ks], after, name="ag_waitd_" + nm)
        return [g[0] for g in got], token

    got_in, tok = gather_wait(ag_in, [0, 1], w2_b, "in")
    ag_kvo = gather_start(kvo_b, tok, "kvo", forwards=got_in)
    in_n = _forward_halves(ag_kvo[2][2:], NEAR, ag_kvo[3], name="ag_fwdn_in")
    ag_kvo = (ag_kvo[0], ag_kvo[1], list(ag_kvo[2][:2]) + list(in_n), ag_kvo[3])
    in_d, tok = diagonal_wait(ag_kvo, [2, 3], ag_kvo[3], "in")
    win4, conv4 = _forward_halves(in_d, FAR, tok, name="ag_fwdd_in")
    w_in_t = win4.reshape(DIN, D)
    conv_full = conv4[:, :3, :dcv4].transpose(1, 0, 2).reshape(3, DC)

    proj_w = lambda tn, tk: pl.BlockSpec((tn, tk), lambda j, i, k, s: (s[j], k))
    proj_cols = lambda tm, tn: [pl.BlockSpec((tm, tn), lambda j, i, k, s: (i, s[j]))]
    proj_half = lambda which, into, after: _matmul(
        h, w_in_t, name="mm_proj_%d" % which, tb=True, M=S, N=DIN // 2, K=D, tn=DIN // 2, b_spec=proj_w,
        out_specs=proj_cols, outs=[sds((S, DIN), F32)], slots=jnp.full((1,), which, jnp.int32), into=into,
        after=after)[0]
    proj = proj_half(0, [], [ag_kvo[3]])
    got_kvo, tok = gather_wait(ag_kvo, [0, 1], proj, "kvo")
    ag_w1 = gather_start([w1_b], tok, "ffn1", forwards=got_kvo)
    kvo_n = _forward_halves(ag_w1[2][1:], NEAR, ag_w1[3], name="ag_fwdn_kvo")
    ag_w1 = (ag_w1[0], ag_w1[1], [ag_w1[2][0]] + list(kvo_n), ag_w1[3])
    proj = proj_half(1, [proj], list(kvo_n))
    kvo_d, tok = diagonal_wait(ag_w1, [1, 2], proj, "kvo")
    wkv4, wo4 = _forward_halves(kvo_d, FAR, tok, name="ag_fwdd_kvo")
    w_kv_full = wkv4.reshape(D, 2 * DM)
    w_o_full = wo4.reshape(D, D)
    (kv,) = _matmul(mem_n, w_kv_full, name="mm_kv", M=NMEM, N=2 * DM, K=D, outs=[sds((NMEM, 2 * DM), F32)])
    heads, hn, ycv = _mix_fwd(proj, kv, ws3, bs_t, ln_v_g, ln_v_b, conv_full, g_head, name="mix_fwd")
    def residual_and_norm(acc, res, g):
        x2v = acc + res
        r = lax.rsqrt(jnp.mean(x2v * x2v, axis=-1, keepdims=True) + EPS)
        return x2v, (x2v * r) * g

    row_vec = lambda tm, tn: pl.BlockSpec((1, tn), lambda j, i, k, *s: (0, j))
    x2, h2 = _matmul(hn, w_o_full, name="mm_wo", M=S, N=D, K=D, tn=D, n_split=1, epi=residual_and_norm,
                     outs=[sds((S, D), F32), sds((S, D), BF16)], extras=[(x2d, _tile_spec()), (g_ffn, row_vec)])
    near = jnp.stack([shard, 2 * (1 - xi) + yi, 2 * xi + (1 - yi)]).astype(jnp.int32)
    far = jnp.reshape(2 * (1 - xi) + (1 - yi), (1,)).astype(jnp.int32)

    w1_shard = lambda tn, tk: pl.BlockSpec((None, tk, tn), lambda j, i, k, s: (s[j], k, 0))
    act_cols = lambda tm, tn: [pl.BlockSpec((tm, tn), lambda j, i, k, s: (i, s[j]))] * 2

    def relu2(acc):
        r = jnp.maximum(acc, 0.0)
        return r * r, 2.0 * r

    got_w1, tok = gather_wait(ag_w1, [0], h2, "ffn1")
    ag_w2 = gather_start([w2_b], tok, "ffn2", forwards=got_w1)
    (w1n,) = _forward_halves([ag_w2[2][1]], NEAR, ag_w2[3], name="ag_fwdn_ffn1")
    ag_w2 = (ag_w2[0], ag_w2[1], [ag_w2[2][0], w1n], ag_w2[3])
    act, dact_df = _matmul(h2, w1n, name="mm_ffn1_near", M=S, N=3 * dff4, K=D, tm=2 * TM, tn=dff4, b_spec=w1_shard,
                           out_specs=act_cols, outs=[sds((S, DFF), BF16)] * 2, epi=relu2, slots=near)
    w1d, tok = diagonal_wait(ag_w2, [1], act, "ffn1")
    (w14,) = _forward_halves(w1d, FAR, tok, name="ag_fwdd_ffn1")
    act, dact_df = _matmul(h2, w14, name="mm_ffn1_far", M=S, N=dff4, K=D, tm=2 * TM, tn=dff4, b_spec=w1_shard,
                           out_specs=act_cols, outs=[sds((S, DFF), BF16)] * 2, epi=relu2, slots=far,
                           into=[act, dact_df])

    act_shard = lambda tm, tk: pl.BlockSpec((tm, tk), lambda j, i, k, s: (i, s[k]))
    w2_shard = lambda tn, tk: pl.BlockSpec((None, tk, tn), lambda j, i, k, s: (s[k], 0, j))
    got_w2, tok = gather_wait(ag_w2, [0], act, "ffn2")
    ag_w2d = gather_start([], tok, "ffn2d", forwards=got_w2)
    (w2n,) = _forward_halves(ag_w2d[2], NEAR, ag_w2d[3], name="ag_fwdn_ffn2")
    ag_w2d = (ag_w2d[0], ag_w2d[1], [w2n], ag_w2d[3])
    (x3,) = _matmul(act, w2n, name="mm_ffn2_near", M=S, N=D, K=3 * dff4, tm=2 * TM, tk=dff4,
                    a_spec=act_shard, b_spec=w2_shard, outs=[sds((S, D), F32)], epi=lambda acc, res: (acc + res,),
                    extras=[(x2, _tile_spec())], slots=near)
    w2d, tok = diagonal_wait(ag_w2d, [0], x3, "ffn2")
    (w24,) = _forward_halves(w2d, FAR, tok, name="ag_fwdd_ffn2")
    (x3,) = _matmul(act, w24, name="mm_ffn2_far", M=S, N=D, K=dff4, tm=2 * TM, tk=dff4, a_spec=act_shard,
                    b_spec=w2_shard, outs=[sds((S, D), F32)], epi=lambda acc, res: (acc + res,),
                    extras=[(x3, _tile_spec())], slots=far)
    w2_full = w24.reshape(DFF, D)

    ci = lax.axis_index("c")

    def rs_sibling(g4, nm):
        return _sibling_start([g4], False, 1 + big_names.index(nm), name="rs_sib_" + nm)

    def rs_chips(state, after, nm):
        send, recv, g4, land, _ = state
        (((land_, g4_),), _) = _transfer_wait(send, recv, [[land[0], g4[0]]], [(NCHIP, land[0].shape[1])], after,
                                             name="rs_sibwait_" + nm)
        part, buf = _select_half_bf16(g4_, ci, land_, slot, name="rs_add_" + nm)
        return _scatter_start([part], [buf], 1 + 2 * len(big_names) + big_names.index(nm), name="rs_start_" + nm)

    def dw_half(a, b, nm, *, by_rows, hr, cols, which, land, after):
        tile = lambda tm, tn: pl.BlockSpec(
            (None, tm, tn), (lambda j, i, k, s: (i, 0, j)) if by_rows else (lambda j, i, k, s: (j, 0, 0)))
        a_half = lambda tm, tk: pl.BlockSpec(
            (tk, tm), (lambda j, i, k, s: (k, 2 * i + s[0])) if by_rows else (lambda j, i, k, s: (k, s[0])))
        (out,) = _matmul(a, b, name=nm, ta=True, M=NCHIP * hr if by_rows else hr, N=cols if by_rows else NCHIP * cols,
                         K=S, tm=hr, tn=cols, a_spec=a_half, out_specs=lambda tm, tn: [tile(tm, tn)],
                         outs=[sds((NCHIP, hr, cols), BF16)], slots=jnp.reshape(which, (1,)).astype(jnp.int32),
                         epi=None if land is None else (lambda acc, other: (acc + other.astype(F32),)),
                         extras=[] if land is None else [(land, tile)], after=after)
        return out

    def rs_sibling_half(half, nm):
        return _sibling_start([half], True, 1 + big_names.index(nm), name="rs_sib_" + nm)

    def rs_chips_fused(state, grad_half, after, nm):
        send, recv, mine, land, _ = state
        (((land_, _),), tok) = _transfer_wait(send, recv, [[land[0], mine[0]]], [(NCHIP, land[0].shape[1])], after,
                                             name="rs_sibwait_" + nm)
        part = grad_half(land_, [tok])
        buf = _own_slot(part, slot, name="rs_own_" + nm)
        return _scatter_start([part], [buf], 1 + 2 * len(big_names) + big_names.index(nm), name="rs_start_" + nm)

    def rs_end(state, after, nm):
        send, recv, parts, bufs, _ = state
        (((buf, _),), _) = _transfer_wait(send, recv, [[bufs[0], parts[0]]], [(N_PEER_CHIPS, bufs[0].shape[1])], after,
                                          name="rs_wait_" + nm)
        return _sibling_start([buf], True, 1 + len(big_names) + big_names.index(nm), name="rs_share_" + nm)

    big_m = [m_w_in[0].T, m_w_kv[0], m_w_o[0], m_w_ffn1[0], m_w_ffn2[0]]
    big_v = [v_w_in[0].T, v_w_kv[0], v_w_o[0], v_w_ffn1[0], v_w_ffn2[0]]
    big_out = {}

    def rs_finish(k, state, after):
        send, recv, mine, land, _ = state
        nm = big_names[k]
        (((land_, mine_),), _) = _transfer_wait(send, recv, [[land[0], mine[0]]], [(NCHIP, land[0].shape[1])], after,
                                               name="rs_sharewait_" + nm)
        big_out[nm] = _adamw(big[k], mine_, land_, big_m[k], big_v[k], core, name="adamw_" + nm)
        return big_out[nm][1]

    dx3, dx3b, dg_final, loss11 = _loss_bwd(x3, g_final2, tgt, name="loss_bwd")
    dw2_half = lambda which, land, after, nm: dw_half(
        act, dx3b, nm, by_rows=True, hr=dff4 // 2, cols=D, which=which, land=land, after=after)
    sib_w2 = rs_sibling_half(dw2_half(1 - ci, None, [], "mm_dw2_sib"), "w_ffn2")
    (dfb,) = _matmul(dx3b, w2_full, name="mm_dact", tb=True, M=S, N=DFF, K=D, tm=2 * TM, tn=dff4, outs=[sds((S, DFF), BF16)],
                     epi=lambda acc, g: (acc * g.astype(F32),), extras=[(dact_df, _tile_spec())],
                     after=[sib_w2[4]])
    rs_w2 = rs_chips_fused(sib_w2, lambda land, after: dw2_half(ci, land, after, "mm_dw2_own"), dfb, "w_ffn2")

    dw1_half = lambda which, land, after, nm: dw_half(
        h2, dfb, nm, by_rows=False, hr=D // 2, cols=dff4, which=which, land=land, after=after)
    sib_w1 = rs_sibling_half(dw1_half(1 - ci, None, [rs_w2[4]], "mm_dw1_sib"), "w_ffn1")

    def w1_rows(tn, tk):
        kb = dff4 // tk
        return pl.BlockSpec((None, tn, tk), lambda j, i, k: (k // kb, j, k % kb))

    (dh2,) = _matmul(dfb, w14, name="mm_dh2", tb=True, M=S, N=D, K=DFF, tm=2 * TM, b_spec=w1_rows,
                     outs=[sds((S, D), F32)], after=[sib_w1[4]])
    rs_w1 = rs_chips_fused(sib_w1, lambda land, after: dw1_half(ci, land, after, "mm_dw1_own"), dh2, "w_ffn1")
    dx2, dx2b, dg_ffn = _rms_bwd(dh2, x2, g_ffn, dx3, name="rms_ffn_bwd", after=[rs_w1[4]])
    dwo_half = lambda which, land, after, nm: dw_half(
        hn, dx2b, nm, by_rows=True, hr=D // NCHIP // 2, cols=D, which=which, land=land, after=after)
    sib_wo = rs_sibling_half(dwo_half(1 - ci, None, [], "mm_dwo_sib"), "w_o")
    (dhn,) = _matmul(dx2b, w_o_full, name="mm_dhn", tb=True, M=S, N=D, K=D, tm=2 * TM, outs=[sds((S, D), F32)],
                     after=[sib_wo[4]])
    rs_wo = rs_chips_fused(sib_wo, lambda land, after: dwo_half(ci, land, after, "mm_dwo_own"), dhn, "w_o")
    sh_w2 = rs_end(rs_w2, rs_wo[4], "w_ffn2")
    dproj, dkv, dws, dbs8, dlng, dlnb, dcw8, dgh = _mix_bwd(
        dhn, heads, proj, ycv, kv, ws3, bs_t, ln_v_g, ln_v_b, conv_full, g_head, sh_w2[4], name="mix_bwd")
    (dwin_t,) = _matmul(dproj, h, name="mm_dwin", ta=True, M=DIN, N=D, K=S, tm=DIN // 2, outs=[sds((DIN, D), BF16)])
    sib_win = rs_sibling(dwin_t.reshape(NCHIP, din4, D), "w_in")
    (dwkv,) = _matmul(mem_n, dkv, name="mm_dwkv", ta=True, M=D, N=2 * DM, K=NMEM, outs=[sds((D, 2 * DM), BF16)],
                      after=[sib_win[4]])
    sib_wkv = rs_sibling(dwkv.reshape(NCHIP, D // NCHIP, 2 * DM), "w_kv")
    (dh,) = _matmul(dproj, w_in_t, name="mm_dh", M=S, N=D, K=DIN, tm=2 * TM, tk=DIN, outs=[sds((S, D), F32)],
                    after=[sib_wkv[4]])
    rs_win = rs_chips(sib_win, dh, "w_in")
    rs_wkv = rs_chips(sib_wkv, rs_win[4], "w_kv")
    dx, dg_mix = _rms_bwd(dh, x2d, g_mix, dx2, name="rms_mix_bwd", want_bf=False, after=[rs_wkv[4]])
    sh_w1 = rs_end(rs_w1, dx, "w_ffn1")
    (dmem_n,) = _matmul(dkv, w_kv_full, name="mm_dmem", tb=True, M=NMEM, N=D, K=2 * DM, outs=[sds((NMEM, D), F32)],
                        after=[sh_w1[4]])
    (dg_mem,) = _rms_bwd(dmem_n, mem2d, g_mem, None, name="rms_mem_bwd", want_dx=False)
    sh_wo = rs_end(rs_wo, dg_mem, "w_o")
    done = rs_finish(4, sh_w2, sh_wo[4])
    done = rs_finish(3, sh_w1, done)
    sh_win = rs_end(rs_win, done, "w_in")
    sh_wkv = rs_end(rs_wkv, sh_win[4], "w_kv")
    done = rs_finish(2, sh_wo, sh_wkv[4])
    done = rs_finish(0, sh_win, done)
    done = rs_finish(1, sh_wkv, done)

    small_names = ["g_mix", "ln_v_g", "ln_v_b", "w_s", "b_s", "conv_w", "g_mem", "g_head", "g_ffn", "g_final"]
    small_part = [dg_mix, dlng, dlnb, dws, dbs8[:, 0, :], dcw8[:3], dg_mem, dgh, dg_ffn, dg_final, loss11]
    small_shapes = [(1, D), (1, DS), (1, DS), (NSH, CHUNK, CHUNK), (NSH, CHUNK), (3, DC), (1, D), (1, D), (1, D), (1, D),
                    (1, 1)]
    total = _allreduce_small(_pack(small_part), done, name="allreduce_small")
    small_g = _unpack(total, small_shapes)
    loss = small_g.pop()[0, 0]
    small_g[5] = lax.dynamic_slice(small_g[5], (0, shard * dcv4), (3, dcv4))
    small_w = [g_mix, ln_v_g, ln_v_b, ws3, bs2, conv_w[0], g_mem, g_head, g_ffn, g_final2]
    small_m = [m_g_mix, m_ln_v_g, m_ln_v_b, m_w_s[0], m_b_s[0], m_conv_w[0], m_g_mem, m_g_head, m_g_ffn,
               m_g_final.reshape(1, D)]
    small_v = [v_g_mix, v_ln_v_g, v_ln_v_b, v_w_s[0], v_b_s[0], v_conv_w[0], v_g_mem, v_g_head, v_g_ffn,
               v_g_final.reshape(1, D)]
    s_delta, s_m, s_v = _adamw_small(small_w, small_g, small_m, small_v, name="adamw_small")
    small_out = {nm: (g, d, mn, vn) for nm, g, d, mn, vn in zip(small_names, small_g, s_delta, s_m, s_v)}

    order = ["g_mix", "w_in", "ln_v_g", "ln_v_b", "w_s", "b_s", "conv_w", "g_mem", "w_kv", "g_head", "w_o",
             "g_ffn", "w_ffn1", "w_ffn2", "g_final"]
    like = dict(g_mix=g_mix, w_in=w_in, ln_v_g=ln_v_g, ln_v_b=ln_v_b, w_s=w_s, b_s=b_s, conv_w=conv_w, g_mem=g_mem,
                w_kv=w_kv, g_head=g_head, w_o=w_o, g_ffn=g_ffn, w_ffn1=w_ffn1, w_ffn2=w_ffn2, g_final=g_final)
    res = {**big_out, **small_out}
    res["w_in"] = [a.T for a in res["w_in"]]
    outs = [loss, dx[None]]
    for k in range(4):
        outs += [res[nm][k].reshape(like[nm].shape) for nm in order]
    return tuple(outs)
```

```python
import math

import jax
import jax.numpy as jnp
from jax import lax
from jax.experimental import pallas as pl
from jax.experimental.pallas import tpu as pltpu

F32 = jnp.float32
BF16 = jnp.bfloat16
MESH = pl.DeviceIdType.MESH

D = 2048
S = 2048
HD = 128
NH = D // HD
NMH = 4
NSH = (NH - NMH) // 2
NCH = NH - NMH - NSH
DS = NSH * HD
DC = NCH * HD
DM = NMH * HD
DIN = 2 * DS + 3 * DC + DM
CHUNK = 128
NMEM = 256
DFF = 4 * D
EPS = 1e-6
NCHIP = 4
SCALE = HD ** -0.5

ADAM_LR = 0.001
ADAM_B1 = 0.9
ADAM_B2 = 0.999
ADAM_EPS = 1e-08
ADAM_WD = 0.01
ADAM_STEP = 10

TR_EW = 256
TR_MIX = 256
TM = 512
TN = 1024
TK = 2048
N_SUB = 512
VMEM_MB = 56
HALO = 8


def _pick(n, target, q=128):
    best = None
    for t in range(q, min(n, target) + 1, q):
        if n % t == 0:
            best = t
    return n if best is None else best


def _pick_rows(n, q):
    below = _pick(n, TR_EW, q)
    if 2 * below >= TR_EW:
        return below
    above = [t for t in range(TR_EW, min(n, 4 * TR_EW) + 1, q) if n % t == 0]
    return above[0] if above else below


def _cp(sem=None, vmem_mb=None, **kw):
    d = dict(kw)
    if sem is not None:
        d["dimension_semantics"] = sem
    if vmem_mb is not None:
        d["vmem_limit_bytes"] = vmem_mb << 20
    return pltpu.CompilerParams(**d)


def _gelu(x):
    z = 0.7978845608028654 * (x + 0.044715 * (x * x * x))
    return 0.5 * x * (1.0 + jnp.tanh(z))


def _gelu_with_grad(x):
    x2 = x * x
    t = jnp.tanh(0.7978845608028654 * (x + 0.044715 * (x2 * x)))
    half = 0.5 * (1.0 + t)
    return x * half, half + 0.5 * x * (1.0 - t * t) * (0.7978845608028654 * (1.0 + 3.0 * 0.044715 * x2))


def _matmul(a, b, *, name, ta=False, tb=False, M, N, K, tm=None, tn=None, tk=None, outs, epi=None,
            extras=(), a_spec=None, b_spec=None, out_specs=None, after=(), n_split=None, slots=None, into=()):
    n_after = len(after)
    tm = _pick(M, TM if tm is None else tm, 8)
    tn = _pick(N, TN if tn is None else tn)
    tk = _pick(K, TK if tk is None else tk)
    if n_split is None:
        n_split = tn // N_SUB if tn % N_SUB == 0 else 1
    nk = K // tk
    grid = (N // tn, M // tm, nk)
    if a_spec is None:
        a_spec = (pl.BlockSpec((tk, tm), lambda j, i, k, *s: (k, i)) if ta
                  else pl.BlockSpec((tm, tk), lambda j, i, k, *s: (i, k)))
    else:
        a_spec = a_spec(tm, tk)
    if b_spec is None:
        b_spec = (pl.BlockSpec((tn, tk), lambda j, i, k, *s: (j, k)) if tb
                  else pl.BlockSpec((tk, tn), lambda j, i, k, *s: (k, j)))
    else:
        b_spec = b_spec(tn, tk)
    if out_specs is None:
        out_specs = [pl.BlockSpec((tm, tn), lambda j, i, k, *s: (i, j)) for _ in outs]
    else:
        out_specs = out_specs(tm, tn)
    dn = (((0 if ta else 1,), (1 if tb else 0,)), ((), ()))
    n_ex, n_out = len(extras), len(outs)
    n_pre = 0 if slots is None else 1
    n_into = len(into)
    ns = tn // n_split

    def body(*refs):
        a_ref, b_ref = refs[n_pre], refs[n_pre + 1]
        ex = refs[n_pre + 2:n_pre + 2 + n_ex]
        first_out = n_pre + 2 + n_ex + n_after + n_into
        o = refs[first_out:first_out + n_out]
        acc = refs[first_out + n_out:]
        k = pl.program_id(2)

        def finish(val, cols):
            res = (val,) if epi is None else epi(val, *[e[:, cols] for e in ex])
            for r, o_ref in zip(res, o):
                o_ref[:, cols] = r.astype(o_ref.dtype)

        if nk > 1:
            @pl.when(k == 0)
            def _():
                acc[0][...] = jnp.zeros_like(acc[0])

        av = a_ref[...].astype(BF16)
        for q in range(n_split):
            cols = slice(q * ns, (q + 1) * ns)
            bq = (b_ref[cols, :] if tb else b_ref[:, cols]).astype(BF16)
            part = lax.dot_general(av, bq, dn, preferred_element_type=F32)
            if nk == 1:
                finish(part, cols)
            else:
                acc[0][:, cols] += part

        if nk > 1:
            @pl.when(k == nk - 1)
            def _():
                finish(acc[0][...], slice(0, tn))

    in_specs = ([a_spec, b_spec] + [sp(tm, tn) for _, sp in extras] + [ANY] * (n_after + n_into))
    scratch = [pltpu.VMEM((tm, tn), F32)] if nk > 1 else []
    args = [a, b] + [arr for arr, _ in extras] + list(after) + list(into)
    aliases = {n_pre + len(args) - n_into + t: t for t in range(n_into)}
    params = _cp(("parallel", "parallel", "arbitrary"), VMEM_MB)
    if slots is None:
        return pl.pallas_call(body, name=name, grid=grid, in_specs=in_specs, out_specs=out_specs, out_shape=outs,
                              scratch_shapes=scratch, input_output_aliases=aliases, compiler_params=params)(*args)
    return pl.pallas_call(
        body, name=name,
        grid_spec=pltpu.PrefetchScalarGridSpec(num_scalar_prefetch=1, grid=grid, in_specs=in_specs,
                                               out_specs=out_specs, scratch_shapes=scratch),
        out_shape=outs, input_output_aliases=aliases, compiler_params=params)(slots, *args)


def _tile_spec():
    return lambda tm, tn: pl.BlockSpec((tm, tn), lambda j, i, k, *s: (i, j))


def _cast_into_slot(w, slot, after, *, name):
    R, C = w.shape
    tr = _pick_rows(R, 16)

    def body(s_ref, w_ref, _after_ref, o_ref):
        o_ref[...] = w_ref[...].astype(BF16)

    return pl.pallas_call(
        body, name=name,
        grid_spec=pltpu.PrefetchScalarGridSpec(
            num_scalar_prefetch=1, grid=(R // tr,),
            in_specs=[pl.BlockSpec((tr, C), lambda i, s: (i, 0)), ANY],
            out_specs=pl.BlockSpec((None, tr, C), lambda i, s: (s[0], i, 0))),
        out_shape=jax.ShapeDtypeStruct((NCHIP, R, C), BF16),
        compiler_params=_cp(("parallel",), VMEM_MB),
    )(slot, w, after)


def _own_slot(part, slot, *, name):
    _, R, C = part.shape
    tr = _pick_rows(R, 16)

    def body(s_ref, p_ref, o_ref):
        o_ref[...] = p_ref[...]

    spec = pl.BlockSpec((None, tr, C), lambda i, s: (s[0], i, 0))
    return pl.pallas_call(
        body, name=name,
        grid_spec=pltpu.PrefetchScalarGridSpec(num_scalar_prefetch=1, grid=(R // tr,), in_specs=[spec],
                                               out_specs=spec),
        out_shape=jax.ShapeDtypeStruct(part.shape, part.dtype),
        compiler_params=_cp(("parallel",), VMEM_MB),
    )(slot, part)


def _rms_fwd(x, g, *, name, after=()):
    R, C = x.shape
    tr = _pick(R, TR_EW, 16)
    n_after = len(after)

    def body(x_ref, g_ref, *rest):
        o_ref = rest[n_after]
        xv = x_ref[...]
        r = lax.rsqrt(jnp.mean(xv * xv, axis=-1, keepdims=True) + EPS)
        o_ref[...] = ((xv * r) * g_ref[...]).astype(BF16)

    return pl.pallas_call(
        body, name=name, grid=(R // tr,),
        in_specs=[pl.BlockSpec((tr, C), lambda i: (i, 0)), pl.BlockSpec((1, C), lambda i: (0, 0))] + [ANY] * n_after,
        out_specs=pl.BlockSpec((tr, C), lambda i: (i, 0)),
        out_shape=jax.ShapeDtypeStruct((R, C), BF16),
        compiler_params=_cp(("parallel",), VMEM_MB),
    )(x, g, *after)


def _rms_bwd(dh, x, g, dres, *, name, want_dx=True, want_bf=True, after=()):
    R, C = x.shape
    tr = _pick(R, TR_EW, 16)
    has_res = dres is not None
    row = pl.BlockSpec((tr, C), lambda i: (i, 0))
    vec = pl.BlockSpec((1, C), lambda i: (0, 0))

    def body(*refs):
        dh_ref, x_ref, g_ref = refs[:3]
        pos = 3
        dres_ref = None
        if has_res:
            dres_ref = refs[pos]
            pos += 1
        outs = refs[pos + len(after):]
        i = pl.program_id(0)
        xv = x_ref[...]
        r = lax.rsqrt(jnp.mean(xv * xv, axis=-1, keepdims=True) + EPS)
        xh = xv * r
        dhv = dh_ref[...]
        dg_ref = outs[-1]
        dgp = jnp.sum(dhv * xh, axis=0, keepdims=True)

        @pl.when(i == 0)
        def _():
            dg_ref[...] = dgp

        @pl.when(i > 0)
        def _():
            dg_ref[...] += dgp

        if want_dx:
            t = dhv * g_ref[...]
            dx = r * (t - xh * jnp.mean(t * xh, axis=-1, keepdims=True))
            if has_res:
                dx = dx + dres_ref[...]
            outs[0][...] = dx
            if want_bf:
                outs[1][...] = dx.astype(BF16)

    in_specs = [row, row, vec] + ([row] if has_res else []) + [ANY] * len(after)
    out_specs, out_shape = [], []
    if want_dx:
        out_specs.append(row)
        out_shape.append(jax.ShapeDtypeStruct((R, C), F32))
        if want_bf:
            out_specs.append(row)
            out_shape.append(jax.ShapeDtypeStruct((R, C), BF16))
    out_specs.append(vec)
    out_shape.append(jax.ShapeDtypeStruct((1, C), F32))
    args = [dh, x, g] + ([dres] if has_res else []) + list(after)
    return pl.pallas_call(
        body, name=name, grid=(R // tr,), in_specs=in_specs, out_specs=out_specs, out_shape=out_shape,
        compiler_params=_cp(("arbitrary",), VMEM_MB),
    )(*args)


def _loss_bwd(x3, g, tgt, *, name):
    R, C = x3.shape
    tr = _pick(R, TR_EW, 16)
    n = R // tr
    row = pl.BlockSpec((tr, C), lambda i: (i, 0))
    vec = pl.BlockSpec((1, C), lambda i: (0, 0))

    def body(x_ref, g_ref, t_ref, dx_ref, dxb_ref, dg_ref, loss_ref, acc_ref):
        i = pl.program_id(0)
        xv = x_ref[...]
        gv = g_ref[...]
        r = lax.rsqrt(jnp.mean(xv * xv, axis=-1, keepdims=True) + EPS)
        xh = xv * r
        e = xh * gv - t_ref[...]
        dy = e * (1.0 / C)
        sq = jnp.sum(e * e, axis=0, keepdims=True)
        dgp = jnp.sum(dy * xh, axis=0, keepdims=True)

        @pl.when(i == 0)
        def _():
            acc_ref[...] = sq
            dg_ref[...] = dgp

        @pl.when(i > 0)
        def _():
            acc_ref[...] += sq
            dg_ref[...] += dgp

        t = dy * gv
        dx = r * (t - xh * jnp.mean(t * xh, axis=-1, keepdims=True))
        dx_ref[...] = dx
        dxb_ref[...] = dx.astype(BF16)

        @pl.when(i == n - 1)
        def _():
            loss_ref[...] = jnp.sum(acc_ref[...], axis=-1, keepdims=True) * (0.5 / C)

    return pl.pallas_call(
        body, name=name, grid=(n,),
        in_specs=[row, vec, row],
        out_specs=[row, row, vec, pl.BlockSpec((1, 1), lambda i: (0, 0))],
        out_shape=[jax.ShapeDtypeStruct((R, C), F32), jax.ShapeDtypeStruct((R, C), BF16),
                   jax.ShapeDtypeStruct((1, C), F32), jax.ShapeDtypeStruct((1, 1), F32)],
        scratch_shapes=[pltpu.VMEM((1, C), F32)],
        compiler_params=_cp(("arbitrary",), VMEM_MB),
    )(x3, g, tgt)


def _offsets():
    u0 = 0
    v0 = DS
    b0 = 2 * DS
    c0 = b0 + DC
    x0 = c0 + DC
    q0 = x0 + DC
    return u0, v0, b0, c0, x0, q0


def _tri_mask(lower):
    r = lax.broadcasted_iota(jnp.int32, (CHUNK, CHUNK), 0)
    c = lax.broadcasted_iota(jnp.int32, (CHUNK, CHUNK), 1)
    return (r >= c) if lower else (c >= r)


def _layer_norm_stats(vg):
    mu = jnp.mean(vg, axis=-1, keepdims=True)
    vc = vg - mu
    rstd = lax.rsqrt(jnp.mean(vc * vc, axis=-1, keepdims=True) + EPS)
    return vc * rstd, rstd


def _softmax_rows(qh, kh):
    s = lax.dot_general(qh, kh, (((1,), (1,)), ((), ())), preferred_element_type=F32)
    m = jnp.max(s, axis=-1, keepdims=True)
    e = jnp.exp(s - m)
    return e / jnp.sum(e, axis=-1, keepdims=True)


def _mix_fwd(proj, kv, w_s, bs_t, ln_g, ln_b, conv_w, g_head, *, name):
    assert DS == DC
    tr = _pick(S, TR_MIX, CHUNK)
    n = S // tr
    nck = tr // CHUNK
    u0, v0, b0, c0, x0, q0 = _offsets()
    hb = tr // HALO

    def body(p_ref, cprev_ref, xprev_ref, kv_ref, ws_ref, bst_ref, lng_ref, lnb_ref, cw_ref, gh_ref,
             heads_ref, hn_ref, ycv_ref, buf_ref):
        i = pl.program_id(0)

        def emit(col, val):
            rs = lax.rsqrt(jnp.mean(val * val, axis=-1, keepdims=True) + EPS)
            heads_ref[:, col:col + HD] = val
            hn_ref[:, col:col + HD] = ((val * rs) * gh_ref[:, col:col + HD]).astype(BF16)

        vhat, _ = _layer_norm_stats(_gelu(p_ref[:, v0:v0 + DS]))
        vnb = (vhat * lng_ref[...] + lnb_ref[...]).astype(BF16)
        low = _tri_mask(True)
        for h in range(NSH):
            wt = jnp.where(low, ws_ref[h], 0.0).astype(BF16)
            bcol = bst_ref[:, h:h + 1]
            parts = []
            for c in range(nck):
                blk = vnb[c * CHUNK:(c + 1) * CHUNK, h * HD:(h + 1) * HD]
                parts.append(jnp.dot(wt, blk, preferred_element_type=F32) + bcol)
            mixed = parts[0] if nck == 1 else jnp.concatenate(parts, axis=0)
            emit(h * HD, _gelu(p_ref[:, u0 + h * HD:u0 + (h + 1) * HD]) * mixed)

        xc = p_ref[:, c0:c0 + DC] * p_ref[:, x0:x0 + DC]
        prev = cprev_ref[...] * xprev_ref[...]
        buf_ref[0:HALO, :] = jnp.where(i > 0, prev, 0.0)
        buf_ref[HALO:HALO + tr, :] = xc
        y = (cw_ref[2:3, :] * xc + cw_ref[1:2, :] * buf_ref[HALO - 1:HALO - 1 + tr, :]
             + cw_ref[0:1, :] * buf_ref[HALO - 2:HALO - 2 + tr, :])
        ycv_ref[...] = y
        cout = p_ref[:, b0:b0 + DC] * y
        for h in range(NCH):
            emit(DS + h * HD, cout[:, h * HD:(h + 1) * HD])

        for h in range(NMH):
            qh = (p_ref[:, q0 + h * HD:q0 + (h + 1) * HD] * SCALE).astype(BF16)
            kh = kv_ref[:, h * HD:(h + 1) * HD].astype(BF16)
            vh = kv_ref[:, DM + h * HD:DM + (h + 1) * HD].astype(BF16)
            p = _softmax_rows(qh, kh)
            emit(DS + DC + h * HD, jnp.dot(p.astype(BF16), vh, preferred_element_type=F32))

    full = lambda shape: pl.BlockSpec(shape, lambda i: (0,) * len(shape))
    halo_c = pl.BlockSpec((HALO, DC), lambda i: (jnp.maximum(i * hb - 1, 0), c0 // DC))
    halo_x = pl.BlockSpec((HALO, DC), lambda i: (jnp.maximum(i * hb - 1, 0), x0 // DC))
    return pl.pallas_call(
        body, name=name, grid=(n,),
        in_specs=[pl.BlockSpec((tr, DIN), lambda i: (i, 0)), halo_c, halo_x,
                  full((NMEM, 2 * DM)), full((NSH, CHUNK, CHUNK)), full((CHUNK, NSH)),
                  full((1, DS)), full((1, DS)), full((3, DC)), full((1, D))],
        out_specs=[pl.BlockSpec((tr, D), lambda i: (i, 0)), pl.BlockSpec((tr, D), lambda i: (i, 0)),
                   pl.BlockSpec((tr, DC), lambda i: (i, 0))],
        out_shape=[jax.ShapeDtypeStruct((S, D), F32), jax.ShapeDtypeStruct((S, D), BF16),
                   jax.ShapeDtypeStruct((S, DC), F32)],
        scratch_shapes=[pltpu.VMEM((tr + HALO, DC), F32)],
        compiler_params=_cp(("parallel",), VMEM_MB),
    )(proj, proj, proj, kv, w_s, bs_t, ln_g, ln_b, conv_w, g_head)


def _mix_bwd(dhn, heads, proj, ycv, kv, w_s, bs_t, ln_g, ln_b, conv_w, g_head, after, *, name):
    assert DS == DC
    tr = _pick(S, TR_MIX, CHUNK)
    n = S // tr
    nck = tr // CHUNK
    u0, v0, b0, c0, x0, q0 = _offsets()
    hb = tr // HALO
    last_hb = S // HALO - 1

    def body(dhn_ref, heads_ref, p_ref, ycv_ref, dhn_nx_ref, heads_nx_ref, b_nx_ref, kv_ref, ws_ref, bst_ref,
             lng_ref, lnb_ref, cw_ref, gh_ref, _after_ref,
             dp_ref, dkv_ref, dws_ref, dbs_ref, dlng_ref, dlnb_ref, dcw_ref, dgh_ref, buf_ref, dvn_ref):
        i = pl.program_id(0)

        @pl.when(i == 0)
        def _():
            dkv_ref[...] = jnp.zeros_like(dkv_ref)
            dws_ref[...] = jnp.zeros_like(dws_ref)
            dbs_ref[...] = jnp.zeros_like(dbs_ref)
            dlng_ref[...] = jnp.zeros_like(dlng_ref)
            dlnb_ref[...] = jnp.zeros_like(dlnb_ref)
            dcw_ref[...] = jnp.zeros_like(dcw_ref)
            dgh_ref[...] = jnp.zeros_like(dgh_ref)

        def head_bwd(a, dn, gh):
            rs = lax.rsqrt(jnp.mean(a * a, axis=-1, keepdims=True) + EPS)
            ah = a * rs
            t = dn * gh
            return rs * (t - ah * jnp.mean(t * ah, axis=-1, keepdims=True)), jnp.sum(dn * ah, axis=0, keepdims=True)

        def head_grad(col):
            da, dg = head_bwd(heads_ref[:, col:col + HD], dhn_ref[:, col:col + HD], gh_ref[:, col:col + HD])
            dgh_ref[:, col:col + HD] += dg
            return da

        vg, dvg_dv = _gelu_with_grad(p_ref[:, v0:v0 + DS])
        vhat, rstd = _layer_norm_stats(vg)
        vnb = (vhat * lng_ref[...] + lnb_ref[...]).astype(BF16)
        low = _tri_mask(True)
        ones = jnp.ones((HALO, HD), BF16)
        for h in range(NSH):
            w_h = ws_ref[h]
            wt = jnp.where(low, w_h, 0.0).astype(BF16)
            bcol = bst_ref[:, h:h + 1]
            da = head_grad(h * HD)
            ug, dug_du = _gelu_with_grad(p_ref[:, u0 + h * HD:u0 + (h + 1) * HD])
            dws = jnp.zeros((CHUNK, CHUNK), F32)
            dbs = jnp.zeros((HALO, CHUNK), F32)
            mixed_parts = []
            for c in range(nck):
                rows = slice(c * CHUNK, (c + 1) * CHUNK)
                blk = vnb[rows, h * HD:(h + 1) * HD]
                mixed_parts.append(jnp.dot(wt, blk, preferred_element_type=F32) + bcol)
                dmb = (da[rows] * ug[rows]).astype(BF16)
                dws = dws + lax.dot_general(dmb, blk, (((1,), (1,)), ((), ())), preferred_element_type=F32)
                dbs = dbs + lax.dot_general(ones, dmb, (((1,), (1,)), ((), ())), preferred_element_type=F32)
                dvn_ref[c * CHUNK:(c + 1) * CHUNK, h * HD:(h + 1) * HD] = lax.dot_general(
                    wt, dmb, (((0,), (0,)), ((), ())), preferred_element_type=F32)
            mixed = mixed_parts[0] if nck == 1 else jnp.concatenate(mixed_parts, axis=0)
            dp_ref[:, u0 + h * HD:u0 + (h + 1) * HD] = ((da * mixed) * dug_du).astype(BF16)
            dws_ref[h] += jnp.where(low, dws, 0.0)
            dbs_ref[h] += dbs
        dvn = dvn_ref[...]
        dlng_ref[...] += jnp.sum(dvn * vhat, axis=0, keepdims=True)
        dlnb_ref[...] += jnp.sum(dvn, axis=0, keepdims=True)
        dvh = dvn * lng_ref[...]
        dvg = rstd * (dvh - jnp.mean(dvh, axis=-1, keepdims=True)
                      - vhat * jnp.mean(dvh * vhat, axis=-1, keepdims=True))
        dp_ref[:, v0:v0 + DS] = (dvg * dvg_dv).astype(BF16)

        dc = jnp.concatenate([head_grad(DS + h * HD) for h in range(NCH)], axis=1)
        dc_nx = jnp.concatenate(
            [head_bwd(heads_nx_ref[:, h * HD:(h + 1) * HD], dhn_nx_ref[:, h * HD:(h + 1) * HD],
                      gh_ref[:, DS + h * HD:DS + (h + 1) * HD])[0] for h in range(NCH)], axis=1)
        bg = p_ref[:, b0:b0 + DC]
        cg = p_ref[:, c0:c0 + DC]
        xin = p_ref[:, x0:x0 + DC]
        dp_ref[:, b0:b0 + DC] = (dc * ycv_ref[...]).astype(BF16)
        dyv = dc * bg
        buf_ref[0:tr, :] = dyv
        buf_ref[tr:tr + HALO, :] = jnp.where(i < n - 1, dc_nx * b_nx_ref[...], 0.0)
        sh1 = buf_ref[1:1 + tr, :]
        sh0 = buf_ref[2:2 + tr, :]
        dxc = cw_ref[2:3, :] * dyv + cw_ref[1:2, :] * sh1 + cw_ref[0:1, :] * sh0
        xc = cg * xin
        dp_ref[:, c0:c0 + DC] = (dxc * xin).astype(BF16)
        dp_ref[:, x0:x0 + DC] = (dxc * cg).astype(BF16)
        dcw_ref[0:1, :] += jnp.sum(sh0 * xc, axis=0, keepdims=True)
        dcw_ref[1:2, :] += jnp.sum(sh1 * xc, axis=0, keepdims=True)
        dcw_ref[2:3, :] += jnp.sum(dyv * xc, axis=0, keepdims=True)

        for h in range(NMH):
            do = head_grad(DS + DC + h * HD).astype(BF16)
            qh = (p_ref[:, q0 + h * HD:q0 + (h + 1) * HD] * SCALE).astype(BF16)
            kh = kv_ref[:, h * HD:(h + 1) * HD].astype(BF16)
            vh = kv_ref[:, DM + h * HD:DM + (h + 1) * HD].astype(BF16)
            p = _softmax_rows(qh, kh)
            dpr = lax.dot_general(do, vh, (((1,), (1,)), ((), ())), preferred_element_type=F32)
            ds = (p * (dpr - jnp.sum(dpr * p, axis=-1, keepdims=True))).astype(BF16)
            dp_ref[:, q0 + h * HD:q0 + (h + 1) * HD] = (
                jnp.dot(ds, kh, preferred_element_type=F32) * SCALE).astype(BF16)
            dkv_ref[:, h * HD:(h + 1) * HD] += lax.dot_general(
                ds, qh, (((0,), (0,)), ((), ())), preferred_element_type=F32)
            dkv_ref[:, DM + h * HD:DM + (h + 1) * HD] += lax.dot_general(
                p.astype(BF16), do, (((0,), (0,)), ((), ())), preferred_element_type=F32)

    full = lambda shape: pl.BlockSpec(shape, lambda i: (0,) * len(shape))
    row = lambda c: pl.BlockSpec((tr, c), lambda i: (i, 0))
    nxt = lambda col: pl.BlockSpec((HALO, DC), lambda i: (jnp.minimum((i + 1) * hb, last_hb), col))
    return pl.pallas_call(
        body, name=name, grid=(n,),
        in_specs=[row(D), row(D), row(DIN), row(DC), nxt(DS // DC), nxt(DS // DC), nxt(b0 // DC),
                  full((NMEM, 2 * DM)), full((NSH, CHUNK, CHUNK)), full((CHUNK, NSH)),
                  full((1, DS)), full((1, DS)), full((3, DC)), full((1, D)), ANY],
        out_specs=[row(DIN), full((NMEM, 2 * DM)), full((NSH, CHUNK, CHUNK)), full((NSH, HALO, CHUNK)),
                   full((1, DS)), full((1, DS)), full((HALO, DC)), full((1, D))],
        out_shape=[jax.ShapeDtypeStruct((S, DIN), BF16), jax.ShapeDtypeStruct((NMEM, 2 * DM), F32),
                   jax.ShapeDtypeStruct((NSH, CHUNK, CHUNK), F32), jax.ShapeDtypeStruct((NSH, HALO, CHUNK), F32),
                   jax.ShapeDtypeStruct((1, DS), F32), jax.ShapeDtypeStruct((1, DS), F32),
                   jax.ShapeDtypeStruct((HALO, DC), F32), jax.ShapeDtypeStruct((1, D), F32)],
        scratch_shapes=[pltpu.VMEM((tr + HALO, DC), F32), pltpu.VMEM((tr, DS), F32)],
        compiler_params=_cp(("arbitrary",), VMEM_MB),
    )(dhn, heads, proj, ycv, dhn, heads, proj, kv, w_s, bs_t, ln_g, ln_b, conv_w, g_head, after)


def _place():
    x, y, c = lax.axis_index("x"), lax.axis_index("y"), lax.axis_index("c")
    chips = [(1 - x, y), (x, 1 - y), (1 - x, 1 - y)]
    return x, y, c, chips


ANY = pl.BlockSpec(memory_space=pl.ANY)


HBM = pl.BlockSpec(memory_space=pltpu.HBM)
SEM = pl.BlockSpec(memory_space=pltpu.SEMAPHORE)
EFFECT = pltpu.SideEffectType.DATAFLOW_SIDE_EFFECTING
N_PEER_CHIPS = 3
N_NEIGHBOUR_CHIPS = 2
CONV_PAD = (32, 256)


def _in_hbm(a):
    return pltpu.with_memory_space_constraint(a, pltpu.HBM)


def _allgather_start(bufs, forwards, after, collective_id, *, name):
    arrs = list(bufs) + list(forwards)
    nw, nb = len(arrs), len(bufs)

    def body(*refs):
        ins, send, recv = refs[:nw], refs[nw + 1:2 * nw + 1], refs[2 * nw + 1:3 * nw + 1]
        token = refs[4 * nw + 1]
        x, y, c, chips = _place()
        s = 2 * x + y
        slots = [2 * cx + cy for cx, cy in chips]
        _handshake([(cx, cy, c) for cx, cy in chips[:N_NEIGHBOUR_CHIPS]])
        for w in range(nb, nw):
            q = arrs[w].shape[1] // 4
            for j in range(N_NEIGHBOUR_CHIPS):
                rows = ins[w].at[slots[j], pl.ds(c * 2 * q + j * q, q)]
                pltpu.make_async_remote_copy(src_ref=rows, dst_ref=rows, send_sem=send[w], recv_sem=recv[w],
                                             device_id=(*chips[1 - j], c), device_id_type=MESH).start()
        for w in range(nb):
            hr = arrs[w].shape[1] // 2
            rows = ins[w].at[s, pl.ds(c * hr, hr)]
            for cx, cy in chips[:N_NEIGHBOUR_CHIPS]:
                pltpu.make_async_remote_copy(src_ref=rows, dst_ref=rows, send_sem=send[w], recv_sem=recv[w],
                                             device_id=(cx, cy, c), device_id_type=MESH).start()
        token[...] = jnp.zeros_like(token)

    res = pl.pallas_call(
        body, name=name,
        in_specs=[HBM] * nw + [ANY],
        out_specs=[SEM] * (2 * nw) + [HBM] * nw + [pl.BlockSpec(memory_space=pltpu.VMEM)],
        out_shape=[pltpu.SemaphoreType.DMA(())] * (2 * nw) + [pltpu.HBM(a.shape, a.dtype) for a in arrs]
        + [jax.ShapeDtypeStruct((8, 128), F32)],
        input_output_aliases={w: 2 * nw + w for w in range(nw)},
        compiler_params=pltpu.CompilerParams(has_side_effects=EFFECT, collective_id=collective_id),
    )(*[_in_hbm(a) for a in arrs], after)
    return res[:nw], res[nw:2 * nw], res[2 * nw:3 * nw], res[3 * nw]


def _handshake(peers):
    barrier = pltpu.get_barrier_semaphore()
    for peer in peers:
        pl.semaphore_signal(barrier, inc=1, device_id=peer, device_id_type=MESH)
    pl.semaphore_wait(barrier, len(peers))


def _scatter_start(parts, bufs, collective_id, *, name):
    nw = len(parts)

    def body(*refs):
        src, dst = refs[:nw], refs[nw:2 * nw]
        send, recv = refs[2 * nw:3 * nw], refs[3 * nw:4 * nw]
        token = refs[6 * nw]
        x, y, c, chips = _place()
        s = 2 * x + y
        _handshake([(cx, cy, c) for cx, cy in chips])
        for w in range(nw):
            for cx, cy in chips:
                pltpu.make_async_remote_copy(src_ref=src[w].at[2 * cx + cy], dst_ref=dst[w].at[s], send_sem=send[w],
                                             recv_sem=recv[w], device_id=(cx, cy, c), device_id_type=MESH).start()
        token[...] = jnp.zeros_like(token)

    res = pl.pallas_call(
        body, name=name,
        in_specs=[HBM] * (2 * nw),
        out_specs=[SEM] * (2 * nw) + [HBM] * (2 * nw) + [pl.BlockSpec(memory_space=pltpu.VMEM)],
        out_shape=[pltpu.SemaphoreType.DMA(())] * (2 * nw) + [pltpu.HBM(a.shape, a.dtype) for a in parts + bufs]
        + [jax.ShapeDtypeStruct((8, 128), F32)],
        input_output_aliases={k: 2 * nw + k for k in range(2 * nw)},
        compiler_params=pltpu.CompilerParams(has_side_effects=EFFECT, collective_id=collective_id),
    )(*[_in_hbm(a) for a in parts + bufs])
    return res[:nw], res[nw:2 * nw], res[2 * nw:3 * nw], res[3 * nw:4 * nw], res[4 * nw]


def _sibling_start(srcs, whole, collective_id, *, name):
    nw = len(srcs)
    lands = [lax.empty((a.shape[0], a.shape[1] if whole else a.shape[1] // 2, a.shape[2]), a.dtype) for a in srcs]

    def body(*refs):
        src, land = refs[:nw], refs[nw:2 * nw]
        send, recv = refs[2 * nw:3 * nw], refs[3 * nw:4 * nw]
        token = refs[6 * nw]
        x, y, c, _ = _place()
        _handshake([(x, y, 1 - c)])
        for w in range(nw):
            hr = srcs[w].shape[1] // 2
            rows = src[w] if whole else src[w].at[:, pl.ds((1 - c) * hr, hr)]
            pltpu.make_async_remote_copy(src_ref=rows, dst_ref=land[w], send_sem=send[w], recv_sem=recv[w],
                                         device_id=(x, y, 1 - c), device_id_type=MESH).start()
        token[...] = jnp.zeros_like(token)

    res = pl.pallas_call(
        body, name=name,
        in_specs=[HBM] * (2 * nw),
        out_specs=[SEM] * (2 * nw) + [HBM] * (2 * nw) + [pl.BlockSpec(memory_space=pltpu.VMEM)],
        out_shape=[pltpu.SemaphoreType.DMA(())] * (2 * nw) + [pltpu.HBM(a.shape, a.dtype) for a in srcs + lands]
        + [jax.ShapeDtypeStruct((8, 128), F32)],
        input_output_aliases={k: 2 * nw + k for k in range(2 * nw)},
        compiler_params=pltpu.CompilerParams(has_side_effects=EFFECT, collective_id=collective_id),
    )(*[_in_hbm(a) for a in srcs + lands])
    return res[:nw], res[nw:2 * nw], res[2 * nw:3 * nw], res[3 * nw:4 * nw], res[4 * nw]


def _transfer_wait(sends, recvs, thru, sizes, after, *, name):
    n = len(sends)
    flat = [a for group in thru for a in group]

    def body(*refs):
        bufs = refs[:len(flat)]
        send = refs[len(flat):len(flat) + n]
        recv = refs[len(flat) + n:len(flat) + 2 * n]
        token = refs[2 * len(flat) + 2 * n + 1]
        token[...] = jnp.zeros_like(token)
        x, y, c, _ = _place()
        pos = 0
        for k in range(n):
            slots, rows = sizes[k]
            region = bufs[pos].at[pl.ds(0, slots), pl.ds(0, rows)]
            pos += len(thru[k])
            cp = pltpu.make_async_remote_copy(src_ref=region, dst_ref=region, send_sem=send[k], recv_sem=recv[k],
                                              device_id=(x, y, 1 - c), device_id_type=MESH)
            cp.wait_send()
            cp.wait_recv()

    res = pl.pallas_call(
        body, name=name,
        in_specs=[HBM] * len(flat) + [SEM] * (2 * n) + [pl.BlockSpec(memory_space=pl.ANY)],
        out_specs=[HBM] * len(flat) + [pl.BlockSpec(memory_space=pltpu.VMEM)],
        out_shape=[pltpu.HBM(a.shape, a.dtype) for a in flat] + [jax.ShapeDtypeStruct((8, 128), F32)],
        input_output_aliases={k: k for k in range(len(flat))},
        compiler_params=pltpu.CompilerParams(has_side_effects=EFFECT),
    )(*flat, *sends, *recvs, after)
    out, pos = [], 0
    for group in thru:
        out.append(res[pos:pos + len(group)])
        pos += len(group)
    return out, res[len(flat)]


def _forward_halves(bufs, which, after, *, name):
    nw = len(bufs)
    n = len(which)

    def body(*refs):
        outs = refs[nw + 1:2 * nw + 1]
        send, recv = refs[2 * nw + 1:]
        x, y, c, chips = _place()
        me, sibling = (x, y, c), (x, y, 1 - c)

        def d2d(w, t, half, to):
            cx, cy = chips[which[t]]
            hr = bufs[w].shape[1] // 2
            rows = outs[w].at[2 * cx + cy, pl.ds(half * hr, hr)]
            return pltpu.make_async_remote_copy(src_ref=rows, dst_ref=rows, send_sem=send.at[n * w + t],
                                                recv_sem=recv.at[n * w + t], device_id=to, device_id_type=MESH)

        passed = [d2d(w, t, c, sibling) for w in range(nw) for t in range(n)]
        for cp in passed:
            cp.start()
        for w in range(nw):
            for t in range(n):
                d2d(w, t, 1 - c, me).wait_recv()
        for cp in passed:
            cp.wait_send()

    return pl.pallas_call(
        body, name=name,
        in_specs=[ANY] * (nw + 1), out_specs=[ANY] * nw,
        out_shape=[jax.ShapeDtypeStruct(a.shape, a.dtype) for a in bufs],
        input_output_aliases={w: w for w in range(nw)},
        scratch_shapes=[pltpu.SemaphoreType.DMA((n * nw,)), pltpu.SemaphoreType.DMA((n * nw,))],
    )(*bufs, after)


def _allreduce_small(p, after, *, name):
    R = p.shape[0]
    hr = R // 2

    def body(p_ref, _after_ref, out_ref, sib_ref, sum_ref, gat_ref, tot_ref, send, recv):
        x, y, c, chips = _place()
        s = 2 * x + y
        sibling = (x, y, 1 - c)
        rows = pl.ds(pl.multiple_of(c * hr, 8), hr)
        swap = pltpu.make_async_remote_copy(src_ref=p_ref, dst_ref=sib_ref, send_sem=send.at[0], recv_sem=recv.at[0],
                                            device_id=sibling, device_id_type=MESH)
        swap.start()
        swap.wait()
        sum_ref[...] = p_ref[...] + sib_ref[...]
        gat_ref[s] = sum_ref[rows, :]
        cps = [pltpu.make_async_remote_copy(src_ref=sum_ref.at[rows], dst_ref=gat_ref.at[s], send_sem=send.at[1 + j],
                                            recv_sem=recv.at[1 + j], device_id=(cx, cy, c), device_id_type=MESH)
               for j, (cx, cy) in enumerate(chips)]
        for cp in cps:
            cp.start()
        for cp in cps:
            cp.wait()
        tot_ref[...] = ((gat_ref[0] + gat_ref[1]) + gat_ref[2]) + gat_ref[3]
        out_ref[rows, :] = tot_ref[...]
        share = pltpu.make_async_remote_copy(src_ref=tot_ref, dst_ref=out_ref.at[rows], send_sem=send.at[4],
                                             recv_sem=recv.at[4], device_id=sibling, device_id_type=MESH)
        share.start()
        share.wait_send()
        other = out_ref.at[pl.ds(pl.multiple_of((1 - c) * hr, 8), hr)]
        pltpu.make_async_remote_copy(src_ref=other, dst_ref=other, send_sem=send.at[4], recv_sem=recv.at[4],
                                     device_id=(x, y, c), device_id_type=MESH).wait_recv()

    vmem = pl.BlockSpec(memory_space=pltpu.VMEM)
    return pl.pallas_call(
        body, name=name, in_specs=[vmem, ANY], out_specs=vmem,
        out_shape=jax.ShapeDtypeStruct((R, 128), F32),
        scratch_shapes=[pltpu.VMEM((R, 128), F32), pltpu.VMEM((R, 128), F32), pltpu.VMEM((NCHIP, hr, 128), F32),
                        pltpu.VMEM((hr, 128), F32), pltpu.SemaphoreType.DMA((5,)), pltpu.SemaphoreType.DMA((5,))],
    )(p, after)


def _select_half_bf16(g, half, add, slot, *, name):
    _, R, C = g.shape
    hr = R // 2
    tr = _pick_rows(hr, 16)
    nb = hr // tr
    sel = jnp.concatenate([jnp.reshape(half, (1,)).astype(jnp.int32), slot])

    def body(s_ref, g_ref, a_ref, o_ref, own_ref):
        val = (g_ref[...].astype(F32) + a_ref[...].astype(F32)).astype(BF16)
        o_ref[...] = val

        @pl.when(pl.program_id(1) == s_ref[1])
        def _():
            own_ref[...] = val

    g_spec = pl.BlockSpec((None, tr, C), lambda i, j, s: (j, s[0] * nb + i, 0))
    o_spec = pl.BlockSpec((None, tr, C), lambda i, j, s: (j, i, 0))
    own_spec = pl.BlockSpec((None, tr, C), lambda i, j, s: (s[1], i, 0))
    shape = jax.ShapeDtypeStruct((NCHIP, hr, C), BF16)
    return pl.pallas_call(
        body, name=name,
        grid_spec=pltpu.PrefetchScalarGridSpec(
            num_scalar_prefetch=1, grid=(nb, NCHIP), in_specs=[g_spec, o_spec], out_specs=[o_spec, own_spec]),
        out_shape=[shape, shape],
        compiler_params=_cp(("parallel", "arbitrary"), VMEM_MB),
    )(sel, g, add)


def _adamw_math(w, g, m, v):
    m = ADAM_B1 * m + (1.0 - ADAM_B1) * g
    v = ADAM_B2 * v + (1.0 - ADAM_B2) * (g * g)
    m_hat = m / (1.0 - ADAM_B1 ** ADAM_STEP)
    v_hat = v / (1.0 - ADAM_B2 ** ADAM_STEP)
    delta = -ADAM_LR * (m_hat / (jnp.sqrt(v_hat) + ADAM_EPS) + ADAM_WD * w)
    return delta, m, v


def _adamw(w, g_mine, g_sib, m, v, core, *, name):
    R, C = w.shape
    hr = R // 2
    tr = _pick_rows(hr, 16)
    nb = hr // tr
    row = pl.BlockSpec((tr, C), lambda hh, i, c: (hh * nb + i, 0))
    mine = pl.BlockSpec((NCHIP, tr, C), lambda hh, i, c: (0, jnp.where(hh == c[0], i, 0), 0))
    sibs = pl.BlockSpec((NCHIP, tr, C), lambda hh, i, c: (0, jnp.where(hh == c[0], 0, i), 0))

    def slot_sum(ref):
        acc = ref[0].astype(F32) + ref[1].astype(F32)
        for j in range(2, NCHIP):
            acc = acc + ref[j].astype(F32)
        return acc

    def body(c_ref, w_ref, gm_ref, gs_ref, m_ref, v_ref, go_ref, d_ref, mo_ref, vo_ref):
        gv = jnp.where(pl.program_id(0) == c_ref[0], slot_sum(gm_ref), slot_sum(gs_ref))
        d, mn, vn = _adamw_math(w_ref[...], gv, m_ref[...], v_ref[...])
        go_ref[...] = gv
        d_ref[...] = d
        mo_ref[...] = mn
        vo_ref[...] = vn

    return pl.pallas_call(
        body, name=name,
        grid_spec=pltpu.PrefetchScalarGridSpec(
            num_scalar_prefetch=1, grid=(2, nb),
            in_specs=[row, mine, sibs, row, row], out_specs=[row] * 4),
        out_shape=[jax.ShapeDtypeStruct((R, C), F32)] * 4,
        compiler_params=_cp(("parallel", "parallel"), VMEM_MB),
    )(core, w, g_mine, g_sib, m, v)


def _adamw_small(ws, gs, ms, vs, *, name):
    n = len(ws)

    def body(*refs):
        w_r, g_r, m_r, v_r = refs[:n], refs[n:2 * n], refs[2 * n:3 * n], refs[3 * n:4 * n]
        d_r, mo_r, vo_r = refs[4 * n:5 * n], refs[5 * n:6 * n], refs[6 * n:7 * n]
        for k in range(n):
            d, mn, vn = _adamw_math(w_r[k][...], g_r[k][...], m_r[k][...], v_r[k][...])
            d_r[k][...] = d
            mo_r[k][...] = mn
            vo_r[k][...] = vn

    shapes = [jax.ShapeDtypeStruct(w.shape, F32) for w in ws]
    res = pl.pallas_call(body, name=name, out_shape=shapes * 3)(*ws, *gs, *ms, *vs)
    return res[:n], res[n:2 * n], res[2 * n:]


_PACK_ROWS = 8


def _pack(parts):
    rows = []
    for a in parts:
        flat = a.reshape(-1)
        n = -(-flat.shape[0] // (_PACK_ROWS * 128)) * (_PACK_ROWS * 128)
        rows.append(jnp.pad(flat, (0, n - flat.shape[0])).reshape(-1, 128))
    total = sum(r.shape[0] for r in rows)
    if total % 16:
        rows.append(jnp.zeros((16 - total % 16, 128), F32))
    return jnp.concatenate(rows, axis=0)


def _unpack(p, shapes):
    out, r = [], 0
    for shp in shapes:
        n = math.prod(shp)
        nr = -(-n // (_PACK_ROWS * 128)) * _PACK_ROWS
        out.append(p[r:r + nr].reshape(-1)[:n].reshape(shp))
        r += nr
    return out


def kernel(x, mem, g_mix, w_in, ln_v_g, ln_v_b, w_s, b_s, conv_w, g_mem, w_kv, g_head, w_o, g_ffn, w_ffn1, w_ffn2, g_final, loss_target, m_g_mix, m_w_in, m_ln_v_g, m_ln_v_b, m_w_s, m_b_s, m_conv_w, m_g_mem, m_w_kv, m_g_head, m_w_o, m_g_ffn, m_w_ffn1, m_w_ffn2, m_g_final, v_g_mix, v_w_in, v_ln_v_g, v_ln_v_b, v_w_s, v_b_s, v_conv_w, v_g_mem, v_w_kv, v_g_head, v_w_o, v_g_ffn, v_w_ffn1, v_w_ffn2, v_g_final):
    sds = jax.ShapeDtypeStruct
    xi, yi = lax.axis_index("x"), lax.axis_index("y")
    shard = 2 * xi + yi
    x2d, mem2d, tgt = x[0], mem[0], loss_target[0]
    ws3, bs2 = w_s[0], b_s[0]
    g_final2 = g_final.reshape(1, D)
    dff4 = DFF // NCHIP
    din4 = DIN // NCHIP
    dcv4 = DC // NCHIP

    big = [w_in[0].T, w_kv[0], w_o[0], w_ffn1[0], w_ffn2[0]]
    big_names = ["w_in", "w_kv", "w_o", "w_ffn1", "w_ffn2"]
    slot = jnp.reshape(shard, (1,)).astype(jnp.int32)
    core = jnp.reshape(lax.axis_index("c"), (1,)).astype(jnp.int32)
    conv_pad = jnp.pad(conv_w[0], ((0, CONV_PAD[0] - 3), (0, CONV_PAD[1] - dcv4)))
    conv_slots = lax.dynamic_update_slice(jnp.zeros((NCHIP,) + CONV_PAD, F32), conv_pad[None], (shard, 0, 0))

    gather_ids = {"in": 16, "kvo": 17, "ffn1": 18, "ffn2": 19, "ffn2d": 20}

    def gather_start(bufs, after, nm, forwards=()):
        return _allgather_start(bufs, forwards, after, gather_ids[nm], name="ag_start_" + nm)

    def gather_wait(state, idx, after, nm):
        send, recv, bufs, _ = state
        got, token = _transfer_wait([send[k] for k in idx], [recv[k] for k in idx], [[bufs[k]] for k in idx],
                                    [(N_NEIGHBOUR_CHIPS, bufs[k].shape[1] // 2) for k in idx], after, name="ag_wait_" + nm)
        return [g[0] for g in got], token

    cast = lambda k, after: _cast_into_slot(big[k], slot, after, name="cast_" + big_names[k])
    ag_in = gather_start([cast(0, slot), conv_slots], slot, "in")
    bs_t = bs2.T

    h = _rms_fwd(x2d, g_mix, name="rms_mix", after=[ag_in[3]])
    mem_n = _rms_fwd(mem2d, g_mem, name="rms_mem", after=[h])
    kvo_b = [cast(1, mem_n)]
    kvo_b.append(cast(2, kvo_b[0]))
    w1_b = cast(3, kvo_b[1])
    w2_b = cast(4, w1_b)
    NEAR, FAR = [0, 1], [2]

    def diagonal_wait(state, ks, after, nm):
        send, recv, bufs, _ = state
        got, token = _transfer_wait([send[k] for k in ks], [recv[k] for k in ks], [[bufs[k]] for k in ks],
                                    [(1, bufs[k].shape[1] // 2) for k in ks], after, name="ag_waitd_" + nm)
        return [g[0] for g in got], token

    got_in, tok = gather_wait(ag_in, [0, 1], w2_b, "in")
    ag_kvo = gather_start(kvo_b, tok, "kvo", forwards=got_in)
    in_n = _forward_halves(ag_kvo[2][2:], NEAR, ag_kvo[3], name="ag_fwdn_in")
    ag_kvo = (ag_kvo[0], ag_kvo[1], list(ag_kvo[2][:2]) + list(in_n), ag_kvo[3])
    in_d, tok = diagonal_wait(ag_kvo, [2, 3], ag_kvo[3], "in")
    win4, conv4 = _forward_halves(in_d, FAR, tok, name="ag_fwdd_in")
    w_in_t = win4.reshape(DIN, D)
    conv_full = conv4[:, :3, :dcv4].transpose(1, 0, 2).reshape(3, DC)

    proj_w = lambda tn, tk: pl.BlockSpec((tn, tk), lambda j, i, k, s: (s[j], k))
    proj_cols = lambda tm, tn: [pl.BlockSpec((tm, tn), lambda j, i, k, s: (i, s[j]))]
    proj_half = lambda which, into, after: _matmul(
        h, w_in_t, name="mm_proj_%d" % which, tb=True, M=S, N=DIN // 2, K=D, tn=DIN // 2, b_spec=proj_w,
        out_specs=proj_cols, outs=[sds((S, DIN), F32)], slots=jnp.full((1,), which, jnp.int32), into=into,
        after=after)[0]
    proj = proj_half(0, [], [ag_kvo[3]])
    got_kvo, tok = gather_wait(ag_kvo, [0, 1], proj, "kvo")
    ag_w1 = gather_start([w1_b], tok, "ffn1", forwards=got_kvo)
    kvo_n = _forward_halves(ag_w1[2][1:], NEAR, ag_w1[3], name="ag_fwdn_kvo")
    ag_w1 = (ag_w1[0], ag_w1[1], [ag_w1[2][0]] + list(kvo_n), ag_w1[3])
    proj = proj_half(1, [proj], list(kvo_n))
    kvo_d, tok = diagonal_wait(ag_w1, [1, 2], proj, "kvo")
    wkv4, wo4 = _forward_halves(kvo_d, FAR, tok, name="ag_fwdd_kvo")
    w_kv_full = wkv4.reshape(D, 2 * DM)
    w_o_full = wo4.reshape(D, D)
    (kv,) = _matmul(mem_n, w_kv_full, name="mm_kv", M=NMEM, N=2 * DM, K=D, outs=[sds((NMEM, 2 * DM), F32)])
    heads, hn, ycv = _mix_fwd(proj, kv, ws3, bs_t, ln_v_g, ln_v_b, conv_full, g_head, name="mix_fwd")
    def residual_and_norm(acc, res, g):
        x2v = acc + res
        r = lax.rsqrt(jnp.mean(x2v * x2v, axis=-1, keepdims=True) + EPS)
        return x2v, (x2v * r) * g

    row_vec = lambda tm, tn: pl.BlockSpec((1, tn), lambda j, i, k, *s: (0, j))
    x2, h2 = _matmul(hn, w_o_full, name="mm_wo", M=S, N=D, K=D, tn=D, n_split=1, epi=residual_and_norm,
                     outs=[sds((S, D), F32), sds((S, D), BF16)], extras=[(x2d, _tile_spec()), (g_ffn, row_vec)])
    near = jnp.stack([shard, 2 * (1 - xi) + yi, 2 * xi + (1 - yi)]).astype(jnp.int32)
    far = jnp.reshape(2 * (1 - xi) + (1 - yi), (1,)).astype(jnp.int32)

    w1_shard = lambda tn, tk: pl.BlockSpec((None, tk, tn), lambda j, i, k, s: (s[j], k, 0))
    act_cols = lambda tm, tn: [pl.BlockSpec((tm, tn), lambda j, i, k, s: (i, s[j]))] * 2

    def relu2(acc):
        r = jnp.maximum(acc, 0.0)
        return r * r, 2.0 * r

    got_w1, tok = gather_wait(ag_w1, [0], h2, "ffn1")
    ag_w2 = gather_start([w2_b], tok, "ffn2", forwards=got_w1)
    (w1n,) = _forward_halves([ag_w2[2][1]], NEAR, ag_w2[3], name="ag_fwdn_ffn1")
    ag_w2 = (ag_w2[0], ag_w2[1], [ag_w2[2][0], w1n], ag_w2[3])
    act, dact_df = _matmul(h2, w1n, name="mm_ffn1_near", M=S, N=3 * dff4, K=D, tm=2 * TM, tn=dff4, b_spec=w1_shard,
                           out_specs=act_cols, outs=[sds((S, DFF), BF16)] * 2, epi=relu2, slots=near)
    w1d, tok = diagonal_wait(ag_w2, [1], act, "ffn1")
    (w14,) = _forward_halves(w1d, FAR, tok, name="ag_fwdd_ffn1")
    act, dact_df = _matmul(h2, w14, name="mm_ffn1_far", M=S, N=dff4, K=D, tm=2 * TM, tn=dff4, b_spec=w1_shard,
                           out_specs=act_cols, outs=[sds((S, DFF), BF16)] * 2, epi=relu2, slots=far,
                           into=[act, dact_df])

    act_shard = lambda tm, tk: pl.BlockSpec((tm, tk), lambda j, i, k, s: (i, s[k]))
    w2_shard = lambda tn, tk: pl.BlockSpec((None, tk, tn), lambda j, i, k, s: (s[k], 0, j))
    got_w2, tok = gather_wait(ag_w2, [0], act, "ffn2")
    ag_w2d = gather_start([], tok, "ffn2d", forwards=got_w2)
    (w2n,) = _forward_halves(ag_w2d[2], NEAR, ag_w2d[3], name="ag_fwdn_ffn2")
    ag_w2d = (ag_w2d[0], ag_w2d[1], [w2n], ag_w2d[3])
    (x3,) = _matmul(act, w2n, name="mm_ffn2_near", M=S, N=D, K=3 * dff4, tm=2 * TM, tk=dff4,
                    a_spec=act_shard, b_spec=w2_shard, outs=[sds((S, D), F32)], epi=lambda acc, res: (acc + res,),
                    extras=[(x2, _tile_spec())], slots=near)
    w2d, tok = diagonal_wait(ag_w2d, [0], x3, "ffn2")
    (w24,) = _forward_halves(w2d, FAR, tok, name="ag_fwdd_ffn2")
    (x3,) = _matmul(act, w24, name="mm_ffn2_far", M=S, N=D, K=dff4, tm=2 * TM, tk=dff4, a_spec=act_shard,
                    b_spec=w2_shard, outs=[sds((S, D), F32)], epi=lambda acc, res: (acc + res,),
                    extras=[(x3, _tile_spec())], slots=far)
    w2_full = w24.reshape(DFF, D)

    ci = lax.axis_index("c")

    def rs_sibling(g4, nm):
        return _sibling_start([g4], False, 1 + big_names.index(nm), name="rs_sib_" + nm)

    def rs_chips(state, after, nm):
        send, recv, g4, land, _ = state
        (((land_, g4_),), _) = _transfer_wait(send, recv, [[land[0], g4[0]]], [(NCHIP, land[0].shape[1])], after,
                                             name="rs_sibwait_" + nm)
        part, buf = _select_half_bf16(g4_, ci, land_, slot, name="rs_add_" + nm)
        return _scatter_start([part], [buf], 1 + 2 * len(big_names) + big_names.index(nm), name="rs_start_" + nm)

    def dw_half(a, b, nm, *, by_rows, hr, cols, which, land, after):
        tile = lambda tm, tn: pl.BlockSpec(
            (None, tm, tn), (lambda j, i, k, s: (i, 0, j)) if by_rows else (lambda j, i, k, s: (j, 0, 0)))
        a_half = lambda tm, tk: pl.BlockSpec(
            (tk, tm), (lambda j, i, k, s: (k, 2 * i + s[0])) if by_rows else (lambda j, i, k, s: (k, s[0])))
        (out,) = _matmul(a, b, name=nm, ta=True, M=NCHIP * hr if by_rows else hr, N=cols if by_rows else NCHIP * cols,
                         K=S, tm=hr, tn=cols, a_spec=a_half, out_specs=lambda tm, tn: [tile(tm, tn)],
                         outs=[sds((NCHIP, hr, cols), BF16)], slots=jnp.reshape(which, (1,)).astype(jnp.int32),
                         epi=None if land is None else (lambda acc, other: (acc + other.astype(F32),)),
                         extras=[] if land is None else [(land, tile)], after=after)
        return out

    def rs_sibling_half(half, nm):
        return _sibling_start([half], True, 1 + big_names.index(nm), name="rs_sib_" + nm)

    def rs_chips_fused(state, grad_half, after, nm):
        send, recv, mine, land, _ = state
        (((land_, _),), tok) = _transfer_wait(send, recv, [[land[0], mine[0]]], [(NCHIP, land[0].shape[1])], after,
                                             name="rs_sibwait_" + nm)
        part = grad_half(land_, [tok])
        buf = _own_slot(part, slot, name="rs_own_" + nm)
        return _scatter_start([part], [buf], 1 + 2 * len(big_names) + big_names.index(nm), name="rs_start_" + nm)

    def rs_end(state, after, nm):
        send, recv, parts, bufs, _ = state
        (((buf, _),), _) = _transfer_wait(send, recv, [[bufs[0], parts[0]]], [(N_PEER_CHIPS, bufs[0].shape[1])], after,
                                          name="rs_wait_" + nm)
        return _sibling_start([buf], True, 1 + len(big_names) + big_names.index(nm), name="rs_share_" + nm)

    big_m = [m_w_in[0].T, m_w_kv[0], m_w_o[0], m_w_ffn1[0], m_w_ffn2[0]]
    big_v = [v_w_in[0].T, v_w_kv[0], v_w_o[0], v_w_ffn1[0], v_w_ffn2[0]]
    big_out = {}

    def rs_finish(k, state, after):
        send, recv, mine, land, _ = state
        nm = big_names[k]
        (((land_, mine_),), _) = _transfer_wait(send, recv, [[land[0], mine[0]]], [(NCHIP, land[0].shape[1])], after,
                                               name="rs_sharewait_" + nm)
        big_out[nm] = _adamw(big[k], mine_, land_, big_m[k], big_v[k], core, name="adamw_" + nm)
        return big_out[nm][1]

    dx3, dx3b, dg_final, loss11 = _loss_bwd(x3, g_final2, tgt, name="loss_bwd")
    dw2_half = lambda which, land, after, nm: dw_half(
        act, dx3b, nm, by_rows=True, hr=dff4 // 2, cols=D, which=which, land=land, after=after)
    sib_w2 = rs_sibling_half(dw2_half(1 - ci, None, [], "mm_dw2_sib"), "w_ffn2")
    (dfb,) = _matmul(dx3b, w2_full, name="mm_dact", tb=True, M=S, N=DFF, K=D, tm=2 * TM, tn=dff4, outs=[sds((S, DFF), BF16)],
                     epi=lambda acc, g: (acc * g.astype(F32),), extras=[(dact_df, _tile_spec())],
                     after=[sib_w2[4]])
    rs_w2 = rs_chips_fused(sib_w2, lambda land, after: dw2_half(ci, land, after, "mm_dw2_own"), dfb, "w_ffn2")

    dw1_half = lambda which, land, after, nm: dw_half(
        h2, dfb, nm, by_rows=False, hr=D // 2, cols=dff4, which=which, land=land, after=after)
    sib_w1 = rs_sibling_half(dw1_half(1 - ci, None, [rs_w2[4]], "mm_dw1_sib"), "w_ffn1")

    def w1_rows(tn, tk):
        kb = dff4 // tk
        return pl.BlockSpec((None, tn, tk), lambda j, i, k: (k // kb, j, k % kb))

    (dh2,) = _matmul(dfb, w14, name="mm_dh2", tb=True, M=S, N=D, K=DFF, tm=2 * TM, b_spec=w1_rows,
                     outs=[sds((S, D), F32)], after=[sib_w1[4]])
    rs_w1 = rs_chips_fused(sib_w1, lambda land, after: dw1_half(ci, land, after, "mm_dw1_own"), dh2, "w_ffn1")
    dx2, dx2b, dg_ffn = _rms_bwd(dh2, x2, g_ffn, dx3, name="rms_ffn_bwd", after=[rs_w1[4]])
    dwo_half = lambda which, land, after, nm: dw_half(
        hn, dx2b, nm, by_rows=True, hr=D // NCHIP // 2, cols=D, which=which, land=land, after=after)
    sib_wo = rs_sibling_half(dwo_half(1 - ci, None, [], "mm_dwo_sib"), "w_o")
    (dhn,) = _matmul(dx2b, w_o_full, name="mm_dhn", tb=True, M=S, N=D, K=D, tm=2 * TM, outs=[sds((S, D), F32)],
                     after=[sib_wo[4]])
    rs_wo = rs_chips_fused(sib_wo, lambda land, after: dwo_half(ci, land, after, "mm_dwo_own"), dhn, "w_o")
    sh_w2 = rs_end(rs_w2, rs_wo[4], "w_ffn2")
    dproj, dkv, dws, dbs8, dlng, dlnb, dcw8, dgh = _mix_bwd(
        dhn, heads, proj, ycv, kv, ws3, bs_t, ln_v_g, ln_v_b, conv_full, g_head, sh_w2[4], name="mix_bwd")
    (dwin_t,) = _matmul(dproj, h, name="mm_dwin", ta=True, M=DIN, N=D, K=S, tm=DIN // 2, outs=[sds((DIN, D), BF16)])
    sib_win = rs_sibling(dwin_t.reshape(NCHIP, din4, D), "w_in")
    (dwkv,) = _matmul(mem_n, dkv, name="mm_dwkv", ta=True, M=D, N=2 * DM, K=NMEM, outs=[sds((D, 2 * DM), BF16)],
                      after=[sib_win[4]])
    sib_wkv = rs_sibling(dwkv.reshape(NCHIP, D // NCHIP, 2 * DM), "w_kv")
    sh_w1 = rs_end(rs_w1, sib_wkv[4], "w_ffn1")
    (dh,) = _matmul(dproj, w_in_t, name="mm_dh", M=S, N=D, K=DIN, tm=2 * TM, tk=DIN, outs=[sds((S, D), F32)],
                    after=[sh_w1[4]])
    rs_win = rs_chips(sib_win, dh, "w_in")
    rs_wkv = rs_chips(sib_wkv, rs_win[4], "w_kv")
    dx, dg_mix = _rms_bwd(dh, x2d, g_mix, dx2, name="rms_mix_bwd", want_bf=False, after=[rs_wkv[4]])
    (dmem_n,) = _matmul(dkv, w_kv_full, name="mm_dmem", tb=True, M=NMEM, N=D, K=2 * DM, outs=[sds((NMEM, D), F32)],
                        after=[dx])
    (dg_mem,) = _rms_bwd(dmem_n, mem2d, g_mem, None, name="rms_mem_bwd", want_dx=False)
    sh_wo = rs_end(rs_wo, dg_mem, "w_o")
    done = rs_finish(4, sh_w2, sh_wo[4])
    done = rs_finish(3, sh_w1, done)
    sh_win = rs_end(rs_win, done, "w_in")
    sh_wkv = rs_end(rs_wkv, sh_win[4], "w_kv")
    done = rs_finish(2, sh_wo, sh_wkv[4])
    done = rs_finish(0, sh_win, done)
    done = rs_finish(1, sh_wkv, done)

    small_names = ["g_mix", "ln_v_g", "ln_v_b", "w_s", "b_s", "conv_w", "g_mem", "g_head", "g_ffn", "g_final"]
    small_part = [dg_mix, dlng, dlnb, dws, dbs8[:, 0, :], dcw8[:3], dg_mem, dgh, dg_ffn, dg_final, loss11]
    small_shapes = [(1, D), (1, DS), (1, DS), (NSH, CHUNK, CHUNK), (NSH, CHUNK), (3, DC), (1, D), (1, D), (1, D), (1, D),
                    (1, 1)]
    total = _allreduce_small(_pack(small_part), done, name="allreduce_small")
    small_g = _unpack(total, small_shapes)
    loss = small_g.pop()[0, 0]
    small_g[5] = lax.dynamic_slice(small_g[5], (0, shard * dcv4), (3, dcv4))
    small_w = [g_mix, ln_v_g, ln_v_b, ws3, bs2, conv_w[0], g_mem, g_head, g_ffn, g_final2]
    small_m = [m_g_mix, m_ln_v_g, m_ln_v_b, m_w_s[0], m_b_s[0], m_conv_w[0], m_g_mem, m_g_head, m_g_ffn,
               m_g_final.reshape(1, D)]
    small_v = [v_g_mix, v_ln_v_g, v_ln_v_b, v_w_s[0], v_b_s[0], v_conv_w[0], v_g_mem, v_g_head, v_g_ffn,
               v_g_final.reshape(1, D)]
    s_delta, s_m, s_v = _adamw_small(small_w, small_g, small_m, small_v, name="adamw_small")
    small_out = {nm: (g, d, mn, vn) for nm, g, d, mn, vn in zip(small_names, small_g, s_delta, s_m, s_v)}

    order = ["g_mix", "w_in", "ln_v_g", "ln_v_b", "w_s", "b_s", "conv_w", "g_mem", "w_kv", "g_head", "w_o",
             "g_ffn", "w_ffn1", "w_ffn2", "g_final"]
    like = dict(g_mix=g_mix, w_in=w_in, ln_v_g=ln_v_g, ln_v_b=ln_v_b, w_s=w_s, b_s=b_s, conv_w=conv_w, g_mem=g_mem,
                w_kv=w_kv, g_head=g_head, w_o=w_o, g_ffn=g_ffn, w_ffn1=w_ffn1, w_ffn2=w_ffn2, g_final=g_final)
    res = {**big_out, **small_out}
    res["w_in"] = [a.T for a in res["w_in"]]
    outs = [loss, dx[None]]
    for k in range(4):
        outs += [res[nm][k].reshape(like[nm].shape) for nm in order]
    return tuple(outs)
```

```python
import math

import jax
import jax.numpy as jnp
from jax import lax
from jax.experimental import pallas as pl
from jax.experimental.pallas import tpu as pltpu

F32 = jnp.float32
BF16 = jnp.bfloat16
MESH = pl.DeviceIdType.MESH

D = 2048
S = 2048
HD = 128
NH = D // HD
NMH = 4
NSH = (NH - NMH) // 2
NCH = NH - NMH - NSH
DS = NSH * HD
DC = NCH * HD
DM = NMH * HD
DIN = 2 * DS + 3 * DC + DM
CHUNK = 128
NMEM = 256
DFF = 4 * D
EPS = 1e-6
NCHIP = 4
SCALE = HD ** -0.5

ADAM_LR = 0.001
ADAM_B1 = 0.9
ADAM_B2 = 0.999
ADAM_EPS = 1e-08
ADAM_WD = 0.01
ADAM_STEP = 10

TR_EW = 256
TR_MIX = 256
TM = 512
TN = 1024
TK = 2048
N_SUB = 512
VMEM_MB = 56
HALO = 8


def _pick(n, target, q=128):
    best = None
    for t in range(q, min(n, target) + 1, q):
        if n % t == 0:
            best = t
    return n if best is None else best


def _pick_rows(n, q):
    below = _pick(n, TR_EW, q)
    if 2 * below >= TR_EW:
        return below
    above = [t for t in range(TR_EW, min(n, 4 * TR_EW) + 1, q) if n % t == 0]
    return above[0] if above else below


def _cp(sem=None, vmem_mb=None, **kw):
    d = dict(kw)
    if sem is not None:
        d["dimension_semantics"] = sem
    if vmem_mb is not None:
        d["vmem_limit_bytes"] = vmem_mb << 20
    return pltpu.CompilerParams(**d)


def _gelu(x):
    z = 0.7978845608028654 * (x + 0.044715 * (x * x * x))
    return 0.5 * x * (1.0 + jnp.tanh(z))


def _gelu_with_grad(x):
    x2 = x * x
    t = jnp.tanh(0.7978845608028654 * (x + 0.044715 * (x2 * x)))
    half = 0.5 * (1.0 + t)
    return x * half, half + 0.5 * x * (1.0 - t * t) * (0.7978845608028654 * (1.0 + 3.0 * 0.044715 * x2))


def _matmul(a, b, *, name, ta=False, tb=False, M, N, K, tm=None, tn=None, tk=None, outs, epi=None,
            extras=(), a_spec=None, b_spec=None, out_specs=None, after=(), n_split=None, slots=None, into=()):
    n_after = len(after)
    tm = _pick(M, TM if tm is None else tm, 8)
    tn = _pick(N, TN if tn is None else tn)
    tk = _pick(K, TK if tk is None else tk)
    if n_split is None:
        n_split = tn // N_SUB if tn % N_SUB == 0 else 1
    nk = K // tk
    grid = (N // tn, M // tm, nk)
    if a_spec is None:
        a_spec = (pl.BlockSpec((tk, tm), lambda j, i, k, *s: (k, i)) if ta
                  else pl.BlockSpec((tm, tk), lambda j, i, k, *s: (i, k)))
    else:
        a_spec = a_spec(tm, tk)
    if b_spec is None:
        b_spec = (pl.BlockSpec((tn, tk), lambda j, i, k, *s: (j, k)) if tb
                  else pl.BlockSpec((tk, tn), lambda j, i, k, *s: (k, j)))
    else:
        b_spec = b_spec(tn, tk)
    if out_specs is None:
        out_specs = [pl.BlockSpec((tm, tn), lambda j, i, k, *s: (i, j)) for _ in outs]
    else:
        out_specs = out_specs(tm, tn)
    dn = (((0 if ta else 1,), (1 if tb else 0,)), ((), ()))
    n_ex, n_out = len(extras), len(outs)
    n_pre = 0 if slots is None else 1
    n_into = len(into)
    ns = tn // n_split

    def body(*refs):
        a_ref, b_ref = refs[n_pre], refs[n_pre + 1]
        ex = refs[n_pre + 2:n_pre + 2 + n_ex]
        first_out = n_pre + 2 + n_ex + n_after + n_into
        o = refs[first_out:first_out + n_out]
        acc = refs[first_out + n_out:]
        k = pl.program_id(2)

        def finish(val, cols):
            res = (val,) if epi is None else epi(val, *[e[:, cols] for e in ex])
            for r, o_ref in zip(res, o):
                o_ref[:, cols] = r.astype(o_ref.dtype)

        if nk > 1:
            @pl.when(k == 0)
            def _():
                acc[0][...] = jnp.zeros_like(acc[0])

        av = a_ref[...].astype(BF16)
        for q in range(n_split):
            cols = slice(q * ns, (q + 1) * ns)
            bq = (b_ref[cols, :] if tb else b_ref[:, cols]).astype(BF16)
            part = lax.dot_general(av, bq, dn, preferred_element_type=F32)
            if nk == 1:
                finish(part, cols)
            else:
                acc[0][:, cols] += part

        if nk > 1:
            @pl.when(k == nk - 1)
            def _():
                finish(acc[0][...], slice(0, tn))

    in_specs = ([a_spec, b_spec] + [sp(tm, tn) for _, sp in extras] + [ANY] * (n_after + n_into))
    scratch = [pltpu.VMEM((tm, tn), F32)] if nk > 1 else []
    args = [a, b] + [arr for arr, _ in extras] + list(after) + list(into)
    aliases = {n_pre + len(args) - n_into + t: t for t in range(n_into)}
    params = _cp(("parallel", "parallel", "arbitrary"), VMEM_MB)
    if slots is None:
        return pl.pallas_call(body, name=name, grid=grid, in_specs=in_specs, out_specs=out_specs, out_shape=outs,
                              scratch_shapes=scratch, input_output_aliases=aliases, compiler_params=params)(*args)
    return pl.pallas_call(
        body, name=name,
        grid_spec=pltpu.PrefetchScalarGridSpec(num_scalar_prefetch=1, grid=grid, in_specs=in_specs,
                                               out_specs=out_specs, scratch_shapes=scratch),
        out_shape=outs, input_output_aliases=aliases, compiler_params=params)(slots, *args)


def _tile_spec():
    return lambda tm, tn: pl.BlockSpec((tm, tn), lambda j, i, k, *s: (i, j))


def _cast_into_slot(w, slot, after, *, name):
    R, C = w.shape
    tr = _pick_rows(R, 16)

    def body(s_ref, w_ref, _after_ref, o_ref):
        o_ref[...] = w_ref[...].astype(BF16)

    return pl.pallas_call(
        body, name=name,
        grid_spec=pltpu.PrefetchScalarGridSpec(
            num_scalar_prefetch=1, grid=(R // tr,),
            in_specs=[pl.BlockSpec((tr, C), lambda i, s: (i, 0)), ANY],
            out_specs=pl.BlockSpec((None, tr, C), lambda i, s: (s[0], i, 0))),
        out_shape=jax.ShapeDtypeStruct((NCHIP, R, C), BF16),
        compiler_params=_cp(("parallel",), VMEM_MB),
    )(slot, w, after)


def _own_slot(part, slot, *, name):
    _, R, C = part.shape
    tr = _pick_rows(R, 16)

    def body(s_ref, p_ref, o_ref):
        o_ref[...] = p_ref[...]

    spec = pl.BlockSpec((None, tr, C), lambda i, s: (s[0], i, 0))
    return pl.pallas_call(
        body, name=name,
        grid_spec=pltpu.PrefetchScalarGridSpec(num_scalar_prefetch=1, grid=(R // tr,), in_specs=[spec],
                                               out_specs=spec),
        out_shape=jax.ShapeDtypeStruct(part.shape, part.dtype),
        compiler_params=_cp(("parallel",), VMEM_MB),
    )(slot, part)


def _rms_fwd(x, g, *, name, after=()):
    R, C = x.shape
    tr = _pick(R, TR_EW, 16)
    n_after = len(after)

    def body(x_ref, g_ref, *rest):
        o_ref = rest[n_after]
        xv = x_ref[...]
        r = lax.rsqrt(jnp.mean(xv * xv, axis=-1, keepdims=True) + EPS)
        o_ref[...] = ((xv * r) * g_ref[...]).astype(BF16)

    return pl.pallas_call(
        body, name=name, grid=(R // tr,),
        in_specs=[pl.BlockSpec((tr, C), lambda i: (i, 0)), pl.BlockSpec((1, C), lambda i: (0, 0))] + [ANY] * n_after,
        out_specs=pl.BlockSpec((tr, C), lambda i: (i, 0)),
        out_shape=jax.ShapeDtypeStruct((R, C), BF16),
        compiler_params=_cp(("parallel",), VMEM_MB),
    )(x, g, *after)


def _rms_bwd(dh, x, g, dres, *, name, want_dx=True, want_bf=True, after=()):
    R, C = x.shape
    tr = _pick(R, TR_EW, 16)
    has_res = dres is not None
    row = pl.BlockSpec((tr, C), lambda i: (i, 0))
    vec = pl.BlockSpec((1, C), lambda i: (0, 0))

    def body(*refs):
        dh_ref, x_ref, g_ref = refs[:3]
        pos = 3
        dres_ref = None
        if has_res:
            dres_ref = refs[pos]
            pos += 1
        outs = refs[pos + len(after):]
        i = pl.program_id(0)
        xv = x_ref[...]
        r = lax.rsqrt(jnp.mean(xv * xv, axis=-1, keepdims=True) + EPS)
        xh = xv * r
        dhv = dh_ref[...]
        dg_ref = outs[-1]
        dgp = jnp.sum(dhv * xh, axis=0, keepdims=True)

        @pl.when(i == 0)
        def _():
            dg_ref[...] = dgp

        @pl.when(i > 0)
        def _():
            dg_ref[...] += dgp

        if want_dx:
            t = dhv * g_ref[...]
            dx = r * (t - xh * jnp.mean(t * xh, axis=-1, keepdims=True))
            if has_res:
                dx = dx + dres_ref[...]
            outs[0][...] = dx
            if want_bf:
                outs[1][...] = dx.astype(BF16)

    in_specs = [row, row, vec] + ([row] if has_res else []) + [ANY] * len(after)
    out_specs, out_shape = [], []
    if want_dx:
        out_specs.append(row)
        out_shape.append(jax.ShapeDtypeStruct((R, C), F32))
        if want_bf:
            out_specs.append(row)
            out_shape.append(jax.ShapeDtypeStruct((R, C), BF16))
    out_specs.append(vec)
    out_shape.append(jax.ShapeDtypeStruct((1, C), F32))
    args = [dh, x, g] + ([dres] if has_res else []) + list(after)
    return pl.pallas_call(
        body, name=name, grid=(R // tr,), in_specs=in_specs, out_specs=out_specs, out_shape=out_shape,
        compiler_params=_cp(("arbitrary",), VMEM_MB),
    )(*args)


def _loss_bwd(x3, g, tgt, *, name):
    R, C = x3.shape
    tr = _pick(R, TR_EW, 16)
    n = R // tr
    row = pl.BlockSpec((tr, C), lambda i: (i, 0))
    vec = pl.BlockSpec((1, C), lambda i: (0, 0))

    def body(x_ref, g_ref, t_ref, dx_ref, dxb_ref, dg_ref, loss_ref, acc_ref):
        i = pl.program_id(0)
        xv = x_ref[...]
        gv = g_ref[...]
        r = lax.rsqrt(jnp.mean(xv * xv, axis=-1, keepdims=True) + EPS)
        xh = xv * r
        e = xh * gv - t_ref[...]
        dy = e * (1.0 / C)
        sq = jnp.sum(e * e, axis=0, keepdims=True)
        dgp = jnp.sum(dy * xh, axis=0, keepdims=True)

        @pl.when(i == 0)
        def _():
            acc_ref[...] = sq
            dg_ref[...] = dgp

        @pl.when(i > 0)
        def _():
            acc_ref[...] += sq
            dg_ref[...] += dgp

        t = dy * gv
        dx = r * (t - xh * jnp.mean(t * xh, axis=-1, keepdims=True))
        dx_ref[...] = dx
        dxb_ref[...] = dx.astype(BF16)

        @pl.when(i == n - 1)
        def _():
            loss_ref[...] = jnp.sum(acc_ref[...], axis=-1, keepdims=True) * (0.5 / C)

    return pl.pallas_call(
        body, name=name, grid=(n,),
        in_specs=[row, vec, row],
        out_specs=[row, row, vec, pl.BlockSpec((1, 1), lambda i: (0, 0))],
        out_shape=[jax.ShapeDtypeStruct((R, C), F32), jax.ShapeDtypeStruct((R, C), BF16),
                   jax.ShapeDtypeStruct((1, C), F32), jax.ShapeDtypeStruct((1, 1), F32)],
        scratch_shapes=[pltpu.VMEM((1, C), F32)],
        compiler_params=_cp(("arbitrary",), VMEM_MB),
    )(x3, g, tgt)


def _offsets():
    u0 = 0
    v0 = DS
    b0 = 2 * DS
    c0 = b0 + DC
    x0 = c0 + DC
    q0 = x0 + DC
    return u0, v0, b0, c0, x0, q0


def _tri_mask(lower):
    r = lax.broadcasted_iota(jnp.int32, (CHUNK, CHUNK), 0)
    c = lax.broadcasted_iota(jnp.int32, (CHUNK, CHUNK), 1)
    return (r >= c) if lower else (c >= r)


def _layer_norm_stats(vg):
    mu = jnp.mean(vg, axis=-1, keepdims=True)
    vc = vg - mu
    rstd = lax.rsqrt(jnp.mean(vc * vc, axis=-1, keepdims=True) + EPS)
    return vc * rstd, rstd


def _softmax_rows(qh, kh):
    s = lax.dot_general(qh, kh, (((1,), (1,)), ((), ())), preferred_element_type=F32)
    m = jnp.max(s, axis=-1, keepdims=True)
    e = jnp.exp(s - m)
    return e / jnp.sum(e, axis=-1, keepdims=True)


def _mix_fwd(proj, kv, w_s, bs_t, ln_g, ln_b, conv_w, g_head, *, name):
    assert DS == DC
    tr = _pick(S, TR_MIX, CHUNK)
    n = S // tr
    nck = tr // CHUNK
    u0, v0, b0, c0, x0, q0 = _offsets()
    hb = tr // HALO

    def body(p_ref, cprev_ref, xprev_ref, kv_ref, ws_ref, bst_ref, lng_ref, lnb_ref, cw_ref, gh_ref,
             heads_ref, hn_ref, ycv_ref, buf_ref):
        i = pl.program_id(0)

        def emit(col, val):
            rs = lax.rsqrt(jnp.mean(val * val, axis=-1, keepdims=True) + EPS)
            heads_ref[:, col:col + HD] = val
            hn_ref[:, col:col + HD] = ((val * rs) * gh_ref[:, col:col + HD]).astype(BF16)

        vhat, _ = _layer_norm_stats(_gelu(p_ref[:, v0:v0 + DS]))
        vnb = (vhat * lng_ref[...] + lnb_ref[...]).astype(BF16)
        low = _tri_mask(True)
        for h in range(NSH):
            wt = jnp.where(low, ws_ref[h], 0.0).astype(BF16)
            bcol = bst_ref[:, h:h + 1]
            parts = []
            for c in range(nck):
                blk = vnb[c * CHUNK:(c + 1) * CHUNK, h * HD:(h + 1) * HD]
                parts.append(jnp.dot(wt, blk, preferred_element_type=F32) + bcol)
            mixed = parts[0] if nck == 1 else jnp.concatenate(parts, axis=0)
            emit(h * HD, _gelu(p_ref[:, u0 + h * HD:u0 + (h + 1) * HD]) * mixed)

        xc = p_ref[:, c0:c0 + DC] * p_ref[:, x0:x0 + DC]
        prev = cprev_ref[...] * xprev_ref[...]
        buf_ref[0:HALO, :] = jnp.where(i > 0, prev, 0.0)
        buf_ref[HALO:HALO + tr, :] = xc
        y = (cw_ref[2:3, :] * xc + cw_ref[1:2, :] * buf_ref[HALO - 1:HALO - 1 + tr, :]
             + cw_ref[0:1, :] * buf_ref[HALO - 2:HALO - 2 + tr, :])
        ycv_ref[...] = y
        cout = p_ref[:, b0:b0 + DC] * y
        for h in range(NCH):
            emit(DS + h * HD, cout[:, h * HD:(h + 1) * HD])

        for h in range(NMH):
            qh = (p_ref[:, q0 + h * HD:q0 + (h + 1) * HD] * SCALE).astype(BF16)
            kh = kv_ref[:, h * HD:(h + 1) * HD].astype(BF16)
            vh = kv_ref[:, DM + h * HD:DM + (h + 1) * HD].astype(BF16)
            p = _softmax_rows(qh, kh)
            emit(DS + DC + h * HD, jnp.dot(p.astype(BF16), vh, preferred_element_type=F32))

    full = lambda shape: pl.BlockSpec(shape, lambda i: (0,) * len(shape))
    halo_c = pl.BlockSpec((HALO, DC), lambda i: (jnp.maximum(i * hb - 1, 0), c0 // DC))
    halo_x = pl.BlockSpec((HALO, DC), lambda i: (jnp.maximum(i * hb - 1, 0), x0 // DC))
    return pl.pallas_call(
        body, name=name, grid=(n,),
        in_specs=[pl.BlockSpec((tr, DIN), lambda i: (i, 0)), halo_c, halo_x,
                  full((NMEM, 2 * DM)), full((NSH, CHUNK, CHUNK)), full((CHUNK, NSH)),
                  full((1, DS)), full((1, DS)), full((3, DC)), full((1, D))],
        out_specs=[pl.BlockSpec((tr, D), lambda i: (i, 0)), pl.BlockSpec((tr, D), lambda i: (i, 0)),
                   pl.BlockSpec((tr, DC), lambda i: (i, 0))],
        out_shape=[jax.ShapeDtypeStruct((S, D), F32), jax.ShapeDtypeStruct((S, D), BF16),
                   jax.ShapeDtypeStruct((S, DC), F32)],
        scratch_shapes=[pltpu.VMEM((tr + HALO, DC), F32)],
        compiler_params=_cp(("parallel",), VMEM_MB),
    )(proj, proj, proj, kv, w_s, bs_t, ln_g, ln_b, conv_w, g_head)


def _mix_bwd(dhn, heads, proj, ycv, kv, w_s, bs_t, ln_g, ln_b, conv_w, g_head, after, *, name):
    assert DS == DC
    tr = _pick(S, TR_MIX, CHUNK)
    n = S // tr
    nck = tr // CHUNK
    u0, v0, b0, c0, x0, q0 = _offsets()
    hb = tr // HALO
    last_hb = S // HALO - 1

    def body(dhn_ref, heads_ref, p_ref, ycv_ref, dhn_nx_ref, heads_nx_ref, b_nx_ref, kv_ref, ws_ref, bst_ref,
             lng_ref, lnb_ref, cw_ref, gh_ref, _after_ref,
             dp_ref, dkv_ref, dws_ref, dbs_ref, dlng_ref, dlnb_ref, dcw_ref, dgh_ref, buf_ref, dvn_ref):
        i = pl.program_id(0)

        @pl.when(i == 0)
        def _():
            dkv_ref[...] = jnp.zeros_like(dkv_ref)
            dws_ref[...] = jnp.zeros_like(dws_ref)
            dbs_ref[...] = jnp.zeros_like(dbs_ref)
            dlng_ref[...] = jnp.zeros_like(dlng_ref)
            dlnb_ref[...] = jnp.zeros_like(dlnb_ref)
            dcw_ref[...] = jnp.zeros_like(dcw_ref)
            dgh_ref[...] = jnp.zeros_like(dgh_ref)

        def head_bwd(a, dn, gh):
            rs = lax.rsqrt(jnp.mean(a * a, axis=-1, keepdims=True) + EPS)
            ah = a * rs
            t = dn * gh
            return rs * (t - ah * jnp.mean(t * ah, axis=-1, keepdims=True)), jnp.sum(dn * ah, axis=0, keepdims=True)

        def head_grad(col):
            da, dg = head_bwd(heads_ref[:, col:col + HD], dhn_ref[:, col:col + HD], gh_ref[:, col:col + HD])
            dgh_ref[:, col:col + HD] += dg
            return da

        vg, dvg_dv = _gelu_with_grad(p_ref[:, v0:v0 + DS])
        vhat, rstd = _layer_norm_stats(vg)
        vnb = (vhat * lng_ref[...] + lnb_ref[...]).astype(BF16)
        low = _tri_mask(True)
        ones = jnp.ones((HALO, HD), BF16)
        for h in range(NSH):
            w_h = ws_ref[h]
            wt = jnp.where(low, w_h, 0.0).astype(BF16)
            bcol = bst_ref[:, h:h + 1]
            da = head_grad(h * HD)
            ug, dug_du = _gelu_with_grad(p_ref[:, u0 + h * HD:u0 + (h + 1) * HD])
            dws = jnp.zeros((CHUNK, CHUNK), F32)
            dbs = jnp.zeros((HALO, CHUNK), F32)
            mixed_parts = []
            for c in range(nck):
                rows = slice(c * CHUNK, (c + 1) * CHUNK)
                blk = vnb[rows, h * HD:(h + 1) * HD]
                mixed_parts.append(jnp.dot(wt, blk, preferred_element_type=F32) + bcol)
                dmb = (da[rows] * ug[rows]).astype(BF16)
                dws = dws + lax.dot_general(dmb, blk, (((1,), (1,)), ((), ())), preferred_element_type=F32)
                dbs = dbs + lax.dot_general(ones, dmb, (((1,), (1,)), ((), ())), preferred_element_type=F32)
                dvn_ref[c * CHUNK:(c + 1) * CHUNK, h * HD:(h + 1) * HD] = lax.dot_general(
                    wt, dmb, (((0,), (0,)), ((), ())), preferred_element_type=F32)
            mixed = mixed_parts[0] if nck == 1 else jnp.concatenate(mixed_parts, axis=0)
            dp_ref[:, u0 + h * HD:u0 + (h + 1) * HD] = ((da * mixed) * dug_du).astype(BF16)
            dws_ref[h] += jnp.where(low, dws, 0.0)
            dbs_ref[h] += dbs
        dvn = dvn_ref[...]
        dlng_ref[...] += jnp.sum(dvn * vhat, axis=0, keepdims=True)
        dlnb_ref[...] += jnp.sum(dvn, axis=0, keepdims=True)
        dvh = dvn * lng_ref[...]
        dvg = rstd * (dvh - jnp.mean(dvh, axis=-1, keepdims=True)
                      - vhat * jnp.mean(dvh * vhat, axis=-1, keepdims=True))
        dp_ref[:, v0:v0 + DS] = (dvg * dvg_dv).astype(BF16)

        dc = jnp.concatenate([head_grad(DS + h * HD) for h in range(NCH)], axis=1)
        dc_nx = jnp.concatenate(
            [head_bwd(heads_nx_ref[:, h * HD:(h + 1) * HD], dhn_nx_ref[:, h * HD:(h + 1) * HD],
                      gh_ref[:, DS + h * HD:DS + (h + 1) * HD])[0] for h in range(NCH)], axis=1)
        bg = p_ref[:, b0:b0 + DC]
        cg = p_ref[:, c0:c0 + DC]
        xin = p_ref[:, x0:x0 + DC]
        dp_ref[:, b0:b0 + DC] = (dc * ycv_ref[...]).astype(BF16)
        dyv = dc * bg
        buf_ref[0:tr, :] = dyv
        buf_ref[tr:tr + HALO, :] = jnp.where(i < n - 1, dc_nx * b_nx_ref[...], 0.0)
        sh1 = buf_ref[1:1 + tr, :]
        sh0 = buf_ref[2:2 + tr, :]
        dxc = cw_ref[2:3, :] * dyv + cw_ref[1:2, :] * sh1 + cw_ref[0:1, :] * sh0
        xc = cg * xin
        dp_ref[:, c0:c0 + DC] = (dxc * xin).astype(BF16)
        dp_ref[:, x0:x0 + DC] = (dxc * cg).astype(BF16)
        dcw_ref[0:1, :] += jnp.sum(sh0 * xc, axis=0, keepdims=True)
        dcw_ref[1:2, :] += jnp.sum(sh1 * xc, axis=0, keepdims=True)
        dcw_ref[2:3, :] += jnp.sum(dyv * xc, axis=0, keepdims=True)

        for h in range(NMH):
            do = head_grad(DS + DC + h * HD).astype(BF16)
            qh = (p_ref[:, q0 + h * HD:q0 + (h + 1) * HD] * SCALE).astype(BF16)
            kh = kv_ref[:, h * HD:(h + 1) * HD].astype(BF16)
            vh = kv_ref[:, DM + h * HD:DM + (h + 1) * HD].astype(BF16)
            p = _softmax_rows(qh, kh)
            dpr = lax.dot_general(do, vh, (((1,), (1,)), ((), ())), preferred_element_type=F32)
            ds = (p * (dpr - jnp.sum(dpr * p, axis=-1, keepdims=True))).astype(BF16)
            dp_ref[:, q0 + h * HD:q0 + (h + 1) * HD] = (
                jnp.dot(ds, kh, preferred_element_type=F32) * SCALE).astype(BF16)
            dkv_ref[:, h * HD:(h + 1) * HD] += lax.dot_general(
                ds, qh, (((0,), (0,)), ((), ())), preferred_element_type=F32)
            dkv_ref[:, DM + h * HD:DM + (h + 1) * HD] += lax.dot_general(
                p.astype(BF16), do, (((0,), (0,)), ((), ())), preferred_element_type=F32)

    full = lambda shape: pl.BlockSpec(shape, lambda i: (0,) * len(shape))
    row = lambda c: pl.BlockSpec((tr, c), lambda i: (i, 0))
    nxt = lambda col: pl.BlockSpec((HALO, DC), lambda i: (jnp.minimum((i + 1) * hb, last_hb), col))
    return pl.pallas_call(
        body, name=name, grid=(n,),
        in_specs=[row(D), row(D), row(DIN), row(DC), nxt(DS // DC), nxt(DS // DC), nxt(b0 // DC),
                  full((NMEM, 2 * DM)), full((NSH, CHUNK, CHUNK)), full((CHUNK, NSH)),
                  full((1, DS)), full((1, DS)), full((3, DC)), full((1, D)), ANY],
        out_specs=[row(DIN), full((NMEM, 2 * DM)), full((NSH, CHUNK, CHUNK)), full((NSH, HALO, CHUNK)),
                   full((1, DS)), full((1, DS)), full((HALO, DC)), full((1, D))],
        out_shape=[jax.ShapeDtypeStruct((S, DIN), BF16), jax.ShapeDtypeStruct((NMEM, 2 * DM), F32),
                   jax.ShapeDtypeStruct((NSH, CHUNK, CHUNK), F32), jax.ShapeDtypeStruct((NSH, HALO, CHUNK), F32),
                   jax.ShapeDtypeStruct((1, DS), F32), jax.ShapeDtypeStruct((1, DS), F32),
                   jax.ShapeDtypeStruct((HALO, DC), F32), jax.ShapeDtypeStruct((1, D), F32)],
        scratch_shapes=[pltpu.VMEM((tr + HALO, DC), F32), pltpu.VMEM((tr, DS), F32)],
        compiler_params=_cp(("arbitrary",), VMEM_MB),
    )(dhn, heads, proj, ycv, dhn, heads, proj, kv, w_s, bs_t, ln_g, ln_b, conv_w, g_head, after)


def _place():
    x, y, c = lax.axis_index("x"), lax.axis_index("y"), lax.axis_index("c")
    chips = [(1 - x, y), (x, 1 - y), (1 - x, 1 - y)]
    return x, y, c, chips


ANY = pl.BlockSpec(memory_space=pl.ANY)


HBM = pl.BlockSpec(memory_space=pltpu.HBM)
SEM = pl.BlockSpec(memory_space=pltpu.SEMAPHORE)
EFFECT = pltpu.SideEffectType.DATAFLOW_SIDE_EFFECTING
N_PEER_CHIPS = 3
N_NEIGHBOUR_CHIPS = 2
CONV_PAD = (32, 256)


def _in_hbm(a):
    return pltpu.with_memory_space_constraint(a, pltpu.HBM)


def _allgather_start(bufs, forwards, after, collective_id, *, name):
    arrs = list(bufs) + list(forwards)
    nw, nb = len(arrs), len(bufs)

    def body(*refs):
        ins, send, recv = refs[:nw], refs[nw + 1:2 * nw + 1], refs[2 * nw + 1:3 * nw + 1]
        token = refs[4 * nw + 1]
        x, y, c, chips = _place()
        s = 2 * x + y
        slots = [2 * cx + cy for cx, cy in chips]
        _handshake([(cx, cy, c) for cx, cy in chips[:N_NEIGHBOUR_CHIPS]])
        for w in range(nb, nw):
            q = arrs[w].shape[1] // 4
            for j in range(N_NEIGHBOUR_CHIPS):
                rows = ins[w].at[slots[j], pl.ds(c * 2 * q + j * q, q)]
                pltpu.make_async_remote_copy(src_ref=rows, dst_ref=rows, send_sem=send[w], recv_sem=recv[w],
                                             device_id=(*chips[1 - j], c), device_id_type=MESH).start()
        for w in range(nb):
            hr = arrs[w].shape[1] // 2
            rows = ins[w].at[s, pl.ds(c * hr, hr)]
            for cx, cy in chips[:N_NEIGHBOUR_CHIPS]:
                pltpu.make_async_remote_copy(src_ref=rows, dst_ref=rows, send_sem=send[w], recv_sem=recv[w],
                                             device_id=(cx, cy, c), device_id_type=MESH).start()
        token[...] = jnp.zeros_like(token)

    res = pl.pallas_call(
        body, name=name,
        in_specs=[HBM] * nw + [ANY],
        out_specs=[SEM] * (2 * nw) + [HBM] * nw + [pl.BlockSpec(memory_space=pltpu.VMEM)],
        out_shape=[pltpu.SemaphoreType.DMA(())] * (2 * nw) + [pltpu.HBM(a.shape, a.dtype) for a in arrs]
        + [jax.ShapeDtypeStruct((8, 128), F32)],
        input_output_aliases={w: 2 * nw + w for w in range(nw)},
        compiler_params=pltpu.CompilerParams(has_side_effects=EFFECT, collective_id=collective_id),
    )(*[_in_hbm(a) for a in arrs], after)
    return res[:nw], res[nw:2 * nw], res[2 * nw:3 * nw], res[3 * nw]


def _handshake(peers):
    barrier = pltpu.get_barrier_semaphore()
    for peer in peers:
        pl.semaphore_signal(barrier, inc=1, device_id=peer, device_id_type=MESH)
    pl.semaphore_wait(barrier, len(peers))


def _scatter_start(parts, bufs, collective_id, *, name):
    nw = len(parts)

    def body(*refs):
        src, dst = refs[:nw], refs[nw:2 * nw]
        send, recv = refs[2 * nw:3 * nw], refs[3 * nw:4 * nw]
        token = refs[6 * nw]
        x, y, c, chips = _place()
        s = 2 * x + y
        _handshake([(cx, cy, c) for cx, cy in chips])
        for w in range(nw):
            for cx, cy in chips:
                pltpu.make_async_remote_copy(src_ref=src[w].at[2 * cx + cy], dst_ref=dst[w].at[s], send_sem=send[w],
                                             recv_sem=recv[w], device_id=(cx, cy, c), device_id_type=MESH).start()
        token[...] = jnp.zeros_like(token)

    res = pl.pallas_call(
        body, name=name,
        in_specs=[HBM] * (2 * nw),
        out_specs=[SEM] * (2 * nw) + [HBM] * (2 * nw) + [pl.BlockSpec(memory_space=pltpu.VMEM)],
        out_shape=[pltpu.SemaphoreType.DMA(())] * (2 * nw) + [pltpu.HBM(a.shape, a.dtype) for a in parts + bufs]
        + [jax.ShapeDtypeStruct((8, 128), F32)],
        input_output_aliases={k: 2 * nw + k for k in range(2 * nw)},
        compiler_params=pltpu.CompilerParams(has_side_effects=EFFECT, collective_id=collective_id),
    )(*[_in_hbm(a) for a in parts + bufs])
    return res[:nw], res[nw:2 * nw], res[2 * nw:3 * nw], res[3 * nw:4 * nw], res[4 * nw]


def _sibling_start(srcs, whole, collective_id, *, name):
    nw = len(srcs)
    lands = [lax.empty((a.shape[0], a.shape[1] if whole else a.shape[1] // 2, a.shape[2]), a.dtype) for a in srcs]

    def body(*refs):
        src, land = refs[:nw], refs[nw:2 * nw]
        send, recv = refs[2 * nw:3 * nw], refs[3 * nw:4 * nw]
        token = refs[6 * nw]
        x, y, c, _ = _place()
        _handshake([(x, y, 1 - c)])
        for w in range(nw):
            hr = srcs[w].shape[1] // 2
            rows = src[w] if whole else src[w].at[:, pl.ds((1 - c) * hr, hr)]
            pltpu.make_async_remote_copy(src_ref=rows, dst_ref=land[w], send_sem=send[w], recv_sem=recv[w],
                                         device_id=(x, y, 1 - c), device_id_type=MESH).start()
        token[...] = jnp.zeros_like(token)

    res = pl.pallas_call(
        body, name=name,
        in_specs=[HBM] * (2 * nw),
        out_specs=[SEM] * (2 * nw) + [HBM] * (2 * nw) + [pl.BlockSpec(memory_space=pltpu.VMEM)],
        out_shape=[pltpu.SemaphoreType.DMA(())] * (2 * nw) + [pltpu.HBM(a.shape, a.dtype) for a in srcs + lands]
        + [jax.ShapeDtypeStruct((8, 128), F32)],
        input_output_aliases={k: 2 * nw + k for k in range(2 * nw)},
        compiler_params=pltpu.CompilerParams(has_side_effects=EFFECT, collective_id=collective_id),
    )(*[_in_hbm(a) for a in srcs + lands])
    return res[:nw], res[nw:2 * nw], res[2 * nw:3 * nw], res[3 * nw:4 * nw], res[4 * nw]


def _transfer_wait(sends, recvs, thru, sizes, after, *, name):
    n = len(sends)
    flat = [a for group in thru for a in group]

    def body(*refs):
        bufs = refs[:len(flat)]
        send = refs[len(flat):len(flat) + n]
        recv = refs[len(flat) + n:len(flat) + 2 * n]
        token = refs[2 * len(flat) + 2 * n + 1]
        token[...] = jnp.zeros_like(token)
        x, y, c, _ = _place()
        pos = 0
        for k in range(n):
            slots, rows = sizes[k]
            region = bufs[pos].at[pl.ds(0, slots), pl.ds(0, rows)]
            pos += len(thru[k])
            cp = pltpu.make_async_remote_copy(src_ref=region, dst_ref=region, send_sem=send[k], recv_sem=recv[k],
                                              device_id=(x, y, 1 - c), device_id_type=MESH)
            cp.wait_send()
            cp.wait_recv()

    res = pl.pallas_call(
        body, name=name,
        in_specs=[HBM] * len(flat) + [SEM] * (2 * n) + [pl.BlockSpec(memory_space=pl.ANY)],
        out_specs=[HBM] * len(flat) + [pl.BlockSpec(memory_space=pltpu.VMEM)],
        out_shape=[pltpu.HBM(a.shape, a.dtype) for a in flat] + [jax.ShapeDtypeStruct((8, 128), F32)],
        input_output_aliases={k: k for k in range(len(flat))},
        compiler_params=pltpu.CompilerParams(has_side_effects=EFFECT),
    )(*flat, *sends, *recvs, after)
    out, pos = [], 0
    for group in thru:
        out.append(res[pos:pos + len(group)])
        pos += len(group)
    return out, res[len(flat)]


def _forward_halves(bufs, which, after, *, name):
    nw = len(bufs)
    n = len(which)

    def body(*refs):
        outs = refs[nw + 1:2 * nw + 1]
        send, recv = refs[2 * nw + 1:]
        x, y, c, chips = _place()
        me, sibling = (x, y, c), (x, y, 1 - c)

        def d2d(w, t, half, to):
            cx, cy = chips[which[t]]
            hr = bufs[w].shape[1] // 2
            rows = outs[w].at[2 * cx + cy, pl.ds(half * hr, hr)]
            return pltpu.make_async_remote_copy(src_ref=rows, dst_ref=rows, send_sem=send.at[n * w + t],
                                                recv_sem=recv.at[n * w + t], device_id=to, device_id_type=MESH)

        passed = [d2d(w, t, c, sibling) for w in range(nw) for t in range(n)]
        for cp in passed:
            cp.start()
        for w in range(nw):
            for t in range(n):
                d2d(w, t, 1 - c, me).wait_recv()
        for cp in passed:
            cp.wait_send()

    return pl.pallas_call(
        body, name=name,
        in_specs=[ANY] * (nw + 1), out_specs=[ANY] * nw,
        out_shape=[jax.ShapeDtypeStruct(a.shape, a.dtype) for a in bufs],
        input_output_aliases={w: w for w in range(nw)},
        scratch_shapes=[pltpu.SemaphoreType.DMA((n * nw,)), pltpu.SemaphoreType.DMA((n * nw,))],
    )(*bufs, after)


def _allreduce_small(p, after, *, name):
    R = p.shape[0]
    hr = R // 2

    def body(p_ref, _after_ref, out_ref, sib_ref, sum_ref, gat_ref, tot_ref, send, recv):
        x, y, c, chips = _place()
        s = 2 * x + y
        sibling = (x, y, 1 - c)
        rows = pl.ds(pl.multiple_of(c * hr, 8), hr)
        swap = pltpu.make_async_remote_copy(src_ref=p_ref, dst_ref=sib_ref, send_sem=send.at[0], recv_sem=recv.at[0],
                                            device_id=sibling, device_id_type=MESH)
        swap.start()
        swap.wait()
        sum_ref[...] = p_ref[...] + sib_ref[...]
        gat_ref[s] = sum_ref[rows, :]
        cps = [pltpu.make_async_remote_copy(src_ref=sum_ref.at[rows], dst_ref=gat_ref.at[s], send_sem=send.at[1 + j],
                                            recv_sem=recv.at[1 + j], device_id=(cx, cy, c), device_id_type=MESH)
               for j, (cx, cy) in enumerate(chips)]
        for cp in cps:
            cp.start()
        for cp in cps:
            cp.wait()
        tot_ref[...] = ((gat_ref[0] + gat_ref[1]) + gat_ref[2]) + gat_ref[3]
        out_ref[rows, :] = tot_ref[...]
        share = pltpu.make_async_remote_copy(src_ref=tot_ref, dst_ref=out_ref.at[rows], send_sem=send.at[4],
                                             recv_sem=recv.at[4], device_id=sibling, device_id_type=MESH)
        share.start()
        share.wait_send()
        other = out_ref.at[pl.ds(pl.multiple_of((1 - c) * hr, 8), hr)]
        pltpu.make_async_remote_copy(src_ref=other, dst_ref=other, send_sem=send.at[4], recv_sem=recv.at[4],
                                     device_id=(x, y, c), device_id_type=MESH).wait_recv()

    vmem = pl.BlockSpec(memory_space=pltpu.VMEM)
    return pl.pallas_call(
        body, name=name, in_specs=[vmem, ANY], out_specs=vmem,
        out_shape=jax.ShapeDtypeStruct((R, 128), F32),
        scratch_shapes=[pltpu.VMEM((R, 128), F32), pltpu.VMEM((R, 128), F32), pltpu.VMEM((NCHIP, hr, 128), F32),
                        pltpu.VMEM((hr, 128), F32), pltpu.SemaphoreType.DMA((5,)), pltpu.SemaphoreType.DMA((5,))],
    )(p, after)


def _select_half_bf16(g, half, add, slot, *, name):
    _, R, C = g.shape
    hr = R // 2
    tr = _pick_rows(hr, 16)
    nb = hr // tr
    sel = jnp.concatenate([jnp.reshape(half, (1,)).astype(jnp.int32), slot])

    def body(s_ref, g_ref, a_ref, o_ref, own_ref):
        val = (g_ref[...].astype(F32) + a_ref[...].astype(F32)).astype(BF16)
        o_ref[...] = val

        @pl.when(pl.program_id(1) == s_ref[1])
        def _():
            own_ref[...] = val

    g_spec = pl.BlockSpec((None, tr, C), lambda i, j, s: (j, s[0] * nb + i, 0))
    o_spec = pl.BlockSpec((None, tr, C), lambda i, j, s: (j, i, 0))
    own_spec = pl.BlockSpec((None, tr, C), lambda i, j, s: (s[1], i, 0))
    shape = jax.ShapeDtypeStruct((NCHIP, hr, C), BF16)
    return pl.pallas_call(
        body, name=name,
        grid_spec=pltpu.PrefetchScalarGridSpec(
            num_scalar_prefetch=1, grid=(nb, NCHIP), in_specs=[g_spec, o_spec], out_specs=[o_spec, own_spec]),
        out_shape=[shape, shape],
        compiler_params=_cp(("parallel", "arbitrary"), VMEM_MB),
    )(sel, g, add)


def _adamw_math(w, g, m, v):
    m = ADAM_B1 * m + (1.0 - ADAM_B1) * g
    v = ADAM_B2 * v + (1.0 - ADAM_B2) * (g * g)
    m_hat = m / (1.0 - ADAM_B1 ** ADAM_STEP)
    v_hat = v / (1.0 - ADAM_B2 ** ADAM_STEP)
    delta = -ADAM_LR * (m_hat / (jnp.sqrt(v_hat) + ADAM_EPS) + ADAM_WD * w)
    return delta, m, v


def _adamw(w, g_mine, g_sib, m, v, core, *, name):
    R, C = w.shape
    hr = R // 2
    tr = _pick_rows(hr, 16)
    nb = hr // tr
    row = pl.BlockSpec((tr, C), lambda hh, i, c: (hh * nb + i, 0))
    steps = 2 * nb
    RING = 3
    mine = pl.BlockSpec((NCHIP, tr, C), lambda hh, i, c: (0, jnp.where(hh == c[0], i, 0), 0))
    sibs = pl.BlockSpec((NCHIP, tr, C), lambda hh, i, c: (0, jnp.where(hh == c[0], 0, i), 0))

    def slot_sum(ref):
        acc = ref[0].astype(F32) + ref[1].astype(F32)
        for j in range(2, NCHIP):
            acc = acc + ref[j].astype(F32)
        return acc

    def body(c_ref, w_hbm, gm_ref, gs_ref, m_hbm, v_hbm, go_ref, d_ref, mo_ref, vo_ref, w_ring, m_ring, v_ring, sems):
        t = pl.program_id(0) * nb + pl.program_id(1)
        streams = ((w_hbm, w_ring), (m_hbm, m_ring), (v_hbm, v_ring))

        def fetch(a, step):
            src, ring = streams[a]
            return pltpu.make_async_copy(src.at[pl.ds(step * tr, tr)], ring.at[step % RING], sems.at[a, step % RING])

        @pl.when(t == 0)
        def _():
            for a in range(len(streams)):
                for step in range(min(RING - 1, steps)):
                    fetch(a, step).start()

        @pl.when(t + RING - 1 < steps)
        def _():
            for a in range(len(streams)):
                fetch(a, t + RING - 1).start()

        for a in range(len(streams)):
            fetch(a, t).wait()
        cur = t % RING
        gv = jnp.where(pl.program_id(0) == c_ref[0], slot_sum(gm_ref), slot_sum(gs_ref))
        d, mn, vn = _adamw_math(w_ring[cur], gv, m_ring[cur], v_ring[cur])
        go_ref[...] = gv
        d_ref[...] = d
        mo_ref[...] = mn
        vo_ref[...] = vn

    return pl.pallas_call(
        body, name=name,
        grid_spec=pltpu.PrefetchScalarGridSpec(
            num_scalar_prefetch=1, grid=(2, nb),
            in_specs=[ANY, mine, sibs, ANY, ANY], out_specs=[row] * 4,
            scratch_shapes=[pltpu.VMEM((RING, tr, C), F32)] * 3 + [pltpu.SemaphoreType.DMA((3, RING))]),
        out_shape=[jax.ShapeDtypeStruct((R, C), F32)] * 4,
        compiler_params=_cp(("arbitrary", "arbitrary"), VMEM_MB),
    )(core, w, g_mine, g_sib, m, v)


def _adamw_small(ws, gs, ms, vs, *, name):
    n = len(ws)

    def body(*refs):
        w_r, g_r, m_r, v_r = refs[:n], refs[n:2 * n], refs[2 * n:3 * n], refs[3 * n:4 * n]
        d_r, mo_r, vo_r = refs[4 * n:5 * n], refs[5 * n:6 * n], refs[6 * n:7 * n]
        for k in range(n):
            d, mn, vn = _adamw_math(w_r[k][...], g_r[k][...], m_r[k][...], v_r[k][...])
            d_r[k][...] = d
            mo_r[k][...] = mn
            vo_r[k][...] = vn

    shapes = [jax.ShapeDtypeStruct(w.shape, F32) for w in ws]
    res = pl.pallas_call(body, name=name, out_shape=shapes * 3)(*ws, *gs, *ms, *vs)
    return res[:n], res[n:2 * n], res[2 * n:]


_PACK_ROWS = 8


def _pack(parts):
    rows = []
    for a in parts:
        flat = a.reshape(-1)
        n = -(-flat.shape[0] // (_PACK_ROWS * 128)) * (_PACK_ROWS * 128)
        rows.append(jnp.pad(flat, (0, n - flat.shape[0])).reshape(-1, 128))
    total = sum(r.shape[0] for r in rows)
    if total % 16:
        rows.append(jnp.zeros((16 - total % 16, 128), F32))
    return jnp.concatenate(rows, axis=0)


def _unpack(p, shapes):
    out, r = [], 0
    for shp in shapes:
        n = math.prod(shp)
        nr = -(-n // (_PACK_ROWS * 128)) * _PACK_ROWS
        out.append(p[r:r + nr].reshape(-1)[:n].reshape(shp))
        r += nr
    return out


def kernel(x, mem, g_mix, w_in, ln_v_g, ln_v_b, w_s, b_s, conv_w, g_mem, w_kv, g_head, w_o, g_ffn, w_ffn1, w_ffn2, g_final, loss_target, m_g_mix, m_w_in, m_ln_v_g, m_ln_v_b, m_w_s, m_b_s, m_conv_w, m_g_mem, m_w_kv, m_g_head, m_w_o, m_g_ffn, m_w_ffn1, m_w_ffn2, m_g_final, v_g_mix, v_w_in, v_ln_v_g, v_ln_v_b, v_w_s, v_b_s, v_conv_w, v_g_mem, v_w_kv, v_g_head, v_w_o, v_g_ffn, v_w_ffn1, v_w_ffn2, v_g_final):
    sds = jax.ShapeDtypeStruct
    xi, yi = lax.axis_index("x"), lax.axis_index("y")
    shard = 2 * xi + yi
    x2d, mem2d, tgt = x[0], mem[0], loss_target[0]
    ws3, bs2 = w_s[0], b_s[0]
    g_final2 = g_final.reshape(1, D)
    dff4 = DFF // NCHIP
    din4 = DIN // NCHIP
    dcv4 = DC // NCHIP

    big = [w_in[0].T, w_kv[0], w_o[0], w_ffn1[0], w_ffn2[0]]
    big_names = ["w_in", "w_kv", "w_o", "w_ffn1", "w_ffn2"]
    slot = jnp.reshape(shard, (1,)).astype(jnp.int32)
    core = jnp.reshape(lax.axis_index("c"), (1,)).astype(jnp.int32)
    conv_pad = jnp.pad(conv_w[0], ((0, CONV_PAD[0] - 3), (0, CONV_PAD[1] - dcv4)))
    conv_slots = lax.dynamic_update_slice(jnp.zeros((NCHIP,) + CONV_PAD, F32), conv_pad[None], (shard, 0, 0))

    gather_ids = {"in": 16, "kvo": 17, "ffn1": 18, "ffn2": 19, "ffn2d": 20}

    def gather_start(bufs, after, nm, forwards=()):
        return _allgather_start(bufs, forwards, after, gather_ids[nm], name="ag_start_" + nm)

    def gather_wait(state, idx, after, nm):
        send, recv, bufs, _ = state
        got, token = _transfer_wait([send[k] for k in idx], [recv[k] for k in idx], [[bufs[k]] for k in idx],
                                    [(N_NEIGHBOUR_CHIPS, bufs[k].shape[1] // 2) for k in idx], after, name="ag_wait_" + nm)
        return [g[0] for g in got], token

    cast = lambda k, after: _cast_into_slot(big[k], slot, after, name="cast_" + big_names[k])
    ag_in = gather_start([cast(0, slot), conv_slots], slot, "in")
    bs_t = bs2.T

    h = _rms_fwd(x2d, g_mix, name="rms_mix", after=[ag_in[3]])
    mem_n = _rms_fwd(mem2d, g_mem, name="rms_mem", after=[h])
    kvo_b = [cast(1, mem_n)]
    kvo_b.append(cast(2, kvo_b[0]))
    w1_b = cast(3, kvo_b[1])
    w2_b = cast(4, w1_b)
    NEAR, FAR = [0, 1], [2]

    def diagonal_wait(state, ks, after, nm):
        send, recv, bufs, _ = state
        got, token = _transfer_wait([send[k] for k in ks], [recv[k] for k in ks], [[bufs[k]] for k in ks],
                                    [(1, bufs[k].shape[1] // 2) for k in ks], after, name="ag_waitd_" + nm)
        return [g[0] for g in got], token

    got_in, tok = gather_wait(ag_in, [0, 1], w2_b, "in")
    ag_kvo = gather_start(kvo_b, tok, "kvo", forwards=got_in)
    in_n = _forward_halves(ag_kvo[2][2:], NEAR, ag_kvo[3], name="ag_fwdn_in")
    ag_kvo = (ag_kvo[0], ag_kvo[1], list(ag_kvo[2][:2]) + list(in_n), ag_kvo[3])
    in_d, tok = diagonal_wait(ag_kvo, [2, 3], ag_kvo[3], "in")
    win4, conv4 = _forward_halves(in_d, FAR, tok, name="ag_fwdd_in")
    w_in_t = win4.reshape(DIN, D)
    conv_full = conv4[:, :3, :dcv4].transpose(1, 0, 2).reshape(3, DC)

    proj_w = lambda tn, tk: pl.BlockSpec((tn, tk), lambda j, i, k, s: (s[j], k))
    proj_cols = lambda tm, tn: [pl.BlockSpec((tm, tn), lambda j, i, k, s: (i, s[j]))]
    proj_half = lambda which, into, after: _matmul(
        h, w_in_t, name="mm_proj_%d" % which, tb=True, M=S, N=DIN // 2, K=D, tn=DIN // 2, b_spec=proj_w,
        out_specs=proj_cols, outs=[sds((S, DIN), F32)], slots=jnp.full((1,), which, jnp.int32), into=into,
        after=after)[0]
    proj = proj_half(0, [], [ag_kvo[3]])
    got_kvo, tok = gather_wait(ag_kvo, [0, 1], proj, "kvo")
    ag_w1 = gather_start([w1_b], tok, "ffn1", forwards=got_kvo)
    kvo_n = _forward_halves(ag_w1[2][1:], NEAR, ag_w1[3], name="ag_fwdn_kvo")
    ag_w1 = (ag_w1[0], ag_w1[1], [ag_w1[2][0]] + list(kvo_n), ag_w1[3])
    proj = proj_half(1, [proj], list(kvo_n))
    kvo_d, tok = diagonal_wait(ag_w1, [1, 2], proj, "kvo")
    wkv4, wo4 = _forward_halves(kvo_d, FAR, tok, name="ag_fwdd_kvo")
    w_kv_full = wkv4.reshape(D, 2 * DM)
    w_o_full = wo4.reshape(D, D)
    (kv,) = _matmul(mem_n, w_kv_full, name="mm_kv", M=NMEM, N=2 * DM, K=D, outs=[sds((NMEM, 2 * DM), F32)])
    heads, hn, ycv = _mix_fwd(proj, kv, ws3, bs_t, ln_v_g, ln_v_b, conv_full, g_head, name="mix_fwd")
    def residual_and_norm(acc, res, g):
        x2v = acc + res
        r = lax.rsqrt(jnp.mean(x2v * x2v, axis=-1, keepdims=True) + EPS)
        return x2v, (x2v * r) * g

    row_vec = lambda tm, tn: pl.BlockSpec((1, tn), lambda j, i, k, *s: (0, j))
    x2, h2 = _matmul(hn, w_o_full, name="mm_wo", M=S, N=D, K=D, tn=D, n_split=1, epi=residual_and_norm,
                     outs=[sds((S, D), F32), sds((S, D), BF16)], extras=[(x2d, _tile_spec()), (g_ffn, row_vec)])
    near = jnp.stack([shard, 2 * (1 - xi) + yi, 2 * xi + (1 - yi)]).astype(jnp.int32)
    far = jnp.reshape(2 * (1 - xi) + (1 - yi), (1,)).astype(jnp.int32)

    w1_shard = lambda tn, tk: pl.BlockSpec((None, tk, tn), lambda j, i, k, s: (s[j], k, 0))
    act_cols = lambda tm, tn: [pl.BlockSpec((tm, tn), lambda j, i, k, s: (i, s[j]))] * 2

    def relu2(acc):
        r = jnp.maximum(acc, 0.0)
        return r * r, 2.0 * r

    got_w1, tok = gather_wait(ag_w1, [0], h2, "ffn1")
    ag_w2 = gather_start([w2_b], tok, "ffn2", forwards=got_w1)
    (w1n,) = _forward_halves([ag_w2[2][1]], NEAR, ag_w2[3], name="ag_fwdn_ffn1")
    ag_w2 = (ag_w2[0], ag_w2[1], [ag_w2[2][0], w1n], ag_w2[3])
    act, dact_df = _matmul(h2, w1n, name="mm_ffn1_near", M=S, N=3 * dff4, K=D, tm=2 * TM, tn=dff4, b_spec=w1_shard,
                           out_specs=act_cols, outs=[sds((S, DFF), BF16)] * 2, epi=relu2, slots=near)
    w1d, tok = diagonal_wait(ag_w2, [1], act, "ffn1")
    (w14,) = _forward_halves(w1d, FAR, tok, name="ag_fwdd_ffn1")
    act, dact_df = _matmul(h2, w14, name="mm_ffn1_far", M=S, N=dff4, K=D, tm=2 * TM, tn=dff4, b_spec=w1_shard,
                           out_specs=act_cols, outs=[sds((S, DFF), BF16)] * 2, epi=relu2, slots=far,
                           into=[act, dact_df])

    act_shard = lambda tm, tk: pl.BlockSpec((tm, tk), lambda j, i, k, s: (i, s[k]))
    w2_shard = lambda tn, tk: pl.BlockSpec((None, tk, tn), lambda j, i, k, s: (s[k], 0, j))
    got_w2, tok = gather_wait(ag_w2, [0], act, "ffn2")
    ag_w2d = gather_start([], tok, "ffn2d", forwards=got_w2)
    (w2n,) = _forward_halves(ag_w2d[2], NEAR, ag_w2d[3], name="ag_fwdn_ffn2")
    ag_w2d = (ag_w2d[0], ag_w2d[1], [w2n], ag_w2d[3])
    (x3,) = _matmul(act, w2n, name="mm_ffn2_near", M=S, N=D, K=3 * dff4, tm=2 * TM, tk=dff4,
                    a_spec=act_shard, b_spec=w2_shard, outs=[sds((S, D), F32)], epi=lambda acc, res: (acc + res,),
                    extras=[(x2, _tile_spec())], slots=near)
    w2d, tok = diagonal_wait(ag_w2d, [0], x3, "ffn2")
    (w24,) = _forward_halves(w2d, FAR, tok, name="ag_fwdd_ffn2")
    (x3,) = _matmul(act, w24, name="mm_ffn2_far", M=S, N=D, K=dff4, tm=2 * TM, tk=dff4, a_spec=act_shard,
                    b_spec=w2_shard, outs=[sds((S, D), F32)], epi=lambda acc, res: (acc + res,),
                    extras=[(x3, _tile_spec())], slots=far)
    w2_full = w24.reshape(DFF, D)

    ci = lax.axis_index("c")

    def rs_sibling(g4, nm):
        return _sibling_start([g4], False, 1 + big_names.index(nm), name="rs_sib_" + nm)

    def rs_chips(state, after, nm):
        send, recv, g4, land, _ = state
        (((land_, g4_),), _) = _transfer_wait(send, recv, [[land[0], g4[0]]], [(NCHIP, land[0].shape[1])], after,
                                             name="rs_sibwait_" + nm)
        part, buf = _select_half_bf16(g4_, ci, land_, slot, name="rs_add_" + nm)
        return _scatter_start([part], [buf], 1 + 2 * len(big_names) + big_names.index(nm), name="rs_start_" + nm)

    def dw_half(a, b, nm, *, by_rows, hr, cols, which, land, after):
        tile = lambda tm, tn: pl.BlockSpec(
            (None, tm, tn), (lambda j, i, k, s: (i, 0, j)) if by_rows else (lambda j, i, k, s: (j, 0, 0)))
        a_half = lambda tm, tk: pl.BlockSpec(
            (tk, tm), (lambda j, i, k, s: (k, 2 * i + s[0])) if by_rows else (lambda j, i, k, s: (k, s[0])))
        (out,) = _matmul(a, b, name=nm, ta=True, M=NCHIP * hr if by_rows else hr, N=cols if by_rows else NCHIP * cols,
                         K=S, tm=hr, tn=cols, a_spec=a_half, out_specs=lambda tm, tn: [tile(tm, tn)],
                         outs=[sds((NCHIP, hr, cols), BF16)], slots=jnp.reshape(which, (1,)).astype(jnp.int32),
                         epi=None if land is None else (lambda acc, other: (acc + other.astype(F32),)),
                         extras=[] if land is None else [(land, tile)], after=after)
        return out

    def rs_sibling_half(half, nm):
        return _sibling_start([half], True, 1 + big_names.index(nm), name="rs_sib_" + nm)

    def rs_chips_fused(state, grad_half, after, nm):
        send, recv, mine, land, _ = state
        (((land_, _),), tok) = _transfer_wait(send, recv, [[land[0], mine[0]]], [(NCHIP, land[0].shape[1])], after,
                                             name="rs_sibwait_" + nm)
        part = grad_half(land_, [tok])
        buf = _own_slot(part, slot, name="rs_own_" + nm)
        return _scatter_start([part], [buf], 1 + 2 * len(big_names) + big_names.index(nm), name="rs_start_" + nm)

    def rs_end(state, after, nm):
        send, recv, parts, bufs, _ = state
        (((buf, _),), _) = _transfer_wait(send, recv, [[bufs[0], parts[0]]], [(N_PEER_CHIPS, bufs[0].shape[1])], after,
                                          name="rs_wait_" + nm)
        return _sibling_start([buf], True, 1 + len(big_names) + big_names.index(nm), name="rs_share_" + nm)

    big_m = [m_w_in[0].T, m_w_kv[0], m_w_o[0], m_w_ffn1[0], m_w_ffn2[0]]
    big_v = [v_w_in[0].T, v_w_kv[0], v_w_o[0], v_w_ffn1[0], v_w_ffn2[0]]
    big_out = {}

    def rs_finish(k, state, after):
        send, recv, mine, land, _ = state
        nm = big_names[k]
        (((land_, mine_),), _) = _transfer_wait(send, recv, [[land[0], mine[0]]], [(NCHIP, land[0].shape[1])], after,
                                               name="rs_sharewait_" + nm)
        big_out[nm] = _adamw(big[k], mine_, land_, big_m[k], big_v[k], core, name="adamw_" + nm)
        return big_out[nm][1]

    dx3, dx3b, dg_final, loss11 = _loss_bwd(x3, g_final2, tgt, name="loss_bwd")
    dw2_half = lambda which, land, after, nm: dw_half(
        act, dx3b, nm, by_rows=True, hr=dff4 // 2, cols=D, which=which, land=land, after=after)
    sib_w2 = rs_sibling_half(dw2_half(1 - ci, None, [], "mm_dw2_sib"), "w_ffn2")
    (dfb,) = _matmul(dx3b, w2_full, name="mm_dact", tb=True, M=S, N=DFF, K=D, tm=2 * TM, tn=dff4, outs=[sds((S, DFF), BF16)],
                     epi=lambda acc, g: (acc * g.astype(F32),), extras=[(dact_df, _tile_spec())],
                     after=[sib_w2[4]])
    rs_w2 = rs_chips_fused(sib_w2, lambda land, after: dw2_half(ci, land, after, "mm_dw2_own"), dfb, "w_ffn2")

    dw1_half = lambda which, land, after, nm: dw_half(
        h2, dfb, nm, by_rows=False, hr=D // 2, cols=dff4, which=which, land=land, after=after)
    sib_w1 = rs_sibling_half(dw1_half(1 - ci, None, [rs_w2[4]], "mm_dw1_sib"), "w_ffn1")

    def w1_rows(tn, tk):
        kb = dff4 // tk
        return pl.BlockSpec((None, tn, tk), lambda j, i, k: (k // kb, j, k % kb))

    (dh2,) = _matmul(dfb, w14, name="mm_dh2", tb=True, M=S, N=D, K=DFF, tm=2 * TM, b_spec=w1_rows,
                     outs=[sds((S, D), F32)], after=[sib_w1[4]])
    rs_w1 = rs_chips_fused(sib_w1, lambda land, after: dw1_half(ci, land, after, "mm_dw1_own"), dh2, "w_ffn1")
    dx2, dx2b, dg_ffn = _rms_bwd(dh2, x2, g_ffn, dx3, name="rms_ffn_bwd", after=[rs_w1[4]])
    dwo_half = lambda which, land, after, nm: dw_half(
        hn, dx2b, nm, by_rows=True, hr=D // NCHIP // 2, cols=D, which=which, land=land, after=after)
    sib_wo = rs_sibling_half(dwo_half(1 - ci, None, [], "mm_dwo_sib"), "w_o")
    (dhn,) = _matmul(dx2b, w_o_full, name="mm_dhn", tb=True, M=S, N=D, K=D, tm=2 * TM, outs=[sds((S, D), F32)],
                     after=[sib_wo[4]])
    rs_wo = rs_chips_fused(sib_wo, lambda land, after: dwo_half(ci, land, after, "mm_dwo_own"), dhn, "w_o")
    sh_w2 = rs_end(rs_w2, rs_wo[4], "w_ffn2")
    dproj, dkv, dws, dbs8, dlng, dlnb, dcw8, dgh = _mix_bwd(
        dhn, heads, proj, ycv, kv, ws3, bs_t, ln_v_g, ln_v_b, conv_full, g_head, sh_w2[4], name="mix_bwd")
    (dwin_t,) = _matmul(dproj, h, name="mm_dwin", ta=True, M=DIN, N=D, K=S, tm=DIN // 2, outs=[sds((DIN, D), BF16)])
    sib_win = rs_sibling(dwin_t.reshape(NCHIP, din4, D), "w_in")
    (dwkv,) = _matmul(mem_n, dkv, name="mm_dwkv", ta=True, M=D, N=2 * DM, K=NMEM, outs=[sds((D, 2 * DM), BF16)],
                      after=[sib_win[4]])
    sib_wkv = rs_sibling(dwkv.reshape(NCHIP, D // NCHIP, 2 * DM), "w_kv")
    sh_w1 = rs_end(rs_w1, sib_wkv[4], "w_ffn1")
    (dh,) = _matmul(dproj, w_in_t, name="mm_dh", M=S, N=D, K=DIN, tm=2 * TM, tk=DIN, outs=[sds((S, D), F32)],
                    after=[sh_w1[4]])
    rs_win = rs_chips(sib_win, dh, "w_in")
    rs_wkv = rs_chips(sib_wkv, rs_win[4], "w_kv")
    dx, dg_mix = _rms_bwd(dh, x2d, g_mix, dx2, name="rms_mix_bwd", want_bf=False, after=[rs_wkv[4]])
    (dmem_n,) = _matmul(dkv, w_kv_full, name="mm_dmem", tb=True, M=NMEM, N=D, K=2 * DM, outs=[sds((NMEM, D), F32)],
                        after=[dx])
    (dg_mem,) = _rms_bwd(dmem_n, mem2d, g_mem, None, name="rms_mem_bwd", want_dx=False)
    sh_wo = rs_end(rs_wo, dg_mem, "w_o")
    done = rs_finish(4, sh_w2, sh_wo[4])
    done = rs_finish(3, sh_w1, done)
    sh_win = rs_end(rs_win, done, "w_in")
    sh_wkv = rs_end(rs_wkv, sh_win[4], "w_kv")
    done = rs_finish(2, sh_wo, sh_wkv[4])
    done = rs_finish(0, sh_win, done)
    done = rs_finish(1, sh_wkv, done)

    small_names = ["g_mix", "ln_v_g", "ln_v_b", "w_s", "b_s", "conv_w", "g_mem", "g_head", "g_ffn", "g_final"]
    small_part = [dg_mix, dlng, dlnb, dws, dbs8[:, 0, :], dcw8[:3], dg_mem, dgh, dg_ffn, dg_final, loss11]
    small_shapes = [(1, D), (1, DS), (1, DS), (NSH, CHUNK, CHUNK), (NSH, CHUNK), (3, DC), (1, D), (1, D), (1, D), (1, D),
                    (1, 1)]
    total = _allreduce_small(_pack(small_part), done, name="allreduce_small")
    small_g = _unpack(total, small_shapes)
    loss = small_g.pop()[0, 0]
    small_g[5] = lax.dynamic_slice(small_g[5], (0, shard * dcv4), (3, dcv4))
    small_w = [g_mix, ln_v_g, ln_v_b, ws3, bs2, conv_w[0], g_mem, g_head, g_ffn, g_final2]
    small_m = [m_g_mix, m_ln_v_g, m_ln_v_b, m_w_s[0], m_b_s[0], m_conv_w[0], m_g_mem, m_g_head, m_g_ffn,
               m_g_final.reshape(1, D)]
    small_v = [v_g_mix, v_ln_v_g, v_ln_v_b, v_w_s[0], v_b_s[0], v_conv_w[0], v_g_mem, v_g_head, v_g_ffn,
               v_g_final.reshape(1, D)]
    s_delta, s_m, s_v = _adamw_small(small_w, small_g, small_m, small_v, name="adamw_small")
    small_out = {nm: (g, d, mn, vn) for nm, g, d, mn, vn in zip(small_names, small_g, s_delta, s_m, s_v)}

    order = ["g_mix", "w_in", "ln_v_g", "ln_v_b", "w_s", "b_s", "conv_w", "g_mem", "w_kv", "g_head", "w_o",
             "g_ffn", "w_ffn1", "w_ffn2", "g_final"]
    like = dict(g_mix=g_mix, w_in=w_in, ln_v_g=ln_v_g, ln_v_b=ln_v_b, w_s=w_s, b_s=b_s, conv_w=conv_w, g_mem=g_mem,
                w_kv=w_kv, g_head=g_head, w_o=w_o, g_ffn=g_ffn, w_ffn1=w_ffn1, w_ffn2=w_ffn2, g_final=g_final)
    res = {**big_out, **small_out}
    res["w_in"] = [a.T for a in res["w_in"]]
    outs = [loss, dx[None]]
    for k in range(4):
        outs += [res[nm][k].reshape(like[nm].shape) for nm in order]
    return tuple(outs)
```

```python
import math

import jax
import jax.numpy as jnp
from jax import lax
from jax.experimental import pallas as pl
from jax.experimental.pallas import tpu as pltpu

F32 = jnp.float32
BF16 = jnp.bfloat16
MESH = pl.DeviceIdType.MESH

D = 2048
S = 2048
HD = 128
NH = D // HD
NMH = 4
NSH = (NH - NMH) // 2
NCH = NH - NMH - NSH
DS = NSH * HD
DC = NCH * HD
DM = NMH * HD
DIN = 2 * DS + 3 * DC + DM
CHUNK = 128
NMEM = 256
DFF = 4 * D
EPS = 1e-6
NCHIP = 4
SCALE = HD ** -0.5

ADAM_LR = 0.001
ADAM_B1 = 0.9
ADAM_B2 = 0.999
ADAM_EPS = 1e-08
ADAM_WD = 0.01
ADAM_STEP = 10

TR_EW = 256
TR_MIX = 256
TM = 512
TN = 1024
TK = 2048
N_SUB = 512
VMEM_MB = 56
HALO = 8


def _pick(n, target, q=128):
    best = None
    for t in range(q, min(n, target) + 1, q):
        if n % t == 0:
            best = t
    return n if best is None else best


def _pick_rows(n, q):
    below = _pick(n, TR_EW, q)
    if 2 * below >= TR_EW:
        return below
    above = [t for t in range(TR_EW, min(n, 4 * TR_EW) + 1, q) if n % t == 0]
    return above[0] if above else below


def _cp(sem=None, vmem_mb=None, **kw):
    d = dict(kw)
    if sem is not None:
        d["dimension_semantics"] = sem
    if vmem_mb is not None:
        d["vmem_limit_bytes"] = vmem_mb << 20
    return pltpu.CompilerParams(**d)


def _gelu(x):
    z = 0.7978845608028654 * (x + 0.044715 * (x * x * x))
    return 0.5 * x * (1.0 + jnp.tanh(z))


def _gelu_with_grad(x):
    x2 = x * x
    t = jnp.tanh(0.7978845608028654 * (x + 0.044715 * (x2 * x)))
    half = 0.5 * (1.0 + t)
    return x * half, half + 0.5 * x * (1.0 - t * t) * (0.7978845608028654 * (1.0 + 3.0 * 0.044715 * x2))


def _matmul(a, b, *, name, ta=False, tb=False, M, N, K, tm=None, tn=None, tk=None, outs, epi=None,
            extras=(), a_spec=None, b_spec=None, out_specs=None, after=(), n_split=None, slots=None, into=()):
    n_after = len(after)
    tm = _pick(M, TM if tm is None else tm, 8)
    tn = _pick(N, TN if tn is None else tn)
    tk = _pick(K, TK if tk is None else tk)
    if n_split is None:
        n_split = tn // N_SUB if tn % N_SUB == 0 else 1
    nk = K // tk
    grid = (N // tn, M // tm, nk)
    if a_spec is None:
        a_spec = (pl.BlockSpec((tk, tm), lambda j, i, k, *s: (k, i)) if ta
                  else pl.BlockSpec((tm, tk), lambda j, i, k, *s: (i, k)))
    else:
        a_spec = a_spec(tm, tk)
    if b_spec is None:
        b_spec = (pl.BlockSpec((tn, tk), lambda j, i, k, *s: (j, k)) if tb
                  else pl.BlockSpec((tk, tn), lambda j, i, k, *s: (k, j)))
    else:
        b_spec = b_spec(tn, tk)
    if out_specs is None:
        out_specs = [pl.BlockSpec((tm, tn), lambda j, i, k, *s: (i, j)) for _ in outs]
    else:
        out_specs = out_specs(tm, tn)
    dn = (((0 if ta else 1,), (1 if tb else 0,)), ((), ()))
    n_ex, n_out = len(extras), len(outs)
    n_pre = 0 if slots is None else 1
    n_into = len(into)
    ns = tn // n_split

    def body(*refs):
        a_ref, b_ref = refs[n_pre], refs[n_pre + 1]
        ex = refs[n_pre + 2:n_pre + 2 + n_ex]
        first_out = n_pre + 2 + n_ex + n_after + n_into
        o = refs[first_out:first_out + n_out]
        acc = refs[first_out + n_out:]
        k = pl.program_id(2)

        def finish(val, cols):
            res = (val,) if epi is None else epi(val, *[e[:, cols] for e in ex])
            for r, o_ref in zip(res, o):
                o_ref[:, cols] = r.astype(o_ref.dtype)

        if nk > 1:
            @pl.when(k == 0)
            def _():
                acc[0][...] = jnp.zeros_like(acc[0])

        av = a_ref[...].astype(BF16)
        for q in range(n_split):
            cols = slice(q * ns, (q + 1) * ns)
            bq = (b_ref[cols, :] if tb else b_ref[:, cols]).astype(BF16)
            part = lax.dot_general(av, bq, dn, preferred_element_type=F32)
            if nk == 1:
                finish(part, cols)
            else:
                acc[0][:, cols] += part

        if nk > 1:
            @pl.when(k == nk - 1)
            def _():
                finish(acc[0][...], slice(0, tn))

    in_specs = ([a_spec, b_spec] + [sp(tm, tn) for _, sp in extras] + [ANY] * (n_after + n_into))
    scratch = [pltpu.VMEM((tm, tn), F32)] if nk > 1 else []
    args = [a, b] + [arr for arr, _ in extras] + list(after) + list(into)
    aliases = {n_pre + len(args) - n_into + t: t for t in range(n_into)}
    params = _cp(("parallel", "parallel", "arbitrary"), VMEM_MB)
    if slots is None:
        return pl.pallas_call(body, name=name, grid=grid, in_specs=in_specs, out_specs=out_specs, out_shape=outs,
                              scratch_shapes=scratch, input_output_aliases=aliases, compiler_params=params)(*args)
    return pl.pallas_call(
        body, name=name,
        grid_spec=pltpu.PrefetchScalarGridSpec(num_scalar_prefetch=1, grid=grid, in_specs=in_specs,
                                               out_specs=out_specs, scratch_shapes=scratch),
        out_shape=outs, input_output_aliases=aliases, compiler_params=params)(slots, *args)


def _tile_spec():
    return lambda tm, tn: pl.BlockSpec((tm, tn), lambda j, i, k, *s: (i, j))


def _cast_into_slot(w, slot, after, *, name):
    R, C = w.shape
    tr = _pick_rows(R, 16)

    def body(s_ref, w_ref, _after_ref, o_ref):
        o_ref[...] = w_ref[...].astype(BF16)

    return pl.pallas_call(
        body, name=name,
        grid_spec=pltpu.PrefetchScalarGridSpec(
            num_scalar_prefetch=1, grid=(R // tr,),
            in_specs=[pl.BlockSpec((tr, C), lambda i, s: (i, 0)), ANY],
            out_specs=pl.BlockSpec((None, tr, C), lambda i, s: (s[0], i, 0))),
        out_shape=jax.ShapeDtypeStruct((NCHIP, R, C), BF16),
        compiler_params=_cp(("parallel",), VMEM_MB),
    )(slot, w, after)


def _own_slot(part, slot, *, name):
    _, R, C = part.shape
    tr = _pick_rows(R, 16)

    def body(s_ref, p_ref, o_ref):
        o_ref[...] = p_ref[...]

    spec = pl.BlockSpec((None, tr, C), lambda i, s: (s[0], i, 0))
    return pl.pallas_call(
        body, name=name,
        grid_spec=pltpu.PrefetchScalarGridSpec(num_scalar_prefetch=1, grid=(R // tr,), in_specs=[spec],
                                               out_specs=spec),
        out_shape=jax.ShapeDtypeStruct(part.shape, part.dtype),
        compiler_params=_cp(("parallel",), VMEM_MB),
    )(slot, part)


def _rms_fwd(x, g, *, name, after=()):
    R, C = x.shape
    tr = _pick(R, TR_EW, 16)
    n_after = len(after)

    def body(x_ref, g_ref, *rest):
        o_ref = rest[n_after]
        xv = x_ref[...]
        r = lax.rsqrt(jnp.mean(xv * xv, axis=-1, keepdims=True) + EPS)
        o_ref[...] = ((xv * r) * g_ref[...]).astype(BF16)

    return pl.pallas_call(
        body, name=name, grid=(R // tr,),
        in_specs=[pl.BlockSpec((tr, C), lambda i: (i, 0)), pl.BlockSpec((1, C), lambda i: (0, 0))] + [ANY] * n_after,
        out_specs=pl.BlockSpec((tr, C), lambda i: (i, 0)),
        out_shape=jax.ShapeDtypeStruct((R, C), BF16),
        compiler_params=_cp(("parallel",), VMEM_MB),
    )(x, g, *after)


def _rms_bwd(dh, x, g, dres, *, name, want_dx=True, want_bf=True, after=()):
    R, C = x.shape
    tr = _pick(R, TR_EW, 16)
    has_res = dres is not None
    row = pl.BlockSpec((tr, C), lambda i: (i, 0))
    vec = pl.BlockSpec((1, C), lambda i: (0, 0))

    def body(*refs):
        dh_ref, x_ref, g_ref = refs[:3]
        pos = 3
        dres_ref = None
        if has_res:
            dres_ref = refs[pos]
            pos += 1
        outs = refs[pos + len(after):]
        i = pl.program_id(0)
        xv = x_ref[...]
        r = lax.rsqrt(jnp.mean(xv * xv, axis=-1, keepdims=True) + EPS)
        xh = xv * r
        dhv = dh_ref[...]
        dg_ref = outs[-1]
        dgp = jnp.sum(dhv * xh, axis=0, keepdims=True)

        @pl.when(i == 0)
        def _():
            dg_ref[...] = dgp

        @pl.when(i > 0)
        def _():
            dg_ref[...] += dgp

        if want_dx:
            t = dhv * g_ref[...]
            dx = r * (t - xh * jnp.mean(t * xh, axis=-1, keepdims=True))
            if has_res:
                dx = dx + dres_ref[...]
            outs[0][...] = dx
            if want_bf:
                outs[1][...] = dx.astype(BF16)

    in_specs = [row, row, vec] + ([row] if has_res else []) + [ANY] * len(after)
    out_specs, out_shape = [], []
    if want_dx:
        out_specs.append(row)
        out_shape.append(jax.ShapeDtypeStruct((R, C), F32))
        if want_bf:
            out_specs.append(row)
            out_shape.append(jax.ShapeDtypeStruct((R, C), BF16))
    out_specs.append(vec)
    out_shape.append(jax.ShapeDtypeStruct((1, C), F32))
    args = [dh, x, g] + ([dres] if has_res else []) + list(after)
    return pl.pallas_call(
        body, name=name, grid=(R // tr,), in_specs=in_specs, out_specs=out_specs, out_shape=out_shape,
        compiler_params=_cp(("arbitrary",), VMEM_MB),
    )(*args)


def _loss_bwd(x3, g, tgt, *, name):
    R, C = x3.shape
    tr = _pick(R, TR_EW, 16)
    n = R // tr
    row = pl.BlockSpec((tr, C), lambda i: (i, 0))
    vec = pl.BlockSpec((1, C), lambda i: (0, 0))

    def body(x_ref, g_ref, t_ref, dx_ref, dxb_ref, dg_ref, loss_ref, acc_ref):
        i = pl.program_id(0)
        xv = x_ref[...]
        gv = g_ref[...]
        r = lax.rsqrt(jnp.mean(xv * xv, axis=-1, keepdims=True) + EPS)
        xh = xv * r
        e = xh * gv - t_ref[...]
        dy = e * (1.0 / C)
        sq = jnp.sum(e * e, axis=0, keepdims=True)
        dgp = jnp.sum(dy * xh, axis=0, keepdims=True)

        @pl.when(i == 0)
        def _():
            acc_ref[...] = sq
            dg_ref[...] = dgp

        @pl.when(i > 0)
        def _():
            acc_ref[...] += sq
            dg_ref[...] += dgp

        t = dy * gv
        dx = r * (t - xh * jnp.mean(t * xh, axis=-1, keepdims=True))
        dx_ref[...] = dx
        dxb_ref[...] = dx.astype(BF16)

        @pl.when(i == n - 1)
        def _():
            loss_ref[...] = jnp.sum(acc_ref[...], axis=-1, keepdims=True) * (0.5 / C)

    return pl.pallas_call(
        body, name=name, grid=(n,),
        in_specs=[row, vec, row],
        out_specs=[row, row, vec, pl.BlockSpec((1, 1), lambda i: (0, 0))],
        out_shape=[jax.ShapeDtypeStruct((R, C), F32), jax.ShapeDtypeStruct((R, C), BF16),
                   jax.ShapeDtypeStruct((1, C), F32), jax.ShapeDtypeStruct((1, 1), F32)],
        scratch_shapes=[pltpu.VMEM((1, C), F32)],
        compiler_params=_cp(("arbitrary",), VMEM_MB),
    )(x3, g, tgt)


def _offsets():
    u0 = 0
    v0 = DS
    b0 = 2 * DS
    c0 = b0 + DC
    x0 = c0 + DC
    q0 = x0 + DC
    return u0, v0, b0, c0, x0, q0


def _tri_mask(lower):
    r = lax.broadcasted_iota(jnp.int32, (CHUNK, CHUNK), 0)
    c = lax.broadcasted_iota(jnp.int32, (CHUNK, CHUNK), 1)
    return (r >= c) if lower else (c >= r)


def _layer_norm_stats(vg):
    mu = jnp.mean(vg, axis=-1, keepdims=True)
    vc = vg - mu
    rstd = lax.rsqrt(jnp.mean(vc * vc, axis=-1, keepdims=True) + EPS)
    return vc * rstd, rstd


def _softmax_rows(qh, kh):
    s = lax.dot_general(qh, kh, (((1,), (1,)), ((), ())), preferred_element_type=F32)
    m = jnp.max(s, axis=-1, keepdims=True)
    e = jnp.exp(s - m)
    return e / jnp.sum(e, axis=-1, keepdims=True)


def _mix_fwd(proj, kv, w_s, bs_t, ln_g, ln_b, conv_w, g_head, *, name):
    assert DS == DC
    tr = _pick(S, TR_MIX, CHUNK)
    n = S // tr
    nck = tr // CHUNK
    u0, v0, b0, c0, x0, q0 = _offsets()
    hb = tr // HALO

    def body(p_ref, cprev_ref, xprev_ref, kv_ref, ws_ref, bst_ref, lng_ref, lnb_ref, cw_ref, gh_ref,
             heads_ref, hn_ref, ycv_ref, buf_ref):
        i = pl.program_id(0)

        def emit(col, val):
            rs = lax.rsqrt(jnp.mean(val * val, axis=-1, keepdims=True) + EPS)
            heads_ref[:, col:col + HD] = val
            hn_ref[:, col:col + HD] = ((val * rs) * gh_ref[:, col:col + HD]).astype(BF16)

        vhat, _ = _layer_norm_stats(_gelu(p_ref[:, v0:v0 + DS]))
        vnb = (vhat * lng_ref[...] + lnb_ref[...]).astype(BF16)
        low = _tri_mask(True)
        for h in range(NSH):
            wt = jnp.where(low, ws_ref[h], 0.0).astype(BF16)
            bcol = bst_ref[:, h:h + 1]
            parts = []
            for c in range(nck):
                blk = vnb[c * CHUNK:(c + 1) * CHUNK, h * HD:(h + 1) * HD]
                parts.append(jnp.dot(wt, blk, preferred_element_type=F32) + bcol)
            mixed = parts[0] if nck == 1 else jnp.concatenate(parts, axis=0)
            emit(h * HD, _gelu(p_ref[:, u0 + h * HD:u0 + (h + 1) * HD]) * mixed)

        xc = p_ref[:, c0:c0 + DC] * p_ref[:, x0:x0 + DC]
        prev = cprev_ref[...] * xprev_ref[...]
        buf_ref[0:HALO, :] = jnp.where(i > 0, prev, 0.0)
        buf_ref[HALO:HALO + tr, :] = xc
        y = (cw_ref[2:3, :] * xc + cw_ref[1:2, :] * buf_ref[HALO - 1:HALO - 1 + tr, :]
             + cw_ref[0:1, :] * buf_ref[HALO - 2:HALO - 2 + tr, :])
        ycv_ref[...] = y
        cout = p_ref[:, b0:b0 + DC] * y
        for h in range(NCH):
            emit(DS + h * HD, cout[:, h * HD:(h + 1) * HD])

        for h in range(NMH):
            qh = (p_ref[:, q0 + h * HD:q0 + (h + 1) * HD] * SCALE).astype(BF16)
            kh = kv_ref[:, h * HD:(h + 1) * HD].astype(BF16)
            vh = kv_ref[:, DM + h * HD:DM + (h + 1) * HD].astype(BF16)
            p = _softmax_rows(qh, kh)
            emit(DS + DC + h * HD, jnp.dot(p.astype(BF16), vh, preferred_element_type=F32))

    full = lambda shape: pl.BlockSpec(shape, lambda i: (0,) * len(shape))
    halo_c = pl.BlockSpec((HALO, DC), lambda i: (jnp.maximum(i * hb - 1, 0), c0 // DC))
    halo_x = pl.BlockSpec((HALO, DC), lambda i: (jnp.maximum(i * hb - 1, 0), x0 // DC))
    return pl.pallas_call(
        body, name=name, grid=(n,),
        in_specs=[pl.BlockSpec((tr, DIN), lambda i: (i, 0)), halo_c, halo_x,
                  full((NMEM, 2 * DM)), full((NSH, CHUNK, CHUNK)), full((CHUNK, NSH)),
                  full((1, DS)), full((1, DS)), full((3, DC)), full((1, D))],
        out_specs=[pl.BlockSpec((tr, D), lambda i: (i, 0)), pl.BlockSpec((tr, D), lambda i: (i, 0)),
                   pl.BlockSpec((tr, DC), lambda i: (i, 0))],
        out_shape=[jax.ShapeDtypeStruct((S, D), F32), jax.ShapeDtypeStruct((S, D), BF16),
                   jax.ShapeDtypeStruct((S, DC), F32)],
        scratch_shapes=[pltpu.VMEM((tr + HALO, DC), F32)],
        compiler_params=_cp(("parallel",), VMEM_MB),
    )(proj, proj, proj, kv, w_s, bs_t, ln_g, ln_b, conv_w, g_head)


def _mix_bwd(dhn, heads, proj, ycv, kv, w_s, bs_t, ln_g, ln_b, conv_w, g_head, after, *, name):
    assert DS == DC
    tr = _pick(S, TR_MIX, CHUNK)
    n = S // tr
    nck = tr // CHUNK
    u0, v0, b0, c0, x0, q0 = _offsets()
    hb = tr // HALO
    last_hb = S // HALO - 1

    def body(dhn_ref, heads_ref, p_ref, ycv_ref, dhn_nx_ref, heads_nx_ref, b_nx_ref, kv_ref, ws_ref, bst_ref,
             lng_ref, lnb_ref, cw_ref, gh_ref, _after_ref,
             dp_ref, dkv_ref, dws_ref, dbs_ref, dlng_ref, dlnb_ref, dcw_ref, dgh_ref, buf_ref, dvn_ref):
        i = pl.program_id(0)

        @pl.when(i == 0)
        def _():
            dkv_ref[...] = jnp.zeros_like(dkv_ref)
            dws_ref[...] = jnp.zeros_like(dws_ref)
            dbs_ref[...] = jnp.zeros_like(dbs_ref)
            dlng_ref[...] = jnp.zeros_like(dlng_ref)
            dlnb_ref[...] = jnp.zeros_like(dlnb_ref)
            dcw_ref[...] = jnp.zeros_like(dcw_ref)
            dgh_ref[...] = jnp.zeros_like(dgh_ref)

        def head_bwd(a, dn, gh):
            rs = lax.rsqrt(jnp.mean(a * a, axis=-1, keepdims=True) + EPS)
            ah = a * rs
            t = dn * gh
            return rs * (t - ah * jnp.mean(t * ah, axis=-1, keepdims=True)), jnp.sum(dn * ah, axis=0, keepdims=True)

        def head_grad(col):
            da, dg = head_bwd(heads_ref[:, col:col + HD], dhn_ref[:, col:col + HD], gh_ref[:, col:col + HD])
            dgh_ref[:, col:col + HD] += dg
            return da

        vg, dvg_dv = _gelu_with_grad(p_ref[:, v0:v0 + DS])
        vhat, rstd = _layer_norm_stats(vg)
        vnb = (vhat * lng_ref[...] + lnb_ref[...]).astype(BF16)
        low = _tri_mask(True)
        ones = jnp.ones((HALO, HD), BF16)
        for h in range(NSH):
            w_h = ws_ref[h]
            wt = jnp.where(low, w_h, 0.0).astype(BF16)
            bcol = bst_ref[:, h:h + 1]
            da = head_grad(h * HD)
            ug, dug_du = _gelu_with_grad(p_ref[:, u0 + h * HD:u0 + (h + 1) * HD])
            dws = jnp.zeros((CHUNK, CHUNK), F32)
            dbs = jnp.zeros((HALO, CHUNK), F32)
            mixed_parts = []
            for c in range(nck):
                rows = slice(c * CHUNK, (c + 1) * CHUNK)
                blk = vnb[rows, h * HD:(h + 1) * HD]
                mixed_parts.append(jnp.dot(wt, blk, preferred_element_type=F32) + bcol)
                dmb = (da[rows] * ug[rows]).astype(BF16)
                dws = dws + lax.dot_general(dmb, blk, (((1,), (1,)), ((), ())), preferred_element_type=F32)
                dbs = dbs + lax.dot_general(ones, dmb, (((1,), (1,)), ((), ())), preferred_element_type=F32)
                dvn_ref[c * CHUNK:(c + 1) * CHUNK, h * HD:(h + 1) * HD] = lax.dot_general(
                    wt, dmb, (((0,), (0,)), ((), ())), preferred_element_type=F32)
            mixed = mixed_parts[0] if nck == 1 else jnp.concatenate(mixed_parts, axis=0)
            dp_ref[:, u0 + h * HD:u0 + (h + 1) * HD] = ((da * mixed) * dug_du).astype(BF16)
            dws_ref[h] += jnp.where(low, dws, 0.0)
            dbs_ref[h] += dbs
        dvn = dvn_ref[...]
        dlng_ref[...] += jnp.sum(dvn * vhat, axis=0, keepdims=True)
        dlnb_ref[...] += jnp.sum(dvn, axis=0, keepdims=True)
        dvh = dvn * lng_ref[...]
        dvg = rstd * (dvh - jnp.mean(dvh, axis=-1, keepdims=True)
                      - vhat * jnp.mean(dvh * vhat, axis=-1, keepdims=True))
        dp_ref[:, v0:v0 + DS] = (dvg * dvg_dv).astype(BF16)

        dc = jnp.concatenate([head_grad(DS + h * HD) for h in range(NCH)], axis=1)
        dc_nx = jnp.concatenate(
            [head_bwd(heads_nx_ref[:, h * HD:(h + 1) * HD], dhn_nx_ref[:, h * HD:(h + 1) * HD],
                      gh_ref[:, DS + h * HD:DS + (h + 1) * HD])[0] for h in range(NCH)], axis=1)
        bg = p_ref[:, b0:b0 + DC]
        cg = p_ref[:, c0:c0 + DC]
        xin = p_ref[:, x0:x0 + DC]
        dp_ref[:, b0:b0 + DC] = (dc * ycv_ref[...]).astype(BF16)
        dyv = dc * bg
        buf_ref[0:tr, :] = dyv
        buf_ref[tr:tr + HALO, :] = jnp.where(i < n - 1, dc_nx * b_nx_ref[...], 0.0)
        sh1 = buf_ref[1:1 + tr, :]
        sh0 = buf_ref[2:2 + tr, :]
        dxc = cw_ref[2:3, :] * dyv + cw_ref[1:2, :] * sh1 + cw_ref[0:1, :] * sh0
        xc = cg * xin
        dp_ref[:, c0:c0 + DC] = (dxc * xin).astype(BF16)
        dp_ref[:, x0:x0 + DC] = (dxc * cg).astype(BF16)
        dcw_ref[0:1, :] += jnp.sum(sh0 * xc, axis=0, keepdims=True)
        dcw_ref[1:2, :] += jnp.sum(sh1 * xc, axis=0, keepdims=True)
        dcw_ref[2:3, :] += jnp.sum(dyv * xc, axis=0, keepdims=True)

        for h in range(NMH):
            do = head_grad(DS + DC + h * HD).astype(BF16)
            qh = (p_ref[:, q0 + h * HD:q0 + (h + 1) * HD] * SCALE).astype(BF16)
            kh = kv_ref[:, h * HD:(h + 1) * HD].astype(BF16)
            vh = kv_ref[:, DM + h * HD:DM + (h + 1) * HD].astype(BF16)
            p = _softmax_rows(qh, kh)
            dpr = lax.dot_general(do, vh, (((1,), (1,)), ((), ())), preferred_element_type=F32)
            ds = (p * (dpr - jnp.sum(dpr * p, axis=-1, keepdims=True))).astype(BF16)
            dp_ref[:, q0 + h * HD:q0 + (h + 1) * HD] = (
                jnp.dot(ds, kh, preferred_element_type=F32) * SCALE).astype(BF16)
            dkv_ref[:, h * HD:(h + 1) * HD] += lax.dot_general(
                ds, qh, (((0,), (0,)), ((), ())), preferred_element_type=F32)
            dkv_ref[:, DM + h * HD:DM + (h + 1) * HD] += lax.dot_general(
                p.astype(BF16), do, (((0,), (0,)), ((), ())), preferred_element_type=F32)

    full = lambda shape: pl.BlockSpec(shape, lambda i: (0,) * len(shape))
    row = lambda c: pl.BlockSpec((tr, c), lambda i: (i, 0))
    nxt = lambda col: pl.BlockSpec((HALO, DC), lambda i: (jnp.minimum((i + 1) * hb, last_hb), col))
    return pl.pallas_call(
        body, name=name, grid=(n,),
        in_specs=[row(D), row(D), row(DIN), row(DC), nxt(DS // DC), nxt(DS // DC), nxt(b0 // DC),
                  full((NMEM, 2 * DM)), full((NSH, CHUNK, CHUNK)), full((CHUNK, NSH)),
                  full((1, DS)), full((1, DS)), full((3, DC)), full((1, D)), ANY],
        out_specs=[row(DIN), full((NMEM, 2 * DM)), full((NSH, CHUNK, CHUNK)), full((NSH, HALO, CHUNK)),
                   full((1, DS)), full((1, DS)), full((HALO, DC)), full((1, D))],
        out_shape=[jax.ShapeDtypeStruct((S, DIN), BF16), jax.ShapeDtypeStruct((NMEM, 2 * DM), F32),
                   jax.ShapeDtypeStruct((NSH, CHUNK, CHUNK), F32), jax.ShapeDtypeStruct((NSH, HALO, CHUNK), F32),
                   jax.ShapeDtypeStruct((1, DS), F32), jax.ShapeDtypeStruct((1, DS), F32),
                   jax.ShapeDtypeStruct((HALO, DC), F32), jax.ShapeDtypeStruct((1, D), F32)],
        scratch_shapes=[pltpu.VMEM((tr + HALO, DC), F32), pltpu.VMEM((tr, DS), F32)],
        compiler_params=_cp(("arbitrary",), VMEM_MB),
    )(dhn, heads, proj, ycv, dhn, heads, proj, kv, w_s, bs_t, ln_g, ln_b, conv_w, g_head, after)


def _place():
    x, y, c = lax.axis_index("x"), lax.axis_index("y"), lax.axis_index("c")
    chips = [(1 - x, y), (x, 1 - y), (1 - x, 1 - y)]
    return x, y, c, chips


ANY = pl.BlockSpec(memory_space=pl.ANY)


HBM = pl.BlockSpec(memory_space=pltpu.HBM)
SEM = pl.BlockSpec(memory_space=pltpu.SEMAPHORE)
EFFECT = pltpu.SideEffectType.DATAFLOW_SIDE_EFFECTING
N_PEER_CHIPS = 3
N_NEIGHBOUR_CHIPS = 2
CONV_PAD = (32, 256)


def _in_hbm(a):
    return pltpu.with_memory_space_constraint(a, pltpu.HBM)


def _allgather_start(bufs, forwards, after, collective_id, *, name):
    arrs = list(bufs) + list(forwards)
    nw, nb = len(arrs), len(bufs)

    def body(*refs):
        ins, send, recv = refs[:nw], refs[nw + 1:2 * nw + 1], refs[2 * nw + 1:3 * nw + 1]
        token = refs[4 * nw + 1]
        x, y, c, chips = _place()
        s = 2 * x + y
        slots = [2 * cx + cy for cx, cy in chips]
        _handshake([(cx, cy, c) for cx, cy in chips[:N_NEIGHBOUR_CHIPS]])
        for w in range(nb, nw):
            q = arrs[w].shape[1] // 4
            for j in range(N_NEIGHBOUR_CHIPS):
                rows = ins[w].at[slots[j], pl.ds(c * 2 * q + j * q, q)]
                pltpu.make_async_remote_copy(src_ref=rows, dst_ref=rows, send_sem=send[w], recv_sem=recv[w],
                                             device_id=(*chips[1 - j], c), device_id_type=MESH).start()
        for w in range(nb):
            hr = arrs[w].shape[1] // 2
            rows = ins[w].at[s, pl.ds(c * hr, hr)]
            for cx, cy in chips[:N_NEIGHBOUR_CHIPS]:
                pltpu.make_async_remote_copy(src_ref=rows, dst_ref=rows, send_sem=send[w], recv_sem=recv[w],
                                             device_id=(cx, cy, c), device_id_type=MESH).start()
        token[...] = jnp.zeros_like(token)

    res = pl.pallas_call(
        body, name=name,
        in_specs=[HBM] * nw + [ANY],
        out_specs=[SEM] * (2 * nw) + [HBM] * nw + [pl.BlockSpec(memory_space=pltpu.VMEM)],
        out_shape=[pltpu.SemaphoreType.DMA(())] * (2 * nw) + [pltpu.HBM(a.shape, a.dtype) for a in arrs]
        + [jax.ShapeDtypeStruct((8, 128), F32)],
        input_output_aliases={w: 2 * nw + w for w in range(nw)},
        compiler_params=pltpu.CompilerParams(has_side_effects=EFFECT, collective_id=collective_id),
    )(*[_in_hbm(a) for a in arrs], after)
    return res[:nw], res[nw:2 * nw], res[2 * nw:3 * nw], res[3 * nw]


def _handshake(peers):
    barrier = pltpu.get_barrier_semaphore()
    for peer in peers:
        pl.semaphore_signal(barrier, inc=1, device_id=peer, device_id_type=MESH)
    pl.semaphore_wait(barrier, len(peers))


def _scatter_start(parts, bufs, collective_id, *, name):
    nw = len(parts)

    def body(*refs):
        src, dst = refs[:nw], refs[nw:2 * nw]
        send, recv = refs[2 * nw:3 * nw], refs[3 * nw:4 * nw]
        token = refs[6 * nw]
        x, y, c, chips = _place()
        s = 2 * x + y
        _handshake([(cx, cy, c) for cx, cy in chips])
        for w in range(nw):
            for cx, cy in chips:
                pltpu.make_async_remote_copy(src_ref=src[w].at[2 * cx + cy], dst_ref=dst[w].at[s], send_sem=send[w],
                                             recv_sem=recv[w], device_id=(cx, cy, c), device_id_type=MESH).start()
        token[...] = jnp.zeros_like(token)

    res = pl.pallas_call(
        body, name=name,
        in_specs=[HBM] * (2 * nw),
        out_specs=[SEM] * (2 * nw) + [HBM] * (2 * nw) + [pl.BlockSpec(memory_space=pltpu.VMEM)],
        out_shape=[pltpu.SemaphoreType.DMA(())] * (2 * nw) + [pltpu.HBM(a.shape, a.dtype) for a in parts + bufs]
        + [jax.ShapeDtypeStruct((8, 128), F32)],
        input_output_aliases={k: 2 * nw + k for k in range(2 * nw)},
        compiler_params=pltpu.CompilerParams(has_side_effects=EFFECT, collective_id=collective_id),
    )(*[_in_hbm(a) for a in parts + bufs])
    return res[:nw], res[nw:2 * nw], res[2 * nw:3 * nw], res[3 * nw:4 * nw], res[4 * nw]


def _sibling_start(srcs, whole, collective_id, *, name):
    nw = len(srcs)
    lands = [lax.empty((a.shape[0], a.shape[1] if whole else a.shape[1] // 2, a.shape[2]), a.dtype) for a in srcs]

    def body(*refs):
        src, land = refs[:nw], refs[nw:2 * nw]
        send, recv = refs[2 * nw:3 * nw], refs[3 * nw:4 * nw]
        token = refs[6 * nw]
        x, y, c, _ = _place()
        _handshake([(x, y, 1 - c)])
        for w in range(nw):
            hr = srcs[w].shape[1] // 2
            rows = src[w] if whole else src[w].at[:, pl.ds((1 - c) * hr, hr)]
            pltpu.make_async_remote_copy(src_ref=rows, dst_ref=land[w], send_sem=send[w], recv_sem=recv[w],
                                         device_id=(x, y, 1 - c), device_id_type=MESH).start()
        token[...] = jnp.zeros_like(token)

    res = pl.pallas_call(
        body, name=name,
        in_specs=[HBM] * (2 * nw),
        out_specs=[SEM] * (2 * nw) + [HBM] * (2 * nw) + [pl.BlockSpec(memory_space=pltpu.VMEM)],
        out_shape=[pltpu.SemaphoreType.DMA(())] * (2 * nw) + [pltpu.HBM(a.shape, a.dtype) for a in srcs + lands]
        + [jax.ShapeDtypeStruct((8, 128), F32)],
        input_output_aliases={k: 2 * nw + k for k in range(2 * nw)},
        compiler_params=pltpu.CompilerParams(has_side_effects=EFFECT, collective_id=collective_id),
    )(*[_in_hbm(a) for a in srcs + lands])
    return res[:nw], res[nw:2 * nw], res[2 * nw:3 * nw], res[3 * nw:4 * nw], res[4 * nw]


def _transfer_wait(sends, recvs, thru, sizes, after, *, name):
    n = len(sends)
    flat = [a for group in thru for a in group]

    def body(*refs):
        bufs = refs[:len(flat)]
        send = refs[len(flat):len(flat) + n]
        recv = refs[len(flat) + n:len(flat) + 2 * n]
        token = refs[2 * len(flat) + 2 * n + 1]
        token[...] = jnp.zeros_like(token)
        x, y, c, _ = _place()
        pos = 0
        for k in range(n):
            slots, rows = sizes[k]
            region = bufs[pos].at[pl.ds(0, slots), pl.ds(0, rows)]
            pos += len(thru[k])
            cp = pltpu.make_async_remote_copy(src_ref=region, dst_ref=region, send_sem=send[k], recv_sem=recv[k],
                                              device_id=(x, y, 1 - c), device_id_type=MESH)
            cp.wait_send()
            cp.wait_recv()

    res = pl.pallas_call(
        body, name=name,
        in_specs=[HBM] * len(flat) + [SEM] * (2 * n) + [pl.BlockSpec(memory_space=pl.ANY)],
        out_specs=[HBM] * len(flat) + [pl.BlockSpec(memory_space=pltpu.VMEM)],
        out_shape=[pltpu.HBM(a.shape, a.dtype) for a in flat] + [jax.ShapeDtypeStruct((8, 128), F32)],
        input_output_aliases={k: k for k in range(len(flat))},
        compiler_params=pltpu.CompilerParams(has_side_effects=EFFECT),
    )(*flat, *sends, *recvs, after)
    out, pos = [], 0
    for group in thru:
        out.append(res[pos:pos + len(group)])
        pos += len(group)
    return out, res[len(flat)]


def _forward_halves(bufs, which, after, *, name):
    nw = len(bufs)
    n = len(which)

    def body(*refs):
        outs = refs[nw + 1:2 * nw + 1]
        send, recv = refs[2 * nw + 1:]
        x, y, c, chips = _place()
        me, sibling = (x, y, c), (x, y, 1 - c)

        def d2d(w, t, half, to):
            cx, cy = chips[which[t]]
            hr = bufs[w].shape[1] // 2
            rows = outs[w].at[2 * cx + cy, pl.ds(half * hr, hr)]
            return pltpu.make_async_remote_copy(src_ref=rows, dst_ref=rows, send_sem=send.at[n * w + t],
                                                recv_sem=recv.at[n * w + t], device_id=to, device_id_type=MESH)

        passed = [d2d(w, t, c, sibling) for w in range(nw) for t in range(n)]
        for cp in passed:
            cp.start()
        for w in range(nw):
            for t in range(n):
                d2d(w, t, 1 - c, me).wait_recv()
        for cp in passed:
            cp.wait_send()

    return pl.pallas_call(
        body, name=name,
        in_specs=[ANY] * (nw + 1), out_specs=[ANY] * nw,
        out_shape=[jax.ShapeDtypeStruct(a.shape, a.dtype) for a in bufs],
        input_output_aliases={w: w for w in range(nw)},
        scratch_shapes=[pltpu.SemaphoreType.DMA((n * nw,)), pltpu.SemaphoreType.DMA((n * nw,))],
    )(*bufs, after)


def _allreduce_small(p, after, *, name):
    R = p.shape[0]
    hr = R // 2

    def body(p_ref, _after_ref, out_ref, sib_ref, sum_ref, gat_ref, tot_ref, send, recv):
        x, y, c, chips = _place()
        s = 2 * x + y
        sibling = (x, y, 1 - c)
        rows = pl.ds(pl.multiple_of(c * hr, 8), hr)
        swap = pltpu.make_async_remote_copy(src_ref=p_ref, dst_ref=sib_ref, send_sem=send.at[0], recv_sem=recv.at[0],
                                            device_id=sibling, device_id_type=MESH)
        swap.start()
        swap.wait()
        sum_ref[...] = p_ref[...] + sib_ref[...]
        gat_ref[s] = sum_ref[rows, :]
        cps = [pltpu.make_async_remote_copy(src_ref=sum_ref.at[rows], dst_ref=gat_ref.at[s], send_sem=send.at[1 + j],
                                            recv_sem=recv.at[1 + j], device_id=(cx, cy, c), device_id_type=MESH)
               for j, (cx, cy) in enumerate(chips)]
        for cp in cps:
            cp.start()
        for cp in cps:
            cp.wait()
        tot_ref[...] = ((gat_ref[0] + gat_ref[1]) + gat_ref[2]) + gat_ref[3]
        out_ref[rows, :] = tot_ref[...]
        share = pltpu.make_async_remote_copy(src_ref=tot_ref, dst_ref=out_ref.at[rows], send_sem=send.at[4],
                                             recv_sem=recv.at[4], device_id=sibling, device_id_type=MESH)
        share.start()
        share.wait_send()
        other = out_ref.at[pl.ds(pl.multiple_of((1 - c) * hr, 8), hr)]
        pltpu.make_async_remote_copy(src_ref=other, dst_ref=other, send_sem=send.at[4], recv_sem=recv.at[4],
                                     device_id=(x, y, c), device_id_type=MESH).wait_recv()

    vmem = pl.BlockSpec(memory_space=pltpu.VMEM)
    return pl.pallas_call(
        body, name=name, in_specs=[vmem, ANY], out_specs=vmem,
        out_shape=jax.ShapeDtypeStruct((R, 128), F32),
        scratch_shapes=[pltpu.VMEM((R, 128), F32), pltpu.VMEM((R, 128), F32), pltpu.VMEM((NCHIP, hr, 128), F32),
                        pltpu.VMEM((hr, 128), F32), pltpu.SemaphoreType.DMA((5,)), pltpu.SemaphoreType.DMA((5,))],
    )(p, after)


def _select_half_bf16(g, half, add, slot, *, name):
    _, R, C = g.shape
    hr = R // 2
    tr = _pick_rows(hr, 16)
    nb = hr // tr
    sel = jnp.concatenate([jnp.reshape(half, (1,)).astype(jnp.int32), slot])

    def body(s_ref, g_ref, a_ref, o_ref, own_ref):
        val = (g_ref[...].astype(F32) + a_ref[...].astype(F32)).astype(BF16)
        o_ref[...] = val

        @pl.when(pl.program_id(1) == s_ref[1])
        def _():
            own_ref[...] = val

    g_spec = pl.BlockSpec((None, tr, C), lambda i, j, s: (j, s[0] * nb + i, 0))
    o_spec = pl.BlockSpec((None, tr, C), lambda i, j, s: (j, i, 0))
    own_spec = pl.BlockSpec((None, tr, C), lambda i, j, s: (s[1], i, 0))
    shape = jax.ShapeDtypeStruct((NCHIP, hr, C), BF16)
    return pl.pallas_call(
        body, name=name,
        grid_spec=pltpu.PrefetchScalarGridSpec(
            num_scalar_prefetch=1, grid=(nb, NCHIP), in_specs=[g_spec, o_spec], out_specs=[o_spec, own_spec]),
        out_shape=[shape, shape],
        compiler_params=_cp(("parallel", "arbitrary"), VMEM_MB),
    )(sel, g, add)


def _adamw_math(w, g, m, v):
    m = ADAM_B1 * m + (1.0 - ADAM_B1) * g
    v = ADAM_B2 * v + (1.0 - ADAM_B2) * (g * g)
    m_hat = m / (1.0 - ADAM_B1 ** ADAM_STEP)
    v_hat = v / (1.0 - ADAM_B2 ** ADAM_STEP)
    delta = -ADAM_LR * (m_hat / (jnp.sqrt(v_hat) + ADAM_EPS) + ADAM_WD * w)
    return delta, m, v


def _adamw(w, g_mine, g_sib, m, v, core, *, name):
    R, C = w.shape
    hr = R // 2
    tr = _pick_rows(hr, 16)
    while hr // tr < 4 and tr % 32 == 0:
        tr //= 2
    nb = hr // tr
    row = pl.BlockSpec((tr, C), lambda hh, i, c: (hh * nb + i, 0))
    steps = 2 * nb
    RING = 3
    mine = pl.BlockSpec((NCHIP, tr, C), lambda hh, i, c: (0, jnp.where(hh == c[0], i, 0), 0))
    sibs = pl.BlockSpec((NCHIP, tr, C), lambda hh, i, c: (0, jnp.where(hh == c[0], 0, i), 0))

    def slot_sum(ref):
        acc = ref[0].astype(F32) + ref[1].astype(F32)
        for j in range(2, NCHIP):
            acc = acc + ref[j].astype(F32)
        return acc

    def body(c_ref, w_hbm, gm_ref, gs_ref, m_hbm, v_hbm, go_ref, d_ref, mo_ref, vo_ref, w_ring, m_ring, v_ring, sems):
        t = pl.program_id(0) * nb + pl.program_id(1)
        streams = ((w_hbm, w_ring), (m_hbm, m_ring), (v_hbm, v_ring))

        def fetch(a, step):
            src, ring = streams[a]
            return pltpu.make_async_copy(src.at[pl.ds(step * tr, tr)], ring.at[step % RING], sems.at[a, step % RING])

        @pl.when(t == 0)
        def _():
            for a in range(len(streams)):
                for step in range(min(RING - 1, steps)):
                    fetch(a, step).start()

        @pl.when(t + RING - 1 < steps)
        def _():
            for a in range(len(streams)):
                fetch(a, t + RING - 1).start()

        for a in range(len(streams)):
            fetch(a, t).wait()
        cur = t % RING
        gv = jnp.where(pl.program_id(0) == c_ref[0], slot_sum(gm_ref), slot_sum(gs_ref))
        d, mn, vn = _adamw_math(w_ring[cur], gv, m_ring[cur], v_ring[cur])
        go_ref[...] = gv
        d_ref[...] = d
        mo_ref[...] = mn
        vo_ref[...] = vn

    return pl.pallas_call(
        body, name=name,
        grid_spec=pltpu.PrefetchScalarGridSpec(
            num_scalar_prefetch=1, grid=(2, nb),
            in_specs=[ANY, mine, sibs, ANY, ANY], out_specs=[row] * 4,
            scratch_shapes=[pltpu.VMEM((RING, tr, C), F32)] * 3 + [pltpu.SemaphoreType.DMA((3, RING))]),
        out_shape=[jax.ShapeDtypeStruct((R, C), F32)] * 4,
        compiler_params=_cp(("arbitrary", "arbitrary"), VMEM_MB),
    )(core, w, g_mine, g_sib, m, v)


def _adamw_small(ws, gs, ms, vs, *, name):
    n = len(ws)

    def body(*refs):
        w_r, g_r, m_r, v_r = refs[:n], refs[n:2 * n], refs[2 * n:3 * n], refs[3 * n:4 * n]
        d_r, mo_r, vo_r = refs[4 * n:5 * n], refs[5 * n:6 * n], refs[6 * n:7 * n]
        for k in range(n):
            d, mn, vn = _adamw_math(w_r[k][...], g_r[k][...], m_r[k][...], v_r[k][...])
            d_r[k][...] = d
            mo_r[k][...] = mn
            vo_r[k][...] = vn

    shapes = [jax.ShapeDtypeStruct(w.shape, F32) for w in ws]
    res = pl.pallas_call(body, name=name, out_shape=shapes * 3)(*ws, *gs, *ms, *vs)
    return res[:n], res[n:2 * n], res[2 * n:]


_PACK_ROWS = 8


def _pack(parts):
    rows = []
    for a in parts:
        flat = a.reshape(-1)
        n = -(-flat.shape[0] // (_PACK_ROWS * 128)) * (_PACK_ROWS * 128)
        rows.append(jnp.pad(flat, (0, n - flat.shape[0])).reshape(-1, 128))
    total = sum(r.shape[0] for r in rows)
    if total % 16:
        rows.append(jnp.zeros((16 - total % 16, 128), F32))
    return jnp.concatenate(rows, axis=0)


def _unpack(p, shapes):
    out, r = [], 0
    for shp in shapes:
        n = math.prod(shp)
        nr = -(-n // (_PACK_ROWS * 128)) * _PACK_ROWS
        out.append(p[r:r + nr].reshape(-1)[:n].reshape(shp))
        r += nr
    return out


def kernel(x, mem, g_mix, w_in, ln_v_g, ln_v_b, w_s, b_s, conv_w, g_mem, w_kv, g_head, w_o, g_ffn, w_ffn1, w_ffn2, g_final, loss_target, m_g_mix, m_w_in, m_ln_v_g, m_ln_v_b, m_w_s, m_b_s, m_conv_w, m_g_mem, m_w_kv, m_g_head, m_w_o, m_g_ffn, m_w_ffn1, m_w_ffn2, m_g_final, v_g_mix, v_w_in, v_ln_v_g, v_ln_v_b, v_w_s, v_b_s, v_conv_w, v_g_mem, v_w_kv, v_g_head, v_w_o, v_g_ffn, v_w_ffn1, v_w_ffn2, v_g_final):
    sds = jax.ShapeDtypeStruct
    xi, yi = lax.axis_index("x"), lax.axis_index("y")
    shard = 2 * xi + yi
    x2d, mem2d, tgt = x[0], mem[0], loss_target[0]
    ws3, bs2 = w_s[0], b_s[0]
    g_final2 = g_final.reshape(1, D)
    dff4 = DFF // NCHIP
    din4 = DIN // NCHIP
    dcv4 = DC // NCHIP

    big = [w_in[0].T, w_kv[0], w_o[0], w_ffn1[0], w_ffn2[0]]
    big_names = ["w_in", "w_kv", "w_o", "w_ffn1", "w_ffn2"]
    slot = jnp.reshape(shard, (1,)).astype(jnp.int32)
    core = jnp.reshape(lax.axis_index("c"), (1,)).astype(jnp.int32)
    conv_pad = jnp.pad(conv_w[0], ((0, CONV_PAD[0] - 3), (0, CONV_PAD[1] - dcv4)))
    conv_slots = lax.dynamic_update_slice(jnp.zeros((NCHIP,) + CONV_PAD, F32), conv_pad[None], (shard, 0, 0))

    gather_ids = {"in": 16, "kvo": 17, "ffn1": 18, "ffn2": 19, "ffn2d": 20}

    def gather_start(bufs, after, nm, forwards=()):
        return _allgather_start(bufs, forwards, after, gather_ids[nm], name="ag_start_" + nm)

    def gather_wait(state, idx, after, nm):
        send, recv, bufs, _ = state
        got, token = _transfer_wait([send[k] for k in idx], [recv[k] for k in idx], [[bufs[k]] for k in idx],
                                    [(N_NEIGHBOUR_CHIPS, bufs[k].shape[1] // 2) for k in idx], after, name="ag_wait_" + nm)
        return [g[0] for g in got], token

    cast = lambda k, after: _cast_into_slot(big[k], slot, after, name="cast_" + big_names[k])
    ag_in = gather_start([cast(0, slot), conv_slots], slot, "in")
    bs_t = bs2.T

    h = _rms_fwd(x2d, g_mix, name="rms_mix", after=[ag_in[3]])
    mem_n = _rms_fwd(mem2d, g_mem, name="rms_mem", after=[h])
    kvo_b = [cast(1, mem_n)]
    kvo_b.append(cast(2, kvo_b[0]))
    w1_b = cast(3, kvo_b[1])
    w2_b = cast(4, w1_b)
    NEAR, FAR = [0, 1], [2]

    def diagonal_wait(state, ks, after, nm):
        send, recv, bufs, _ = state
        got, token = _transfer_wait([send[k] for k in ks], [recv[k] for k in ks], [[bufs[k]] for k in ks],
                                    [(1, bufs[k].shape[1] // 2) for k in ks], after, name="ag_waitd_" + nm)
        return [g[0] for g in got], token

    got_in, tok = gather_wait(ag_in, [0, 1], w2_b, "in")
    ag_kvo = gather_start(kvo_b, tok, "kvo", forwards=got_in)
    in_n = _forward_halves(ag_kvo[2][2:], NEAR, ag_kvo[3], name="ag_fwdn_in")
    ag_kvo = (ag_kvo[0], ag_kvo[1], list(ag_kvo[2][:2]) + list(in_n), ag_kvo[3])
    in_d, tok = diagonal_wait(ag_kvo, [2, 3], ag_kvo[3], "in")
    win4, conv4 = _forward_halves(in_d, FAR, tok, name="ag_fwdd_in")
    w_in_t = win4.reshape(DIN, D)
    conv_full = conv4[:, :3, :dcv4].transpose(1, 0, 2).reshape(3, DC)

    proj_w = lambda tn, tk: pl.BlockSpec((tn, tk), lambda j, i, k, s: (s[j], k))
    proj_cols = lambda tm, tn: [pl.BlockSpec((tm, tn), lambda j, i, k, s: (i, s[j]))]
    proj_half = lambda which, into, after: _matmul(
        h, w_in_t, name="mm_proj_%d" % which, tb=True, M=S, N=DIN // 2, K=D, tn=DIN // 2, b_spec=proj_w,
        out_specs=proj_cols, outs=[sds((S, DIN), F32)], slots=jnp.full((1,), which, jnp.int32), into=into,
        after=after)[0]
    proj = proj_half(0, [], [ag_kvo[3]])
    got_kvo, tok = gather_wait(ag_kvo, [0, 1], proj, "kvo")
    ag_w1 = gather_start([w1_b], tok, "ffn1", forwards=got_kvo)
    kvo_n = _forward_halves(ag_w1[2][1:], NEAR, ag_w1[3], name="ag_fwdn_kvo")
    ag_w1 = (ag_w1[0], ag_w1[1], [ag_w1[2][0]] + list(kvo_n), ag_w1[3])
    proj = proj_half(1, [proj], list(kvo_n))
    kvo_d, tok = diagonal_wait(ag_w1, [1, 2], proj, "kvo")
    wkv4, wo4 = _forward_halves(kvo_d, FAR, tok, name="ag_fwdd_kvo")
    w_kv_full = wkv4.reshape(D, 2 * DM)
    w_o_full = wo4.reshape(D, D)
    (kv,) = _matmul(mem_n, w_kv_full, name="mm_kv", M=NMEM, N=2 * DM, K=D, outs=[sds((NMEM, 2 * DM), F32)])
    heads, hn, ycv = _mix_fwd(proj, kv, ws3, bs_t, ln_v_g, ln_v_b, conv_full, g_head, name="mix_fwd")
    def residual_and_norm(acc, res, g):
        x2v = acc + res
        r = lax.rsqrt(jnp.mean(x2v * x2v, axis=-1, keepdims=True) + EPS)
        return x2v, (x2v * r) * g

    row_vec = lambda tm, tn: pl.BlockSpec((1, tn), lambda j, i, k, *s: (0, j))
    x2, h2 = _matmul(hn, w_o_full, name="mm_wo", M=S, N=D, K=D, tn=D, n_split=1, epi=residual_and_norm,
                     outs=[sds((S, D), F32), sds((S, D), BF16)], extras=[(x2d, _tile_spec()), (g_ffn, row_vec)])
    near = jnp.stack([shard, 2 * (1 - xi) + yi, 2 * xi + (1 - yi)]).astype(jnp.int32)
    far = jnp.reshape(2 * (1 - xi) + (1 - yi), (1,)).astype(jnp.int32)

    w1_shard = lambda tn, tk: pl.BlockSpec((None, tk, tn), lambda j, i, k, s: (s[j], k, 0))
    act_cols = lambda tm, tn: [pl.BlockSpec((tm, tn), lambda j, i, k, s: (i, s[j]))] * 2

    def relu2(acc):
        r = jnp.maximum(acc, 0.0)
        return r * r, 2.0 * r

    got_w1, tok = gather_wait(ag_w1, [0], h2, "ffn1")
    ag_w2 = gather_start([w2_b], tok, "ffn2", forwards=got_w1)
    (w1n,) = _forward_halves([ag_w2[2][1]], NEAR, ag_w2[3], name="ag_fwdn_ffn1")
    ag_w2 = (ag_w2[0], ag_w2[1], [ag_w2[2][0], w1n], ag_w2[3])
    act, dact_df = _matmul(h2, w1n, name="mm_ffn1_near", M=S, N=3 * dff4, K=D, tm=2 * TM, tn=dff4, b_spec=w1_shard,
                           out_specs=act_cols, outs=[sds((S, DFF), BF16)] * 2, epi=relu2, slots=near)
    w1d, tok = diagonal_wait(ag_w2, [1], act, "ffn1")
    (w14,) = _forward_halves(w1d, FAR, tok, name="ag_fwdd_ffn1")
    act, dact_df = _matmul(h2, w14, name="mm_ffn1_far", M=S, N=dff4, K=D, tm=2 * TM, tn=dff4, b_spec=w1_shard,
                           out_specs=act_cols, outs=[sds((S, DFF), BF16)] * 2, epi=relu2, slots=far,
                           into=[act, dact_df])

    act_shard = lambda tm, tk: pl.BlockSpec((tm, tk), lambda j, i, k, s: (i, s[k]))
    w2_shard = lambda tn, tk: pl.BlockSpec((None, tk, tn), lambda j, i, k, s: (s[k], 0, j))
    got_w2, tok = gather_wait(ag_w2, [0], act, "ffn2")
    ag_w2d = gather_start([], tok, "ffn2d", forwards=got_w2)
    (w2n,) = _forward_halves(ag_w2d[2], NEAR, ag_w2d[3], name="ag_fwdn_ffn2")
    ag_w2d = (ag_w2d[0], ag_w2d[1], [w2n], ag_w2d[3])
    (x3,) = _matmul(act, w2n, name="mm_ffn2_near", M=S, N=D, K=3 * dff4, tm=2 * TM, tk=dff4,
                    a_spec=act_shard, b_spec=w2_shard, outs=[sds((S, D), F32)], epi=lambda acc, res: (acc + res,),
                    extras=[(x2, _tile_spec())], slots=near)
    w2d, tok = diagonal_wait(ag_w2d, [0], x3, "ffn2")
    (w24,) = _forward_halves(w2d, FAR, tok, name="ag_fwdd_ffn2")
    (x3,) = _matmul(act, w24, name="mm_ffn2_far", M=S, N=D, K=dff4, tm=2 * TM, tk=dff4, a_spec=act_shard,
                    b_spec=w2_shard, outs=[sds((S, D), F32)], epi=lambda acc, res: (acc + res,),
                    extras=[(x3, _tile_spec())], slots=far)
    w2_full = w24.reshape(DFF, D)

    ci = lax.axis_index("c")

    def rs_sibling(g4, nm):
        return _sibling_start([g4], False, 1 + big_names.index(nm), name="rs_sib_" + nm)

    def rs_chips(state, after, nm):
        send, recv, g4, land, _ = state
        (((land_, g4_),), _) = _transfer_wait(send, recv, [[land[0], g4[0]]], [(NCHIP, land[0].shape[1])], after,
                                             name="rs_sibwait_" + nm)
        part, buf = _select_half_bf16(g4_, ci, land_, slot, name="rs_add_" + nm)
        return _scatter_start([part], [buf], 1 + 2 * len(big_names) + big_names.index(nm), name="rs_start_" + nm)

    def dw_half(a, b, nm, *, by_rows, hr, cols, which, land, after):
        tile = lambda tm, tn: pl.BlockSpec(
            (None, tm, tn), (lambda j, i, k, s: (i, 0, j)) if by_rows else (lambda j, i, k, s: (j, 0, 0)))
        a_half = lambda tm, tk: pl.BlockSpec(
            (tk, tm), (lambda j, i, k, s: (k, 2 * i + s[0])) if by_rows else (lambda j, i, k, s: (k, s[0])))
        (out,) = _matmul(a, b, name=nm, ta=True, M=NCHIP * hr if by_rows else hr, N=cols if by_rows else NCHIP * cols,
                         K=S, tm=hr, tn=cols, a_spec=a_half, out_specs=lambda tm, tn: [tile(tm, tn)],
                         outs=[sds((NCHIP, hr, cols), BF16)], slots=jnp.reshape(which, (1,)).astype(jnp.int32),
                         epi=None if land is None else (lambda acc, other: (acc + other.astype(F32),)),
                         extras=[] if land is None else [(land, tile)], after=after)
        return out

    def rs_sibling_half(half, nm):
        return _sibling_start([half], True, 1 + big_names.index(nm), name="rs_sib_" + nm)

    def rs_chips_fused(state, grad_half, after, nm):
        send, recv, mine, land, _ = state
        (((land_, _),), tok) = _transfer_wait(send, recv, [[land[0], mine[0]]], [(NCHIP, land[0].shape[1])], after,
                                             name="rs_sibwait_" + nm)
        part = grad_half(land_, [tok])
        buf = _own_slot(part, slot, name="rs_own_" + nm)
        return _scatter_start([part], [buf], 1 + 2 * len(big_names) + big_names.index(nm), name="rs_start_" + nm)

    def rs_end(state, after, nm):
        send, recv, parts, bufs, _ = state
        (((buf, _),), _) = _transfer_wait(send, recv, [[bufs[0], parts[0]]], [(N_PEER_CHIPS, bufs[0].shape[1])], after,
                                          name="rs_wait_" + nm)
        return _sibling_start([buf], True, 1 + len(big_names) + big_names.index(nm), name="rs_share_" + nm)

    big_m = [m_w_in[0].T, m_w_kv[0], m_w_o[0], m_w_ffn1[0], m_w_ffn2[0]]
    big_v = [v_w_in[0].T, v_w_kv[0], v_w_o[0], v_w_ffn1[0], v_w_ffn2[0]]
    big_out = {}

    def rs_finish(k, state, after):
        send, recv, mine, land, _ = state
        nm = big_names[k]
        (((land_, mine_),), _) = _transfer_wait(send, recv, [[land[0], mine[0]]], [(NCHIP, land[0].shape[1])], after,
                                               name="rs_sharewait_" + nm)
        big_out[nm] = _adamw(big[k], mine_, land_, big_m[k], big_v[k], core, name="adamw_" + nm)
        return big_out[nm][1]

    dx3, dx3b, dg_final, loss11 = _loss_bwd(x3, g_final2, tgt, name="loss_bwd")
    dw2_half = lambda which, land, after, nm: dw_half(
        act, dx3b, nm, by_rows=True, hr=dff4 // 2, cols=D, which=which, land=land, after=after)
    sib_w2 = rs_sibling_half(dw2_half(1 - ci, None, [], "mm_dw2_sib"), "w_ffn2")
    (dfb,) = _matmul(dx3b, w2_full, name="mm_dact", tb=True, M=S, N=DFF, K=D, tm=2 * TM, tn=dff4, outs=[sds((S, DFF), BF16)],
                     epi=lambda acc, g: (acc * g.astype(F32),), extras=[(dact_df, _tile_spec())],
                     after=[sib_w2[4]])
    rs_w2 = rs_chips_fused(sib_w2, lambda land, after: dw2_half(ci, land, after, "mm_dw2_own"), dfb, "w_ffn2")

    dw1_half = lambda which, land, after, nm: dw_half(
        h2, dfb, nm, by_rows=False, hr=D // 2, cols=dff4, which=which, land=land, after=after)
    sib_w1 = rs_sibling_half(dw1_half(1 - ci, None, [rs_w2[4]], "mm_dw1_sib"), "w_ffn1")

    def w1_rows(tn, tk):
        kb = dff4 // tk
        return pl.BlockSpec((None, tn, tk), lambda j, i, k: (k // kb, j, k % kb))

    (dh2,) = _matmul(dfb, w14, name="mm_dh2", tb=True, M=S, N=D, K=DFF, tm=2 * TM, b_spec=w1_rows,
                     outs=[sds((S, D), F32)], after=[sib_w1[4]])
    rs_w1 = rs_chips_fused(sib_w1, lambda land, after: dw1_half(ci, land, after, "mm_dw1_own"), dh2, "w_ffn1")
    dx2, dx2b, dg_ffn = _rms_bwd(dh2, x2, g_ffn, dx3, name="rms_ffn_bwd", after=[rs_w1[4]])
    dwo_half = lambda which, land, after, nm: dw_half(
        hn, dx2b, nm, by_rows=True, hr=D // NCHIP // 2, cols=D, which=which, land=land, after=after)
    sib_wo = rs_sibling_half(dwo_half(1 - ci, None, [], "mm_dwo_sib"), "w_o")
    (dhn,) = _matmul(dx2b, w_o_full, name="mm_dhn", tb=True, M=S, N=D, K=D, tm=2 * TM, outs=[sds((S, D), F32)],
                     after=[sib_wo[4]])
    rs_wo = rs_chips_fused(sib_wo, lambda land, after: dwo_half(ci, land, after, "mm_dwo_own"), dhn, "w_o")
    sh_w2 = rs_end(rs_w2, rs_wo[4], "w_ffn2")
    dproj, dkv, dws, dbs8, dlng, dlnb, dcw8, dgh = _mix_bwd(
        dhn, heads, proj, ycv, kv, ws3, bs_t, ln_v_g, ln_v_b, conv_full, g_head, sh_w2[4], name="mix_bwd")
    (dwin_t,) = _matmul(dproj, h, name="mm_dwin", ta=True, M=DIN, N=D, K=S, tm=DIN // 2, outs=[sds((DIN, D), BF16)])
    sib_win = rs_sibling(dwin_t.reshape(NCHIP, din4, D), "w_in")
    (dwkv,) = _matmul(mem_n, dkv, name="mm_dwkv", ta=True, M=D, N=2 * DM, K=NMEM, outs=[sds((D, 2 * DM), BF16)],
                      after=[sib_win[4]])
    sib_wkv = rs_sibling(dwkv.reshape(NCHIP, D // NCHIP, 2 * DM), "w_kv")
    sh_w1 = rs_end(rs_w1, sib_wkv[4], "w_ffn1")
    (dh,) = _matmul(dproj, w_in_t, name="mm_dh", M=S, N=D, K=DIN, tm=2 * TM, tk=DIN, outs=[sds((S, D), F32)],
                    after=[sh_w1[4]])
    rs_win = rs_chips(sib_win, dh, "w_in")
    rs_wkv = rs_chips(sib_wkv, rs_win[4], "w_kv")
    dx, dg_mix = _rms_bwd(dh, x2d, g_mix, dx2, name="rms_mix_bwd", want_bf=False, after=[rs_wkv[4]])
    (dmem_n,) = _matmul(dkv, w_kv_full, name="mm_dmem", tb=True, M=NMEM, N=D, K=2 * DM, outs=[sds((NMEM, D), F32)],
                        after=[dx])
    (dg_mem,) = _rms_bwd(dmem_n, mem2d, g_mem, None, name="rms_mem_bwd", want_dx=False)
    sh_wo = rs_end(rs_wo, dg_mem, "w_o")
    done = rs_finish(4, sh_w2, sh_wo[4])
    done = rs_finish(3, sh_w1, done)
    sh_win = rs_end(rs_win, done, "w_in")
    sh_wkv = rs_end(rs_wkv, sh_win[4], "w_kv")
    done = rs_finish(2, sh_wo, sh_wkv[4])
    done = rs_finish(0, sh_win, done)
    done = rs_finish(1, sh_wkv, done)

    small_names = ["g_mix", "ln_v_g", "ln_v_b", "w_s", "b_s", "conv_w", "g_mem", "g_head", "g_ffn", "g_final"]
    small_part = [dg_mix, dlng, dlnb, dws, dbs8[:, 0, :], dcw8[:3], dg_mem, dgh, dg_ffn, dg_final, loss11]
    small_shapes = [(1, D), (1, DS), (1, DS), (NSH, CHUNK, CHUNK), (NSH, CHUNK), (3, DC), (1, D), (1, D), (1, D), (1, D),
                    (1, 1)]
    total = _allreduce_small(_pack(small_part), done, name="allreduce_small")
    small_g = _unpack(total, small_shapes)
    loss = small_g.pop()[0, 0]
    small_g[5] = lax.dynamic_slice(small_g[5], (0, shard * dcv4), (3, dcv4))
    small_w = [g_mix, ln_v_g, ln_v_b, ws3, bs2, conv_w[0], g_mem, g_head, g_ffn, g_final2]
    small_m = [m_g_mix, m_ln_v_g, m_ln_v_b, m_w_s[0], m_b_s[0], m_conv_w[0], m_g_mem, m_g_head, m_g_ffn,
               m_g_final.reshape(1, D)]
    small_v = [v_g_mix, v_ln_v_g, v_ln_v_b, v_w_s[0], v_b_s[0], v_conv_w[0], v_g_mem, v_g_head, v_g_ffn,
               v_g_final.reshape(1, D)]
    s_delta, s_m, s_v = _adamw_small(small_w, small_g, small_m, small_v, name="adamw_small")
    small_out = {nm: (g, d, mn, vn) for nm, g, d, mn, vn in zip(small_names, small_g, s_delta, s_m, s_v)}

    order = ["g_mix", "w_in", "ln_v_g", "ln_v_b", "w_s", "b_s", "conv_w", "g_mem", "w_kv", "g_head", "w_o",
             "g_ffn", "w_ffn1", "w_ffn2", "g_final"]
    like = dict(g_mix=g_mix, w_in=w_in, ln_v_g=ln_v_g, ln_v_b=ln_v_b, w_s=w_s, b_s=b_s, conv_w=conv_w, g_mem=g_mem,
                w_kv=w_kv, g_head=g_head, w_o=w_o, g_ffn=g_ffn, w_ffn1=w_ffn1, w_ffn2=w_ffn2, g_final=g_final)
    res = {**big_out, **small_out}
    res["w_in"] = [a.T for a in res["w_in"]]
    outs = [loss, dx[None]]
    for k in range(4):
        outs += [res[nm][k].reshape(like[nm].shape) for nm in order]
    return tuple(outs)
```

```python
import math

import jax
import jax.numpy as jnp
from jax import lax
from jax.experimental import pallas as pl
from jax.experimental.pallas import tpu as pltpu

F32 = jnp.float32
BF16 = jnp.bfloat16
MESH = pl.DeviceIdType.MESH

D = 2048
S = 2048
HD = 128
NH = D // HD
NMH = 4
NSH = (NH - NMH) // 2
NCH = NH - NMH - NSH
DS = NSH * HD
DC = NCH * HD
DM = NMH * HD
DIN = 2 * DS + 3 * DC + DM
CHUNK = 128
NMEM = 256
DFF = 4 * D
EPS = 1e-6
NCHIP = 4
SCALE = HD ** -0.5

ADAM_LR = 0.001
ADAM_B1 = 0.9
ADAM_B2 = 0.999
ADAM_EPS = 1e-08
ADAM_WD = 0.01
ADAM_STEP = 10

TR_EW = 256
TR_MIX = 256
TM = 512
TN = 1024
TK = 2048
N_SUB = 512
VMEM_MB = 56
HALO = 8


def _pick(n, target, q=128):
    best = None
    for t in range(q, min(n, target) + 1, q):
        if n % t == 0:
            best = t
    return n if best is None else best


def _pick_rows(n, q):
    below = _pick(n, TR_EW, q)
    if 2 * below >= TR_EW:
        return below
    above = [t for t in range(TR_EW, min(n, 4 * TR_EW) + 1, q) if n % t == 0]
    return above[0] if above else below


def _cp(sem=None, vmem_mb=None, **kw):
    d = dict(kw)
    if sem is not None:
        d["dimension_semantics"] = sem
    if vmem_mb is not None:
        d["vmem_limit_bytes"] = vmem_mb << 20
    return pltpu.CompilerParams(**d)


def _gelu(x):
    z = 0.7978845608028654 * (x + 0.044715 * (x * x * x))
    return 0.5 * x * (1.0 + jnp.tanh(z))


def _gelu_with_grad(x):
    x2 = x * x
    t = jnp.tanh(0.7978845608028654 * (x + 0.044715 * (x2 * x)))
    half = 0.5 * (1.0 + t)
    return x * half, half + 0.5 * x * (1.0 - t * t) * (0.7978845608028654 * (1.0 + 3.0 * 0.044715 * x2))


def _matmul(a, b, *, name, ta=False, tb=False, M, N, K, tm=None, tn=None, tk=None, outs, epi=None,
            extras=(), a_spec=None, b_spec=None, out_specs=None, after=(), n_split=None, slots=None, into=()):
    n_after = len(after)
    tm = _pick(M, TM if tm is None else tm, 8)
    tn = _pick(N, TN if tn is None else tn)
    tk = _pick(K, TK if tk is None else tk)
    if n_split is None:
        n_split = tn // N_SUB if tn % N_SUB == 0 else 1
    nk = K // tk
    grid = (N // tn, M // tm, nk)
    if a_spec is None:
        a_spec = (pl.BlockSpec((tk, tm), lambda j, i, k, *s: (k, i)) if ta
                  else pl.BlockSpec((tm, tk), lambda j, i, k, *s: (i, k)))
    else:
        a_spec = a_spec(tm, tk)
    if b_spec is None:
        b_spec = (pl.BlockSpec((tn, tk), lambda j, i, k, *s: (j, k)) if tb
                  else pl.BlockSpec((tk, tn), lambda j, i, k, *s: (k, j)))
    else:
        b_spec = b_spec(tn, tk)
    if out_specs is None:
        out_specs = [pl.BlockSpec((tm, tn), lambda j, i, k, *s: (i, j)) for _ in outs]
    else:
        out_specs = out_specs(tm, tn)
    dn = (((0 if ta else 1,), (1 if tb else 0,)), ((), ()))
    n_ex, n_out = len(extras), len(outs)
    n_pre = 0 if slots is None else 1
    n_into = len(into)
    ns = tn // n_split

    def body(*refs):
        a_ref, b_ref = refs[n_pre], refs[n_pre + 1]
        ex = refs[n_pre + 2:n_pre + 2 + n_ex]
        first_out = n_pre + 2 + n_ex + n_after + n_into
        o = refs[first_out:first_out + n_out]
        acc = refs[first_out + n_out:]
        k = pl.program_id(2)

        def finish(val, cols):
            res = (val,) if epi is None else epi(val, *[e[:, cols] for e in ex])
            for r, o_ref in zip(res, o):
                o_ref[:, cols] = r.astype(o_ref.dtype)

        if nk > 1:
            @pl.when(k == 0)
            def _():
                acc[0][...] = jnp.zeros_like(acc[0])

        av = a_ref[...].astype(BF16)
        for q in range(n_split):
            cols = slice(q * ns, (q + 1) * ns)
            bq = (b_ref[cols, :] if tb else b_ref[:, cols]).astype(BF16)
            part = lax.dot_general(av, bq, dn, preferred_element_type=F32)
            if nk == 1:
                finish(part, cols)
            else:
                acc[0][:, cols] += part

        if nk > 1:
            @pl.when(k == nk - 1)
            def _():
                finish(acc[0][...], slice(0, tn))

    in_specs = ([a_spec, b_spec] + [sp(tm, tn) for _, sp in extras] + [ANY] * (n_after + n_into))
    scratch = [pltpu.VMEM((tm, tn), F32)] if nk > 1 else []
    args = [a, b] + [arr for arr, _ in extras] + list(after) + list(into)
    aliases = {n_pre + len(args) - n_into + t: t for t in range(n_into)}
    params = _cp(("parallel", "parallel", "arbitrary"), VMEM_MB)
    if slots is None:
        return pl.pallas_call(body, name=name, grid=grid, in_specs=in_specs, out_specs=out_specs, out_shape=outs,
                              scratch_shapes=scratch, input_output_aliases=aliases, compiler_params=params)(*args)
    return pl.pallas_call(
        body, name=name,
        grid_spec=pltpu.PrefetchScalarGridSpec(num_scalar_prefetch=1, grid=grid, in_specs=in_specs,
                                               out_specs=out_specs, scratch_shapes=scratch),
        out_shape=outs, input_output_aliases=aliases, compiler_params=params)(slots, *args)


def _tile_spec():
    return lambda tm, tn: pl.BlockSpec((tm, tn), lambda j, i, k, *s: (i, j))


def _cast_into_slot(w, slot, after, *, name):
    R, C = w.shape
    tr = _pick_rows(R, 16)

    def body(s_ref, w_ref, _after_ref, o_ref):
        o_ref[...] = w_ref[...].astype(BF16)

    return pl.pallas_call(
        body, name=name,
        grid_spec=pltpu.PrefetchScalarGridSpec(
            num_scalar_prefetch=1, grid=(R // tr,),
            in_specs=[pl.BlockSpec((tr, C), lambda i, s: (i, 0)), ANY],
            out_specs=pl.BlockSpec((None, tr, C), lambda i, s: (s[0], i, 0))),
        out_shape=jax.ShapeDtypeStruct((NCHIP, R, C), BF16),
        compiler_params=_cp(("parallel",), VMEM_MB),
    )(slot, w, after)


def _own_slot(part, slot, *, name):
    _, R, C = part.shape
    tr = _pick_rows(R, 16)

    def body(s_ref, p_ref, o_ref):
        o_ref[...] = p_ref[...]

    spec = pl.BlockSpec((None, tr, C), lambda i, s: (s[0], i, 0))
    return pl.pallas_call(
        body, name=name,
        grid_spec=pltpu.PrefetchScalarGridSpec(num_scalar_prefetch=1, grid=(R // tr,), in_specs=[spec],
                                               out_specs=spec),
        out_shape=jax.ShapeDtypeStruct(part.shape, part.dtype),
        compiler_params=_cp(("parallel",), VMEM_MB),
    )(slot, part)


def _rms_fwd(x, g, *, name, after=()):
    R, C = x.shape
    tr = _pick(R, TR_EW, 16)
    n_after = len(after)

    def body(x_ref, g_ref, *rest):
        o_ref = rest[n_after]
        xv = x_ref[...]
        r = lax.rsqrt(jnp.mean(xv * xv, axis=-1, keepdims=True) + EPS)
        o_ref[...] = ((xv * r) * g_ref[...]).astype(BF16)

    return pl.pallas_call(
        body, name=name, grid=(R // tr,),
        in_specs=[pl.BlockSpec((tr, C), lambda i: (i, 0)), pl.BlockSpec((1, C), lambda i: (0, 0))] + [ANY] * n_after,
        out_specs=pl.BlockSpec((tr, C), lambda i: (i, 0)),
        out_shape=jax.ShapeDtypeStruct((R, C), BF16),
        compiler_params=_cp(("parallel",), VMEM_MB),
    )(x, g, *after)


def _rms_bwd(dh, x, g, dres, *, name, want_dx=True, want_bf=True, after=()):
    R, C = x.shape
    tr = _pick(R, TR_EW, 16)
    has_res = dres is not None
    row = pl.BlockSpec((tr, C), lambda i: (i, 0))
    vec = pl.BlockSpec((1, C), lambda i: (0, 0))

    def body(*refs):
        dh_ref, x_ref, g_ref = refs[:3]
        pos = 3
        dres_ref = None
        if has_res:
            dres_ref = refs[pos]
            pos += 1
        outs = refs[pos + len(after):]
        i = pl.program_id(0)
        xv = x_ref[...]
        r = lax.rsqrt(jnp.mean(xv * xv, axis=-1, keepdims=True) + EPS)
        xh = xv * r
        dhv = dh_ref[...]
        dg_ref = outs[-1]
        dgp = jnp.sum(dhv * xh, axis=0, keepdims=True)

        @pl.when(i == 0)
        def _():
            dg_ref[...] = dgp

        @pl.when(i > 0)
        def _():
            dg_ref[...] += dgp

        if want_dx:
            t = dhv * g_ref[...]
            dx = r * (t - xh * jnp.mean(t * xh, axis=-1, keepdims=True))
            if has_res:
                dx = dx + dres_ref[...]
            outs[0][...] = dx
            if want_bf:
                outs[1][...] = dx.astype(BF16)

    in_specs = [row, row, vec] + ([row] if has_res else []) + [ANY] * len(after)
    out_specs, out_shape = [], []
    if want_dx:
        out_specs.append(row)
        out_shape.append(jax.ShapeDtypeStruct((R, C), F32))
        if want_bf:
            out_specs.append(row)
            out_shape.append(jax.ShapeDtypeStruct((R, C), BF16))
    out_specs.append(vec)
    out_shape.append(jax.ShapeDtypeStruct((1, C), F32))
    args = [dh, x, g] + ([dres] if has_res else []) + list(after)
    return pl.pallas_call(
        body, name=name, grid=(R // tr,), in_specs=in_specs, out_specs=out_specs, out_shape=out_shape,
        compiler_params=_cp(("arbitrary",), VMEM_MB),
    )(*args)


def _loss_bwd(x3, g, tgt, *, name):
    R, C = x3.shape
    tr = _pick(R, TR_EW, 16)
    n = R // tr
    row = pl.BlockSpec((tr, C), lambda i: (i, 0))
    vec = pl.BlockSpec((1, C), lambda i: (0, 0))

    def body(x_ref, g_ref, t_ref, dx_ref, dxb_ref, dg_ref, loss_ref, acc_ref):
        i = pl.program_id(0)
        xv = x_ref[...]
        gv = g_ref[...]
        r = lax.rsqrt(jnp.mean(xv * xv, axis=-1, keepdims=True) + EPS)
        xh = xv * r
        e = xh * gv - t_ref[...]
        dy = e * (1.0 / C)
        sq = jnp.sum(e * e, axis=0, keepdims=True)
        dgp = jnp.sum(dy * xh, axis=0, keepdims=True)

        @pl.when(i == 0)
        def _():
            acc_ref[...] = sq
            dg_ref[...] = dgp

        @pl.when(i > 0)
        def _():
            acc_ref[...] += sq
            dg_ref[...] += dgp

        t = dy * gv
        dx = r * (t - xh * jnp.mean(t * xh, axis=-1, keepdims=True))
        dx_ref[...] = dx
        dxb_ref[...] = dx.astype(BF16)

        @pl.when(i == n - 1)
        def _():
            loss_ref[...] = jnp.sum(acc_ref[...], axis=-1, keepdims=True) * (0.5 / C)

    return pl.pallas_call(
        body, name=name, grid=(n,),
        in_specs=[row, vec, row],
        out_specs=[row, row, vec, pl.BlockSpec((1, 1), lambda i: (0, 0))],
        out_shape=[jax.ShapeDtypeStruct((R, C), F32), jax.ShapeDtypeStruct((R, C), BF16),
                   jax.ShapeDtypeStruct((1, C), F32), jax.ShapeDtypeStruct((1, 1), F32)],
        scratch_shapes=[pltpu.VMEM((1, C), F32)],
        compiler_params=_cp(("arbitrary",), VMEM_MB),
    )(x3, g, tgt)


def _offsets():
    u0 = 0
    v0 = DS
    b0 = 2 * DS
    c0 = b0 + DC
    x0 = c0 + DC
    q0 = x0 + DC
    return u0, v0, b0, c0, x0, q0


def _tri_mask(lower):
    r = lax.broadcasted_iota(jnp.int32, (CHUNK, CHUNK), 0)
    c = lax.broadcasted_iota(jnp.int32, (CHUNK, CHUNK), 1)
    return (r >= c) if lower else (c >= r)


def _layer_norm_stats(vg):
    mu = jnp.mean(vg, axis=-1, keepdims=True)
    vc = vg - mu
    rstd = lax.rsqrt(jnp.mean(vc * vc, axis=-1, keepdims=True) + EPS)
    return vc * rstd, rstd


def _softmax_rows(qh, kh):
    s = lax.dot_general(qh, kh, (((1,), (1,)), ((), ())), preferred_element_type=F32)
    m = jnp.max(s, axis=-1, keepdims=True)
    e = jnp.exp(s - m)
    return e / jnp.sum(e, axis=-1, keepdims=True)


def _mix_fwd(proj, kv, w_s, bs_t, ln_g, ln_b, conv_w, g_head, *, name):
    assert DS == DC
    tr = _pick(S, TR_MIX, CHUNK)
    n = S // tr
    nck = tr // CHUNK
    u0, v0, b0, c0, x0, q0 = _offsets()
    hb = tr // HALO

    def body(p_ref, cprev_ref, xprev_ref, kv_ref, ws_ref, bst_ref, lng_ref, lnb_ref, cw_ref, gh_ref,
             heads_ref, hn_ref, ycv_ref, buf_ref):
        i = pl.program_id(0)

        def emit(col, val):
            rs = lax.rsqrt(jnp.mean(val * val, axis=-1, keepdims=True) + EPS)
            heads_ref[:, col:col + HD] = val
            hn_ref[:, col:col + HD] = ((val * rs) * gh_ref[:, col:col + HD]).astype(BF16)

        vhat, _ = _layer_norm_stats(_gelu(p_ref[:, v0:v0 + DS]))
        vnb = (vhat * lng_ref[...] + lnb_ref[...]).astype(BF16)
        low = _tri_mask(True)
        for h in range(NSH):
            wt = jnp.where(low, ws_ref[h], 0.0).astype(BF16)
            bcol = bst_ref[:, h:h + 1]
            parts = []
            for c in range(nck):
                blk = vnb[c * CHUNK:(c + 1) * CHUNK, h * HD:(h + 1) * HD]
                parts.append(jnp.dot(wt, blk, preferred_element_type=F32) + bcol)
            mixed = parts[0] if nck == 1 else jnp.concatenate(parts, axis=0)
            emit(h * HD, _gelu(p_ref[:, u0 + h * HD:u0 + (h + 1) * HD]) * mixed)

        xc = p_ref[:, c0:c0 + DC] * p_ref[:, x0:x0 + DC]
        prev = cprev_ref[...] * xprev_ref[...]
        buf_ref[0:HALO, :] = jnp.where(i > 0, prev, 0.0)
        buf_ref[HALO:HALO + tr, :] = xc
        y = (cw_ref[2:3, :] * xc + cw_ref[1:2, :] * buf_ref[HALO - 1:HALO - 1 + tr, :]
             + cw_ref[0:1, :] * buf_ref[HALO - 2:HALO - 2 + tr, :])
        ycv_ref[...] = y
        cout = p_ref[:, b0:b0 + DC] * y
        for h in range(NCH):
            emit(DS + h * HD, cout[:, h * HD:(h + 1) * HD])

        for h in range(NMH):
            qh = (p_ref[:, q0 + h * HD:q0 + (h + 1) * HD] * SCALE).astype(BF16)
            kh = kv_ref[:, h * HD:(h + 1) * HD].astype(BF16)
            vh = kv_ref[:, DM + h * HD:DM + (h + 1) * HD].astype(BF16)
            p = _softmax_rows(qh, kh)
            emit(DS + DC + h * HD, jnp.dot(p.astype(BF16), vh, preferred_element_type=F32))

    full = lambda shape: pl.BlockSpec(shape, lambda i: (0,) * len(shape))
    halo_c = pl.BlockSpec((HALO, DC), lambda i: (jnp.maximum(i * hb - 1, 0), c0 // DC))
    halo_x = pl.BlockSpec((HALO, DC), lambda i: (jnp.maximum(i * hb - 1, 0), x0 // DC))
    return pl.pallas_call(
        body, name=name, grid=(n,),
        in_specs=[pl.BlockSpec((tr, DIN), lambda i: (i, 0)), halo_c, halo_x,
                  full((NMEM, 2 * DM)), full((NSH, CHUNK, CHUNK)), full((CHUNK, NSH)),
                  full((1, DS)), full((1, DS)), full((3, DC)), full((1, D))],
        out_specs=[pl.BlockSpec((tr, D), lambda i: (i, 0)), pl.BlockSpec((tr, D), lambda i: (i, 0)),
                   pl.BlockSpec((tr, DC), lambda i: (i, 0))],
        out_shape=[jax.ShapeDtypeStruct((S, D), F32), jax.ShapeDtypeStruct((S, D), BF16),
                   jax.ShapeDtypeStruct((S, DC), F32)],
        scratch_shapes=[pltpu.VMEM((tr + HALO, DC), F32)],
        compiler_params=_cp(("parallel",), VMEM_MB),
    )(proj, proj, proj, kv, w_s, bs_t, ln_g, ln_b, conv_w, g_head)


def _mix_bwd(dhn, heads, proj, ycv, kv, w_s, bs_t, ln_g, ln_b, conv_w, g_head, after, *, name):
    assert DS == DC
    tr = _pick(S, TR_MIX, CHUNK)
    n = S // tr
    nck = tr // CHUNK
    u0, v0, b0, c0, x0, q0 = _offsets()
    hb = tr // HALO
    last_hb = S // HALO - 1

    def body(dhn_ref, heads_ref, p_ref, ycv_ref, dhn_nx_ref, heads_nx_ref, b_nx_ref, kv_ref, ws_ref, bst_ref,
             lng_ref, lnb_ref, cw_ref, gh_ref, _after_ref,
             dp_ref, dkv_ref, dws_ref, dbs_ref, dlng_ref, dlnb_ref, dcw_ref, dgh_ref, buf_ref, dvn_ref):
        i = pl.program_id(0)

        @pl.when(i == 0)
        def _():
            dkv_ref[...] = jnp.zeros_like(dkv_ref)
            dws_ref[...] = jnp.zeros_like(dws_ref)
            dbs_ref[...] = jnp.zeros_like(dbs_ref)
            dlng_ref[...] = jnp.zeros_like(dlng_ref)
            dlnb_ref[...] = jnp.zeros_like(dlnb_ref)
            dcw_ref[...] = jnp.zeros_like(dcw_ref)
            dgh_ref[...] = jnp.zeros_like(dgh_ref)

        def head_bwd(a, dn, gh):
            rs = lax.rsqrt(jnp.mean(a * a, axis=-1, keepdims=True) + EPS)
            ah = a * rs
            t = dn * gh
            return rs * (t - ah * jnp.mean(t * ah, axis=-1, keepdims=True)), jnp.sum(dn * ah, axis=0, keepdims=True)

        def head_grad(col):
            da, dg = head_bwd(heads_ref[:, col:col + HD], dhn_ref[:, col:col + HD], gh_ref[:, col:col + HD])
            dgh_ref[:, col:col + HD] += dg
            return da

        vg, dvg_dv = _gelu_with_grad(p_ref[:, v0:v0 + DS])
        vhat, rstd = _layer_norm_stats(vg)
        vnb = (vhat * lng_ref[...] + lnb_ref[...]).astype(BF16)
        low = _tri_mask(True)
        ones = jnp.ones((HALO, HD), BF16)
        for h in range(NSH):
            w_h = ws_ref[h]
            wt = jnp.where(low, w_h, 0.0).astype(BF16)
            bcol = bst_ref[:, h:h + 1]
            da = head_grad(h * HD)
            ug, dug_du = _gelu_with_grad(p_ref[:, u0 + h * HD:u0 + (h + 1) * HD])
            dws = jnp.zeros((CHUNK, CHUNK), F32)
            dbs = jnp.zeros((HALO, CHUNK), F32)
            mixed_parts = []
            for c in range(nck):
                rows = slice(c * CHUNK, (c + 1) * CHUNK)
                blk = vnb[rows, h * HD:(h + 1) * HD]
                mixed_parts.append(jnp.dot(wt, blk, preferred_element_type=F32) + bcol)
                dmb = (da[rows] * ug[rows]).astype(BF16)
                dws = dws + lax.dot_general(dmb, blk, (((1,), (1,)), ((), ())), preferred_element_type=F32)
                dbs = dbs + lax.dot_general(ones, dmb, (((1,), (1,)), ((), ())), preferred_element_type=F32)
                dvn_ref[c * CHUNK:(c + 1) * CHUNK, h * HD:(h + 1) * HD] = lax.dot_general(
                    wt, dmb, (((0,), (0,)), ((), ())), preferred_element_type=F32)
            mixed = mixed_parts[0] if nck == 1 else jnp.concatenate(mixed_parts, axis=0)
            dp_ref[:, u0 + h * HD:u0 + (h + 1) * HD] = ((da * mixed) * dug_du).astype(BF16)
            dws_ref[h] += jnp.where(low, dws, 0.0)
            dbs_ref[h] += dbs
        dvn = dvn_ref[...]
        dlng_ref[...] += jnp.sum(dvn * vhat, axis=0, keepdims=True)
        dlnb_ref[...] += jnp.sum(dvn, axis=0, keepdims=True)
        dvh = dvn * lng_ref[...]
        dvg = rstd * (dvh - jnp.mean(dvh, axis=-1, keepdims=True)
                      - vhat * jnp.mean(dvh * vhat, axis=-1, keepdims=True))
        dp_ref[:, v0:v0 + DS] = (dvg * dvg_dv).astype(BF16)

        dc = jnp.concatenate([head_grad(DS + h * HD) for h in range(NCH)], axis=1)
        dc_nx = jnp.concatenate(
            [head_bwd(heads_nx_ref[:, h * HD:(h + 1) * HD], dhn_nx_ref[:, h * HD:(h + 1) * HD],
                      gh_ref[:, DS + h * HD:DS + (h + 1) * HD])[0] for h in range(NCH)], axis=1)
        bg = p_ref[:, b0:b0 + DC]
        cg = p_ref[:, c0:c0 + DC]
        xin = p_ref[:, x0:x0 + DC]
        dp_ref[:, b0:b0 + DC] = (dc * ycv_ref[...]).astype(BF16)
        dyv = dc * bg
        buf_ref[0:tr, :] = dyv
        buf_ref[tr:tr + HALO, :] = jnp.where(i < n - 1, dc_nx * b_nx_ref[...], 0.0)
        sh1 = buf_ref[1:1 + tr, :]
        sh0 = buf_ref[2:2 + tr, :]
        dxc = cw_ref[2:3, :] * dyv + cw_ref[1:2, :] * sh1 + cw_ref[0:1, :] * sh0
        xc = cg * xin
        dp_ref[:, c0:c0 + DC] = (dxc * xin).astype(BF16)
        dp_ref[:, x0:x0 + DC] = (dxc * cg).astype(BF16)
        dcw_ref[0:1, :] += jnp.sum(sh0 * xc, axis=0, keepdims=True)
        dcw_ref[1:2, :] += jnp.sum(sh1 * xc, axis=0, keepdims=True)
        dcw_ref[2:3, :] += jnp.sum(dyv * xc, axis=0, keepdims=True)

        for h in range(NMH):
            do = head_grad(DS + DC + h * HD).astype(BF16)
            qh = (p_ref[:, q0 + h * HD:q0 + (h + 1) * HD] * SCALE).astype(BF16)
            kh = kv_ref[:, h * HD:(h + 1) * HD].astype(BF16)
            vh = kv_ref[:, DM + h * HD:DM + (h + 1) * HD].astype(BF16)
            p = _softmax_rows(qh, kh)
            dpr = lax.dot_general(do, vh, (((1,), (1,)), ((), ())), preferred_element_type=F32)
            ds = (p * (dpr - jnp.sum(dpr * p, axis=-1, keepdims=True))).astype(BF16)
            dp_ref[:, q0 + h * HD:q0 + (h + 1) * HD] = (
                jnp.dot(ds, kh, preferred_element_type=F32) * SCALE).astype(BF16)
            dkv_ref[:, h * HD:(h + 1) * HD] += lax.dot_general(
                ds, qh, (((0,), (0,)), ((), ())), preferred_element_type=F32)
            dkv_ref[:, DM + h * HD:DM + (h + 1) * HD] += lax.dot_general(
                p.astype(BF16), do, (((0,), (0,)), ((), ())), preferred_element_type=F32)

    full = lambda shape: pl.BlockSpec(shape, lambda i: (0,) * len(shape))
    row = lambda c: pl.BlockSpec((tr, c), lambda i: (i, 0))
    nxt = lambda col: pl.BlockSpec((HALO, DC), lambda i: (jnp.minimum((i + 1) * hb, last_hb), col))
    return pl.pallas_call(
        body, name=name, grid=(n,),
        in_specs=[row(D), row(D), row(DIN), row(DC), nxt(DS // DC), nxt(DS // DC), nxt(b0 // DC),
                  full((NMEM, 2 * DM)), full((NSH, CHUNK, CHUNK)), full((CHUNK, NSH)),
                  full((1, DS)), full((1, DS)), full((3, DC)), full((1, D)), ANY],
        out_specs=[row(DIN), full((NMEM, 2 * DM)), full((NSH, CHUNK, CHUNK)), full((NSH, HALO, CHUNK)),
                   full((1, DS)), full((1, DS)), full((HALO, DC)), full((1, D))],
        out_shape=[jax.ShapeDtypeStruct((S, DIN), BF16), jax.ShapeDtypeStruct((NMEM, 2 * DM), F32),
                   jax.ShapeDtypeStruct((NSH, CHUNK, CHUNK), F32), jax.ShapeDtypeStruct((NSH, HALO, CHUNK), F32),
                   jax.ShapeDtypeStruct((1, DS), F32), jax.ShapeDtypeStruct((1, DS), F32),
                   jax.ShapeDtypeStruct((HALO, DC), F32), jax.ShapeDtypeStruct((1, D), F32)],
        scratch_shapes=[pltpu.VMEM((tr + HALO, DC), F32), pltpu.VMEM((tr, DS), F32)],
        compiler_params=_cp(("arbitrary",), VMEM_MB),
    )(dhn, heads, proj, ycv, dhn, heads, proj, kv, w_s, bs_t, ln_g, ln_b, conv_w, g_head, after)


def _place():
    x, y, c = lax.axis_index("x"), lax.axis_index("y"), lax.axis_index("c")
    chips = [(1 - x, y), (x, 1 - y), (1 - x, 1 - y)]
    return x, y, c, chips


ANY = pl.BlockSpec(memory_space=pl.ANY)


HBM = pl.BlockSpec(memory_space=pltpu.HBM)
SEM = pl.BlockSpec(memory_space=pltpu.SEMAPHORE)
EFFECT = pltpu.SideEffectType.DATAFLOW_SIDE_EFFECTING
N_PEER_CHIPS = 3
N_NEIGHBOUR_CHIPS = 2
CONV_PAD = (32, 256)


def _in_hbm(a):
    return pltpu.with_memory_space_constraint(a, pltpu.HBM)


def _allgather_start(bufs, forwards, after, collective_id, *, name):
    arrs = list(bufs) + list(forwards)
    nw, nb = len(arrs), len(bufs)

    def body(*refs):
        ins, send, recv = refs[:nw], refs[nw + 1:2 * nw + 1], refs[2 * nw + 1:3 * nw + 1]
        token = refs[4 * nw + 1]
        x, y, c, chips = _place()
        s = 2 * x + y
        slots = [2 * cx + cy for cx, cy in chips]
        _handshake([(cx, cy, c) for cx, cy in chips[:N_NEIGHBOUR_CHIPS]])
        for w in range(nb, nw):
            q = arrs[w].shape[1] // 4
            for j in range(N_NEIGHBOUR_CHIPS):
                rows = ins[w].at[slots[j], pl.ds(c * 2 * q + j * q, q)]
                pltpu.make_async_remote_copy(src_ref=rows, dst_ref=rows, send_sem=send[w], recv_sem=recv[w],
                                             device_id=(*chips[1 - j], c), device_id_type=MESH).start()
        for w in range(nb):
            hr = arrs[w].shape[1] // 2
            rows = ins[w].at[s, pl.ds(c * hr, hr)]
            for cx, cy in chips[:N_NEIGHBOUR_CHIPS]:
                pltpu.make_async_remote_copy(src_ref=rows, dst_ref=rows, send_sem=send[w], recv_sem=recv[w],
                                             device_id=(cx, cy, c), device_id_type=MESH).start()
        token[...] = jnp.zeros_like(token)

    res = pl.pallas_call(
        body, name=name,
        in_specs=[HBM] * nw + [ANY],
        out_specs=[SEM] * (2 * nw) + [HBM] * nw + [pl.BlockSpec(memory_space=pltpu.VMEM)],
        out_shape=[pltpu.SemaphoreType.DMA(())] * (2 * nw) + [pltpu.HBM(a.shape, a.dtype) for a in arrs]
        + [jax.ShapeDtypeStruct((8, 128), F32)],
        input_output_aliases={w: 2 * nw + w for w in range(nw)},
        compiler_params=pltpu.CompilerParams(has_side_effects=EFFECT, collective_id=collective_id),
    )(*[_in_hbm(a) for a in arrs], after)
    return res[:nw], res[nw:2 * nw], res[2 * nw:3 * nw], res[3 * nw]


def _handshake(peers):
    barrier = pltpu.get_barrier_semaphore()
    for peer in peers:
        pl.semaphore_signal(barrier, inc=1, device_id=peer, device_id_type=MESH)
    pl.semaphore_wait(barrier, len(peers))


def _scatter_start(parts, bufs, collective_id, *, name):
    nw = len(parts)

    def body(*refs):
        src, dst = refs[:nw], refs[nw:2 * nw]
        send, recv = refs[2 * nw:3 * nw], refs[3 * nw:4 * nw]
        token = refs[6 * nw]
        x, y, c, chips = _place()
        s = 2 * x + y
        _handshake([(cx, cy, c) for cx, cy in chips])
        for w in range(nw):
            for cx, cy in chips:
                pltpu.make_async_remote_copy(src_ref=src[w].at[2 * cx + cy], dst_ref=dst[w].at[s], send_sem=send[w],
                                             recv_sem=recv[w], device_id=(cx, cy, c), device_id_type=MESH).start()
        token[...] = jnp.zeros_like(token)

    res = pl.pallas_call(
        body, name=name,
        in_specs=[HBM] * (2 * nw),
        out_specs=[SEM] * (2 * nw) + [HBM] * (2 * nw) + [pl.BlockSpec(memory_space=pltpu.VMEM)],
        out_shape=[pltpu.SemaphoreType.DMA(())] * (2 * nw) + [pltpu.HBM(a.shape, a.dtype) for a in parts + bufs]
        + [jax.ShapeDtypeStruct((8, 128), F32)],
        input_output_aliases={k: 2 * nw + k for k in range(2 * nw)},
        compiler_params=pltpu.CompilerParams(has_side_effects=EFFECT, collective_id=collective_id),
    )(*[_in_hbm(a) for a in parts + bufs])
    return res[:nw], res[nw:2 * nw], res[2 * nw:3 * nw], res[3 * nw:4 * nw], res[4 * nw]


def _sibling_start(srcs, whole, collective_id, *, name):
    nw = len(srcs)
    lands = [lax.empty((a.shape[0], a.shape[1] if whole else a.shape[1] // 2, a.shape[2]), a.dtype) for a in srcs]

    def body(*refs):
        src, land = refs[:nw], refs[nw:2 * nw]
        send, recv = refs[2 * nw:3 * nw], refs[3 * nw:4 * nw]
        token = refs[6 * nw]
        x, y, c, _ = _place()
        _handshake([(x, y, 1 - c)])
        for w in range(nw):
            hr = srcs[w].shape[1] // 2
            rows = src[w] if whole else src[w].at[:, pl.ds((1 - c) * hr, hr)]
            pltpu.make_async_remote_copy(src_ref=rows, dst_ref=land[w], send_sem=send[w], recv_sem=recv[w],
                                         device_id=(x, y, 1 - c), device_id_type=MESH).start()
        token[...] = jnp.zeros_like(token)

    res = pl.pallas_call(
        body, name=name,
        in_specs=[HBM] * (2 * nw),
        out_specs=[SEM] * (2 * nw) + [HBM] * (2 * nw) + [pl.BlockSpec(memory_space=pltpu.VMEM)],
        out_shape=[pltpu.SemaphoreType.DMA(())] * (2 * nw) + [pltpu.HBM(a.shape, a.dtype) for a in srcs + lands]
        + [jax.ShapeDtypeStruct((8, 128), F32)],
        input_output_aliases={k: 2 * nw + k for k in range(2 * nw)},
        compiler_params=pltpu.CompilerParams(has_side_effects=EFFECT, collective_id=collective_id),
    )(*[_in_hbm(a) for a in srcs + lands])
    return res[:nw], res[nw:2 * nw], res[2 * nw:3 * nw], res[3 * nw:4 * nw], res[4 * nw]


def _transfer_wait(sends, recvs, thru, sizes, after, *, name):
    n = len(sends)
    flat = [a for group in thru for a in group]

    def body(*refs):
        bufs = refs[:len(flat)]
        send = refs[len(flat):len(flat) + n]
        recv = refs[len(flat) + n:len(flat) + 2 * n]
        token = refs[2 * len(flat) + 2 * n + 1]
        token[...] = jnp.zeros_like(token)
        x, y, c, _ = _place()
        pos = 0
        for k in range(n):
            slots, rows = sizes[k]
            region = bufs[pos].at[pl.ds(0, slots), pl.ds(0, rows)]
            pos += len(thru[k])
            cp = pltpu.make_async_remote_copy(src_ref=region, dst_ref=region, send_sem=send[k], recv_sem=recv[k],
                                              device_id=(x, y, 1 - c), device_id_type=MESH)
            cp.wait_send()
            cp.wait_recv()

    res = pl.pallas_call(
        body, name=name,
        in_specs=[HBM] * len(flat) + [SEM] * (2 * n) + [pl.BlockSpec(memory_space=pl.ANY)],
        out_specs=[HBM] * len(flat) + [pl.BlockSpec(memory_space=pltpu.VMEM)],
        out_shape=[pltpu.HBM(a.shape, a.dtype) for a in flat] + [jax.ShapeDtypeStruct((8, 128), F32)],
        input_output_aliases={k: k for k in range(len(flat))},
        compiler_params=pltpu.CompilerParams(has_side_effects=EFFECT),
    )(*flat, *sends, *recvs, after)
    out, pos = [], 0
    for group in thru:
        out.append(res[pos:pos + len(group)])
        pos += len(group)
    return out, res[len(flat)]


def _forward_halves(bufs, which, after, *, name):
    nw = len(bufs)
    n = len(which)

    def body(*refs):
        outs = refs[nw + 1:2 * nw + 1]
        send, recv = refs[2 * nw + 1:]
        x, y, c, chips = _place()
        me, sibling = (x, y, c), (x, y, 1 - c)

        def d2d(w, t, half, to):
            cx, cy = chips[which[t]]
            hr = bufs[w].shape[1] // 2
            rows = outs[w].at[2 * cx + cy, pl.ds(half * hr, hr)]
            return pltpu.make_async_remote_copy(src_ref=rows, dst_ref=rows, send_sem=send.at[n * w + t],
                                                recv_sem=recv.at[n * w + t], device_id=to, device_id_type=MESH)

        passed = [d2d(w, t, c, sibling) for w in range(nw) for t in range(n)]
        for cp in passed:
            cp.start()
        for w in range(nw):
            for t in range(n):
                d2d(w, t, 1 - c, me).wait_recv()
        for cp in passed:
            cp.wait_send()

    return pl.pallas_call(
        body, name=name,
        in_specs=[ANY] * (nw + 1), out_specs=[ANY] * nw,
        out_shape=[jax.ShapeDtypeStruct(a.shape, a.dtype) for a in bufs],
        input_output_aliases={w: w for w in range(nw)},
        scratch_shapes=[pltpu.SemaphoreType.DMA((n * nw,)), pltpu.SemaphoreType.DMA((n * nw,))],
    )(*bufs, after)


def _allreduce_small(p, after, *, name):
    R = p.shape[0]
    hr = R // 2

    def body(p_ref, _after_ref, out_ref, sib_ref, sum_ref, gat_ref, tot_ref, send, recv):
        x, y, c, chips = _place()
        s = 2 * x + y
        sibling = (x, y, 1 - c)
        rows = pl.ds(pl.multiple_of(c * hr, 8), hr)
        swap = pltpu.make_async_remote_copy(src_ref=p_ref, dst_ref=sib_ref, send_sem=send.at[0], recv_sem=recv.at[0],
                                            device_id=sibling, device_id_type=MESH)
        swap.start()
        swap.wait()
        sum_ref[...] = p_ref[...] + sib_ref[...]
        gat_ref[s] = sum_ref[rows, :]
        cps = [pltpu.make_async_remote_copy(src_ref=sum_ref.at[rows], dst_ref=gat_ref.at[s], send_sem=send.at[1 + j],
                                            recv_sem=recv.at[1 + j], device_id=(cx, cy, c), device_id_type=MESH)
               for j, (cx, cy) in enumerate(chips)]
        for cp in cps:
            cp.start()
        for cp in cps:
            cp.wait()
        tot_ref[...] = ((gat_ref[0] + gat_ref[1]) + gat_ref[2]) + gat_ref[3]
        out_ref[rows, :] = tot_ref[...]
        share = pltpu.make_async_remote_copy(src_ref=tot_ref, dst_ref=out_ref.at[rows], send_sem=send.at[4],
                                             recv_sem=recv.at[4], device_id=sibling, device_id_type=MESH)
        share.start()
        share.wait_send()
        other = out_ref.at[pl.ds(pl.multiple_of((1 - c) * hr, 8), hr)]
        pltpu.make_async_remote_copy(src_ref=other, dst_ref=other, send_sem=send.at[4], recv_sem=recv.at[4],
                                     device_id=(x, y, c), device_id_type=MESH).wait_recv()

    vmem = pl.BlockSpec(memory_space=pltpu.VMEM)
    return pl.pallas_call(
        body, name=name, in_specs=[vmem, ANY], out_specs=vmem,
        out_shape=jax.ShapeDtypeStruct((R, 128), F32),
        scratch_shapes=[pltpu.VMEM((R, 128), F32), pltpu.VMEM((R, 128), F32), pltpu.VMEM((NCHIP, hr, 128), F32),
                        pltpu.VMEM((hr, 128), F32), pltpu.SemaphoreType.DMA((5,)), pltpu.SemaphoreType.DMA((5,))],
    )(p, after)


def _select_half_bf16(g, half, add, slot, *, name):
    _, R, C = g.shape
    hr = R // 2
    tr = _pick_rows(hr, 16)
    nb = hr // tr
    sel = jnp.concatenate([jnp.reshape(half, (1,)).astype(jnp.int32), slot])

    def body(s_ref, g_ref, a_ref, o_ref, own_ref):
        val = (g_ref[...].astype(F32) + a_ref[...].astype(F32)).astype(BF16)
        o_ref[...] = val

        @pl.when(pl.program_id(1) == s_ref[1])
        def _():
            own_ref[...] = val

    g_spec = pl.BlockSpec((None, tr, C), lambda i, j, s: (j, s[0] * nb + i, 0))
    o_spec = pl.BlockSpec((None, tr, C), lambda i, j, s: (j, i, 0))
    own_spec = pl.BlockSpec((None, tr, C), lambda i, j, s: (s[1], i, 0))
    shape = jax.ShapeDtypeStruct((NCHIP, hr, C), BF16)
    return pl.pallas_call(
        body, name=name,
        grid_spec=pltpu.PrefetchScalarGridSpec(
            num_scalar_prefetch=1, grid=(nb, NCHIP), in_specs=[g_spec, o_spec], out_specs=[o_spec, own_spec]),
        out_shape=[shape, shape],
        compiler_params=_cp(("parallel", "arbitrary"), VMEM_MB),
    )(sel, g, add)


def _adamw_math(w, g, m, v):
    m = ADAM_B1 * m + (1.0 - ADAM_B1) * g
    v = ADAM_B2 * v + (1.0 - ADAM_B2) * (g * g)
    m_hat = m / (1.0 - ADAM_B1 ** ADAM_STEP)
    v_hat = v / (1.0 - ADAM_B2 ** ADAM_STEP)
    delta = -ADAM_LR * (m_hat / (jnp.sqrt(v_hat) + ADAM_EPS) + ADAM_WD * w)
    return delta, m, v


def _adamw(w, g_mine, g_sib, m, v, core, *, name):
    R, C = w.shape
    hr = R // 2
    tr = _pick_rows(hr, 16)
    if tr == hr and tr % 32 == 0:
        tr //= 2
    nb = hr // tr
    row = pl.BlockSpec((tr, C), lambda hh, i, c: (hh * nb + i, 0))
    steps = 2 * nb
    RING = 3
    mine = pl.BlockSpec((NCHIP, tr, C), lambda hh, i, c: (0, jnp.where(hh == c[0], i, 0), 0))
    sibs = pl.BlockSpec((NCHIP, tr, C), lambda hh, i, c: (0, jnp.where(hh == c[0], 0, i), 0))

    def slot_sum(ref):
        acc = ref[0].astype(F32) + ref[1].astype(F32)
        for j in range(2, NCHIP):
            acc = acc + ref[j].astype(F32)
        return acc

    def body(c_ref, w_hbm, gm_ref, gs_ref, m_hbm, v_hbm, go_ref, d_ref, mo_ref, vo_ref, w_ring, m_ring, v_ring, sems):
        t = pl.program_id(0) * nb + pl.program_id(1)
        streams = ((w_hbm, w_ring), (m_hbm, m_ring), (v_hbm, v_ring))

        def fetch(a, step):
            src, ring = streams[a]
            return pltpu.make_async_copy(src.at[pl.ds(step * tr, tr)], ring.at[step % RING], sems.at[a, step % RING])

        @pl.when(t == 0)
        def _():
            for a in range(len(streams)):
                for step in range(min(RING - 1, steps)):
                    fetch(a, step).start()

        @pl.when(t + RING - 1 < steps)
        def _():
            for a in range(len(streams)):
                fetch(a, t + RING - 1).start()

        for a in range(len(streams)):
            fetch(a, t).wait()
        cur = t % RING
        gv = jnp.where(pl.program_id(0) == c_ref[0], slot_sum(gm_ref), slot_sum(gs_ref))
        d, mn, vn = _adamw_math(w_ring[cur], gv, m_ring[cur], v_ring[cur])
        go_ref[...] = gv
        d_ref[...] = d
        mo_ref[...] = mn
        vo_ref[...] = vn

    return pl.pallas_call(
        body, name=name,
        grid_spec=pltpu.PrefetchScalarGridSpec(
            num_scalar_prefetch=1, grid=(2, nb),
            in_specs=[ANY, mine, sibs, ANY, ANY], out_specs=[row] * 4,
            scratch_shapes=[pltpu.VMEM((RING, tr, C), F32)] * 3 + [pltpu.SemaphoreType.DMA((3, RING))]),
        out_shape=[jax.ShapeDtypeStruct((R, C), F32)] * 4,
        compiler_params=_cp(("arbitrary", "arbitrary"), VMEM_MB),
    )(core, w, g_mine, g_sib, m, v)


def _adamw_small(ws, gs, ms, vs, *, name):
    n = len(ws)

    def body(*refs):
        w_r, g_r, m_r, v_r = refs[:n], refs[n:2 * n], refs[2 * n:3 * n], refs[3 * n:4 * n]
        d_r, mo_r, vo_r = refs[4 * n:5 * n], refs[5 * n:6 * n], refs[6 * n:7 * n]
        for k in range(n):
            d, mn, vn = _adamw_math(w_r[k][...], g_r[k][...], m_r[k][...], v_r[k][...])
            d_r[k][...] = d
            mo_r[k][...] = mn
            vo_r[k][...] = vn

    shapes = [jax.ShapeDtypeStruct(w.shape, F32) for w in ws]
    res = pl.pallas_call(body, name=name, out_shape=shapes * 3)(*ws, *gs, *ms, *vs)
    return res[:n], res[n:2 * n], res[2 * n:]


_PACK_ROWS = 8


def _pack(parts):
    rows = []
    for a in parts:
        flat = a.reshape(-1)
        n = -(-flat.shape[0] // (_PACK_ROWS * 128)) * (_PACK_ROWS * 128)
        rows.append(jnp.pad(flat, (0, n - flat.shape[0])).reshape(-1, 128))
    total = sum(r.shape[0] for r in rows)
    if total % 16:
        rows.append(jnp.zeros((16 - total % 16, 128), F32))
    return jnp.concatenate(rows, axis=0)


def _unpack(p, shapes):
    out, r = [], 0
    for shp in shapes:
        n = math.prod(shp)
        nr = -(-n // (_PACK_ROWS * 128)) * _PACK_ROWS
        out.append(p[r:r + nr].reshape(-1)[:n].reshape(shp))
        r += nr
    return out


def kernel(x, mem, g_mix, w_in, ln_v_g, ln_v_b, w_s, b_s, conv_w, g_mem, w_kv, g_head, w_o, g_ffn, w_ffn1, w_ffn2, g_final, loss_target, m_g_mix, m_w_in, m_ln_v_g, m_ln_v_b, m_w_s, m_b_s, m_conv_w, m_g_mem, m_w_kv, m_g_head, m_w_o, m_g_ffn, m_w_ffn1, m_w_ffn2, m_g_final, v_g_mix, v_w_in, v_ln_v_g, v_ln_v_b, v_w_s, v_b_s, v_conv_w, v_g_mem, v_w_kv, v_g_head, v_w_o, v_g_ffn, v_w_ffn1, v_w_ffn2, v_g_final):
    sds = jax.ShapeDtypeStruct
    xi, yi = lax.axis_index("x"), lax.axis_index("y")
    shard = 2 * xi + yi
    x2d, mem2d, tgt = x[0], mem[0], loss_target[0]
    ws3, bs2 = w_s[0], b_s[0]
    g_final2 = g_final.reshape(1, D)
    dff4 = DFF // NCHIP
    din4 = DIN // NCHIP
    dcv4 = DC // NCHIP

    big = [w_in[0].T, w_kv[0], w_o[0], w_ffn1[0], w_ffn2[0]]
    big_names = ["w_in", "w_kv", "w_o", "w_ffn1", "w_ffn2"]
    slot = jnp.reshape(shard, (1,)).astype(jnp.int32)
    core = jnp.reshape(lax.axis_index("c"), (1,)).astype(jnp.int32)
    conv_pad = jnp.pad(conv_w[0], ((0, CONV_PAD[0] - 3), (0, CONV_PAD[1] - dcv4)))
    conv_slots = lax.dynamic_update_slice(jnp.zeros((NCHIP,) + CONV_PAD, F32), conv_pad[None], (shard, 0, 0))

    gather_ids = {"in": 16, "kvo": 17, "ffn1": 18, "ffn2": 19, "ffn2d": 20}

    def gather_start(bufs, after, nm, forwards=()):
        return _allgather_start(bufs, forwards, after, gather_ids[nm], name="ag_start_" + nm)

    def gather_wait(state, idx, after, nm):
        send, recv, bufs, _ = state
        got, token = _transfer_wait([send[k] for k in idx], [recv[k] for k in idx], [[bufs[k]] for k in idx],
                                    [(N_NEIGHBOUR_CHIPS, bufs[k].shape[1] // 2) for k in idx], after, name="ag_wait_" + nm)
        return [g[0] for g in got], token

    cast = lambda k, after: _cast_into_slot(big[k], slot, after, name="cast_" + big_names[k])
    ag_in = gather_start([cast(0, slot), conv_slots], slot, "in")
    bs_t = bs2.T

    h = _rms_fwd(x2d, g_mix, name="rms_mix", after=[ag_in[3]])
    mem_n = _rms_fwd(mem2d, g_mem, name="rms_mem", after=[h])
    kvo_b = [cast(1, mem_n)]
    kvo_b.append(cast(2, kvo_b[0]))
    w1_b = cast(3, kvo_b[1])
    w2_b = cast(4, w1_b)
    NEAR, FAR = [0, 1], [2]

    def diagonal_wait(state, ks, after, nm):
        send, recv, bufs, _ = state
        got, token = _transfer_wait([send[k] for k in ks], [recv[k] for k in ks], [[bufs[k]] for k in ks],
                                    [(1, bufs[k].shape[1] // 2) for k in ks], after, name="ag_waitd_" + nm)
        return [g[0] for g in got], token

    got_in, tok = gather_wait(ag_in, [0, 1], w2_b, "in")
    ag_kvo = gather_start(kvo_b, tok, "kvo", forwards=got_in)
    in_n = _forward_halves(ag_kvo[2][2:], NEAR, ag_kvo[3], name="ag_fwdn_in")
    ag_kvo = (ag_kvo[0], ag_kvo[1], list(ag_kvo[2][:2]) + list(in_n), ag_kvo[3])
    in_d, tok = diagonal_wait(ag_kvo, [2, 3], ag_kvo[3], "in")
    win4, conv4 = _forward_halves(in_d, FAR, tok, name="ag_fwdd_in")
    w_in_t = win4.reshape(DIN, D)
    conv_full = conv4[:, :3, :dcv4].transpose(1, 0, 2).reshape(3, DC)

    proj_w = lambda tn, tk: pl.BlockSpec((tn, tk), lambda j, i, k, s: (s[j], k))
    proj_cols = lambda tm, tn: [pl.BlockSpec((tm, tn), lambda j, i, k, s: (i, s[j]))]
    proj_half = lambda which, into, after: _matmul(
        h, w_in_t, name="mm_proj_%d" % which, tb=True, M=S, N=DIN // 2, K=D, tn=DIN // 2, b_spec=proj_w,
        out_specs=proj_cols, outs=[sds((S, DIN), F32)], slots=jnp.full((1,), which, jnp.int32), into=into,
        after=after)[0]
    proj = proj_half(0, [], [ag_kvo[3]])
    got_kvo, tok = gather_wait(ag_kvo, [0, 1], proj, "kvo")
    ag_w1 = gather_start([w1_b], tok, "ffn1", forwards=got_kvo)
    kvo_n = _forward_halves(ag_w1[2][1:], NEAR, ag_w1[3], name="ag_fwdn_kvo")
    ag_w1 = (ag_w1[0], ag_w1[1], [ag_w1[2][0]] + list(kvo_n), ag_w1[3])
    proj = proj_half(1, [proj], list(kvo_n))
    kvo_d, tok = diagonal_wait(ag_w1, [1, 2], proj, "kvo")
    wkv4, wo4 = _forward_halves(kvo_d, FAR, tok, name="ag_fwdd_kvo")
    w_kv_full = wkv4.reshape(D, 2 * DM)
    w_o_full = wo4.reshape(D, D)
    (kv,) = _matmul(mem_n, w_kv_full, name="mm_kv", M=NMEM, N=2 * DM, K=D, outs=[sds((NMEM, 2 * DM), F32)])
    heads, hn, ycv = _mix_fwd(proj, kv, ws3, bs_t, ln_v_g, ln_v_b, conv_full, g_head, name="mix_fwd")
    def residual_and_norm(acc, res, g):
        x2v = acc + res
        r = lax.rsqrt(jnp.mean(x2v * x2v, axis=-1, keepdims=True) + EPS)
        return x2v, (x2v * r) * g

    row_vec = lambda tm, tn: pl.BlockSpec((1, tn), lambda j, i, k, *s: (0, j))
    x2, h2 = _matmul(hn, w_o_full, name="mm_wo", M=S, N=D, K=D, tn=D, n_split=1, epi=residual_and_norm,
                     outs=[sds((S, D), F32), sds((S, D), BF16)], extras=[(x2d, _tile_spec()), (g_ffn, row_vec)])
    near = jnp.stack([shard, 2 * (1 - xi) + yi, 2 * xi + (1 - yi)]).astype(jnp.int32)
    far = jnp.reshape(2 * (1 - xi) + (1 - yi), (1,)).astype(jnp.int32)

    w1_shard = lambda tn, tk: pl.BlockSpec((None, tk, tn), lambda j, i, k, s: (s[j], k, 0))
    act_cols = lambda tm, tn: [pl.BlockSpec((tm, tn), lambda j, i, k, s: (i, s[j]))] * 2

    def relu2(acc):
        r = jnp.maximum(acc, 0.0)
        return r * r, 2.0 * r

    got_w1, tok = gather_wait(ag_w1, [0], h2, "ffn1")
    ag_w2 = gather_start([w2_b], tok, "ffn2", forwards=got_w1)
    (w1n,) = _forward_halves([ag_w2[2][1]], NEAR, ag_w2[3], name="ag_fwdn_ffn1")
    ag_w2 = (ag_w2[0], ag_w2[1], [ag_w2[2][0], w1n], ag_w2[3])
    act, dact_df = _matmul(h2, w1n, name="mm_ffn1_near", M=S, N=3 * dff4, K=D, tm=2 * TM, tn=dff4, b_spec=w1_shard,
                           out_specs=act_cols, outs=[sds((S, DFF), BF16)] * 2, epi=relu2, slots=near)
    w1d, tok = diagonal_wait(ag_w2, [1], act, "ffn1")
    (w14,) = _forward_halves(w1d, FAR, tok, name="ag_fwdd_ffn1")
    act, dact_df = _matmul(h2, w14, name="mm_ffn1_far", M=S, N=dff4, K=D, tm=2 * TM, tn=dff4, b_spec=w1_shard,
                           out_specs=act_cols, outs=[sds((S, DFF), BF16)] * 2, epi=relu2, slots=far,
                           into=[act, dact_df])

    act_shard = lambda tm, tk: pl.BlockSpec((tm, tk), lambda j, i, k, s: (i, s[k]))
    w2_shard = lambda tn, tk: pl.BlockSpec((None, tk, tn), lambda j, i, k, s: (s[k], 0, j))
    got_w2, tok = gather_wait(ag_w2, [0], act, "ffn2")
    ag_w2d = gather_start([], tok, "ffn2d", forwards=got_w2)
    (w2n,) = _forward_halves(ag_w2d[2], NEAR, ag_w2d[3], name="ag_fwdn_ffn2")
    ag_w2d = (ag_w2d[0], ag_w2d[1], [w2n], ag_w2d[3])
    (x3,) = _matmul(act, w2n, name="mm_ffn2_near", M=S, N=D, K=3 * dff4, tm=2 * TM, tk=dff4,
                    a_spec=act_shard, b_spec=w2_shard, outs=[sds((S, D), F32)], epi=lambda acc, res: (acc + res,),
                    extras=[(x2, _tile_spec())], slots=near)
    w2d, tok = diagonal_wait(ag_w2d, [0], x3, "ffn2")
    (w24,) = _forward_halves(w2d, FAR, tok, name="ag_fwdd_ffn2")
    (x3,) = _matmul(act, w24, name="mm_ffn2_far", M=S, N=D, K=dff4, tm=2 * TM, tk=dff4, a_spec=act_shard,
                    b_spec=w2_shard, outs=[sds((S, D), F32)], epi=lambda acc, res: (acc + res,),
                    extras=[(x3, _tile_spec())], slots=far)
    w2_full = w24.reshape(DFF, D)

    ci = lax.axis_index("c")

    def rs_sibling(g4, nm):
        return _sibling_start([g4], False, 1 + big_names.index(nm), name="rs_sib_" + nm)

    def rs_chips(state, after, nm):
        send, recv, g4, land, _ = state
        (((land_, g4_),), _) = _transfer_wait(send, recv, [[land[0], g4[0]]], [(NCHIP, land[0].shape[1])], after,
                                             name="rs_sibwait_" + nm)
        part, buf = _select_half_bf16(g4_, ci, land_, slot, name="rs_add_" + nm)
        return _scatter_start([part], [buf], 1 + 2 * len(big_names) + big_names.index(nm), name="rs_start_" + nm)

    def dw_half(a, b, nm, *, by_rows, hr, cols, which, land, after):
        tile = lambda tm, tn: pl.BlockSpec(
            (None, tm, tn), (lambda j, i, k, s: (i, 0, j)) if by_rows else (lambda j, i, k, s: (j, 0, 0)))
        a_half = lambda tm, tk: pl.BlockSpec(
            (tk, tm), (lambda j, i, k, s: (k, 2 * i + s[0])) if by_rows else (lambda j, i, k, s: (k, s[0])))
        (out,) = _matmul(a, b, name=nm, ta=True, M=NCHIP * hr if by_rows else hr, N=cols if by_rows else NCHIP * cols,
                         K=S, tm=hr, tn=cols, a_spec=a_half, out_specs=lambda tm, tn: [tile(tm, tn)],
                         outs=[sds((NCHIP, hr, cols), BF16)], slots=jnp.reshape(which, (1,)).astype(jnp.int32),
                         epi=None if land is None else (lambda acc, other: (acc + other.astype(F32),)),
                         extras=[] if land is None else [(land, tile)], after=after)
        return out

    def rs_sibling_half(half, nm):
        return _sibling_start([half], True, 1 + big_names.index(nm), name="rs_sib_" + nm)

    def rs_chips_fused(state, grad_half, after, nm):
        send, recv, mine, land, _ = state
        (((land_, _),), tok) = _transfer_wait(send, recv, [[land[0], mine[0]]], [(NCHIP, land[0].shape[1])], after,
                                             name="rs_sibwait_" + nm)
        part = grad_half(land_, [tok])
        buf = _own_slot(part, slot, name="rs_own_" + nm)
        return _scatter_start([part], [buf], 1 + 2 * len(big_names) + big_names.index(nm), name="rs_start_" + nm)

    def rs_end(state, after, nm):
        send, recv, parts, bufs, _ = state
        (((buf, _),), _) = _transfer_wait(send, recv, [[bufs[0], parts[0]]], [(N_PEER_CHIPS, bufs[0].shape[1])], after,
                                          name="rs_wait_" + nm)
        return _sibling_start([buf], True, 1 + len(big_names) + big_names.index(nm), name="rs_share_" + nm)

    big_m = [m_w_in[0].T, m_w_kv[0], m_w_o[0], m_w_ffn1[0], m_w_ffn2[0]]
    big_v = [v_w_in[0].T, v_w_kv[0], v_w_o[0], v_w_ffn1[0], v_w_ffn2[0]]
    big_out = {}

    def rs_finish(k, state, after):
        send, recv, mine, land, _ = state
        nm = big_names[k]
        (((land_, mine_),), _) = _transfer_wait(send, recv, [[land[0], mine[0]]], [(NCHIP, land[0].shape[1])], after,
                                               name="rs_sharewait_" + nm)
        big_out[nm] = _adamw(big[k], mine_, land_, big_m[k], big_v[k], core, name="adamw_" + nm)
        return big_out[nm][1]

    dx3, dx3b, dg_final, loss11 = _loss_bwd(x3, g_final2, tgt, name="loss_bwd")
    dw2_half = lambda which, land, after, nm: dw_half(
        act, dx3b, nm, by_rows=True, hr=dff4 // 2, cols=D, which=which, land=land, after=after)
    sib_w2 = rs_sibling_half(dw2_half(1 - ci, None, [], "mm_dw2_sib"), "w_ffn2")
    (dfb,) = _matmul(dx3b, w2_full, name="mm_dact", tb=True, M=S, N=DFF, K=D, tm=2 * TM, tn=dff4, outs=[sds((S, DFF), BF16)],
                     epi=lambda acc, g: (acc * g.astype(F32),), extras=[(dact_df, _tile_spec())],
                     after=[sib_w2[4]])
    rs_w2 = rs_chips_fused(sib_w2, lambda land, after: dw2_half(ci, land, after, "mm_dw2_own"), dfb, "w_ffn2")

    dw1_half = lambda which, land, after, nm: dw_half(
        h2, dfb, nm, by_rows=False, hr=D // 2, cols=dff4, which=which, land=land, after=after)
    sib_w1 = rs_sibling_half(dw1_half(1 - ci, None, [rs_w2[4]], "mm_dw1_sib"), "w_ffn1")

    def w1_rows(tn, tk):
        kb = dff4 // tk
        return pl.BlockSpec((None, tn, tk), lambda j, i, k: (k // kb, j, k % kb))

    (dh2,) = _matmul(dfb, w14, name="mm_dh2", tb=True, M=S, N=D, K=DFF, tm=2 * TM, b_spec=w1_rows,
                     outs=[sds((S, D), F32)], after=[sib_w1[4]])
    rs_w1 = rs_chips_fused(sib_w1, lambda land, after: dw1_half(ci, land, after, "mm_dw1_own"), dh2, "w_ffn1")
    dx2, dx2b, dg_ffn = _rms_bwd(dh2, x2, g_ffn, dx3, name="rms_ffn_bwd", after=[rs_w1[4]])
    dwo_half = lambda which, land, after, nm: dw_half(
        hn, dx2b, nm, by_rows=True, hr=D // NCHIP // 2, cols=D, which=which, land=land, after=after)
    sib_wo = rs_sibling_half(dwo_half(1 - ci, None, [], "mm_dwo_sib"), "w_o")
    (dhn,) = _matmul(dx2b, w_o_full, name="mm_dhn", tb=True, M=S, N=D, K=D, tm=2 * TM, outs=[sds((S, D), F32)],
                     after=[sib_wo[4]])
    rs_wo = rs_chips_fused(sib_wo, lambda land, after: dwo_half(ci, land, after, "mm_dwo_own"), dhn, "w_o")
    sh_w2 = rs_end(rs_w2, rs_wo[4], "w_ffn2")
    dproj, dkv, dws, dbs8, dlng, dlnb, dcw8, dgh = _mix_bwd(
        dhn, heads, proj, ycv, kv, ws3, bs_t, ln_v_g, ln_v_b, conv_full, g_head, sh_w2[4], name="mix_bwd")
    (dwin_t,) = _matmul(dproj, h, name="mm_dwin", ta=True, M=DIN, N=D, K=S, tm=DIN // 2, outs=[sds((DIN, D), BF16)])
    sib_win = rs_sibling(dwin_t.reshape(NCHIP, din4, D), "w_in")
    (dwkv,) = _matmul(mem_n, dkv, name="mm_dwkv", ta=True, M=D, N=2 * DM, K=NMEM, outs=[sds((D, 2 * DM), BF16)],
                      after=[sib_win[4]])
    sib_wkv = rs_sibling(dwkv.reshape(NCHIP, D // NCHIP, 2 * DM), "w_kv")
    sh_w1 = rs_end(rs_w1, sib_wkv[4], "w_ffn1")
    (dh,) = _matmul(dproj, w_in_t, name="mm_dh", M=S, N=D, K=DIN, tm=2 * TM, tk=DIN, outs=[sds((S, D), F32)],
                    after=[sh_w1[4]])
    rs_win = rs_chips(sib_win, dh, "w_in")
    rs_wkv = rs_chips(sib_wkv, rs_win[4], "w_kv")
    dx, dg_mix = _rms_bwd(dh, x2d, g_mix, dx2, name="rms_mix_bwd", want_bf=False, after=[rs_wkv[4]])
    (dmem_n,) = _matmul(dkv, w_kv_full, name="mm_dmem", tb=True, M=NMEM, N=D, K=2 * DM, outs=[sds((NMEM, D), F32)],
                        after=[dx])
    (dg_mem,) = _rms_bwd(dmem_n, mem2d, g_mem, None, name="rms_mem_bwd", want_dx=False)
    sh_wo = rs_end(rs_wo, dg_mem, "w_o")
    done = rs_finish(4, sh_w2, sh_wo[4])
    done = rs_finish(3, sh_w1, done)
    sh_win = rs_end(rs_win, done, "w_in")
    sh_wkv = rs_end(rs_wkv, sh_win[4], "w_kv")
    done = rs_finish(2, sh_wo, sh_wkv[4])
    done = rs_finish(0, sh_win, done)
    done = rs_finish(1, sh_wkv, done)

    small_names = ["g_mix", "ln_v_g", "ln_v_b", "w_s", "b_s", "conv_w", "g_mem", "g_head", "g_ffn", "g_final"]
    small_part = [dg_mix, dlng, dlnb, dws, dbs8[:, 0, :], dcw8[:3], dg_mem, dgh, dg_ffn, dg_final, loss11]
    small_shapes = [(1, D), (1, DS), (1, DS), (NSH, CHUNK, CHUNK), (NSH, CHUNK), (3, DC), (1, D), (1, D), (1, D), (1, D),
                    (1, 1)]
    total = _allreduce_small(_pack(small_part), done, name="allreduce_small")
    small_g = _unpack(total, small_shapes)
    loss = small_g.pop()[0, 0]
    small_g[5] = lax.dynamic_slice(small_g[5], (0, shard * dcv4), (3, dcv4))
    small_w = [g_mix, ln_v_g, ln_v_b, ws3, bs2, conv_w[0], g_mem, g_head, g_ffn, g_final2]
    small_m = [m_g_mix, m_ln_v_g, m_ln_v_b, m_w_s[0], m_b_s[0], m_conv_w[0], m_g_mem, m_g_head, m_g_ffn,
               m_g_final.reshape(1, D)]
    small_v = [v_g_mix, v_ln_v_g, v_ln_v_b, v_w_s[0], v_b_s[0], v_conv_w[0], v_g_mem, v_g_head, v_g_ffn,
               v_g_final.reshape(1, D)]
    s_delta, s_m, s_v = _adamw_small(small_w, small_g, small_m, small_v, name="adamw_small")
    small_out = {nm: (g, d, mn, vn) for nm, g, d, mn, vn in zip(small_names, small_g, s_delta, s_m, s_v)}

    order = ["g_mix", "w_in", "ln_v_g", "ln_v_b", "w_s", "b_s", "conv_w", "g_mem", "w_kv", "g_head", "w_o",
             "g_ffn", "w_ffn1", "w_ffn2", "g_final"]
    like = dict(g_mix=g_mix, w_in=w_in, ln_v_g=ln_v_g, ln_v_b=ln_v_b, w_s=w_s, b_s=b_s, conv_w=conv_w, g_mem=g_mem,
                w_kv=w_kv, g_head=g_head, w_o=w_o, g_ffn=g_ffn, w_ffn1=w_ffn1, w_ffn2=w_ffn2, g_final=g_final)
    res = {**big_out, **small_out}
    res["w_in"] = [a.T for a in res["w_in"]]
    outs = [loss, dx[None]]
    for k in range(4):
        outs += [res[nm][k].reshape(like[nm].shape) for nm in order]
    return tuple(outs)
```
